```python
import math
import jax, jax.numpy as jnp
from jax import lax
import numpy as np

D_MODEL = 1024
BATCH = 16
SEQ = 2048
DEPTH = 1

HGRN_HEADS = 8
HGRN_DK = 64
HGRN_DV = 64
HGRN_KEY = HGRN_HEADS * HGRN_DK
HGRN_WIDTH = HGRN_HEADS * HGRN_DV
SB_HEADS = 8
SB_DH = 64
SB_WIDTH = SB_HEADS * SB_DH
D_MIX = HGRN_WIDTH + SB_WIDTH
IN_COLS = 2 * HGRN_KEY + 2 * HGRN_WIDTH + 3 * SB_WIDTH
CHUNK = 64
Q_BLOCK = 128
D_FF = -(-8 * D_MODEL // (3 * 256)) * 256
EPS = 1e-6

kernel_name = "hymba_hgrn2_stickbreaking_block"


def rmsnorm(x, g):
    xf = x.astype(jnp.float32)
    y = xf * lax.rsqrt(jnp.mean(xf * xf, axis=-1, keepdims=True) + EPS)
    return (y * g.astype(jnp.float32)).astype(x.dtype)


def hgrn2_mix(q, f_logit, inp, gate, lb, norm_g):
    B, S, _ = q.shape
    n_chunks = S // CHUNK
    dt = q.dtype

    def heads(t, d):
        t = t.astype(jnp.float32).reshape(B, n_chunks, CHUNK, HGRN_HEADS, d)
        return jnp.transpose(t, (0, 3, 1, 2, 4))

    lbf = lb.astype(jnp.float32)
    f = lbf + (1.0 - lbf) * jax.nn.sigmoid(f_logit.astype(jnp.float32))
    qh = heads(q, HGRN_DK)
    fh = heads(f, HGRN_DK)
    kh = 1.0 - fh
    vh = heads(inp, HGRN_DV)
    b = jnp.cumsum(jnp.log(fh), axis=3)
    b_end = b[:, :, :, -1:, :]

    qe = qh * jnp.exp(b)
    ke = kh * jnp.exp(-b)
    scores = jnp.einsum('bhnid,bhnjd->bhnij', qe, ke)
    tri = jnp.tril(jnp.ones((CHUNK, CHUNK), dtype=bool))
    scores = jnp.where(tri, scores, 0.0)
    o_intra = jnp.einsum('bhnij,bhnje->bhnie', scores, vh)

    kv = jnp.einsum('bhncd,bhnce->bhnde', kh * jnp.exp(b_end - b), vh)
    decay = jnp.exp(b_end[:, :, :, 0, :])

    def step(state, xs):
        dec, upd = xs
        return dec[..., None] * state + upd, state

    s0 = jnp.zeros((B, HGRN_HEADS, HGRN_DK, HGRN_DV), jnp.float32)
    _, s_prev = lax.scan(step, s0, (jnp.moveaxis(decay, 2, 0), jnp.moveaxis(kv, 2, 0)))
    s_prev = jnp.moveaxis(s_prev, 0, 2)
    o_inter = jnp.einsum('bhncd,bhnde->bhnce', qe, s_prev)

    o = o_intra + o_inter
    o = jnp.transpose(o, (0, 2, 3, 1, 4)).reshape(B, S, HGRN_HEADS, HGRN_DV)
    o = o * lax.rsqrt(jnp.mean(o * o, axis=-1, keepdims=True) + EPS)
    o = o.reshape(B, S, HGRN_WIDTH) * norm_g.astype(jnp.float32)
    o = o * jax.nn.silu(gate.astype(jnp.float32))
    return o.astype(dt)


def stick_breaking_mix(q, k, v, norm_g):
    B, S, _ = q.shape
    dt = q.dtype

    def heads(t):
        return jnp.transpose(t.astype(jnp.float32).reshape(B, S, SB_HEADS, SB_DH), (0, 2, 1, 3))

    qh, kh, vh = heads(q), heads(k), heads(v)
    scale = 1.0 / math.sqrt(SB_DH)
    outs = []
    for blk in range(S // Q_BLOCK):
        start = blk * Q_BLOCK
        end = start + Q_BLOCK
        qb = qh[:, :, start:end]
        kb = kh[:, :, :end]
        vb = vh[:, :, :end]
        z = jnp.einsum('bhqd,bhkd->bhqk', qb, kb) * scale
        q_pos = start + jnp.arange(Q_BLOCK)
        k_pos = jnp.arange(end)
        mask = k_pos[None, :] < q_pos[:, None]
        c = jnp.where(mask, jax.nn.softplus(z), 0.0)
        after = lax.cumsum(c, axis=3, reverse=True) - c
        attn = jnp.where(mask, jnp.exp(jax.nn.log_sigmoid(z) - after), 0.0)
        outs.append(jnp.einsum('bhqk,bhkd->bhqd', attn, vb))
    o = jnp.concatenate(outs, axis=2)
    o = jnp.transpose(o, (0, 2, 1, 3))
    o = o * lax.rsqrt(jnp.mean(o * o, axis=-1, keepdims=True) + EPS)
    o = o.reshape(B, S, SB_WIDTH) * norm_g.astype(jnp.float32)
    return o.astype(dt)


def _fwd_setup_inputs(seed: int = 0) -> dict:
    key = jax.random.key(seed)
    ks = jax.random.split(key, 12)
    f32 = jnp.float32

    def w(k, shape, fan_in):
        return jax.random.normal(k, shape, f32) * fan_in ** -0.5

    def gain(k, shape):
        return 1.0 + 0.01 * jax.random.normal(k, shape, f32)

    return {
        "x": jax.random.normal(ks[0], (BATCH, SEQ, D_MODEL), f32),
        "mix_norm_g": gain(ks[1], (DEPTH, D_MODEL)),
        "w_in": w(ks[2], (DEPTH, D_MODEL, IN_COLS), D_MODEL),
        "lower_bounds": 0.1 * jax.random.normal(ks[3], (DEPTH + 1, HGRN_KEY), f32),
        "hgrn_norm_g": gain(ks[4], (DEPTH, HGRN_WIDTH)),
        "sb_norm_g": gain(ks[5], (DEPTH, SB_WIDTH)),
        "w_out": w(ks[6], (DEPTH, D_MIX, D_MODEL), D_MIX),
        "ffn_norm_g": gain(ks[7], (DEPTH, D_MODEL)),
        "w_gate": w(ks[8], (DEPTH, D_MODEL, D_FF), D_MODEL),
        "w_up": w(ks[9], (DEPTH, D_MODEL, D_FF), D_MODEL),
        "w_down": w(ks[10], (DEPTH, D_FF, D_MODEL), D_FF),
        "final_norm_g": gain(ks[11], (D_MODEL,)),
    }


def _fwd_reference(x, mix_norm_g, w_in, lower_bounds, hgrn_norm_g, sb_norm_g, w_out,
              ffn_norm_g, w_gate, w_up, w_down, final_norm_g):
    lb_all = jnp.cumsum(jax.nn.softmax(lower_bounds.astype(jnp.float32), axis=0), axis=0)
    splits = np.cumsum([HGRN_KEY, HGRN_KEY, HGRN_WIDTH, HGRN_WIDTH,
                        SB_WIDTH, SB_WIDTH])
    for l in range(DEPTH):
        h = rmsnorm(x, mix_norm_g[l])
        proj = jnp.einsum('bsd,de->bse', h, w_in[l])
        hq, hf, hi, hg, sq, sk, sv = jnp.split(proj, splits, axis=-1)
        o_a = hgrn2_mix(hq, hf, hi, hg, lb_all[l].astype(x.dtype), hgrn_norm_g[l])
        o_b = stick_breaking_mix(sq, sk, sv, sb_norm_g[l])
        mixed = jnp.concatenate([o_a, o_b], axis=-1)
        x = x + jnp.einsum('bse,ed->bsd', mixed, w_out[l])
        h = rmsnorm(x, ffn_norm_g[l])
        ff = jax.nn.silu(jnp.einsum('bsd,df->bsf', h, w_gate[l])) * jnp.einsum('bsd,df->bsf', h, w_up[l])
        x = x + jnp.einsum('bsf,fd->bsd', ff, w_down[l])
    return rmsnorm(x, final_norm_g)


import jax as _jax
import jax.numpy as _jnp

TWIN_FORMAT = 'train_step'
FWD_PARAMS = ['x', 'mix_norm_g', 'w_in', 'lower_bounds', 'hgrn_norm_g', 'sb_norm_g', 'w_out', 'ffn_norm_g', 'w_gate', 'w_up', 'w_down', 'final_norm_g']
TWIN_WEIGHTS = ['mix_norm_g', 'w_in', 'lower_bounds', 'hgrn_norm_g', 'sb_norm_g', 'w_out', 'ffn_norm_g', 'w_gate', 'w_up', 'w_down', 'final_norm_g']
TWIN_DIFF_INPUT = 'x'
TWIN_INPUTS = ['x', 'mix_norm_g', 'w_in', 'lower_bounds', 'hgrn_norm_g', 'sb_norm_g', 'w_out', 'ffn_norm_g', 'w_gate', 'w_up', 'w_down', 'final_norm_g', 'loss_target', 'm_mix_norm_g', 'm_w_in', 'm_lower_bounds', 'm_hgrn_norm_g', 'm_sb_norm_g', 'm_w_out', 'm_ffn_norm_g', 'm_w_gate', 'm_w_up', 'm_w_down', 'm_final_norm_g', 'v_mix_norm_g', 'v_w_in', 'v_lower_bounds', 'v_hgrn_norm_g', 'v_sb_norm_g', 'v_w_out', 'v_ffn_norm_g', 'v_w_gate', 'v_w_up', 'v_w_down', 'v_final_norm_g']
TWIN_OUTPUTS = ['loss', 'grad_x', 'grad_mix_norm_g', 'grad_w_in', 'grad_lower_bounds', 'grad_hgrn_norm_g', 'grad_sb_norm_g', 'grad_w_out', 'grad_ffn_norm_g', 'grad_w_gate', 'grad_w_up', 'grad_w_down', 'grad_final_norm_g', 'delta_mix_norm_g', 'delta_w_in', 'delta_lower_bounds', 'delta_hgrn_norm_g', 'delta_sb_norm_g', 'delta_w_out', 'delta_ffn_norm_g', 'delta_w_gate', 'delta_w_up', 'delta_w_down', 'delta_final_norm_g', 'new_m_mix_norm_g', 'new_m_w_in', 'new_m_lower_bounds', 'new_m_hgrn_norm_g', 'new_m_sb_norm_g', 'new_m_w_out', 'new_m_ffn_norm_g', 'new_m_w_gate', 'new_m_w_up', 'new_m_w_down', 'new_m_final_norm_g', 'new_v_mix_norm_g', 'new_v_w_in', 'new_v_lower_bounds', 'new_v_hgrn_norm_g', 'new_v_sb_norm_g', 'new_v_w_out', 'new_v_ffn_norm_g', 'new_v_w_gate', 'new_v_w_up', 'new_v_w_down', 'new_v_final_norm_g']
TWIN_LEAF_KINDS = {'loss': 'loss', 'grad_x': 'grad_x', 'grad_mix_norm_g': 'grad_w', 'grad_w_in': 'grad_w', 'grad_lower_bounds': 'grad_w', 'grad_hgrn_norm_g': 'grad_w', 'grad_sb_norm_g': 'grad_w', 'grad_w_out': 'grad_w', 'grad_ffn_norm_g': 'grad_w', 'grad_w_gate': 'grad_w', 'grad_w_up': 'grad_w', 'grad_w_down': 'grad_w', 'grad_final_norm_g': 'grad_w', 'delta_mix_norm_g': 'delta_w', 'delta_w_in': 'delta_w', 'delta_lower_bounds': 'delta_w', 'delta_hgrn_norm_g': 'delta_w', 'delta_sb_norm_g': 'delta_w', 'delta_w_out': 'delta_w', 'delta_ffn_norm_g': 'delta_w', 'delta_w_gate': 'delta_w', 'delta_w_up': 'delta_w', 'delta_w_down': 'delta_w', 'delta_final_norm_g': 'delta_w', 'new_m_mix_norm_g': 'new_m', 'new_m_w_in': 'new_m', 'new_m_lower_bounds': 'new_m', 'new_m_hgrn_norm_g': 'new_m', 'new_m_sb_norm_g': 'new_m', 'new_m_w_out': 'new_m', 'new_m_ffn_norm_g': 'new_m', 'new_m_w_gate': 'new_m', 'new_m_w_up': 'new_m', 'new_m_w_down': 'new_m', 'new_m_final_norm_g': 'new_m', 'new_v_mix_norm_g': 'new_v', 'new_v_w_in': 'new_v', 'new_v_lower_bounds': 'new_v', 'new_v_hgrn_norm_g': 'new_v', 'new_v_sb_norm_g': 'new_v', 'new_v_w_out': 'new_v', 'new_v_ffn_norm_g': 'new_v', 'new_v_w_gate': 'new_v', 'new_v_w_up': 'new_v', 'new_v_w_down': 'new_v', 'new_v_final_norm_g': 'new_v'}


def _forward(args):
    return _fwd_reference(*[args[k] for k in FWD_PARAMS])


def _output_shape():
    out = _jax.eval_shape(lambda: _forward(_fwd_setup_inputs(0)))
    return out.shape, out.dtype

N_MICROBATCH = 1
ADAM_LR = 0.001
ADAM_B1 = 0.9
ADAM_B2 = 0.999
ADAM_EPS = 1e-08
ADAM_WD = 0.01
ADAM_STEP = 10
PER_EXAMPLE_BATCH_AXIS = {'x': 0, 'loss_target': 0}
SHARED_INPUTS = []
_WEIGHT_DTYPES = {'mix_norm_g': _jnp.float32, 'w_in': _jnp.float32, 'lower_bounds': _jnp.float32, 'hgrn_norm_g': _jnp.float32, 'sb_norm_g': _jnp.float32, 'w_out': _jnp.float32, 'ffn_norm_g': _jnp.float32, 'w_gate': _jnp.float32, 'w_up': _jnp.float32, 'w_down': _jnp.float32, 'final_norm_g': _jnp.float32}
MOMENT_SCALE = {'mix_norm_g': 1.862138e-01, 'w_in': 1.015691e-01, 'lower_bounds': 6.085693e-02, 'hgrn_norm_g': 8.525475e-02, 'sb_norm_g': 1.641601e-01, 'w_out': 1.224283e-01, 'ffn_norm_g': 1.082751e-01, 'w_gate': 4.618733e-02, 'w_up': 4.452542e-02, 'w_down': 7.394742e-02, 'final_norm_g': 3.196360e+01}


def _to_microbatches(a, axis):
    t = _jnp.moveaxis(a, axis, 0)
    t = t.reshape((N_MICROBATCH, t.shape[0] // N_MICROBATCH) + t.shape[1:])
    return _jnp.moveaxis(t, 1, axis + 1)


def setup_inputs(seed: int = 0) -> dict:
    inp = _fwd_setup_inputs(seed)
    key = _jax.random.fold_in(_jax.random.key(seed), 7919)
    shape, _ = _output_shape()
    out = dict(inp)
    out["loss_target"] = _jax.random.normal(_jax.random.fold_in(key, 0), shape, _jnp.float32)
    for i, name in enumerate(TWIN_WEIGHTS):
        w = inp[name].astype(_jnp.float32)
        if MOMENT_SCALE is None:
            s = _jnp.sqrt(_jnp.mean(_jnp.square(w)) + 1e-30)
        else:
            s = MOMENT_SCALE[name]
        km, kv = _jax.random.split(_jax.random.fold_in(key, i + 1))
        out[name] = w
        out["m_" + name] = s * _jax.random.normal(km, w.shape, _jnp.float32)
        out["v_" + name] = (s * s) * _jax.random.uniform(kv, w.shape, _jnp.float32, 0.5, 1.5)
    if N_MICROBATCH > 1:
        for name, axis in PER_EXAMPLE_BATCH_AXIS.items():
            out[name] = _to_microbatches(out[name], axis)
    return {'x': out['x'], 'mix_norm_g': out['mix_norm_g'], 'w_in': out['w_in'], 'lower_bounds': out['lower_bounds'], 'hgrn_norm_g': out['hgrn_norm_g'], 'sb_norm_g': out['sb_norm_g'], 'w_out': out['w_out'], 'ffn_norm_g': out['ffn_norm_g'], 'w_gate': out['w_gate'], 'w_up': out['w_up'], 'w_down': out['w_down'], 'final_norm_g': out['final_norm_g'], 'loss_target': out['loss_target'], 'm_mix_norm_g': out['m_mix_norm_g'], 'm_w_in': out['m_w_in'], 'm_lower_bounds': out['m_lower_bounds'], 'm_hgrn_norm_g': out['m_hgrn_norm_g'], 'm_sb_norm_g': out['m_sb_norm_g'], 'm_w_out': out['m_w_out'], 'm_ffn_norm_g': out['m_ffn_norm_g'], 'm_w_gate': out['m_w_gate'], 'm_w_up': out['m_w_up'], 'm_w_down': out['m_w_down'], 'm_final_norm_g': out['m_final_norm_g'], 'v_mix_norm_g': out['v_mix_norm_g'], 'v_w_in': out['v_w_in'], 'v_lower_bounds': out['v_lower_bounds'], 'v_hgrn_norm_g': out['v_hgrn_norm_g'], 'v_sb_norm_g': out['v_sb_norm_g'], 'v_w_out': out['v_w_out'], 'v_ffn_norm_g': out['v_ffn_norm_g'], 'v_w_gate': out['v_w_gate'], 'v_w_up': out['v_w_up'], 'v_w_down': out['v_w_down'], 'v_final_norm_g': out['v_final_norm_g']}


def _loss(weights, diff, rest, loss_target):
    with _jax.named_scope("forward"):
        args = {**rest, TWIN_DIFF_INPUT: diff, **{k: w.astype(_WEIGHT_DTYPES[k]) for k, w in weights.items()}}
        y = _forward(args)
    with _jax.named_scope("loss_head"):
        err = _jnp.square(y.astype(_jnp.float32) - loss_target)
        return 0.5 * _jnp.sum(_jnp.mean(err, axis=-1)) if err.ndim else 0.5 * err


def _adamw(w, g, m, v):
    m = ADAM_B1 * m + (1.0 - ADAM_B1) * g
    v = ADAM_B2 * v + (1.0 - ADAM_B2) * _jnp.square(g)
    m_hat = m / (1.0 - ADAM_B1 ** ADAM_STEP)
    v_hat = v / (1.0 - ADAM_B2 ** ADAM_STEP)
    delta = -ADAM_LR * (m_hat / (_jnp.sqrt(v_hat) + ADAM_EPS) + ADAM_WD * w)
    return delta, m, v


def reference(x, mix_norm_g, w_in, lower_bounds, hgrn_norm_g, sb_norm_g, w_out, ffn_norm_g, w_gate, w_up, w_down, final_norm_g, loss_target, m_mix_norm_g, m_w_in, m_lower_bounds, m_hgrn_norm_g, m_sb_norm_g, m_w_out, m_ffn_norm_g, m_w_gate, m_w_up, m_w_down, m_final_norm_g, v_mix_norm_g, v_w_in, v_lower_bounds, v_hgrn_norm_g, v_sb_norm_g, v_w_out, v_ffn_norm_g, v_w_gate, v_w_up, v_w_down, v_final_norm_g):
    given = dict(x=x, mix_norm_g=mix_norm_g, w_in=w_in, lower_bounds=lower_bounds, hgrn_norm_g=hgrn_norm_g, sb_norm_g=sb_norm_g, w_out=w_out, ffn_norm_g=ffn_norm_g, w_gate=w_gate, w_up=w_up, w_down=w_down, final_norm_g=final_norm_g, loss_target=loss_target, m_mix_norm_g=m_mix_norm_g, m_w_in=m_w_in, m_lower_bounds=m_lower_bounds, m_hgrn_norm_g=m_hgrn_norm_g, m_sb_norm_g=m_sb_norm_g, m_w_out=m_w_out, m_ffn_norm_g=m_ffn_norm_g, m_w_gate=m_w_gate, m_w_up=m_w_up, m_w_down=m_w_down, m_final_norm_g=m_final_norm_g, v_mix_norm_g=v_mix_norm_g, v_w_in=v_w_in, v_lower_bounds=v_lower_bounds, v_hgrn_norm_g=v_hgrn_norm_g, v_sb_norm_g=v_sb_norm_g, v_w_out=v_w_out, v_ffn_norm_g=v_ffn_norm_g, v_w_gate=v_w_gate, v_w_up=v_w_up, v_w_down=v_w_down, v_final_norm_g=v_final_norm_g)
    weights = {n: given[n] for n in TWIN_WEIGHTS}
    shared = {n: given[n] for n in SHARED_INPUTS}
    per_example = {n: given[n] for n in ['x']}
    grad_fn = _jax.value_and_grad(_loss, argnums=(0, 1))

    def one_microbatch(ex, loss_target):
        ex = dict(ex)
        diff = ex.pop(TWIN_DIFF_INPUT)
        return grad_fn(weights, diff, {**shared, **ex}, loss_target)

    if N_MICROBATCH == 1:
        loss, (grad_w, grad_x) = one_microbatch(per_example, given["loss_target"])
    else:
        def body(carry, xs):
            loss_sum, grad_sum = carry
            l_k, (gw_k, gx_k) = one_microbatch(xs[0], xs[1])
            with _jax.named_scope("update"):
                return (loss_sum + l_k, _jax.tree.map(_jnp.add, grad_sum, gw_k)), gx_k

        init = (_jnp.zeros((), _jnp.float32), _jax.tree.map(_jnp.zeros_like, weights))
        (loss, grad_w), grad_x = _jax.lax.scan(body, init, (per_example, given["loss_target"]))
    with _jax.named_scope("update"):
        delta_w, new_m, new_v = {}, {}, {}
        for n in TWIN_WEIGHTS:
            delta_w[n], new_m[n], new_v[n] = _adamw(weights[n], grad_w[n], given["m_" + n], given["v_" + n])
    return (loss, grad_x, *[grad_w[n] for n in TWIN_WEIGHTS], *[delta_w[n] for n in TWIN_WEIGHTS],
            *[new_m[n] for n in TWIN_WEIGHTS], *[new_v[n] for n in TWIN_WEIGHTS])
```

```python
import functools
import math

import jax
import jax.numpy as jnp
from jax import lax
from jax.experimental import pallas as pl
from jax.experimental.pallas import tpu as pltpu

F32 = jnp.float32
BF16 = jnp.bfloat16
MXU_DTYPE = BF16

EPS = 1e-6
D_MODEL = 1024
N_HEADS = 8
D_HEAD = 64
GROUP = N_HEADS * D_HEAD
IN_COLS = 7 * GROUP
D_FF = 2816
CHUNK = 64
LANES = 128
N_PAIRS = GROUP // LANES
SUPER = 256
SB_BLOCK = 256
N_DEV = 8

ADAM_LR = 0.001
ADAM_B1 = 0.9
ADAM_B2 = 0.999
ADAM_EPS = 1e-08
ADAM_WD = 0.01
ADAM_STEP = 10

ROWS_IN = D_MODEL * (IN_COLS // N_DEV) // D_MODEL
ROWS_OUT = D_MODEL // N_DEV
ROWS_FF = D_FF // N_DEV
OFF_OUT = ROWS_IN
OFF_GATE = OFF_OUT + ROWS_OUT
OFF_UP = OFF_GATE + ROWS_FF
OFF_DOWN = OFF_UP + ROWS_FF
PACK_ROWS = OFF_DOWN + ROWS_FF
SMALL_ROWS = 8

VMEM_LIMIT = 48 * 1024 * 1024


def _params(n_axes, vmem=VMEM_LIMIT):
    return pltpu.CompilerParams(dimension_semantics=("arbitrary",) * n_axes, vmem_limit_bytes=vmem)


def _dot(a, b):
    return jnp.dot(a.astype(MXU_DTYPE), b.astype(MXU_DTYPE), preferred_element_type=F32)


def _dot_nt(a, b):
    return lax.dot_general(a.astype(MXU_DTYPE), b.astype(MXU_DTYPE), (((1,), (1,)), ((), ())),
                           preferred_element_type=F32)


def _dot_tn(a, b):
    return lax.dot_general(a.astype(MXU_DTYPE), b.astype(MXU_DTYPE), (((0,), (0,)), ((), ())),
                           preferred_element_type=F32)


def _split(x, parts):
    out, r = [], x
    for _ in range(parts):
        h = r.astype(BF16)
        out.append(h)
        r = r - h.astype(F32)
    return out


def _rsum_right(x, u, parts):
    acc = None
    for h in _split(x, parts):
        d = jnp.dot(h, u, preferred_element_type=F32)
        acc = d if acc is None else acc + d
    return acc


def _rsum_left(u, x, parts):
    acc = None
    for h in _split(x, parts):
        d = jnp.dot(u, h, preferred_element_type=F32)
        acc = d if acc is None else acc + d
    return acc


def _ones_where(mask):
    return jnp.where(mask, 1.0, 0.0).astype(BF16)


def _sigmoid(x):
    return 1.0 / (1.0 + jnp.exp(-x))


def _softplus(x):
    return jnp.maximum(x, 0.0) + jnp.log(1.0 + jnp.exp(-jnp.abs(x)))


def _head_masks():
    lane = lax.broadcasted_iota(jnp.int32, (1, LANES), 1)
    return [jnp.where(lane < D_HEAD, 1.0, 0.0), jnp.where(lane >= D_HEAD, 1.0, 0.0)]


def _head_rstd(o, masks):
    sq = o * o
    r = None
    for m in masks:
        ms = jnp.sum(sq * m, axis=1, keepdims=True) * (1.0 / D_HEAD)
        t = lax.rsqrt(ms + EPS) * m
        r = t if r is None else r + t
    return r


def _head_mean(t, masks):
    out = None
    for m in masks:
        v = jnp.sum(t * m, axis=1, keepdims=True) * (1.0 / D_HEAD) * m
        out = v if out is None else out + v
    return out


def _rmsnorm_fwd(x, g, name):
    t, d = x.shape
    tm = min(512, t)

    def body(x_ref, g_ref, o_ref):
        xv = x_ref[...]
        r = lax.rsqrt(jnp.mean(xv * xv, axis=-1, keepdims=True) + EPS)
        o_ref[...] = (xv * r * g_ref[...]).astype(o_ref.dtype)

    return pl.pallas_call(
        body, name=name, grid=(t // tm,),
        in_specs=[pl.BlockSpec((tm, d), lambda i: (i, 0)), pl.BlockSpec((1, d), lambda i: (0, 0))],
        out_specs=pl.BlockSpec((tm, d), lambda i: (i, 0)),
        out_shape=jax.ShapeDtypeStruct((t, d), MXU_DTYPE),
        compiler_params=_params(1),
    )(x, g)


def _rmsnorm_bwd(x, g, dh, res, name):
    t, d = x.shape
    tm = min(512, t)

    def body(x_ref, g_ref, dh_ref, res_ref, dx_ref, dg_ref):
        xv = x_ref[...]
        r = lax.rsqrt(jnp.mean(xv * xv, axis=-1, keepdims=True) + EPS)
        xh = xv * r
        dhv = dh_ref[...]
        dxh = dhv * g_ref[...]
        dx_ref[...] = res_ref[...] + r * (dxh - xh * jnp.mean(dxh * xh, axis=-1, keepdims=True))

        @pl.when(pl.program_id(0) == 0)
        def _():
            dg_ref[...] = jnp.zeros_like(dg_ref)

        dg_ref[...] += jnp.sum(dhv * xh, axis=0, keepdims=True)

    row = pl.BlockSpec((tm, d), lambda i: (i, 0))
    vec = pl.BlockSpec((1, d), lambda i: (0, 0))
    return pl.pallas_call(
        body, name=name, grid=(t // tm,),
        in_specs=[row, vec, row, row], out_specs=[row, vec],
        out_shape=[jax.ShapeDtypeStruct((t, d), F32), jax.ShapeDtypeStruct((1, d), F32)],
        compiler_params=_params(1),
    )(x, g, dh, res)


def _final_loss(x, target, g, name):
    t, d = x.shape
    tm = min(512, t)

    def body(x_ref, t_ref, g_ref, dx_ref, dg_ref, loss_ref):
        xv = x_ref[...]
        gv = g_ref[...]
        r = lax.rsqrt(jnp.mean(xv * xv, axis=-1, keepdims=True) + EPS)
        xh = xv * r
        e = xh * gv - t_ref[...]
        dy = e * (1.0 / d)
        dxh = dy * gv
        dx_ref[...] = r * (dxh - xh * jnp.mean(dxh * xh, axis=-1, keepdims=True))

        @pl.when(pl.program_id(0) == 0)
        def _():
            dg_ref[...] = jnp.zeros_like(dg_ref)
            loss_ref[...] = jnp.zeros_like(loss_ref)

        dg_ref[...] += jnp.sum(dy * xh, axis=0, keepdims=True)
        part = 0.5 * jnp.sum(jnp.mean(e * e, axis=-1, keepdims=True), axis=0, keepdims=True)
        loss_ref[...] += jnp.broadcast_to(part, loss_ref.shape)

    row = pl.BlockSpec((tm, d), lambda i: (i, 0))
    vec = pl.BlockSpec((1, d), lambda i: (0, 0))
    return pl.pallas_call(
        body, name=name, grid=(t // tm,),
        in_specs=[row, row, vec],
        out_specs=[row, vec, pl.BlockSpec((1, LANES), lambda i: (0, 0))],
        out_shape=[jax.ShapeDtypeStruct((t, d), F32), jax.ShapeDtypeStruct((1, d), F32),
                   jax.ShapeDtypeStruct((1, LANES), F32)],
        compiler_params=_params(1),
    )(x, target, g)


def _mm_nn(a, w, name, res=None, out_dtype=F32, tm=512, tn=512):
    m, k = a.shape
    n = w.shape[1]
    tm, tn = min(tm, m), min(tn, n)

    def body(*refs):
        if res is None:
            a_ref, w_ref, o_ref = refs
            o_ref[...] = _dot(a_ref[...], w_ref[...]).astype(o_ref.dtype)
        else:
            a_ref, w_ref, r_ref, o_ref = refs
            o_ref[...] = (r_ref[...] + _dot(a_ref[...], w_ref[...])).astype(o_ref.dtype)

    in_specs = [pl.BlockSpec((tm, k), lambda j, i: (i, 0)), pl.BlockSpec((k, tn), lambda j, i: (0, j))]
    args = [a, w]
    if res is not None:
        in_specs.append(pl.BlockSpec((tm, tn), lambda j, i: (i, j)))
        args.append(res)
    return pl.pallas_call(
        body, name=name, grid=(n // tn, m // tm), in_specs=in_specs,
        out_specs=pl.BlockSpec((tm, tn), lambda j, i: (i, j)),
        out_shape=jax.ShapeDtypeStruct((m, n), out_dtype),
        compiler_params=_params(2),
    )(*args)


def _mm_nt(pairs, name, tm=512, tk=512):
    m = pairs[0][0].shape[0]
    k = pairs[0][1].shape[0]
    tm, tk = min(tm, m), min(tk, k)
    n_pairs = len(pairs)

    def body(*refs):
        o_ref = refs[-1]
        acc = None
        for q in range(n_pairs):
            d = _dot_nt(refs[2 * q][...], refs[2 * q + 1][...])
            acc = d if acc is None else acc + d
        o_ref[...] = acc

    in_specs, args = [], []
    for a, w in pairs:
        n = a.shape[1]
        in_specs += [pl.BlockSpec((tm, n), lambda j, i: (i, 0)), pl.BlockSpec((tk, n), lambda j, i: (j, 0))]
        args += [a, w]
    return pl.pallas_call(
        body, name=name, grid=(k // tk, m // tm), in_specs=in_specs,
        out_specs=pl.BlockSpec((tm, tk), lambda j, i: (i, j)),
        out_shape=jax.ShapeDtypeStruct((m, k), F32),
        compiler_params=_params(2),
    )(*args)


def _mm_tn(a, b, name, tk, tn, tt=512, out_dtype=BF16):
    t, k = a.shape
    n = b.shape[1]
    tt = min(tt, t)
    steps = t // tt

    def body(a_ref, b_ref, o_ref, acc):
        s = pl.program_id(2)

        @pl.when(s == 0)
        def _():
            acc[...] = jnp.zeros_like(acc)

        acc[...] += _dot_tn(a_ref[...], b_ref[...])

        @pl.when(s == steps - 1)
        def _():
            o_ref[...] = acc[...].astype(o_ref.dtype)

    return pl.pallas_call(
        body, name=name, grid=(k // tk, n // tn, steps),
        in_specs=[pl.BlockSpec((tt, tk), lambda i, j, s: (s, i)), pl.BlockSpec((tt, tn), lambda i, j, s: (s, j))],
        out_specs=pl.BlockSpec((tk, tn), lambda i, j, s: (i, j)),
        out_shape=jax.ShapeDtypeStruct((k, n), out_dtype),
        scratch_shapes=[pltpu.VMEM((tk, tn), F32)],
        compiler_params=_params(3),
    )(a, b)


def _ffn_up(h, wg, wu, name, tm=512, tn=256):
    m, k = h.shape
    n = wg.shape[1]
    tm = min(tm, m)

    def body(h_ref, wg_ref, wu_ref, gate_ref, up_ref, ff_ref):
        hv = h_ref[...]
        gate = _dot(hv, wg_ref[...])
        up = _dot(hv, wu_ref[...])
        gate_ref[...] = gate
        up_ref[...] = up
        ff_ref[...] = (gate * _sigmoid(gate) * up).astype(ff_ref.dtype)

    wspec = pl.BlockSpec((k, tn), lambda j, i: (0, j))
    ospec = pl.BlockSpec((tm, tn), lambda j, i: (i, j))
    return pl.pallas_call(
        body, name=name, grid=(n // tn, m // tm),
        in_specs=[pl.BlockSpec((tm, k), lambda j, i: (i, 0)), wspec, wspec],
        out_specs=[ospec, ospec, ospec],
        out_shape=[jax.ShapeDtypeStruct((m, n), F32), jax.ShapeDtypeStruct((m, n), F32),
                   jax.ShapeDtypeStruct((m, n), MXU_DTYPE)],
        compiler_params=_params(2),
    )(h, wg, wu)


def _ffn_bwd_act(dx, wd, gate, up, name, tm=512, tn=256):
    m, k = dx.shape
    n = wd.shape[0]
    tm = min(tm, m)

    def body(dx_ref, wd_ref, gate_ref, up_ref, dgate_ref, dup_ref):
        dff = _dot_nt(dx_ref[...], wd_ref[...])
        gate = gate_ref[...]
        sg = _sigmoid(gate)
        dgate_ref[...] = (dff * up_ref[...] * sg * (1.0 + gate * (1.0 - sg))).astype(dgate_ref.dtype)
        dup_ref[...] = (dff * gate * sg).astype(dup_ref.dtype)

    ospec = pl.BlockSpec((tm, tn), lambda j, i: (i, j))
    return pl.pallas_call(
        body, name=name, grid=(n // tn, m // tm),
        in_specs=[pl.BlockSpec((tm, k), lambda j, i: (i, 0)), pl.BlockSpec((tn, k), lambda j, i: (j, 0)),
                  ospec, ospec],
        out_specs=[ospec, ospec],
        out_shape=[jax.ShapeDtypeStruct((m, n), MXU_DTYPE), jax.ShapeDtypeStruct((m, n), MXU_DTYPE)],
        compiler_params=_params(2),
    )(dx, wd, gate, up)


def _chunk_masks():
    r = lax.broadcasted_iota(jnp.int32, (SUPER, SUPER), 0)
    c = lax.broadcasted_iota(jnp.int32, (SUPER, SUPER), 1)
    same = jnp.right_shift(r, 6) == jnp.right_shift(c, 6)
    lower = jnp.logical_and(same, c <= r)
    upper = jnp.logical_and(same, c >= r)
    return same, lower, upper


def _head_block_mask():
    r = lax.broadcasted_iota(jnp.int32, (LANES, LANES), 0)
    c = lax.broadcasted_iota(jnp.int32, (LANES, LANES), 1)
    return jnp.where(jnp.right_shift(r, 6) == jnp.right_shift(c, 6), 1.0, 0.0)


def _lower_bound(lb_raw):
    return 1.0 / (1.0 + jnp.exp(lb_raw[1:2, :] - lb_raw[0:1, :]))


def _hgrn_gates(q, hf, lb, same_b, lower_b):
    sig = _sigmoid(hf)
    f = lb + (1.0 - lb) * sig
    k = 1.0 - f
    lf = jnp.log(f)
    b = _rsum_left(lower_b, lf, 3)
    be = _rsum_left(same_b, lf, 3)
    eb = jnp.exp(b)
    enb = jnp.exp(-b)
    edb = jnp.exp(be - b)
    return sig, f, k, b, be, eb, enb, edb, q * eb, k * enb, k * edb


def _hgrn_fwd(proj, lower_bounds, norm_g, n_seq, seq):
    t = n_seq * seq
    n_super = seq // SUPER
    n_chunks = seq // CHUNK

    def body(q_ref, f_ref, i_ref, g_ref, lb_ref, ng_ref, out_ref, opre_ref, st_ref):
        masks = _head_masks()
        same, lower, _ = _chunk_masks()
        same_b, lower_b = _ones_where(same), _ones_where(lower)
        bd = _head_block_mask()
        lb = _lower_bound(lb_ref[...])
        ng = ng_ref[...]

        def step(sb, st):
            rows = pl.ds(pl.multiple_of(sb * SUPER, SUPER), SUPER)
            q, hf, v, hg = q_ref[rows, :], f_ref[rows, :], i_ref[rows, :], g_ref[rows, :]
            _, _, _, _, be, _, _, _, qe, ke, kd = _hgrn_gates(q, hf, lb, same_b, lower_b)
            o = None
            for m in masks:
                p = jnp.where(lower, _dot_nt(qe * m, ke), 0.0)
                d = _dot(p, v) * m
                o = d if o is None else o + d
            inter = []
            for c in range(SUPER // CHUNK):
                cr = slice(c * CHUNK, (c + 1) * CHUNK)
                st_ref[0, 0, sb * (SUPER // CHUNK) + c] = st
                inter.append(_dot_nt(qe[cr], st))
                dec = jnp.exp(be[c * CHUNK:c * CHUNK + 1, :])
                st = st * dec + bd * _dot_tn(v[cr], kd[cr])
            o = o + jnp.concatenate(inter, axis=0)
            opre_ref[rows, :] = o
            on = o * _head_rstd(o, masks) * ng
            out_ref[rows, :] = (on * hg * _sigmoid(hg)).astype(out_ref.dtype)
            return st

        lax.fori_loop(0, n_super, step, jnp.zeros((LANES, LANES), F32))

    def col(k):
        return pl.BlockSpec((seq, LANES), lambda p, b: (b, k * N_PAIRS + p))

    vec = lambda rows: pl.BlockSpec((rows, LANES), lambda p, b: (0, p))
    ospec = pl.BlockSpec((seq, LANES), lambda p, b: (b, p))
    return pl.pallas_call(
        body, name="hgrn_fwd", grid=(N_PAIRS, n_seq),
        in_specs=[col(0), col(1), col(2), col(3), vec(2), vec(1)],
        out_specs=[ospec, ospec,
                   pl.BlockSpec((1, 1, n_chunks, LANES, LANES), lambda p, b: (b, p, 0, 0, 0))],
        out_shape=[jax.ShapeDtypeStruct((t, GROUP), MXU_DTYPE), jax.ShapeDtypeStruct((t, GROUP), F32),
                   jax.ShapeDtypeStruct((n_seq, N_PAIRS, n_chunks, LANES, LANES), F32)],
        compiler_params=_params(2),
    )(proj, proj, proj, proj, lower_bounds, norm_g)


def _hgrn_bwd(proj, lower_bounds, norm_g, dmix, opre, states, n_seq, seq):
    t = n_seq * seq
    n_super = seq // SUPER
    n_chunks = seq // CHUNK
    per = SUPER // CHUNK

    def body(q_ref, f_ref, i_ref, g_ref, lb_ref, ng_ref, dm_ref, opre_ref, st_ref,
             dq_ref, df_ref, di_ref, dg_ref, dlb_ref, dng_ref):
        masks = _head_masks()
        same, lower, upper = _chunk_masks()
        same_b, lower_b, upper_b = _ones_where(same), _ones_where(lower), _ones_where(upper)
        bd = _head_block_mask()
        lb_raw = lb_ref[...]
        lb = _lower_bound(lb_raw)
        ng = ng_ref[...]
        row_id = lax.broadcasted_iota(jnp.int32, (SUPER, LANES), 0)

        @pl.when(pl.program_id(1) == 0)
        def _():
            dlb_ref[...] = jnp.zeros_like(dlb_ref)
            dng_ref[...] = jnp.zeros_like(dng_ref)

        def step(it, dst):
            sb = n_super - 1 - it
            rows = pl.ds(pl.multiple_of(sb * SUPER, SUPER), SUPER)
            q, hf, v, hg = q_ref[rows, :], f_ref[rows, :], i_ref[rows, :], g_ref[rows, :]
            sig, f, k, _, be, eb, enb, edb, qe, ke, kd = _hgrn_gates(q, hf, lb, same_b, lower_b)
            o = opre_ref[rows, :]
            r = _head_rstd(o, masks)
            oh = o * r
            dm = dm_ref[rows, :]
            sg = _sigmoid(hg)
            dg_ref[rows, :] = (dm * oh * ng * sg * (1.0 + hg * (1.0 - sg))).astype(dg_ref.dtype)
            don = dm * hg * sg
            dng_ref[...] += jnp.sum(don * oh, axis=0, keepdims=True)
            doh = don * ng
            do = r * (doh - oh * _head_mean(doh * oh, masks))
            dqe, dke, dv = None, None, None
            for m in masks:
                dom = do * m
                qem = qe * m
                p = jnp.where(lower, _dot_nt(qem, ke), 0.0)
                dp = jnp.where(lower, _dot_nt(dom, v), 0.0)
                a = _dot(dp, ke) * m
                bq = _dot_tn(dp, qem)
                cv = _dot_tn(p, dom)
                dqe = a if dqe is None else dqe + a
                dke = bq if dke is None else dke + bq
                dv = cv if dv is None else dv + cv
            dqe_i, dv_i, dkd_i, dbe_rows = [None] * per, [None] * per, [None] * per, None
            for c in reversed(range(per)):
                cr = slice(c * CHUNK, (c + 1) * CHUNK)
                st_prev = st_ref[0, 0, sb * per + c]
                dec = jnp.exp(be[c * CHUNK:c * CHUNK + 1, :])
                du = bd * dst
                dv_i[c] = _dot_nt(kd[cr], du)
                dkd_i[c] = _dot(v[cr], du)
                ddec = jnp.sum(dst * st_prev, axis=0, keepdims=True)
                dqe_i[c] = _dot(do[cr], st_prev)
                dst = bd * (dst * dec + _dot_tn(do[cr], qe[cr]))
                put = jnp.where(row_id == c * CHUNK, ddec * dec, 0.0)
                dbe_rows = put if dbe_rows is None else dbe_rows + put
            dqe = dqe + jnp.concatenate(dqe_i, axis=0)
            dv = dv + jnp.concatenate(dv_i, axis=0)
            dkd = jnp.concatenate(dkd_i, axis=0)
            dk = dke * enb + dkd * edb
            db = dqe * qe - dke * ke - dkd * kd
            dbe = dkd * kd + dbe_rows
            dlf = _rsum_left(upper_b, db, 3) + _rsum_left(same_b, dbe, 3)
            dfv = dlf / f - dk
            dq_ref[rows, :] = (dqe * eb).astype(dq_ref.dtype)
            di_ref[rows, :] = dv.astype(di_ref.dtype)
            df_ref[rows, :] = (dfv * (1.0 - lb) * sig * (1.0 - sig)).astype(df_ref.dtype)
            dlb = jnp.sum(dfv * (1.0 - sig), axis=0, keepdims=True)
            da0 = dlb * lb * (1.0 - lb)
            dlb_ref[0:1, :] += da0
            dlb_ref[1:2, :] -= da0
            return dst

        lax.fori_loop(0, n_super, step, jnp.zeros((LANES, LANES), F32))

    def col(k):
        return pl.BlockSpec((seq, LANES), lambda p, b: (b, k * N_PAIRS + p))

    vec = lambda rows: pl.BlockSpec((rows, LANES), lambda p, b: (0, p))
    ospec = pl.BlockSpec((seq, LANES), lambda p, b: (b, p))
    piece = jax.ShapeDtypeStruct((t, GROUP), MXU_DTYPE)
    return pl.pallas_call(
        body, name="hgrn_bwd", grid=(N_PAIRS, n_seq),
        in_specs=[col(0), col(1), col(2), col(3), vec(2), vec(1), ospec, ospec,
                  pl.BlockSpec((1, 1, n_chunks, LANES, LANES), lambda p, b: (b, p, 0, 0, 0))],
        out_specs=[ospec, ospec, ospec, ospec, vec(2), vec(1)],
        out_shape=[piece, piece, piece, piece,
                   jax.ShapeDtypeStruct((2, GROUP), F32), jax.ShapeDtypeStruct((1, GROUP), F32)],
        compiler_params=_params(2),
    )(proj, proj, proj, proj, lower_bounds, norm_g, dmix, opre, states)


SB_SCALE = 1.0 / math.sqrt(D_HEAD)


def _tile_masks():
    r = lax.broadcasted_iota(jnp.int32, (SB_BLOCK, SB_BLOCK), 0)
    c = lax.broadcasted_iota(jnp.int32, (SB_BLOCK, SB_BLOCK), 1)
    return r, c


def _sb_fwd(proj, norm_g, n_seq, seq):
    t = n_seq * seq
    nq = seq // SB_BLOCK
    q0, k0, v0 = 4 * N_PAIRS, 5 * N_PAIRS, 6 * N_PAIRS

    def body(q_ref, k_ref, v_ref, ng_ref, out_ref, opre_ref, ctot_ref):
        i = pl.program_id(2)
        masks = _head_masks()
        r, c = _tile_masks()
        strict = c < r
        suffix = _ones_where(r >= c)
        qs = q_ref[...] * SB_SCALE
        opre = jnp.zeros((SB_BLOCK, LANES), F32)
        ctot = jnp.zeros((SB_BLOCK, LANES), F32)
        for m in masks:
            qh = (qs * m).astype(MXU_DTYPE)

            def tile(j, run, acc, diag):
                rows = pl.ds(pl.multiple_of(j * SB_BLOCK, SB_BLOCK), SB_BLOCK)
                z = _dot_nt(qh, k_ref[rows, :])
                cc = _softplus(z)
                if diag:
                    cc = jnp.where(strict, cc, 0.0)
                cs = _rsum_right(cc, suffix, 2) + run
                a = jnp.exp(z - cs)
                if diag:
                    a = jnp.where(strict, a, 0.0)
                acc = acc + _dot(a, v_ref[rows, :])
                return run + jnp.sum(cc, axis=1, keepdims=True), acc

            run, acc = tile(i, jnp.zeros((SB_BLOCK, 1), F32), jnp.zeros((SB_BLOCK, LANES), F32), True)
            run, acc = lax.fori_loop(0, i, lambda s, carry: tile(i - 1 - s, carry[0], carry[1], False), (run, acc))
            opre = opre + acc * m
            ctot = ctot + run * m
        opre_ref[...] = opre
        ctot_ref[...] = ctot
        out_ref[...] = (opre * _head_rstd(opre, masks) * ng_ref[...]).astype(out_ref.dtype)

    qspec = pl.BlockSpec((SB_BLOCK, LANES), lambda p, b, i: (b * nq + i, q0 + p))
    ospec = pl.BlockSpec((SB_BLOCK, LANES), lambda p, b, i: (b * nq + i, p))
    return pl.pallas_call(
        body, name="sb_fwd", grid=(N_PAIRS, n_seq, nq),
        in_specs=[qspec,
                  pl.BlockSpec((seq, LANES), lambda p, b, i: (b, k0 + p)),
                  pl.BlockSpec((seq, LANES), lambda p, b, i: (b, v0 + p)),
                  pl.BlockSpec((1, LANES), lambda p, b, i: (0, p))],
        out_specs=[ospec, ospec, ospec],
        out_shape=[jax.ShapeDtypeStruct((t, GROUP), MXU_DTYPE), jax.ShapeDtypeStruct((t, GROUP), F32),
                   jax.ShapeDtypeStruct((t, GROUP), F32)],
        compiler_params=_params(3),
    )(proj, proj, proj, norm_g)


def _sb_bwd(proj, norm_g, dmix, opre, ctot, n_seq, seq):
    t = n_seq * seq
    nq = seq // SB_BLOCK
    q0, k0, v0 = 4 * N_PAIRS, 5 * N_PAIRS, 6 * N_PAIRS

    def body(q_ref, k_ref, v_ref, ng_ref, dm_ref, opre_ref, ctot_ref,
             dq_ref, dk_ref, dv_ref, dng_ref, dk_acc, dv_acc):
        p_id, b_id, i = pl.program_id(0), pl.program_id(1), pl.program_id(2)
        masks = _head_masks()
        r, c = _tile_masks()
        strict = c < r
        before = _ones_where(r < c)
        upto = _ones_where(r <= c)

        @pl.when(i == 0)
        def _():
            dk_acc[...] = jnp.zeros_like(dk_acc)
            dv_acc[...] = jnp.zeros_like(dv_acc)

        @pl.when(jnp.logical_and(b_id == 0, i == 0))
        def _():
            dng_ref[...] = jnp.zeros_like(dng_ref)

        o = opre_ref[...]
        rs = _head_rstd(o, masks)
        oh = o * rs
        dm = dm_ref[...]
        dng_ref[...] += jnp.sum(dm * oh, axis=0, keepdims=True)
        doh = dm * ng_ref[...]
        do = rs * (doh - oh * _head_mean(doh * oh, masks))

        qs = q_ref[...] * SB_SCALE
        ct = ctot_ref[...]
        dq = jnp.zeros((SB_BLOCK, LANES), F32)
        for m in masks:
            qh = (qs * m).astype(MXU_DTYPE)
            dom = (do * m).astype(MXU_DTYPE)
            total = jnp.max(ct * m, axis=1, keepdims=True)

            def tile(j, pc, pdl, dqa, diag):
                rows = pl.ds(pl.multiple_of(j * SB_BLOCK, SB_BLOCK), SB_BLOCK)
                kj = k_ref[rows, :].astype(MXU_DTYPE)
                vj = v_ref[rows, :].astype(MXU_DTYPE)
                z = _dot_nt(qh, kj)
                sp = _softplus(z)
                sig = jnp.exp(z - sp)
                cc = jnp.where(strict, sp, 0.0) if diag else sp
                cs = total - pc - _rsum_right(cc, before, 2)
                a = jnp.exp(z - cs)
                if diag:
                    a = jnp.where(strict, a, 0.0)
                dl = a * _dot_nt(dom, vj)
                dz = dl - sig * (pdl + _rsum_right(dl, upto, 2))
                if diag:
                    dz = jnp.where(strict, dz, 0.0)
                dzb = dz.astype(MXU_DTYPE)
                dqa = dqa + _dot(dzb, kj)
                dk_acc[rows, :] += _dot_tn(dzb, qh)
                dv_acc[rows, :] += _dot_tn(a, dom)
                return pc + jnp.sum(cc, axis=1, keepdims=True), pdl + jnp.sum(dl, axis=1, keepdims=True), dqa

            zero = jnp.zeros((SB_BLOCK, 1), F32)
            carry = lax.fori_loop(0, i, lambda j, cy: tile(j, cy[0], cy[1], cy[2], False),
                                  (zero, zero, jnp.zeros((SB_BLOCK, LANES), F32)))
            _, _, dqa = tile(i, carry[0], carry[1], carry[2], True)
            dq = dq + dqa * m
        dq_ref[...] = (dq * SB_SCALE).astype(dq_ref.dtype)

        @pl.when(i == nq - 1)
        def _():
            dk_ref[...] = dk_acc[...].astype(dk_ref.dtype)
            dv_ref[...] = dv_acc[...].astype(dv_ref.dtype)

    qspec = pl.BlockSpec((SB_BLOCK, LANES), lambda p, b, i: (b * nq + i, q0 + p))
    ospec = pl.BlockSpec((SB_BLOCK, LANES), lambda p, b, i: (b * nq + i, p))
    dmspec = pl.BlockSpec((SB_BLOCK, LANES), lambda p, b, i: (b * nq + i, N_PAIRS + p))
    full = lambda k: pl.BlockSpec((seq, LANES), lambda p, b, i: (b, k + p))
    vec = pl.BlockSpec((1, LANES), lambda p, b, i: (0, p))
    piece = jax.ShapeDtypeStruct((t, GROUP), MXU_DTYPE)
    return pl.pallas_call(
        body, name="sb_bwd", grid=(N_PAIRS, n_seq, nq),
        in_specs=[qspec, full(k0), full(v0), vec, dmspec, ospec, ospec],
        out_specs=[ospec, full(0), full(0), vec],
        out_shape=[piece, piece, piece, jax.ShapeDtypeStruct((1, GROUP), F32)],
        scratch_shapes=[pltpu.VMEM((seq, LANES), F32), pltpu.VMEM((seq, LANES), F32)],
        compiler_params=_params(3),
    )(proj, proj, proj, norm_g, dmix, opre, ctot)


def _mesh_place():
    x, y, c = lax.axis_index("x"), lax.axis_index("y"), lax.axis_index("c")
    return x, y, c


def _peer(x, y, c, k):
    px = lax.rem(x + ((k >> 2) & 1), 2)
    py = lax.rem(y + ((k >> 1) & 1), 2)
    pc = lax.rem(c + (k & 1), 2)
    return (px, py, pc), 4 * px + 2 * py + pc


def _gather_weights(wpack):
    rows, width = wpack.shape

    def body(w_ref, out_ref, stage, send_sems, recv_sems, local_sem):
        x, y, c = _mesh_place()
        me = 4 * x + 2 * y + c
        stage[...] = w_ref[...].astype(stage.dtype)
        mine = pltpu.make_async_copy(stage, out_ref.at[me], local_sem)
        mine.start()
        sends = []
        for k in range(1, N_DEV):
            peer, _ = _peer(x, y, c, k)
            cp = pltpu.make_async_remote_copy(
                src_ref=stage, dst_ref=out_ref.at[me], send_sem=send_sems.at[k - 1], recv_sem=recv_sems.at[k - 1],
                device_id=peer, device_id_type=pl.DeviceIdType.MESH)
            cp.start()
            sends.append(cp)
        for k in range(1, N_DEV):
            peer, pidx = _peer(x, y, c, k)
            pltpu.make_async_remote_copy(
                src_ref=stage, dst_ref=out_ref.at[pidx], send_sem=send_sems.at[k - 1], recv_sem=recv_sems.at[k - 1],
                device_id=peer, device_id_type=pl.DeviceIdType.MESH).wait_recv()
        for cp in sends:
            cp.wait_send()
        mine.wait()

    return pl.pallas_call(
        body, name="gather_weights",
        in_specs=[pl.BlockSpec(memory_space=pltpu.VMEM)],
        out_specs=pl.BlockSpec(memory_space=pltpu.HBM),
        out_shape=jax.ShapeDtypeStruct((N_DEV, rows, width), BF16),
        scratch_shapes=[pltpu.VMEM((rows, width), BF16), pltpu.SemaphoreType.DMA((N_DEV - 1,)),
                        pltpu.SemaphoreType.DMA((N_DEV - 1,)), pltpu.SemaphoreType.DMA],
        compiler_params=pltpu.CompilerParams(vmem_limit_bytes=VMEM_LIMIT),
    )(wpack)


def _scatter_grads(gpack, small):
    _, rows, width = gpack.shape

    def body(g_ref, s_ref, land_ref, sland_ref, send_sems, recv_sems, ssend_sems, srecv_sems, local_sems):
        x, y, c = _mesh_place()
        me = 4 * x + 2 * y + c
        mine = pltpu.make_async_copy(g_ref.at[me], land_ref.at[me], local_sems.at[0])
        smine = pltpu.make_async_copy(s_ref, sland_ref.at[me], local_sems.at[1])
        mine.start()
        smine.start()
        sends = []
        for k in range(1, N_DEV):
            peer, pidx = _peer(x, y, c, k)
            cp = pltpu.make_async_remote_copy(
                src_ref=g_ref.at[pidx], dst_ref=land_ref.at[me], send_sem=send_sems.at[k - 1],
                recv_sem=recv_sems.at[k - 1], device_id=peer, device_id_type=pl.DeviceIdType.MESH)
            scp = pltpu.make_async_remote_copy(
                src_ref=s_ref, dst_ref=sland_ref.at[me], send_sem=ssend_sems.at[k - 1],
                recv_sem=srecv_sems.at[k - 1], device_id=peer, device_id_type=pl.DeviceIdType.MESH)
            cp.start()
            scp.start()
            sends += [cp, scp]
        for k in range(1, N_DEV):
            peer, pidx = _peer(x, y, c, k)
            pltpu.make_async_remote_copy(
                src_ref=g_ref.at[pidx], dst_ref=land_ref.at[pidx], send_sem=send_sems.at[k - 1],
                recv_sem=recv_sems.at[k - 1], device_id=peer, device_id_type=pl.DeviceIdType.MESH).wait_recv()
            pltpu.make_async_remote_copy(
                src_ref=s_ref, dst_ref=sland_ref.at[pidx], send_sem=ssend_sems.at[k - 1],
                recv_sem=srecv_sems.at[k - 1], device_id=peer, device_id_type=pl.DeviceIdType.MESH).wait_recv()
        for cp in sends:
            cp.wait_send()
        mine.wait()
        smine.wait()

    hbm = pl.BlockSpec(memory_space=pltpu.HBM)
    sems = pltpu.SemaphoreType.DMA((N_DEV - 1,))
    return pl.pallas_call(
        body, name="scatter_grads",
        in_specs=[hbm, hbm], out_specs=[hbm, hbm],
        out_shape=[jax.ShapeDtypeStruct((N_DEV, rows, width), gpack.dtype),
                   jax.ShapeDtypeStruct((N_DEV,) + small.shape, small.dtype)],
        scratch_shapes=[sems, sems, sems, sems, pltpu.SemaphoreType.DMA((2,))],
    )(gpack, small)


def _adam(w, g, m, v):
    m = ADAM_B1 * m + (1.0 - ADAM_B1) * g
    v = ADAM_B2 * v + (1.0 - ADAM_B2) * (g * g)
    m_hat = m / (1.0 - ADAM_B1 ** ADAM_STEP)
    v_hat = v / (1.0 - ADAM_B2 ** ADAM_STEP)
    delta = -ADAM_LR * (m_hat / (jnp.sqrt(v_hat) + ADAM_EPS) + ADAM_WD * w)
    return delta, m, v


def _reduce_adamw(land, w, m, v, name, tr):
    _, rows, width = land.shape

    def body(land_ref, w_ref, m_ref, v_ref, g_ref, d_ref, nm_ref, nv_ref):
        g = land_ref[0].astype(F32)
        for d in range(1, N_DEV):
            g = g + land_ref[d].astype(F32)
        delta, nm, nv = _adam(w_ref[...], g, m_ref[...], v_ref[...])
        g_ref[...] = g
        d_ref[...] = delta
        nm_ref[...] = nm
        nv_ref[...] = nv

    row = pl.BlockSpec((tr, width), lambda i: (i, 0))
    out = jax.ShapeDtypeStruct((rows, width), F32)
    return pl.pallas_call(
        body, name=name, grid=(rows // tr,),
        in_specs=[pl.BlockSpec((N_DEV, tr, width), lambda i: (0, i, 0)), row, row, row],
        out_specs=[row, row, row, row], out_shape=[out, out, out, out],
        compiler_params=_params(1),
    )(land, w, m, v)


def _pack_shards(w_in, w_out, w_gate, w_up, w_down):
    return jnp.concatenate([w_in.reshape(ROWS_IN, D_MODEL), w_out.reshape(ROWS_OUT, D_MODEL),
                            w_gate.reshape(ROWS_FF, D_MODEL), w_up.reshape(ROWS_FF, D_MODEL),
                            w_down.reshape(ROWS_FF, D_MODEL)], axis=0)


def _unpack_shards(p):
    return (p[:OFF_OUT].reshape(1, D_MODEL, IN_COLS // N_DEV), p[OFF_OUT:OFF_GATE].reshape(1, ROWS_OUT, D_MODEL),
            p[OFF_GATE:OFF_UP].reshape(1, D_MODEL, ROWS_FF), p[OFF_UP:OFF_DOWN].reshape(1, D_MODEL, ROWS_FF),
            p[OFF_DOWN:].reshape(1, ROWS_FF, D_MODEL))


def _cols_from_slots(g, cols):
    return g.reshape(N_DEV, D_MODEL, cols).transpose(1, 0, 2).reshape(D_MODEL, N_DEV * cols)


def _slots_from_cols(w, cols):
    return w.reshape(D_MODEL, N_DEV, cols).transpose(1, 0, 2).reshape(N_DEV, cols, D_MODEL)


def _unpack_full(g):
    return (_cols_from_slots(g[:, :OFF_OUT], IN_COLS // N_DEV), g[:, OFF_OUT:OFF_GATE].reshape(D_MODEL, D_MODEL),
            _cols_from_slots(g[:, OFF_GATE:OFF_UP], ROWS_FF), _cols_from_slots(g[:, OFF_UP:OFF_DOWN], ROWS_FF),
            g[:, OFF_DOWN:].reshape(D_FF, D_MODEL))


def _pack_full(dw_in, dw_out, dw_gate, dw_up, dw_down):
    return jnp.concatenate([_slots_from_cols(dw_in, IN_COLS // N_DEV), dw_out.reshape(N_DEV, ROWS_OUT, D_MODEL),
                            _slots_from_cols(dw_gate, ROWS_FF), _slots_from_cols(dw_up, ROWS_FF),
                            dw_down.reshape(N_DEV, ROWS_FF, D_MODEL)], axis=1)


def _pack_small(mix_g, lb, hgrn_g, sb_g, ffn_g, final_g, extra):
    rows = [mix_g.reshape(1, D_MODEL), lb.reshape(1, D_MODEL),
            jnp.concatenate([hgrn_g.reshape(1, GROUP), sb_g.reshape(1, GROUP)], axis=1),
            ffn_g.reshape(1, D_MODEL), final_g.reshape(1, D_MODEL), extra,
            jnp.zeros((SMALL_ROWS - 6, D_MODEL), F32)]
    return jnp.concatenate(rows, axis=0)


def _unpack_small(p):
    return (p[0:1], p[1].reshape(2, GROUP), p[2:3, :GROUP], p[2:3, GROUP:], p[3:4], p[4])


def kernel(x, mix_norm_g, w_in, lower_bounds, hgrn_norm_g, sb_norm_g, w_out, ffn_norm_g, w_gate, w_up, w_down, final_norm_g, loss_target, m_mix_norm_g, m_w_in, m_lower_bounds, m_hgrn_norm_g, m_sb_norm_g, m_w_out, m_ffn_norm_g, m_w_gate, m_w_up, m_w_down, m_final_norm_g, v_mix_norm_g, v_w_in, v_lower_bounds, v_hgrn_norm_g, v_sb_norm_g, v_w_out, v_ffn_norm_g, v_w_gate, v_w_up, v_w_down, v_final_norm_g):
    n_seq, seq, d = x.shape
    t = n_seq * seq
    x2d = x.reshape(t, d)
    tgt = loss_target.reshape(t, d)
    final_g = final_norm_g.reshape(1, d)

    wpack = _pack_shards(w_in, w_out, w_gate, w_up, w_down)
    wf_in, wf_out, wf_gate, wf_up, wf_down = _unpack_full(_gather_weights(wpack))

    h1 = _rmsnorm_fwd(x2d, mix_norm_g, "norm_mix")
    proj = _mm_nn(h1, wf_in, "proj")
    o_a, oa_pre, states = _hgrn_fwd(proj, lower_bounds, hgrn_norm_g, n_seq, seq)
    o_b, ob_pre, ctot = _sb_fwd(proj, sb_norm_g, n_seq, seq)
    mixed = jnp.concatenate([o_a, o_b], axis=1)
    x1 = _mm_nn(mixed, wf_out, "mix_out", res=x2d)
    h2 = _rmsnorm_fwd(x1, ffn_norm_g, "norm_ffn")
    gate, up, ff = _ffn_up(h2, wf_gate, wf_up, "ffn_up")
    x2 = _mm_nn(ff, wf_down, "ffn_down", res=x1)
    dx2, d_final_g, loss_part = _final_loss(x2, tgt, final_g, "loss_head")

    dgate, dup = _ffn_bwd_act(dx2, wf_down, gate, up, "ffn_bwd_act")
    dw_down = _mm_tn(ff, dx2, "dw_down", tk=1408, tn=1024)
    dh2 = _mm_nt([(dgate, wf_gate), (dup, wf_up)], "dh_ffn", tm=256)
    dw_gate = _mm_tn(h2, dgate, "dw_gate", tk=512, tn=1408)
    dw_up = _mm_tn(h2, dup, "dw_up", tk=512, tn=1408)
    dx1, d_ffn_g = _rmsnorm_bwd(x1, ffn_norm_g, dh2, dx2, "norm_ffn_bwd")
    dmix = _mm_nt([(dx1, wf_out)], "dmix")
    dw_out = _mm_tn(mixed, dx1, "dw_out", tk=512, tn=1024)
    dsq, dsk, dsv, d_sb_g = _sb_bwd(proj, sb_norm_g, dmix, ob_pre, ctot, n_seq, seq)
    dhq, dhf, dhi, dhg, d_lb, d_hgrn_g = _hgrn_bwd(proj, lower_bounds, hgrn_norm_g, dmix, oa_pre, states, n_seq, seq)
    dproj = jnp.concatenate([dhq, dhf, dhi, dhg, dsq, dsk, dsv], axis=1)
    dh1 = _mm_nt([(dproj, wf_in)], "dh_mix", tm=256)
    dw_in = _mm_tn(h1, dproj, "dw_in", tk=512, tn=1792)
    dx, d_mix_g = _rmsnorm_bwd(x2d, mix_norm_g, dh1, dx1, "norm_mix_bwd")

    gpack = _pack_full(dw_in, dw_out, dw_gate, dw_up, dw_down)
    extra = jnp.pad(loss_part, ((0, 0), (0, d - LANES)))
    small = _pack_small(d_mix_g, d_lb, d_hgrn_g, d_sb_g, d_ffn_g, d_final_g, extra)
    land, small_land = _scatter_grads(gpack, small)
    mpack = _pack_shards(m_w_in, m_w_out, m_w_gate, m_w_up, m_w_down)
    vpack = _pack_shards(v_w_in, v_w_out, v_w_gate, v_w_up, v_w_down)
    g_big, d_big, m_big, v_big = _reduce_adamw(land, wpack, mpack, vpack, "adamw_big", tr=272)
    zero_row = jnp.zeros((1, d), F32)
    w_small = _pack_small(mix_norm_g, lower_bounds, hgrn_norm_g, sb_norm_g, ffn_norm_g, final_norm_g, zero_row)
    m_small = _pack_small(m_mix_norm_g, m_lower_bounds, m_hgrn_norm_g, m_sb_norm_g, m_ffn_norm_g, m_final_norm_g, zero_row)
    v_small = _pack_small(v_mix_norm_g, v_lower_bounds, v_hgrn_norm_g, v_sb_norm_g, v_ffn_norm_g, v_final_norm_g, zero_row)
    g_sm, d_sm, m_sm, v_sm = _reduce_adamw(small_land, w_small, m_small, v_small, "adamw_small", tr=SMALL_ROWS)

    loss = g_sm[5, 0]
    big = [_unpack_shards(p) for p in (g_big, d_big, m_big, v_big)]
    sm = [_unpack_small(p) for p in (g_sm, d_sm, m_sm, v_sm)]
    outs = [loss, dx.reshape(n_seq, seq, d)]
    for bg, s in zip(big, sm):
        b_in, b_out, b_gate, b_up, b_down = bg
        s_mix, s_lb, s_hgrn, s_sb, s_ffn, s_final = s
        outs += [s_mix, b_in, s_lb, s_hgrn, s_sb, b_out, s_ffn, b_gate, b_up, b_down, s_final]
    return tuple(outs)
```

```python
import functools
import math

import jax
import jax.numpy as jnp
from jax import lax
from jax.experimental import pallas as pl
from jax.experimental.pallas import tpu as pltpu

F32 = jnp.float32
BF16 = jnp.bfloat16
MXU_DTYPE = BF16

EPS = 1e-6
D_MODEL = 1024
N_HEADS = 8
D_HEAD = 64
GROUP = N_HEADS * D_HEAD
IN_COLS = 7 * GROUP
D_FF = 2816
CHUNK = 64
LANES = 128
N_PAIRS = GROUP // LANES
SUPER = 256
SB_BLOCK = 256
N_DEV = 8

ADAM_LR = 0.001
ADAM_B1 = 0.9
ADAM_B2 = 0.999
ADAM_EPS = 1e-08
ADAM_WD = 0.01
ADAM_STEP = 10

ROWS_IN = D_MODEL * (IN_COLS // N_DEV) // D_MODEL
ROWS_OUT = D_MODEL // N_DEV
ROWS_FF = D_FF // N_DEV
OFF_OUT = ROWS_IN
OFF_GATE = OFF_OUT + ROWS_OUT
OFF_UP = OFF_GATE + ROWS_FF
OFF_DOWN = OFF_UP + ROWS_FF
PACK_ROWS = OFF_DOWN + ROWS_FF
SMALL_ROWS = 8

VMEM_LIMIT = 48 * 1024 * 1024


def _params(n_axes, vmem=VMEM_LIMIT):
    return pltpu.CompilerParams(dimension_semantics=("arbitrary",) * n_axes, vmem_limit_bytes=vmem)


def _dot(a, b):
    return jnp.dot(a.astype(MXU_DTYPE), b.astype(MXU_DTYPE), preferred_element_type=F32)


def _dot_nt(a, b):
    return lax.dot_general(a.astype(MXU_DTYPE), b.astype(MXU_DTYPE), (((1,), (1,)), ((), ())),
                           preferred_element_type=F32)


def _dot_tn(a, b):
    return lax.dot_general(a.astype(MXU_DTYPE), b.astype(MXU_DTYPE), (((0,), (0,)), ((), ())),
                           preferred_element_type=F32)


def _split(x, parts):
    out, r = [], x
    for _ in range(parts):
        h = r.astype(BF16)
        out.append(h)
        r = r - h.astype(F32)
    return out


def _rsum_right(x, u, parts):
    acc = None
    for h in _split(x, parts):
        d = jnp.dot(h, u, preferred_element_type=F32)
        acc = d if acc is None else acc + d
    return acc


def _rsum_left(u, x, parts):
    acc = None
    for h in _split(x, parts):
        d = jnp.dot(u, h, preferred_element_type=F32)
        acc = d if acc is None else acc + d
    return acc


def _ones_where(mask):
    return jnp.where(mask, 1.0, 0.0).astype(BF16)


def _sigmoid(x):
    return 1.0 / (1.0 + jnp.exp(-x))


def _softplus(x):
    return jnp.maximum(x, 0.0) + jnp.log(1.0 + jnp.exp(-jnp.abs(x)))


def _head_masks():
    lane = lax.broadcasted_iota(jnp.int32, (1, LANES), 1)
    return [jnp.where(lane < D_HEAD, 1.0, 0.0), jnp.where(lane >= D_HEAD, 1.0, 0.0)]


def _head_rstd(o, masks):
    sq = o * o
    r = None
    for m in masks:
        ms = jnp.sum(sq * m, axis=1, keepdims=True) * (1.0 / D_HEAD)
        t = lax.rsqrt(ms + EPS) * m
        r = t if r is None else r + t
    return r


def _head_mean(t, masks):
    out = None
    for m in masks:
        v = jnp.sum(t * m, axis=1, keepdims=True) * (1.0 / D_HEAD) * m
        out = v if out is None else out + v
    return out


def _rmsnorm_fwd(x, g, name):
    t, d = x.shape
    tm = min(512, t)

    def body(x_ref, g_ref, o_ref):
        xv = x_ref[...]
        r = lax.rsqrt(jnp.mean(xv * xv, axis=-1, keepdims=True) + EPS)
        o_ref[...] = (xv * r * g_ref[...]).astype(o_ref.dtype)

    return pl.pallas_call(
        body, name=name, grid=(t // tm,),
        in_specs=[pl.BlockSpec((tm, d), lambda i: (i, 0)), pl.BlockSpec((1, d), lambda i: (0, 0))],
        out_specs=pl.BlockSpec((tm, d), lambda i: (i, 0)),
        out_shape=jax.ShapeDtypeStruct((t, d), MXU_DTYPE),
        compiler_params=_params(1),
    )(x, g)


def _rmsnorm_bwd(x, g, dh, res, name):
    t, d = x.shape
    tm = min(512, t)

    def body(x_ref, g_ref, dh_ref, res_ref, dx_ref, dg_ref):
        xv = x_ref[...]
        r = lax.rsqrt(jnp.mean(xv * xv, axis=-1, keepdims=True) + EPS)
        xh = xv * r
        dhv = dh_ref[...]
        dxh = dhv * g_ref[...]
        dx_ref[...] = res_ref[...] + r * (dxh - xh * jnp.mean(dxh * xh, axis=-1, keepdims=True))

        @pl.when(pl.program_id(0) == 0)
        def _():
            dg_ref[...] = jnp.zeros_like(dg_ref)

        dg_ref[...] += jnp.sum(dhv * xh, axis=0, keepdims=True)

    row = pl.BlockSpec((tm, d), lambda i: (i, 0))
    vec = pl.BlockSpec((1, d), lambda i: (0, 0))
    return pl.pallas_call(
        body, name=name, grid=(t // tm,),
        in_specs=[row, vec, row, row], out_specs=[row, vec],
        out_shape=[jax.ShapeDtypeStruct((t, d), F32), jax.ShapeDtypeStruct((1, d), F32)],
        compiler_params=_params(1),
    )(x, g, dh, res)


def _final_loss(x, target, g, name):
    t, d = x.shape
    tm = min(512, t)

    def body(x_ref, t_ref, g_ref, dx_ref, dg_ref, loss_ref):
        xv = x_ref[...]
        gv = g_ref[...]
        r = lax.rsqrt(jnp.mean(xv * xv, axis=-1, keepdims=True) + EPS)
        xh = xv * r
        e = xh * gv - t_ref[...]
        dy = e * (1.0 / d)
        dxh = dy * gv
        dx_ref[...] = r * (dxh - xh * jnp.mean(dxh * xh, axis=-1, keepdims=True))

        @pl.when(pl.program_id(0) == 0)
        def _():
            dg_ref[...] = jnp.zeros_like(dg_ref)
            loss_ref[...] = jnp.zeros_like(loss_ref)

        dg_ref[...] += jnp.sum(dy * xh, axis=0, keepdims=True)
        part = 0.5 * jnp.sum(jnp.mean(e * e, axis=-1, keepdims=True), axis=0, keepdims=True)
        loss_ref[...] += jnp.broadcast_to(part, loss_ref.shape)

    row = pl.BlockSpec((tm, d), lambda i: (i, 0))
    vec = pl.BlockSpec((1, d), lambda i: (0, 0))
    return pl.pallas_call(
        body, name=name, grid=(t // tm,),
        in_specs=[row, row, vec],
        out_specs=[row, vec, pl.BlockSpec((1, LANES), lambda i: (0, 0))],
        out_shape=[jax.ShapeDtypeStruct((t, d), F32), jax.ShapeDtypeStruct((1, d), F32),
                   jax.ShapeDtypeStruct((1, LANES), F32)],
        compiler_params=_params(1),
    )(x, target, g)


def _mm_nn(a, w, name, res=None, out_dtype=F32, tm=512, tn=512):
    m, k = a.shape
    n = w.shape[1]
    tm, tn = min(tm, m), min(tn, n)

    def body(*refs):
        if res is None:
            a_ref, w_ref, o_ref = refs
            o_ref[...] = _dot(a_ref[...], w_ref[...]).astype(o_ref.dtype)
        else:
            a_ref, w_ref, r_ref, o_ref = refs
            o_ref[...] = (r_ref[...] + _dot(a_ref[...], w_ref[...])).astype(o_ref.dtype)

    in_specs = [pl.BlockSpec((tm, k), lambda j, i: (i, 0)), pl.BlockSpec((k, tn), lambda j, i: (0, j))]
    args = [a, w]
    if res is not None:
        in_specs.append(pl.BlockSpec((tm, tn), lambda j, i: (i, j)))
        args.append(res)
    return pl.pallas_call(
        body, name=name, grid=(n // tn, m // tm), in_specs=in_specs,
        out_specs=pl.BlockSpec((tm, tn), lambda j, i: (i, j)),
        out_shape=jax.ShapeDtypeStruct((m, n), out_dtype),
        compiler_params=_params(2),
    )(*args)


def _mm_nt(pairs, name, tm=512, tk=512):
    m = pairs[0][0].shape[0]
    k = pairs[0][1].shape[0]
    tm, tk = min(tm, m), min(tk, k)
    n_pairs = len(pairs)

    def body(*refs):
        o_ref = refs[-1]
        acc = None
        for q in range(n_pairs):
            d = _dot_nt(refs[2 * q][...], refs[2 * q + 1][...])
            acc = d if acc is None else acc + d
        o_ref[...] = acc

    in_specs, args = [], []
    for a, w in pairs:
        n = a.shape[1]
        in_specs += [pl.BlockSpec((tm, n), lambda j, i: (i, 0)), pl.BlockSpec((tk, n), lambda j, i: (j, 0))]
        args += [a, w]
    return pl.pallas_call(
        body, name=name, grid=(k // tk, m // tm), in_specs=in_specs,
        out_specs=pl.BlockSpec((tm, tk), lambda j, i: (i, j)),
        out_shape=jax.ShapeDtypeStruct((m, k), F32),
        compiler_params=_params(2),
    )(*args)


def _mm_tn(a, b, name, tk, tn, tt=512, out_dtype=BF16):
    t, k = a.shape
    n = b.shape[1]
    tt = min(tt, t)
    steps = t // tt

    def body(a_ref, b_ref, o_ref, acc):
        s = pl.program_id(2)

        @pl.when(s == 0)
        def _():
            acc[...] = jnp.zeros_like(acc)

        acc[...] += _dot_tn(a_ref[...], b_ref[...])

        @pl.when(s == steps - 1)
        def _():
            o_ref[...] = acc[...].astype(o_ref.dtype)

    return pl.pallas_call(
        body, name=name, grid=(k // tk, n // tn, steps),
        in_specs=[pl.BlockSpec((tt, tk), lambda i, j, s: (s, i)), pl.BlockSpec((tt, tn), lambda i, j, s: (s, j))],
        out_specs=pl.BlockSpec((tk, tn), lambda i, j, s: (i, j)),
        out_shape=jax.ShapeDtypeStruct((k, n), out_dtype),
        scratch_shapes=[pltpu.VMEM((tk, tn), F32)],
        compiler_params=_params(3),
    )(a, b)


def _ffn_up(h, wg, wu, name, tm=512, tn=256):
    m, k = h.shape
    n = wg.shape[1]
    tm = min(tm, m)

    def body(h_ref, wg_ref, wu_ref, gate_ref, up_ref, ff_ref):
        hv = h_ref[...]
        gate = _dot(hv, wg_ref[...])
        up = _dot(hv, wu_ref[...])
        gate_ref[...] = gate
        up_ref[...] = up
        ff_ref[...] = (gate * _sigmoid(gate) * up).astype(ff_ref.dtype)

    wspec = pl.BlockSpec((k, tn), lambda j, i: (0, j))
    ospec = pl.BlockSpec((tm, tn), lambda j, i: (i, j))
    return pl.pallas_call(
        body, name=name, grid=(n // tn, m // tm),
        in_specs=[pl.BlockSpec((tm, k), lambda j, i: (i, 0)), wspec, wspec],
        out_specs=[ospec, ospec, ospec],
        out_shape=[jax.ShapeDtypeStruct((m, n), F32), jax.ShapeDtypeStruct((m, n), F32),
                   jax.ShapeDtypeStruct((m, n), MXU_DTYPE)],
        compiler_params=_params(2),
    )(h, wg, wu)


def _ffn_bwd_act(dx, wd, gate, up, name, tm=512, tn=256):
    m, k = dx.shape
    n = wd.shape[0]
    tm = min(tm, m)

    def body(dx_ref, wd_ref, gate_ref, up_ref, dgate_ref, dup_ref):
        dff = _dot_nt(dx_ref[...], wd_ref[...])
        gate = gate_ref[...]
        sg = _sigmoid(gate)
        dgate_ref[...] = (dff * up_ref[...] * sg * (1.0 + gate * (1.0 - sg))).astype(dgate_ref.dtype)
        dup_ref[...] = (dff * gate * sg).astype(dup_ref.dtype)

    ospec = pl.BlockSpec((tm, tn), lambda j, i: (i, j))
    return pl.pallas_call(
        body, name=name, grid=(n // tn, m // tm),
        in_specs=[pl.BlockSpec((tm, k), lambda j, i: (i, 0)), pl.BlockSpec((tn, k), lambda j, i: (j, 0)),
                  ospec, ospec],
        out_specs=[ospec, ospec],
        out_shape=[jax.ShapeDtypeStruct((m, n), MXU_DTYPE), jax.ShapeDtypeStruct((m, n), MXU_DTYPE)],
        compiler_params=_params(2),
    )(dx, wd, gate, up)


def _chunk_masks():
    r = lax.broadcasted_iota(jnp.int32, (SUPER, SUPER), 0)
    c = lax.broadcasted_iota(jnp.int32, (SUPER, SUPER), 1)
    same = jnp.right_shift(r, 6) == jnp.right_shift(c, 6)
    lower = jnp.logical_and(same, c <= r)
    upper = jnp.logical_and(same, c >= r)
    return same, lower, upper


def _head_block_mask():
    r = lax.broadcasted_iota(jnp.int32, (LANES, LANES), 0)
    c = lax.broadcasted_iota(jnp.int32, (LANES, LANES), 1)
    return jnp.where(jnp.right_shift(r, 6) == jnp.right_shift(c, 6), 1.0, 0.0)


def _lower_bound(lb_raw):
    return 1.0 / (1.0 + jnp.exp(lb_raw[1:2, :] - lb_raw[0:1, :]))


def _hgrn_gates(q, hf, lb, same_b, lower_b):
    sig = _sigmoid(hf)
    f = lb + (1.0 - lb) * sig
    k = 1.0 - f
    lf = jnp.log(f)
    b = _rsum_left(lower_b, lf, 3)
    be = _rsum_left(same_b, lf, 3)
    eb = jnp.exp(b)
    enb = jnp.exp(-b)
    edb = jnp.exp(be - b)
    return sig, f, k, b, be, eb, enb, edb, q * eb, k * enb, k * edb


def _hgrn_fwd(proj, lower_bounds, norm_g, n_seq, seq):
    t = n_seq * seq
    n_super = seq // SUPER
    n_chunks = seq // CHUNK

    def body(q_ref, f_ref, i_ref, g_ref, lb_ref, ng_ref, out_ref, opre_ref, st_ref):
        masks = _head_masks()
        same, lower, _ = _chunk_masks()
        same_b, lower_b = _ones_where(same), _ones_where(lower)
        bd = _head_block_mask()
        lb = _lower_bound(lb_ref[...])
        ng = ng_ref[...]

        def step(sb, st):
            rows = pl.ds(pl.multiple_of(sb * SUPER, SUPER), SUPER)
            q, hf, v, hg = q_ref[rows, :], f_ref[rows, :], i_ref[rows, :], g_ref[rows, :]
            _, _, _, _, be, _, _, _, qe, ke, kd = _hgrn_gates(q, hf, lb, same_b, lower_b)
            o = None
            for m in masks:
                p = jnp.where(lower, _dot_nt(qe * m, ke), 0.0)
                d = _dot(p, v) * m
                o = d if o is None else o + d
            inter = []
            for c in range(SUPER // CHUNK):
                cr = slice(c * CHUNK, (c + 1) * CHUNK)
                st_ref[0, 0, sb * (SUPER // CHUNK) + c] = st
                inter.append(_dot_nt(qe[cr], st))
                dec = jnp.exp(be[c * CHUNK:c * CHUNK + 1, :])
                st = st * dec + bd * _dot_tn(v[cr], kd[cr])
            o = o + jnp.concatenate(inter, axis=0)
            opre_ref[rows, :] = o
            on = o * _head_rstd(o, masks) * ng
            out_ref[rows, :] = (on * hg * _sigmoid(hg)).astype(out_ref.dtype)
            return st

        lax.fori_loop(0, n_super, step, jnp.zeros((LANES, LANES), F32))

    def col(k):
        return pl.BlockSpec((seq, LANES), lambda p, b: (b, k * N_PAIRS + p))

    vec = lambda rows: pl.BlockSpec((rows, LANES), lambda p, b: (0, p))
    ospec = pl.BlockSpec((seq, LANES), lambda p, b: (b, p))
    return pl.pallas_call(
        body, name="hgrn_fwd", grid=(N_PAIRS, n_seq),
        in_specs=[col(0), col(1), col(2), col(3), vec(2), vec(1)],
        out_specs=[ospec, ospec,
                   pl.BlockSpec((1, 1, n_chunks, LANES, LANES), lambda p, b: (b, p, 0, 0, 0))],
        out_shape=[jax.ShapeDtypeStruct((t, GROUP), MXU_DTYPE), jax.ShapeDtypeStruct((t, GROUP), F32),
                   jax.ShapeDtypeStruct((n_seq, N_PAIRS, n_chunks, LANES, LANES), F32)],
        compiler_params=_params(2),
    )(proj, proj, proj, proj, lower_bounds, norm_g)


def _hgrn_bwd(proj, lower_bounds, norm_g, dmix, opre, states, n_seq, seq):
    t = n_seq * seq
    n_super = seq // SUPER
    n_chunks = seq // CHUNK
    per = SUPER // CHUNK

    def body(q_ref, f_ref, i_ref, g_ref, lb_ref, ng_ref, dm_ref, opre_ref, st_ref,
             dq_ref, df_ref, di_ref, dg_ref, dlb_ref, dng_ref):
        masks = _head_masks()
        same, lower, upper = _chunk_masks()
        same_b, lower_b, upper_b = _ones_where(same), _ones_where(lower), _ones_where(upper)
        bd = _head_block_mask()
        lb_raw = lb_ref[...]
        lb = _lower_bound(lb_raw)
        ng = ng_ref[...]
        row_id = lax.broadcasted_iota(jnp.int32, (SUPER, LANES), 0)

        @pl.when(pl.program_id(1) == 0)
        def _():
            dlb_ref[...] = jnp.zeros_like(dlb_ref)
            dng_ref[...] = jnp.zeros_like(dng_ref)

        def step(it, dst):
            sb = n_super - 1 - it
            rows = pl.ds(pl.multiple_of(sb * SUPER, SUPER), SUPER)
            q, hf, v, hg = q_ref[rows, :], f_ref[rows, :], i_ref[rows, :], g_ref[rows, :]
            sig, f, k, _, be, eb, enb, edb, qe, ke, kd = _hgrn_gates(q, hf, lb, same_b, lower_b)
            o = opre_ref[rows, :]
            r = _head_rstd(o, masks)
            oh = o * r
            dm = dm_ref[rows, :]
            sg = _sigmoid(hg)
            dg_ref[rows, :] = (dm * oh * ng * sg * (1.0 + hg * (1.0 - sg))).astype(dg_ref.dtype)
            don = dm * hg * sg
            dng_ref[...] += jnp.sum(don * oh, axis=0, keepdims=True)
            doh = don * ng
            do = r * (doh - oh * _head_mean(doh * oh, masks))
            dqe, dke, dv = None, None, None
            for m in masks:
                dom = do * m
                qem = qe * m
                p = jnp.where(lower, _dot_nt(qem, ke), 0.0)
                dp = jnp.where(lower, _dot_nt(dom, v), 0.0)
                a = _dot(dp, ke) * m
                bq = _dot_tn(dp, qem)
                cv = _dot_tn(p, dom)
                dqe = a if dqe is None else dqe + a
                dke = bq if dke is None else dke + bq
                dv = cv if dv is None else dv + cv
            dqe_i, dv_i, dkd_i, dbe_rows = [None] * per, [None] * per, [None] * per, None
            for c in reversed(range(per)):
                cr = slice(c * CHUNK, (c + 1) * CHUNK)
                st_prev = st_ref[0, 0, sb * per + c]
                dec = jnp.exp(be[c * CHUNK:c * CHUNK + 1, :])
                du = bd * dst
                dv_i[c] = _dot_nt(kd[cr], du)
                dkd_i[c] = _dot(v[cr], du)
                ddec = jnp.sum(dst * st_prev, axis=0, keepdims=True)
                dqe_i[c] = _dot(do[cr], st_prev)
                dst = bd * (dst * dec + _dot_tn(do[cr], qe[cr]))
                put = jnp.where(row_id == c * CHUNK, ddec * dec, 0.0)
                dbe_rows = put if dbe_rows is None else dbe_rows + put
            dqe = dqe + jnp.concatenate(dqe_i, axis=0)
            dv = dv + jnp.concatenate(dv_i, axis=0)
            dkd = jnp.concatenate(dkd_i, axis=0)
            dk = dke * enb + dkd * edb
            db = dqe * qe - dke * ke - dkd * kd
            dbe = dkd * kd + dbe_rows
            dlf = _rsum_left(upper_b, db, 3) + _rsum_left(same_b, dbe, 3)
            dfv = dlf / f - dk
            dq_ref[rows, :] = (dqe * eb).astype(dq_ref.dtype)
            di_ref[rows, :] = dv.astype(di_ref.dtype)
            df_ref[rows, :] = (dfv * (1.0 - lb) * sig * (1.0 - sig)).astype(df_ref.dtype)
            dlb = jnp.sum(dfv * (1.0 - sig), axis=0, keepdims=True)
            da0 = dlb * lb * (1.0 - lb)
            dlb_ref[0:1, :] += da0
            dlb_ref[1:2, :] -= da0
            return dst

        lax.fori_loop(0, n_super, step, jnp.zeros((LANES, LANES), F32))

    def col(k):
        return pl.BlockSpec((seq, LANES), lambda p, b: (b, k * N_PAIRS + p))

    vec = lambda rows: pl.BlockSpec((rows, LANES), lambda p, b: (0, p))
    ospec = pl.BlockSpec((seq, LANES), lambda p, b: (b, p))
    piece = jax.ShapeDtypeStruct((t, GROUP), MXU_DTYPE)
    return pl.pallas_call(
        body, name="hgrn_bwd", grid=(N_PAIRS, n_seq),
        in_specs=[col(0), col(1), col(2), col(3), vec(2), vec(1), ospec, ospec,
                  pl.BlockSpec((1, 1, n_chunks, LANES, LANES), lambda p, b: (b, p, 0, 0, 0))],
        out_specs=[ospec, ospec, ospec, ospec, vec(2), vec(1)],
        out_shape=[piece, piece, piece, piece,
                   jax.ShapeDtypeStruct((2, GROUP), F32), jax.ShapeDtypeStruct((1, GROUP), F32)],
        compiler_params=_params(2),
    )(proj, proj, proj, proj, lower_bounds, norm_g, dmix, opre, states)


SB_SCALE = 1.0 / math.sqrt(D_HEAD)


def _tile_masks():
    r = lax.broadcasted_iota(jnp.int32, (SB_BLOCK, SB_BLOCK), 0)
    c = lax.broadcasted_iota(jnp.int32, (SB_BLOCK, SB_BLOCK), 1)
    return r, c


def _sb_fwd(proj, norm_g, n_seq, seq):
    t = n_seq * seq
    nq = seq // SB_BLOCK
    q0, k0, v0 = 4 * N_PAIRS, 5 * N_PAIRS, 6 * N_PAIRS

    def body(q_ref, k_ref, v_ref, ng_ref, out_ref, opre_ref, ctot_ref):
        i = pl.program_id(2)
        masks = _head_masks()
        r, c = _tile_masks()
        strict = c < r
        neg_suffix = jnp.where(r >= c, -1.0, 0.0).astype(BF16)
        qs = q_ref[...] * SB_SCALE
        qhs = [(qs * m).astype(MXU_DTYPE) for m in masks]

        def tiles(js, carry, diag):
            rows = [pl.ds(pl.multiple_of(j * SB_BLOCK, SB_BLOCK), SB_BLOCK) for j in js]
            ks = [k_ref[rw, :].astype(MXU_DTYPE) for rw in rows]
            vs = [v_ref[rw, :].astype(MXU_DTYPE) for rw in rows]
            zs = [[_dot_nt(qh, kj) for qh in qhs] for kj in ks]
            ccs = [[_softplus(z) for z in zt] for zt in zs]
            if diag:
                ccs = [[jnp.where(strict, cc, 0.0) for cc in ct] for ct in ccs]
            logits = [[_dot_nt(qh, kj) + _rsum_right(cc, neg_suffix, 2) for qh, cc in zip(qhs, ct)]
                      for kj, ct in zip(ks, ccs)]
            out = []
            for h, (run, acc) in enumerate(carry):
                for t in range(len(js)):
                    a = jnp.exp(logits[t][h] - run)
                    if diag:
                        a = jnp.where(strict, a, 0.0)
                    acc = acc + _dot(a, vs[t])
                    run = run + jnp.sum(ccs[t][h], axis=1, keepdims=True)
                out.append((run, acc))
            return tuple(out)

        start = ((jnp.zeros((SB_BLOCK, 1), F32), jnp.zeros((SB_BLOCK, LANES), F32)),) * 2
        carry = tiles([i], start, True)
        carry = lax.fori_loop(0, i // 2, lambda s, cy: tiles([i - 1 - 2 * s, i - 2 - 2 * s], cy, False), carry)
        carry = lax.fori_loop(0, i % 2, lambda s, cy: tiles([0], cy, False), carry)
        opre = carry[0][1] * masks[0] + carry[1][1] * masks[1]
        ctot = carry[0][0] * masks[0] + carry[1][0] * masks[1]
        opre_ref[...] = opre
        ctot_ref[...] = ctot
        out_ref[...] = (opre * _head_rstd(opre, masks) * ng_ref[...]).astype(out_ref.dtype)

    qspec = pl.BlockSpec((SB_BLOCK, LANES), lambda p, b, i: (b * nq + i, q0 + p))
    ospec = pl.BlockSpec((SB_BLOCK, LANES), lambda p, b, i: (b * nq + i, p))
    return pl.pallas_call(
        body, name="sb_fwd", grid=(N_PAIRS, n_seq, nq),
        in_specs=[qspec,
                  pl.BlockSpec((seq, LANES), lambda p, b, i: (b, k0 + p)),
                  pl.BlockSpec((seq, LANES), lambda p, b, i: (b, v0 + p)),
                  pl.BlockSpec((1, LANES), lambda p, b, i: (0, p))],
        out_specs=[ospec, ospec, ospec],
        out_shape=[jax.ShapeDtypeStruct((t, GROUP), MXU_DTYPE), jax.ShapeDtypeStruct((t, GROUP), F32),
                   jax.ShapeDtypeStruct((t, GROUP), F32)],
        compiler_params=_params(3),
    )(proj, proj, proj, norm_g)


def _sb_bwd(proj, norm_g, dmix, opre, ctot, n_seq, seq):
    t = n_seq * seq
    nq = seq // SB_BLOCK
    q0, k0, v0 = 4 * N_PAIRS, 5 * N_PAIRS, 6 * N_PAIRS

    def body(q_ref, k_ref, v_ref, ng_ref, dm_ref, opre_ref, ctot_ref,
             dq_ref, dk_ref, dv_ref, dng_ref, dk_acc, dv_acc):
        p_id, b_id, i = pl.program_id(0), pl.program_id(1), pl.program_id(2)
        masks = _head_masks()
        r, c = _tile_masks()
        strict = c < r
        before = _ones_where(r < c)
        upto = _ones_where(r <= c)

        @pl.when(i == 0)
        def _():
            dk_acc[...] = jnp.zeros_like(dk_acc)
            dv_acc[...] = jnp.zeros_like(dv_acc)

        @pl.when(jnp.logical_and(b_id == 0, i == 0))
        def _():
            dng_ref[...] = jnp.zeros_like(dng_ref)

        o = opre_ref[...]
        rs = _head_rstd(o, masks)
        oh = o * rs
        dm = dm_ref[...]
        dng_ref[...] += jnp.sum(dm * oh, axis=0, keepdims=True)
        doh = dm * ng_ref[...]
        do = rs * (doh - oh * _head_mean(doh * oh, masks))

        qs = q_ref[...] * SB_SCALE
        ct = ctot_ref[...]
        qhs = [(qs * m).astype(MXU_DTYPE) for m in masks]
        doms = [(do * m).astype(MXU_DTYPE) for m in masks]
        totals = [jnp.max(ct * m, axis=1, keepdims=True) for m in masks]
        heads = range(len(masks))

        def tiles(js, carry, diag):
            nt = range(len(js))
            rows = [pl.ds(pl.multiple_of(j * SB_BLOCK, SB_BLOCK), SB_BLOCK) for j in js]
            ks = [k_ref[rw, :].astype(MXU_DTYPE) for rw in rows]
            vs = [v_ref[rw, :].astype(MXU_DTYPE) for rw in rows]
            zs = [[_dot_nt(qhs[h], ks[t]) for h in heads] for t in nt]
            das = [[_dot_nt(doms[h], vs[t]) for h in heads] for t in nt]
            sps = [[_softplus(z) for z in zt] for zt in zs]
            sigs = [[jnp.exp(z - sp) for z, sp in zip(zt, st)] for zt, st in zip(zs, sps)]
            ccs = [[jnp.where(strict, sp, 0.0) for sp in st] for st in sps] if diag else sps
            logits = [[zs[t][h] + _rsum_right(ccs[t][h], before, 2) for h in heads] for t in nt]
            out = []
            for h in heads:
                pc, pdl, dq_h = carry[h]
                for t in nt:
                    a = jnp.exp(logits[t][h] - (totals[h] - pc))
                    if diag:
                        a = jnp.where(strict, a, 0.0)
                    dl = a * das[t][h]
                    dv_acc[rows[t], :] += _dot_tn(a, doms[h])
                    dz = dl - sigs[t][h] * (pdl + _rsum_right(dl, upto, 2))
                    if diag:
                        dz = jnp.where(strict, dz, 0.0)
                    dzb = dz.astype(MXU_DTYPE)
                    dq_h = dq_h + _dot(dzb, ks[t])
                    dk_acc[rows[t], :] += _dot_tn(dzb, qhs[h])
                    pc = pc + jnp.sum(ccs[t][h], axis=1, keepdims=True)
                    pdl = pdl + jnp.sum(dl, axis=1, keepdims=True)
                out.append((pc, pdl, dq_h))
            return tuple(out)

        zero = jnp.zeros((SB_BLOCK, 1), F32)
        carry = ((zero, zero, jnp.zeros((SB_BLOCK, LANES), F32)),) * 2
        carry = lax.fori_loop(0, i // 2, lambda s, cy: tiles([2 * s, 2 * s + 1], cy, False), carry)
        carry = lax.fori_loop(0, i % 2, lambda s, cy: tiles([i - 1], cy, False), carry)
        carry = tiles([i], carry, True)
        dq = carry[0][2] * masks[0] + carry[1][2] * masks[1]
        dq_ref[...] = (dq * SB_SCALE).astype(dq_ref.dtype)

        @pl.when(i == nq - 1)
        def _():
            dk_ref[...] = dk_acc[...].astype(dk_ref.dtype)
            dv_ref[...] = dv_acc[...].astype(dv_ref.dtype)

    qspec = pl.BlockSpec((SB_BLOCK, LANES), lambda p, b, i: (b * nq + i, q0 + p))
    ospec = pl.BlockSpec((SB_BLOCK, LANES), lambda p, b, i: (b * nq + i, p))
    dmspec = pl.BlockSpec((SB_BLOCK, LANES), lambda p, b, i: (b * nq + i, N_PAIRS + p))
    full = lambda k: pl.BlockSpec((seq, LANES), lambda p, b, i: (b, k + p))
    vec = pl.BlockSpec((1, LANES), lambda p, b, i: (0, p))
    piece = jax.ShapeDtypeStruct((t, GROUP), MXU_DTYPE)
    return pl.pallas_call(
        body, name="sb_bwd", grid=(N_PAIRS, n_seq, nq),
        in_specs=[qspec, full(k0), full(v0), vec, dmspec, ospec, ospec],
        out_specs=[ospec, full(0), full(0), vec],
        out_shape=[piece, piece, piece, jax.ShapeDtypeStruct((1, GROUP), F32)],
        scratch_shapes=[pltpu.VMEM((seq, LANES), F32), pltpu.VMEM((seq, LANES), F32)],
        compiler_params=_params(3),
    )(proj, proj, proj, norm_g, dmix, opre, ctot)


def _mesh_place():
    x, y, c = lax.axis_index("x"), lax.axis_index("y"), lax.axis_index("c")
    return x, y, c


def _peer(x, y, c, k):
    px = lax.rem(x + ((k >> 2) & 1), 2)
    py = lax.rem(y + ((k >> 1) & 1), 2)
    pc = lax.rem(c + (k & 1), 2)
    return (px, py, pc), 4 * px + 2 * py + pc


def _gather_weights(wpack):
    rows, width = wpack.shape

    def body(w_ref, out_ref, stage, send_sems, recv_sems, local_sem):
        x, y, c = _mesh_place()
        me = 4 * x + 2 * y + c
        stage[...] = w_ref[...].astype(stage.dtype)
        mine = pltpu.make_async_copy(stage, out_ref.at[me], local_sem)
        mine.start()
        sends = []
        for k in range(1, N_DEV):
            peer, _ = _peer(x, y, c, k)
            cp = pltpu.make_async_remote_copy(
                src_ref=stage, dst_ref=out_ref.at[me], send_sem=send_sems.at[k - 1], recv_sem=recv_sems.at[k - 1],
                device_id=peer, device_id_type=pl.DeviceIdType.MESH)
            cp.start()
            sends.append(cp)
        for k in range(1, N_DEV):
            peer, pidx = _peer(x, y, c, k)
            pltpu.make_async_remote_copy(
                src_ref=stage, dst_ref=out_ref.at[pidx], send_sem=send_sems.at[k - 1], recv_sem=recv_sems.at[k - 1],
                device_id=peer, device_id_type=pl.DeviceIdType.MESH).wait_recv()
        for cp in sends:
            cp.wait_send()
        mine.wait()

    return pl.pallas_call(
        body, name="gather_weights",
        in_specs=[pl.BlockSpec(memory_space=pltpu.VMEM)],
        out_specs=pl.BlockSpec(memory_space=pltpu.HBM),
        out_shape=jax.ShapeDtypeStruct((N_DEV, rows, width), BF16),
        scratch_shapes=[pltpu.VMEM((rows, width), BF16), pltpu.SemaphoreType.DMA((N_DEV - 1,)),
                        pltpu.SemaphoreType.DMA((N_DEV - 1,)), pltpu.SemaphoreType.DMA],
        compiler_params=pltpu.CompilerParams(vmem_limit_bytes=VMEM_LIMIT),
    )(wpack)


def _scatter_grads(gpack, small):
    _, rows, width = gpack.shape

    def body(g_ref, s_ref, land_ref, sland_ref, send_sems, recv_sems, ssend_sems, srecv_sems, local_sems):
        x, y, c = _mesh_place()
        me = 4 * x + 2 * y + c
        mine = pltpu.make_async_copy(g_ref.at[me], land_ref.at[me], local_sems.at[0])
        smine = pltpu.make_async_copy(s_ref, sland_ref.at[me], local_sems.at[1])
        mine.start()
        smine.start()
        sends = []
        for k in range(1, N_DEV):
            peer, pidx = _peer(x, y, c, k)
            cp = pltpu.make_async_remote_copy(
                src_ref=g_ref.at[pidx], dst_ref=land_ref.at[me], send_sem=send_sems.at[k - 1],
                recv_sem=recv_sems.at[k - 1], device_id=peer, device_id_type=pl.DeviceIdType.MESH)
            scp = pltpu.make_async_remote_copy(
                src_ref=s_ref, dst_ref=sland_ref.at[me], send_sem=ssend_sems.at[k - 1],
                recv_sem=srecv_sems.at[k - 1], device_id=peer, device_id_type=pl.DeviceIdType.MESH)
            cp.start()
            scp.start()
            sends += [cp, scp]
        for k in range(1, N_DEV):
            peer, pidx = _peer(x, y, c, k)
            pltpu.make_async_remote_copy(
                src_ref=g_ref.at[pidx], dst_ref=land_ref.at[pidx], send_sem=send_sems.at[k - 1],
                recv_sem=recv_sems.at[k - 1], device_id=peer, device_id_type=pl.DeviceIdType.MESH).wait_recv()
            pltpu.make_async_remote_copy(
                src_ref=s_ref, dst_ref=sland_ref.at[pidx], send_sem=ssend_sems.at[k - 1],
                recv_sem=srecv_sems.at[k - 1], device_id=peer, device_id_type=pl.DeviceIdType.MESH).wait_recv()
        for cp in sends:
            cp.wait_send()
        mine.wait()
        smine.wait()

    hbm = pl.BlockSpec(memory_space=pltpu.HBM)
    sems = pltpu.SemaphoreType.DMA((N_DEV - 1,))
    return pl.pallas_call(
        body, name="scatter_grads",
        in_specs=[hbm, hbm], out_specs=[hbm, hbm],
        out_shape=[jax.ShapeDtypeStruct((N_DEV, rows, width), gpack.dtype),
                   jax.ShapeDtypeStruct((N_DEV,) + small.shape, small.dtype)],
        scratch_shapes=[sems, sems, sems, sems, pltpu.SemaphoreType.DMA((2,))],
    )(gpack, small)


def _adam(w, g, m, v):
    m = ADAM_B1 * m + (1.0 - ADAM_B1) * g
    v = ADAM_B2 * v + (1.0 - ADAM_B2) * (g * g)
    m_hat = m / (1.0 - ADAM_B1 ** ADAM_STEP)
    v_hat = v / (1.0 - ADAM_B2 ** ADAM_STEP)
    delta = -ADAM_LR * (m_hat / (jnp.sqrt(v_hat) + ADAM_EPS) + ADAM_WD * w)
    return delta, m, v


def _reduce_adamw(land, w, m, v, name, tr):
    _, rows, width = land.shape

    def body(land_ref, w_ref, m_ref, v_ref, g_ref, d_ref, nm_ref, nv_ref):
        g = land_ref[0].astype(F32)
        for d in range(1, N_DEV):
            g = g + land_ref[d].astype(F32)
        delta, nm, nv = _adam(w_ref[...], g, m_ref[...], v_ref[...])
        g_ref[...] = g
        d_ref[...] = delta
        nm_ref[...] = nm
        nv_ref[...] = nv

    row = pl.BlockSpec((tr, width), lambda i: (i, 0))
    out = jax.ShapeDtypeStruct((rows, width), F32)
    return pl.pallas_call(
        body, name=name, grid=(rows // tr,),
        in_specs=[pl.BlockSpec((N_DEV, tr, width), lambda i: (0, i, 0)), row, row, row],
        out_specs=[row, row, row, row], out_shape=[out, out, out, out],
        compiler_params=_params(1),
    )(land, w, m, v)


def _pack_shards(w_in, w_out, w_gate, w_up, w_down):
    return jnp.concatenate([w_in.reshape(ROWS_IN, D_MODEL), w_out.reshape(ROWS_OUT, D_MODEL),
                            w_gate.reshape(ROWS_FF, D_MODEL), w_up.reshape(ROWS_FF, D_MODEL),
                            w_down.reshape(ROWS_FF, D_MODEL)], axis=0)


def _unpack_shards(p):
    return (p[:OFF_OUT].reshape(1, D_MODEL, IN_COLS // N_DEV), p[OFF_OUT:OFF_GATE].reshape(1, ROWS_OUT, D_MODEL),
            p[OFF_GATE:OFF_UP].reshape(1, D_MODEL, ROWS_FF), p[OFF_UP:OFF_DOWN].reshape(1, D_MODEL, ROWS_FF),
            p[OFF_DOWN:].reshape(1, ROWS_FF, D_MODEL))


def _cols_from_slots(g, cols):
    return g.reshape(N_DEV, D_MODEL, cols).transpose(1, 0, 2).reshape(D_MODEL, N_DEV * cols)


def _slots_from_cols(w, cols):
    return w.reshape(D_MODEL, N_DEV, cols).transpose(1, 0, 2).reshape(N_DEV, cols, D_MODEL)


def _unpack_full(g):
    return (_cols_from_slots(g[:, :OFF_OUT], IN_COLS // N_DEV), g[:, OFF_OUT:OFF_GATE].reshape(D_MODEL, D_MODEL),
            _cols_from_slots(g[:, OFF_GATE:OFF_UP], ROWS_FF), _cols_from_slots(g[:, OFF_UP:OFF_DOWN], ROWS_FF),
            g[:, OFF_DOWN:].reshape(D_FF, D_MODEL))


def _pack_full(dw_in, dw_out, dw_gate, dw_up, dw_down):
    return jnp.concatenate([_slots_from_cols(dw_in, IN_COLS // N_DEV), dw_out.reshape(N_DEV, ROWS_OUT, D_MODEL),
                            _slots_from_cols(dw_gate, ROWS_FF), _slots_from_cols(dw_up, ROWS_FF),
                            dw_down.reshape(N_DEV, ROWS_FF, D_MODEL)], axis=1)


def _pack_small(mix_g, lb, hgrn_g, sb_g, ffn_g, final_g, extra):
    rows = [mix_g.reshape(1, D_MODEL), lb.reshape(1, D_MODEL),
            jnp.concatenate([hgrn_g.reshape(1, GROUP), sb_g.reshape(1, GROUP)], axis=1),
            ffn_g.reshape(1, D_MODEL), final_g.reshape(1, D_MODEL), extra,
            jnp.zeros((SMALL_ROWS - 6, D_MODEL), F32)]
    return jnp.concatenate(rows, axis=0)


def _unpack_small(p):
    return (p[0:1], p[1].reshape(2, GROUP), p[2:3, :GROUP], p[2:3, GROUP:], p[3:4], p[4])


def kernel(x, mix_norm_g, w_in, lower_bounds, hgrn_norm_g, sb_norm_g, w_out, ffn_norm_g, w_gate, w_up, w_down, final_norm_g, loss_target, m_mix_norm_g, m_w_in, m_lower_bounds, m_hgrn_norm_g, m_sb_norm_g, m_w_out, m_ffn_norm_g, m_w_gate, m_w_up, m_w_down, m_final_norm_g, v_mix_norm_g, v_w_in, v_lower_bounds, v_hgrn_norm_g, v_sb_norm_g, v_w_out, v_ffn_norm_g, v_w_gate, v_w_up, v_w_down, v_final_norm_g):
    n_seq, seq, d = x.shape
    t = n_seq * seq
    x2d = x.reshape(t, d)
    tgt = loss_target.reshape(t, d)
    final_g = final_norm_g.reshape(1, d)

    wpack = _pack_shards(w_in, w_out, w_gate, w_up, w_down)
    wf_in, wf_out, wf_gate, wf_up, wf_down = _unpack_full(_gather_weights(wpack))

    h1 = _rmsnorm_fwd(x2d, mix_norm_g, "norm_mix")
    proj = _mm_nn(h1, wf_in, "proj")
    o_a, oa_pre, states = _hgrn_fwd(proj, lower_bounds, hgrn_norm_g, n_seq, seq)
    o_b, ob_pre, ctot = _sb_fwd(proj, sb_norm_g, n_seq, seq)
    mixed = jnp.concatenate([o_a, o_b], axis=1)
    x1 = _mm_nn(mixed, wf_out, "mix_out", res=x2d)
    h2 = _rmsnorm_fwd(x1, ffn_norm_g, "norm_ffn")
    gate, up, ff = _ffn_up(h2, wf_gate, wf_up, "ffn_up")
    x2 = _mm_nn(ff, wf_down, "ffn_down", res=x1)
    dx2, d_final_g, loss_part = _final_loss(x2, tgt, final_g, "loss_head")

    dgate, dup = _ffn_bwd_act(dx2, wf_down, gate, up, "ffn_bwd_act")
    dw_down = _mm_tn(ff, dx2, "dw_down", tk=1408, tn=1024)
    dh2 = _mm_nt([(dgate, wf_gate), (dup, wf_up)], "dh_ffn", tm=256)
    dw_gate = _mm_tn(h2, dgate, "dw_gate", tk=512, tn=1408)
    dw_up = _mm_tn(h2, dup, "dw_up", tk=512, tn=1408)
    dx1, d_ffn_g = _rmsnorm_bwd(x1, ffn_norm_g, dh2, dx2, "norm_ffn_bwd")
    dmix = _mm_nt([(dx1, wf_out)], "dmix")
    dw_out = _mm_tn(mixed, dx1, "dw_out", tk=512, tn=1024)
    dsq, dsk, dsv, d_sb_g = _sb_bwd(proj, sb_norm_g, dmix, ob_pre, ctot, n_seq, seq)
    dhq, dhf, dhi, dhg, d_lb, d_hgrn_g = _hgrn_bwd(proj, lower_bounds, hgrn_norm_g, dmix, oa_pre, states, n_seq, seq)
    dproj = jnp.concatenate([dhq, dhf, dhi, dhg, dsq, dsk, dsv], axis=1)
    dh1 = _mm_nt([(dproj, wf_in)], "dh_mix", tm=256)
    dw_in = _mm_tn(h1, dproj, "dw_in", tk=512, tn=1792)
    dx, d_mix_g = _rmsnorm_bwd(x2d, mix_norm_g, dh1, dx1, "norm_mix_bwd")

    gpack = _pack_full(dw_in, dw_out, dw_gate, dw_up, dw_down)
    extra = jnp.pad(loss_part, ((0, 0), (0, d - LANES)))
    small = _pack_small(d_mix_g, d_lb, d_hgrn_g, d_sb_g, d_ffn_g, d_final_g, extra)
    land, small_land = _scatter_grads(gpack, small)
    mpack = _pack_shards(m_w_in, m_w_out, m_w_gate, m_w_up, m_w_down)
    vpack = _pack_shards(v_w_in, v_w_out, v_w_gate, v_w_up, v_w_down)
    g_big, d_big, m_big, v_big = _reduce_adamw(land, wpack, mpack, vpack, "adamw_big", tr=272)
    zero_row = jnp.zeros((1, d), F32)
    w_small = _pack_small(mix_norm_g, lower_bounds, hgrn_norm_g, sb_norm_g, ffn_norm_g, final_norm_g, zero_row)
    m_small = _pack_small(m_mix_norm_g, m_lower_bounds, m_hgrn_norm_g, m_sb_norm_g, m_ffn_norm_g, m_final_norm_g, zero_row)
    v_small = _pack_small(v_mix_norm_g, v_lower_bounds, v_hgrn_norm_g, v_sb_norm_g, v_ffn_norm_g, v_final_norm_g, zero_row)
    g_sm, d_sm, m_sm, v_sm = _reduce_adamw(small_land, w_small, m_small, v_small, "adamw_small", tr=SMALL_ROWS)

    loss = g_sm[5, 0]
    big = [_unpack_shards(p) for p in (g_big, d_big, m_big, v_big)]
    sm = [_unpack_small(p) for p in (g_sm, d_sm, m_sm, v_sm)]
    outs = [loss, dx.reshape(n_seq, seq, d)]
    for bg, s in zip(big, sm):
        b_in, b_out, b_gate, b_up, b_down = bg
        s_mix, s_lb, s_hgrn, s_sb, s_ffn, s_final = s
        outs += [s_mix, b_in, s_lb, s_hgrn, s_sb, b_out, s_ffn, b_gate, b_up, b_down, s_final]
    return tuple(outs)
```

```python
import functools
import math

import jax
import jax.numpy as jnp
from jax import lax
from jax.experimental import pallas as pl
from jax.experimental.pallas import tpu as pltpu

F32 = jnp.float32
BF16 = jnp.bfloat16
MXU_DTYPE = BF16

EPS = 1e-6
D_MODEL = 1024
N_HEADS = 8
D_HEAD = 64
GROUP = N_HEADS * D_HEAD
IN_COLS = 7 * GROUP
D_FF = 2816
CHUNK = 64
LANES = 128
N_PAIRS = GROUP // LANES
SUPER = 256
SB_BLOCK = 256
N_DEV = 8

ADAM_LR = 0.001
ADAM_B1 = 0.9
ADAM_B2 = 0.999
ADAM_EPS = 1e-08
ADAM_WD = 0.01
ADAM_STEP = 10

ROWS_IN = D_MODEL * (IN_COLS // N_DEV) // D_MODEL
ROWS_OUT = D_MODEL // N_DEV
ROWS_FF = D_FF // N_DEV
OFF_OUT = ROWS_IN
OFF_GATE = OFF_OUT + ROWS_OUT
OFF_UP = OFF_GATE + ROWS_FF
OFF_DOWN = OFF_UP + ROWS_FF
PACK_ROWS = OFF_DOWN + ROWS_FF
SMALL_ROWS = 8

VMEM_LIMIT = 48 * 1024 * 1024


def _params(n_axes, vmem=VMEM_LIMIT):
    return pltpu.CompilerParams(dimension_semantics=("arbitrary",) * n_axes, vmem_limit_bytes=vmem)


def _dot(a, b):
    return jnp.dot(a.astype(MXU_DTYPE), b.astype(MXU_DTYPE), preferred_element_type=F32)


def _dot_nt(a, b):
    return lax.dot_general(a.astype(MXU_DTYPE), b.astype(MXU_DTYPE), (((1,), (1,)), ((), ())),
                           preferred_element_type=F32)


def _dot_tn(a, b):
    return lax.dot_general(a.astype(MXU_DTYPE), b.astype(MXU_DTYPE), (((0,), (0,)), ((), ())),
                           preferred_element_type=F32)


def _split(x, parts):
    out, r = [], x
    for _ in range(parts):
        h = r.astype(BF16)
        out.append(h)
        r = r - h.astype(F32)
    return out


def _rsum_right(x, u, parts):
    acc = None
    for h in _split(x, parts):
        d = jnp.dot(h, u, preferred_element_type=F32)
        acc = d if acc is None else acc + d
    return acc


def _rsum_left(u, x, parts):
    acc = None
    for h in _split(x, parts):
        d = jnp.dot(u, h, preferred_element_type=F32)
        acc = d if acc is None else acc + d
    return acc


def _ones_where(mask):
    return jnp.where(mask, 1.0, 0.0).astype(BF16)


def _sigmoid(x):
    return 1.0 / (1.0 + jnp.exp(-x))


def _softplus(x):
    return jnp.maximum(x, 0.0) + jnp.log(1.0 + jnp.exp(-jnp.abs(x)))


def _head_masks():
    lane = lax.broadcasted_iota(jnp.int32, (1, LANES), 1)
    return [jnp.where(lane < D_HEAD, 1.0, 0.0), jnp.where(lane >= D_HEAD, 1.0, 0.0)]


def _head_rstd(o, masks):
    sq = o * o
    r = None
    for m in masks:
        ms = jnp.sum(sq * m, axis=1, keepdims=True) * (1.0 / D_HEAD)
        t = lax.rsqrt(ms + EPS) * m
        r = t if r is None else r + t
    return r


def _head_mean(t, masks):
    out = None
    for m in masks:
        v = jnp.sum(t * m, axis=1, keepdims=True) * (1.0 / D_HEAD) * m
        out = v if out is None else out + v
    return out


def _rmsnorm_fwd(x, g, name):
    t, d = x.shape
    tm = min(512, t)

    def body(x_ref, g_ref, o_ref):
        xv = x_ref[...]
        r = lax.rsqrt(jnp.mean(xv * xv, axis=-1, keepdims=True) + EPS)
        o_ref[...] = (xv * r * g_ref[...]).astype(o_ref.dtype)

    return pl.pallas_call(
        body, name=name, grid=(t // tm,),
        in_specs=[pl.BlockSpec((tm, d), lambda i: (i, 0)), pl.BlockSpec((1, d), lambda i: (0, 0))],
        out_specs=pl.BlockSpec((tm, d), lambda i: (i, 0)),
        out_shape=jax.ShapeDtypeStruct((t, d), MXU_DTYPE),
        compiler_params=_params(1),
    )(x, g)


def _rmsnorm_bwd(x, g, dh, res, name):
    t, d = x.shape
    tm = min(512, t)

    def body(x_ref, g_ref, dh_ref, res_ref, dx_ref, dg_ref):
        xv = x_ref[...]
        r = lax.rsqrt(jnp.mean(xv * xv, axis=-1, keepdims=True) + EPS)
        xh = xv * r
        dhv = dh_ref[...]
        dxh = dhv * g_ref[...]
        dx_ref[...] = res_ref[...] + r * (dxh - xh * jnp.mean(dxh * xh, axis=-1, keepdims=True))

        @pl.when(pl.program_id(0) == 0)
        def _():
            dg_ref[...] = jnp.zeros_like(dg_ref)

        dg_ref[...] += jnp.sum(dhv * xh, axis=0, keepdims=True)

    row = pl.BlockSpec((tm, d), lambda i: (i, 0))
    vec = pl.BlockSpec((1, d), lambda i: (0, 0))
    return pl.pallas_call(
        body, name=name, grid=(t // tm,),
        in_specs=[row, vec, row, row], out_specs=[row, vec],
        out_shape=[jax.ShapeDtypeStruct((t, d), F32), jax.ShapeDtypeStruct((1, d), F32)],
        compiler_params=_params(1),
    )(x, g, dh, res)


def _final_loss(x, target, g, name):
    t, d = x.shape
    tm = min(512, t)

    def body(x_ref, t_ref, g_ref, dx_ref, dg_ref, loss_ref):
        xv = x_ref[...]
        gv = g_ref[...]
        r = lax.rsqrt(jnp.mean(xv * xv, axis=-1, keepdims=True) + EPS)
        xh = xv * r
        e = xh * gv - t_ref[...]
        dy = e * (1.0 / d)
        dxh = dy * gv
        dx_ref[...] = r * (dxh - xh * jnp.mean(dxh * xh, axis=-1, keepdims=True))

        @pl.when(pl.program_id(0) == 0)
        def _():
            dg_ref[...] = jnp.zeros_like(dg_ref)
            loss_ref[...] = jnp.zeros_like(loss_ref)

        dg_ref[...] += jnp.sum(dy * xh, axis=0, keepdims=True)
        part = 0.5 * jnp.sum(jnp.mean(e * e, axis=-1, keepdims=True), axis=0, keepdims=True)
        loss_ref[...] += jnp.broadcast_to(part, loss_ref.shape)

    row = pl.BlockSpec((tm, d), lambda i: (i, 0))
    vec = pl.BlockSpec((1, d), lambda i: (0, 0))
    return pl.pallas_call(
        body, name=name, grid=(t // tm,),
        in_specs=[row, row, vec],
        out_specs=[row, vec, pl.BlockSpec((1, LANES), lambda i: (0, 0))],
        out_shape=[jax.ShapeDtypeStruct((t, d), F32), jax.ShapeDtypeStruct((1, d), F32),
                   jax.ShapeDtypeStruct((1, LANES), F32)],
        compiler_params=_params(1),
    )(x, target, g)


def _mm_nn(a, w, name, res=None, out_dtype=F32, tm=512, tn=512):
    m, k = a.shape
    n = w.shape[1]
    tm, tn = min(tm, m), min(tn, n)

    def body(*refs):
        if res is None:
            a_ref, w_ref, o_ref = refs
            o_ref[...] = _dot(a_ref[...], w_ref[...]).astype(o_ref.dtype)
        else:
            a_ref, w_ref, r_ref, o_ref = refs
            o_ref[...] = (r_ref[...] + _dot(a_ref[...], w_ref[...])).astype(o_ref.dtype)

    in_specs = [pl.BlockSpec((tm, k), lambda j, i: (i, 0)), pl.BlockSpec((k, tn), lambda j, i: (0, j))]
    args = [a, w]
    if res is not None:
        in_specs.append(pl.BlockSpec((tm, tn), lambda j, i: (i, j)))
        args.append(res)
    return pl.pallas_call(
        body, name=name, grid=(n // tn, m // tm), in_specs=in_specs,
        out_specs=pl.BlockSpec((tm, tn), lambda j, i: (i, j)),
        out_shape=jax.ShapeDtypeStruct((m, n), out_dtype),
        compiler_params=_params(2),
    )(*args)


def _mm_nt(pairs, name, tm=512, tk=512):
    m = pairs[0][0].shape[0]
    k = pairs[0][1].shape[0]
    tm, tk = min(tm, m), min(tk, k)
    n_pairs = len(pairs)

    def body(*refs):
        o_ref = refs[-1]
        acc = None
        for q in range(n_pairs):
            d = _dot_nt(refs[2 * q][...], refs[2 * q + 1][...])
            acc = d if acc is None else acc + d
        o_ref[...] = acc

    in_specs, args = [], []
    for a, w in pairs:
        n = a.shape[1]
        in_specs += [pl.BlockSpec((tm, n), lambda j, i: (i, 0)), pl.BlockSpec((tk, n), lambda j, i: (j, 0))]
        args += [a, w]
    return pl.pallas_call(
        body, name=name, grid=(k // tk, m // tm), in_specs=in_specs,
        out_specs=pl.BlockSpec((tm, tk), lambda j, i: (i, j)),
        out_shape=jax.ShapeDtypeStruct((m, k), F32),
        compiler_params=_params(2),
    )(*args)


def _mm_tn(a, b, name, tk, tn, tt=512, out_dtype=BF16):
    t, k = a.shape
    n = b.shape[1]
    tt = min(tt, t)
    steps = t // tt

    def body(a_ref, b_ref, o_ref, acc):
        s = pl.program_id(2)

        @pl.when(s == 0)
        def _():
            acc[...] = jnp.zeros_like(acc)

        acc[...] += _dot_tn(a_ref[...], b_ref[...])

        @pl.when(s == steps - 1)
        def _():
            o_ref[...] = acc[...].astype(o_ref.dtype)

    return pl.pallas_call(
        body, name=name, grid=(k // tk, n // tn, steps),
        in_specs=[pl.BlockSpec((tt, tk), lambda i, j, s: (s, i)), pl.BlockSpec((tt, tn), lambda i, j, s: (s, j))],
        out_specs=pl.BlockSpec((tk, tn), lambda i, j, s: (i, j)),
        out_shape=jax.ShapeDtypeStruct((k, n), out_dtype),
        scratch_shapes=[pltpu.VMEM((tk, tn), F32)],
        compiler_params=_params(3),
    )(a, b)


def _ffn_up(h, wg, wu, name, tm=512, tn=256):
    m, k = h.shape
    n = wg.shape[1]
    tm = min(tm, m)

    def body(h_ref, wg_ref, wu_ref, gate_ref, up_ref, ff_ref):
        hv = h_ref[...]
        gate = _dot(hv, wg_ref[...])
        up = _dot(hv, wu_ref[...])
        gate_ref[...] = gate
        up_ref[...] = up
        ff_ref[...] = (gate * _sigmoid(gate) * up).astype(ff_ref.dtype)

    wspec = pl.BlockSpec((k, tn), lambda j, i: (0, j))
    ospec = pl.BlockSpec((tm, tn), lambda j, i: (i, j))
    return pl.pallas_call(
        body, name=name, grid=(n // tn, m // tm),
        in_specs=[pl.BlockSpec((tm, k), lambda j, i: (i, 0)), wspec, wspec],
        out_specs=[ospec, ospec, ospec],
        out_shape=[jax.ShapeDtypeStruct((m, n), F32), jax.ShapeDtypeStruct((m, n), F32),
                   jax.ShapeDtypeStruct((m, n), MXU_DTYPE)],
        compiler_params=_params(2),
    )(h, wg, wu)


def _ffn_bwd_act(dx, wd, gate, up, name, tm=512, tn=256):
    m, k = dx.shape
    n = wd.shape[0]
    tm = min(tm, m)

    def body(dx_ref, wd_ref, gate_ref, up_ref, dgate_ref, dup_ref):
        dff = _dot_nt(dx_ref[...], wd_ref[...])
        gate = gate_ref[...]
        sg = _sigmoid(gate)
        dgate_ref[...] = (dff * up_ref[...] * sg * (1.0 + gate * (1.0 - sg))).astype(dgate_ref.dtype)
        dup_ref[...] = (dff * gate * sg).astype(dup_ref.dtype)

    ospec = pl.BlockSpec((tm, tn), lambda j, i: (i, j))
    return pl.pallas_call(
        body, name=name, grid=(n // tn, m // tm),
        in_specs=[pl.BlockSpec((tm, k), lambda j, i: (i, 0)), pl.BlockSpec((tn, k), lambda j, i: (j, 0)),
                  ospec, ospec],
        out_specs=[ospec, ospec],
        out_shape=[jax.ShapeDtypeStruct((m, n), MXU_DTYPE), jax.ShapeDtypeStruct((m, n), MXU_DTYPE)],
        compiler_params=_params(2),
    )(dx, wd, gate, up)


def _chunk_masks():
    r = lax.broadcasted_iota(jnp.int32, (SUPER, SUPER), 0)
    c = lax.broadcasted_iota(jnp.int32, (SUPER, SUPER), 1)
    same = jnp.right_shift(r, 6) == jnp.right_shift(c, 6)
    lower = jnp.logical_and(same, c <= r)
    upper = jnp.logical_and(same, c >= r)
    return same, lower, upper


def _head_block_mask():
    r = lax.broadcasted_iota(jnp.int32, (LANES, LANES), 0)
    c = lax.broadcasted_iota(jnp.int32, (LANES, LANES), 1)
    return jnp.where(jnp.right_shift(r, 6) == jnp.right_shift(c, 6), 1.0, 0.0)


def _lower_bound(lb_raw):
    return 1.0 / (1.0 + jnp.exp(lb_raw[1:2, :] - lb_raw[0:1, :]))


def _hgrn_gates(q, hf, lb, same_b, lower_b):
    sig = _sigmoid(hf)
    f = lb + (1.0 - lb) * sig
    k = 1.0 - f
    lf = jnp.log(f)
    b = _rsum_left(lower_b, lf, 3)
    be = _rsum_left(same_b, lf, 3)
    eb = jnp.exp(b)
    enb = jnp.exp(-b)
    edb = jnp.exp(be - b)
    return sig, f, k, b, be, eb, enb, edb, q * eb, k * enb, k * edb


def _hgrn_fwd(proj, lower_bounds, norm_g, n_seq, seq):
    t = n_seq * seq
    n_super = seq // SUPER
    n_chunks = seq // CHUNK

    def body(q_ref, f_ref, i_ref, g_ref, lb_ref, ng_ref, out_ref, opre_ref, st_ref):
        masks = _head_masks()
        same, lower, _ = _chunk_masks()
        same_b, lower_b = _ones_where(same), _ones_where(lower)
        bd = _head_block_mask()
        lb = _lower_bound(lb_ref[...])
        ng = ng_ref[...]

        def step(sb, st):
            rows = pl.ds(pl.multiple_of(sb * SUPER, SUPER), SUPER)
            q, hf, v, hg = q_ref[rows, :], f_ref[rows, :], i_ref[rows, :], g_ref[rows, :]
            _, _, _, _, be, _, _, _, qe, ke, kd = _hgrn_gates(q, hf, lb, same_b, lower_b)
            o = None
            for m in masks:
                p = jnp.where(lower, _dot_nt(qe * m, ke), 0.0)
                d = _dot(p, v) * m
                o = d if o is None else o + d
            inter = []
            for c in range(SUPER // CHUNK):
                cr = slice(c * CHUNK, (c + 1) * CHUNK)
                st_ref[0, 0, sb * (SUPER // CHUNK) + c] = st
                inter.append(_dot_nt(qe[cr], st))
                dec = jnp.exp(be[c * CHUNK:c * CHUNK + 1, :])
                st = st * dec + bd * _dot_tn(v[cr], kd[cr])
            o = o + jnp.concatenate(inter, axis=0)
            opre_ref[rows, :] = o
            on = o * _head_rstd(o, masks) * ng
            out_ref[rows, :] = (on * hg * _sigmoid(hg)).astype(out_ref.dtype)
            return st

        lax.fori_loop(0, n_super, step, jnp.zeros((LANES, LANES), F32))

    def col(k):
        return pl.BlockSpec((seq, LANES), lambda p, b: (b, k * N_PAIRS + p))

    vec = lambda rows: pl.BlockSpec((rows, LANES), lambda p, b: (0, p))
    ospec = pl.BlockSpec((seq, LANES), lambda p, b: (b, p))
    return pl.pallas_call(
        body, name="hgrn_fwd", grid=(N_PAIRS, n_seq),
        in_specs=[col(0), col(1), col(2), col(3), vec(2), vec(1)],
        out_specs=[ospec, ospec,
                   pl.BlockSpec((1, 1, n_chunks, LANES, LANES), lambda p, b: (b, p, 0, 0, 0))],
        out_shape=[jax.ShapeDtypeStruct((t, 2 * GROUP), MXU_DTYPE), jax.ShapeDtypeStruct((t, GROUP), F32),
                   jax.ShapeDtypeStruct((n_seq, N_PAIRS, n_chunks, LANES, LANES), F32)],
        compiler_params=_params(2),
    )(proj, proj, proj, proj, lower_bounds, norm_g)


def _hgrn_bwd(proj, lower_bounds, norm_g, dmix, opre, states, n_seq, seq):
    t = n_seq * seq
    n_super = seq // SUPER
    n_chunks = seq // CHUNK
    per = SUPER // CHUNK

    def body(q_ref, f_ref, i_ref, g_ref, lb_ref, ng_ref, dm_ref, opre_ref, st_ref,
             dq_ref, df_ref, di_ref, dg_ref, dlb_ref, dng_ref):
        masks = _head_masks()
        same, lower, upper = _chunk_masks()
        same_b, lower_b, upper_b = _ones_where(same), _ones_where(lower), _ones_where(upper)
        bd = _head_block_mask()
        lb_raw = lb_ref[...]
        lb = _lower_bound(lb_raw)
        ng = ng_ref[...]
        row_id = lax.broadcasted_iota(jnp.int32, (SUPER, LANES), 0)

        @pl.when(pl.program_id(1) == 0)
        def _():
            dlb_ref[...] = jnp.zeros_like(dlb_ref)
            dng_ref[...] = jnp.zeros_like(dng_ref)

        def step(it, dst):
            sb = n_super - 1 - it
            rows = pl.ds(pl.multiple_of(sb * SUPER, SUPER), SUPER)
            q, hf, v, hg = q_ref[rows, :], f_ref[rows, :], i_ref[rows, :], g_ref[rows, :]
            sig, f, k, _, be, eb, enb, edb, qe, ke, kd = _hgrn_gates(q, hf, lb, same_b, lower_b)
            o = opre_ref[rows, :]
            r = _head_rstd(o, masks)
            oh = o * r
            dm = dm_ref[rows, :]
            sg = _sigmoid(hg)
            dg_ref[rows, :] = (dm * oh * ng * sg * (1.0 + hg * (1.0 - sg))).astype(dg_ref.dtype)
            don = dm * hg * sg
            dng_ref[...] += jnp.sum(don * oh, axis=0, keepdims=True)
            doh = don * ng
            do = r * (doh - oh * _head_mean(doh * oh, masks))
            dqe, dke, dv = None, None, None
            for m in masks:
                dom = do * m
                qem = qe * m
                p = jnp.where(lower, _dot_nt(qem, ke), 0.0)
                dp = jnp.where(lower, _dot_nt(dom, v), 0.0)
                a = _dot(dp, ke) * m
                bq = _dot_tn(dp, qem)
                cv = _dot_tn(p, dom)
                dqe = a if dqe is None else dqe + a
                dke = bq if dke is None else dke + bq
                dv = cv if dv is None else dv + cv
            dqe_i, dv_i, dkd_i, dbe_rows = [None] * per, [None] * per, [None] * per, None
            for c in reversed(range(per)):
                cr = slice(c * CHUNK, (c + 1) * CHUNK)
                st_prev = st_ref[0, 0, sb * per + c]
                dec = jnp.exp(be[c * CHUNK:c * CHUNK + 1, :])
                du = bd * dst
                dv_i[c] = _dot_nt(kd[cr], du)
                dkd_i[c] = _dot(v[cr], du)
                ddec = jnp.sum(dst * st_prev, axis=0, keepdims=True)
                dqe_i[c] = _dot(do[cr], st_prev)
                dst = bd * (dst * dec + _dot_tn(do[cr], qe[cr]))
                put = jnp.where(row_id == c * CHUNK, ddec * dec, 0.0)
                dbe_rows = put if dbe_rows is None else dbe_rows + put
            dqe = dqe + jnp.concatenate(dqe_i, axis=0)
            dv = dv + jnp.concatenate(dv_i, axis=0)
            dkd = jnp.concatenate(dkd_i, axis=0)
            dk = dke * enb + dkd * edb
            db = dqe * qe - dke * ke - dkd * kd
            dbe = dkd * kd + dbe_rows
            dlf = _rsum_left(upper_b, db, 3) + _rsum_left(same_b, dbe, 3)
            dfv = dlf / f - dk
            dq_ref[rows, :] = (dqe * eb).astype(dq_ref.dtype)
            di_ref[rows, :] = dv.astype(di_ref.dtype)
            df_ref[rows, :] = (dfv * (1.0 - lb) * sig * (1.0 - sig)).astype(df_ref.dtype)
            dlb = jnp.sum(dfv * (1.0 - sig), axis=0, keepdims=True)
            da0 = dlb * lb * (1.0 - lb)
            dlb_ref[0:1, :] += da0
            dlb_ref[1:2, :] -= da0
            return dst

        lax.fori_loop(0, n_super, step, jnp.zeros((LANES, LANES), F32))

    def col(k):
        return pl.BlockSpec((seq, LANES), lambda p, b: (b, k * N_PAIRS + p))

    vec = lambda rows: pl.BlockSpec((rows, LANES), lambda p, b: (0, p))
    ospec = pl.BlockSpec((seq, LANES), lambda p, b: (b, p))
    piece = jax.ShapeDtypeStruct((t, GROUP), MXU_DTYPE)
    return pl.pallas_call(
        body, name="hgrn_bwd", grid=(N_PAIRS, n_seq),
        in_specs=[col(0), col(1), col(2), col(3), vec(2), vec(1), ospec, ospec,
                  pl.BlockSpec((1, 1, n_chunks, LANES, LANES), lambda p, b: (b, p, 0, 0, 0))],
        out_specs=[ospec, ospec, ospec, ospec, vec(2), vec(1)],
        out_shape=[piece, piece, piece, piece,
                   jax.ShapeDtypeStruct((2, GROUP), F32), jax.ShapeDtypeStruct((1, GROUP), F32)],
        compiler_params=_params(2),
    )(proj, proj, proj, proj, lower_bounds, norm_g, dmix, opre, states)


SB_SCALE = 1.0 / math.sqrt(D_HEAD)


def _tile_masks():
    r = lax.broadcasted_iota(jnp.int32, (SB_BLOCK, SB_BLOCK), 0)
    c = lax.broadcasted_iota(jnp.int32, (SB_BLOCK, SB_BLOCK), 1)
    return r, c


def _sb_fwd(proj, norm_g, mixed, shards, n_seq, seq):
    t = n_seq * seq
    nq = seq // SB_BLOCK
    q0, k0, v0 = 4 * N_PAIRS, 5 * N_PAIRS, 6 * N_PAIRS
    n_w = len(shards)
    n_steps = N_PAIRS * n_seq * nq

    def body(q_ref, k_ref, v_ref, ng_ref, mixed_in, *rest):
        del mixed_in
        shard_refs = rest[:n_w]
        out_ref, opre_ref, ctot_ref = rest[n_w:n_w + 3]
        gathered = rest[n_w + 3:2 * n_w + 3]
        send_sems, recv_sems, local_sems = rest[2 * n_w + 3:]
        i = pl.program_id(2)
        step = (pl.program_id(0) * n_seq + pl.program_id(1)) * nq + i
        plan = _GatherPlan(shard_refs, gathered, send_sems, recv_sems, local_sems)

        @pl.when(step == 0)
        def _():
            plan.start()

        @pl.when(step == n_steps // 2)
        def _():
            plan.forward()

        masks = _head_masks()
        r, c = _tile_masks()
        strict = c < r
        neg_suffix = jnp.where(r >= c, -1.0, 0.0).astype(BF16)
        qs = q_ref[...] * SB_SCALE
        qhs = [(qs * m).astype(MXU_DTYPE) for m in masks]

        def tiles(js, carry, diag):
            rows = [pl.ds(pl.multiple_of(j * SB_BLOCK, SB_BLOCK), SB_BLOCK) for j in js]
            ks = [k_ref[rw, :].astype(MXU_DTYPE) for rw in rows]
            vs = [v_ref[rw, :].astype(MXU_DTYPE) for rw in rows]
            zs = [[_dot_nt(qh, kj) for qh in qhs] for kj in ks]
            ccs = [[_softplus(z) for z in zt] for zt in zs]
            if diag:
                ccs = [[jnp.where(strict, cc, 0.0) for cc in ct] for ct in ccs]
            logits = [[_dot_nt(qh, kj) + _rsum_right(cc, neg_suffix, 2) for qh, cc in zip(qhs, ct)]
                      for kj, ct in zip(ks, ccs)]
            out = []
            for h, (run, acc) in enumerate(carry):
                for t in range(len(js)):
                    a = jnp.exp(logits[t][h] - run)
                    if diag:
                        a = jnp.where(strict, a, 0.0)
                    acc = acc + _dot(a, vs[t])
                    run = run + jnp.sum(ccs[t][h], axis=1, keepdims=True)
                out.append((run, acc))
            return tuple(out)

        start = ((jnp.zeros((SB_BLOCK, 1), F32), jnp.zeros((SB_BLOCK, LANES), F32)),) * 2
        carry = tiles([i], start, True)
        carry = lax.fori_loop(0, i // 2, lambda s, cy: tiles([i - 1 - 2 * s, i - 2 - 2 * s], cy, False), carry)
        carry = lax.fori_loop(0, i % 2, lambda s, cy: tiles([0], cy, False), carry)
        opre = carry[0][1] * masks[0] + carry[1][1] * masks[1]
        ctot = carry[0][0] * masks[0] + carry[1][0] * masks[1]
        opre_ref[...] = opre
        ctot_ref[...] = ctot
        out_ref[...] = (opre * _head_rstd(opre, masks) * ng_ref[...]).astype(out_ref.dtype)

        @pl.when(step == n_steps - 1)
        def _():
            plan.finish()

    qspec = pl.BlockSpec((SB_BLOCK, LANES), lambda p, b, i: (b * nq + i, q0 + p))
    ospec = pl.BlockSpec((SB_BLOCK, LANES), lambda p, b, i: (b * nq + i, p))
    hbm = pl.BlockSpec(memory_space=pltpu.HBM)
    outs = pl.pallas_call(
        body, name="sb_fwd", grid=(N_PAIRS, n_seq, nq),
        in_specs=[qspec,
                  pl.BlockSpec((seq, LANES), lambda p, b, i: (b, k0 + p)),
                  pl.BlockSpec((seq, LANES), lambda p, b, i: (b, v0 + p)),
                  pl.BlockSpec((1, LANES), lambda p, b, i: (0, p)), hbm] + [hbm] * n_w,
        out_specs=[pl.BlockSpec((SB_BLOCK, LANES), lambda p, b, i: (b * nq + i, N_PAIRS + p)), ospec, ospec]
        + [hbm] * n_w,
        out_shape=[jax.ShapeDtypeStruct(mixed.shape, mixed.dtype), jax.ShapeDtypeStruct((t, GROUP), F32),
                   jax.ShapeDtypeStruct((t, GROUP), F32)]
        + [jax.ShapeDtypeStruct((N_DEV,) + s.shape, s.dtype) for s in shards],
        scratch_shapes=[pltpu.SemaphoreType.DMA((n_w * _GatherPlan.COPIES,)),
                        pltpu.SemaphoreType.DMA((n_w * _GatherPlan.COPIES,)), pltpu.SemaphoreType.DMA((n_w,))],
        input_output_aliases={4: 0},
        compiler_params=_params(3),
    )(proj, proj, proj, norm_g, mixed, *shards)
    return outs[0], outs[1], outs[2], list(outs[3:])


def _sb_bwd(proj, norm_g, dmix, opre, ctot, grads, n_seq, seq):
    t = n_seq * seq
    nq = seq // SB_BLOCK
    q0, k0, v0 = 4 * N_PAIRS, 5 * N_PAIRS, 6 * N_PAIRS
    n_w = len(grads)
    n_steps = N_PAIRS * n_seq * nq

    def body(q_ref, k_ref, v_ref, ng_ref, dm_ref, opre_ref, ctot_ref, *rest):
        grad_refs = rest[:n_w]
        dq_ref, dk_ref, dv_ref, dng_ref = rest[n_w:n_w + 4]
        lands = rest[n_w + 4:2 * n_w + 4]
        dk_acc, dv_acc, send_sems, recv_sems, local_sems = rest[2 * n_w + 4:]
        p_id, b_id, i = pl.program_id(0), pl.program_id(1), pl.program_id(2)
        step = (p_id * n_seq + b_id) * nq + i
        plan = _ScatterPlan(grad_refs, lands, send_sems, recv_sems, local_sems)

        @pl.when(step == 0)
        def _():
            plan.start()

        masks = _head_masks()
        r, c = _tile_masks()
        strict = c < r
        before = _ones_where(r < c)
        upto = _ones_where(r <= c)

        @pl.when(i == 0)
        def _():
            dk_acc[...] = jnp.zeros_like(dk_acc)
            dv_acc[...] = jnp.zeros_like(dv_acc)

        @pl.when(jnp.logical_and(b_id == 0, i == 0))
        def _():
            dng_ref[...] = jnp.zeros_like(dng_ref)

        o = opre_ref[...]
        rs = _head_rstd(o, masks)
        oh = o * rs
        dm = dm_ref[...]
        dng_ref[...] += jnp.sum(dm * oh, axis=0, keepdims=True)
        doh = dm * ng_ref[...]
        do = rs * (doh - oh * _head_mean(doh * oh, masks))

        qs = q_ref[...] * SB_SCALE
        ct = ctot_ref[...]
        qhs = [(qs * m).astype(MXU_DTYPE) for m in masks]
        doms = [(do * m).astype(MXU_DTYPE) for m in masks]
        totals = [jnp.max(ct * m, axis=1, keepdims=True) for m in masks]
        heads = range(len(masks))

        def tiles(js, carry, diag):
            nt = range(len(js))
            rows = [pl.ds(pl.multiple_of(j * SB_BLOCK, SB_BLOCK), SB_BLOCK) for j in js]
            ks = [k_ref[rw, :].astype(MXU_DTYPE) for rw in rows]
            vs = [v_ref[rw, :].astype(MXU_DTYPE) for rw in rows]
            zs = [[_dot_nt(qhs[h], ks[t]) for h in heads] for t in nt]
            das = [[_dot_nt(doms[h], vs[t]) for h in heads] for t in nt]
            sps = [[_softplus(z) for z in zt] for zt in zs]
            sigs = [[jnp.exp(z - sp) for z, sp in zip(zt, st)] for zt, st in zip(zs, sps)]
            ccs = [[jnp.where(strict, sp, 0.0) for sp in st] for st in sps] if diag else sps
            logits = [[zs[t][h] + _rsum_right(ccs[t][h], before, 2) for h in heads] for t in nt]
            out = []
            for h in heads:
                pc, pdl, dq_h = carry[h]
                for t in nt:
                    a = jnp.exp(logits[t][h] - (totals[h] - pc))
                    if diag:
                        a = jnp.where(strict, a, 0.0)
                    dl = a * das[t][h]
                    dv_acc[rows[t], :] += _dot_tn(a, doms[h])
                    dz = dl - sigs[t][h] * (pdl + _rsum_right(dl, upto, 2))
                    if diag:
                        dz = jnp.where(strict, dz, 0.0)
                    dzb = dz.astype(MXU_DTYPE)
                    dq_h = dq_h + _dot(dzb, ks[t])
                    dk_acc[rows[t], :] += _dot_tn(dzb, qhs[h])
                    pc = pc + jnp.sum(ccs[t][h], axis=1, keepdims=True)
                    pdl = pdl + jnp.sum(dl, axis=1, keepdims=True)
                out.append((pc, pdl, dq_h))
            return tuple(out)

        zero = jnp.zeros((SB_BLOCK, 1), F32)
        carry = ((zero, zero, jnp.zeros((SB_BLOCK, LANES), F32)),) * 2
        carry = lax.fori_loop(0, i // 2, lambda s, cy: tiles([2 * s, 2 * s + 1], cy, False), carry)
        carry = lax.fori_loop(0, i % 2, lambda s, cy: tiles([i - 1], cy, False), carry)
        carry = tiles([i], carry, True)
        dq = carry[0][2] * masks[0] + carry[1][2] * masks[1]
        dq_ref[...] = (dq * SB_SCALE).astype(dq_ref.dtype)

        @pl.when(i == nq - 1)
        def _():
            dk_ref[...] = dk_acc[...].astype(dk_ref.dtype)
            dv_ref[...] = dv_acc[...].astype(dv_ref.dtype)

        @pl.when(step == n_steps - 1)
        def _():
            plan.finish()

    qspec = pl.BlockSpec((SB_BLOCK, LANES), lambda p, b, i: (b * nq + i, q0 + p))
    ospec = pl.BlockSpec((SB_BLOCK, LANES), lambda p, b, i: (b * nq + i, p))
    dmspec = pl.BlockSpec((SB_BLOCK, LANES), lambda p, b, i: (b * nq + i, N_PAIRS + p))
    full = lambda k: pl.BlockSpec((seq, LANES), lambda p, b, i: (b, k + p))
    vec = pl.BlockSpec((1, LANES), lambda p, b, i: (0, p))
    hbm = pl.BlockSpec(memory_space=pltpu.HBM)
    piece = jax.ShapeDtypeStruct((t, GROUP), MXU_DTYPE)
    outs = pl.pallas_call(
        body, name="sb_bwd", grid=(N_PAIRS, n_seq, nq),
        in_specs=[qspec, full(k0), full(v0), vec, dmspec, ospec, ospec] + [hbm] * n_w,
        out_specs=[ospec, full(0), full(0), vec] + [hbm] * n_w,
        out_shape=[piece, piece, piece, jax.ShapeDtypeStruct((1, GROUP), F32)]
        + [jax.ShapeDtypeStruct(g.shape, g.dtype) for g in grads],
        scratch_shapes=[pltpu.VMEM((seq, LANES), F32), pltpu.VMEM((seq, LANES), F32),
                        pltpu.SemaphoreType.DMA((n_w * (N_DEV - 1),)), pltpu.SemaphoreType.DMA((n_w * (N_DEV - 1),)),
                        pltpu.SemaphoreType.DMA((n_w,))],
        compiler_params=_params(3),
    )(proj, proj, proj, norm_g, dmix, opre, ctot, *grads)
    return outs[0], outs[1], outs[2], outs[3], list(outs[4:])


def _mesh_place():
    x, y, c = lax.axis_index("x"), lax.axis_index("y"), lax.axis_index("c")
    return x, y, c


def _peer(x, y, c, k):
    px = lax.rem(x + ((k >> 2) & 1), 2)
    py = lax.rem(y + ((k >> 1) & 1), 2)
    pc = lax.rem(c + (k & 1), 2)
    return (px, py, pc), 4 * px + 2 * py + pc


def _remote(src, dst, send_sem, recv_sem, to):
    return pltpu.make_async_remote_copy(src_ref=src, dst_ref=dst, send_sem=send_sem, recv_sem=recv_sem,
                                        device_id=to, device_id_type=pl.DeviceIdType.MESH)


class _GatherPlan:
    COPIES = 7

    def __init__(self, shards, gathered, send_sems, recv_sems, local_sems):
        x, y, c = _mesh_place()
        self.c = c
        self.me = (x, y, c)
        self.sibling = (x, y, 1 - c)
        self.chips = [(1 - x, y), (x, 1 - y), (1 - x, 1 - y)]
        self.tensors = list(zip(shards, gathered))
        self.send_sems, self.recv_sems, self.local_sems = send_sems, recv_sems, local_sems

    @staticmethod
    def _index(place):
        return 4 * place[0] + 2 * place[1] + place[2]

    def _copy(self, w, k, block, to, own=False):
        shard, gathered = self.tensors[w]
        slot = gathered.at[self._index(block)]
        n = w * self.COPIES + k
        return _remote(shard if own else slot, slot, self.send_sems.at[n], self.recv_sems.at[n], to)

    def _local(self, w):
        shard, gathered = self.tensors[w]
        return pltpu.make_async_copy(shard, gathered.at[self._index(self.me)], self.local_sems.at[w])

    def _first(self, w):
        return [self._copy(w, 0, self.me, self.sibling, own=True)] + [
            self._copy(w, 1 + j, self.me, (*chip, self.c), own=True) for j, chip in enumerate(self.chips)]

    def _passed(self, w):
        return [self._copy(w, 4 + j, (*chip, self.c), self.sibling) for j, chip in enumerate(self.chips)]

    def start(self):
        for w in range(len(self.tensors)):
            self._local(w).start()
            for cp in self._first(w):
                cp.start()

    def forward(self):
        for w in range(len(self.tensors)):
            passed = self._passed(w)
            for j, chip in enumerate(self.chips):
                self._copy(w, 1 + j, (*chip, self.c), self.me).wait_recv()
                passed[j].start()

    def finish(self):
        for w in range(len(self.tensors)):
            self._copy(w, 0, self.sibling, self.me).wait_recv()
            for j, chip in enumerate(self.chips):
                self._copy(w, 4 + j, (*chip, 1 - self.c), self.me).wait_recv()
            for cp in self._first(w) + self._passed(w):
                cp.wait_send()
            self._local(w).wait()


class _ScatterPlan:
    def __init__(self, grads, lands, send_sems, recv_sems, local_sems):
        self.place = _mesh_place()
        x, y, c = self.place
        self.me = 4 * x + 2 * y + c
        self.tensors = list(zip(grads, lands))
        self.send_sems, self.recv_sems, self.local_sems = send_sems, recv_sems, local_sems

    def _copies(self, w):
        grad, land = self.tensors[w]
        out = []
        for k in range(1, N_DEV):
            peer, pidx = _peer(*self.place, k)
            n = w * (N_DEV - 1) + k - 1
            sems = (self.send_sems.at[n], self.recv_sems.at[n], peer)
            out.append((_remote(grad.at[pidx], land.at[self.me], *sems), _remote(grad.at[pidx], land.at[pidx], *sems)))
        return out

    def _local(self, w):
        grad, land = self.tensors[w]
        return pltpu.make_async_copy(grad.at[self.me], land.at[self.me], self.local_sems.at[w])

    def start(self):
        for w in range(len(self.tensors)):
            self._local(w).start()
            for send, _ in self._copies(w):
                send.start()

    def finish(self):
        for w in range(len(self.tensors)):
            copies = self._copies(w)
            for _, arrival in copies:
                arrival.wait_recv()
            for send, _ in copies:
                send.wait_send()
            self._local(w).wait()


def _cast_shards(shards):
    def body(*refs):
        n = len(refs) // 2
        for src, dst in zip(refs[:n], refs[n:]):
            dst[...] = src[...].astype(dst.dtype)

    vmem = pl.BlockSpec(memory_space=pltpu.VMEM)
    return pl.pallas_call(
        body, name="cast_shards", in_specs=[vmem] * len(shards), out_specs=[vmem] * len(shards),
        out_shape=[jax.ShapeDtypeStruct(s.shape, BF16) for s in shards],
        compiler_params=pltpu.CompilerParams(vmem_limit_bytes=VMEM_LIMIT),
    )(*shards)


def _gather_w_in(shard):
    rows, cols = shard.shape

    def body(w_ref, out_ref, blocks, send_sems, recv_sems, local_sems):
        plan = _GatherPlan([w_ref], [blocks], send_sems, recv_sems, local_sems)
        plan.start()
        plan.forward()
        plan.finish()
        for d in range(N_DEV):
            out_ref[:, d * cols:(d + 1) * cols] = blocks[d]

    vmem = pl.BlockSpec(memory_space=pltpu.VMEM)
    return pl.pallas_call(
        body, name="gather_w_in", in_specs=[vmem], out_specs=vmem,
        out_shape=jax.ShapeDtypeStruct((rows, N_DEV * cols), shard.dtype),
        scratch_shapes=[pltpu.VMEM((N_DEV, rows, cols), shard.dtype), pltpu.SemaphoreType.DMA((_GatherPlan.COPIES,)),
                        pltpu.SemaphoreType.DMA((_GatherPlan.COPIES,)), pltpu.SemaphoreType.DMA((1,))],
        compiler_params=pltpu.CompilerParams(vmem_limit_bytes=VMEM_LIMIT),
    )(shard)


def _slots_to_cols(g, name, tk=256):
    _, k, cols = g.shape

    def body(g_ref, o_ref):
        for d in range(N_DEV):
            o_ref[:, d * cols:(d + 1) * cols] = g_ref[d]

    return pl.pallas_call(
        body, name=name, grid=(k // tk,),
        in_specs=[pl.BlockSpec((N_DEV, tk, cols), lambda i: (0, i, 0))],
        out_specs=pl.BlockSpec((tk, N_DEV * cols), lambda i: (i, 0)),
        out_shape=jax.ShapeDtypeStruct((k, N_DEV * cols), g.dtype),
        compiler_params=_params(1),
    )(g)


def _dw_slots(a, pieces, name, tk=512, tt=512):
    t, k = a.shape
    widths = [p.shape[1] for p in pieces]
    n = sum(widths)
    cols = n // N_DEV
    tt = min(tt, t)
    steps = t // tt
    n_p = len(pieces)

    def body(a_ref, *rest):
        piece_refs, o_ref, acc = rest[:n_p], rest[n_p], rest[n_p + 1]
        s = pl.program_id(1)

        @pl.when(s == 0)
        def _():
            acc[...] = jnp.zeros_like(acc)

        av = a_ref[...]
        off = 0
        for p_ref, width in zip(piece_refs, widths):
            acc[:, off:off + width] += _dot_tn(av, p_ref[...])
            off += width

        @pl.when(s == steps - 1)
        def _():
            for d in range(N_DEV):
                o_ref[d] = acc[:, d * cols:(d + 1) * cols].astype(o_ref.dtype)

    return pl.pallas_call(
        body, name=name, grid=(k // tk, steps),
        in_specs=[pl.BlockSpec((tt, tk), lambda i, s: (s, i))]
        + [pl.BlockSpec((tt, width), lambda i, s: (s, 0)) for width in widths],
        out_specs=pl.BlockSpec((N_DEV, tk, cols), lambda i, s: (0, i, 0)),
        out_shape=jax.ShapeDtypeStruct((N_DEV, k, cols), BF16),
        scratch_shapes=[pltpu.VMEM((tk, n), F32)],
        compiler_params=_params(2),
    )(a, *pieces)


def _dh_pieces(pieces, w, grads, name, tm=256, tk=512):
    m = pieces[0].shape[0]
    k = w.shape[0]
    tm, tk = min(tm, m), min(tk, k)
    widths = [p.shape[1] for p in pieces]
    n_p, n_w = len(pieces), len(grads)
    n_steps = (k // tk) * (m // tm)

    def body(*refs):
        piece_refs, w_refs = refs[:n_p], refs[n_p:2 * n_p]
        grad_refs = refs[2 * n_p:2 * n_p + n_w]
        o_ref = refs[2 * n_p + n_w]
        lands = refs[2 * n_p + n_w + 1:2 * n_p + 2 * n_w + 1]
        send_sems, recv_sems, local_sems = refs[2 * n_p + 2 * n_w + 1:]
        step = pl.program_id(0) * (m // tm) + pl.program_id(1)
        plan = _ScatterPlan(grad_refs, lands, send_sems, recv_sems, local_sems)

        @pl.when(step == 0)
        def _():
            plan.start()

        acc = None
        for p_ref, w_ref in zip(piece_refs, w_refs):
            d = _dot_nt(p_ref[...], w_ref[...])
            acc = d if acc is None else acc + d
        o_ref[...] = acc

        @pl.when(step == n_steps - 1)
        def _():
            plan.finish()

    offs = [sum(widths[:q]) // widths[q] for q in range(n_p)]
    hbm = pl.BlockSpec(memory_space=pltpu.HBM)
    in_specs = [pl.BlockSpec((tm, width), lambda j, i: (i, 0)) for width in widths]
    in_specs += [pl.BlockSpec((tk, width), functools.partial(lambda j, i, o: (j, o), o=o))
                 for width, o in zip(widths, offs)]
    outs = pl.pallas_call(
        body, name=name, grid=(k // tk, m // tm), in_specs=in_specs + [hbm] * n_w,
        out_specs=[pl.BlockSpec((tm, tk), lambda j, i: (i, j))] + [hbm] * n_w,
        out_shape=[jax.ShapeDtypeStruct((m, k), F32)] + [jax.ShapeDtypeStruct(g.shape, g.dtype) for g in grads],
        scratch_shapes=[pltpu.SemaphoreType.DMA((n_w * (N_DEV - 1),)), pltpu.SemaphoreType.DMA((n_w * (N_DEV - 1),)),
                        pltpu.SemaphoreType.DMA((n_w,))],
        compiler_params=_params(2),
    )(*pieces, *([w] * n_p), *grads)
    return outs[0], list(outs[1:])


SMALL_LAYOUT = ((0, 0, 0, 0, D_MODEL), (1, 0, 1, 0, GROUP), (1, 1, 1, GROUP, GROUP), (2, 0, 2, 0, GROUP),
                (3, 0, 2, GROUP, GROUP), (4, 0, 3, 0, D_MODEL), (5, 0, 4, 0, D_MODEL))
LOSS_ROW = 5
N_SMALL = 6


def _small_step(grads, loss_part, ws, ms, vs):
    def body(*refs):
        g_in, loss_in = refs[:N_SMALL], refs[N_SMALL]
        params = [refs[1 + (q + 1) * N_SMALL:1 + (q + 2) * N_SMALL] for q in range(3)]
        o0 = 1 + 4 * N_SMALL
        outs = [refs[o0 + q * N_SMALL:o0 + (q + 1) * N_SMALL] for q in range(4)]
        loss_out = refs[o0 + 4 * N_SMALL]
        pack, land, wp, mp, vp, send_sems, recv_sems = refs[o0 + 4 * N_SMALL + 1:]

        def place(dst, srcs):
            dst[...] = jnp.zeros_like(dst)
            for p, sr, dr, dc, width in SMALL_LAYOUT:
                dst[dr:dr + 1, dc:dc + width] = srcs[p][sr:sr + 1, :]

        place(pack, g_in)
        pack[LOSS_ROW:LOSS_ROW + 1, 0:LANES] = loss_in[...]
        for dst, srcs in zip((wp, mp, vp), params):
            place(dst, srcs)

        x, y, c = _mesh_place()
        me = 4 * x + 2 * y + c
        land[me] = pack[...]
        sends = []
        for k in range(1, N_DEV):
            peer, _ = _peer(x, y, c, k)
            cp = _remote(pack, land.at[me], send_sems.at[k - 1], recv_sems.at[k - 1], peer)
            cp.start()
            sends.append(cp)
        for k in range(1, N_DEV):
            peer, pidx = _peer(x, y, c, k)
            _remote(pack, land.at[pidx], send_sems.at[k - 1], recv_sems.at[k - 1], peer).wait_recv()
        for cp in sends:
            cp.wait_send()

        g = land[0]
        for d in range(1, N_DEV):
            g = g + land[d]
        delta, nm, nv = _adam(wp[...], g, mp[...], vp[...])
        for val, out in zip((g, delta, nm, nv), outs):
            for p, sr, dr, dc, width in SMALL_LAYOUT:
                out[p][sr:sr + 1, :] = val[dr:dr + 1, dc:dc + width]
        loss_out[...] = g[LOSS_ROW:LOSS_ROW + 1, 0:LANES]

    vmem = pl.BlockSpec(memory_space=pltpu.VMEM)
    n_in = 1 + 4 * N_SMALL
    shapes = [jax.ShapeDtypeStruct(w.shape, F32) for w in ws]
    packed = pltpu.VMEM((SMALL_ROWS, D_MODEL), F32)
    outs = pl.pallas_call(
        body, name="small_step", in_specs=[vmem] * n_in, out_specs=[vmem] * (4 * N_SMALL + 1),
        out_shape=shapes * 4 + [jax.ShapeDtypeStruct((1, LANES), F32)],
        scratch_shapes=[packed, pltpu.VMEM((N_DEV, SMALL_ROWS, D_MODEL), F32), packed, packed, packed,
                        pltpu.SemaphoreType.DMA((N_DEV - 1,)), pltpu.SemaphoreType.DMA((N_DEV - 1,))],
    )(*grads, loss_part, *ws, *ms, *vs)
    return [outs[q * N_SMALL:(q + 1) * N_SMALL] for q in range(4)], outs[4 * N_SMALL]


def _adam(w, g, m, v):
    m = ADAM_B1 * m + (1.0 - ADAM_B1) * g
    v = ADAM_B2 * v + (1.0 - ADAM_B2) * (g * g)
    m_hat = m / (1.0 - ADAM_B1 ** ADAM_STEP)
    v_hat = v / (1.0 - ADAM_B2 ** ADAM_STEP)
    delta = -ADAM_LR * (m_hat / (jnp.sqrt(v_hat) + ADAM_EPS) + ADAM_WD * w)
    return delta, m, v


def _reduce_adamw(land, w, m, v, name, tr):
    _, rows, width = land.shape

    def body(land_ref, w_ref, m_ref, v_ref, g_ref, d_ref, nm_ref, nv_ref):
        g = land_ref[0].astype(F32)
        for d in range(1, N_DEV):
            g = g + land_ref[d].astype(F32)
        delta, nm, nv = _adam(w_ref[...], g, m_ref[...], v_ref[...])
        g_ref[...] = g
        d_ref[...] = delta
        nm_ref[...] = nm
        nv_ref[...] = nv

    row = pl.BlockSpec((tr, width), lambda i: (i, 0))
    out = jax.ShapeDtypeStruct((rows, width), F32)
    return pl.pallas_call(
        body, name=name, grid=(rows // tr,),
        in_specs=[pl.BlockSpec((N_DEV, tr, width), lambda i: (0, i, 0)), row, row, row],
        out_specs=[row, row, row, row], out_shape=[out, out, out, out],
        compiler_params=_params(1),
    )(land, w, m, v)


def kernel(x, mix_norm_g, w_in, lower_bounds, hgrn_norm_g, sb_norm_g, w_out, ffn_norm_g, w_gate, w_up, w_down, final_norm_g, loss_target, m_mix_norm_g, m_w_in, m_lower_bounds, m_hgrn_norm_g, m_sb_norm_g, m_w_out, m_ffn_norm_g, m_w_gate, m_w_up, m_w_down, m_final_norm_g, v_mix_norm_g, v_w_in, v_lower_bounds, v_hgrn_norm_g, v_sb_norm_g, v_w_out, v_ffn_norm_g, v_w_gate, v_w_up, v_w_down, v_final_norm_g):
    n_seq, seq, d = x.shape
    t = n_seq * seq
    x2d = x.reshape(t, d)
    tgt = loss_target.reshape(t, d)
    final_g = final_norm_g.reshape(1, d)
    big_w = [w_in[0], w_out[0], w_gate[0], w_up[0], w_down[0]]
    big_m = [m_w_in[0], m_w_out[0], m_w_gate[0], m_w_up[0], m_w_down[0]]
    big_v = [v_w_in[0], v_w_out[0], v_w_gate[0], v_w_up[0], v_w_down[0]]

    sh_in, sh_out, sh_gate, sh_up, sh_down = _cast_shards(big_w)
    wf_in = _gather_w_in(sh_in)
    h1 = _rmsnorm_fwd(x2d, mix_norm_g, "norm_mix")
    proj = _mm_nn(h1, wf_in, "proj")
    mixed, oa_pre, states = _hgrn_fwd(proj, lower_bounds, hgrn_norm_g, n_seq, seq)
    mixed, ob_pre, ctot, gathered = _sb_fwd(proj, sb_norm_g, mixed, [sh_out, sh_gate, sh_up, sh_down], n_seq, seq)
    wf_out = gathered[0].reshape(d, d)
    wf_gate = _slots_to_cols(gathered[1], "cols_gate")
    wf_up = _slots_to_cols(gathered[2], "cols_up")
    wf_down = gathered[3].reshape(D_FF, d)
    x1 = _mm_nn(mixed, wf_out, "mix_out", res=x2d)
    h2 = _rmsnorm_fwd(x1, ffn_norm_g, "norm_ffn")
    gate, up, ff = _ffn_up(h2, wf_gate, wf_up, "ffn_up")
    x2 = _mm_nn(ff, wf_down, "ffn_down", res=x1)
    dx2, d_final_g, loss_part = _final_loss(x2, tgt, final_g, "loss_head")

    dgate, dup = _ffn_bwd_act(dx2, wf_down, gate, up, "ffn_bwd_act")
    dw_down = _mm_tn(ff, dx2, "dw_down", tk=1408, tn=1024).reshape(N_DEV, D_FF // N_DEV, d)
    dh2 = _mm_nt([(dgate, wf_gate), (dup, wf_up)], "dh_ffn", tm=256)
    dw_gate = _dw_slots(h2, [dgate], "dw_gate")
    dw_up = _dw_slots(h2, [dup], "dw_up")
    dx1, d_ffn_g = _rmsnorm_bwd(x1, ffn_norm_g, dh2, dx2, "norm_ffn_bwd")
    dmix = _mm_nt([(dx1, wf_out)], "dmix")
    dw_out = _mm_tn(mixed, dx1, "dw_out", tk=512, tn=1024).reshape(N_DEV, d // N_DEV, d)
    dsq, dsk, dsv, d_sb_g, lands = _sb_bwd(proj, sb_norm_g, dmix, ob_pre, ctot, [dw_out, dw_gate, dw_up, dw_down],
                                            n_seq, seq)
    dhq, dhf, dhi, dhg, d_lb, d_hgrn_g = _hgrn_bwd(proj, lower_bounds, hgrn_norm_g, dmix, oa_pre, states, n_seq, seq)
    dproj = [dhq, dhf, dhi, dhg, dsq, dsk, dsv]
    dw_in = _dw_slots(h1, dproj, "dw_in")
    dh1, (land_in,) = _dh_pieces(dproj, wf_in, [dw_in], "dh_mix")
    dx, d_mix_g = _rmsnorm_bwd(x2d, mix_norm_g, dh1, dx1, "norm_mix_bwd")

    tiles = {"in": 256, "out": 128, "gate": 256, "up": 256, "down": 176}
    big = [_reduce_adamw(land, w, m, v, "adamw_" + key, tr=tiles[key])
           for key, land, w, m, v in zip(tiles, [land_in] + lands, big_w, big_m, big_v)]
    small, loss_row = _small_step(
        [d_mix_g, d_lb, d_hgrn_g, d_sb_g, d_ffn_g, d_final_g], loss_part,
        [mix_norm_g, lower_bounds, hgrn_norm_g, sb_norm_g, ffn_norm_g, final_g],
        [m_mix_norm_g, m_lower_bounds, m_hgrn_norm_g, m_sb_norm_g, m_ffn_norm_g, m_final_norm_g.reshape(1, d)],
        [v_mix_norm_g, v_lower_bounds, v_hgrn_norm_g, v_sb_norm_g, v_ffn_norm_g, v_final_norm_g.reshape(1, d)])

    outs = [loss_row[0, 0], dx.reshape(n_seq, seq, d)]
    for q in range(4):
        b_in, b_out, b_gate, b_up, b_down = [res[q][None] for res in big]
        s_mix, s_lb, s_hgrn, s_sb, s_ffn, s_final = small[q]
        outs += [s_mix, b_in, s_lb, s_hgrn, s_sb, b_out, s_ffn, b_gate, b_up, b_down, s_final.reshape(d)]
    return tuple(outs)
```

```python
import functools
import math

import jax
import jax.numpy as jnp
from jax import lax
from jax.experimental import pallas as pl
from jax.experimental.pallas import tpu as pltpu

F32 = jnp.float32
BF16 = jnp.bfloat16
MXU_DTYPE = BF16

EPS = 1e-6
D_MODEL = 1024
N_HEADS = 8
D_HEAD = 64
GROUP = N_HEADS * D_HEAD
IN_COLS = 7 * GROUP
D_FF = 2816
CHUNK = 64
LANES = 128
N_PAIRS = GROUP // LANES
SUPER = 256
SB_BLOCK = 256
N_DEV = 8

ADAM_LR = 0.001
ADAM_B1 = 0.9
ADAM_B2 = 0.999
ADAM_EPS = 1e-08
ADAM_WD = 0.01
ADAM_STEP = 10

SMALL_ROWS = 8
FF_TILE = D_FF // 2

VMEM_LIMIT = 48 * 1024 * 1024


def _params(n_axes, vmem=VMEM_LIMIT):
    return pltpu.CompilerParams(dimension_semantics=("arbitrary",) * n_axes, vmem_limit_bytes=vmem)


def _dot(a, b):
    return jnp.dot(a.astype(MXU_DTYPE), b.astype(MXU_DTYPE), preferred_element_type=F32)


def _dot_nt(a, b):
    return lax.dot_general(a.astype(MXU_DTYPE), b.astype(MXU_DTYPE), (((1,), (1,)), ((), ())),
                           preferred_element_type=F32)


def _dot_tn(a, b):
    return lax.dot_general(a.astype(MXU_DTYPE), b.astype(MXU_DTYPE), (((0,), (0,)), ((), ())),
                           preferred_element_type=F32)


def _split(x, parts):
    out, r = [], x
    for _ in range(parts):
        h = r.astype(BF16)
        out.append(h)
        r = r - h.astype(F32)
    return out


def _rsum_right(x, u, parts):
    acc = None
    for h in _split(x, parts):
        d = jnp.dot(h, u, preferred_element_type=F32)
        acc = d if acc is None else acc + d
    return acc


def _rsum_left(u, x, parts):
    acc = None
    for h in _split(x, parts):
        d = jnp.dot(u, h, preferred_element_type=F32)
        acc = d if acc is None else acc + d
    return acc


def _ones_where(mask):
    return jnp.where(mask, 1.0, 0.0).astype(BF16)


def _sigmoid(x):
    return 1.0 / (1.0 + jnp.exp(-x))


def _softplus(x):
    return jnp.maximum(x, 0.0) + jnp.log(1.0 + jnp.exp(-jnp.abs(x)))


def _head_masks():
    lane = lax.broadcasted_iota(jnp.int32, (1, LANES), 1)
    return [jnp.where(lane < D_HEAD, 1.0, 0.0), jnp.where(lane >= D_HEAD, 1.0, 0.0)]


def _head_rstd(o, masks):
    sq = o * o
    r = None
    for m in masks:
        ms = jnp.sum(sq * m, axis=1, keepdims=True) * (1.0 / D_HEAD)
        t = lax.rsqrt(ms + EPS) * m
        r = t if r is None else r + t
    return r


def _head_mean(t, masks):
    out = None
    for m in masks:
        v = jnp.sum(t * m, axis=1, keepdims=True) * (1.0 / D_HEAD) * m
        out = v if out is None else out + v
    return out


def _rmsnorm_fwd(x, g, name):
    t, d = x.shape
    tm = min(512, t)

    def body(x_ref, g_ref, o_ref):
        xv = x_ref[...]
        r = lax.rsqrt(jnp.mean(xv * xv, axis=-1, keepdims=True) + EPS)
        o_ref[...] = (xv * r * g_ref[...]).astype(o_ref.dtype)

    return pl.pallas_call(
        body, name=name, grid=(t // tm,),
        in_specs=[pl.BlockSpec((tm, d), lambda i: (i, 0)), pl.BlockSpec((1, d), lambda i: (0, 0))],
        out_specs=pl.BlockSpec((tm, d), lambda i: (i, 0)),
        out_shape=jax.ShapeDtypeStruct((t, d), MXU_DTYPE),
        compiler_params=_params(1),
    )(x, g)


def _rmsnorm_bwd(x, g, dh, res, name):
    t, d = x.shape
    tm = min(512, t)

    def body(x_ref, g_ref, dh_ref, res_ref, dx_ref, dg_ref):
        xv = x_ref[...]
        r = lax.rsqrt(jnp.mean(xv * xv, axis=-1, keepdims=True) + EPS)
        xh = xv * r
        dhv = dh_ref[...]
        dxh = dhv * g_ref[...]
        dx_ref[...] = res_ref[...] + r * (dxh - xh * jnp.mean(dxh * xh, axis=-1, keepdims=True))

        @pl.when(pl.program_id(0) == 0)
        def _():
            dg_ref[...] = jnp.zeros_like(dg_ref)

        dg_ref[...] += jnp.sum(dhv * xh, axis=0, keepdims=True)

    row = pl.BlockSpec((tm, d), lambda i: (i, 0))
    vec = pl.BlockSpec((1, d), lambda i: (0, 0))
    return pl.pallas_call(
        body, name=name, grid=(t // tm,),
        in_specs=[row, vec, row, row], out_specs=[row, vec],
        out_shape=[jax.ShapeDtypeStruct((t, d), F32), jax.ShapeDtypeStruct((1, d), F32)],
        compiler_params=_params(1),
    )(x, g, dh, res)


def _final_loss(x, target, g, name):
    t, d = x.shape
    tm = min(512, t)

    def body(x_ref, t_ref, g_ref, dx_ref, dxm_ref, dg_ref, loss_ref):
        xv = x_ref[...]
        gv = g_ref[...]
        r = lax.rsqrt(jnp.mean(xv * xv, axis=-1, keepdims=True) + EPS)
        xh = xv * r
        e = xh * gv - t_ref[...]
        dy = e * (1.0 / d)
        dxh = dy * gv
        dxv = r * (dxh - xh * jnp.mean(dxh * xh, axis=-1, keepdims=True))
        dx_ref[...] = dxv
        dxm_ref[...] = dxv.astype(dxm_ref.dtype)

        @pl.when(pl.program_id(0) == 0)
        def _():
            dg_ref[...] = jnp.zeros_like(dg_ref)
            loss_ref[...] = jnp.zeros_like(loss_ref)

        dg_ref[...] += jnp.sum(dy * xh, axis=0, keepdims=True)
        part = 0.5 * jnp.sum(jnp.mean(e * e, axis=-1, keepdims=True), axis=0, keepdims=True)
        loss_ref[...] += jnp.broadcast_to(part, loss_ref.shape)

    row = pl.BlockSpec((tm, d), lambda i: (i, 0))
    vec = pl.BlockSpec((1, d), lambda i: (0, 0))
    return pl.pallas_call(
        body, name=name, grid=(t // tm,),
        in_specs=[row, row, vec],
        out_specs=[row, row, vec, pl.BlockSpec((1, LANES), lambda i: (0, 0))],
        out_shape=[jax.ShapeDtypeStruct((t, d), F32), jax.ShapeDtypeStruct((t, d), MXU_DTYPE),
                   jax.ShapeDtypeStruct((1, d), F32), jax.ShapeDtypeStruct((1, LANES), F32)],
        compiler_params=_params(1),
    )(x, target, g)


def _mm_nn(pairs, name, res=None, tm=512, tn=512):
    m = pairs[0][0].shape[0]
    n = pairs[0][1].shape[1]
    tm, tn = min(tm, m), min(tn, n)
    n_pairs = len(pairs)

    def body(*refs):
        o_ref = refs[-1]
        acc = None if res is None else refs[-2][...]
        for q in range(n_pairs):
            d = _dot(refs[2 * q][...], refs[2 * q + 1][...])
            acc = d if acc is None else acc + d
        o_ref[...] = acc

    in_specs, args = [], []
    for a, w, r in pairs:
        k = a.shape[1]
        in_specs += [pl.BlockSpec((tm, k), lambda j, i: (i, 0)),
                     pl.BlockSpec((k, tn), functools.partial(lambda j, i, r: (r, j), r=r or 0))]
        args += [a, w]
    if res is not None:
        in_specs.append(pl.BlockSpec((tm, tn), lambda j, i: (i, j)))
        args.append(res)
    return pl.pallas_call(
        body, name=name, grid=(n // tn, m // tm), in_specs=in_specs,
        out_specs=pl.BlockSpec((tm, tn), lambda j, i: (i, j)),
        out_shape=jax.ShapeDtypeStruct((m, n), F32),
        compiler_params=_params(2),
    )(*args)


def _mm_nt(pairs, name, tm=512, tk=512):
    m = pairs[0][0].shape[0]
    k = pairs[0][1].shape[0]
    tm, tk = min(tm, m), min(tk, k)
    n_pairs = len(pairs)

    def body(*refs):
        o_ref = refs[-1]
        acc = None
        for q in range(n_pairs):
            d = _dot_nt(refs[2 * q][...], refs[2 * q + 1][...])
            acc = d if acc is None else acc + d
        o_ref[...] = acc

    in_specs, args = [], []
    for a, w in pairs:
        n = a.shape[1]
        in_specs += [pl.BlockSpec((tm, n), lambda j, i: (i, 0)), pl.BlockSpec((tk, n), lambda j, i: (j, 0))]
        args += [a, w]
    return pl.pallas_call(
        body, name=name, grid=(k // tk, m // tm), in_specs=in_specs,
        out_specs=pl.BlockSpec((tm, tk), lambda j, i: (i, j)),
        out_shape=jax.ShapeDtypeStruct((m, k), F32),
        compiler_params=_params(2),
    )(*args)


def _mm_tn(a, b, name, tk, tn, tt=512, out_dtype=BF16):
    t, k = a.shape
    n = b.shape[1]
    tt = min(tt, t)
    steps = t // tt

    def body(a_ref, b_ref, o_ref, acc):
        s = pl.program_id(2)

        @pl.when(s == 0)
        def _():
            acc[...] = jnp.zeros_like(acc)

        acc[...] += _dot_tn(a_ref[...], b_ref[...])

        @pl.when(s == steps - 1)
        def _():
            o_ref[...] = acc[...].astype(o_ref.dtype)

    return pl.pallas_call(
        body, name=name, grid=(k // tk, n // tn, steps),
        in_specs=[pl.BlockSpec((tt, tk), lambda i, j, s: (s, i)), pl.BlockSpec((tt, tn), lambda i, j, s: (s, j))],
        out_specs=pl.BlockSpec((tk, tn), lambda i, j, s: (i, j)),
        out_shape=jax.ShapeDtypeStruct((k, n), out_dtype),
        scratch_shapes=[pltpu.VMEM((tk, tn), F32)],
        compiler_params=_params(3),
    )(a, b)


def _ffn_up(h, wg_t, wu_t, name, tm=512, tn=FF_TILE):
    m, k = h.shape
    n = wg_t.shape[0]
    tm = min(tm, m)

    def body(h_ref, wg_ref, wu_ref, gate_ref, up_ref, ff_ref):
        hv = h_ref[...]
        gate = _dot_nt(hv, wg_ref[...])
        up = _dot_nt(hv, wu_ref[...])
        gate_ref[...] = gate
        up_ref[...] = up
        ff_ref[...] = (gate * _sigmoid(gate) * up).astype(ff_ref.dtype)

    wspec = pl.BlockSpec((tn, k), lambda j, i: (j, 0))
    ospec = pl.BlockSpec((tm, tn), lambda j, i: (i, j))
    return pl.pallas_call(
        body, name=name, grid=(n // tn, m // tm),
        in_specs=[pl.BlockSpec((tm, k), lambda j, i: (i, 0)), wspec, wspec],
        out_specs=[ospec, ospec, ospec],
        out_shape=[jax.ShapeDtypeStruct((m, n), F32), jax.ShapeDtypeStruct((m, n), F32),
                   jax.ShapeDtypeStruct((m, n), MXU_DTYPE)],
        compiler_params=_params(2),
    )(h, wg_t, wu_t)


def _ffn_bwd_act(dx, wd, gate, up, name, tm=512, tn=FF_TILE):
    m, k = dx.shape
    n = wd.shape[0]
    tm = min(tm, m)

    def body(dx_ref, wd_ref, gate_ref, up_ref, dgate_ref, dup_ref):
        dff = _dot_nt(dx_ref[...], wd_ref[...])
        gate = gate_ref[...]
        sg = _sigmoid(gate)
        dgate_ref[...] = (dff * up_ref[...] * sg * (1.0 + gate * (1.0 - sg))).astype(dgate_ref.dtype)
        dup_ref[...] = (dff * gate * sg).astype(dup_ref.dtype)

    ospec = pl.BlockSpec((tm, tn), lambda j, i: (i, j))
    return pl.pallas_call(
        body, name=name, grid=(n // tn, m // tm),
        in_specs=[pl.BlockSpec((tm, k), lambda j, i: (i, 0)), pl.BlockSpec((tn, k), lambda j, i: (j, 0)),
                  ospec, ospec],
        out_specs=[ospec, ospec],
        out_shape=[jax.ShapeDtypeStruct((m, n), MXU_DTYPE), jax.ShapeDtypeStruct((m, n), MXU_DTYPE)],
        compiler_params=_params(2),
    )(dx, wd, gate, up)


def _chunk_masks():
    r = lax.broadcasted_iota(jnp.int32, (SUPER, SUPER), 0)
    c = lax.broadcasted_iota(jnp.int32, (SUPER, SUPER), 1)
    same = jnp.right_shift(r, 6) == jnp.right_shift(c, 6)
    lower = jnp.logical_and(same, c <= r)
    upper = jnp.logical_and(same, c >= r)
    return same, lower, upper


def _head_block_mask():
    r = lax.broadcasted_iota(jnp.int32, (LANES, LANES), 0)
    c = lax.broadcasted_iota(jnp.int32, (LANES, LANES), 1)
    return jnp.where(jnp.right_shift(r, 6) == jnp.right_shift(c, 6), 1.0, 0.0)


def _lower_bound(lb_raw):
    return 1.0 / (1.0 + jnp.exp(lb_raw[1:2, :] - lb_raw[0:1, :]))


def _hgrn_gates(q, hf, lb, same_b, lower_b):
    sig = _sigmoid(hf)
    f = lb + (1.0 - lb) * sig
    k = 1.0 - f
    lf = jnp.log(f)
    b = _rsum_left(lower_b, lf, 3)
    be = _rsum_left(same_b, lf, 3)
    eb = jnp.exp(b)
    enb = jnp.exp(-b)
    edb = jnp.exp(be - b)
    return sig, f, k, b, be, eb, enb, edb, q * eb, k * enb, k * edb


def _hgrn_fwd(proj, lower_bounds, norm_g, n_seq, seq):
    t = n_seq * seq
    n_super = seq // SUPER
    n_chunks = seq // CHUNK

    def body(q_ref, f_ref, i_ref, g_ref, lb_ref, ng_ref, out_ref, opre_ref, st_ref):
        masks = _head_masks()
        same, lower, _ = _chunk_masks()
        same_b, lower_b = _ones_where(same), _ones_where(lower)
        bd = _head_block_mask()
        lb = _lower_bound(lb_ref[...])
        ng = ng_ref[...]

        def step(sb, st):
            rows = pl.ds(pl.multiple_of(sb * SUPER, SUPER), SUPER)
            q, hf, v, hg = q_ref[rows, :], f_ref[rows, :], i_ref[rows, :], g_ref[rows, :]
            _, _, _, _, be, _, _, _, qe, ke, kd = _hgrn_gates(q, hf, lb, same_b, lower_b)
            o = None
            for m in masks:
                p = jnp.where(lower, _dot_nt(qe * m, ke), 0.0)
                d = _dot(p, v) * m
                o = d if o is None else o + d
            inter = []
            for c in range(SUPER // CHUNK):
                cr = slice(c * CHUNK, (c + 1) * CHUNK)
                st_ref[0, 0, sb * (SUPER // CHUNK) + c] = st
                inter.append(_dot_nt(qe[cr], st))
                dec = jnp.exp(be[c * CHUNK:c * CHUNK + 1, :])
                st = st * dec + bd * _dot_tn(v[cr], kd[cr])
            o = o + jnp.concatenate(inter, axis=0)
            opre_ref[rows, :] = o
            on = o * _head_rstd(o, masks) * ng
            out_ref[rows, :] = (on * hg * _sigmoid(hg)).astype(out_ref.dtype)
            return st

        lax.fori_loop(0, n_super, step, jnp.zeros((LANES, LANES), F32))

    def col(k):
        return pl.BlockSpec((seq, LANES), lambda p, b: (b, k * N_PAIRS + p))

    vec = lambda rows: pl.BlockSpec((rows, LANES), lambda p, b: (0, p))
    ospec = pl.BlockSpec((seq, LANES), lambda p, b: (b, p))
    return pl.pallas_call(
        body, name="hgrn_fwd", grid=(N_PAIRS, n_seq),
        in_specs=[col(0), col(1), col(2), col(3), vec(2), vec(1)],
        out_specs=[ospec, ospec,
                   pl.BlockSpec((1, 1, n_chunks, LANES, LANES), lambda p, b: (b, p, 0, 0, 0))],
        out_shape=[jax.ShapeDtypeStruct((t, 2 * GROUP), MXU_DTYPE), jax.ShapeDtypeStruct((t, GROUP), F32),
                   jax.ShapeDtypeStruct((n_seq, N_PAIRS, n_chunks, LANES, LANES), F32)],
        compiler_params=_params(2),
    )(proj, proj, proj, proj, lower_bounds, norm_g)


def _hgrn_bwd(proj, lower_bounds, norm_g, dmix, opre, states, n_seq, seq):
    t = n_seq * seq
    n_super = seq // SUPER
    n_chunks = seq // CHUNK
    per = SUPER // CHUNK

    def body(q_ref, f_ref, i_ref, g_ref, lb_ref, ng_ref, dm_ref, opre_ref, st_ref,
             dq_ref, df_ref, di_ref, dg_ref, dlb_ref, dng_ref):
        masks = _head_masks()
        same, lower, upper = _chunk_masks()
        same_b, lower_b, upper_b = _ones_where(same), _ones_where(lower), _ones_where(upper)
        bd = _head_block_mask()
        lb_raw = lb_ref[...]
        lb = _lower_bound(lb_raw)
        ng = ng_ref[...]
        row_id = lax.broadcasted_iota(jnp.int32, (SUPER, LANES), 0)

        @pl.when(pl.program_id(1) == 0)
        def _():
            dlb_ref[...] = jnp.zeros_like(dlb_ref)
            dng_ref[...] = jnp.zeros_like(dng_ref)

        def step(it, dst):
            sb = n_super - 1 - it
            rows = pl.ds(pl.multiple_of(sb * SUPER, SUPER), SUPER)
            q, hf, v, hg = q_ref[rows, :], f_ref[rows, :], i_ref[rows, :], g_ref[rows, :]
            sig, f, k, _, be, eb, enb, edb, qe, ke, kd = _hgrn_gates(q, hf, lb, same_b, lower_b)
            o = opre_ref[rows, :]
            r = _head_rstd(o, masks)
            oh = o * r
            dm = dm_ref[rows, :]
            sg = _sigmoid(hg)
            dg_ref[rows, :] = (dm * oh * ng * sg * (1.0 + hg * (1.0 - sg))).astype(dg_ref.dtype)
            don = dm * hg * sg
            dng_ref[...] += jnp.sum(don * oh, axis=0, keepdims=True)
            doh = don * ng
            do = r * (doh - oh * _head_mean(doh * oh, masks))
            dqe, dke, dv = None, None, None
            for m in masks:
                dom = do * m
                qem = qe * m
                p = jnp.where(lower, _dot_nt(qem, ke), 0.0)
                dp = jnp.where(lower, _dot_nt(dom, v), 0.0)
                a = _dot(dp, ke) * m
                bq = _dot_tn(dp, qem)
                cv = _dot_tn(p, dom)
                dqe = a if dqe is None else dqe + a
                dke = bq if dke is None else dke + bq
                dv = cv if dv is None else dv + cv
            dqe_i, dv_i, dkd_i, dbe_rows = [None] * per, [None] * per, [None] * per, None
            for c in reversed(range(per)):
                cr = slice(c * CHUNK, (c + 1) * CHUNK)
                st_prev = st_ref[0, 0, sb * per + c]
                dec = jnp.exp(be[c * CHUNK:c * CHUNK + 1, :])
                du = bd * dst
                dv_i[c] = _dot_nt(kd[cr], du)
                dkd_i[c] = _dot(v[cr], du)
                ddec = jnp.sum(dst * st_prev, axis=0, keepdims=True)
                dqe_i[c] = _dot(do[cr], st_prev)
                dst = bd * (dst * dec + _dot_tn(do[cr], qe[cr]))
                put = jnp.where(row_id == c * CHUNK, ddec * dec, 0.0)
                dbe_rows = put if dbe_rows is None else dbe_rows + put
            dqe = dqe + jnp.concatenate(dqe_i, axis=0)
            dv = dv + jnp.concatenate(dv_i, axis=0)
            dkd = jnp.concatenate(dkd_i, axis=0)
            dk = dke * enb + dkd * edb
            db = dqe * qe - dke * ke - dkd * kd
            dbe = dkd * kd + dbe_rows
            dlf = _rsum_left(upper_b, db, 3) + _rsum_left(same_b, dbe, 3)
            dfv = dlf / f - dk
            dq_ref[rows, :] = (dqe * eb).astype(dq_ref.dtype)
            di_ref[rows, :] = dv.astype(di_ref.dtype)
            df_ref[rows, :] = (dfv * (1.0 - lb) * sig * (1.0 - sig)).astype(df_ref.dtype)
            dlb = jnp.sum(dfv * (1.0 - sig), axis=0, keepdims=True)
            da0 = dlb * lb * (1.0 - lb)
            dlb_ref[0:1, :] += da0
            dlb_ref[1:2, :] -= da0
            return dst

        lax.fori_loop(0, n_super, step, jnp.zeros((LANES, LANES), F32))

    def col(k):
        return pl.BlockSpec((seq, LANES), lambda p, b: (b, k * N_PAIRS + p))

    vec = lambda rows: pl.BlockSpec((rows, LANES), lambda p, b: (0, p))
    ospec = pl.BlockSpec((seq, LANES), lambda p, b: (b, p))
    piece = jax.ShapeDtypeStruct((t, GROUP), MXU_DTYPE)
    return pl.pallas_call(
        body, name="hgrn_bwd", grid=(N_PAIRS, n_seq),
        in_specs=[col(0), col(1), col(2), col(3), vec(2), vec(1), ospec, ospec,
                  pl.BlockSpec((1, 1, n_chunks, LANES, LANES), lambda p, b: (b, p, 0, 0, 0))],
        out_specs=[ospec, ospec, ospec, ospec, vec(2), vec(1)],
        out_shape=[piece, piece, piece, piece,
                   jax.ShapeDtypeStruct((2, GROUP), F32), jax.ShapeDtypeStruct((1, GROUP), F32)],
        compiler_params=_params(2),
    )(proj, proj, proj, proj, lower_bounds, norm_g, dmix, opre, states)


SB_SCALE = 1.0 / math.sqrt(D_HEAD)


def _tile_masks():
    r = lax.broadcasted_iota(jnp.int32, (SB_BLOCK, SB_BLOCK), 0)
    c = lax.broadcasted_iota(jnp.int32, (SB_BLOCK, SB_BLOCK), 1)
    return r, c


def _sb_fwd(proj, norm_g, mixed, shards, n_seq, seq):
    t = n_seq * seq
    nq = seq // SB_BLOCK
    q0, k0, v0 = 4 * N_PAIRS, 5 * N_PAIRS, 6 * N_PAIRS
    n_w = len(shards)
    n_steps = N_PAIRS * n_seq * nq

    def body(q_ref, k_ref, v_ref, ng_ref, mixed_in, *rest):
        del mixed_in
        shard_refs = rest[:n_w]
        out_ref, opre_ref, ctot_ref = rest[n_w:n_w + 3]
        gathered = rest[n_w + 3:2 * n_w + 3]
        send_sems, recv_sems, local_sems = rest[2 * n_w + 3:]
        i = pl.program_id(2)
        step = (pl.program_id(0) * n_seq + pl.program_id(1)) * nq + i
        plan = _GatherPlan(shard_refs, gathered, send_sems, recv_sems, local_sems)

        @pl.when(step == 0)
        def _():
            plan.start()

        @pl.when(step == n_steps // 2)
        def _():
            plan.forward()

        masks = _head_masks()
        r, c = _tile_masks()
        strict = c < r
        neg_suffix = jnp.where(r >= c, -1.0, 0.0).astype(BF16)
        qs = q_ref[...] * SB_SCALE
        qhs = [(qs * m).astype(MXU_DTYPE) for m in masks]

        def tiles(js, carry, diag):
            rows = [pl.ds(pl.multiple_of(j * SB_BLOCK, SB_BLOCK), SB_BLOCK) for j in js]
            ks = [k_ref[rw, :].astype(MXU_DTYPE) for rw in rows]
            vs = [v_ref[rw, :].astype(MXU_DTYPE) for rw in rows]
            zs = [[_dot_nt(qh, kj) for qh in qhs] for kj in ks]
            ccs = [[_softplus(z) for z in zt] for zt in zs]
            if diag:
                ccs = [[jnp.where(strict, cc, 0.0) for cc in ct] for ct in ccs]
            logits = [[_dot_nt(qh, kj) + _rsum_right(cc, neg_suffix, 2) for qh, cc in zip(qhs, ct)]
                      for kj, ct in zip(ks, ccs)]
            out = []
            for h, (run, acc) in enumerate(carry):
                for t in range(len(js)):
                    a = jnp.exp(logits[t][h] - run)
                    if diag:
                        a = jnp.where(strict, a, 0.0)
                    acc = acc + _dot(a, vs[t])
                    run = run + jnp.sum(ccs[t][h], axis=1, keepdims=True)
                out.append((run, acc))
            return tuple(out)

        start = ((jnp.zeros((SB_BLOCK, 1), F32), jnp.zeros((SB_BLOCK, LANES), F32)),) * 2
        carry = tiles([i], start, True)
        carry = lax.fori_loop(0, i // 2, lambda s, cy: tiles([i - 1 - 2 * s, i - 2 - 2 * s], cy, False), carry)
        carry = lax.fori_loop(0, i % 2, lambda s, cy: tiles([0], cy, False), carry)
        opre = carry[0][1] * masks[0] + carry[1][1] * masks[1]
        ctot = carry[0][0] * masks[0] + carry[1][0] * masks[1]
        opre_ref[...] = opre
        ctot_ref[...] = ctot
        out_ref[...] = (opre * _head_rstd(opre, masks) * ng_ref[...]).astype(out_ref.dtype)

        @pl.when(step == n_steps - 1)
        def _():
            plan.finish()

    qspec = pl.BlockSpec((SB_BLOCK, LANES), lambda p, b, i: (b * nq + i, q0 + p))
    ospec = pl.BlockSpec((SB_BLOCK, LANES), lambda p, b, i: (b * nq + i, p))
    hbm = pl.BlockSpec(memory_space=pltpu.HBM)
    outs = pl.pallas_call(
        body, name="sb_fwd", grid=(N_PAIRS, n_seq, nq),
        in_specs=[qspec,
                  pl.BlockSpec((seq, LANES), lambda p, b, i: (b, k0 + p)),
                  pl.BlockSpec((seq, LANES), lambda p, b, i: (b, v0 + p)),
                  pl.BlockSpec((1, LANES), lambda p, b, i: (0, p)), hbm] + [hbm] * n_w,
        out_specs=[pl.BlockSpec((SB_BLOCK, LANES), lambda p, b, i: (b * nq + i, N_PAIRS + p)), ospec, ospec]
        + [hbm] * n_w,
        out_shape=[jax.ShapeDtypeStruct(mixed.shape, mixed.dtype), jax.ShapeDtypeStruct((t, GROUP), F32),
                   jax.ShapeDtypeStruct((t, GROUP), F32)]
        + [jax.ShapeDtypeStruct((N_DEV,) + s.shape, s.dtype) for s in shards],
        scratch_shapes=[pltpu.SemaphoreType.DMA((n_w * _GatherPlan.COPIES,)),
                        pltpu.SemaphoreType.DMA((n_w * _GatherPlan.COPIES,)), pltpu.SemaphoreType.DMA((n_w,))],
        input_output_aliases={4: 0},
        compiler_params=_params(3),
    )(proj, proj, proj, norm_g, mixed, *shards)
    return outs[0], outs[1], outs[2], list(outs[3:])


def _sb_bwd(proj, norm_g, dmix, opre, ctot, grads, n_seq, seq):
    t = n_seq * seq
    nq = seq // SB_BLOCK
    q0, k0, v0 = 4 * N_PAIRS, 5 * N_PAIRS, 6 * N_PAIRS
    n_w = len(grads)
    n_steps = N_PAIRS * n_seq * nq

    def body(q_ref, k_ref, v_ref, ng_ref, dm_ref, opre_ref, ctot_ref, *rest):
        grad_refs = rest[:n_w]
        dq_ref, dk_ref, dv_ref, dng_ref = rest[n_w:n_w + 4]
        lands = rest[n_w + 4:2 * n_w + 4]
        dk_acc, dv_acc, send_sems, recv_sems, local_sems = rest[2 * n_w + 4:]
        p_id, b_id, i = pl.program_id(0), pl.program_id(1), pl.program_id(2)
        step = (p_id * n_seq + b_id) * nq + i
        plan = _ScatterPlan(grad_refs, lands, send_sems, recv_sems, local_sems)

        @pl.when(step == 0)
        def _():
            plan.start()

        masks = _head_masks()
        r, c = _tile_masks()
        strict = c < r
        before = _ones_where(r < c)
        upto = _ones_where(r <= c)

        @pl.when(i == 0)
        def _():
            dk_acc[...] = jnp.zeros_like(dk_acc)
            dv_acc[...] = jnp.zeros_like(dv_acc)

        @pl.when(jnp.logical_and(b_id == 0, i == 0))
        def _():
            dng_ref[...] = jnp.zeros_like(dng_ref)

        o = opre_ref[...]
        rs = _head_rstd(o, masks)
        oh = o * rs
        dm = dm_ref[...]
        dng_ref[...] += jnp.sum(dm * oh, axis=0, keepdims=True)
        doh = dm * ng_ref[...]
        do = rs * (doh - oh * _head_mean(doh * oh, masks))

        qs = q_ref[...] * SB_SCALE
        ct = ctot_ref[...]
        qhs = [(qs * m).astype(MXU_DTYPE) for m in masks]
        doms = [(do * m).astype(MXU_DTYPE) for m in masks]
        totals = [jnp.max(ct * m, axis=1, keepdims=True) for m in masks]
        heads = range(len(masks))

        def tiles(js, carry, diag):
            nt = range(len(js))
            rows = [pl.ds(pl.multiple_of(j * SB_BLOCK, SB_BLOCK), SB_BLOCK) for j in js]
            ks = [k_ref[rw, :].astype(MXU_DTYPE) for rw in rows]
            vs = [v_ref[rw, :].astype(MXU_DTYPE) for rw in rows]
            zs = [[_dot_nt(qhs[h], ks[t]) for h in heads] for t in nt]
            das = [[_dot_nt(doms[h], vs[t]) for h in heads] for t in nt]
            sps = [[_softplus(z) for z in zt] for zt in zs]
            sigs = [[jnp.exp(z - sp) for z, sp in zip(zt, st)] for zt, st in zip(zs, sps)]
            ccs = [[jnp.where(strict, sp, 0.0) for sp in st] for st in sps] if diag else sps
            logits = [[zs[t][h] + _rsum_right(ccs[t][h], before, 2) for h in heads] for t in nt]
            out = []
            for h in heads:
                pc, pdl, dq_h = carry[h]
                for t in nt:
                    a = jnp.exp(logits[t][h] - (totals[h] - pc))
                    if diag:
                        a = jnp.where(strict, a, 0.0)
                    dl = a * das[t][h]
                    dv_acc[rows[t], :] += _dot_tn(a, doms[h])
                    dz = dl - sigs[t][h] * (pdl + _rsum_right(dl, upto, 2))
                    if diag:
                        dz = jnp.where(strict, dz, 0.0)
                    dzb = dz.astype(MXU_DTYPE)
                    dq_h = dq_h + _dot(dzb, ks[t])
                    dk_acc[rows[t], :] += _dot_tn(dzb, qhs[h])
                    pc = pc + jnp.sum(ccs[t][h], axis=1, keepdims=True)
                    pdl = pdl + jnp.sum(dl, axis=1, keepdims=True)
                out.append((pc, pdl, dq_h))
            return tuple(out)

        zero = jnp.zeros((SB_BLOCK, 1), F32)
        carry = ((zero, zero, jnp.zeros((SB_BLOCK, LANES), F32)),) * 2
        carry = lax.fori_loop(0, i // 2, lambda s, cy: tiles([2 * s, 2 * s + 1], cy, False), carry)
        carry = lax.fori_loop(0, i % 2, lambda s, cy: tiles([i - 1], cy, False), carry)
        carry = tiles([i], carry, True)
        dq = carry[0][2] * masks[0] + carry[1][2] * masks[1]
        dq_ref[...] = (dq * SB_SCALE).astype(dq_ref.dtype)

        @pl.when(i == nq - 1)
        def _():
            dk_ref[...] = dk_acc[...].astype(dk_ref.dtype)
            dv_ref[...] = dv_acc[...].astype(dv_ref.dtype)

        @pl.when(step == n_steps - 1)
        def _():
            plan.finish()

    qspec = pl.BlockSpec((SB_BLOCK, LANES), lambda p, b, i: (b * nq + i, q0 + p))
    ospec = pl.BlockSpec((SB_BLOCK, LANES), lambda p, b, i: (b * nq + i, p))
    dmspec = pl.BlockSpec((SB_BLOCK, LANES), lambda p, b, i: (b * nq + i, N_PAIRS + p))
    full = lambda k: pl.BlockSpec((seq, LANES), lambda p, b, i: (b, k + p))
    vec = pl.BlockSpec((1, LANES), lambda p, b, i: (0, p))
    hbm = pl.BlockSpec(memory_space=pltpu.HBM)
    piece = jax.ShapeDtypeStruct((t, GROUP), MXU_DTYPE)
    outs = pl.pallas_call(
        body, name="sb_bwd", grid=(N_PAIRS, n_seq, nq),
        in_specs=[qspec, full(k0), full(v0), vec, dmspec, ospec, ospec] + [hbm] * n_w,
        out_specs=[ospec, full(0), full(0), vec] + [hbm] * n_w,
        out_shape=[piece, piece, piece, jax.ShapeDtypeStruct((1, GROUP), F32)]
        + [jax.ShapeDtypeStruct(g.shape, g.dtype) for g in grads],
        scratch_shapes=[pltpu.VMEM((seq, LANES), F32), pltpu.VMEM((seq, LANES), F32),
                        pltpu.SemaphoreType.DMA((n_w * (N_DEV - 1),)), pltpu.SemaphoreType.DMA((n_w * (N_DEV - 1),)),
                        pltpu.SemaphoreType.DMA((n_w,))],
        compiler_params=_params(3),
    )(proj, proj, proj, norm_g, dmix, opre, ctot, *grads)
    return outs[0], outs[1], outs[2], outs[3], list(outs[4:])


def _mesh_place():
    x, y, c = lax.axis_index("x"), lax.axis_index("y"), lax.axis_index("c")
    return x, y, c


def _peer(x, y, c, k):
    px = lax.rem(x + ((k >> 2) & 1), 2)
    py = lax.rem(y + ((k >> 1) & 1), 2)
    pc = lax.rem(c + (k & 1), 2)
    return (px, py, pc), 4 * px + 2 * py + pc


def _remote(src, dst, send_sem, recv_sem, to):
    return pltpu.make_async_remote_copy(src_ref=src, dst_ref=dst, send_sem=send_sem, recv_sem=recv_sem,
                                        device_id=to, device_id_type=pl.DeviceIdType.MESH)


class _GatherPlan:
    COPIES = 7

    def __init__(self, shards, gathered, send_sems, recv_sems, local_sems):
        x, y, c = _mesh_place()
        self.c = c
        self.me = (x, y, c)
        self.sibling = (x, y, 1 - c)
        self.chips = [(1 - x, y), (x, 1 - y), (1 - x, 1 - y)]
        self.tensors = list(zip(shards, gathered))
        self.send_sems, self.recv_sems, self.local_sems = send_sems, recv_sems, local_sems

    @staticmethod
    def _index(place):
        return 4 * place[0] + 2 * place[1] + place[2]

    def _copy(self, w, k, block, to, own=False):
        shard, gathered = self.tensors[w]
        slot = gathered.at[self._index(block)]
        n = w * self.COPIES + k
        return _remote(shard if own else slot, slot, self.send_sems.at[n], self.recv_sems.at[n], to)

    def _local(self, w):
        shard, gathered = self.tensors[w]
        return pltpu.make_async_copy(shard, gathered.at[self._index(self.me)], self.local_sems.at[w])

    def _first(self, w):
        return [self._copy(w, 0, self.me, self.sibling, own=True)] + [
            self._copy(w, 1 + j, self.me, (*chip, self.c), own=True) for j, chip in enumerate(self.chips)]

    def _passed(self, w):
        return [self._copy(w, 4 + j, (*chip, self.c), self.sibling) for j, chip in enumerate(self.chips)]

    def start(self):
        for w in range(len(self.tensors)):
            self._local(w).start()
            for cp in self._first(w):
                cp.start()

    def forward(self):
        for w in range(len(self.tensors)):
            passed = self._passed(w)
            for j, chip in enumerate(self.chips):
                self._copy(w, 1 + j, (*chip, self.c), self.me).wait_recv()
                passed[j].start()

    def finish(self):
        for w in range(len(self.tensors)):
            self._copy(w, 0, self.sibling, self.me).wait_recv()
            for j, chip in enumerate(self.chips):
                self._copy(w, 4 + j, (*chip, 1 - self.c), self.me).wait_recv()
            for cp in self._first(w) + self._passed(w):
                cp.wait_send()
            self._local(w).wait()


class _ScatterPlan:
    def __init__(self, grads, lands, send_sems, recv_sems, local_sems):
        self.place = _mesh_place()
        x, y, c = self.place
        self.me = 4 * x + 2 * y + c
        self.tensors = list(zip(grads, lands))
        self.send_sems, self.recv_sems, self.local_sems = send_sems, recv_sems, local_sems

    def _copies(self, w):
        grad, land = self.tensors[w]
        out = []
        for k in range(1, N_DEV):
            peer, pidx = _peer(*self.place, k)
            n = w * (N_DEV - 1) + k - 1
            sems = (self.send_sems.at[n], self.recv_sems.at[n], peer)
            out.append((_remote(grad.at[pidx], land.at[self.me], *sems), _remote(grad.at[pidx], land.at[pidx], *sems)))
        return out

    def _local(self, w):
        grad, land = self.tensors[w]
        return pltpu.make_async_copy(grad.at[self.me], land.at[self.me], self.local_sems.at[w])

    def start(self):
        for w in range(len(self.tensors)):
            self._local(w).start()
            for send, _ in self._copies(w):
                send.start()

    def finish(self):
        for w in range(len(self.tensors)):
            copies = self._copies(w)
            for _, arrival in copies:
                arrival.wait_recv()
            for send, _ in copies:
                send.wait_send()
            self._local(w).wait()


def _cast_shards(shards):
    def body(*refs):
        n = len(refs) // 2
        for src, dst in zip(refs[:n], refs[n:]):
            dst[...] = src[...].astype(dst.dtype)

    vmem = pl.BlockSpec(memory_space=pltpu.VMEM)
    return pl.pallas_call(
        body, name="cast_shards", in_specs=[vmem] * len(shards), out_specs=[vmem] * len(shards),
        out_shape=[jax.ShapeDtypeStruct(s.shape, BF16) for s in shards],
        compiler_params=pltpu.CompilerParams(vmem_limit_bytes=VMEM_LIMIT),
    )(*shards)


def _gather_w_in(shard):
    rows, cols = shard.shape

    def body(w_ref, out_ref, send_sems, recv_sems, local_sems):
        plan = _GatherPlan([w_ref], [out_ref], send_sems, recv_sems, local_sems)
        plan.start()
        plan.forward()
        plan.finish()

    vmem = pl.BlockSpec(memory_space=pltpu.VMEM)
    return pl.pallas_call(
        body, name="gather_w_in", in_specs=[vmem], out_specs=vmem,
        out_shape=jax.ShapeDtypeStruct((N_DEV, rows, cols), shard.dtype),
        scratch_shapes=[pltpu.SemaphoreType.DMA((_GatherPlan.COPIES,)), pltpu.SemaphoreType.DMA((_GatherPlan.COPIES,)),
                        pltpu.SemaphoreType.DMA((1,))],
        compiler_params=pltpu.CompilerParams(vmem_limit_bytes=VMEM_LIMIT),
    )(shard)


def _dw_rows(pieces, b, name, tn=512, tt=512):
    t, n = b.shape
    widths = [p.shape[1] for p in pieces]
    rows = sum(widths)
    tt = min(tt, t)
    steps = t // tt
    n_p = len(pieces)

    def body(*refs):
        piece_refs, b_ref, o_ref, acc = refs[:n_p], refs[n_p], refs[n_p + 1], refs[n_p + 2]
        s = pl.program_id(1)

        @pl.when(s == 0)
        def _():
            acc[...] = jnp.zeros_like(acc)

        bv = b_ref[...]
        off = 0
        for p_ref, width in zip(piece_refs, widths):
            acc[off:off + width, :] += _dot_tn(p_ref[...], bv)
            off += width

        @pl.when(s == steps - 1)
        def _():
            o_ref[...] = acc[...].astype(o_ref.dtype)

    return pl.pallas_call(
        body, name=name, grid=(n // tn, steps),
        in_specs=[pl.BlockSpec((tt, width), lambda j, s: (s, 0)) for width in widths]
        + [pl.BlockSpec((tt, tn), lambda j, s: (s, j))],
        out_specs=pl.BlockSpec((rows, tn), lambda j, s: (0, j)),
        out_shape=jax.ShapeDtypeStruct((rows, n), BF16),
        scratch_shapes=[pltpu.VMEM((rows, tn), F32)],
        compiler_params=_params(2),
    )(*pieces, b)


def _dh_pieces(pieces, w, grads, name, tm=256, tk=512):
    m = pieces[0].shape[0]
    k = w.shape[1]
    tm, tk = min(tm, m), min(tk, k)
    widths = [p.shape[1] for p in pieces]
    n_p, n_w = len(pieces), len(grads)
    n_steps = (k // tk) * (m // tm)

    def body(*refs):
        piece_refs, w_refs = refs[:n_p], refs[n_p:2 * n_p]
        grad_refs = refs[2 * n_p:2 * n_p + n_w]
        o_ref = refs[2 * n_p + n_w]
        lands = refs[2 * n_p + n_w + 1:2 * n_p + 2 * n_w + 1]
        send_sems, recv_sems, local_sems = refs[2 * n_p + 2 * n_w + 1:]
        step = pl.program_id(0) * (m // tm) + pl.program_id(1)
        plan = _ScatterPlan(grad_refs, lands, send_sems, recv_sems, local_sems)

        @pl.when(step == 0)
        def _():
            plan.start()

        acc = None
        for p_ref, w_ref in zip(piece_refs, w_refs):
            d = _dot(p_ref[...], w_ref[...])
            acc = d if acc is None else acc + d
        o_ref[...] = acc

        @pl.when(step == n_steps - 1)
        def _():
            plan.finish()

    offs = [sum(widths[:q]) // widths[q] for q in range(n_p)]
    hbm = pl.BlockSpec(memory_space=pltpu.HBM)
    in_specs = [pl.BlockSpec((tm, width), lambda j, i: (i, 0)) for width in widths]
    in_specs += [pl.BlockSpec((width, tk), functools.partial(lambda j, i, o: (o, j), o=o))
                 for width, o in zip(widths, offs)]
    outs = pl.pallas_call(
        body, name=name, grid=(k // tk, m // tm), in_specs=in_specs + [hbm] * n_w,
        out_specs=[pl.BlockSpec((tm, tk), lambda j, i: (i, j))] + [hbm] * n_w,
        out_shape=[jax.ShapeDtypeStruct((m, k), F32)] + [jax.ShapeDtypeStruct(g.shape, g.dtype) for g in grads],
        scratch_shapes=[pltpu.SemaphoreType.DMA((n_w * (N_DEV - 1),)), pltpu.SemaphoreType.DMA((n_w * (N_DEV - 1),)),
                        pltpu.SemaphoreType.DMA((n_w,))],
        compiler_params=_params(2),
    )(*pieces, *([w] * n_p), *grads)
    return outs[0], list(outs[1:])


SMALL_LAYOUT = ((0, 0, 0, 0, D_MODEL), (1, 0, 1, 0, GROUP), (1, 1, 1, GROUP, GROUP), (2, 0, 2, 0, GROUP),
                (3, 0, 2, GROUP, GROUP), (4, 0, 3, 0, D_MODEL), (5, 0, 4, 0, D_MODEL))
LOSS_ROW = 5
N_SMALL = 6


def _small_step(grads, loss_part, ws, ms, vs):
    def body(*refs):
        g_in, loss_in = refs[:N_SMALL], refs[N_SMALL]
        params = [refs[1 + (q + 1) * N_SMALL:1 + (q + 2) * N_SMALL] for q in range(3)]
        o0 = 1 + 4 * N_SMALL
        outs = [refs[o0 + q * N_SMALL:o0 + (q + 1) * N_SMALL] for q in range(4)]
        loss_out = refs[o0 + 4 * N_SMALL]
        pack, land, wp, mp, vp, send_sems, recv_sems = refs[o0 + 4 * N_SMALL + 1:]

        def place(dst, srcs):
            dst[...] = jnp.zeros_like(dst)
            for p, sr, dr, dc, width in SMALL_LAYOUT:
                dst[dr:dr + 1, dc:dc + width] = srcs[p][sr:sr + 1, :]

        place(pack, g_in)
        pack[LOSS_ROW:LOSS_ROW + 1, 0:LANES] = loss_in[...]
        for dst, srcs in zip((wp, mp, vp), params):
            place(dst, srcs)

        x, y, c = _mesh_place()
        me = 4 * x + 2 * y + c
        land[me] = pack[...]
        sends = []
        for k in range(1, N_DEV):
            peer, _ = _peer(x, y, c, k)
            cp = _remote(pack, land.at[me], send_sems.at[k - 1], recv_sems.at[k - 1], peer)
            cp.start()
            sends.append(cp)
        for k in range(1, N_DEV):
            peer, pidx = _peer(x, y, c, k)
            _remote(pack, land.at[pidx], send_sems.at[k - 1], recv_sems.at[k - 1], peer).wait_recv()
        for cp in sends:
            cp.wait_send()

        g = land[0]
        for d in range(1, N_DEV):
            g = g + land[d]
        delta, nm, nv = _adam(wp[...], g, mp[...], vp[...])
        for val, out in zip((g, delta, nm, nv), outs):
            for p, sr, dr, dc, width in SMALL_LAYOUT:
                out[p][sr:sr + 1, :] = val[dr:dr + 1, dc:dc + width]
        loss_out[...] = g[LOSS_ROW:LOSS_ROW + 1, 0:LANES]

    vmem = pl.BlockSpec(memory_space=pltpu.VMEM)
    n_in = 1 + 4 * N_SMALL
    shapes = [jax.ShapeDtypeStruct(w.shape, F32) for w in ws]
    packed = pltpu.VMEM((SMALL_ROWS, D_MODEL), F32)
    outs = pl.pallas_call(
        body, name="small_step", in_specs=[vmem] * n_in, out_specs=[vmem] * (4 * N_SMALL + 1),
        out_shape=shapes * 4 + [jax.ShapeDtypeStruct((1, LANES), F32)],
        scratch_shapes=[packed, pltpu.VMEM((N_DEV, SMALL_ROWS, D_MODEL), F32), packed, packed, packed,
                        pltpu.SemaphoreType.DMA((N_DEV - 1,)), pltpu.SemaphoreType.DMA((N_DEV - 1,))],
    )(*grads, loss_part, *ws, *ms, *vs)
    return [outs[q * N_SMALL:(q + 1) * N_SMALL] for q in range(4)], outs[4 * N_SMALL]


def _adam(w, g, m, v):
    m = ADAM_B1 * m + (1.0 - ADAM_B1) * g
    v = ADAM_B2 * v + (1.0 - ADAM_B2) * (g * g)
    m_hat = m / (1.0 - ADAM_B1 ** ADAM_STEP)
    v_hat = v / (1.0 - ADAM_B2 ** ADAM_STEP)
    delta = -ADAM_LR * (m_hat / (jnp.sqrt(v_hat) + ADAM_EPS) + ADAM_WD * w)
    return delta, m, v


def _reduce_adamw(land, w, m, v, name, tr):
    _, rows, width = land.shape

    def body(land_ref, w_ref, m_ref, v_ref, g_ref, d_ref, nm_ref, nv_ref):
        g = land_ref[0].astype(F32)
        for d in range(1, N_DEV):
            g = g + land_ref[d].astype(F32)
        delta, nm, nv = _adam(w_ref[...], g, m_ref[...], v_ref[...])
        g_ref[...] = g
        d_ref[...] = delta
        nm_ref[...] = nm
        nv_ref[...] = nv

    row = pl.BlockSpec((tr, width), lambda i: (i, 0))
    out = jax.ShapeDtypeStruct((rows, width), F32)
    return pl.pallas_call(
        body, name=name, grid=(rows // tr,),
        in_specs=[pl.BlockSpec((N_DEV, tr, width), lambda i: (0, i, 0)), row, row, row],
        out_specs=[row, row, row, row], out_shape=[out, out, out, out],
        compiler_params=_params(1),
    )(land, w, m, v)


def kernel(x, mix_norm_g, w_in, lower_bounds, hgrn_norm_g, sb_norm_g, w_out, ffn_norm_g, w_gate, w_up, w_down, final_norm_g, loss_target, m_mix_norm_g, m_w_in, m_lower_bounds, m_hgrn_norm_g, m_sb_norm_g, m_w_out, m_ffn_norm_g, m_w_gate, m_w_up, m_w_down, m_final_norm_g, v_mix_norm_g, v_w_in, v_lower_bounds, v_hgrn_norm_g, v_sb_norm_g, v_w_out, v_ffn_norm_g, v_w_gate, v_w_up, v_w_down, v_final_norm_g):
    n_seq, seq, d = x.shape
    t = n_seq * seq
    x2d = x.reshape(t, d)
    tgt = loss_target.reshape(t, d)
    final_g = final_norm_g.reshape(1, d)
    col_sharded = (True, False, True, True, False)

    def as_rows(ws):
        return [w[0].T if tr else w[0] for w, tr in zip(ws, col_sharded)]

    big_w = as_rows([w_in, w_out, w_gate, w_up, w_down])
    big_m = as_rows([m_w_in, m_w_out, m_w_gate, m_w_up, m_w_down])
    big_v = as_rows([v_w_in, v_w_out, v_w_gate, v_w_up, v_w_down])

    sh_in, sh_out, sh_gate, sh_up, sh_down = _cast_shards(big_w)
    wt_in = _gather_w_in(sh_in).reshape(IN_COLS, d)
    h1 = _rmsnorm_fwd(x2d, mix_norm_g, "norm_mix")
    proj = _mm_nt([(h1, wt_in)], "proj", tk=896)
    mixed, oa_pre, states = _hgrn_fwd(proj, lower_bounds, hgrn_norm_g, n_seq, seq)
    mixed, ob_pre, ctot, gathered = _sb_fwd(proj, sb_norm_g, mixed, [sh_out, sh_gate, sh_up, sh_down], n_seq, seq)
    wf_out = gathered[0].reshape(d, d)
    wt_gate = gathered[1].reshape(D_FF, d)
    wt_up = gathered[2].reshape(D_FF, d)
    wf_down = gathered[3].reshape(D_FF, d)
    x1 = _mm_nn([(mixed, wf_out, None)], "mix_out", res=x2d)
    h2 = _rmsnorm_fwd(x1, ffn_norm_g, "norm_ffn")
    gate, up, ff = _ffn_up(h2, wt_gate, wt_up, "ffn_up")
    x2 = _mm_nn([(ff, wf_down, None)], "ffn_down", res=x1)
    dx2, dx2m, d_final_g, loss_part = _final_loss(x2, tgt, final_g, "loss_head")

    dgate, dup = _ffn_bwd_act(dx2m, wf_down, gate, up, "ffn_bwd_act")
    dw_down = _mm_tn(ff, dx2m, "dw_down", tk=1408, tn=1024).reshape(N_DEV, D_FF // N_DEV, d)
    dh2 = _mm_nn([(dgate, wt_gate, None), (dup, wt_up, None)], "dh_ffn", tm=256)
    dw_gate = _mm_tn(dgate, h2, "dw_gate", tk=1408, tn=1024).reshape(N_DEV, D_FF // N_DEV, d)
    dw_up = _mm_tn(dup, h2, "dw_up", tk=1408, tn=1024).reshape(N_DEV, D_FF // N_DEV, d)
    dx1, d_ffn_g = _rmsnorm_bwd(x1, ffn_norm_g, dh2, dx2, "norm_ffn_bwd")
    dmix = _mm_nt([(dx1, wf_out)], "dmix")
    dw_out = _mm_tn(mixed, dx1, "dw_out", tk=512, tn=1024).reshape(N_DEV, d // N_DEV, d)
    dsq, dsk, dsv, d_sb_g, lands = _sb_bwd(proj, sb_norm_g, dmix, ob_pre, ctot, [dw_out, dw_gate, dw_up, dw_down],
                                            n_seq, seq)
    dhq, dhf, dhi, dhg, d_lb, d_hgrn_g = _hgrn_bwd(proj, lower_bounds, hgrn_norm_g, dmix, oa_pre, states, n_seq, seq)
    dproj = [dhq, dhf, dhi, dhg, dsq, dsk, dsv]
    dw_in = _dw_rows(dproj, h1, "dw_in").reshape(N_DEV, IN_COLS // N_DEV, d)
    dh1, (land_in,) = _dh_pieces(dproj, wt_in, [dw_in], "dh_mix")
    dx, d_mix_g = _rmsnorm_bwd(x2d, mix_norm_g, dh1, dx1, "norm_mix_bwd")

    tiles = {"in": 224, "out": 128, "gate": 176, "up": 176, "down": 176}
    big = [_reduce_adamw(land, w, m, v, "adamw_" + key, tr=tiles[key])
           for key, land, w, m, v in zip(tiles, [land_in] + lands, big_w, big_m, big_v)]
    big = [[r.T if tr else r for r in res] for res, tr in zip(big, col_sharded)]
    small, loss_row = _small_step(
        [d_mix_g, d_lb, d_hgrn_g, d_sb_g, d_ffn_g, d_final_g], loss_part,
        [mix_norm_g, lower_bounds, hgrn_norm_g, sb_norm_g, ffn_norm_g, final_g],
        [m_mix_norm_g, m_lower_bounds, m_hgrn_norm_g, m_sb_norm_g, m_ffn_norm_g, m_final_norm_g.reshape(1, d)],
        [v_mix_norm_g, v_lower_bounds, v_hgrn_norm_g, v_sb_norm_g, v_ffn_norm_g, v_final_norm_g.reshape(1, d)])

    outs = [loss_row[0, 0], dx.reshape(n_seq, seq, d)]
    for q in range(4):
        b_in, b_out, b_gate, b_up, b_down = [res[q][None] for res in big]
        s_mix, s_lb, s_hgrn, s_sb, s_ffn, s_final = small[q]
        outs += [s_mix, b_in, s_lb, s_hgrn, s_sb, b_out, s_ffn, b_gate, b_up, b_down, s_final.reshape(d)]
    return tuple(outs)
```

```python
import functools
import math

import jax
import jax.numpy as jnp
from jax import lax
from jax.experimental import pallas as pl
from jax.experimental.pallas import tpu as pltpu

F32 = jnp.float32
BF16 = jnp.bfloat16
MXU_DTYPE = BF16

EPS = 1e-6
D_MODEL = 1024
N_HEADS = 8
D_HEAD = 64
GROUP = N_HEADS * D_HEAD
IN_COLS = 7 * GROUP
D_FF = 2816
CHUNK = 64
LANES = 128
N_PAIRS = GROUP // LANES
SUPER = 256
SB_BLOCK = 256
N_DEV = 8

ADAM_LR = 0.001
ADAM_B1 = 0.9
ADAM_B2 = 0.999
ADAM_EPS = 1e-08
ADAM_WD = 0.01
ADAM_STEP = 10

SMALL_ROWS = 8
FF_TILE = D_FF // 2

VMEM_LIMIT = 48 * 1024 * 1024


def _params(n_axes, vmem=VMEM_LIMIT):
    return pltpu.CompilerParams(dimension_semantics=("arbitrary",) * n_axes, vmem_limit_bytes=vmem)


def _dot(a, b):
    return jnp.dot(a.astype(MXU_DTYPE), b.astype(MXU_DTYPE), preferred_element_type=F32)


def _dot_nt(a, b):
    return lax.dot_general(a.astype(MXU_DTYPE), b.astype(MXU_DTYPE), (((1,), (1,)), ((), ())),
                           preferred_element_type=F32)


def _dot_tn(a, b):
    return lax.dot_general(a.astype(MXU_DTYPE), b.astype(MXU_DTYPE), (((0,), (0,)), ((), ())),
                           preferred_element_type=F32)


def _split(x, parts):
    out, r = [], x
    for _ in range(parts):
        h = r.astype(BF16)
        out.append(h)
        r = r - h.astype(F32)
    return out


def _rsum_right(x, u, parts):
    acc = None
    for h in _split(x, parts):
        d = jnp.dot(h, u, preferred_element_type=F32)
        acc = d if acc is None else acc + d
    return acc


def _rsum_left(u, x, parts):
    acc = None
    for h in _split(x, parts):
        d = jnp.dot(u, h, preferred_element_type=F32)
        acc = d if acc is None else acc + d
    return acc


def _ones_where(mask):
    return jnp.where(mask, 1.0, 0.0).astype(BF16)


def _sigmoid(x):
    return 1.0 / (1.0 + jnp.exp(-x))


def _softplus(x):
    return jnp.maximum(x, 0.0) + jnp.log(1.0 + jnp.exp(-jnp.abs(x)))


def _head_masks():
    lane = lax.broadcasted_iota(jnp.int32, (1, LANES), 1)
    return [jnp.where(lane < D_HEAD, 1.0, 0.0), jnp.where(lane >= D_HEAD, 1.0, 0.0)]


def _head_rstd(o, masks):
    sq = o * o
    r = None
    for m in masks:
        ms = jnp.sum(sq * m, axis=1, keepdims=True) * (1.0 / D_HEAD)
        t = lax.rsqrt(ms + EPS) * m
        r = t if r is None else r + t
    return r


def _head_mean(t, masks):
    out = None
    for m in masks:
        v = jnp.sum(t * m, axis=1, keepdims=True) * (1.0 / D_HEAD) * m
        out = v if out is None else out + v
    return out


def _rmsnorm_fwd(x, g, name):
    t, d = x.shape
    tm = min(512, t)

    def body(x_ref, g_ref, o_ref):
        xv = x_ref[...]
        r = lax.rsqrt(jnp.mean(xv * xv, axis=-1, keepdims=True) + EPS)
        o_ref[...] = (xv * r * g_ref[...]).astype(o_ref.dtype)

    return pl.pallas_call(
        body, name=name, grid=(t // tm,),
        in_specs=[pl.BlockSpec((tm, d), lambda i: (i, 0)), pl.BlockSpec((1, d), lambda i: (0, 0))],
        out_specs=pl.BlockSpec((tm, d), lambda i: (i, 0)),
        out_shape=jax.ShapeDtypeStruct((t, d), MXU_DTYPE),
        compiler_params=_params(1),
    )(x, g)


def _final_loss(x, target, g, name):
    t, d = x.shape
    tm = min(512, t)

    def body(x_ref, t_ref, g_ref, dx_ref, dxm_ref, dg_ref, loss_ref):
        xv = x_ref[...]
        gv = g_ref[...]
        r = lax.rsqrt(jnp.mean(xv * xv, axis=-1, keepdims=True) + EPS)
        xh = xv * r
        e = xh * gv - t_ref[...]
        dy = e * (1.0 / d)
        dxh = dy * gv
        dxv = r * (dxh - xh * jnp.mean(dxh * xh, axis=-1, keepdims=True))
        dx_ref[...] = dxv
        dxm_ref[...] = dxv.astype(dxm_ref.dtype)

        @pl.when(pl.program_id(0) == 0)
        def _():
            dg_ref[...] = jnp.zeros_like(dg_ref)
            loss_ref[...] = jnp.zeros_like(loss_ref)

        dg_ref[...] += jnp.sum(dy * xh, axis=0, keepdims=True)
        part = 0.5 * jnp.sum(jnp.mean(e * e, axis=-1, keepdims=True), axis=0, keepdims=True)
        loss_ref[...] += jnp.broadcast_to(part, loss_ref.shape)

    row = pl.BlockSpec((tm, d), lambda i: (i, 0))
    vec = pl.BlockSpec((1, d), lambda i: (0, 0))
    return pl.pallas_call(
        body, name=name, grid=(t // tm,),
        in_specs=[row, row, vec],
        out_specs=[row, row, vec, pl.BlockSpec((1, LANES), lambda i: (0, 0))],
        out_shape=[jax.ShapeDtypeStruct((t, d), F32), jax.ShapeDtypeStruct((t, d), MXU_DTYPE),
                   jax.ShapeDtypeStruct((1, d), F32), jax.ShapeDtypeStruct((1, LANES), F32)],
        compiler_params=_params(1),
    )(x, target, g)


def _mm_nn(pairs, name, res=None, tm=512, tn=512):
    m = pairs[0][0].shape[0]
    n = pairs[0][1].shape[1]
    tm, tn = min(tm, m), min(tn, n)
    n_pairs = len(pairs)

    def body(*refs):
        o_ref = refs[-1]
        acc = None if res is None else refs[-2][...]
        for q in range(n_pairs):
            d = _dot(refs[2 * q][...], refs[2 * q + 1][...])
            acc = d if acc is None else acc + d
        o_ref[...] = acc

    in_specs, args = [], []
    for a, w, r in pairs:
        k = a.shape[1]
        in_specs += [pl.BlockSpec((tm, k), lambda j, i: (i, 0)),
                     pl.BlockSpec((k, tn), functools.partial(lambda j, i, r: (r, j), r=r or 0))]
        args += [a, w]
    if res is not None:
        in_specs.append(pl.BlockSpec((tm, tn), lambda j, i: (i, j)))
        args.append(res)
    return pl.pallas_call(
        body, name=name, grid=(n // tn, m // tm), in_specs=in_specs,
        out_specs=pl.BlockSpec((tm, tn), lambda j, i: (i, j)),
        out_shape=jax.ShapeDtypeStruct((m, n), F32),
        compiler_params=_params(2),
    )(*args)


def _mm_nt(pairs, name, tm=512, tk=512):
    m = pairs[0][0].shape[0]
    k = pairs[0][1].shape[0]
    tm, tk = min(tm, m), min(tk, k)
    n_pairs = len(pairs)

    def body(*refs):
        o_ref = refs[-1]
        acc = None
        for q in range(n_pairs):
            d = _dot_nt(refs[2 * q][...], refs[2 * q + 1][...])
            acc = d if acc is None else acc + d
        o_ref[...] = acc

    in_specs, args = [], []
    for a, w in pairs:
        n = a.shape[1]
        in_specs += [pl.BlockSpec((tm, n), lambda j, i: (i, 0)), pl.BlockSpec((tk, n), lambda j, i: (j, 0))]
        args += [a, w]
    return pl.pallas_call(
        body, name=name, grid=(k // tk, m // tm), in_specs=in_specs,
        out_specs=pl.BlockSpec((tm, tk), lambda j, i: (i, j)),
        out_shape=jax.ShapeDtypeStruct((m, k), F32),
        compiler_params=_params(2),
    )(*args)


def _mm_tn(a, b, name, tk, tn, tt=512, out_dtype=BF16):
    t, k = a.shape
    n = b.shape[1]
    tt = min(tt, t)
    steps = t // tt

    def body(a_ref, b_ref, o_ref, acc):
        s = pl.program_id(2)

        @pl.when(s == 0)
        def _():
            acc[...] = jnp.zeros_like(acc)

        acc[...] += _dot_tn(a_ref[...], b_ref[...])

        @pl.when(s == steps - 1)
        def _():
            o_ref[...] = acc[...].astype(o_ref.dtype)

    return pl.pallas_call(
        body, name=name, grid=(k // tk, n // tn, steps),
        in_specs=[pl.BlockSpec((tt, tk), lambda i, j, s: (s, i)), pl.BlockSpec((tt, tn), lambda i, j, s: (s, j))],
        out_specs=pl.BlockSpec((tk, tn), lambda i, j, s: (i, j)),
        out_shape=jax.ShapeDtypeStruct((k, n), out_dtype),
        scratch_shapes=[pltpu.VMEM((tk, tn), F32)],
        compiler_params=_params(3),
    )(a, b)


def _ffn_up(h, wg_t, wu_t, name, tm=512, tn=FF_TILE):
    m, k = h.shape
    n = wg_t.shape[0]
    tm = min(tm, m)

    def body(h_ref, wg_ref, wu_ref, gate_ref, up_ref, ff_ref):
        hv = h_ref[...]
        gate = _dot_nt(hv, wg_ref[...])
        up = _dot_nt(hv, wu_ref[...])
        gate_ref[...] = gate
        up_ref[...] = up
        ff_ref[...] = (gate * _sigmoid(gate) * up).astype(ff_ref.dtype)

    wspec = pl.BlockSpec((tn, k), lambda j, i: (j, 0))
    ospec = pl.BlockSpec((tm, tn), lambda j, i: (i, j))
    return pl.pallas_call(
        body, name=name, grid=(n // tn, m // tm),
        in_specs=[pl.BlockSpec((tm, k), lambda j, i: (i, 0)), wspec, wspec],
        out_specs=[ospec, ospec, ospec],
        out_shape=[jax.ShapeDtypeStruct((m, n), F32), jax.ShapeDtypeStruct((m, n), F32),
                   jax.ShapeDtypeStruct((m, n), MXU_DTYPE)],
        compiler_params=_params(2),
    )(h, wg_t, wu_t)


def _ffn_bwd_act(dx, wd, gate, up, name, tm=512, tn=FF_TILE):
    m, k = dx.shape
    n = wd.shape[0]
    tm = min(tm, m)

    def body(dx_ref, wd_ref, gate_ref, up_ref, dgate_ref, dup_ref):
        dff = _dot_nt(dx_ref[...], wd_ref[...])
        gate = gate_ref[...]
        sg = _sigmoid(gate)
        dgate_ref[...] = (dff * up_ref[...] * sg * (1.0 + gate * (1.0 - sg))).astype(dgate_ref.dtype)
        dup_ref[...] = (dff * gate * sg).astype(dup_ref.dtype)

    ospec = pl.BlockSpec((tm, tn), lambda j, i: (i, j))
    return pl.pallas_call(
        body, name=name, grid=(n // tn, m // tm),
        in_specs=[pl.BlockSpec((tm, k), lambda j, i: (i, 0)), pl.BlockSpec((tn, k), lambda j, i: (j, 0)),
                  ospec, ospec],
        out_specs=[ospec, ospec],
        out_shape=[jax.ShapeDtypeStruct((m, n), MXU_DTYPE), jax.ShapeDtypeStruct((m, n), MXU_DTYPE)],
        compiler_params=_params(2),
    )(dx, wd, gate, up)


def _chunk_masks():
    r = lax.broadcasted_iota(jnp.int32, (SUPER, SUPER), 0)
    c = lax.broadcasted_iota(jnp.int32, (SUPER, SUPER), 1)
    same = jnp.right_shift(r, 6) == jnp.right_shift(c, 6)
    lower = jnp.logical_and(same, c <= r)
    upper = jnp.logical_and(same, c >= r)
    return same, lower, upper


def _head_block_mask():
    r = lax.broadcasted_iota(jnp.int32, (LANES, LANES), 0)
    c = lax.broadcasted_iota(jnp.int32, (LANES, LANES), 1)
    return jnp.where(jnp.right_shift(r, 6) == jnp.right_shift(c, 6), 1.0, 0.0)


def _lower_bound(lb_raw):
    return 1.0 / (1.0 + jnp.exp(lb_raw[1:2, :] - lb_raw[0:1, :]))


PER_SUPER = SUPER // CHUNK
CHUNK_ROWS = [slice(c * CHUNK, (c + 1) * CHUNK) for c in range(PER_SUPER)]


def _over_chunks(rows):
    return jnp.concatenate([jnp.broadcast_to(r, (CHUNK, LANES)) for r in rows], axis=0)


def _hgrn_gates(q, hf, lb, lower_b):
    sig = _sigmoid(hf)
    f = lb + (1.0 - lb) * sig
    k = 1.0 - f
    lf = jnp.log(f)
    b = _rsum_left(lower_b, lf, 2)
    ends = [b[cr.stop - 1:cr.stop, :] for cr in CHUNK_ROWS]
    eb = jnp.exp(b)
    enb = jnp.exp(-b)
    edb = jnp.exp(_over_chunks(ends) - b)
    decs = [jnp.exp(e) for e in ends]
    return sig, f, k, decs, eb, enb, edb, q * eb, k * enb, k * edb


def _hgrn_fwd(proj, lower_bounds, norm_g, n_seq, seq):
    t = n_seq * seq
    n_super = seq // SUPER
    n_chunks = seq // CHUNK

    def body(q_ref, f_ref, i_ref, g_ref, lb_ref, ng_ref, out_ref, opre_ref, st_ref):
        masks = _head_masks()
        _, lower, _ = _chunk_masks()
        lower_b = _ones_where(lower)
        bd = _head_block_mask()
        lb = _lower_bound(lb_ref[...])
        ng = ng_ref[...]

        def step(sb, st):
            rows = pl.ds(pl.multiple_of(sb * SUPER, SUPER), SUPER)
            q, hf, v, hg = q_ref[rows, :], f_ref[rows, :], i_ref[rows, :], g_ref[rows, :]
            _, _, _, decs, _, _, _, qe, ke, kd = _hgrn_gates(q, hf, lb, lower_b)
            scores = [_dot_nt(qe * m, ke) for m in masks]
            updates = [_dot_tn(v[cr], kd[cr]) for cr in CHUNK_ROWS]
            states = [st]
            for dec, upd in zip(decs, updates):
                states.append(states[-1] * dec + bd * upd)
            for c in range(PER_SUPER):
                st_ref[0, 0, sb * PER_SUPER + c] = states[c]
            intra = [_dot(jnp.where(lower, p, 0.0), v) for p in scores]
            inter = [_dot_nt(qe[cr], s) for cr, s in zip(CHUNK_ROWS, states)]
            o = intra[0] * masks[0] + intra[1] * masks[1] + jnp.concatenate(inter, axis=0)
            opre_ref[rows, :] = o
            on = o * _head_rstd(o, masks) * ng
            out_ref[rows, :] = (on * hg * _sigmoid(hg)).astype(out_ref.dtype)
            return states[-1]

        lax.fori_loop(0, n_super, step, jnp.zeros((LANES, LANES), F32))

    def col(k):
        return pl.BlockSpec((seq, LANES), lambda p, b: (b, k * N_PAIRS + p))

    vec = lambda rows: pl.BlockSpec((rows, LANES), lambda p, b: (0, p))
    ospec = pl.BlockSpec((seq, LANES), lambda p, b: (b, p))
    return pl.pallas_call(
        body, name="hgrn_fwd", grid=(N_PAIRS, n_seq),
        in_specs=[col(0), col(1), col(2), col(3), vec(2), vec(1)],
        out_specs=[ospec, ospec,
                   pl.BlockSpec((1, 1, n_chunks, LANES, LANES), lambda p, b: (b, p, 0, 0, 0))],
        out_shape=[jax.ShapeDtypeStruct((t, 2 * GROUP), MXU_DTYPE), jax.ShapeDtypeStruct((t, GROUP), F32),
                   jax.ShapeDtypeStruct((n_seq, N_PAIRS, n_chunks, LANES, LANES), F32)],
        compiler_params=_params(2),
    )(proj, proj, proj, proj, lower_bounds, norm_g)


def _hgrn_bwd(proj, lower_bounds, norm_g, dmix, opre, states, n_seq, seq):
    t = n_seq * seq
    n_super = seq // SUPER
    n_chunks = seq // CHUNK
    per = SUPER // CHUNK

    def body(q_ref, f_ref, i_ref, g_ref, lb_ref, ng_ref, dm_ref, opre_ref, st_ref,
             dq_ref, df_ref, di_ref, dg_ref, dlb_ref, dng_ref):
        masks = _head_masks()
        _, lower, upper = _chunk_masks()
        lower_b, upper_b = _ones_where(lower), _ones_where(upper)
        bd = _head_block_mask()
        lb_raw = lb_ref[...]
        lb = _lower_bound(lb_raw)
        ng = ng_ref[...]

        @pl.when(pl.program_id(1) == 0)
        def _():
            dlb_ref[...] = jnp.zeros_like(dlb_ref)
            dng_ref[...] = jnp.zeros_like(dng_ref)

        def step(it, dst):
            sb = n_super - 1 - it
            rows = pl.ds(pl.multiple_of(sb * SUPER, SUPER), SUPER)
            q, hf, v, hg = q_ref[rows, :], f_ref[rows, :], i_ref[rows, :], g_ref[rows, :]
            sig, f, k, decs, eb, enb, edb, qe, ke, kd = _hgrn_gates(q, hf, lb, lower_b)
            o = opre_ref[rows, :]
            r = _head_rstd(o, masks)
            oh = o * r
            dm = dm_ref[rows, :]
            sg = _sigmoid(hg)
            dg_ref[rows, :] = (dm * oh * ng * sg * (1.0 + hg * (1.0 - sg))).astype(dg_ref.dtype)
            don = dm * hg * sg
            dng_ref[...] += jnp.sum(don * oh, axis=0, keepdims=True)
            doh = don * ng
            do = r * (doh - oh * _head_mean(doh * oh, masks))
            doms = [do * m for m in masks]
            qems = [qe * m for m in masks]
            scores = [_dot_nt(qem, ke) for qem in qems]
            dscores = [_dot_nt(dom, v) for dom in doms]
            prevs = [st_ref[0, 0, sb * per + c] for c in range(per)]
            dst_in = [_dot_tn(do[cr], qe[cr]) for cr in CHUNK_ROWS]
            dqe_i = [_dot(do[cr], prev) for cr, prev in zip(CHUNK_ROWS, prevs)]
            dsts = [None] * per
            for c in reversed(range(per)):
                dsts[c] = dst
                dst = bd * (dst * decs[c] + dst_in[c])
            ps = [jnp.where(lower, p, 0.0) for p in scores]
            dps = [jnp.where(lower, dp, 0.0) for dp in dscores]
            dqe_h = [_dot(dp, ke) for dp in dps]
            dke_h = [_dot_tn(dp, qem) for dp, qem in zip(dps, qems)]
            dv_h = [_dot_tn(p, dom) for p, dom in zip(ps, doms)]
            dus = [bd * d for d in dsts]
            dv_i = [_dot_nt(kd[cr], du) for cr, du in zip(CHUNK_ROWS, dus)]
            dkd_i = [_dot(v[cr], du) for cr, du in zip(CHUNK_ROWS, dus)]
            dqe = dqe_h[0] * masks[0] + dqe_h[1] * masks[1] + jnp.concatenate(dqe_i, axis=0)
            dke = dke_h[0] + dke_h[1]
            dv = dv_h[0] + dv_h[1] + jnp.concatenate(dv_i, axis=0)
            dkd = jnp.concatenate(dkd_i, axis=0)
            dk = dke * enb + dkd * edb
            db = dqe * qe - dke * ke - dkd * kd
            dkd_kd = dkd * kd
            dends = [jnp.sum(dkd_kd[cr], axis=0, keepdims=True)
                     + jnp.sum(dsts[c] * prevs[c], axis=0, keepdims=True) * decs[c]
                     for c, cr in enumerate(CHUNK_ROWS)]
            dlf = _rsum_left(upper_b, db, 2) + _over_chunks(dends)
            dfv = dlf / f - dk
            dq_ref[rows, :] = (dqe * eb).astype(dq_ref.dtype)
            di_ref[rows, :] = dv.astype(di_ref.dtype)
            df_ref[rows, :] = (dfv * (1.0 - lb) * sig * (1.0 - sig)).astype(df_ref.dtype)
            dlb = jnp.sum(dfv * (1.0 - sig), axis=0, keepdims=True)
            da0 = dlb * lb * (1.0 - lb)
            dlb_ref[0:1, :] += da0
            dlb_ref[1:2, :] -= da0
            return dst

        lax.fori_loop(0, n_super, step, jnp.zeros((LANES, LANES), F32))

    def col(k):
        return pl.BlockSpec((seq, LANES), lambda p, b: (b, k * N_PAIRS + p))

    vec = lambda rows: pl.BlockSpec((rows, LANES), lambda p, b: (0, p))
    ospec = pl.BlockSpec((seq, LANES), lambda p, b: (b, p))
    piece = jax.ShapeDtypeStruct((t, GROUP), MXU_DTYPE)
    return pl.pallas_call(
        body, name="hgrn_bwd", grid=(N_PAIRS, n_seq),
        in_specs=[col(0), col(1), col(2), col(3), vec(2), vec(1), ospec, ospec,
                  pl.BlockSpec((1, 1, n_chunks, LANES, LANES), lambda p, b: (b, p, 0, 0, 0))],
        out_specs=[ospec, ospec, ospec, ospec, vec(2), vec(1)],
        out_shape=[piece, piece, piece, piece,
                   jax.ShapeDtypeStruct((2, GROUP), F32), jax.ShapeDtypeStruct((1, GROUP), F32)],
        compiler_params=_params(2),
    )(proj, proj, proj, proj, lower_bounds, norm_g, dmix, opre, states)


SB_SCALE = 1.0 / math.sqrt(D_HEAD)


def _tile_masks():
    r = lax.broadcasted_iota(jnp.int32, (SB_BLOCK, SB_BLOCK), 0)
    c = lax.broadcasted_iota(jnp.int32, (SB_BLOCK, SB_BLOCK), 1)
    return r, c


def _sb_fwd(proj, norm_g, mixed, shards, n_seq, seq):
    t = n_seq * seq
    nq = seq // SB_BLOCK
    q0, k0, v0 = 4 * N_PAIRS, 5 * N_PAIRS, 6 * N_PAIRS
    n_w = len(shards)
    n_steps = N_PAIRS * n_seq * nq

    def body(q_ref, k_ref, v_ref, ng_ref, mixed_in, *rest):
        del mixed_in
        shard_refs = rest[:n_w]
        out_ref, opre_ref, ctot_ref = rest[n_w:n_w + 3]
        gathered = rest[n_w + 3:2 * n_w + 3]
        send_sems, recv_sems, local_sems = rest[2 * n_w + 3:]
        i = pl.program_id(2)
        step = (pl.program_id(0) * n_seq + pl.program_id(1)) * nq + i
        plan = _GatherPlan(shard_refs, gathered, send_sems, recv_sems, local_sems)

        @pl.when(step == 0)
        def _():
            plan.start()

        @pl.when(step == n_steps // 2)
        def _():
            plan.forward()

        masks = _head_masks()
        r, c = _tile_masks()
        strict = c < r
        neg_suffix = jnp.where(r >= c, -1.0, 0.0).astype(BF16)
        qs = q_ref[...] * SB_SCALE
        qhs = [(qs * m).astype(MXU_DTYPE) for m in masks]

        def tiles(js, carry, diag):
            rows = [pl.ds(pl.multiple_of(j * SB_BLOCK, SB_BLOCK), SB_BLOCK) for j in js]
            ks = [k_ref[rw, :].astype(MXU_DTYPE) for rw in rows]
            vs = [v_ref[rw, :].astype(MXU_DTYPE) for rw in rows]
            zs = [[_dot_nt(qh, kj) for qh in qhs] for kj in ks]
            ccs = [[_softplus(z) for z in zt] for zt in zs]
            if diag:
                ccs = [[jnp.where(strict, cc, 0.0) for cc in ct] for ct in ccs]
            logits = [[_dot_nt(qh, kj) + _rsum_right(cc, neg_suffix, 2) for qh, cc in zip(qhs, ct)]
                      for kj, ct in zip(ks, ccs)]
            out = []
            for h, (run, acc) in enumerate(carry):
                for t in range(len(js)):
                    a = jnp.exp(logits[t][h] - run)
                    if diag:
                        a = jnp.where(strict, a, 0.0)
                    acc = acc + _dot(a, vs[t])
                    run = run + jnp.sum(ccs[t][h], axis=1, keepdims=True)
                out.append((run, acc))
            return tuple(out)

        start = ((jnp.zeros((SB_BLOCK, 1), F32), jnp.zeros((SB_BLOCK, LANES), F32)),) * 2
        carry = tiles([i], start, True)
        carry = lax.fori_loop(0, i // 2, lambda s, cy: tiles([i - 1 - 2 * s, i - 2 - 2 * s], cy, False), carry)
        carry = lax.fori_loop(0, i % 2, lambda s, cy: tiles([0], cy, False), carry)
        opre = carry[0][1] * masks[0] + carry[1][1] * masks[1]
        ctot = carry[0][0] * masks[0] + carry[1][0] * masks[1]
        opre_ref[...] = opre
        ctot_ref[...] = ctot
        out_ref[...] = (opre * _head_rstd(opre, masks) * ng_ref[...]).astype(out_ref.dtype)

        @pl.when(step == n_steps - 1)
        def _():
            plan.finish()

    qspec = pl.BlockSpec((SB_BLOCK, LANES), lambda p, b, i: (b * nq + i, q0 + p))
    ospec = pl.BlockSpec((SB_BLOCK, LANES), lambda p, b, i: (b * nq + i, p))
    hbm = pl.BlockSpec(memory_space=pltpu.HBM)
    outs = pl.pallas_call(
        body, name="sb_fwd", grid=(N_PAIRS, n_seq, nq),
        in_specs=[qspec,
                  pl.BlockSpec((seq, LANES), lambda p, b, i: (b, k0 + p)),
                  pl.BlockSpec((seq, LANES), lambda p, b, i: (b, v0 + p)),
                  pl.BlockSpec((1, LANES), lambda p, b, i: (0, p)), hbm] + [hbm] * n_w,
        out_specs=[pl.BlockSpec((SB_BLOCK, LANES), lambda p, b, i: (b * nq + i, N_PAIRS + p)), ospec, ospec]
        + [hbm] * n_w,
        out_shape=[jax.ShapeDtypeStruct(mixed.shape, mixed.dtype), jax.ShapeDtypeStruct((t, GROUP), F32),
                   jax.ShapeDtypeStruct((t, GROUP), F32)]
        + [jax.ShapeDtypeStruct((N_DEV,) + s.shape, s.dtype) for s in shards],
        scratch_shapes=[pltpu.SemaphoreType.DMA((n_w * _GatherPlan.COPIES,)),
                        pltpu.SemaphoreType.DMA((n_w * _GatherPlan.COPIES,)), pltpu.SemaphoreType.DMA((n_w,))],
        input_output_aliases={4: 0},
        compiler_params=_params(3),
    )(proj, proj, proj, norm_g, mixed, *shards)
    return outs[0], outs[1], outs[2], list(outs[3:])


def _sb_bwd(proj, norm_g, dmix, opre, ctot, grads, n_seq, seq):
    t = n_seq * seq
    nq = seq // SB_BLOCK
    q0, k0, v0 = 4 * N_PAIRS, 5 * N_PAIRS, 6 * N_PAIRS
    n_w = len(grads)
    n_steps = N_PAIRS * n_seq * nq

    def body(q_ref, k_ref, v_ref, ng_ref, dm_ref, opre_ref, ctot_ref, *rest):
        grad_refs = rest[:n_w]
        dq_ref, dk_ref, dv_ref, dng_ref = rest[n_w:n_w + 4]
        lands = rest[n_w + 4:2 * n_w + 4]
        dk_acc, dv_acc, send_sems, recv_sems, local_sems = rest[2 * n_w + 4:]
        p_id, b_id, i = pl.program_id(0), pl.program_id(1), pl.program_id(2)
        step = (p_id * n_seq + b_id) * nq + i
        plan = _ScatterPlan(grad_refs, lands, send_sems, recv_sems, local_sems)

        @pl.when(step == 0)
        def _():
            plan.start()

        masks = _head_masks()
        r, c = _tile_masks()
        strict = c < r
        before = _ones_where(r < c)
        upto = _ones_where(r <= c)

        @pl.when(i == 0)
        def _():
            dk_acc[...] = jnp.zeros_like(dk_acc)
            dv_acc[...] = jnp.zeros_like(dv_acc)

        @pl.when(jnp.logical_and(b_id == 0, i == 0))
        def _():
            dng_ref[...] = jnp.zeros_like(dng_ref)

        o = opre_ref[...]
        rs = _head_rstd(o, masks)
        oh = o * rs
        dm = dm_ref[...]
        dng_ref[...] += jnp.sum(dm * oh, axis=0, keepdims=True)
        doh = dm * ng_ref[...]
        do = rs * (doh - oh * _head_mean(doh * oh, masks))

        qs = q_ref[...] * SB_SCALE
        ct = ctot_ref[...]
        qhs = [(qs * m).astype(MXU_DTYPE) for m in masks]
        doms = [(do * m).astype(MXU_DTYPE) for m in masks]
        totals = [jnp.max(ct * m, axis=1, keepdims=True) for m in masks]
        heads = range(len(masks))

        def tiles(js, carry, diag):
            nt = range(len(js))
            rows = [pl.ds(pl.multiple_of(j * SB_BLOCK, SB_BLOCK), SB_BLOCK) for j in js]
            ks = [k_ref[rw, :].astype(MXU_DTYPE) for rw in rows]
            vs = [v_ref[rw, :].astype(MXU_DTYPE) for rw in rows]
            zs = [[_dot_nt(qhs[h], ks[t]) for h in heads] for t in nt]
            das = [[_dot_nt(doms[h], vs[t]) for h in heads] for t in nt]
            sps = [[_softplus(z) for z in zt] for zt in zs]
            sigs = [[jnp.exp(z - sp) for z, sp in zip(zt, st)] for zt, st in zip(zs, sps)]
            ccs = [[jnp.where(strict, sp, 0.0) for sp in st] for st in sps] if diag else sps
            logits = [[zs[t][h] + _rsum_right(ccs[t][h], before, 2) for h in heads] for t in nt]
            out = []
            for h in heads:
                pc, pdl, dq_h = carry[h]
                for t in nt:
                    a = jnp.exp(logits[t][h] - (totals[h] - pc))
                    if diag:
                        a = jnp.where(strict, a, 0.0)
                    dl = a * das[t][h]
                    dv_acc[rows[t], :] += _dot_tn(a, doms[h])
                    dz = dl - sigs[t][h] * (pdl + _rsum_right(dl, upto, 2))
                    if diag:
                        dz = jnp.where(strict, dz, 0.0)
                    dzb = dz.astype(MXU_DTYPE)
                    dq_h = dq_h + _dot(dzb, ks[t])
                    dk_acc[rows[t], :] += _dot_tn(dzb, qhs[h])
                    pc = pc + jnp.sum(ccs[t][h], axis=1, keepdims=True)
                    pdl = pdl + jnp.sum(dl, axis=1, keepdims=True)
                out.append((pc, pdl, dq_h))
            return tuple(out)

        zero = jnp.zeros((SB_BLOCK, 1), F32)
        carry = ((zero, zero, jnp.zeros((SB_BLOCK, LANES), F32)),) * 2
        carry = lax.fori_loop(0, i // 2, lambda s, cy: tiles([2 * s, 2 * s + 1], cy, False), carry)
        carry = lax.fori_loop(0, i % 2, lambda s, cy: tiles([i - 1], cy, False), carry)
        carry = tiles([i], carry, True)
        dq = carry[0][2] * masks[0] + carry[1][2] * masks[1]
        dq_ref[...] = (dq * SB_SCALE).astype(dq_ref.dtype)

        @pl.when(i == nq - 1)
        def _():
            dk_ref[...] = dk_acc[...].astype(dk_ref.dtype)
            dv_ref[...] = dv_acc[...].astype(dv_ref.dtype)

        @pl.when(step == n_steps - 1)
        def _():
            plan.finish()

    qspec = pl.BlockSpec((SB_BLOCK, LANES), lambda p, b, i: (b * nq + i, q0 + p))
    ospec = pl.BlockSpec((SB_BLOCK, LANES), lambda p, b, i: (b * nq + i, p))
    dmspec = pl.BlockSpec((SB_BLOCK, LANES), lambda p, b, i: (b * nq + i, N_PAIRS + p))
    full = lambda k: pl.BlockSpec((seq, LANES), lambda p, b, i: (b, k + p))
    vec = pl.BlockSpec((1, LANES), lambda p, b, i: (0, p))
    hbm = pl.BlockSpec(memory_space=pltpu.HBM)
    piece = jax.ShapeDtypeStruct((t, GROUP), MXU_DTYPE)
    outs = pl.pallas_call(
        body, name="sb_bwd", grid=(N_PAIRS, n_seq, nq),
        in_specs=[qspec, full(k0), full(v0), vec, dmspec, ospec, ospec] + [hbm] * n_w,
        out_specs=[ospec, full(0), full(0), vec] + [hbm] * n_w,
        out_shape=[piece, piece, piece, jax.ShapeDtypeStruct((1, GROUP), F32)]
        + [jax.ShapeDtypeStruct(g.shape, g.dtype) for g in grads],
        scratch_shapes=[pltpu.VMEM((seq, LANES), F32), pltpu.VMEM((seq, LANES), F32),
                        pltpu.SemaphoreType.DMA((n_w * (N_DEV - 1),)), pltpu.SemaphoreType.DMA((n_w * (N_DEV - 1),)),
                        pltpu.SemaphoreType.DMA((n_w,))],
        compiler_params=_params(3),
    )(proj, proj, proj, norm_g, dmix, opre, ctot, *grads)
    return outs[0], outs[1], outs[2], outs[3], list(outs[4:])


def _mesh_place():
    x, y, c = lax.axis_index("x"), lax.axis_index("y"), lax.axis_index("c")
    return x, y, c


def _peer(x, y, c, k):
    px = lax.rem(x + ((k >> 2) & 1), 2)
    py = lax.rem(y + ((k >> 1) & 1), 2)
    pc = lax.rem(c + (k & 1), 2)
    return (px, py, pc), 4 * px + 2 * py + pc


def _remote(src, dst, send_sem, recv_sem, to):
    return pltpu.make_async_remote_copy(src_ref=src, dst_ref=dst, send_sem=send_sem, recv_sem=recv_sem,
                                        device_id=to, device_id_type=pl.DeviceIdType.MESH)


class _GatherPlan:
    COPIES = 7

    def __init__(self, shards, gathered, send_sems, recv_sems, local_sems):
        x, y, c = _mesh_place()
        self.c = c
        self.me = (x, y, c)
        self.sibling = (x, y, 1 - c)
        self.chips = [(1 - x, y), (x, 1 - y), (1 - x, 1 - y)]
        self.tensors = list(zip(shards, gathered))
        self.send_sems, self.recv_sems, self.local_sems = send_sems, recv_sems, local_sems

    @staticmethod
    def _index(place):
        return 4 * place[0] + 2 * place[1] + place[2]

    def _copy(self, w, k, block, to, own=False):
        shard, gathered = self.tensors[w]
        slot = gathered.at[self._index(block)]
        n = w * self.COPIES + k
        return _remote(shard if own else slot, slot, self.send_sems.at[n], self.recv_sems.at[n], to)

    def _local(self, w):
        shard, gathered = self.tensors[w]
        return pltpu.make_async_copy(shard, gathered.at[self._index(self.me)], self.local_sems.at[w])

    def _first(self, w):
        return [self._copy(w, 0, self.me, self.sibling, own=True)] + [
            self._copy(w, 1 + j, self.me, (*chip, self.c), own=True) for j, chip in enumerate(self.chips)]

    def _passed(self, w):
        return [self._copy(w, 4 + j, (*chip, self.c), self.sibling) for j, chip in enumerate(self.chips)]

    def start(self):
        for w in range(len(self.tensors)):
            self._local(w).start()
            for cp in self._first(w):
                cp.start()

    def forward(self):
        for w in range(len(self.tensors)):
            passed = self._passed(w)
            for j, chip in enumerate(self.chips):
                self._copy(w, 1 + j, (*chip, self.c), self.me).wait_recv()
                passed[j].start()

    def finish(self):
        for w in range(len(self.tensors)):
            self._copy(w, 0, self.sibling, self.me).wait_recv()
            for j, chip in enumerate(self.chips):
                self._copy(w, 4 + j, (*chip, 1 - self.c), self.me).wait_recv()
            for cp in self._first(w) + self._passed(w):
                cp.wait_send()
            self._local(w).wait()


class _ScatterPlan:
    def __init__(self, grads, lands, send_sems, recv_sems, local_sems):
        self.place = _mesh_place()
        x, y, c = self.place
        self.me = 4 * x + 2 * y + c
        self.tensors = list(zip(grads, lands))
        self.send_sems, self.recv_sems, self.local_sems = send_sems, recv_sems, local_sems

    def _copies(self, w):
        grad, land = self.tensors[w]
        out = []
        for k in range(1, N_DEV):
            peer, pidx = _peer(*self.place, k)
            n = w * (N_DEV - 1) + k - 1
            sems = (self.send_sems.at[n], self.recv_sems.at[n], peer)
            out.append((_remote(grad.at[pidx], land.at[self.me], *sems), _remote(grad.at[pidx], land.at[pidx], *sems)))
        return out

    def _local(self, w):
        grad, land = self.tensors[w]
        return pltpu.make_async_copy(grad.at[self.me], land.at[self.me], self.local_sems.at[w])

    def start(self):
        for w in range(len(self.tensors)):
            self._local(w).start()
            for send, _ in self._copies(w):
                send.start()

    def finish(self):
        for w in range(len(self.tensors)):
            copies = self._copies(w)
            for _, arrival in copies:
                arrival.wait_recv()
            for send, _ in copies:
                send.wait_send()
            self._local(w).wait()


def _cast_shards(shards):
    def body(*refs):
        n = len(refs) // 2
        for src, dst in zip(refs[:n], refs[n:]):
            dst[...] = src[...].astype(dst.dtype)

    vmem = pl.BlockSpec(memory_space=pltpu.VMEM)
    return pl.pallas_call(
        body, name="cast_shards", in_specs=[vmem] * len(shards), out_specs=[vmem] * len(shards),
        out_shape=[jax.ShapeDtypeStruct(s.shape, BF16) for s in shards],
        compiler_params=pltpu.CompilerParams(vmem_limit_bytes=VMEM_LIMIT),
    )(*shards)


def _gather_w_in(shard):
    rows, cols = shard.shape

    def body(w_ref, out_ref, send_sems, recv_sems, local_sems):
        plan = _GatherPlan([w_ref], [out_ref], send_sems, recv_sems, local_sems)
        plan.start()
        plan.forward()
        plan.finish()

    vmem = pl.BlockSpec(memory_space=pltpu.VMEM)
    return pl.pallas_call(
        body, name="gather_w_in", in_specs=[vmem], out_specs=vmem,
        out_shape=jax.ShapeDtypeStruct((N_DEV, rows, cols), shard.dtype),
        scratch_shapes=[pltpu.SemaphoreType.DMA((_GatherPlan.COPIES,)), pltpu.SemaphoreType.DMA((_GatherPlan.COPIES,)),
                        pltpu.SemaphoreType.DMA((1,))],
        compiler_params=pltpu.CompilerParams(vmem_limit_bytes=VMEM_LIMIT),
    )(shard)


def _dw_rows(pieces, b, name, tn=512, tt=512):
    t, n = b.shape
    widths = [p.shape[1] for p in pieces]
    rows = sum(widths)
    tt = min(tt, t)
    steps = t // tt
    n_p = len(pieces)

    def body(*refs):
        piece_refs, b_ref, o_ref, acc = refs[:n_p], refs[n_p], refs[n_p + 1], refs[n_p + 2]
        s = pl.program_id(1)

        @pl.when(s == 0)
        def _():
            acc[...] = jnp.zeros_like(acc)

        bv = b_ref[...]
        off = 0
        for p_ref, width in zip(piece_refs, widths):
            acc[off:off + width, :] += _dot_tn(p_ref[...], bv)
            off += width

        @pl.when(s == steps - 1)
        def _():
            o_ref[...] = acc[...].astype(o_ref.dtype)

    return pl.pallas_call(
        body, name=name, grid=(n // tn, steps),
        in_specs=[pl.BlockSpec((tt, width), lambda j, s: (s, 0)) for width in widths]
        + [pl.BlockSpec((tt, tn), lambda j, s: (s, j))],
        out_specs=pl.BlockSpec((rows, tn), lambda j, s: (0, j)),
        out_shape=jax.ShapeDtypeStruct((rows, n), BF16),
        scratch_shapes=[pltpu.VMEM((rows, tn), F32)],
        compiler_params=_params(2),
    )(*pieces, b)


def _dh_norm_bwd(pairs, x, g, res, grads, name, tm=256):
    m, d = x.shape
    tm = min(tm, m)
    n_p, n_w = len(pairs), len(grads)
    n_steps = m // tm

    def body(*refs):
        x_ref, g_ref, res_ref = refs[2 * n_p:2 * n_p + 3]
        grad_refs = refs[2 * n_p + 3:2 * n_p + 3 + n_w]
        dx_ref, dg_ref = refs[2 * n_p + 3 + n_w:2 * n_p + 5 + n_w]
        lands = refs[2 * n_p + 5 + n_w:2 * n_p + 5 + 2 * n_w]
        sems = refs[2 * n_p + 5 + 2 * n_w:]
        step = pl.program_id(0)
        plan = _ScatterPlan(grad_refs, lands, *sems) if n_w else None

        @pl.when(step == 0)
        def _():
            dg_ref[...] = jnp.zeros_like(dg_ref)
            if plan:
                plan.start()

        dh = None
        for q in range(n_p):
            part = _dot(refs[2 * q][...], refs[2 * q + 1][...])
            dh = part if dh is None else dh + part
        xv = x_ref[...]
        r = lax.rsqrt(jnp.mean(xv * xv, axis=-1, keepdims=True) + EPS)
        xh = xv * r
        dxh = dh * g_ref[...]
        dx_ref[...] = res_ref[...] + r * (dxh - xh * jnp.mean(dxh * xh, axis=-1, keepdims=True))
        dg_ref[...] += jnp.sum(dh * xh, axis=0, keepdims=True)

        if plan:
            @pl.when(step == n_steps - 1)
            def _():
                plan.finish()

    in_specs, args = [], []
    for a, w, r in pairs:
        k = a.shape[1]
        in_specs += [pl.BlockSpec((tm, k), lambda i: (i, 0)),
                     pl.BlockSpec((k, d), functools.partial(lambda i, r: (r, 0), r=r or 0))]
        args += [a, w]
    row = pl.BlockSpec((tm, d), lambda i: (i, 0))
    vec = pl.BlockSpec((1, d), lambda i: (0, 0))
    hbm = pl.BlockSpec(memory_space=pltpu.HBM)
    sems = [pltpu.SemaphoreType.DMA((n_w * (N_DEV - 1),)), pltpu.SemaphoreType.DMA((n_w * (N_DEV - 1),)),
            pltpu.SemaphoreType.DMA((n_w,))] if n_w else []
    outs = pl.pallas_call(
        body, name=name, grid=(n_steps,), in_specs=in_specs + [row, vec, row] + [hbm] * n_w,
        out_specs=[row, vec] + [hbm] * n_w,
        out_shape=[jax.ShapeDtypeStruct((m, d), F32), jax.ShapeDtypeStruct((1, d), F32)]
        + [jax.ShapeDtypeStruct(gr.shape, gr.dtype) for gr in grads],
        scratch_shapes=sems, compiler_params=_params(1),
    )(*args, x, g, res, *grads)
    return outs[0], outs[1], list(outs[2:])


SMALL_LAYOUT = ((0, 0, 0, 0, D_MODEL), (1, 0, 1, 0, GROUP), (1, 1, 1, GROUP, GROUP), (2, 0, 2, 0, GROUP),
                (3, 0, 2, GROUP, GROUP), (4, 0, 3, 0, D_MODEL), (5, 0, 4, 0, D_MODEL))
LOSS_ROW = 5
N_SMALL = 6


def _small_step(grads, loss_part, ws, ms, vs):
    def body(*refs):
        g_in, loss_in = refs[:N_SMALL], refs[N_SMALL]
        params = [refs[1 + (q + 1) * N_SMALL:1 + (q + 2) * N_SMALL] for q in range(3)]
        o0 = 1 + 4 * N_SMALL
        outs = [refs[o0 + q * N_SMALL:o0 + (q + 1) * N_SMALL] for q in range(4)]
        loss_out = refs[o0 + 4 * N_SMALL]
        pack, land, wp, mp, vp, send_sems, recv_sems = refs[o0 + 4 * N_SMALL + 1:]

        def place(dst, srcs):
            dst[...] = jnp.zeros_like(dst)
            for p, sr, dr, dc, width in SMALL_LAYOUT:
                dst[dr:dr + 1, dc:dc + width] = srcs[p][sr:sr + 1, :]

        place(pack, g_in)
        pack[LOSS_ROW:LOSS_ROW + 1, 0:LANES] = loss_in[...]
        for dst, srcs in zip((wp, mp, vp), params):
            place(dst, srcs)

        x, y, c = _mesh_place()
        me = 4 * x + 2 * y + c
        land[me] = pack[...]
        sends = []
        for k in range(1, N_DEV):
            peer, _ = _peer(x, y, c, k)
            cp = _remote(pack, land.at[me], send_sems.at[k - 1], recv_sems.at[k - 1], peer)
            cp.start()
            sends.append(cp)
        for k in range(1, N_DEV):
            peer, pidx = _peer(x, y, c, k)
            _remote(pack, land.at[pidx], send_sems.at[k - 1], recv_sems.at[k - 1], peer).wait_recv()
        for cp in sends:
            cp.wait_send()

        g = land[0]
        for d in range(1, N_DEV):
            g = g + land[d]
        delta, nm, nv = _adam(wp[...], g, mp[...], vp[...])
        for val, out in zip((g, delta, nm, nv), outs):
            for p, sr, dr, dc, width in SMALL_LAYOUT:
                out[p][sr:sr + 1, :] = val[dr:dr + 1, dc:dc + width]
        loss_out[...] = g[LOSS_ROW:LOSS_ROW + 1, 0:LANES]

    vmem = pl.BlockSpec(memory_space=pltpu.VMEM)
    n_in = 1 + 4 * N_SMALL
    shapes = [jax.ShapeDtypeStruct(w.shape, F32) for w in ws]
    packed = pltpu.VMEM((SMALL_ROWS, D_MODEL), F32)
    outs = pl.pallas_call(
        body, name="small_step", in_specs=[vmem] * n_in, out_specs=[vmem] * (4 * N_SMALL + 1),
        out_shape=shapes * 4 + [jax.ShapeDtypeStruct((1, LANES), F32)],
        scratch_shapes=[packed, pltpu.VMEM((N_DEV, SMALL_ROWS, D_MODEL), F32), packed, packed, packed,
                        pltpu.SemaphoreType.DMA((N_DEV - 1,)), pltpu.SemaphoreType.DMA((N_DEV - 1,))],
    )(*grads, loss_part, *ws, *ms, *vs)
    return [outs[q * N_SMALL:(q + 1) * N_SMALL] for q in range(4)], outs[4 * N_SMALL]


def _adam(w, g, m, v):
    m = ADAM_B1 * m + (1.0 - ADAM_B1) * g
    v = ADAM_B2 * v + (1.0 - ADAM_B2) * (g * g)
    m_hat = m / (1.0 - ADAM_B1 ** ADAM_STEP)
    v_hat = v / (1.0 - ADAM_B2 ** ADAM_STEP)
    delta = -ADAM_LR * (m_hat / (jnp.sqrt(v_hat) + ADAM_EPS) + ADAM_WD * w)
    return delta, m, v


def _reduce_adamw(land, w, m, v, name, tr):
    _, rows, width = land.shape

    def body(land_ref, w_ref, m_ref, v_ref, g_ref, d_ref, nm_ref, nv_ref):
        g = land_ref[0].astype(F32)
        for d in range(1, N_DEV):
            g = g + land_ref[d].astype(F32)
        delta, nm, nv = _adam(w_ref[...], g, m_ref[...], v_ref[...])
        g_ref[...] = g
        d_ref[...] = delta
        nm_ref[...] = nm
        nv_ref[...] = nv

    row = pl.BlockSpec((tr, width), lambda i: (i, 0))
    out = jax.ShapeDtypeStruct((rows, width), F32)
    return pl.pallas_call(
        body, name=name, grid=(rows // tr,),
        in_specs=[pl.BlockSpec((N_DEV, tr, width), lambda i: (0, i, 0)), row, row, row],
        out_specs=[row, row, row, row], out_shape=[out, out, out, out],
        compiler_params=_params(1),
    )(land, w, m, v)


def kernel(x, mix_norm_g, w_in, lower_bounds, hgrn_norm_g, sb_norm_g, w_out, ffn_norm_g, w_gate, w_up, w_down, final_norm_g, loss_target, m_mix_norm_g, m_w_in, m_lower_bounds, m_hgrn_norm_g, m_sb_norm_g, m_w_out, m_ffn_norm_g, m_w_gate, m_w_up, m_w_down, m_final_norm_g, v_mix_norm_g, v_w_in, v_lower_bounds, v_hgrn_norm_g, v_sb_norm_g, v_w_out, v_ffn_norm_g, v_w_gate, v_w_up, v_w_down, v_final_norm_g):
    n_seq, seq, d = x.shape
    t = n_seq * seq
    x2d = x.reshape(t, d)
    tgt = loss_target.reshape(t, d)
    final_g = final_norm_g.reshape(1, d)
    col_sharded = (True, False, True, True, False)

    def as_rows(ws):
        return [w[0].T if tr else w[0] for w, tr in zip(ws, col_sharded)]

    big_w = as_rows([w_in, w_out, w_gate, w_up, w_down])
    big_m = as_rows([m_w_in, m_w_out, m_w_gate, m_w_up, m_w_down])
    big_v = as_rows([v_w_in, v_w_out, v_w_gate, v_w_up, v_w_down])

    sh_in, sh_out, sh_gate, sh_up, sh_down = _cast_shards(big_w)
    wt_in = _gather_w_in(sh_in).reshape(IN_COLS, d)
    h1 = _rmsnorm_fwd(x2d, mix_norm_g, "norm_mix")
    proj = _mm_nt([(h1, wt_in)], "proj", tk=896)
    mixed, oa_pre, states = _hgrn_fwd(proj, lower_bounds, hgrn_norm_g, n_seq, seq)
    mixed, ob_pre, ctot, gathered = _sb_fwd(proj, sb_norm_g, mixed, [sh_out, sh_gate, sh_up, sh_down], n_seq, seq)
    wf_out = gathered[0].reshape(d, d)
    wt_gate = gathered[1].reshape(D_FF, d)
    wt_up = gathered[2].reshape(D_FF, d)
    wf_down = gathered[3].reshape(D_FF, d)
    x1 = _mm_nn([(mixed, wf_out, None)], "mix_out", res=x2d)
    h2 = _rmsnorm_fwd(x1, ffn_norm_g, "norm_ffn")
    gate, up, ff = _ffn_up(h2, wt_gate, wt_up, "ffn_up")
    x2 = _mm_nn([(ff, wf_down, None)], "ffn_down", res=x1)
    dx2, dx2m, d_final_g, loss_part = _final_loss(x2, tgt, final_g, "loss_head")

    dgate, dup = _ffn_bwd_act(dx2m, wf_down, gate, up, "ffn_bwd_act")
    dw_down = _mm_tn(ff, dx2m, "dw_down", tk=1408, tn=1024).reshape(N_DEV, D_FF // N_DEV, d)
    dw_gate = _mm_tn(dgate, h2, "dw_gate", tk=1408, tn=1024).reshape(N_DEV, D_FF // N_DEV, d)
    dw_up = _mm_tn(dup, h2, "dw_up", tk=1408, tn=1024).reshape(N_DEV, D_FF // N_DEV, d)
    dx1, d_ffn_g, _ = _dh_norm_bwd([(dgate, wt_gate, None), (dup, wt_up, None)], x1, ffn_norm_g, dx2, [], "dh_ffn")
    dmix = _mm_nt([(dx1, wf_out)], "dmix")
    dw_out = _mm_tn(mixed, dx1, "dw_out", tk=512, tn=1024).reshape(N_DEV, d // N_DEV, d)
    dsq, dsk, dsv, d_sb_g, lands = _sb_bwd(proj, sb_norm_g, dmix, ob_pre, ctot, [dw_out, dw_gate, dw_up, dw_down],
                                            n_seq, seq)
    dhq, dhf, dhi, dhg, d_lb, d_hgrn_g = _hgrn_bwd(proj, lower_bounds, hgrn_norm_g, dmix, oa_pre, states, n_seq, seq)
    dproj = [dhq, dhf, dhi, dhg, dsq, dsk, dsv]
    dw_in = _dw_rows(dproj, h1, "dw_in").reshape(N_DEV, IN_COLS // N_DEV, d)
    dx, d_mix_g, (land_in,) = _dh_norm_bwd([(piece, wt_in, k) for k, piece in enumerate(dproj)], x2d, mix_norm_g,
                                           dx1, [dw_in], "dh_mix")

    tiles = {"in": 224, "out": 128, "gate": 176, "up": 176, "down": 176}
    big = [_reduce_adamw(land, w, m, v, "adamw_" + key, tr=tiles[key])
           for key, land, w, m, v in zip(tiles, [land_in] + lands, big_w, big_m, big_v)]
    big = [[r.T if tr else r for r in res] for res, tr in zip(big, col_sharded)]
    small, loss_row = _small_step(
        [d_mix_g, d_lb, d_hgrn_g, d_sb_g, d_ffn_g, d_final_g], loss_part,
        [mix_norm_g, lower_bounds, hgrn_norm_g, sb_norm_g, ffn_norm_g, final_g],
        [m_mix_norm_g, m_lower_bounds, m_hgrn_norm_g, m_sb_norm_g, m_ffn_norm_g, m_final_norm_g.reshape(1, d)],
        [v_mix_norm_g, v_lower_bounds, v_hgrn_norm_g, v_sb_norm_g, v_ffn_norm_g, v_final_norm_g.reshape(1, d)])

    outs = [loss_row[0, 0], dx.reshape(n_seq, seq, d)]
    for q in range(4):
        b_in, b_out, b_gate, b_up, b_down = [res[q][None] for res in big]
        s_mix, s_lb, s_hgrn, s_sb, s_ffn, s_final = small[q]
        outs += [s_mix, b_in, s_lb, s_hgrn, s_sb, b_out, s_ffn, b_gate, b_up, b_down, s_final.reshape(d)]
    return tuple(outs)
```

```python
import functools
import math

import jax
import jax.numpy as jnp
from jax import lax
from jax.experimental import pallas as pl
from jax.experimental.pallas import tpu as pltpu

F32 = jnp.float32
BF16 = jnp.bfloat16
MXU_DTYPE = BF16

EPS = 1e-6
D_MODEL = 1024
N_HEADS = 8
D_HEAD = 64
GROUP = N_HEADS * D_HEAD
IN_COLS = 7 * GROUP
D_FF = 2816
CHUNK = 64
LANES = 128
N_PAIRS = GROUP // LANES
SUPER = 256
SB_BLOCK = 256
N_DEV = 8

ADAM_LR = 0.001
ADAM_B1 = 0.9
ADAM_B2 = 0.999
ADAM_EPS = 1e-08
ADAM_WD = 0.01
ADAM_STEP = 10

SMALL_ROWS = 8
FF_TILE = D_FF // 2

VMEM_LIMIT = 48 * 1024 * 1024


def _params(n_axes, vmem=VMEM_LIMIT):
    return pltpu.CompilerParams(dimension_semantics=("arbitrary",) * n_axes, vmem_limit_bytes=vmem)


def _dot(a, b):
    return jnp.dot(a.astype(MXU_DTYPE), b.astype(MXU_DTYPE), preferred_element_type=F32)


def _dot_nt(a, b):
    return lax.dot_general(a.astype(MXU_DTYPE), b.astype(MXU_DTYPE), (((1,), (1,)), ((), ())),
                           preferred_element_type=F32)


def _dot_tn(a, b):
    return lax.dot_general(a.astype(MXU_DTYPE), b.astype(MXU_DTYPE), (((0,), (0,)), ((), ())),
                           preferred_element_type=F32)


def _split(x, parts):
    out, r = [], x
    for _ in range(parts):
        h = r.astype(BF16)
        out.append(h)
        r = r - h.astype(F32)
    return out


def _rsum_right(x, u, parts):
    acc = None
    for h in _split(x, parts):
        d = jnp.dot(h, u, preferred_element_type=F32)
        acc = d if acc is None else acc + d
    return acc


def _rsum_left(u, x, parts):
    acc = None
    for h in _split(x, parts):
        d = jnp.dot(u, h, preferred_element_type=F32)
        acc = d if acc is None else acc + d
    return acc


def _ones_where(mask):
    return jnp.where(mask, 1.0, 0.0).astype(BF16)


def _sigmoid(x):
    return 1.0 / (1.0 + jnp.exp(-x))


def _softplus(x):
    return jnp.maximum(x, 0.0) + jnp.log(1.0 + jnp.exp(-jnp.abs(x)))


def _head_masks():
    lane = lax.broadcasted_iota(jnp.int32, (1, LANES), 1)
    return [jnp.where(lane < D_HEAD, 1.0, 0.0), jnp.where(lane >= D_HEAD, 1.0, 0.0)]


def _head_rstd(o, masks):
    sq = o * o
    r = None
    for m in masks:
        ms = jnp.sum(sq * m, axis=1, keepdims=True) * (1.0 / D_HEAD)
        t = lax.rsqrt(ms + EPS) * m
        r = t if r is None else r + t
    return r


def _head_mean(t, masks):
    out = None
    for m in masks:
        v = jnp.sum(t * m, axis=1, keepdims=True) * (1.0 / D_HEAD) * m
        out = v if out is None else out + v
    return out


def _rmsnorm_fwd(x, g, name):
    t, d = x.shape
    tm = min(512, t)

    def body(x_ref, g_ref, o_ref):
        xv = x_ref[...]
        r = lax.rsqrt(jnp.mean(xv * xv, axis=-1, keepdims=True) + EPS)
        o_ref[...] = (xv * r * g_ref[...]).astype(o_ref.dtype)

    return pl.pallas_call(
        body, name=name, grid=(t // tm,),
        in_specs=[pl.BlockSpec((tm, d), lambda i: (i, 0)), pl.BlockSpec((1, d), lambda i: (0, 0))],
        out_specs=pl.BlockSpec((tm, d), lambda i: (i, 0)),
        out_shape=jax.ShapeDtypeStruct((t, d), MXU_DTYPE),
        compiler_params=_params(1),
    )(x, g)


def _final_loss(x, target, g, name):
    t, d = x.shape
    tm = min(512, t)

    def body(x_ref, t_ref, g_ref, dx_ref, dxm_ref, dg_ref, loss_ref):
        xv = x_ref[...]
        gv = g_ref[...]
        r = lax.rsqrt(jnp.mean(xv * xv, axis=-1, keepdims=True) + EPS)
        xh = xv * r
        e = xh * gv - t_ref[...]
        dy = e * (1.0 / d)
        dxh = dy * gv
        dxv = r * (dxh - xh * jnp.mean(dxh * xh, axis=-1, keepdims=True))
        dx_ref[...] = dxv
        dxm_ref[...] = dxv.astype(dxm_ref.dtype)

        @pl.when(pl.program_id(0) == 0)
        def _():
            dg_ref[...] = jnp.zeros_like(dg_ref)
            loss_ref[...] = jnp.zeros_like(loss_ref)

        dg_ref[...] += jnp.sum(dy * xh, axis=0, keepdims=True)
        part = 0.5 * jnp.sum(jnp.mean(e * e, axis=-1, keepdims=True), axis=0, keepdims=True)
        loss_ref[...] += jnp.broadcast_to(part, loss_ref.shape)

    row = pl.BlockSpec((tm, d), lambda i: (i, 0))
    vec = pl.BlockSpec((1, d), lambda i: (0, 0))
    return pl.pallas_call(
        body, name=name, grid=(t // tm,),
        in_specs=[row, row, vec],
        out_specs=[row, row, vec, pl.BlockSpec((1, LANES), lambda i: (0, 0))],
        out_shape=[jax.ShapeDtypeStruct((t, d), F32), jax.ShapeDtypeStruct((t, d), MXU_DTYPE),
                   jax.ShapeDtypeStruct((1, d), F32), jax.ShapeDtypeStruct((1, LANES), F32)],
        compiler_params=_params(1),
    )(x, target, g)


def _mm_nn(pairs, name, res=None, tm=512, tn=512):
    m = pairs[0][0].shape[0]
    n = pairs[0][1].shape[1]
    tm, tn = min(tm, m), min(tn, n)
    n_pairs = len(pairs)

    def body(*refs):
        o_ref = refs[-1]
        acc = None if res is None else refs[-2][...]
        for q in range(n_pairs):
            d = _dot(refs[2 * q][...], refs[2 * q + 1][...])
            acc = d if acc is None else acc + d
        o_ref[...] = acc

    in_specs, args = [], []
    for a, w, r in pairs:
        k = a.shape[1]
        in_specs += [pl.BlockSpec((tm, k), lambda j, i: (i, 0)),
                     pl.BlockSpec((k, tn), functools.partial(lambda j, i, r: (r, j), r=r or 0))]
        args += [a, w]
    if res is not None:
        in_specs.append(pl.BlockSpec((tm, tn), lambda j, i: (i, j)))
        args.append(res)
    return pl.pallas_call(
        body, name=name, grid=(n // tn, m // tm), in_specs=in_specs,
        out_specs=pl.BlockSpec((tm, tn), lambda j, i: (i, j)),
        out_shape=jax.ShapeDtypeStruct((m, n), F32),
        compiler_params=_params(2),
    )(*args)


def _mm_nt(pairs, name, tm=512, tk=512):
    m = pairs[0][0].shape[0]
    k = pairs[0][1].shape[0]
    tm, tk = min(tm, m), min(tk, k)
    n_pairs = len(pairs)

    def body(*refs):
        o_ref = refs[-1]
        acc = None
        for q in range(n_pairs):
            d = _dot_nt(refs[2 * q][...], refs[2 * q + 1][...])
            acc = d if acc is None else acc + d
        o_ref[...] = acc

    in_specs, args = [], []
    for a, w in pairs:
        n = a.shape[1]
        in_specs += [pl.BlockSpec((tm, n), lambda j, i: (i, 0)), pl.BlockSpec((tk, n), lambda j, i: (j, 0))]
        args += [a, w]
    return pl.pallas_call(
        body, name=name, grid=(k // tk, m // tm), in_specs=in_specs,
        out_specs=pl.BlockSpec((tm, tk), lambda j, i: (i, j)),
        out_shape=jax.ShapeDtypeStruct((m, k), F32),
        compiler_params=_params(2),
    )(*args)


def _mm_tn(a, b, name, tk, tn, tt=512, out_dtype=BF16):
    t, k = a.shape
    n = b.shape[1]
    tt = min(tt, t)
    steps = t // tt

    def body(a_ref, b_ref, o_ref, acc):
        s = pl.program_id(2)

        @pl.when(s == 0)
        def _():
            acc[...] = jnp.zeros_like(acc)

        acc[...] += _dot_tn(a_ref[...], b_ref[...])

        @pl.when(s == steps - 1)
        def _():
            o_ref[...] = acc[...].astype(o_ref.dtype)

    return pl.pallas_call(
        body, name=name, grid=(k // tk, n // tn, steps),
        in_specs=[pl.BlockSpec((tt, tk), lambda i, j, s: (s, i)), pl.BlockSpec((tt, tn), lambda i, j, s: (s, j))],
        out_specs=pl.BlockSpec((tk, tn), lambda i, j, s: (i, j)),
        out_shape=jax.ShapeDtypeStruct((k, n), out_dtype),
        scratch_shapes=[pltpu.VMEM((tk, tn), F32)],
        compiler_params=_params(3),
    )(a, b)


def _ffn_up(h, wg_t, wu_t, name, tm=512, tn=FF_TILE):
    m, k = h.shape
    n = wg_t.shape[0]
    tm = min(tm, m)

    def body(h_ref, wg_ref, wu_ref, gate_ref, up_ref, ff_ref):
        hv = h_ref[...]
        gate = _dot_nt(hv, wg_ref[...])
        up = _dot_nt(hv, wu_ref[...])
        gate_ref[...] = gate
        up_ref[...] = up
        ff_ref[...] = (gate * _sigmoid(gate) * up).astype(ff_ref.dtype)

    wspec = pl.BlockSpec((tn, k), lambda j, i: (j, 0))
    ospec = pl.BlockSpec((tm, tn), lambda j, i: (i, j))
    return pl.pallas_call(
        body, name=name, grid=(n // tn, m // tm),
        in_specs=[pl.BlockSpec((tm, k), lambda j, i: (i, 0)), wspec, wspec],
        out_specs=[ospec, ospec, ospec],
        out_shape=[jax.ShapeDtypeStruct((m, n), F32), jax.ShapeDtypeStruct((m, n), F32),
                   jax.ShapeDtypeStruct((m, n), MXU_DTYPE)],
        compiler_params=_params(2),
    )(h, wg_t, wu_t)


def _ffn_bwd_act(dx, wd, gate, up, name, tm=512, tn=FF_TILE):
    m, k = dx.shape
    n = wd.shape[0]
    tm = min(tm, m)

    def body(dx_ref, wd_ref, gate_ref, up_ref, dgate_ref, dup_ref):
        dff = _dot_nt(dx_ref[...], wd_ref[...])
        gate = gate_ref[...]
        sg = _sigmoid(gate)
        dgate_ref[...] = (dff * up_ref[...] * sg * (1.0 + gate * (1.0 - sg))).astype(dgate_ref.dtype)
        dup_ref[...] = (dff * gate * sg).astype(dup_ref.dtype)

    ospec = pl.BlockSpec((tm, tn), lambda j, i: (i, j))
    return pl.pallas_call(
        body, name=name, grid=(n // tn, m // tm),
        in_specs=[pl.BlockSpec((tm, k), lambda j, i: (i, 0)), pl.BlockSpec((tn, k), lambda j, i: (j, 0)),
                  ospec, ospec],
        out_specs=[ospec, ospec],
        out_shape=[jax.ShapeDtypeStruct((m, n), MXU_DTYPE), jax.ShapeDtypeStruct((m, n), MXU_DTYPE)],
        compiler_params=_params(2),
    )(dx, wd, gate, up)


def _chunk_masks():
    r = lax.broadcasted_iota(jnp.int32, (SUPER, SUPER), 0)
    c = lax.broadcasted_iota(jnp.int32, (SUPER, SUPER), 1)
    same = jnp.right_shift(r, 6) == jnp.right_shift(c, 6)
    lower = jnp.logical_and(same, c <= r)
    upper = jnp.logical_and(same, c >= r)
    return same, lower, upper


def _head_block_mask():
    r = lax.broadcasted_iota(jnp.int32, (LANES, LANES), 0)
    c = lax.broadcasted_iota(jnp.int32, (LANES, LANES), 1)
    return jnp.where(jnp.right_shift(r, 6) == jnp.right_shift(c, 6), 1.0, 0.0)


def _lower_bound(lb_raw):
    return 1.0 / (1.0 + jnp.exp(lb_raw[1:2, :] - lb_raw[0:1, :]))


PER_SUPER = SUPER // CHUNK
CHUNK_ROWS = [slice(c * CHUNK, (c + 1) * CHUNK) for c in range(PER_SUPER)]


def _over_chunks(rows):
    return jnp.concatenate([jnp.broadcast_to(r, (CHUNK, LANES)) for r in rows], axis=0)


def _hgrn_gates(q, hf, lb, lower_b):
    sig = _sigmoid(hf)
    f = lb + (1.0 - lb) * sig
    k = 1.0 - f
    lf = jnp.log(f)
    b = _rsum_left(lower_b, lf, 2)
    ends = [b[cr.stop - 1:cr.stop, :] for cr in CHUNK_ROWS]
    eb = jnp.exp(b)
    enb = jnp.exp(-b)
    edb = jnp.exp(_over_chunks(ends) - b)
    decs = [jnp.exp(e) for e in ends]
    return sig, f, k, decs, eb, enb, edb, q * eb, k * enb, k * edb


def _hgrn_fwd(proj, lower_bounds, norm_g, n_seq, seq):
    t = n_seq * seq
    n_super = seq // SUPER
    n_chunks = seq // CHUNK

    def body(q_ref, f_ref, i_ref, g_ref, lb_ref, ng_ref, out_ref, opre_ref, st_ref):
        masks = _head_masks()
        _, lower, _ = _chunk_masks()
        lower_b = _ones_where(lower)
        bd = _head_block_mask()
        lb = _lower_bound(lb_ref[...])
        ng = ng_ref[...]

        def step(sb, st):
            rows = pl.ds(pl.multiple_of(sb * SUPER, SUPER), SUPER)
            q, hf, v, hg = q_ref[rows, :], f_ref[rows, :], i_ref[rows, :], g_ref[rows, :]
            _, _, _, decs, _, _, _, qe, ke, kd = _hgrn_gates(q, hf, lb, lower_b)
            scores = [_dot_nt(qe * m, ke) for m in masks]
            updates = [_dot_tn(v[cr], kd[cr]) for cr in CHUNK_ROWS]
            states = [st]
            for dec, upd in zip(decs, updates):
                states.append(states[-1] * dec + bd * upd)
            for c in range(PER_SUPER):
                st_ref[0, 0, sb * PER_SUPER + c] = states[c]
            intra = [_dot(jnp.where(lower, p, 0.0), v) for p in scores]
            inter = [_dot_nt(qe[cr], s) for cr, s in zip(CHUNK_ROWS, states)]
            o = intra[0] * masks[0] + intra[1] * masks[1] + jnp.concatenate(inter, axis=0)
            opre_ref[rows, :] = o
            on = o * _head_rstd(o, masks) * ng
            out_ref[rows, :] = (on * hg * _sigmoid(hg)).astype(out_ref.dtype)
            return states[-1]

        lax.fori_loop(0, n_super, step, jnp.zeros((LANES, LANES), F32))

    def col(k):
        return pl.BlockSpec((seq, LANES), lambda p, b: (b, k * N_PAIRS + p))

    vec = lambda rows: pl.BlockSpec((rows, LANES), lambda p, b: (0, p))
    ospec = pl.BlockSpec((seq, LANES), lambda p, b: (b, p))
    return pl.pallas_call(
        body, name="hgrn_fwd", grid=(N_PAIRS, n_seq),
        in_specs=[col(0), col(1), col(2), col(3), vec(2), vec(1)],
        out_specs=[ospec, ospec,
                   pl.BlockSpec((1, 1, n_chunks, LANES, LANES), lambda p, b: (b, p, 0, 0, 0))],
        out_shape=[jax.ShapeDtypeStruct((t, 2 * GROUP), MXU_DTYPE), jax.ShapeDtypeStruct((t, GROUP), F32),
                   jax.ShapeDtypeStruct((n_seq, N_PAIRS, n_chunks, LANES, LANES), F32)],
        compiler_params=_params(2),
    )(proj, proj, proj, proj, lower_bounds, norm_g)


def _hgrn_bwd(proj, lower_bounds, norm_g, dmix, opre, states, n_seq, seq):
    t = n_seq * seq
    n_super = seq // SUPER
    n_chunks = seq // CHUNK
    per = SUPER // CHUNK

    def body(q_ref, f_ref, i_ref, g_ref, lb_ref, ng_ref, dm_ref, opre_ref, st_ref,
             dq_ref, df_ref, di_ref, dg_ref, dlb_ref, dng_ref):
        masks = _head_masks()
        _, lower, upper = _chunk_masks()
        lower_b, upper_b = _ones_where(lower), _ones_where(upper)
        bd = _head_block_mask()
        lb_raw = lb_ref[...]
        lb = _lower_bound(lb_raw)
        ng = ng_ref[...]

        @pl.when(pl.program_id(1) == 0)
        def _():
            dlb_ref[...] = jnp.zeros_like(dlb_ref)
            dng_ref[...] = jnp.zeros_like(dng_ref)

        def step(it, dst):
            sb = n_super - 1 - it
            rows = pl.ds(pl.multiple_of(sb * SUPER, SUPER), SUPER)
            q, hf, v, hg = q_ref[rows, :], f_ref[rows, :], i_ref[rows, :], g_ref[rows, :]
            sig, f, k, decs, eb, enb, edb, qe, ke, kd = _hgrn_gates(q, hf, lb, lower_b)
            o = opre_ref[rows, :]
            r = _head_rstd(o, masks)
            oh = o * r
            dm = dm_ref[rows, :]
            sg = _sigmoid(hg)
            dg_ref[rows, :] = (dm * oh * ng * sg * (1.0 + hg * (1.0 - sg))).astype(dg_ref.dtype)
            don = dm * hg * sg
            dng_ref[...] += jnp.sum(don * oh, axis=0, keepdims=True)
            doh = don * ng
            do = r * (doh - oh * _head_mean(doh * oh, masks))
            doms = [do * m for m in masks]
            qems = [qe * m for m in masks]
            scores = [_dot_nt(qem, ke) for qem in qems]
            dscores = [_dot_nt(dom, v) for dom in doms]
            prevs = [st_ref[0, 0, sb * per + c] for c in range(per)]
            dst_in = [_dot_tn(do[cr], qe[cr]) for cr in CHUNK_ROWS]
            dqe_i = [_dot(do[cr], prev) for cr, prev in zip(CHUNK_ROWS, prevs)]
            dsts = [None] * per
            for c in reversed(range(per)):
                dsts[c] = dst
                dst = bd * (dst * decs[c] + dst_in[c])
            ps = [jnp.where(lower, p, 0.0) for p in scores]
            dps = [jnp.where(lower, dp, 0.0) for dp in dscores]
            dqe_h = [_dot(dp, ke) for dp in dps]
            dke_h = [_dot_tn(dp, qem) for dp, qem in zip(dps, qems)]
            dv_h = [_dot_tn(p, dom) for p, dom in zip(ps, doms)]
            dus = [bd * d for d in dsts]
            dv_i = [_dot_nt(kd[cr], du) for cr, du in zip(CHUNK_ROWS, dus)]
            dkd_i = [_dot(v[cr], du) for cr, du in zip(CHUNK_ROWS, dus)]
            dqe = dqe_h[0] * masks[0] + dqe_h[1] * masks[1] + jnp.concatenate(dqe_i, axis=0)
            dke = dke_h[0] + dke_h[1]
            dv = dv_h[0] + dv_h[1] + jnp.concatenate(dv_i, axis=0)
            dkd = jnp.concatenate(dkd_i, axis=0)
            dk = dke * enb + dkd * edb
            db = dqe * qe - dke * ke - dkd * kd
            dkd_kd = dkd * kd
            dends = [jnp.sum(dkd_kd[cr], axis=0, keepdims=True)
                     + jnp.sum(dsts[c] * prevs[c], axis=0, keepdims=True) * decs[c]
                     for c, cr in enumerate(CHUNK_ROWS)]
            dlf = _rsum_left(upper_b, db, 2) + _over_chunks(dends)
            dfv = dlf / f - dk
            dq_ref[rows, :] = (dqe * eb).astype(dq_ref.dtype)
            di_ref[rows, :] = dv.astype(di_ref.dtype)
            df_ref[rows, :] = (dfv * (1.0 - lb) * sig * (1.0 - sig)).astype(df_ref.dtype)
            dlb = jnp.sum(dfv * (1.0 - sig), axis=0, keepdims=True)
            da0 = dlb * lb * (1.0 - lb)
            dlb_ref[0:1, :] += da0
            dlb_ref[1:2, :] -= da0
            return dst

        lax.fori_loop(0, n_super, step, jnp.zeros((LANES, LANES), F32))

    def col(k):
        return pl.BlockSpec((seq, LANES), lambda p, b: (b, k * N_PAIRS + p))

    vec = lambda rows: pl.BlockSpec((rows, LANES), lambda p, b: (0, p))
    ospec = pl.BlockSpec((seq, LANES), lambda p, b: (b, p))
    piece = jax.ShapeDtypeStruct((t, GROUP), MXU_DTYPE)
    return pl.pallas_call(
        body, name="hgrn_bwd", grid=(N_PAIRS, n_seq),
        in_specs=[col(0), col(1), col(2), col(3), vec(2), vec(1), ospec, ospec,
                  pl.BlockSpec((1, 1, n_chunks, LANES, LANES), lambda p, b: (b, p, 0, 0, 0))],
        out_specs=[ospec, ospec, ospec, ospec, vec(2), vec(1)],
        out_shape=[piece, piece, piece, piece,
                   jax.ShapeDtypeStruct((2, GROUP), F32), jax.ShapeDtypeStruct((1, GROUP), F32)],
        compiler_params=_params(2),
    )(proj, proj, proj, proj, lower_bounds, norm_g, dmix, opre, states)


SB_SCALE = 1.0 / math.sqrt(D_HEAD)


def _tile_masks():
    r = lax.broadcasted_iota(jnp.int32, (SB_BLOCK, SB_BLOCK), 0)
    c = lax.broadcasted_iota(jnp.int32, (SB_BLOCK, SB_BLOCK), 1)
    return r, c


def _sb_fwd(proj, norm_g, mixed, shards, n_seq, seq):
    t = n_seq * seq
    nq = seq // SB_BLOCK
    q0, k0, v0 = 4 * N_PAIRS, 5 * N_PAIRS, 6 * N_PAIRS
    n_w = len(shards)
    n_steps = N_PAIRS * n_seq * nq

    def body(q_ref, k_ref, v_ref, ng_ref, mixed_in, *rest):
        del mixed_in
        shard_refs = rest[:n_w]
        out_ref, opre_ref, ctot_ref = rest[n_w:n_w + 3]
        gathered = rest[n_w + 3:2 * n_w + 3]
        send_sems, recv_sems, local_sems = rest[2 * n_w + 3:]
        i = pl.program_id(2)
        step = (pl.program_id(0) * n_seq + pl.program_id(1)) * nq + i
        plan = _GatherPlan(shard_refs, gathered, send_sems, recv_sems, local_sems)

        @pl.when(step == 0)
        def _():
            plan.start()

        @pl.when(step == n_steps // 2)
        def _():
            plan.forward()

        masks = _head_masks()
        r, c = _tile_masks()
        strict = c < r
        suffix = _ones_where(r >= c)
        suffix2 = jnp.concatenate([suffix, suffix], axis=0)
        qs = q_ref[...] * SB_SCALE
        qhs = [(qs * m).astype(MXU_DTYPE) for m in masks]

        def tiles(js, carry, diag):
            rows = [pl.ds(pl.multiple_of(j * SB_BLOCK, SB_BLOCK), SB_BLOCK) for j in js]
            ks = [k_ref[rw, :].astype(MXU_DTYPE) for rw in rows]
            vs = [v_ref[rw, :].astype(MXU_DTYPE) for rw in rows]
            zs = [[_dot_nt(qh, kj) for qh in qhs] for kj in ks]
            ccs = [[_softplus(z) for z in zt] for zt in zs]
            if diag:
                ccs = [[jnp.where(strict, cc, 0.0) for cc in ct] for ct in ccs]
            sums = [[jnp.dot(jnp.concatenate(_split(cc, 2), axis=1), suffix2, preferred_element_type=F32)
                     for cc in ct] for ct in ccs]
            out = []
            for h, (run, acc) in enumerate(carry):
                for t in range(len(js)):
                    a = jnp.exp(zs[t][h] - (sums[t][h] + run))
                    if diag:
                        a = jnp.where(strict, a, 0.0)
                    acc = acc + _dot(a, vs[t])
                    run = run + sums[t][h][:, 0:1]
                out.append((run, acc))
            return tuple(out)

        start = ((jnp.zeros((SB_BLOCK, 1), F32), jnp.zeros((SB_BLOCK, LANES), F32)),) * 2
        carry = tiles([i], start, True)
        carry = lax.fori_loop(0, i // 4, lambda s, cy: tiles([i - 1 - 4 * s - u for u in range(4)], cy, False), carry)
        carry = lax.fori_loop(0, (i % 4) // 2, lambda s, cy: tiles([(i % 4) - 1, (i % 4) - 2], cy, False), carry)
        carry = lax.fori_loop(0, i % 2, lambda s, cy: tiles([0], cy, False), carry)
        opre = carry[0][1] * masks[0] + carry[1][1] * masks[1]
        ctot = carry[0][0] * masks[0] + carry[1][0] * masks[1]
        opre_ref[...] = opre
        ctot_ref[...] = ctot
        out_ref[...] = (opre * _head_rstd(opre, masks) * ng_ref[...]).astype(out_ref.dtype)

        @pl.when(step == n_steps - 1)
        def _():
            plan.finish()

    qspec = pl.BlockSpec((SB_BLOCK, LANES), lambda p, b, i: (b * nq + i, q0 + p))
    ospec = pl.BlockSpec((SB_BLOCK, LANES), lambda p, b, i: (b * nq + i, p))
    hbm = pl.BlockSpec(memory_space=pltpu.HBM)
    outs = pl.pallas_call(
        body, name="sb_fwd", grid=(N_PAIRS, n_seq, nq),
        in_specs=[qspec,
                  pl.BlockSpec((seq, LANES), lambda p, b, i: (b, k0 + p)),
                  pl.BlockSpec((seq, LANES), lambda p, b, i: (b, v0 + p)),
                  pl.BlockSpec((1, LANES), lambda p, b, i: (0, p)), hbm] + [hbm] * n_w,
        out_specs=[pl.BlockSpec((SB_BLOCK, LANES), lambda p, b, i: (b * nq + i, N_PAIRS + p)), ospec, ospec]
        + [hbm] * n_w,
        out_shape=[jax.ShapeDtypeStruct(mixed.shape, mixed.dtype), jax.ShapeDtypeStruct((t, GROUP), F32),
                   jax.ShapeDtypeStruct((t, GROUP), F32)]
        + [jax.ShapeDtypeStruct((N_DEV,) + s.shape, s.dtype) for s in shards],
        scratch_shapes=[pltpu.SemaphoreType.DMA((n_w * _GatherPlan.COPIES,)),
                        pltpu.SemaphoreType.DMA((n_w * _GatherPlan.COPIES,)), pltpu.SemaphoreType.DMA((n_w,))],
        input_output_aliases={4: 0},
        compiler_params=_params(3),
    )(proj, proj, proj, norm_g, mixed, *shards)
    return outs[0], outs[1], outs[2], list(outs[3:])


def _sb_bwd(proj, norm_g, dmix, opre, ctot, grads, n_seq, seq):
    t = n_seq * seq
    nq = seq // SB_BLOCK
    q0, k0, v0 = 4 * N_PAIRS, 5 * N_PAIRS, 6 * N_PAIRS
    n_w = len(grads)
    n_steps = N_PAIRS * n_seq * nq

    def body(q_ref, k_ref, v_ref, ng_ref, dm_ref, opre_ref, ctot_ref, *rest):
        grad_refs = rest[:n_w]
        dq_ref, dk_ref, dv_ref, dng_ref = rest[n_w:n_w + 4]
        lands = rest[n_w + 4:2 * n_w + 4]
        dk_acc, dv_acc, send_sems, recv_sems, local_sems = rest[2 * n_w + 4:]
        p_id, b_id, i = pl.program_id(0), pl.program_id(1), pl.program_id(2)
        step = (p_id * n_seq + b_id) * nq + i
        plan = _ScatterPlan(grad_refs, lands, send_sems, recv_sems, local_sems)

        @pl.when(step == 0)
        def _():
            plan.start()

        masks = _head_masks()
        r, c = _tile_masks()
        strict = c < r
        upto = _ones_where(r <= c)
        upto2 = jnp.concatenate([upto, upto], axis=0)

        def prefix(x):
            return jnp.dot(jnp.concatenate(_split(x, 2), axis=1), upto2, preferred_element_type=F32)

        @pl.when(i == 0)
        def _():
            dk_acc[...] = jnp.zeros_like(dk_acc)
            dv_acc[...] = jnp.zeros_like(dv_acc)

        @pl.when(jnp.logical_and(b_id == 0, i == 0))
        def _():
            dng_ref[...] = jnp.zeros_like(dng_ref)

        o = opre_ref[...]
        rs = _head_rstd(o, masks)
        oh = o * rs
        dm = dm_ref[...]
        dng_ref[...] += jnp.sum(dm * oh, axis=0, keepdims=True)
        doh = dm * ng_ref[...]
        do = rs * (doh - oh * _head_mean(doh * oh, masks))

        qs = q_ref[...] * SB_SCALE
        ct = ctot_ref[...]
        qhs = [(qs * m).astype(MXU_DTYPE) for m in masks]
        doms = [(do * m).astype(MXU_DTYPE) for m in masks]
        qhts = [(qs * m).T.astype(MXU_DTYPE) for m in masks]
        domts = [(do * m).T.astype(MXU_DTYPE) for m in masks]
        totals = [jnp.max(ct * m, axis=1, keepdims=True) for m in masks]
        heads = range(len(masks))

        def tiles(js, carry, diag):
            nt = range(len(js))
            rows = [pl.ds(pl.multiple_of(j * SB_BLOCK, SB_BLOCK), SB_BLOCK) for j in js]
            ks = [k_ref[rw, :].astype(MXU_DTYPE) for rw in rows]
            vs = [v_ref[rw, :].astype(MXU_DTYPE) for rw in rows]
            zs = [[_dot_nt(qhs[h], ks[t]) for h in heads] for t in nt]
            das = [[_dot_nt(doms[h], vs[t]) for h in heads] for t in nt]
            sps = [[_softplus(z) for z in zt] for zt in zs]
            lsigs = [[z - sp for z, sp in zip(zt, st)] for zt, st in zip(zs, sps)]
            sigs = [[jnp.exp(ls) for ls in lt] for lt in lsigs]
            ccs = [[jnp.where(strict, sp, 0.0) for sp in st] for st in sps] if diag else sps
            pres = [[prefix(cc) for cc in ct] for ct in ccs]
            out = []
            for h in heads:
                pc, pdl, dq_h = carry[h]
                for t in nt:
                    a = jnp.exp(lsigs[t][h] + pres[t][h] - (totals[h] - pc))
                    if diag:
                        a = jnp.where(strict, a, 0.0)
                    dl = a * das[t][h]
                    dv_acc[js[t]] += _dot(domts[h], a)
                    dpre = jnp.dot(dl.astype(BF16), upto, preferred_element_type=F32)
                    dz = dl - sigs[t][h] * (pdl + dpre)
                    if diag:
                        dz = jnp.where(strict, dz, 0.0)
                    dzb = dz.astype(MXU_DTYPE)
                    dq_h = dq_h + _dot(dzb, ks[t])
                    dk_acc[js[t]] += _dot(qhts[h], dzb)
                    pc = pc + pres[t][h][:, SB_BLOCK - 1:SB_BLOCK]
                    pdl = pdl + dpre[:, SB_BLOCK - 1:SB_BLOCK]
                out.append((pc, pdl, dq_h))
            return tuple(out)

        zero = jnp.zeros((SB_BLOCK, 1), F32)
        carry = ((zero, zero, jnp.zeros((SB_BLOCK, LANES), F32)),) * 2
        carry = lax.fori_loop(0, i // 2, lambda s, cy: tiles([2 * s, 2 * s + 1], cy, False), carry)
        carry = lax.fori_loop(0, i % 2, lambda s, cy: tiles([i - 1], cy, False), carry)
        carry = tiles([i], carry, True)
        dq = carry[0][2] * masks[0] + carry[1][2] * masks[1]
        dq_ref[...] = (dq * SB_SCALE).astype(dq_ref.dtype)

        @pl.when(i == nq - 1)
        def _():
            for j in range(nq):
                tile_rows = slice(j * SB_BLOCK, (j + 1) * SB_BLOCK)
                dk_ref[tile_rows, :] = dk_acc[j].T.astype(dk_ref.dtype)
                dv_ref[tile_rows, :] = dv_acc[j].T.astype(dv_ref.dtype)

        @pl.when(step == n_steps - 1)
        def _():
            plan.finish()

    qspec = pl.BlockSpec((SB_BLOCK, LANES), lambda p, b, i: (b * nq + i, q0 + p))
    ospec = pl.BlockSpec((SB_BLOCK, LANES), lambda p, b, i: (b * nq + i, p))
    dmspec = pl.BlockSpec((SB_BLOCK, LANES), lambda p, b, i: (b * nq + i, N_PAIRS + p))
    full = lambda k: pl.BlockSpec((seq, LANES), lambda p, b, i: (b, k + p))
    vec = pl.BlockSpec((1, LANES), lambda p, b, i: (0, p))
    hbm = pl.BlockSpec(memory_space=pltpu.HBM)
    piece = jax.ShapeDtypeStruct((t, GROUP), MXU_DTYPE)
    outs = pl.pallas_call(
        body, name="sb_bwd", grid=(N_PAIRS, n_seq, nq),
        in_specs=[qspec, full(k0), full(v0), vec, dmspec, ospec, ospec] + [hbm] * n_w,
        out_specs=[ospec, full(0), full(0), vec] + [hbm] * n_w,
        out_shape=[piece, piece, piece, jax.ShapeDtypeStruct((1, GROUP), F32)]
        + [jax.ShapeDtypeStruct(g.shape, g.dtype) for g in grads],
        scratch_shapes=[pltpu.VMEM((nq, LANES, SB_BLOCK), F32), pltpu.VMEM((nq, LANES, SB_BLOCK), F32),
                        pltpu.SemaphoreType.DMA((n_w * (N_DEV - 1),)), pltpu.SemaphoreType.DMA((n_w * (N_DEV - 1),)),
                        pltpu.SemaphoreType.DMA((n_w,))],
        compiler_params=_params(3),
    )(proj, proj, proj, norm_g, dmix, opre, ctot, *grads)
    return outs[0], outs[1], outs[2], outs[3], list(outs[4:])


def _mesh_place():
    x, y, c = lax.axis_index("x"), lax.axis_index("y"), lax.axis_index("c")
    return x, y, c


def _peer(x, y, c, k):
    px = lax.rem(x + ((k >> 2) & 1), 2)
    py = lax.rem(y + ((k >> 1) & 1), 2)
    pc = lax.rem(c + (k & 1), 2)
    return (px, py, pc), 4 * px + 2 * py + pc


def _remote(src, dst, send_sem, recv_sem, to):
    return pltpu.make_async_remote_copy(src_ref=src, dst_ref=dst, send_sem=send_sem, recv_sem=recv_sem,
                                        device_id=to, device_id_type=pl.DeviceIdType.MESH)


class _GatherPlan:
    COPIES = 7

    def __init__(self, shards, gathered, send_sems, recv_sems, local_sems):
        x, y, c = _mesh_place()
        self.c = c
        self.me = (x, y, c)
        self.sibling = (x, y, 1 - c)
        self.chips = [(1 - x, y), (x, 1 - y), (1 - x, 1 - y)]
        self.tensors = list(zip(shards, gathered))
        self.send_sems, self.recv_sems, self.local_sems = send_sems, recv_sems, local_sems

    @staticmethod
    def _index(place):
        return 4 * place[0] + 2 * place[1] + place[2]

    def _copy(self, w, k, block, to, own=False):
        shard, gathered = self.tensors[w]
        slot = gathered.at[self._index(block)]
        n = w * self.COPIES + k
        return _remote(shard if own else slot, slot, self.send_sems.at[n], self.recv_sems.at[n], to)

    def _local(self, w):
        shard, gathered = self.tensors[w]
        return pltpu.make_async_copy(shard, gathered.at[self._index(self.me)], self.local_sems.at[w])

    def _first(self, w):
        return [self._copy(w, 0, self.me, self.sibling, own=True)] + [
            self._copy(w, 1 + j, self.me, (*chip, self.c), own=True) for j, chip in enumerate(self.chips)]

    def _passed(self, w):
        return [self._copy(w, 4 + j, (*chip, self.c), self.sibling) for j, chip in enumerate(self.chips)]

    def start(self):
        for w in range(len(self.tensors)):
            self._local(w).start()
            for cp in self._first(w):
                cp.start()

    def forward(self):
        for w in range(len(self.tensors)):
            passed = self._passed(w)
            for j, chip in enumerate(self.chips):
                self._copy(w, 1 + j, (*chip, self.c), self.me).wait_recv()
                passed[j].start()

    def finish(self):
        for w in range(len(self.tensors)):
            self._copy(w, 0, self.sibling, self.me).wait_recv()
            for j, chip in enumerate(self.chips):
                self._copy(w, 4 + j, (*chip, 1 - self.c), self.me).wait_recv()
            for cp in self._first(w) + self._passed(w):
                cp.wait_send()
            self._local(w).wait()


class _ScatterPlan:
    def __init__(self, grads, lands, send_sems, recv_sems, local_sems):
        self.place = _mesh_place()
        x, y, c = self.place
        self.me = 4 * x + 2 * y + c
        self.tensors = list(zip(grads, lands))
        self.send_sems, self.recv_sems, self.local_sems = send_sems, recv_sems, local_sems

    def _copies(self, w):
        grad, land = self.tensors[w]
        out = []
        for k in range(1, N_DEV):
            peer, pidx = _peer(*self.place, k)
            n = w * (N_DEV - 1) + k - 1
            sems = (self.send_sems.at[n], self.recv_sems.at[n], peer)
            out.append((_remote(grad.at[pidx], land.at[self.me], *sems), _remote(grad.at[pidx], land.at[pidx], *sems)))
        return out

    def _local(self, w):
        grad, land = self.tensors[w]
        return pltpu.make_async_copy(grad.at[self.me], land.at[self.me], self.local_sems.at[w])

    def start(self):
        for w in range(len(self.tensors)):
            self._local(w).start()
            for send, _ in self._copies(w):
                send.start()

    def finish(self):
        for w in range(len(self.tensors)):
            copies = self._copies(w)
            for _, arrival in copies:
                arrival.wait_recv()
            for send, _ in copies:
                send.wait_send()
            self._local(w).wait()


def _cast_shards(shards):
    def body(*refs):
        n = len(refs) // 2
        for src, dst in zip(refs[:n], refs[n:]):
            dst[...] = src[...].astype(dst.dtype)

    vmem = pl.BlockSpec(memory_space=pltpu.VMEM)
    return pl.pallas_call(
        body, name="cast_shards", in_specs=[vmem] * len(shards), out_specs=[vmem] * len(shards),
        out_shape=[jax.ShapeDtypeStruct(s.shape, BF16) for s in shards],
        compiler_params=pltpu.CompilerParams(vmem_limit_bytes=VMEM_LIMIT),
    )(*shards)


def _gather_w_in(shard):
    rows, cols = shard.shape

    def body(w_ref, out_ref, send_sems, recv_sems, local_sems):
        plan = _GatherPlan([w_ref], [out_ref], send_sems, recv_sems, local_sems)
        plan.start()
        plan.forward()
        plan.finish()

    vmem = pl.BlockSpec(memory_space=pltpu.VMEM)
    return pl.pallas_call(
        body, name="gather_w_in", in_specs=[vmem], out_specs=vmem,
        out_shape=jax.ShapeDtypeStruct((N_DEV, rows, cols), shard.dtype),
        scratch_shapes=[pltpu.SemaphoreType.DMA((_GatherPlan.COPIES,)), pltpu.SemaphoreType.DMA((_GatherPlan.COPIES,)),
                        pltpu.SemaphoreType.DMA((1,))],
        compiler_params=pltpu.CompilerParams(vmem_limit_bytes=VMEM_LIMIT),
    )(shard)


def _dw_rows(pieces, b, name, tn=512, tt=512):
    t, n = b.shape
    widths = [p.shape[1] for p in pieces]
    rows = sum(widths)
    tt = min(tt, t)
    steps = t // tt
    n_p = len(pieces)

    def body(*refs):
        piece_refs, b_ref, o_ref, acc = refs[:n_p], refs[n_p], refs[n_p + 1], refs[n_p + 2]
        s = pl.program_id(1)

        @pl.when(s == 0)
        def _():
            acc[...] = jnp.zeros_like(acc)

        bv = b_ref[...]
        off = 0
        for p_ref, width in zip(piece_refs, widths):
            acc[off:off + width, :] += _dot_tn(p_ref[...], bv)
            off += width

        @pl.when(s == steps - 1)
        def _():
            o_ref[...] = acc[...].astype(o_ref.dtype)

    return pl.pallas_call(
        body, name=name, grid=(n // tn, steps),
        in_specs=[pl.BlockSpec((tt, width), lambda j, s: (s, 0)) for width in widths]
        + [pl.BlockSpec((tt, tn), lambda j, s: (s, j))],
        out_specs=pl.BlockSpec((rows, tn), lambda j, s: (0, j)),
        out_shape=jax.ShapeDtypeStruct((rows, n), BF16),
        scratch_shapes=[pltpu.VMEM((rows, tn), F32)],
        compiler_params=_params(2),
    )(*pieces, b)


def _dh_norm_bwd(pairs, x, g, res, grads, name, tm=256):
    m, d = x.shape
    tm = min(tm, m)
    n_p, n_w = len(pairs), len(grads)
    n_steps = m // tm

    def body(*refs):
        x_ref, g_ref, res_ref = refs[2 * n_p:2 * n_p + 3]
        grad_refs = refs[2 * n_p + 3:2 * n_p + 3 + n_w]
        dx_ref, dg_ref = refs[2 * n_p + 3 + n_w:2 * n_p + 5 + n_w]
        lands = refs[2 * n_p + 5 + n_w:2 * n_p + 5 + 2 * n_w]
        sems = refs[2 * n_p + 5 + 2 * n_w:]
        step = pl.program_id(0)
        plan = _ScatterPlan(grad_refs, lands, *sems) if n_w else None

        @pl.when(step == 0)
        def _():
            dg_ref[...] = jnp.zeros_like(dg_ref)
            if plan:
                plan.start()

        dh = None
        for q in range(n_p):
            part = _dot(refs[2 * q][...], refs[2 * q + 1][...])
            dh = part if dh is None else dh + part
        xv = x_ref[...]
        r = lax.rsqrt(jnp.mean(xv * xv, axis=-1, keepdims=True) + EPS)
        xh = xv * r
        dxh = dh * g_ref[...]
        dx_ref[...] = res_ref[...] + r * (dxh - xh * jnp.mean(dxh * xh, axis=-1, keepdims=True))
        dg_ref[...] += jnp.sum(dh * xh, axis=0, keepdims=True)

        if plan:
            @pl.when(step == n_steps - 1)
            def _():
                plan.finish()

    in_specs, args = [], []
    for a, w, r in pairs:
        k = a.shape[1]
        in_specs += [pl.BlockSpec((tm, k), lambda i: (i, 0)),
                     pl.BlockSpec((k, d), functools.partial(lambda i, r: (r, 0), r=r or 0))]
        args += [a, w]
    row = pl.BlockSpec((tm, d), lambda i: (i, 0))
    vec = pl.BlockSpec((1, d), lambda i: (0, 0))
    hbm = pl.BlockSpec(memory_space=pltpu.HBM)
    sems = [pltpu.SemaphoreType.DMA((n_w * (N_DEV - 1),)), pltpu.SemaphoreType.DMA((n_w * (N_DEV - 1),)),
            pltpu.SemaphoreType.DMA((n_w,))] if n_w else []
    outs = pl.pallas_call(
        body, name=name, grid=(n_steps,), in_specs=in_specs + [row, vec, row] + [hbm] * n_w,
        out_specs=[row, vec] + [hbm] * n_w,
        out_shape=[jax.ShapeDtypeStruct((m, d), F32), jax.ShapeDtypeStruct((1, d), F32)]
        + [jax.ShapeDtypeStruct(gr.shape, gr.dtype) for gr in grads],
        scratch_shapes=sems, compiler_params=_params(1),
    )(*args, x, g, res, *grads)
    return outs[0], outs[1], list(outs[2:])


SMALL_LAYOUT = ((0, 0, 0, 0, D_MODEL), (1, 0, 1, 0, GROUP), (1, 1, 1, GROUP, GROUP), (2, 0, 2, 0, GROUP),
                (3, 0, 2, GROUP, GROUP), (4, 0, 3, 0, D_MODEL), (5, 0, 4, 0, D_MODEL))
LOSS_ROW = 5
N_SMALL = 6


def _small_step(grads, loss_part, ws, ms, vs):
    def body(*refs):
        g_in, loss_in = refs[:N_SMALL], refs[N_SMALL]
        params = [refs[1 + (q + 1) * N_SMALL:1 + (q + 2) * N_SMALL] for q in range(3)]
        o0 = 1 + 4 * N_SMALL
        outs = [refs[o0 + q * N_SMALL:o0 + (q + 1) * N_SMALL] for q in range(4)]
        loss_out = refs[o0 + 4 * N_SMALL]
        pack, land, wp, mp, vp, send_sems, recv_sems = refs[o0 + 4 * N_SMALL + 1:]

        def place(dst, srcs):
            dst[...] = jnp.zeros_like(dst)
            for p, sr, dr, dc, width in SMALL_LAYOUT:
                dst[dr:dr + 1, dc:dc + width] = srcs[p][sr:sr + 1, :]

        place(pack, g_in)
        pack[LOSS_ROW:LOSS_ROW + 1, 0:LANES] = loss_in[...]
        for dst, srcs in zip((wp, mp, vp), params):
            place(dst, srcs)

        x, y, c = _mesh_place()
        me = 4 * x + 2 * y + c
        land[me] = pack[...]
        sends = []
        for k in range(1, N_DEV):
            peer, _ = _peer(x, y, c, k)
            cp = _remote(pack, land.at[me], send_sems.at[k - 1], recv_sems.at[k - 1], peer)
            cp.start()
            sends.append(cp)
        for k in range(1, N_DEV):
            peer, pidx = _peer(x, y, c, k)
            _remote(pack, land.at[pidx], send_sems.at[k - 1], recv_sems.at[k - 1], peer).wait_recv()
        for cp in sends:
            cp.wait_send()

        g = land[0]
        for d in range(1, N_DEV):
            g = g + land[d]
        delta, nm, nv = _adam(wp[...], g, mp[...], vp[...])
        for val, out in zip((g, delta, nm, nv), outs):
            for p, sr, dr, dc, width in SMALL_LAYOUT:
                out[p][sr:sr + 1, :] = val[dr:dr + 1, dc:dc + width]
        loss_out[...] = g[LOSS_ROW:LOSS_ROW + 1, 0:LANES]

    vmem = pl.BlockSpec(memory_space=pltpu.VMEM)
    n_in = 1 + 4 * N_SMALL
    shapes = [jax.ShapeDtypeStruct(w.shape, F32) for w in ws]
    packed = pltpu.VMEM((SMALL_ROWS, D_MODEL), F32)
    outs = pl.pallas_call(
        body, name="small_step", in_specs=[vmem] * n_in, out_specs=[vmem] * (4 * N_SMALL + 1),
        out_shape=shapes * 4 + [jax.ShapeDtypeStruct((1, LANES), F32)],
        scratch_shapes=[packed, pltpu.VMEM((N_DEV, SMALL_ROWS, D_MODEL), F32), packed, packed, packed,
                        pltpu.SemaphoreType.DMA((N_DEV - 1,)), pltpu.SemaphoreType.DMA((N_DEV - 1,))],
    )(*grads, loss_part, *ws, *ms, *vs)
    return [outs[q * N_SMALL:(q + 1) * N_SMALL] for q in range(4)], outs[4 * N_SMALL]


def _adam(w, g, m, v):
    m = ADAM_B1 * m + (1.0 - ADAM_B1) * g
    v = ADAM_B2 * v + (1.0 - ADAM_B2) * (g * g)
    m_hat = m / (1.0 - ADAM_B1 ** ADAM_STEP)
    v_hat = v / (1.0 - ADAM_B2 ** ADAM_STEP)
    delta = -ADAM_LR * (m_hat / (jnp.sqrt(v_hat) + ADAM_EPS) + ADAM_WD * w)
    return delta, m, v


def _reduce_adamw(land, w, m, v, name, tr):
    _, rows, width = land.shape

    def body(land_ref, w_ref, m_ref, v_ref, g_ref, d_ref, nm_ref, nv_ref):
        g = land_ref[0].astype(F32)
        for d in range(1, N_DEV):
            g = g + land_ref[d].astype(F32)
        delta, nm, nv = _adam(w_ref[...], g, m_ref[...], v_ref[...])
        g_ref[...] = g
        d_ref[...] = delta
        nm_ref[...] = nm
        nv_ref[...] = nv

    row = pl.BlockSpec((tr, width), lambda i: (i, 0))
    out = jax.ShapeDtypeStruct((rows, width), F32)
    return pl.pallas_call(
        body, name=name, grid=(rows // tr,),
        in_specs=[pl.BlockSpec((N_DEV, tr, width), lambda i: (0, i, 0)), row, row, row],
        out_specs=[row, row, row, row], out_shape=[out, out, out, out],
        compiler_params=_params(1),
    )(land, w, m, v)


def kernel(x, mix_norm_g, w_in, lower_bounds, hgrn_norm_g, sb_norm_g, w_out, ffn_norm_g, w_gate, w_up, w_down, final_norm_g, loss_target, m_mix_norm_g, m_w_in, m_lower_bounds, m_hgrn_norm_g, m_sb_norm_g, m_w_out, m_ffn_norm_g, m_w_gate, m_w_up, m_w_down, m_final_norm_g, v_mix_norm_g, v_w_in, v_lower_bounds, v_hgrn_norm_g, v_sb_norm_g, v_w_out, v_ffn_norm_g, v_w_gate, v_w_up, v_w_down, v_final_norm_g):
    n_seq, seq, d = x.shape
    t = n_seq * seq
    x2d = x.reshape(t, d)
    tgt = loss_target.reshape(t, d)
    final_g = final_norm_g.reshape(1, d)
    col_sharded = (True, False, True, True, False)

    def as_rows(ws):
        return [w[0].T if tr else w[0] for w, tr in zip(ws, col_sharded)]

    big_w = as_rows([w_in, w_out, w_gate, w_up, w_down])
    big_m = as_rows([m_w_in, m_w_out, m_w_gate, m_w_up, m_w_down])
    big_v = as_rows([v_w_in, v_w_out, v_w_gate, v_w_up, v_w_down])

    sh_in, sh_out, sh_gate, sh_up, sh_down = _cast_shards(big_w)
    wt_in = _gather_w_in(sh_in).reshape(IN_COLS, d)
    h1 = _rmsnorm_fwd(x2d, mix_norm_g, "norm_mix")
    proj = _mm_nt([(h1, wt_in)], "proj", tk=896)
    mixed, oa_pre, states = _hgrn_fwd(proj, lower_bounds, hgrn_norm_g, n_seq, seq)
    mixed, ob_pre, ctot, gathered = _sb_fwd(proj, sb_norm_g, mixed, [sh_out, sh_gate, sh_up, sh_down], n_seq, seq)
    wf_out = gathered[0].reshape(d, d)
    wt_gate = gathered[1].reshape(D_FF, d)
    wt_up = gathered[2].reshape(D_FF, d)
    wf_down = gathered[3].reshape(D_FF, d)
    x1 = _mm_nn([(mixed, wf_out, None)], "mix_out", res=x2d)
    h2 = _rmsnorm_fwd(x1, ffn_norm_g, "norm_ffn")
    gate, up, ff = _ffn_up(h2, wt_gate, wt_up, "ffn_up")
    x2 = _mm_nn([(ff, wf_down, None)], "ffn_down", res=x1)
    dx2, dx2m, d_final_g, loss_part = _final_loss(x2, tgt, final_g, "loss_head")

    dgate, dup = _ffn_bwd_act(dx2m, wf_down, gate, up, "ffn_bwd_act")
    dw_down = _mm_tn(ff, dx2m, "dw_down", tk=1408, tn=1024).reshape(N_DEV, D_FF // N_DEV, d)
    dw_gate = _mm_tn(dgate, h2, "dw_gate", tk=1408, tn=1024).reshape(N_DEV, D_FF // N_DEV, d)
    dw_up = _mm_tn(dup, h2, "dw_up", tk=1408, tn=1024).reshape(N_DEV, D_FF // N_DEV, d)
    dx1, d_ffn_g, _ = _dh_norm_bwd([(dgate, wt_gate, None), (dup, wt_up, None)], x1, ffn_norm_g, dx2, [], "dh_ffn")
    dmix = _mm_nt([(dx1, wf_out)], "dmix")
    dw_out = _mm_tn(mixed, dx1, "dw_out", tk=512, tn=1024).reshape(N_DEV, d // N_DEV, d)
    dsq, dsk, dsv, d_sb_g, lands = _sb_bwd(proj, sb_norm_g, dmix, ob_pre, ctot, [dw_out, dw_gate, dw_up, dw_down],
                                            n_seq, seq)
    dhq, dhf, dhi, dhg, d_lb, d_hgrn_g = _hgrn_bwd(proj, lower_bounds, hgrn_norm_g, dmix, oa_pre, states, n_seq, seq)
    dproj = [dhq, dhf, dhi, dhg, dsq, dsk, dsv]
    dw_in = _dw_rows(dproj, h1, "dw_in").reshape(N_DEV, IN_COLS // N_DEV, d)
    dx, d_mix_g, (land_in,) = _dh_norm_bwd([(piece, wt_in, k) for k, piece in enumerate(dproj)], x2d, mix_norm_g,
                                           dx1, [dw_in], "dh_mix")

    tiles = {"in": 224, "out": 128, "gate": 176, "up": 176, "down": 176}
    big = [_reduce_adamw(land, w, m, v, "adamw_" + key, tr=tiles[key])
           for key, land, w, m, v in zip(tiles, [land_in] + lands, big_w, big_m, big_v)]
    big = [[r.T if tr else r for r in res] for res, tr in zip(big, col_sharded)]
    small, loss_row = _small_step(
        [d_mix_g, d_lb, d_hgrn_g, d_sb_g, d_ffn_g, d_final_g], loss_part,
        [mix_norm_g, lower_bounds, hgrn_norm_g, sb_norm_g, ffn_norm_g, final_g],
        [m_mix_norm_g, m_lower_bounds, m_hgrn_norm_g, m_sb_norm_g, m_ffn_norm_g, m_final_norm_g.reshape(1, d)],
        [v_mix_norm_g, v_lower_bounds, v_hgrn_norm_g, v_sb_norm_g, v_ffn_norm_g, v_final_norm_g.reshape(1, d)])

    outs = [loss_row[0, 0], dx.reshape(n_seq, seq, d)]
    for q in range(4):
        b_in, b_out, b_gate, b_up, b_down = [res[q][None] for res in big]
        s_mix, s_lb, s_hgrn, s_sb, s_ffn, s_final = small[q]
        outs += [s_mix, b_in, s_lb, s_hgrn, s_sb, b_out, s_ffn, b_gate, b_up, b_down, s_final.reshape(d)]
    return tuple(outs)
```

```python
import functools
import math

import jax
import jax.numpy as jnp
from jax import lax
from jax.experimental import pallas as pl
from jax.experimental.pallas import tpu as pltpu

F32 = jnp.float32
BF16 = jnp.bfloat16
MXU_DTYPE = BF16

EPS = 1e-6
D_MODEL = 1024
N_HEADS = 8
D_HEAD = 64
GROUP = N_HEADS * D_HEAD
IN_COLS = 7 * GROUP
D_FF = 2816
CHUNK = 64
LANES = 128
N_PAIRS = GROUP // LANES
SUPER = 256
SB_BLOCK = 256
N_DEV = 8

ADAM_LR = 0.001
ADAM_B1 = 0.9
ADAM_B2 = 0.999
ADAM_EPS = 1e-08
ADAM_WD = 0.01
ADAM_STEP = 10

SMALL_ROWS = 8
FF_TILE = D_FF // 2

VMEM_LIMIT = 48 * 1024 * 1024


def _params(n_axes, vmem=VMEM_LIMIT):
    return pltpu.CompilerParams(dimension_semantics=("arbitrary",) * n_axes, vmem_limit_bytes=vmem)


def _dot(a, b):
    return jnp.dot(a.astype(MXU_DTYPE), b.astype(MXU_DTYPE), preferred_element_type=F32)


def _dot_nt(a, b):
    return lax.dot_general(a.astype(MXU_DTYPE), b.astype(MXU_DTYPE), (((1,), (1,)), ((), ())),
                           preferred_element_type=F32)


def _dot_tn(a, b):
    return lax.dot_general(a.astype(MXU_DTYPE), b.astype(MXU_DTYPE), (((0,), (0,)), ((), ())),
                           preferred_element_type=F32)


def _split(x, parts):
    out, r = [], x
    for _ in range(parts):
        h = r.astype(BF16)
        out.append(h)
        r = r - h.astype(F32)
    return out


def _rsum_right(x, u, parts):
    acc = None
    for h in _split(x, parts):
        d = jnp.dot(h, u, preferred_element_type=F32)
        acc = d if acc is None else acc + d
    return acc


def _rsum_left(u, x, parts):
    acc = None
    for h in _split(x, parts):
        d = jnp.dot(u, h, preferred_element_type=F32)
        acc = d if acc is None else acc + d
    return acc


def _ones_where(mask):
    return jnp.where(mask, 1.0, 0.0).astype(BF16)


def _sigmoid(x):
    return 1.0 / (1.0 + jnp.exp(-x))


def _softplus(x):
    return jnp.maximum(x, 0.0) + jnp.log(1.0 + jnp.exp(-jnp.abs(x)))


def _head_masks():
    lane = lax.broadcasted_iota(jnp.int32, (1, LANES), 1)
    return [jnp.where(lane < D_HEAD, 1.0, 0.0), jnp.where(lane >= D_HEAD, 1.0, 0.0)]


def _head_rstd(o, masks):
    sq = o * o
    r = None
    for m in masks:
        ms = jnp.sum(sq * m, axis=1, keepdims=True) * (1.0 / D_HEAD)
        t = lax.rsqrt(ms + EPS) * m
        r = t if r is None else r + t
    return r


def _head_mean(t, masks):
    out = None
    for m in masks:
        v = jnp.sum(t * m, axis=1, keepdims=True) * (1.0 / D_HEAD) * m
        out = v if out is None else out + v
    return out


def _rmsnorm_fwd(x, g, name):
    t, d = x.shape
    tm = min(512, t)

    def body(x_ref, g_ref, o_ref):
        xv = x_ref[...]
        r = lax.rsqrt(jnp.mean(xv * xv, axis=-1, keepdims=True) + EPS)
        o_ref[...] = (xv * r * g_ref[...]).astype(o_ref.dtype)

    return pl.pallas_call(
        body, name=name, grid=(t // tm,),
        in_specs=[pl.BlockSpec((tm, d), lambda i: (i, 0)), pl.BlockSpec((1, d), lambda i: (0, 0))],
        out_specs=pl.BlockSpec((tm, d), lambda i: (i, 0)),
        out_shape=jax.ShapeDtypeStruct((t, d), MXU_DTYPE),
        compiler_params=_params(1),
    )(x, g)


def _mix_out_norm(a, w, res, g, name, tm=512):
    m, k = a.shape
    d = w.shape[1]
    tm = min(tm, m)

    def body(a_ref, w_ref, res_ref, g_ref, x_ref, h_ref):
        xv = res_ref[...] + _dot(a_ref[...], w_ref[...])
        x_ref[...] = xv
        r = lax.rsqrt(jnp.mean(xv * xv, axis=-1, keepdims=True) + EPS)
        h_ref[...] = (xv * r * g_ref[...]).astype(h_ref.dtype)

    row = pl.BlockSpec((tm, d), lambda i: (i, 0))
    return pl.pallas_call(
        body, name=name, grid=(m // tm,),
        in_specs=[pl.BlockSpec((tm, k), lambda i: (i, 0)), pl.BlockSpec((k, d), lambda i: (0, 0)), row,
                  pl.BlockSpec((1, d), lambda i: (0, 0))],
        out_specs=[row, row],
        out_shape=[jax.ShapeDtypeStruct((m, d), F32), jax.ShapeDtypeStruct((m, d), MXU_DTYPE)],
        compiler_params=_params(1),
    )(a, w, res, g)


def _ffn_down_loss(a, w, res, target, g, name, tm=512):
    m, k = a.shape
    d = w.shape[1]
    tm = min(tm, m)

    def body(a_ref, w_ref, res_ref, t_ref, g_ref, dx_ref, dxm_ref, dg_ref, loss_ref):
        xv = res_ref[...] + _dot(a_ref[...], w_ref[...])
        gv = g_ref[...]
        r = lax.rsqrt(jnp.mean(xv * xv, axis=-1, keepdims=True) + EPS)
        xh = xv * r
        e = xh * gv - t_ref[...]
        dy = e * (1.0 / d)
        dxh = dy * gv
        dxv = r * (dxh - xh * jnp.mean(dxh * xh, axis=-1, keepdims=True))
        dx_ref[...] = dxv
        dxm_ref[...] = dxv.astype(dxm_ref.dtype)

        @pl.when(pl.program_id(0) == 0)
        def _():
            dg_ref[...] = jnp.zeros_like(dg_ref)
            loss_ref[...] = jnp.zeros_like(loss_ref)

        dg_ref[...] += jnp.sum(dy * xh, axis=0, keepdims=True)
        part = 0.5 * jnp.sum(jnp.mean(e * e, axis=-1, keepdims=True), axis=0, keepdims=True)
        loss_ref[...] += jnp.broadcast_to(part, loss_ref.shape)

    row = pl.BlockSpec((tm, d), lambda i: (i, 0))
    vec = pl.BlockSpec((1, d), lambda i: (0, 0))
    return pl.pallas_call(
        body, name=name, grid=(m // tm,),
        in_specs=[pl.BlockSpec((tm, k), lambda i: (i, 0)), pl.BlockSpec((k, d), lambda i: (0, 0)), row, row, vec],
        out_specs=[row, row, vec, pl.BlockSpec((1, LANES), lambda i: (0, 0))],
        out_shape=[jax.ShapeDtypeStruct((m, d), F32), jax.ShapeDtypeStruct((m, d), MXU_DTYPE),
                   jax.ShapeDtypeStruct((1, d), F32), jax.ShapeDtypeStruct((1, LANES), F32)],
        compiler_params=_params(1),
    )(a, w, res, target, g)


def _mm_nt(a, w, name, rows=None, out_dtype=F32, tm=512, tk=512):
    m, n = a.shape
    row0, k = rows or (0, w.shape[0])
    tm, tk = min(tm, m), min(tk, k)
    first = row0 // tk

    def body(a_ref, w_ref, o_ref):
        o_ref[...] = _dot_nt(a_ref[...], w_ref[...]).astype(o_ref.dtype)

    return pl.pallas_call(
        body, name=name, grid=(k // tk, m // tm),
        in_specs=[pl.BlockSpec((tm, n), lambda j, i: (i, 0)), pl.BlockSpec((tk, n), lambda j, i: (first + j, 0))],
        out_specs=pl.BlockSpec((tm, tk), lambda j, i: (i, j)),
        out_shape=jax.ShapeDtypeStruct((m, k), out_dtype),
        compiler_params=_params(2),
    )(a, w)


def _mm_tn(a, b, name, tk, tn, tt=512, out_dtype=BF16):
    t, k = a.shape
    n = b.shape[1]
    tt = min(tt, t)
    steps = t // tt

    def body(a_ref, b_ref, o_ref, acc):
        s = pl.program_id(2)

        @pl.when(s == 0)
        def _():
            acc[...] = jnp.zeros_like(acc)

        acc[...] += _dot_tn(a_ref[...], b_ref[...])

        @pl.when(s == steps - 1)
        def _():
            o_ref[...] = acc[...].astype(o_ref.dtype)

    return pl.pallas_call(
        body, name=name, grid=(k // tk, n // tn, steps),
        in_specs=[pl.BlockSpec((tt, tk), lambda i, j, s: (s, i)), pl.BlockSpec((tt, tn), lambda i, j, s: (s, j))],
        out_specs=pl.BlockSpec((tk, tn), lambda i, j, s: (i, j)),
        out_shape=jax.ShapeDtypeStruct((k, n), out_dtype),
        scratch_shapes=[pltpu.VMEM((tk, tn), F32)],
        compiler_params=_params(3),
    )(a, b)


def _ffn_up(h, wg_t, wu_t, name, tm=512, tn=FF_TILE):
    m, k = h.shape
    n = wg_t.shape[0]
    tm = min(tm, m)

    def body(h_ref, wg_ref, wu_ref, gate_ref, up_ref, ff_ref):
        hv = h_ref[...]
        gate = _dot_nt(hv, wg_ref[...])
        up = _dot_nt(hv, wu_ref[...])
        gate_ref[...] = gate.astype(gate_ref.dtype)
        up_ref[...] = up.astype(up_ref.dtype)
        ff_ref[...] = (gate * _sigmoid(gate) * up).astype(ff_ref.dtype)

    wspec = pl.BlockSpec((tn, k), lambda j, i: (j, 0))
    ospec = pl.BlockSpec((tm, tn), lambda j, i: (i, j))
    return pl.pallas_call(
        body, name=name, grid=(n // tn, m // tm),
        in_specs=[pl.BlockSpec((tm, k), lambda j, i: (i, 0)), wspec, wspec],
        out_specs=[ospec, ospec, ospec],
        out_shape=[jax.ShapeDtypeStruct((m, n), MXU_DTYPE)] * 3,
        compiler_params=_params(2),
    )(h, wg_t, wu_t)


def _ffn_bwd_act(dx, wd, gate, up, name, tm=512, tn=FF_TILE):
    m, k = dx.shape
    n = wd.shape[0]
    tm = min(tm, m)

    def body(dx_ref, wd_ref, gate_ref, up_ref, dgate_ref, dup_ref):
        dff = _dot_nt(dx_ref[...], wd_ref[...])
        gate = gate_ref[...].astype(F32)
        sg = _sigmoid(gate)
        dgate_ref[...] = (dff * up_ref[...].astype(F32) * sg * (1.0 + gate * (1.0 - sg))).astype(dgate_ref.dtype)
        dup_ref[...] = (dff * gate * sg).astype(dup_ref.dtype)

    ospec = pl.BlockSpec((tm, tn), lambda j, i: (i, j))
    return pl.pallas_call(
        body, name=name, grid=(n // tn, m // tm),
        in_specs=[pl.BlockSpec((tm, k), lambda j, i: (i, 0)), pl.BlockSpec((tn, k), lambda j, i: (j, 0)),
                  ospec, ospec],
        out_specs=[ospec, ospec],
        out_shape=[jax.ShapeDtypeStruct((m, n), MXU_DTYPE), jax.ShapeDtypeStruct((m, n), MXU_DTYPE)],
        compiler_params=_params(2),
    )(dx, wd, gate, up)


def _chunk_masks():
    r = lax.broadcasted_iota(jnp.int32, (SUPER, SUPER), 0)
    c = lax.broadcasted_iota(jnp.int32, (SUPER, SUPER), 1)
    same = jnp.right_shift(r, 6) == jnp.right_shift(c, 6)
    lower = jnp.logical_and(same, c <= r)
    upper = jnp.logical_and(same, c >= r)
    return same, lower, upper


def _head_block_mask():
    r = lax.broadcasted_iota(jnp.int32, (LANES, LANES), 0)
    c = lax.broadcasted_iota(jnp.int32, (LANES, LANES), 1)
    return jnp.where(jnp.right_shift(r, 6) == jnp.right_shift(c, 6), 1.0, 0.0)


def _lower_bound(lb_raw):
    return 1.0 / (1.0 + jnp.exp(lb_raw[1:2, :] - lb_raw[0:1, :]))


PER_SUPER = SUPER // CHUNK
CHUNK_ROWS = [slice(c * CHUNK, (c + 1) * CHUNK) for c in range(PER_SUPER)]


def _over_chunks(rows):
    return jnp.concatenate([jnp.broadcast_to(r, (CHUNK, LANES)) for r in rows], axis=0)


def _hgrn_gates(q, hf, lb, lower_b):
    sig = _sigmoid(hf)
    f = lb + (1.0 - lb) * sig
    k = 1.0 - f
    lf = jnp.log(f)
    b = _rsum_left(lower_b, lf, 2)
    ends = [b[cr.stop - 1:cr.stop, :] for cr in CHUNK_ROWS]
    eb = jnp.exp(b)
    enb = jnp.exp(-b)
    edb = jnp.exp(_over_chunks(ends) - b)
    decs = [jnp.exp(e) for e in ends]
    return sig, f, k, decs, eb, enb, edb, q * eb, k * enb, k * edb


def _hgrn_fwd(proj, lower_bounds, norm_g, n_seq, seq):
    t = n_seq * seq
    n_super = seq // SUPER
    n_chunks = seq // CHUNK

    def body(q_ref, f_ref, i_ref, g_ref, lb_ref, ng_ref, out_ref, opre_ref, st_ref):
        masks = _head_masks()
        _, lower, _ = _chunk_masks()
        lower_b = _ones_where(lower)
        bd = _head_block_mask()
        lb = _lower_bound(lb_ref[...])
        ng = ng_ref[...]

        def step(sb, st):
            rows = pl.ds(pl.multiple_of(sb * SUPER, SUPER), SUPER)
            q, hf, v, hg = q_ref[rows, :], f_ref[rows, :], i_ref[rows, :], g_ref[rows, :]
            _, _, _, decs, _, _, _, qe, ke, kd = _hgrn_gates(q, hf, lb, lower_b)
            scores = [_dot_nt(qe * m, ke) for m in masks]
            updates = [_dot_tn(v[cr], kd[cr]) for cr in CHUNK_ROWS]
            states = [st]
            for dec, upd in zip(decs, updates):
                states.append(states[-1] * dec + bd * upd)
            for c in range(PER_SUPER):
                st_ref[0, 0, sb * PER_SUPER + c] = states[c]
            intra = [_dot(jnp.where(lower, p, 0.0), v) for p in scores]
            inter = [_dot_nt(qe[cr], s) for cr, s in zip(CHUNK_ROWS, states)]
            o = intra[0] * masks[0] + intra[1] * masks[1] + jnp.concatenate(inter, axis=0)
            opre_ref[rows, :] = o
            on = o * _head_rstd(o, masks) * ng
            out_ref[rows, :] = (on * hg * _sigmoid(hg)).astype(out_ref.dtype)
            return states[-1]

        lax.fori_loop(0, n_super, step, jnp.zeros((LANES, LANES), F32))

    def col(k):
        return pl.BlockSpec((seq, LANES), lambda p, b: (b, k * N_PAIRS + p))

    vec = lambda rows: pl.BlockSpec((rows, LANES), lambda p, b: (0, p))
    ospec = pl.BlockSpec((seq, LANES), lambda p, b: (b, p))
    return pl.pallas_call(
        body, name="hgrn_fwd", grid=(N_PAIRS, n_seq),
        in_specs=[col(0), col(1), col(2), col(3), vec(2), vec(1)],
        out_specs=[ospec, ospec,
                   pl.BlockSpec((1, 1, n_chunks, LANES, LANES), lambda p, b: (b, p, 0, 0, 0))],
        out_shape=[jax.ShapeDtypeStruct((t, 2 * GROUP), MXU_DTYPE), jax.ShapeDtypeStruct((t, GROUP), F32),
                   jax.ShapeDtypeStruct((n_seq, N_PAIRS, n_chunks, LANES, LANES), F32)],
        compiler_params=_params(2),
    )(proj, proj, proj, proj, lower_bounds, norm_g)


def _hgrn_bwd(proj, lower_bounds, norm_g, dmix, opre, states, n_seq, seq):
    t = n_seq * seq
    n_super = seq // SUPER
    n_chunks = seq // CHUNK
    per = SUPER // CHUNK

    def body(q_ref, f_ref, i_ref, g_ref, lb_ref, ng_ref, dm_ref, opre_ref, st_ref,
             dq_ref, df_ref, di_ref, dg_ref, dlb_ref, dng_ref):
        masks = _head_masks()
        _, lower, upper = _chunk_masks()
        lower_b, upper_b = _ones_where(lower), _ones_where(upper)
        bd = _head_block_mask()
        lb_raw = lb_ref[...]
        lb = _lower_bound(lb_raw)
        ng = ng_ref[...]

        @pl.when(pl.program_id(1) == 0)
        def _():
            dlb_ref[...] = jnp.zeros_like(dlb_ref)
            dng_ref[...] = jnp.zeros_like(dng_ref)

        def step(it, dst):
            sb = n_super - 1 - it
            rows = pl.ds(pl.multiple_of(sb * SUPER, SUPER), SUPER)
            q, hf, v, hg = q_ref[rows, :], f_ref[rows, :], i_ref[rows, :], g_ref[rows, :]
            sig, f, k, decs, eb, enb, edb, qe, ke, kd = _hgrn_gates(q, hf, lb, lower_b)
            o = opre_ref[rows, :]
            r = _head_rstd(o, masks)
            oh = o * r
            dm = dm_ref[rows, :]
            sg = _sigmoid(hg)
            dg_ref[rows, :] = (dm * oh * ng * sg * (1.0 + hg * (1.0 - sg))).astype(dg_ref.dtype)
            don = dm * hg * sg
            dng_ref[...] += jnp.sum(don * oh, axis=0, keepdims=True)
            doh = don * ng
            do = r * (doh - oh * _head_mean(doh * oh, masks))
            doms = [do * m for m in masks]
            qems = [qe * m for m in masks]
            scores = [_dot_nt(qem, ke) for qem in qems]
            dscores = [_dot_nt(dom, v) for dom in doms]
            prevs = [st_ref[0, 0, sb * per + c] for c in range(per)]
            dst_in = [_dot_tn(do[cr], qe[cr]) for cr in CHUNK_ROWS]
            dqe_i = [_dot(do[cr], prev) for cr, prev in zip(CHUNK_ROWS, prevs)]
            dsts = [None] * per
            for c in reversed(range(per)):
                dsts[c] = dst
                dst = bd * (dst * decs[c] + dst_in[c])
            ps = [jnp.where(lower, p, 0.0) for p in scores]
            dps = [jnp.where(lower, dp, 0.0) for dp in dscores]
            dqe_h = [_dot(dp, ke) for dp in dps]
            dke_h = [_dot_tn(dp, qem) for dp, qem in zip(dps, qems)]
            dv_h = [_dot_tn(p, dom) for p, dom in zip(ps, doms)]
            dus = [bd * d for d in dsts]
            dv_i = [_dot_nt(kd[cr], du) for cr, du in zip(CHUNK_ROWS, dus)]
            dkd_i = [_dot(v[cr], du) for cr, du in zip(CHUNK_ROWS, dus)]
            dqe = dqe_h[0] * masks[0] + dqe_h[1] * masks[1] + jnp.concatenate(dqe_i, axis=0)
            dke = dke_h[0] + dke_h[1]
            dv = dv_h[0] + dv_h[1] + jnp.concatenate(dv_i, axis=0)
            dkd = jnp.concatenate(dkd_i, axis=0)
            dk = dke * enb + dkd * edb
            db = dqe * qe - dke * ke - dkd * kd
            dkd_kd = dkd * kd
            dends = [jnp.sum(dkd_kd[cr], axis=0, keepdims=True)
                     + jnp.sum(dsts[c] * prevs[c], axis=0, keepdims=True) * decs[c]
                     for c, cr in enumerate(CHUNK_ROWS)]
            dlf = _rsum_left(upper_b, db, 2) + _over_chunks(dends)
            dfv = dlf / f - dk
            dq_ref[rows, :] = (dqe * eb).astype(dq_ref.dtype)
            di_ref[rows, :] = dv.astype(di_ref.dtype)
            df_ref[rows, :] = (dfv * (1.0 - lb) * sig * (1.0 - sig)).astype(df_ref.dtype)
            dlb = jnp.sum(dfv * (1.0 - sig), axis=0, keepdims=True)
            da0 = dlb * lb * (1.0 - lb)
            dlb_ref[0:1, :] += da0
            dlb_ref[1:2, :] -= da0
            return dst

        lax.fori_loop(0, n_super, step, jnp.zeros((LANES, LANES), F32))

    def col(k):
        return pl.BlockSpec((seq, LANES), lambda p, b: (b, k * N_PAIRS + p))

    vec = lambda rows: pl.BlockSpec((rows, LANES), lambda p, b: (0, p))
    ospec = pl.BlockSpec((seq, LANES), lambda p, b: (b, p))
    piece = jax.ShapeDtypeStruct((t, GROUP), MXU_DTYPE)
    return pl.pallas_call(
        body, name="hgrn_bwd", grid=(N_PAIRS, n_seq),
        in_specs=[col(0), col(1), col(2), col(3), vec(2), vec(1), ospec, ospec,
                  pl.BlockSpec((1, 1, n_chunks, LANES, LANES), lambda p, b: (b, p, 0, 0, 0))],
        out_specs=[ospec, ospec, ospec, ospec, vec(2), vec(1)],
        out_shape=[piece, piece, piece, piece,
                   jax.ShapeDtypeStruct((2, GROUP), F32), jax.ShapeDtypeStruct((1, GROUP), F32)],
        compiler_params=_params(2),
    )(proj, proj, proj, proj, lower_bounds, norm_g, dmix, opre, states)


SB_SCALE = 1.0 / math.sqrt(D_HEAD)


def _triangle(keep):
    row = lax.broadcasted_iota(jnp.int32, (2 * SB_BLOCK, SB_BLOCK), 0) % SB_BLOCK
    col = lax.broadcasted_iota(jnp.int32, (2 * SB_BLOCK, SB_BLOCK), 1)
    return _ones_where(keep(row, col))


def _tile_masks():
    r = lax.broadcasted_iota(jnp.int32, (SB_BLOCK, SB_BLOCK), 0)
    c = lax.broadcasted_iota(jnp.int32, (SB_BLOCK, SB_BLOCK), 1)
    return r, c


def _sb_fwd(proj, norm_g, mixed, shards, n_seq, seq):
    t = n_seq * seq
    nq = seq // SB_BLOCK
    q0, k0, v0 = 0, N_PAIRS, 2 * N_PAIRS
    n_w = len(shards)
    n_steps = N_PAIRS * n_seq * nq
    tri = _triangle(lambda row, col: row >= col)

    def body(q_ref, k_ref, v_ref, ng_ref, tri_ref, mixed_in, *rest):
        del mixed_in
        shard_refs = rest[:n_w]
        out_ref, opre_ref, ctot_ref = rest[n_w:n_w + 3]
        gathered = rest[n_w + 3:2 * n_w + 3]
        send_sems, recv_sems, local_sems = rest[2 * n_w + 3:]
        i = pl.program_id(2)
        step = (pl.program_id(0) * n_seq + pl.program_id(1)) * nq + i
        plan = _GatherPlan(shard_refs, gathered, send_sems, recv_sems, local_sems)

        @pl.when(step == 0)
        def _():
            plan.start()

        @pl.when(step == n_steps // 2)
        def _():
            plan.forward()

        masks = _head_masks()
        r, c = _tile_masks()
        strict = c < r
        suffix2 = tri_ref[...]
        qs = q_ref[...] * SB_SCALE
        qhs = [(qs * m).astype(MXU_DTYPE) for m in masks]

        def tiles(js, carry, diag):
            rows = [pl.ds(pl.multiple_of(j * SB_BLOCK, SB_BLOCK), SB_BLOCK) for j in js]
            ks = [k_ref[rw, :].astype(MXU_DTYPE) for rw in rows]
            vs = [v_ref[rw, :].astype(MXU_DTYPE) for rw in rows]
            zs = [[_dot_nt(qh, kj) for qh in qhs] for kj in ks]
            ccs = [[_softplus(z) for z in zt] for zt in zs]
            if diag:
                ccs = [[jnp.where(strict, cc, 0.0) for cc in ct] for ct in ccs]
            sums = [[jnp.dot(jnp.concatenate(_split(cc, 2), axis=1), suffix2, preferred_element_type=F32)
                     for cc in ct] for ct in ccs]
            out = []
            for h, (run, acc) in enumerate(carry):
                for t in range(len(js)):
                    a = jnp.exp(zs[t][h] - (sums[t][h] + run))
                    if diag:
                        a = jnp.where(strict, a, 0.0)
                    acc = acc + _dot(a, vs[t])
                    run = run + sums[t][h][:, 0:1]
                out.append((run, acc))
            return tuple(out)

        start = ((jnp.zeros((SB_BLOCK, 1), F32), jnp.zeros((SB_BLOCK, LANES), F32)),) * 2
        carry = tiles([i], start, True)
        carry = lax.fori_loop(0, i // 4, lambda s, cy: tiles([i - 1 - 4 * s - u for u in range(4)], cy, False), carry)
        carry = lax.fori_loop(0, (i % 4) // 2, lambda s, cy: tiles([(i % 4) - 1, (i % 4) - 2], cy, False), carry)
        carry = lax.fori_loop(0, i % 2, lambda s, cy: tiles([0], cy, False), carry)
        opre = carry[0][1] * masks[0] + carry[1][1] * masks[1]
        ctot = carry[0][0] * masks[0] + carry[1][0] * masks[1]
        opre_ref[...] = opre
        ctot_ref[...] = ctot
        out_ref[...] = (opre * _head_rstd(opre, masks) * ng_ref[...]).astype(out_ref.dtype)

        @pl.when(step == n_steps - 1)
        def _():
            plan.finish()

    qspec = pl.BlockSpec((SB_BLOCK, LANES), lambda p, b, i: (b * nq + i, q0 + p))
    ospec = pl.BlockSpec((SB_BLOCK, LANES), lambda p, b, i: (b * nq + i, p))
    hbm = pl.BlockSpec(memory_space=pltpu.HBM)
    outs = pl.pallas_call(
        body, name="sb_fwd", grid=(N_PAIRS, n_seq, nq),
        in_specs=[qspec,
                  pl.BlockSpec((seq, LANES), lambda p, b, i: (b, k0 + p)),
                  pl.BlockSpec((seq, LANES), lambda p, b, i: (b, v0 + p)),
                  pl.BlockSpec((1, LANES), lambda p, b, i: (0, p)),
                  pl.BlockSpec(tri.shape, lambda p, b, i: (0, 0)), hbm] + [hbm] * n_w,
        out_specs=[pl.BlockSpec((SB_BLOCK, LANES), lambda p, b, i: (b * nq + i, N_PAIRS + p)), ospec, ospec]
        + [hbm] * n_w,
        out_shape=[jax.ShapeDtypeStruct(mixed.shape, mixed.dtype), jax.ShapeDtypeStruct((t, GROUP), F32),
                   jax.ShapeDtypeStruct((t, GROUP), F32)]
        + [jax.ShapeDtypeStruct((N_DEV,) + s.shape, s.dtype) for s in shards],
        scratch_shapes=[pltpu.SemaphoreType.DMA((n_w * _GatherPlan.COPIES,)),
                        pltpu.SemaphoreType.DMA((n_w * _GatherPlan.COPIES,)), pltpu.SemaphoreType.DMA((n_w,))],
        input_output_aliases={5: 0},
        compiler_params=_params(3),
    )(proj, proj, proj, norm_g, tri, mixed, *shards)
    return outs[0], outs[1], outs[2], list(outs[3:])


def _sb_bwd(proj, norm_g, dmix, opre, ctot, grads, n_seq, seq):
    t = n_seq * seq
    nq = seq // SB_BLOCK
    q0, k0, v0 = 0, N_PAIRS, 2 * N_PAIRS
    n_w = len(grads)
    n_steps = N_PAIRS * n_seq * nq
    tri = _triangle(lambda row, col: row <= col)

    def body(q_ref, k_ref, v_ref, ng_ref, tri_ref, dm_ref, opre_ref, ctot_ref, *rest):
        grad_refs = rest[:n_w]
        dq_ref, dk_ref, dv_ref, dng_ref = rest[n_w:n_w + 4]
        lands = rest[n_w + 4:2 * n_w + 4]
        dk_acc, dv_acc, send_sems, recv_sems, local_sems = rest[2 * n_w + 4:]
        p_id, b_id, i = pl.program_id(0), pl.program_id(1), pl.program_id(2)
        step = (p_id * n_seq + b_id) * nq + i
        plan = _ScatterPlan(grad_refs, lands, send_sems, recv_sems, local_sems)

        @pl.when(step == 0)
        def _():
            plan.start()

        masks = _head_masks()
        r, c = _tile_masks()
        strict = c < r
        upto2 = tri_ref[...]
        upto = tri_ref[0:SB_BLOCK, :]

        def prefix(x):
            return jnp.dot(jnp.concatenate(_split(x, 2), axis=1), upto2, preferred_element_type=F32)

        @pl.when(i == 0)
        def _():
            dk_acc[...] = jnp.zeros_like(dk_acc)
            dv_acc[...] = jnp.zeros_like(dv_acc)

        @pl.when(jnp.logical_and(b_id == 0, i == 0))
        def _():
            dng_ref[...] = jnp.zeros_like(dng_ref)

        o = opre_ref[...]
        rs = _head_rstd(o, masks)
        oh = o * rs
        dm = dm_ref[...]
        dng_ref[...] += jnp.sum(dm * oh, axis=0, keepdims=True)
        doh = dm * ng_ref[...]
        do = rs * (doh - oh * _head_mean(doh * oh, masks))

        qs = q_ref[...] * SB_SCALE
        ct = ctot_ref[...]
        qhs = [(qs * m).astype(MXU_DTYPE) for m in masks]
        doms = [(do * m).astype(MXU_DTYPE) for m in masks]
        qhts = [(qs * m).T.astype(MXU_DTYPE) for m in masks]
        domts = [(do * m).T.astype(MXU_DTYPE) for m in masks]
        totals = [jnp.max(ct * m, axis=1, keepdims=True) for m in masks]
        heads = range(len(masks))

        def tiles(js, carry, diag):
            nt = range(len(js))
            rows = [pl.ds(pl.multiple_of(j * SB_BLOCK, SB_BLOCK), SB_BLOCK) for j in js]
            ks = [k_ref[rw, :].astype(MXU_DTYPE) for rw in rows]
            vs = [v_ref[rw, :].astype(MXU_DTYPE) for rw in rows]
            zs = [[_dot_nt(qhs[h], ks[t]) for h in heads] for t in nt]
            das = [[_dot_nt(doms[h], vs[t]) for h in heads] for t in nt]
            sps = [[_softplus(z) for z in zt] for zt in zs]
            lsigs = [[z - sp for z, sp in zip(zt, st)] for zt, st in zip(zs, sps)]
            sigs = [[jnp.exp(ls) for ls in lt] for lt in lsigs]
            ccs = [[jnp.where(strict, sp, 0.0) for sp in st] for st in sps] if diag else sps
            pres = [[prefix(cc) for cc in ct] for ct in ccs]
            out = []
            for h in heads:
                pc, pdl, dq_h = carry[h]
                for t in nt:
                    a = jnp.exp(lsigs[t][h] + pres[t][h] - (totals[h] - pc))
                    if diag:
                        a = jnp.where(strict, a, 0.0)
                    dl = a * das[t][h]
                    dv_acc[js[t]] += _dot(domts[h], a)
                    dpre = jnp.dot(dl.astype(BF16), upto, preferred_element_type=F32)
                    dz = dl - sigs[t][h] * (pdl + dpre)
                    if diag:
                        dz = jnp.where(strict, dz, 0.0)
                    dzb = dz.astype(MXU_DTYPE)
                    dq_h = dq_h + _dot(dzb, ks[t])
                    dk_acc[js[t]] += _dot(qhts[h], dzb)
                    pc = pc + pres[t][h][:, SB_BLOCK - 1:SB_BLOCK]
                    pdl = pdl + dpre[:, SB_BLOCK - 1:SB_BLOCK]
                out.append((pc, pdl, dq_h))
            return tuple(out)

        zero = jnp.zeros((SB_BLOCK, 1), F32)
        carry = ((zero, zero, jnp.zeros((SB_BLOCK, LANES), F32)),) * 2
        carry = lax.fori_loop(0, i // 2, lambda s, cy: tiles([2 * s, 2 * s + 1], cy, False), carry)
        carry = lax.fori_loop(0, i % 2, lambda s, cy: tiles([i - 1], cy, False), carry)
        carry = tiles([i], carry, True)
        dq = carry[0][2] * masks[0] + carry[1][2] * masks[1]
        dq_ref[...] = (dq * SB_SCALE).astype(dq_ref.dtype)

        @pl.when(i == nq - 1)
        def _():
            for j in range(nq):
                tile_rows = slice(j * SB_BLOCK, (j + 1) * SB_BLOCK)
                dk_ref[tile_rows, :] = dk_acc[j].T.astype(dk_ref.dtype)
                dv_ref[tile_rows, :] = dv_acc[j].T.astype(dv_ref.dtype)

        @pl.when(step == n_steps - 1)
        def _():
            plan.finish()

    qspec = pl.BlockSpec((SB_BLOCK, LANES), lambda p, b, i: (b * nq + i, q0 + p))
    ospec = pl.BlockSpec((SB_BLOCK, LANES), lambda p, b, i: (b * nq + i, p))
    dmspec = pl.BlockSpec((SB_BLOCK, LANES), lambda p, b, i: (b * nq + i, N_PAIRS + p))
    full = lambda k: pl.BlockSpec((seq, LANES), lambda p, b, i: (b, k + p))
    vec = pl.BlockSpec((1, LANES), lambda p, b, i: (0, p))
    hbm = pl.BlockSpec(memory_space=pltpu.HBM)
    piece = jax.ShapeDtypeStruct((t, GROUP), MXU_DTYPE)
    outs = pl.pallas_call(
        body, name="sb_bwd", grid=(N_PAIRS, n_seq, nq),
        in_specs=[qspec, full(k0), full(v0), vec, pl.BlockSpec(tri.shape, lambda p, b, i: (0, 0)), dmspec, ospec, ospec]
        + [hbm] * n_w,
        out_specs=[ospec, full(0), full(0), vec] + [hbm] * n_w,
        out_shape=[piece, piece, piece, jax.ShapeDtypeStruct((1, GROUP), F32)]
        + [jax.ShapeDtypeStruct(g.shape, g.dtype) for g in grads],
        scratch_shapes=[pltpu.VMEM((nq, LANES, SB_BLOCK), F32), pltpu.VMEM((nq, LANES, SB_BLOCK), F32),
                        pltpu.SemaphoreType.DMA((n_w * (N_DEV - 1),)), pltpu.SemaphoreType.DMA((n_w * (N_DEV - 1),)),
                        pltpu.SemaphoreType.DMA((n_w,))],
        compiler_params=_params(3),
    )(proj, proj, proj, norm_g, tri, dmix, opre, ctot, *grads)
    return outs[0], outs[1], outs[2], outs[3], list(outs[4:])


def _mesh_place():
    x, y, c = lax.axis_index("x"), lax.axis_index("y"), lax.axis_index("c")
    return x, y, c


def _peer(x, y, c, k):
    px = lax.rem(x + ((k >> 2) & 1), 2)
    py = lax.rem(y + ((k >> 1) & 1), 2)
    pc = lax.rem(c + (k & 1), 2)
    return (px, py, pc), 4 * px + 2 * py + pc


def _remote(src, dst, send_sem, recv_sem, to):
    return pltpu.make_async_remote_copy(src_ref=src, dst_ref=dst, send_sem=send_sem, recv_sem=recv_sem,
                                        device_id=to, device_id_type=pl.DeviceIdType.MESH)


class _GatherPlan:
    COPIES = 7

    def __init__(self, shards, gathered, send_sems, recv_sems, local_sems):
        x, y, c = _mesh_place()
        self.c = c
        self.me = (x, y, c)
        self.sibling = (x, y, 1 - c)
        self.chips = [(1 - x, y), (x, 1 - y), (1 - x, 1 - y)]
        self.tensors = list(zip(shards, gathered))
        self.send_sems, self.recv_sems, self.local_sems = send_sems, recv_sems, local_sems

    @staticmethod
    def _index(place):
        return 4 * place[0] + 2 * place[1] + place[2]

    def _copy(self, w, k, block, to, own=False):
        shard, gathered = self.tensors[w]
        slot = gathered.at[self._index(block)]
        n = w * self.COPIES + k
        return _remote(shard if own else slot, slot, self.send_sems.at[n], self.recv_sems.at[n], to)

    def _local(self, w):
        shard, gathered = self.tensors[w]
        return pltpu.make_async_copy(shard, gathered.at[self._index(self.me)], self.local_sems.at[w])

    def _first(self, w):
        return [self._copy(w, 0, self.me, self.sibling, own=True)] + [
            self._copy(w, 1 + j, self.me, (*chip, self.c), own=True) for j, chip in enumerate(self.chips)]

    def _passed(self, w):
        return [self._copy(w, 4 + j, (*chip, self.c), self.sibling) for j, chip in enumerate(self.chips)]

    def start(self):
        for w in range(len(self.tensors)):
            self._local(w).start()
            for cp in self._first(w):
                cp.start()

    def forward(self):
        for w in range(len(self.tensors)):
            passed = self._passed(w)
            for j, chip in enumerate(self.chips):
                self._copy(w, 1 + j, (*chip, self.c), self.me).wait_recv()
                passed[j].start()

    def finish(self):
        for w in range(len(self.tensors)):
            self._copy(w, 0, self.sibling, self.me).wait_recv()
            for j, chip in enumerate(self.chips):
                self._copy(w, 4 + j, (*chip, 1 - self.c), self.me).wait_recv()
            for cp in self._first(w) + self._passed(w):
                cp.wait_send()
            self._local(w).wait()


class _ScatterPlan:
    def __init__(self, grads, lands, send_sems, recv_sems, local_sems):
        self.place = _mesh_place()
        x, y, c = self.place
        self.me = 4 * x + 2 * y + c
        self.tensors = list(zip(grads, lands))
        self.send_sems, self.recv_sems, self.local_sems = send_sems, recv_sems, local_sems

    def _copies(self, w):
        grad, land = self.tensors[w]
        out = []
        for k in range(1, N_DEV):
            peer, pidx = _peer(*self.place, k)
            n = w * (N_DEV - 1) + k - 1
            sems = (self.send_sems.at[n], self.recv_sems.at[n], peer)
            out.append((_remote(grad.at[pidx], land.at[self.me], *sems), _remote(grad.at[pidx], land.at[pidx], *sems)))
        return out

    def _local(self, w):
        grad, land = self.tensors[w]
        return pltpu.make_async_copy(grad.at[self.me], land.at[self.me], self.local_sems.at[w])

    def start(self):
        for w in range(len(self.tensors)):
            self._local(w).start()
            for send, _ in self._copies(w):
                send.start()

    def finish(self):
        for w in range(len(self.tensors)):
            copies = self._copies(w)
            for _, arrival in copies:
                arrival.wait_recv()
            for send, _ in copies:
                send.wait_send()
            self._local(w).wait()


def _cast_shards(shards):
    def body(*refs):
        n = len(refs) // 2
        for src, dst in zip(refs[:n], refs[n:]):
            dst[...] = src[...].astype(dst.dtype)

    vmem = pl.BlockSpec(memory_space=pltpu.VMEM)
    return pl.pallas_call(
        body, name="cast_shards", in_specs=[vmem] * len(shards), out_specs=[vmem] * len(shards),
        out_shape=[jax.ShapeDtypeStruct(s.shape, BF16) for s in shards],
        compiler_params=pltpu.CompilerParams(vmem_limit_bytes=VMEM_LIMIT),
    )(*shards)


def _gather_w_in(shard):
    rows, cols = shard.shape

    def body(w_ref, out_ref, send_sems, recv_sems, local_sems):
        plan = _GatherPlan([w_ref], [out_ref], send_sems, recv_sems, local_sems)
        plan.start()
        plan.forward()
        plan.finish()

    vmem = pl.BlockSpec(memory_space=pltpu.VMEM)
    return pl.pallas_call(
        body, name="gather_w_in", in_specs=[vmem], out_specs=vmem,
        out_shape=jax.ShapeDtypeStruct((N_DEV, rows, cols), shard.dtype),
        scratch_shapes=[pltpu.SemaphoreType.DMA((_GatherPlan.COPIES,)), pltpu.SemaphoreType.DMA((_GatherPlan.COPIES,)),
                        pltpu.SemaphoreType.DMA((1,))],
        compiler_params=pltpu.CompilerParams(vmem_limit_bytes=VMEM_LIMIT),
    )(shard)


def _dw_rows(pieces, b, name, tn=512, tt=512):
    t, n = b.shape
    widths = [p.shape[1] for p in pieces]
    rows = sum(widths)
    tt = min(tt, t)
    steps = t // tt
    n_p = len(pieces)

    def body(*refs):
        piece_refs, b_ref, o_ref, acc = refs[:n_p], refs[n_p], refs[n_p + 1], refs[n_p + 2]
        s = pl.program_id(1)

        @pl.when(s == 0)
        def _():
            acc[...] = jnp.zeros_like(acc)

        bv = b_ref[...]
        off = 0
        for p_ref, width in zip(piece_refs, widths):
            acc[off:off + width, :] += _dot_tn(p_ref[...], bv)
            off += width

        @pl.when(s == steps - 1)
        def _():
            o_ref[...] = acc[...].astype(o_ref.dtype)

    return pl.pallas_call(
        body, name=name, grid=(n // tn, steps),
        in_specs=[pl.BlockSpec((tt, width), lambda j, s: (s, 0)) for width in widths]
        + [pl.BlockSpec((tt, tn), lambda j, s: (s, j))],
        out_specs=pl.BlockSpec((rows, tn), lambda j, s: (0, j)),
        out_shape=jax.ShapeDtypeStruct((rows, n), BF16),
        scratch_shapes=[pltpu.VMEM((rows, tn), F32)],
        compiler_params=_params(2),
    )(*pieces, b)


def _dh_norm_bwd(pairs, x, g, res, grads, name, tm=256):
    m, d = x.shape
    tm = min(tm, m)
    n_p, n_w = len(pairs), len(grads)
    n_steps = m // tm

    def body(*refs):
        x_ref, g_ref, res_ref = refs[2 * n_p:2 * n_p + 3]
        grad_refs = refs[2 * n_p + 3:2 * n_p + 3 + n_w]
        dx_ref, dg_ref = refs[2 * n_p + 3 + n_w:2 * n_p + 5 + n_w]
        lands = refs[2 * n_p + 5 + n_w:2 * n_p + 5 + 2 * n_w]
        sems = refs[2 * n_p + 5 + 2 * n_w:]
        step = pl.program_id(0)
        plan = _ScatterPlan(grad_refs, lands, *sems) if n_w else None

        @pl.when(step == 0)
        def _():
            dg_ref[...] = jnp.zeros_like(dg_ref)
            if plan:
                plan.start()

        dh = None
        for q in range(n_p):
            part = _dot(refs[2 * q][...], refs[2 * q + 1][...])
            dh = part if dh is None else dh + part
        xv = x_ref[...]
        r = lax.rsqrt(jnp.mean(xv * xv, axis=-1, keepdims=True) + EPS)
        xh = xv * r
        dxh = dh * g_ref[...]
        dx_ref[...] = res_ref[...] + r * (dxh - xh * jnp.mean(dxh * xh, axis=-1, keepdims=True))
        dg_ref[...] += jnp.sum(dh * xh, axis=0, keepdims=True)

        if plan:
            @pl.when(step == n_steps - 1)
            def _():
                plan.finish()

    in_specs, args = [], []
    for a, w, r in pairs:
        k = a.shape[1]
        in_specs += [pl.BlockSpec((tm, k), lambda i: (i, 0)),
                     pl.BlockSpec((k, d), functools.partial(lambda i, r: (r, 0), r=r or 0))]
        args += [a, w]
    row = pl.BlockSpec((tm, d), lambda i: (i, 0))
    vec = pl.BlockSpec((1, d), lambda i: (0, 0))
    hbm = pl.BlockSpec(memory_space=pltpu.HBM)
    sems = [pltpu.SemaphoreType.DMA((n_w * (N_DEV - 1),)), pltpu.SemaphoreType.DMA((n_w * (N_DEV - 1),)),
            pltpu.SemaphoreType.DMA((n_w,))] if n_w else []
    outs = pl.pallas_call(
        body, name=name, grid=(n_steps,), in_specs=in_specs + [row, vec, row] + [hbm] * n_w,
        out_specs=[row, vec] + [hbm] * n_w,
        out_shape=[jax.ShapeDtypeStruct((m, d), F32), jax.ShapeDtypeStruct((1, d), F32)]
        + [jax.ShapeDtypeStruct(gr.shape, gr.dtype) for gr in grads],
        scratch_shapes=sems, compiler_params=_params(1),
    )(*args, x, g, res, *grads)
    return outs[0], outs[1], list(outs[2:])


SMALL_LAYOUT = ((0, 0, 0, 0, D_MODEL), (1, 0, 1, 0, GROUP), (1, 1, 1, GROUP, GROUP), (2, 0, 2, 0, GROUP),
                (3, 0, 2, GROUP, GROUP), (4, 0, 3, 0, D_MODEL), (5, 0, 4, 0, D_MODEL))
LOSS_ROW = 5
N_SMALL = 6


def _small_step(grads, loss_part, ws, ms, vs):
    def body(*refs):
        g_in, loss_in = refs[:N_SMALL], refs[N_SMALL]
        params = [refs[1 + (q + 1) * N_SMALL:1 + (q + 2) * N_SMALL] for q in range(3)]
        o0 = 1 + 4 * N_SMALL
        outs = [refs[o0 + q * N_SMALL:o0 + (q + 1) * N_SMALL] for q in range(4)]
        loss_out = refs[o0 + 4 * N_SMALL]
        pack, land, wp, mp, vp, send_sems, recv_sems = refs[o0 + 4 * N_SMALL + 1:]

        def place(dst, srcs):
            dst[...] = jnp.zeros_like(dst)
            for p, sr, dr, dc, width in SMALL_LAYOUT:
                dst[dr:dr + 1, dc:dc + width] = srcs[p][sr:sr + 1, :]

        place(pack, g_in)
        pack[LOSS_ROW:LOSS_ROW + 1, 0:LANES] = loss_in[...]
        for dst, srcs in zip((wp, mp, vp), params):
            place(dst, srcs)

        x, y, c = _mesh_place()
        me = 4 * x + 2 * y + c
        land[me] = pack[...]
        sends = []
        for k in range(1, N_DEV):
            peer, _ = _peer(x, y, c, k)
            cp = _remote(pack, land.at[me], send_sems.at[k - 1], recv_sems.at[k - 1], peer)
            cp.start()
            sends.append(cp)
        for k in range(1, N_DEV):
            peer, pidx = _peer(x, y, c, k)
            _remote(pack, land.at[pidx], send_sems.at[k - 1], recv_sems.at[k - 1], peer).wait_recv()
        for cp in sends:
            cp.wait_send()

        g = land[0]
        for d in range(1, N_DEV):
            g = g + land[d]
        delta, nm, nv = _adam(wp[...], g, mp[...], vp[...])
        for val, out in zip((g, delta, nm, nv), outs):
            for p, sr, dr, dc, width in SMALL_LAYOUT:
                out[p][sr:sr + 1, :] = val[dr:dr + 1, dc:dc + width]
        loss_out[...] = g[LOSS_ROW:LOSS_ROW + 1, 0:LANES]

    vmem = pl.BlockSpec(memory_space=pltpu.VMEM)
    n_in = 1 + 4 * N_SMALL
    shapes = [jax.ShapeDtypeStruct(w.shape, F32) for w in ws]
    packed = pltpu.VMEM((SMALL_ROWS, D_MODEL), F32)
    outs = pl.pallas_call(
        body, name="small_step", in_specs=[vmem] * n_in, out_specs=[vmem] * (4 * N_SMALL + 1),
        out_shape=shapes * 4 + [jax.ShapeDtypeStruct((1, LANES), F32)],
        scratch_shapes=[packed, pltpu.VMEM((N_DEV, SMALL_ROWS, D_MODEL), F32), packed, packed, packed,
                        pltpu.SemaphoreType.DMA((N_DEV - 1,)), pltpu.SemaphoreType.DMA((N_DEV - 1,))],
    )(*grads, loss_part, *ws, *ms, *vs)
    return [outs[q * N_SMALL:(q + 1) * N_SMALL] for q in range(4)], outs[4 * N_SMALL]


def _adam(w, g, m, v):
    m = ADAM_B1 * m + (1.0 - ADAM_B1) * g
    v = ADAM_B2 * v + (1.0 - ADAM_B2) * (g * g)
    m_hat = m / (1.0 - ADAM_B1 ** ADAM_STEP)
    v_hat = v / (1.0 - ADAM_B2 ** ADAM_STEP)
    delta = -ADAM_LR * (m_hat / (jnp.sqrt(v_hat) + ADAM_EPS) + ADAM_WD * w)
    return delta, m, v


def _reduce_adamw(land, w, m, v, name, tr):
    _, rows, width = land.shape

    def body(land_ref, w_ref, m_ref, v_ref, g_ref, d_ref, nm_ref, nv_ref):
        g = land_ref[0].astype(F32)
        for d in range(1, N_DEV):
            g = g + land_ref[d].astype(F32)
        delta, nm, nv = _adam(w_ref[...], g, m_ref[...], v_ref[...])
        g_ref[...] = g
        d_ref[...] = delta
        nm_ref[...] = nm
        nv_ref[...] = nv

    row = pl.BlockSpec((tr, width), lambda i: (i, 0))
    out = jax.ShapeDtypeStruct((rows, width), F32)
    return pl.pallas_call(
        body, name=name, grid=(rows // tr,),
        in_specs=[pl.BlockSpec((N_DEV, tr, width), lambda i: (0, i, 0)), row, row, row],
        out_specs=[row, row, row, row], out_shape=[out, out, out, out],
        compiler_params=_params(1),
    )(land, w, m, v)


def kernel(x, mix_norm_g, w_in, lower_bounds, hgrn_norm_g, sb_norm_g, w_out, ffn_norm_g, w_gate, w_up, w_down, final_norm_g, loss_target, m_mix_norm_g, m_w_in, m_lower_bounds, m_hgrn_norm_g, m_sb_norm_g, m_w_out, m_ffn_norm_g, m_w_gate, m_w_up, m_w_down, m_final_norm_g, v_mix_norm_g, v_w_in, v_lower_bounds, v_hgrn_norm_g, v_sb_norm_g, v_w_out, v_ffn_norm_g, v_w_gate, v_w_up, v_w_down, v_final_norm_g):
    n_seq, seq, d = x.shape
    t = n_seq * seq
    x2d = x.reshape(t, d)
    tgt = loss_target.reshape(t, d)
    final_g = final_norm_g.reshape(1, d)
    col_sharded = (True, False, True, True, False)

    def as_rows(ws):
        return [w[0].T if tr else w[0] for w, tr in zip(ws, col_sharded)]

    big_w = as_rows([w_in, w_out, w_gate, w_up, w_down])
    big_m = as_rows([m_w_in, m_w_out, m_w_gate, m_w_up, m_w_down])
    big_v = as_rows([v_w_in, v_w_out, v_w_gate, v_w_up, v_w_down])

    sh_in, sh_out, sh_gate, sh_up, sh_down = _cast_shards(big_w)
    wt_in = _gather_w_in(sh_in).reshape(IN_COLS, d)
    h1 = _rmsnorm_fwd(x2d, mix_norm_g, "norm_mix")
    proj_h = _mm_nt(h1, wt_in, "proj_hgrn", rows=(0, 4 * GROUP), tk=1024)
    proj_s = _mm_nt(h1, wt_in, "proj_sb", rows=(4 * GROUP, 3 * GROUP), out_dtype=MXU_DTYPE, tk=512)
    mixed, oa_pre, states = _hgrn_fwd(proj_h, lower_bounds, hgrn_norm_g, n_seq, seq)
    mixed, ob_pre, ctot, gathered = _sb_fwd(proj_s, sb_norm_g, mixed, [sh_out, sh_gate, sh_up, sh_down], n_seq, seq)
    wf_out = gathered[0].reshape(d, d)
    wt_gate = gathered[1].reshape(D_FF, d)
    wt_up = gathered[2].reshape(D_FF, d)
    wf_down = gathered[3].reshape(D_FF, d)
    x1, h2 = _mix_out_norm(mixed, wf_out, x2d, ffn_norm_g, "mix_out")
    gate, up, ff = _ffn_up(h2, wt_gate, wt_up, "ffn_up")
    dx2, dx2m, d_final_g, loss_part = _ffn_down_loss(ff, wf_down, x1, tgt, final_g, "ffn_down_loss")

    dgate, dup = _ffn_bwd_act(dx2m, wf_down, gate, up, "ffn_bwd_act")
    dw_down = _mm_tn(ff, dx2m, "dw_down", tk=1408, tn=1024).reshape(N_DEV, D_FF // N_DEV, d)
    dw_gate = _mm_tn(dgate, h2, "dw_gate", tk=1408, tn=1024).reshape(N_DEV, D_FF // N_DEV, d)
    dw_up = _mm_tn(dup, h2, "dw_up", tk=1408, tn=1024).reshape(N_DEV, D_FF // N_DEV, d)
    dx1, d_ffn_g, _ = _dh_norm_bwd([(dgate, wt_gate, None), (dup, wt_up, None)], x1, ffn_norm_g, dx2, [], "dh_ffn")
    dmix = _mm_nt(dx1, wf_out, "dmix")
    dw_out = _mm_tn(mixed, dx1, "dw_out", tk=512, tn=1024).reshape(N_DEV, d // N_DEV, d)
    dsq, dsk, dsv, d_sb_g, lands = _sb_bwd(proj_s, sb_norm_g, dmix, ob_pre, ctot, [dw_out, dw_gate, dw_up, dw_down],
                                            n_seq, seq)
    dhq, dhf, dhi, dhg, d_lb, d_hgrn_g = _hgrn_bwd(proj_h, lower_bounds, hgrn_norm_g, dmix, oa_pre, states, n_seq,
                                                   seq)
    dproj = [dhq, dhf, dhi, dhg, dsq, dsk, dsv]
    dw_in = _dw_rows(dproj, h1, "dw_in").reshape(N_DEV, IN_COLS // N_DEV, d)
    dx, d_mix_g, (land_in,) = _dh_norm_bwd([(piece, wt_in, k) for k, piece in enumerate(dproj)], x2d, mix_norm_g,
                                           dx1, [dw_in], "dh_mix")

    tiles = {"in": 224, "out": 128, "gate": 176, "up": 176, "down": 176}
    big = [_reduce_adamw(land, w, m, v, "adamw_" + key, tr=tiles[key])
           for key, land, w, m, v in zip(tiles, [land_in] + lands, big_w, big_m, big_v)]
    big = [[r.T if tr else r for r in res] for res, tr in zip(big, col_sharded)]
    small, loss_row = _small_step(
        [d_mix_g, d_lb, d_hgrn_g, d_sb_g, d_ffn_g, d_final_g], loss_part,
        [mix_norm_g, lower_bounds, hgrn_norm_g, sb_norm_g, ffn_norm_g, final_g],
        [m_mix_norm_g, m_lower_bounds, m_hgrn_norm_g, m_sb_norm_g, m_ffn_norm_g, m_final_norm_g.reshape(1, d)],
        [v_mix_norm_g, v_lower_bounds, v_hgrn_norm_g, v_sb_norm_g, v_ffn_norm_g, v_final_norm_g.reshape(1, d)])

    outs = [loss_row[0, 0], dx.reshape(n_seq, seq, d)]
    for q in range(4):
        b_in, b_out, b_gate, b_up, b_down = [res[q][None] for res in big]
        s_mix, s_lb, s_hgrn, s_sb, s_ffn, s_final = small[q]
        outs += [s_mix, b_in, s_lb, s_hgrn, s_sb, b_out, s_ffn, b_gate, b_up, b_down, s_final.reshape(d)]
    return tuple(outs)
```

```python
import functools
import math

import jax
import jax.numpy as jnp
from jax import lax
from jax.experimental import pallas as pl
from jax.experimental.pallas import tpu as pltpu

F32 = jnp.float32
BF16 = jnp.bfloat16
MXU_DTYPE = BF16

EPS = 1e-6
D_MODEL = 1024
N_HEADS = 8
D_HEAD = 64
GROUP = N_HEADS * D_HEAD
IN_COLS = 7 * GROUP
D_FF = 2816
CHUNK = 64
LANES = 128
N_PAIRS = GROUP // LANES
SUPER = 256
SB_BLOCK = 256
N_DEV = 8

ADAM_LR = 0.001
ADAM_B1 = 0.9
ADAM_B2 = 0.999
ADAM_EPS = 1e-08
ADAM_WD = 0.01
ADAM_STEP = 10

SMALL_ROWS = 8
FF_TILE = D_FF // 2

VMEM_LIMIT = 48 * 1024 * 1024


def _params(n_axes, vmem=VMEM_LIMIT):
    return pltpu.CompilerParams(dimension_semantics=("arbitrary",) * n_axes, vmem_limit_bytes=vmem)


def _dot(a, b):
    return jnp.dot(a.astype(MXU_DTYPE), b.astype(MXU_DTYPE), preferred_element_type=F32)


def _dot_nt(a, b):
    return lax.dot_general(a.astype(MXU_DTYPE), b.astype(MXU_DTYPE), (((1,), (1,)), ((), ())),
                           preferred_element_type=F32)


def _dot_tn(a, b):
    return lax.dot_general(a.astype(MXU_DTYPE), b.astype(MXU_DTYPE), (((0,), (0,)), ((), ())),
                           preferred_element_type=F32)


def _split(x, parts):
    out, r = [], x
    for _ in range(parts):
        h = r.astype(BF16)
        out.append(h)
        r = r - h.astype(F32)
    return out


def _rsum_right(x, u, parts):
    acc = None
    for h in _split(x, parts):
        d = jnp.dot(h, u, preferred_element_type=F32)
        acc = d if acc is None else acc + d
    return acc


def _rsum_left(u, x, parts):
    acc = None
    for h in _split(x, parts):
        d = jnp.dot(u, h, preferred_element_type=F32)
        acc = d if acc is None else acc + d
    return acc


def _ones_where(mask):
    return jnp.where(mask, 1.0, 0.0).astype(BF16)


def _sigmoid(x):
    return 1.0 / (1.0 + jnp.exp(-x))


def _softplus(x):
    return jnp.maximum(x, 0.0) + jnp.log(1.0 + jnp.exp(-jnp.abs(x)))


def _head_masks():
    lane = lax.broadcasted_iota(jnp.int32, (1, LANES), 1)
    return [jnp.where(lane < D_HEAD, 1.0, 0.0), jnp.where(lane >= D_HEAD, 1.0, 0.0)]


def _head_rstd(o, masks):
    sq = o * o
    r = None
    for m in masks:
        ms = jnp.sum(sq * m, axis=1, keepdims=True) * (1.0 / D_HEAD)
        t = lax.rsqrt(ms + EPS) * m
        r = t if r is None else r + t
    return r


def _head_mean(t, masks):
    out = None
    for m in masks:
        v = jnp.sum(t * m, axis=1, keepdims=True) * (1.0 / D_HEAD) * m
        out = v if out is None else out + v
    return out


def _rmsnorm_fwd(x, g, name):
    t, d = x.shape
    tm = min(512, t)

    def body(x_ref, g_ref, o_ref):
        xv = x_ref[...]
        r = lax.rsqrt(jnp.mean(xv * xv, axis=-1, keepdims=True) + EPS)
        o_ref[...] = (xv * r * g_ref[...]).astype(o_ref.dtype)

    return pl.pallas_call(
        body, name=name, grid=(t // tm,),
        in_specs=[pl.BlockSpec((tm, d), lambda i: (i, 0)), pl.BlockSpec((1, d), lambda i: (0, 0))],
        out_specs=pl.BlockSpec((tm, d), lambda i: (i, 0)),
        out_shape=jax.ShapeDtypeStruct((t, d), MXU_DTYPE),
        compiler_params=_params(1),
    )(x, g)


def _mix_out_norm(a, w, res, g, name, tm=512):
    m, k = a.shape
    d = w.shape[1]
    tm = min(tm, m)

    def body(a_ref, w_ref, res_ref, g_ref, x_ref, h_ref):
        xv = res_ref[...] + _dot(a_ref[...], w_ref[...])
        x_ref[...] = xv
        r = lax.rsqrt(jnp.mean(xv * xv, axis=-1, keepdims=True) + EPS)
        h_ref[...] = (xv * r * g_ref[...]).astype(h_ref.dtype)

    row = pl.BlockSpec((tm, d), lambda i: (i, 0))
    return pl.pallas_call(
        body, name=name, grid=(m // tm,),
        in_specs=[pl.BlockSpec((tm, k), lambda i: (i, 0)), pl.BlockSpec((k, d), lambda i: (0, 0)), row,
                  pl.BlockSpec((1, d), lambda i: (0, 0))],
        out_specs=[row, row],
        out_shape=[jax.ShapeDtypeStruct((m, d), F32), jax.ShapeDtypeStruct((m, d), MXU_DTYPE)],
        compiler_params=_params(1),
    )(a, w, res, g)


def _ffn_down_loss(a, w, res, target, g, name, tm=512):
    m, k = a.shape
    d = w.shape[1]
    tm = min(tm, m)

    def body(a_ref, w_ref, res_ref, t_ref, g_ref, dx_ref, dxm_ref, dg_ref, loss_ref):
        xv = res_ref[...] + _dot(a_ref[...], w_ref[...])
        gv = g_ref[...]
        r = lax.rsqrt(jnp.mean(xv * xv, axis=-1, keepdims=True) + EPS)
        xh = xv * r
        e = xh * gv - t_ref[...]
        dy = e * (1.0 / d)
        dxh = dy * gv
        dxv = r * (dxh - xh * jnp.mean(dxh * xh, axis=-1, keepdims=True))
        dx_ref[...] = dxv
        dxm_ref[...] = dxv.astype(dxm_ref.dtype)

        @pl.when(pl.program_id(0) == 0)
        def _():
            dg_ref[...] = jnp.zeros_like(dg_ref)
            loss_ref[...] = jnp.zeros_like(loss_ref)

        dg_ref[...] += jnp.sum(dy * xh, axis=0, keepdims=True)
        part = 0.5 * jnp.sum(jnp.mean(e * e, axis=-1, keepdims=True), axis=0, keepdims=True)
        loss_ref[...] += jnp.broadcast_to(part, loss_ref.shape)

    row = pl.BlockSpec((tm, d), lambda i: (i, 0))
    vec = pl.BlockSpec((1, d), lambda i: (0, 0))
    return pl.pallas_call(
        body, name=name, grid=(m // tm,),
        in_specs=[pl.BlockSpec((tm, k), lambda i: (i, 0)), pl.BlockSpec((k, d), lambda i: (0, 0)), row, row, vec],
        out_specs=[row, row, vec, pl.BlockSpec((1, LANES), lambda i: (0, 0))],
        out_shape=[jax.ShapeDtypeStruct((m, d), F32), jax.ShapeDtypeStruct((m, d), MXU_DTYPE),
                   jax.ShapeDtypeStruct((1, d), F32), jax.ShapeDtypeStruct((1, LANES), F32)],
        compiler_params=_params(1),
    )(a, w, res, target, g)


def _mm_nt(a, w, name, rows=None, out_dtype=F32, tm=512, tk=512):
    m, n = a.shape
    row0, k = rows or (0, w.shape[0])
    tm, tk = min(tm, m), min(tk, k)
    first = row0 // tk

    def body(a_ref, w_ref, o_ref):
        o_ref[...] = _dot_nt(a_ref[...], w_ref[...]).astype(o_ref.dtype)

    return pl.pallas_call(
        body, name=name, grid=(k // tk, m // tm),
        in_specs=[pl.BlockSpec((tm, n), lambda j, i: (i, 0)), pl.BlockSpec((tk, n), lambda j, i: (first + j, 0))],
        out_specs=pl.BlockSpec((tm, tk), lambda j, i: (i, j)),
        out_shape=jax.ShapeDtypeStruct((m, k), out_dtype),
        compiler_params=_params(2),
    )(a, w)


def _mm_tn(a, b, name, tk, tn, tt=512, out_dtype=BF16):
    t, k = a.shape
    n = b.shape[1]
    tt = min(tt, t)
    steps = t // tt

    def body(a_ref, b_ref, o_ref, acc):
        s = pl.program_id(2)

        @pl.when(s == 0)
        def _():
            acc[...] = jnp.zeros_like(acc)

        acc[...] += _dot_tn(a_ref[...], b_ref[...])

        @pl.when(s == steps - 1)
        def _():
            o_ref[...] = acc[...].astype(o_ref.dtype)

    return pl.pallas_call(
        body, name=name, grid=(k // tk, n // tn, steps),
        in_specs=[pl.BlockSpec((tt, tk), lambda i, j, s: (s, i)), pl.BlockSpec((tt, tn), lambda i, j, s: (s, j))],
        out_specs=pl.BlockSpec((tk, tn), lambda i, j, s: (i, j)),
        out_shape=jax.ShapeDtypeStruct((k, n), out_dtype),
        scratch_shapes=[pltpu.VMEM((tk, tn), F32)],
        compiler_params=_params(3),
    )(a, b)


def _ffn_up(h, wg_t, wu_t, name, tm=512, tn=FF_TILE):
    m, k = h.shape
    n = wg_t.shape[0]
    tm = min(tm, m)

    def body(h_ref, wg_ref, wu_ref, gate_ref, up_ref, ff_ref):
        hv = h_ref[...]
        gate = _dot_nt(hv, wg_ref[...])
        up = _dot_nt(hv, wu_ref[...])
        gate_ref[...] = gate.astype(gate_ref.dtype)
        up_ref[...] = up.astype(up_ref.dtype)
        ff_ref[...] = (gate * _sigmoid(gate) * up).astype(ff_ref.dtype)

    wspec = pl.BlockSpec((tn, k), lambda j, i: (j, 0))
    ospec = pl.BlockSpec((tm, tn), lambda j, i: (i, j))
    return pl.pallas_call(
        body, name=name, grid=(n // tn, m // tm),
        in_specs=[pl.BlockSpec((tm, k), lambda j, i: (i, 0)), wspec, wspec],
        out_specs=[ospec, ospec, ospec],
        out_shape=[jax.ShapeDtypeStruct((m, n), MXU_DTYPE)] * 3,
        compiler_params=_params(2),
    )(h, wg_t, wu_t)


def _ffn_bwd_act(dx, wd, gate, up, name, tm=512, tn=FF_TILE):
    m, k = dx.shape
    n = wd.shape[0]
    tm = min(tm, m)

    def body(dx_ref, wd_ref, gate_ref, up_ref, dgate_ref, dup_ref):
        dff = _dot_nt(dx_ref[...], wd_ref[...])
        gate = gate_ref[...].astype(F32)
        sg = _sigmoid(gate)
        dgate_ref[...] = (dff * up_ref[...].astype(F32) * sg * (1.0 + gate * (1.0 - sg))).astype(dgate_ref.dtype)
        dup_ref[...] = (dff * gate * sg).astype(dup_ref.dtype)

    ospec = pl.BlockSpec((tm, tn), lambda j, i: (i, j))
    return pl.pallas_call(
        body, name=name, grid=(n // tn, m // tm),
        in_specs=[pl.BlockSpec((tm, k), lambda j, i: (i, 0)), pl.BlockSpec((tn, k), lambda j, i: (j, 0)),
                  ospec, ospec],
        out_specs=[ospec, ospec],
        out_shape=[jax.ShapeDtypeStruct((m, n), MXU_DTYPE), jax.ShapeDtypeStruct((m, n), MXU_DTYPE)],
        compiler_params=_params(2),
    )(dx, wd, gate, up)


def _chunk_masks():
    r = lax.broadcasted_iota(jnp.int32, (SUPER, SUPER), 0)
    c = lax.broadcasted_iota(jnp.int32, (SUPER, SUPER), 1)
    same = jnp.right_shift(r, 6) == jnp.right_shift(c, 6)
    lower = jnp.logical_and(same, c <= r)
    upper = jnp.logical_and(same, c >= r)
    return same, lower, upper


def _head_block_mask():
    r = lax.broadcasted_iota(jnp.int32, (LANES, LANES), 0)
    c = lax.broadcasted_iota(jnp.int32, (LANES, LANES), 1)
    return jnp.where(jnp.right_shift(r, 6) == jnp.right_shift(c, 6), 1.0, 0.0)


def _lower_bound(lb_raw):
    return 1.0 / (1.0 + jnp.exp(lb_raw[1:2, :] - lb_raw[0:1, :]))


PER_SUPER = SUPER // CHUNK
CHUNK_ROWS = [slice(c * CHUNK, (c + 1) * CHUNK) for c in range(PER_SUPER)]


def _over_chunks(rows):
    return jnp.concatenate([jnp.broadcast_to(r, (CHUNK, LANES)) for r in rows], axis=0)


def _hgrn_gates(q, hf, lb, lower_b):
    sig = _sigmoid(hf)
    f = lb + (1.0 - lb) * sig
    k = 1.0 - f
    lf = jnp.log(f)
    b = _rsum_left(lower_b, lf, 2)
    ends = [b[cr.stop - 1:cr.stop, :] for cr in CHUNK_ROWS]
    eb = jnp.exp(b)
    enb = jnp.exp(-b)
    edb = jnp.exp(_over_chunks(ends) - b)
    decs = [jnp.exp(e) for e in ends]
    return sig, f, k, decs, eb, enb, edb, q * eb, k * enb, k * edb


def _hgrn_fwd(proj, lower_bounds, norm_g, n_seq, seq):
    t = n_seq * seq
    n_super = seq // SUPER
    n_chunks = seq // CHUNK

    def body(q_ref, f_ref, i_ref, g_ref, lb_ref, ng_ref, out_ref, opre_ref, st_ref):
        masks = _head_masks()
        _, lower, _ = _chunk_masks()
        lower_b = _ones_where(lower)
        bd = _head_block_mask()
        lb = _lower_bound(lb_ref[...])
        ng = ng_ref[...]

        def step(sb, st):
            rows = pl.ds(pl.multiple_of(sb * SUPER, SUPER), SUPER)
            q, hf, v, hg = q_ref[rows, :], f_ref[rows, :], i_ref[rows, :], g_ref[rows, :]
            _, _, _, decs, _, _, _, qe, ke, kd = _hgrn_gates(q, hf, lb, lower_b)
            scores = [_dot_nt(qe * m, ke) for m in masks]
            updates = [_dot_tn(v[cr], kd[cr]) for cr in CHUNK_ROWS]
            states = [st]
            for dec, upd in zip(decs, updates):
                states.append(states[-1] * dec + bd * upd)
            for c in range(PER_SUPER):
                st_ref[0, 0, sb * PER_SUPER + c] = states[c]
            intra = [_dot(jnp.where(lower, p, 0.0), v) for p in scores]
            inter = [_dot_nt(qe[cr], s) for cr, s in zip(CHUNK_ROWS, states)]
            o = intra[0] * masks[0] + intra[1] * masks[1] + jnp.concatenate(inter, axis=0)
            opre_ref[rows, :] = o
            on = o * _head_rstd(o, masks) * ng
            out_ref[rows, :] = (on * hg * _sigmoid(hg)).astype(out_ref.dtype)
            return states[-1]

        lax.fori_loop(0, n_super, step, jnp.zeros((LANES, LANES), F32))

    def col(k):
        return pl.BlockSpec((seq, LANES), lambda p, b: (b, k * N_PAIRS + p))

    vec = lambda rows: pl.BlockSpec((rows, LANES), lambda p, b: (0, p))
    ospec = pl.BlockSpec((seq, LANES), lambda p, b: (b, p))
    return pl.pallas_call(
        body, name="hgrn_fwd", grid=(N_PAIRS, n_seq),
        in_specs=[col(0), col(1), col(2), col(3), vec(2), vec(1)],
        out_specs=[ospec, ospec,
                   pl.BlockSpec((1, 1, n_chunks, LANES, LANES), lambda p, b: (b, p, 0, 0, 0))],
        out_shape=[jax.ShapeDtypeStruct((t, 2 * GROUP), MXU_DTYPE), jax.ShapeDtypeStruct((t, GROUP), F32),
                   jax.ShapeDtypeStruct((n_seq, N_PAIRS, n_chunks, LANES, LANES), F32)],
        compiler_params=_params(2),
    )(proj, proj, proj, proj, lower_bounds, norm_g)


def _hgrn_bwd(proj, lower_bounds, norm_g, dmix, opre, states, n_seq, seq):
    t = n_seq * seq
    n_super = seq // SUPER
    n_chunks = seq // CHUNK
    per = SUPER // CHUNK

    def body(q_ref, f_ref, i_ref, g_ref, lb_ref, ng_ref, dm_ref, opre_ref, st_ref,
             dq_ref, df_ref, di_ref, dg_ref, dlb_ref, dng_ref):
        masks = _head_masks()
        _, lower, upper = _chunk_masks()
        lower_b, upper_b = _ones_where(lower), _ones_where(upper)
        bd = _head_block_mask()
        lb_raw = lb_ref[...]
        lb = _lower_bound(lb_raw)
        ng = ng_ref[...]

        @pl.when(pl.program_id(1) == 0)
        def _():
            dlb_ref[...] = jnp.zeros_like(dlb_ref)
            dng_ref[...] = jnp.zeros_like(dng_ref)

        def step(it, dst):
            sb = n_super - 1 - it
            rows = pl.ds(pl.multiple_of(sb * SUPER, SUPER), SUPER)
            q, hf, v, hg = q_ref[rows, :], f_ref[rows, :], i_ref[rows, :], g_ref[rows, :]
            sig, f, k, decs, eb, enb, edb, qe, ke, kd = _hgrn_gates(q, hf, lb, lower_b)
            o = opre_ref[rows, :]
            r = _head_rstd(o, masks)
            oh = o * r
            dm = dm_ref[rows, :]
            sg = _sigmoid(hg)
            dg_ref[rows, :] = (dm * oh * ng * sg * (1.0 + hg * (1.0 - sg))).astype(dg_ref.dtype)
            don = dm * hg * sg
            dng_ref[...] += jnp.sum(don * oh, axis=0, keepdims=True)
            doh = don * ng
            do = r * (doh - oh * _head_mean(doh * oh, masks))
            doms = [do * m for m in masks]
            qems = [qe * m for m in masks]
            scores = [_dot_nt(qem, ke) for qem in qems]
            dscores = [_dot_nt(dom, v) for dom in doms]
            prevs = [st_ref[0, 0, sb * per + c] for c in range(per)]
            dst_in = [_dot_tn(do[cr], qe[cr]) for cr in CHUNK_ROWS]
            dqe_i = [_dot(do[cr], prev) for cr, prev in zip(CHUNK_ROWS, prevs)]
            dsts = [None] * per
            for c in reversed(range(per)):
                dsts[c] = dst
                dst = bd * (dst * decs[c] + dst_in[c])
            ps = [jnp.where(lower, p, 0.0) for p in scores]
            dps = [jnp.where(lower, dp, 0.0) for dp in dscores]
            dqe_h = [_dot(dp, ke) for dp in dps]
            dke_h = [_dot_tn(dp, qem) for dp, qem in zip(dps, qems)]
            dv_h = [_dot_tn(p, dom) for p, dom in zip(ps, doms)]
            dus = [bd * d for d in dsts]
            dv_i = [_dot_nt(kd[cr], du) for cr, du in zip(CHUNK_ROWS, dus)]
            dkd_i = [_dot(v[cr], du) for cr, du in zip(CHUNK_ROWS, dus)]
            dqe = dqe_h[0] * masks[0] + dqe_h[1] * masks[1] + jnp.concatenate(dqe_i, axis=0)
            dke = dke_h[0] + dke_h[1]
            dv = dv_h[0] + dv_h[1] + jnp.concatenate(dv_i, axis=0)
            dkd = jnp.concatenate(dkd_i, axis=0)
            dk = dke * enb + dkd * edb
            db = dqe * qe - dke * ke - dkd * kd
            dkd_kd = dkd * kd
            dends = [jnp.sum(dkd_kd[cr], axis=0, keepdims=True)
                     + jnp.sum(dsts[c] * prevs[c], axis=0, keepdims=True) * decs[c]
                     for c, cr in enumerate(CHUNK_ROWS)]
            dlf = _rsum_left(upper_b, db, 2) + _over_chunks(dends)
            dfv = dlf / f - dk
            dq_ref[rows, :] = (dqe * eb).astype(dq_ref.dtype)
            di_ref[rows, :] = dv.astype(di_ref.dtype)
            df_ref[rows, :] = (dfv * (1.0 - lb) * sig * (1.0 - sig)).astype(df_ref.dtype)
            dlb = jnp.sum(dfv * (1.0 - sig), axis=0, keepdims=True)
            da0 = dlb * lb * (1.0 - lb)
            dlb_ref[0:1, :] += da0
            dlb_ref[1:2, :] -= da0
            return dst

        lax.fori_loop(0, n_super, step, jnp.zeros((LANES, LANES), F32))

    def col(k):
        return pl.BlockSpec((seq, LANES), lambda p, b: (b, k * N_PAIRS + p))

    vec = lambda rows: pl.BlockSpec((rows, LANES), lambda p, b: (0, p))
    ospec = pl.BlockSpec((seq, LANES), lambda p, b: (b, p))
    piece = jax.ShapeDtypeStruct((t, GROUP), MXU_DTYPE)
    return pl.pallas_call(
        body, name="hgrn_bwd", grid=(N_PAIRS, n_seq),
        in_specs=[col(0), col(1), col(2), col(3), vec(2), vec(1), ospec, ospec,
                  pl.BlockSpec((1, 1, n_chunks, LANES, LANES), lambda p, b: (b, p, 0, 0, 0))],
        out_specs=[ospec, ospec, ospec, ospec, vec(2), vec(1)],
        out_shape=[piece, piece, piece, piece,
                   jax.ShapeDtypeStruct((2, GROUP), F32), jax.ShapeDtypeStruct((1, GROUP), F32)],
        compiler_params=_params(2),
    )(proj, proj, proj, proj, lower_bounds, norm_g, dmix, opre, states)


SB_SCALE = 1.0 / math.sqrt(D_HEAD)


SB_STEP = 2 * SB_BLOCK
QUERY_BLOCKS = (slice(0, SB_BLOCK), slice(SB_BLOCK, SB_STEP))


def _triangle(keep):
    row = lax.broadcasted_iota(jnp.int32, (2 * SB_BLOCK, SB_BLOCK), 0) % SB_BLOCK
    col = lax.broadcasted_iota(jnp.int32, (2 * SB_BLOCK, SB_BLOCK), 1)
    return _ones_where(keep(row, col))


def _tile_masks():
    r = lax.broadcasted_iota(jnp.int32, (SB_BLOCK, SB_BLOCK), 0)
    c = lax.broadcasted_iota(jnp.int32, (SB_BLOCK, SB_BLOCK), 1)
    return r, c


def _sb_fwd(proj, norm_g, mixed, shards, n_seq, seq):
    t = n_seq * seq
    nq = seq // SB_STEP
    q0, k0, v0 = 0, N_PAIRS, 2 * N_PAIRS
    n_w = len(shards)
    n_steps = N_PAIRS * n_seq * nq
    tri = _triangle(lambda row, col: row >= col)

    def body(q_ref, k_ref, v_ref, ng_ref, tri_ref, mixed_in, *rest):
        del mixed_in
        shard_refs = rest[:n_w]
        out_ref, opre_ref, ctot_ref = rest[n_w:n_w + 3]
        gathered = rest[n_w + 3:2 * n_w + 3]
        send_sems, recv_sems, local_sems = rest[2 * n_w + 3:]
        i = pl.program_id(2)
        step = (pl.program_id(0) * n_seq + pl.program_id(1)) * nq + i
        plan = _GatherPlan(shard_refs, gathered, send_sems, recv_sems, local_sems)

        @pl.when(step == 0)
        def _():
            plan.start()

        @pl.when(step == n_steps // 2)
        def _():
            plan.forward()

        masks = _head_masks()
        r, c = _tile_masks()
        strict = c < r
        suffix2 = tri_ref[...]
        qhs = [[(q_ref[blk, :] * SB_SCALE * m).astype(MXU_DTYPE) for m in masks] for blk in QUERY_BLOCKS]

        def tiles(js, work, carry):
            rows = [pl.ds(pl.multiple_of(j * SB_BLOCK, SB_BLOCK), SB_BLOCK) for j in js]
            ks = [k_ref[rw, :].astype(MXU_DTYPE) for rw in rows]
            vs = [v_ref[rw, :].astype(MXU_DTYPE) for rw in rows]
            zs = [[_dot_nt(qh, ks[ts]) for qh in qhs[qb]] for qb, ts, _ in work]
            ccs = [[_softplus(z) for z in zw] for zw in zs]
            ccs = [[jnp.where(strict, cc, 0.0) for cc in cw] if diag else cw for cw, (_, _, diag) in zip(ccs, work)]
            sums = [[jnp.dot(jnp.concatenate(_split(cc, 2), axis=1), suffix2, preferred_element_type=F32)
                     for cc in cw] for cw in ccs]
            out = [list(per_block) for per_block in carry]
            for h in range(len(masks)):
                for w, (qb, ts, diag) in enumerate(work):
                    run, acc = out[qb][h]
                    a = jnp.exp(zs[w][h] - (sums[w][h] + run))
                    if diag:
                        a = jnp.where(strict, a, 0.0)
                    out[qb][h] = (run + sums[w][h][:, 0:1], acc + _dot(a, vs[ts]))
            return tuple(tuple(per_block) for per_block in out)

        start = ((jnp.zeros((SB_BLOCK, 1), F32), jnp.zeros((SB_BLOCK, LANES), F32)),) * 2
        carry = tiles([2 * i, 2 * i + 1], [(0, 0, True), (1, 1, True), (1, 0, False)], (start, start))
        both = [(0, 0, False), (0, 1, False), (1, 0, False), (1, 1, False)]
        carry = lax.fori_loop(0, i, lambda s, cy: tiles([2 * (i - s) - 1, 2 * (i - s) - 2], both, cy), carry)
        opre = jnp.concatenate([cb[0][1] * masks[0] + cb[1][1] * masks[1] for cb in carry], axis=0)
        ctot = jnp.concatenate([cb[0][0] * masks[0] + cb[1][0] * masks[1] for cb in carry], axis=0)
        opre_ref[...] = opre
        ctot_ref[...] = ctot
        out_ref[...] = (opre * _head_rstd(opre, masks) * ng_ref[...]).astype(out_ref.dtype)

        @pl.when(step == n_steps - 1)
        def _():
            plan.finish()

    qspec = pl.BlockSpec((SB_STEP, LANES), lambda p, b, i: (b * nq + i, q0 + p))
    ospec = pl.BlockSpec((SB_STEP, LANES), lambda p, b, i: (b * nq + i, p))
    hbm = pl.BlockSpec(memory_space=pltpu.HBM)
    outs = pl.pallas_call(
        body, name="sb_fwd", grid=(N_PAIRS, n_seq, nq),
        in_specs=[qspec,
                  pl.BlockSpec((seq, LANES), lambda p, b, i: (b, k0 + p)),
                  pl.BlockSpec((seq, LANES), lambda p, b, i: (b, v0 + p)),
                  pl.BlockSpec((1, LANES), lambda p, b, i: (0, p)),
                  pl.BlockSpec(tri.shape, lambda p, b, i: (0, 0)), hbm] + [hbm] * n_w,
        out_specs=[pl.BlockSpec((SB_STEP, LANES), lambda p, b, i: (b * nq + i, N_PAIRS + p)), ospec, ospec]
        + [hbm] * n_w,
        out_shape=[jax.ShapeDtypeStruct(mixed.shape, mixed.dtype), jax.ShapeDtypeStruct((t, GROUP), F32),
                   jax.ShapeDtypeStruct((t, GROUP), F32)]
        + [jax.ShapeDtypeStruct((N_DEV,) + s.shape, s.dtype) for s in shards],
        scratch_shapes=[pltpu.SemaphoreType.DMA((n_w * _GatherPlan.COPIES,)),
                        pltpu.SemaphoreType.DMA((n_w * _GatherPlan.COPIES,)), pltpu.SemaphoreType.DMA((n_w,))],
        input_output_aliases={5: 0},
        compiler_params=_params(3),
    )(proj, proj, proj, norm_g, tri, mixed, *shards)
    return outs[0], outs[1], outs[2], list(outs[3:])


def _sb_bwd(proj, norm_g, dmix, opre, ctot, grads, n_seq, seq):
    t = n_seq * seq
    nq = seq // SB_STEP
    q0, k0, v0 = 0, N_PAIRS, 2 * N_PAIRS
    n_w = len(grads)
    n_steps = N_PAIRS * n_seq * nq
    tri = _triangle(lambda row, col: row <= col)

    def body(q_ref, k_ref, v_ref, ng_ref, tri_ref, dm_ref, opre_ref, ctot_ref, *rest):
        grad_refs = rest[:n_w]
        dq_ref, dk_ref, dv_ref, dng_ref = rest[n_w:n_w + 4]
        lands = rest[n_w + 4:2 * n_w + 4]
        dk_acc, dv_acc, send_sems, recv_sems, local_sems = rest[2 * n_w + 4:]
        p_id, b_id, i = pl.program_id(0), pl.program_id(1), pl.program_id(2)
        step = (p_id * n_seq + b_id) * nq + i
        plan = _ScatterPlan(grad_refs, lands, send_sems, recv_sems, local_sems)

        @pl.when(step == 0)
        def _():
            plan.start()

        masks = _head_masks()
        r, c = _tile_masks()
        strict = c < r
        upto2 = tri_ref[...]
        upto = tri_ref[0:SB_BLOCK, :]

        def prefix(x):
            return jnp.dot(jnp.concatenate(_split(x, 2), axis=1), upto2, preferred_element_type=F32)

        @pl.when(i == 0)
        def _():
            dk_acc[...] = jnp.zeros_like(dk_acc)
            dv_acc[...] = jnp.zeros_like(dv_acc)

        @pl.when(jnp.logical_and(b_id == 0, i == 0))
        def _():
            dng_ref[...] = jnp.zeros_like(dng_ref)

        o = opre_ref[...]
        rs = _head_rstd(o, masks)
        oh = o * rs
        dm = dm_ref[...]
        dng_ref[...] += jnp.sum(dm * oh, axis=0, keepdims=True)
        doh = dm * ng_ref[...]
        do = rs * (doh - oh * _head_mean(doh * oh, masks))

        heads = range(len(masks))
        qs = [q_ref[blk, :] * SB_SCALE for blk in QUERY_BLOCKS]
        dos = [do[blk] for blk in QUERY_BLOCKS]
        qhs = [[(q * m).astype(MXU_DTYPE) for m in masks] for q in qs]
        doms = [[(d * m).astype(MXU_DTYPE) for m in masks] for d in dos]
        qhts = [[(q * m).T.astype(MXU_DTYPE) for m in masks] for q in qs]
        domts = [[(d * m).T.astype(MXU_DTYPE) for m in masks] for d in dos]
        totals = [[jnp.max(ctot_ref[blk, :] * m, axis=1, keepdims=True) for m in masks] for blk in QUERY_BLOCKS]

        def tiles(js, work, carry):
            rows = [pl.ds(pl.multiple_of(j * SB_BLOCK, SB_BLOCK), SB_BLOCK) for j in js]
            ks = [k_ref[rw, :].astype(MXU_DTYPE) for rw in rows]
            vs = [v_ref[rw, :].astype(MXU_DTYPE) for rw in rows]
            zs = [[_dot_nt(qhs[qb][h], ks[ts]) for h in heads] for qb, ts, _ in work]
            das = [[_dot_nt(doms[qb][h], vs[ts]) for h in heads] for qb, ts, _ in work]
            sps = [[_softplus(z) for z in zw] for zw in zs]
            lsigs = [[z - sp for z, sp in zip(zw, sw)] for zw, sw in zip(zs, sps)]
            sigs = [[jnp.exp(ls) for ls in lw] for lw in lsigs]
            ccs = [[jnp.where(strict, sp, 0.0) for sp in sw] if diag else sw for sw, (_, _, diag) in zip(sps, work)]
            pres = [[prefix(cc) for cc in cw] for cw in ccs]
            out = [list(per_block) for per_block in carry]
            for h in heads:
                for w, (qb, ts, diag) in enumerate(work):
                    pc, pdl, dq_h = out[qb][h]
                    a = jnp.exp(lsigs[w][h] + pres[w][h] - (totals[qb][h] - pc))
                    if diag:
                        a = jnp.where(strict, a, 0.0)
                    dl = a * das[w][h]
                    dv_acc[js[ts]] += _dot(domts[qb][h], a)
                    dpre = jnp.dot(dl.astype(BF16), upto, preferred_element_type=F32)
                    dz = dl - sigs[w][h] * (pdl + dpre)
                    if diag:
                        dz = jnp.where(strict, dz, 0.0)
                    dzb = dz.astype(MXU_DTYPE)
                    dk_acc[js[ts]] += _dot(qhts[qb][h], dzb)
                    out[qb][h] = (pc + pres[w][h][:, SB_BLOCK - 1:SB_BLOCK], pdl + dpre[:, SB_BLOCK - 1:SB_BLOCK],
                                  dq_h + _dot(dzb, ks[ts]))
            return tuple(tuple(per_block) for per_block in out)

        zero = jnp.zeros((SB_BLOCK, 1), F32)
        start = ((zero, zero, jnp.zeros((SB_BLOCK, LANES), F32)),) * 2
        both = [(0, 0, False), (0, 1, False), (1, 0, False), (1, 1, False)]
        carry = lax.fori_loop(0, i, lambda s, cy: tiles([2 * s, 2 * s + 1], both, cy), (start, start))
        carry = tiles([2 * i, 2 * i + 1], [(0, 0, True), (1, 0, False), (1, 1, True)], carry)
        dq = jnp.concatenate([cb[0][2] * masks[0] + cb[1][2] * masks[1] for cb in carry], axis=0)
        dq_ref[...] = (dq * SB_SCALE).astype(dq_ref.dtype)

        @pl.when(i == nq - 1)
        def _():
            for j in range(seq // SB_BLOCK):
                tile_rows = slice(j * SB_BLOCK, (j + 1) * SB_BLOCK)
                dk_ref[tile_rows, :] = dk_acc[j].T.astype(dk_ref.dtype)
                dv_ref[tile_rows, :] = dv_acc[j].T.astype(dv_ref.dtype)

        @pl.when(step == n_steps - 1)
        def _():
            plan.finish()

    qspec = pl.BlockSpec((SB_STEP, LANES), lambda p, b, i: (b * nq + i, q0 + p))
    ospec = pl.BlockSpec((SB_STEP, LANES), lambda p, b, i: (b * nq + i, p))
    dmspec = pl.BlockSpec((SB_STEP, LANES), lambda p, b, i: (b * nq + i, N_PAIRS + p))
    full = lambda k: pl.BlockSpec((seq, LANES), lambda p, b, i: (b, k + p))
    vec = pl.BlockSpec((1, LANES), lambda p, b, i: (0, p))
    hbm = pl.BlockSpec(memory_space=pltpu.HBM)
    piece = jax.ShapeDtypeStruct((t, GROUP), MXU_DTYPE)
    outs = pl.pallas_call(
        body, name="sb_bwd", grid=(N_PAIRS, n_seq, nq),
        in_specs=[qspec, full(k0), full(v0), vec, pl.BlockSpec(tri.shape, lambda p, b, i: (0, 0)), dmspec, ospec, ospec]
        + [hbm] * n_w,
        out_specs=[ospec, full(0), full(0), vec] + [hbm] * n_w,
        out_shape=[piece, piece, piece, jax.ShapeDtypeStruct((1, GROUP), F32)]
        + [jax.ShapeDtypeStruct(g.shape, g.dtype) for g in grads],
        scratch_shapes=[pltpu.VMEM((seq // SB_BLOCK, LANES, SB_BLOCK), F32),
                        pltpu.VMEM((seq // SB_BLOCK, LANES, SB_BLOCK), F32),
                        pltpu.SemaphoreType.DMA((n_w * (N_DEV - 1),)), pltpu.SemaphoreType.DMA((n_w * (N_DEV - 1),)),
                        pltpu.SemaphoreType.DMA((n_w,))],
        compiler_params=_params(3),
    )(proj, proj, proj, norm_g, tri, dmix, opre, ctot, *grads)
    return outs[0], outs[1], outs[2], outs[3], list(outs[4:])


def _mesh_place():
    x, y, c = lax.axis_index("x"), lax.axis_index("y"), lax.axis_index("c")
    return x, y, c


def _peer(x, y, c, k):
    px = lax.rem(x + ((k >> 2) & 1), 2)
    py = lax.rem(y + ((k >> 1) & 1), 2)
    pc = lax.rem(c + (k & 1), 2)
    return (px, py, pc), 4 * px + 2 * py + pc


def _remote(src, dst, send_sem, recv_sem, to):
    return pltpu.make_async_remote_copy(src_ref=src, dst_ref=dst, send_sem=send_sem, recv_sem=recv_sem,
                                        device_id=to, device_id_type=pl.DeviceIdType.MESH)


class _GatherPlan:
    COPIES = 7

    def __init__(self, shards, gathered, send_sems, recv_sems, local_sems):
        x, y, c = _mesh_place()
        self.c = c
        self.me = (x, y, c)
        self.sibling = (x, y, 1 - c)
        self.chips = [(1 - x, y), (x, 1 - y), (1 - x, 1 - y)]
        self.tensors = list(zip(shards, gathered))
        self.send_sems, self.recv_sems, self.local_sems = send_sems, recv_sems, local_sems

    @staticmethod
    def _index(place):
        return 4 * place[0] + 2 * place[1] + place[2]

    def _copy(self, w, k, block, to, own=False):
        shard, gathered = self.tensors[w]
        slot = gathered.at[self._index(block)]
        n = w * self.COPIES + k
        return _remote(shard if own else slot, slot, self.send_sems.at[n], self.recv_sems.at[n], to)

    def _local(self, w):
        shard, gathered = self.tensors[w]
        return pltpu.make_async_copy(shard, gathered.at[self._index(self.me)], self.local_sems.at[w])

    def _first(self, w):
        return [self._copy(w, 0, self.me, self.sibling, own=True)] + [
            self._copy(w, 1 + j, self.me, (*chip, self.c), own=True) for j, chip in enumerate(self.chips)]

    def _passed(self, w):
        return [self._copy(w, 4 + j, (*chip, self.c), self.sibling) for j, chip in enumerate(self.chips)]

    def start(self):
        for w in range(len(self.tensors)):
            self._local(w).start()
            for cp in self._first(w):
                cp.start()

    def forward(self):
        for w in range(len(self.tensors)):
            passed = self._passed(w)
            for j, chip in enumerate(self.chips):
                self._copy(w, 1 + j, (*chip, self.c), self.me).wait_recv()
                passed[j].start()

    def finish(self):
        for w in range(len(self.tensors)):
            self._copy(w, 0, self.sibling, self.me).wait_recv()
            for j, chip in enumerate(self.chips):
                self._copy(w, 4 + j, (*chip, 1 - self.c), self.me).wait_recv()
            for cp in self._first(w) + self._passed(w):
                cp.wait_send()
            self._local(w).wait()


class _ScatterPlan:
    def __init__(self, grads, lands, send_sems, recv_sems, local_sems):
        self.place = _mesh_place()
        x, y, c = self.place
        self.me = 4 * x + 2 * y + c
        self.tensors = list(zip(grads, lands))
        self.send_sems, self.recv_sems, self.local_sems = send_sems, recv_sems, local_sems

    def _copies(self, w):
        grad, land = self.tensors[w]
        out = []
        for k in range(1, N_DEV):
            peer, pidx = _peer(*self.place, k)
            n = w * (N_DEV - 1) + k - 1
            sems = (self.send_sems.at[n], self.recv_sems.at[n], peer)
            out.append((_remote(grad.at[pidx], land.at[self.me], *sems), _remote(grad.at[pidx], land.at[pidx], *sems)))
        return out

    def _local(self, w):
        grad, land = self.tensors[w]
        return pltpu.make_async_copy(grad.at[self.me], land.at[self.me], self.local_sems.at[w])

    def start(self):
        for w in range(len(self.tensors)):
            self._local(w).start()
            for send, _ in self._copies(w):
                send.start()

    def finish(self):
        for w in range(len(self.tensors)):
            copies = self._copies(w)
            for _, arrival in copies:
                arrival.wait_recv()
            for send, _ in copies:
                send.wait_send()
            self._local(w).wait()


def _cast_shards(shards):
    def body(*refs):
        n = len(refs) // 2
        for src, dst in zip(refs[:n], refs[n:]):
            dst[...] = src[...].astype(dst.dtype)

    vmem = pl.BlockSpec(memory_space=pltpu.VMEM)
    return pl.pallas_call(
        body, name="cast_shards", in_specs=[vmem] * len(shards), out_specs=[vmem] * len(shards),
        out_shape=[jax.ShapeDtypeStruct(s.shape, BF16) for s in shards],
        compiler_params=pltpu.CompilerParams(vmem_limit_bytes=VMEM_LIMIT),
    )(*shards)


def _gather_w_in(shard):
    rows, cols = shard.shape

    def body(w_ref, out_ref, send_sems, recv_sems, local_sems):
        plan = _GatherPlan([w_ref], [out_ref], send_sems, recv_sems, local_sems)
        plan.start()
        plan.forward()
        plan.finish()

    vmem = pl.BlockSpec(memory_space=pltpu.VMEM)
    return pl.pallas_call(
        body, name="gather_w_in", in_specs=[vmem], out_specs=vmem,
        out_shape=jax.ShapeDtypeStruct((N_DEV, rows, cols), shard.dtype),
        scratch_shapes=[pltpu.SemaphoreType.DMA((_GatherPlan.COPIES,)), pltpu.SemaphoreType.DMA((_GatherPlan.COPIES,)),
                        pltpu.SemaphoreType.DMA((1,))],
        compiler_params=pltpu.CompilerParams(vmem_limit_bytes=VMEM_LIMIT),
    )(shard)


def _dw_rows(pieces, b, name, tn=512, tt=512):
    t, n = b.shape
    widths = [p.shape[1] for p in pieces]
    rows = sum(widths)
    tt = min(tt, t)
    steps = t // tt
    n_p = len(pieces)

    def body(*refs):
        piece_refs, b_ref, o_ref, acc = refs[:n_p], refs[n_p], refs[n_p + 1], refs[n_p + 2]
        s = pl.program_id(1)

        @pl.when(s == 0)
        def _():
            acc[...] = jnp.zeros_like(acc)

        bv = b_ref[...]
        off = 0
        for p_ref, width in zip(piece_refs, widths):
            acc[off:off + width, :] += _dot_tn(p_ref[...], bv)
            off += width

        @pl.when(s == steps - 1)
        def _():
            o_ref[...] = acc[...].astype(o_ref.dtype)

    return pl.pallas_call(
        body, name=name, grid=(n // tn, steps),
        in_specs=[pl.BlockSpec((tt, width), lambda j, s: (s, 0)) for width in widths]
        + [pl.BlockSpec((tt, tn), lambda j, s: (s, j))],
        out_specs=pl.BlockSpec((rows, tn), lambda j, s: (0, j)),
        out_shape=jax.ShapeDtypeStruct((rows, n), BF16),
        scratch_shapes=[pltpu.VMEM((rows, tn), F32)],
        compiler_params=_params(2),
    )(*pieces, b)


def _dh_norm_bwd(pairs, x, g, res, grads, name, tm=256):
    m, d = x.shape
    tm = min(tm, m)
    n_p, n_w = len(pairs), len(grads)
    n_steps = m // tm

    def body(*refs):
        x_ref, g_ref, res_ref = refs[2 * n_p:2 * n_p + 3]
        grad_refs = refs[2 * n_p + 3:2 * n_p + 3 + n_w]
        dx_ref, dg_ref = refs[2 * n_p + 3 + n_w:2 * n_p + 5 + n_w]
        lands = refs[2 * n_p + 5 + n_w:2 * n_p + 5 + 2 * n_w]
        sems = refs[2 * n_p + 5 + 2 * n_w:]
        step = pl.program_id(0)
        plan = _ScatterPlan(grad_refs, lands, *sems) if n_w else None

        @pl.when(step == 0)
        def _():
            dg_ref[...] = jnp.zeros_like(dg_ref)
            if plan:
                plan.start()

        dh = None
        for q in range(n_p):
            part = _dot(refs[2 * q][...], refs[2 * q + 1][...])
            dh = part if dh is None else dh + part
        xv = x_ref[...]
        r = lax.rsqrt(jnp.mean(xv * xv, axis=-1, keepdims=True) + EPS)
        xh = xv * r
        dxh = dh * g_ref[...]
        dx_ref[...] = res_ref[...] + r * (dxh - xh * jnp.mean(dxh * xh, axis=-1, keepdims=True))
        dg_ref[...] += jnp.sum(dh * xh, axis=0, keepdims=True)

        if plan:
            @pl.when(step == n_steps - 1)
            def _():
                plan.finish()

    in_specs, args = [], []
    for a, w, r in pairs:
        k = a.shape[1]
        in_specs += [pl.BlockSpec((tm, k), lambda i: (i, 0)),
                     pl.BlockSpec((k, d), functools.partial(lambda i, r: (r, 0), r=r or 0))]
        args += [a, w]
    row = pl.BlockSpec((tm, d), lambda i: (i, 0))
    vec = pl.BlockSpec((1, d), lambda i: (0, 0))
    hbm = pl.BlockSpec(memory_space=pltpu.HBM)
    sems = [pltpu.SemaphoreType.DMA((n_w * (N_DEV - 1),)), pltpu.SemaphoreType.DMA((n_w * (N_DEV - 1),)),
            pltpu.SemaphoreType.DMA((n_w,))] if n_w else []
    outs = pl.pallas_call(
        body, name=name, grid=(n_steps,), in_specs=in_specs + [row, vec, row] + [hbm] * n_w,
        out_specs=[row, vec] + [hbm] * n_w,
        out_shape=[jax.ShapeDtypeStruct((m, d), F32), jax.ShapeDtypeStruct((1, d), F32)]
        + [jax.ShapeDtypeStruct(gr.shape, gr.dtype) for gr in grads],
        scratch_shapes=sems, compiler_params=_params(1),
    )(*args, x, g, res, *grads)
    return outs[0], outs[1], list(outs[2:])


SMALL_LAYOUT = ((0, 0, 0, 0, D_MODEL), (1, 0, 1, 0, GROUP), (1, 1, 1, GROUP, GROUP), (2, 0, 2, 0, GROUP),
                (3, 0, 2, GROUP, GROUP), (4, 0, 3, 0, D_MODEL), (5, 0, 4, 0, D_MODEL))
LOSS_ROW = 5
N_SMALL = 6


def _small_step(grads, loss_part, ws, ms, vs):
    def body(*refs):
        g_in, loss_in = refs[:N_SMALL], refs[N_SMALL]
        params = [refs[1 + (q + 1) * N_SMALL:1 + (q + 2) * N_SMALL] for q in range(3)]
        o0 = 1 + 4 * N_SMALL
        outs = [refs[o0 + q * N_SMALL:o0 + (q + 1) * N_SMALL] for q in range(4)]
        loss_out = refs[o0 + 4 * N_SMALL]
        pack, land, wp, mp, vp, send_sems, recv_sems = refs[o0 + 4 * N_SMALL + 1:]

        def place(dst, srcs):
            dst[...] = jnp.zeros_like(dst)
            for p, sr, dr, dc, width in SMALL_LAYOUT:
                dst[dr:dr + 1, dc:dc + width] = srcs[p][sr:sr + 1, :]

        place(pack, g_in)
        pack[LOSS_ROW:LOSS_ROW + 1, 0:LANES] = loss_in[...]
        for dst, srcs in zip((wp, mp, vp), params):
            place(dst, srcs)

        x, y, c = _mesh_place()
        me = 4 * x + 2 * y + c
        land[me] = pack[...]
        sends = []
        for k in range(1, N_DEV):
            peer, _ = _peer(x, y, c, k)
            cp = _remote(pack, land.at[me], send_sems.at[k - 1], recv_sems.at[k - 1], peer)
            cp.start()
            sends.append(cp)
        for k in range(1, N_DEV):
            peer, pidx = _peer(x, y, c, k)
            _remote(pack, land.at[pidx], send_sems.at[k - 1], recv_sems.at[k - 1], peer).wait_recv()
        for cp in sends:
            cp.wait_send()

        g = land[0]
        for d in range(1, N_DEV):
            g = g + land[d]
        delta, nm, nv = _adam(wp[...], g, mp[...], vp[...])
        for val, out in zip((g, delta, nm, nv), outs):
            for p, sr, dr, dc, width in SMALL_LAYOUT:
                out[p][sr:sr + 1, :] = val[dr:dr + 1, dc:dc + width]
        loss_out[...] = g[LOSS_ROW:LOSS_ROW + 1, 0:LANES]

    vmem = pl.BlockSpec(memory_space=pltpu.VMEM)
    n_in = 1 + 4 * N_SMALL
    shapes = [jax.ShapeDtypeStruct(w.shape, F32) for w in ws]
    packed = pltpu.VMEM((SMALL_ROWS, D_MODEL), F32)
    outs = pl.pallas_call(
        body, name="small_step", in_specs=[vmem] * n_in, out_specs=[vmem] * (4 * N_SMALL + 1),
        out_shape=shapes * 4 + [jax.ShapeDtypeStruct((1, LANES), F32)],
        scratch_shapes=[packed, pltpu.VMEM((N_DEV, SMALL_ROWS, D_MODEL), F32), packed, packed, packed,
                        pltpu.SemaphoreType.DMA((N_DEV - 1,)), pltpu.SemaphoreType.DMA((N_DEV - 1,))],
    )(*grads, loss_part, *ws, *ms, *vs)
    return [outs[q * N_SMALL:(q + 1) * N_SMALL] for q in range(4)], outs[4 * N_SMALL]


def _adam(w, g, m, v):
    m = ADAM_B1 * m + (1.0 - ADAM_B1) * g
    v = ADAM_B2 * v + (1.0 - ADAM_B2) * (g * g)
    m_hat = m / (1.0 - ADAM_B1 ** ADAM_STEP)
    v_hat = v / (1.0 - ADAM_B2 ** ADAM_STEP)
    delta = -ADAM_LR * (m_hat / (jnp.sqrt(v_hat) + ADAM_EPS) + ADAM_WD * w)
    return delta, m, v


def _reduce_adamw(land, w, m, v, name, tr):
    _, rows, width = land.shape

    def body(land_ref, w_ref, m_ref, v_ref, g_ref, d_ref, nm_ref, nv_ref):
        g = land_ref[0].astype(F32)
        for d in range(1, N_DEV):
            g = g + land_ref[d].astype(F32)
        delta, nm, nv = _adam(w_ref[...], g, m_ref[...], v_ref[...])
        g_ref[...] = g
        d_ref[...] = delta
        nm_ref[...] = nm
        nv_ref[...] = nv

    row = pl.BlockSpec((tr, width), lambda i: (i, 0))
    out = jax.ShapeDtypeStruct((rows, width), F32)
    return pl.pallas_call(
        body, name=name, grid=(rows // tr,),
        in_specs=[pl.BlockSpec((N_DEV, tr, width), lambda i: (0, i, 0)), row, row, row],
        out_specs=[row, row, row, row], out_shape=[out, out, out, out],
        compiler_params=_params(1),
    )(land, w, m, v)


def kernel(x, mix_norm_g, w_in, lower_bounds, hgrn_norm_g, sb_norm_g, w_out, ffn_norm_g, w_gate, w_up, w_down, final_norm_g, loss_target, m_mix_norm_g, m_w_in, m_lower_bounds, m_hgrn_norm_g, m_sb_norm_g, m_w_out, m_ffn_norm_g, m_w_gate, m_w_up, m_w_down, m_final_norm_g, v_mix_norm_g, v_w_in, v_lower_bounds, v_hgrn_norm_g, v_sb_norm_g, v_w_out, v_ffn_norm_g, v_w_gate, v_w_up, v_w_down, v_final_norm_g):
    n_seq, seq, d = x.shape
    t = n_seq * seq
    x2d = x.reshape(t, d)
    tgt = loss_target.reshape(t, d)
    final_g = final_norm_g.reshape(1, d)
    col_sharded = (True, False, True, True, False)

    def as_rows(ws):
        return [w[0].T if tr else w[0] for w, tr in zip(ws, col_sharded)]

    big_w = as_rows([w_in, w_out, w_gate, w_up, w_down])
    big_m = as_rows([m_w_in, m_w_out, m_w_gate, m_w_up, m_w_down])
    big_v = as_rows([v_w_in, v_w_out, v_w_gate, v_w_up, v_w_down])

    sh_in, sh_out, sh_gate, sh_up, sh_down = _cast_shards(big_w)
    wt_in = _gather_w_in(sh_in).reshape(IN_COLS, d)
    h1 = _rmsnorm_fwd(x2d, mix_norm_g, "norm_mix")
    proj_h = _mm_nt(h1, wt_in, "proj_hgrn", rows=(0, 4 * GROUP), tm=1024, tk=1024)
    proj_s = _mm_nt(h1, wt_in, "proj_sb", rows=(4 * GROUP, 3 * GROUP), out_dtype=MXU_DTYPE, tm=1024, tk=512)
    mixed, oa_pre, states = _hgrn_fwd(proj_h, lower_bounds, hgrn_norm_g, n_seq, seq)
    mixed, ob_pre, ctot, gathered = _sb_fwd(proj_s, sb_norm_g, mixed, [sh_out, sh_gate, sh_up, sh_down], n_seq, seq)
    wf_out = gathered[0].reshape(d, d)
    wt_gate = gathered[1].reshape(D_FF, d)
    wt_up = gathered[2].reshape(D_FF, d)
    wf_down = gathered[3].reshape(D_FF, d)
    x1, h2 = _mix_out_norm(mixed, wf_out, x2d, ffn_norm_g, "mix_out")
    gate, up, ff = _ffn_up(h2, wt_gate, wt_up, "ffn_up")
    dx2, dx2m, d_final_g, loss_part = _ffn_down_loss(ff, wf_down, x1, tgt, final_g, "ffn_down_loss")

    dgate, dup = _ffn_bwd_act(dx2m, wf_down, gate, up, "ffn_bwd_act")
    dw_down = _mm_tn(ff, dx2m, "dw_down", tk=1408, tn=1024).reshape(N_DEV, D_FF // N_DEV, d)
    dw_gate = _mm_tn(dgate, h2, "dw_gate", tk=1408, tn=1024).reshape(N_DEV, D_FF // N_DEV, d)
    dw_up = _mm_tn(dup, h2, "dw_up", tk=1408, tn=1024).reshape(N_DEV, D_FF // N_DEV, d)
    dx1, d_ffn_g, _ = _dh_norm_bwd([(dgate, wt_gate, None), (dup, wt_up, None)], x1, ffn_norm_g, dx2, [], "dh_ffn")
    dmix = _mm_nt(dx1, wf_out, "dmix")
    dw_out = _mm_tn(mixed, dx1, "dw_out", tk=512, tn=1024).reshape(N_DEV, d // N_DEV, d)
    dsq, dsk, dsv, d_sb_g, lands = _sb_bwd(proj_s, sb_norm_g, dmix, ob_pre, ctot, [dw_out, dw_gate, dw_up, dw_down],
                                            n_seq, seq)
    dhq, dhf, dhi, dhg, d_lb, d_hgrn_g = _hgrn_bwd(proj_h, lower_bounds, hgrn_norm_g, dmix, oa_pre, states, n_seq,
                                                   seq)
    dproj = [dhq, dhf, dhi, dhg, dsq, dsk, dsv]
    dw_in = _dw_rows(dproj, h1, "dw_in").reshape(N_DEV, IN_COLS // N_DEV, d)
    dx, d_mix_g, (land_in,) = _dh_norm_bwd([(piece, wt_in, k) for k, piece in enumerate(dproj)], x2d, mix_norm_g,
                                           dx1, [dw_in], "dh_mix")

    tiles = {"in": 224, "out": 128, "gate": 176, "up": 176, "down": 176}
    big = [_reduce_adamw(land, w, m, v, "adamw_" + key, tr=tiles[key])
           for key, land, w, m, v in zip(tiles, [land_in] + lands, big_w, big_m, big_v)]
    big = [[r.T if tr else r for r in res] for res, tr in zip(big, col_sharded)]
    small, loss_row = _small_step(
        [d_mix_g, d_lb, d_hgrn_g, d_sb_g, d_ffn_g, d_final_g], loss_part,
        [mix_norm_g, lower_bounds, hgrn_norm_g, sb_norm_g, ffn_norm_g, final_g],
        [m_mix_norm_g, m_lower_bounds, m_hgrn_norm_g, m_sb_norm_g, m_ffn_norm_g, m_final_norm_g.reshape(1, d)],
        [v_mix_norm_g, v_lower_bounds, v_hgrn_norm_g, v_sb_norm_g, v_ffn_norm_g, v_final_norm_g.reshape(1, d)])

    outs = [loss_row[0, 0], dx.reshape(n_seq, seq, d)]
    for q in range(4):
        b_in, b_out, b_gate, b_up, b_down = [res[q][None] for res in big]
        s_mix, s_lb, s_hgrn, s_sb, s_ffn, s_final = small[q]
        outs += [s_mix, b_in, s_lb, s_hgrn, s_sb, b_out, s_ffn, b_gate, b_up, b_down, s_final.reshape(d)]
    return tuple(outs)
```

```python
import functools
import math

import jax
import jax.numpy as jnp
from jax import lax
from jax.experimental import pallas as pl
from jax.experimental.pallas import tpu as pltpu

F32 = jnp.float32
BF16 = jnp.bfloat16
MXU_DTYPE = BF16

EPS = 1e-6
D_MODEL = 1024
N_HEADS = 8
D_HEAD = 64
GROUP = N_HEADS * D_HEAD
IN_COLS = 7 * GROUP
D_FF = 2816
CHUNK = 64
LANES = 128
N_PAIRS = GROUP // LANES
SUPER = 256
SB_BLOCK = 256
N_DEV = 8

ADAM_LR = 0.001
ADAM_B1 = 0.9
ADAM_B2 = 0.999
ADAM_EPS = 1e-08
ADAM_WD = 0.01
ADAM_STEP = 10

SMALL_ROWS = 8
FF_TILE = D_FF // 2

VMEM_LIMIT = 48 * 1024 * 1024


def _params(n_axes, vmem=VMEM_LIMIT):
    return pltpu.CompilerParams(dimension_semantics=("arbitrary",) * n_axes, vmem_limit_bytes=vmem)


def _dot(a, b):
    return jnp.dot(a.astype(MXU_DTYPE), b.astype(MXU_DTYPE), preferred_element_type=F32)


def _dot_nt(a, b):
    return lax.dot_general(a.astype(MXU_DTYPE), b.astype(MXU_DTYPE), (((1,), (1,)), ((), ())),
                           preferred_element_type=F32)


def _dot_tn(a, b):
    return lax.dot_general(a.astype(MXU_DTYPE), b.astype(MXU_DTYPE), (((0,), (0,)), ((), ())),
                           preferred_element_type=F32)


def _split(x, parts):
    out, r = [], x
    for _ in range(parts):
        h = r.astype(BF16)
        out.append(h)
        r = r - h.astype(F32)
    return out


def _rsum_right(x, u, parts):
    acc = None
    for h in _split(x, parts):
        d = jnp.dot(h, u, preferred_element_type=F32)
        acc = d if acc is None else acc + d
    return acc


def _rsum_left(u, x, parts):
    acc = None
    for h in _split(x, parts):
        d = jnp.dot(u, h, preferred_element_type=F32)
        acc = d if acc is None else acc + d
    return acc


def _ones_where(mask):
    return jnp.where(mask, 1.0, 0.0).astype(BF16)


def _sigmoid(x):
    return 1.0 / (1.0 + jnp.exp(-x))


def _softplus(x):
    return jnp.maximum(x, 0.0) + jnp.log(1.0 + jnp.exp(-jnp.abs(x)))


def _head_masks():
    lane = lax.broadcasted_iota(jnp.int32, (1, LANES), 1)
    return [jnp.where(lane < D_HEAD, 1.0, 0.0), jnp.where(lane >= D_HEAD, 1.0, 0.0)]


def _head_rstd(o, masks):
    sq = o * o
    r = None
    for m in masks:
        ms = jnp.sum(sq * m, axis=1, keepdims=True) * (1.0 / D_HEAD)
        t = lax.rsqrt(ms + EPS) * m
        r = t if r is None else r + t
    return r


def _head_mean(t, masks):
    out = None
    for m in masks:
        v = jnp.sum(t * m, axis=1, keepdims=True) * (1.0 / D_HEAD) * m
        out = v if out is None else out + v
    return out


def _rmsnorm_fwd(x, g, name):
    t, d = x.shape
    tm = min(512, t)

    def body(x_ref, g_ref, o_ref):
        xv = x_ref[...]
        r = lax.rsqrt(jnp.mean(xv * xv, axis=-1, keepdims=True) + EPS)
        o_ref[...] = (xv * r * g_ref[...]).astype(o_ref.dtype)

    return pl.pallas_call(
        body, name=name, grid=(t // tm,),
        in_specs=[pl.BlockSpec((tm, d), lambda i: (i, 0)), pl.BlockSpec((1, d), lambda i: (0, 0))],
        out_specs=pl.BlockSpec((tm, d), lambda i: (i, 0)),
        out_shape=jax.ShapeDtypeStruct((t, d), MXU_DTYPE),
        compiler_params=_params(1),
    )(x, g)


def _mix_out_norm(a, w, res, g, name, tm=512):
    m, k = a.shape
    d = w.shape[1]
    tm = min(tm, m)

    def body(a_ref, w_ref, res_ref, g_ref, x_ref, h_ref):
        xv = res_ref[...] + _dot(a_ref[...], w_ref[...])
        x_ref[...] = xv
        r = lax.rsqrt(jnp.mean(xv * xv, axis=-1, keepdims=True) + EPS)
        h_ref[...] = (xv * r * g_ref[...]).astype(h_ref.dtype)

    row = pl.BlockSpec((tm, d), lambda i: (i, 0))
    return pl.pallas_call(
        body, name=name, grid=(m // tm,),
        in_specs=[pl.BlockSpec((tm, k), lambda i: (i, 0)), pl.BlockSpec((k, d), lambda i: (0, 0)), row,
                  pl.BlockSpec((1, d), lambda i: (0, 0))],
        out_specs=[row, row],
        out_shape=[jax.ShapeDtypeStruct((m, d), F32), jax.ShapeDtypeStruct((m, d), MXU_DTYPE)],
        compiler_params=_params(1),
    )(a, w, res, g)


def _ffn_down_loss(a, w, res, target, g, name, tm=512):
    m, k = a.shape
    d = w.shape[1]
    tm = min(tm, m)

    def body(a_ref, w_ref, res_ref, t_ref, g_ref, dx_ref, dxm_ref, dg_ref, loss_ref):
        xv = res_ref[...] + _dot(a_ref[...], w_ref[...])
        gv = g_ref[...]
        r = lax.rsqrt(jnp.mean(xv * xv, axis=-1, keepdims=True) + EPS)
        xh = xv * r
        e = xh * gv - t_ref[...]
        dy = e * (1.0 / d)
        dxh = dy * gv
        dxv = r * (dxh - xh * jnp.mean(dxh * xh, axis=-1, keepdims=True))
        dx_ref[...] = dxv
        dxm_ref[...] = dxv.astype(dxm_ref.dtype)

        @pl.when(pl.program_id(0) == 0)
        def _():
            dg_ref[...] = jnp.zeros_like(dg_ref)
            loss_ref[...] = jnp.zeros_like(loss_ref)

        dg_ref[...] += jnp.sum(dy * xh, axis=0, keepdims=True)
        part = 0.5 * jnp.sum(jnp.mean(e * e, axis=-1, keepdims=True), axis=0, keepdims=True)
        loss_ref[...] += jnp.broadcast_to(part, loss_ref.shape)

    row = pl.BlockSpec((tm, d), lambda i: (i, 0))
    vec = pl.BlockSpec((1, d), lambda i: (0, 0))
    return pl.pallas_call(
        body, name=name, grid=(m // tm,),
        in_specs=[pl.BlockSpec((tm, k), lambda i: (i, 0)), pl.BlockSpec((k, d), lambda i: (0, 0)), row, row, vec],
        out_specs=[row, row, vec, pl.BlockSpec((1, LANES), lambda i: (0, 0))],
        out_shape=[jax.ShapeDtypeStruct((m, d), F32), jax.ShapeDtypeStruct((m, d), MXU_DTYPE),
                   jax.ShapeDtypeStruct((1, d), F32), jax.ShapeDtypeStruct((1, LANES), F32)],
        compiler_params=_params(1),
    )(a, w, res, target, g)


def _mm_nt(a, w, name, rows=None, out_dtype=F32, tm=512, tk=512):
    m, n = a.shape
    row0, k = rows or (0, w.shape[0])
    tm, tk = min(tm, m), min(tk, k)
    first = row0 // tk

    def body(a_ref, w_ref, o_ref):
        o_ref[...] = _dot_nt(a_ref[...], w_ref[...]).astype(o_ref.dtype)

    return pl.pallas_call(
        body, name=name, grid=(k // tk, m // tm),
        in_specs=[pl.BlockSpec((tm, n), lambda j, i: (i, 0)), pl.BlockSpec((tk, n), lambda j, i: (first + j, 0))],
        out_specs=pl.BlockSpec((tm, tk), lambda j, i: (i, j)),
        out_shape=jax.ShapeDtypeStruct((m, k), out_dtype),
        compiler_params=_params(2),
    )(a, w)


def _mm_tn(a, b, name, tk, tn, tt=512, out_dtype=BF16):
    t, k = a.shape
    n = b.shape[1]
    tt = min(tt, t)
    steps = t // tt

    def body(a_ref, b_ref, o_ref, acc):
        s = pl.program_id(2)

        @pl.when(s == 0)
        def _():
            acc[...] = jnp.zeros_like(acc)

        acc[...] += _dot_tn(a_ref[...], b_ref[...])

        @pl.when(s == steps - 1)
        def _():
            o_ref[...] = acc[...].astype(o_ref.dtype)

    return pl.pallas_call(
        body, name=name, grid=(k // tk, n // tn, steps),
        in_specs=[pl.BlockSpec((tt, tk), lambda i, j, s: (s, i)), pl.BlockSpec((tt, tn), lambda i, j, s: (s, j))],
        out_specs=pl.BlockSpec((tk, tn), lambda i, j, s: (i, j)),
        out_shape=jax.ShapeDtypeStruct((k, n), out_dtype),
        scratch_shapes=[pltpu.VMEM((tk, tn), F32)],
        compiler_params=_params(3),
    )(a, b)


def _ffn_up(h, wg_t, wu_t, name, tm=512, tn=FF_TILE):
    m, k = h.shape
    n = wg_t.shape[0]
    tm = min(tm, m)

    def body(h_ref, wg_ref, wu_ref, gate_ref, up_ref, ff_ref):
        hv = h_ref[...]
        gate = _dot_nt(hv, wg_ref[...])
        up = _dot_nt(hv, wu_ref[...])
        gate_ref[...] = gate.astype(gate_ref.dtype)
        up_ref[...] = up.astype(up_ref.dtype)
        ff_ref[...] = (gate * _sigmoid(gate) * up).astype(ff_ref.dtype)

    wspec = pl.BlockSpec((tn, k), lambda j, i: (j, 0))
    ospec = pl.BlockSpec((tm, tn), lambda j, i: (i, j))
    return pl.pallas_call(
        body, name=name, grid=(n // tn, m // tm),
        in_specs=[pl.BlockSpec((tm, k), lambda j, i: (i, 0)), wspec, wspec],
        out_specs=[ospec, ospec, ospec],
        out_shape=[jax.ShapeDtypeStruct((m, n), MXU_DTYPE)] * 3,
        compiler_params=_params(2),
    )(h, wg_t, wu_t)


def _ffn_bwd_act(dx, wd, gate, up, name, tm=512, tn=FF_TILE):
    m, k = dx.shape
    n = wd.shape[0]
    tm = min(tm, m)

    def body(dx_ref, wd_ref, gate_ref, up_ref, dgate_ref, dup_ref):
        dff = _dot_nt(dx_ref[...], wd_ref[...])
        gate = gate_ref[...].astype(F32)
        sg = _sigmoid(gate)
        dgate_ref[...] = (dff * up_ref[...].astype(F32) * sg * (1.0 + gate * (1.0 - sg))).astype(dgate_ref.dtype)
        dup_ref[...] = (dff * gate * sg).astype(dup_ref.dtype)

    ospec = pl.BlockSpec((tm, tn), lambda j, i: (i, j))
    return pl.pallas_call(
        body, name=name, grid=(n // tn, m // tm),
        in_specs=[pl.BlockSpec((tm, k), lambda j, i: (i, 0)), pl.BlockSpec((tn, k), lambda j, i: (j, 0)),
                  ospec, ospec],
        out_specs=[ospec, ospec],
        out_shape=[jax.ShapeDtypeStruct((m, n), MXU_DTYPE), jax.ShapeDtypeStruct((m, n), MXU_DTYPE)],
        compiler_params=_params(2),
    )(dx, wd, gate, up)


def _chunk_masks():
    r = lax.broadcasted_iota(jnp.int32, (SUPER, SUPER), 0)
    c = lax.broadcasted_iota(jnp.int32, (SUPER, SUPER), 1)
    same = jnp.right_shift(r, 6) == jnp.right_shift(c, 6)
    lower = jnp.logical_and(same, c <= r)
    upper = jnp.logical_and(same, c >= r)
    return same, lower, upper


def _head_block_mask():
    r = lax.broadcasted_iota(jnp.int32, (LANES, LANES), 0)
    c = lax.broadcasted_iota(jnp.int32, (LANES, LANES), 1)
    return jnp.where(jnp.right_shift(r, 6) == jnp.right_shift(c, 6), 1.0, 0.0)


def _lower_bound(lb_raw):
    return 1.0 / (1.0 + jnp.exp(lb_raw[1:2, :] - lb_raw[0:1, :]))


PER_SUPER = SUPER // CHUNK
CHUNK_ROWS = [slice(c * CHUNK, (c + 1) * CHUNK) for c in range(PER_SUPER)]


def _over_chunks(rows):
    return jnp.concatenate([jnp.broadcast_to(r, (CHUNK, LANES)) for r in rows], axis=0)


def _hgrn_gates(q, hf, lb, lower_b):
    sig = _sigmoid(hf)
    f = lb + (1.0 - lb) * sig
    k = 1.0 - f
    lf = jnp.log(f)
    b = _rsum_left(lower_b, lf, 2)
    ends = [b[cr.stop - 1:cr.stop, :] for cr in CHUNK_ROWS]
    eb = jnp.exp(b)
    enb = jnp.exp(-b)
    edb = jnp.exp(_over_chunks(ends) - b)
    decs = [jnp.exp(e) for e in ends]
    return sig, f, k, decs, eb, enb, edb, q * eb, k * enb, k * edb


def _hgrn_fwd(proj, lower_bounds, norm_g, n_seq, seq):
    t = n_seq * seq
    n_super = seq // SUPER
    n_chunks = seq // CHUNK

    def body(q_ref, f_ref, i_ref, g_ref, lb_ref, ng_ref, out_ref, opre_ref, st_ref):
        masks = _head_masks()
        _, lower, _ = _chunk_masks()
        lower_b = _ones_where(lower)
        bd = _head_block_mask()
        lb = _lower_bound(lb_ref[...])
        ng = ng_ref[...]

        def step(sb, st):
            rows = pl.ds(pl.multiple_of(sb * SUPER, SUPER), SUPER)
            q, hf, v, hg = q_ref[rows, :], f_ref[rows, :], i_ref[rows, :], g_ref[rows, :]
            _, _, _, decs, _, _, _, qe, ke, kd = _hgrn_gates(q, hf, lb, lower_b)
            scores = [_dot_nt(qe * m, ke) for m in masks]
            updates = [_dot_tn(v[cr], kd[cr]) for cr in CHUNK_ROWS]
            states = [st]
            for dec, upd in zip(decs, updates):
                states.append(states[-1] * dec + bd * upd)
            for c in range(PER_SUPER):
                st_ref[0, 0, sb * PER_SUPER + c] = states[c]
            intra = [_dot(jnp.where(lower, p, 0.0), v) for p in scores]
            inter = [_dot_nt(qe[cr], s) for cr, s in zip(CHUNK_ROWS, states)]
            o = intra[0] * masks[0] + intra[1] * masks[1] + jnp.concatenate(inter, axis=0)
            opre_ref[rows, :] = o
            on = o * _head_rstd(o, masks) * ng
            out_ref[rows, :] = (on * hg * _sigmoid(hg)).astype(out_ref.dtype)
            return states[-1]

        lax.fori_loop(0, n_super, step, jnp.zeros((LANES, LANES), F32))

    def col(k):
        return pl.BlockSpec((seq, LANES), lambda p, b: (b, k * N_PAIRS + p))

    vec = lambda rows: pl.BlockSpec((rows, LANES), lambda p, b: (0, p))
    ospec = pl.BlockSpec((seq, LANES), lambda p, b: (b, p))
    return pl.pallas_call(
        body, name="hgrn_fwd", grid=(N_PAIRS, n_seq),
        in_specs=[col(0), col(1), col(2), col(3), vec(2), vec(1)],
        out_specs=[ospec, ospec,
                   pl.BlockSpec((1, 1, n_chunks, LANES, LANES), lambda p, b: (b, p, 0, 0, 0))],
        out_shape=[jax.ShapeDtypeStruct((t, 2 * GROUP), MXU_DTYPE), jax.ShapeDtypeStruct((t, GROUP), F32),
                   jax.ShapeDtypeStruct((n_seq, N_PAIRS, n_chunks, LANES, LANES), F32)],
        compiler_params=_params(2),
    )(proj, proj, proj, proj, lower_bounds, norm_g)


def _hgrn_bwd(proj, lower_bounds, norm_g, dmix, opre, states, n_seq, seq):
    t = n_seq * seq
    n_super = seq // SUPER
    n_chunks = seq // CHUNK
    per = SUPER // CHUNK

    def body(q_ref, f_ref, i_ref, g_ref, lb_ref, ng_ref, dm_ref, opre_ref, st_ref,
             dq_ref, df_ref, di_ref, dg_ref, dlb_ref, dng_ref):
        masks = _head_masks()
        _, lower, upper = _chunk_masks()
        lower_b, upper_b = _ones_where(lower), _ones_where(upper)
        bd = _head_block_mask()
        lb_raw = lb_ref[...]
        lb = _lower_bound(lb_raw)
        ng = ng_ref[...]

        @pl.when(pl.program_id(1) == 0)
        def _():
            dlb_ref[...] = jnp.zeros_like(dlb_ref)
            dng_ref[...] = jnp.zeros_like(dng_ref)

        def step(it, dst):
            sb = n_super - 1 - it
            rows = pl.ds(pl.multiple_of(sb * SUPER, SUPER), SUPER)
            q, hf, v, hg = q_ref[rows, :], f_ref[rows, :], i_ref[rows, :], g_ref[rows, :]
            sig, f, k, decs, eb, enb, edb, qe, ke, kd = _hgrn_gates(q, hf, lb, lower_b)
            o = opre_ref[rows, :]
            r = _head_rstd(o, masks)
            oh = o * r
            dm = dm_ref[rows, :]
            sg = _sigmoid(hg)
            dg_ref[rows, :] = (dm * oh * ng * sg * (1.0 + hg * (1.0 - sg))).astype(dg_ref.dtype)
            don = dm * hg * sg
            dng_ref[...] += jnp.sum(don * oh, axis=0, keepdims=True)
            doh = don * ng
            do = r * (doh - oh * _head_mean(doh * oh, masks))
            doms = [do * m for m in masks]
            qems = [qe * m for m in masks]
            scores = [_dot_nt(qem, ke) for qem in qems]
            dscores = [_dot_nt(dom, v) for dom in doms]
            prevs = [st_ref[0, 0, sb * per + c] for c in range(per)]
            dst_in = [_dot_tn(do[cr], qe[cr]) for cr in CHUNK_ROWS]
            dqe_i = [_dot(do[cr], prev) for cr, prev in zip(CHUNK_ROWS, prevs)]
            dsts = [None] * per
            for c in reversed(range(per)):
                dsts[c] = dst
                dst = bd * (dst * decs[c] + dst_in[c])
            ps = [jnp.where(lower, p, 0.0) for p in scores]
            dps = [jnp.where(lower, dp, 0.0) for dp in dscores]
            dqe_h = [_dot(dp, ke) for dp in dps]
            dke_h = [_dot_tn(dp, qem) for dp, qem in zip(dps, qems)]
            dv_h = [_dot_tn(p, dom) for p, dom in zip(ps, doms)]
            dus = [bd * d for d in dsts]
            dv_i = [_dot_nt(kd[cr], du) for cr, du in zip(CHUNK_ROWS, dus)]
            dkd_i = [_dot(v[cr], du) for cr, du in zip(CHUNK_ROWS, dus)]
            dqe = dqe_h[0] * masks[0] + dqe_h[1] * masks[1] + jnp.concatenate(dqe_i, axis=0)
            dke = dke_h[0] + dke_h[1]
            dv = dv_h[0] + dv_h[1] + jnp.concatenate(dv_i, axis=0)
            dkd = jnp.concatenate(dkd_i, axis=0)
            dk = dke * enb + dkd * edb
            db = dqe * qe - dke * ke - dkd * kd
            dkd_kd = dkd * kd
            dends = [jnp.sum(dkd_kd[cr], axis=0, keepdims=True)
                     + jnp.sum(dsts[c] * prevs[c], axis=0, keepdims=True) * decs[c]
                     for c, cr in enumerate(CHUNK_ROWS)]
            dlf = _rsum_left(upper_b, db, 2) + _over_chunks(dends)
            dfv = dlf / f - dk
            dq_ref[rows, :] = (dqe * eb).astype(dq_ref.dtype)
            di_ref[rows, :] = dv.astype(di_ref.dtype)
            df_ref[rows, :] = (dfv * (1.0 - lb) * sig * (1.0 - sig)).astype(df_ref.dtype)
            dlb = jnp.sum(dfv * (1.0 - sig), axis=0, keepdims=True)
            da0 = dlb * lb * (1.0 - lb)
            dlb_ref[0:1, :] += da0
            dlb_ref[1:2, :] -= da0
            return dst

        lax.fori_loop(0, n_super, step, jnp.zeros((LANES, LANES), F32))

    def col(k):
        return pl.BlockSpec((seq, LANES), lambda p, b: (b, k * N_PAIRS + p))

    vec = lambda rows: pl.BlockSpec((rows, LANES), lambda p, b: (0, p))
    ospec = pl.BlockSpec((seq, LANES), lambda p, b: (b, p))
    piece = jax.ShapeDtypeStruct((t, GROUP), MXU_DTYPE)
    return pl.pallas_call(
        body, name="hgrn_bwd", grid=(N_PAIRS, n_seq),
        in_specs=[col(0), col(1), col(2), col(3), vec(2), vec(1), ospec, ospec,
                  pl.BlockSpec((1, 1, n_chunks, LANES, LANES), lambda p, b: (b, p, 0, 0, 0))],
        out_specs=[ospec, ospec, ospec, ospec, vec(2), vec(1)],
        out_shape=[piece, piece, piece, piece,
                   jax.ShapeDtypeStruct((2, GROUP), F32), jax.ShapeDtypeStruct((1, GROUP), F32)],
        compiler_params=_params(2),
    )(proj, proj, proj, proj, lower_bounds, norm_g, dmix, opre, states)


SB_SCALE = 1.0 / math.sqrt(D_HEAD)


SB_STEP = 2 * SB_BLOCK
QUERY_BLOCKS = (slice(0, SB_BLOCK), slice(SB_BLOCK, SB_STEP))


def _triangle(keep):
    row = lax.broadcasted_iota(jnp.int32, (2 * SB_BLOCK, SB_BLOCK), 0) % SB_BLOCK
    col = lax.broadcasted_iota(jnp.int32, (2 * SB_BLOCK, SB_BLOCK), 1)
    return _ones_where(keep(row, col))


def _tile_masks():
    r = lax.broadcasted_iota(jnp.int32, (SB_BLOCK, SB_BLOCK), 0)
    c = lax.broadcasted_iota(jnp.int32, (SB_BLOCK, SB_BLOCK), 1)
    return r, c


def _sb_fwd(proj, norm_g, mixed, shards, n_seq, seq):
    t = n_seq * seq
    nq = seq // SB_STEP
    q0, k0, v0 = 0, N_PAIRS, 2 * N_PAIRS
    n_w = len(shards)
    n_steps = N_PAIRS * n_seq * nq
    tri = _triangle(lambda row, col: row >= col)

    def body(q_ref, k_ref, v_ref, ng_ref, tri_ref, mixed_in, *rest):
        del mixed_in
        shard_refs = rest[:n_w]
        out_ref, opre_ref, ctot_ref = rest[n_w:n_w + 3]
        gathered = rest[n_w + 3:2 * n_w + 3]
        send_sems, recv_sems, local_sems = rest[2 * n_w + 3:]
        i = pl.program_id(2)
        step = (pl.program_id(0) * n_seq + pl.program_id(1)) * nq + i
        plan = _GatherPlan(shard_refs, gathered, send_sems, recv_sems, local_sems)

        @pl.when(step == 0)
        def _():
            plan.start()

        @pl.when(step == n_steps // 2)
        def _():
            plan.forward()

        masks = _head_masks()
        r, c = _tile_masks()
        strict = c < r
        suffix2 = tri_ref[...]
        qhs = [[(q_ref[blk, :] * SB_SCALE * m).astype(MXU_DTYPE) for m in masks] for blk in QUERY_BLOCKS]

        def tiles(js, work, carry):
            rows = [pl.ds(pl.multiple_of(j * SB_BLOCK, SB_BLOCK), SB_BLOCK) for j in js]
            ks = [k_ref[rw, :].astype(MXU_DTYPE) for rw in rows]
            vs = [v_ref[rw, :].astype(MXU_DTYPE) for rw in rows]
            zs = [[_dot_nt(qh, ks[ts]) for qh in qhs[qb]] for qb, ts, _ in work]
            ccs = [[_softplus(z) for z in zw] for zw in zs]
            ccs = [[jnp.where(strict, cc, 0.0) for cc in cw] if diag else cw for cw, (_, _, diag) in zip(ccs, work)]
            sums = [[jnp.dot(jnp.concatenate(_split(cc, 2), axis=1), suffix2, preferred_element_type=F32)
                     for cc in cw] for cw in ccs]
            out = [list(per_block) for per_block in carry]
            for h in range(len(masks)):
                for w, (qb, ts, diag) in enumerate(work):
                    run, acc = out[qb][h]
                    a = jnp.exp(zs[w][h] - (sums[w][h] + run))
                    if diag:
                        a = jnp.where(strict, a, 0.0)
                    out[qb][h] = (run + sums[w][h][:, 0:1], acc + _dot(a, vs[ts]))
            return tuple(tuple(per_block) for per_block in out)

        start = ((jnp.zeros((SB_BLOCK, 1), F32), jnp.zeros((SB_BLOCK, LANES), F32)),) * 2
        carry = tiles([2 * i, 2 * i + 1], [(0, 0, True), (1, 1, True), (1, 0, False)], (start, start))
        both = [(0, 0, False), (0, 1, False), (1, 0, False), (1, 1, False)]
        carry = lax.fori_loop(0, i, lambda s, cy: tiles([2 * (i - s) - 1, 2 * (i - s) - 2], both, cy), carry)
        opre = jnp.concatenate([cb[0][1] * masks[0] + cb[1][1] * masks[1] for cb in carry], axis=0)
        ctot = jnp.concatenate([cb[0][0] * masks[0] + cb[1][0] * masks[1] for cb in carry], axis=0)
        opre_ref[...] = opre
        ctot_ref[...] = ctot
        out_ref[...] = (opre * _head_rstd(opre, masks) * ng_ref[...]).astype(out_ref.dtype)

        @pl.when(step == n_steps - 1)
        def _():
            plan.finish()

    qspec = pl.BlockSpec((SB_STEP, LANES), lambda p, b, i: (b * nq + i, q0 + p))
    ospec = pl.BlockSpec((SB_STEP, LANES), lambda p, b, i: (b * nq + i, p))
    hbm = pl.BlockSpec(memory_space=pltpu.HBM)
    outs = pl.pallas_call(
        body, name="sb_fwd", grid=(N_PAIRS, n_seq, nq),
        in_specs=[qspec,
                  pl.BlockSpec((seq, LANES), lambda p, b, i: (b, k0 + p)),
                  pl.BlockSpec((seq, LANES), lambda p, b, i: (b, v0 + p)),
                  pl.BlockSpec((1, LANES), lambda p, b, i: (0, p)),
                  pl.BlockSpec(tri.shape, lambda p, b, i: (0, 0)), hbm] + [hbm] * n_w,
        out_specs=[pl.BlockSpec((SB_STEP, LANES), lambda p, b, i: (b * nq + i, N_PAIRS + p)), ospec, ospec]
        + [hbm] * n_w,
        out_shape=[jax.ShapeDtypeStruct(mixed.shape, mixed.dtype), jax.ShapeDtypeStruct((t, GROUP), F32),
                   jax.ShapeDtypeStruct((t, GROUP), F32)]
        + [jax.ShapeDtypeStruct((N_DEV,) + s.shape, s.dtype) for s in shards],
        scratch_shapes=[pltpu.SemaphoreType.DMA((n_w * _GatherPlan.COPIES,)),
                        pltpu.SemaphoreType.DMA((n_w * _GatherPlan.COPIES,)), pltpu.SemaphoreType.DMA((n_w,))],
        input_output_aliases={5: 0},
        compiler_params=_params(3),
    )(proj, proj, proj, norm_g, tri, mixed, *shards)
    return outs[0], outs[1], outs[2], list(outs[3:])


def _sb_bwd(proj, norm_g, dmix, opre, ctot, grads, n_seq, seq):
    t = n_seq * seq
    nq = seq // SB_STEP
    q0, k0, v0 = 0, N_PAIRS, 2 * N_PAIRS
    n_w = len(grads)
    n_steps = N_PAIRS * n_seq * nq
    tri = _triangle(lambda row, col: row <= col)

    def body(q_ref, k_ref, v_ref, ng_ref, tri_ref, dm_ref, opre_ref, ctot_ref, *rest):
        grad_refs = rest[:n_w]
        dq_ref, dk_ref, dv_ref, dng_ref = rest[n_w:n_w + 4]
        lands = rest[n_w + 4:2 * n_w + 4]
        dk_acc, dv_acc, send_sems, recv_sems, local_sems = rest[2 * n_w + 4:]
        p_id, b_id, i = pl.program_id(0), pl.program_id(1), pl.program_id(2)
        step = (p_id * n_seq + b_id) * nq + i
        plan = _ScatterPlan(grad_refs, lands, send_sems, recv_sems, local_sems)

        @pl.when(step == 0)
        def _():
            plan.start()

        masks = _head_masks()
        r, c = _tile_masks()
        strict = c < r
        upto2 = tri_ref[...]
        upto = tri_ref[0:SB_BLOCK, :]

        def prefix(x):
            return jnp.dot(jnp.concatenate(_split(x, 2), axis=1), upto2, preferred_element_type=F32)

        @pl.when(i == 0)
        def _():
            dk_acc[...] = jnp.zeros_like(dk_acc)
            dv_acc[...] = jnp.zeros_like(dv_acc)

        @pl.when(jnp.logical_and(b_id == 0, i == 0))
        def _():
            dng_ref[...] = jnp.zeros_like(dng_ref)

        o = opre_ref[...]
        rs = _head_rstd(o, masks)
        oh = o * rs
        dm = dm_ref[...]
        dng_ref[...] += jnp.sum(dm * oh, axis=0, keepdims=True)
        doh = dm * ng_ref[...]
        do = rs * (doh - oh * _head_mean(doh * oh, masks))

        heads = range(len(masks))
        qs = [q_ref[blk, :] * SB_SCALE for blk in QUERY_BLOCKS]
        dos = [do[blk] for blk in QUERY_BLOCKS]
        qhs = [[(q * m).astype(MXU_DTYPE) for m in masks] for q in qs]
        doms = [[(d * m).astype(MXU_DTYPE) for m in masks] for d in dos]
        qhts = [[(q * m).T.astype(MXU_DTYPE) for m in masks] for q in qs]
        domts = [[(d * m).T.astype(MXU_DTYPE) for m in masks] for d in dos]
        totals = [[jnp.max(ctot_ref[blk, :] * m, axis=1, keepdims=True) for m in masks] for blk in QUERY_BLOCKS]

        def tiles(js, work, carry):
            rows = [pl.ds(pl.multiple_of(j * SB_BLOCK, SB_BLOCK), SB_BLOCK) for j in js]
            ks = [k_ref[rw, :].astype(MXU_DTYPE) for rw in rows]
            vs = [v_ref[rw, :].astype(MXU_DTYPE) for rw in rows]
            zs = [[_dot_nt(qhs[qb][h], ks[ts]) for h in heads] for qb, ts, _ in work]
            das = [[_dot_nt(doms[qb][h], vs[ts]) for h in heads] for qb, ts, _ in work]
            sps = [[_softplus(z) for z in zw] for zw in zs]
            lsigs = [[z - sp for z, sp in zip(zw, sw)] for zw, sw in zip(zs, sps)]
            sigs = [[jnp.exp(ls) for ls in lw] for lw in lsigs]
            ccs = [[jnp.where(strict, sp, 0.0) for sp in sw] if diag else sw for sw, (_, _, diag) in zip(sps, work)]
            pres = [[prefix(cc) for cc in cw] for cw in ccs]
            out = [list(per_block) for per_block in carry]
            for h in heads:
                for w, (qb, ts, diag) in enumerate(work):
                    pc, pdl, dq_h = out[qb][h]
                    a = jnp.exp(lsigs[w][h] + pres[w][h] - (totals[qb][h] - pc))
                    if diag:
                        a = jnp.where(strict, a, 0.0)
                    dl = a * das[w][h]
                    dv_acc[js[ts]] += _dot(domts[qb][h], a)
                    dpre = jnp.dot(dl.astype(BF16), upto, preferred_element_type=F32)
                    dz = dl - sigs[w][h] * (pdl + dpre)
                    if diag:
                        dz = jnp.where(strict, dz, 0.0)
                    dzb = dz.astype(MXU_DTYPE)
                    dk_acc[js[ts]] += _dot(qhts[qb][h], dzb)
                    out[qb][h] = (pc + pres[w][h][:, SB_BLOCK - 1:SB_BLOCK], pdl + dpre[:, SB_BLOCK - 1:SB_BLOCK],
                                  dq_h + _dot(dzb, ks[ts]))
            return tuple(tuple(per_block) for per_block in out)

        zero = jnp.zeros((SB_BLOCK, 1), F32)
        start = ((zero, zero, jnp.zeros((SB_BLOCK, LANES), F32)),) * 2
        both = [(0, 0, False), (0, 1, False), (1, 0, False), (1, 1, False)]
        carry = lax.fori_loop(0, i, lambda s, cy: tiles([2 * s, 2 * s + 1], both, cy), (start, start))
        carry = tiles([2 * i, 2 * i + 1], [(0, 0, True), (1, 0, False), (1, 1, True)], carry)
        dq = jnp.concatenate([cb[0][2] * masks[0] + cb[1][2] * masks[1] for cb in carry], axis=0)
        dq_ref[...] = (dq * SB_SCALE).astype(dq_ref.dtype)

        @pl.when(i == nq - 1)
        def _():
            for j in range(seq // SB_BLOCK):
                tile_rows = slice(j * SB_BLOCK, (j + 1) * SB_BLOCK)
                dk_ref[tile_rows, :] = dk_acc[j].T.astype(dk_ref.dtype)
                dv_ref[tile_rows, :] = dv_acc[j].T.astype(dv_ref.dtype)

        @pl.when(step == n_steps - 1)
        def _():
            plan.finish()

    qspec = pl.BlockSpec((SB_STEP, LANES), lambda p, b, i: (b * nq + i, q0 + p))
    ospec = pl.BlockSpec((SB_STEP, LANES), lambda p, b, i: (b * nq + i, p))
    dmspec = pl.BlockSpec((SB_STEP, LANES), lambda p, b, i: (b * nq + i, N_PAIRS + p))
    full = lambda k: pl.BlockSpec((seq, LANES), lambda p, b, i: (b, k + p))
    vec = pl.BlockSpec((1, LANES), lambda p, b, i: (0, p))
    hbm = pl.BlockSpec(memory_space=pltpu.HBM)
    piece = jax.ShapeDtypeStruct((t, GROUP), MXU_DTYPE)
    outs = pl.pallas_call(
        body, name="sb_bwd", grid=(N_PAIRS, n_seq, nq),
        in_specs=[qspec, full(k0), full(v0), vec, pl.BlockSpec(tri.shape, lambda p, b, i: (0, 0)), dmspec, ospec, ospec]
        + [hbm] * n_w,
        out_specs=[ospec, full(0), full(0), vec] + [hbm] * n_w,
        out_shape=[piece, piece, piece, jax.ShapeDtypeStruct((1, GROUP), F32)]
        + [jax.ShapeDtypeStruct(g.shape, g.dtype) for g in grads],
        scratch_shapes=[pltpu.VMEM((seq // SB_BLOCK, LANES, SB_BLOCK), F32),
                        pltpu.VMEM((seq // SB_BLOCK, LANES, SB_BLOCK), F32),
                        pltpu.SemaphoreType.DMA((n_w * (N_DEV - 1),)), pltpu.SemaphoreType.DMA((n_w * (N_DEV - 1),)),
                        pltpu.SemaphoreType.DMA((n_w,))],
        compiler_params=_params(3),
    )(proj, proj, proj, norm_g, tri, dmix, opre, ctot, *grads)
    return outs[0], outs[1], outs[2], outs[3], list(outs[4:])


def _mesh_place():
    x, y, c = lax.axis_index("x"), lax.axis_index("y"), lax.axis_index("c")
    return x, y, c


def _peer(x, y, c, k):
    px = lax.rem(x + ((k >> 2) & 1), 2)
    py = lax.rem(y + ((k >> 1) & 1), 2)
    pc = lax.rem(c + (k & 1), 2)
    return (px, py, pc), 4 * px + 2 * py + pc


def _remote(src, dst, send_sem, recv_sem, to):
    return pltpu.make_async_remote_copy(src_ref=src, dst_ref=dst, send_sem=send_sem, recv_sem=recv_sem,
                                        device_id=to, device_id_type=pl.DeviceIdType.MESH)


class _GatherPlan:
    COPIES = 7

    def __init__(self, shards, gathered, send_sems, recv_sems, local_sems):
        x, y, c = _mesh_place()
        self.c = c
        self.me = (x, y, c)
        self.sibling = (x, y, 1 - c)
        self.chips = [(1 - x, y), (x, 1 - y), (1 - x, 1 - y)]
        self.tensors = list(zip(shards, gathered))
        self.send_sems, self.recv_sems, self.local_sems = send_sems, recv_sems, local_sems

    @staticmethod
    def _index(place):
        return 4 * place[0] + 2 * place[1] + place[2]

    def _copy(self, w, k, block, to, own=False):
        shard, gathered = self.tensors[w]
        slot = gathered.at[self._index(block)]
        n = w * self.COPIES + k
        return _remote(shard if own else slot, slot, self.send_sems.at[n], self.recv_sems.at[n], to)

    def _local(self, w):
        shard, gathered = self.tensors[w]
        return pltpu.make_async_copy(shard, gathered.at[self._index(self.me)], self.local_sems.at[w])

    def _first(self, w):
        return [self._copy(w, 0, self.me, self.sibling, own=True)] + [
            self._copy(w, 1 + j, self.me, (*chip, self.c), own=True) for j, chip in enumerate(self.chips)]

    def _passed(self, w):
        return [self._copy(w, 4 + j, (*chip, self.c), self.sibling) for j, chip in enumerate(self.chips)]

    def start(self):
        for w in range(len(self.tensors)):
            self._local(w).start()
            for cp in self._first(w):
                cp.start()

    def forward(self):
        for w in range(len(self.tensors)):
            passed = self._passed(w)
            for j, chip in enumerate(self.chips):
                self._copy(w, 1 + j, (*chip, self.c), self.me).wait_recv()
                passed[j].start()

    def finish(self):
        for w in range(len(self.tensors)):
            self._copy(w, 0, self.sibling, self.me).wait_recv()
            for j, chip in enumerate(self.chips):
                self._copy(w, 4 + j, (*chip, 1 - self.c), self.me).wait_recv()
            for cp in self._first(w) + self._passed(w):
                cp.wait_send()
            self._local(w).wait()


class _ScatterPlan:
    def __init__(self, grads, lands, send_sems, recv_sems, local_sems):
        self.place = _mesh_place()
        x, y, c = self.place
        self.me = 4 * x + 2 * y + c
        self.tensors = list(zip(grads, lands))
        self.send_sems, self.recv_sems, self.local_sems = send_sems, recv_sems, local_sems

    def _copies(self, w):
        grad, land = self.tensors[w]
        out = []
        for k in range(1, N_DEV):
            peer, pidx = _peer(*self.place, k)
            n = w * (N_DEV - 1) + k - 1
            sems = (self.send_sems.at[n], self.recv_sems.at[n], peer)
            out.append((_remote(grad.at[pidx], land.at[self.me], *sems), _remote(grad.at[pidx], land.at[pidx], *sems)))
        return out

    def _local(self, w):
        grad, land = self.tensors[w]
        return pltpu.make_async_copy(grad.at[self.me], land.at[self.me], self.local_sems.at[w])

    def start(self):
        for w in range(len(self.tensors)):
            self._local(w).start()
            for send, _ in self._copies(w):
                send.start()

    def finish(self):
        for w in range(len(self.tensors)):
            copies = self._copies(w)
            for _, arrival in copies:
                arrival.wait_recv()
            for send, _ in copies:
                send.wait_send()
            self._local(w).wait()


def _cast_shards(shards):
    def body(*refs):
        n = len(refs) // 2
        for src, dst in zip(refs[:n], refs[n:]):
            dst[...] = src[...].astype(dst.dtype)

    vmem = pl.BlockSpec(memory_space=pltpu.VMEM)
    return pl.pallas_call(
        body, name="cast_shards", in_specs=[vmem] * len(shards), out_specs=[vmem] * len(shards),
        out_shape=[jax.ShapeDtypeStruct(s.shape, BF16) for s in shards],
        compiler_params=pltpu.CompilerParams(vmem_limit_bytes=VMEM_LIMIT),
    )(*shards)


def _gather_w_in(shard):
    rows, cols = shard.shape

    def body(w_ref, out_ref, send_sems, recv_sems, local_sems):
        plan = _GatherPlan([w_ref], [out_ref], send_sems, recv_sems, local_sems)
        plan.start()
        plan.forward()
        plan.finish()

    vmem = pl.BlockSpec(memory_space=pltpu.VMEM)
    return pl.pallas_call(
        body, name="gather_w_in", in_specs=[vmem], out_specs=vmem,
        out_shape=jax.ShapeDtypeStruct((N_DEV, rows, cols), shard.dtype),
        scratch_shapes=[pltpu.SemaphoreType.DMA((_GatherPlan.COPIES,)), pltpu.SemaphoreType.DMA((_GatherPlan.COPIES,)),
                        pltpu.SemaphoreType.DMA((1,))],
        compiler_params=pltpu.CompilerParams(vmem_limit_bytes=VMEM_LIMIT),
    )(shard)


def _dw_rows(pieces, b, name, tn=512, tt=512):
    t, n = b.shape
    widths = [p.shape[1] for p in pieces]
    rows = sum(widths)
    tt = min(tt, t)
    steps = t // tt
    n_p = len(pieces)

    def body(*refs):
        piece_refs, b_ref, o_ref, acc = refs[:n_p], refs[n_p], refs[n_p + 1], refs[n_p + 2]
        s = pl.program_id(1)

        @pl.when(s == 0)
        def _():
            acc[...] = jnp.zeros_like(acc)

        bv = b_ref[...]
        off = 0
        for p_ref, width in zip(piece_refs, widths):
            acc[off:off + width, :] += _dot_tn(p_ref[...], bv)
            off += width

        @pl.when(s == steps - 1)
        def _():
            o_ref[...] = acc[...].astype(o_ref.dtype)

    return pl.pallas_call(
        body, name=name, grid=(n // tn, steps),
        in_specs=[pl.BlockSpec((tt, width), lambda j, s: (s, 0)) for width in widths]
        + [pl.BlockSpec((tt, tn), lambda j, s: (s, j))],
        out_specs=pl.BlockSpec((rows, tn), lambda j, s: (0, j)),
        out_shape=jax.ShapeDtypeStruct((rows, n), BF16),
        scratch_shapes=[pltpu.VMEM((rows, tn), F32)],
        compiler_params=_params(2),
    )(*pieces, b)


def _dh_norm_bwd(pairs, x, g, res, name, tm=256, after=()):
    m, d = x.shape
    tm = min(tm, m)
    n_p = len(pairs)
    n_steps = m // tm

    def body(*refs):
        x_ref, g_ref, res_ref = refs[2 * n_p:2 * n_p + 3]
        dx_ref, dg_ref = refs[2 * n_p + 3 + len(after):]

        @pl.when(pl.program_id(0) == 0)
        def _():
            dg_ref[...] = jnp.zeros_like(dg_ref)

        dh = None
        for q in range(n_p):
            part = _dot(refs[2 * q][...], refs[2 * q + 1][...])
            dh = part if dh is None else dh + part
        xv = x_ref[...]
        r = lax.rsqrt(jnp.mean(xv * xv, axis=-1, keepdims=True) + EPS)
        xh = xv * r
        dxh = dh * g_ref[...]
        dx_ref[...] = res_ref[...] + r * (dxh - xh * jnp.mean(dxh * xh, axis=-1, keepdims=True))
        dg_ref[...] += jnp.sum(dh * xh, axis=0, keepdims=True)

    in_specs, args = [], []
    for a, w, r in pairs:
        k = a.shape[1]
        in_specs += [pl.BlockSpec((tm, k), lambda i: (i, 0)),
                     pl.BlockSpec((k, d), functools.partial(lambda i, r: (r, 0), r=r or 0))]
        args += [a, w]
    row = pl.BlockSpec((tm, d), lambda i: (i, 0))
    vec = pl.BlockSpec((1, d), lambda i: (0, 0))
    return pl.pallas_call(
        body, name=name, grid=(n_steps,),
        in_specs=in_specs + [row, vec, row] + [pl.BlockSpec(memory_space=pl.ANY)] * len(after),
        out_specs=[row, vec],
        out_shape=[jax.ShapeDtypeStruct((m, d), F32), jax.ShapeDtypeStruct((1, d), F32)],
        compiler_params=_params(1),
    )(*args, x, g, res, *after)


def _exchange_start(grad, name):
    def body(g_ref, land_ref, send_sem, recv_sem, local_sem, g_thru, land_thru, token):
        del g_thru, land_thru
        x, y, c = _mesh_place()
        me = 4 * x + 2 * y + c
        pltpu.make_async_copy(g_ref.at[me], land_ref.at[me], local_sem).start()
        for k in range(1, N_DEV):
            peer, pidx = _peer(x, y, c, k)
            _remote(g_ref.at[pidx], land_ref.at[me], send_sem, recv_sem, peer).start()
        token[...] = jnp.zeros_like(token)

    hbm = pl.BlockSpec(memory_space=pltpu.HBM)
    sem = pl.BlockSpec(memory_space=pltpu.SEMAPHORE)
    buf = pltpu.HBM(grad.shape, grad.dtype)
    return pl.pallas_call(
        body, name=name,
        out_shape=(pltpu.SemaphoreType.DMA(()), pltpu.SemaphoreType.DMA(()), pltpu.SemaphoreType.DMA(()), buf, buf,
                   jax.ShapeDtypeStruct((8, LANES), F32)),
        in_specs=(hbm, hbm), out_specs=(sem, sem, sem, hbm, hbm, pl.BlockSpec(memory_space=pltpu.VMEM)),
        input_output_aliases={0: 3, 1: 4},
        compiler_params=pltpu.CompilerParams(has_side_effects=pltpu.SideEffectType.DATAFLOW_SIDE_EFFECTING),
    )(pltpu.with_memory_space_constraint(grad, pltpu.HBM),
      pltpu.with_memory_space_constraint(lax.empty(grad.shape, grad.dtype), pltpu.HBM))


def _exchange_wait(send_sem, recv_sem, local_sem, grad, land, after, name):
    def body(g_ref, land_ref, send_sem, recv_sem, local_sem, *rest):
        x, y, c = _mesh_place()
        peer, _ = _peer(x, y, c, 1)
        others = pl.ds(0, N_DEV - 1)
        seven = _remote(g_ref.at[others], land_ref.at[others], send_sem, recv_sem, peer)
        seven.wait_send()
        seven.wait_recv()
        pltpu.make_async_copy(g_ref.at[0], land_ref.at[0], local_sem).wait()

    hbm = pl.BlockSpec(memory_space=pltpu.HBM)
    sem = pl.BlockSpec(memory_space=pltpu.SEMAPHORE)
    buf = pltpu.HBM(grad.shape, grad.dtype)
    return pl.pallas_call(
        body, name=name, out_shape=(buf, buf),
        in_specs=(hbm, hbm, sem, sem, sem) + (pl.BlockSpec(memory_space=pl.ANY),) * len(after), out_specs=(hbm, hbm),
        input_output_aliases={0: 0, 1: 1},
        compiler_params=pltpu.CompilerParams(has_side_effects=pltpu.SideEffectType.DATAFLOW_SIDE_EFFECTING),
    )(grad, land, send_sem, recv_sem, local_sem, *after)[1]


SMALL_LAYOUT = ((0, 0, 0, 0, D_MODEL), (1, 0, 1, 0, GROUP), (1, 1, 1, GROUP, GROUP), (2, 0, 2, 0, GROUP),
                (3, 0, 2, GROUP, GROUP), (4, 0, 3, 0, D_MODEL), (5, 0, 4, 0, D_MODEL))
LOSS_ROW = 5
N_SMALL = 6


def _small_step(grads, loss_part, ws, ms, vs):
    def body(*refs):
        g_in, loss_in = refs[:N_SMALL], refs[N_SMALL]
        params = [refs[1 + (q + 1) * N_SMALL:1 + (q + 2) * N_SMALL] for q in range(3)]
        o0 = 1 + 4 * N_SMALL
        outs = [refs[o0 + q * N_SMALL:o0 + (q + 1) * N_SMALL] for q in range(4)]
        loss_out = refs[o0 + 4 * N_SMALL]
        pack, land, wp, mp, vp, send_sems, recv_sems = refs[o0 + 4 * N_SMALL + 1:]

        def place(dst, srcs):
            dst[...] = jnp.zeros_like(dst)
            for p, sr, dr, dc, width in SMALL_LAYOUT:
                dst[dr:dr + 1, dc:dc + width] = srcs[p][sr:sr + 1, :]

        place(pack, g_in)
        pack[LOSS_ROW:LOSS_ROW + 1, 0:LANES] = loss_in[...]
        for dst, srcs in zip((wp, mp, vp), params):
            place(dst, srcs)

        x, y, c = _mesh_place()
        me = 4 * x + 2 * y + c
        land[me] = pack[...]
        sends = []
        for k in range(1, N_DEV):
            peer, _ = _peer(x, y, c, k)
            cp = _remote(pack, land.at[me], send_sems.at[k - 1], recv_sems.at[k - 1], peer)
            cp.start()
            sends.append(cp)
        for k in range(1, N_DEV):
            peer, pidx = _peer(x, y, c, k)
            _remote(pack, land.at[pidx], send_sems.at[k - 1], recv_sems.at[k - 1], peer).wait_recv()
        for cp in sends:
            cp.wait_send()

        g = land[0]
        for d in range(1, N_DEV):
            g = g + land[d]
        delta, nm, nv = _adam(wp[...], g, mp[...], vp[...])
        for val, out in zip((g, delta, nm, nv), outs):
            for p, sr, dr, dc, width in SMALL_LAYOUT:
                out[p][sr:sr + 1, :] = val[dr:dr + 1, dc:dc + width]
        loss_out[...] = g[LOSS_ROW:LOSS_ROW + 1, 0:LANES]

    vmem = pl.BlockSpec(memory_space=pltpu.VMEM)
    n_in = 1 + 4 * N_SMALL
    shapes = [jax.ShapeDtypeStruct(w.shape, F32) for w in ws]
    packed = pltpu.VMEM((SMALL_ROWS, D_MODEL), F32)
    outs = pl.pallas_call(
        body, name="small_step", in_specs=[vmem] * n_in, out_specs=[vmem] * (4 * N_SMALL + 1),
        out_shape=shapes * 4 + [jax.ShapeDtypeStruct((1, LANES), F32)],
        scratch_shapes=[packed, pltpu.VMEM((N_DEV, SMALL_ROWS, D_MODEL), F32), packed, packed, packed,
                        pltpu.SemaphoreType.DMA((N_DEV - 1,)), pltpu.SemaphoreType.DMA((N_DEV - 1,))],
    )(*grads, loss_part, *ws, *ms, *vs)
    return [outs[q * N_SMALL:(q + 1) * N_SMALL] for q in range(4)], outs[4 * N_SMALL]


def _adam(w, g, m, v):
    m = ADAM_B1 * m + (1.0 - ADAM_B1) * g
    v = ADAM_B2 * v + (1.0 - ADAM_B2) * (g * g)
    m_hat = m / (1.0 - ADAM_B1 ** ADAM_STEP)
    v_hat = v / (1.0 - ADAM_B2 ** ADAM_STEP)
    delta = -ADAM_LR * (m_hat / (jnp.sqrt(v_hat) + ADAM_EPS) + ADAM_WD * w)
    return delta, m, v


def _reduce_adamw(land, w, m, v, name, tr, after=()):
    _, rows, width = land.shape

    def body(land_ref, w_ref, m_ref, v_ref, *rest):
        g_ref, d_ref, nm_ref, nv_ref = rest[len(after):]
        g = land_ref[0].astype(F32)
        for d in range(1, N_DEV):
            g = g + land_ref[d].astype(F32)
        delta, nm, nv = _adam(w_ref[...], g, m_ref[...], v_ref[...])
        g_ref[...] = g
        d_ref[...] = delta
        nm_ref[...] = nm
        nv_ref[...] = nv

    row = pl.BlockSpec((tr, width), lambda i: (i, 0))
    out = jax.ShapeDtypeStruct((rows, width), F32)
    return pl.pallas_call(
        body, name=name, grid=(rows // tr,),
        in_specs=[pl.BlockSpec((N_DEV, tr, width), lambda i: (0, i, 0)), row, row, row]
        + [pl.BlockSpec(memory_space=pl.ANY)] * len(after),
        out_specs=[row, row, row, row], out_shape=[out, out, out, out],
        compiler_params=_params(1),
    )(land, w, m, v, *after)


def kernel(x, mix_norm_g, w_in, lower_bounds, hgrn_norm_g, sb_norm_g, w_out, ffn_norm_g, w_gate, w_up, w_down, final_norm_g, loss_target, m_mix_norm_g, m_w_in, m_lower_bounds, m_hgrn_norm_g, m_sb_norm_g, m_w_out, m_ffn_norm_g, m_w_gate, m_w_up, m_w_down, m_final_norm_g, v_mix_norm_g, v_w_in, v_lower_bounds, v_hgrn_norm_g, v_sb_norm_g, v_w_out, v_ffn_norm_g, v_w_gate, v_w_up, v_w_down, v_final_norm_g):
    n_seq, seq, d = x.shape
    t = n_seq * seq
    x2d = x.reshape(t, d)
    tgt = loss_target.reshape(t, d)
    final_g = final_norm_g.reshape(1, d)
    col_sharded = (True, False, True, True, False)

    def as_rows(ws):
        return [w[0].T if tr else w[0] for w, tr in zip(ws, col_sharded)]

    big_w = as_rows([w_in, w_out, w_gate, w_up, w_down])
    big_m = as_rows([m_w_in, m_w_out, m_w_gate, m_w_up, m_w_down])
    big_v = as_rows([v_w_in, v_w_out, v_w_gate, v_w_up, v_w_down])

    sh_in, sh_out, sh_gate, sh_up, sh_down = _cast_shards(big_w)
    wt_in = _gather_w_in(sh_in).reshape(IN_COLS, d)
    h1 = _rmsnorm_fwd(x2d, mix_norm_g, "norm_mix")
    proj_h = _mm_nt(h1, wt_in, "proj_hgrn", rows=(0, 4 * GROUP), tm=1024, tk=1024)
    proj_s = _mm_nt(h1, wt_in, "proj_sb", rows=(4 * GROUP, 3 * GROUP), out_dtype=MXU_DTYPE, tm=1024, tk=512)
    mixed, oa_pre, states = _hgrn_fwd(proj_h, lower_bounds, hgrn_norm_g, n_seq, seq)
    mixed, ob_pre, ctot, gathered = _sb_fwd(proj_s, sb_norm_g, mixed, [sh_out, sh_gate, sh_up, sh_down], n_seq, seq)
    wf_out = gathered[0].reshape(d, d)
    wt_gate = gathered[1].reshape(D_FF, d)
    wt_up = gathered[2].reshape(D_FF, d)
    wf_down = gathered[3].reshape(D_FF, d)
    x1, h2 = _mix_out_norm(mixed, wf_out, x2d, ffn_norm_g, "mix_out")
    gate, up, ff = _ffn_up(h2, wt_gate, wt_up, "ffn_up")
    dx2, dx2m, d_final_g, loss_part = _ffn_down_loss(ff, wf_down, x1, tgt, final_g, "ffn_down_loss")

    dgate, dup = _ffn_bwd_act(dx2m, wf_down, gate, up, "ffn_bwd_act")
    dw_down = _mm_tn(ff, dx2m, "dw_down", tk=1408, tn=1024).reshape(N_DEV, D_FF // N_DEV, d)
    dw_gate = _mm_tn(dgate, h2, "dw_gate", tk=1408, tn=1024).reshape(N_DEV, D_FF // N_DEV, d)
    dw_up = _mm_tn(dup, h2, "dw_up", tk=1408, tn=1024).reshape(N_DEV, D_FF // N_DEV, d)
    dx1, d_ffn_g = _dh_norm_bwd([(dgate, wt_gate, None), (dup, wt_up, None)], x1, ffn_norm_g, dx2, "dh_ffn")
    dmix = _mm_nt(dx1, wf_out, "dmix")
    dw_out = _mm_tn(mixed, dx1, "dw_out", tk=512, tn=1024).reshape(N_DEV, d // N_DEV, d)
    dsq, dsk, dsv, d_sb_g, lands = _sb_bwd(proj_s, sb_norm_g, dmix, ob_pre, ctot, [dw_out, dw_gate, dw_up, dw_down],
                                            n_seq, seq)
    dhq, dhf, dhi, dhg, d_lb, d_hgrn_g = _hgrn_bwd(proj_h, lower_bounds, hgrn_norm_g, dmix, oa_pre, states, n_seq,
                                                   seq)
    dproj = [dhq, dhf, dhi, dhg, dsq, dsk, dsv]
    dw_in = _dw_rows(dproj, h1, "dw_in").reshape(N_DEV, IN_COLS // N_DEV, d)
    send_sem, recv_sem, local_sem, dw_in, land_in, token = _exchange_start(dw_in, "dw_in_send")
    dx, d_mix_g = _dh_norm_bwd([(piece, wt_in, k) for k, piece in enumerate(dproj)], x2d, mix_norm_g, dx1, "dh_mix",
                               after=(token,))

    tiles = {"in": 224, "out": 128, "gate": 176, "up": 176, "down": 176}
    keys = list(tiles)
    rest = [_reduce_adamw(land, w, m, v, "adamw_" + key, tr=tiles[key], after=(token,))
            for key, land, w, m, v in zip(keys[1:], lands, big_w[1:], big_m[1:], big_v[1:])]
    land_in = _exchange_wait(send_sem, recv_sem, local_sem, dw_in, land_in, [dx] + [res[0] for res in rest],
                             "dw_in_await")
    big = [_reduce_adamw(land_in, big_w[0], big_m[0], big_v[0], "adamw_in", tr=tiles["in"])] + rest
    big = [[r.T if tr else r for r in res] for res, tr in zip(big, col_sharded)]
    small, loss_row = _small_step(
        [d_mix_g, d_lb, d_hgrn_g, d_sb_g, d_ffn_g, d_final_g], loss_part,
        [mix_norm_g, lower_bounds, hgrn_norm_g, sb_norm_g, ffn_norm_g, final_g],
        [m_mix_norm_g, m_lower_bounds, m_hgrn_norm_g, m_sb_norm_g, m_ffn_norm_g, m_final_norm_g.reshape(1, d)],
        [v_mix_norm_g, v_lower_bounds, v_hgrn_norm_g, v_sb_norm_g, v_ffn_norm_g, v_final_norm_g.reshape(1, d)])

    outs = [loss_row[0, 0], dx.reshape(n_seq, seq, d)]
    for q in range(4):
        b_in, b_out, b_gate, b_up, b_down = [res[q][None] for res in big]
        s_mix, s_lb, s_hgrn, s_sb, s_ffn, s_final = small[q]
        outs += [s_mix, b_in, s_lb, s_hgrn, s_sb, b_out, s_ffn, b_gate, b_up, b_down, s_final.reshape(d)]
    return tuple(outs)
```

```python
import functools
import math

import jax
import jax.numpy as jnp
from jax import lax
from jax.experimental import pallas as pl
from jax.experimental.pallas import tpu as pltpu

F32 = jnp.float32
BF16 = jnp.bfloat16
MXU_DTYPE = BF16

EPS = 1e-6
D_MODEL = 1024
N_HEADS = 8
D_HEAD = 64
GROUP = N_HEADS * D_HEAD
IN_COLS = 7 * GROUP
D_FF = 2816
CHUNK = 64
LANES = 128
N_PAIRS = GROUP // LANES
SUPER = 256
SB_BLOCK = 256
N_DEV = 8

ADAM_LR = 0.001
ADAM_B1 = 0.9
ADAM_B2 = 0.999
ADAM_EPS = 1e-08
ADAM_WD = 0.01
ADAM_STEP = 10

SMALL_ROWS = 8
FF_TILE = D_FF // 2

VMEM_LIMIT = 48 * 1024 * 1024


def _params(n_axes, vmem=VMEM_LIMIT):
    return pltpu.CompilerParams(dimension_semantics=("arbitrary",) * n_axes, vmem_limit_bytes=vmem)


def _dot(a, b):
    return jnp.dot(a.astype(MXU_DTYPE), b.astype(MXU_DTYPE), preferred_element_type=F32)


def _dot_nt(a, b):
    return lax.dot_general(a.astype(MXU_DTYPE), b.astype(MXU_DTYPE), (((1,), (1,)), ((), ())),
                           preferred_element_type=F32)


def _dot_tn(a, b):
    return lax.dot_general(a.astype(MXU_DTYPE), b.astype(MXU_DTYPE), (((0,), (0,)), ((), ())),
                           preferred_element_type=F32)


def _split(x, parts):
    out, r = [], x
    for _ in range(parts):
        h = r.astype(BF16)
        out.append(h)
        r = r - h.astype(F32)
    return out


def _rsum_left(u, x, parts):
    acc = None
    for h in _split(x, parts):
        d = jnp.dot(u, h, preferred_element_type=F32)
        acc = d if acc is None else acc + d
    return acc


def _ones_where(mask):
    return jnp.where(mask, 1.0, 0.0).astype(BF16)


def _sigmoid(x):
    return 1.0 / (1.0 + jnp.exp(-x))


def _softplus(x):
    return jnp.maximum(x, 0.0) + jnp.log(1.0 + jnp.exp(-jnp.abs(x)))


def _head_masks():
    lane = lax.broadcasted_iota(jnp.int32, (1, LANES), 1)
    return [jnp.where(lane < D_HEAD, 1.0, 0.0), jnp.where(lane >= D_HEAD, 1.0, 0.0)]


def _head_rstd(o, masks):
    sq = o * o
    r = None
    for m in masks:
        ms = jnp.sum(sq * m, axis=1, keepdims=True) * (1.0 / D_HEAD)
        t = lax.rsqrt(ms + EPS) * m
        r = t if r is None else r + t
    return r


def _head_mean(t, masks):
    out = None
    for m in masks:
        v = jnp.sum(t * m, axis=1, keepdims=True) * (1.0 / D_HEAD) * m
        out = v if out is None else out + v
    return out


def _rmsnorm_fwd(x, g, name):
    t, d = x.shape
    tm = min(512, t)

    def body(x_ref, g_ref, o_ref):
        xv = x_ref[...]
        r = lax.rsqrt(jnp.mean(xv * xv, axis=-1, keepdims=True) + EPS)
        o_ref[...] = (xv * r * g_ref[...]).astype(o_ref.dtype)

    return pl.pallas_call(
        body, name=name, grid=(t // tm,),
        in_specs=[pl.BlockSpec((tm, d), lambda i: (i, 0)), pl.BlockSpec((1, d), lambda i: (0, 0))],
        out_specs=pl.BlockSpec((tm, d), lambda i: (i, 0)),
        out_shape=jax.ShapeDtypeStruct((t, d), MXU_DTYPE),
        compiler_params=_params(1),
    )(x, g)


def _mix_out_norm(a, w, res, g, name, tm=512):
    m, k = a.shape
    d = w.shape[1]
    tm = min(tm, m)

    def body(a_ref, w_ref, res_ref, g_ref, x_ref, h_ref):
        xv = res_ref[...] + _dot(a_ref[...], w_ref[...])
        x_ref[...] = xv
        r = lax.rsqrt(jnp.mean(xv * xv, axis=-1, keepdims=True) + EPS)
        h_ref[...] = (xv * r * g_ref[...]).astype(h_ref.dtype)

    row = pl.BlockSpec((tm, d), lambda i: (i, 0))
    return pl.pallas_call(
        body, name=name, grid=(m // tm,),
        in_specs=[pl.BlockSpec((tm, k), lambda i: (i, 0)), pl.BlockSpec((k, d), lambda i: (0, 0)), row,
                  pl.BlockSpec((1, d), lambda i: (0, 0))],
        out_specs=[row, row],
        out_shape=[jax.ShapeDtypeStruct((m, d), F32), jax.ShapeDtypeStruct((m, d), MXU_DTYPE)],
        compiler_params=_params(1),
    )(a, w, res, g)


def _ffn_down_loss(a, w, res, target, g, name, tm=512):
    m, k = a.shape
    d = w.shape[1]
    tm = min(tm, m)

    def body(a_ref, w_ref, res_ref, t_ref, g_ref, dx_ref, dxm_ref, dg_ref, loss_ref):
        xv = res_ref[...] + _dot(a_ref[...], w_ref[...])
        gv = g_ref[...]
        r = lax.rsqrt(jnp.mean(xv * xv, axis=-1, keepdims=True) + EPS)
        xh = xv * r
        e = xh * gv - t_ref[...]
        dy = e * (1.0 / d)
        dxh = dy * gv
        dxv = r * (dxh - xh * jnp.mean(dxh * xh, axis=-1, keepdims=True))
        dx_ref[...] = dxv
        dxm_ref[...] = dxv.astype(dxm_ref.dtype)

        @pl.when(pl.program_id(0) == 0)
        def _():
            dg_ref[...] = jnp.zeros_like(dg_ref)
            loss_ref[...] = jnp.zeros_like(loss_ref)

        dg_ref[...] += jnp.sum(dy * xh, axis=0, keepdims=True)
        part = 0.5 * jnp.sum(jnp.mean(e * e, axis=-1, keepdims=True), axis=0, keepdims=True)
        loss_ref[...] += jnp.broadcast_to(part, loss_ref.shape)

    row = pl.BlockSpec((tm, d), lambda i: (i, 0))
    vec = pl.BlockSpec((1, d), lambda i: (0, 0))
    return pl.pallas_call(
        body, name=name, grid=(m // tm,),
        in_specs=[pl.BlockSpec((tm, k), lambda i: (i, 0)), pl.BlockSpec((k, d), lambda i: (0, 0)), row, row, vec],
        out_specs=[row, row, vec, pl.BlockSpec((1, LANES), lambda i: (0, 0))],
        out_shape=[jax.ShapeDtypeStruct((m, d), F32), jax.ShapeDtypeStruct((m, d), MXU_DTYPE),
                   jax.ShapeDtypeStruct((1, d), F32), jax.ShapeDtypeStruct((1, LANES), F32)],
        compiler_params=_params(1),
    )(a, w, res, target, g)


def _mm_nt(a, w, name, rows=None, out_dtype=F32, tm=512, tk=512):
    m, n = a.shape
    row0, k = rows or (0, w.shape[0])
    tm, tk = min(tm, m), min(tk, k)
    first = row0 // tk

    def body(a_ref, w_ref, o_ref):
        o_ref[...] = _dot_nt(a_ref[...], w_ref[...]).astype(o_ref.dtype)

    return pl.pallas_call(
        body, name=name, grid=(k // tk, m // tm),
        in_specs=[pl.BlockSpec((tm, n), lambda j, i: (i, 0)), pl.BlockSpec((tk, n), lambda j, i: (first + j, 0))],
        out_specs=pl.BlockSpec((tm, tk), lambda j, i: (i, j)),
        out_shape=jax.ShapeDtypeStruct((m, k), out_dtype),
        compiler_params=_params(2),
    )(a, w)


def _mm_tn(a, b, name, tk, tn, tt=512, out_dtype=BF16):
    t, k = a.shape
    n = b.shape[1]
    tt = min(tt, t)
    steps = t // tt

    def body(a_ref, b_ref, o_ref, acc):
        s = pl.program_id(2)

        @pl.when(s == 0)
        def _():
            acc[...] = jnp.zeros_like(acc)

        acc[...] += _dot_tn(a_ref[...], b_ref[...])

        @pl.when(s == steps - 1)
        def _():
            o_ref[...] = acc[...].astype(o_ref.dtype)

    return pl.pallas_call(
        body, name=name, grid=(k // tk, n // tn, steps),
        in_specs=[pl.BlockSpec((tt, tk), lambda i, j, s: (s, i)), pl.BlockSpec((tt, tn), lambda i, j, s: (s, j))],
        out_specs=pl.BlockSpec((tk, tn), lambda i, j, s: (i, j)),
        out_shape=jax.ShapeDtypeStruct((k, n), out_dtype),
        scratch_shapes=[pltpu.VMEM((tk, tn), F32)],
        compiler_params=_params(3),
    )(a, b)


def _ffn_up(h, wg_t, wu_t, name, tm=512, tn=FF_TILE):
    m, k = h.shape
    n = wg_t.shape[0]
    tm = min(tm, m)

    def body(h_ref, wg_ref, wu_ref, gate_ref, up_ref, ff_ref):
        hv = h_ref[...]
        gate = _dot_nt(hv, wg_ref[...])
        up = _dot_nt(hv, wu_ref[...])
        gate_ref[...] = gate.astype(gate_ref.dtype)
        up_ref[...] = up.astype(up_ref.dtype)
        ff_ref[...] = (gate * _sigmoid(gate) * up).astype(ff_ref.dtype)

    wspec = pl.BlockSpec((tn, k), lambda j, i: (j, 0))
    ospec = pl.BlockSpec((tm, tn), lambda j, i: (i, j))
    return pl.pallas_call(
        body, name=name, grid=(n // tn, m // tm),
        in_specs=[pl.BlockSpec((tm, k), lambda j, i: (i, 0)), wspec, wspec],
        out_specs=[ospec, ospec, ospec],
        out_shape=[jax.ShapeDtypeStruct((m, n), MXU_DTYPE)] * 3,
        compiler_params=_params(2),
    )(h, wg_t, wu_t)


def _ffn_bwd_act(dx, wd, gate, up, name, tm=512, tn=FF_TILE):
    m, k = dx.shape
    n = wd.shape[0]
    tm = min(tm, m)

    def body(dx_ref, wd_ref, gate_ref, up_ref, dgate_ref, dup_ref):
        dff = _dot_nt(dx_ref[...], wd_ref[...])
        gate = gate_ref[...].astype(F32)
        sg = _sigmoid(gate)
        dgate_ref[...] = (dff * up_ref[...].astype(F32) * sg * (1.0 + gate * (1.0 - sg))).astype(dgate_ref.dtype)
        dup_ref[...] = (dff * gate * sg).astype(dup_ref.dtype)

    ospec = pl.BlockSpec((tm, tn), lambda j, i: (i, j))
    return pl.pallas_call(
        body, name=name, grid=(n // tn, m // tm),
        in_specs=[pl.BlockSpec((tm, k), lambda j, i: (i, 0)), pl.BlockSpec((tn, k), lambda j, i: (j, 0)),
                  ospec, ospec],
        out_specs=[ospec, ospec],
        out_shape=[jax.ShapeDtypeStruct((m, n), MXU_DTYPE), jax.ShapeDtypeStruct((m, n), MXU_DTYPE)],
        compiler_params=_params(2),
    )(dx, wd, gate, up)


def _chunk_masks():
    r = lax.broadcasted_iota(jnp.int32, (SUPER, SUPER), 0)
    c = lax.broadcasted_iota(jnp.int32, (SUPER, SUPER), 1)
    same = jnp.right_shift(r, 6) == jnp.right_shift(c, 6)
    lower = jnp.logical_and(same, c <= r)
    upper = jnp.logical_and(same, c >= r)
    return same, lower, upper


def _head_block_mask():
    r = lax.broadcasted_iota(jnp.int32, (LANES, LANES), 0)
    c = lax.broadcasted_iota(jnp.int32, (LANES, LANES), 1)
    return jnp.where(jnp.right_shift(r, 6) == jnp.right_shift(c, 6), 1.0, 0.0)


def _lower_bound(lb_raw):
    return 1.0 / (1.0 + jnp.exp(lb_raw[1:2, :] - lb_raw[0:1, :]))


PER_SUPER = SUPER // CHUNK
CHUNK_ROWS = [slice(c * CHUNK, (c + 1) * CHUNK) for c in range(PER_SUPER)]


def _over_chunks(rows):
    return jnp.concatenate([jnp.broadcast_to(r, (CHUNK, LANES)) for r in rows], axis=0)


def _hgrn_gates(q, hf, lb, lower_b):
    sig = _sigmoid(hf)
    f = lb + (1.0 - lb) * sig
    k = 1.0 - f
    lf = jnp.log(f)
    b = _rsum_left(lower_b, lf, 2)
    ends = [b[cr.stop - 1:cr.stop, :] for cr in CHUNK_ROWS]
    eb = jnp.exp(b)
    enb = jnp.exp(-b)
    edb = jnp.exp(_over_chunks(ends) - b)
    decs = [jnp.exp(e) for e in ends]
    return sig, f, k, decs, eb, enb, edb, q * eb, k * enb, k * edb


def _hgrn_fwd(proj, lower_bounds, norm_g, n_seq, seq):
    t = n_seq * seq
    n_super = seq // SUPER
    n_chunks = seq // CHUNK

    def body(q_ref, f_ref, i_ref, g_ref, lb_ref, ng_ref, out_ref, opre_ref, st_ref):
        masks = _head_masks()
        _, lower, _ = _chunk_masks()
        lower_b = _ones_where(lower)
        bd = _head_block_mask()
        lb = _lower_bound(lb_ref[...])
        ng = ng_ref[...]

        def step(sb, st):
            rows = pl.ds(pl.multiple_of(sb * SUPER, SUPER), SUPER)
            q, hf, v, hg = q_ref[rows, :], f_ref[rows, :], i_ref[rows, :], g_ref[rows, :]
            _, _, _, decs, _, _, _, qe, ke, kd = _hgrn_gates(q, hf, lb, lower_b)
            scores = [_dot_nt(qe * m, ke) for m in masks]
            updates = [_dot_tn(v[cr], kd[cr]) for cr in CHUNK_ROWS]
            states = [st]
            for dec, upd in zip(decs, updates):
                states.append(states[-1] * dec + bd * upd)
            for c in range(PER_SUPER):
                st_ref[0, 0, sb * PER_SUPER + c] = states[c]
            intra = [_dot(jnp.where(lower, p, 0.0), v) for p in scores]
            inter = [_dot_nt(qe[cr], s) for cr, s in zip(CHUNK_ROWS, states)]
            o = intra[0] * masks[0] + intra[1] * masks[1] + jnp.concatenate(inter, axis=0)
            opre_ref[rows, :] = o
            on = o * _head_rstd(o, masks) * ng
            out_ref[rows, :] = (on * hg * _sigmoid(hg)).astype(out_ref.dtype)
            return states[-1]

        lax.fori_loop(0, n_super, step, jnp.zeros((LANES, LANES), F32))

    def col(k):
        return pl.BlockSpec((seq, LANES), lambda p, b: (b, k * N_PAIRS + p))

    vec = lambda rows: pl.BlockSpec((rows, LANES), lambda p, b: (0, p))
    ospec = pl.BlockSpec((seq, LANES), lambda p, b: (b, p))
    return pl.pallas_call(
        body, name="hgrn_fwd", grid=(N_PAIRS, n_seq),
        in_specs=[col(0), col(1), col(2), col(3), vec(2), vec(1)],
        out_specs=[ospec, ospec,
                   pl.BlockSpec((1, 1, n_chunks, LANES, LANES), lambda p, b: (b, p, 0, 0, 0))],
        out_shape=[jax.ShapeDtypeStruct((t, 2 * GROUP), MXU_DTYPE), jax.ShapeDtypeStruct((t, GROUP), F32),
                   jax.ShapeDtypeStruct((n_seq, N_PAIRS, n_chunks, LANES, LANES), F32)],
        compiler_params=_params(2),
    )(proj, proj, proj, proj, lower_bounds, norm_g)


def _hgrn_bwd(proj, lower_bounds, norm_g, dmix, opre, states, n_seq, seq):
    t = n_seq * seq
    n_super = seq // SUPER
    n_chunks = seq // CHUNK
    per = SUPER // CHUNK

    def body(q_ref, f_ref, i_ref, g_ref, lb_ref, ng_ref, dm_ref, opre_ref, st_ref,
             dq_ref, df_ref, di_ref, dg_ref, dlb_ref, dng_ref):
        masks = _head_masks()
        _, lower, upper = _chunk_masks()
        lower_b, upper_b = _ones_where(lower), _ones_where(upper)
        bd = _head_block_mask()
        lb_raw = lb_ref[...]
        lb = _lower_bound(lb_raw)
        ng = ng_ref[...]

        @pl.when(pl.program_id(1) == 0)
        def _():
            dlb_ref[...] = jnp.zeros_like(dlb_ref)
            dng_ref[...] = jnp.zeros_like(dng_ref)

        def step(it, dst):
            sb = n_super - 1 - it
            rows = pl.ds(pl.multiple_of(sb * SUPER, SUPER), SUPER)
            q, hf, v, hg = q_ref[rows, :], f_ref[rows, :], i_ref[rows, :], g_ref[rows, :]
            sig, f, k, decs, eb, enb, edb, qe, ke, kd = _hgrn_gates(q, hf, lb, lower_b)
            o = opre_ref[rows, :]
            r = _head_rstd(o, masks)
            oh = o * r
            dm = dm_ref[rows, :]
            sg = _sigmoid(hg)
            dg_ref[rows, :] = (dm * oh * ng * sg * (1.0 + hg * (1.0 - sg))).astype(dg_ref.dtype)
            don = dm * hg * sg
            dng_ref[...] += jnp.sum(don * oh, axis=0, keepdims=True)
            doh = don * ng
            do = r * (doh - oh * _head_mean(doh * oh, masks))
            doms = [do * m for m in masks]
            qems = [qe * m for m in masks]
            scores = [_dot_nt(qem, ke) for qem in qems]
            dscores = [_dot_nt(dom, v) for dom in doms]
            prevs = [st_ref[0, 0, sb * per + c] for c in range(per)]
            dst_in = [_dot_tn(do[cr], qe[cr]) for cr in CHUNK_ROWS]
            dqe_i = [_dot(do[cr], prev) for cr, prev in zip(CHUNK_ROWS, prevs)]
            dsts = [None] * per
            for c in reversed(range(per)):
                dsts[c] = dst
                dst = bd * (dst * decs[c] + dst_in[c])
            ps = [jnp.where(lower, p, 0.0) for p in scores]
            dps = [jnp.where(lower, dp, 0.0) for dp in dscores]
            dqe_h = [_dot(dp, ke) for dp in dps]
            dke_h = [_dot_tn(dp, qem) for dp, qem in zip(dps, qems)]
            dv_h = [_dot_tn(p, dom) for p, dom in zip(ps, doms)]
            dus = [bd * d for d in dsts]
            dv_i = [_dot_nt(kd[cr], du) for cr, du in zip(CHUNK_ROWS, dus)]
            dkd_i = [_dot(v[cr], du) for cr, du in zip(CHUNK_ROWS, dus)]
            dqe = dqe_h[0] * masks[0] + dqe_h[1] * masks[1] + jnp.concatenate(dqe_i, axis=0)
            dke = dke_h[0] + dke_h[1]
            dv = dv_h[0] + dv_h[1] + jnp.concatenate(dv_i, axis=0)
            dkd = jnp.concatenate(dkd_i, axis=0)
            dk = dke * enb + dkd * edb
            db = dqe * qe - dke * ke - dkd * kd
            dkd_kd = dkd * kd
            dends = [jnp.sum(dkd_kd[cr], axis=0, keepdims=True)
                     + jnp.sum(dsts[c] * prevs[c], axis=0, keepdims=True) * decs[c]
                     for c, cr in enumerate(CHUNK_ROWS)]
            dlf = _rsum_left(upper_b, db, 2) + _over_chunks(dends)
            dfv = dlf / f - dk
            dq_ref[rows, :] = (dqe * eb).astype(dq_ref.dtype)
            di_ref[rows, :] = dv.astype(di_ref.dtype)
            df_ref[rows, :] = (dfv * (1.0 - lb) * sig * (1.0 - sig)).astype(df_ref.dtype)
            dlb = jnp.sum(dfv * (1.0 - sig), axis=0, keepdims=True)
            da0 = dlb * lb * (1.0 - lb)
            dlb_ref[0:1, :] += da0
            dlb_ref[1:2, :] -= da0
            return dst

        lax.fori_loop(0, n_super, step, jnp.zeros((LANES, LANES), F32))

    def col(k):
        return pl.BlockSpec((seq, LANES), lambda p, b: (b, k * N_PAIRS + p))

    vec = lambda rows: pl.BlockSpec((rows, LANES), lambda p, b: (0, p))
    ospec = pl.BlockSpec((seq, LANES), lambda p, b: (b, p))
    piece = jax.ShapeDtypeStruct((t, GROUP), MXU_DTYPE)
    return pl.pallas_call(
        body, name="hgrn_bwd", grid=(N_PAIRS, n_seq),
        in_specs=[col(0), col(1), col(2), col(3), vec(2), vec(1), ospec, ospec,
                  pl.BlockSpec((1, 1, n_chunks, LANES, LANES), lambda p, b: (b, p, 0, 0, 0))],
        out_specs=[ospec, ospec, ospec, ospec, vec(2), vec(1)],
        out_shape=[piece, piece, piece, piece,
                   jax.ShapeDtypeStruct((2, GROUP), F32), jax.ShapeDtypeStruct((1, GROUP), F32)],
        compiler_params=_params(2),
    )(proj, proj, proj, proj, lower_bounds, norm_g, dmix, opre, states)


SB_SCALE = 1.0 / math.sqrt(D_HEAD)


SB_STEP = 2 * SB_BLOCK
QUERY_BLOCKS = (slice(0, SB_BLOCK), slice(SB_BLOCK, SB_STEP))


def _triangle(keep):
    row = lax.broadcasted_iota(jnp.int32, (SB_BLOCK, SB_BLOCK), 0)
    col = lax.broadcasted_iota(jnp.int32, (SB_BLOCK, SB_BLOCK), 1)
    return _ones_where(keep(row, col))


def _tile_masks():
    r = lax.broadcasted_iota(jnp.int32, (SB_BLOCK, SB_BLOCK), 0)
    c = lax.broadcasted_iota(jnp.int32, (SB_BLOCK, SB_BLOCK), 1)
    return r, c


def _sb_fwd(proj, norm_g, mixed, shards, n_seq, seq):
    t = n_seq * seq
    nq = seq // SB_STEP
    q0, k0, v0 = 0, N_PAIRS, 2 * N_PAIRS
    n_w = len(shards)
    n_steps = N_PAIRS * n_seq * nq
    tri = _triangle(lambda row, col: row >= col)

    def body(q_ref, k_ref, v_ref, ng_ref, tri_ref, mixed_in, *rest):
        del mixed_in
        shard_refs = rest[:n_w]
        out_ref, opre_ref, ctot_ref = rest[n_w:n_w + 3]
        gathered = rest[n_w + 3:2 * n_w + 3]
        send_sems, recv_sems, local_sems = rest[2 * n_w + 3:]
        i = pl.program_id(2)
        step = (pl.program_id(0) * n_seq + pl.program_id(1)) * nq + i
        plan = _GatherPlan(shard_refs, gathered, send_sems, recv_sems, local_sems)

        @pl.when(step == 0)
        def _():
            plan.start()

        @pl.when(step == n_steps // 2)
        def _():
            plan.forward()

        masks = _head_masks()
        r, c = _tile_masks()
        strict = c < r
        suffix = tri_ref[...]
        qhs = [[(q_ref[blk, :] * SB_SCALE * m).astype(MXU_DTYPE) for m in masks] for blk in QUERY_BLOCKS]

        def tiles(js, work, carry):
            rows = [pl.ds(pl.multiple_of(j * SB_BLOCK, SB_BLOCK), SB_BLOCK) for j in js]
            ks = [k_ref[rw, :].astype(MXU_DTYPE) for rw in rows]
            vs = [v_ref[rw, :].astype(MXU_DTYPE) for rw in rows]
            zs = [[_dot_nt(qh, ks[ts]) for qh in qhs[qb]] for qb, ts, _ in work]
            ccs = [[_softplus(z) for z in zw] for zw in zs]
            ccs = [[jnp.where(strict, cc, 0.0) for cc in cw] if diag else cw for cw, (_, _, diag) in zip(ccs, work)]
            sums = [[jnp.dot(cc.astype(BF16), suffix, preferred_element_type=F32) for cc in cw] for cw in ccs]
            out = [list(per_block) for per_block in carry]
            for h in range(len(masks)):
                for w, (qb, ts, diag) in enumerate(work):
                    run, acc = out[qb][h]
                    a = jnp.exp(zs[w][h] - (sums[w][h] + run))
                    if diag:
                        a = jnp.where(strict, a, 0.0)
                    out[qb][h] = (run + sums[w][h][:, 0:1], acc + _dot(a, vs[ts]))
            return tuple(tuple(per_block) for per_block in out)

        start = ((jnp.zeros((SB_BLOCK, 1), F32), jnp.zeros((SB_BLOCK, LANES), F32)),) * 2
        carry = tiles([2 * i, 2 * i + 1], [(0, 0, True), (1, 1, True), (1, 0, False)], (start, start))
        both = [(0, 0, False), (0, 1, False), (1, 0, False), (1, 1, False)]
        carry = lax.fori_loop(0, i, lambda s, cy: tiles([2 * (i - s) - 1, 2 * (i - s) - 2], both, cy), carry)
        opre = jnp.concatenate([cb[0][1] * masks[0] + cb[1][1] * masks[1] for cb in carry], axis=0)
        ctot = jnp.concatenate([cb[0][0] * masks[0] + cb[1][0] * masks[1] for cb in carry], axis=0)
        opre_ref[...] = opre
        ctot_ref[...] = ctot
        out_ref[...] = (opre * _head_rstd(opre, masks) * ng_ref[...]).astype(out_ref.dtype)

        @pl.when(step == n_steps - 1)
        def _():
            plan.finish()

    qspec = pl.BlockSpec((SB_STEP, LANES), lambda p, b, i: (b * nq + i, q0 + p))
    ospec = pl.BlockSpec((SB_STEP, LANES), lambda p, b, i: (b * nq + i, p))
    hbm = pl.BlockSpec(memory_space=pltpu.HBM)
    outs = pl.pallas_call(
        body, name="sb_fwd", grid=(N_PAIRS, n_seq, nq),
        in_specs=[qspec,
                  pl.BlockSpec((seq, LANES), lambda p, b, i: (b, k0 + p)),
                  pl.BlockSpec((seq, LANES), lambda p, b, i: (b, v0 + p)),
                  pl.BlockSpec((1, LANES), lambda p, b, i: (0, p)),
                  pl.BlockSpec(tri.shape, lambda p, b, i: (0, 0)), hbm] + [hbm] * n_w,
        out_specs=[pl.BlockSpec((SB_STEP, LANES), lambda p, b, i: (b * nq + i, N_PAIRS + p)), ospec, ospec]
        + [hbm] * n_w,
        out_shape=[jax.ShapeDtypeStruct(mixed.shape, mixed.dtype), jax.ShapeDtypeStruct((t, GROUP), F32),
                   jax.ShapeDtypeStruct((t, GROUP), F32)]
        + [jax.ShapeDtypeStruct((N_DEV,) + s.shape, s.dtype) for s in shards],
        scratch_shapes=[pltpu.SemaphoreType.DMA((n_w * _GatherPlan.COPIES,)),
                        pltpu.SemaphoreType.DMA((n_w * _GatherPlan.COPIES,)), pltpu.SemaphoreType.DMA((n_w,))],
        input_output_aliases={5: 0},
        compiler_params=_params(3),
    )(proj, proj, proj, norm_g, tri, mixed, *shards)
    return outs[0], outs[1], outs[2], list(outs[3:])


def _sb_bwd(proj, norm_g, dmix, opre, ctot, grads, n_seq, seq):
    t = n_seq * seq
    nq = seq // SB_STEP
    q0, k0, v0 = 0, N_PAIRS, 2 * N_PAIRS
    n_w = len(grads)
    n_steps = N_PAIRS * n_seq * nq
    tri = _triangle(lambda row, col: row <= col)

    def body(q_ref, k_ref, v_ref, ng_ref, tri_ref, dm_ref, opre_ref, ctot_ref, *rest):
        grad_refs = rest[:n_w]
        dq_ref, dk_ref, dv_ref, dng_ref = rest[n_w:n_w + 4]
        lands = rest[n_w + 4:2 * n_w + 4]
        dk_acc, dv_acc, send_sems, recv_sems, local_sems = rest[2 * n_w + 4:]
        p_id, b_id, i = pl.program_id(0), pl.program_id(1), pl.program_id(2)
        step = (p_id * n_seq + b_id) * nq + i
        plan = _ScatterPlan(grad_refs, lands, send_sems, recv_sems, local_sems)

        @pl.when(step == 0)
        def _():
            plan.start()

        masks = _head_masks()
        r, c = _tile_masks()
        strict = c < r
        upto = tri_ref[...]

        def prefix(x):
            return jnp.dot(x.astype(BF16), upto, preferred_element_type=F32)

        @pl.when(i == 0)
        def _():
            dk_acc[...] = jnp.zeros_like(dk_acc)
            dv_acc[...] = jnp.zeros_like(dv_acc)

        @pl.when(jnp.logical_and(b_id == 0, i == 0))
        def _():
            dng_ref[...] = jnp.zeros_like(dng_ref)

        o = opre_ref[...]
        rs = _head_rstd(o, masks)
        oh = o * rs
        dm = dm_ref[...]
        dng_ref[...] += jnp.sum(dm * oh, axis=0, keepdims=True)
        doh = dm * ng_ref[...]
        do = rs * (doh - oh * _head_mean(doh * oh, masks))

        heads = range(len(masks))
        qs = [q_ref[blk, :] * SB_SCALE for blk in QUERY_BLOCKS]
        dos = [do[blk] for blk in QUERY_BLOCKS]
        qhs = [[(q * m).astype(MXU_DTYPE) for m in masks] for q in qs]
        doms = [[(d * m).astype(MXU_DTYPE) for m in masks] for d in dos]
        qhts = [[(q * m).T.astype(MXU_DTYPE) for m in masks] for q in qs]
        domts = [[(d * m).T.astype(MXU_DTYPE) for m in masks] for d in dos]
        totals = [[jnp.max(ctot_ref[blk, :] * m, axis=1, keepdims=True) for m in masks] for blk in QUERY_BLOCKS]

        def tiles(js, work, carry):
            rows = [pl.ds(pl.multiple_of(j * SB_BLOCK, SB_BLOCK), SB_BLOCK) for j in js]
            ks = [k_ref[rw, :].astype(MXU_DTYPE) for rw in rows]
            vs = [v_ref[rw, :].astype(MXU_DTYPE) for rw in rows]
            zs = [[_dot_nt(qhs[qb][h], ks[ts]) for h in heads] for qb, ts, _ in work]
            das = [[_dot_nt(doms[qb][h], vs[ts]) for h in heads] for qb, ts, _ in work]
            sps = [[_softplus(z) for z in zw] for zw in zs]
            lsigs = [[z - sp for z, sp in zip(zw, sw)] for zw, sw in zip(zs, sps)]
            sigs = [[jnp.exp(ls) for ls in lw] for lw in lsigs]
            ccs = [[jnp.where(strict, sp, 0.0) for sp in sw] if diag else sw for sw, (_, _, diag) in zip(sps, work)]
            pres = [[prefix(cc) for cc in cw] for cw in ccs]
            out = [list(per_block) for per_block in carry]
            for h in heads:
                for w, (qb, ts, diag) in enumerate(work):
                    pc, pdl, dq_h = out[qb][h]
                    a = jnp.exp(lsigs[w][h] + pres[w][h] - (totals[qb][h] - pc))
                    if diag:
                        a = jnp.where(strict, a, 0.0)
                    dl = a * das[w][h]
                    dv_acc[js[ts]] += _dot(domts[qb][h], a)
                    dpre = jnp.dot(dl.astype(BF16), upto, preferred_element_type=F32)
                    dz = dl - sigs[w][h] * (pdl + dpre)
                    if diag:
                        dz = jnp.where(strict, dz, 0.0)
                    dzb = dz.astype(MXU_DTYPE)
                    dk_acc[js[ts]] += _dot(qhts[qb][h], dzb)
                    out[qb][h] = (pc + pres[w][h][:, SB_BLOCK - 1:SB_BLOCK], pdl + dpre[:, SB_BLOCK - 1:SB_BLOCK],
                                  dq_h + _dot(dzb, ks[ts]))
            return tuple(tuple(per_block) for per_block in out)

        zero = jnp.zeros((SB_BLOCK, 1), F32)
        start = ((zero, zero, jnp.zeros((SB_BLOCK, LANES), F32)),) * 2
        both = [(0, 0, False), (0, 1, False), (1, 0, False), (1, 1, False)]
        carry = lax.fori_loop(0, i, lambda s, cy: tiles([2 * s, 2 * s + 1], both, cy), (start, start))
        carry = tiles([2 * i, 2 * i + 1], [(0, 0, True), (1, 0, False), (1, 1, True)], carry)
        dq = jnp.concatenate([cb[0][2] * masks[0] + cb[1][2] * masks[1] for cb in carry], axis=0)
        dq_ref[...] = (dq * SB_SCALE).astype(dq_ref.dtype)

        @pl.when(i == nq - 1)
        def _():
            for j in range(seq // SB_BLOCK):
                tile_rows = slice(j * SB_BLOCK, (j + 1) * SB_BLOCK)
                dk_ref[tile_rows, :] = dk_acc[j].T.astype(dk_ref.dtype)
                dv_ref[tile_rows, :] = dv_acc[j].T.astype(dv_ref.dtype)

        @pl.when(step == n_steps - 1)
        def _():
            plan.finish()

    qspec = pl.BlockSpec((SB_STEP, LANES), lambda p, b, i: (b * nq + i, q0 + p))
    ospec = pl.BlockSpec((SB_STEP, LANES), lambda p, b, i: (b * nq + i, p))
    dmspec = pl.BlockSpec((SB_STEP, LANES), lambda p, b, i: (b * nq + i, N_PAIRS + p))
    full = lambda k: pl.BlockSpec((seq, LANES), lambda p, b, i: (b, k + p))
    vec = pl.BlockSpec((1, LANES), lambda p, b, i: (0, p))
    hbm = pl.BlockSpec(memory_space=pltpu.HBM)
    piece = jax.ShapeDtypeStruct((t, GROUP), MXU_DTYPE)
    outs = pl.pallas_call(
        body, name="sb_bwd", grid=(N_PAIRS, n_seq, nq),
        in_specs=[qspec, full(k0), full(v0), vec, pl.BlockSpec(tri.shape, lambda p, b, i: (0, 0)), dmspec, ospec, ospec]
        + [hbm] * n_w,
        out_specs=[ospec, full(0), full(0), vec] + [hbm] * n_w,
        out_shape=[piece, piece, piece, jax.ShapeDtypeStruct((1, GROUP), F32)]
        + [jax.ShapeDtypeStruct(g.shape, g.dtype) for g in grads],
        scratch_shapes=[pltpu.VMEM((seq // SB_BLOCK, LANES, SB_BLOCK), F32),
                        pltpu.VMEM((seq // SB_BLOCK, LANES, SB_BLOCK), F32),
                        pltpu.SemaphoreType.DMA((n_w * (N_DEV - 1),)), pltpu.SemaphoreType.DMA((n_w * (N_DEV - 1),)),
                        pltpu.SemaphoreType.DMA((n_w,))],
        compiler_params=_params(3),
    )(proj, proj, proj, norm_g, tri, dmix, opre, ctot, *grads)
    return outs[0], outs[1], outs[2], outs[3], list(outs[4:])


def _mesh_place():
    x, y, c = lax.axis_index("x"), lax.axis_index("y"), lax.axis_index("c")
    return x, y, c


def _peer(x, y, c, k):
    px = lax.rem(x + ((k >> 2) & 1), 2)
    py = lax.rem(y + ((k >> 1) & 1), 2)
    pc = lax.rem(c + (k & 1), 2)
    return (px, py, pc), 4 * px + 2 * py + pc


def _remote(src, dst, send_sem, recv_sem, to):
    return pltpu.make_async_remote_copy(src_ref=src, dst_ref=dst, send_sem=send_sem, recv_sem=recv_sem,
                                        device_id=to, device_id_type=pl.DeviceIdType.MESH)


class _GatherPlan:
    COPIES = 7

    def __init__(self, shards, gathered, send_sems, recv_sems, local_sems):
        x, y, c = _mesh_place()
        self.c = c
        self.me = (x, y, c)
        self.sibling = (x, y, 1 - c)
        self.chips = [(1 - x, y), (x, 1 - y), (1 - x, 1 - y)]
        self.tensors = list(zip(shards, gathered))
        self.send_sems, self.recv_sems, self.local_sems = send_sems, recv_sems, local_sems

    @staticmethod
    def _index(place):
        return 4 * place[0] + 2 * place[1] + place[2]

    def _copy(self, w, k, block, to, own=False):
        shard, gathered = self.tensors[w]
        slot = gathered.at[self._index(block)]
        n = w * self.COPIES + k
        return _remote(shard if own else slot, slot, self.send_sems.at[n], self.recv_sems.at[n], to)

    def _local(self, w):
        shard, gathered = self.tensors[w]
        return pltpu.make_async_copy(shard, gathered.at[self._index(self.me)], self.local_sems.at[w])

    def _first(self, w):
        return [self._copy(w, 0, self.me, self.sibling, own=True)] + [
            self._copy(w, 1 + j, self.me, (*chip, self.c), own=True) for j, chip in enumerate(self.chips)]

    def _passed(self, w):
        return [self._copy(w, 4 + j, (*chip, self.c), self.sibling) for j, chip in enumerate(self.chips)]

    def start(self):
        for w in range(len(self.tensors)):
            self._local(w).start()
            for cp in self._first(w):
                cp.start()

    def forward(self):
        for w in range(len(self.tensors)):
            passed = self._passed(w)
            for j, chip in enumerate(self.chips):
                self._copy(w, 1 + j, (*chip, self.c), self.me).wait_recv()
                passed[j].start()

    def finish(self):
        for w in range(len(self.tensors)):
            self._copy(w, 0, self.sibling, self.me).wait_recv()
            for j, chip in enumerate(self.chips):
                self._copy(w, 4 + j, (*chip, 1 - self.c), self.me).wait_recv()
            for cp in self._first(w) + self._passed(w):
                cp.wait_send()
            self._local(w).wait()


class _ScatterPlan:
    def __init__(self, grads, lands, send_sems, recv_sems, local_sems):
        self.place = _mesh_place()
        x, y, c = self.place
        self.me = 4 * x + 2 * y + c
        self.tensors = list(zip(grads, lands))
        self.send_sems, self.recv_sems, self.local_sems = send_sems, recv_sems, local_sems

    def _copies(self, w):
        grad, land = self.tensors[w]
        out = []
        for k in range(1, N_DEV):
            peer, pidx = _peer(*self.place, k)
            n = w * (N_DEV - 1) + k - 1
            sems = (self.send_sems.at[n], self.recv_sems.at[n], peer)
            out.append((_remote(grad.at[pidx], land.at[self.me], *sems), _remote(grad.at[pidx], land.at[pidx], *sems)))
        return out

    def _local(self, w):
        grad, land = self.tensors[w]
        return pltpu.make_async_copy(grad.at[self.me], land.at[self.me], self.local_sems.at[w])

    def start(self):
        for w in range(len(self.tensors)):
            self._local(w).start()
            for send, _ in self._copies(w):
                send.start()

    def finish(self):
        for w in range(len(self.tensors)):
            copies = self._copies(w)
            for _, arrival in copies:
                arrival.wait_recv()
            for send, _ in copies:
                send.wait_send()
            self._local(w).wait()


def _cast_shards(shards):
    def body(*refs):
        n = len(refs) // 2
        for src, dst in zip(refs[:n], refs[n:]):
            dst[...] = src[...].astype(dst.dtype)

    vmem = pl.BlockSpec(memory_space=pltpu.VMEM)
    return pl.pallas_call(
        body, name="cast_shards", in_specs=[vmem] * len(shards), out_specs=[vmem] * len(shards),
        out_shape=[jax.ShapeDtypeStruct(s.shape, BF16) for s in shards],
        compiler_params=pltpu.CompilerParams(vmem_limit_bytes=VMEM_LIMIT),
    )(*shards)


def _gather_w_in(shard):
    rows, cols = shard.shape

    def body(w_ref, out_ref, send_sems, recv_sems, local_sems):
        plan = _GatherPlan([w_ref], [out_ref], send_sems, recv_sems, local_sems)
        plan.start()
        plan.forward()
        plan.finish()

    vmem = pl.BlockSpec(memory_space=pltpu.VMEM)
    return pl.pallas_call(
        body, name="gather_w_in", in_specs=[vmem], out_specs=vmem,
        out_shape=jax.ShapeDtypeStruct((N_DEV, rows, cols), shard.dtype),
        scratch_shapes=[pltpu.SemaphoreType.DMA((_GatherPlan.COPIES,)), pltpu.SemaphoreType.DMA((_GatherPlan.COPIES,)),
                        pltpu.SemaphoreType.DMA((1,))],
        compiler_params=pltpu.CompilerParams(vmem_limit_bytes=VMEM_LIMIT),
    )(shard)


def _dw_rows(pieces, b, name, tn=512, tt=512):
    t, n = b.shape
    widths = [p.shape[1] for p in pieces]
    rows = sum(widths)
    tt = min(tt, t)
    steps = t // tt
    n_p = len(pieces)

    def body(*refs):
        piece_refs, b_ref, o_ref, acc = refs[:n_p], refs[n_p], refs[n_p + 1], refs[n_p + 2]
        s = pl.program_id(1)

        @pl.when(s == 0)
        def _():
            acc[...] = jnp.zeros_like(acc)

        bv = b_ref[...]
        off = 0
        for p_ref, width in zip(piece_refs, widths):
            acc[off:off + width, :] += _dot_tn(p_ref[...], bv)
            off += width

        @pl.when(s == steps - 1)
        def _():
            o_ref[...] = acc[...].astype(o_ref.dtype)

    return pl.pallas_call(
        body, name=name, grid=(n // tn, steps),
        in_specs=[pl.BlockSpec((tt, width), lambda j, s: (s, 0)) for width in widths]
        + [pl.BlockSpec((tt, tn), lambda j, s: (s, j))],
        out_specs=pl.BlockSpec((rows, tn), lambda j, s: (0, j)),
        out_shape=jax.ShapeDtypeStruct((rows, n), BF16),
        scratch_shapes=[pltpu.VMEM((rows, tn), F32)],
        compiler_params=_params(2),
    )(*pieces, b)


def _dh_norm_bwd(pairs, x, g, res, name, tm=256, after=()):
    m, d = x.shape
    tm = min(tm, m)
    n_p = len(pairs)
    n_steps = m // tm

    def body(*refs):
        x_ref, g_ref, res_ref = refs[2 * n_p:2 * n_p + 3]
        dx_ref, dg_ref = refs[2 * n_p + 3 + len(after):]

        @pl.when(pl.program_id(0) == 0)
        def _():
            dg_ref[...] = jnp.zeros_like(dg_ref)

        dh = None
        for q in range(n_p):
            part = _dot(refs[2 * q][...], refs[2 * q + 1][...])
            dh = part if dh is None else dh + part
        xv = x_ref[...]
        r = lax.rsqrt(jnp.mean(xv * xv, axis=-1, keepdims=True) + EPS)
        xh = xv * r
        dxh = dh * g_ref[...]
        dx_ref[...] = res_ref[...] + r * (dxh - xh * jnp.mean(dxh * xh, axis=-1, keepdims=True))
        dg_ref[...] += jnp.sum(dh * xh, axis=0, keepdims=True)

    in_specs, args = [], []
    for a, w, r in pairs:
        k = a.shape[1]
        in_specs += [pl.BlockSpec((tm, k), lambda i: (i, 0)),
                     pl.BlockSpec((k, d), functools.partial(lambda i, r: (r, 0), r=r or 0))]
        args += [a, w]
    row = pl.BlockSpec((tm, d), lambda i: (i, 0))
    vec = pl.BlockSpec((1, d), lambda i: (0, 0))
    return pl.pallas_call(
        body, name=name, grid=(n_steps,),
        in_specs=in_specs + [row, vec, row] + [pl.BlockSpec(memory_space=pl.ANY)] * len(after),
        out_specs=[row, vec],
        out_shape=[jax.ShapeDtypeStruct((m, d), F32), jax.ShapeDtypeStruct((1, d), F32)],
        compiler_params=_params(1),
    )(*args, x, g, res, *after)


def _exchange_start(grad, name):
    def body(g_ref, land_ref, send_sem, recv_sem, local_sem, g_thru, land_thru, token):
        del g_thru, land_thru
        x, y, c = _mesh_place()
        me = 4 * x + 2 * y + c
        pltpu.make_async_copy(g_ref.at[me], land_ref.at[me], local_sem).start()
        for k in range(1, N_DEV):
            peer, pidx = _peer(x, y, c, k)
            _remote(g_ref.at[pidx], land_ref.at[me], send_sem, recv_sem, peer).start()
        token[...] = jnp.zeros_like(token)

    hbm = pl.BlockSpec(memory_space=pltpu.HBM)
    sem = pl.BlockSpec(memory_space=pltpu.SEMAPHORE)
    buf = pltpu.HBM(grad.shape, grad.dtype)
    return pl.pallas_call(
        body, name=name,
        out_shape=(pltpu.SemaphoreType.DMA(()), pltpu.SemaphoreType.DMA(()), pltpu.SemaphoreType.DMA(()), buf, buf,
                   jax.ShapeDtypeStruct((8, LANES), F32)),
        in_specs=(hbm, hbm), out_specs=(sem, sem, sem, hbm, hbm, pl.BlockSpec(memory_space=pltpu.VMEM)),
        input_output_aliases={0: 3, 1: 4},
        compiler_params=pltpu.CompilerParams(has_side_effects=pltpu.SideEffectType.DATAFLOW_SIDE_EFFECTING),
    )(pltpu.with_memory_space_constraint(grad, pltpu.HBM),
      pltpu.with_memory_space_constraint(lax.empty(grad.shape, grad.dtype), pltpu.HBM))


def _exchange_wait(send_sem, recv_sem, local_sem, grad, land, after, name):
    def body(g_ref, land_ref, send_sem, recv_sem, local_sem, *rest):
        x, y, c = _mesh_place()
        peer, _ = _peer(x, y, c, 1)
        others = pl.ds(0, N_DEV - 1)
        seven = _remote(g_ref.at[others], land_ref.at[others], send_sem, recv_sem, peer)
        seven.wait_send()
        seven.wait_recv()
        pltpu.make_async_copy(g_ref.at[0], land_ref.at[0], local_sem).wait()

    hbm = pl.BlockSpec(memory_space=pltpu.HBM)
    sem = pl.BlockSpec(memory_space=pltpu.SEMAPHORE)
    buf = pltpu.HBM(grad.shape, grad.dtype)
    return pl.pallas_call(
        body, name=name, out_shape=(buf, buf),
        in_specs=(hbm, hbm, sem, sem, sem) + (pl.BlockSpec(memory_space=pl.ANY),) * len(after), out_specs=(hbm, hbm),
        input_output_aliases={0: 0, 1: 1},
        compiler_params=pltpu.CompilerParams(has_side_effects=pltpu.SideEffectType.DATAFLOW_SIDE_EFFECTING),
    )(grad, land, send_sem, recv_sem, local_sem, *after)[1]


SMALL_LAYOUT = ((0, 0, 0, 0, D_MODEL), (1, 0, 1, 0, GROUP), (1, 1, 1, GROUP, GROUP), (2, 0, 2, 0, GROUP),
                (3, 0, 2, GROUP, GROUP), (4, 0, 3, 0, D_MODEL), (5, 0, 4, 0, D_MODEL))
LOSS_ROW = 5
N_SMALL = 6


def _small_step(grads, loss_part, ws, ms, vs):
    def body(*refs):
        g_in, loss_in = refs[:N_SMALL], refs[N_SMALL]
        params = [refs[1 + (q + 1) * N_SMALL:1 + (q + 2) * N_SMALL] for q in range(3)]
        o0 = 1 + 4 * N_SMALL
        outs = [refs[o0 + q * N_SMALL:o0 + (q + 1) * N_SMALL] for q in range(4)]
        loss_out = refs[o0 + 4 * N_SMALL]
        pack, land, wp, mp, vp, send_sems, recv_sems = refs[o0 + 4 * N_SMALL + 1:]

        def place(dst, srcs):
            dst[...] = jnp.zeros_like(dst)
            for p, sr, dr, dc, width in SMALL_LAYOUT:
                dst[dr:dr + 1, dc:dc + width] = srcs[p][sr:sr + 1, :]

        place(pack, g_in)
        pack[LOSS_ROW:LOSS_ROW + 1, 0:LANES] = loss_in[...]
        for dst, srcs in zip((wp, mp, vp), params):
            place(dst, srcs)

        x, y, c = _mesh_place()
        me = 4 * x + 2 * y + c
        land[me] = pack[...]
        sends = []
        for k in range(1, N_DEV):
            peer, _ = _peer(x, y, c, k)
            cp = _remote(pack, land.at[me], send_sems.at[k - 1], recv_sems.at[k - 1], peer)
            cp.start()
            sends.append(cp)
        for k in range(1, N_DEV):
            peer, pidx = _peer(x, y, c, k)
            _remote(pack, land.at[pidx], send_sems.at[k - 1], recv_sems.at[k - 1], peer).wait_recv()
        for cp in sends:
            cp.wait_send()

        g = land[0]
        for d in range(1, N_DEV):
            g = g + land[d]
        delta, nm, nv = _adam(wp[...], g, mp[...], vp[...])
        for val, out in zip((g, delta, nm, nv), outs):
            for p, sr, dr, dc, width in SMALL_LAYOUT:
                out[p][sr:sr + 1, :] = val[dr:dr + 1, dc:dc + width]
        loss_out[...] = g[LOSS_ROW:LOSS_ROW + 1, 0:LANES]

    vmem = pl.BlockSpec(memory_space=pltpu.VMEM)
    n_in = 1 + 4 * N_SMALL
    shapes = [jax.ShapeDtypeStruct(w.shape, F32) for w in ws]
    packed = pltpu.VMEM((SMALL_ROWS, D_MODEL), F32)
    outs = pl.pallas_call(
        body, name="small_step", in_specs=[vmem] * n_in, out_specs=[vmem] * (4 * N_SMALL + 1),
        out_shape=shapes * 4 + [jax.ShapeDtypeStruct((1, LANES), F32)],
        scratch_shapes=[packed, pltpu.VMEM((N_DEV, SMALL_ROWS, D_MODEL), F32), packed, packed, packed,
                        pltpu.SemaphoreType.DMA((N_DEV - 1,)), pltpu.SemaphoreType.DMA((N_DEV - 1,))],
    )(*grads, loss_part, *ws, *ms, *vs)
    return [outs[q * N_SMALL:(q + 1) * N_SMALL] for q in range(4)], outs[4 * N_SMALL]


def _adam(w, g, m, v):
    m = ADAM_B1 * m + (1.0 - ADAM_B1) * g
    v = ADAM_B2 * v + (1.0 - ADAM_B2) * (g * g)
    m_hat = m / (1.0 - ADAM_B1 ** ADAM_STEP)
    v_hat = v / (1.0 - ADAM_B2 ** ADAM_STEP)
    delta = -ADAM_LR * (m_hat / (jnp.sqrt(v_hat) + ADAM_EPS) + ADAM_WD * w)
    return delta, m, v


def _reduce_adamw(land, w, m, v, name, tr, after=()):
    _, rows, width = land.shape

    def body(land_ref, w_ref, m_ref, v_ref, *rest):
        g_ref, d_ref, nm_ref, nv_ref = rest[len(after):]
        g = land_ref[0].astype(F32)
        for d in range(1, N_DEV):
            g = g + land_ref[d].astype(F32)
        delta, nm, nv = _adam(w_ref[...], g, m_ref[...], v_ref[...])
        g_ref[...] = g
        d_ref[...] = delta
        nm_ref[...] = nm
        nv_ref[...] = nv

    row = pl.BlockSpec((tr, width), lambda i: (i, 0))
    out = jax.ShapeDtypeStruct((rows, width), F32)
    return pl.pallas_call(
        body, name=name, grid=(rows // tr,),
        in_specs=[pl.BlockSpec((N_DEV, tr, width), lambda i: (0, i, 0)), row, row, row]
        + [pl.BlockSpec(memory_space=pl.ANY)] * len(after),
        out_specs=[row, row, row, row], out_shape=[out, out, out, out],
        compiler_params=_params(1),
    )(land, w, m, v, *after)


def kernel(x, mix_norm_g, w_in, lower_bounds, hgrn_norm_g, sb_norm_g, w_out, ffn_norm_g, w_gate, w_up, w_down, final_norm_g, loss_target, m_mix_norm_g, m_w_in, m_lower_bounds, m_hgrn_norm_g, m_sb_norm_g, m_w_out, m_ffn_norm_g, m_w_gate, m_w_up, m_w_down, m_final_norm_g, v_mix_norm_g, v_w_in, v_lower_bounds, v_hgrn_norm_g, v_sb_norm_g, v_w_out, v_ffn_norm_g, v_w_gate, v_w_up, v_w_down, v_final_norm_g):
    n_seq, seq, d = x.shape
    t = n_seq * seq
    x2d = x.reshape(t, d)
    tgt = loss_target.reshape(t, d)
    final_g = final_norm_g.reshape(1, d)
    col_sharded = (True, False, True, True, False)

    def as_rows(ws):
        return [w[0].T if tr else w[0] for w, tr in zip(ws, col_sharded)]

    big_w = as_rows([w_in, w_out, w_gate, w_up, w_down])
    big_m = as_rows([m_w_in, m_w_out, m_w_gate, m_w_up, m_w_down])
    big_v = as_rows([v_w_in, v_w_out, v_w_gate, v_w_up, v_w_down])

    sh_in, sh_out, sh_gate, sh_up, sh_down = _cast_shards(big_w)
    wt_in = _gather_w_in(sh_in).reshape(IN_COLS, d)
    h1 = _rmsnorm_fwd(x2d, mix_norm_g, "norm_mix")
    proj_h = _mm_nt(h1, wt_in, "proj_hgrn", rows=(0, 4 * GROUP), tm=1024, tk=1024)
    proj_s = _mm_nt(h1, wt_in, "proj_sb", rows=(4 * GROUP, 3 * GROUP), out_dtype=MXU_DTYPE, tm=1024, tk=512)
    mixed, oa_pre, states = _hgrn_fwd(proj_h, lower_bounds, hgrn_norm_g, n_seq, seq)
    mixed, ob_pre, ctot, gathered = _sb_fwd(proj_s, sb_norm_g, mixed, [sh_out, sh_gate, sh_up, sh_down], n_seq, seq)
    wf_out = gathered[0].reshape(d, d)
    wt_gate = gathered[1].reshape(D_FF, d)
    wt_up = gathered[2].reshape(D_FF, d)
    wf_down = gathered[3].reshape(D_FF, d)
    x1, h2 = _mix_out_norm(mixed, wf_out, x2d, ffn_norm_g, "mix_out")
    gate, up, ff = _ffn_up(h2, wt_gate, wt_up, "ffn_up")
    dx2, dx2m, d_final_g, loss_part = _ffn_down_loss(ff, wf_down, x1, tgt, final_g, "ffn_down_loss")

    dgate, dup = _ffn_bwd_act(dx2m, wf_down, gate, up, "ffn_bwd_act")
    dw_down = _mm_tn(ff, dx2m, "dw_down", tk=1408, tn=1024).reshape(N_DEV, D_FF // N_DEV, d)
    dw_gate = _mm_tn(dgate, h2, "dw_gate", tk=1408, tn=1024).reshape(N_DEV, D_FF // N_DEV, d)
    dw_up = _mm_tn(dup, h2, "dw_up", tk=1408, tn=1024).reshape(N_DEV, D_FF // N_DEV, d)
    dx1, d_ffn_g = _dh_norm_bwd([(dgate, wt_gate, None), (dup, wt_up, None)], x1, ffn_norm_g, dx2, "dh_ffn")
    dmix = _mm_nt(dx1, wf_out, "dmix")
    dw_out = _mm_tn(mixed, dx1, "dw_out", tk=512, tn=1024).reshape(N_DEV, d // N_DEV, d)
    dsq, dsk, dsv, d_sb_g, lands = _sb_bwd(proj_s, sb_norm_g, dmix, ob_pre, ctot, [dw_out, dw_gate, dw_up, dw_down],
                                            n_seq, seq)
    dhq, dhf, dhi, dhg, d_lb, d_hgrn_g = _hgrn_bwd(proj_h, lower_bounds, hgrn_norm_g, dmix, oa_pre, states, n_seq,
                                                   seq)
    dproj = [dhq, dhf, dhi, dhg, dsq, dsk, dsv]
    dw_in = _dw_rows(dproj, h1, "dw_in").reshape(N_DEV, IN_COLS // N_DEV, d)
    send_sem, recv_sem, local_sem, dw_in, land_in, token = _exchange_start(dw_in, "dw_in_send")
    dx, d_mix_g = _dh_norm_bwd([(piece, wt_in, k) for k, piece in enumerate(dproj)], x2d, mix_norm_g, dx1, "dh_mix",
                               after=(token,))

    tiles = {"in": 224, "out": 128, "gate": 176, "up": 176, "down": 176}
    keys = list(tiles)
    rest = [_reduce_adamw(land, w, m, v, "adamw_" + key, tr=tiles[key], after=(token,))
            for key, land, w, m, v in zip(keys[1:], lands, big_w[1:], big_m[1:], big_v[1:])]
    land_in = _exchange_wait(send_sem, recv_sem, local_sem, dw_in, land_in, [dx] + [res[0] for res in rest],
                             "dw_in_await")
    big = [_reduce_adamw(land_in, big_w[0], big_m[0], big_v[0], "adamw_in", tr=tiles["in"])] + rest
    big = [[r.T if tr else r for r in res] for res, tr in zip(big, col_sharded)]
    small, loss_row = _small_step(
        [d_mix_g, d_lb, d_hgrn_g, d_sb_g, d_ffn_g, d_final_g], loss_part,
        [mix_norm_g, lower_bounds, hgrn_norm_g, sb_norm_g, ffn_norm_g, final_g],
        [m_mix_norm_g, m_lower_bounds, m_hgrn_norm_g, m_sb_norm_g, m_ffn_norm_g, m_final_norm_g.reshape(1, d)],
        [v_mix_norm_g, v_lower_bounds, v_hgrn_norm_g, v_sb_norm_g, v_ffn_norm_g, v_final_norm_g.reshape(1, d)])

    outs = [loss_row[0, 0], dx.reshape(n_seq, seq, d)]
    for q in range(4):
        b_in, b_out, b_gate, b_up, b_down = [res[q][None] for res in big]
        s_mix, s_lb, s_hgrn, s_sb, s_ffn, s_final = small[q]
        outs += [s_mix, b_in, s_lb, s_hgrn, s_sb, b_out, s_ffn, b_gate, b_up, b_down, s_final.reshape(d)]
    return tuple(outs)
```

```python
import functools
import math

import jax
import jax.numpy as jnp
from jax import lax
from jax.experimental import pallas as pl
from jax.experimental.pallas import tpu as pltpu

F32 = jnp.float32
BF16 = jnp.bfloat16
MXU_DTYPE = BF16

EPS = 1e-6
D_MODEL = 1024
N_HEADS = 8
D_HEAD = 64
GROUP = N_HEADS * D_HEAD
IN_COLS = 7 * GROUP
D_FF = 2816
CHUNK = 64
LANES = 128
N_PAIRS = GROUP // LANES
SUPER = 256
SB_BLOCK = 256
N_DEV = 8

ADAM_LR = 0.001
ADAM_B1 = 0.9
ADAM_B2 = 0.999
ADAM_EPS = 1e-08
ADAM_WD = 0.01
ADAM_STEP = 10

SMALL_ROWS = 8
FF_TILE = D_FF // 2

VMEM_LIMIT = 48 * 1024 * 1024


def _params(n_axes, vmem=VMEM_LIMIT):
    return pltpu.CompilerParams(dimension_semantics=("arbitrary",) * n_axes, vmem_limit_bytes=vmem)


def _dot(a, b):
    return jnp.dot(a.astype(MXU_DTYPE), b.astype(MXU_DTYPE), preferred_element_type=F32)


def _dot_nt(a, b):
    return lax.dot_general(a.astype(MXU_DTYPE), b.astype(MXU_DTYPE), (((1,), (1,)), ((), ())),
                           preferred_element_type=F32)


def _dot_tn(a, b):
    return lax.dot_general(a.astype(MXU_DTYPE), b.astype(MXU_DTYPE), (((0,), (0,)), ((), ())),
                           preferred_element_type=F32)


def _split(x, parts):
    out, r = [], x
    for _ in range(parts):
        h = r.astype(BF16)
        out.append(h)
        r = r - h.astype(F32)
    return out


def _rsum_left(u, x, parts):
    acc = None
    for h in _split(x, parts):
        d = jnp.dot(u, h, preferred_element_type=F32)
        acc = d if acc is None else acc + d
    return acc


def _ones_where(mask):
    return jnp.where(mask, 1.0, 0.0).astype(BF16)


def _sigmoid(x):
    return 1.0 / (1.0 + jnp.exp(-x))


def _softplus(x):
    return jnp.maximum(x, 0.0) + jnp.log(1.0 + jnp.exp(-jnp.abs(x)))


def _head_masks():
    lane = lax.broadcasted_iota(jnp.int32, (1, LANES), 1)
    return [jnp.where(lane < D_HEAD, 1.0, 0.0), jnp.where(lane >= D_HEAD, 1.0, 0.0)]


def _head_rstd(o, masks):
    sq = o * o
    r = None
    for m in masks:
        ms = jnp.sum(sq * m, axis=1, keepdims=True) * (1.0 / D_HEAD)
        t = lax.rsqrt(ms + EPS) * m
        r = t if r is None else r + t
    return r


def _head_mean(t, masks):
    out = None
    for m in masks:
        v = jnp.sum(t * m, axis=1, keepdims=True) * (1.0 / D_HEAD) * m
        out = v if out is None else out + v
    return out


def _rmsnorm_fwd(x, g, name):
    t, d = x.shape
    tm = min(512, t)

    def body(x_ref, g_ref, o_ref):
        xv = x_ref[...]
        r = lax.rsqrt(jnp.mean(xv * xv, axis=-1, keepdims=True) + EPS)
        o_ref[...] = (xv * r * g_ref[...]).astype(o_ref.dtype)

    return pl.pallas_call(
        body, name=name, grid=(t // tm,),
        in_specs=[pl.BlockSpec((tm, d), lambda i: (i, 0)), pl.BlockSpec((1, d), lambda i: (0, 0))],
        out_specs=pl.BlockSpec((tm, d), lambda i: (i, 0)),
        out_shape=jax.ShapeDtypeStruct((t, d), MXU_DTYPE),
        compiler_params=_params(1),
    )(x, g)


def _mix_out_norm(a, w, res, g, name, tm=512):
    m, k = a.shape
    d = w.shape[1]
    tm = min(tm, m)

    def body(a_ref, w_ref, res_ref, g_ref, x_ref, h_ref):
        xv = res_ref[...] + _dot(a_ref[...], w_ref[...])
        x_ref[...] = xv
        r = lax.rsqrt(jnp.mean(xv * xv, axis=-1, keepdims=True) + EPS)
        h_ref[...] = (xv * r * g_ref[...]).astype(h_ref.dtype)

    row = pl.BlockSpec((tm, d), lambda i: (i, 0))
    return pl.pallas_call(
        body, name=name, grid=(m // tm,),
        in_specs=[pl.BlockSpec((tm, k), lambda i: (i, 0)), pl.BlockSpec((k, d), lambda i: (0, 0)), row,
                  pl.BlockSpec((1, d), lambda i: (0, 0))],
        out_specs=[row, row],
        out_shape=[jax.ShapeDtypeStruct((m, d), F32), jax.ShapeDtypeStruct((m, d), MXU_DTYPE)],
        compiler_params=_params(1),
    )(a, w, res, g)


def _ffn_down_loss(a, w, res, target, g, name, tm=512):
    m, k = a.shape
    d = w.shape[1]
    tm = min(tm, m)

    def body(a_ref, w_ref, res_ref, t_ref, g_ref, dx_ref, dxm_ref, dg_ref, loss_ref):
        xv = res_ref[...] + _dot(a_ref[...], w_ref[...])
        gv = g_ref[...]
        r = lax.rsqrt(jnp.mean(xv * xv, axis=-1, keepdims=True) + EPS)
        xh = xv * r
        e = xh * gv - t_ref[...]
        dy = e * (1.0 / d)
        dxh = dy * gv
        dxv = r * (dxh - xh * jnp.mean(dxh * xh, axis=-1, keepdims=True))
        dx_ref[...] = dxv
        dxm_ref[...] = dxv.astype(dxm_ref.dtype)

        @pl.when(pl.program_id(0) == 0)
        def _():
            dg_ref[...] = jnp.zeros_like(dg_ref)
            loss_ref[...] = jnp.zeros_like(loss_ref)

        dg_ref[...] += jnp.sum(dy * xh, axis=0, keepdims=True)
        part = 0.5 * jnp.sum(jnp.mean(e * e, axis=-1, keepdims=True), axis=0, keepdims=True)
        loss_ref[...] += jnp.broadcast_to(part, loss_ref.shape)

    row = pl.BlockSpec((tm, d), lambda i: (i, 0))
    vec = pl.BlockSpec((1, d), lambda i: (0, 0))
    return pl.pallas_call(
        body, name=name, grid=(m // tm,),
        in_specs=[pl.BlockSpec((tm, k), lambda i: (i, 0)), pl.BlockSpec((k, d), lambda i: (0, 0)), row, row, vec],
        out_specs=[row, row, vec, pl.BlockSpec((1, LANES), lambda i: (0, 0))],
        out_shape=[jax.ShapeDtypeStruct((m, d), F32), jax.ShapeDtypeStruct((m, d), MXU_DTYPE),
                   jax.ShapeDtypeStruct((1, d), F32), jax.ShapeDtypeStruct((1, LANES), F32)],
        compiler_params=_params(1),
    )(a, w, res, target, g)


def _mm_nt(a, w, name, rows=None, out_dtype=F32, tm=512, tk=512):
    m, n = a.shape
    row0, k = rows or (0, w.shape[0])
    tm, tk = min(tm, m), min(tk, k)
    first = row0 // tk

    def body(a_ref, w_ref, o_ref):
        o_ref[...] = _dot_nt(a_ref[...], w_ref[...]).astype(o_ref.dtype)

    return pl.pallas_call(
        body, name=name, grid=(k // tk, m // tm),
        in_specs=[pl.BlockSpec((tm, n), lambda j, i: (i, 0)), pl.BlockSpec((tk, n), lambda j, i: (first + j, 0))],
        out_specs=pl.BlockSpec((tm, tk), lambda j, i: (i, j)),
        out_shape=jax.ShapeDtypeStruct((m, k), out_dtype),
        compiler_params=_params(2),
    )(a, w)


def _mm_tn(a, b, name, tk, tn, tt=512, out_dtype=BF16):
    t, k = a.shape
    n = b.shape[1]
    tt = min(tt, t)
    steps = t // tt

    def body(a_ref, b_ref, o_ref, acc):
        s = pl.program_id(2)

        @pl.when(s == 0)
        def _():
            acc[...] = jnp.zeros_like(acc)

        acc[...] += _dot_tn(a_ref[...], b_ref[...])

        @pl.when(s == steps - 1)
        def _():
            o_ref[...] = acc[...].astype(o_ref.dtype)

    return pl.pallas_call(
        body, name=name, grid=(k // tk, n // tn, steps),
        in_specs=[pl.BlockSpec((tt, tk), lambda i, j, s: (s, i)), pl.BlockSpec((tt, tn), lambda i, j, s: (s, j))],
        out_specs=pl.BlockSpec((tk, tn), lambda i, j, s: (i, j)),
        out_shape=jax.ShapeDtypeStruct((k, n), out_dtype),
        scratch_shapes=[pltpu.VMEM((tk, tn), F32)],
        compiler_params=_params(3),
    )(a, b)


def _ffn_up(h, wg_t, wu_t, name, tm=512, tn=FF_TILE):
    m, k = h.shape
    n = wg_t.shape[0]
    tm = min(tm, m)

    def body(h_ref, wg_ref, wu_ref, gate_ref, up_ref, ff_ref):
        hv = h_ref[...]
        gate = _dot_nt(hv, wg_ref[...])
        up = _dot_nt(hv, wu_ref[...])
        gate_ref[...] = gate.astype(gate_ref.dtype)
        up_ref[...] = up.astype(up_ref.dtype)
        ff_ref[...] = (gate * _sigmoid(gate) * up).astype(ff_ref.dtype)

    wspec = pl.BlockSpec((tn, k), lambda j, i: (j, 0))
    ospec = pl.BlockSpec((tm, tn), lambda j, i: (i, j))
    return pl.pallas_call(
        body, name=name, grid=(n // tn, m // tm),
        in_specs=[pl.BlockSpec((tm, k), lambda j, i: (i, 0)), wspec, wspec],
        out_specs=[ospec, ospec, ospec],
        out_shape=[jax.ShapeDtypeStruct((m, n), MXU_DTYPE)] * 3,
        compiler_params=_params(2),
    )(h, wg_t, wu_t)


def _ffn_bwd_act(dx, wd, gate, up, name, tm=512, tn=FF_TILE):
    m, k = dx.shape
    n = wd.shape[0]
    tm = min(tm, m)

    def body(dx_ref, wd_ref, gate_ref, up_ref, dgate_ref, dup_ref):
        dff = _dot_nt(dx_ref[...], wd_ref[...])
        gate = gate_ref[...].astype(F32)
        sg = _sigmoid(gate)
        dgate_ref[...] = (dff * up_ref[...].astype(F32) * sg * (1.0 + gate * (1.0 - sg))).astype(dgate_ref.dtype)
        dup_ref[...] = (dff * gate * sg).astype(dup_ref.dtype)

    ospec = pl.BlockSpec((tm, tn), lambda j, i: (i, j))
    return pl.pallas_call(
        body, name=name, grid=(n // tn, m // tm),
        in_specs=[pl.BlockSpec((tm, k), lambda j, i: (i, 0)), pl.BlockSpec((tn, k), lambda j, i: (j, 0)),
                  ospec, ospec],
        out_specs=[ospec, ospec],
        out_shape=[jax.ShapeDtypeStruct((m, n), MXU_DTYPE), jax.ShapeDtypeStruct((m, n), MXU_DTYPE)],
        compiler_params=_params(2),
    )(dx, wd, gate, up)


def _chunk_masks():
    r = lax.broadcasted_iota(jnp.int32, (SUPER, SUPER), 0)
    c = lax.broadcasted_iota(jnp.int32, (SUPER, SUPER), 1)
    same = jnp.right_shift(r, 6) == jnp.right_shift(c, 6)
    lower = jnp.logical_and(same, c <= r)
    upper = jnp.logical_and(same, c >= r)
    return same, lower, upper


def _head_block_mask():
    r = lax.broadcasted_iota(jnp.int32, (LANES, LANES), 0)
    c = lax.broadcasted_iota(jnp.int32, (LANES, LANES), 1)
    return jnp.where(jnp.right_shift(r, 6) == jnp.right_shift(c, 6), 1.0, 0.0)


def _lower_bound(lb_raw):
    return 1.0 / (1.0 + jnp.exp(lb_raw[1:2, :] - lb_raw[0:1, :]))


PER_SUPER = SUPER // CHUNK
CHUNK_ROWS = [slice(c * CHUNK, (c + 1) * CHUNK) for c in range(PER_SUPER)]


def _over_chunks(rows):
    return jnp.concatenate([jnp.broadcast_to(r, (CHUNK, LANES)) for r in rows], axis=0)


def _hgrn_gates(q, hf, lb, lower_b):
    sig = _sigmoid(hf)
    f = lb + (1.0 - lb) * sig
    k = 1.0 - f
    lf = jnp.log(f)
    b = _rsum_left(lower_b, lf, 2)
    ends = [b[cr.stop - 1:cr.stop, :] for cr in CHUNK_ROWS]
    eb = jnp.exp(b)
    enb = jnp.exp(-b)
    edb = jnp.exp(_over_chunks(ends) - b)
    decs = [jnp.exp(e) for e in ends]
    return sig, f, k, decs, eb, enb, edb, q * eb, k * enb, k * edb


def _hgrn_fwd(proj, lower_bounds, norm_g, n_seq, seq):
    t = n_seq * seq
    n_super = seq // SUPER
    n_chunks = seq // CHUNK

    def body(q_ref, f_ref, i_ref, g_ref, lb_ref, ng_ref, out_ref, opre_ref, st_ref):
        masks = _head_masks()
        _, lower, _ = _chunk_masks()
        lower_b = _ones_where(lower)
        bd = _head_block_mask()
        lb = _lower_bound(lb_ref[...])
        ng = ng_ref[...]

        def step(sb, st):
            rows = pl.ds(pl.multiple_of(sb * SUPER, SUPER), SUPER)
            q, hf, v, hg = q_ref[rows, :], f_ref[rows, :], i_ref[rows, :], g_ref[rows, :]
            _, _, _, decs, _, _, _, qe, ke, kd = _hgrn_gates(q, hf, lb, lower_b)
            scores = [_dot_nt(qe * m, ke) for m in masks]
            updates = [_dot_tn(v[cr], kd[cr]) for cr in CHUNK_ROWS]
            states = [st]
            for dec, upd in zip(decs, updates):
                states.append(states[-1] * dec + bd * upd)
            for c in range(PER_SUPER):
                st_ref[0, 0, sb * PER_SUPER + c] = states[c]
            intra = [_dot(jnp.where(lower, p, 0.0), v) for p in scores]
            inter = [_dot_nt(qe[cr], s) for cr, s in zip(CHUNK_ROWS, states)]
            o = intra[0] * masks[0] + intra[1] * masks[1] + jnp.concatenate(inter, axis=0)
            opre_ref[rows, :] = o
            on = o * _head_rstd(o, masks) * ng
            out_ref[rows, :] = (on * hg * _sigmoid(hg)).astype(out_ref.dtype)
            return states[-1]

        lax.fori_loop(0, n_super, step, jnp.zeros((LANES, LANES), F32))

    def col(k):
        return pl.BlockSpec((seq, LANES), lambda p, b: (b, k * N_PAIRS + p))

    vec = lambda rows: pl.BlockSpec((rows, LANES), lambda p, b: (0, p))
    ospec = pl.BlockSpec((seq, LANES), lambda p, b: (b, p))
    return pl.pallas_call(
        body, name="hgrn_fwd", grid=(N_PAIRS, n_seq),
        in_specs=[col(0), col(1), col(2), col(3), vec(2), vec(1)],
        out_specs=[ospec, ospec,
                   pl.BlockSpec((1, 1, n_chunks, LANES, LANES), lambda p, b: (b, p, 0, 0, 0))],
        out_shape=[jax.ShapeDtypeStruct((t, 2 * GROUP), MXU_DTYPE), jax.ShapeDtypeStruct((t, GROUP), F32),
                   jax.ShapeDtypeStruct((n_seq, N_PAIRS, n_chunks, LANES, LANES), F32)],
        compiler_params=_params(2),
    )(proj, proj, proj, proj, lower_bounds, norm_g)


def _hgrn_bwd(proj, lower_bounds, norm_g, dmix, opre, states, n_seq, seq):
    t = n_seq * seq
    n_super = seq // SUPER
    n_chunks = seq // CHUNK
    per = SUPER // CHUNK

    def body(q_ref, f_ref, i_ref, g_ref, lb_ref, ng_ref, dm_ref, opre_ref, st_ref,
             dq_ref, df_ref, di_ref, dg_ref, dlb_ref, dng_ref):
        masks = _head_masks()
        _, lower, upper = _chunk_masks()
        lower_b, upper_b = _ones_where(lower), _ones_where(upper)
        bd = _head_block_mask()
        lb_raw = lb_ref[...]
        lb = _lower_bound(lb_raw)
        ng = ng_ref[...]

        @pl.when(pl.program_id(1) == 0)
        def _():
            dlb_ref[...] = jnp.zeros_like(dlb_ref)
            dng_ref[...] = jnp.zeros_like(dng_ref)

        def step(it, dst):
            sb = n_super - 1 - it
            rows = pl.ds(pl.multiple_of(sb * SUPER, SUPER), SUPER)
            q, hf, v, hg = q_ref[rows, :], f_ref[rows, :], i_ref[rows, :], g_ref[rows, :]
            sig, f, k, decs, eb, enb, edb, qe, ke, kd = _hgrn_gates(q, hf, lb, lower_b)
            o = opre_ref[rows, :]
            r = _head_rstd(o, masks)
            oh = o * r
            dm = dm_ref[rows, :]
            sg = _sigmoid(hg)
            dg_ref[rows, :] = (dm * oh * ng * sg * (1.0 + hg * (1.0 - sg))).astype(dg_ref.dtype)
            don = dm * hg * sg
            dng_ref[...] += jnp.sum(don * oh, axis=0, keepdims=True)
            doh = don * ng
            do = r * (doh - oh * _head_mean(doh * oh, masks))
            doms = [do * m for m in masks]
            qems = [qe * m for m in masks]
            scores = [_dot_nt(qem, ke) for qem in qems]
            dscores = [_dot_nt(dom, v) for dom in doms]
            prevs = [st_ref[0, 0, sb * per + c] for c in range(per)]
            dst_in = [_dot_tn(do[cr], qe[cr]) for cr in CHUNK_ROWS]
            dqe_i = [_dot(do[cr], prev) for cr, prev in zip(CHUNK_ROWS, prevs)]
            dsts = [None] * per
            for c in reversed(range(per)):
                dsts[c] = dst
                dst = bd * (dst * decs[c] + dst_in[c])
            ps = [jnp.where(lower, p, 0.0) for p in scores]
            dps = [jnp.where(lower, dp, 0.0) for dp in dscores]
            dqe_h = [_dot(dp, ke) for dp in dps]
            dke_h = [_dot_tn(dp, qem) for dp, qem in zip(dps, qems)]
            dv_h = [_dot_tn(p, dom) for p, dom in zip(ps, doms)]
            dus = [bd * d for d in dsts]
            dv_i = [_dot_nt(kd[cr], du) for cr, du in zip(CHUNK_ROWS, dus)]
            dkd_i = [_dot(v[cr], du) for cr, du in zip(CHUNK_ROWS, dus)]
            dqe = dqe_h[0] * masks[0] + dqe_h[1] * masks[1] + jnp.concatenate(dqe_i, axis=0)
            dke = dke_h[0] + dke_h[1]
            dv = dv_h[0] + dv_h[1] + jnp.concatenate(dv_i, axis=0)
            dkd = jnp.concatenate(dkd_i, axis=0)
            dk = dke * enb + dkd * edb
            db = dqe * qe - dke * ke - dkd * kd
            dkd_kd = dkd * kd
            dends = [jnp.sum(dkd_kd[cr], axis=0, keepdims=True)
                     + jnp.sum(dsts[c] * prevs[c], axis=0, keepdims=True) * decs[c]
                     for c, cr in enumerate(CHUNK_ROWS)]
            dlf = _rsum_left(upper_b, db, 2) + _over_chunks(dends)
            dfv = dlf / f - dk
            dq_ref[rows, :] = (dqe * eb).astype(dq_ref.dtype)
            di_ref[rows, :] = dv.astype(di_ref.dtype)
            df_ref[rows, :] = (dfv * (1.0 - lb) * sig * (1.0 - sig)).astype(df_ref.dtype)
            dlb = jnp.sum(dfv * (1.0 - sig), axis=0, keepdims=True)
            da0 = dlb * lb * (1.0 - lb)
            dlb_ref[0:1, :] += da0
            dlb_ref[1:2, :] -= da0
            return dst

        lax.fori_loop(0, n_super, step, jnp.zeros((LANES, LANES), F32))

    def col(k):
        return pl.BlockSpec((seq, LANES), lambda p, b: (b, k * N_PAIRS + p))

    vec = lambda rows: pl.BlockSpec((rows, LANES), lambda p, b: (0, p))
    ospec = pl.BlockSpec((seq, LANES), lambda p, b: (b, p))
    piece = jax.ShapeDtypeStruct((t, GROUP), MXU_DTYPE)
    return pl.pallas_call(
        body, name="hgrn_bwd", grid=(N_PAIRS, n_seq),
        in_specs=[col(0), col(1), col(2), col(3), vec(2), vec(1), ospec, ospec,
                  pl.BlockSpec((1, 1, n_chunks, LANES, LANES), lambda p, b: (b, p, 0, 0, 0))],
        out_specs=[ospec, ospec, ospec, ospec, vec(2), vec(1)],
        out_shape=[piece, piece, piece, piece,
                   jax.ShapeDtypeStruct((2, GROUP), F32), jax.ShapeDtypeStruct((1, GROUP), F32)],
        compiler_params=_params(2),
    )(proj, proj, proj, proj, lower_bounds, norm_g, dmix, opre, states)


SB_SCALE = 1.0 / math.sqrt(D_HEAD)


SB_STEP = 2 * SB_BLOCK
QUERY_BLOCKS = (slice(0, SB_BLOCK), slice(SB_BLOCK, SB_STEP))


def _triangle(keep):
    row = lax.broadcasted_iota(jnp.int32, (SB_BLOCK, SB_BLOCK), 0)
    col = lax.broadcasted_iota(jnp.int32, (SB_BLOCK, SB_BLOCK), 1)
    return _ones_where(keep(row, col))


def _tile_masks():
    r = lax.broadcasted_iota(jnp.int32, (SB_BLOCK, SB_BLOCK), 0)
    c = lax.broadcasted_iota(jnp.int32, (SB_BLOCK, SB_BLOCK), 1)
    return r, c


def _sb_fwd(proj, norm_g, mixed, shards, n_seq, seq):
    t = n_seq * seq
    nq = seq // SB_STEP
    q0, k0, v0 = 0, N_PAIRS, 2 * N_PAIRS
    n_w = len(shards)
    n_steps = N_PAIRS * n_seq * nq
    tri = _triangle(lambda row, col: row >= col)

    def body(q_ref, k_ref, v_ref, ng_ref, tri_ref, mixed_in, *rest):
        del mixed_in
        shard_refs = rest[:n_w]
        out_ref, opre_ref, ctot_ref = rest[n_w:n_w + 3]
        gathered = rest[n_w + 3:2 * n_w + 3]
        send_sems, recv_sems, local_sems = rest[2 * n_w + 3:]
        i = pl.program_id(2)
        step = (pl.program_id(0) * n_seq + pl.program_id(1)) * nq + i
        plan = _GatherPlan(shard_refs, gathered, send_sems, recv_sems, local_sems)

        @pl.when(step == 0)
        def _():
            plan.start()

        @pl.when(step == (3 * n_steps) // 4)
        def _():
            plan.forward()

        masks = _head_masks()
        r, c = _tile_masks()
        strict = c < r
        suffix = tri_ref[...]
        qhs = [[(q_ref[blk, :] * SB_SCALE * m).astype(MXU_DTYPE) for m in masks] for blk in QUERY_BLOCKS]

        def tiles(js, work, carry):
            rows = [pl.ds(pl.multiple_of(j * SB_BLOCK, SB_BLOCK), SB_BLOCK) for j in js]
            ks = [k_ref[rw, :].astype(MXU_DTYPE) for rw in rows]
            vs = [v_ref[rw, :].astype(MXU_DTYPE) for rw in rows]
            zs = [[_dot_nt(qh, ks[ts]) for qh in qhs[qb]] for qb, ts, _ in work]
            ccs = [[_softplus(z) for z in zw] for zw in zs]
            ccs = [[jnp.where(strict, cc, 0.0) for cc in cw] if diag else cw for cw, (_, _, diag) in zip(ccs, work)]
            sums = [[jnp.dot(cc.astype(BF16), suffix, preferred_element_type=F32) for cc in cw] for cw in ccs]
            out = [list(per_block) for per_block in carry]
            for h in range(len(masks)):
                for w, (qb, ts, diag) in enumerate(work):
                    run, acc = out[qb][h]
                    a = jnp.exp(zs[w][h] - (sums[w][h] + run))
                    if diag:
                        a = jnp.where(strict, a, 0.0)
                    out[qb][h] = (run + sums[w][h][:, 0:1], acc + _dot(a, vs[ts]))
            return tuple(tuple(per_block) for per_block in out)

        start = ((jnp.zeros((SB_BLOCK, 1), F32), jnp.zeros((SB_BLOCK, LANES), F32)),) * 2
        carry = tiles([2 * i, 2 * i + 1], [(0, 0, True), (1, 1, True), (1, 0, False)], (start, start))
        both = [(0, 0, False), (0, 1, False), (1, 0, False), (1, 1, False)]
        carry = lax.fori_loop(0, i, lambda s, cy: tiles([2 * (i - s) - 1, 2 * (i - s) - 2], both, cy), carry)
        opre = jnp.concatenate([cb[0][1] * masks[0] + cb[1][1] * masks[1] for cb in carry], axis=0)
        ctot = jnp.concatenate([cb[0][0] * masks[0] + cb[1][0] * masks[1] for cb in carry], axis=0)
        opre_ref[...] = opre
        ctot_ref[...] = ctot
        out_ref[...] = (opre * _head_rstd(opre, masks) * ng_ref[...]).astype(out_ref.dtype)

        @pl.when(step == n_steps - 1)
        def _():
            plan.finish()

    qspec = pl.BlockSpec((SB_STEP, LANES), lambda p, b, i: (b * nq + i, q0 + p))
    ospec = pl.BlockSpec((SB_STEP, LANES), lambda p, b, i: (b * nq + i, p))
    hbm = pl.BlockSpec(memory_space=pltpu.HBM)
    outs = pl.pallas_call(
        body, name="sb_fwd", grid=(N_PAIRS, n_seq, nq),
        in_specs=[qspec,
                  pl.BlockSpec((seq, LANES), lambda p, b, i: (b, k0 + p)),
                  pl.BlockSpec((seq, LANES), lambda p, b, i: (b, v0 + p)),
                  pl.BlockSpec((1, LANES), lambda p, b, i: (0, p)),
                  pl.BlockSpec(tri.shape, lambda p, b, i: (0, 0)), hbm] + [hbm] * n_w,
        out_specs=[pl.BlockSpec((SB_STEP, LANES), lambda p, b, i: (b * nq + i, N_PAIRS + p)), ospec, ospec]
        + [hbm] * n_w,
        out_shape=[jax.ShapeDtypeStruct(mixed.shape, mixed.dtype), jax.ShapeDtypeStruct((t, GROUP), F32),
                   jax.ShapeDtypeStruct((t, GROUP), F32)]
        + [jax.ShapeDtypeStruct((N_DEV,) + s.shape, s.dtype) for s in shards],
        scratch_shapes=[pltpu.SemaphoreType.DMA((n_w * _GatherPlan.COPIES,)),
                        pltpu.SemaphoreType.DMA((n_w * _GatherPlan.COPIES,)), pltpu.SemaphoreType.DMA((n_w,))],
        input_output_aliases={5: 0},
        compiler_params=_params(3),
    )(proj, proj, proj, norm_g, tri, mixed, *shards)
    return outs[0], outs[1], outs[2], list(outs[3:])


def _sb_bwd(proj, norm_g, dmix, opre, ctot, grads, n_seq, seq):
    t = n_seq * seq
    nq = seq // SB_STEP
    q0, k0, v0 = 0, N_PAIRS, 2 * N_PAIRS
    n_w = len(grads)
    n_steps = N_PAIRS * n_seq * nq
    tri = _triangle(lambda row, col: row <= col)

    def body(q_ref, k_ref, v_ref, ng_ref, tri_ref, dm_ref, opre_ref, ctot_ref, *rest):
        grad_refs = rest[:n_w]
        dq_ref, dk_ref, dv_ref, dng_ref = rest[n_w:n_w + 4]
        lands = rest[n_w + 4:2 * n_w + 4]
        dk_acc, dv_acc, send_sems, recv_sems, local_sems = rest[2 * n_w + 4:]
        p_id, b_id, i = pl.program_id(0), pl.program_id(1), pl.program_id(2)
        step = (p_id * n_seq + b_id) * nq + i
        plan = _ScatterPlan(grad_refs, lands, send_sems, recv_sems, local_sems)

        @pl.when(step == 0)
        def _():
            plan.start()

        masks = _head_masks()
        r, c = _tile_masks()
        strict = c < r
        upto = tri_ref[...]

        def prefix(x):
            return jnp.dot(x.astype(BF16), upto, preferred_element_type=F32)

        @pl.when(i == 0)
        def _():
            dk_acc[...] = jnp.zeros_like(dk_acc)
            dv_acc[...] = jnp.zeros_like(dv_acc)

        @pl.when(jnp.logical_and(b_id == 0, i == 0))
        def _():
            dng_ref[...] = jnp.zeros_like(dng_ref)

        o = opre_ref[...]
        rs = _head_rstd(o, masks)
        oh = o * rs
        dm = dm_ref[...]
        dng_ref[...] += jnp.sum(dm * oh, axis=0, keepdims=True)
        doh = dm * ng_ref[...]
        do = rs * (doh - oh * _head_mean(doh * oh, masks))

        heads = range(len(masks))
        qs = [q_ref[blk, :] * SB_SCALE for blk in QUERY_BLOCKS]
        dos = [do[blk] for blk in QUERY_BLOCKS]
        qhs = [[(q * m).astype(MXU_DTYPE) for m in masks] for q in qs]
        doms = [[(d * m).astype(MXU_DTYPE) for m in masks] for d in dos]
        qhts = [[(q * m).T.astype(MXU_DTYPE) for m in masks] for q in qs]
        domts = [[(d * m).T.astype(MXU_DTYPE) for m in masks] for d in dos]
        totals = [[jnp.max(ctot_ref[blk, :] * m, axis=1, keepdims=True) for m in masks] for blk in QUERY_BLOCKS]

        def tiles(js, work, carry):
            rows = [pl.ds(pl.multiple_of(j * SB_BLOCK, SB_BLOCK), SB_BLOCK) for j in js]
            ks = [k_ref[rw, :].astype(MXU_DTYPE) for rw in rows]
            vs = [v_ref[rw, :].astype(MXU_DTYPE) for rw in rows]
            zs = [[_dot_nt(qhs[qb][h], ks[ts]) for h in heads] for qb, ts, _ in work]
            das = [[_dot_nt(doms[qb][h], vs[ts]) for h in heads] for qb, ts, _ in work]
            sps = [[_softplus(z) for z in zw] for zw in zs]
            lsigs = [[z - sp for z, sp in zip(zw, sw)] for zw, sw in zip(zs, sps)]
            sigs = [[jnp.exp(ls) for ls in lw] for lw in lsigs]
            ccs = [[jnp.where(strict, sp, 0.0) for sp in sw] if diag else sw for sw, (_, _, diag) in zip(sps, work)]
            pres = [[prefix(cc) for cc in cw] for cw in ccs]
            out = [list(per_block) for per_block in carry]
            for h in heads:
                for w, (qb, ts, diag) in enumerate(work):
                    pc, pdl, dq_h = out[qb][h]
                    a = jnp.exp(lsigs[w][h] + pres[w][h] - (totals[qb][h] - pc))
                    if diag:
                        a = jnp.where(strict, a, 0.0)
                    dl = a * das[w][h]
                    dv_acc[js[ts]] += _dot(domts[qb][h], a)
                    dpre = jnp.dot(dl.astype(BF16), upto, preferred_element_type=F32)
                    dz = dl - sigs[w][h] * (pdl + dpre)
                    if diag:
                        dz = jnp.where(strict, dz, 0.0)
                    dzb = dz.astype(MXU_DTYPE)
                    dk_acc[js[ts]] += _dot(qhts[qb][h], dzb)
                    out[qb][h] = (pc + pres[w][h][:, SB_BLOCK - 1:SB_BLOCK], pdl + dpre[:, SB_BLOCK - 1:SB_BLOCK],
                                  dq_h + _dot(dzb, ks[ts]))
            return tuple(tuple(per_block) for per_block in out)

        zero = jnp.zeros((SB_BLOCK, 1), F32)
        start = ((zero, zero, jnp.zeros((SB_BLOCK, LANES), F32)),) * 2
        both = [(0, 0, False), (0, 1, False), (1, 0, False), (1, 1, False)]
        carry = lax.fori_loop(0, i, lambda s, cy: tiles([2 * s, 2 * s + 1], both, cy), (start, start))
        carry = tiles([2 * i, 2 * i + 1], [(0, 0, True), (1, 0, False), (1, 1, True)], carry)
        dq = jnp.concatenate([cb[0][2] * masks[0] + cb[1][2] * masks[1] for cb in carry], axis=0)
        dq_ref[...] = (dq * SB_SCALE).astype(dq_ref.dtype)

        @pl.when(i == nq - 1)
        def _():
            for j in range(seq // SB_BLOCK):
                tile_rows = slice(j * SB_BLOCK, (j + 1) * SB_BLOCK)
                dk_ref[tile_rows, :] = dk_acc[j].T.astype(dk_ref.dtype)
                dv_ref[tile_rows, :] = dv_acc[j].T.astype(dv_ref.dtype)

        @pl.when(step == n_steps - 1)
        def _():
            plan.finish()

    qspec = pl.BlockSpec((SB_STEP, LANES), lambda p, b, i: (b * nq + i, q0 + p))
    ospec = pl.BlockSpec((SB_STEP, LANES), lambda p, b, i: (b * nq + i, p))
    dmspec = pl.BlockSpec((SB_STEP, LANES), lambda p, b, i: (b * nq + i, N_PAIRS + p))
    full = lambda k: pl.BlockSpec((seq, LANES), lambda p, b, i: (b, k + p))
    vec = pl.BlockSpec((1, LANES), lambda p, b, i: (0, p))
    hbm = pl.BlockSpec(memory_space=pltpu.HBM)
    piece = jax.ShapeDtypeStruct((t, GROUP), MXU_DTYPE)
    outs = pl.pallas_call(
        body, name="sb_bwd", grid=(N_PAIRS, n_seq, nq),
        in_specs=[qspec, full(k0), full(v0), vec, pl.BlockSpec(tri.shape, lambda p, b, i: (0, 0)), dmspec, ospec, ospec]
        + [hbm] * n_w,
        out_specs=[ospec, full(0), full(0), vec] + [hbm] * n_w,
        out_shape=[piece, piece, piece, jax.ShapeDtypeStruct((1, GROUP), F32)]
        + [jax.ShapeDtypeStruct(g.shape, g.dtype) for g in grads],
        scratch_shapes=[pltpu.VMEM((seq // SB_BLOCK, LANES, SB_BLOCK), F32),
                        pltpu.VMEM((seq // SB_BLOCK, LANES, SB_BLOCK), F32),
                        pltpu.SemaphoreType.DMA((n_w * (N_DEV - 1),)), pltpu.SemaphoreType.DMA((n_w * (N_DEV - 1),)),
                        pltpu.SemaphoreType.DMA((n_w,))],
        compiler_params=_params(3),
    )(proj, proj, proj, norm_g, tri, dmix, opre, ctot, *grads)
    return outs[0], outs[1], outs[2], outs[3], list(outs[4:])


def _mesh_place():
    x, y, c = lax.axis_index("x"), lax.axis_index("y"), lax.axis_index("c")
    return x, y, c


def _peer(x, y, c, k):
    px = lax.rem(x + ((k >> 2) & 1), 2)
    py = lax.rem(y + ((k >> 1) & 1), 2)
    pc = lax.rem(c + (k & 1), 2)
    return (px, py, pc), 4 * px + 2 * py + pc


def _remote(src, dst, send_sem, recv_sem, to):
    return pltpu.make_async_remote_copy(src_ref=src, dst_ref=dst, send_sem=send_sem, recv_sem=recv_sem,
                                        device_id=to, device_id_type=pl.DeviceIdType.MESH)


class _GatherPlan:
    COPIES = 7

    def __init__(self, shards, gathered, send_sems, recv_sems, local_sems):
        x, y, c = _mesh_place()
        self.c = c
        self.me = (x, y, c)
        self.sibling = (x, y, 1 - c)
        self.chips = [(1 - x, y), (x, 1 - y), (1 - x, 1 - y)]
        self.tensors = list(zip(shards, gathered))
        self.send_sems, self.recv_sems, self.local_sems = send_sems, recv_sems, local_sems

    @staticmethod
    def _index(place):
        return 4 * place[0] + 2 * place[1] + place[2]

    def _copy(self, w, k, block, to, own=False):
        shard, gathered = self.tensors[w]
        slot = gathered.at[self._index(block)]
        n = w * self.COPIES + k
        return _remote(shard if own else slot, slot, self.send_sems.at[n], self.recv_sems.at[n], to)

    def _local(self, w):
        shard, gathered = self.tensors[w]
        return pltpu.make_async_copy(shard, gathered.at[self._index(self.me)], self.local_sems.at[w])

    def _first(self, w):
        return [self._copy(w, 0, self.me, self.sibling, own=True)] + [
            self._copy(w, 1 + j, self.me, (*chip, self.c), own=True) for j, chip in enumerate(self.chips)]

    def _passed(self, w):
        return [self._copy(w, 4 + j, (*chip, self.c), self.sibling) for j, chip in enumerate(self.chips)]

    def start(self):
        for w in range(len(self.tensors)):
            self._local(w).start()
            for cp in self._first(w):
                cp.start()

    def forward(self):
        for w in range(len(self.tensors)):
            passed = self._passed(w)
            for j, chip in enumerate(self.chips):
                self._copy(w, 1 + j, (*chip, self.c), self.me).wait_recv()
                passed[j].start()

    def finish(self):
        for w in range(len(self.tensors)):
            self._copy(w, 0, self.sibling, self.me).wait_recv()
            for j, chip in enumerate(self.chips):
                self._copy(w, 4 + j, (*chip, 1 - self.c), self.me).wait_recv()
            for cp in self._first(w) + self._passed(w):
                cp.wait_send()
            self._local(w).wait()


class _ScatterPlan:
    def __init__(self, grads, lands, send_sems, recv_sems, local_sems):
        self.place = _mesh_place()
        x, y, c = self.place
        self.me = 4 * x + 2 * y + c
        self.tensors = list(zip(grads, lands))
        self.send_sems, self.recv_sems, self.local_sems = send_sems, recv_sems, local_sems

    def _copies(self, w):
        grad, land = self.tensors[w]
        out = []
        for k in range(1, N_DEV):
            peer, pidx = _peer(*self.place, k)
            n = w * (N_DEV - 1) + k - 1
            sems = (self.send_sems.at[n], self.recv_sems.at[n], peer)
            out.append((_remote(grad.at[pidx], land.at[self.me], *sems), _remote(grad.at[pidx], land.at[pidx], *sems)))
        return out

    def _local(self, w):
        grad, land = self.tensors[w]
        return pltpu.make_async_copy(grad.at[self.me], land.at[self.me], self.local_sems.at[w])

    def start(self):
        for w in range(len(self.tensors)):
            self._local(w).start()
            for send, _ in self._copies(w):
                send.start()

    def finish(self):
        for w in range(len(self.tensors)):
            copies = self._copies(w)
            for _, arrival in copies:
                arrival.wait_recv()
            for send, _ in copies:
                send.wait_send()
            self._local(w).wait()


def _cast_shards(shards):
    def body(*refs):
        n = len(refs) // 2
        for src, dst in zip(refs[:n], refs[n:]):
            dst[...] = src[...].astype(dst.dtype)

    vmem = pl.BlockSpec(memory_space=pltpu.VMEM)
    return pl.pallas_call(
        body, name="cast_shards", in_specs=[vmem] * len(shards), out_specs=[vmem] * len(shards),
        out_shape=[jax.ShapeDtypeStruct(s.shape, BF16) for s in shards],
        compiler_params=pltpu.CompilerParams(vmem_limit_bytes=VMEM_LIMIT),
    )(*shards)


def _gather_w_in(shard):
    rows, cols = shard.shape

    def body(w_ref, out_ref, send_sems, recv_sems, local_sems):
        plan = _GatherPlan([w_ref], [out_ref], send_sems, recv_sems, local_sems)
        plan.start()
        plan.forward()
        plan.finish()

    vmem = pl.BlockSpec(memory_space=pltpu.VMEM)
    return pl.pallas_call(
        body, name="gather_w_in", in_specs=[vmem], out_specs=vmem,
        out_shape=jax.ShapeDtypeStruct((N_DEV, rows, cols), shard.dtype),
        scratch_shapes=[pltpu.SemaphoreType.DMA((_GatherPlan.COPIES,)), pltpu.SemaphoreType.DMA((_GatherPlan.COPIES,)),
                        pltpu.SemaphoreType.DMA((1,))],
        compiler_params=pltpu.CompilerParams(vmem_limit_bytes=VMEM_LIMIT),
    )(shard)


def _dw_rows(pieces, b, name, tn=512, tt=512):
    t, n = b.shape
    widths = [p.shape[1] for p in pieces]
    rows = sum(widths)
    tt = min(tt, t)
    steps = t // tt
    n_p = len(pieces)

    def body(*refs):
        piece_refs, b_ref, o_ref, acc = refs[:n_p], refs[n_p], refs[n_p + 1], refs[n_p + 2]
        s = pl.program_id(1)

        @pl.when(s == 0)
        def _():
            acc[...] = jnp.zeros_like(acc)

        bv = b_ref[...]
        off = 0
        for p_ref, width in zip(piece_refs, widths):
            acc[off:off + width, :] += _dot_tn(p_ref[...], bv)
            off += width

        @pl.when(s == steps - 1)
        def _():
            o_ref[...] = acc[...].astype(o_ref.dtype)

    return pl.pallas_call(
        body, name=name, grid=(n // tn, steps),
        in_specs=[pl.BlockSpec((tt, width), lambda j, s: (s, 0)) for width in widths]
        + [pl.BlockSpec((tt, tn), lambda j, s: (s, j))],
        out_specs=pl.BlockSpec((rows, tn), lambda j, s: (0, j)),
        out_shape=jax.ShapeDtypeStruct((rows, n), BF16),
        scratch_shapes=[pltpu.VMEM((rows, tn), F32)],
        compiler_params=_params(2),
    )(*pieces, b)


def _dh_norm_bwd(pairs, x, g, res, name, tm=512, after=()):
    m, d = x.shape
    tm = min(tm, m)
    n_p = len(pairs)
    n_steps = m // tm

    def body(*refs):
        x_ref, g_ref, res_ref = refs[2 * n_p:2 * n_p + 3]
        dx_ref, dg_ref = refs[2 * n_p + 3 + len(after):]

        @pl.when(pl.program_id(0) == 0)
        def _():
            dg_ref[...] = jnp.zeros_like(dg_ref)

        dh = None
        for q in range(n_p):
            part = _dot(refs[2 * q][...], refs[2 * q + 1][...])
            dh = part if dh is None else dh + part
        xv = x_ref[...]
        r = lax.rsqrt(jnp.mean(xv * xv, axis=-1, keepdims=True) + EPS)
        xh = xv * r
        dxh = dh * g_ref[...]
        dx_ref[...] = res_ref[...] + r * (dxh - xh * jnp.mean(dxh * xh, axis=-1, keepdims=True))
        dg_ref[...] += jnp.sum(dh * xh, axis=0, keepdims=True)

    in_specs, args = [], []
    for a, w, r in pairs:
        k = a.shape[1]
        in_specs += [pl.BlockSpec((tm, k), lambda i: (i, 0)),
                     pl.BlockSpec((k, d), functools.partial(lambda i, r: (r, 0), r=r or 0),
                                  pipeline_mode=pl.Buffered(1))]
        args += [a, w]
    row = pl.BlockSpec((tm, d), lambda i: (i, 0))
    vec = pl.BlockSpec((1, d), lambda i: (0, 0))
    return pl.pallas_call(
        body, name=name, grid=(n_steps,),
        in_specs=in_specs + [row, vec, row] + [pl.BlockSpec(memory_space=pl.ANY)] * len(after),
        out_specs=[row, vec],
        out_shape=[jax.ShapeDtypeStruct((m, d), F32), jax.ShapeDtypeStruct((1, d), F32)],
        compiler_params=_params(1),
    )(*args, x, g, res, *after)


def _exchange_start(grad, name):
    def body(g_ref, land_ref, send_sem, recv_sem, local_sem, g_thru, land_thru, token):
        del g_thru, land_thru
        x, y, c = _mesh_place()
        me = 4 * x + 2 * y + c
        pltpu.make_async_copy(g_ref.at[me], land_ref.at[me], local_sem).start()
        for k in range(1, N_DEV):
            peer, pidx = _peer(x, y, c, k)
            _remote(g_ref.at[pidx], land_ref.at[me], send_sem, recv_sem, peer).start()
        token[...] = jnp.zeros_like(token)

    hbm = pl.BlockSpec(memory_space=pltpu.HBM)
    sem = pl.BlockSpec(memory_space=pltpu.SEMAPHORE)
    buf = pltpu.HBM(grad.shape, grad.dtype)
    return pl.pallas_call(
        body, name=name,
        out_shape=(pltpu.SemaphoreType.DMA(()), pltpu.SemaphoreType.DMA(()), pltpu.SemaphoreType.DMA(()), buf, buf,
                   jax.ShapeDtypeStruct((8, LANES), F32)),
        in_specs=(hbm, hbm), out_specs=(sem, sem, sem, hbm, hbm, pl.BlockSpec(memory_space=pltpu.VMEM)),
        input_output_aliases={0: 3, 1: 4},
        compiler_params=pltpu.CompilerParams(has_side_effects=pltpu.SideEffectType.DATAFLOW_SIDE_EFFECTING),
    )(pltpu.with_memory_space_constraint(grad, pltpu.HBM),
      pltpu.with_memory_space_constraint(lax.empty(grad.shape, grad.dtype), pltpu.HBM))


def _exchange_wait(send_sem, recv_sem, local_sem, grad, land, after, name):
    def body(g_ref, land_ref, send_sem, recv_sem, local_sem, *rest):
        x, y, c = _mesh_place()
        peer, _ = _peer(x, y, c, 1)
        others = pl.ds(0, N_DEV - 1)
        seven = _remote(g_ref.at[others], land_ref.at[others], send_sem, recv_sem, peer)
        seven.wait_send()
        seven.wait_recv()
        pltpu.make_async_copy(g_ref.at[0], land_ref.at[0], local_sem).wait()

    hbm = pl.BlockSpec(memory_space=pltpu.HBM)
    sem = pl.BlockSpec(memory_space=pltpu.SEMAPHORE)
    buf = pltpu.HBM(grad.shape, grad.dtype)
    return pl.pallas_call(
        body, name=name, out_shape=(buf, buf),
        in_specs=(hbm, hbm, sem, sem, sem) + (pl.BlockSpec(memory_space=pl.ANY),) * len(after), out_specs=(hbm, hbm),
        input_output_aliases={0: 0, 1: 1},
        compiler_params=pltpu.CompilerParams(has_side_effects=pltpu.SideEffectType.DATAFLOW_SIDE_EFFECTING),
    )(grad, land, send_sem, recv_sem, local_sem, *after)[1]


SMALL_LAYOUT = ((0, 0, 0, 0, D_MODEL), (1, 0, 1, 0, GROUP), (1, 1, 1, GROUP, GROUP), (2, 0, 2, 0, GROUP),
                (3, 0, 2, GROUP, GROUP), (4, 0, 3, 0, D_MODEL), (5, 0, 4, 0, D_MODEL))
LOSS_ROW = 5
N_SMALL = 6


def _small_step(grads, loss_part, ws, ms, vs):
    def body(*refs):
        g_in, loss_in = refs[:N_SMALL], refs[N_SMALL]
        params = [refs[1 + (q + 1) * N_SMALL:1 + (q + 2) * N_SMALL] for q in range(3)]
        o0 = 1 + 4 * N_SMALL
        outs = [refs[o0 + q * N_SMALL:o0 + (q + 1) * N_SMALL] for q in range(4)]
        loss_out = refs[o0 + 4 * N_SMALL]
        pack, land, wp, mp, vp, send_sems, recv_sems = refs[o0 + 4 * N_SMALL + 1:]

        def place(dst, srcs):
            dst[...] = jnp.zeros_like(dst)
            for p, sr, dr, dc, width in SMALL_LAYOUT:
                dst[dr:dr + 1, dc:dc + width] = srcs[p][sr:sr + 1, :]

        place(pack, g_in)
        pack[LOSS_ROW:LOSS_ROW + 1, 0:LANES] = loss_in[...]
        for dst, srcs in zip((wp, mp, vp), params):
            place(dst, srcs)

        x, y, c = _mesh_place()
        me = 4 * x + 2 * y + c
        land[me] = pack[...]
        sends = []
        for k in range(1, N_DEV):
            peer, _ = _peer(x, y, c, k)
            cp = _remote(pack, land.at[me], send_sems.at[k - 1], recv_sems.at[k - 1], peer)
            cp.start()
            sends.append(cp)
        for k in range(1, N_DEV):
            peer, pidx = _peer(x, y, c, k)
            _remote(pack, land.at[pidx], send_sems.at[k - 1], recv_sems.at[k - 1], peer).wait_recv()
        for cp in sends:
            cp.wait_send()

        g = land[0]
        for d in range(1, N_DEV):
            g = g + land[d]
        delta, nm, nv = _adam(wp[...], g, mp[...], vp[...])
        for val, out in zip((g, delta, nm, nv), outs):
            for p, sr, dr, dc, width in SMALL_LAYOUT:
                out[p][sr:sr + 1, :] = val[dr:dr + 1, dc:dc + width]
        loss_out[...] = g[LOSS_ROW:LOSS_ROW + 1, 0:LANES]

    vmem = pl.BlockSpec(memory_space=pltpu.VMEM)
    n_in = 1 + 4 * N_SMALL
    shapes = [jax.ShapeDtypeStruct(w.shape, F32) for w in ws]
    packed = pltpu.VMEM((SMALL_ROWS, D_MODEL), F32)
    outs = pl.pallas_call(
        body, name="small_step", in_specs=[vmem] * n_in, out_specs=[vmem] * (4 * N_SMALL + 1),
        out_shape=shapes * 4 + [jax.ShapeDtypeStruct((1, LANES), F32)],
        scratch_shapes=[packed, pltpu.VMEM((N_DEV, SMALL_ROWS, D_MODEL), F32), packed, packed, packed,
                        pltpu.SemaphoreType.DMA((N_DEV - 1,)), pltpu.SemaphoreType.DMA((N_DEV - 1,))],
    )(*grads, loss_part, *ws, *ms, *vs)
    return [outs[q * N_SMALL:(q + 1) * N_SMALL] for q in range(4)], outs[4 * N_SMALL]


def _adam(w, g, m, v):
    m = ADAM_B1 * m + (1.0 - ADAM_B1) * g
    v = ADAM_B2 * v + (1.0 - ADAM_B2) * (g * g)
    m_hat = m / (1.0 - ADAM_B1 ** ADAM_STEP)
    v_hat = v / (1.0 - ADAM_B2 ** ADAM_STEP)
    delta = -ADAM_LR * (m_hat / (jnp.sqrt(v_hat) + ADAM_EPS) + ADAM_WD * w)
    return delta, m, v


def _reduce_adamw(land, w, m, v, name, tr, after=()):
    _, rows, width = land.shape

    def body(land_ref, w_ref, m_ref, v_ref, *rest):
        g_ref, d_ref, nm_ref, nv_ref = rest[len(after):]
        g = land_ref[0].astype(F32)
        for d in range(1, N_DEV):
            g = g + land_ref[d].astype(F32)
        delta, nm, nv = _adam(w_ref[...], g, m_ref[...], v_ref[...])
        g_ref[...] = g
        d_ref[...] = delta
        nm_ref[...] = nm
        nv_ref[...] = nv

    row = pl.BlockSpec((tr, width), lambda i: (i, 0))
    out = jax.ShapeDtypeStruct((rows, width), F32)
    return pl.pallas_call(
        body, name=name, grid=(rows // tr,),
        in_specs=[pl.BlockSpec((N_DEV, tr, width), lambda i: (0, i, 0)), row, row, row]
        + [pl.BlockSpec(memory_space=pl.ANY)] * len(after),
        out_specs=[row, row, row, row], out_shape=[out, out, out, out],
        compiler_params=_params(1),
    )(land, w, m, v, *after)


def kernel(x, mix_norm_g, w_in, lower_bounds, hgrn_norm_g, sb_norm_g, w_out, ffn_norm_g, w_gate, w_up, w_down, final_norm_g, loss_target, m_mix_norm_g, m_w_in, m_lower_bounds, m_hgrn_norm_g, m_sb_norm_g, m_w_out, m_ffn_norm_g, m_w_gate, m_w_up, m_w_down, m_final_norm_g, v_mix_norm_g, v_w_in, v_lower_bounds, v_hgrn_norm_g, v_sb_norm_g, v_w_out, v_ffn_norm_g, v_w_gate, v_w_up, v_w_down, v_final_norm_g):
    n_seq, seq, d = x.shape
    t = n_seq * seq
    x2d = x.reshape(t, d)
    tgt = loss_target.reshape(t, d)
    final_g = final_norm_g.reshape(1, d)
    col_sharded = (True, False, True, True, False)

    def as_rows(ws):
        return [w[0].T if tr else w[0] for w, tr in zip(ws, col_sharded)]

    big_w = as_rows([w_in, w_out, w_gate, w_up, w_down])
    big_m = as_rows([m_w_in, m_w_out, m_w_gate, m_w_up, m_w_down])
    big_v = as_rows([v_w_in, v_w_out, v_w_gate, v_w_up, v_w_down])

    sh_in, sh_out, sh_gate, sh_up, sh_down = _cast_shards(big_w)
    wt_in = _gather_w_in(sh_in).reshape(IN_COLS, d)
    h1 = _rmsnorm_fwd(x2d, mix_norm_g, "norm_mix")
    proj_h = _mm_nt(h1, wt_in, "proj_hgrn", rows=(0, 4 * GROUP), tm=1024, tk=1024)
    proj_s = _mm_nt(h1, wt_in, "proj_sb", rows=(4 * GROUP, 3 * GROUP), out_dtype=MXU_DTYPE, tm=1024, tk=512)
    mixed, oa_pre, states = _hgrn_fwd(proj_h, lower_bounds, hgrn_norm_g, n_seq, seq)
    mixed, ob_pre, ctot, gathered = _sb_fwd(proj_s, sb_norm_g, mixed, [sh_out, sh_gate, sh_up, sh_down], n_seq, seq)
    wf_out = gathered[0].reshape(d, d)
    wt_gate = gathered[1].reshape(D_FF, d)
    wt_up = gathered[2].reshape(D_FF, d)
    wf_down = gathered[3].reshape(D_FF, d)
    x1, h2 = _mix_out_norm(mixed, wf_out, x2d, ffn_norm_g, "mix_out")
    gate, up, ff = _ffn_up(h2, wt_gate, wt_up, "ffn_up")
    dx2, dx2m, d_final_g, loss_part = _ffn_down_loss(ff, wf_down, x1, tgt, final_g, "ffn_down_loss")

    dgate, dup = _ffn_bwd_act(dx2m, wf_down, gate, up, "ffn_bwd_act")
    dw_down = _mm_tn(ff, dx2m, "dw_down", tk=1408, tn=1024).reshape(N_DEV, D_FF // N_DEV, d)
    dw_gate = _mm_tn(dgate, h2, "dw_gate", tk=1408, tn=1024).reshape(N_DEV, D_FF // N_DEV, d)
    dw_up = _mm_tn(dup, h2, "dw_up", tk=1408, tn=1024).reshape(N_DEV, D_FF // N_DEV, d)
    dx1, d_ffn_g = _dh_norm_bwd([(dgate, wt_gate, None), (dup, wt_up, None)], x1, ffn_norm_g, dx2, "dh_ffn")
    dmix = _mm_nt(dx1, wf_out, "dmix")
    dw_out = _mm_tn(mixed, dx1, "dw_out", tk=512, tn=1024).reshape(N_DEV, d // N_DEV, d)
    dsq, dsk, dsv, d_sb_g, lands = _sb_bwd(proj_s, sb_norm_g, dmix, ob_pre, ctot, [dw_out, dw_gate, dw_up, dw_down],
                                            n_seq, seq)
    dhq, dhf, dhi, dhg, d_lb, d_hgrn_g = _hgrn_bwd(proj_h, lower_bounds, hgrn_norm_g, dmix, oa_pre, states, n_seq,
                                                   seq)
    dproj = [dhq, dhf, dhi, dhg, dsq, dsk, dsv]
    dw_in = _dw_rows(dproj, h1, "dw_in").reshape(N_DEV, IN_COLS // N_DEV, d)
    send_sem, recv_sem, local_sem, dw_in, land_in, token = _exchange_start(dw_in, "dw_in_send")
    dx, d_mix_g = _dh_norm_bwd([(piece, wt_in, k) for k, piece in enumerate(dproj)], x2d, mix_norm_g, dx1, "dh_mix",
                               after=(token,))

    tiles = {"in": 224, "out": 128, "gate": 176, "up": 176, "down": 176}
    keys = list(tiles)
    rest = [_reduce_adamw(land, w, m, v, "adamw_" + key, tr=tiles[key], after=(token,))
            for key, land, w, m, v in zip(keys[1:], lands, big_w[1:], big_m[1:], big_v[1:])]
    land_in = _exchange_wait(send_sem, recv_sem, local_sem, dw_in, land_in, [dx] + [res[0] for res in rest],
                             "dw_in_await")
    big = [_reduce_adamw(land_in, big_w[0], big_m[0], big_v[0], "adamw_in", tr=tiles["in"])] + rest
    big = [[r.T if tr else r for r in res] for res, tr in zip(big, col_sharded)]
    small, loss_row = _small_step(
        [d_mix_g, d_lb, d_hgrn_g, d_sb_g, d_ffn_g, d_final_g], loss_part,
        [mix_norm_g, lower_bounds, hgrn_norm_g, sb_norm_g, ffn_norm_g, final_g],
        [m_mix_norm_g, m_lower_bounds, m_hgrn_norm_g, m_sb_norm_g, m_ffn_norm_g, m_final_norm_g.reshape(1, d)],
        [v_mix_norm_g, v_lower_bounds, v_hgrn_norm_g, v_sb_norm_g, v_ffn_norm_g, v_final_norm_g.reshape(1, d)])

    outs = [loss_row[0, 0], dx.reshape(n_seq, seq, d)]
    for q in range(4):
        b_in, b_out, b_gate, b_up, b_down = [res[q][None] for res in big]
        s_mix, s_lb, s_hgrn, s_sb, s_ffn, s_final = small[q]
        outs += [s_mix, b_in, s_lb, s_hgrn, s_sb, b_out, s_ffn, b_gate, b_up, b_down, s_final.reshape(d)]
    return tuple(outs)
```

```python
import functools
import math

import jax
import jax.numpy as jnp
from jax import lax
from jax.experimental import pallas as pl
from jax.experimental.pallas import tpu as pltpu

F32 = jnp.float32
BF16 = jnp.bfloat16
MXU_DTYPE = BF16

EPS = 1e-6
D_MODEL = 1024
N_HEADS = 8
D_HEAD = 64
GROUP = N_HEADS * D_HEAD
IN_COLS = 7 * GROUP
D_FF = 2816
CHUNK = 64
LANES = 128
N_PAIRS = GROUP // LANES
SUPER = 256
SB_BLOCK = 256
N_DEV = 8

ADAM_LR = 0.001
ADAM_B1 = 0.9
ADAM_B2 = 0.999
ADAM_EPS = 1e-08
ADAM_WD = 0.01
ADAM_STEP = 10

SMALL_ROWS = 8
FF_TILE = D_FF // 2

VMEM_LIMIT = 48 * 1024 * 1024


def _params(n_axes, vmem=VMEM_LIMIT):
    return pltpu.CompilerParams(dimension_semantics=("arbitrary",) * n_axes, vmem_limit_bytes=vmem)


def _dot(a, b):
    return jnp.dot(a.astype(MXU_DTYPE), b.astype(MXU_DTYPE), preferred_element_type=F32)


def _dot_nt(a, b):
    return lax.dot_general(a.astype(MXU_DTYPE), b.astype(MXU_DTYPE), (((1,), (1,)), ((), ())),
                           preferred_element_type=F32)


def _dot_tn(a, b):
    return lax.dot_general(a.astype(MXU_DTYPE), b.astype(MXU_DTYPE), (((0,), (0,)), ((), ())),
                           preferred_element_type=F32)


def _split(x, parts):
    out, r = [], x
    for _ in range(parts):
        h = r.astype(BF16)
        out.append(h)
        r = r - h.astype(F32)
    return out


def _rsum_left(u, x, parts):
    acc = None
    for h in _split(x, parts):
        d = jnp.dot(u, h, preferred_element_type=F32)
        acc = d if acc is None else acc + d
    return acc


def _ones_where(mask):
    return jnp.where(mask, 1.0, 0.0).astype(BF16)


def _sigmoid(x):
    return 1.0 / (1.0 + jnp.exp(-x))


def _softplus(x):
    return jnp.maximum(x, 0.0) + jnp.log(1.0 + jnp.exp(-jnp.abs(x)))


def _head_masks():
    lane = lax.broadcasted_iota(jnp.int32, (1, LANES), 1)
    return [jnp.where(lane < D_HEAD, 1.0, 0.0), jnp.where(lane >= D_HEAD, 1.0, 0.0)]


def _head_rstd(o, masks):
    sq = o * o
    r = None
    for m in masks:
        ms = jnp.sum(sq * m, axis=1, keepdims=True) * (1.0 / D_HEAD)
        t = lax.rsqrt(ms + EPS) * m
        r = t if r is None else r + t
    return r


def _head_mean(t, masks):
    out = None
    for m in masks:
        v = jnp.sum(t * m, axis=1, keepdims=True) * (1.0 / D_HEAD) * m
        out = v if out is None else out + v
    return out


def _rmsnorm_fwd(x, g, name):
    t, d = x.shape
    tm = min(512, t)

    def body(x_ref, g_ref, o_ref):
        xv = x_ref[...]
        r = lax.rsqrt(jnp.mean(xv * xv, axis=-1, keepdims=True) + EPS)
        o_ref[...] = (xv * r * g_ref[...]).astype(o_ref.dtype)

    return pl.pallas_call(
        body, name=name, grid=(t // tm,),
        in_specs=[pl.BlockSpec((tm, d), lambda i: (i, 0)), pl.BlockSpec((1, d), lambda i: (0, 0))],
        out_specs=pl.BlockSpec((tm, d), lambda i: (i, 0)),
        out_shape=jax.ShapeDtypeStruct((t, d), MXU_DTYPE),
        compiler_params=_params(1),
    )(x, g)


def _mix_out_norm(a, w, res, g, name, tm=512):
    m, k = a.shape
    d = w.shape[1]
    tm = min(tm, m)

    def body(a_ref, w_ref, res_ref, g_ref, x_ref, h_ref):
        xv = res_ref[...] + _dot(a_ref[...], w_ref[...])
        x_ref[...] = xv
        r = lax.rsqrt(jnp.mean(xv * xv, axis=-1, keepdims=True) + EPS)
        h_ref[...] = (xv * r * g_ref[...]).astype(h_ref.dtype)

    row = pl.BlockSpec((tm, d), lambda i: (i, 0))
    return pl.pallas_call(
        body, name=name, grid=(m // tm,),
        in_specs=[pl.BlockSpec((tm, k), lambda i: (i, 0)), pl.BlockSpec((k, d), lambda i: (0, 0)), row,
                  pl.BlockSpec((1, d), lambda i: (0, 0))],
        out_specs=[row, row],
        out_shape=[jax.ShapeDtypeStruct((m, d), F32), jax.ShapeDtypeStruct((m, d), MXU_DTYPE)],
        compiler_params=_params(1),
    )(a, w, res, g)


def _ffn_down_loss(a, w, res, target, g, name, tm=512):
    m, k = a.shape
    d = w.shape[1]
    tm = min(tm, m)

    def body(a_ref, w_ref, res_ref, t_ref, g_ref, dx_ref, dxm_ref, dg_ref, loss_ref):
        xv = res_ref[...] + _dot(a_ref[...], w_ref[...])
        gv = g_ref[...]
        r = lax.rsqrt(jnp.mean(xv * xv, axis=-1, keepdims=True) + EPS)
        xh = xv * r
        e = xh * gv - t_ref[...]
        dy = e * (1.0 / d)
        dxh = dy * gv
        dxv = r * (dxh - xh * jnp.mean(dxh * xh, axis=-1, keepdims=True))
        dx_ref[...] = dxv
        dxm_ref[...] = dxv.astype(dxm_ref.dtype)

        @pl.when(pl.program_id(0) == 0)
        def _():
            dg_ref[...] = jnp.zeros_like(dg_ref)
            loss_ref[...] = jnp.zeros_like(loss_ref)

        dg_ref[...] += jnp.sum(dy * xh, axis=0, keepdims=True)
        part = 0.5 * jnp.sum(jnp.mean(e * e, axis=-1, keepdims=True), axis=0, keepdims=True)
        loss_ref[...] += jnp.broadcast_to(part, loss_ref.shape)

    row = pl.BlockSpec((tm, d), lambda i: (i, 0))
    vec = pl.BlockSpec((1, d), lambda i: (0, 0))
    return pl.pallas_call(
        body, name=name, grid=(m // tm,),
        in_specs=[pl.BlockSpec((tm, k), lambda i: (i, 0)), pl.BlockSpec((k, d), lambda i: (0, 0)), row, row, vec],
        out_specs=[row, row, vec, pl.BlockSpec((1, LANES), lambda i: (0, 0))],
        out_shape=[jax.ShapeDtypeStruct((m, d), F32), jax.ShapeDtypeStruct((m, d), MXU_DTYPE),
                   jax.ShapeDtypeStruct((1, d), F32), jax.ShapeDtypeStruct((1, LANES), F32)],
        compiler_params=_params(1),
    )(a, w, res, target, g)


def _mm_nt(a, w, name, rows=None, out_dtype=F32, tm=512, tk=512):
    m, n = a.shape
    row0, k = rows or (0, w.shape[0])
    tm, tk = min(tm, m), min(tk, k)
    first = row0 // tk

    def body(a_ref, w_ref, o_ref):
        o_ref[...] = _dot_nt(a_ref[...], w_ref[...]).astype(o_ref.dtype)

    return pl.pallas_call(
        body, name=name, grid=(k // tk, m // tm),
        in_specs=[pl.BlockSpec((tm, n), lambda j, i: (i, 0)), pl.BlockSpec((tk, n), lambda j, i: (first + j, 0))],
        out_specs=pl.BlockSpec((tm, tk), lambda j, i: (i, j)),
        out_shape=jax.ShapeDtypeStruct((m, k), out_dtype),
        compiler_params=_params(2),
    )(a, w)


def _mm_tn(a, b, name, tk, tn, tt=512, out_dtype=BF16):
    t, k = a.shape
    n = b.shape[1]
    tt = min(tt, t)
    steps = t // tt

    def body(a_ref, b_ref, o_ref, acc):
        s = pl.program_id(2)

        @pl.when(s == 0)
        def _():
            acc[...] = jnp.zeros_like(acc)

        acc[...] += _dot_tn(a_ref[...], b_ref[...])

        @pl.when(s == steps - 1)
        def _():
            o_ref[...] = acc[...].astype(o_ref.dtype)

    return pl.pallas_call(
        body, name=name, grid=(k // tk, n // tn, steps),
        in_specs=[pl.BlockSpec((tt, tk), lambda i, j, s: (s, i)), pl.BlockSpec((tt, tn), lambda i, j, s: (s, j))],
        out_specs=pl.BlockSpec((tk, tn), lambda i, j, s: (i, j)),
        out_shape=jax.ShapeDtypeStruct((k, n), out_dtype),
        scratch_shapes=[pltpu.VMEM((tk, tn), F32)],
        compiler_params=_params(3),
    )(a, b)


def _ffn_up(h, wg_t, wu_t, name, tm=512, tn=FF_TILE):
    m, k = h.shape
    n = wg_t.shape[0]
    tm = min(tm, m)

    def body(h_ref, wg_ref, wu_ref, gate_ref, up_ref, ff_ref):
        hv = h_ref[...]
        gate = _dot_nt(hv, wg_ref[...])
        up = _dot_nt(hv, wu_ref[...])
        gate_ref[...] = gate.astype(gate_ref.dtype)
        up_ref[...] = up.astype(up_ref.dtype)
        ff_ref[...] = (gate * _sigmoid(gate) * up).astype(ff_ref.dtype)

    wspec = pl.BlockSpec((tn, k), lambda j, i: (j, 0))
    ospec = pl.BlockSpec((tm, tn), lambda j, i: (i, j))
    return pl.pallas_call(
        body, name=name, grid=(n // tn, m // tm),
        in_specs=[pl.BlockSpec((tm, k), lambda j, i: (i, 0)), wspec, wspec],
        out_specs=[ospec, ospec, ospec],
        out_shape=[jax.ShapeDtypeStruct((m, n), MXU_DTYPE)] * 3,
        compiler_params=_params(2),
    )(h, wg_t, wu_t)


def _ffn_bwd_act(dx, wd, gate, up, name, tm=512, tn=FF_TILE):
    m, k = dx.shape
    n = wd.shape[0]
    tm = min(tm, m)

    def body(dx_ref, wd_ref, gate_ref, up_ref, dgate_ref, dup_ref):
        dff = _dot_nt(dx_ref[...], wd_ref[...])
        gate = gate_ref[...].astype(F32)
        sg = _sigmoid(gate)
        dgate_ref[...] = (dff * up_ref[...].astype(F32) * sg * (1.0 + gate * (1.0 - sg))).astype(dgate_ref.dtype)
        dup_ref[...] = (dff * gate * sg).astype(dup_ref.dtype)

    ospec = pl.BlockSpec((tm, tn), lambda j, i: (i, j))
    return pl.pallas_call(
        body, name=name, grid=(n // tn, m // tm),
        in_specs=[pl.BlockSpec((tm, k), lambda j, i: (i, 0)), pl.BlockSpec((tn, k), lambda j, i: (j, 0)),
                  ospec, ospec],
        out_specs=[ospec, ospec],
        out_shape=[jax.ShapeDtypeStruct((m, n), MXU_DTYPE), jax.ShapeDtypeStruct((m, n), MXU_DTYPE)],
        compiler_params=_params(2),
    )(dx, wd, gate, up)


def _chunk_masks():
    r = lax.broadcasted_iota(jnp.int32, (SUPER, SUPER), 0)
    c = lax.broadcasted_iota(jnp.int32, (SUPER, SUPER), 1)
    same = jnp.right_shift(r, 6) == jnp.right_shift(c, 6)
    lower = jnp.logical_and(same, c <= r)
    upper = jnp.logical_and(same, c >= r)
    return same, lower, upper


def _head_block_mask():
    r = lax.broadcasted_iota(jnp.int32, (LANES, LANES), 0)
    c = lax.broadcasted_iota(jnp.int32, (LANES, LANES), 1)
    return jnp.where(jnp.right_shift(r, 6) == jnp.right_shift(c, 6), 1.0, 0.0)


def _lower_bound(lb_raw):
    return 1.0 / (1.0 + jnp.exp(lb_raw[1:2, :] - lb_raw[0:1, :]))


HGRN_UNROLL = 2
PER_SUPER = SUPER // CHUNK
CHUNK_ROWS = [slice(c * CHUNK, (c + 1) * CHUNK) for c in range(PER_SUPER)]


def _over_chunks(rows):
    return jnp.concatenate([jnp.broadcast_to(r, (CHUNK, LANES)) for r in rows], axis=0)


def _hgrn_gates(q, hf, lb, lower_b):
    sig = _sigmoid(hf)
    f = lb + (1.0 - lb) * sig
    k = 1.0 - f
    lf = jnp.log(f)
    b = _rsum_left(lower_b, lf, 2)
    ends = [b[cr.stop - 1:cr.stop, :] for cr in CHUNK_ROWS]
    eb = jnp.exp(b)
    enb = jnp.exp(-b)
    edb = jnp.exp(_over_chunks(ends) - b)
    decs = [jnp.exp(e) for e in ends]
    return sig, f, k, decs, eb, enb, edb, q * eb, k * enb, k * edb


def _hgrn_fwd(proj, lower_bounds, norm_g, n_seq, seq):
    t = n_seq * seq
    n_super = seq // SUPER
    n_chunks = seq // CHUNK

    def body(q_ref, f_ref, i_ref, g_ref, lb_ref, ng_ref, out_ref, opre_ref, st_ref):
        masks = _head_masks()
        _, lower, _ = _chunk_masks()
        lower_b = _ones_where(lower)
        bd = _head_block_mask()
        lb = _lower_bound(lb_ref[...])
        ng = ng_ref[...]

        def step(it, st):
            blocks = [HGRN_UNROLL * it + u for u in range(HGRN_UNROLL)]
            rows = [pl.ds(pl.multiple_of(sb * SUPER, SUPER), SUPER) for sb in blocks]
            vs = [i_ref[rw, :] for rw in rows]
            gates = [_hgrn_gates(q_ref[rw, :], f_ref[rw, :], lb, lower_b) for rw in rows]
            decs, qes, kes, kds = ([g[k] for g in gates] for k in (3, 7, 8, 9))
            scores = [[_dot_nt(qe * m, ke) for m in masks] for qe, ke in zip(qes, kes)]
            updates = [[_dot_tn(v[cr], kd[cr]) for cr in CHUNK_ROWS] for v, kd in zip(vs, kds)]
            states = [st]
            for dec_b, upd_b in zip(decs, updates):
                for dec, upd in zip(dec_b, upd_b):
                    states.append(states[-1] * dec + bd * upd)
            for u, sb in enumerate(blocks):
                for c in range(PER_SUPER):
                    st_ref[0, 0, sb * PER_SUPER + c] = states[u * PER_SUPER + c]
            intra = [[_dot(jnp.where(lower, p, 0.0), v) for p in sc] for sc, v in zip(scores, vs)]
            inter = [[_dot_nt(qe[cr], states[u * PER_SUPER + c]) for c, cr in enumerate(CHUNK_ROWS)]
                     for u, qe in enumerate(qes)]
            for rw, intra_b, inter_b in zip(rows, intra, inter):
                o = intra_b[0] * masks[0] + intra_b[1] * masks[1] + jnp.concatenate(inter_b, axis=0)
                opre_ref[rw, :] = o
                hg = g_ref[rw, :]
                on = o * _head_rstd(o, masks) * ng
                out_ref[rw, :] = (on * hg * _sigmoid(hg)).astype(out_ref.dtype)
            return states[-1]

        lax.fori_loop(0, n_super // HGRN_UNROLL, step, jnp.zeros((LANES, LANES), F32))

    def col(k):
        return pl.BlockSpec((seq, LANES), lambda p, b: (b, k * N_PAIRS + p))

    vec = lambda rows: pl.BlockSpec((rows, LANES), lambda p, b: (0, p))
    ospec = pl.BlockSpec((seq, LANES), lambda p, b: (b, p))
    return pl.pallas_call(
        body, name="hgrn_fwd", grid=(N_PAIRS, n_seq),
        in_specs=[col(0), col(1), col(2), col(3), vec(2), vec(1)],
        out_specs=[ospec, ospec,
                   pl.BlockSpec((1, 1, n_chunks, LANES, LANES), lambda p, b: (b, p, 0, 0, 0))],
        out_shape=[jax.ShapeDtypeStruct((t, 2 * GROUP), MXU_DTYPE), jax.ShapeDtypeStruct((t, GROUP), F32),
                   jax.ShapeDtypeStruct((n_seq, N_PAIRS, n_chunks, LANES, LANES), F32)],
        compiler_params=_params(2),
    )(proj, proj, proj, proj, lower_bounds, norm_g)


def _hgrn_bwd(proj, lower_bounds, norm_g, dmix, opre, states, n_seq, seq):
    t = n_seq * seq
    n_super = seq // SUPER
    n_chunks = seq // CHUNK
    per = SUPER // CHUNK

    def body(q_ref, f_ref, i_ref, g_ref, lb_ref, ng_ref, dm_ref, opre_ref, st_ref,
             dq_ref, df_ref, di_ref, dg_ref, dlb_ref, dng_ref):
        masks = _head_masks()
        _, lower, upper = _chunk_masks()
        lower_b, upper_b = _ones_where(lower), _ones_where(upper)
        bd = _head_block_mask()
        lb_raw = lb_ref[...]
        lb = _lower_bound(lb_raw)
        ng = ng_ref[...]

        @pl.when(pl.program_id(1) == 0)
        def _():
            dlb_ref[...] = jnp.zeros_like(dlb_ref)
            dng_ref[...] = jnp.zeros_like(dng_ref)

        def first_half(sb):
            rows = pl.ds(pl.multiple_of(sb * SUPER, SUPER), SUPER)
            q, hf, v, hg = q_ref[rows, :], f_ref[rows, :], i_ref[rows, :], g_ref[rows, :]
            sig, f, k, decs, eb, enb, edb, qe, ke, kd = _hgrn_gates(q, hf, lb, lower_b)
            o = opre_ref[rows, :]
            r = _head_rstd(o, masks)
            oh = o * r
            dm = dm_ref[rows, :]
            sg = _sigmoid(hg)
            dg_ref[rows, :] = (dm * oh * ng * sg * (1.0 + hg * (1.0 - sg))).astype(dg_ref.dtype)
            don = dm * hg * sg
            dng_ref[...] += jnp.sum(don * oh, axis=0, keepdims=True)
            doh = don * ng
            do = r * (doh - oh * _head_mean(doh * oh, masks))
            doms = [do * m for m in masks]
            qems = [qe * m for m in masks]
            scores = [_dot_nt(qem, ke) for qem in qems]
            dscores = [_dot_nt(dom, v) for dom in doms]
            prevs = [st_ref[0, 0, sb * per + c] for c in range(per)]
            dst_in = [_dot_tn(do[cr], qe[cr]) for cr in CHUNK_ROWS]
            dqe_i = [_dot(do[cr], prev) for cr, prev in zip(CHUNK_ROWS, prevs)]
            return dict(rows=rows, v=v, sig=sig, f=f, decs=decs, eb=eb, enb=enb, edb=edb, qe=qe, ke=ke, kd=kd,
                        doms=doms, qems=qems, scores=scores, dscores=dscores, prevs=prevs, dst_in=dst_in, dqe_i=dqe_i)

        def second_half(blk, dsts):
            v, qe, ke, kd = blk["v"], blk["qe"], blk["ke"], blk["kd"]
            ps = [jnp.where(lower, p, 0.0) for p in blk["scores"]]
            dps = [jnp.where(lower, dp, 0.0) for dp in blk["dscores"]]
            dqe_h = [_dot(dp, ke) for dp in dps]
            dke_h = [_dot_tn(dp, qem) for dp, qem in zip(dps, blk["qems"])]
            dv_h = [_dot_tn(p, dom) for p, dom in zip(ps, blk["doms"])]
            dus = [bd * d for d in dsts]
            dv_i = [_dot_nt(kd[cr], du) for cr, du in zip(CHUNK_ROWS, dus)]
            dkd_i = [_dot(v[cr], du) for cr, du in zip(CHUNK_ROWS, dus)]

            def finish():
                dqe = dqe_h[0] * masks[0] + dqe_h[1] * masks[1] + jnp.concatenate(blk["dqe_i"], axis=0)
                dke = dke_h[0] + dke_h[1]
                dv = dv_h[0] + dv_h[1] + jnp.concatenate(dv_i, axis=0)
                dkd = jnp.concatenate(dkd_i, axis=0)
                dk = dke * blk["enb"] + dkd * blk["edb"]
                db = dqe * qe - dke * ke - dkd * kd
                dkd_kd = dkd * kd
                dends = [jnp.sum(dkd_kd[cr], axis=0, keepdims=True)
                         + jnp.sum(dsts[c] * blk["prevs"][c], axis=0, keepdims=True) * blk["decs"][c]
                         for c, cr in enumerate(CHUNK_ROWS)]
                dlf = _rsum_left(upper_b, db, 2) + _over_chunks(dends)
                sig = blk["sig"]
                dfv = dlf / blk["f"] - dk
                rows = blk["rows"]
                dq_ref[rows, :] = (dqe * blk["eb"]).astype(dq_ref.dtype)
                di_ref[rows, :] = dv.astype(di_ref.dtype)
                df_ref[rows, :] = (dfv * (1.0 - lb) * sig * (1.0 - sig)).astype(df_ref.dtype)
                dlb = jnp.sum(dfv * (1.0 - sig), axis=0, keepdims=True)
                da0 = dlb * lb * (1.0 - lb)
                dlb_ref[0:1, :] += da0
                dlb_ref[1:2, :] -= da0

            return finish

        def step(it, dst):
            blocks = [first_half(n_super - 1 - HGRN_UNROLL * it - u) for u in range(HGRN_UNROLL)]
            all_dsts = []
            for blk in blocks:
                dsts = [None] * per
                for c in reversed(range(per)):
                    dsts[c] = dst
                    dst = bd * (dst * blk["decs"][c] + blk["dst_in"][c])
                all_dsts.append(dsts)
            for finish in [second_half(blk, dsts) for blk, dsts in zip(blocks, all_dsts)]:
                finish()
            return dst

        lax.fori_loop(0, n_super // HGRN_UNROLL, step, jnp.zeros((LANES, LANES), F32))

    def col(k):
        return pl.BlockSpec((seq, LANES), lambda p, b: (b, k * N_PAIRS + p))

    vec = lambda rows: pl.BlockSpec((rows, LANES), lambda p, b: (0, p))
    ospec = pl.BlockSpec((seq, LANES), lambda p, b: (b, p))
    piece = jax.ShapeDtypeStruct((t, GROUP), MXU_DTYPE)
    return pl.pallas_call(
        body, name="hgrn_bwd", grid=(N_PAIRS, n_seq),
        in_specs=[col(0), col(1), col(2), col(3), vec(2), vec(1), ospec, ospec,
                  pl.BlockSpec((1, 1, n_chunks, LANES, LANES), lambda p, b: (b, p, 0, 0, 0))],
        out_specs=[ospec, ospec, ospec, ospec, vec(2), vec(1)],
        out_shape=[piece, piece, piece, piece,
                   jax.ShapeDtypeStruct((2, GROUP), F32), jax.ShapeDtypeStruct((1, GROUP), F32)],
        compiler_params=_params(2),
    )(proj, proj, proj, proj, lower_bounds, norm_g, dmix, opre, states)


SB_SCALE = 1.0 / math.sqrt(D_HEAD)


SB_STEP = 2 * SB_BLOCK
QUERY_BLOCKS = (slice(0, SB_BLOCK), slice(SB_BLOCK, SB_STEP))


def _triangle(keep):
    row = lax.broadcasted_iota(jnp.int32, (SB_BLOCK, SB_BLOCK), 0)
    col = lax.broadcasted_iota(jnp.int32, (SB_BLOCK, SB_BLOCK), 1)
    return _ones_where(keep(row, col))


def _tile_masks():
    r = lax.broadcasted_iota(jnp.int32, (SB_BLOCK, SB_BLOCK), 0)
    c = lax.broadcasted_iota(jnp.int32, (SB_BLOCK, SB_BLOCK), 1)
    return r, c


def _sb_fwd(proj, norm_g, mixed, shards, n_seq, seq):
    t = n_seq * seq
    nq = seq // SB_STEP
    q0, k0, v0 = 0, N_PAIRS, 2 * N_PAIRS
    n_w = len(shards)
    n_steps = N_PAIRS * n_seq * nq
    tri = _triangle(lambda row, col: row >= col)

    def body(q_ref, k_ref, v_ref, ng_ref, tri_ref, mixed_in, *rest):
        del mixed_in
        shard_refs = rest[:n_w]
        out_ref, opre_ref, ctot_ref = rest[n_w:n_w + 3]
        gathered = rest[n_w + 3:2 * n_w + 3]
        send_sems, recv_sems, local_sems = rest[2 * n_w + 3:]
        i = pl.program_id(2)
        step = (pl.program_id(0) * n_seq + pl.program_id(1)) * nq + i
        plan = _GatherPlan(shard_refs, gathered, send_sems, recv_sems, local_sems)

        @pl.when(step == 0)
        def _():
            plan.start()

        @pl.when(step == (3 * n_steps) // 4)
        def _():
            plan.forward()

        masks = _head_masks()
        r, c = _tile_masks()
        strict = c < r
        suffix = tri_ref[...]
        qhs = [[(q_ref[blk, :] * SB_SCALE * m).astype(MXU_DTYPE) for m in masks] for blk in QUERY_BLOCKS]

        def tiles(js, work, carry):
            rows = [pl.ds(pl.multiple_of(j * SB_BLOCK, SB_BLOCK), SB_BLOCK) for j in js]
            ks = [k_ref[rw, :].astype(MXU_DTYPE) for rw in rows]
            vs = [v_ref[rw, :].astype(MXU_DTYPE) for rw in rows]
            zs = [[_dot_nt(qh, ks[ts]) for qh in qhs[qb]] for qb, ts, _ in work]
            ccs = [[_softplus(z) for z in zw] for zw in zs]
            ccs = [[jnp.where(strict, cc, 0.0) for cc in cw] if diag else cw for cw, (_, _, diag) in zip(ccs, work)]
            sums = [[jnp.dot(cc.astype(BF16), suffix, preferred_element_type=F32) for cc in cw] for cw in ccs]
            out = [list(per_block) for per_block in carry]
            for h in range(len(masks)):
                for w, (qb, ts, diag) in enumerate(work):
                    run, acc = out[qb][h]
                    a = jnp.exp(zs[w][h] - (sums[w][h] + run))
                    if diag:
                        a = jnp.where(strict, a, 0.0)
                    out[qb][h] = (run + sums[w][h][:, 0:1], acc + _dot(a, vs[ts]))
            return tuple(tuple(per_block) for per_block in out)

        start = ((jnp.zeros((SB_BLOCK, 1), F32), jnp.zeros((SB_BLOCK, LANES), F32)),) * 2
        carry = tiles([2 * i, 2 * i + 1], [(0, 0, True), (1, 1, True), (1, 0, False)], (start, start))
        both = [(0, 0, False), (0, 1, False), (1, 0, False), (1, 1, False)]
        carry = lax.fori_loop(0, i, lambda s, cy: tiles([2 * (i - s) - 1, 2 * (i - s) - 2], both, cy), carry)
        opre = jnp.concatenate([cb[0][1] * masks[0] + cb[1][1] * masks[1] for cb in carry], axis=0)
        ctot = jnp.concatenate([cb[0][0] * masks[0] + cb[1][0] * masks[1] for cb in carry], axis=0)
        opre_ref[...] = opre
        ctot_ref[...] = ctot
        out_ref[...] = (opre * _head_rstd(opre, masks) * ng_ref[...]).astype(out_ref.dtype)

        @pl.when(step == n_steps - 1)
        def _():
            plan.finish()

    qspec = pl.BlockSpec((SB_STEP, LANES), lambda p, b, i: (b * nq + i, q0 + p))
    ospec = pl.BlockSpec((SB_STEP, LANES), lambda p, b, i: (b * nq + i, p))
    hbm = pl.BlockSpec(memory_space=pltpu.HBM)
    outs = pl.pallas_call(
        body, name="sb_fwd", grid=(N_PAIRS, n_seq, nq),
        in_specs=[qspec,
                  pl.BlockSpec((seq, LANES), lambda p, b, i: (b, k0 + p)),
                  pl.BlockSpec((seq, LANES), lambda p, b, i: (b, v0 + p)),
                  pl.BlockSpec((1, LANES), lambda p, b, i: (0, p)),
                  pl.BlockSpec(tri.shape, lambda p, b, i: (0, 0)), hbm] + [hbm] * n_w,
        out_specs=[pl.BlockSpec((SB_STEP, LANES), lambda p, b, i: (b * nq + i, N_PAIRS + p)), ospec, ospec]
        + [hbm] * n_w,
        out_shape=[jax.ShapeDtypeStruct(mixed.shape, mixed.dtype), jax.ShapeDtypeStruct((t, GROUP), F32),
                   jax.ShapeDtypeStruct((t, GROUP), F32)]
        + [jax.ShapeDtypeStruct((N_DEV,) + s.shape, s.dtype) for s in shards],
        scratch_shapes=[pltpu.SemaphoreType.DMA((n_w * _GatherPlan.COPIES,)),
                        pltpu.SemaphoreType.DMA((n_w * _GatherPlan.COPIES,)), pltpu.SemaphoreType.DMA((n_w,))],
        input_output_aliases={5: 0},
        compiler_params=_params(3),
    )(proj, proj, proj, norm_g, tri, mixed, *shards)
    return outs[0], outs[1], outs[2], list(outs[3:])


def _sb_bwd(proj, norm_g, dmix, opre, ctot, grads, n_seq, seq):
    t = n_seq * seq
    nq = seq // SB_STEP
    q0, k0, v0 = 0, N_PAIRS, 2 * N_PAIRS
    n_w = len(grads)
    n_steps = N_PAIRS * n_seq * nq
    tri = _triangle(lambda row, col: row <= col)

    def body(q_ref, k_ref, v_ref, ng_ref, tri_ref, dm_ref, opre_ref, ctot_ref, *rest):
        grad_refs = rest[:n_w]
        dq_ref, dk_ref, dv_ref, dng_ref = rest[n_w:n_w + 4]
        lands = rest[n_w + 4:2 * n_w + 4]
        dk_acc, dv_acc, send_sems, recv_sems, local_sems = rest[2 * n_w + 4:]
        p_id, b_id, i = pl.program_id(0), pl.program_id(1), pl.program_id(2)
        step = (p_id * n_seq + b_id) * nq + i
        plan = _ScatterPlan(grad_refs, lands, send_sems, recv_sems, local_sems)

        @pl.when(step == 0)
        def _():
            plan.start()

        masks = _head_masks()
        r, c = _tile_masks()
        strict = c < r
        upto = tri_ref[...]

        def prefix(x):
            return jnp.dot(x.astype(BF16), upto, preferred_element_type=F32)

        @pl.when(i == 0)
        def _():
            dk_acc[...] = jnp.zeros_like(dk_acc)
            dv_acc[...] = jnp.zeros_like(dv_acc)

        @pl.when(jnp.logical_and(b_id == 0, i == 0))
        def _():
            dng_ref[...] = jnp.zeros_like(dng_ref)

        o = opre_ref[...]
        rs = _head_rstd(o, masks)
        oh = o * rs
        dm = dm_ref[...]
        dng_ref[...] += jnp.sum(dm * oh, axis=0, keepdims=True)
        doh = dm * ng_ref[...]
        do = rs * (doh - oh * _head_mean(doh * oh, masks))

        heads = range(len(masks))
        qs = [q_ref[blk, :] * SB_SCALE for blk in QUERY_BLOCKS]
        dos = [do[blk] for blk in QUERY_BLOCKS]
        qhs = [[(q * m).astype(MXU_DTYPE) for m in masks] for q in qs]
        doms = [[(d * m).astype(MXU_DTYPE) for m in masks] for d in dos]
        qhts = [[(q * m).T.astype(MXU_DTYPE) for m in masks] for q in qs]
        domts = [[(d * m).T.astype(MXU_DTYPE) for m in masks] for d in dos]
        totals = [[ctot_ref[blk, h * D_HEAD:h * D_HEAD + 1] for h in heads] for blk in QUERY_BLOCKS]

        def tiles(js, work, carry):
            rows = [pl.ds(pl.multiple_of(j * SB_BLOCK, SB_BLOCK), SB_BLOCK) for j in js]
            ks = [k_ref[rw, :].astype(MXU_DTYPE) for rw in rows]
            vs = [v_ref[rw, :].astype(MXU_DTYPE) for rw in rows]
            zs = [[_dot_nt(qhs[qb][h], ks[ts]) for h in heads] for qb, ts, _ in work]
            das = [[_dot_nt(doms[qb][h], vs[ts]) for h in heads] for qb, ts, _ in work]
            sps = [[_softplus(z) for z in zw] for zw in zs]
            lsigs = [[z - sp for z, sp in zip(zw, sw)] for zw, sw in zip(zs, sps)]
            sigs = [[jnp.exp(ls) for ls in lw] for lw in lsigs]
            ccs = [[jnp.where(strict, sp, 0.0) for sp in sw] if diag else sw for sw, (_, _, diag) in zip(sps, work)]
            pres = [[prefix(cc) for cc in cw] for cw in ccs]
            out = [list(per_block) for per_block in carry]
            for h in heads:
                for w, (qb, ts, diag) in enumerate(work):
                    pc, pdl, dq_h = out[qb][h]
                    a = jnp.exp(lsigs[w][h] + pres[w][h] - (totals[qb][h] - pc))
                    if diag:
                        a = jnp.where(strict, a, 0.0)
                    dl = a * das[w][h]
                    dv_acc[js[ts]] += _dot(domts[qb][h], a)
                    dpre = jnp.dot(dl.astype(BF16), upto, preferred_element_type=F32)
                    dz = dl - sigs[w][h] * (pdl + dpre)
                    if diag:
                        dz = jnp.where(strict, dz, 0.0)
                    dzb = dz.astype(MXU_DTYPE)
                    dk_acc[js[ts]] += _dot(qhts[qb][h], dzb)
                    out[qb][h] = (pc + pres[w][h][:, SB_BLOCK - 1:SB_BLOCK], pdl + dpre[:, SB_BLOCK - 1:SB_BLOCK],
                                  dq_h + _dot(dzb, ks[ts]))
            return tuple(tuple(per_block) for per_block in out)

        zero = jnp.zeros((SB_BLOCK, 1), F32)
        start = ((zero, zero, jnp.zeros((SB_BLOCK, LANES), F32)),) * 2
        both = [(0, 0, False), (0, 1, False), (1, 0, False), (1, 1, False)]
        carry = lax.fori_loop(0, i, lambda s, cy: tiles([2 * s, 2 * s + 1], both, cy), (start, start))
        carry = tiles([2 * i, 2 * i + 1], [(0, 0, True), (1, 0, False), (1, 1, True)], carry)
        dq = jnp.concatenate([cb[0][2] * masks[0] + cb[1][2] * masks[1] for cb in carry], axis=0)
        dq_ref[...] = (dq * SB_SCALE).astype(dq_ref.dtype)

        @pl.when(i == nq - 1)
        def _():
            for j in range(seq // SB_BLOCK):
                tile_rows = slice(j * SB_BLOCK, (j + 1) * SB_BLOCK)
                dk_ref[tile_rows, :] = dk_acc[j].T.astype(dk_ref.dtype)
                dv_ref[tile_rows, :] = dv_acc[j].T.astype(dv_ref.dtype)

        @pl.when(step == n_steps - 1)
        def _():
            plan.finish()

    qspec = pl.BlockSpec((SB_STEP, LANES), lambda p, b, i: (b * nq + i, q0 + p))
    ospec = pl.BlockSpec((SB_STEP, LANES), lambda p, b, i: (b * nq + i, p))
    dmspec = pl.BlockSpec((SB_STEP, LANES), lambda p, b, i: (b * nq + i, N_PAIRS + p))
    full = lambda k: pl.BlockSpec((seq, LANES), lambda p, b, i: (b, k + p))
    vec = pl.BlockSpec((1, LANES), lambda p, b, i: (0, p))
    hbm = pl.BlockSpec(memory_space=pltpu.HBM)
    piece = jax.ShapeDtypeStruct((t, GROUP), MXU_DTYPE)
    outs = pl.pallas_call(
        body, name="sb_bwd", grid=(N_PAIRS, n_seq, nq),
        in_specs=[qspec, full(k0), full(v0), vec, pl.BlockSpec(tri.shape, lambda p, b, i: (0, 0)), dmspec, ospec, ospec]
        + [hbm] * n_w,
        out_specs=[ospec, full(0), full(0), vec] + [hbm] * n_w,
        out_shape=[piece, piece, piece, jax.ShapeDtypeStruct((1, GROUP), F32)]
        + [jax.ShapeDtypeStruct(g.shape, g.dtype) for g in grads],
        scratch_shapes=[pltpu.VMEM((seq // SB_BLOCK, LANES, SB_BLOCK), F32),
                        pltpu.VMEM((seq // SB_BLOCK, LANES, SB_BLOCK), F32),
                        pltpu.SemaphoreType.DMA((n_w * (N_DEV - 1),)), pltpu.SemaphoreType.DMA((n_w * (N_DEV - 1),)),
                        pltpu.SemaphoreType.DMA((n_w,))],
        compiler_params=_params(3),
    )(proj, proj, proj, norm_g, tri, dmix, opre, ctot, *grads)
    return outs[0], outs[1], outs[2], outs[3], list(outs[4:])


def _mesh_place():
    x, y, c = lax.axis_index("x"), lax.axis_index("y"), lax.axis_index("c")
    return x, y, c


def _peer(x, y, c, k):
    px = lax.rem(x + ((k >> 2) & 1), 2)
    py = lax.rem(y + ((k >> 1) & 1), 2)
    pc = lax.rem(c + (k & 1), 2)
    return (px, py, pc), 4 * px + 2 * py + pc


def _remote(src, dst, send_sem, recv_sem, to):
    return pltpu.make_async_remote_copy(src_ref=src, dst_ref=dst, send_sem=send_sem, recv_sem=recv_sem,
                                        device_id=to, device_id_type=pl.DeviceIdType.MESH)


class _GatherPlan:
    COPIES = 7

    def __init__(self, shards, gathered, send_sems, recv_sems, local_sems):
        x, y, c = _mesh_place()
        self.c = c
        self.me = (x, y, c)
        self.sibling = (x, y, 1 - c)
        self.chips = [(1 - x, y), (x, 1 - y), (1 - x, 1 - y)]
        self.tensors = list(zip(shards, gathered))
        self.send_sems, self.recv_sems, self.local_sems = send_sems, recv_sems, local_sems

    @staticmethod
    def _index(place):
        return 4 * place[0] + 2 * place[1] + place[2]

    def _copy(self, w, k, block, to, own=False):
        shard, gathered = self.tensors[w]
        slot = gathered.at[self._index(block)]
        n = w * self.COPIES + k
        return _remote(shard if own else slot, slot, self.send_sems.at[n], self.recv_sems.at[n], to)

    def _local(self, w):
        shard, gathered = self.tensors[w]
        return pltpu.make_async_copy(shard, gathered.at[self._index(self.me)], self.local_sems.at[w])

    def _first(self, w):
        return [self._copy(w, 0, self.me, self.sibling, own=True)] + [
            self._copy(w, 1 + j, self.me, (*chip, self.c), own=True) for j, chip in enumerate(self.chips)]

    def _passed(self, w):
        return [self._copy(w, 4 + j, (*chip, self.c), self.sibling) for j, chip in enumerate(self.chips)]

    def start(self):
        for w in range(len(self.tensors)):
            self._local(w).start()
            for cp in self._first(w):
                cp.start()

    def forward(self):
        for w in range(len(self.tensors)):
            passed = self._passed(w)
            for j, chip in enumerate(self.chips):
                self._copy(w, 1 + j, (*chip, self.c), self.me).wait_recv()
                passed[j].start()

    def finish(self):
        for w in range(len(self.tensors)):
            self._copy(w, 0, self.sibling, self.me).wait_recv()
            for j, chip in enumerate(self.chips):
                self._copy(w, 4 + j, (*chip, 1 - self.c), self.me).wait_recv()
            for cp in self._first(w) + self._passed(w):
                cp.wait_send()
            self._local(w).wait()


class _ScatterPlan:
    def __init__(self, grads, lands, send_sems, recv_sems, local_sems):
        self.place = _mesh_place()
        x, y, c = self.place
        self.me = 4 * x + 2 * y + c
        self.tensors = list(zip(grads, lands))
        self.send_sems, self.recv_sems, self.local_sems = send_sems, recv_sems, local_sems

    def _copies(self, w):
        grad, land = self.tensors[w]
        out = []
        for k in range(1, N_DEV):
            peer, pidx = _peer(*self.place, k)
            n = w * (N_DEV - 1) + k - 1
            sems = (self.send_sems.at[n], self.recv_sems.at[n], peer)
            out.append((_remote(grad.at[pidx], land.at[self.me], *sems), _remote(grad.at[pidx], land.at[pidx], *sems)))
        return out

    def _local(self, w):
        grad, land = self.tensors[w]
        return pltpu.make_async_copy(grad.at[self.me], land.at[self.me], self.local_sems.at[w])

    def start(self):
        for w in range(len(self.tensors)):
            self._local(w).start()
            for send, _ in self._copies(w):
                send.start()

    def finish(self):
        for w in range(len(self.tensors)):
            copies = self._copies(w)
            for _, arrival in copies:
                arrival.wait_recv()
            for send, _ in copies:
                send.wait_send()
            self._local(w).wait()


def _cast_shards(shards):
    def body(*refs):
        n = len(refs) // 2
        for src, dst in zip(refs[:n], refs[n:]):
            dst[...] = src[...].astype(dst.dtype)

    vmem = pl.BlockSpec(memory_space=pltpu.VMEM)
    return pl.pallas_call(
        body, name="cast_shards", in_specs=[vmem] * len(shards), out_specs=[vmem] * len(shards),
        out_shape=[jax.ShapeDtypeStruct(s.shape, BF16) for s in shards],
        compiler_params=pltpu.CompilerParams(vmem_limit_bytes=VMEM_LIMIT),
    )(*shards)


def _gather_w_in(shard):
    rows, cols = shard.shape

    def body(w_ref, out_ref, send_sems, recv_sems, local_sems):
        plan = _GatherPlan([w_ref], [out_ref], send_sems, recv_sems, local_sems)
        plan.start()
        plan.forward()
        plan.finish()

    vmem = pl.BlockSpec(memory_space=pltpu.VMEM)
    return pl.pallas_call(
        body, name="gather_w_in", in_specs=[vmem], out_specs=vmem,
        out_shape=jax.ShapeDtypeStruct((N_DEV, rows, cols), shard.dtype),
        scratch_shapes=[pltpu.SemaphoreType.DMA((_GatherPlan.COPIES,)), pltpu.SemaphoreType.DMA((_GatherPlan.COPIES,)),
                        pltpu.SemaphoreType.DMA((1,))],
        compiler_params=pltpu.CompilerParams(vmem_limit_bytes=VMEM_LIMIT),
    )(shard)


def _dw_rows(pieces, b, name, tn=512, tt=512):
    t, n = b.shape
    widths = [p.shape[1] for p in pieces]
    rows = sum(widths)
    tt = min(tt, t)
    steps = t // tt
    n_p = len(pieces)

    def body(*refs):
        piece_refs, b_ref, o_ref, acc = refs[:n_p], refs[n_p], refs[n_p + 1], refs[n_p + 2]
        s = pl.program_id(1)

        @pl.when(s == 0)
        def _():
            acc[...] = jnp.zeros_like(acc)

        bv = b_ref[...]
        off = 0
        for p_ref, width in zip(piece_refs, widths):
            acc[off:off + width, :] += _dot_tn(p_ref[...], bv)
            off += width

        @pl.when(s == steps - 1)
        def _():
            o_ref[...] = acc[...].astype(o_ref.dtype)

    return pl.pallas_call(
        body, name=name, grid=(n // tn, steps),
        in_specs=[pl.BlockSpec((tt, width), lambda j, s: (s, 0)) for width in widths]
        + [pl.BlockSpec((tt, tn), lambda j, s: (s, j))],
        out_specs=pl.BlockSpec((rows, tn), lambda j, s: (0, j)),
        out_shape=jax.ShapeDtypeStruct((rows, n), BF16),
        scratch_shapes=[pltpu.VMEM((rows, tn), F32)],
        compiler_params=_params(2),
    )(*pieces, b)


def _dh_norm_bwd(pairs, x, g, res, name, tm=512, after=()):
    m, d = x.shape
    tm = min(tm, m)
    n_p = len(pairs)
    n_steps = m // tm

    def body(*refs):
        x_ref, g_ref, res_ref = refs[2 * n_p:2 * n_p + 3]
        dx_ref, dg_ref = refs[2 * n_p + 3 + len(after):]

        @pl.when(pl.program_id(0) == 0)
        def _():
            dg_ref[...] = jnp.zeros_like(dg_ref)

        dh = None
        for q in range(n_p):
            part = _dot(refs[2 * q][...], refs[2 * q + 1][...])
            dh = part if dh is None else dh + part
        xv = x_ref[...]
        r = lax.rsqrt(jnp.mean(xv * xv, axis=-1, keepdims=True) + EPS)
        xh = xv * r
        dxh = dh * g_ref[...]
        dx_ref[...] = res_ref[...] + r * (dxh - xh * jnp.mean(dxh * xh, axis=-1, keepdims=True))
        dg_ref[...] += jnp.sum(dh * xh, axis=0, keepdims=True)

    in_specs, args = [], []
    for a, w, r in pairs:
        k = a.shape[1]
        in_specs += [pl.BlockSpec((tm, k), lambda i: (i, 0)),
                     pl.BlockSpec((k, d), functools.partial(lambda i, r: (r, 0), r=r or 0),
                                  pipeline_mode=pl.Buffered(1))]
        args += [a, w]
    row = pl.BlockSpec((tm, d), lambda i: (i, 0))
    vec = pl.BlockSpec((1, d), lambda i: (0, 0))
    return pl.pallas_call(
        body, name=name, grid=(n_steps,),
        in_specs=in_specs + [row, vec, row] + [pl.BlockSpec(memory_space=pl.ANY)] * len(after),
        out_specs=[row, vec],
        out_shape=[jax.ShapeDtypeStruct((m, d), F32), jax.ShapeDtypeStruct((1, d), F32)],
        compiler_params=_params(1),
    )(*args, x, g, res, *after)


def _exchange_start(grad, name):
    def body(g_ref, land_ref, send_sem, recv_sem, local_sem, g_thru, land_thru, token):
        del g_thru, land_thru
        x, y, c = _mesh_place()
        me = 4 * x + 2 * y + c
        pltpu.make_async_copy(g_ref.at[me], land_ref.at[me], local_sem).start()
        for k in range(1, N_DEV):
            peer, pidx = _peer(x, y, c, k)
            _remote(g_ref.at[pidx], land_ref.at[me], send_sem, recv_sem, peer).start()
        token[...] = jnp.zeros_like(token)

    hbm = pl.BlockSpec(memory_space=pltpu.HBM)
    sem = pl.BlockSpec(memory_space=pltpu.SEMAPHORE)
    buf = pltpu.HBM(grad.shape, grad.dtype)
    return pl.pallas_call(
        body, name=name,
        out_shape=(pltpu.SemaphoreType.DMA(()), pltpu.SemaphoreType.DMA(()), pltpu.SemaphoreType.DMA(()), buf, buf,
                   jax.ShapeDtypeStruct((8, LANES), F32)),
        in_specs=(hbm, hbm), out_specs=(sem, sem, sem, hbm, hbm, pl.BlockSpec(memory_space=pltpu.VMEM)),
        input_output_aliases={0: 3, 1: 4},
        compiler_params=pltpu.CompilerParams(has_side_effects=pltpu.SideEffectType.DATAFLOW_SIDE_EFFECTING),
    )(pltpu.with_memory_space_constraint(grad, pltpu.HBM),
      pltpu.with_memory_space_constraint(lax.empty(grad.shape, grad.dtype), pltpu.HBM))


def _exchange_wait(send_sem, recv_sem, local_sem, grad, land, after, name):
    def body(g_ref, land_ref, send_sem, recv_sem, local_sem, *rest):
        x, y, c = _mesh_place()
        peer, _ = _peer(x, y, c, 1)
        others = pl.ds(0, N_DEV - 1)
        seven = _remote(g_ref.at[others], land_ref.at[others], send_sem, recv_sem, peer)
        seven.wait_send()
        seven.wait_recv()
        pltpu.make_async_copy(g_ref.at[0], land_ref.at[0], local_sem).wait()

    hbm = pl.BlockSpec(memory_space=pltpu.HBM)
    sem = pl.BlockSpec(memory_space=pltpu.SEMAPHORE)
    buf = pltpu.HBM(grad.shape, grad.dtype)
    return pl.pallas_call(
        body, name=name, out_shape=(buf, buf),
        in_specs=(hbm, hbm, sem, sem, sem) + (pl.BlockSpec(memory_space=pl.ANY),) * len(after), out_specs=(hbm, hbm),
        input_output_aliases={0: 0, 1: 1},
        compiler_params=pltpu.CompilerParams(has_side_effects=pltpu.SideEffectType.DATAFLOW_SIDE_EFFECTING),
    )(grad, land, send_sem, recv_sem, local_sem, *after)[1]


SMALL_LAYOUT = ((0, 0, 0, 0, D_MODEL), (1, 0, 1, 0, GROUP), (1, 1, 1, GROUP, GROUP), (2, 0, 2, 0, GROUP),
                (3, 0, 2, GROUP, GROUP), (4, 0, 3, 0, D_MODEL), (5, 0, 4, 0, D_MODEL))
LOSS_ROW = 5
N_SMALL = 6


def _small_step(grads, loss_part, ws, ms, vs):
    def body(*refs):
        g_in, loss_in = refs[:N_SMALL], refs[N_SMALL]
        params = [refs[1 + (q + 1) * N_SMALL:1 + (q + 2) * N_SMALL] for q in range(3)]
        o0 = 1 + 4 * N_SMALL
        outs = [refs[o0 + q * N_SMALL:o0 + (q + 1) * N_SMALL] for q in range(4)]
        loss_out = refs[o0 + 4 * N_SMALL]
        pack, land, wp, mp, vp, send_sems, recv_sems = refs[o0 + 4 * N_SMALL + 1:]

        def place(dst, srcs):
            dst[...] = jnp.zeros_like(dst)
            for p, sr, dr, dc, width in SMALL_LAYOUT:
                dst[dr:dr + 1, dc:dc + width] = srcs[p][sr:sr + 1, :]

        place(pack, g_in)
        pack[LOSS_ROW:LOSS_ROW + 1, 0:LANES] = loss_in[...]
        for dst, srcs in zip((wp, mp, vp), params):
            place(dst, srcs)

        x, y, c = _mesh_place()
        me = 4 * x + 2 * y + c
        land[me] = pack[...]
        sends = []
        for k in range(1, N_DEV):
            peer, _ = _peer(x, y, c, k)
            cp = _remote(pack, land.at[me], send_sems.at[k - 1], recv_sems.at[k - 1], peer)
            cp.start()
            sends.append(cp)
        for k in range(1, N_DEV):
            peer, pidx = _peer(x, y, c, k)
            _remote(pack, land.at[pidx], send_sems.at[k - 1], recv_sems.at[k - 1], peer).wait_recv()
        for cp in sends:
            cp.wait_send()

        g = land[0]
        for d in range(1, N_DEV):
            g = g + land[d]
        delta, nm, nv = _adam(wp[...], g, mp[...], vp[...])
        for val, out in zip((g, delta, nm, nv), outs):
            for p, sr, dr, dc, width in SMALL_LAYOUT:
                out[p][sr:sr + 1, :] = val[dr:dr + 1, dc:dc + width]
        loss_out[...] = g[LOSS_ROW:LOSS_ROW + 1, 0:LANES]

    vmem = pl.BlockSpec(memory_space=pltpu.VMEM)
    n_in = 1 + 4 * N_SMALL
    shapes = [jax.ShapeDtypeStruct(w.shape, F32) for w in ws]
    packed = pltpu.VMEM((SMALL_ROWS, D_MODEL), F32)
    outs = pl.pallas_call(
        body, name="small_step", in_specs=[vmem] * n_in, out_specs=[vmem] * (4 * N_SMALL + 1),
        out_shape=shapes * 4 + [jax.ShapeDtypeStruct((1, LANES), F32)],
        scratch_shapes=[packed, pltpu.VMEM((N_DEV, SMALL_ROWS, D_MODEL), F32), packed, packed, packed,
                        pltpu.SemaphoreType.DMA((N_DEV - 1,)), pltpu.SemaphoreType.DMA((N_DEV - 1,))],
    )(*grads, loss_part, *ws, *ms, *vs)
    return [outs[q * N_SMALL:(q + 1) * N_SMALL] for q in range(4)], outs[4 * N_SMALL]


def _adam(w, g, m, v):
    m = ADAM_B1 * m + (1.0 - ADAM_B1) * g
    v = ADAM_B2 * v + (1.0 - ADAM_B2) * (g * g)
    m_hat = m / (1.0 - ADAM_B1 ** ADAM_STEP)
    v_hat = v / (1.0 - ADAM_B2 ** ADAM_STEP)
    delta = -ADAM_LR * (m_hat / (jnp.sqrt(v_hat) + ADAM_EPS) + ADAM_WD * w)
    return delta, m, v


def _reduce_adamw(land, w, m, v, name, tr, after=()):
    _, rows, width = land.shape

    def body(land_ref, w_ref, m_ref, v_ref, *rest):
        g_ref, d_ref, nm_ref, nv_ref = rest[len(after):]
        g = land_ref[0].astype(F32)
        for d in range(1, N_DEV):
            g = g + land_ref[d].astype(F32)
        delta, nm, nv = _adam(w_ref[...], g, m_ref[...], v_ref[...])
        g_ref[...] = g
        d_ref[...] = delta
        nm_ref[...] = nm
        nv_ref[...] = nv

    row = pl.BlockSpec((tr, width), lambda i: (i, 0))
    out = jax.ShapeDtypeStruct((rows, width), F32)
    return pl.pallas_call(
        body, name=name, grid=(rows // tr,),
        in_specs=[pl.BlockSpec((N_DEV, tr, width), lambda i: (0, i, 0)), row, row, row]
        + [pl.BlockSpec(memory_space=pl.ANY)] * len(after),
        out_specs=[row, row, row, row], out_shape=[out, out, out, out],
        compiler_params=_params(1),
    )(land, w, m, v, *after)


def kernel(x, mix_norm_g, w_in, lower_bounds, hgrn_norm_g, sb_norm_g, w_out, ffn_norm_g, w_gate, w_up, w_down, final_norm_g, loss_target, m_mix_norm_g, m_w_in, m_lower_bounds, m_hgrn_norm_g, m_sb_norm_g, m_w_out, m_ffn_norm_g, m_w_gate, m_w_up, m_w_down, m_final_norm_g, v_mix_norm_g, v_w_in, v_lower_bounds, v_hgrn_norm_g, v_sb_norm_g, v_w_out, v_ffn_norm_g, v_w_gate, v_w_up, v_w_down, v_final_norm_g):
    n_seq, seq, d = x.shape
    t = n_seq * seq
    x2d = x.reshape(t, d)
    tgt = loss_target.reshape(t, d)
    final_g = final_norm_g.reshape(1, d)
    col_sharded = (True, False, True, True, False)

    def as_rows(ws):
        return [w[0].T if tr else w[0] for w, tr in zip(ws, col_sharded)]

    big_w = as_rows([w_in, w_out, w_gate, w_up, w_down])
    big_m = as_rows([m_w_in, m_w_out, m_w_gate, m_w_up, m_w_down])
    big_v = as_rows([v_w_in, v_w_out, v_w_gate, v_w_up, v_w_down])

    sh_in, sh_out, sh_gate, sh_up, sh_down = _cast_shards(big_w)
    wt_in = _gather_w_in(sh_in).reshape(IN_COLS, d)
    h1 = _rmsnorm_fwd(x2d, mix_norm_g, "norm_mix")
    proj_h = _mm_nt(h1, wt_in, "proj_hgrn", rows=(0, 4 * GROUP), tm=1024, tk=1024)
    proj_s = _mm_nt(h1, wt_in, "proj_sb", rows=(4 * GROUP, 3 * GROUP), out_dtype=MXU_DTYPE, tm=1024, tk=512)
    mixed, oa_pre, states = _hgrn_fwd(proj_h, lower_bounds, hgrn_norm_g, n_seq, seq)
    mixed, ob_pre, ctot, gathered = _sb_fwd(proj_s, sb_norm_g, mixed, [sh_out, sh_gate, sh_up, sh_down], n_seq, seq)
    wf_out = gathered[0].reshape(d, d)
    wt_gate = gathered[1].reshape(D_FF, d)
    wt_up = gathered[2].reshape(D_FF, d)
    wf_down = gathered[3].reshape(D_FF, d)
    x1, h2 = _mix_out_norm(mixed, wf_out, x2d, ffn_norm_g, "mix_out")
    gate, up, ff = _ffn_up(h2, wt_gate, wt_up, "ffn_up")
    dx2, dx2m, d_final_g, loss_part = _ffn_down_loss(ff, wf_down, x1, tgt, final_g, "ffn_down_loss")

    dgate, dup = _ffn_bwd_act(dx2m, wf_down, gate, up, "ffn_bwd_act")
    dw_down = _mm_tn(ff, dx2m, "dw_down", tk=1408, tn=1024).reshape(N_DEV, D_FF // N_DEV, d)
    dw_gate = _mm_tn(dgate, h2, "dw_gate", tk=1408, tn=1024).reshape(N_DEV, D_FF // N_DEV, d)
    dw_up = _mm_tn(dup, h2, "dw_up", tk=1408, tn=1024).reshape(N_DEV, D_FF // N_DEV, d)
    dx1, d_ffn_g = _dh_norm_bwd([(dgate, wt_gate, None), (dup, wt_up, None)], x1, ffn_norm_g, dx2, "dh_ffn")
    dmix = _mm_nt(dx1, wf_out, "dmix")
    dw_out = _mm_tn(mixed, dx1, "dw_out", tk=512, tn=1024).reshape(N_DEV, d // N_DEV, d)
    dsq, dsk, dsv, d_sb_g, lands = _sb_bwd(proj_s, sb_norm_g, dmix, ob_pre, ctot, [dw_out, dw_gate, dw_up, dw_down],
                                            n_seq, seq)
    dhq, dhf, dhi, dhg, d_lb, d_hgrn_g = _hgrn_bwd(proj_h, lower_bounds, hgrn_norm_g, dmix, oa_pre, states, n_seq,
                                                   seq)
    dproj = [dhq, dhf, dhi, dhg, dsq, dsk, dsv]
    dw_in = _dw_rows(dproj, h1, "dw_in").reshape(N_DEV, IN_COLS // N_DEV, d)
    send_sem, recv_sem, local_sem, dw_in, land_in, token = _exchange_start(dw_in, "dw_in_send")
    dx, d_mix_g = _dh_norm_bwd([(piece, wt_in, k) for k, piece in enumerate(dproj)], x2d, mix_norm_g, dx1, "dh_mix",
                               after=(token,))

    tiles = {"in": 224, "out": 128, "gate": 176, "up": 176, "down": 176}
    keys = list(tiles)
    rest = [_reduce_adamw(land, w, m, v, "adamw_" + key, tr=tiles[key], after=(token,))
            for key, land, w, m, v in zip(keys[1:], lands, big_w[1:], big_m[1:], big_v[1:])]
    land_in = _exchange_wait(send_sem, recv_sem, local_sem, dw_in, land_in, [dx] + [res[0] for res in rest],
                             "dw_in_await")
    big = [_reduce_adamw(land_in, big_w[0], big_m[0], big_v[0], "adamw_in", tr=tiles["in"])] + rest
    big = [[r.T if tr else r for r in res] for res, tr in zip(big, col_sharded)]
    small, loss_row = _small_step(
        [d_mix_g, d_lb, d_hgrn_g, d_sb_g, d_ffn_g, d_final_g], loss_part,
        [mix_norm_g, lower_bounds, hgrn_norm_g, sb_norm_g, ffn_norm_g, final_g],
        [m_mix_norm_g, m_lower_bounds, m_hgrn_norm_g, m_sb_norm_g, m_ffn_norm_g, m_final_norm_g.reshape(1, d)],
        [v_mix_norm_g, v_lower_bounds, v_hgrn_norm_g, v_sb_norm_g, v_ffn_norm_g, v_final_norm_g.reshape(1, d)])

    outs = [loss_row[0, 0], dx.reshape(n_seq, seq, d)]
    for q in range(4):
        b_in, b_out, b_gate, b_up, b_down = [res[q][None] for res in big]
        s_mix, s_lb, s_hgrn, s_sb, s_ffn, s_final = small[q]
        outs += [s_mix, b_in, s_lb, s_hgrn, s_sb, b_out, s_ffn, b_gate, b_up, b_down, s_final.reshape(d)]
    return tuple(outs)
```

```python
import functools
import math

import jax
import jax.numpy as jnp
from jax import lax
from jax.experimental import pallas as pl
from jax.experimental.pallas import tpu as pltpu

F32 = jnp.float32
BF16 = jnp.bfloat16
MXU_DTYPE = BF16

EPS = 1e-6
D_MODEL = 1024
N_HEADS = 8
D_HEAD = 64
GROUP = N_HEADS * D_HEAD
IN_COLS = 7 * GROUP
D_FF = 2816
CHUNK = 64
LANES = 128
N_PAIRS = GROUP // LANES
SUPER = 256
SB_BLOCK = 256
N_DEV = 8

ADAM_LR = 0.001
ADAM_B1 = 0.9
ADAM_B2 = 0.999
ADAM_EPS = 1e-08
ADAM_WD = 0.01
ADAM_STEP = 10

SMALL_ROWS = 8
FF_TILE = D_FF // 2

VMEM_LIMIT = 48 * 1024 * 1024


def _params(n_axes, vmem=VMEM_LIMIT):
    return pltpu.CompilerParams(dimension_semantics=("arbitrary",) * n_axes, vmem_limit_bytes=vmem)


def _dot(a, b):
    return jnp.dot(a.astype(MXU_DTYPE), b.astype(MXU_DTYPE), preferred_element_type=F32)


def _dot_nt(a, b):
    return lax.dot_general(a.astype(MXU_DTYPE), b.astype(MXU_DTYPE), (((1,), (1,)), ((), ())),
                           preferred_element_type=F32)


def _dot_tn(a, b):
    return lax.dot_general(a.astype(MXU_DTYPE), b.astype(MXU_DTYPE), (((0,), (0,)), ((), ())),
                           preferred_element_type=F32)


def _split(x, parts):
    out, r = [], x
    for _ in range(parts):
        h = r.astype(BF16)
        out.append(h)
        r = r - h.astype(F32)
    return out


def _rsum_left(u, x, parts):
    acc = None
    for h in _split(x, parts):
        d = jnp.dot(u, h, preferred_element_type=F32)
        acc = d if acc is None else acc + d
    return acc


def _ones_where(mask):
    return jnp.where(mask, 1.0, 0.0).astype(BF16)


def _sigmoid(x):
    return 1.0 / (1.0 + jnp.exp(-x))


def _softplus(x):
    return jnp.maximum(x, 0.0) + jnp.log(1.0 + jnp.exp(-jnp.abs(x)))


def _head_masks():
    lane = lax.broadcasted_iota(jnp.int32, (1, LANES), 1)
    return [jnp.where(lane < D_HEAD, 1.0, 0.0), jnp.where(lane >= D_HEAD, 1.0, 0.0)]


def _head_rstd(o, masks):
    sq = o * o
    r = None
    for m in masks:
        ms = jnp.sum(sq * m, axis=1, keepdims=True) * (1.0 / D_HEAD)
        t = lax.rsqrt(ms + EPS) * m
        r = t if r is None else r + t
    return r


def _head_mean(t, masks):
    out = None
    for m in masks:
        v = jnp.sum(t * m, axis=1, keepdims=True) * (1.0 / D_HEAD) * m
        out = v if out is None else out + v
    return out


def _mix_out_norm(a, w, res, g, name, tm=512):
    m, k = a.shape
    d = w.shape[1]
    tm = min(tm, m)

    def body(a_ref, w_ref, res_ref, g_ref, x_ref, h_ref):
        xv = res_ref[...] + _dot(a_ref[...], w_ref[...])
        x_ref[...] = xv
        r = lax.rsqrt(jnp.mean(xv * xv, axis=-1, keepdims=True) + EPS)
        h_ref[...] = (xv * r * g_ref[...]).astype(h_ref.dtype)

    row = pl.BlockSpec((tm, d), lambda i: (i, 0))
    return pl.pallas_call(
        body, name=name, grid=(m // tm,),
        in_specs=[pl.BlockSpec((tm, k), lambda i: (i, 0)), pl.BlockSpec((k, d), lambda i: (0, 0)), row,
                  pl.BlockSpec((1, d), lambda i: (0, 0))],
        out_specs=[row, row],
        out_shape=[jax.ShapeDtypeStruct((m, d), F32), jax.ShapeDtypeStruct((m, d), MXU_DTYPE)],
        compiler_params=_params(1),
    )(a, w, res, g)


def _ffn_down_loss(a, w, res, target, g, name, tm=512):
    m, k = a.shape
    d = w.shape[1]
    tm = min(tm, m)

    def body(a_ref, w_ref, res_ref, t_ref, g_ref, dx_ref, dxm_ref, dg_ref, loss_ref):
        xv = res_ref[...] + _dot(a_ref[...], w_ref[...])
        gv = g_ref[...]
        r = lax.rsqrt(jnp.mean(xv * xv, axis=-1, keepdims=True) + EPS)
        xh = xv * r
        e = xh * gv - t_ref[...]
        dy = e * (1.0 / d)
        dxh = dy * gv
        dxv = r * (dxh - xh * jnp.mean(dxh * xh, axis=-1, keepdims=True))
        dx_ref[...] = dxv
        dxm_ref[...] = dxv.astype(dxm_ref.dtype)

        @pl.when(pl.program_id(0) == 0)
        def _():
            dg_ref[...] = jnp.zeros_like(dg_ref)
            loss_ref[...] = jnp.zeros_like(loss_ref)

        dg_ref[...] += jnp.sum(dy * xh, axis=0, keepdims=True)
        part = 0.5 * jnp.sum(jnp.mean(e * e, axis=-1, keepdims=True), axis=0, keepdims=True)
        loss_ref[...] += jnp.broadcast_to(part, loss_ref.shape)

    row = pl.BlockSpec((tm, d), lambda i: (i, 0))
    vec = pl.BlockSpec((1, d), lambda i: (0, 0))
    return pl.pallas_call(
        body, name=name, grid=(m // tm,),
        in_specs=[pl.BlockSpec((tm, k), lambda i: (i, 0)), pl.BlockSpec((k, d), lambda i: (0, 0)), row, row, vec],
        out_specs=[row, row, vec, pl.BlockSpec((1, LANES), lambda i: (0, 0))],
        out_shape=[jax.ShapeDtypeStruct((m, d), F32), jax.ShapeDtypeStruct((m, d), MXU_DTYPE),
                   jax.ShapeDtypeStruct((1, d), F32), jax.ShapeDtypeStruct((1, LANES), F32)],
        compiler_params=_params(1),
    )(a, w, res, target, g)


def _mm_nt(a, w, name, rows=None, out_dtype=F32, tm=512, tk=512):
    m, n = a.shape
    row0, k = rows or (0, w.shape[0])
    tm, tk = min(tm, m), min(tk, k)
    first = row0 // tk

    def body(a_ref, w_ref, o_ref):
        o_ref[...] = _dot_nt(a_ref[...], w_ref[...]).astype(o_ref.dtype)

    return pl.pallas_call(
        body, name=name, grid=(k // tk, m // tm),
        in_specs=[pl.BlockSpec((tm, n), lambda j, i: (i, 0)), pl.BlockSpec((tk, n), lambda j, i: (first + j, 0))],
        out_specs=pl.BlockSpec((tm, tk), lambda j, i: (i, j)),
        out_shape=jax.ShapeDtypeStruct((m, k), out_dtype),
        compiler_params=_params(2),
    )(a, w)


def _mm_tn(a, b, name, tk, tn, tt=1024, out_dtype=BF16):
    t, k = a.shape
    n = b.shape[1]
    tt = min(tt, t)
    steps = t // tt

    def body(a_ref, b_ref, o_ref, acc):
        s = pl.program_id(2)

        @pl.when(s == 0)
        def _():
            acc[...] = jnp.zeros_like(acc)

        acc[...] += _dot_tn(a_ref[...], b_ref[...])

        @pl.when(s == steps - 1)
        def _():
            o_ref[...] = acc[...].astype(o_ref.dtype)

    return pl.pallas_call(
        body, name=name, grid=(k // tk, n // tn, steps),
        in_specs=[pl.BlockSpec((tt, tk), lambda i, j, s: (s, i)), pl.BlockSpec((tt, tn), lambda i, j, s: (s, j))],
        out_specs=pl.BlockSpec((tk, tn), lambda i, j, s: (i, j)),
        out_shape=jax.ShapeDtypeStruct((k, n), out_dtype),
        scratch_shapes=[pltpu.VMEM((tk, tn), F32)],
        compiler_params=_params(3),
    )(a, b)


def _ffn_up(h, wg_t, wu_t, name, tm=512, tn=FF_TILE):
    m, k = h.shape
    n = wg_t.shape[0]
    tm = min(tm, m)

    def body(h_ref, wg_ref, wu_ref, gate_ref, up_ref, ff_ref):
        hv = h_ref[...]
        gate = _dot_nt(hv, wg_ref[...])
        up = _dot_nt(hv, wu_ref[...])
        gate_ref[...] = gate.astype(gate_ref.dtype)
        up_ref[...] = up.astype(up_ref.dtype)
        ff_ref[...] = (gate * _sigmoid(gate) * up).astype(ff_ref.dtype)

    wspec = pl.BlockSpec((tn, k), lambda j, i: (j, 0))
    ospec = pl.BlockSpec((tm, tn), lambda j, i: (i, j))
    return pl.pallas_call(
        body, name=name, grid=(n // tn, m // tm),
        in_specs=[pl.BlockSpec((tm, k), lambda j, i: (i, 0)), wspec, wspec],
        out_specs=[ospec, ospec, ospec],
        out_shape=[jax.ShapeDtypeStruct((m, n), MXU_DTYPE)] * 3,
        compiler_params=_params(2),
    )(h, wg_t, wu_t)


def _ffn_bwd_act(dx, wd, gate, up, name, tm=512, tn=FF_TILE):
    m, k = dx.shape
    n = wd.shape[0]
    tm = min(tm, m)

    def body(dx_ref, wd_ref, gate_ref, up_ref, dgate_ref, dup_ref):
        dff = _dot_nt(dx_ref[...], wd_ref[...])
        gate = gate_ref[...].astype(F32)
        sg = _sigmoid(gate)
        dgate_ref[...] = (dff * up_ref[...].astype(F32) * sg * (1.0 + gate * (1.0 - sg))).astype(dgate_ref.dtype)
        dup_ref[...] = (dff * gate * sg).astype(dup_ref.dtype)

    ospec = pl.BlockSpec((tm, tn), lambda j, i: (i, j))
    return pl.pallas_call(
        body, name=name, grid=(n // tn, m // tm),
        in_specs=[pl.BlockSpec((tm, k), lambda j, i: (i, 0)), pl.BlockSpec((tn, k), lambda j, i: (j, 0)),
                  ospec, ospec],
        out_specs=[ospec, ospec],
        out_shape=[jax.ShapeDtypeStruct((m, n), MXU_DTYPE), jax.ShapeDtypeStruct((m, n), MXU_DTYPE)],
        compiler_params=_params(2),
    )(dx, wd, gate, up)


def _chunk_masks():
    r = lax.broadcasted_iota(jnp.int32, (SUPER, SUPER), 0)
    c = lax.broadcasted_iota(jnp.int32, (SUPER, SUPER), 1)
    same = jnp.right_shift(r, 6) == jnp.right_shift(c, 6)
    lower = jnp.logical_and(same, c <= r)
    upper = jnp.logical_and(same, c >= r)
    return same, lower, upper


def _head_block_mask():
    r = lax.broadcasted_iota(jnp.int32, (LANES, LANES), 0)
    c = lax.broadcasted_iota(jnp.int32, (LANES, LANES), 1)
    return jnp.where(jnp.right_shift(r, 6) == jnp.right_shift(c, 6), 1.0, 0.0)


def _lower_bound(lb_raw):
    return 1.0 / (1.0 + jnp.exp(lb_raw[1:2, :] - lb_raw[0:1, :]))


HGRN_UNROLL = 2
PER_SUPER = SUPER // CHUNK
CHUNK_ROWS = [slice(c * CHUNK, (c + 1) * CHUNK) for c in range(PER_SUPER)]


def _over_chunks(rows):
    return jnp.concatenate([jnp.broadcast_to(r, (CHUNK, LANES)) for r in rows], axis=0)


def _hgrn_gates(q, hf, lb, lower_b):
    sig = _sigmoid(hf)
    f = lb + (1.0 - lb) * sig
    k = 1.0 - f
    lf = jnp.log(f)
    b = _rsum_left(lower_b, lf, 2)
    ends = [b[cr.stop - 1:cr.stop, :] for cr in CHUNK_ROWS]
    eb = jnp.exp(b)
    enb = jnp.exp(-b)
    edb = jnp.exp(_over_chunks(ends) - b)
    decs = [jnp.exp(e) for e in ends]
    return sig, f, k, decs, eb, enb, edb, q * eb, k * enb, k * edb


def _hgrn_fwd(proj, lower_bounds, norm_g, n_seq, seq):
    t = n_seq * seq
    n_super = seq // SUPER
    n_chunks = seq // CHUNK

    def body(q_ref, f_ref, i_ref, g_ref, lb_ref, ng_ref, out_ref, opre_ref, st_ref):
        masks = _head_masks()
        _, lower, _ = _chunk_masks()
        lower_b = _ones_where(lower)
        bd = _head_block_mask()
        lb = _lower_bound(lb_ref[...])
        ng = ng_ref[...]

        def step(it, st):
            blocks = [HGRN_UNROLL * it + u for u in range(HGRN_UNROLL)]
            rows = [pl.ds(pl.multiple_of(sb * SUPER, SUPER), SUPER) for sb in blocks]
            vs = [i_ref[rw, :] for rw in rows]
            gates = [_hgrn_gates(q_ref[rw, :], f_ref[rw, :], lb, lower_b) for rw in rows]
            decs, qes, kes, kds = ([g[k] for g in gates] for k in (3, 7, 8, 9))
            scores = [[_dot_nt(qe * m, ke) for m in masks] for qe, ke in zip(qes, kes)]
            updates = [[_dot_tn(v[cr], kd[cr]) for cr in CHUNK_ROWS] for v, kd in zip(vs, kds)]
            states = [st]
            for dec_b, upd_b in zip(decs, updates):
                for dec, upd in zip(dec_b, upd_b):
                    states.append(states[-1] * dec + bd * upd)
            for u, sb in enumerate(blocks):
                for c in range(PER_SUPER):
                    st_ref[0, 0, sb * PER_SUPER + c] = states[u * PER_SUPER + c]
            intra = [[_dot(jnp.where(lower, p, 0.0), v) for p in sc] for sc, v in zip(scores, vs)]
            inter = [[_dot_nt(qe[cr], states[u * PER_SUPER + c]) for c, cr in enumerate(CHUNK_ROWS)]
                     for u, qe in enumerate(qes)]
            for rw, intra_b, inter_b in zip(rows, intra, inter):
                o = intra_b[0] * masks[0] + intra_b[1] * masks[1] + jnp.concatenate(inter_b, axis=0)
                opre_ref[rw, :] = o
                hg = g_ref[rw, :]
                on = o * _head_rstd(o, masks) * ng
                out_ref[rw, :] = (on * hg * _sigmoid(hg)).astype(out_ref.dtype)
            return states[-1]

        lax.fori_loop(0, n_super // HGRN_UNROLL, step, jnp.zeros((LANES, LANES), F32))

    def col(k):
        return pl.BlockSpec((seq, LANES), lambda p, b: (b, k * N_PAIRS + p))

    vec = lambda rows: pl.BlockSpec((rows, LANES), lambda p, b: (0, p))
    ospec = pl.BlockSpec((seq, LANES), lambda p, b: (b, p))
    return pl.pallas_call(
        body, name="hgrn_fwd", grid=(N_PAIRS, n_seq),
        in_specs=[col(0), col(1), col(2), col(3), vec(2), vec(1)],
        out_specs=[ospec, ospec,
                   pl.BlockSpec((1, 1, n_chunks, LANES, LANES), lambda p, b: (b, p, 0, 0, 0))],
        out_shape=[jax.ShapeDtypeStruct((t, 2 * GROUP), MXU_DTYPE), jax.ShapeDtypeStruct((t, GROUP), F32),
                   jax.ShapeDtypeStruct((n_seq, N_PAIRS, n_chunks, LANES, LANES), F32)],
        compiler_params=_params(2),
    )(proj, proj, proj, proj, lower_bounds, norm_g)


def _hgrn_bwd(proj, lower_bounds, norm_g, dmix, opre, states, n_seq, seq):
    t = n_seq * seq
    n_super = seq // SUPER
    n_chunks = seq // CHUNK
    per = SUPER // CHUNK

    def body(q_ref, f_ref, i_ref, g_ref, lb_ref, ng_ref, dm_ref, opre_ref, st_ref,
             dq_ref, df_ref, di_ref, dg_ref, dlb_ref, dng_ref):
        masks = _head_masks()
        _, lower, upper = _chunk_masks()
        lower_b, upper_b = _ones_where(lower), _ones_where(upper)
        bd = _head_block_mask()
        lb_raw = lb_ref[...]
        lb = _lower_bound(lb_raw)
        ng = ng_ref[...]

        @pl.when(pl.program_id(1) == 0)
        def _():
            dlb_ref[...] = jnp.zeros_like(dlb_ref)
            dng_ref[...] = jnp.zeros_like(dng_ref)

        def first_half(sb):
            rows = pl.ds(pl.multiple_of(sb * SUPER, SUPER), SUPER)
            q, hf, v, hg = q_ref[rows, :], f_ref[rows, :], i_ref[rows, :], g_ref[rows, :]
            sig, f, k, decs, eb, enb, edb, qe, ke, kd = _hgrn_gates(q, hf, lb, lower_b)
            o = opre_ref[rows, :]
            r = _head_rstd(o, masks)
            oh = o * r
            dm = dm_ref[rows, :]
            sg = _sigmoid(hg)
            dg_ref[rows, :] = (dm * oh * ng * sg * (1.0 + hg * (1.0 - sg))).astype(dg_ref.dtype)
            don = dm * hg * sg
            dng_ref[...] += jnp.sum(don * oh, axis=0, keepdims=True)
            doh = don * ng
            do = r * (doh - oh * _head_mean(doh * oh, masks))
            doms = [do * m for m in masks]
            qems = [qe * m for m in masks]
            scores = [_dot_nt(qem, ke) for qem in qems]
            dscores = [_dot_nt(dom, v) for dom in doms]
            prevs = [st_ref[0, 0, sb * per + c] for c in range(per)]
            dst_in = [_dot_tn(do[cr], qe[cr]) for cr in CHUNK_ROWS]
            dqe_i = [_dot(do[cr], prev) for cr, prev in zip(CHUNK_ROWS, prevs)]
            return dict(rows=rows, v=v, sig=sig, f=f, decs=decs, eb=eb, enb=enb, edb=edb, qe=qe, ke=ke, kd=kd,
                        doms=doms, qems=qems, scores=scores, dscores=dscores, prevs=prevs, dst_in=dst_in, dqe_i=dqe_i)

        def second_half(blk, dsts):
            v, qe, ke, kd = blk["v"], blk["qe"], blk["ke"], blk["kd"]
            ps = [jnp.where(lower, p, 0.0) for p in blk["scores"]]
            dps = [jnp.where(lower, dp, 0.0) for dp in blk["dscores"]]
            dqe_h = [_dot(dp, ke) for dp in dps]
            dke_h = [_dot_tn(dp, qem) for dp, qem in zip(dps, blk["qems"])]
            dv_h = [_dot_tn(p, dom) for p, dom in zip(ps, blk["doms"])]
            dus = [bd * d for d in dsts]
            dv_i = [_dot_nt(kd[cr], du) for cr, du in zip(CHUNK_ROWS, dus)]
            dkd_i = [_dot(v[cr], du) for cr, du in zip(CHUNK_ROWS, dus)]

            def finish():
                dqe = dqe_h[0] * masks[0] + dqe_h[1] * masks[1] + jnp.concatenate(blk["dqe_i"], axis=0)
                dke = dke_h[0] + dke_h[1]
                dv = dv_h[0] + dv_h[1] + jnp.concatenate(dv_i, axis=0)
                dkd = jnp.concatenate(dkd_i, axis=0)
                dk = dke * blk["enb"] + dkd * blk["edb"]
                db = dqe * qe - dke * ke - dkd * kd
                dkd_kd = dkd * kd
                dends = [jnp.sum(dkd_kd[cr], axis=0, keepdims=True)
                         + jnp.sum(dsts[c] * blk["prevs"][c], axis=0, keepdims=True) * blk["decs"][c]
                         for c, cr in enumerate(CHUNK_ROWS)]
                dlf = _rsum_left(upper_b, db, 2) + _over_chunks(dends)
                sig = blk["sig"]
                dfv = dlf / blk["f"] - dk
                rows = blk["rows"]
                dq_ref[rows, :] = (dqe * blk["eb"]).astype(dq_ref.dtype)
                di_ref[rows, :] = dv.astype(di_ref.dtype)
                df_ref[rows, :] = (dfv * (1.0 - lb) * sig * (1.0 - sig)).astype(df_ref.dtype)
                dlb = jnp.sum(dfv * (1.0 - sig), axis=0, keepdims=True)
                da0 = dlb * lb * (1.0 - lb)
                dlb_ref[0:1, :] += da0
                dlb_ref[1:2, :] -= da0

            return finish

        def step(it, dst):
            blocks = [first_half(n_super - 1 - HGRN_UNROLL * it - u) for u in range(HGRN_UNROLL)]
            all_dsts = []
            for blk in blocks:
                dsts = [None] * per
                for c in reversed(range(per)):
                    dsts[c] = dst
                    dst = bd * (dst * blk["decs"][c] + blk["dst_in"][c])
                all_dsts.append(dsts)
            for finish in [second_half(blk, dsts) for blk, dsts in zip(blocks, all_dsts)]:
                finish()
            return dst

        lax.fori_loop(0, n_super // HGRN_UNROLL, step, jnp.zeros((LANES, LANES), F32))

    def col(k):
        return pl.BlockSpec((seq, LANES), lambda p, b: (b, k * N_PAIRS + p))

    vec = lambda rows: pl.BlockSpec((rows, LANES), lambda p, b: (0, p))
    ospec = pl.BlockSpec((seq, LANES), lambda p, b: (b, p))
    piece = jax.ShapeDtypeStruct((t, GROUP), MXU_DTYPE)
    return pl.pallas_call(
        body, name="hgrn_bwd", grid=(N_PAIRS, n_seq),
        in_specs=[col(0), col(1), col(2), col(3), vec(2), vec(1), ospec, ospec,
                  pl.BlockSpec((1, 1, n_chunks, LANES, LANES), lambda p, b: (b, p, 0, 0, 0))],
        out_specs=[ospec, ospec, ospec, ospec, vec(2), vec(1)],
        out_shape=[piece, piece, piece, piece,
                   jax.ShapeDtypeStruct((2, GROUP), F32), jax.ShapeDtypeStruct((1, GROUP), F32)],
        compiler_params=_params(2),
    )(proj, proj, proj, proj, lower_bounds, norm_g, dmix, opre, states)


SB_SCALE = 1.0 / math.sqrt(D_HEAD)


SB_STEP = 2 * SB_BLOCK
QUERY_BLOCKS = (slice(0, SB_BLOCK), slice(SB_BLOCK, SB_STEP))


def _triangle(keep):
    row = lax.broadcasted_iota(jnp.int32, (SB_BLOCK, SB_BLOCK), 0)
    col = lax.broadcasted_iota(jnp.int32, (SB_BLOCK, SB_BLOCK), 1)
    return _ones_where(keep(row, col))


def _tile_masks():
    r = lax.broadcasted_iota(jnp.int32, (SB_BLOCK, SB_BLOCK), 0)
    c = lax.broadcasted_iota(jnp.int32, (SB_BLOCK, SB_BLOCK), 1)
    return r, c


def _sb_fwd(proj, norm_g, mixed, shards, n_seq, seq):
    t = n_seq * seq
    nq = seq // SB_STEP
    q0, k0, v0 = 0, N_PAIRS, 2 * N_PAIRS
    n_w = len(shards)
    n_steps = N_PAIRS * n_seq * nq
    tri = _triangle(lambda row, col: row >= col)

    def body(q_ref, k_ref, v_ref, ng_ref, tri_ref, mixed_in, *rest):
        del mixed_in
        shard_refs = rest[:n_w]
        out_ref, opre_ref, ctot_ref = rest[n_w:n_w + 3]
        gathered = rest[n_w + 3:2 * n_w + 3]
        send_sems, recv_sems, local_sems = rest[2 * n_w + 3:]
        i = pl.program_id(2)
        step = (pl.program_id(0) * n_seq + pl.program_id(1)) * nq + i
        plan = _GatherPlan(shard_refs, gathered, send_sems, recv_sems, local_sems)

        @pl.when(step == 0)
        def _():
            plan.start()

        @pl.when(step == (3 * n_steps) // 4)
        def _():
            plan.forward()

        masks = _head_masks()
        r, c = _tile_masks()
        strict = c < r
        suffix = tri_ref[...]
        qhs = [[(q_ref[blk, :] * SB_SCALE * m).astype(MXU_DTYPE) for m in masks] for blk in QUERY_BLOCKS]

        def tiles(js, work, carry):
            rows = [pl.ds(pl.multiple_of(j * SB_BLOCK, SB_BLOCK), SB_BLOCK) for j in js]
            ks = [k_ref[rw, :].astype(MXU_DTYPE) for rw in rows]
            vs = [v_ref[rw, :].astype(MXU_DTYPE) for rw in rows]
            zs = [[_dot_nt(qh, ks[ts]) for qh in qhs[qb]] for qb, ts, _ in work]
            ccs = [[_softplus(z) for z in zw] for zw in zs]
            ccs = [[jnp.where(strict, cc, 0.0) for cc in cw] if diag else cw for cw, (_, _, diag) in zip(ccs, work)]
            sums = [[jnp.dot(cc.astype(BF16), suffix, preferred_element_type=F32) for cc in cw] for cw in ccs]
            out = [list(per_block) for per_block in carry]
            for h in range(len(masks)):
                for w, (qb, ts, diag) in enumerate(work):
                    run, acc = out[qb][h]
                    a = jnp.exp(zs[w][h] - (sums[w][h] + run))
                    if diag:
                        a = jnp.where(strict, a, 0.0)
                    out[qb][h] = (run + sums[w][h][:, 0:1], acc + _dot(a, vs[ts]))
            return tuple(tuple(per_block) for per_block in out)

        start = ((jnp.zeros((SB_BLOCK, 1), F32), jnp.zeros((SB_BLOCK, LANES), F32)),) * 2
        carry = tiles([2 * i, 2 * i + 1], [(0, 0, True), (1, 1, True), (1, 0, False)], (start, start))
        both = [(0, 0, False), (0, 1, False), (1, 0, False), (1, 1, False)]
        carry = lax.fori_loop(0, i, lambda s, cy: tiles([2 * (i - s) - 1, 2 * (i - s) - 2], both, cy), carry)
        opre = jnp.concatenate([cb[0][1] * masks[0] + cb[1][1] * masks[1] for cb in carry], axis=0)
        ctot = jnp.concatenate([cb[0][0] * masks[0] + cb[1][0] * masks[1] for cb in carry], axis=0)
        opre_ref[...] = opre
        ctot_ref[...] = ctot
        out_ref[...] = (opre * _head_rstd(opre, masks) * ng_ref[...]).astype(out_ref.dtype)

        @pl.when(step == n_steps - 1)
        def _():
            plan.finish()

    qspec = pl.BlockSpec((SB_STEP, LANES), lambda p, b, i: (b * nq + i, q0 + p))
    ospec = pl.BlockSpec((SB_STEP, LANES), lambda p, b, i: (b * nq + i, p))
    hbm = pl.BlockSpec(memory_space=pltpu.HBM)
    outs = pl.pallas_call(
        body, name="sb_fwd", grid=(N_PAIRS, n_seq, nq),
        in_specs=[qspec,
                  pl.BlockSpec((seq, LANES), lambda p, b, i: (b, k0 + p)),
                  pl.BlockSpec((seq, LANES), lambda p, b, i: (b, v0 + p)),
                  pl.BlockSpec((1, LANES), lambda p, b, i: (0, p)),
                  pl.BlockSpec(tri.shape, lambda p, b, i: (0, 0)), hbm] + [hbm] * n_w,
        out_specs=[pl.BlockSpec((SB_STEP, LANES), lambda p, b, i: (b * nq + i, N_PAIRS + p)), ospec, ospec]
        + [hbm] * n_w,
        out_shape=[jax.ShapeDtypeStruct(mixed.shape, mixed.dtype), jax.ShapeDtypeStruct((t, GROUP), F32),
                   jax.ShapeDtypeStruct((t, GROUP), F32)]
        + [jax.ShapeDtypeStruct((N_DEV,) + s.shape, s.dtype) for s in shards],
        scratch_shapes=[pltpu.SemaphoreType.DMA((n_w * _GatherPlan.COPIES,)),
                        pltpu.SemaphoreType.DMA((n_w * _GatherPlan.COPIES,)), pltpu.SemaphoreType.DMA((n_w,))],
        input_output_aliases={5: 0},
        compiler_params=_params(3),
    )(proj, proj, proj, norm_g, tri, mixed, *shards)
    return outs[0], outs[1], outs[2], list(outs[3:])


def _sb_bwd(proj, norm_g, dmix, opre, ctot, grads, n_seq, seq):
    t = n_seq * seq
    nq = seq // SB_STEP
    q0, k0, v0 = 0, N_PAIRS, 2 * N_PAIRS
    n_w = len(grads)
    n_steps = N_PAIRS * n_seq * nq
    tri = _triangle(lambda row, col: row <= col)

    def body(q_ref, k_ref, v_ref, ng_ref, tri_ref, dm_ref, opre_ref, ctot_ref, *rest):
        grad_refs = rest[:n_w]
        dq_ref, dk_ref, dv_ref, dng_ref = rest[n_w:n_w + 4]
        lands = rest[n_w + 4:2 * n_w + 4]
        dk_acc, dv_acc, send_sems, recv_sems, local_sems = rest[2 * n_w + 4:]
        p_id, b_id, i = pl.program_id(0), pl.program_id(1), pl.program_id(2)
        step = (p_id * n_seq + b_id) * nq + i
        plan = _ScatterPlan(grad_refs, lands, send_sems, recv_sems, local_sems)

        @pl.when(step == 0)
        def _():
            plan.start()

        masks = _head_masks()
        r, c = _tile_masks()
        strict = c < r
        upto = tri_ref[...]

        def prefix(x):
            return jnp.dot(x.astype(BF16), upto, preferred_element_type=F32)

        @pl.when(i == 0)
        def _():
            dk_acc[...] = jnp.zeros_like(dk_acc)
            dv_acc[...] = jnp.zeros_like(dv_acc)

        @pl.when(jnp.logical_and(b_id == 0, i == 0))
        def _():
            dng_ref[...] = jnp.zeros_like(dng_ref)

        o = opre_ref[...]
        rs = _head_rstd(o, masks)
        oh = o * rs
        dm = dm_ref[...]
        dng_ref[...] += jnp.sum(dm * oh, axis=0, keepdims=True)
        doh = dm * ng_ref[...]
        do = rs * (doh - oh * _head_mean(doh * oh, masks))

        heads = range(len(masks))
        qs = [q_ref[blk, :] * SB_SCALE for blk in QUERY_BLOCKS]
        dos = [do[blk] for blk in QUERY_BLOCKS]
        qhs = [[(q * m).astype(MXU_DTYPE) for m in masks] for q in qs]
        doms = [[(d * m).astype(MXU_DTYPE) for m in masks] for d in dos]
        qhts = [[(q * m).T.astype(MXU_DTYPE) for m in masks] for q in qs]
        domts = [[(d * m).T.astype(MXU_DTYPE) for m in masks] for d in dos]
        totals = [[ctot_ref[blk, h * D_HEAD:h * D_HEAD + 1] for h in heads] for blk in QUERY_BLOCKS]

        def tiles(js, work, carry):
            rows = [pl.ds(pl.multiple_of(j * SB_BLOCK, SB_BLOCK), SB_BLOCK) for j in js]
            ks = [k_ref[rw, :].astype(MXU_DTYPE) for rw in rows]
            vs = [v_ref[rw, :].astype(MXU_DTYPE) for rw in rows]
            zs = [[_dot_nt(qhs[qb][h], ks[ts]) for h in heads] for qb, ts, _ in work]
            das = [[_dot_nt(doms[qb][h], vs[ts]) for h in heads] for qb, ts, _ in work]
            sps = [[_softplus(z) for z in zw] for zw in zs]
            lsigs = [[z - sp for z, sp in zip(zw, sw)] for zw, sw in zip(zs, sps)]
            sigs = [[jnp.exp(ls) for ls in lw] for lw in lsigs]
            ccs = [[jnp.where(strict, sp, 0.0) for sp in sw] if diag else sw for sw, (_, _, diag) in zip(sps, work)]
            pres = [[prefix(cc) for cc in cw] for cw in ccs]
            out = [list(per_block) for per_block in carry]
            for h in heads:
                for w, (qb, ts, diag) in enumerate(work):
                    pc, pdl, dq_h = out[qb][h]
                    a = jnp.exp(lsigs[w][h] + pres[w][h] - (totals[qb][h] - pc))
                    if diag:
                        a = jnp.where(strict, a, 0.0)
                    dl = a * das[w][h]
                    dv_acc[js[ts]] += _dot(domts[qb][h], a)
                    dpre = jnp.dot(dl.astype(BF16), upto, preferred_element_type=F32)
                    dz = dl - sigs[w][h] * (pdl + dpre)
                    if diag:
                        dz = jnp.where(strict, dz, 0.0)
                    dzb = dz.astype(MXU_DTYPE)
                    dk_acc[js[ts]] += _dot(qhts[qb][h], dzb)
                    out[qb][h] = (pc + pres[w][h][:, SB_BLOCK - 1:SB_BLOCK], pdl + dpre[:, SB_BLOCK - 1:SB_BLOCK],
                                  dq_h + _dot(dzb, ks[ts]))
            return tuple(tuple(per_block) for per_block in out)

        zero = jnp.zeros((SB_BLOCK, 1), F32)
        start = ((zero, zero, jnp.zeros((SB_BLOCK, LANES), F32)),) * 2
        both = [(0, 0, False), (0, 1, False), (1, 0, False), (1, 1, False)]
        carry = lax.fori_loop(0, i, lambda s, cy: tiles([2 * s, 2 * s + 1], both, cy), (start, start))
        carry = tiles([2 * i, 2 * i + 1], [(0, 0, True), (1, 0, False), (1, 1, True)], carry)
        dq = jnp.concatenate([cb[0][2] * masks[0] + cb[1][2] * masks[1] for cb in carry], axis=0)
        dq_ref[...] = (dq * SB_SCALE).astype(dq_ref.dtype)

        @pl.when(i == nq - 1)
        def _():
            for j in range(seq // SB_BLOCK):
                tile_rows = slice(j * SB_BLOCK, (j + 1) * SB_BLOCK)
                dk_ref[tile_rows, :] = dk_acc[j].T.astype(dk_ref.dtype)
                dv_ref[tile_rows, :] = dv_acc[j].T.astype(dv_ref.dtype)

        @pl.when(step == n_steps - 1)
        def _():
            plan.finish()

    qspec = pl.BlockSpec((SB_STEP, LANES), lambda p, b, i: (b * nq + i, q0 + p))
    ospec = pl.BlockSpec((SB_STEP, LANES), lambda p, b, i: (b * nq + i, p))
    dmspec = pl.BlockSpec((SB_STEP, LANES), lambda p, b, i: (b * nq + i, N_PAIRS + p))
    full = lambda k: pl.BlockSpec((seq, LANES), lambda p, b, i: (b, k + p))
    vec = pl.BlockSpec((1, LANES), lambda p, b, i: (0, p))
    hbm = pl.BlockSpec(memory_space=pltpu.HBM)
    piece = jax.ShapeDtypeStruct((t, GROUP), MXU_DTYPE)
    outs = pl.pallas_call(
        body, name="sb_bwd", grid=(N_PAIRS, n_seq, nq),
        in_specs=[qspec, full(k0), full(v0), vec, pl.BlockSpec(tri.shape, lambda p, b, i: (0, 0)), dmspec, ospec, ospec]
        + [hbm] * n_w,
        out_specs=[ospec, full(0), full(0), vec] + [hbm] * n_w,
        out_shape=[piece, piece, piece, jax.ShapeDtypeStruct((1, GROUP), F32)]
        + [jax.ShapeDtypeStruct(g.shape, g.dtype) for g in grads],
        scratch_shapes=[pltpu.VMEM((seq // SB_BLOCK, LANES, SB_BLOCK), F32),
                        pltpu.VMEM((seq // SB_BLOCK, LANES, SB_BLOCK), F32),
                        pltpu.SemaphoreType.DMA((n_w * (N_DEV - 1),)), pltpu.SemaphoreType.DMA((n_w * (N_DEV - 1),)),
                        pltpu.SemaphoreType.DMA((n_w,))],
        compiler_params=_params(3),
    )(proj, proj, proj, norm_g, tri, dmix, opre, ctot, *grads)
    return outs[0], outs[1], outs[2], outs[3], list(outs[4:])


def _mesh_place():
    x, y, c = lax.axis_index("x"), lax.axis_index("y"), lax.axis_index("c")
    return x, y, c


def _peer(x, y, c, k):
    px = lax.rem(x + ((k >> 2) & 1), 2)
    py = lax.rem(y + ((k >> 1) & 1), 2)
    pc = lax.rem(c + (k & 1), 2)
    return (px, py, pc), 4 * px + 2 * py + pc


def _remote(src, dst, send_sem, recv_sem, to):
    return pltpu.make_async_remote_copy(src_ref=src, dst_ref=dst, send_sem=send_sem, recv_sem=recv_sem,
                                        device_id=to, device_id_type=pl.DeviceIdType.MESH)


class _GatherPlan:
    COPIES = 7

    def __init__(self, shards, gathered, send_sems, recv_sems, local_sems):
        x, y, c = _mesh_place()
        self.c = c
        self.me = (x, y, c)
        self.sibling = (x, y, 1 - c)
        self.chips = [(1 - x, y), (x, 1 - y), (1 - x, 1 - y)]
        self.tensors = list(zip(shards, gathered))
        self.send_sems, self.recv_sems, self.local_sems = send_sems, recv_sems, local_sems

    @staticmethod
    def _index(place):
        return 4 * place[0] + 2 * place[1] + place[2]

    def _copy(self, w, k, block, to, own=False):
        shard, gathered = self.tensors[w]
        slot = gathered.at[self._index(block)]
        n = w * self.COPIES + k
        return _remote(shard if own else slot, slot, self.send_sems.at[n], self.recv_sems.at[n], to)

    def _local(self, w):
        shard, gathered = self.tensors[w]
        return pltpu.make_async_copy(shard, gathered.at[self._index(self.me)], self.local_sems.at[w])

    def _first(self, w):
        return [self._copy(w, 0, self.me, self.sibling, own=True)] + [
            self._copy(w, 1 + j, self.me, (*chip, self.c), own=True) for j, chip in enumerate(self.chips)]

    def _passed(self, w):
        return [self._copy(w, 4 + j, (*chip, self.c), self.sibling) for j, chip in enumerate(self.chips)]

    def start(self):
        for w in range(len(self.tensors)):
            self._local(w).start()
            for cp in self._first(w):
                cp.start()

    def forward(self):
        for w in range(len(self.tensors)):
            passed = self._passed(w)
            for j, chip in enumerate(self.chips):
                self._copy(w, 1 + j, (*chip, self.c), self.me).wait_recv()
                passed[j].start()

    def finish(self):
        for w in range(len(self.tensors)):
            self._copy(w, 0, self.sibling, self.me).wait_recv()
            for j, chip in enumerate(self.chips):
                self._copy(w, 4 + j, (*chip, 1 - self.c), self.me).wait_recv()
            for cp in self._first(w) + self._passed(w):
                cp.wait_send()
            self._local(w).wait()


class _ScatterPlan:
    def __init__(self, grads, lands, send_sems, recv_sems, local_sems):
        self.place = _mesh_place()
        x, y, c = self.place
        self.me = 4 * x + 2 * y + c
        self.tensors = list(zip(grads, lands))
        self.send_sems, self.recv_sems, self.local_sems = send_sems, recv_sems, local_sems

    def _copies(self, w):
        grad, land = self.tensors[w]
        out = []
        for k in range(1, N_DEV):
            peer, pidx = _peer(*self.place, k)
            n = w * (N_DEV - 1) + k - 1
            sems = (self.send_sems.at[n], self.recv_sems.at[n], peer)
            out.append((_remote(grad.at[pidx], land.at[self.me], *sems), _remote(grad.at[pidx], land.at[pidx], *sems)))
        return out

    def _local(self, w):
        grad, land = self.tensors[w]
        return pltpu.make_async_copy(grad.at[self.me], land.at[self.me], self.local_sems.at[w])

    def start(self):
        for w in range(len(self.tensors)):
            self._local(w).start()
            for send, _ in self._copies(w):
                send.start()

    def finish(self):
        for w in range(len(self.tensors)):
            copies = self._copies(w)
            for _, arrival in copies:
                arrival.wait_recv()
            for send, _ in copies:
                send.wait_send()
            self._local(w).wait()


def _cast_shards(shards):
    def body(*refs):
        n = len(refs) // 2
        for src, dst in zip(refs[:n], refs[n:]):
            dst[...] = src[...].astype(dst.dtype)

    vmem = pl.BlockSpec(memory_space=pltpu.VMEM)
    return pl.pallas_call(
        body, name="cast_shards", in_specs=[vmem] * len(shards), out_specs=[vmem] * len(shards),
        out_shape=[jax.ShapeDtypeStruct(s.shape, BF16) for s in shards],
        compiler_params=pltpu.CompilerParams(vmem_limit_bytes=VMEM_LIMIT),
    )(*shards)


def _gather_w_in(shard, x, g, chunk=512):
    rows, cols = shard.shape
    t, d = x.shape
    chunk = min(chunk, t)
    n_chunks = t // chunk

    def body(w_ref, x_ref, g_ref, out_ref, h_ref, x_buf, send_sems, recv_sems, local_sems, x_sems):
        plan = _GatherPlan([w_ref], [out_ref], send_sems, recv_sems, local_sems)
        plan.start()

        def fetch(j):
            return pltpu.make_async_copy(x_ref.at[pl.ds(j * chunk, chunk)], x_buf.at[j % 2], x_sems.at[j % 2])

        fetch(0).start()
        gv = g_ref[...]
        for j in range(n_chunks):
            if j + 1 < n_chunks:
                fetch(j + 1).start()
            fetch(j).wait()
            xv = x_buf[j % 2]
            r = lax.rsqrt(jnp.mean(xv * xv, axis=-1, keepdims=True) + EPS)
            h_ref[j * chunk:(j + 1) * chunk, :] = (xv * r * gv).astype(h_ref.dtype)
        plan.forward()
        plan.finish()

    vmem = pl.BlockSpec(memory_space=pltpu.VMEM)
    return pl.pallas_call(
        body, name="gather_w_in", in_specs=[vmem, pl.BlockSpec(memory_space=pltpu.HBM), vmem], out_specs=[vmem, vmem],
        out_shape=[jax.ShapeDtypeStruct((N_DEV, rows, cols), shard.dtype), jax.ShapeDtypeStruct((t, d), MXU_DTYPE)],
        scratch_shapes=[pltpu.VMEM((2, chunk, d), F32), pltpu.SemaphoreType.DMA((_GatherPlan.COPIES,)),
                        pltpu.SemaphoreType.DMA((_GatherPlan.COPIES,)), pltpu.SemaphoreType.DMA((1,)),
                        pltpu.SemaphoreType.DMA((2,))],
        compiler_params=pltpu.CompilerParams(vmem_limit_bytes=VMEM_LIMIT),
    )(shard, x, g)


def _dw_rows(pieces, b, name, tn=512, tt=1024):
    t, n = b.shape
    widths = [p.shape[1] for p in pieces]
    rows = sum(widths)
    tt = min(tt, t)
    steps = t // tt
    n_p = len(pieces)

    def body(*refs):
        piece_refs, b_ref, o_ref, acc = refs[:n_p], refs[n_p], refs[n_p + 1], refs[n_p + 2]
        s = pl.program_id(1)

        @pl.when(s == 0)
        def _():
            acc[...] = jnp.zeros_like(acc)

        bv = b_ref[...]
        off = 0
        for p_ref, width in zip(piece_refs, widths):
            acc[off:off + width, :] += _dot_tn(p_ref[...], bv)
            off += width

        @pl.when(s == steps - 1)
        def _():
            o_ref[...] = acc[...].astype(o_ref.dtype)

    return pl.pallas_call(
        body, name=name, grid=(n // tn, steps),
        in_specs=[pl.BlockSpec((tt, width), lambda j, s: (s, 0)) for width in widths]
        + [pl.BlockSpec((tt, tn), lambda j, s: (s, j))],
        out_specs=pl.BlockSpec((rows, tn), lambda j, s: (0, j)),
        out_shape=jax.ShapeDtypeStruct((rows, n), BF16),
        scratch_shapes=[pltpu.VMEM((rows, tn), F32)],
        compiler_params=_params(2),
    )(*pieces, b)


def _dh_norm_bwd(pairs, x, g, res, name, tm=512, after=()):
    m, d = x.shape
    tm = min(tm, m)
    n_p = len(pairs)
    n_steps = m // tm

    def body(*refs):
        x_ref, g_ref, res_ref = refs[2 * n_p:2 * n_p + 3]
        dx_ref, dg_ref = refs[2 * n_p + 3 + len(after):]

        @pl.when(pl.program_id(0) == 0)
        def _():
            dg_ref[...] = jnp.zeros_like(dg_ref)

        dh = None
        for q in range(n_p):
            part = _dot(refs[2 * q][...], refs[2 * q + 1][...])
            dh = part if dh is None else dh + part
        xv = x_ref[...]
        r = lax.rsqrt(jnp.mean(xv * xv, axis=-1, keepdims=True) + EPS)
        xh = xv * r
        dxh = dh * g_ref[...]
        dx_ref[...] = res_ref[...] + r * (dxh - xh * jnp.mean(dxh * xh, axis=-1, keepdims=True))
        dg_ref[...] += jnp.sum(dh * xh, axis=0, keepdims=True)

    in_specs, args = [], []
    for a, w, r in pairs:
        k = a.shape[1]
        in_specs += [pl.BlockSpec((tm, k), lambda i: (i, 0)),
                     pl.BlockSpec((k, d), functools.partial(lambda i, r: (r, 0), r=r or 0),
                                  pipeline_mode=pl.Buffered(1))]
        args += [a, w]
    row = pl.BlockSpec((tm, d), lambda i: (i, 0))
    vec = pl.BlockSpec((1, d), lambda i: (0, 0))
    return pl.pallas_call(
        body, name=name, grid=(n_steps,),
        in_specs=in_specs + [row, vec, row] + [pl.BlockSpec(memory_space=pl.ANY)] * len(after),
        out_specs=[row, vec],
        out_shape=[jax.ShapeDtypeStruct((m, d), F32), jax.ShapeDtypeStruct((1, d), F32)],
        compiler_params=_params(1),
    )(*args, x, g, res, *after)


def _exchange_start(grad, name):
    def body(g_ref, land_ref, send_sem, recv_sem, local_sem, g_thru, land_thru, token):
        del g_thru, land_thru
        x, y, c = _mesh_place()
        me = 4 * x + 2 * y + c
        pltpu.make_async_copy(g_ref.at[me], land_ref.at[me], local_sem).start()
        for k in range(1, N_DEV):
            peer, pidx = _peer(x, y, c, k)
            _remote(g_ref.at[pidx], land_ref.at[me], send_sem, recv_sem, peer).start()
        token[...] = jnp.zeros_like(token)

    hbm = pl.BlockSpec(memory_space=pltpu.HBM)
    sem = pl.BlockSpec(memory_space=pltpu.SEMAPHORE)
    buf = pltpu.HBM(grad.shape, grad.dtype)
    return pl.pallas_call(
        body, name=name,
        out_shape=(pltpu.SemaphoreType.DMA(()), pltpu.SemaphoreType.DMA(()), pltpu.SemaphoreType.DMA(()), buf, buf,
                   jax.ShapeDtypeStruct((8, LANES), F32)),
        in_specs=(hbm, hbm), out_specs=(sem, sem, sem, hbm, hbm, pl.BlockSpec(memory_space=pltpu.VMEM)),
        input_output_aliases={0: 3, 1: 4},
        compiler_params=pltpu.CompilerParams(has_side_effects=pltpu.SideEffectType.DATAFLOW_SIDE_EFFECTING),
    )(pltpu.with_memory_space_constraint(grad, pltpu.HBM),
      pltpu.with_memory_space_constraint(lax.empty(grad.shape, grad.dtype), pltpu.HBM))


def _exchange_wait(send_sem, recv_sem, local_sem, grad, land, after, name):
    def body(g_ref, land_ref, send_sem, recv_sem, local_sem, *rest):
        x, y, c = _mesh_place()
        peer, _ = _peer(x, y, c, 1)
        others = pl.ds(0, N_DEV - 1)
        seven = _remote(g_ref.at[others], land_ref.at[others], send_sem, recv_sem, peer)
        seven.wait_send()
        seven.wait_recv()
        pltpu.make_async_copy(g_ref.at[0], land_ref.at[0], local_sem).wait()

    hbm = pl.BlockSpec(memory_space=pltpu.HBM)
    sem = pl.BlockSpec(memory_space=pltpu.SEMAPHORE)
    buf = pltpu.HBM(grad.shape, grad.dtype)
    return pl.pallas_call(
        body, name=name, out_shape=(buf, buf),
        in_specs=(hbm, hbm, sem, sem, sem) + (pl.BlockSpec(memory_space=pl.ANY),) * len(after), out_specs=(hbm, hbm),
        input_output_aliases={0: 0, 1: 1},
        compiler_params=pltpu.CompilerParams(has_side_effects=pltpu.SideEffectType.DATAFLOW_SIDE_EFFECTING),
    )(grad, land, send_sem, recv_sem, local_sem, *after)[1]


SMALL_LAYOUT = ((0, 0, 0, 0, D_MODEL), (1, 0, 1, 0, GROUP), (1, 1, 1, GROUP, GROUP), (2, 0, 2, 0, GROUP),
                (3, 0, 2, GROUP, GROUP), (4, 0, 3, 0, D_MODEL), (5, 0, 4, 0, D_MODEL))
LOSS_ROW = 5
N_SMALL = 6


def _small_step(grads, loss_part, ws, ms, vs):
    def body(*refs):
        g_in, loss_in = refs[:N_SMALL], refs[N_SMALL]
        params = [refs[1 + (q + 1) * N_SMALL:1 + (q + 2) * N_SMALL] for q in range(3)]
        o0 = 1 + 4 * N_SMALL
        outs = [refs[o0 + q * N_SMALL:o0 + (q + 1) * N_SMALL] for q in range(4)]
        loss_out = refs[o0 + 4 * N_SMALL]
        pack, land, wp, mp, vp, send_sems, recv_sems = refs[o0 + 4 * N_SMALL + 1:]

        def place(dst, srcs):
            dst[...] = jnp.zeros_like(dst)
            for p, sr, dr, dc, width in SMALL_LAYOUT:
                dst[dr:dr + 1, dc:dc + width] = srcs[p][sr:sr + 1, :]

        place(pack, g_in)
        pack[LOSS_ROW:LOSS_ROW + 1, 0:LANES] = loss_in[...]
        for dst, srcs in zip((wp, mp, vp), params):
            place(dst, srcs)

        x, y, c = _mesh_place()
        me = 4 * x + 2 * y + c
        land[me] = pack[...]
        sends = []
        for k in range(1, N_DEV):
            peer, _ = _peer(x, y, c, k)
            cp = _remote(pack, land.at[me], send_sems.at[k - 1], recv_sems.at[k - 1], peer)
            cp.start()
            sends.append(cp)
        for k in range(1, N_DEV):
            peer, pidx = _peer(x, y, c, k)
            _remote(pack, land.at[pidx], send_sems.at[k - 1], recv_sems.at[k - 1], peer).wait_recv()
        for cp in sends:
            cp.wait_send()

        g = land[0]
        for d in range(1, N_DEV):
            g = g + land[d]
        delta, nm, nv = _adam(wp[...], g, mp[...], vp[...])
        for val, out in zip((g, delta, nm, nv), outs):
            for p, sr, dr, dc, width in SMALL_LAYOUT:
                out[p][sr:sr + 1, :] = val[dr:dr + 1, dc:dc + width]
        loss_out[...] = g[LOSS_ROW:LOSS_ROW + 1, 0:LANES]

    vmem = pl.BlockSpec(memory_space=pltpu.VMEM)
    n_in = 1 + 4 * N_SMALL
    shapes = [jax.ShapeDtypeStruct(w.shape, F32) for w in ws]
    packed = pltpu.VMEM((SMALL_ROWS, D_MODEL), F32)
    outs = pl.pallas_call(
        body, name="small_step", in_specs=[vmem] * n_in, out_specs=[vmem] * (4 * N_SMALL + 1),
        out_shape=shapes * 4 + [jax.ShapeDtypeStruct((1, LANES), F32)],
        scratch_shapes=[packed, pltpu.VMEM((N_DEV, SMALL_ROWS, D_MODEL), F32), packed, packed, packed,
                        pltpu.SemaphoreType.DMA((N_DEV - 1,)), pltpu.SemaphoreType.DMA((N_DEV - 1,))],
    )(*grads, loss_part, *ws, *ms, *vs)
    return [outs[q * N_SMALL:(q + 1) * N_SMALL] for q in range(4)], outs[4 * N_SMALL]


def _adam(w, g, m, v):
    m = ADAM_B1 * m + (1.0 - ADAM_B1) * g
    v = ADAM_B2 * v + (1.0 - ADAM_B2) * (g * g)
    m_hat = m / (1.0 - ADAM_B1 ** ADAM_STEP)
    v_hat = v / (1.0 - ADAM_B2 ** ADAM_STEP)
    delta = -ADAM_LR * (m_hat / (jnp.sqrt(v_hat) + ADAM_EPS) + ADAM_WD * w)
    return delta, m, v


def _reduce_adamw(land, w, m, v, name, tr, after=()):
    _, rows, width = land.shape

    def body(land_ref, w_ref, m_ref, v_ref, *rest):
        g_ref, d_ref, nm_ref, nv_ref = rest[len(after):]
        g = land_ref[0].astype(F32)
        for d in range(1, N_DEV):
            g = g + land_ref[d].astype(F32)
        delta, nm, nv = _adam(w_ref[...], g, m_ref[...], v_ref[...])
        g_ref[...] = g
        d_ref[...] = delta
        nm_ref[...] = nm
        nv_ref[...] = nv

    row = pl.BlockSpec((tr, width), lambda i: (i, 0))
    out = jax.ShapeDtypeStruct((rows, width), F32)
    return pl.pallas_call(
        body, name=name, grid=(rows // tr,),
        in_specs=[pl.BlockSpec((N_DEV, tr, width), lambda i: (0, i, 0)), row, row, row]
        + [pl.BlockSpec(memory_space=pl.ANY)] * len(after),
        out_specs=[row, row, row, row], out_shape=[out, out, out, out],
        compiler_params=_params(1),
    )(land, w, m, v, *after)


def kernel(x, mix_norm_g, w_in, lower_bounds, hgrn_norm_g, sb_norm_g, w_out, ffn_norm_g, w_gate, w_up, w_down, final_norm_g, loss_target, m_mix_norm_g, m_w_in, m_lower_bounds, m_hgrn_norm_g, m_sb_norm_g, m_w_out, m_ffn_norm_g, m_w_gate, m_w_up, m_w_down, m_final_norm_g, v_mix_norm_g, v_w_in, v_lower_bounds, v_hgrn_norm_g, v_sb_norm_g, v_w_out, v_ffn_norm_g, v_w_gate, v_w_up, v_w_down, v_final_norm_g):
    n_seq, seq, d = x.shape
    t = n_seq * seq
    x2d = x.reshape(t, d)
    tgt = loss_target.reshape(t, d)
    final_g = final_norm_g.reshape(1, d)
    col_sharded = (True, False, True, True, False)

    def as_rows(ws):
        return [w[0].T if tr else w[0] for w, tr in zip(ws, col_sharded)]

    big_w = as_rows([w_in, w_out, w_gate, w_up, w_down])
    big_m = as_rows([m_w_in, m_w_out, m_w_gate, m_w_up, m_w_down])
    big_v = as_rows([v_w_in, v_w_out, v_w_gate, v_w_up, v_w_down])

    sh_in, sh_out, sh_gate, sh_up, sh_down = _cast_shards(big_w)
    wt_in, h1 = _gather_w_in(sh_in, x2d, mix_norm_g)
    wt_in = wt_in.reshape(IN_COLS, d)
    proj_h = _mm_nt(h1, wt_in, "proj_hgrn", rows=(0, 4 * GROUP), tm=1024, tk=1024)
    proj_s = _mm_nt(h1, wt_in, "proj_sb", rows=(4 * GROUP, 3 * GROUP), out_dtype=MXU_DTYPE, tm=1024, tk=512)
    mixed, oa_pre, states = _hgrn_fwd(proj_h, lower_bounds, hgrn_norm_g, n_seq, seq)
    mixed, ob_pre, ctot, gathered = _sb_fwd(proj_s, sb_norm_g, mixed, [sh_out, sh_gate, sh_up, sh_down], n_seq, seq)
    wf_out = gathered[0].reshape(d, d)
    wt_gate = gathered[1].reshape(D_FF, d)
    wt_up = gathered[2].reshape(D_FF, d)
    wf_down = gathered[3].reshape(D_FF, d)
    x1, h2 = _mix_out_norm(mixed, wf_out, x2d, ffn_norm_g, "mix_out")
    gate, up, ff = _ffn_up(h2, wt_gate, wt_up, "ffn_up")
    dx2, dx2m, d_final_g, loss_part = _ffn_down_loss(ff, wf_down, x1, tgt, final_g, "ffn_down_loss")

    dgate, dup = _ffn_bwd_act(dx2m, wf_down, gate, up, "ffn_bwd_act")
    dw_down = _mm_tn(ff, dx2m, "dw_down", tk=1408, tn=1024).reshape(N_DEV, D_FF // N_DEV, d)
    dw_gate = _mm_tn(dgate, h2, "dw_gate", tk=1408, tn=1024).reshape(N_DEV, D_FF // N_DEV, d)
    dw_up = _mm_tn(dup, h2, "dw_up", tk=1408, tn=1024).reshape(N_DEV, D_FF // N_DEV, d)
    dx1, d_ffn_g = _dh_norm_bwd([(dgate, wt_gate, None), (dup, wt_up, None)], x1, ffn_norm_g, dx2, "dh_ffn")
    dmix = _mm_nt(dx1, wf_out, "dmix")
    dw_out = _mm_tn(mixed, dx1, "dw_out", tk=512, tn=1024).reshape(N_DEV, d // N_DEV, d)
    dsq, dsk, dsv, d_sb_g, lands = _sb_bwd(proj_s, sb_norm_g, dmix, ob_pre, ctot, [dw_out, dw_gate, dw_up, dw_down],
                                            n_seq, seq)
    dhq, dhf, dhi, dhg, d_lb, d_hgrn_g = _hgrn_bwd(proj_h, lower_bounds, hgrn_norm_g, dmix, oa_pre, states, n_seq,
                                                   seq)
    dproj = [dhq, dhf, dhi, dhg, dsq, dsk, dsv]
    dw_in = _dw_rows(dproj, h1, "dw_in").reshape(N_DEV, IN_COLS // N_DEV, d)
    send_sem, recv_sem, local_sem, dw_in, land_in, token = _exchange_start(dw_in, "dw_in_send")
    dx, d_mix_g = _dh_norm_bwd([(piece, wt_in, k) for k, piece in enumerate(dproj)], x2d, mix_norm_g, dx1, "dh_mix",
                               after=(token,))

    tiles = {"in": 224, "out": 128, "gate": 176, "up": 176, "down": 176}
    keys = list(tiles)
    rest = [_reduce_adamw(land, w, m, v, "adamw_" + key, tr=tiles[key], after=(token,))
            for key, land, w, m, v in zip(keys[1:], lands, big_w[1:], big_m[1:], big_v[1:])]
    small, loss_row = _small_step(
        [d_mix_g, d_lb, d_hgrn_g, d_sb_g, d_ffn_g, d_final_g], loss_part,
        [mix_norm_g, lower_bounds, hgrn_norm_g, sb_norm_g, ffn_norm_g, final_g],
        [m_mix_norm_g, m_lower_bounds, m_hgrn_norm_g, m_sb_norm_g, m_ffn_norm_g, m_final_norm_g.reshape(1, d)],
        [v_mix_norm_g, v_lower_bounds, v_hgrn_norm_g, v_sb_norm_g, v_ffn_norm_g, v_final_norm_g.reshape(1, d)])
    land_in = _exchange_wait(send_sem, recv_sem, local_sem, dw_in, land_in,
                             [dx, loss_row] + [res[0] for res in rest], "dw_in_await")
    big = [_reduce_adamw(land_in, big_w[0], big_m[0], big_v[0], "adamw_in", tr=tiles["in"])] + rest
    big = [[r.T if tr else r for r in res] for res, tr in zip(big, col_sharded)]

    outs = [loss_row[0, 0], dx.reshape(n_seq, seq, d)]
    for q in range(4):
        b_in, b_out, b_gate, b_up, b_down = [res[q][None] for res in big]
        s_mix, s_lb, s_hgrn, s_sb, s_ffn, s_final = small[q]
        outs += [s_mix, b_in, s_lb, s_hgrn, s_sb, b_out, s_ffn, b_gate, b_up, b_down, s_final.reshape(d)]
    return tuple(outs)
```

```python
import functools
import math

import jax
import jax.numpy as jnp
from jax import lax
from jax.experimental import pallas as pl
from jax.experimental.pallas import tpu as pltpu

F32 = jnp.float32
BF16 = jnp.bfloat16
MXU_DTYPE = BF16

EPS = 1e-6
D_MODEL = 1024
N_HEADS = 8
D_HEAD = 64
GROUP = N_HEADS * D_HEAD
IN_COLS = 7 * GROUP
D_FF = 2816
CHUNK = 64
LANES = 128
N_PAIRS = GROUP // LANES
SUPER = 256
SB_BLOCK = 256
N_DEV = 8

ADAM_LR = 0.001
ADAM_B1 = 0.9
ADAM_B2 = 0.999
ADAM_EPS = 1e-08
ADAM_WD = 0.01
ADAM_STEP = 10

SMALL_ROWS = 8
FF_TILE = D_FF // 2

VMEM_LIMIT = 48 * 1024 * 1024


def _params(n_axes, vmem=VMEM_LIMIT):
    return pltpu.CompilerParams(dimension_semantics=("arbitrary",) * n_axes, vmem_limit_bytes=vmem)


def _dot(a, b):
    return jnp.dot(a.astype(MXU_DTYPE), b.astype(MXU_DTYPE), preferred_element_type=F32)


def _dot_nt(a, b):
    return lax.dot_general(a.astype(MXU_DTYPE), b.astype(MXU_DTYPE), (((1,), (1,)), ((), ())),
                           preferred_element_type=F32)


def _dot_tn(a, b):
    return lax.dot_general(a.astype(MXU_DTYPE), b.astype(MXU_DTYPE), (((0,), (0,)), ((), ())),
                           preferred_element_type=F32)


def _split(x, parts):
    out, r = [], x
    for _ in range(parts):
        h = r.astype(BF16)
        out.append(h)
        r = r - h.astype(F32)
    return out


def _rsum_left(u, x, parts):
    acc = None
    for h in _split(x, parts):
        d = jnp.dot(u, h, preferred_element_type=F32)
        acc = d if acc is None else acc + d
    return acc


def _ones_where(mask):
    return jnp.where(mask, 1.0, 0.0).astype(BF16)


def _sigmoid(x):
    return 1.0 / (1.0 + jnp.exp(-x))


def _softplus(x):
    return jnp.maximum(x, 0.0) + jnp.log(1.0 + jnp.exp(-jnp.abs(x)))


def _head_masks():
    lane = lax.broadcasted_iota(jnp.int32, (1, LANES), 1)
    return [jnp.where(lane < D_HEAD, 1.0, 0.0), jnp.where(lane >= D_HEAD, 1.0, 0.0)]


def _head_rstd(o, masks):
    sq = o * o
    r = None
    for m in masks:
        ms = jnp.sum(sq * m, axis=1, keepdims=True) * (1.0 / D_HEAD)
        t = lax.rsqrt(ms + EPS) * m
        r = t if r is None else r + t
    return r


def _head_mean(t, masks):
    out = None
    for m in masks:
        v = jnp.sum(t * m, axis=1, keepdims=True) * (1.0 / D_HEAD) * m
        out = v if out is None else out + v
    return out


def _mix_out_norm(a, w, res, g, name, tm=512):
    m, k = a.shape
    d = w.shape[1]
    tm = min(tm, m)

    def body(a_ref, w_ref, res_ref, g_ref, x_ref, h_ref):
        xv = res_ref[...] + _dot(a_ref[...], w_ref[...])
        x_ref[...] = xv
        r = lax.rsqrt(jnp.mean(xv * xv, axis=-1, keepdims=True) + EPS)
        h_ref[...] = (xv * r * g_ref[...]).astype(h_ref.dtype)

    row = pl.BlockSpec((tm, d), lambda i: (i, 0))
    return pl.pallas_call(
        body, name=name, grid=(m // tm,),
        in_specs=[pl.BlockSpec((tm, k), lambda i: (i, 0)), pl.BlockSpec((k, d), lambda i: (0, 0)), row,
                  pl.BlockSpec((1, d), lambda i: (0, 0))],
        out_specs=[row, row],
        out_shape=[jax.ShapeDtypeStruct((m, d), F32), jax.ShapeDtypeStruct((m, d), MXU_DTYPE)],
        compiler_params=_params(1),
    )(a, w, res, g)


def _ffn_down_loss(a, w, res, target, g, name, tm=512):
    m, k = a.shape
    d = w.shape[1]
    tm = min(tm, m)

    def body(a_ref, w_ref, res_ref, t_ref, g_ref, dx_ref, dxm_ref, dg_ref, loss_ref):
        xv = res_ref[...] + _dot(a_ref[...], w_ref[...])
        gv = g_ref[...]
        r = lax.rsqrt(jnp.mean(xv * xv, axis=-1, keepdims=True) + EPS)
        xh = xv * r
        e = xh * gv - t_ref[...]
        dy = e * (1.0 / d)
        dxh = dy * gv
        dxv = r * (dxh - xh * jnp.mean(dxh * xh, axis=-1, keepdims=True))
        dx_ref[...] = dxv
        dxm_ref[...] = dxv.astype(dxm_ref.dtype)

        @pl.when(pl.program_id(0) == 0)
        def _():
            dg_ref[...] = jnp.zeros_like(dg_ref)
            loss_ref[...] = jnp.zeros_like(loss_ref)

        dg_ref[...] += jnp.sum(dy * xh, axis=0, keepdims=True)
        part = 0.5 * jnp.sum(jnp.mean(e * e, axis=-1, keepdims=True), axis=0, keepdims=True)
        loss_ref[...] += jnp.broadcast_to(part, loss_ref.shape)

    row = pl.BlockSpec((tm, d), lambda i: (i, 0))
    vec = pl.BlockSpec((1, d), lambda i: (0, 0))
    return pl.pallas_call(
        body, name=name, grid=(m // tm,),
        in_specs=[pl.BlockSpec((tm, k), lambda i: (i, 0)), pl.BlockSpec((k, d), lambda i: (0, 0)), row, row, vec],
        out_specs=[row, row, vec, pl.BlockSpec((1, LANES), lambda i: (0, 0))],
        out_shape=[jax.ShapeDtypeStruct((m, d), F32), jax.ShapeDtypeStruct((m, d), MXU_DTYPE),
                   jax.ShapeDtypeStruct((1, d), F32), jax.ShapeDtypeStruct((1, LANES), F32)],
        compiler_params=_params(1),
    )(a, w, res, target, g)


def _mm_nt(a, w, name, rows=None, out_dtype=F32, tm=512, tk=512):
    m, n = a.shape
    row0, k = rows or (0, w.shape[0])
    tm, tk = min(tm, m), min(tk, k)
    first = row0 // tk

    def body(a_ref, w_ref, o_ref):
        o_ref[...] = _dot_nt(a_ref[...], w_ref[...]).astype(o_ref.dtype)

    return pl.pallas_call(
        body, name=name, grid=(k // tk, m // tm),
        in_specs=[pl.BlockSpec((tm, n), lambda j, i: (i, 0)), pl.BlockSpec((tk, n), lambda j, i: (first + j, 0))],
        out_specs=pl.BlockSpec((tm, tk), lambda j, i: (i, j)),
        out_shape=jax.ShapeDtypeStruct((m, k), out_dtype),
        compiler_params=_params(2),
    )(a, w)


def _mm_tn(a, b, name, tk, tn, tt=1024, out_dtype=BF16):
    t, k = a.shape
    n = b.shape[1]
    tt = min(tt, t)
    steps = t // tt

    def body(a_ref, b_ref, o_ref, acc):
        s = pl.program_id(2)

        @pl.when(s == 0)
        def _():
            acc[...] = jnp.zeros_like(acc)

        acc[...] += _dot_tn(a_ref[...], b_ref[...])

        @pl.when(s == steps - 1)
        def _():
            o_ref[...] = acc[...].astype(o_ref.dtype)

    return pl.pallas_call(
        body, name=name, grid=(k // tk, n // tn, steps),
        in_specs=[pl.BlockSpec((tt, tk), lambda i, j, s: (s, i)), pl.BlockSpec((tt, tn), lambda i, j, s: (s, j))],
        out_specs=pl.BlockSpec((tk, tn), lambda i, j, s: (i, j)),
        out_shape=jax.ShapeDtypeStruct((k, n), out_dtype),
        scratch_shapes=[pltpu.VMEM((tk, tn), F32)],
        compiler_params=_params(3),
    )(a, b)


def _ffn_up(h, wg_t, wu_t, name, tm=512, tn=FF_TILE):
    m, k = h.shape
    n = wg_t.shape[0]
    tm = min(tm, m)

    def body(h_ref, wg_ref, wu_ref, gate_ref, up_ref, ff_ref):
        hv = h_ref[...]
        gate = _dot_nt(hv, wg_ref[...])
        up = _dot_nt(hv, wu_ref[...])
        gate_ref[...] = gate.astype(gate_ref.dtype)
        up_ref[...] = up.astype(up_ref.dtype)
        ff_ref[...] = (gate * _sigmoid(gate) * up).astype(ff_ref.dtype)

    wspec = pl.BlockSpec((tn, k), lambda j, i: (j, 0))
    ospec = pl.BlockSpec((tm, tn), lambda j, i: (i, j))
    return pl.pallas_call(
        body, name=name, grid=(n // tn, m // tm),
        in_specs=[pl.BlockSpec((tm, k), lambda j, i: (i, 0)), wspec, wspec],
        out_specs=[ospec, ospec, ospec],
        out_shape=[jax.ShapeDtypeStruct((m, n), MXU_DTYPE)] * 3,
        compiler_params=_params(2),
    )(h, wg_t, wu_t)


def _ffn_bwd_act(dx, wd, gate, up, name, tm=512, tn=FF_TILE):
    m, k = dx.shape
    n = wd.shape[0]
    tm = min(tm, m)

    def body(dx_ref, wd_ref, gate_ref, up_ref, dgate_ref, dup_ref):
        dff = _dot_nt(dx_ref[...], wd_ref[...])
        gate = gate_ref[...].astype(F32)
        sg = _sigmoid(gate)
        dgate_ref[...] = (dff * up_ref[...].astype(F32) * sg * (1.0 + gate * (1.0 - sg))).astype(dgate_ref.dtype)
        dup_ref[...] = (dff * gate * sg).astype(dup_ref.dtype)

    ospec = pl.BlockSpec((tm, tn), lambda j, i: (i, j))
    return pl.pallas_call(
        body, name=name, grid=(n // tn, m // tm),
        in_specs=[pl.BlockSpec((tm, k), lambda j, i: (i, 0)), pl.BlockSpec((tn, k), lambda j, i: (j, 0)),
                  ospec, ospec],
        out_specs=[ospec, ospec],
        out_shape=[jax.ShapeDtypeStruct((m, n), MXU_DTYPE), jax.ShapeDtypeStruct((m, n), MXU_DTYPE)],
        compiler_params=_params(2),
    )(dx, wd, gate, up)


def _chunk_masks():
    r = lax.broadcasted_iota(jnp.int32, (SUPER, SUPER), 0)
    c = lax.broadcasted_iota(jnp.int32, (SUPER, SUPER), 1)
    same = jnp.right_shift(r, 6) == jnp.right_shift(c, 6)
    lower = jnp.logical_and(same, c <= r)
    upper = jnp.logical_and(same, c >= r)
    return same, lower, upper


def _head_block_mask():
    r = lax.broadcasted_iota(jnp.int32, (LANES, LANES), 0)
    c = lax.broadcasted_iota(jnp.int32, (LANES, LANES), 1)
    return jnp.where(jnp.right_shift(r, 6) == jnp.right_shift(c, 6), 1.0, 0.0)


def _lower_bound(lb_raw):
    return 1.0 / (1.0 + jnp.exp(lb_raw[1:2, :] - lb_raw[0:1, :]))


HGRN_UNROLL = 2
PER_SUPER = SUPER // CHUNK
CHUNK_ROWS = [slice(c * CHUNK, (c + 1) * CHUNK) for c in range(PER_SUPER)]


def _over_chunks(rows):
    return jnp.concatenate([jnp.broadcast_to(r, (CHUNK, LANES)) for r in rows], axis=0)


def _hgrn_gates(q, hf, lb, lower_b):
    sig = _sigmoid(hf)
    f = lb + (1.0 - lb) * sig
    k = 1.0 - f
    lf = jnp.log(f)
    b = _rsum_left(lower_b, lf, 2)
    ends = [b[cr.stop - 1:cr.stop, :] for cr in CHUNK_ROWS]
    eb = jnp.exp(b)
    enb = jnp.exp(-b)
    edb = jnp.exp(_over_chunks(ends) - b)
    decs = [jnp.exp(e) for e in ends]
    return sig, f, k, decs, eb, enb, edb, q * eb, k * enb, k * edb


def _hgrn_fwd(proj, lower_bounds, norm_g, n_seq, seq):
    t = n_seq * seq
    n_super = seq // SUPER
    n_chunks = seq // CHUNK

    def body(q_ref, f_ref, i_ref, g_ref, lb_ref, ng_ref, out_ref, opre_ref, st_ref):
        masks = _head_masks()
        _, lower, _ = _chunk_masks()
        lower_b = _ones_where(lower)
        bd = _head_block_mask()
        lb = _lower_bound(lb_ref[...])
        ng = ng_ref[...]

        def step(it, st):
            blocks = [HGRN_UNROLL * it + u for u in range(HGRN_UNROLL)]
            rows = [pl.ds(pl.multiple_of(sb * SUPER, SUPER), SUPER) for sb in blocks]
            vs = [i_ref[rw, :] for rw in rows]
            gates = [_hgrn_gates(q_ref[rw, :], f_ref[rw, :], lb, lower_b) for rw in rows]
            decs, qes, kes, kds = ([g[k] for g in gates] for k in (3, 7, 8, 9))
            scores = [[_dot_nt(qe * m, ke) for m in masks] for qe, ke in zip(qes, kes)]
            updates = [[_dot_tn(v[cr], kd[cr]) for cr in CHUNK_ROWS] for v, kd in zip(vs, kds)]
            states = [st]
            for dec_b, upd_b in zip(decs, updates):
                for dec, upd in zip(dec_b, upd_b):
                    states.append(states[-1] * dec + bd * upd)
            for u, sb in enumerate(blocks):
                for c in range(PER_SUPER):
                    st_ref[0, 0, sb * PER_SUPER + c] = states[u * PER_SUPER + c]
            intra = [[_dot(jnp.where(lower, p, 0.0), v) for p in sc] for sc, v in zip(scores, vs)]
            inter = [[_dot_nt(qe[cr], states[u * PER_SUPER + c]) for c, cr in enumerate(CHUNK_ROWS)]
                     for u, qe in enumerate(qes)]
            for rw, intra_b, inter_b in zip(rows, intra, inter):
                o = intra_b[0] * masks[0] + intra_b[1] * masks[1] + jnp.concatenate(inter_b, axis=0)
                opre_ref[rw, :] = o
                hg = g_ref[rw, :]
                on = o * _head_rstd(o, masks) * ng
                out_ref[rw, :] = (on * hg * _sigmoid(hg)).astype(out_ref.dtype)
            return states[-1]

        lax.fori_loop(0, n_super // HGRN_UNROLL, step, jnp.zeros((LANES, LANES), F32))

    def col(k):
        return pl.BlockSpec((seq, LANES), lambda p, b: (b, k * N_PAIRS + p))

    vec = lambda rows: pl.BlockSpec((rows, LANES), lambda p, b: (0, p))
    ospec = pl.BlockSpec((seq, LANES), lambda p, b: (b, p))
    return pl.pallas_call(
        body, name="hgrn_fwd", grid=(N_PAIRS, n_seq),
        in_specs=[col(0), col(1), col(2), col(3), vec(2), vec(1)],
        out_specs=[ospec, ospec,
                   pl.BlockSpec((1, 1, n_chunks, LANES, LANES), lambda p, b: (b, p, 0, 0, 0))],
        out_shape=[jax.ShapeDtypeStruct((t, 2 * GROUP), MXU_DTYPE), jax.ShapeDtypeStruct((t, GROUP), F32),
                   jax.ShapeDtypeStruct((n_seq, N_PAIRS, n_chunks, LANES, LANES), F32)],
        compiler_params=_params(2),
    )(proj, proj, proj, proj, lower_bounds, norm_g)


def _hgrn_bwd(proj, lower_bounds, norm_g, dmix, opre, states, n_seq, seq):
    t = n_seq * seq
    n_super = seq // SUPER
    n_chunks = seq // CHUNK
    per = SUPER // CHUNK

    def body(q_ref, f_ref, i_ref, g_ref, lb_ref, ng_ref, dm_ref, opre_ref, st_ref,
             dq_ref, df_ref, di_ref, dg_ref, dlb_ref, dng_ref):
        masks = _head_masks()
        _, lower, upper = _chunk_masks()
        lower_b, upper_b = _ones_where(lower), _ones_where(upper)
        bd = _head_block_mask()
        lb_raw = lb_ref[...]
        lb = _lower_bound(lb_raw)
        ng = ng_ref[...]

        @pl.when(pl.program_id(1) == 0)
        def _():
            dlb_ref[...] = jnp.zeros_like(dlb_ref)
            dng_ref[...] = jnp.zeros_like(dng_ref)

        def first_half(sb):
            rows = pl.ds(pl.multiple_of(sb * SUPER, SUPER), SUPER)
            q, hf, v, hg = q_ref[rows, :], f_ref[rows, :], i_ref[rows, :], g_ref[rows, :]
            sig, f, k, decs, eb, enb, edb, qe, ke, kd = _hgrn_gates(q, hf, lb, lower_b)
            o = opre_ref[rows, :]
            r = _head_rstd(o, masks)
            oh = o * r
            dm = dm_ref[rows, :]
            sg = _sigmoid(hg)
            dg_ref[rows, :] = (dm * oh * ng * sg * (1.0 + hg * (1.0 - sg))).astype(dg_ref.dtype)
            don = dm * hg * sg
            dng_ref[...] += jnp.sum(don * oh, axis=0, keepdims=True)
            doh = don * ng
            do = r * (doh - oh * _head_mean(doh * oh, masks))
            doms = [do * m for m in masks]
            qems = [qe * m for m in masks]
            scores = [_dot_nt(qem, ke) for qem in qems]
            dscores = [_dot_nt(dom, v) for dom in doms]
            prevs = [st_ref[0, 0, sb * per + c] for c in range(per)]
            dst_in = [_dot_tn(do[cr], qe[cr]) for cr in CHUNK_ROWS]
            dqe_i = [_dot(do[cr], prev) for cr, prev in zip(CHUNK_ROWS, prevs)]
            return dict(rows=rows, v=v, sig=sig, f=f, decs=decs, eb=eb, enb=enb, edb=edb, qe=qe, ke=ke, kd=kd,
                        doms=doms, qems=qems, scores=scores, dscores=dscores, prevs=prevs, dst_in=dst_in, dqe_i=dqe_i)

        def second_half(blk, dsts):
            v, qe, ke, kd = blk["v"], blk["qe"], blk["ke"], blk["kd"]
            ps = [jnp.where(lower, p, 0.0) for p in blk["scores"]]
            dps = [jnp.where(lower, dp, 0.0) for dp in blk["dscores"]]
            dqe_h = [_dot(dp, ke) for dp in dps]
            dke_h = [_dot_tn(dp, qem) for dp, qem in zip(dps, blk["qems"])]
            dv_h = [_dot_tn(p, dom) for p, dom in zip(ps, blk["doms"])]
            dus = [bd * d for d in dsts]
            dv_i = [_dot_nt(kd[cr], du) for cr, du in zip(CHUNK_ROWS, dus)]
            dkd_i = [_dot(v[cr], du) for cr, du in zip(CHUNK_ROWS, dus)]

            def finish():
                dqe = dqe_h[0] * masks[0] + dqe_h[1] * masks[1] + jnp.concatenate(blk["dqe_i"], axis=0)
                dke = dke_h[0] + dke_h[1]
                dv = dv_h[0] + dv_h[1] + jnp.concatenate(dv_i, axis=0)
                dkd = jnp.concatenate(dkd_i, axis=0)
                dk = dke * blk["enb"] + dkd * blk["edb"]
                db = dqe * qe - dke * ke - dkd * kd
                dkd_kd = dkd * kd
                dends = [jnp.sum(dkd_kd[cr], axis=0, keepdims=True)
                         + jnp.sum(dsts[c] * blk["prevs"][c], axis=0, keepdims=True) * blk["decs"][c]
                         for c, cr in enumerate(CHUNK_ROWS)]
                dlf = _rsum_left(upper_b, db, 2) + _over_chunks(dends)
                sig = blk["sig"]
                dfv = dlf / blk["f"] - dk
                rows = blk["rows"]
                dq_ref[rows, :] = (dqe * blk["eb"]).astype(dq_ref.dtype)
                di_ref[rows, :] = dv.astype(di_ref.dtype)
                df_ref[rows, :] = (dfv * (1.0 - lb) * sig * (1.0 - sig)).astype(df_ref.dtype)
                dlb = jnp.sum(dfv * (1.0 - sig), axis=0, keepdims=True)
                da0 = dlb * lb * (1.0 - lb)
                dlb_ref[0:1, :] += da0
                dlb_ref[1:2, :] -= da0

            return finish

        def step(it, dst):
            blocks = [first_half(n_super - 1 - HGRN_UNROLL * it - u) for u in range(HGRN_UNROLL)]
            all_dsts = []
            for blk in blocks:
                dsts = [None] * per
                for c in reversed(range(per)):
                    dsts[c] = dst
                    dst = bd * (dst * blk["decs"][c] + blk["dst_in"][c])
                all_dsts.append(dsts)
            for finish in [second_half(blk, dsts) for blk, dsts in zip(blocks, all_dsts)]:
                finish()
            return dst

        lax.fori_loop(0, n_super // HGRN_UNROLL, step, jnp.zeros((LANES, LANES), F32))

    def col(k):
        return pl.BlockSpec((seq, LANES), lambda p, b: (b, k * N_PAIRS + p))

    vec = lambda rows: pl.BlockSpec((rows, LANES), lambda p, b: (0, p))
    ospec = pl.BlockSpec((seq, LANES), lambda p, b: (b, p))
    piece = jax.ShapeDtypeStruct((t, GROUP), MXU_DTYPE)
    return pl.pallas_call(
        body, name="hgrn_bwd", grid=(N_PAIRS, n_seq),
        in_specs=[col(0), col(1), col(2), col(3), vec(2), vec(1), ospec, ospec,
                  pl.BlockSpec((1, 1, n_chunks, LANES, LANES), lambda p, b: (b, p, 0, 0, 0))],
        out_specs=[ospec, ospec, ospec, ospec, vec(2), vec(1)],
        out_shape=[piece, piece, piece, piece,
                   jax.ShapeDtypeStruct((2, GROUP), F32), jax.ShapeDtypeStruct((1, GROUP), F32)],
        compiler_params=_params(2),
    )(proj, proj, proj, proj, lower_bounds, norm_g, dmix, opre, states)


SB_SCALE = 1.0 / math.sqrt(D_HEAD)


SB_STEP = 2 * SB_BLOCK
QUERY_BLOCKS = (slice(0, SB_BLOCK), slice(SB_BLOCK, SB_STEP))


def _triangle(keep):
    row = lax.broadcasted_iota(jnp.int32, (SB_BLOCK, SB_BLOCK), 0)
    col = lax.broadcasted_iota(jnp.int32, (SB_BLOCK, SB_BLOCK), 1)
    return _ones_where(keep(row, col))


def _tile_masks():
    r = lax.broadcasted_iota(jnp.int32, (SB_BLOCK, SB_BLOCK), 0)
    c = lax.broadcasted_iota(jnp.int32, (SB_BLOCK, SB_BLOCK), 1)
    return r, c


def _sb_fwd(proj, norm_g, mixed, shards, n_seq, seq):
    t = n_seq * seq
    nq = seq // SB_STEP
    q0, k0, v0 = 0, N_PAIRS, 2 * N_PAIRS
    n_w = len(shards)
    n_steps = N_PAIRS * n_seq * nq
    tri = _triangle(lambda row, col: row >= col)

    def body(q_ref, k_ref, v_ref, ng_ref, tri_ref, mixed_in, *rest):
        del mixed_in
        shard_refs = rest[:n_w]
        out_ref, opre_ref, ctot_ref = rest[n_w:n_w + 3]
        gathered = rest[n_w + 3:2 * n_w + 3]
        send_sems, recv_sems, local_sems = rest[2 * n_w + 3:]
        i = pl.program_id(2)
        step = (pl.program_id(0) * n_seq + pl.program_id(1)) * nq + i
        plan = _GatherPlan(shard_refs, gathered, send_sems, recv_sems, local_sems)

        @pl.when(step == 0)
        def _():
            plan.start()

        @pl.when(step == (3 * n_steps) // 4)
        def _():
            plan.forward()

        masks = _head_masks()
        r, c = _tile_masks()
        strict = c < r
        suffix = tri_ref[...]
        qhs = [[(q_ref[blk, :] * SB_SCALE * m).astype(MXU_DTYPE) for m in masks] for blk in QUERY_BLOCKS]

        def tiles(js, work, carry):
            rows = [pl.ds(pl.multiple_of(j * SB_BLOCK, SB_BLOCK), SB_BLOCK) for j in js]
            ks = [k_ref[rw, :].astype(MXU_DTYPE) for rw in rows]
            vs = [v_ref[rw, :].astype(MXU_DTYPE) for rw in rows]
            zs = [[_dot_nt(qh, ks[ts]) for qh in qhs[qb]] for qb, ts, _ in work]
            ccs = [[_softplus(z) for z in zw] for zw in zs]
            ccs = [[jnp.where(strict, cc, 0.0) for cc in cw] if diag else cw for cw, (_, _, diag) in zip(ccs, work)]
            sums = [[jnp.dot(cc.astype(BF16), suffix, preferred_element_type=F32) for cc in cw] for cw in ccs]
            out = [list(per_block) for per_block in carry]
            for h in range(len(masks)):
                for w, (qb, ts, diag) in enumerate(work):
                    run, acc = out[qb][h]
                    a = jnp.exp(zs[w][h] - (sums[w][h] + run))
                    if diag:
                        a = jnp.where(strict, a, 0.0)
                    out[qb][h] = (run + sums[w][h][:, 0:1], acc + _dot(a, vs[ts]))
            return tuple(tuple(per_block) for per_block in out)

        start = ((jnp.zeros((SB_BLOCK, 1), F32), jnp.zeros((SB_BLOCK, LANES), F32)),) * 2
        carry = tiles([2 * i, 2 * i + 1], [(0, 0, True), (1, 1, True), (1, 0, False)], (start, start))
        both = [(0, 0, False), (0, 1, False), (1, 0, False), (1, 1, False)]
        carry = lax.fori_loop(0, i, lambda s, cy: tiles([2 * (i - s) - 1, 2 * (i - s) - 2], both, cy), carry)
        opre = jnp.concatenate([cb[0][1] * masks[0] + cb[1][1] * masks[1] for cb in carry], axis=0)
        ctot = jnp.concatenate([cb[0][0] * masks[0] + cb[1][0] * masks[1] for cb in carry], axis=0)
        opre_ref[...] = opre
        ctot_ref[...] = ctot
        out_ref[...] = (opre * _head_rstd(opre, masks) * ng_ref[...]).astype(out_ref.dtype)

        @pl.when(step == n_steps - 1)
        def _():
            plan.finish()

    qspec = pl.BlockSpec((SB_STEP, LANES), lambda p, b, i: (b * nq + i, q0 + p))
    ospec = pl.BlockSpec((SB_STEP, LANES), lambda p, b, i: (b * nq + i, p))
    hbm = pl.BlockSpec(memory_space=pltpu.HBM)
    outs = pl.pallas_call(
        body, name="sb_fwd", grid=(N_PAIRS, n_seq, nq),
        in_specs=[qspec,
                  pl.BlockSpec((seq, LANES), lambda p, b, i: (b, k0 + p)),
                  pl.BlockSpec((seq, LANES), lambda p, b, i: (b, v0 + p)),
                  pl.BlockSpec((1, LANES), lambda p, b, i: (0, p)),
                  pl.BlockSpec(tri.shape, lambda p, b, i: (0, 0)), hbm] + [hbm] * n_w,
        out_specs=[pl.BlockSpec((SB_STEP, LANES), lambda p, b, i: (b * nq + i, N_PAIRS + p)), ospec, ospec]
        + [hbm] * n_w,
        out_shape=[jax.ShapeDtypeStruct(mixed.shape, mixed.dtype), jax.ShapeDtypeStruct((t, GROUP), F32),
                   jax.ShapeDtypeStruct((t, GROUP), F32)]
        + [jax.ShapeDtypeStruct((N_DEV,) + s.shape, s.dtype) for s in shards],
        scratch_shapes=[pltpu.SemaphoreType.DMA((n_w * _GatherPlan.COPIES,)),
                        pltpu.SemaphoreType.DMA((n_w * _GatherPlan.COPIES,)), pltpu.SemaphoreType.DMA((n_w,))],
        input_output_aliases={5: 0},
        compiler_params=_params(3),
    )(proj, proj, proj, norm_g, tri, mixed, *shards)
    return outs[0], outs[1], outs[2], list(outs[3:])


def _sb_bwd(proj, norm_g, dmix, opre, ctot, grads, n_seq, seq):
    t = n_seq * seq
    nq = seq // SB_STEP
    q0, k0, v0 = 0, N_PAIRS, 2 * N_PAIRS
    n_w = len(grads)
    n_steps = N_PAIRS * n_seq * nq
    tri = _triangle(lambda row, col: row <= col)

    def body(q_ref, k_ref, v_ref, ng_ref, tri_ref, dm_ref, opre_ref, ctot_ref, *rest):
        grad_refs = rest[:n_w]
        dq_ref, dk_ref, dv_ref, dng_ref = rest[n_w:n_w + 4]
        lands = rest[n_w + 4:2 * n_w + 4]
        dk_acc, dv_acc, send_sems, recv_sems, local_sems = rest[2 * n_w + 4:]
        p_id, b_id, i = pl.program_id(0), pl.program_id(1), pl.program_id(2)
        step = (p_id * n_seq + b_id) * nq + i
        plan = _ScatterPlan(grad_refs, lands, send_sems, recv_sems, local_sems)

        @pl.when(step == 0)
        def _():
            plan.start()

        masks = _head_masks()
        r, c = _tile_masks()
        strict = c < r
        upto = tri_ref[...]

        def prefix(x):
            return jnp.dot(x.astype(BF16), upto, preferred_element_type=F32)

        @pl.when(i == 0)
        def _():
            dk_acc[...] = jnp.zeros_like(dk_acc)
            dv_acc[...] = jnp.zeros_like(dv_acc)

        @pl.when(jnp.logical_and(b_id == 0, i == 0))
        def _():
            dng_ref[...] = jnp.zeros_like(dng_ref)

        o = opre_ref[...]
        rs = _head_rstd(o, masks)
        oh = o * rs
        dm = dm_ref[...]
        dng_ref[...] += jnp.sum(dm * oh, axis=0, keepdims=True)
        doh = dm * ng_ref[...]
        do = rs * (doh - oh * _head_mean(doh * oh, masks))

        heads = range(len(masks))
        qs = [q_ref[blk, :] * SB_SCALE for blk in QUERY_BLOCKS]
        dos = [do[blk] for blk in QUERY_BLOCKS]
        qhs = [[(q * m).astype(MXU_DTYPE) for m in masks] for q in qs]
        doms = [[(d * m).astype(MXU_DTYPE) for m in masks] for d in dos]
        head_rows = [jnp.where(jnp.right_shift(lax.broadcasted_iota(jnp.int32, (LANES, 1), 0), 6) == h, 1.0, 0.0)
                     for h in heads]
        qhts = [[(qt * hr).astype(MXU_DTYPE) for hr in head_rows] for qt in [q.astype(F32).T for q in qs]]
        domts = [[(dt * hr).astype(MXU_DTYPE) for hr in head_rows] for dt in [d.T for d in dos]]
        totals = [[ctot_ref[blk, h * D_HEAD:h * D_HEAD + 1] for h in heads] for blk in QUERY_BLOCKS]

        def tiles(js, work, carry):
            rows = [pl.ds(pl.multiple_of(j * SB_BLOCK, SB_BLOCK), SB_BLOCK) for j in js]
            ks = [k_ref[rw, :].astype(MXU_DTYPE) for rw in rows]
            vs = [v_ref[rw, :].astype(MXU_DTYPE) for rw in rows]
            zs = [[_dot_nt(qhs[qb][h], ks[ts]) for h in heads] for qb, ts, _ in work]
            das = [[_dot_nt(doms[qb][h], vs[ts]) for h in heads] for qb, ts, _ in work]
            sps = [[_softplus(z) for z in zw] for zw in zs]
            lsigs = [[z - sp for z, sp in zip(zw, sw)] for zw, sw in zip(zs, sps)]
            sigs = [[jnp.exp(ls) for ls in lw] for lw in lsigs]
            ccs = [[jnp.where(strict, sp, 0.0) for sp in sw] if diag else sw for sw, (_, _, diag) in zip(sps, work)]
            pres = [[prefix(cc) for cc in cw] for cw in ccs]
            out = [list(per_block) for per_block in carry]
            for h in heads:
                for w, (qb, ts, diag) in enumerate(work):
                    pc, pdl, dq_h = out[qb][h]
                    a = jnp.exp(lsigs[w][h] + pres[w][h] - (totals[qb][h] - pc))
                    if diag:
                        a = jnp.where(strict, a, 0.0)
                    dl = a * das[w][h]
                    dv_acc[js[ts]] += _dot(domts[qb][h], a)
                    dpre = prefix(dl)
                    dz = dl - sigs[w][h] * (pdl + dpre)
                    if diag:
                        dz = jnp.where(strict, dz, 0.0)
                    dzb = dz.astype(MXU_DTYPE)
                    dk_acc[js[ts]] += _dot(qhts[qb][h], dzb)
                    out[qb][h] = (pc + pres[w][h][:, SB_BLOCK - 1:SB_BLOCK], pdl + dpre[:, SB_BLOCK - 1:SB_BLOCK],
                                  dq_h + _dot(dzb, ks[ts]))
            return tuple(tuple(per_block) for per_block in out)

        zero = jnp.zeros((SB_BLOCK, 1), F32)
        start = ((zero, zero, jnp.zeros((SB_BLOCK, LANES), F32)),) * 2
        both = [(0, 0, False), (0, 1, False), (1, 0, False), (1, 1, False)]
        carry = lax.fori_loop(0, i, lambda s, cy: tiles([2 * s, 2 * s + 1], both, cy), (start, start))
        carry = tiles([2 * i, 2 * i + 1], [(0, 0, True), (1, 0, False), (1, 1, True)], carry)
        dq = jnp.concatenate([cb[0][2] * masks[0] + cb[1][2] * masks[1] for cb in carry], axis=0)
        dq_ref[...] = (dq * SB_SCALE).astype(dq_ref.dtype)

        @pl.when(i == nq - 1)
        def _():
            for j in range(seq // SB_BLOCK):
                tile_rows = slice(j * SB_BLOCK, (j + 1) * SB_BLOCK)
                dk_ref[tile_rows, :] = dk_acc[j].T.astype(dk_ref.dtype)
                dv_ref[tile_rows, :] = dv_acc[j].T.astype(dv_ref.dtype)

        @pl.when(step == n_steps - 1)
        def _():
            plan.finish()

    qspec = pl.BlockSpec((SB_STEP, LANES), lambda p, b, i: (b * nq + i, q0 + p))
    ospec = pl.BlockSpec((SB_STEP, LANES), lambda p, b, i: (b * nq + i, p))
    dmspec = pl.BlockSpec((SB_STEP, LANES), lambda p, b, i: (b * nq + i, N_PAIRS + p))
    full = lambda k: pl.BlockSpec((seq, LANES), lambda p, b, i: (b, k + p))
    vec = pl.BlockSpec((1, LANES), lambda p, b, i: (0, p))
    hbm = pl.BlockSpec(memory_space=pltpu.HBM)
    piece = jax.ShapeDtypeStruct((t, GROUP), MXU_DTYPE)
    outs = pl.pallas_call(
        body, name="sb_bwd", grid=(N_PAIRS, n_seq, nq),
        in_specs=[qspec, full(k0), full(v0), vec, pl.BlockSpec(tri.shape, lambda p, b, i: (0, 0)), dmspec, ospec, ospec]
        + [hbm] * n_w,
        out_specs=[ospec, full(0), full(0), vec] + [hbm] * n_w,
        out_shape=[piece, piece, piece, jax.ShapeDtypeStruct((1, GROUP), F32)]
        + [jax.ShapeDtypeStruct(g.shape, g.dtype) for g in grads],
        scratch_shapes=[pltpu.VMEM((seq // SB_BLOCK, LANES, SB_BLOCK), F32),
                        pltpu.VMEM((seq // SB_BLOCK, LANES, SB_BLOCK), F32),
                        pltpu.SemaphoreType.DMA((n_w * (N_DEV - 1),)), pltpu.SemaphoreType.DMA((n_w * (N_DEV - 1),)),
                        pltpu.SemaphoreType.DMA((n_w,))],
        compiler_params=_params(3),
    )(proj, proj, proj, norm_g, tri, dmix, opre, ctot, *grads)
    return outs[0], outs[1], outs[2], outs[3], list(outs[4:])


def _mesh_place():
    x, y, c = lax.axis_index("x"), lax.axis_index("y"), lax.axis_index("c")
    return x, y, c


def _peer(x, y, c, k):
    px = lax.rem(x + ((k >> 2) & 1), 2)
    py = lax.rem(y + ((k >> 1) & 1), 2)
    pc = lax.rem(c + (k & 1), 2)
    return (px, py, pc), 4 * px + 2 * py + pc


def _remote(src, dst, send_sem, recv_sem, to):
    return pltpu.make_async_remote_copy(src_ref=src, dst_ref=dst, send_sem=send_sem, recv_sem=recv_sem,
                                        device_id=to, device_id_type=pl.DeviceIdType.MESH)


class _GatherPlan:
    COPIES = 7

    def __init__(self, shards, gathered, send_sems, recv_sems, local_sems):
        x, y, c = _mesh_place()
        self.c = c
        self.me = (x, y, c)
        self.sibling = (x, y, 1 - c)
        self.chips = [(1 - x, y), (x, 1 - y), (1 - x, 1 - y)]
        self.tensors = list(zip(shards, gathered))
        self.send_sems, self.recv_sems, self.local_sems = send_sems, recv_sems, local_sems

    @staticmethod
    def _index(place):
        return 4 * place[0] + 2 * place[1] + place[2]

    def _copy(self, w, k, block, to, own=False):
        shard, gathered = self.tensors[w]
        slot = gathered.at[self._index(block)]
        n = w * self.COPIES + k
        return _remote(shard if own else slot, slot, self.send_sems.at[n], self.recv_sems.at[n], to)

    def _local(self, w):
        shard, gathered = self.tensors[w]
        return pltpu.make_async_copy(shard, gathered.at[self._index(self.me)], self.local_sems.at[w])

    def _first(self, w):
        return [self._copy(w, 0, self.me, self.sibling, own=True)] + [
            self._copy(w, 1 + j, self.me, (*chip, self.c), own=True) for j, chip in enumerate(self.chips)]

    def _passed(self, w):
        return [self._copy(w, 4 + j, (*chip, self.c), self.sibling) for j, chip in enumerate(self.chips)]

    def start(self):
        for w in range(len(self.tensors)):
            self._local(w).start()
            for cp in self._first(w):
                cp.start()

    def forward(self):
        for w in range(len(self.tensors)):
            passed = self._passed(w)
            for j, chip in enumerate(self.chips):
                self._copy(w, 1 + j, (*chip, self.c), self.me).wait_recv()
                passed[j].start()

    def finish(self):
        for w in range(len(self.tensors)):
            self._copy(w, 0, self.sibling, self.me).wait_recv()
            for j, chip in enumerate(self.chips):
                self._copy(w, 4 + j, (*chip, 1 - self.c), self.me).wait_recv()
            for cp in self._first(w) + self._passed(w):
                cp.wait_send()
            self._local(w).wait()


class _ScatterPlan:
    def __init__(self, grads, lands, send_sems, recv_sems, local_sems):
        self.place = _mesh_place()
        x, y, c = self.place
        self.me = 4 * x + 2 * y + c
        self.tensors = list(zip(grads, lands))
        self.send_sems, self.recv_sems, self.local_sems = send_sems, recv_sems, local_sems

    def _copies(self, w):
        grad, land = self.tensors[w]
        out = []
        for k in range(1, N_DEV):
            peer, pidx = _peer(*self.place, k)
            n = w * (N_DEV - 1) + k - 1
            sems = (self.send_sems.at[n], self.recv_sems.at[n], peer)
            out.append((_remote(grad.at[pidx], land.at[self.me], *sems), _remote(grad.at[pidx], land.at[pidx], *sems)))
        return out

    def _local(self, w):
        grad, land = self.tensors[w]
        return pltpu.make_async_copy(grad.at[self.me], land.at[self.me], self.local_sems.at[w])

    def start(self):
        for w in range(len(self.tensors)):
            self._local(w).start()
            for send, _ in self._copies(w):
                send.start()

    def finish(self):
        for w in range(len(self.tensors)):
            copies = self._copies(w)
            for _, arrival in copies:
                arrival.wait_recv()
            for send, _ in copies:
                send.wait_send()
            self._local(w).wait()


def _cast_shards(shards):
    def body(*refs):
        n = len(refs) // 2
        for src, dst in zip(refs[:n], refs[n:]):
            dst[...] = src[...].astype(dst.dtype)

    vmem = pl.BlockSpec(memory_space=pltpu.VMEM)
    return pl.pallas_call(
        body, name="cast_shards", in_specs=[vmem] * len(shards), out_specs=[vmem] * len(shards),
        out_shape=[jax.ShapeDtypeStruct(s.shape, BF16) for s in shards],
        compiler_params=pltpu.CompilerParams(vmem_limit_bytes=VMEM_LIMIT),
    )(*shards)


def _gather_w_in(shard, x, g, chunk=512):
    rows, cols = shard.shape
    t, d = x.shape
    chunk = min(chunk, t)
    n_chunks = t // chunk

    def body(w_ref, x_ref, g_ref, out_ref, h_ref, x_buf, send_sems, recv_sems, local_sems, x_sems):
        plan = _GatherPlan([w_ref], [out_ref], send_sems, recv_sems, local_sems)
        plan.start()

        def fetch(j):
            return pltpu.make_async_copy(x_ref.at[pl.ds(j * chunk, chunk)], x_buf.at[j % 2], x_sems.at[j % 2])

        fetch(0).start()
        gv = g_ref[...]
        for j in range(n_chunks):
            if j + 1 < n_chunks:
                fetch(j + 1).start()
            fetch(j).wait()
            xv = x_buf[j % 2]
            r = lax.rsqrt(jnp.mean(xv * xv, axis=-1, keepdims=True) + EPS)
            h_ref[j * chunk:(j + 1) * chunk, :] = (xv * r * gv).astype(h_ref.dtype)
        plan.forward()
        plan.finish()

    vmem = pl.BlockSpec(memory_space=pltpu.VMEM)
    return pl.pallas_call(
        body, name="gather_w_in", in_specs=[vmem, pl.BlockSpec(memory_space=pltpu.HBM), vmem], out_specs=[vmem, vmem],
        out_shape=[jax.ShapeDtypeStruct((N_DEV, rows, cols), shard.dtype), jax.ShapeDtypeStruct((t, d), MXU_DTYPE)],
        scratch_shapes=[pltpu.VMEM((2, chunk, d), F32), pltpu.SemaphoreType.DMA((_GatherPlan.COPIES,)),
                        pltpu.SemaphoreType.DMA((_GatherPlan.COPIES,)), pltpu.SemaphoreType.DMA((1,)),
                        pltpu.SemaphoreType.DMA((2,))],
        compiler_params=pltpu.CompilerParams(vmem_limit_bytes=VMEM_LIMIT),
    )(shard, x, g)


def _dw_rows(pieces, b, name, tn=512, tt=1024):
    t, n = b.shape
    widths = [p.shape[1] for p in pieces]
    rows = sum(widths)
    tt = min(tt, t)
    steps = t // tt
    n_p = len(pieces)

    def body(*refs):
        piece_refs, b_ref, o_ref, acc = refs[:n_p], refs[n_p], refs[n_p + 1], refs[n_p + 2]
        s = pl.program_id(1)

        @pl.when(s == 0)
        def _():
            acc[...] = jnp.zeros_like(acc)

        bv = b_ref[...]
        off = 0
        for p_ref, width in zip(piece_refs, widths):
            acc[off:off + width, :] += _dot_tn(p_ref[...], bv)
            off += width

        @pl.when(s == steps - 1)
        def _():
            o_ref[...] = acc[...].astype(o_ref.dtype)

    return pl.pallas_call(
        body, name=name, grid=(n // tn, steps),
        in_specs=[pl.BlockSpec((tt, width), lambda j, s: (s, 0)) for width in widths]
        + [pl.BlockSpec((tt, tn), lambda j, s: (s, j))],
        out_specs=pl.BlockSpec((rows, tn), lambda j, s: (0, j)),
        out_shape=jax.ShapeDtypeStruct((rows, n), BF16),
        scratch_shapes=[pltpu.VMEM((rows, tn), F32)],
        compiler_params=_params(2),
    )(*pieces, b)


def _dh_norm_bwd(pairs, x, g, res, name, tm=512, after=(), next_w=None):
    m, d = x.shape
    tm = min(tm, m)
    n_p = len(pairs)
    n_steps = m // tm
    n_in = 2 * n_p + 3 + len(after) + (next_w is not None)

    def body(*refs):
        x_ref, g_ref, res_ref = refs[2 * n_p:2 * n_p + 3]
        dx_ref, dg_ref = refs[n_in:n_in + 2]

        @pl.when(pl.program_id(0) == 0)
        def _():
            dg_ref[...] = jnp.zeros_like(dg_ref)

        dh = None
        for q in range(n_p):
            part = _dot(refs[2 * q][...], refs[2 * q + 1][...])
            dh = part if dh is None else dh + part
        xv = x_ref[...]
        r = lax.rsqrt(jnp.mean(xv * xv, axis=-1, keepdims=True) + EPS)
        xh = xv * r
        dxh = dh * g_ref[...]
        dxv = res_ref[...] + r * (dxh - xh * jnp.mean(dxh * xh, axis=-1, keepdims=True))
        dx_ref[...] = dxv
        dg_ref[...] += jnp.sum(dh * xh, axis=0, keepdims=True)
        if next_w is not None:
            refs[n_in + 2][...] = _dot_nt(dxv, refs[n_in - 1][...])

    in_specs, args = [], []
    for a, w, r in pairs:
        k = a.shape[1]
        in_specs += [pl.BlockSpec((tm, k), lambda i: (i, 0)),
                     pl.BlockSpec((k, d), functools.partial(lambda i, r: (r, 0), r=r or 0),
                                  pipeline_mode=pl.Buffered(1))]
        args += [a, w]
    row = pl.BlockSpec((tm, d), lambda i: (i, 0))
    vec = pl.BlockSpec((1, d), lambda i: (0, 0))
    extra_in, extra_out, extra_shape = [], [], []
    if next_w is not None:
        e = next_w.shape[0]
        extra_in = [pl.BlockSpec((e, d), lambda i: (0, 0), pipeline_mode=pl.Buffered(1))]
        extra_out = [pl.BlockSpec((tm, e), lambda i: (i, 0))]
        extra_shape = [jax.ShapeDtypeStruct((m, e), F32)]
        args_tail = (next_w,)
    else:
        args_tail = ()
    return pl.pallas_call(
        body, name=name, grid=(n_steps,),
        in_specs=in_specs + [row, vec, row] + [pl.BlockSpec(memory_space=pl.ANY)] * len(after) + extra_in,
        out_specs=[row, vec] + extra_out,
        out_shape=[jax.ShapeDtypeStruct((m, d), F32), jax.ShapeDtypeStruct((1, d), F32)] + extra_shape,
        compiler_params=_params(1),
    )(*args, x, g, res, *after, *args_tail)


def _exchange_start(grad, name):
    def body(g_ref, land_ref, send_sem, recv_sem, local_sem, g_thru, land_thru, token):
        del g_thru, land_thru
        x, y, c = _mesh_place()
        me = 4 * x + 2 * y + c
        pltpu.make_async_copy(g_ref.at[me], land_ref.at[me], local_sem).start()
        for k in range(1, N_DEV):
            peer, pidx = _peer(x, y, c, k)
            _remote(g_ref.at[pidx], land_ref.at[me], send_sem, recv_sem, peer).start()
        token[...] = jnp.zeros_like(token)

    hbm = pl.BlockSpec(memory_space=pltpu.HBM)
    sem = pl.BlockSpec(memory_space=pltpu.SEMAPHORE)
    buf = pltpu.HBM(grad.shape, grad.dtype)
    return pl.pallas_call(
        body, name=name,
        out_shape=(pltpu.SemaphoreType.DMA(()), pltpu.SemaphoreType.DMA(()), pltpu.SemaphoreType.DMA(()), buf, buf,
                   jax.ShapeDtypeStruct((8, LANES), F32)),
        in_specs=(hbm, hbm), out_specs=(sem, sem, sem, hbm, hbm, pl.BlockSpec(memory_space=pltpu.VMEM)),
        input_output_aliases={0: 3, 1: 4},
        compiler_params=pltpu.CompilerParams(has_side_effects=pltpu.SideEffectType.DATAFLOW_SIDE_EFFECTING),
    )(pltpu.with_memory_space_constraint(grad, pltpu.HBM),
      pltpu.with_memory_space_constraint(lax.empty(grad.shape, grad.dtype), pltpu.HBM))


def _exchange_wait(send_sem, recv_sem, local_sem, grad, land, after, name):
    def body(g_ref, land_ref, send_sem, recv_sem, local_sem, *rest):
        x, y, c = _mesh_place()
        peer, _ = _peer(x, y, c, 1)
        others = pl.ds(0, N_DEV - 1)
        seven = _remote(g_ref.at[others], land_ref.at[others], send_sem, recv_sem, peer)
        seven.wait_send()
        seven.wait_recv()
        pltpu.make_async_copy(g_ref.at[0], land_ref.at[0], local_sem).wait()

    hbm = pl.BlockSpec(memory_space=pltpu.HBM)
    sem = pl.BlockSpec(memory_space=pltpu.SEMAPHORE)
    buf = pltpu.HBM(grad.shape, grad.dtype)
    return pl.pallas_call(
        body, name=name, out_shape=(buf, buf),
        in_specs=(hbm, hbm, sem, sem, sem) + (pl.BlockSpec(memory_space=pl.ANY),) * len(after), out_specs=(hbm, hbm),
        input_output_aliases={0: 0, 1: 1},
        compiler_params=pltpu.CompilerParams(has_side_effects=pltpu.SideEffectType.DATAFLOW_SIDE_EFFECTING),
    )(grad, land, send_sem, recv_sem, local_sem, *after)[1]


SMALL_LAYOUT = ((0, 0, 0, 0, D_MODEL), (1, 0, 1, 0, GROUP), (1, 1, 1, GROUP, GROUP), (2, 0, 2, 0, GROUP),
                (3, 0, 2, GROUP, GROUP), (4, 0, 3, 0, D_MODEL), (5, 0, 4, 0, D_MODEL))
LOSS_ROW = 5
N_SMALL = 6


def _small_step(grads, loss_part, ws, ms, vs):
    def body(*refs):
        g_in, loss_in = refs[:N_SMALL], refs[N_SMALL]
        params = [refs[1 + (q + 1) * N_SMALL:1 + (q + 2) * N_SMALL] for q in range(3)]
        o0 = 1 + 4 * N_SMALL
        outs = [refs[o0 + q * N_SMALL:o0 + (q + 1) * N_SMALL] for q in range(4)]
        loss_out = refs[o0 + 4 * N_SMALL]
        pack, land, wp, mp, vp, send_sems, recv_sems = refs[o0 + 4 * N_SMALL + 1:]

        def place(dst, srcs):
            dst[...] = jnp.zeros_like(dst)
            for p, sr, dr, dc, width in SMALL_LAYOUT:
                dst[dr:dr + 1, dc:dc + width] = srcs[p][sr:sr + 1, :]

        place(pack, g_in)
        pack[LOSS_ROW:LOSS_ROW + 1, 0:LANES] = loss_in[...]
        for dst, srcs in zip((wp, mp, vp), params):
            place(dst, srcs)

        x, y, c = _mesh_place()
        me = 4 * x + 2 * y + c
        land[me] = pack[...]
        sends = []
        for k in range(1, N_DEV):
            peer, _ = _peer(x, y, c, k)
            cp = _remote(pack, land.at[me], send_sems.at[k - 1], recv_sems.at[k - 1], peer)
            cp.start()
            sends.append(cp)
        for k in range(1, N_DEV):
            peer, pidx = _peer(x, y, c, k)
            _remote(pack, land.at[pidx], send_sems.at[k - 1], recv_sems.at[k - 1], peer).wait_recv()
        for cp in sends:
            cp.wait_send()

        g = land[0]
        for d in range(1, N_DEV):
            g = g + land[d]
        delta, nm, nv = _adam(wp[...], g, mp[...], vp[...])
        for val, out in zip((g, delta, nm, nv), outs):
            for p, sr, dr, dc, width in SMALL_LAYOUT:
                out[p][sr:sr + 1, :] = val[dr:dr + 1, dc:dc + width]
        loss_out[...] = g[LOSS_ROW:LOSS_ROW + 1, 0:LANES]

    vmem = pl.BlockSpec(memory_space=pltpu.VMEM)
    n_in = 1 + 4 * N_SMALL
    shapes = [jax.ShapeDtypeStruct(w.shape, F32) for w in ws]
    packed = pltpu.VMEM((SMALL_ROWS, D_MODEL), F32)
    outs = pl.pallas_call(
        body, name="small_step", in_specs=[vmem] * n_in, out_specs=[vmem] * (4 * N_SMALL + 1),
        out_shape=shapes * 4 + [jax.ShapeDtypeStruct((1, LANES), F32)],
        scratch_shapes=[packed, pltpu.VMEM((N_DEV, SMALL_ROWS, D_MODEL), F32), packed, packed, packed,
                        pltpu.SemaphoreType.DMA((N_DEV - 1,)), pltpu.SemaphoreType.DMA((N_DEV - 1,))],
    )(*grads, loss_part, *ws, *ms, *vs)
    return [outs[q * N_SMALL:(q + 1) * N_SMALL] for q in range(4)], outs[4 * N_SMALL]


def _adam(w, g, m, v):
    m = ADAM_B1 * m + (1.0 - ADAM_B1) * g
    v = ADAM_B2 * v + (1.0 - ADAM_B2) * (g * g)
    m_hat = m / (1.0 - ADAM_B1 ** ADAM_STEP)
    v_hat = v / (1.0 - ADAM_B2 ** ADAM_STEP)
    delta = -ADAM_LR * (m_hat / (jnp.sqrt(v_hat) + ADAM_EPS) + ADAM_WD * w)
    return delta, m, v


def _reduce_adamw(land, w, m, v, name, tr, after=()):
    _, rows, width = land.shape

    def body(land_ref, w_ref, m_ref, v_ref, *rest):
        g_ref, d_ref, nm_ref, nv_ref = rest[len(after):]
        g = land_ref[0].astype(F32)
        for d in range(1, N_DEV):
            g = g + land_ref[d].astype(F32)
        delta, nm, nv = _adam(w_ref[...], g, m_ref[...], v_ref[...])
        g_ref[...] = g
        d_ref[...] = delta
        nm_ref[...] = nm
        nv_ref[...] = nv

    row = pl.BlockSpec((tr, width), lambda i: (i, 0))
    out = jax.ShapeDtypeStruct((rows, width), F32)
    return pl.pallas_call(
        body, name=name, grid=(rows // tr,),
        in_specs=[pl.BlockSpec((N_DEV, tr, width), lambda i: (0, i, 0)), row, row, row]
        + [pl.BlockSpec(memory_space=pl.ANY)] * len(after),
        out_specs=[row, row, row, row], out_shape=[out, out, out, out],
        compiler_params=_params(1),
    )(land, w, m, v, *after)


def kernel(x, mix_norm_g, w_in, lower_bounds, hgrn_norm_g, sb_norm_g, w_out, ffn_norm_g, w_gate, w_up, w_down, final_norm_g, loss_target, m_mix_norm_g, m_w_in, m_lower_bounds, m_hgrn_norm_g, m_sb_norm_g, m_w_out, m_ffn_norm_g, m_w_gate, m_w_up, m_w_down, m_final_norm_g, v_mix_norm_g, v_w_in, v_lower_bounds, v_hgrn_norm_g, v_sb_norm_g, v_w_out, v_ffn_norm_g, v_w_gate, v_w_up, v_w_down, v_final_norm_g):
    n_seq, seq, d = x.shape
    t = n_seq * seq
    x2d = x.reshape(t, d)
    tgt = loss_target.reshape(t, d)
    final_g = final_norm_g.reshape(1, d)
    col_sharded = (True, False, True, True, False)

    def as_rows(ws):
        return [w[0].T if tr else w[0] for w, tr in zip(ws, col_sharded)]

    big_w = as_rows([w_in, w_out, w_gate, w_up, w_down])
    big_m = as_rows([m_w_in, m_w_out, m_w_gate, m_w_up, m_w_down])
    big_v = as_rows([v_w_in, v_w_out, v_w_gate, v_w_up, v_w_down])

    sh_in, sh_out, sh_gate, sh_up, sh_down = _cast_shards(big_w)
    wt_in, h1 = _gather_w_in(sh_in, x2d, mix_norm_g)
    wt_in = wt_in.reshape(IN_COLS, d)
    proj_h = _mm_nt(h1, wt_in, "proj_hgrn", rows=(0, 4 * GROUP), tm=1024, tk=1024)
    proj_s = _mm_nt(h1, wt_in, "proj_sb", rows=(4 * GROUP, 3 * GROUP), out_dtype=MXU_DTYPE, tm=1024, tk=512)
    mixed, oa_pre, states = _hgrn_fwd(proj_h, lower_bounds, hgrn_norm_g, n_seq, seq)
    mixed, ob_pre, ctot, gathered = _sb_fwd(proj_s, sb_norm_g, mixed, [sh_out, sh_gate, sh_up, sh_down], n_seq, seq)
    wf_out = gathered[0].reshape(d, d)
    wt_gate = gathered[1].reshape(D_FF, d)
    wt_up = gathered[2].reshape(D_FF, d)
    wf_down = gathered[3].reshape(D_FF, d)
    x1, h2 = _mix_out_norm(mixed, wf_out, x2d, ffn_norm_g, "mix_out")
    gate, up, ff = _ffn_up(h2, wt_gate, wt_up, "ffn_up")
    dx2, dx2m, d_final_g, loss_part = _ffn_down_loss(ff, wf_down, x1, tgt, final_g, "ffn_down_loss")

    dgate, dup = _ffn_bwd_act(dx2m, wf_down, gate, up, "ffn_bwd_act")
    dw_down = _mm_tn(ff, dx2m, "dw_down", tk=1408, tn=1024).reshape(N_DEV, D_FF // N_DEV, d)
    dw_gate = _mm_tn(dgate, h2, "dw_gate", tk=1408, tn=1024).reshape(N_DEV, D_FF // N_DEV, d)
    dw_up = _mm_tn(dup, h2, "dw_up", tk=1408, tn=1024).reshape(N_DEV, D_FF // N_DEV, d)
    dx1, d_ffn_g, dmix = _dh_norm_bwd([(dgate, wt_gate, None), (dup, wt_up, None)], x1, ffn_norm_g, dx2, "dh_ffn",
                                      tm=256, next_w=wf_out)
    dw_out = _mm_tn(mixed, dx1, "dw_out", tk=512, tn=1024).reshape(N_DEV, d // N_DEV, d)
    dsq, dsk, dsv, d_sb_g, lands = _sb_bwd(proj_s, sb_norm_g, dmix, ob_pre, ctot, [dw_out, dw_gate, dw_up, dw_down],
                                            n_seq, seq)
    dhq, dhf, dhi, dhg, d_lb, d_hgrn_g = _hgrn_bwd(proj_h, lower_bounds, hgrn_norm_g, dmix, oa_pre, states, n_seq,
                                                   seq)
    dproj = [dhq, dhf, dhi, dhg, dsq, dsk, dsv]
    dw_in = _dw_rows(dproj, h1, "dw_in").reshape(N_DEV, IN_COLS // N_DEV, d)
    send_sem, recv_sem, local_sem, dw_in, land_in, token = _exchange_start(dw_in, "dw_in_send")
    dx, d_mix_g = _dh_norm_bwd([(piece, wt_in, k) for k, piece in enumerate(dproj)], x2d, mix_norm_g, dx1, "dh_mix",
                               after=(token,))

    tiles = {"in": 224, "out": 128, "gate": 176, "up": 176, "down": 176}
    keys = list(tiles)
    rest = [_reduce_adamw(land, w, m, v, "adamw_" + key, tr=tiles[key], after=(token,))
            for key, land, w, m, v in zip(keys[1:], lands, big_w[1:], big_m[1:], big_v[1:])]
    small, loss_row = _small_step(
        [d_mix_g, d_lb, d_hgrn_g, d_sb_g, d_ffn_g, d_final_g], loss_part,
        [mix_norm_g, lower_bounds, hgrn_norm_g, sb_norm_g, ffn_norm_g, final_g],
        [m_mix_norm_g, m_lower_bounds, m_hgrn_norm_g, m_sb_norm_g, m_ffn_norm_g, m_final_norm_g.reshape(1, d)],
        [v_mix_norm_g, v_lower_bounds, v_hgrn_norm_g, v_sb_norm_g, v_ffn_norm_g, v_final_norm_g.reshape(1, d)])
    land_in = _exchange_wait(send_sem, recv_sem, local_sem, dw_in, land_in,
                             [dx, loss_row] + [res[0] for res in rest], "dw_in_await")
    big = [_reduce_adamw(land_in, big_w[0], big_m[0], big_v[0], "adamw_in", tr=tiles["in"])] + rest
    big = [[r.T if tr else r for r in res] for res, tr in zip(big, col_sharded)]

    outs = [loss_row[0, 0], dx.reshape(n_seq, seq, d)]
    for q in range(4):
        b_in, b_out, b_gate, b_up, b_down = [res[q][None] for res in big]
        s_mix, s_lb, s_hgrn, s_sb, s_ffn, s_final = small[q]
        outs += [s_mix, b_in, s_lb, s_hgrn, s_sb, b_out, s_ffn, b_gate, b_up, b_down, s_final.reshape(d)]
    return tuple(outs)
```

```python
import functools
import math

import jax
import jax.numpy as jnp
from jax import lax
from jax.experimental import pallas as pl
from jax.experimental.pallas import tpu as pltpu

F32 = jnp.float32
BF16 = jnp.bfloat16
MXU_DTYPE = BF16

EPS = 1e-6
D_MODEL = 1024
N_HEADS = 8
D_HEAD = 64
GROUP = N_HEADS * D_HEAD
IN_COLS = 7 * GROUP
D_FF = 2816
CHUNK = 64
LANES = 128
N_PAIRS = GROUP // LANES
SUPER = 256
SB_BLOCK = 256
N_DEV = 8

ADAM_LR = 0.001
ADAM_B1 = 0.9
ADAM_B2 = 0.999
ADAM_EPS = 1e-08
ADAM_WD = 0.01
ADAM_STEP = 10

SMALL_ROWS = 8
FF_TILE = D_FF // 2

VMEM_LIMIT = 48 * 1024 * 1024


def _params(n_axes, vmem=VMEM_LIMIT):
    return pltpu.CompilerParams(dimension_semantics=("arbitrary",) * n_axes, vmem_limit_bytes=vmem)


def _dot(a, b):
    return jnp.dot(a.astype(MXU_DTYPE), b.astype(MXU_DTYPE), preferred_element_type=F32)


def _dot_nt(a, b):
    return lax.dot_general(a.astype(MXU_DTYPE), b.astype(MXU_DTYPE), (((1,), (1,)), ((), ())),
                           preferred_element_type=F32)


def _dot_tn(a, b):
    return lax.dot_general(a.astype(MXU_DTYPE), b.astype(MXU_DTYPE), (((0,), (0,)), ((), ())),
                           preferred_element_type=F32)


def _split(x, parts):
    out, r = [], x
    for _ in range(parts):
        h = r.astype(BF16)
        out.append(h)
        r = r - h.astype(F32)
    return out


def _rsum_left(u, x, parts):
    acc = None
    for h in _split(x, parts):
        d = jnp.dot(u, h, preferred_element_type=F32)
        acc = d if acc is None else acc + d
    return acc


def _ones_where(mask):
    return jnp.where(mask, 1.0, 0.0).astype(BF16)


def _sigmoid(x):
    return 1.0 / (1.0 + jnp.exp(-x))


def _softplus(x):
    return jnp.maximum(x, 0.0) + jnp.log(1.0 + jnp.exp(-jnp.abs(x)))


def _head_masks():
    lane = lax.broadcasted_iota(jnp.int32, (1, LANES), 1)
    return [jnp.where(lane < D_HEAD, 1.0, 0.0), jnp.where(lane >= D_HEAD, 1.0, 0.0)]


def _head_rstd(o, masks):
    sq = o * o
    r = None
    for m in masks:
        ms = jnp.sum(sq * m, axis=1, keepdims=True) * (1.0 / D_HEAD)
        t = lax.rsqrt(ms + EPS) * m
        r = t if r is None else r + t
    return r


def _head_mean(t, masks):
    out = None
    for m in masks:
        v = jnp.sum(t * m, axis=1, keepdims=True) * (1.0 / D_HEAD) * m
        out = v if out is None else out + v
    return out


def _mix_out_norm(a, w, res, g, name, tm=512):
    m, k = a.shape
    d = w.shape[1]
    tm = min(tm, m)

    def body(a_ref, w_ref, res_ref, g_ref, x_ref, h_ref):
        xv = res_ref[...] + _dot(a_ref[...], w_ref[...])
        x_ref[...] = xv
        r = lax.rsqrt(jnp.mean(xv * xv, axis=-1, keepdims=True) + EPS)
        h_ref[...] = (xv * r * g_ref[...]).astype(h_ref.dtype)

    row = pl.BlockSpec((tm, d), lambda i: (i, 0))
    return pl.pallas_call(
        body, name=name, grid=(m // tm,),
        in_specs=[pl.BlockSpec((tm, k), lambda i: (i, 0)), pl.BlockSpec((k, d), lambda i: (0, 0)), row,
                  pl.BlockSpec((1, d), lambda i: (0, 0))],
        out_specs=[row, row],
        out_shape=[jax.ShapeDtypeStruct((m, d), F32), jax.ShapeDtypeStruct((m, d), MXU_DTYPE)],
        compiler_params=_params(1),
    )(a, w, res, g)


def _ffn_down_loss(a, w, res, target, g, name, tm=512):
    m, k = a.shape
    d = w.shape[1]
    tm = min(tm, m)

    def body(a_ref, w_ref, res_ref, t_ref, g_ref, dx_ref, dxm_ref, dg_ref, loss_ref):
        xv = res_ref[...] + _dot(a_ref[...], w_ref[...])
        gv = g_ref[...]
        r = lax.rsqrt(jnp.mean(xv * xv, axis=-1, keepdims=True) + EPS)
        xh = xv * r
        e = xh * gv - t_ref[...]
        dy = e * (1.0 / d)
        dxh = dy * gv
        dxv = r * (dxh - xh * jnp.mean(dxh * xh, axis=-1, keepdims=True))
        dx_ref[...] = dxv
        dxm_ref[...] = dxv.astype(dxm_ref.dtype)

        @pl.when(pl.program_id(0) == 0)
        def _():
            dg_ref[...] = jnp.zeros_like(dg_ref)
            loss_ref[...] = jnp.zeros_like(loss_ref)

        dg_ref[...] += jnp.sum(dy * xh, axis=0, keepdims=True)
        part = 0.5 * jnp.sum(jnp.mean(e * e, axis=-1, keepdims=True), axis=0, keepdims=True)
        loss_ref[...] += jnp.broadcast_to(part, loss_ref.shape)

    row = pl.BlockSpec((tm, d), lambda i: (i, 0))
    vec = pl.BlockSpec((1, d), lambda i: (0, 0))
    return pl.pallas_call(
        body, name=name, grid=(m // tm,),
        in_specs=[pl.BlockSpec((tm, k), lambda i: (i, 0)), pl.BlockSpec((k, d), lambda i: (0, 0)), row, row, vec],
        out_specs=[row, row, vec, pl.BlockSpec((1, LANES), lambda i: (0, 0))],
        out_shape=[jax.ShapeDtypeStruct((m, d), F32), jax.ShapeDtypeStruct((m, d), MXU_DTYPE),
                   jax.ShapeDtypeStruct((1, d), F32), jax.ShapeDtypeStruct((1, LANES), F32)],
        compiler_params=_params(1),
    )(a, w, res, target, g)


def _mm_nt(a, w, name, rows=None, out_dtype=F32, tm=512, tk=512):
    m, n = a.shape
    row0, k = rows or (0, w.shape[0])
    tm, tk = min(tm, m), min(tk, k)
    first = row0 // tk

    def body(a_ref, w_ref, o_ref):
        o_ref[...] = _dot_nt(a_ref[...], w_ref[...]).astype(o_ref.dtype)

    return pl.pallas_call(
        body, name=name, grid=(k // tk, m // tm),
        in_specs=[pl.BlockSpec((tm, n), lambda j, i: (i, 0)), pl.BlockSpec((tk, n), lambda j, i: (first + j, 0))],
        out_specs=pl.BlockSpec((tm, tk), lambda j, i: (i, j)),
        out_shape=jax.ShapeDtypeStruct((m, k), out_dtype),
        compiler_params=_params(2),
    )(a, w)


def _mm_tn(a, b, name, tk, tn, tt=1024, out_dtype=BF16):
    t, k = a.shape
    n = b.shape[1]
    tt = min(tt, t)
    steps = t // tt

    def body(a_ref, b_ref, o_ref, acc):
        s = pl.program_id(2)

        @pl.when(s == 0)
        def _():
            acc[...] = jnp.zeros_like(acc)

        acc[...] += _dot_tn(a_ref[...], b_ref[...])

        @pl.when(s == steps - 1)
        def _():
            o_ref[...] = acc[...].astype(o_ref.dtype)

    return pl.pallas_call(
        body, name=name, grid=(k // tk, n // tn, steps),
        in_specs=[pl.BlockSpec((tt, tk), lambda i, j, s: (s, i)), pl.BlockSpec((tt, tn), lambda i, j, s: (s, j))],
        out_specs=pl.BlockSpec((tk, tn), lambda i, j, s: (i, j)),
        out_shape=jax.ShapeDtypeStruct((k, n), out_dtype),
        scratch_shapes=[pltpu.VMEM((tk, tn), F32)],
        compiler_params=_params(3),
    )(a, b)


def _ffn_up(h, wg_t, wu_t, name, tm=512, tn=FF_TILE):
    m, k = h.shape
    n = wg_t.shape[0]
    tm = min(tm, m)

    def body(h_ref, wg_ref, wu_ref, gate_ref, up_ref, ff_ref):
        hv = h_ref[...]
        gate = _dot_nt(hv, wg_ref[...])
        up = _dot_nt(hv, wu_ref[...])
        gate_ref[...] = gate.astype(gate_ref.dtype)
        up_ref[...] = up.astype(up_ref.dtype)
        ff_ref[...] = (gate * _sigmoid(gate) * up).astype(ff_ref.dtype)

    wspec = pl.BlockSpec((tn, k), lambda j, i: (j, 0))
    ospec = pl.BlockSpec((tm, tn), lambda j, i: (i, j))
    return pl.pallas_call(
        body, name=name, grid=(n // tn, m // tm),
        in_specs=[pl.BlockSpec((tm, k), lambda j, i: (i, 0)), wspec, wspec],
        out_specs=[ospec, ospec, ospec],
        out_shape=[jax.ShapeDtypeStruct((m, n), MXU_DTYPE)] * 3,
        compiler_params=_params(2),
    )(h, wg_t, wu_t)


def _ffn_bwd_act(dx, wd, gate, up, name, tm=512, tn=FF_TILE):
    m, k = dx.shape
    n = wd.shape[0]
    tm = min(tm, m)

    def body(dx_ref, wd_ref, gate_ref, up_ref, dgate_ref, dup_ref):
        dff = _dot_nt(dx_ref[...], wd_ref[...])
        gate = gate_ref[...].astype(F32)
        sg = _sigmoid(gate)
        dgate_ref[...] = (dff * up_ref[...].astype(F32) * sg * (1.0 + gate * (1.0 - sg))).astype(dgate_ref.dtype)
        dup_ref[...] = (dff * gate * sg).astype(dup_ref.dtype)

    ospec = pl.BlockSpec((tm, tn), lambda j, i: (i, j))
    return pl.pallas_call(
        body, name=name, grid=(n // tn, m // tm),
        in_specs=[pl.BlockSpec((tm, k), lambda j, i: (i, 0)), pl.BlockSpec((tn, k), lambda j, i: (j, 0)),
                  ospec, ospec],
        out_specs=[ospec, ospec],
        out_shape=[jax.ShapeDtypeStruct((m, n), MXU_DTYPE), jax.ShapeDtypeStruct((m, n), MXU_DTYPE)],
        compiler_params=_params(2),
    )(dx, wd, gate, up)


def _chunk_masks():
    r = lax.broadcasted_iota(jnp.int32, (SUPER, SUPER), 0)
    c = lax.broadcasted_iota(jnp.int32, (SUPER, SUPER), 1)
    same = jnp.right_shift(r, 6) == jnp.right_shift(c, 6)
    lower = jnp.logical_and(same, c <= r)
    upper = jnp.logical_and(same, c >= r)
    return same, lower, upper


def _head_block_mask():
    r = lax.broadcasted_iota(jnp.int32, (LANES, LANES), 0)
    c = lax.broadcasted_iota(jnp.int32, (LANES, LANES), 1)
    return jnp.where(jnp.right_shift(r, 6) == jnp.right_shift(c, 6), 1.0, 0.0)


def _lower_bound(lb_raw):
    return 1.0 / (1.0 + jnp.exp(lb_raw[1:2, :] - lb_raw[0:1, :]))


HGRN_UNROLL = 2
PER_SUPER = SUPER // CHUNK
CHUNK_ROWS = [slice(c * CHUNK, (c + 1) * CHUNK) for c in range(PER_SUPER)]


def _over_chunks(rows):
    return jnp.concatenate([jnp.broadcast_to(r, (CHUNK, LANES)) for r in rows], axis=0)


def _hgrn_gates(q, hf, lb, lower_b):
    sig = _sigmoid(hf)
    f = lb + (1.0 - lb) * sig
    k = 1.0 - f
    lf = jnp.log(f)
    b = _rsum_left(lower_b, lf, 2)
    ends = [b[cr.stop - 1:cr.stop, :] for cr in CHUNK_ROWS]
    eb = jnp.exp(b)
    enb = jnp.exp(-b)
    edb = jnp.exp(_over_chunks(ends) - b)
    decs = [jnp.exp(e) for e in ends]
    return sig, f, k, decs, eb, enb, edb, q * eb, k * enb, k * edb


def _hgrn_fwd(proj, lower_bounds, norm_g, n_seq, seq):
    t = n_seq * seq
    n_super = seq // SUPER
    n_chunks = seq // CHUNK

    def body(q_ref, f_ref, i_ref, g_ref, lb_ref, ng_ref, out_ref, opre_ref, st_ref):
        masks = _head_masks()
        _, lower, _ = _chunk_masks()
        lower_b = _ones_where(lower)
        bd = _head_block_mask()
        lb = _lower_bound(lb_ref[...])
        ng = ng_ref[...]

        def step(it, st):
            blocks = [HGRN_UNROLL * it + u for u in range(HGRN_UNROLL)]
            rows = [pl.ds(pl.multiple_of(sb * SUPER, SUPER), SUPER) for sb in blocks]
            vs = [i_ref[rw, :] for rw in rows]
            gates = [_hgrn_gates(q_ref[rw, :], f_ref[rw, :], lb, lower_b) for rw in rows]
            decs, qes, kes, kds = ([g[k] for g in gates] for k in (3, 7, 8, 9))
            scores = [[_dot_nt(qe * m, ke) for m in masks] for qe, ke in zip(qes, kes)]
            updates = [[_dot_tn(v[cr], kd[cr]) for cr in CHUNK_ROWS] for v, kd in zip(vs, kds)]
            states = [st]
            for dec_b, upd_b in zip(decs, updates):
                for dec, upd in zip(dec_b, upd_b):
                    states.append(states[-1] * dec + bd * upd)
            for u, sb in enumerate(blocks):
                for c in range(PER_SUPER):
                    st_ref[0, 0, sb * PER_SUPER + c] = states[u * PER_SUPER + c]
            intra = [[_dot(jnp.where(lower, p, 0.0), v) for p in sc] for sc, v in zip(scores, vs)]
            inter = [[_dot_nt(qe[cr], states[u * PER_SUPER + c]) for c, cr in enumerate(CHUNK_ROWS)]
                     for u, qe in enumerate(qes)]
            for rw, intra_b, inter_b in zip(rows, intra, inter):
                o = intra_b[0] * masks[0] + intra_b[1] * masks[1] + jnp.concatenate(inter_b, axis=0)
                opre_ref[rw, :] = o
                hg = g_ref[rw, :]
                on = o * _head_rstd(o, masks) * ng
                out_ref[rw, :] = (on * hg * _sigmoid(hg)).astype(out_ref.dtype)
            return states[-1]

        lax.fori_loop(0, n_super // HGRN_UNROLL, step, jnp.zeros((LANES, LANES), F32))

    def col(k):
        return pl.BlockSpec((seq, LANES), lambda p, b: (b, k * N_PAIRS + p))

    vec = lambda rows: pl.BlockSpec((rows, LANES), lambda p, b: (0, p))
    ospec = pl.BlockSpec((seq, LANES), lambda p, b: (b, p))
    return pl.pallas_call(
        body, name="hgrn_fwd", grid=(N_PAIRS, n_seq),
        in_specs=[col(0), col(1), col(2), col(3), vec(2), vec(1)],
        out_specs=[ospec, ospec,
                   pl.BlockSpec((1, 1, n_chunks, LANES, LANES), lambda p, b: (b, p, 0, 0, 0))],
        out_shape=[jax.ShapeDtypeStruct((t, 2 * GROUP), MXU_DTYPE), jax.ShapeDtypeStruct((t, GROUP), F32),
                   jax.ShapeDtypeStruct((n_seq, N_PAIRS, n_chunks, LANES, LANES), F32)],
        compiler_params=_params(2),
    )(proj, proj, proj, proj, lower_bounds, norm_g)


def _hgrn_bwd(proj, lower_bounds, norm_g, dmix, opre, states, n_seq, seq):
    t = n_seq * seq
    n_super = seq // SUPER
    n_chunks = seq // CHUNK
    per = SUPER // CHUNK

    def body(q_ref, f_ref, i_ref, g_ref, lb_ref, ng_ref, dm_ref, opre_ref, st_ref,
             dq_ref, df_ref, di_ref, dg_ref, dlb_ref, dng_ref):
        masks = _head_masks()
        _, lower, upper = _chunk_masks()
        lower_b, upper_b = _ones_where(lower), _ones_where(upper)
        bd = _head_block_mask()
        lb_raw = lb_ref[...]
        lb = _lower_bound(lb_raw)
        ng = ng_ref[...]

        @pl.when(pl.program_id(1) == 0)
        def _():
            dlb_ref[...] = jnp.zeros_like(dlb_ref)
            dng_ref[...] = jnp.zeros_like(dng_ref)

        def first_half(sb):
            rows = pl.ds(pl.multiple_of(sb * SUPER, SUPER), SUPER)
            q, hf, v, hg = q_ref[rows, :], f_ref[rows, :], i_ref[rows, :], g_ref[rows, :]
            sig, f, k, decs, eb, enb, edb, qe, ke, kd = _hgrn_gates(q, hf, lb, lower_b)
            o = opre_ref[rows, :]
            r = _head_rstd(o, masks)
            oh = o * r
            dm = dm_ref[rows, :]
            sg = _sigmoid(hg)
            dg_ref[rows, :] = (dm * oh * ng * sg * (1.0 + hg * (1.0 - sg))).astype(dg_ref.dtype)
            don = dm * hg * sg
            dng_ref[...] += jnp.sum(don * oh, axis=0, keepdims=True)
            doh = don * ng
            do = r * (doh - oh * _head_mean(doh * oh, masks))
            doms = [do * m for m in masks]
            qems = [qe * m for m in masks]
            scores = [_dot_nt(qem, ke) for qem in qems]
            dscores = [_dot_nt(dom, v) for dom in doms]
            prevs = [st_ref[0, 0, sb * per + c] for c in range(per)]
            dst_in = [_dot_tn(do[cr], qe[cr]) for cr in CHUNK_ROWS]
            dqe_i = [_dot(do[cr], prev) for cr, prev in zip(CHUNK_ROWS, prevs)]
            return dict(rows=rows, v=v, sig=sig, f=f, decs=decs, eb=eb, enb=enb, edb=edb, qe=qe, ke=ke, kd=kd,
                        doms=doms, qems=qems, scores=scores, dscores=dscores, prevs=prevs, dst_in=dst_in, dqe_i=dqe_i)

        def second_half(blk, dsts):
            v, qe, ke, kd = blk["v"], blk["qe"], blk["ke"], blk["kd"]
            ps = [jnp.where(lower, p, 0.0) for p in blk["scores"]]
            dps = [jnp.where(lower, dp, 0.0) for dp in blk["dscores"]]
            dqe_h = [_dot(dp, ke) for dp in dps]
            dke_h = [_dot_tn(dp, qem) for dp, qem in zip(dps, blk["qems"])]
            dv_h = [_dot_tn(p, dom) for p, dom in zip(ps, blk["doms"])]
            dus = [bd * d for d in dsts]
            dv_i = [_dot_nt(kd[cr], du) for cr, du in zip(CHUNK_ROWS, dus)]
            dkd_i = [_dot(v[cr], du) for cr, du in zip(CHUNK_ROWS, dus)]

            def finish():
                dqe = dqe_h[0] * masks[0] + dqe_h[1] * masks[1] + jnp.concatenate(blk["dqe_i"], axis=0)
                dke = dke_h[0] + dke_h[1]
                dv = dv_h[0] + dv_h[1] + jnp.concatenate(dv_i, axis=0)
                dkd = jnp.concatenate(dkd_i, axis=0)
                dk = dke * blk["enb"] + dkd * blk["edb"]
                db = dqe * qe - dke * ke - dkd * kd
                dkd_kd = dkd * kd
                dends = [jnp.sum(dkd_kd[cr], axis=0, keepdims=True)
                         + jnp.sum(dsts[c] * blk["prevs"][c], axis=0, keepdims=True) * blk["decs"][c]
                         for c, cr in enumerate(CHUNK_ROWS)]
                dlf = _rsum_left(upper_b, db, 2) + _over_chunks(dends)
                sig = blk["sig"]
                dfv = dlf / blk["f"] - dk
                rows = blk["rows"]
                dq_ref[rows, :] = (dqe * blk["eb"]).astype(dq_ref.dtype)
                di_ref[rows, :] = dv.astype(di_ref.dtype)
                df_ref[rows, :] = (dfv * (1.0 - lb) * sig * (1.0 - sig)).astype(df_ref.dtype)
                dlb = jnp.sum(dfv * (1.0 - sig), axis=0, keepdims=True)
                da0 = dlb * lb * (1.0 - lb)
                dlb_ref[0:1, :] += da0
                dlb_ref[1:2, :] -= da0

            return finish

        def step(it, dst):
            blocks = [first_half(n_super - 1 - HGRN_UNROLL * it - u) for u in range(HGRN_UNROLL)]
            all_dsts = []
            for blk in blocks:
                dsts = [None] * per
                for c in reversed(range(per)):
                    dsts[c] = dst
                    dst = bd * (dst * blk["decs"][c] + blk["dst_in"][c])
                all_dsts.append(dsts)
            for finish in [second_half(blk, dsts) for blk, dsts in zip(blocks, all_dsts)]:
                finish()
            return dst

        lax.fori_loop(0, n_super // HGRN_UNROLL, step, jnp.zeros((LANES, LANES), F32))

    def col(k):
        return pl.BlockSpec((seq, LANES), lambda p, b: (b, k * N_PAIRS + p))

    vec = lambda rows: pl.BlockSpec((rows, LANES), lambda p, b: (0, p))
    ospec = pl.BlockSpec((seq, LANES), lambda p, b: (b, p))
    piece = jax.ShapeDtypeStruct((t, GROUP), MXU_DTYPE)
    return pl.pallas_call(
        body, name="hgrn_bwd", grid=(N_PAIRS, n_seq),
        in_specs=[col(0), col(1), col(2), col(3), vec(2), vec(1), ospec, ospec,
                  pl.BlockSpec((1, 1, n_chunks, LANES, LANES), lambda p, b: (b, p, 0, 0, 0))],
        out_specs=[ospec, ospec, ospec, ospec, vec(2), vec(1)],
        out_shape=[piece, piece, piece, piece,
                   jax.ShapeDtypeStruct((2, GROUP), F32), jax.ShapeDtypeStruct((1, GROUP), F32)],
        compiler_params=_params(2),
    )(proj, proj, proj, proj, lower_bounds, norm_g, dmix, opre, states)


SB_SCALE = 1.0 / math.sqrt(D_HEAD)


SB_STEP = 2 * SB_BLOCK
QUERY_BLOCKS = (slice(0, SB_BLOCK), slice(SB_BLOCK, SB_STEP))


def _triangle(keep):
    row = lax.broadcasted_iota(jnp.int32, (SB_BLOCK, SB_BLOCK), 0)
    col = lax.broadcasted_iota(jnp.int32, (SB_BLOCK, SB_BLOCK), 1)
    return _ones_where(keep(row, col))


SB_HALF = SB_BLOCK // 2


def _keep(on_diagonal, x):
    if not on_diagonal:
        return x
    row = lax.broadcasted_iota(jnp.int32, (SB_HALF, SB_HALF), 0)
    col = lax.broadcasted_iota(jnp.int32, (SB_HALF, SB_HALF), 1)
    return jnp.where(col < row, x, 0.0)


def _lower_quadrants(fn, n_out, *tiles):
    def quadrant(r, c):
        return [t[r * SB_HALF:(r + 1) * SB_HALF, c * SB_HALF:(c + 1) * SB_HALF] for t in tiles]

    top, left, bottom = fn(True, *quadrant(0, 0)), fn(False, *quadrant(1, 0)), fn(True, *quadrant(1, 1))
    zero = jnp.zeros((SB_HALF, SB_HALF), F32)
    return [jnp.concatenate([jnp.concatenate([top[o], zero], axis=1), jnp.concatenate([left[o], bottom[o]], axis=1)],
                            axis=0) for o in range(n_out)]


def _sb_fwd(proj, norm_g, mixed, shards, n_seq, seq):
    t = n_seq * seq
    nq = seq // SB_STEP
    q0, k0, v0 = 0, N_PAIRS, 2 * N_PAIRS
    n_w = len(shards)
    n_steps = N_PAIRS * n_seq * nq
    tri = _triangle(lambda row, col: row >= col)

    def body(q_ref, k_ref, v_ref, ng_ref, tri_ref, mixed_in, *rest):
        del mixed_in
        shard_refs = rest[:n_w]
        out_ref, opre_ref, ctot_ref = rest[n_w:n_w + 3]
        gathered = rest[n_w + 3:2 * n_w + 3]
        send_sems, recv_sems, local_sems = rest[2 * n_w + 3:]
        i = pl.program_id(2)
        step = (pl.program_id(0) * n_seq + pl.program_id(1)) * nq + i
        plan = _GatherPlan(shard_refs, gathered, send_sems, recv_sems, local_sems)

        @pl.when(step == 0)
        def _():
            plan.start()

        @pl.when(step == (3 * n_steps) // 4)
        def _():
            plan.forward()

        masks = _head_masks()
        suffix = tri_ref[...]
        qhs = [[(q_ref[blk, :] * SB_SCALE * m).astype(MXU_DTYPE) for m in masks] for blk in QUERY_BLOCKS]

        def tiles(js, work, carry):
            rows = [pl.ds(pl.multiple_of(j * SB_BLOCK, SB_BLOCK), SB_BLOCK) for j in js]
            ks = [k_ref[rw, :].astype(MXU_DTYPE) for rw in rows]
            vs = [v_ref[rw, :].astype(MXU_DTYPE) for rw in rows]
            zs = [[_dot_nt(qh, ks[ts]) for qh in qhs[qb]] for qb, ts, _ in work]
            ccs = [[_lower_quadrants(lambda on, zq: (_keep(on, _softplus(zq)),), 1, z)[0] if diag else _softplus(z)
                    for z in zw] for zw, (_, _, diag) in zip(zs, work)]
            sums = [[jnp.dot(cc.astype(BF16), suffix, preferred_element_type=F32) for cc in cw] for cw in ccs]
            out = [list(per_block) for per_block in carry]
            for h in range(len(masks)):
                for w, (qb, ts, diag) in enumerate(work):
                    run, acc = out[qb][h]
                    logit = zs[w][h] - (sums[w][h] + run)
                    if diag:
                        a = _lower_quadrants(lambda on, lq: (_keep(on, jnp.exp(lq)),), 1, logit)[0]
                    else:
                        a = jnp.exp(logit)
                    out[qb][h] = (run + sums[w][h][:, 0:1], acc + _dot(a, vs[ts]))
            return tuple(tuple(per_block) for per_block in out)

        start = ((jnp.zeros((SB_BLOCK, 1), F32), jnp.zeros((SB_BLOCK, LANES), F32)),) * 2
        carry = tiles([2 * i, 2 * i + 1], [(0, 0, True), (1, 1, True), (1, 0, False)], (start, start))
        both = [(0, 0, False), (0, 1, False), (1, 0, False), (1, 1, False)]
        carry = lax.fori_loop(0, i, lambda s, cy: tiles([2 * (i - s) - 1, 2 * (i - s) - 2], both, cy), carry)
        opre = jnp.concatenate([cb[0][1] * masks[0] + cb[1][1] * masks[1] for cb in carry], axis=0)
        ctot = jnp.concatenate([cb[0][0] * masks[0] + cb[1][0] * masks[1] for cb in carry], axis=0)
        opre_ref[...] = opre
        ctot_ref[...] = ctot
        out_ref[...] = (opre * _head_rstd(opre, masks) * ng_ref[...]).astype(out_ref.dtype)

        @pl.when(step == n_steps - 1)
        def _():
            plan.finish()

    qspec = pl.BlockSpec((SB_STEP, LANES), lambda p, b, i: (b * nq + i, q0 + p))
    ospec = pl.BlockSpec((SB_STEP, LANES), lambda p, b, i: (b * nq + i, p))
    hbm = pl.BlockSpec(memory_space=pltpu.HBM)
    outs = pl.pallas_call(
        body, name="sb_fwd", grid=(N_PAIRS, n_seq, nq),
        in_specs=[qspec,
                  pl.BlockSpec((seq, LANES), lambda p, b, i: (b, k0 + p)),
                  pl.BlockSpec((seq, LANES), lambda p, b, i: (b, v0 + p)),
                  pl.BlockSpec((1, LANES), lambda p, b, i: (0, p)),
                  pl.BlockSpec(tri.shape, lambda p, b, i: (0, 0)), hbm] + [hbm] * n_w,
        out_specs=[pl.BlockSpec((SB_STEP, LANES), lambda p, b, i: (b * nq + i, N_PAIRS + p)), ospec, ospec]
        + [hbm] * n_w,
        out_shape=[jax.ShapeDtypeStruct(mixed.shape, mixed.dtype), jax.ShapeDtypeStruct((t, GROUP), F32),
                   jax.ShapeDtypeStruct((t, GROUP), F32)]
        + [jax.ShapeDtypeStruct((N_DEV,) + s.shape, s.dtype) for s in shards],
        scratch_shapes=[pltpu.SemaphoreType.DMA((n_w * _GatherPlan.COPIES,)),
                        pltpu.SemaphoreType.DMA((n_w * _GatherPlan.COPIES,)), pltpu.SemaphoreType.DMA((n_w,))],
        input_output_aliases={5: 0},
        compiler_params=_params(3),
    )(proj, proj, proj, norm_g, tri, mixed, *shards)
    return outs[0], outs[1], outs[2], list(outs[3:])


def _sb_bwd(proj, norm_g, dmix, opre, ctot, grads, n_seq, seq):
    t = n_seq * seq
    nq = seq // SB_STEP
    q0, k0, v0 = 0, N_PAIRS, 2 * N_PAIRS
    n_w = len(grads)
    n_steps = N_PAIRS * n_seq * nq
    tri = _triangle(lambda row, col: row <= col)

    def body(q_ref, k_ref, v_ref, ng_ref, tri_ref, dm_ref, opre_ref, ctot_ref, *rest):
        grad_refs = rest[:n_w]
        dq_ref, dk_ref, dv_ref, dng_ref = rest[n_w:n_w + 4]
        lands = rest[n_w + 4:2 * n_w + 4]
        dk_acc, dv_acc, send_sems, recv_sems, local_sems = rest[2 * n_w + 4:]
        p_id, b_id, i = pl.program_id(0), pl.program_id(1), pl.program_id(2)
        step = (p_id * n_seq + b_id) * nq + i
        plan = _ScatterPlan(grad_refs, lands, send_sems, recv_sems, local_sems)

        @pl.when(step == 0)
        def _():
            plan.start()

        masks = _head_masks()
        upto = tri_ref[...]

        def prefix(x):
            return jnp.dot(x.astype(BF16), upto, preferred_element_type=F32)

        @pl.when(i == 0)
        def _():
            dk_acc[...] = jnp.zeros_like(dk_acc)
            dv_acc[...] = jnp.zeros_like(dv_acc)

        @pl.when(jnp.logical_and(b_id == 0, i == 0))
        def _():
            dng_ref[...] = jnp.zeros_like(dng_ref)

        o = opre_ref[...]
        rs = _head_rstd(o, masks)
        oh = o * rs
        dm = dm_ref[...]
        dng_ref[...] += jnp.sum(dm * oh, axis=0, keepdims=True)
        doh = dm * ng_ref[...]
        do = rs * (doh - oh * _head_mean(doh * oh, masks))

        heads = range(len(masks))
        qs = [q_ref[blk, :] * SB_SCALE for blk in QUERY_BLOCKS]
        dos = [do[blk] for blk in QUERY_BLOCKS]
        qhs = [[(q * m).astype(MXU_DTYPE) for m in masks] for q in qs]
        doms = [[(d * m).astype(MXU_DTYPE) for m in masks] for d in dos]
        head_rows = [jnp.where(jnp.right_shift(lax.broadcasted_iota(jnp.int32, (LANES, 1), 0), 6) == h, 1.0, 0.0)
                     for h in heads]
        qhts = [[(qt * hr).astype(MXU_DTYPE) for hr in head_rows] for qt in [q.astype(F32).T for q in qs]]
        domts = [[(dt * hr).astype(MXU_DTYPE) for hr in head_rows] for dt in [d.T for d in dos]]
        totals = [[ctot_ref[blk, h * D_HEAD:h * D_HEAD + 1] for h in heads] for blk in QUERY_BLOCKS]

        def tiles(js, work, carry):
            rows = [pl.ds(pl.multiple_of(j * SB_BLOCK, SB_BLOCK), SB_BLOCK) for j in js]
            ks = [k_ref[rw, :].astype(MXU_DTYPE) for rw in rows]
            vs = [v_ref[rw, :].astype(MXU_DTYPE) for rw in rows]
            zs = [[_dot_nt(qhs[qb][h], ks[ts]) for h in heads] for qb, ts, _ in work]
            das = [[_dot_nt(doms[qb][h], vs[ts]) for h in heads] for qb, ts, _ in work]
            def gates(on, zq):
                sp = _softplus(zq)
                return _keep(on, sp), zq - sp, _keep(on, jnp.exp(zq - sp))

            trio = [[_lower_quadrants(gates, 3, z) if diag else gates(False, z) for z in zw]
                    for zw, (_, _, diag) in zip(zs, work)]
            ccs, lsigs, sigs = ([[t[o] for t in tw] for tw in trio] for o in range(3))
            pres = [[prefix(cc) for cc in cw] for cw in ccs]
            out = [list(per_block) for per_block in carry]
            for h in heads:
                for w, (qb, ts, diag) in enumerate(work):
                    pc, pdl, dq_h = out[qb][h]
                    logit = lsigs[w][h] + pres[w][h] - (totals[qb][h] - pc)
                    if diag:
                        a = _lower_quadrants(lambda on, lq: (_keep(on, jnp.exp(lq)),), 1, logit)[0]
                    else:
                        a = jnp.exp(logit)
                    dl = a * das[w][h]
                    dv_acc[js[ts]] += _dot(domts[qb][h], a)
                    dpre = prefix(dl)
                    dz = dl - sigs[w][h] * (pdl + dpre)
                    dzb = dz.astype(MXU_DTYPE)
                    dk_acc[js[ts]] += _dot(qhts[qb][h], dzb)
                    out[qb][h] = (pc + pres[w][h][:, SB_BLOCK - 1:SB_BLOCK], pdl + dpre[:, SB_BLOCK - 1:SB_BLOCK],
                                  dq_h + _dot(dzb, ks[ts]))
            return tuple(tuple(per_block) for per_block in out)

        zero = jnp.zeros((SB_BLOCK, 1), F32)
        start = ((zero, zero, jnp.zeros((SB_BLOCK, LANES), F32)),) * 2
        both = [(0, 0, False), (0, 1, False), (1, 0, False), (1, 1, False)]
        carry = lax.fori_loop(0, i, lambda s, cy: tiles([2 * s, 2 * s + 1], both, cy), (start, start))
        carry = tiles([2 * i, 2 * i + 1], [(0, 0, True), (1, 0, False), (1, 1, True)], carry)
        dq = jnp.concatenate([cb[0][2] * masks[0] + cb[1][2] * masks[1] for cb in carry], axis=0)
        dq_ref[...] = (dq * SB_SCALE).astype(dq_ref.dtype)

        @pl.when(i == nq - 1)
        def _():
            for j in range(seq // SB_BLOCK):
                tile_rows = slice(j * SB_BLOCK, (j + 1) * SB_BLOCK)
                dk_ref[tile_rows, :] = dk_acc[j].T.astype(dk_ref.dtype)
                dv_ref[tile_rows, :] = dv_acc[j].T.astype(dv_ref.dtype)

        @pl.when(step == n_steps - 1)
        def _():
            plan.finish()

    qspec = pl.BlockSpec((SB_STEP, LANES), lambda p, b, i: (b * nq + i, q0 + p))
    ospec = pl.BlockSpec((SB_STEP, LANES), lambda p, b, i: (b * nq + i, p))
    dmspec = pl.BlockSpec((SB_STEP, LANES), lambda p, b, i: (b * nq + i, N_PAIRS + p))
    full = lambda k: pl.BlockSpec((seq, LANES), lambda p, b, i: (b, k + p))
    vec = pl.BlockSpec((1, LANES), lambda p, b, i: (0, p))
    hbm = pl.BlockSpec(memory_space=pltpu.HBM)
    piece = jax.ShapeDtypeStruct((t, GROUP), MXU_DTYPE)
    outs = pl.pallas_call(
        body, name="sb_bwd", grid=(N_PAIRS, n_seq, nq),
        in_specs=[qspec, full(k0), full(v0), vec, pl.BlockSpec(tri.shape, lambda p, b, i: (0, 0)), dmspec, ospec, ospec]
        + [hbm] * n_w,
        out_specs=[ospec, full(0), full(0), vec] + [hbm] * n_w,
        out_shape=[piece, piece, piece, jax.ShapeDtypeStruct((1, GROUP), F32)]
        + [jax.ShapeDtypeStruct(g.shape, g.dtype) for g in grads],
        scratch_shapes=[pltpu.VMEM((seq // SB_BLOCK, LANES, SB_BLOCK), F32),
                        pltpu.VMEM((seq // SB_BLOCK, LANES, SB_BLOCK), F32),
                        pltpu.SemaphoreType.DMA((n_w * (N_DEV - 1),)), pltpu.SemaphoreType.DMA((n_w * (N_DEV - 1),)),
                        pltpu.SemaphoreType.DMA((n_w,))],
        compiler_params=_params(3),
    )(proj, proj, proj, norm_g, tri, dmix, opre, ctot, *grads)
    return outs[0], outs[1], outs[2], outs[3], list(outs[4:])


def _mesh_place():
    x, y, c = lax.axis_index("x"), lax.axis_index("y"), lax.axis_index("c")
    return x, y, c


def _peer(x, y, c, k):
    px = lax.rem(x + ((k >> 2) & 1), 2)
    py = lax.rem(y + ((k >> 1) & 1), 2)
    pc = lax.rem(c + (k & 1), 2)
    return (px, py, pc), 4 * px + 2 * py + pc


def _remote(src, dst, send_sem, recv_sem, to):
    return pltpu.make_async_remote_copy(src_ref=src, dst_ref=dst, send_sem=send_sem, recv_sem=recv_sem,
                                        device_id=to, device_id_type=pl.DeviceIdType.MESH)


class _GatherPlan:
    COPIES = 7

    def __init__(self, shards, gathered, send_sems, recv_sems, local_sems):
        x, y, c = _mesh_place()
        self.c = c
        self.me = (x, y, c)
        self.sibling = (x, y, 1 - c)
        self.chips = [(1 - x, y), (x, 1 - y), (1 - x, 1 - y)]
        self.tensors = list(zip(shards, gathered))
        self.send_sems, self.recv_sems, self.local_sems = send_sems, recv_sems, local_sems

    @staticmethod
    def _index(place):
        return 4 * place[0] + 2 * place[1] + place[2]

    def _copy(self, w, k, block, to, own=False):
        shard, gathered = self.tensors[w]
        slot = gathered.at[self._index(block)]
        n = w * self.COPIES + k
        return _remote(shard if own else slot, slot, self.send_sems.at[n], self.recv_sems.at[n], to)

    def _local(self, w):
        shard, gathered = self.tensors[w]
        return pltpu.make_async_copy(shard, gathered.at[self._index(self.me)], self.local_sems.at[w])

    def _first(self, w):
        return [self._copy(w, 0, self.me, self.sibling, own=True)] + [
            self._copy(w, 1 + j, self.me, (*chip, self.c), own=True) for j, chip in enumerate(self.chips)]

    def _passed(self, w):
        return [self._copy(w, 4 + j, (*chip, self.c), self.sibling) for j, chip in enumerate(self.chips)]

    def start(self):
        for w in range(len(self.tensors)):
            self._local(w).start()
            for cp in self._first(w):
                cp.start()

    def forward(self):
        for w in range(len(self.tensors)):
            passed = self._passed(w)
            for j, chip in enumerate(self.chips):
                self._copy(w, 1 + j, (*chip, self.c), self.me).wait_recv()
                passed[j].start()

    def finish(self):
        for w in range(len(self.tensors)):
            self._copy(w, 0, self.sibling, self.me).wait_recv()
            for j, chip in enumerate(self.chips):
                self._copy(w, 4 + j, (*chip, 1 - self.c), self.me).wait_recv()
            for cp in self._first(w) + self._passed(w):
                cp.wait_send()
            self._local(w).wait()


class _ScatterPlan:
    def __init__(self, grads, lands, send_sems, recv_sems, local_sems):
        self.place = _mesh_place()
        x, y, c = self.place
        self.me = 4 * x + 2 * y + c
        self.tensors = list(zip(grads, lands))
        self.send_sems, self.recv_sems, self.local_sems = send_sems, recv_sems, local_sems

    def _copies(self, w):
        grad, land = self.tensors[w]
        out = []
        for k in range(1, N_DEV):
            peer, pidx = _peer(*self.place, k)
            n = w * (N_DEV - 1) + k - 1
            sems = (self.send_sems.at[n], self.recv_sems.at[n], peer)
            out.append((_remote(grad.at[pidx], land.at[self.me], *sems), _remote(grad.at[pidx], land.at[pidx], *sems)))
        return out

    def _local(self, w):
        grad, land = self.tensors[w]
        return pltpu.make_async_copy(grad.at[self.me], land.at[self.me], self.local_sems.at[w])

    def start(self):
        for w in range(len(self.tensors)):
            self._local(w).start()
            for send, _ in self._copies(w):
                send.start()

    def finish(self):
        for w in range(len(self.tensors)):
            copies = self._copies(w)
            for _, arrival in copies:
                arrival.wait_recv()
            for send, _ in copies:
                send.wait_send()
            self._local(w).wait()


def _cast_shards(shards):
    def body(*refs):
        n = len(refs) // 2
        for src, dst in zip(refs[:n], refs[n:]):
            dst[...] = src[...].astype(dst.dtype)

    vmem = pl.BlockSpec(memory_space=pltpu.VMEM)
    return pl.pallas_call(
        body, name="cast_shards", in_specs=[vmem] * len(shards), out_specs=[vmem] * len(shards),
        out_shape=[jax.ShapeDtypeStruct(s.shape, BF16) for s in shards],
        compiler_params=pltpu.CompilerParams(vmem_limit_bytes=VMEM_LIMIT),
    )(*shards)


def _gather_w_in(shard, x, g, chunk=512):
    rows, cols = shard.shape
    t, d = x.shape
    chunk = min(chunk, t)
    n_chunks = t // chunk

    def body(w_ref, x_ref, g_ref, out_ref, h_ref, x_buf, send_sems, recv_sems, local_sems, x_sems):
        plan = _GatherPlan([w_ref], [out_ref], send_sems, recv_sems, local_sems)
        plan.start()

        def fetch(j):
            return pltpu.make_async_copy(x_ref.at[pl.ds(j * chunk, chunk)], x_buf.at[j % 2], x_sems.at[j % 2])

        fetch(0).start()
        gv = g_ref[...]
        for j in range(n_chunks):
            if j + 1 < n_chunks:
                fetch(j + 1).start()
            fetch(j).wait()
            xv = x_buf[j % 2]
            r = lax.rsqrt(jnp.mean(xv * xv, axis=-1, keepdims=True) + EPS)
            h_ref[j * chunk:(j + 1) * chunk, :] = (xv * r * gv).astype(h_ref.dtype)
        plan.forward()
        plan.finish()

    vmem = pl.BlockSpec(memory_space=pltpu.VMEM)
    return pl.pallas_call(
        body, name="gather_w_in", in_specs=[vmem, pl.BlockSpec(memory_space=pltpu.HBM), vmem], out_specs=[vmem, vmem],
        out_shape=[jax.ShapeDtypeStruct((N_DEV, rows, cols), shard.dtype), jax.ShapeDtypeStruct((t, d), MXU_DTYPE)],
        scratch_shapes=[pltpu.VMEM((2, chunk, d), F32), pltpu.SemaphoreType.DMA((_GatherPlan.COPIES,)),
                        pltpu.SemaphoreType.DMA((_GatherPlan.COPIES,)), pltpu.SemaphoreType.DMA((1,)),
                        pltpu.SemaphoreType.DMA((2,))],
        compiler_params=pltpu.CompilerParams(vmem_limit_bytes=VMEM_LIMIT),
    )(shard, x, g)


def _dw_rows(pieces, b, name, tn=512, tt=1024):
    t, n = b.shape
    widths = [p.shape[1] for p in pieces]
    rows = sum(widths)
    tt = min(tt, t)
    steps = t // tt
    n_p = len(pieces)

    def body(*refs):
        piece_refs, b_ref, o_ref, acc = refs[:n_p], refs[n_p], refs[n_p + 1], refs[n_p + 2]
        s = pl.program_id(1)

        @pl.when(s == 0)
        def _():
            acc[...] = jnp.zeros_like(acc)

        bv = b_ref[...]
        off = 0
        for p_ref, width in zip(piece_refs, widths):
            acc[off:off + width, :] += _dot_tn(p_ref[...], bv)
            off += width

        @pl.when(s == steps - 1)
        def _():
            o_ref[...] = acc[...].astype(o_ref.dtype)

    return pl.pallas_call(
        body, name=name, grid=(n // tn, steps),
        in_specs=[pl.BlockSpec((tt, width), lambda j, s: (s, 0)) for width in widths]
        + [pl.BlockSpec((tt, tn), lambda j, s: (s, j))],
        out_specs=pl.BlockSpec((rows, tn), lambda j, s: (0, j)),
        out_shape=jax.ShapeDtypeStruct((rows, n), BF16),
        scratch_shapes=[pltpu.VMEM((rows, tn), F32)],
        compiler_params=_params(2),
    )(*pieces, b)


def _dh_norm_bwd(pairs, x, g, res, name, tm=512, after=(), next_w=None):
    m, d = x.shape
    tm = min(tm, m)
    n_p = len(pairs)
    n_steps = m // tm
    n_in = 2 * n_p + 3 + len(after) + (next_w is not None)

    def body(*refs):
        x_ref, g_ref, res_ref = refs[2 * n_p:2 * n_p + 3]
        dx_ref, dg_ref = refs[n_in:n_in + 2]

        @pl.when(pl.program_id(0) == 0)
        def _():
            dg_ref[...] = jnp.zeros_like(dg_ref)

        dh = None
        for q in range(n_p):
            part = _dot(refs[2 * q][...], refs[2 * q + 1][...])
            dh = part if dh is None else dh + part
        xv = x_ref[...]
        r = lax.rsqrt(jnp.mean(xv * xv, axis=-1, keepdims=True) + EPS)
        xh = xv * r
        dxh = dh * g_ref[...]
        dxv = res_ref[...] + r * (dxh - xh * jnp.mean(dxh * xh, axis=-1, keepdims=True))
        dx_ref[...] = dxv
        dg_ref[...] += jnp.sum(dh * xh, axis=0, keepdims=True)
        if next_w is not None:
            refs[n_in + 2][...] = _dot_nt(dxv, refs[n_in - 1][...])

    in_specs, args = [], []
    for a, w, r in pairs:
        k = a.shape[1]
        in_specs += [pl.BlockSpec((tm, k), lambda i: (i, 0)),
                     pl.BlockSpec((k, d), functools.partial(lambda i, r: (r, 0), r=r or 0),
                                  pipeline_mode=pl.Buffered(1))]
        args += [a, w]
    row = pl.BlockSpec((tm, d), lambda i: (i, 0))
    vec = pl.BlockSpec((1, d), lambda i: (0, 0))
    extra_in, extra_out, extra_shape = [], [], []
    if next_w is not None:
        e = next_w.shape[0]
        extra_in = [pl.BlockSpec((e, d), lambda i: (0, 0), pipeline_mode=pl.Buffered(1))]
        extra_out = [pl.BlockSpec((tm, e), lambda i: (i, 0))]
        extra_shape = [jax.ShapeDtypeStruct((m, e), F32)]
        args_tail = (next_w,)
    else:
        args_tail = ()
    return pl.pallas_call(
        body, name=name, grid=(n_steps,),
        in_specs=in_specs + [row, vec, row] + [pl.BlockSpec(memory_space=pl.ANY)] * len(after) + extra_in,
        out_specs=[row, vec] + extra_out,
        out_shape=[jax.ShapeDtypeStruct((m, d), F32), jax.ShapeDtypeStruct((1, d), F32)] + extra_shape,
        compiler_params=_params(1),
    )(*args, x, g, res, *after, *args_tail)


def _exchange_start(grad, name):
    def body(g_ref, land_ref, send_sem, recv_sem, local_sem, g_thru, land_thru, token):
        del g_thru, land_thru
        x, y, c = _mesh_place()
        me = 4 * x + 2 * y + c
        pltpu.make_async_copy(g_ref.at[me], land_ref.at[me], local_sem).start()
        for k in range(1, N_DEV):
            peer, pidx = _peer(x, y, c, k)
            _remote(g_ref.at[pidx], land_ref.at[me], send_sem, recv_sem, peer).start()
        token[...] = jnp.zeros_like(token)

    hbm = pl.BlockSpec(memory_space=pltpu.HBM)
    sem = pl.BlockSpec(memory_space=pltpu.SEMAPHORE)
    buf = pltpu.HBM(grad.shape, grad.dtype)
    return pl.pallas_call(
        body, name=name,
        out_shape=(pltpu.SemaphoreType.DMA(()), pltpu.SemaphoreType.DMA(()), pltpu.SemaphoreType.DMA(()), buf, buf,
                   jax.ShapeDtypeStruct((8, LANES), F32)),
        in_specs=(hbm, hbm), out_specs=(sem, sem, sem, hbm, hbm, pl.BlockSpec(memory_space=pltpu.VMEM)),
        input_output_aliases={0: 3, 1: 4},
        compiler_params=pltpu.CompilerParams(has_side_effects=pltpu.SideEffectType.DATAFLOW_SIDE_EFFECTING),
    )(pltpu.with_memory_space_constraint(grad, pltpu.HBM),
      pltpu.with_memory_space_constraint(lax.empty(grad.shape, grad.dtype), pltpu.HBM))


def _exchange_wait(send_sem, recv_sem, local_sem, grad, land, after, name):
    def body(g_ref, land_ref, send_sem, recv_sem, local_sem, *rest):
        x, y, c = _mesh_place()
        peer, _ = _peer(x, y, c, 1)
        others = pl.ds(0, N_DEV - 1)
        seven = _remote(g_ref.at[others], land_ref.at[others], send_sem, recv_sem, peer)
        seven.wait_send()
        seven.wait_recv()
        pltpu.make_async_copy(g_ref.at[0], land_ref.at[0], local_sem).wait()

    hbm = pl.BlockSpec(memory_space=pltpu.HBM)
    sem = pl.BlockSpec(memory_space=pltpu.SEMAPHORE)
    buf = pltpu.HBM(grad.shape, grad.dtype)
    return pl.pallas_call(
        body, name=name, out_shape=(buf, buf),
        in_specs=(hbm, hbm, sem, sem, sem) + (pl.BlockSpec(memory_space=pl.ANY),) * len(after), out_specs=(hbm, hbm),
        input_output_aliases={0: 0, 1: 1},
        compiler_params=pltpu.CompilerParams(has_side_effects=pltpu.SideEffectType.DATAFLOW_SIDE_EFFECTING),
    )(grad, land, send_sem, recv_sem, local_sem, *after)[1]


SMALL_LAYOUT = ((0, 0, 0, 0, D_MODEL), (1, 0, 1, 0, GROUP), (1, 1, 1, GROUP, GROUP), (2, 0, 2, 0, GROUP),
                (3, 0, 2, GROUP, GROUP), (4, 0, 3, 0, D_MODEL), (5, 0, 4, 0, D_MODEL))
LOSS_ROW = 5
N_SMALL = 6


def _small_step(grads, loss_part, ws, ms, vs):
    def body(*refs):
        g_in, loss_in = refs[:N_SMALL], refs[N_SMALL]
        params = [refs[1 + (q + 1) * N_SMALL:1 + (q + 2) * N_SMALL] for q in range(3)]
        o0 = 1 + 4 * N_SMALL
        outs = [refs[o0 + q * N_SMALL:o0 + (q + 1) * N_SMALL] for q in range(4)]
        loss_out = refs[o0 + 4 * N_SMALL]
        pack, land, wp, mp, vp, send_sems, recv_sems = refs[o0 + 4 * N_SMALL + 1:]

        def place(dst, srcs):
            dst[...] = jnp.zeros_like(dst)
            for p, sr, dr, dc, width in SMALL_LAYOUT:
                dst[dr:dr + 1, dc:dc + width] = srcs[p][sr:sr + 1, :]

        place(pack, g_in)
        pack[LOSS_ROW:LOSS_ROW + 1, 0:LANES] = loss_in[...]
        for dst, srcs in zip((wp, mp, vp), params):
            place(dst, srcs)

        x, y, c = _mesh_place()
        me = 4 * x + 2 * y + c
        land[me] = pack[...]
        sends = []
        for k in range(1, N_DEV):
            peer, _ = _peer(x, y, c, k)
            cp = _remote(pack, land.at[me], send_sems.at[k - 1], recv_sems.at[k - 1], peer)
            cp.start()
            sends.append(cp)
        for k in range(1, N_DEV):
            peer, pidx = _peer(x, y, c, k)
            _remote(pack, land.at[pidx], send_sems.at[k - 1], recv_sems.at[k - 1], peer).wait_recv()
        for cp in sends:
            cp.wait_send()

        g = land[0]
        for d in range(1, N_DEV):
            g = g + land[d]
        delta, nm, nv = _adam(wp[...], g, mp[...], vp[...])
        for val, out in zip((g, delta, nm, nv), outs):
            for p, sr, dr, dc, width in SMALL_LAYOUT:
                out[p][sr:sr + 1, :] = val[dr:dr + 1, dc:dc + width]
        loss_out[...] = g[LOSS_ROW:LOSS_ROW + 1, 0:LANES]

    vmem = pl.BlockSpec(memory_space=pltpu.VMEM)
    n_in = 1 + 4 * N_SMALL
    shapes = [jax.ShapeDtypeStruct(w.shape, F32) for w in ws]
    packed = pltpu.VMEM((SMALL_ROWS, D_MODEL), F32)
    outs = pl.pallas_call(
        body, name="small_step", in_specs=[vmem] * n_in, out_specs=[vmem] * (4 * N_SMALL + 1),
        out_shape=shapes * 4 + [jax.ShapeDtypeStruct((1, LANES), F32)],
        scratch_shapes=[packed, pltpu.VMEM((N_DEV, SMALL_ROWS, D_MODEL), F32), packed, packed, packed,
                        pltpu.SemaphoreType.DMA((N_DEV - 1,)), pltpu.SemaphoreType.DMA((N_DEV - 1,))],
    )(*grads, loss_part, *ws, *ms, *vs)
    return [outs[q * N_SMALL:(q + 1) * N_SMALL] for q in range(4)], outs[4 * N_SMALL]


def _adam(w, g, m, v):
    m = ADAM_B1 * m + (1.0 - ADAM_B1) * g
    v = ADAM_B2 * v + (1.0 - ADAM_B2) * (g * g)
    m_hat = m / (1.0 - ADAM_B1 ** ADAM_STEP)
    v_hat = v / (1.0 - ADAM_B2 ** ADAM_STEP)
    delta = -ADAM_LR * (m_hat / (jnp.sqrt(v_hat) + ADAM_EPS) + ADAM_WD * w)
    return delta, m, v


def _reduce_adamw(land, w, m, v, name, tr, after=()):
    _, rows, width = land.shape

    def body(land_ref, w_ref, m_ref, v_ref, *rest):
        g_ref, d_ref, nm_ref, nv_ref = rest[len(after):]
        g = land_ref[0].astype(F32)
        for d in range(1, N_DEV):
            g = g + land_ref[d].astype(F32)
        delta, nm, nv = _adam(w_ref[...], g, m_ref[...], v_ref[...])
        g_ref[...] = g
        d_ref[...] = delta
        nm_ref[...] = nm
        nv_ref[...] = nv

    row = pl.BlockSpec((tr, width), lambda i: (i, 0))
    out = jax.ShapeDtypeStruct((rows, width), F32)
    return pl.pallas_call(
        body, name=name, grid=(rows // tr,),
        in_specs=[pl.BlockSpec((N_DEV, tr, width), lambda i: (0, i, 0)), row, row, row]
        + [pl.BlockSpec(memory_space=pl.ANY)] * len(after),
        out_specs=[row, row, row, row], out_shape=[out, out, out, out],
        compiler_params=_params(1),
    )(land, w, m, v, *after)


def kernel(x, mix_norm_g, w_in, lower_bounds, hgrn_norm_g, sb_norm_g, w_out, ffn_norm_g, w_gate, w_up, w_down, final_norm_g, loss_target, m_mix_norm_g, m_w_in, m_lower_bounds, m_hgrn_norm_g, m_sb_norm_g, m_w_out, m_ffn_norm_g, m_w_gate, m_w_up, m_w_down, m_final_norm_g, v_mix_norm_g, v_w_in, v_lower_bounds, v_hgrn_norm_g, v_sb_norm_g, v_w_out, v_ffn_norm_g, v_w_gate, v_w_up, v_w_down, v_final_norm_g):
    n_seq, seq, d = x.shape
    t = n_seq * seq
    x2d = x.reshape(t, d)
    tgt = loss_target.reshape(t, d)
    final_g = final_norm_g.reshape(1, d)
    col_sharded = (True, False, True, True, False)

    def as_rows(ws):
        return [w[0].T if tr else w[0] for w, tr in zip(ws, col_sharded)]

    big_w = as_rows([w_in, w_out, w_gate, w_up, w_down])
    big_m = as_rows([m_w_in, m_w_out, m_w_gate, m_w_up, m_w_down])
    big_v = as_rows([v_w_in, v_w_out, v_w_gate, v_w_up, v_w_down])

    sh_in, sh_out, sh_gate, sh_up, sh_down = _cast_shards(big_w)
    wt_in, h1 = _gather_w_in(sh_in, x2d, mix_norm_g)
    wt_in = wt_in.reshape(IN_COLS, d)
    proj_h = _mm_nt(h1, wt_in, "proj_hgrn", rows=(0, 4 * GROUP), tm=1024, tk=1024)
    proj_s = _mm_nt(h1, wt_in, "proj_sb", rows=(4 * GROUP, 3 * GROUP), out_dtype=MXU_DTYPE, tm=1024, tk=512)
    mixed, oa_pre, states = _hgrn_fwd(proj_h, lower_bounds, hgrn_norm_g, n_seq, seq)
    mixed, ob_pre, ctot, gathered = _sb_fwd(proj_s, sb_norm_g, mixed, [sh_out, sh_gate, sh_up, sh_down], n_seq, seq)
    wf_out = gathered[0].reshape(d, d)
    wt_gate = gathered[1].reshape(D_FF, d)
    wt_up = gathered[2].reshape(D_FF, d)
    wf_down = gathered[3].reshape(D_FF, d)
    x1, h2 = _mix_out_norm(mixed, wf_out, x2d, ffn_norm_g, "mix_out")
    gate, up, ff = _ffn_up(h2, wt_gate, wt_up, "ffn_up")
    dx2, dx2m, d_final_g, loss_part = _ffn_down_loss(ff, wf_down, x1, tgt, final_g, "ffn_down_loss")

    dgate, dup = _ffn_bwd_act(dx2m, wf_down, gate, up, "ffn_bwd_act")
    dw_down = _mm_tn(ff, dx2m, "dw_down", tk=1408, tn=1024).reshape(N_DEV, D_FF // N_DEV, d)
    dw_gate = _mm_tn(dgate, h2, "dw_gate", tk=1408, tn=1024).reshape(N_DEV, D_FF // N_DEV, d)
    dw_up = _mm_tn(dup, h2, "dw_up", tk=1408, tn=1024).reshape(N_DEV, D_FF // N_DEV, d)
    dx1, d_ffn_g, dmix = _dh_norm_bwd([(dgate, wt_gate, None), (dup, wt_up, None)], x1, ffn_norm_g, dx2, "dh_ffn",
                                      tm=256, next_w=wf_out)
    dw_out = _mm_tn(mixed, dx1, "dw_out", tk=512, tn=1024).reshape(N_DEV, d // N_DEV, d)
    dsq, dsk, dsv, d_sb_g, lands = _sb_bwd(proj_s, sb_norm_g, dmix, ob_pre, ctot, [dw_out, dw_gate, dw_up, dw_down],
                                            n_seq, seq)
    dhq, dhf, dhi, dhg, d_lb, d_hgrn_g = _hgrn_bwd(proj_h, lower_bounds, hgrn_norm_g, dmix, oa_pre, states, n_seq,
                                                   seq)
    dproj = [dhq, dhf, dhi, dhg, dsq, dsk, dsv]
    dw_in = _dw_rows(dproj, h1, "dw_in").reshape(N_DEV, IN_COLS // N_DEV, d)
    send_sem, recv_sem, local_sem, dw_in, land_in, token = _exchange_start(dw_in, "dw_in_send")
    dx, d_mix_g = _dh_norm_bwd([(piece, wt_in, k) for k, piece in enumerate(dproj)], x2d, mix_norm_g, dx1, "dh_mix",
                               after=(token,))

    tiles = {"in": 224, "out": 128, "gate": 176, "up": 176, "down": 176}
    keys = list(tiles)
    rest = [_reduce_adamw(land, w, m, v, "adamw_" + key, tr=tiles[key], after=(token,))
            for key, land, w, m, v in zip(keys[1:], lands, big_w[1:], big_m[1:], big_v[1:])]
    small, loss_row = _small_step(
        [d_mix_g, d_lb, d_hgrn_g, d_sb_g, d_ffn_g, d_final_g], loss_part,
        [mix_norm_g, lower_bounds, hgrn_norm_g, sb_norm_g, ffn_norm_g, final_g],
        [m_mix_norm_g, m_lower_bounds, m_hgrn_norm_g, m_sb_norm_g, m_ffn_norm_g, m_final_norm_g.reshape(1, d)],
        [v_mix_norm_g, v_lower_bounds, v_hgrn_norm_g, v_sb_norm_g, v_ffn_norm_g, v_final_norm_g.reshape(1, d)])
    land_in = _exchange_wait(send_sem, recv_sem, local_sem, dw_in, land_in,
                             [dx, loss_row] + [res[0] for res in rest], "dw_in_await")
    big = [_reduce_adamw(land_in, big_w[0], big_m[0], big_v[0], "adamw_in", tr=tiles["in"])] + rest
    big = [[r.T if tr else r for r in res] for res, tr in zip(big, col_sharded)]

    outs = [loss_row[0, 0], dx.reshape(n_seq, seq, d)]
    for q in range(4):
        b_in, b_out, b_gate, b_up, b_down = [res[q][None] for res in big]
        s_mix, s_lb, s_hgrn, s_sb, s_ffn, s_final = small[q]
        outs += [s_mix, b_in, s_lb, s_hgrn, s_sb, b_out, s_ffn, b_gate, b_up, b_down, s_final.reshape(d)]
    return tuple(outs)
```

```python
import functools
import math

import jax
import jax.numpy as jnp
from jax import lax
from jax.experimental import pallas as pl
from jax.experimental.pallas import tpu as pltpu

F32 = jnp.float32
BF16 = jnp.bfloat16
MXU_DTYPE = BF16

EPS = 1e-6
D_MODEL = 1024
N_HEADS = 8
D_HEAD = 64
GROUP = N_HEADS * D_HEAD
IN_COLS = 7 * GROUP
D_FF = 2816
CHUNK = 64
LANES = 128
N_PAIRS = GROUP // LANES
SUPER = 256
SB_BLOCK = 256
N_DEV = 8

ADAM_LR = 0.001
ADAM_B1 = 0.9
ADAM_B2 = 0.999
ADAM_EPS = 1e-08
ADAM_WD = 0.01
ADAM_STEP = 10

SMALL_ROWS = 8
FF_TILE = D_FF // 2

VMEM_LIMIT = 48 * 1024 * 1024


def _params(n_axes, vmem=VMEM_LIMIT):
    return pltpu.CompilerParams(dimension_semantics=("arbitrary",) * n_axes, vmem_limit_bytes=vmem)


def _dot(a, b):
    return jnp.dot(a.astype(MXU_DTYPE), b.astype(MXU_DTYPE), preferred_element_type=F32)


def _dot_nt(a, b):
    return lax.dot_general(a.astype(MXU_DTYPE), b.astype(MXU_DTYPE), (((1,), (1,)), ((), ())),
                           preferred_element_type=F32)


def _dot_tn(a, b):
    return lax.dot_general(a.astype(MXU_DTYPE), b.astype(MXU_DTYPE), (((0,), (0,)), ((), ())),
                           preferred_element_type=F32)


def _split(x, parts):
    out, r = [], x
    for _ in range(parts):
        h = r.astype(BF16)
        out.append(h)
        r = r - h.astype(F32)
    return out


def _rsum_left(u, x, parts):
    acc = None
    for h in _split(x, parts):
        d = jnp.dot(u, h, preferred_element_type=F32)
        acc = d if acc is None else acc + d
    return acc


def _ones_where(mask):
    return jnp.where(mask, 1.0, 0.0).astype(BF16)


def _sigmoid(x):
    return 1.0 / (1.0 + jnp.exp(-x))


def _softplus(x):
    return jnp.maximum(x, 0.0) + jnp.log(1.0 + jnp.exp(-jnp.abs(x)))


def _head_masks():
    lane = lax.broadcasted_iota(jnp.int32, (1, LANES), 1)
    return [jnp.where(lane < D_HEAD, 1.0, 0.0), jnp.where(lane >= D_HEAD, 1.0, 0.0)]


def _head_rstd(o, masks):
    sq = o * o
    r = None
    for m in masks:
        ms = jnp.sum(sq * m, axis=1, keepdims=True) * (1.0 / D_HEAD)
        t = lax.rsqrt(ms + EPS) * m
        r = t if r is None else r + t
    return r


def _head_mean(t, masks):
    out = None
    for m in masks:
        v = jnp.sum(t * m, axis=1, keepdims=True) * (1.0 / D_HEAD) * m
        out = v if out is None else out + v
    return out


def _mix_out_norm(a, w, res, g, name, tm=512):
    m, k = a.shape
    d = w.shape[1]
    tm = min(tm, m)

    def body(a_ref, w_ref, res_ref, g_ref, x_ref, h_ref):
        xv = res_ref[...] + _dot(a_ref[...], w_ref[...])
        x_ref[...] = xv
        r = lax.rsqrt(jnp.mean(xv * xv, axis=-1, keepdims=True) + EPS)
        h_ref[...] = (xv * r * g_ref[...]).astype(h_ref.dtype)

    row = pl.BlockSpec((tm, d), lambda i: (i, 0))
    return pl.pallas_call(
        body, name=name, grid=(m // tm,),
        in_specs=[pl.BlockSpec((tm, k), lambda i: (i, 0)), pl.BlockSpec((k, d), lambda i: (0, 0)), row,
                  pl.BlockSpec((1, d), lambda i: (0, 0))],
        out_specs=[row, row],
        out_shape=[jax.ShapeDtypeStruct((m, d), F32), jax.ShapeDtypeStruct((m, d), MXU_DTYPE)],
        compiler_params=_params(1),
    )(a, w, res, g)


def _ffn_down_loss(a, w, res, target, g, name, tm=512):
    m, k = a.shape
    d = w.shape[1]
    tm = min(tm, m)

    def body(a_ref, w_ref, res_ref, t_ref, g_ref, dx_ref, dxm_ref, dg_ref, loss_ref):
        xv = res_ref[...] + _dot(a_ref[...], w_ref[...])
        gv = g_ref[...]
        r = lax.rsqrt(jnp.mean(xv * xv, axis=-1, keepdims=True) + EPS)
        xh = xv * r
        e = xh * gv - t_ref[...]
        dy = e * (1.0 / d)
        dxh = dy * gv
        dxv = r * (dxh - xh * jnp.mean(dxh * xh, axis=-1, keepdims=True))
        dx_ref[...] = dxv
        dxm_ref[...] = dxv.astype(dxm_ref.dtype)

        @pl.when(pl.program_id(0) == 0)
        def _():
            dg_ref[...] = jnp.zeros_like(dg_ref)
            loss_ref[...] = jnp.zeros_like(loss_ref)

        dg_ref[...] += jnp.sum(dy * xh, axis=0, keepdims=True)
        part = 0.5 * jnp.sum(jnp.mean(e * e, axis=-1, keepdims=True), axis=0, keepdims=True)
        loss_ref[...] += jnp.broadcast_to(part, loss_ref.shape)

    row = pl.BlockSpec((tm, d), lambda i: (i, 0))
    vec = pl.BlockSpec((1, d), lambda i: (0, 0))
    return pl.pallas_call(
        body, name=name, grid=(m // tm,),
        in_specs=[pl.BlockSpec((tm, k), lambda i: (i, 0)), pl.BlockSpec((k, d), lambda i: (0, 0)), row, row, vec],
        out_specs=[row, row, vec, pl.BlockSpec((1, LANES), lambda i: (0, 0))],
        out_shape=[jax.ShapeDtypeStruct((m, d), F32), jax.ShapeDtypeStruct((m, d), MXU_DTYPE),
                   jax.ShapeDtypeStruct((1, d), F32), jax.ShapeDtypeStruct((1, LANES), F32)],
        compiler_params=_params(1),
    )(a, w, res, target, g)


def _mm_nt(a, w, name, rows=None, out_dtype=F32, tm=512, tk=512):
    m, n = a.shape
    row0, k = rows or (0, w.shape[0])
    tm, tk = min(tm, m), min(tk, k)
    first = row0 // tk

    def body(a_ref, w_ref, o_ref):
        o_ref[...] = _dot_nt(a_ref[...], w_ref[...]).astype(o_ref.dtype)

    return pl.pallas_call(
        body, name=name, grid=(k // tk, m // tm),
        in_specs=[pl.BlockSpec((tm, n), lambda j, i: (i, 0)), pl.BlockSpec((tk, n), lambda j, i: (first + j, 0))],
        out_specs=pl.BlockSpec((tm, tk), lambda j, i: (i, j)),
        out_shape=jax.ShapeDtypeStruct((m, k), out_dtype),
        compiler_params=_params(2),
    )(a, w)


def _mm_tn(a, b, name, tk, tn, tt=1024, out_dtype=BF16):
    t, k = a.shape
    n = b.shape[1]
    tt = min(tt, t)
    steps = t // tt

    def body(a_ref, b_ref, o_ref, acc):
        s = pl.program_id(2)

        @pl.when(s == 0)
        def _():
            acc[...] = jnp.zeros_like(acc)

        acc[...] += _dot_tn(a_ref[...], b_ref[...])

        @pl.when(s == steps - 1)
        def _():
            o_ref[...] = acc[...].astype(o_ref.dtype)

    return pl.pallas_call(
        body, name=name, grid=(k // tk, n // tn, steps),
        in_specs=[pl.BlockSpec((tt, tk), lambda i, j, s: (s, i)), pl.BlockSpec((tt, tn), lambda i, j, s: (s, j))],
        out_specs=pl.BlockSpec((tk, tn), lambda i, j, s: (i, j)),
        out_shape=jax.ShapeDtypeStruct((k, n), out_dtype),
        scratch_shapes=[pltpu.VMEM((tk, tn), F32)],
        compiler_params=_params(3),
    )(a, b)


def _ffn_up(h, wg_t, wu_t, name, tm=1024, tn=FF_TILE):
    m, k = h.shape
    n = wg_t.shape[0]
    tm = min(tm, m)

    def body(h_ref, wg_ref, wu_ref, gate_ref, up_ref, ff_ref):
        hv = h_ref[...]
        gate = _dot_nt(hv, wg_ref[...])
        up = _dot_nt(hv, wu_ref[...])
        gate_ref[...] = gate.astype(gate_ref.dtype)
        up_ref[...] = up.astype(up_ref.dtype)
        ff_ref[...] = (gate * _sigmoid(gate) * up).astype(ff_ref.dtype)

    wspec = pl.BlockSpec((tn, k), lambda j, i: (j, 0))
    ospec = pl.BlockSpec((tm, tn), lambda j, i: (i, j))
    return pl.pallas_call(
        body, name=name, grid=(n // tn, m // tm),
        in_specs=[pl.BlockSpec((tm, k), lambda j, i: (i, 0)), wspec, wspec],
        out_specs=[ospec, ospec, ospec],
        out_shape=[jax.ShapeDtypeStruct((m, n), MXU_DTYPE)] * 3,
        compiler_params=_params(2),
    )(h, wg_t, wu_t)


def _ffn_bwd_act(dx, wd, gate, up, name, tm=1024, tn=FF_TILE):
    m, k = dx.shape
    n = wd.shape[0]
    tm = min(tm, m)

    def body(dx_ref, wd_ref, gate_ref, up_ref, dgate_ref, dup_ref):
        dff = _dot_nt(dx_ref[...], wd_ref[...])
        gate = gate_ref[...].astype(F32)
        sg = _sigmoid(gate)
        dgate_ref[...] = (dff * up_ref[...].astype(F32) * sg * (1.0 + gate * (1.0 - sg))).astype(dgate_ref.dtype)
        dup_ref[...] = (dff * gate * sg).astype(dup_ref.dtype)

    ospec = pl.BlockSpec((tm, tn), lambda j, i: (i, j))
    return pl.pallas_call(
        body, name=name, grid=(n // tn, m // tm),
        in_specs=[pl.BlockSpec((tm, k), lambda j, i: (i, 0)), pl.BlockSpec((tn, k), lambda j, i: (j, 0)),
                  ospec, ospec],
        out_specs=[ospec, ospec],
        out_shape=[jax.ShapeDtypeStruct((m, n), MXU_DTYPE), jax.ShapeDtypeStruct((m, n), MXU_DTYPE)],
        compiler_params=_params(2),
    )(dx, wd, gate, up)


def _chunk_masks():
    r = lax.broadcasted_iota(jnp.int32, (SUPER, SUPER), 0)
    c = lax.broadcasted_iota(jnp.int32, (SUPER, SUPER), 1)
    same = jnp.right_shift(r, 6) == jnp.right_shift(c, 6)
    lower = jnp.logical_and(same, c <= r)
    upper = jnp.logical_and(same, c >= r)
    return same, lower, upper


def _head_block_mask():
    r = lax.broadcasted_iota(jnp.int32, (LANES, LANES), 0)
    c = lax.broadcasted_iota(jnp.int32, (LANES, LANES), 1)
    return jnp.where(jnp.right_shift(r, 6) == jnp.right_shift(c, 6), 1.0, 0.0)


def _lower_bound(lb_raw):
    return 1.0 / (1.0 + jnp.exp(lb_raw[1:2, :] - lb_raw[0:1, :]))


HGRN_UNROLL = 4
PER_SUPER = SUPER // CHUNK
CHUNK_ROWS = [slice(c * CHUNK, (c + 1) * CHUNK) for c in range(PER_SUPER)]


def _over_chunks(rows):
    return jnp.concatenate([jnp.broadcast_to(r, (CHUNK, LANES)) for r in rows], axis=0)


def _hgrn_gates(q, hf, lb, lower_b):
    sig = _sigmoid(hf)
    f = lb + (1.0 - lb) * sig
    k = 1.0 - f
    lf = jnp.log(f)
    b = _rsum_left(lower_b, lf, 2)
    ends = [b[cr.stop - 1:cr.stop, :] for cr in CHUNK_ROWS]
    eb = jnp.exp(b)
    enb = jnp.exp(-b)
    edb = jnp.exp(_over_chunks(ends) - b)
    decs = [jnp.exp(e) for e in ends]
    return sig, f, k, decs, eb, enb, edb, q * eb, k * enb, k * edb


def _hgrn_fwd(proj, lower_bounds, norm_g, n_seq, seq):
    t = n_seq * seq
    n_super = seq // SUPER
    n_chunks = seq // CHUNK

    def body(q_ref, f_ref, i_ref, g_ref, lb_ref, ng_ref, out_ref, opre_ref, st_ref):
        masks = _head_masks()
        _, lower, _ = _chunk_masks()
        lower_b = _ones_where(lower)
        bd = _head_block_mask()
        lb = _lower_bound(lb_ref[...])
        ng = ng_ref[...]

        def step(it, st):
            blocks = [HGRN_UNROLL * it + u for u in range(HGRN_UNROLL)]
            rows = [pl.ds(pl.multiple_of(sb * SUPER, SUPER), SUPER) for sb in blocks]
            vs = [i_ref[rw, :] for rw in rows]
            gates = [_hgrn_gates(q_ref[rw, :], f_ref[rw, :], lb, lower_b) for rw in rows]
            decs, qes, kes, kds = ([g[k] for g in gates] for k in (3, 7, 8, 9))
            scores = [[_dot_nt(qe * m, ke) for m in masks] for qe, ke in zip(qes, kes)]
            updates = [[_dot_tn(v[cr], kd[cr]) for cr in CHUNK_ROWS] for v, kd in zip(vs, kds)]
            states = [st]
            for dec_b, upd_b in zip(decs, updates):
                for dec, upd in zip(dec_b, upd_b):
                    states.append(states[-1] * dec + bd * upd)
            for u, sb in enumerate(blocks):
                for c in range(PER_SUPER):
                    st_ref[0, 0, sb * PER_SUPER + c] = states[u * PER_SUPER + c]
            intra = [[_dot(jnp.where(lower, p, 0.0), v) for p in sc] for sc, v in zip(scores, vs)]
            inter = [[_dot_nt(qe[cr], states[u * PER_SUPER + c]) for c, cr in enumerate(CHUNK_ROWS)]
                     for u, qe in enumerate(qes)]
            for rw, intra_b, inter_b in zip(rows, intra, inter):
                o = intra_b[0] * masks[0] + intra_b[1] * masks[1] + jnp.concatenate(inter_b, axis=0)
                opre_ref[rw, :] = o
                hg = g_ref[rw, :]
                on = o * _head_rstd(o, masks) * ng
                out_ref[rw, :] = (on * hg * _sigmoid(hg)).astype(out_ref.dtype)
            return states[-1]

        lax.fori_loop(0, n_super // HGRN_UNROLL, step, jnp.zeros((LANES, LANES), F32))

    def col(k):
        return pl.BlockSpec((seq, LANES), lambda p, b: (b, k * N_PAIRS + p))

    vec = lambda rows: pl.BlockSpec((rows, LANES), lambda p, b: (0, p))
    ospec = pl.BlockSpec((seq, LANES), lambda p, b: (b, p))
    return pl.pallas_call(
        body, name="hgrn_fwd", grid=(N_PAIRS, n_seq),
        in_specs=[col(0), col(1), col(2), col(3), vec(2), vec(1)],
        out_specs=[ospec, ospec,
                   pl.BlockSpec((1, 1, n_chunks, LANES, LANES), lambda p, b: (b, p, 0, 0, 0))],
        out_shape=[jax.ShapeDtypeStruct((t, 2 * GROUP), MXU_DTYPE), jax.ShapeDtypeStruct((t, GROUP), F32),
                   jax.ShapeDtypeStruct((n_seq, N_PAIRS, n_chunks, LANES, LANES), F32)],
        compiler_params=_params(2),
    )(proj, proj, proj, proj, lower_bounds, norm_g)


def _hgrn_bwd(proj, lower_bounds, norm_g, dmix, opre, states, n_seq, seq):
    t = n_seq * seq
    n_super = seq // SUPER
    n_chunks = seq // CHUNK
    per = SUPER // CHUNK

    def body(q_ref, f_ref, i_ref, g_ref, lb_ref, ng_ref, dm_ref, opre_ref, st_ref,
             dq_ref, df_ref, di_ref, dg_ref, dlb_ref, dng_ref):
        masks = _head_masks()
        _, lower, upper = _chunk_masks()
        lower_b, upper_b = _ones_where(lower), _ones_where(upper)
        bd = _head_block_mask()
        lb_raw = lb_ref[...]
        lb = _lower_bound(lb_raw)
        ng = ng_ref[...]

        @pl.when(pl.program_id(1) == 0)
        def _():
            dlb_ref[...] = jnp.zeros_like(dlb_ref)
            dng_ref[...] = jnp.zeros_like(dng_ref)

        def first_half(sb):
            rows = pl.ds(pl.multiple_of(sb * SUPER, SUPER), SUPER)
            q, hf, v, hg = q_ref[rows, :], f_ref[rows, :], i_ref[rows, :], g_ref[rows, :]
            sig, f, k, decs, eb, enb, edb, qe, ke, kd = _hgrn_gates(q, hf, lb, lower_b)
            o = opre_ref[rows, :]
            r = _head_rstd(o, masks)
            oh = o * r
            dm = dm_ref[rows, :]
            sg = _sigmoid(hg)
            dg_ref[rows, :] = (dm * oh * ng * sg * (1.0 + hg * (1.0 - sg))).astype(dg_ref.dtype)
            don = dm * hg * sg
            dng_ref[...] += jnp.sum(don * oh, axis=0, keepdims=True)
            doh = don * ng
            do = r * (doh - oh * _head_mean(doh * oh, masks))
            doms = [do * m for m in masks]
            qems = [qe * m for m in masks]
            scores = [_dot_nt(qem, ke) for qem in qems]
            dscores = [_dot_nt(dom, v) for dom in doms]
            prevs = [st_ref[0, 0, sb * per + c] for c in range(per)]
            dst_in = [_dot_tn(do[cr], qe[cr]) for cr in CHUNK_ROWS]
            dqe_i = [_dot(do[cr], prev) for cr, prev in zip(CHUNK_ROWS, prevs)]
            return dict(rows=rows, v=v, sig=sig, f=f, decs=decs, eb=eb, enb=enb, edb=edb, qe=qe, ke=ke, kd=kd,
                        doms=doms, qems=qems, scores=scores, dscores=dscores, prevs=prevs, dst_in=dst_in, dqe_i=dqe_i)

        def second_half(blk, dsts):
            v, qe, ke, kd = blk["v"], blk["qe"], blk["ke"], blk["kd"]
            ps = [jnp.where(lower, p, 0.0) for p in blk["scores"]]
            dps = [jnp.where(lower, dp, 0.0) for dp in blk["dscores"]]
            dqe_h = [_dot(dp, ke) for dp in dps]
            dke_h = [_dot_tn(dp, qem) for dp, qem in zip(dps, blk["qems"])]
            dv_h = [_dot_tn(p, dom) for p, dom in zip(ps, blk["doms"])]
            dus = [bd * d for d in dsts]
            dv_i = [_dot_nt(kd[cr], du) for cr, du in zip(CHUNK_ROWS, dus)]
            dkd_i = [_dot(v[cr], du) for cr, du in zip(CHUNK_ROWS, dus)]

            def finish():
                dqe = dqe_h[0] * masks[0] + dqe_h[1] * masks[1] + jnp.concatenate(blk["dqe_i"], axis=0)
                dke = dke_h[0] + dke_h[1]
                dv = dv_h[0] + dv_h[1] + jnp.concatenate(dv_i, axis=0)
                dkd = jnp.concatenate(dkd_i, axis=0)
                dk = dke * blk["enb"] + dkd * blk["edb"]
                db = dqe * qe - dke * ke - dkd * kd
                dkd_kd = dkd * kd
                dends = [jnp.sum(dkd_kd[cr], axis=0, keepdims=True)
                         + jnp.sum(dsts[c] * blk["prevs"][c], axis=0, keepdims=True) * blk["decs"][c]
                         for c, cr in enumerate(CHUNK_ROWS)]
                dlf = _rsum_left(upper_b, db, 2) + _over_chunks(dends)
                sig = blk["sig"]
                dfv = dlf / blk["f"] - dk
                rows = blk["rows"]
                dq_ref[rows, :] = (dqe * blk["eb"]).astype(dq_ref.dtype)
                di_ref[rows, :] = dv.astype(di_ref.dtype)
                df_ref[rows, :] = (dfv * (1.0 - lb) * sig * (1.0 - sig)).astype(df_ref.dtype)
                dlb = jnp.sum(dfv * (1.0 - sig), axis=0, keepdims=True)
                da0 = dlb * lb * (1.0 - lb)
                dlb_ref[0:1, :] += da0
                dlb_ref[1:2, :] -= da0

            return finish

        def step(it, dst):
            blocks = [first_half(n_super - 1 - HGRN_UNROLL * it - u) for u in range(HGRN_UNROLL)]
            all_dsts = []
            for blk in blocks:
                dsts = [None] * per
                for c in reversed(range(per)):
                    dsts[c] = dst
                    dst = bd * (dst * blk["decs"][c] + blk["dst_in"][c])
                all_dsts.append(dsts)
            for finish in [second_half(blk, dsts) for blk, dsts in zip(blocks, all_dsts)]:
                finish()
            return dst

        lax.fori_loop(0, n_super // HGRN_UNROLL, step, jnp.zeros((LANES, LANES), F32))

    def col(k):
        return pl.BlockSpec((seq, LANES), lambda p, b: (b, k * N_PAIRS + p))

    vec = lambda rows: pl.BlockSpec((rows, LANES), lambda p, b: (0, p))
    ospec = pl.BlockSpec((seq, LANES), lambda p, b: (b, p))
    piece = jax.ShapeDtypeStruct((t, GROUP), MXU_DTYPE)
    return pl.pallas_call(
        body, name="hgrn_bwd", grid=(N_PAIRS, n_seq),
        in_specs=[col(0), col(1), col(2), col(3), vec(2), vec(1), ospec, ospec,
                  pl.BlockSpec((1, 1, n_chunks, LANES, LANES), lambda p, b: (b, p, 0, 0, 0))],
        out_specs=[ospec, ospec, ospec, ospec, vec(2), vec(1)],
        out_shape=[piece, piece, piece, piece,
                   jax.ShapeDtypeStruct((2, GROUP), F32), jax.ShapeDtypeStruct((1, GROUP), F32)],
        compiler_params=_params(2),
    )(proj, proj, proj, proj, lower_bounds, norm_g, dmix, opre, states)


SB_SCALE = 1.0 / math.sqrt(D_HEAD)


SB_STEP = 2 * SB_BLOCK
QUERY_BLOCKS = (slice(0, SB_BLOCK), slice(SB_BLOCK, SB_STEP))


def _triangle(keep):
    row = lax.broadcasted_iota(jnp.int32, (SB_BLOCK, SB_BLOCK), 0)
    col = lax.broadcasted_iota(jnp.int32, (SB_BLOCK, SB_BLOCK), 1)
    return _ones_where(keep(row, col))


SB_HALF = SB_BLOCK // 2


def _keep(on_diagonal, x):
    if not on_diagonal:
        return x
    row = lax.broadcasted_iota(jnp.int32, (SB_HALF, SB_HALF), 0)
    col = lax.broadcasted_iota(jnp.int32, (SB_HALF, SB_HALF), 1)
    return jnp.where(col < row, x, 0.0)


def _lower_quadrants(fn, n_out, *tiles):
    def quadrant(r, c):
        return [t[r * SB_HALF:(r + 1) * SB_HALF, c * SB_HALF:(c + 1) * SB_HALF] for t in tiles]

    top, left, bottom = fn(True, *quadrant(0, 0)), fn(False, *quadrant(1, 0)), fn(True, *quadrant(1, 1))
    zero = jnp.zeros((SB_HALF, SB_HALF), F32)
    return [jnp.concatenate([jnp.concatenate([top[o], zero], axis=1), jnp.concatenate([left[o], bottom[o]], axis=1)],
                            axis=0) for o in range(n_out)]


def _sb_fwd(proj, norm_g, mixed, shards, n_seq, seq):
    t = n_seq * seq
    nq = seq // SB_STEP
    q0, k0, v0 = 0, N_PAIRS, 2 * N_PAIRS
    n_w = len(shards)
    n_steps = N_PAIRS * n_seq * nq
    tri = _triangle(lambda row, col: row >= col)

    def body(q_ref, k_ref, v_ref, ng_ref, tri_ref, mixed_in, *rest):
        del mixed_in
        shard_refs = rest[:n_w]
        out_ref, opre_ref, ctot_ref = rest[n_w:n_w + 3]
        gathered = rest[n_w + 3:2 * n_w + 3]
        send_sems, recv_sems, local_sems = rest[2 * n_w + 3:]
        i = pl.program_id(2)
        step = (pl.program_id(0) * n_seq + pl.program_id(1)) * nq + i
        plan = _GatherPlan(shard_refs, gathered, send_sems, recv_sems, local_sems)

        @pl.when(step == 0)
        def _():
            plan.start()

        @pl.when(step == (3 * n_steps) // 4)
        def _():
            plan.forward()

        masks = _head_masks()
        suffix = tri_ref[...]
        qhs = [[(q_ref[blk, :] * SB_SCALE * m).astype(MXU_DTYPE) for m in masks] for blk in QUERY_BLOCKS]

        def tiles(js, work, carry):
            rows = [pl.ds(pl.multiple_of(j * SB_BLOCK, SB_BLOCK), SB_BLOCK) for j in js]
            ks = [k_ref[rw, :].astype(MXU_DTYPE) for rw in rows]
            vs = [v_ref[rw, :].astype(MXU_DTYPE) for rw in rows]
            zs = [[_dot_nt(qh, ks[ts]) for qh in qhs[qb]] for qb, ts, _ in work]
            ccs = [[_lower_quadrants(lambda on, zq: (_keep(on, _softplus(zq)),), 1, z)[0] if diag else _softplus(z)
                    for z in zw] for zw, (_, _, diag) in zip(zs, work)]
            sums = [[jnp.dot(cc.astype(BF16), suffix, preferred_element_type=F32) for cc in cw] for cw in ccs]
            out = [list(per_block) for per_block in carry]
            for h in range(len(masks)):
                for w, (qb, ts, diag) in enumerate(work):
                    run, acc = out[qb][h]
                    logit = zs[w][h] - (sums[w][h] + run)
                    if diag:
                        a = _lower_quadrants(lambda on, lq: (_keep(on, jnp.exp(lq)),), 1, logit)[0]
                    else:
                        a = jnp.exp(logit)
                    out[qb][h] = (run + sums[w][h][:, 0:1], acc + _dot(a, vs[ts]))
            return tuple(tuple(per_block) for per_block in out)

        start = ((jnp.zeros((SB_BLOCK, 1), F32), jnp.zeros((SB_BLOCK, LANES), F32)),) * 2
        carry = tiles([2 * i, 2 * i + 1], [(0, 0, True), (1, 1, True), (1, 0, False)], (start, start))
        both = [(0, 0, False), (0, 1, False), (1, 0, False), (1, 1, False)]
        carry = lax.fori_loop(0, i, lambda s, cy: tiles([2 * (i - s) - 1, 2 * (i - s) - 2], both, cy), carry)
        opre = jnp.concatenate([cb[0][1] * masks[0] + cb[1][1] * masks[1] for cb in carry], axis=0)
        ctot = jnp.concatenate([cb[0][0] * masks[0] + cb[1][0] * masks[1] for cb in carry], axis=0)
        opre_ref[...] = opre
        ctot_ref[...] = ctot
        out_ref[...] = (opre * _head_rstd(opre, masks) * ng_ref[...]).astype(out_ref.dtype)

        @pl.when(step == n_steps - 1)
        def _():
            plan.finish()

    qspec = pl.BlockSpec((SB_STEP, LANES), lambda p, b, i: (b * nq + i, q0 + p))
    ospec = pl.BlockSpec((SB_STEP, LANES), lambda p, b, i: (b * nq + i, p))
    hbm = pl.BlockSpec(memory_space=pltpu.HBM)
    outs = pl.pallas_call(
        body, name="sb_fwd", grid=(N_PAIRS, n_seq, nq),
        in_specs=[qspec,
                  pl.BlockSpec((seq, LANES), lambda p, b, i: (b, k0 + p)),
                  pl.BlockSpec((seq, LANES), lambda p, b, i: (b, v0 + p)),
                  pl.BlockSpec((1, LANES), lambda p, b, i: (0, p)),
                  pl.BlockSpec(tri.shape, lambda p, b, i: (0, 0)), hbm] + [hbm] * n_w,
        out_specs=[pl.BlockSpec((SB_STEP, LANES), lambda p, b, i: (b * nq + i, N_PAIRS + p)), ospec, ospec]
        + [hbm] * n_w,
        out_shape=[jax.ShapeDtypeStruct(mixed.shape, mixed.dtype), jax.ShapeDtypeStruct((t, GROUP), F32),
                   jax.ShapeDtypeStruct((t, GROUP), F32)]
        + [jax.ShapeDtypeStruct((N_DEV,) + s.shape, s.dtype) for s in shards],
        scratch_shapes=[pltpu.SemaphoreType.DMA((n_w * _GatherPlan.COPIES,)),
                        pltpu.SemaphoreType.DMA((n_w * _GatherPlan.COPIES,)), pltpu.SemaphoreType.DMA((n_w,))],
        input_output_aliases={5: 0},
        compiler_params=_params(3),
    )(proj, proj, proj, norm_g, tri, mixed, *shards)
    return outs[0], outs[1], outs[2], list(outs[3:])


def _sb_bwd(proj, norm_g, dmix, opre, ctot, grads, n_seq, seq):
    t = n_seq * seq
    nq = seq // SB_STEP
    q0, k0, v0 = 0, N_PAIRS, 2 * N_PAIRS
    n_w = len(grads)
    n_steps = N_PAIRS * n_seq * nq
    tri = _triangle(lambda row, col: row <= col)

    def body(q_ref, k_ref, v_ref, ng_ref, tri_ref, dm_ref, opre_ref, ctot_ref, *rest):
        grad_refs = rest[:n_w]
        dq_ref, dk_ref, dv_ref, dng_ref = rest[n_w:n_w + 4]
        lands = rest[n_w + 4:2 * n_w + 4]
        dk_acc, dv_acc, send_sems, recv_sems, local_sems = rest[2 * n_w + 4:]
        p_id, b_id, i = pl.program_id(0), pl.program_id(1), pl.program_id(2)
        step = (p_id * n_seq + b_id) * nq + i
        plan = _ScatterPlan(grad_refs, lands, send_sems, recv_sems, local_sems)

        @pl.when(step == 0)
        def _():
            plan.start()

        masks = _head_masks()
        upto = tri_ref[...]

        def prefix(x):
            return jnp.dot(x.astype(BF16), upto, preferred_element_type=F32)

        @pl.when(i == 0)
        def _():
            dk_acc[...] = jnp.zeros_like(dk_acc)
            dv_acc[...] = jnp.zeros_like(dv_acc)

        @pl.when(jnp.logical_and(b_id == 0, i == 0))
        def _():
            dng_ref[...] = jnp.zeros_like(dng_ref)

        o = opre_ref[...]
        rs = _head_rstd(o, masks)
        oh = o * rs
        dm = dm_ref[...]
        dng_ref[...] += jnp.sum(dm * oh, axis=0, keepdims=True)
        doh = dm * ng_ref[...]
        do = rs * (doh - oh * _head_mean(doh * oh, masks))

        heads = range(len(masks))
        qs = [q_ref[blk, :] * SB_SCALE for blk in QUERY_BLOCKS]
        dos = [do[blk] for blk in QUERY_BLOCKS]
        qhs = [[(q * m).astype(MXU_DTYPE) for m in masks] for q in qs]
        doms = [[(d * m).astype(MXU_DTYPE) for m in masks] for d in dos]
        head_rows = [jnp.where(jnp.right_shift(lax.broadcasted_iota(jnp.int32, (LANES, 1), 0), 6) == h, 1.0, 0.0)
                     for h in heads]
        qhts = [[(qt * hr).astype(MXU_DTYPE) for hr in head_rows] for qt in [q.astype(F32).T for q in qs]]
        domts = [[(dt * hr).astype(MXU_DTYPE) for hr in head_rows] for dt in [d.T for d in dos]]
        totals = [[ctot_ref[blk, h * D_HEAD:h * D_HEAD + 1] for h in heads] for blk in QUERY_BLOCKS]

        def tiles(js, work, carry):
            rows = [pl.ds(pl.multiple_of(j * SB_BLOCK, SB_BLOCK), SB_BLOCK) for j in js]
            ks = [k_ref[rw, :].astype(MXU_DTYPE) for rw in rows]
            vs = [v_ref[rw, :].astype(MXU_DTYPE) for rw in rows]
            zs = [[_dot_nt(qhs[qb][h], ks[ts]) for h in heads] for qb, ts, _ in work]
            das = [[_dot_nt(doms[qb][h], vs[ts]) for h in heads] for qb, ts, _ in work]
            def gates(on, zq):
                sp = _softplus(zq)
                return _keep(on, sp), zq - sp, _keep(on, jnp.exp(zq - sp))

            trio = [[_lower_quadrants(gates, 3, z) if diag else gates(False, z) for z in zw]
                    for zw, (_, _, diag) in zip(zs, work)]
            ccs, lsigs, sigs = ([[t[o] for t in tw] for tw in trio] for o in range(3))
            pres = [[prefix(cc) for cc in cw] for cw in ccs]
            out = [list(per_block) for per_block in carry]
            for h in heads:
                for w, (qb, ts, diag) in enumerate(work):
                    pc, pdl, dq_h = out[qb][h]
                    logit = lsigs[w][h] + pres[w][h] - (totals[qb][h] - pc)
                    if diag:
                        a = _lower_quadrants(lambda on, lq: (_keep(on, jnp.exp(lq)),), 1, logit)[0]
                    else:
                        a = jnp.exp(logit)
                    dl = a * das[w][h]
                    dv_acc[js[ts]] += _dot(domts[qb][h], a)
                    dpre = prefix(dl)
                    dz = dl - sigs[w][h] * (pdl + dpre)
                    dzb = dz.astype(MXU_DTYPE)
                    dk_acc[js[ts]] += _dot(qhts[qb][h], dzb)
                    out[qb][h] = (pc + pres[w][h][:, SB_BLOCK - 1:SB_BLOCK], pdl + dpre[:, SB_BLOCK - 1:SB_BLOCK],
                                  dq_h + _dot(dzb, ks[ts]))
            return tuple(tuple(per_block) for per_block in out)

        zero = jnp.zeros((SB_BLOCK, 1), F32)
        start = ((zero, zero, jnp.zeros((SB_BLOCK, LANES), F32)),) * 2
        both = [(0, 0, False), (0, 1, False), (1, 0, False), (1, 1, False)]
        carry = lax.fori_loop(0, i, lambda s, cy: tiles([2 * s, 2 * s + 1], both, cy), (start, start))
        carry = tiles([2 * i, 2 * i + 1], [(0, 0, True), (1, 0, False), (1, 1, True)], carry)
        dq = jnp.concatenate([cb[0][2] * masks[0] + cb[1][2] * masks[1] for cb in carry], axis=0)
        dq_ref[...] = (dq * SB_SCALE).astype(dq_ref.dtype)

        @pl.when(i == nq - 1)
        def _():
            for j in range(seq // SB_BLOCK):
                tile_rows = slice(j * SB_BLOCK, (j + 1) * SB_BLOCK)
                dk_ref[tile_rows, :] = dk_acc[j].T.astype(dk_ref.dtype)
                dv_ref[tile_rows, :] = dv_acc[j].T.astype(dv_ref.dtype)

        @pl.when(step == n_steps - 1)
        def _():
            plan.finish()

    qspec = pl.BlockSpec((SB_STEP, LANES), lambda p, b, i: (b * nq + i, q0 + p))
    ospec = pl.BlockSpec((SB_STEP, LANES), lambda p, b, i: (b * nq + i, p))
    dmspec = pl.BlockSpec((SB_STEP, LANES), lambda p, b, i: (b * nq + i, N_PAIRS + p))
    full = lambda k: pl.BlockSpec((seq, LANES), lambda p, b, i: (b, k + p))
    vec = pl.BlockSpec((1, LANES), lambda p, b, i: (0, p))
    hbm = pl.BlockSpec(memory_space=pltpu.HBM)
    piece = jax.ShapeDtypeStruct((t, GROUP), MXU_DTYPE)
    outs = pl.pallas_call(
        body, name="sb_bwd", grid=(N_PAIRS, n_seq, nq),
        in_specs=[qspec, full(k0), full(v0), vec, pl.BlockSpec(tri.shape, lambda p, b, i: (0, 0)), dmspec, ospec, ospec]
        + [hbm] * n_w,
        out_specs=[ospec, full(0), full(0), vec] + [hbm] * n_w,
        out_shape=[piece, piece, piece, jax.ShapeDtypeStruct((1, GROUP), F32)]
        + [jax.ShapeDtypeStruct(g.shape, g.dtype) for g in grads],
        scratch_shapes=[pltpu.VMEM((seq // SB_BLOCK, LANES, SB_BLOCK), F32),
                        pltpu.VMEM((seq // SB_BLOCK, LANES, SB_BLOCK), F32),
                        pltpu.SemaphoreType.DMA((n_w * (N_DEV - 1),)), pltpu.SemaphoreType.DMA((n_w * (N_DEV - 1),)),
                        pltpu.SemaphoreType.DMA((n_w,))],
        compiler_params=_params(3),
    )(proj, proj, proj, norm_g, tri, dmix, opre, ctot, *grads)
    return outs[0], outs[1], outs[2], outs[3], list(outs[4:])


def _mesh_place():
    x, y, c = lax.axis_index("x"), lax.axis_index("y"), lax.axis_index("c")
    return x, y, c


def _peer(x, y, c, k):
    px = lax.rem(x + ((k >> 2) & 1), 2)
    py = lax.rem(y + ((k >> 1) & 1), 2)
    pc = lax.rem(c + (k & 1), 2)
    return (px, py, pc), 4 * px + 2 * py + pc


def _remote(src, dst, send_sem, recv_sem, to):
    return pltpu.make_async_remote_copy(src_ref=src, dst_ref=dst, send_sem=send_sem, recv_sem=recv_sem,
                                        device_id=to, device_id_type=pl.DeviceIdType.MESH)


class _GatherPlan:
    COPIES = 7

    def __init__(self, shards, gathered, send_sems, recv_sems, local_sems):
        x, y, c = _mesh_place()
        self.c = c
        self.me = (x, y, c)
        self.sibling = (x, y, 1 - c)
        self.chips = [(1 - x, y), (x, 1 - y), (1 - x, 1 - y)]
        self.tensors = list(zip(shards, gathered))
        self.send_sems, self.recv_sems, self.local_sems = send_sems, recv_sems, local_sems

    @staticmethod
    def _index(place):
        return 4 * place[0] + 2 * place[1] + place[2]

    def _copy(self, w, k, block, to, own=False):
        shard, gathered = self.tensors[w]
        slot = gathered.at[self._index(block)]
        n = w * self.COPIES + k
        return _remote(shard if own else slot, slot, self.send_sems.at[n], self.recv_sems.at[n], to)

    def _local(self, w):
        shard, gathered = self.tensors[w]
        return pltpu.make_async_copy(shard, gathered.at[self._index(self.me)], self.local_sems.at[w])

    def _first(self, w):
        return [self._copy(w, 0, self.me, self.sibling, own=True)] + [
            self._copy(w, 1 + j, self.me, (*chip, self.c), own=True) for j, chip in enumerate(self.chips)]

    def _passed(self, w):
        return [self._copy(w, 4 + j, (*chip, self.c), self.sibling) for j, chip in enumerate(self.chips)]

    def start(self):
        for w in range(len(self.tensors)):
            self._local(w).start()
            for cp in self._first(w):
                cp.start()

    def forward(self):
        for w in range(len(self.tensors)):
            passed = self._passed(w)
            for j, chip in enumerate(self.chips):
                self._copy(w, 1 + j, (*chip, self.c), self.me).wait_recv()
                passed[j].start()

    def finish(self):
        for w in range(len(self.tensors)):
            self._copy(w, 0, self.sibling, self.me).wait_recv()
            for j, chip in enumerate(self.chips):
                self._copy(w, 4 + j, (*chip, 1 - self.c), self.me).wait_recv()
            for cp in self._first(w) + self._passed(w):
                cp.wait_send()
            self._local(w).wait()


class _ScatterPlan:
    def __init__(self, grads, lands, send_sems, recv_sems, local_sems):
        self.place = _mesh_place()
        x, y, c = self.place
        self.me = 4 * x + 2 * y + c
        self.tensors = list(zip(grads, lands))
        self.send_sems, self.recv_sems, self.local_sems = send_sems, recv_sems, local_sems

    def _copies(self, w):
        grad, land = self.tensors[w]
        out = []
        for k in range(1, N_DEV):
            peer, pidx = _peer(*self.place, k)
            n = w * (N_DEV - 1) + k - 1
            sems = (self.send_sems.at[n], self.recv_sems.at[n], peer)
            out.append((_remote(grad.at[pidx], land.at[self.me], *sems), _remote(grad.at[pidx], land.at[pidx], *sems)))
        return out

    def _local(self, w):
        grad, land = self.tensors[w]
        return pltpu.make_async_copy(grad.at[self.me], land.at[self.me], self.local_sems.at[w])

    def start(self):
        for w in range(len(self.tensors)):
            self._local(w).start()
            for send, _ in self._copies(w):
                send.start()

    def finish(self):
        for w in range(len(self.tensors)):
            copies = self._copies(w)
            for _, arrival in copies:
                arrival.wait_recv()
            for send, _ in copies:
                send.wait_send()
            self._local(w).wait()


def _cast_shards(shards):
    def body(*refs):
        n = len(refs) // 2
        for src, dst in zip(refs[:n], refs[n:]):
            dst[...] = src[...].astype(dst.dtype)

    vmem = pl.BlockSpec(memory_space=pltpu.VMEM)
    return pl.pallas_call(
        body, name="cast_shards", in_specs=[vmem] * len(shards), out_specs=[vmem] * len(shards),
        out_shape=[jax.ShapeDtypeStruct(s.shape, BF16) for s in shards],
        compiler_params=pltpu.CompilerParams(vmem_limit_bytes=VMEM_LIMIT),
    )(*shards)


def _gather_w_in(shard, x, g, chunk=512):
    rows, cols = shard.shape
    t, d = x.shape
    chunk = min(chunk, t)
    n_chunks = t // chunk

    def body(w_ref, x_ref, g_ref, out_ref, h_ref, x_buf, send_sems, recv_sems, local_sems, x_sems):
        plan = _GatherPlan([w_ref], [out_ref], send_sems, recv_sems, local_sems)
        plan.start()

        def fetch(j):
            return pltpu.make_async_copy(x_ref.at[pl.ds(j * chunk, chunk)], x_buf.at[j % 2], x_sems.at[j % 2])

        fetch(0).start()
        gv = g_ref[...]
        for j in range(n_chunks):
            if j + 1 < n_chunks:
                fetch(j + 1).start()
            fetch(j).wait()
            xv = x_buf[j % 2]
            r = lax.rsqrt(jnp.mean(xv * xv, axis=-1, keepdims=True) + EPS)
            h_ref[j * chunk:(j + 1) * chunk, :] = (xv * r * gv).astype(h_ref.dtype)
        plan.forward()
        plan.finish()

    vmem = pl.BlockSpec(memory_space=pltpu.VMEM)
    return pl.pallas_call(
        body, name="gather_w_in", in_specs=[vmem, pl.BlockSpec(memory_space=pltpu.HBM), vmem], out_specs=[vmem, vmem],
        out_shape=[jax.ShapeDtypeStruct((N_DEV, rows, cols), shard.dtype), jax.ShapeDtypeStruct((t, d), MXU_DTYPE)],
        scratch_shapes=[pltpu.VMEM((2, chunk, d), F32), pltpu.SemaphoreType.DMA((_GatherPlan.COPIES,)),
                        pltpu.SemaphoreType.DMA((_GatherPlan.COPIES,)), pltpu.SemaphoreType.DMA((1,)),
                        pltpu.SemaphoreType.DMA((2,))],
        compiler_params=pltpu.CompilerParams(vmem_limit_bytes=VMEM_LIMIT),
    )(shard, x, g)


def _dw_rows(pieces, b, name, tn=512, tt=1024):
    t, n = b.shape
    widths = [p.shape[1] for p in pieces]
    rows = sum(widths)
    tt = min(tt, t)
    steps = t // tt
    n_p = len(pieces)

    def body(*refs):
        piece_refs, b_ref, o_ref, acc = refs[:n_p], refs[n_p], refs[n_p + 1], refs[n_p + 2]
        s = pl.program_id(1)

        @pl.when(s == 0)
        def _():
            acc[...] = jnp.zeros_like(acc)

        bv = b_ref[...]
        off = 0
        for p_ref, width in zip(piece_refs, widths):
            acc[off:off + width, :] += _dot_tn(p_ref[...], bv)
            off += width

        @pl.when(s == steps - 1)
        def _():
            o_ref[...] = acc[...].astype(o_ref.dtype)

    return pl.pallas_call(
        body, name=name, grid=(n // tn, steps),
        in_specs=[pl.BlockSpec((tt, width), lambda j, s: (s, 0)) for width in widths]
        + [pl.BlockSpec((tt, tn), lambda j, s: (s, j))],
        out_specs=pl.BlockSpec((rows, tn), lambda j, s: (0, j)),
        out_shape=jax.ShapeDtypeStruct((rows, n), BF16),
        scratch_shapes=[pltpu.VMEM((rows, tn), F32)],
        compiler_params=_params(2),
    )(*pieces, b)


def _dh_norm_bwd(pairs, x, g, res, name, tm=512, after=(), next_w=None):
    m, d = x.shape
    tm = min(tm, m)
    n_p = len(pairs)
    n_steps = m // tm
    n_in = 2 * n_p + 3 + len(after) + (next_w is not None)

    def body(*refs):
        x_ref, g_ref, res_ref = refs[2 * n_p:2 * n_p + 3]
        dx_ref, dg_ref = refs[n_in:n_in + 2]

        @pl.when(pl.program_id(0) == 0)
        def _():
            dg_ref[...] = jnp.zeros_like(dg_ref)

        dh = None
        for q in range(n_p):
            part = _dot(refs[2 * q][...], refs[2 * q + 1][...])
            dh = part if dh is None else dh + part
        xv = x_ref[...]
        r = lax.rsqrt(jnp.mean(xv * xv, axis=-1, keepdims=True) + EPS)
        xh = xv * r
        dxh = dh * g_ref[...]
        dxv = res_ref[...] + r * (dxh - xh * jnp.mean(dxh * xh, axis=-1, keepdims=True))
        dx_ref[...] = dxv
        dg_ref[...] += jnp.sum(dh * xh, axis=0, keepdims=True)
        if next_w is not None:
            refs[n_in + 2][...] = _dot_nt(dxv, refs[n_in - 1][...])

    in_specs, args = [], []
    for a, w, r in pairs:
        k = a.shape[1]
        in_specs += [pl.BlockSpec((tm, k), lambda i: (i, 0)),
                     pl.BlockSpec((k, d), functools.partial(lambda i, r: (r, 0), r=r or 0),
                                  pipeline_mode=pl.Buffered(1))]
        args += [a, w]
    row = pl.BlockSpec((tm, d), lambda i: (i, 0))
    vec = pl.BlockSpec((1, d), lambda i: (0, 0))
    extra_in, extra_out, extra_shape = [], [], []
    if next_w is not None:
        e = next_w.shape[0]
        extra_in = [pl.BlockSpec((e, d), lambda i: (0, 0), pipeline_mode=pl.Buffered(1))]
        extra_out = [pl.BlockSpec((tm, e), lambda i: (i, 0))]
        extra_shape = [jax.ShapeDtypeStruct((m, e), F32)]
        args_tail = (next_w,)
    else:
        args_tail = ()
    return pl.pallas_call(
        body, name=name, grid=(n_steps,),
        in_specs=in_specs + [row, vec, row] + [pl.BlockSpec(memory_space=pl.ANY)] * len(after) + extra_in,
        out_specs=[row, vec] + extra_out,
        out_shape=[jax.ShapeDtypeStruct((m, d), F32), jax.ShapeDtypeStruct((1, d), F32)] + extra_shape,
        compiler_params=_params(1),
    )(*args, x, g, res, *after, *args_tail)


def _exchange_start(grad, name):
    def body(g_ref, land_ref, send_sem, recv_sem, local_sem, g_thru, land_thru, token):
        del g_thru, land_thru
        x, y, c = _mesh_place()
        me = 4 * x + 2 * y + c
        pltpu.make_async_copy(g_ref.at[me], land_ref.at[me], local_sem).start()
        for k in range(1, N_DEV):
            peer, pidx = _peer(x, y, c, k)
            _remote(g_ref.at[pidx], land_ref.at[me], send_sem, recv_sem, peer).start()
        token[...] = jnp.zeros_like(token)

    hbm = pl.BlockSpec(memory_space=pltpu.HBM)
    sem = pl.BlockSpec(memory_space=pltpu.SEMAPHORE)
    buf = pltpu.HBM(grad.shape, grad.dtype)
    return pl.pallas_call(
        body, name=name,
        out_shape=(pltpu.SemaphoreType.DMA(()), pltpu.SemaphoreType.DMA(()), pltpu.SemaphoreType.DMA(()), buf, buf,
                   jax.ShapeDtypeStruct((8, LANES), F32)),
        in_specs=(hbm, hbm), out_specs=(sem, sem, sem, hbm, hbm, pl.BlockSpec(memory_space=pltpu.VMEM)),
        input_output_aliases={0: 3, 1: 4},
        compiler_params=pltpu.CompilerParams(has_side_effects=pltpu.SideEffectType.DATAFLOW_SIDE_EFFECTING),
    )(pltpu.with_memory_space_constraint(grad, pltpu.HBM),
      pltpu.with_memory_space_constraint(lax.empty(grad.shape, grad.dtype), pltpu.HBM))


def _exchange_wait(send_sem, recv_sem, local_sem, grad, land, after, name):
    def body(g_ref, land_ref, send_sem, recv_sem, local_sem, *rest):
        x, y, c = _mesh_place()
        peer, _ = _peer(x, y, c, 1)
        others = pl.ds(0, N_DEV - 1)
        seven = _remote(g_ref.at[others], land_ref.at[others], send_sem, recv_sem, peer)
        seven.wait_send()
        seven.wait_recv()
        pltpu.make_async_copy(g_ref.at[0], land_ref.at[0], local_sem).wait()

    hbm = pl.BlockSpec(memory_space=pltpu.HBM)
    sem = pl.BlockSpec(memory_space=pltpu.SEMAPHORE)
    buf = pltpu.HBM(grad.shape, grad.dtype)
    return pl.pallas_call(
        body, name=name, out_shape=(buf, buf),
        in_specs=(hbm, hbm, sem, sem, sem) + (pl.BlockSpec(memory_space=pl.ANY),) * len(after), out_specs=(hbm, hbm),
        input_output_aliases={0: 0, 1: 1},
        compiler_params=pltpu.CompilerParams(has_side_effects=pltpu.SideEffectType.DATAFLOW_SIDE_EFFECTING),
    )(grad, land, send_sem, recv_sem, local_sem, *after)[1]


SMALL_LAYOUT = ((0, 0, 0, 0, D_MODEL), (1, 0, 1, 0, GROUP), (1, 1, 1, GROUP, GROUP), (2, 0, 2, 0, GROUP),
                (3, 0, 2, GROUP, GROUP), (4, 0, 3, 0, D_MODEL), (5, 0, 4, 0, D_MODEL))
LOSS_ROW = 5
N_SMALL = 6


def _small_step(grads, loss_part, ws, ms, vs):
    def body(*refs):
        g_in, loss_in = refs[:N_SMALL], refs[N_SMALL]
        params = [refs[1 + (q + 1) * N_SMALL:1 + (q + 2) * N_SMALL] for q in range(3)]
        o0 = 1 + 4 * N_SMALL
        outs = [refs[o0 + q * N_SMALL:o0 + (q + 1) * N_SMALL] for q in range(4)]
        loss_out = refs[o0 + 4 * N_SMALL]
        pack, land, wp, mp, vp, send_sems, recv_sems = refs[o0 + 4 * N_SMALL + 1:]

        def place(dst, srcs):
            dst[...] = jnp.zeros_like(dst)
            for p, sr, dr, dc, width in SMALL_LAYOUT:
                dst[dr:dr + 1, dc:dc + width] = srcs[p][sr:sr + 1, :]

        place(pack, g_in)
        pack[LOSS_ROW:LOSS_ROW + 1, 0:LANES] = loss_in[...]
        for dst, srcs in zip((wp, mp, vp), params):
            place(dst, srcs)

        x, y, c = _mesh_place()
        me = 4 * x + 2 * y + c
        land[me] = pack[...]
        sends = []
        for k in range(1, N_DEV):
            peer, _ = _peer(x, y, c, k)
            cp = _remote(pack, land.at[me], send_sems.at[k - 1], recv_sems.at[k - 1], peer)
            cp.start()
            sends.append(cp)
        for k in range(1, N_DEV):
            peer, pidx = _peer(x, y, c, k)
            _remote(pack, land.at[pidx], send_sems.at[k - 1], recv_sems.at[k - 1], peer).wait_recv()
        for cp in sends:
            cp.wait_send()

        g = land[0]
        for d in range(1, N_DEV):
            g = g + land[d]
        delta, nm, nv = _adam(wp[...], g, mp[...], vp[...])
        for val, out in zip((g, delta, nm, nv), outs):
            for p, sr, dr, dc, width in SMALL_LAYOUT:
                out[p][sr:sr + 1, :] = val[dr:dr + 1, dc:dc + width]
        loss_out[...] = g[LOSS_ROW:LOSS_ROW + 1, 0:LANES]

    vmem = pl.BlockSpec(memory_space=pltpu.VMEM)
    n_in = 1 + 4 * N_SMALL
    shapes = [jax.ShapeDtypeStruct(w.shape, F32) for w in ws]
    packed = pltpu.VMEM((SMALL_ROWS, D_MODEL), F32)
    outs = pl.pallas_call(
        body, name="small_step", in_specs=[vmem] * n_in, out_specs=[vmem] * (4 * N_SMALL + 1),
        out_shape=shapes * 4 + [jax.ShapeDtypeStruct((1, LANES), F32)],
        scratch_shapes=[packed, pltpu.VMEM((N_DEV, SMALL_ROWS, D_MODEL), F32), packed, packed, packed,
                        pltpu.SemaphoreType.DMA((N_DEV - 1,)), pltpu.SemaphoreType.DMA((N_DEV - 1,))],
    )(*grads, loss_part, *ws, *ms, *vs)
    return [outs[q * N_SMALL:(q + 1) * N_SMALL] for q in range(4)], outs[4 * N_SMALL]


def _adam(w, g, m, v):
    m = ADAM_B1 * m + (1.0 - ADAM_B1) * g
    v = ADAM_B2 * v + (1.0 - ADAM_B2) * (g * g)
    m_hat = m / (1.0 - ADAM_B1 ** ADAM_STEP)
    v_hat = v / (1.0 - ADAM_B2 ** ADAM_STEP)
    delta = -ADAM_LR * (m_hat / (jnp.sqrt(v_hat) + ADAM_EPS) + ADAM_WD * w)
    return delta, m, v


def _reduce_adamw(land, w, m, v, name, tr, after=()):
    _, rows, width = land.shape

    def body(land_ref, w_ref, m_ref, v_ref, *rest):
        g_ref, d_ref, nm_ref, nv_ref = rest[len(after):]
        g = land_ref[0].astype(F32)
        for d in range(1, N_DEV):
            g = g + land_ref[d].astype(F32)
        delta, nm, nv = _adam(w_ref[...], g, m_ref[...], v_ref[...])
        g_ref[...] = g
        d_ref[...] = delta
        nm_ref[...] = nm
        nv_ref[...] = nv

    row = pl.BlockSpec((tr, width), lambda i: (i, 0))
    out = jax.ShapeDtypeStruct((rows, width), F32)
    return pl.pallas_call(
        body, name=name, grid=(rows // tr,),
        in_specs=[pl.BlockSpec((N_DEV, tr, width), lambda i: (0, i, 0)), row, row, row]
        + [pl.BlockSpec(memory_space=pl.ANY)] * len(after),
        out_specs=[row, row, row, row], out_shape=[out, out, out, out],
        compiler_params=_params(1),
    )(land, w, m, v, *after)


def kernel(x, mix_norm_g, w_in, lower_bounds, hgrn_norm_g, sb_norm_g, w_out, ffn_norm_g, w_gate, w_up, w_down, final_norm_g, loss_target, m_mix_norm_g, m_w_in, m_lower_bounds, m_hgrn_norm_g, m_sb_norm_g, m_w_out, m_ffn_norm_g, m_w_gate, m_w_up, m_w_down, m_final_norm_g, v_mix_norm_g, v_w_in, v_lower_bounds, v_hgrn_norm_g, v_sb_norm_g, v_w_out, v_ffn_norm_g, v_w_gate, v_w_up, v_w_down, v_final_norm_g):
    n_seq, seq, d = x.shape
    t = n_seq * seq
    x2d = x.reshape(t, d)
    tgt = loss_target.reshape(t, d)
    final_g = final_norm_g.reshape(1, d)
    col_sharded = (True, False, True, True, False)

    def as_rows(ws):
        return [w[0].T if tr else w[0] for w, tr in zip(ws, col_sharded)]

    big_w = as_rows([w_in, w_out, w_gate, w_up, w_down])
    big_m = as_rows([m_w_in, m_w_out, m_w_gate, m_w_up, m_w_down])
    big_v = as_rows([v_w_in, v_w_out, v_w_gate, v_w_up, v_w_down])

    sh_in, sh_out, sh_gate, sh_up, sh_down = _cast_shards(big_w)
    wt_in, h1 = _gather_w_in(sh_in, x2d, mix_norm_g)
    wt_in = wt_in.reshape(IN_COLS, d)
    proj_h = _mm_nt(h1, wt_in, "proj_hgrn", rows=(0, 4 * GROUP), tm=1024, tk=4 * GROUP)
    proj_s = _mm_nt(h1, wt_in, "proj_sb", rows=(4 * GROUP, 3 * GROUP), out_dtype=MXU_DTYPE, tm=1024, tk=512)
    mixed, oa_pre, states = _hgrn_fwd(proj_h, lower_bounds, hgrn_norm_g, n_seq, seq)
    mixed, ob_pre, ctot, gathered = _sb_fwd(proj_s, sb_norm_g, mixed, [sh_out, sh_gate, sh_up, sh_down], n_seq, seq)
    wf_out = gathered[0].reshape(d, d)
    wt_gate = gathered[1].reshape(D_FF, d)
    wt_up = gathered[2].reshape(D_FF, d)
    wf_down = gathered[3].reshape(D_FF, d)
    x1, h2 = _mix_out_norm(mixed, wf_out, x2d, ffn_norm_g, "mix_out")
    gate, up, ff = _ffn_up(h2, wt_gate, wt_up, "ffn_up")
    dx2, dx2m, d_final_g, loss_part = _ffn_down_loss(ff, wf_down, x1, tgt, final_g, "ffn_down_loss")

    dgate, dup = _ffn_bwd_act(dx2m, wf_down, gate, up, "ffn_bwd_act")
    dw_down = _mm_tn(ff, dx2m, "dw_down", tk=1408, tn=1024).reshape(N_DEV, D_FF // N_DEV, d)
    dw_gate = _mm_tn(dgate, h2, "dw_gate", tk=1408, tn=1024).reshape(N_DEV, D_FF // N_DEV, d)
    dw_up = _mm_tn(dup, h2, "dw_up", tk=1408, tn=1024).reshape(N_DEV, D_FF // N_DEV, d)
    dx1, d_ffn_g, dmix = _dh_norm_bwd([(dgate, wt_gate, None), (dup, wt_up, None)], x1, ffn_norm_g, dx2, "dh_ffn",
                                      tm=256, next_w=wf_out)
    dw_out = _mm_tn(mixed, dx1, "dw_out", tk=1024, tn=1024).reshape(N_DEV, d // N_DEV, d)
    dsq, dsk, dsv, d_sb_g, lands = _sb_bwd(proj_s, sb_norm_g, dmix, ob_pre, ctot, [dw_out, dw_gate, dw_up, dw_down],
                                            n_seq, seq)
    dhq, dhf, dhi, dhg, d_lb, d_hgrn_g = _hgrn_bwd(proj_h, lower_bounds, hgrn_norm_g, dmix, oa_pre, states, n_seq,
                                                   seq)
    dproj = [dhq, dhf, dhi, dhg, dsq, dsk, dsv]
    dw_in = _dw_rows(dproj, h1, "dw_in").reshape(N_DEV, IN_COLS // N_DEV, d)
    send_sem, recv_sem, local_sem, dw_in, land_in, token = _exchange_start(dw_in, "dw_in_send")
    dx, d_mix_g = _dh_norm_bwd([(piece, wt_in, k) for k, piece in enumerate(dproj)], x2d, mix_norm_g, dx1, "dh_mix",
                               after=(token,))

    tiles = {"in": 224, "out": 128, "gate": 176, "up": 176, "down": 176}
    keys = list(tiles)
    rest = [_reduce_adamw(land, w, m, v, "adamw_" + key, tr=tiles[key], after=(token,))
            for key, land, w, m, v in zip(keys[1:], lands, big_w[1:], big_m[1:], big_v[1:])]
    small, loss_row = _small_step(
        [d_mix_g, d_lb, d_hgrn_g, d_sb_g, d_ffn_g, d_final_g], loss_part,
        [mix_norm_g, lower_bounds, hgrn_norm_g, sb_norm_g, ffn_norm_g, final_g],
        [m_mix_norm_g, m_lower_bounds, m_hgrn_norm_g, m_sb_norm_g, m_ffn_norm_g, m_final_norm_g.reshape(1, d)],
        [v_mix_norm_g, v_lower_bounds, v_hgrn_norm_g, v_sb_norm_g, v_ffn_norm_g, v_final_norm_g.reshape(1, d)])
    land_in = _exchange_wait(send_sem, recv_sem, local_sem, dw_in, land_in,
                             [dx, loss_row] + [res[0] for res in rest], "dw_in_await")
    big = [_reduce_adamw(land_in, big_w[0], big_m[0], big_v[0], "adamw_in", tr=tiles["in"])] + rest
    big = [[r.T if tr else r for r in res] for res, tr in zip(big, col_sharded)]

    outs = [loss_row[0, 0], dx.reshape(n_seq, seq, d)]
    for q in range(4):
        b_in, b_out, b_gate, b_up, b_down = [res[q][None] for res in big]
        s_mix, s_lb, s_hgrn, s_sb, s_ffn, s_final = small[q]
        outs += [s_mix, b_in, s_lb, s_hgrn, s_sb, b_out, s_ffn, b_gate, b_up, b_down, s_final.reshape(d)]
    return tuple(outs)
```

```python
import functools
import math

import jax
import jax.numpy as jnp
from jax import lax
from jax.experimental import pallas as pl
from jax.experimental.pallas import tpu as pltpu

F32 = jnp.float32
BF16 = jnp.bfloat16
MXU_DTYPE = BF16

EPS = 1e-6
D_MODEL = 1024
N_HEADS = 8
D_HEAD = 64
GROUP = N_HEADS * D_HEAD
IN_COLS = 7 * GROUP
D_FF = 2816
CHUNK = 64
LANES = 128
N_PAIRS = GROUP // LANES
SUPER = 256
SB_BLOCK = 256
N_DEV = 8

ADAM_LR = 0.001
ADAM_B1 = 0.9
ADAM_B2 = 0.999
ADAM_EPS = 1e-08
ADAM_WD = 0.01
ADAM_STEP = 10

SMALL_ROWS = 8
FF_TILE = D_FF // 2

VMEM_LIMIT = 48 * 1024 * 1024


def _params(n_axes, vmem=VMEM_LIMIT):
    return pltpu.CompilerParams(dimension_semantics=("arbitrary",) * n_axes, vmem_limit_bytes=vmem)


def _dot(a, b):
    return jnp.dot(a.astype(MXU_DTYPE), b.astype(MXU_DTYPE), preferred_element_type=F32)


def _dot_nt(a, b):
    return lax.dot_general(a.astype(MXU_DTYPE), b.astype(MXU_DTYPE), (((1,), (1,)), ((), ())),
                           preferred_element_type=F32)


def _dot_tn(a, b):
    return lax.dot_general(a.astype(MXU_DTYPE), b.astype(MXU_DTYPE), (((0,), (0,)), ((), ())),
                           preferred_element_type=F32)


def _split(x, parts):
    out, r = [], x
    for _ in range(parts):
        h = r.astype(BF16)
        out.append(h)
        r = r - h.astype(F32)
    return out


def _rsum_left(u, x, parts):
    acc = None
    for h in _split(x, parts):
        d = jnp.dot(u, h, preferred_element_type=F32)
        acc = d if acc is None else acc + d
    return acc


def _ones_where(mask):
    return jnp.where(mask, 1.0, 0.0).astype(BF16)


def _sigmoid(x):
    return 1.0 / (1.0 + jnp.exp(-x))


def _softplus(x):
    return jnp.maximum(x, 0.0) + jnp.log(1.0 + jnp.exp(-jnp.abs(x)))


def _head_masks():
    lane = lax.broadcasted_iota(jnp.int32, (1, LANES), 1)
    return [jnp.where(lane < D_HEAD, 1.0, 0.0), jnp.where(lane >= D_HEAD, 1.0, 0.0)]


def _head_rstd(o, masks):
    sq = o * o
    r = None
    for m in masks:
        ms = jnp.sum(sq * m, axis=1, keepdims=True) * (1.0 / D_HEAD)
        t = lax.rsqrt(ms + EPS) * m
        r = t if r is None else r + t
    return r


def _head_mean(t, masks):
    out = None
    for m in masks:
        v = jnp.sum(t * m, axis=1, keepdims=True) * (1.0 / D_HEAD) * m
        out = v if out is None else out + v
    return out


def _mix_out_norm(a, w, res, g, name, tm=512):
    m, k = a.shape
    d = w.shape[1]
    tm = min(tm, m)

    def body(a_ref, w_ref, res_ref, g_ref, x_ref, h_ref):
        xv = res_ref[...] + _dot(a_ref[...], w_ref[...])
        x_ref[...] = xv
        r = lax.rsqrt(jnp.mean(xv * xv, axis=-1, keepdims=True) + EPS)
        h_ref[...] = (xv * r * g_ref[...]).astype(h_ref.dtype)

    row = pl.BlockSpec((tm, d), lambda i: (i, 0))
    return pl.pallas_call(
        body, name=name, grid=(m // tm,),
        in_specs=[pl.BlockSpec((tm, k), lambda i: (i, 0)), pl.BlockSpec((k, d), lambda i: (0, 0)), row,
                  pl.BlockSpec((1, d), lambda i: (0, 0))],
        out_specs=[row, row],
        out_shape=[jax.ShapeDtypeStruct((m, d), F32), jax.ShapeDtypeStruct((m, d), MXU_DTYPE)],
        compiler_params=_params(1),
    )(a, w, res, g)


def _ffn_down_loss(a, w, res, target, g, name, tm=512):
    m, k = a.shape
    d = w.shape[1]
    tm = min(tm, m)

    def body(a_ref, w_ref, res_ref, t_ref, g_ref, dx_ref, dxm_ref, dg_ref, loss_ref):
        xv = res_ref[...] + _dot(a_ref[...], w_ref[...])
        gv = g_ref[...]
        r = lax.rsqrt(jnp.mean(xv * xv, axis=-1, keepdims=True) + EPS)
        xh = xv * r
        e = xh * gv - t_ref[...]
        dy = e * (1.0 / d)
        dxh = dy * gv
        dxv = r * (dxh - xh * jnp.mean(dxh * xh, axis=-1, keepdims=True))
        dx_ref[...] = dxv
        dxm_ref[...] = dxv.astype(dxm_ref.dtype)

        @pl.when(pl.program_id(0) == 0)
        def _():
            dg_ref[...] = jnp.zeros_like(dg_ref)
            loss_ref[...] = jnp.zeros_like(loss_ref)

        dg_ref[...] += jnp.sum(dy * xh, axis=0, keepdims=True)
        part = 0.5 * jnp.sum(jnp.mean(e * e, axis=-1, keepdims=True), axis=0, keepdims=True)
        loss_ref[...] += jnp.broadcast_to(part, loss_ref.shape)

    row = pl.BlockSpec((tm, d), lambda i: (i, 0))
    vec = pl.BlockSpec((1, d), lambda i: (0, 0))
    return pl.pallas_call(
        body, name=name, grid=(m // tm,),
        in_specs=[pl.BlockSpec((tm, k), lambda i: (i, 0)), pl.BlockSpec((k, d), lambda i: (0, 0)), row, row, vec],
        out_specs=[row, row, vec, pl.BlockSpec((1, LANES), lambda i: (0, 0))],
        out_shape=[jax.ShapeDtypeStruct((m, d), F32), jax.ShapeDtypeStruct((m, d), MXU_DTYPE),
                   jax.ShapeDtypeStruct((1, d), F32), jax.ShapeDtypeStruct((1, LANES), F32)],
        compiler_params=_params(1),
    )(a, w, res, target, g)


def _mm_nt(a, w, name, rows=None, out_dtype=F32, tm=512, tk=512, gather=None):
    m, n = a.shape
    row0, k = rows or (0, w.shape[0])
    tm, tk = min(tm, m), min(tk, k)
    first = row0 // tk
    n_steps = (k // tk) * (m // tm)

    def body(a_ref, w_ref, *rest):
        if gather is None:
            (o_ref,) = rest
        else:
            shard_ref, o_ref, gathered_ref, send_sems, recv_sems, local_sems = rest
            step = pl.program_id(0) * (m // tm) + pl.program_id(1)
            plan = _GatherPlan([shard_ref], [gathered_ref], send_sems, recv_sems, local_sems, senders=gather[1],
                               pad_rows=gather[2])
            pl.when(step == 0)(plan.start)
            pl.when(step == n_steps // 2)(plan.forward)
        o_ref[...] = _dot_nt(a_ref[...], w_ref[...]).astype(o_ref.dtype)
        if gather is not None:
            pl.when(step == n_steps - 1)(plan.finish)

    in_specs = [pl.BlockSpec((tm, n), lambda j, i: (i, 0)), pl.BlockSpec((tk, n), lambda j, i: (first + j, 0))]
    out_specs = [pl.BlockSpec((tm, tk), lambda j, i: (i, j))]
    out_shape = [jax.ShapeDtypeStruct((m, k), out_dtype)]
    args, scratch = [a, w], []
    if gather is not None:
        shard, senders, pad_rows = gather
        hbm = pl.BlockSpec(memory_space=pltpu.HBM)
        in_specs.append(hbm)
        out_specs.append(hbm)
        out_shape.append(jax.ShapeDtypeStruct((pad_rows + senders[1] * shard.shape[0], shard.shape[1]), shard.dtype))
        args.append(shard)
        scratch = [pltpu.SemaphoreType.DMA((_GatherPlan.COPIES,)), pltpu.SemaphoreType.DMA((_GatherPlan.COPIES,)),
                   pltpu.SemaphoreType.DMA((1,))]
    outs = pl.pallas_call(
        body, name=name, grid=(k // tk, m // tm), in_specs=in_specs, out_specs=out_specs, out_shape=out_shape,
        scratch_shapes=scratch, compiler_params=_params(2),
    )(*args)
    return outs[0] if gather is None else outs


def _mm_tn(a, b, name, tk, tn, tt=1024, out_dtype=BF16):
    t, k = a.shape
    n = b.shape[1]
    tt = min(tt, t)
    steps = t // tt

    def body(a_ref, b_ref, o_ref, acc):
        s = pl.program_id(2)

        @pl.when(s == 0)
        def _():
            acc[...] = jnp.zeros_like(acc)

        acc[...] += _dot_tn(a_ref[...], b_ref[...])

        @pl.when(s == steps - 1)
        def _():
            o_ref[...] = acc[...].astype(o_ref.dtype)

    return pl.pallas_call(
        body, name=name, grid=(k // tk, n // tn, steps),
        in_specs=[pl.BlockSpec((tt, tk), lambda i, j, s: (s, i)), pl.BlockSpec((tt, tn), lambda i, j, s: (s, j))],
        out_specs=pl.BlockSpec((tk, tn), lambda i, j, s: (i, j)),
        out_shape=jax.ShapeDtypeStruct((k, n), out_dtype),
        scratch_shapes=[pltpu.VMEM((tk, tn), F32)],
        compiler_params=_params(3),
    )(a, b)


def _ffn_up(h, wg_t, wu_t, name, tm=1024, tn=FF_TILE):
    m, k = h.shape
    n = wg_t.shape[0]
    tm = min(tm, m)

    def body(h_ref, wg_ref, wu_ref, gate_ref, up_ref, ff_ref):
        hv = h_ref[...]
        gate = _dot_nt(hv, wg_ref[...])
        up = _dot_nt(hv, wu_ref[...])
        gate_ref[...] = gate.astype(gate_ref.dtype)
        up_ref[...] = up.astype(up_ref.dtype)
        ff_ref[...] = (gate * _sigmoid(gate) * up).astype(ff_ref.dtype)

    wspec = pl.BlockSpec((tn, k), lambda j, i: (j, 0))
    ospec = pl.BlockSpec((tm, tn), lambda j, i: (i, j))
    return pl.pallas_call(
        body, name=name, grid=(n // tn, m // tm),
        in_specs=[pl.BlockSpec((tm, k), lambda j, i: (i, 0)), wspec, wspec],
        out_specs=[ospec, ospec, ospec],
        out_shape=[jax.ShapeDtypeStruct((m, n), MXU_DTYPE)] * 3,
        compiler_params=_params(2),
    )(h, wg_t, wu_t)


def _ffn_bwd_act(dx, wd, gate, up, name, tm=1024, tn=FF_TILE):
    m, k = dx.shape
    n = wd.shape[0]
    tm = min(tm, m)

    def body(dx_ref, wd_ref, gate_ref, up_ref, dgate_ref, dup_ref):
        dff = _dot_nt(dx_ref[...], wd_ref[...])
        gate = gate_ref[...].astype(F32)
        sg = _sigmoid(gate)
        dgate_ref[...] = (dff * up_ref[...].astype(F32) * sg * (1.0 + gate * (1.0 - sg))).astype(dgate_ref.dtype)
        dup_ref[...] = (dff * gate * sg).astype(dup_ref.dtype)

    ospec = pl.BlockSpec((tm, tn), lambda j, i: (i, j))
    return pl.pallas_call(
        body, name=name, grid=(n // tn, m // tm),
        in_specs=[pl.BlockSpec((tm, k), lambda j, i: (i, 0)), pl.BlockSpec((tn, k), lambda j, i: (j, 0)),
                  ospec, ospec],
        out_specs=[ospec, ospec],
        out_shape=[jax.ShapeDtypeStruct((m, n), MXU_DTYPE), jax.ShapeDtypeStruct((m, n), MXU_DTYPE)],
        compiler_params=_params(2),
    )(dx, wd, gate, up)


def _chunk_masks():
    r = lax.broadcasted_iota(jnp.int32, (SUPER, SUPER), 0)
    c = lax.broadcasted_iota(jnp.int32, (SUPER, SUPER), 1)
    same = jnp.right_shift(r, 6) == jnp.right_shift(c, 6)
    lower = jnp.logical_and(same, c <= r)
    upper = jnp.logical_and(same, c >= r)
    return same, lower, upper


def _head_block_mask():
    r = lax.broadcasted_iota(jnp.int32, (LANES, LANES), 0)
    c = lax.broadcasted_iota(jnp.int32, (LANES, LANES), 1)
    return jnp.where(jnp.right_shift(r, 6) == jnp.right_shift(c, 6), 1.0, 0.0)


def _lower_bound(lb_raw):
    return 1.0 / (1.0 + jnp.exp(lb_raw[1:2, :] - lb_raw[0:1, :]))


HGRN_UNROLL = 4
PER_SUPER = SUPER // CHUNK
CHUNK_ROWS = [slice(c * CHUNK, (c + 1) * CHUNK) for c in range(PER_SUPER)]


def _over_chunks(rows):
    return jnp.concatenate([jnp.broadcast_to(r, (CHUNK, LANES)) for r in rows], axis=0)


def _hgrn_gates(q, hf, lb, lower_b):
    sig = _sigmoid(hf)
    f = lb + (1.0 - lb) * sig
    k = 1.0 - f
    lf = jnp.log(f)
    b = _rsum_left(lower_b, lf, 2)
    ends = [b[cr.stop - 1:cr.stop, :] for cr in CHUNK_ROWS]
    eb = jnp.exp(b)
    enb = jnp.exp(-b)
    edb = jnp.exp(_over_chunks(ends) - b)
    decs = [jnp.exp(e) for e in ends]
    return sig, f, k, decs, eb, enb, edb, q * eb, k * enb, k * edb


def _hgrn_fwd(proj, lower_bounds, norm_g, n_seq, seq):
    t = n_seq * seq
    n_super = seq // SUPER
    n_chunks = seq // CHUNK

    def body(q_ref, f_ref, i_ref, g_ref, lb_ref, ng_ref, out_ref, opre_ref, st_ref):
        masks = _head_masks()
        _, lower, _ = _chunk_masks()
        lower_b = _ones_where(lower)
        bd = _head_block_mask()
        lb = _lower_bound(lb_ref[...])
        ng = ng_ref[...]

        def step(it, st):
            blocks = [HGRN_UNROLL * it + u for u in range(HGRN_UNROLL)]
            rows = [pl.ds(pl.multiple_of(sb * SUPER, SUPER), SUPER) for sb in blocks]
            vs = [i_ref[rw, :] for rw in rows]
            gates = [_hgrn_gates(q_ref[rw, :], f_ref[rw, :], lb, lower_b) for rw in rows]
            decs, qes, kes, kds = ([g[k] for g in gates] for k in (3, 7, 8, 9))
            scores = [[_dot_nt(qe * m, ke) for m in masks] for qe, ke in zip(qes, kes)]
            updates = [[_dot_tn(v[cr], kd[cr]) for cr in CHUNK_ROWS] for v, kd in zip(vs, kds)]
            states = [st]
            for dec_b, upd_b in zip(decs, updates):
                for dec, upd in zip(dec_b, upd_b):
                    states.append(states[-1] * dec + bd * upd)
            for u, sb in enumerate(blocks):
                for c in range(PER_SUPER):
                    st_ref[0, 0, sb * PER_SUPER + c] = states[u * PER_SUPER + c]
            intra = [[_dot(jnp.where(lower, p, 0.0), v) for p in sc] for sc, v in zip(scores, vs)]
            inter = [[_dot_nt(qe[cr], states[u * PER_SUPER + c]) for c, cr in enumerate(CHUNK_ROWS)]
                     for u, qe in enumerate(qes)]
            for rw, intra_b, inter_b in zip(rows, intra, inter):
                o = intra_b[0] * masks[0] + intra_b[1] * masks[1] + jnp.concatenate(inter_b, axis=0)
                opre_ref[rw, :] = o
                hg = g_ref[rw, :]
                on = o * _head_rstd(o, masks) * ng
                out_ref[rw, :] = (on * hg * _sigmoid(hg)).astype(out_ref.dtype)
            return states[-1]

        lax.fori_loop(0, n_super // HGRN_UNROLL, step, jnp.zeros((LANES, LANES), F32))

    def col(k):
        return pl.BlockSpec((seq, LANES), lambda p, b: (b, k * N_PAIRS + p))

    vec = lambda rows: pl.BlockSpec((rows, LANES), lambda p, b: (0, p))
    ospec = pl.BlockSpec((seq, LANES), lambda p, b: (b, p))
    return pl.pallas_call(
        body, name="hgrn_fwd", grid=(N_PAIRS, n_seq),
        in_specs=[col(0), col(1), col(2), col(3), vec(2), vec(1)],
        out_specs=[ospec, ospec,
                   pl.BlockSpec((1, 1, n_chunks, LANES, LANES), lambda p, b: (b, p, 0, 0, 0))],
        out_shape=[jax.ShapeDtypeStruct((t, 2 * GROUP), MXU_DTYPE), jax.ShapeDtypeStruct((t, GROUP), F32),
                   jax.ShapeDtypeStruct((n_seq, N_PAIRS, n_chunks, LANES, LANES), F32)],
        compiler_params=_params(2),
    )(proj, proj, proj, proj, lower_bounds, norm_g)


def _hgrn_bwd(proj, lower_bounds, norm_g, dmix, opre, states, n_seq, seq):
    t = n_seq * seq
    n_super = seq // SUPER
    n_chunks = seq // CHUNK
    per = SUPER // CHUNK

    def body(q_ref, f_ref, i_ref, g_ref, lb_ref, ng_ref, dm_ref, opre_ref, st_ref,
             dq_ref, df_ref, di_ref, dg_ref, dlb_ref, dng_ref):
        masks = _head_masks()
        _, lower, upper = _chunk_masks()
        lower_b, upper_b = _ones_where(lower), _ones_where(upper)
        bd = _head_block_mask()
        lb_raw = lb_ref[...]
        lb = _lower_bound(lb_raw)
        ng = ng_ref[...]

        @pl.when(pl.program_id(1) == 0)
        def _():
            dlb_ref[...] = jnp.zeros_like(dlb_ref)
            dng_ref[...] = jnp.zeros_like(dng_ref)

        def first_half(sb):
            rows = pl.ds(pl.multiple_of(sb * SUPER, SUPER), SUPER)
            q, hf, v, hg = q_ref[rows, :], f_ref[rows, :], i_ref[rows, :], g_ref[rows, :]
            sig, f, k, decs, eb, enb, edb, qe, ke, kd = _hgrn_gates(q, hf, lb, lower_b)
            o = opre_ref[rows, :]
            r = _head_rstd(o, masks)
            oh = o * r
            dm = dm_ref[rows, :]
            sg = _sigmoid(hg)
            dg_ref[rows, :] = (dm * oh * ng * sg * (1.0 + hg * (1.0 - sg))).astype(dg_ref.dtype)
            don = dm * hg * sg
            dng_ref[...] += jnp.sum(don * oh, axis=0, keepdims=True)
            doh = don * ng
            do = r * (doh - oh * _head_mean(doh * oh, masks))
            doms = [do * m for m in masks]
            qems = [qe * m for m in masks]
            scores = [_dot_nt(qem, ke) for qem in qems]
            dscores = [_dot_nt(dom, v) for dom in doms]
            prevs = [st_ref[0, 0, sb * per + c] for c in range(per)]
            dst_in = [_dot_tn(do[cr], qe[cr]) for cr in CHUNK_ROWS]
            dqe_i = [_dot(do[cr], prev) for cr, prev in zip(CHUNK_ROWS, prevs)]
            return dict(rows=rows, v=v, sig=sig, f=f, decs=decs, eb=eb, enb=enb, edb=edb, qe=qe, ke=ke, kd=kd,
                        doms=doms, qems=qems, scores=scores, dscores=dscores, prevs=prevs, dst_in=dst_in, dqe_i=dqe_i)

        def second_half(blk, dsts):
            v, qe, ke, kd = blk["v"], blk["qe"], blk["ke"], blk["kd"]
            ps = [jnp.where(lower, p, 0.0) for p in blk["scores"]]
            dps = [jnp.where(lower, dp, 0.0) for dp in blk["dscores"]]
            dqe_h = [_dot(dp, ke) for dp in dps]
            dke_h = [_dot_tn(dp, qem) for dp, qem in zip(dps, blk["qems"])]
            dv_h = [_dot_tn(p, dom) for p, dom in zip(ps, blk["doms"])]
            dus = [bd * d for d in dsts]
            dv_i = [_dot_nt(kd[cr], du) for cr, du in zip(CHUNK_ROWS, dus)]
            dkd_i = [_dot(v[cr], du) for cr, du in zip(CHUNK_ROWS, dus)]

            def finish():
                dqe = dqe_h[0] * masks[0] + dqe_h[1] * masks[1] + jnp.concatenate(blk["dqe_i"], axis=0)
                dke = dke_h[0] + dke_h[1]
                dv = dv_h[0] + dv_h[1] + jnp.concatenate(dv_i, axis=0)
                dkd = jnp.concatenate(dkd_i, axis=0)
                dk = dke * blk["enb"] + dkd * blk["edb"]
                db = dqe * qe - dke * ke - dkd * kd
                dkd_kd = dkd * kd
                dends = [jnp.sum(dkd_kd[cr], axis=0, keepdims=True)
                         + jnp.sum(dsts[c] * blk["prevs"][c], axis=0, keepdims=True) * blk["decs"][c]
                         for c, cr in enumerate(CHUNK_ROWS)]
                dlf = _rsum_left(upper_b, db, 2) + _over_chunks(dends)
                sig = blk["sig"]
                dfv = dlf / blk["f"] - dk
                rows = blk["rows"]
                dq_ref[rows, :] = (dqe * blk["eb"]).astype(dq_ref.dtype)
                di_ref[rows, :] = dv.astype(di_ref.dtype)
                df_ref[rows, :] = (dfv * (1.0 - lb) * sig * (1.0 - sig)).astype(df_ref.dtype)
                dlb = jnp.sum(dfv * (1.0 - sig), axis=0, keepdims=True)
                da0 = dlb * lb * (1.0 - lb)
                dlb_ref[0:1, :] += da0
                dlb_ref[1:2, :] -= da0

            return finish

        def step(it, dst):
            blocks = [first_half(n_super - 1 - HGRN_UNROLL * it - u) for u in range(HGRN_UNROLL)]
            all_dsts = []
            for blk in blocks:
                dsts = [None] * per
                for c in reversed(range(per)):
                    dsts[c] = dst
                    dst = bd * (dst * blk["decs"][c] + blk["dst_in"][c])
                all_dsts.append(dsts)
            for finish in [second_half(blk, dsts) for blk, dsts in zip(blocks, all_dsts)]:
                finish()
            return dst

        lax.fori_loop(0, n_super // HGRN_UNROLL, step, jnp.zeros((LANES, LANES), F32))

    def col(k):
        return pl.BlockSpec((seq, LANES), lambda p, b: (b, k * N_PAIRS + p))

    vec = lambda rows: pl.BlockSpec((rows, LANES), lambda p, b: (0, p))
    ospec = pl.BlockSpec((seq, LANES), lambda p, b: (b, p))
    piece = jax.ShapeDtypeStruct((t, GROUP), MXU_DTYPE)
    return pl.pallas_call(
        body, name="hgrn_bwd", grid=(N_PAIRS, n_seq),
        in_specs=[col(0), col(1), col(2), col(3), vec(2), vec(1), ospec, ospec,
                  pl.BlockSpec((1, 1, n_chunks, LANES, LANES), lambda p, b: (b, p, 0, 0, 0))],
        out_specs=[ospec, ospec, ospec, ospec, vec(2), vec(1)],
        out_shape=[piece, piece, piece, piece,
                   jax.ShapeDtypeStruct((2, GROUP), F32), jax.ShapeDtypeStruct((1, GROUP), F32)],
        compiler_params=_params(2),
    )(proj, proj, proj, proj, lower_bounds, norm_g, dmix, opre, states)


SB_SCALE = 1.0 / math.sqrt(D_HEAD)


SB_STEP = 2 * SB_BLOCK
QUERY_BLOCKS = (slice(0, SB_BLOCK), slice(SB_BLOCK, SB_STEP))


def _triangle(keep):
    row = lax.broadcasted_iota(jnp.int32, (SB_BLOCK, SB_BLOCK), 0)
    col = lax.broadcasted_iota(jnp.int32, (SB_BLOCK, SB_BLOCK), 1)
    return _ones_where(keep(row, col))


SB_HALF = SB_BLOCK // 2


def _keep(on_diagonal, x):
    if not on_diagonal:
        return x
    row = lax.broadcasted_iota(jnp.int32, (SB_HALF, SB_HALF), 0)
    col = lax.broadcasted_iota(jnp.int32, (SB_HALF, SB_HALF), 1)
    return jnp.where(col < row, x, 0.0)


def _lower_quadrants(fn, n_out, *tiles):
    def quadrant(r, c):
        return [t[r * SB_HALF:(r + 1) * SB_HALF, c * SB_HALF:(c + 1) * SB_HALF] for t in tiles]

    top, left, bottom = fn(True, *quadrant(0, 0)), fn(False, *quadrant(1, 0)), fn(True, *quadrant(1, 1))
    zero = jnp.zeros((SB_HALF, SB_HALF), F32)
    return [jnp.concatenate([jnp.concatenate([top[o], zero], axis=1), jnp.concatenate([left[o], bottom[o]], axis=1)],
                            axis=0) for o in range(n_out)]


def _sb_fwd(proj, norm_g, mixed, shards, n_seq, seq):
    t = n_seq * seq
    nq = seq // SB_STEP
    q0, k0, v0 = 0, N_PAIRS, 2 * N_PAIRS
    n_w = len(shards)
    n_steps = N_PAIRS * n_seq * nq
    tri = _triangle(lambda row, col: row >= col)

    def body(q_ref, k_ref, v_ref, ng_ref, tri_ref, mixed_in, *rest):
        del mixed_in
        shard_refs = rest[:n_w]
        out_ref, opre_ref, ctot_ref = rest[n_w:n_w + 3]
        gathered = rest[n_w + 3:2 * n_w + 3]
        send_sems, recv_sems, local_sems = rest[2 * n_w + 3:]
        i = pl.program_id(2)
        step = (pl.program_id(0) * n_seq + pl.program_id(1)) * nq + i
        plan = _GatherPlan(shard_refs, gathered, send_sems, recv_sems, local_sems)

        @pl.when(step == 0)
        def _():
            plan.start()

        @pl.when(step == (3 * n_steps) // 4)
        def _():
            plan.forward()

        masks = _head_masks()
        suffix = tri_ref[...]
        qhs = [[(q_ref[blk, :] * SB_SCALE * m).astype(MXU_DTYPE) for m in masks] for blk in QUERY_BLOCKS]

        def tiles(js, work, carry):
            rows = [pl.ds(pl.multiple_of(j * SB_BLOCK, SB_BLOCK), SB_BLOCK) for j in js]
            ks = [k_ref[rw, :].astype(MXU_DTYPE) for rw in rows]
            vs = [v_ref[rw, :].astype(MXU_DTYPE) for rw in rows]
            zs = [[_dot_nt(qh, ks[ts]) for qh in qhs[qb]] for qb, ts, _ in work]
            ccs = [[_lower_quadrants(lambda on, zq: (_keep(on, _softplus(zq)),), 1, z)[0] if diag else _softplus(z)
                    for z in zw] for zw, (_, _, diag) in zip(zs, work)]
            sums = [[jnp.dot(cc.astype(BF16), suffix, preferred_element_type=F32) for cc in cw] for cw in ccs]
            out = [list(per_block) for per_block in carry]
            for h in range(len(masks)):
                for w, (qb, ts, diag) in enumerate(work):
                    run, acc = out[qb][h]
                    logit = zs[w][h] - (sums[w][h] + run)
                    if diag:
                        a = _lower_quadrants(lambda on, lq: (_keep(on, jnp.exp(lq)),), 1, logit)[0]
                    else:
                        a = jnp.exp(logit)
                    out[qb][h] = (run + sums[w][h][:, 0:1], acc + _dot(a, vs[ts]))
            return tuple(tuple(per_block) for per_block in out)

        start = ((jnp.zeros((SB_BLOCK, 1), F32), jnp.zeros((SB_BLOCK, LANES), F32)),) * 2
        carry = tiles([2 * i, 2 * i + 1], [(0, 0, True), (1, 1, True), (1, 0, False)], (start, start))
        both = [(0, 0, False), (0, 1, False), (1, 0, False), (1, 1, False)]
        carry = lax.fori_loop(0, i, lambda s, cy: tiles([2 * (i - s) - 1, 2 * (i - s) - 2], both, cy), carry)
        opre = jnp.concatenate([cb[0][1] * masks[0] + cb[1][1] * masks[1] for cb in carry], axis=0)
        ctot = jnp.concatenate([cb[0][0] * masks[0] + cb[1][0] * masks[1] for cb in carry], axis=0)
        opre_ref[...] = opre
        ctot_ref[...] = ctot
        out_ref[...] = (opre * _head_rstd(opre, masks) * ng_ref[...]).astype(out_ref.dtype)

        @pl.when(step == n_steps - 1)
        def _():
            plan.finish()

    qspec = pl.BlockSpec((SB_STEP, LANES), lambda p, b, i: (b * nq + i, q0 + p))
    ospec = pl.BlockSpec((SB_STEP, LANES), lambda p, b, i: (b * nq + i, p))
    hbm = pl.BlockSpec(memory_space=pltpu.HBM)
    outs = pl.pallas_call(
        body, name="sb_fwd", grid=(N_PAIRS, n_seq, nq),
        in_specs=[qspec,
                  pl.BlockSpec((seq, LANES), lambda p, b, i: (b, k0 + p)),
                  pl.BlockSpec((seq, LANES), lambda p, b, i: (b, v0 + p)),
                  pl.BlockSpec((1, LANES), lambda p, b, i: (0, p)),
                  pl.BlockSpec(tri.shape, lambda p, b, i: (0, 0)), hbm] + [hbm] * n_w,
        out_specs=[pl.BlockSpec((SB_STEP, LANES), lambda p, b, i: (b * nq + i, N_PAIRS + p)), ospec, ospec]
        + [hbm] * n_w,
        out_shape=[jax.ShapeDtypeStruct(mixed.shape, mixed.dtype), jax.ShapeDtypeStruct((t, GROUP), F32),
                   jax.ShapeDtypeStruct((t, GROUP), F32)]
        + [jax.ShapeDtypeStruct((N_DEV,) + s.shape, s.dtype) for s in shards],
        scratch_shapes=[pltpu.SemaphoreType.DMA((n_w * _GatherPlan.COPIES,)),
                        pltpu.SemaphoreType.DMA((n_w * _GatherPlan.COPIES,)), pltpu.SemaphoreType.DMA((n_w,))],
        input_output_aliases={5: 0},
        compiler_params=_params(3),
    )(proj, proj, proj, norm_g, tri, mixed, *shards)
    return outs[0], outs[1], outs[2], list(outs[3:])


def _sb_bwd(proj, norm_g, dmix, opre, ctot, grads, n_seq, seq):
    t = n_seq * seq
    nq = seq // SB_STEP
    q0, k0, v0 = 0, N_PAIRS, 2 * N_PAIRS
    n_w = len(grads)
    n_steps = N_PAIRS * n_seq * nq
    tri = _triangle(lambda row, col: row <= col)

    def body(q_ref, k_ref, v_ref, ng_ref, tri_ref, dm_ref, opre_ref, ctot_ref, *rest):
        grad_refs = rest[:n_w]
        dq_ref, dk_ref, dv_ref, dng_ref = rest[n_w:n_w + 4]
        lands = rest[n_w + 4:2 * n_w + 4]
        dk_acc, dv_acc, send_sems, recv_sems, local_sems = rest[2 * n_w + 4:]
        p_id, b_id, i = pl.program_id(0), pl.program_id(1), pl.program_id(2)
        step = (p_id * n_seq + b_id) * nq + i
        plan = _ScatterPlan(grad_refs, lands, send_sems, recv_sems, local_sems)

        @pl.when(step == 0)
        def _():
            plan.start()

        masks = _head_masks()
        upto = tri_ref[...]

        def prefix(x):
            return jnp.dot(x.astype(BF16), upto, preferred_element_type=F32)

        @pl.when(i == 0)
        def _():
            dk_acc[...] = jnp.zeros_like(dk_acc)
            dv_acc[...] = jnp.zeros_like(dv_acc)

        @pl.when(jnp.logical_and(b_id == 0, i == 0))
        def _():
            dng_ref[...] = jnp.zeros_like(dng_ref)

        o = opre_ref[...]
        rs = _head_rstd(o, masks)
        oh = o * rs
        dm = dm_ref[...]
        dng_ref[...] += jnp.sum(dm * oh, axis=0, keepdims=True)
        doh = dm * ng_ref[...]
        do = rs * (doh - oh * _head_mean(doh * oh, masks))

        heads = range(len(masks))
        qs = [q_ref[blk, :] * SB_SCALE for blk in QUERY_BLOCKS]
        dos = [do[blk] for blk in QUERY_BLOCKS]
        qhs = [[(q * m).astype(MXU_DTYPE) for m in masks] for q in qs]
        doms = [[(d * m).astype(MXU_DTYPE) for m in masks] for d in dos]
        head_rows = [jnp.where(jnp.right_shift(lax.broadcasted_iota(jnp.int32, (LANES, 1), 0), 6) == h, 1.0, 0.0)
                     for h in heads]
        qhts = [[(qt * hr).astype(MXU_DTYPE) for hr in head_rows] for qt in [q.astype(F32).T for q in qs]]
        domts = [[(dt * hr).astype(MXU_DTYPE) for hr in head_rows] for dt in [d.T for d in dos]]
        totals = [[ctot_ref[blk, h * D_HEAD:h * D_HEAD + 1] for h in heads] for blk in QUERY_BLOCKS]

        def tiles(js, work, carry):
            rows = [pl.ds(pl.multiple_of(j * SB_BLOCK, SB_BLOCK), SB_BLOCK) for j in js]
            ks = [k_ref[rw, :].astype(MXU_DTYPE) for rw in rows]
            vs = [v_ref[rw, :].astype(MXU_DTYPE) for rw in rows]
            zs = [[_dot_nt(qhs[qb][h], ks[ts]) for h in heads] for qb, ts, _ in work]
            das = [[_dot_nt(doms[qb][h], vs[ts]) for h in heads] for qb, ts, _ in work]
            def gates(on, zq):
                sp = _softplus(zq)
                return _keep(on, sp), zq - sp, _keep(on, jnp.exp(zq - sp))

            trio = [[_lower_quadrants(gates, 3, z) if diag else gates(False, z) for z in zw]
                    for zw, (_, _, diag) in zip(zs, work)]
            ccs, lsigs, sigs = ([[t[o] for t in tw] for tw in trio] for o in range(3))
            pres = [[prefix(cc) for cc in cw] for cw in ccs]
            out = [list(per_block) for per_block in carry]
            for h in heads:
                for w, (qb, ts, diag) in enumerate(work):
                    pc, pdl, dq_h = out[qb][h]
                    logit = lsigs[w][h] + pres[w][h] - (totals[qb][h] - pc)
                    if diag:
                        a = _lower_quadrants(lambda on, lq: (_keep(on, jnp.exp(lq)),), 1, logit)[0]
                    else:
                        a = jnp.exp(logit)
                    dl = a * das[w][h]
                    dv_acc[js[ts]] += _dot(domts[qb][h], a)
                    dpre = prefix(dl)
                    dz = dl - sigs[w][h] * (pdl + dpre)
                    dzb = dz.astype(MXU_DTYPE)
                    dk_acc[js[ts]] += _dot(qhts[qb][h], dzb)
                    out[qb][h] = (pc + pres[w][h][:, SB_BLOCK - 1:SB_BLOCK], pdl + dpre[:, SB_BLOCK - 1:SB_BLOCK],
                                  dq_h + _dot(dzb, ks[ts]))
            return tuple(tuple(per_block) for per_block in out)

        zero = jnp.zeros((SB_BLOCK, 1), F32)
        start = ((zero, zero, jnp.zeros((SB_BLOCK, LANES), F32)),) * 2
        both = [(0, 0, False), (0, 1, False), (1, 0, False), (1, 1, False)]
        carry = lax.fori_loop(0, i, lambda s, cy: tiles([2 * s, 2 * s + 1], both, cy), (start, start))
        carry = tiles([2 * i, 2 * i + 1], [(0, 0, True), (1, 0, False), (1, 1, True)], carry)
        dq = jnp.concatenate([cb[0][2] * masks[0] + cb[1][2] * masks[1] for cb in carry], axis=0)
        dq_ref[...] = (dq * SB_SCALE).astype(dq_ref.dtype)

        @pl.when(i == nq - 1)
        def _():
            for j in range(seq // SB_BLOCK):
                tile_rows = slice(j * SB_BLOCK, (j + 1) * SB_BLOCK)
                dk_ref[tile_rows, :] = dk_acc[j].T.astype(dk_ref.dtype)
                dv_ref[tile_rows, :] = dv_acc[j].T.astype(dv_ref.dtype)

        @pl.when(step == n_steps - 1)
        def _():
            plan.finish()

    qspec = pl.BlockSpec((SB_STEP, LANES), lambda p, b, i: (b * nq + i, q0 + p))
    ospec = pl.BlockSpec((SB_STEP, LANES), lambda p, b, i: (b * nq + i, p))
    dmspec = pl.BlockSpec((SB_STEP, LANES), lambda p, b, i: (b * nq + i, N_PAIRS + p))
    full = lambda k: pl.BlockSpec((seq, LANES), lambda p, b, i: (b, k + p))
    vec = pl.BlockSpec((1, LANES), lambda p, b, i: (0, p))
    hbm = pl.BlockSpec(memory_space=pltpu.HBM)
    piece = jax.ShapeDtypeStruct((t, GROUP), MXU_DTYPE)
    outs = pl.pallas_call(
        body, name="sb_bwd", grid=(N_PAIRS, n_seq, nq),
        in_specs=[qspec, full(k0), full(v0), vec, pl.BlockSpec(tri.shape, lambda p, b, i: (0, 0)), dmspec, ospec, ospec]
        + [hbm] * n_w,
        out_specs=[ospec, full(0), full(0), vec] + [hbm] * n_w,
        out_shape=[piece, piece, piece, jax.ShapeDtypeStruct((1, GROUP), F32)]
        + [jax.ShapeDtypeStruct(g.shape, g.dtype) for g in grads],
        scratch_shapes=[pltpu.VMEM((seq // SB_BLOCK, LANES, SB_BLOCK), F32),
                        pltpu.VMEM((seq // SB_BLOCK, LANES, SB_BLOCK), F32),
                        pltpu.SemaphoreType.DMA((n_w * (N_DEV - 1),)), pltpu.SemaphoreType.DMA((n_w * (N_DEV - 1),)),
                        pltpu.SemaphoreType.DMA((n_w,))],
        compiler_params=_params(3),
    )(proj, proj, proj, norm_g, tri, dmix, opre, ctot, *grads)
    return outs[0], outs[1], outs[2], outs[3], list(outs[4:])


def _mesh_place():
    x, y, c = lax.axis_index("x"), lax.axis_index("y"), lax.axis_index("c")
    return x, y, c


def _peer(x, y, c, k):
    px = lax.rem(x + ((k >> 2) & 1), 2)
    py = lax.rem(y + ((k >> 1) & 1), 2)
    pc = lax.rem(c + (k & 1), 2)
    return (px, py, pc), 4 * px + 2 * py + pc


def _remote(src, dst, send_sem, recv_sem, to):
    return pltpu.make_async_remote_copy(src_ref=src, dst_ref=dst, send_sem=send_sem, recv_sem=recv_sem,
                                        device_id=to, device_id_type=pl.DeviceIdType.MESH)


class _GatherPlan:
    COPIES = 7

    def __init__(self, shards, gathered, send_sems, recv_sems, local_sems, senders=None, pad_rows=0):
        x, y, c = _mesh_place()
        self.c = c
        self.me = (x, y, c)
        self.sibling = (x, y, 1 - c)
        self.chips = [(1 - x, y), (x, 1 - y), (1 - x, 1 - y)]
        self.tensors = list(zip(shards, gathered))
        self.send_sems, self.recv_sems, self.local_sems = send_sems, recv_sems, local_sems
        self.senders, self.pad_rows = senders, pad_rows

    @staticmethod
    def _index(place):
        return 4 * place[0] + 2 * place[1] + place[2]

    def _slot(self, w, place):
        shard, gathered = self.tensors[w]
        if self.senders is None:
            return gathered.at[self._index(place)]
        rows = shard.shape[0]
        return gathered.at[pl.ds(self.pad_rows + (self._index(place) - self.senders[0]) * rows, rows)]

    def _if_sent(self, place, fn):
        if self.senders is None:
            fn()
        else:
            idx = self._index(place)
            first, count = self.senders
            pl.when(jnp.logical_and(idx >= first, idx < first + count))(fn)

    def _copy(self, w, k, block, to, own=False):
        slot = self._slot(w, block)
        n = w * self.COPIES + k
        return _remote(self.tensors[w][0] if own else slot, slot, self.send_sems.at[n], self.recv_sems.at[n], to)

    def _local(self, w):
        return pltpu.make_async_copy(self.tensors[w][0], self._slot(w, self.me), self.local_sems.at[w])

    def _first(self, w):
        return [self._copy(w, 0, self.me, self.sibling, own=True)] + [
            self._copy(w, 1 + j, self.me, (*chip, self.c), own=True) for j, chip in enumerate(self.chips)]

    def _passed(self, w, j):
        return self._copy(w, 4 + j, (*self.chips[j], self.c), self.sibling)

    def start(self):
        for w in range(len(self.tensors)):
            def send(w=w):
                self._local(w).start()
                for cp in self._first(w):
                    cp.start()

            self._if_sent(self.me, send)

    def forward(self):
        for w in range(len(self.tensors)):
            for j, chip in enumerate(self.chips):
                def pass_on(w=w, j=j, chip=chip):
                    self._copy(w, 1 + j, (*chip, self.c), self.me).wait_recv()
                    self._passed(w, j).start()

                self._if_sent((*chip, self.c), pass_on)

    def finish(self):
        for w in range(len(self.tensors)):
            self._if_sent(self.sibling, lambda w=w: self._copy(w, 0, self.sibling, self.me).wait_recv())
            for j, chip in enumerate(self.chips):
                self._if_sent((*chip, 1 - self.c),
                              lambda w=w, j=j, chip=chip: self._copy(w, 4 + j, (*chip, 1 - self.c), self.me).wait_recv())
                self._if_sent((*chip, self.c), lambda w=w, j=j: self._passed(w, j).wait_send())

            def sent(w=w):
                for cp in self._first(w):
                    cp.wait_send()
                self._local(w).wait()

            self._if_sent(self.me, sent)


class _ScatterPlan:
    def __init__(self, grads, lands, send_sems, recv_sems, local_sems):
        self.place = _mesh_place()
        x, y, c = self.place
        self.me = 4 * x + 2 * y + c
        self.tensors = list(zip(grads, lands))
        self.send_sems, self.recv_sems, self.local_sems = send_sems, recv_sems, local_sems

    def _copies(self, w):
        grad, land = self.tensors[w]
        out = []
        for k in range(1, N_DEV):
            peer, pidx = _peer(*self.place, k)
            n = w * (N_DEV - 1) + k - 1
            sems = (self.send_sems.at[n], self.recv_sems.at[n], peer)
            out.append((_remote(grad.at[pidx], land.at[self.me], *sems), _remote(grad.at[pidx], land.at[pidx], *sems)))
        return out

    def _local(self, w):
        grad, land = self.tensors[w]
        return pltpu.make_async_copy(grad.at[self.me], land.at[self.me], self.local_sems.at[w])

    def start(self):
        for w in range(len(self.tensors)):
            self._local(w).start()
            for send, _ in self._copies(w):
                send.start()

    def finish(self):
        for w in range(len(self.tensors)):
            copies = self._copies(w)
            for _, arrival in copies:
                arrival.wait_recv()
            for send, _ in copies:
                send.wait_send()
            self._local(w).wait()


def _cast_shards(shards):
    def body(*refs):
        n = len(refs) // 2
        for src, dst in zip(refs[:n], refs[n:]):
            dst[...] = src[...].astype(dst.dtype)

    vmem = pl.BlockSpec(memory_space=pltpu.VMEM)
    return pl.pallas_call(
        body, name="cast_shards", in_specs=[vmem] * len(shards), out_specs=[vmem] * len(shards),
        out_shape=[jax.ShapeDtypeStruct(s.shape, BF16) for s in shards],
        compiler_params=pltpu.CompilerParams(vmem_limit_bytes=VMEM_LIMIT),
    )(*shards)


def _gather_w_in(shard, x, g, senders, chunk=512):
    rows, cols = shard.shape
    t, d = x.shape
    chunk = min(chunk, t)
    n_chunks = t // chunk

    def body(w_ref, x_ref, g_ref, out_ref, h_ref, x_buf, send_sems, recv_sems, local_sems, x_sems):
        plan = _GatherPlan([w_ref], [out_ref], send_sems, recv_sems, local_sems, senders=senders)
        plan.start()

        def fetch(j):
            return pltpu.make_async_copy(x_ref.at[pl.ds(j * chunk, chunk)], x_buf.at[j % 2], x_sems.at[j % 2])

        fetch(0).start()
        gv = g_ref[...]
        for j in range(n_chunks):
            if j + 1 < n_chunks:
                fetch(j + 1).start()
            fetch(j).wait()
            xv = x_buf[j % 2]
            r = lax.rsqrt(jnp.mean(xv * xv, axis=-1, keepdims=True) + EPS)
            h_ref[j * chunk:(j + 1) * chunk, :] = (xv * r * gv).astype(h_ref.dtype)
        plan.forward()
        plan.finish()

    vmem = pl.BlockSpec(memory_space=pltpu.VMEM)
    return pl.pallas_call(
        body, name="gather_w_in", in_specs=[vmem, pl.BlockSpec(memory_space=pltpu.HBM), vmem], out_specs=[vmem, vmem],
        out_shape=[jax.ShapeDtypeStruct((senders[1] * rows, cols), shard.dtype),
                   jax.ShapeDtypeStruct((t, d), MXU_DTYPE)],
        scratch_shapes=[pltpu.VMEM((2, chunk, d), F32), pltpu.SemaphoreType.DMA((_GatherPlan.COPIES,)),
                        pltpu.SemaphoreType.DMA((_GatherPlan.COPIES,)), pltpu.SemaphoreType.DMA((1,)),
                        pltpu.SemaphoreType.DMA((2,))],
        compiler_params=pltpu.CompilerParams(vmem_limit_bytes=VMEM_LIMIT),
    )(shard, x, g)


def _dw_rows(pieces, b, name, tn=512, tt=1024):
    t, n = b.shape
    widths = [p.shape[1] for p in pieces]
    rows = sum(widths)
    tt = min(tt, t)
    steps = t // tt
    n_p = len(pieces)

    def body(*refs):
        piece_refs, b_ref, o_ref, acc = refs[:n_p], refs[n_p], refs[n_p + 1], refs[n_p + 2]
        s = pl.program_id(1)

        @pl.when(s == 0)
        def _():
            acc[...] = jnp.zeros_like(acc)

        bv = b_ref[...]
        off = 0
        for p_ref, width in zip(piece_refs, widths):
            acc[off:off + width, :] += _dot_tn(p_ref[...], bv)
            off += width

        @pl.when(s == steps - 1)
        def _():
            o_ref[...] = acc[...].astype(o_ref.dtype)

    return pl.pallas_call(
        body, name=name, grid=(n // tn, steps),
        in_specs=[pl.BlockSpec((tt, width), lambda j, s: (s, 0)) for width in widths]
        + [pl.BlockSpec((tt, tn), lambda j, s: (s, j))],
        out_specs=pl.BlockSpec((rows, tn), lambda j, s: (0, j)),
        out_shape=jax.ShapeDtypeStruct((rows, n), BF16),
        scratch_shapes=[pltpu.VMEM((rows, tn), F32)],
        compiler_params=_params(2),
    )(*pieces, b)


def _dh_norm_bwd(pairs, x, g, res, name, tm=512, after=(), next_w=None):
    m, d = x.shape
    tm = min(tm, m)
    n_p = len(pairs)
    n_steps = m // tm
    n_in = 2 * n_p + 3 + len(after) + (next_w is not None)

    def body(*refs):
        x_ref, g_ref, res_ref = refs[2 * n_p:2 * n_p + 3]
        dx_ref, dg_ref = refs[n_in:n_in + 2]

        @pl.when(pl.program_id(0) == 0)
        def _():
            dg_ref[...] = jnp.zeros_like(dg_ref)

        dh = None
        for q in range(n_p):
            part = _dot(refs[2 * q][...], refs[2 * q + 1][...])
            dh = part if dh is None else dh + part
        xv = x_ref[...]
        r = lax.rsqrt(jnp.mean(xv * xv, axis=-1, keepdims=True) + EPS)
        xh = xv * r
        dxh = dh * g_ref[...]
        dxv = res_ref[...] + r * (dxh - xh * jnp.mean(dxh * xh, axis=-1, keepdims=True))
        dx_ref[...] = dxv
        dg_ref[...] += jnp.sum(dh * xh, axis=0, keepdims=True)
        if next_w is not None:
            refs[n_in + 2][...] = _dot_nt(dxv, refs[n_in - 1][...])

    in_specs, args = [], []
    for a, w, r in pairs:
        k = a.shape[1]
        in_specs += [pl.BlockSpec((tm, k), lambda i: (i, 0)),
                     pl.BlockSpec((k, d), functools.partial(lambda i, r: (r, 0), r=r or 0),
                                  pipeline_mode=pl.Buffered(1))]
        args += [a, w]
    row = pl.BlockSpec((tm, d), lambda i: (i, 0))
    vec = pl.BlockSpec((1, d), lambda i: (0, 0))
    extra_in, extra_out, extra_shape = [], [], []
    if next_w is not None:
        e = next_w.shape[0]
        extra_in = [pl.BlockSpec((e, d), lambda i: (0, 0), pipeline_mode=pl.Buffered(1))]
        extra_out = [pl.BlockSpec((tm, e), lambda i: (i, 0))]
        extra_shape = [jax.ShapeDtypeStruct((m, e), F32)]
        args_tail = (next_w,)
    else:
        args_tail = ()
    return pl.pallas_call(
        body, name=name, grid=(n_steps,),
        in_specs=in_specs + [row, vec, row] + [pl.BlockSpec(memory_space=pl.ANY)] * len(after) + extra_in,
        out_specs=[row, vec] + extra_out,
        out_shape=[jax.ShapeDtypeStruct((m, d), F32), jax.ShapeDtypeStruct((1, d), F32)] + extra_shape,
        compiler_params=_params(1),
    )(*args, x, g, res, *after, *args_tail)


def _exchange_start(grad, name):
    def body(g_ref, land_ref, send_sem, recv_sem, local_sem, g_thru, land_thru, token):
        del g_thru, land_thru
        x, y, c = _mesh_place()
        me = 4 * x + 2 * y + c
        pltpu.make_async_copy(g_ref.at[me], land_ref.at[me], local_sem).start()
        for k in range(1, N_DEV):
            peer, pidx = _peer(x, y, c, k)
            _remote(g_ref.at[pidx], land_ref.at[me], send_sem, recv_sem, peer).start()
        token[...] = jnp.zeros_like(token)

    hbm = pl.BlockSpec(memory_space=pltpu.HBM)
    sem = pl.BlockSpec(memory_space=pltpu.SEMAPHORE)
    buf = pltpu.HBM(grad.shape, grad.dtype)
    return pl.pallas_call(
        body, name=name,
        out_shape=(pltpu.SemaphoreType.DMA(()), pltpu.SemaphoreType.DMA(()), pltpu.SemaphoreType.DMA(()), buf, buf,
                   jax.ShapeDtypeStruct((8, LANES), F32)),
        in_specs=(hbm, hbm), out_specs=(sem, sem, sem, hbm, hbm, pl.BlockSpec(memory_space=pltpu.VMEM)),
        input_output_aliases={0: 3, 1: 4},
        compiler_params=pltpu.CompilerParams(has_side_effects=pltpu.SideEffectType.DATAFLOW_SIDE_EFFECTING),
    )(pltpu.with_memory_space_constraint(grad, pltpu.HBM),
      pltpu.with_memory_space_constraint(lax.empty(grad.shape, grad.dtype), pltpu.HBM))


def _exchange_wait(send_sem, recv_sem, local_sem, grad, land, after, name):
    def body(g_ref, land_ref, send_sem, recv_sem, local_sem, *rest):
        x, y, c = _mesh_place()
        peer, _ = _peer(x, y, c, 1)
        others = pl.ds(0, N_DEV - 1)
        seven = _remote(g_ref.at[others], land_ref.at[others], send_sem, recv_sem, peer)
        seven.wait_send()
        seven.wait_recv()
        pltpu.make_async_copy(g_ref.at[0], land_ref.at[0], local_sem).wait()

    hbm = pl.BlockSpec(memory_space=pltpu.HBM)
    sem = pl.BlockSpec(memory_space=pltpu.SEMAPHORE)
    buf = pltpu.HBM(grad.shape, grad.dtype)
    return pl.pallas_call(
        body, name=name, out_shape=(buf, buf),
        in_specs=(hbm, hbm, sem, sem, sem) + (pl.BlockSpec(memory_space=pl.ANY),) * len(after), out_specs=(hbm, hbm),
        input_output_aliases={0: 0, 1: 1},
        compiler_params=pltpu.CompilerParams(has_side_effects=pltpu.SideEffectType.DATAFLOW_SIDE_EFFECTING),
    )(grad, land, send_sem, recv_sem, local_sem, *after)[1]


SMALL_LAYOUT = ((0, 0, 0, 0, D_MODEL), (1, 0, 1, 0, GROUP), (1, 1, 1, GROUP, GROUP), (2, 0, 2, 0, GROUP),
                (3, 0, 2, GROUP, GROUP), (4, 0, 3, 0, D_MODEL), (5, 0, 4, 0, D_MODEL))
LOSS_ROW = 5
N_SMALL = 6


def _small_step(grads, loss_part, ws, ms, vs):
    def body(*refs):
        g_in, loss_in = refs[:N_SMALL], refs[N_SMALL]
        params = [refs[1 + (q + 1) * N_SMALL:1 + (q + 2) * N_SMALL] for q in range(3)]
        o0 = 1 + 4 * N_SMALL
        outs = [refs[o0 + q * N_SMALL:o0 + (q + 1) * N_SMALL] for q in range(4)]
        loss_out = refs[o0 + 4 * N_SMALL]
        pack, land, wp, mp, vp, send_sems, recv_sems = refs[o0 + 4 * N_SMALL + 1:]

        def place(dst, srcs):
            dst[...] = jnp.zeros_like(dst)
            for p, sr, dr, dc, width in SMALL_LAYOUT:
                dst[dr:dr + 1, dc:dc + width] = srcs[p][sr:sr + 1, :]

        place(pack, g_in)
        pack[LOSS_ROW:LOSS_ROW + 1, 0:LANES] = loss_in[...]
        for dst, srcs in zip((wp, mp, vp), params):
            place(dst, srcs)

        x, y, c = _mesh_place()
        me = 4 * x + 2 * y + c
        land[me] = pack[...]
        sends = []
        for k in range(1, N_DEV):
            peer, _ = _peer(x, y, c, k)
            cp = _remote(pack, land.at[me], send_sems.at[k - 1], recv_sems.at[k - 1], peer)
            cp.start()
            sends.append(cp)
        for k in range(1, N_DEV):
            peer, pidx = _peer(x, y, c, k)
            _remote(pack, land.at[pidx], send_sems.at[k - 1], recv_sems.at[k - 1], peer).wait_recv()
        for cp in sends:
            cp.wait_send()

        g = land[0]
        for d in range(1, N_DEV):
            g = g + land[d]
        delta, nm, nv = _adam(wp[...], g, mp[...], vp[...])
        for val, out in zip((g, delta, nm, nv), outs):
            for p, sr, dr, dc, width in SMALL_LAYOUT:
                out[p][sr:sr + 1, :] = val[dr:dr + 1, dc:dc + width]
        loss_out[...] = g[LOSS_ROW:LOSS_ROW + 1, 0:LANES]

    vmem = pl.BlockSpec(memory_space=pltpu.VMEM)
    n_in = 1 + 4 * N_SMALL
    shapes = [jax.ShapeDtypeStruct(w.shape, F32) for w in ws]
    packed = pltpu.VMEM((SMALL_ROWS, D_MODEL), F32)
    outs = pl.pallas_call(
        body, name="small_step", in_specs=[vmem] * n_in, out_specs=[vmem] * (4 * N_SMALL + 1),
        out_shape=shapes * 4 + [jax.ShapeDtypeStruct((1, LANES), F32)],
        scratch_shapes=[packed, pltpu.VMEM((N_DEV, SMALL_ROWS, D_MODEL), F32), packed, packed, packed,
                        pltpu.SemaphoreType.DMA((N_DEV - 1,)), pltpu.SemaphoreType.DMA((N_DEV - 1,))],
    )(*grads, loss_part, *ws, *ms, *vs)
    return [outs[q * N_SMALL:(q + 1) * N_SMALL] for q in range(4)], outs[4 * N_SMALL]


def _adam(w, g, m, v):
    m = ADAM_B1 * m + (1.0 - ADAM_B1) * g
    v = ADAM_B2 * v + (1.0 - ADAM_B2) * (g * g)
    m_hat = m / (1.0 - ADAM_B1 ** ADAM_STEP)
    v_hat = v / (1.0 - ADAM_B2 ** ADAM_STEP)
    delta = -ADAM_LR * (m_hat / (jnp.sqrt(v_hat) + ADAM_EPS) + ADAM_WD * w)
    return delta, m, v


def _reduce_adamw(land, w, m, v, name, tr, after=()):
    _, rows, width = land.shape

    def body(land_ref, w_ref, m_ref, v_ref, *rest):
        g_ref, d_ref, nm_ref, nv_ref = rest[len(after):]
        g = land_ref[0].astype(F32)
        for d in range(1, N_DEV):
            g = g + land_ref[d].astype(F32)
        delta, nm, nv = _adam(w_ref[...], g, m_ref[...], v_ref[...])
        g_ref[...] = g
        d_ref[...] = delta
        nm_ref[...] = nm
        nv_ref[...] = nv

    row = pl.BlockSpec((tr, width), lambda i: (i, 0))
    out = jax.ShapeDtypeStruct((rows, width), F32)
    return pl.pallas_call(
        body, name=name, grid=(rows // tr,),
        in_specs=[pl.BlockSpec((N_DEV, tr, width), lambda i: (0, i, 0)), row, row, row]
        + [pl.BlockSpec(memory_space=pl.ANY)] * len(after),
        out_specs=[row, row, row, row], out_shape=[out, out, out, out],
        compiler_params=_params(1),
    )(land, w, m, v, *after)


def kernel(x, mix_norm_g, w_in, lower_bounds, hgrn_norm_g, sb_norm_g, w_out, ffn_norm_g, w_gate, w_up, w_down, final_norm_g, loss_target, m_mix_norm_g, m_w_in, m_lower_bounds, m_hgrn_norm_g, m_sb_norm_g, m_w_out, m_ffn_norm_g, m_w_gate, m_w_up, m_w_down, m_final_norm_g, v_mix_norm_g, v_w_in, v_lower_bounds, v_hgrn_norm_g, v_sb_norm_g, v_w_out, v_ffn_norm_g, v_w_gate, v_w_up, v_w_down, v_final_norm_g):
    n_seq, seq, d = x.shape
    t = n_seq * seq
    x2d = x.reshape(t, d)
    tgt = loss_target.reshape(t, d)
    final_g = final_norm_g.reshape(1, d)
    col_sharded = (True, False, True, True, False)

    def as_rows(ws):
        return [w[0].T if tr else w[0] for w, tr in zip(ws, col_sharded)]

    big_w = as_rows([w_in, w_out, w_gate, w_up, w_down])
    big_m = as_rows([m_w_in, m_w_out, m_w_gate, m_w_up, m_w_down])
    big_v = as_rows([v_w_in, v_w_out, v_w_gate, v_w_up, v_w_down])

    sh_in, sh_out, sh_gate, sh_up, sh_down = _cast_shards(big_w)
    shard_rows = IN_COLS // N_DEV
    n_first = -(-4 * GROUP // shard_rows)
    d_last = 4 * GROUP // shard_rows
    pad = -(4 * GROUP - d_last * shard_rows) % GROUP
    sb_row0 = pad + 4 * GROUP - d_last * shard_rows
    wt_h, h1 = _gather_w_in(sh_in, x2d, mix_norm_g, (0, n_first))
    proj_h, wt_s = _mm_nt(h1, wt_h, "proj_hgrn", rows=(0, 4 * GROUP), tm=1024, tk=4 * GROUP,
                          gather=(sh_in, (d_last, N_DEV - d_last), pad))
    proj_s = _mm_nt(h1, wt_s, "proj_sb", rows=(sb_row0, 3 * GROUP), out_dtype=MXU_DTYPE, tm=1024, tk=512)
    mixed, oa_pre, states = _hgrn_fwd(proj_h, lower_bounds, hgrn_norm_g, n_seq, seq)
    mixed, ob_pre, ctot, gathered = _sb_fwd(proj_s, sb_norm_g, mixed, [sh_out, sh_gate, sh_up, sh_down], n_seq, seq)
    wf_out = gathered[0].reshape(d, d)
    wt_gate = gathered[1].reshape(D_FF, d)
    wt_up = gathered[2].reshape(D_FF, d)
    wf_down = gathered[3].reshape(D_FF, d)
    x1, h2 = _mix_out_norm(mixed, wf_out, x2d, ffn_norm_g, "mix_out")
    gate, up, ff = _ffn_up(h2, wt_gate, wt_up, "ffn_up")
    dx2, dx2m, d_final_g, loss_part = _ffn_down_loss(ff, wf_down, x1, tgt, final_g, "ffn_down_loss")

    dgate, dup = _ffn_bwd_act(dx2m, wf_down, gate, up, "ffn_bwd_act")
    dw_down = _mm_tn(ff, dx2m, "dw_down", tk=1408, tn=1024).reshape(N_DEV, D_FF // N_DEV, d)
    dw_gate = _mm_tn(dgate, h2, "dw_gate", tk=1408, tn=1024).reshape(N_DEV, D_FF // N_DEV, d)
    dw_up = _mm_tn(dup, h2, "dw_up", tk=1408, tn=1024).reshape(N_DEV, D_FF // N_DEV, d)
    dx1, d_ffn_g, dmix = _dh_norm_bwd([(dgate, wt_gate, None), (dup, wt_up, None)], x1, ffn_norm_g, dx2, "dh_ffn",
                                      tm=256, next_w=wf_out)
    dw_out = _mm_tn(mixed, dx1, "dw_out", tk=1024, tn=1024).reshape(N_DEV, d // N_DEV, d)
    dsq, dsk, dsv, d_sb_g, lands = _sb_bwd(proj_s, sb_norm_g, dmix, ob_pre, ctot, [dw_out, dw_gate, dw_up, dw_down],
                                            n_seq, seq)
    dhq, dhf, dhi, dhg, d_lb, d_hgrn_g = _hgrn_bwd(proj_h, lower_bounds, hgrn_norm_g, dmix, oa_pre, states, n_seq,
                                                   seq)
    dproj = [dhq, dhf, dhi, dhg, dsq, dsk, dsv]
    dw_in = _dw_rows(dproj, h1, "dw_in").reshape(N_DEV, IN_COLS // N_DEV, d)
    send_sem, recv_sem, local_sem, dw_in, land_in, token = _exchange_start(dw_in, "dw_in_send")
    w_rows = [(wt_h, k) for k in range(4)] + [(wt_s, sb_row0 // GROUP + k) for k in range(3)]
    dx, d_mix_g = _dh_norm_bwd([(piece, w, r) for piece, (w, r) in zip(dproj, w_rows)], x2d, mix_norm_g, dx1, "dh_mix",
                               after=(token,))

    tiles = {"in": 224, "out": 128, "gate": 176, "up": 176, "down": 176}
    keys = list(tiles)
    rest = [_reduce_adamw(land, w, m, v, "adamw_" + key, tr=tiles[key], after=(token,))
            for key, land, w, m, v in zip(keys[1:], lands, big_w[1:], big_m[1:], big_v[1:])]
    small, loss_row = _small_step(
        [d_mix_g, d_lb, d_hgrn_g, d_sb_g, d_ffn_g, d_final_g], loss_part,
        [mix_norm_g, lower_bounds, hgrn_norm_g, sb_norm_g, ffn_norm_g, final_g],
        [m_mix_norm_g, m_lower_bounds, m_hgrn_norm_g, m_sb_norm_g, m_ffn_norm_g, m_final_norm_g.reshape(1, d)],
        [v_mix_norm_g, v_lower_bounds, v_hgrn_norm_g, v_sb_norm_g, v_ffn_norm_g, v_final_norm_g.reshape(1, d)])
    land_in = _exchange_wait(send_sem, recv_sem, local_sem, dw_in, land_in,
                             [dx, loss_row] + [res[0] for res in rest], "dw_in_await")
    big = [_reduce_adamw(land_in, big_w[0], big_m[0], big_v[0], "adamw_in", tr=tiles["in"])] + rest
    big = [[r.T if tr else r for r in res] for res, tr in zip(big, col_sharded)]

    outs = [loss_row[0, 0], dx.reshape(n_seq, seq, d)]
    for q in range(4):
        b_in, b_out, b_gate, b_up, b_down = [res[q][None] for res in big]
        s_mix, s_lb, s_hgrn, s_sb, s_ffn, s_final = small[q]
        outs += [s_mix, b_in, s_lb, s_hgrn, s_sb, b_out, s_ffn, b_gate, b_up, b_down, s_final.reshape(d)]
    return tuple(outs)
```

```python
import functools
import math

import jax
import jax.numpy as jnp
from jax import lax
from jax.experimental import pallas as pl
from jax.experimental.pallas import tpu as pltpu

F32 = jnp.float32
BF16 = jnp.bfloat16
MXU_DTYPE = BF16

EPS = 1e-6
D_MODEL = 1024
N_HEADS = 8
D_HEAD = 64
GROUP = N_HEADS * D_HEAD
IN_COLS = 7 * GROUP
D_FF = 2816
CHUNK = 64
LANES = 128
N_PAIRS = GROUP // LANES
SUPER = 256
SB_BLOCK = 256
N_DEV = 8

ADAM_LR = 0.001
ADAM_B1 = 0.9
ADAM_B2 = 0.999
ADAM_EPS = 1e-08
ADAM_WD = 0.01
ADAM_STEP = 10

SMALL_ROWS = 8
FF_TILE = D_FF // 2

VMEM_LIMIT = 48 * 1024 * 1024


def _params(n_axes, vmem=VMEM_LIMIT):
    return pltpu.CompilerParams(dimension_semantics=("arbitrary",) * n_axes, vmem_limit_bytes=vmem)


def _whole(a):
    return pl.BlockSpec(a.shape, functools.partial(lambda i, nd: (0,) * nd, nd=len(a.shape)))


def _dot(a, b):
    return jnp.dot(a.astype(MXU_DTYPE), b.astype(MXU_DTYPE), preferred_element_type=F32)


def _dot_nt(a, b):
    return lax.dot_general(a.astype(MXU_DTYPE), b.astype(MXU_DTYPE), (((1,), (1,)), ((), ())),
                           preferred_element_type=F32)


def _dot_tn(a, b):
    return lax.dot_general(a.astype(MXU_DTYPE), b.astype(MXU_DTYPE), (((0,), (0,)), ((), ())),
                           preferred_element_type=F32)


def _split(x, parts):
    out, r = [], x
    for _ in range(parts):
        h = r.astype(BF16)
        out.append(h)
        r = r - h.astype(F32)
    return out


def _rsum_left(u, x, parts):
    acc = None
    for h in _split(x, parts):
        d = jnp.dot(u, h, preferred_element_type=F32)
        acc = d if acc is None else acc + d
    return acc


def _ones_where(mask):
    return jnp.where(mask, 1.0, 0.0).astype(BF16)


def _sigmoid(x):
    return 1.0 / (1.0 + jnp.exp(-x))


def _softplus(x):
    return jnp.maximum(x, 0.0) + jnp.log(1.0 + jnp.exp(-jnp.abs(x)))


def _head_masks():
    lane = lax.broadcasted_iota(jnp.int32, (1, LANES), 1)
    return [jnp.where(lane < D_HEAD, 1.0, 0.0), jnp.where(lane >= D_HEAD, 1.0, 0.0)]


def _head_rstd(o, masks):
    sq = o * o
    r = None
    for m in masks:
        ms = jnp.sum(sq * m, axis=1, keepdims=True) * (1.0 / D_HEAD)
        t = lax.rsqrt(ms + EPS) * m
        r = t if r is None else r + t
    return r


def _head_mean(t, masks):
    out = None
    for m in masks:
        v = jnp.sum(t * m, axis=1, keepdims=True) * (1.0 / D_HEAD) * m
        out = v if out is None else out + v
    return out


def _mix_out_norm(a, w, res, g, name, tm=512):
    m, k = a.shape
    d = w.shape[1]
    tm = min(tm, m)

    def body(a_ref, w_ref, res_ref, g_ref, x_ref, h_ref):
        xv = res_ref[...] + _dot(a_ref[...], w_ref[...])
        x_ref[...] = xv
        r = lax.rsqrt(jnp.mean(xv * xv, axis=-1, keepdims=True) + EPS)
        h_ref[...] = (xv * r * g_ref[...]).astype(h_ref.dtype)

    row = pl.BlockSpec((tm, d), lambda i: (i, 0))
    return pl.pallas_call(
        body, name=name, grid=(m // tm,),
        in_specs=[pl.BlockSpec((tm, k), lambda i: (i, 0)), pl.BlockSpec((k, d), lambda i: (0, 0)), row,
                  pl.BlockSpec((1, d), lambda i: (0, 0))],
        out_specs=[row, row],
        out_shape=[jax.ShapeDtypeStruct((m, d), F32), jax.ShapeDtypeStruct((m, d), MXU_DTYPE)],
        compiler_params=_params(1),
    )(a, w, res, g)


def _ffn_down_loss(a, w, res, target, g, name, tm=512):
    m, k = a.shape
    d = w.shape[1]
    tm = min(tm, m)

    def body(a_ref, w_ref, res_ref, t_ref, g_ref, dx_ref, dxm_ref, dg_ref, loss_ref):
        xv = res_ref[...] + _dot(a_ref[...], w_ref[...])
        gv = g_ref[...]
        r = lax.rsqrt(jnp.mean(xv * xv, axis=-1, keepdims=True) + EPS)
        xh = xv * r
        e = xh * gv - t_ref[...]
        dy = e * (1.0 / d)
        dxh = dy * gv
        dxv = r * (dxh - xh * jnp.mean(dxh * xh, axis=-1, keepdims=True))
        dx_ref[...] = dxv
        dxm_ref[...] = dxv.astype(dxm_ref.dtype)

        @pl.when(pl.program_id(0) == 0)
        def _():
            dg_ref[...] = jnp.zeros_like(dg_ref)
            loss_ref[...] = jnp.zeros_like(loss_ref)

        dg_ref[...] += jnp.sum(dy * xh, axis=0, keepdims=True)
        part = 0.5 * jnp.sum(jnp.mean(e * e, axis=-1, keepdims=True), axis=0, keepdims=True)
        loss_ref[...] += jnp.broadcast_to(part, loss_ref.shape)

    row = pl.BlockSpec((tm, d), lambda i: (i, 0))
    vec = pl.BlockSpec((1, d), lambda i: (0, 0))
    return pl.pallas_call(
        body, name=name, grid=(m // tm,),
        in_specs=[pl.BlockSpec((tm, k), lambda i: (i, 0)), pl.BlockSpec((k, d), lambda i: (0, 0)), row, row, vec],
        out_specs=[row, row, vec, pl.BlockSpec((1, LANES), lambda i: (0, 0))],
        out_shape=[jax.ShapeDtypeStruct((m, d), F32), jax.ShapeDtypeStruct((m, d), MXU_DTYPE),
                   jax.ShapeDtypeStruct((1, d), F32), jax.ShapeDtypeStruct((1, LANES), F32)],
        compiler_params=_params(1),
    )(a, w, res, target, g)


def _mm_nt(a, w, name, rows=None, out_dtype=F32, tm=512, tk=512):
    m, n = a.shape
    row0, k = rows or (0, w.shape[0])
    tm, tk = min(tm, m), min(tk, k)
    first = row0 // tk

    def body(a_ref, w_ref, o_ref):
        o_ref[...] = _dot_nt(a_ref[...], w_ref[...]).astype(o_ref.dtype)

    return pl.pallas_call(
        body, name=name, grid=(k // tk, m // tm),
        in_specs=[pl.BlockSpec((tm, n), lambda j, i: (i, 0)), pl.BlockSpec((tk, n), lambda j, i: (first + j, 0))],
        out_specs=pl.BlockSpec((tm, tk), lambda j, i: (i, j)),
        out_shape=jax.ShapeDtypeStruct((m, k), out_dtype),
        compiler_params=_params(2),
    )(a, w)


def _mm_tn(a, b, name, tk, tn, tt=1024, out_dtype=BF16):
    t, k = a.shape
    n = b.shape[1]
    tt = min(tt, t)
    steps = t // tt

    def body(a_ref, b_ref, o_ref, acc):
        s = pl.program_id(2)

        @pl.when(s == 0)
        def _():
            acc[...] = jnp.zeros_like(acc)

        acc[...] += _dot_tn(a_ref[...], b_ref[...])

        @pl.when(s == steps - 1)
        def _():
            o_ref[...] = acc[...].astype(o_ref.dtype)

    return pl.pallas_call(
        body, name=name, grid=(k // tk, n // tn, steps),
        in_specs=[pl.BlockSpec((tt, tk), lambda i, j, s: (s, i)), pl.BlockSpec((tt, tn), lambda i, j, s: (s, j))],
        out_specs=pl.BlockSpec((tk, tn), lambda i, j, s: (i, j)),
        out_shape=jax.ShapeDtypeStruct((k, n), out_dtype),
        scratch_shapes=[pltpu.VMEM((tk, tn), F32)],
        compiler_params=_params(3),
    )(a, b)


def _ffn_up(h, wg_t, wu_t, name, tm=1024, tn=FF_TILE):
    m, k = h.shape
    n = wg_t.shape[0]
    tm = min(tm, m)

    def body(h_ref, wg_ref, wu_ref, gate_ref, up_ref, ff_ref):
        hv = h_ref[...]
        gate = _dot_nt(hv, wg_ref[...])
        up = _dot_nt(hv, wu_ref[...])
        gate_ref[...] = gate.astype(gate_ref.dtype)
        up_ref[...] = up.astype(up_ref.dtype)
        ff_ref[...] = (gate * _sigmoid(gate) * up).astype(ff_ref.dtype)

    wspec = pl.BlockSpec((tn, k), lambda j, i: (j, 0))
    ospec = pl.BlockSpec((tm, tn), lambda j, i: (i, j))
    return pl.pallas_call(
        body, name=name, grid=(n // tn, m // tm),
        in_specs=[pl.BlockSpec((tm, k), lambda j, i: (i, 0)), wspec, wspec],
        out_specs=[ospec, ospec, ospec],
        out_shape=[jax.ShapeDtypeStruct((m, n), MXU_DTYPE)] * 3,
        compiler_params=_params(2),
    )(h, wg_t, wu_t)


def _ffn_bwd_act(dx, wd, gate, up, name, tm=1024, tn=FF_TILE):
    m, k = dx.shape
    n = wd.shape[0]
    tm = min(tm, m)

    def body(dx_ref, wd_ref, gate_ref, up_ref, dgate_ref, dup_ref):
        dff = _dot_nt(dx_ref[...], wd_ref[...])
        gate = gate_ref[...].astype(F32)
        sg = _sigmoid(gate)
        dgate_ref[...] = (dff * up_ref[...].astype(F32) * sg * (1.0 + gate * (1.0 - sg))).astype(dgate_ref.dtype)
        dup_ref[...] = (dff * gate * sg).astype(dup_ref.dtype)

    ospec = pl.BlockSpec((tm, tn), lambda j, i: (i, j))
    return pl.pallas_call(
        body, name=name, grid=(n // tn, m // tm),
        in_specs=[pl.BlockSpec((tm, k), lambda j, i: (i, 0)), pl.BlockSpec((tn, k), lambda j, i: (j, 0)),
                  ospec, ospec],
        out_specs=[ospec, ospec],
        out_shape=[jax.ShapeDtypeStruct((m, n), MXU_DTYPE), jax.ShapeDtypeStruct((m, n), MXU_DTYPE)],
        compiler_params=_params(2),
    )(dx, wd, gate, up)


def _chunk_masks():
    r = lax.broadcasted_iota(jnp.int32, (SUPER, SUPER), 0)
    c = lax.broadcasted_iota(jnp.int32, (SUPER, SUPER), 1)
    same = jnp.right_shift(r, 6) == jnp.right_shift(c, 6)
    lower = jnp.logical_and(same, c <= r)
    upper = jnp.logical_and(same, c >= r)
    return same, lower, upper


def _head_block_mask():
    r = lax.broadcasted_iota(jnp.int32, (LANES, LANES), 0)
    c = lax.broadcasted_iota(jnp.int32, (LANES, LANES), 1)
    return jnp.where(jnp.right_shift(r, 6) == jnp.right_shift(c, 6), 1.0, 0.0)


def _lower_bound(lb_raw):
    return 1.0 / (1.0 + jnp.exp(lb_raw[1:2, :] - lb_raw[0:1, :]))


HGRN_UNROLL = 4
PER_SUPER = SUPER // CHUNK
CHUNK_ROWS = [slice(c * CHUNK, (c + 1) * CHUNK) for c in range(PER_SUPER)]


def _over_chunks(rows):
    return jnp.concatenate([jnp.broadcast_to(r, (CHUNK, LANES)) for r in rows], axis=0)


def _hgrn_gates(q, hf, lb, lower_b):
    sig = _sigmoid(hf)
    f = lb + (1.0 - lb) * sig
    k = 1.0 - f
    lf = jnp.log(f)
    b = _rsum_left(lower_b, lf, 2)
    ends = [b[cr.stop - 1:cr.stop, :] for cr in CHUNK_ROWS]
    eb = jnp.exp(b)
    enb = jnp.exp(-b)
    edb = jnp.exp(_over_chunks(ends) - b)
    decs = [jnp.exp(e) for e in ends]
    return sig, f, k, decs, eb, enb, edb, q * eb, k * enb, k * edb


def _hgrn_fwd(proj, lower_bounds, norm_g, n_seq, seq):
    t = n_seq * seq
    n_super = seq // SUPER
    n_chunks = seq // CHUNK

    def body(q_ref, f_ref, i_ref, g_ref, lb_ref, ng_ref, out_ref, opre_ref, st_ref):
        masks = _head_masks()
        _, lower, _ = _chunk_masks()
        lower_b = _ones_where(lower)
        bd = _head_block_mask()
        lb = _lower_bound(lb_ref[...])
        ng = ng_ref[...]

        def step(it, st):
            blocks = [HGRN_UNROLL * it + u for u in range(HGRN_UNROLL)]
            rows = [pl.ds(pl.multiple_of(sb * SUPER, SUPER), SUPER) for sb in blocks]
            vs = [i_ref[rw, :] for rw in rows]
            gates = [_hgrn_gates(q_ref[rw, :], f_ref[rw, :], lb, lower_b) for rw in rows]
            decs, qes, kes, kds = ([g[k] for g in gates] for k in (3, 7, 8, 9))
            scores = [[_dot_nt(qe * m, ke) for m in masks] for qe, ke in zip(qes, kes)]
            updates = [[_dot_tn(v[cr], kd[cr]) for cr in CHUNK_ROWS] for v, kd in zip(vs, kds)]
            states = [st]
            for dec_b, upd_b in zip(decs, updates):
                for dec, upd in zip(dec_b, upd_b):
                    states.append(states[-1] * dec + bd * upd)
            for u, sb in enumerate(blocks):
                for c in range(PER_SUPER):
                    st_ref[0, 0, sb * PER_SUPER + c] = states[u * PER_SUPER + c]
            intra = [[_dot(jnp.where(lower, p, 0.0), v) for p in sc] for sc, v in zip(scores, vs)]
            inter = [[_dot_nt(qe[cr], states[u * PER_SUPER + c]) for c, cr in enumerate(CHUNK_ROWS)]
                     for u, qe in enumerate(qes)]
            for rw, intra_b, inter_b in zip(rows, intra, inter):
                o = intra_b[0] * masks[0] + intra_b[1] * masks[1] + jnp.concatenate(inter_b, axis=0)
                opre_ref[rw, :] = o
                hg = g_ref[rw, :]
                on = o * _head_rstd(o, masks) * ng
                out_ref[rw, :] = (on * hg * _sigmoid(hg)).astype(out_ref.dtype)
            return states[-1]

        lax.fori_loop(0, n_super // HGRN_UNROLL, step, jnp.zeros((LANES, LANES), F32))

    def col(k):
        return pl.BlockSpec((seq, LANES), lambda p, b: (b, k * N_PAIRS + p))

    vec = lambda rows: pl.BlockSpec((rows, LANES), lambda p, b: (0, p))
    ospec = pl.BlockSpec((seq, LANES), lambda p, b: (b, p))
    return pl.pallas_call(
        body, name="hgrn_fwd", grid=(N_PAIRS, n_seq),
        in_specs=[col(0), col(1), col(2), col(3), vec(2), vec(1)],
        out_specs=[ospec, ospec,
                   pl.BlockSpec((1, 1, n_chunks, LANES, LANES), lambda p, b: (b, p, 0, 0, 0))],
        out_shape=[jax.ShapeDtypeStruct((t, 2 * GROUP), MXU_DTYPE), jax.ShapeDtypeStruct((t, GROUP), F32),
                   jax.ShapeDtypeStruct((n_seq, N_PAIRS, n_chunks, LANES, LANES), F32)],
        compiler_params=_params(2),
    )(proj, proj, proj, proj, lower_bounds, norm_g)


def _hgrn_bwd(proj, lower_bounds, norm_g, dmix, opre, states, n_seq, seq):
    t = n_seq * seq
    n_super = seq // SUPER
    n_chunks = seq // CHUNK
    per = SUPER // CHUNK

    def body(q_ref, f_ref, i_ref, g_ref, lb_ref, ng_ref, dm_ref, opre_ref, st_ref,
             dq_ref, df_ref, di_ref, dg_ref, dlb_ref, dng_ref):
        masks = _head_masks()
        _, lower, upper = _chunk_masks()
        lower_b, upper_b = _ones_where(lower), _ones_where(upper)
        bd = _head_block_mask()
        lb_raw = lb_ref[...]
        lb = _lower_bound(lb_raw)
        ng = ng_ref[...]

        @pl.when(pl.program_id(1) == 0)
        def _():
            dlb_ref[...] = jnp.zeros_like(dlb_ref)
            dng_ref[...] = jnp.zeros_like(dng_ref)

        def first_half(sb):
            rows = pl.ds(pl.multiple_of(sb * SUPER, SUPER), SUPER)
            q, hf, v, hg = q_ref[rows, :], f_ref[rows, :], i_ref[rows, :], g_ref[rows, :]
            sig, f, k, decs, eb, enb, edb, qe, ke, kd = _hgrn_gates(q, hf, lb, lower_b)
            o = opre_ref[rows, :]
            r = _head_rstd(o, masks)
            oh = o * r
            dm = dm_ref[rows, :]
            sg = _sigmoid(hg)
            dg_ref[rows, :] = (dm * oh * ng * sg * (1.0 + hg * (1.0 - sg))).astype(dg_ref.dtype)
            don = dm * hg * sg
            dng_ref[...] += jnp.sum(don * oh, axis=0, keepdims=True)
            doh = don * ng
            do = r * (doh - oh * _head_mean(doh * oh, masks))
            doms = [do * m for m in masks]
            qems = [qe * m for m in masks]
            scores = [_dot_nt(qem, ke) for qem in qems]
            dscores = [_dot_nt(dom, v) for dom in doms]
            prevs = [st_ref[0, 0, sb * per + c] for c in range(per)]
            dst_in = [_dot_tn(do[cr], qe[cr]) for cr in CHUNK_ROWS]
            dqe_i = [_dot(do[cr], prev) for cr, prev in zip(CHUNK_ROWS, prevs)]
            return dict(rows=rows, v=v, sig=sig, f=f, decs=decs, eb=eb, enb=enb, edb=edb, qe=qe, ke=ke, kd=kd,
                        doms=doms, qems=qems, scores=scores, dscores=dscores, prevs=prevs, dst_in=dst_in, dqe_i=dqe_i)

        def second_half(blk, dsts):
            v, qe, ke, kd = blk["v"], blk["qe"], blk["ke"], blk["kd"]
            ps = [jnp.where(lower, p, 0.0) for p in blk["scores"]]
            dps = [jnp.where(lower, dp, 0.0) for dp in blk["dscores"]]
            dqe_h = [_dot(dp, ke) for dp in dps]
            dke_h = [_dot_tn(dp, qem) for dp, qem in zip(dps, blk["qems"])]
            dv_h = [_dot_tn(p, dom) for p, dom in zip(ps, blk["doms"])]
            dus = [bd * d for d in dsts]
            dv_i = [_dot_nt(kd[cr], du) for cr, du in zip(CHUNK_ROWS, dus)]
            dkd_i = [_dot(v[cr], du) for cr, du in zip(CHUNK_ROWS, dus)]

            def finish():
                dqe = dqe_h[0] * masks[0] + dqe_h[1] * masks[1] + jnp.concatenate(blk["dqe_i"], axis=0)
                dke = dke_h[0] + dke_h[1]
                dv = dv_h[0] + dv_h[1] + jnp.concatenate(dv_i, axis=0)
                dkd = jnp.concatenate(dkd_i, axis=0)
                dk = dke * blk["enb"] + dkd * blk["edb"]
                db = dqe * qe - dke * ke - dkd * kd
                dkd_kd = dkd * kd
                dends = [jnp.sum(dkd_kd[cr], axis=0, keepdims=True)
                         + jnp.sum(dsts[c] * blk["prevs"][c], axis=0, keepdims=True) * blk["decs"][c]
                         for c, cr in enumerate(CHUNK_ROWS)]
                dlf = _rsum_left(upper_b, db, 2) + _over_chunks(dends)
                sig = blk["sig"]
                dfv = dlf / blk["f"] - dk
                rows = blk["rows"]
                dq_ref[rows, :] = (dqe * blk["eb"]).astype(dq_ref.dtype)
                di_ref[rows, :] = dv.astype(di_ref.dtype)
                df_ref[rows, :] = (dfv * (1.0 - lb) * sig * (1.0 - sig)).astype(df_ref.dtype)
                dlb = jnp.sum(dfv * (1.0 - sig), axis=0, keepdims=True)
                da0 = dlb * lb * (1.0 - lb)
                dlb_ref[0:1, :] += da0
                dlb_ref[1:2, :] -= da0

            return finish

        def step(it, dst):
            blocks = [first_half(n_super - 1 - HGRN_UNROLL * it - u) for u in range(HGRN_UNROLL)]
            all_dsts = []
            for blk in blocks:
                dsts = [None] * per
                for c in reversed(range(per)):
                    dsts[c] = dst
                    dst = bd * (dst * blk["decs"][c] + blk["dst_in"][c])
                all_dsts.append(dsts)
            for finish in [second_half(blk, dsts) for blk, dsts in zip(blocks, all_dsts)]:
                finish()
            return dst

        lax.fori_loop(0, n_super // HGRN_UNROLL, step, jnp.zeros((LANES, LANES), F32))

    def col(k):
        return pl.BlockSpec((seq, LANES), lambda p, b: (b, k * N_PAIRS + p))

    vec = lambda rows: pl.BlockSpec((rows, LANES), lambda p, b: (0, p))
    ospec = pl.BlockSpec((seq, LANES), lambda p, b: (b, p))
    piece = jax.ShapeDtypeStruct((t, GROUP), MXU_DTYPE)
    return pl.pallas_call(
        body, name="hgrn_bwd", grid=(N_PAIRS, n_seq),
        in_specs=[col(0), col(1), col(2), col(3), vec(2), vec(1), ospec, ospec,
                  pl.BlockSpec((1, 1, n_chunks, LANES, LANES), lambda p, b: (b, p, 0, 0, 0))],
        out_specs=[ospec, ospec, ospec, ospec, vec(2), vec(1)],
        out_shape=[piece, piece, piece, piece,
                   jax.ShapeDtypeStruct((2, GROUP), F32), jax.ShapeDtypeStruct((1, GROUP), F32)],
        compiler_params=_params(2),
    )(proj, proj, proj, proj, lower_bounds, norm_g, dmix, opre, states)


SB_SCALE = 1.0 / math.sqrt(D_HEAD)


SB_STEP = 2 * SB_BLOCK
QUERY_BLOCKS = (slice(0, SB_BLOCK), slice(SB_BLOCK, SB_STEP))


def _triangle(keep):
    row = lax.broadcasted_iota(jnp.int32, (SB_BLOCK, SB_BLOCK), 0)
    col = lax.broadcasted_iota(jnp.int32, (SB_BLOCK, SB_BLOCK), 1)
    return _ones_where(keep(row, col))


SB_HALF = SB_BLOCK // 2


def _keep(on_diagonal, x):
    if not on_diagonal:
        return x
    row = lax.broadcasted_iota(jnp.int32, (SB_HALF, SB_HALF), 0)
    col = lax.broadcasted_iota(jnp.int32, (SB_HALF, SB_HALF), 1)
    return jnp.where(col < row, x, 0.0)


def _lower_quadrants(fn, n_out, *tiles):
    def quadrant(r, c):
        return [t[r * SB_HALF:(r + 1) * SB_HALF, c * SB_HALF:(c + 1) * SB_HALF] for t in tiles]

    top, left, bottom = fn(True, *quadrant(0, 0)), fn(False, *quadrant(1, 0)), fn(True, *quadrant(1, 1))
    zero = jnp.zeros((SB_HALF, SB_HALF), F32)
    return [jnp.concatenate([jnp.concatenate([top[o], zero], axis=1), jnp.concatenate([left[o], bottom[o]], axis=1)],
                            axis=0) for o in range(n_out)]


def _sb_fwd(proj, norm_g, mixed, shards, n_seq, seq):
    t = n_seq * seq
    nq = seq // SB_STEP
    q0, k0, v0 = 0, N_PAIRS, 2 * N_PAIRS
    n_w = len(shards)
    n_steps = N_PAIRS * n_seq * nq
    tri = _triangle(lambda row, col: row >= col)

    def body(q_ref, k_ref, v_ref, ng_ref, tri_ref, mixed_in, *rest):
        del mixed_in
        shard_refs = rest[:n_w]
        out_ref, opre_ref, ctot_ref = rest[n_w:n_w + 3]
        gathered = rest[n_w + 3:2 * n_w + 3]
        send_sems, recv_sems, local_sems = rest[2 * n_w + 3:]
        i = pl.program_id(2)
        step = (pl.program_id(0) * n_seq + pl.program_id(1)) * nq + i
        plan = _GatherPlan(shard_refs, gathered, send_sems, recv_sems, local_sems)

        @pl.when(step == 0)
        def _():
            plan.start()

        @pl.when(step == (3 * n_steps) // 4)
        def _():
            plan.forward()

        masks = _head_masks()
        suffix = tri_ref[...]
        qhs = [[(q_ref[blk, :] * SB_SCALE * m).astype(MXU_DTYPE) for m in masks] for blk in QUERY_BLOCKS]

        def tiles(js, work, carry):
            rows = [pl.ds(pl.multiple_of(j * SB_BLOCK, SB_BLOCK), SB_BLOCK) for j in js]
            ks = [k_ref[rw, :].astype(MXU_DTYPE) for rw in rows]
            vs = [v_ref[rw, :].astype(MXU_DTYPE) for rw in rows]
            zs = [[_dot_nt(qh, ks[ts]) for qh in qhs[qb]] for qb, ts, _ in work]
            ccs = [[_lower_quadrants(lambda on, zq: (_keep(on, _softplus(zq)),), 1, z)[0] if diag else _softplus(z)
                    for z in zw] for zw, (_, _, diag) in zip(zs, work)]
            sums = [[jnp.dot(cc.astype(BF16), suffix, preferred_element_type=F32) for cc in cw] for cw in ccs]
            out = [list(per_block) for per_block in carry]
            for h in range(len(masks)):
                for w, (qb, ts, diag) in enumerate(work):
                    run, acc = out[qb][h]
                    logit = zs[w][h] - (sums[w][h] + run)
                    if diag:
                        a = _lower_quadrants(lambda on, lq: (_keep(on, jnp.exp(lq)),), 1, logit)[0]
                    else:
                        a = jnp.exp(logit)
                    out[qb][h] = (run + sums[w][h][:, 0:1], acc + _dot(a, vs[ts]))
            return tuple(tuple(per_block) for per_block in out)

        start = ((jnp.zeros((SB_BLOCK, 1), F32), jnp.zeros((SB_BLOCK, LANES), F32)),) * 2
        carry = tiles([2 * i, 2 * i + 1], [(0, 0, True), (1, 1, True), (1, 0, False)], (start, start))
        both = [(0, 0, False), (0, 1, False), (1, 0, False), (1, 1, False)]
        carry = lax.fori_loop(0, i, lambda s, cy: tiles([2 * (i - s) - 1, 2 * (i - s) - 2], both, cy), carry)
        opre = jnp.concatenate([cb[0][1] * masks[0] + cb[1][1] * masks[1] for cb in carry], axis=0)
        ctot = jnp.concatenate([cb[0][0] * masks[0] + cb[1][0] * masks[1] for cb in carry], axis=0)
        opre_ref[...] = opre
        ctot_ref[...] = ctot
        out_ref[...] = (opre * _head_rstd(opre, masks) * ng_ref[...]).astype(out_ref.dtype)

        @pl.when(step == n_steps - 1)
        def _():
            plan.finish()

    qspec = pl.BlockSpec((SB_STEP, LANES), lambda p, b, i: (b * nq + i, q0 + p))
    ospec = pl.BlockSpec((SB_STEP, LANES), lambda p, b, i: (b * nq + i, p))
    hbm = pl.BlockSpec(memory_space=pltpu.HBM)
    outs = pl.pallas_call(
        body, name="sb_fwd", grid=(N_PAIRS, n_seq, nq),
        in_specs=[qspec,
                  pl.BlockSpec((seq, LANES), lambda p, b, i: (b, k0 + p)),
                  pl.BlockSpec((seq, LANES), lambda p, b, i: (b, v0 + p)),
                  pl.BlockSpec((1, LANES), lambda p, b, i: (0, p)),
                  pl.BlockSpec(tri.shape, lambda p, b, i: (0, 0)), hbm] + [hbm] * n_w,
        out_specs=[pl.BlockSpec((SB_STEP, LANES), lambda p, b, i: (b * nq + i, N_PAIRS + p)), ospec, ospec]
        + [hbm] * n_w,
        out_shape=[jax.ShapeDtypeStruct(mixed.shape, mixed.dtype), jax.ShapeDtypeStruct((t, GROUP), F32),
                   jax.ShapeDtypeStruct((t, GROUP), F32)]
        + [jax.ShapeDtypeStruct((N_DEV,) + s.shape, s.dtype) for s in shards],
        scratch_shapes=[pltpu.SemaphoreType.DMA((n_w * _GatherPlan.COPIES,)),
                        pltpu.SemaphoreType.DMA((n_w * _GatherPlan.COPIES,)), pltpu.SemaphoreType.DMA((n_w,))],
        input_output_aliases={5: 0},
        compiler_params=_params(3),
    )(proj, proj, proj, norm_g, tri, mixed, *shards)
    return outs[0], outs[1], outs[2], list(outs[3:])


def _sb_bwd(proj, norm_g, dmix, opre, ctot, grads, n_seq, seq):
    t = n_seq * seq
    nq = seq // SB_STEP
    q0, k0, v0 = 0, N_PAIRS, 2 * N_PAIRS
    n_w = len(grads)
    n_steps = N_PAIRS * n_seq * nq
    tri = _triangle(lambda row, col: row <= col)

    def body(q_ref, k_ref, v_ref, ng_ref, tri_ref, dm_ref, opre_ref, ctot_ref, *rest):
        grad_refs = rest[:n_w]
        dq_ref, dk_ref, dv_ref, dng_ref = rest[n_w:n_w + 4]
        lands = rest[n_w + 4:2 * n_w + 4]
        dk_acc, dv_acc, send_sems, recv_sems, local_sems = rest[2 * n_w + 4:]
        p_id, b_id, i = pl.program_id(0), pl.program_id(1), pl.program_id(2)
        step = (p_id * n_seq + b_id) * nq + i
        plan = _ScatterPlan(grad_refs, lands, send_sems, recv_sems, local_sems)

        @pl.when(step == 0)
        def _():
            plan.start()

        masks = _head_masks()
        upto = tri_ref[...]

        def prefix(x):
            return jnp.dot(x.astype(BF16), upto, preferred_element_type=F32)

        @pl.when(i == 0)
        def _():
            dk_acc[...] = jnp.zeros_like(dk_acc)
            dv_acc[...] = jnp.zeros_like(dv_acc)

        @pl.when(jnp.logical_and(b_id == 0, i == 0))
        def _():
            dng_ref[...] = jnp.zeros_like(dng_ref)

        o = opre_ref[...]
        rs = _head_rstd(o, masks)
        oh = o * rs
        dm = dm_ref[...]
        dng_ref[...] += jnp.sum(dm * oh, axis=0, keepdims=True)
        doh = dm * ng_ref[...]
        do = rs * (doh - oh * _head_mean(doh * oh, masks))

        heads = range(len(masks))
        qs = [q_ref[blk, :] * SB_SCALE for blk in QUERY_BLOCKS]
        dos = [do[blk] for blk in QUERY_BLOCKS]
        qhs = [[(q * m).astype(MXU_DTYPE) for m in masks] for q in qs]
        doms = [[(d * m).astype(MXU_DTYPE) for m in masks] for d in dos]
        head_rows = [jnp.where(jnp.right_shift(lax.broadcasted_iota(jnp.int32, (LANES, 1), 0), 6) == h, 1.0, 0.0)
                     for h in heads]
        qhts = [[(qt * hr).astype(MXU_DTYPE) for hr in head_rows] for qt in [q.astype(F32).T for q in qs]]
        domts = [[(dt * hr).astype(MXU_DTYPE) for hr in head_rows] for dt in [d.T for d in dos]]
        totals = [[ctot_ref[blk, h * D_HEAD:h * D_HEAD + 1] for h in heads] for blk in QUERY_BLOCKS]

        def tiles(js, work, carry):
            rows = [pl.ds(pl.multiple_of(j * SB_BLOCK, SB_BLOCK), SB_BLOCK) for j in js]
            ks = [k_ref[rw, :].astype(MXU_DTYPE) for rw in rows]
            vs = [v_ref[rw, :].astype(MXU_DTYPE) for rw in rows]
            zs = [[_dot_nt(qhs[qb][h], ks[ts]) for h in heads] for qb, ts, _ in work]
            das = [[_dot_nt(doms[qb][h], vs[ts]) for h in heads] for qb, ts, _ in work]
            def gates(on, zq):
                sp = _softplus(zq)
                return _keep(on, sp), zq - sp, _keep(on, jnp.exp(zq - sp))

            trio = [[_lower_quadrants(gates, 3, z) if diag else gates(False, z) for z in zw]
                    for zw, (_, _, diag) in zip(zs, work)]
            ccs, lsigs, sigs = ([[t[o] for t in tw] for tw in trio] for o in range(3))
            pres = [[prefix(cc) for cc in cw] for cw in ccs]
            out = [list(per_block) for per_block in carry]
            for h in heads:
                for w, (qb, ts, diag) in enumerate(work):
                    pc, pdl, dq_h = out[qb][h]
                    logit = lsigs[w][h] + pres[w][h] - (totals[qb][h] - pc)
                    if diag:
                        a = _lower_quadrants(lambda on, lq: (_keep(on, jnp.exp(lq)),), 1, logit)[0]
                    else:
                        a = jnp.exp(logit)
                    dl = a * das[w][h]
                    dv_acc[js[ts]] += _dot(domts[qb][h], a)
                    dpre = prefix(dl)
                    dz = dl - sigs[w][h] * (pdl + dpre)
                    dzb = dz.astype(MXU_DTYPE)
                    dk_acc[js[ts]] += _dot(qhts[qb][h], dzb)
                    out[qb][h] = (pc + pres[w][h][:, SB_BLOCK - 1:SB_BLOCK], pdl + dpre[:, SB_BLOCK - 1:SB_BLOCK],
                                  dq_h + _dot(dzb, ks[ts]))
            return tuple(tuple(per_block) for per_block in out)

        zero = jnp.zeros((SB_BLOCK, 1), F32)
        start = ((zero, zero, jnp.zeros((SB_BLOCK, LANES), F32)),) * 2
        both = [(0, 0, False), (0, 1, False), (1, 0, False), (1, 1, False)]
        carry = lax.fori_loop(0, i, lambda s, cy: tiles([2 * s, 2 * s + 1], both, cy), (start, start))
        carry = tiles([2 * i, 2 * i + 1], [(0, 0, True), (1, 0, False), (1, 1, True)], carry)
        dq = jnp.concatenate([cb[0][2] * masks[0] + cb[1][2] * masks[1] for cb in carry], axis=0)
        dq_ref[...] = (dq * SB_SCALE).astype(dq_ref.dtype)

        @pl.when(i == nq - 1)
        def _():
            for j in range(seq // SB_BLOCK):
                tile_rows = slice(j * SB_BLOCK, (j + 1) * SB_BLOCK)
                dk_ref[tile_rows, :] = dk_acc[j].T.astype(dk_ref.dtype)
                dv_ref[tile_rows, :] = dv_acc[j].T.astype(dv_ref.dtype)

        @pl.when(step == n_steps - 1)
        def _():
            plan.finish()

    qspec = pl.BlockSpec((SB_STEP, LANES), lambda p, b, i: (b * nq + i, q0 + p))
    ospec = pl.BlockSpec((SB_STEP, LANES), lambda p, b, i: (b * nq + i, p))
    dmspec = pl.BlockSpec((SB_STEP, LANES), lambda p, b, i: (b * nq + i, N_PAIRS + p))
    full = lambda k: pl.BlockSpec((seq, LANES), lambda p, b, i: (b, k + p))
    vec = pl.BlockSpec((1, LANES), lambda p, b, i: (0, p))
    hbm = pl.BlockSpec(memory_space=pltpu.HBM)
    piece = jax.ShapeDtypeStruct((t, GROUP), MXU_DTYPE)
    outs = pl.pallas_call(
        body, name="sb_bwd", grid=(N_PAIRS, n_seq, nq),
        in_specs=[qspec, full(k0), full(v0), vec, pl.BlockSpec(tri.shape, lambda p, b, i: (0, 0)), dmspec, ospec, ospec]
        + [hbm] * n_w,
        out_specs=[ospec, full(0), full(0), vec] + [hbm] * n_w,
        out_shape=[piece, piece, piece, jax.ShapeDtypeStruct((1, GROUP), F32)]
        + [jax.ShapeDtypeStruct(g.shape, g.dtype) for g in grads],
        scratch_shapes=[pltpu.VMEM((seq // SB_BLOCK, LANES, SB_BLOCK), F32),
                        pltpu.VMEM((seq // SB_BLOCK, LANES, SB_BLOCK), F32),
                        pltpu.SemaphoreType.DMA((n_w * (N_DEV - 1),)), pltpu.SemaphoreType.DMA((n_w * (N_DEV - 1),)),
                        pltpu.SemaphoreType.DMA((n_w,))],
        compiler_params=_params(3),
    )(proj, proj, proj, norm_g, tri, dmix, opre, ctot, *grads)
    return outs[0], outs[1], outs[2], outs[3], list(outs[4:])


def _mesh_place():
    x, y, c = lax.axis_index("x"), lax.axis_index("y"), lax.axis_index("c")
    return x, y, c


def _peer(x, y, c, k):
    px = lax.rem(x + ((k >> 2) & 1), 2)
    py = lax.rem(y + ((k >> 1) & 1), 2)
    pc = lax.rem(c + (k & 1), 2)
    return (px, py, pc), 4 * px + 2 * py + pc


def _remote(src, dst, send_sem, recv_sem, to):
    return pltpu.make_async_remote_copy(src_ref=src, dst_ref=dst, send_sem=send_sem, recv_sem=recv_sem,
                                        device_id=to, device_id_type=pl.DeviceIdType.MESH)


class _GatherPlan:
    COPIES = 7

    def __init__(self, shards, gathered, send_sems, recv_sems, local_sems):
        x, y, c = _mesh_place()
        self.c = c
        self.me = (x, y, c)
        self.sibling = (x, y, 1 - c)
        self.chips = [(1 - x, y), (x, 1 - y), (1 - x, 1 - y)]
        self.tensors = list(zip(shards, gathered))
        self.send_sems, self.recv_sems, self.local_sems = send_sems, recv_sems, local_sems

    @staticmethod
    def _index(place):
        return 4 * place[0] + 2 * place[1] + place[2]

    def _copy(self, w, k, block, to, own=False):
        shard, gathered = self.tensors[w]
        slot = gathered.at[self._index(block)]
        n = w * self.COPIES + k
        return _remote(shard if own else slot, slot, self.send_sems.at[n], self.recv_sems.at[n], to)

    def _local(self, w):
        shard, gathered = self.tensors[w]
        return pltpu.make_async_copy(shard, gathered.at[self._index(self.me)], self.local_sems.at[w])

    def _first(self, w):
        return [self._copy(w, 0, self.me, self.sibling, own=True)] + [
            self._copy(w, 1 + j, self.me, (*chip, self.c), own=True) for j, chip in enumerate(self.chips)]

    def _passed(self, w):
        return [self._copy(w, 4 + j, (*chip, self.c), self.sibling) for j, chip in enumerate(self.chips)]

    def start(self):
        for w in range(len(self.tensors)):
            self._local(w).start()
            for cp in self._first(w):
                cp.start()

    def forward(self):
        for w in range(len(self.tensors)):
            passed = self._passed(w)
            for j, chip in enumerate(self.chips):
                self._copy(w, 1 + j, (*chip, self.c), self.me).wait_recv()
                passed[j].start()

    def finish(self):
        for w in range(len(self.tensors)):
            self._copy(w, 0, self.sibling, self.me).wait_recv()
            for j, chip in enumerate(self.chips):
                self._copy(w, 4 + j, (*chip, 1 - self.c), self.me).wait_recv()
            for cp in self._first(w) + self._passed(w):
                cp.wait_send()
            self._local(w).wait()


class _ScatterPlan:
    def __init__(self, grads, lands, send_sems, recv_sems, local_sems):
        self.place = _mesh_place()
        x, y, c = self.place
        self.me = 4 * x + 2 * y + c
        self.tensors = list(zip(grads, lands))
        self.send_sems, self.recv_sems, self.local_sems = send_sems, recv_sems, local_sems

    def _copies(self, w):
        grad, land = self.tensors[w]
        out = []
        for k in range(1, N_DEV):
            peer, pidx = _peer(*self.place, k)
            n = w * (N_DEV - 1) + k - 1
            sems = (self.send_sems.at[n], self.recv_sems.at[n], peer)
            out.append((_remote(grad.at[pidx], land.at[self.me], *sems), _remote(grad.at[pidx], land.at[pidx], *sems)))
        return out

    def _local(self, w):
        grad, land = self.tensors[w]
        return pltpu.make_async_copy(grad.at[self.me], land.at[self.me], self.local_sems.at[w])

    def start(self):
        for w in range(len(self.tensors)):
            self._local(w).start()
            for send, _ in self._copies(w):
                send.start()

    def finish(self):
        for w in range(len(self.tensors)):
            copies = self._copies(w)
            for _, arrival in copies:
                arrival.wait_recv()
            for send, _ in copies:
                send.wait_send()
            self._local(w).wait()


def _cast_shards(shards):
    def body(*refs):
        n = len(refs) // 2
        for src, dst in zip(refs[:n], refs[n:]):
            dst[...] = src[...].astype(dst.dtype)

    shapes = [jax.ShapeDtypeStruct(s.shape, BF16) for s in shards]
    return pl.pallas_call(
        body, name="cast_shards", grid=(1,),
        in_specs=[_whole(s) for s in shards], out_specs=[_whole(s) for s in shapes], out_shape=shapes,
        compiler_params=_params(1),
    )(*shards)


def _gather_w_in(shard, x, g, chunk=512):
    rows, cols = shard.shape
    t, d = x.shape
    chunk = min(chunk, t)
    n_chunks = t // chunk

    def body(w_ref, x_ref, g_ref, out_ref, h_ref, x_buf, send_sems, recv_sems, local_sems, x_sems):
        plan = _GatherPlan([w_ref], [out_ref], send_sems, recv_sems, local_sems)
        plan.start()

        def fetch(j):
            return pltpu.make_async_copy(x_ref.at[pl.ds(j * chunk, chunk)], x_buf.at[j % 2], x_sems.at[j % 2])

        fetch(0).start()
        gv = g_ref[...]
        for j in range(n_chunks):
            if j + 1 < n_chunks:
                fetch(j + 1).start()
            fetch(j).wait()
            xv = x_buf[j % 2]
            r = lax.rsqrt(jnp.mean(xv * xv, axis=-1, keepdims=True) + EPS)
            h_ref[j * chunk:(j + 1) * chunk, :] = (xv * r * gv).astype(h_ref.dtype)
        plan.forward()
        plan.finish()

    vmem = pl.BlockSpec(memory_space=pltpu.VMEM)
    hbm = pl.BlockSpec(memory_space=pltpu.HBM)
    return pl.pallas_call(
        body, name="gather_w_in", in_specs=[hbm, hbm, vmem], out_specs=[hbm, vmem],
        out_shape=[jax.ShapeDtypeStruct((N_DEV, rows, cols), shard.dtype), jax.ShapeDtypeStruct((t, d), MXU_DTYPE)],
        scratch_shapes=[pltpu.VMEM((2, chunk, d), F32), pltpu.SemaphoreType.DMA((_GatherPlan.COPIES,)),
                        pltpu.SemaphoreType.DMA((_GatherPlan.COPIES,)), pltpu.SemaphoreType.DMA((1,)),
                        pltpu.SemaphoreType.DMA((2,))],
        compiler_params=pltpu.CompilerParams(vmem_limit_bytes=VMEM_LIMIT),
    )(shard, x, g)


def _dw_rows(pieces, b, name, tn=512, tt=1024):
    t, n = b.shape
    widths = [p.shape[1] for p in pieces]
    rows = sum(widths)
    tt = min(tt, t)
    steps = t // tt
    n_p = len(pieces)

    def body(*refs):
        piece_refs, b_ref, o_ref, acc = refs[:n_p], refs[n_p], refs[n_p + 1], refs[n_p + 2]
        s = pl.program_id(1)

        @pl.when(s == 0)
        def _():
            acc[...] = jnp.zeros_like(acc)

        bv = b_ref[...]
        off = 0
        for p_ref, width in zip(piece_refs, widths):
            acc[off:off + width, :] += _dot_tn(p_ref[...], bv)
            off += width

        @pl.when(s == steps - 1)
        def _():
            o_ref[...] = acc[...].astype(o_ref.dtype)

    return pl.pallas_call(
        body, name=name, grid=(n // tn, steps),
        in_specs=[pl.BlockSpec((tt, width), lambda j, s: (s, 0)) for width in widths]
        + [pl.BlockSpec((tt, tn), lambda j, s: (s, j))],
        out_specs=pl.BlockSpec((rows, tn), lambda j, s: (0, j)),
        out_shape=jax.ShapeDtypeStruct((rows, n), BF16),
        scratch_shapes=[pltpu.VMEM((rows, tn), F32)],
        compiler_params=_params(2),
    )(*pieces, b)


def _dh_norm_bwd(pairs, x, g, res, name, tm=512, after=(), next_w=None):
    m, d = x.shape
    tm = min(tm, m)
    n_p = len(pairs)
    n_steps = m // tm
    n_in = 2 * n_p + 3 + len(after) + (next_w is not None)

    def body(*refs):
        x_ref, g_ref, res_ref = refs[2 * n_p:2 * n_p + 3]
        dx_ref, dg_ref = refs[n_in:n_in + 2]

        @pl.when(pl.program_id(0) == 0)
        def _():
            dg_ref[...] = jnp.zeros_like(dg_ref)

        dh = None
        for q in range(n_p):
            part = _dot(refs[2 * q][...], refs[2 * q + 1][...])
            dh = part if dh is None else dh + part
        xv = x_ref[...]
        r = lax.rsqrt(jnp.mean(xv * xv, axis=-1, keepdims=True) + EPS)
        xh = xv * r
        dxh = dh * g_ref[...]
        dxv = res_ref[...] + r * (dxh - xh * jnp.mean(dxh * xh, axis=-1, keepdims=True))
        dx_ref[...] = dxv
        dg_ref[...] += jnp.sum(dh * xh, axis=0, keepdims=True)
        if next_w is not None:
            refs[n_in + 2][...] = _dot_nt(dxv, refs[n_in - 1][...])

    in_specs, args = [], []
    for a, w, r in pairs:
        k = a.shape[1]
        in_specs += [pl.BlockSpec((tm, k), lambda i: (i, 0)),
                     pl.BlockSpec((k, d), functools.partial(lambda i, r: (r, 0), r=r or 0),
                                  pipeline_mode=pl.Buffered(1))]
        args += [a, w]
    row = pl.BlockSpec((tm, d), lambda i: (i, 0))
    vec = pl.BlockSpec((1, d), lambda i: (0, 0))
    extra_in, extra_out, extra_shape = [], [], []
    if next_w is not None:
        e = next_w.shape[0]
        extra_in = [pl.BlockSpec((e, d), lambda i: (0, 0), pipeline_mode=pl.Buffered(1))]
        extra_out = [pl.BlockSpec((tm, e), lambda i: (i, 0))]
        extra_shape = [jax.ShapeDtypeStruct((m, e), F32)]
        args_tail = (next_w,)
    else:
        args_tail = ()
    return pl.pallas_call(
        body, name=name, grid=(n_steps,),
        in_specs=in_specs + [row, vec, row] + [pl.BlockSpec(memory_space=pl.ANY)] * len(after) + extra_in,
        out_specs=[row, vec] + extra_out,
        out_shape=[jax.ShapeDtypeStruct((m, d), F32), jax.ShapeDtypeStruct((1, d), F32)] + extra_shape,
        compiler_params=_params(1),
    )(*args, x, g, res, *after, *args_tail)


def _exchange_start(grad, name):
    def body(g_ref, land_ref, send_sem, recv_sem, local_sem, g_thru, land_thru, token):
        del g_thru, land_thru
        x, y, c = _mesh_place()
        me = 4 * x + 2 * y + c
        pltpu.make_async_copy(g_ref.at[me], land_ref.at[me], local_sem).start()
        for k in range(1, N_DEV):
            peer, pidx = _peer(x, y, c, k)
            _remote(g_ref.at[pidx], land_ref.at[me], send_sem, recv_sem, peer).start()
        token[...] = jnp.zeros_like(token)

    hbm = pl.BlockSpec(memory_space=pltpu.HBM)
    sem = pl.BlockSpec(memory_space=pltpu.SEMAPHORE)
    buf = pltpu.HBM(grad.shape, grad.dtype)
    return pl.pallas_call(
        body, name=name,
        out_shape=(pltpu.SemaphoreType.DMA(()), pltpu.SemaphoreType.DMA(()), pltpu.SemaphoreType.DMA(()), buf, buf,
                   jax.ShapeDtypeStruct((8, LANES), F32)),
        in_specs=(hbm, hbm), out_specs=(sem, sem, sem, hbm, hbm, pl.BlockSpec(memory_space=pltpu.VMEM)),
        input_output_aliases={0: 3, 1: 4},
        compiler_params=pltpu.CompilerParams(has_side_effects=pltpu.SideEffectType.DATAFLOW_SIDE_EFFECTING),
    )(pltpu.with_memory_space_constraint(grad, pltpu.HBM),
      pltpu.with_memory_space_constraint(lax.empty(grad.shape, grad.dtype), pltpu.HBM))


def _exchange_wait(send_sem, recv_sem, local_sem, grad, land, after, name):
    def body(g_ref, land_ref, send_sem, recv_sem, local_sem, *rest):
        x, y, c = _mesh_place()
        peer, _ = _peer(x, y, c, 1)
        others = pl.ds(0, N_DEV - 1)
        seven = _remote(g_ref.at[others], land_ref.at[others], send_sem, recv_sem, peer)
        seven.wait_send()
        seven.wait_recv()
        pltpu.make_async_copy(g_ref.at[0], land_ref.at[0], local_sem).wait()

    hbm = pl.BlockSpec(memory_space=pltpu.HBM)
    sem = pl.BlockSpec(memory_space=pltpu.SEMAPHORE)
    buf = pltpu.HBM(grad.shape, grad.dtype)
    return pl.pallas_call(
        body, name=name, out_shape=(buf, buf),
        in_specs=(hbm, hbm, sem, sem, sem) + (pl.BlockSpec(memory_space=pl.ANY),) * len(after), out_specs=(hbm, hbm),
        input_output_aliases={0: 0, 1: 1},
        compiler_params=pltpu.CompilerParams(has_side_effects=pltpu.SideEffectType.DATAFLOW_SIDE_EFFECTING),
    )(grad, land, send_sem, recv_sem, local_sem, *after)[1]


SMALL_LAYOUT = ((0, 0, 0, 0, D_MODEL), (1, 0, 1, 0, GROUP), (1, 1, 1, GROUP, GROUP), (2, 0, 2, 0, GROUP),
                (3, 0, 2, GROUP, GROUP), (4, 0, 3, 0, D_MODEL), (5, 0, 4, 0, D_MODEL))
LOSS_ROW = 5
N_SMALL = 6


def _small_step(grads, loss_part, ws, ms, vs):
    def body(*refs):
        g_in, loss_in = refs[:N_SMALL], refs[N_SMALL]
        params = [refs[1 + (q + 1) * N_SMALL:1 + (q + 2) * N_SMALL] for q in range(3)]
        o0 = 1 + 4 * N_SMALL
        outs = [refs[o0 + q * N_SMALL:o0 + (q + 1) * N_SMALL] for q in range(4)]
        loss_out = refs[o0 + 4 * N_SMALL]
        pack, land, wp, mp, vp, send_sems, recv_sems = refs[o0 + 4 * N_SMALL + 1:]

        def place(dst, srcs):
            dst[...] = jnp.zeros_like(dst)
            for p, sr, dr, dc, width in SMALL_LAYOUT:
                dst[dr:dr + 1, dc:dc + width] = srcs[p][sr:sr + 1, :]

        place(pack, g_in)
        pack[LOSS_ROW:LOSS_ROW + 1, 0:LANES] = loss_in[...]
        for dst, srcs in zip((wp, mp, vp), params):
            place(dst, srcs)

        x, y, c = _mesh_place()
        me = 4 * x + 2 * y + c
        land[me] = pack[...]
        sends = []
        for k in range(1, N_DEV):
            peer, _ = _peer(x, y, c, k)
            cp = _remote(pack, land.at[me], send_sems.at[k - 1], recv_sems.at[k - 1], peer)
            cp.start()
            sends.append(cp)
        for k in range(1, N_DEV):
            peer, pidx = _peer(x, y, c, k)
            _remote(pack, land.at[pidx], send_sems.at[k - 1], recv_sems.at[k - 1], peer).wait_recv()
        for cp in sends:
            cp.wait_send()

        g = land[0]
        for d in range(1, N_DEV):
            g = g + land[d]
        delta, nm, nv = _adam(wp[...], g, mp[...], vp[...])
        for val, out in zip((g, delta, nm, nv), outs):
            for p, sr, dr, dc, width in SMALL_LAYOUT:
                out[p][sr:sr + 1, :] = val[dr:dr + 1, dc:dc + width]
        loss_out[...] = g[LOSS_ROW:LOSS_ROW + 1, 0:LANES]

    args = [*grads, loss_part, *ws, *ms, *vs]
    shapes = [jax.ShapeDtypeStruct(w.shape, F32) for w in ws] * 4 + [jax.ShapeDtypeStruct((1, LANES), F32)]
    packed = pltpu.VMEM((SMALL_ROWS, D_MODEL), F32)
    outs = pl.pallas_call(
        body, name="small_step", grid=(1,),
        in_specs=[_whole(a) for a in args], out_specs=[_whole(s) for s in shapes], out_shape=shapes,
        scratch_shapes=[packed, pltpu.VMEM((N_DEV, SMALL_ROWS, D_MODEL), F32), packed, packed, packed,
                        pltpu.SemaphoreType.DMA((N_DEV - 1,)), pltpu.SemaphoreType.DMA((N_DEV - 1,))],
        compiler_params=_params(1),
    )(*args)
    return [outs[q * N_SMALL:(q + 1) * N_SMALL] for q in range(4)], outs[4 * N_SMALL]


def _adam(w, g, m, v):
    m = ADAM_B1 * m + (1.0 - ADAM_B1) * g
    v = ADAM_B2 * v + (1.0 - ADAM_B2) * (g * g)
    m_hat = m / (1.0 - ADAM_B1 ** ADAM_STEP)
    v_hat = v / (1.0 - ADAM_B2 ** ADAM_STEP)
    delta = -ADAM_LR * (m_hat / (jnp.sqrt(v_hat) + ADAM_EPS) + ADAM_WD * w)
    return delta, m, v


def _reduce_adamw(land, w, m, v, name, tr, after=()):
    _, rows, width = land.shape

    def body(land_ref, w_ref, m_ref, v_ref, *rest):
        g_ref, d_ref, nm_ref, nv_ref = rest[len(after):]
        g = land_ref[0].astype(F32)
        for d in range(1, N_DEV):
            g = g + land_ref[d].astype(F32)
        delta, nm, nv = _adam(w_ref[...], g, m_ref[...], v_ref[...])
        g_ref[...] = g
        d_ref[...] = delta
        nm_ref[...] = nm
        nv_ref[...] = nv

    row = pl.BlockSpec((tr, width), lambda i: (i, 0))
    out = jax.ShapeDtypeStruct((rows, width), F32)
    return pl.pallas_call(
        body, name=name, grid=(rows // tr,),
        in_specs=[pl.BlockSpec((N_DEV, tr, width), lambda i: (0, i, 0)), row, row, row]
        + [pl.BlockSpec(memory_space=pl.ANY)] * len(after),
        out_specs=[row, row, row, row], out_shape=[out, out, out, out],
        compiler_params=_params(1),
    )(land, w, m, v, *after)


def kernel(x, mix_norm_g, w_in, lower_bounds, hgrn_norm_g, sb_norm_g, w_out, ffn_norm_g, w_gate, w_up, w_down, final_norm_g, loss_target, m_mix_norm_g, m_w_in, m_lower_bounds, m_hgrn_norm_g, m_sb_norm_g, m_w_out, m_ffn_norm_g, m_w_gate, m_w_up, m_w_down, m_final_norm_g, v_mix_norm_g, v_w_in, v_lower_bounds, v_hgrn_norm_g, v_sb_norm_g, v_w_out, v_ffn_norm_g, v_w_gate, v_w_up, v_w_down, v_final_norm_g):
    n_seq, seq, d = x.shape
    t = n_seq * seq
    x2d = x.reshape(t, d)
    tgt = loss_target.reshape(t, d)
    final_g = final_norm_g.reshape(1, d)
    col_sharded = (True, False, True, True, False)

    def as_rows(ws):
        return [w[0].T if tr else w[0] for w, tr in zip(ws, col_sharded)]

    big_w = as_rows([w_in, w_out, w_gate, w_up, w_down])
    big_m = as_rows([m_w_in, m_w_out, m_w_gate, m_w_up, m_w_down])
    big_v = as_rows([v_w_in, v_w_out, v_w_gate, v_w_up, v_w_down])

    sh_in, sh_out, sh_gate, sh_up, sh_down = _cast_shards(big_w)
    wt_in, h1 = _gather_w_in(sh_in, x2d, mix_norm_g)
    wt_in = wt_in.reshape(IN_COLS, d)
    proj_h = _mm_nt(h1, wt_in, "proj_hgrn", rows=(0, 4 * GROUP), tm=1024, tk=4 * GROUP)
    proj_s = _mm_nt(h1, wt_in, "proj_sb", rows=(4 * GROUP, 3 * GROUP), out_dtype=MXU_DTYPE, tm=1024, tk=512)
    mixed, oa_pre, states = _hgrn_fwd(proj_h, lower_bounds, hgrn_norm_g, n_seq, seq)
    mixed, ob_pre, ctot, gathered = _sb_fwd(proj_s, sb_norm_g, mixed, [sh_out, sh_gate, sh_up, sh_down], n_seq, seq)
    wf_out = gathered[0].reshape(d, d)
    wt_gate = gathered[1].reshape(D_FF, d)
    wt_up = gathered[2].reshape(D_FF, d)
    wf_down = gathered[3].reshape(D_FF, d)
    x1, h2 = _mix_out_norm(mixed, wf_out, x2d, ffn_norm_g, "mix_out")
    gate, up, ff = _ffn_up(h2, wt_gate, wt_up, "ffn_up")
    dx2, dx2m, d_final_g, loss_part = _ffn_down_loss(ff, wf_down, x1, tgt, final_g, "ffn_down_loss")

    dgate, dup = _ffn_bwd_act(dx2m, wf_down, gate, up, "ffn_bwd_act")
    dw_down = _mm_tn(ff, dx2m, "dw_down", tk=1408, tn=1024).reshape(N_DEV, D_FF // N_DEV, d)
    dw_gate = _mm_tn(dgate, h2, "dw_gate", tk=1408, tn=1024).reshape(N_DEV, D_FF // N_DEV, d)
    dw_up = _mm_tn(dup, h2, "dw_up", tk=1408, tn=1024).reshape(N_DEV, D_FF // N_DEV, d)
    dx1, d_ffn_g, dmix = _dh_norm_bwd([(dgate, wt_gate, None), (dup, wt_up, None)], x1, ffn_norm_g, dx2, "dh_ffn",
                                      tm=256, next_w=wf_out)
    dw_out = _mm_tn(mixed, dx1, "dw_out", tk=1024, tn=1024).reshape(N_DEV, d // N_DEV, d)
    dsq, dsk, dsv, d_sb_g, lands = _sb_bwd(proj_s, sb_norm_g, dmix, ob_pre, ctot, [dw_out, dw_gate, dw_up, dw_down],
                                            n_seq, seq)
    dhq, dhf, dhi, dhg, d_lb, d_hgrn_g = _hgrn_bwd(proj_h, lower_bounds, hgrn_norm_g, dmix, oa_pre, states, n_seq,
                                                   seq)
    dproj = [dhq, dhf, dhi, dhg, dsq, dsk, dsv]
    dw_in = _dw_rows(dproj, h1, "dw_in").reshape(N_DEV, IN_COLS // N_DEV, d)
    send_sem, recv_sem, local_sem, dw_in, land_in, token = _exchange_start(dw_in, "dw_in_send")
    dx, d_mix_g = _dh_norm_bwd([(piece, wt_in, k) for k, piece in enumerate(dproj)], x2d, mix_norm_g, dx1, "dh_mix",
                               after=(token,))

    tiles = {"in": 224, "out": 128, "gate": 176, "up": 176, "down": 176}
    keys = list(tiles)
    rest = [_reduce_adamw(land, w, m, v, "adamw_" + key, tr=tiles[key], after=(token,))
            for key, land, w, m, v in zip(keys[1:], lands, big_w[1:], big_m[1:], big_v[1:])]
    small, loss_row = _small_step(
        [d_mix_g, d_lb, d_hgrn_g, d_sb_g, d_ffn_g, d_final_g], loss_part,
        [mix_norm_g, lower_bounds, hgrn_norm_g, sb_norm_g, ffn_norm_g, final_g],
        [m_mix_norm_g, m_lower_bounds, m_hgrn_norm_g, m_sb_norm_g, m_ffn_norm_g, m_final_norm_g.reshape(1, d)],
        [v_mix_norm_g, v_lower_bounds, v_hgrn_norm_g, v_sb_norm_g, v_ffn_norm_g, v_final_norm_g.reshape(1, d)])
    land_in = _exchange_wait(send_sem, recv_sem, local_sem, dw_in, land_in,
                             [dx, loss_row] + [res[0] for res in rest], "dw_in_await")
    big = [_reduce_adamw(land_in, big_w[0], big_m[0], big_v[0], "adamw_in", tr=tiles["in"])] + rest
    big = [[r.T if tr else r for r in res] for res, tr in zip(big, col_sharded)]

    outs = [loss_row[0, 0], dx.reshape(n_seq, seq, d)]
    for q in range(4):
        b_in, b_out, b_gate, b_up, b_down = [res[q][None] for res in big]
        s_mix, s_lb, s_hgrn, s_sb, s_ffn, s_final = small[q]
        outs += [s_mix, b_in, s_lb, s_hgrn, s_sb, b_out, s_ffn, b_gate, b_up, b_down, s_final.reshape(d)]
    return tuple(outs)
```

```python
import functools
import math

import jax
import jax.numpy as jnp
from jax import lax
from jax.experimental import pallas as pl
from jax.experimental.pallas import tpu as pltpu

F32 = jnp.float32
BF16 = jnp.bfloat16
MXU_DTYPE = BF16

EPS = 1e-6
D_MODEL = 1024
N_HEADS = 8
D_HEAD = 64
GROUP = N_HEADS * D_HEAD
IN_COLS = 7 * GROUP
D_FF = 2816
CHUNK = 64
LANES = 128
N_PAIRS = GROUP // LANES
SUPER = 256
SB_BLOCK = 256
N_DEV = 8

ADAM_LR = 0.001
ADAM_B1 = 0.9
ADAM_B2 = 0.999
ADAM_EPS = 1e-08
ADAM_WD = 0.01
ADAM_STEP = 10

SMALL_ROWS = 8
FF_TILE = D_FF // 2

VMEM_LIMIT = 48 * 1024 * 1024


def _params(n_axes, vmem=VMEM_LIMIT):
    return pltpu.CompilerParams(dimension_semantics=("arbitrary",) * n_axes, vmem_limit_bytes=vmem)


def _whole(a):
    return pl.BlockSpec(a.shape, functools.partial(lambda i, nd: (0,) * nd, nd=len(a.shape)))


def _dot(a, b):
    return jnp.dot(a.astype(MXU_DTYPE), b.astype(MXU_DTYPE), preferred_element_type=F32)


def _dot_nt(a, b):
    return lax.dot_general(a.astype(MXU_DTYPE), b.astype(MXU_DTYPE), (((1,), (1,)), ((), ())),
                           preferred_element_type=F32)


def _dot_tn(a, b):
    return lax.dot_general(a.astype(MXU_DTYPE), b.astype(MXU_DTYPE), (((0,), (0,)), ((), ())),
                           preferred_element_type=F32)


def _split(x, parts):
    out, r = [], x
    for _ in range(parts):
        h = r.astype(BF16)
        out.append(h)
        r = r - h.astype(F32)
    return out


def _rsum_left(u, x, parts):
    acc = None
    for h in _split(x, parts):
        d = jnp.dot(u, h, preferred_element_type=F32)
        acc = d if acc is None else acc + d
    return acc


def _ones_where(mask):
    return jnp.where(mask, 1.0, 0.0).astype(BF16)


def _sigmoid(x):
    return 1.0 / (1.0 + jnp.exp(-x))


def _softplus(x):
    return jnp.maximum(x, 0.0) + jnp.log(1.0 + jnp.exp(-jnp.abs(x)))


def _head_masks():
    lane = lax.broadcasted_iota(jnp.int32, (1, LANES), 1)
    return [jnp.where(lane < D_HEAD, 1.0, 0.0), jnp.where(lane >= D_HEAD, 1.0, 0.0)]


def _head_rstd(o, masks):
    sq = o * o
    r = None
    for m in masks:
        ms = jnp.sum(sq * m, axis=1, keepdims=True) * (1.0 / D_HEAD)
        t = lax.rsqrt(ms + EPS) * m
        r = t if r is None else r + t
    return r


def _head_mean(t, masks):
    out = None
    for m in masks:
        v = jnp.sum(t * m, axis=1, keepdims=True) * (1.0 / D_HEAD) * m
        out = v if out is None else out + v
    return out


def _mix_out_norm(a, w, res, g, name, tm=512):
    m, k = a.shape
    d = w.shape[1]
    tm = min(tm, m)

    def body(a_ref, w_ref, res_ref, g_ref, x_ref, h_ref):
        xv = res_ref[...] + _dot(a_ref[...], w_ref[...])
        x_ref[...] = xv
        r = lax.rsqrt(jnp.mean(xv * xv, axis=-1, keepdims=True) + EPS)
        h_ref[...] = (xv * r * g_ref[...]).astype(h_ref.dtype)

    row = pl.BlockSpec((tm, d), lambda i: (i, 0))
    return pl.pallas_call(
        body, name=name, grid=(m // tm,),
        in_specs=[pl.BlockSpec((tm, k), lambda i: (i, 0)), pl.BlockSpec((k, d), lambda i: (0, 0)), row,
                  pl.BlockSpec((1, d), lambda i: (0, 0))],
        out_specs=[row, row],
        out_shape=[jax.ShapeDtypeStruct((m, d), F32), jax.ShapeDtypeStruct((m, d), MXU_DTYPE)],
        compiler_params=_params(1),
    )(a, w, res, g)


def _ffn_down_loss(a, w, res, target, g, name, tm=512):
    m, k = a.shape
    d = w.shape[1]
    tm = min(tm, m)

    def body(a_ref, w_ref, res_ref, t_ref, g_ref, dx_ref, dxm_ref, dg_ref, loss_ref):
        xv = res_ref[...] + _dot(a_ref[...], w_ref[...])
        gv = g_ref[...]
        r = lax.rsqrt(jnp.mean(xv * xv, axis=-1, keepdims=True) + EPS)
        xh = xv * r
        e = xh * gv - t_ref[...]
        dy = e * (1.0 / d)
        dxh = dy * gv
        dxv = r * (dxh - xh * jnp.mean(dxh * xh, axis=-1, keepdims=True))
        dx_ref[...] = dxv
        dxm_ref[...] = dxv.astype(dxm_ref.dtype)

        @pl.when(pl.program_id(0) == 0)
        def _():
            dg_ref[...] = jnp.zeros_like(dg_ref)
            loss_ref[...] = jnp.zeros_like(loss_ref)

        dg_ref[...] += jnp.sum(dy * xh, axis=0, keepdims=True)
        part = 0.5 * jnp.sum(jnp.mean(e * e, axis=-1, keepdims=True), axis=0, keepdims=True)
        loss_ref[...] += jnp.broadcast_to(part, loss_ref.shape)

    row = pl.BlockSpec((tm, d), lambda i: (i, 0))
    vec = pl.BlockSpec((1, d), lambda i: (0, 0))
    return pl.pallas_call(
        body, name=name, grid=(m // tm,),
        in_specs=[pl.BlockSpec((tm, k), lambda i: (i, 0)), pl.BlockSpec((k, d), lambda i: (0, 0)), row, row, vec],
        out_specs=[row, row, vec, pl.BlockSpec((1, LANES), lambda i: (0, 0))],
        out_shape=[jax.ShapeDtypeStruct((m, d), F32), jax.ShapeDtypeStruct((m, d), MXU_DTYPE),
                   jax.ShapeDtypeStruct((1, d), F32), jax.ShapeDtypeStruct((1, LANES), F32)],
        compiler_params=_params(1),
    )(a, w, res, target, g)


def _mm_nt(a, w, name, rows=None, out_dtype=F32, tm=512, tk=512):
    m, n = a.shape
    row0, k = rows or (0, w.shape[0])
    tm, tk = min(tm, m), min(tk, k)
    first = row0 // tk

    def body(a_ref, w_ref, o_ref):
        o_ref[...] = _dot_nt(a_ref[...], w_ref[...]).astype(o_ref.dtype)

    return pl.pallas_call(
        body, name=name, grid=(k // tk, m // tm),
        in_specs=[pl.BlockSpec((tm, n), lambda j, i: (i, 0)), pl.BlockSpec((tk, n), lambda j, i: (first + j, 0))],
        out_specs=pl.BlockSpec((tm, tk), lambda j, i: (i, j)),
        out_shape=jax.ShapeDtypeStruct((m, k), out_dtype),
        compiler_params=_params(2),
    )(a, w)


def _mm_tn(a, b, name, tk, tn, tt=1024, out_dtype=BF16):
    t, k = a.shape
    n = b.shape[1]
    tt = min(tt, t)
    steps = t // tt

    def body(a_ref, b_ref, o_ref, acc):
        s = pl.program_id(2)

        @pl.when(s == 0)
        def _():
            acc[...] = jnp.zeros_like(acc)

        acc[...] += _dot_tn(a_ref[...], b_ref[...])

        @pl.when(s == steps - 1)
        def _():
            o_ref[...] = acc[...].astype(o_ref.dtype)

    return pl.pallas_call(
        body, name=name, grid=(k // tk, n // tn, steps),
        in_specs=[pl.BlockSpec((tt, tk), lambda i, j, s: (s, i)), pl.BlockSpec((tt, tn), lambda i, j, s: (s, j))],
        out_specs=pl.BlockSpec((tk, tn), lambda i, j, s: (i, j)),
        out_shape=jax.ShapeDtypeStruct((k, n), out_dtype),
        scratch_shapes=[pltpu.VMEM((tk, tn), F32)],
        compiler_params=_params(3),
    )(a, b)


def _ffn_up(h, wg_t, wu_t, name, tm=1024, tn=FF_TILE):
    m, k = h.shape
    n = wg_t.shape[0]
    tm = min(tm, m)

    def body(h_ref, wg_ref, wu_ref, gate_ref, up_ref, ff_ref):
        hv = h_ref[...]
        gate = _dot_nt(hv, wg_ref[...])
        up = _dot_nt(hv, wu_ref[...])
        gate_ref[...] = gate.astype(gate_ref.dtype)
        up_ref[...] = up.astype(up_ref.dtype)
        ff_ref[...] = (gate * _sigmoid(gate) * up).astype(ff_ref.dtype)

    wspec = pl.BlockSpec((tn, k), lambda j, i: (j, 0))
    ospec = pl.BlockSpec((tm, tn), lambda j, i: (i, j))
    return pl.pallas_call(
        body, name=name, grid=(n // tn, m // tm),
        in_specs=[pl.BlockSpec((tm, k), lambda j, i: (i, 0)), wspec, wspec],
        out_specs=[ospec, ospec, ospec],
        out_shape=[jax.ShapeDtypeStruct((m, n), MXU_DTYPE)] * 3,
        compiler_params=_params(2),
    )(h, wg_t, wu_t)


def _ffn_bwd_act(dx, wd, gate, up, name, tm=1024, tn=FF_TILE):
    m, k = dx.shape
    n = wd.shape[0]
    tm = min(tm, m)

    def body(dx_ref, wd_ref, gate_ref, up_ref, dgate_ref, dup_ref):
        dff = _dot_nt(dx_ref[...], wd_ref[...])
        gate = gate_ref[...].astype(F32)
        sg = _sigmoid(gate)
        dgate_ref[...] = (dff * up_ref[...].astype(F32) * sg * (1.0 + gate * (1.0 - sg))).astype(dgate_ref.dtype)
        dup_ref[...] = (dff * gate * sg).astype(dup_ref.dtype)

    ospec = pl.BlockSpec((tm, tn), lambda j, i: (i, j))
    return pl.pallas_call(
        body, name=name, grid=(n // tn, m // tm),
        in_specs=[pl.BlockSpec((tm, k), lambda j, i: (i, 0)), pl.BlockSpec((tn, k), lambda j, i: (j, 0)),
                  ospec, ospec],
        out_specs=[ospec, ospec],
        out_shape=[jax.ShapeDtypeStruct((m, n), MXU_DTYPE), jax.ShapeDtypeStruct((m, n), MXU_DTYPE)],
        compiler_params=_params(2),
    )(dx, wd, gate, up)


def _chunk_masks():
    r = lax.broadcasted_iota(jnp.int32, (SUPER, SUPER), 0)
    c = lax.broadcasted_iota(jnp.int32, (SUPER, SUPER), 1)
    same = jnp.right_shift(r, 6) == jnp.right_shift(c, 6)
    lower = jnp.logical_and(same, c <= r)
    upper = jnp.logical_and(same, c >= r)
    return same, lower, upper


def _head_block_mask():
    r = lax.broadcasted_iota(jnp.int32, (LANES, LANES), 0)
    c = lax.broadcasted_iota(jnp.int32, (LANES, LANES), 1)
    return jnp.where(jnp.right_shift(r, 6) == jnp.right_shift(c, 6), 1.0, 0.0)


def _lower_bound(lb_raw):
    return 1.0 / (1.0 + jnp.exp(lb_raw[1:2, :] - lb_raw[0:1, :]))


HGRN_UNROLL = 4
PER_SUPER = SUPER // CHUNK
CHUNK_ROWS = [slice(c * CHUNK, (c + 1) * CHUNK) for c in range(PER_SUPER)]


def _over_chunks(rows):
    return jnp.concatenate([jnp.broadcast_to(r, (CHUNK, LANES)) for r in rows], axis=0)


def _hgrn_gates(q, hf, lb, lower_b):
    sig = _sigmoid(hf)
    f = lb + (1.0 - lb) * sig
    k = 1.0 - f
    lf = jnp.log(f)
    b = _rsum_left(lower_b, lf, 2)
    ends = [b[cr.stop - 1:cr.stop, :] for cr in CHUNK_ROWS]
    eb = jnp.exp(b)
    enb = jnp.exp(-b)
    edb = jnp.exp(_over_chunks(ends) - b)
    decs = [jnp.exp(e) for e in ends]
    return sig, f, k, decs, eb, enb, edb, q * eb, k * enb, k * edb


def _hgrn_fwd(proj, lower_bounds, norm_g, n_seq, seq):
    t = n_seq * seq
    n_super = seq // SUPER
    n_chunks = seq // CHUNK

    def body(q_ref, f_ref, i_ref, g_ref, lb_ref, ng_ref, out_ref, opre_ref, st_ref):
        masks = _head_masks()
        _, lower, _ = _chunk_masks()
        lower_b = _ones_where(lower)
        bd = _head_block_mask()
        lb = _lower_bound(lb_ref[...])
        ng = ng_ref[...]

        def step(it, st):
            blocks = [HGRN_UNROLL * it + u for u in range(HGRN_UNROLL)]
            rows = [pl.ds(pl.multiple_of(sb * SUPER, SUPER), SUPER) for sb in blocks]
            vs = [i_ref[rw, :] for rw in rows]
            gates = [_hgrn_gates(q_ref[rw, :], f_ref[rw, :], lb, lower_b) for rw in rows]
            decs, qes, kes, kds = ([g[k] for g in gates] for k in (3, 7, 8, 9))
            scores = [[_dot_nt(qe * m, ke) for m in masks] for qe, ke in zip(qes, kes)]
            updates = [[_dot_tn(v[cr], kd[cr]) for cr in CHUNK_ROWS] for v, kd in zip(vs, kds)]
            states = [st]
            for dec_b, upd_b in zip(decs, updates):
                for dec, upd in zip(dec_b, upd_b):
                    states.append(states[-1] * dec + bd * upd)
            for u, sb in enumerate(blocks):
                for c in range(PER_SUPER):
                    st_ref[0, 0, sb * PER_SUPER + c] = states[u * PER_SUPER + c]
            intra = [[_dot(jnp.where(lower, p, 0.0), v) for p in sc] for sc, v in zip(scores, vs)]
            inter = [[_dot_nt(qe[cr], states[u * PER_SUPER + c]) for c, cr in enumerate(CHUNK_ROWS)]
                     for u, qe in enumerate(qes)]
            for rw, intra_b, inter_b in zip(rows, intra, inter):
                o = intra_b[0] * masks[0] + intra_b[1] * masks[1] + jnp.concatenate(inter_b, axis=0)
                opre_ref[rw, :] = o
                hg = g_ref[rw, :]
                on = o * _head_rstd(o, masks) * ng
                out_ref[rw, :] = (on * hg * _sigmoid(hg)).astype(out_ref.dtype)
            return states[-1]

        lax.fori_loop(0, n_super // HGRN_UNROLL, step, jnp.zeros((LANES, LANES), F32))

    def col(k):
        return pl.BlockSpec((seq, LANES), lambda p, b: (b, k * N_PAIRS + p))

    vec = lambda rows: pl.BlockSpec((rows, LANES), lambda p, b: (0, p))
    ospec = pl.BlockSpec((seq, LANES), lambda p, b: (b, p))
    return pl.pallas_call(
        body, name="hgrn_fwd", grid=(N_PAIRS, n_seq),
        in_specs=[col(0), col(1), col(2), col(3), vec(2), vec(1)],
        out_specs=[ospec, ospec,
                   pl.BlockSpec((1, 1, n_chunks, LANES, LANES), lambda p, b: (b, p, 0, 0, 0))],
        out_shape=[jax.ShapeDtypeStruct((t, 2 * GROUP), MXU_DTYPE), jax.ShapeDtypeStruct((t, GROUP), F32),
                   jax.ShapeDtypeStruct((n_seq, N_PAIRS, n_chunks, LANES, LANES), F32)],
        compiler_params=_params(2),
    )(proj, proj, proj, proj, lower_bounds, norm_g)


def _hgrn_bwd(proj, lower_bounds, norm_g, dmix, opre, states, n_seq, seq):
    t = n_seq * seq
    n_super = seq // SUPER
    n_chunks = seq // CHUNK
    per = SUPER // CHUNK

    def body(q_ref, f_ref, i_ref, g_ref, lb_ref, ng_ref, dm_ref, opre_ref, st_ref,
             dq_ref, df_ref, di_ref, dg_ref, dlb_ref, dng_ref):
        masks = _head_masks()
        _, lower, upper = _chunk_masks()
        lower_b, upper_b = _ones_where(lower), _ones_where(upper)
        bd = _head_block_mask()
        lb_raw = lb_ref[...]
        lb = _lower_bound(lb_raw)
        ng = ng_ref[...]

        @pl.when(pl.program_id(1) == 0)
        def _():
            dlb_ref[...] = jnp.zeros_like(dlb_ref)
            dng_ref[...] = jnp.zeros_like(dng_ref)

        def first_half(sb):
            rows = pl.ds(pl.multiple_of(sb * SUPER, SUPER), SUPER)
            q, hf, v, hg = q_ref[rows, :], f_ref[rows, :], i_ref[rows, :], g_ref[rows, :]
            sig, f, k, decs, eb, enb, edb, qe, ke, kd = _hgrn_gates(q, hf, lb, lower_b)
            o = opre_ref[rows, :]
            r = _head_rstd(o, masks)
            oh = o * r
            dm = dm_ref[rows, :]
            sg = _sigmoid(hg)
            dg_ref[rows, :] = (dm * oh * ng * sg * (1.0 + hg * (1.0 - sg))).astype(dg_ref.dtype)
            don = dm * hg * sg
            dng_ref[...] += jnp.sum(don * oh, axis=0, keepdims=True)
            doh = don * ng
            do = r * (doh - oh * _head_mean(doh * oh, masks))
            doms = [do * m for m in masks]
            qems = [qe * m for m in masks]
            scores = [_dot_nt(qem, ke) for qem in qems]
            dscores = [_dot_nt(dom, v) for dom in doms]
            prevs = [st_ref[0, 0, sb * per + c] for c in range(per)]
            dst_in = [_dot_tn(do[cr], qe[cr]) for cr in CHUNK_ROWS]
            dqe_i = [_dot(do[cr], prev) for cr, prev in zip(CHUNK_ROWS, prevs)]
            return dict(rows=rows, v=v, sig=sig, f=f, decs=decs, eb=eb, enb=enb, edb=edb, qe=qe, ke=ke, kd=kd,
                        doms=doms, qems=qems, scores=scores, dscores=dscores, prevs=prevs, dst_in=dst_in, dqe_i=dqe_i)

        def second_half(blk, dsts):
            v, qe, ke, kd = blk["v"], blk["qe"], blk["ke"], blk["kd"]
            ps = [jnp.where(lower, p, 0.0) for p in blk["scores"]]
            dps = [jnp.where(lower, dp, 0.0) for dp in blk["dscores"]]
            dqe_h = [_dot(dp, ke) for dp in dps]
            dke_h = [_dot_tn(dp, qem) for dp, qem in zip(dps, blk["qems"])]
            dv_h = [_dot_tn(p, dom) for p, dom in zip(ps, blk["doms"])]
            dus = [bd * d for d in dsts]
            dv_i = [_dot_nt(kd[cr], du) for cr, du in zip(CHUNK_ROWS, dus)]
            dkd_i = [_dot(v[cr], du) for cr, du in zip(CHUNK_ROWS, dus)]

            def finish():
                dqe = dqe_h[0] * masks[0] + dqe_h[1] * masks[1] + jnp.concatenate(blk["dqe_i"], axis=0)
                dke = dke_h[0] + dke_h[1]
                dv = dv_h[0] + dv_h[1] + jnp.concatenate(dv_i, axis=0)
                dkd = jnp.concatenate(dkd_i, axis=0)
                dk = dke * blk["enb"] + dkd * blk["edb"]
                db = dqe * qe - dke * ke - dkd * kd
                dkd_kd = dkd * kd
                dends = [jnp.sum(dkd_kd[cr], axis=0, keepdims=True)
                         + jnp.sum(dsts[c] * blk["prevs"][c], axis=0, keepdims=True) * blk["decs"][c]
                         for c, cr in enumerate(CHUNK_ROWS)]
                dlf = _rsum_left(upper_b, db, 2) + _over_chunks(dends)
                sig = blk["sig"]
                dfv = dlf / blk["f"] - dk
                rows = blk["rows"]
                dq_ref[rows, :] = (dqe * blk["eb"]).astype(dq_ref.dtype)
                di_ref[rows, :] = dv.astype(di_ref.dtype)
                df_ref[rows, :] = (dfv * (1.0 - lb) * sig * (1.0 - sig)).astype(df_ref.dtype)
                dlb = jnp.sum(dfv * (1.0 - sig), axis=0, keepdims=True)
                da0 = dlb * lb * (1.0 - lb)
                dlb_ref[0:1, :] += da0
                dlb_ref[1:2, :] -= da0

            return finish

        def step(it, dst):
            blocks = [first_half(n_super - 1 - HGRN_UNROLL * it - u) for u in range(HGRN_UNROLL)]
            all_dsts = []
            for blk in blocks:
                dsts = [None] * per
                for c in reversed(range(per)):
                    dsts[c] = dst
                    dst = bd * (dst * blk["decs"][c] + blk["dst_in"][c])
                all_dsts.append(dsts)
            for finish in [second_half(blk, dsts) for blk, dsts in zip(blocks, all_dsts)]:
                finish()
            return dst

        lax.fori_loop(0, n_super // HGRN_UNROLL, step, jnp.zeros((LANES, LANES), F32))

    def col(k):
        return pl.BlockSpec((seq, LANES), lambda p, b: (b, k * N_PAIRS + p))

    vec = lambda rows: pl.BlockSpec((rows, LANES), lambda p, b: (0, p))
    ospec = pl.BlockSpec((seq, LANES), lambda p, b: (b, p))
    piece = jax.ShapeDtypeStruct((t, GROUP), MXU_DTYPE)
    return pl.pallas_call(
        body, name="hgrn_bwd", grid=(N_PAIRS, n_seq),
        in_specs=[col(0), col(1), col(2), col(3), vec(2), vec(1), ospec, ospec,
                  pl.BlockSpec((1, 1, n_chunks, LANES, LANES), lambda p, b: (b, p, 0, 0, 0))],
        out_specs=[ospec, ospec, ospec, ospec, vec(2), vec(1)],
        out_shape=[piece, piece, piece, piece,
                   jax.ShapeDtypeStruct((2, GROUP), F32), jax.ShapeDtypeStruct((1, GROUP), F32)],
        compiler_params=_params(2),
    )(proj, proj, proj, proj, lower_bounds, norm_g, dmix, opre, states)


SB_SCALE = 1.0 / math.sqrt(D_HEAD)


SB_STEP = 2 * SB_BLOCK
QUERY_BLOCKS = (slice(0, SB_BLOCK), slice(SB_BLOCK, SB_STEP))


def _triangle(keep):
    row = lax.broadcasted_iota(jnp.int32, (SB_BLOCK, SB_BLOCK), 0)
    col = lax.broadcasted_iota(jnp.int32, (SB_BLOCK, SB_BLOCK), 1)
    return _ones_where(keep(row, col))


SB_HALF = SB_BLOCK // 2


def _keep(on_diagonal, x):
    if not on_diagonal:
        return x
    row = lax.broadcasted_iota(jnp.int32, (SB_HALF, SB_HALF), 0)
    col = lax.broadcasted_iota(jnp.int32, (SB_HALF, SB_HALF), 1)
    return jnp.where(col < row, x, 0.0)


def _lower_quadrants(fn, n_out, *tiles):
    def quadrant(r, c):
        return [t[r * SB_HALF:(r + 1) * SB_HALF, c * SB_HALF:(c + 1) * SB_HALF] for t in tiles]

    top, left, bottom = fn(True, *quadrant(0, 0)), fn(False, *quadrant(1, 0)), fn(True, *quadrant(1, 1))
    zero = jnp.zeros((SB_HALF, SB_HALF), F32)
    return [jnp.concatenate([jnp.concatenate([top[o], zero], axis=1), jnp.concatenate([left[o], bottom[o]], axis=1)],
                            axis=0) for o in range(n_out)]


def _sb_fwd(proj, norm_g, mixed, shards, n_seq, seq):
    t = n_seq * seq
    nq = seq // SB_STEP
    q0, k0, v0 = 0, N_PAIRS, 2 * N_PAIRS
    n_w = len(shards)
    n_steps = N_PAIRS * n_seq * nq
    tri = _triangle(lambda row, col: row >= col)

    def body(q_ref, k_ref, v_ref, ng_ref, tri_ref, mixed_in, *rest):
        del mixed_in
        shard_refs = rest[:n_w]
        out_ref, opre_ref, ctot_ref = rest[n_w:n_w + 3]
        gathered = rest[n_w + 3:2 * n_w + 3]
        send_sems, recv_sems, local_sems = rest[2 * n_w + 3:]
        i = pl.program_id(2)
        step = (pl.program_id(0) * n_seq + pl.program_id(1)) * nq + i
        plan = _GatherPlan(shard_refs, gathered, send_sems, recv_sems, local_sems)

        @pl.when(step == 0)
        def _():
            plan.start()

        @pl.when(step == (3 * n_steps) // 4)
        def _():
            plan.forward()

        masks = _head_masks()
        suffix = tri_ref[...]
        qhs = [[(q_ref[blk, :] * SB_SCALE * m).astype(MXU_DTYPE) for m in masks] for blk in QUERY_BLOCKS]

        def tiles(js, work, carry):
            rows = [pl.ds(pl.multiple_of(j * SB_BLOCK, SB_BLOCK), SB_BLOCK) for j in js]
            ks = [k_ref[rw, :].astype(MXU_DTYPE) for rw in rows]
            vs = [v_ref[rw, :].astype(MXU_DTYPE) for rw in rows]
            zs = [[_dot_nt(qh, ks[ts]) for qh in qhs[qb]] for qb, ts, _ in work]
            ccs = [[_lower_quadrants(lambda on, zq: (_keep(on, _softplus(zq)),), 1, z)[0] if diag else _softplus(z)
                    for z in zw] for zw, (_, _, diag) in zip(zs, work)]
            sums = [[jnp.dot(cc.astype(BF16), suffix, preferred_element_type=F32) for cc in cw] for cw in ccs]
            out = [list(per_block) for per_block in carry]
            for h in range(len(masks)):
                for w, (qb, ts, diag) in enumerate(work):
                    run, acc = out[qb][h]
                    logit = zs[w][h] - (sums[w][h] + run)
                    if diag:
                        a = _lower_quadrants(lambda on, lq: (_keep(on, jnp.exp(lq)),), 1, logit)[0]
                    else:
                        a = jnp.exp(logit)
                    out[qb][h] = (run + sums[w][h][:, 0:1], acc + _dot(a, vs[ts]))
            return tuple(tuple(per_block) for per_block in out)

        start = ((jnp.zeros((SB_BLOCK, 1), F32), jnp.zeros((SB_BLOCK, LANES), F32)),) * 2
        carry = tiles([2 * i, 2 * i + 1], [(0, 0, True), (1, 1, True), (1, 0, False)], (start, start))
        both = [(0, 0, False), (0, 1, False), (1, 0, False), (1, 1, False)]
        carry = lax.fori_loop(0, i, lambda s, cy: tiles([2 * (i - s) - 1, 2 * (i - s) - 2], both, cy), carry)
        opre = jnp.concatenate([cb[0][1] * masks[0] + cb[1][1] * masks[1] for cb in carry], axis=0)
        ctot = jnp.concatenate([cb[0][0] * masks[0] + cb[1][0] * masks[1] for cb in carry], axis=0)
        opre_ref[...] = opre
        ctot_ref[...] = ctot
        out_ref[...] = (opre * _head_rstd(opre, masks) * ng_ref[...]).astype(out_ref.dtype)

        @pl.when(step == n_steps - 1)
        def _():
            plan.finish()

    qspec = pl.BlockSpec((SB_STEP, LANES), lambda p, b, i: (b * nq + i, q0 + p))
    ospec = pl.BlockSpec((SB_STEP, LANES), lambda p, b, i: (b * nq + i, p))
    hbm = pl.BlockSpec(memory_space=pltpu.HBM)
    outs = pl.pallas_call(
        body, name="sb_fwd", grid=(N_PAIRS, n_seq, nq),
        in_specs=[qspec,
                  pl.BlockSpec((seq, LANES), lambda p, b, i: (b, k0 + p)),
                  pl.BlockSpec((seq, LANES), lambda p, b, i: (b, v0 + p)),
                  pl.BlockSpec((1, LANES), lambda p, b, i: (0, p)),
                  pl.BlockSpec(tri.shape, lambda p, b, i: (0, 0)), hbm] + [hbm] * n_w,
        out_specs=[pl.BlockSpec((SB_STEP, LANES), lambda p, b, i: (b * nq + i, N_PAIRS + p)), ospec, ospec]
        + [hbm] * n_w,
        out_shape=[jax.ShapeDtypeStruct(mixed.shape, mixed.dtype), jax.ShapeDtypeStruct((t, GROUP), F32),
                   jax.ShapeDtypeStruct((t, GROUP), F32)]
        + [jax.ShapeDtypeStruct((N_DEV,) + s.shape, s.dtype) for s in shards],
        scratch_shapes=[pltpu.SemaphoreType.DMA((n_w * _GatherPlan.COPIES,)),
                        pltpu.SemaphoreType.DMA((n_w * _GatherPlan.COPIES,)), pltpu.SemaphoreType.DMA((n_w,))],
        input_output_aliases={5: 0},
        compiler_params=_params(3),
    )(proj, proj, proj, norm_g, tri, mixed, *shards)
    return outs[0], outs[1], outs[2], list(outs[3:])


def _sb_bwd(proj, norm_g, dmix, opre, ctot, grads, n_seq, seq):
    t = n_seq * seq
    nq = seq // SB_STEP
    q0, k0, v0 = 0, N_PAIRS, 2 * N_PAIRS
    n_w = len(grads)
    n_steps = N_PAIRS * n_seq * nq
    tri = _triangle(lambda row, col: row <= col)

    def body(q_ref, k_ref, v_ref, ng_ref, tri_ref, dm_ref, opre_ref, ctot_ref, *rest):
        grad_refs = rest[:n_w]
        dq_ref, dk_ref, dv_ref, dng_ref = rest[n_w:n_w + 4]
        lands = rest[n_w + 4:2 * n_w + 4]
        dk_acc, dv_acc, send_sems, recv_sems, local_sems = rest[2 * n_w + 4:]
        p_id, b_id, i = pl.program_id(0), pl.program_id(1), pl.program_id(2)
        step = (p_id * n_seq + b_id) * nq + i
        plan = _ScatterPlan(grad_refs, lands, send_sems, recv_sems, local_sems)

        @pl.when(step == 0)
        def _():
            plan.start()

        masks = _head_masks()
        upto = tri_ref[...]

        def prefix(x):
            return jnp.dot(x.astype(BF16), upto, preferred_element_type=F32)

        @pl.when(i == 0)
        def _():
            dk_acc[...] = jnp.zeros_like(dk_acc)
            dv_acc[...] = jnp.zeros_like(dv_acc)

        @pl.when(jnp.logical_and(b_id == 0, i == 0))
        def _():
            dng_ref[...] = jnp.zeros_like(dng_ref)

        o = opre_ref[...]
        rs = _head_rstd(o, masks)
        oh = o * rs
        dm = dm_ref[...]
        dng_ref[...] += jnp.sum(dm * oh, axis=0, keepdims=True)
        doh = dm * ng_ref[...]
        do = rs * (doh - oh * _head_mean(doh * oh, masks))

        heads = range(len(masks))
        qs = [q_ref[blk, :] * SB_SCALE for blk in QUERY_BLOCKS]
        dos = [do[blk] for blk in QUERY_BLOCKS]
        qhs = [[(q * m).astype(MXU_DTYPE) for m in masks] for q in qs]
        doms = [[(d * m).astype(MXU_DTYPE) for m in masks] for d in dos]
        head_rows = [jnp.where(jnp.right_shift(lax.broadcasted_iota(jnp.int32, (LANES, 1), 0), 6) == h, 1.0, 0.0)
                     for h in heads]
        qhts = [[(qt * hr).astype(MXU_DTYPE) for hr in head_rows] for qt in [q.astype(F32).T for q in qs]]
        domts = [[(dt * hr).astype(MXU_DTYPE) for hr in head_rows] for dt in [d.T for d in dos]]
        totals = [[ctot_ref[blk, h * D_HEAD:h * D_HEAD + 1] for h in heads] for blk in QUERY_BLOCKS]

        def tiles(js, work, carry):
            rows = [pl.ds(pl.multiple_of(j * SB_BLOCK, SB_BLOCK), SB_BLOCK) for j in js]
            ks = [k_ref[rw, :].astype(MXU_DTYPE) for rw in rows]
            vs = [v_ref[rw, :].astype(MXU_DTYPE) for rw in rows]
            zs = [[_dot_nt(qhs[qb][h], ks[ts]) for h in heads] for qb, ts, _ in work]
            das = [[_dot_nt(doms[qb][h], vs[ts]) for h in heads] for qb, ts, _ in work]
            def gates(on, zq):
                sp = _softplus(zq)
                return _keep(on, sp), zq - sp, _keep(on, jnp.exp(zq - sp))

            trio = [[_lower_quadrants(gates, 3, z) if diag else gates(False, z) for z in zw]
                    for zw, (_, _, diag) in zip(zs, work)]
            ccs, lsigs, sigs = ([[t[o] for t in tw] for tw in trio] for o in range(3))
            pres = [[prefix(cc) for cc in cw] for cw in ccs]
            out = [list(per_block) for per_block in carry]
            for h in heads:
                for w, (qb, ts, diag) in enumerate(work):
                    pc, pdl, dq_h = out[qb][h]
                    logit = lsigs[w][h] + pres[w][h] - (totals[qb][h] - pc)
                    if diag:
                        a = _lower_quadrants(lambda on, lq: (_keep(on, jnp.exp(lq)),), 1, logit)[0]
                    else:
                        a = jnp.exp(logit)
                    dl = a * das[w][h]
                    dv_acc[js[ts]] += _dot(domts[qb][h], a)
                    dpre = prefix(dl)
                    dz = dl - sigs[w][h] * (pdl + dpre)
                    dzb = dz.astype(MXU_DTYPE)
                    dk_acc[js[ts]] += _dot(qhts[qb][h], dzb)
                    out[qb][h] = (pc + pres[w][h][:, SB_BLOCK - 1:SB_BLOCK], pdl + dpre[:, SB_BLOCK - 1:SB_BLOCK],
                                  dq_h + _dot(dzb, ks[ts]))
            return tuple(tuple(per_block) for per_block in out)

        zero = jnp.zeros((SB_BLOCK, 1), F32)
        start = ((zero, zero, jnp.zeros((SB_BLOCK, LANES), F32)),) * 2
        both = [(0, 0, False), (0, 1, False), (1, 0, False), (1, 1, False)]
        carry = lax.fori_loop(0, i, lambda s, cy: tiles([2 * s, 2 * s + 1], both, cy), (start, start))
        carry = tiles([2 * i, 2 * i + 1], [(0, 0, True), (1, 0, False), (1, 1, True)], carry)
        dq = jnp.concatenate([cb[0][2] * masks[0] + cb[1][2] * masks[1] for cb in carry], axis=0)
        dq_ref[...] = (dq * SB_SCALE).astype(dq_ref.dtype)

        @pl.when(i == nq - 1)
        def _():
            for j in range(seq // SB_BLOCK):
                tile_rows = slice(j * SB_BLOCK, (j + 1) * SB_BLOCK)
                dk_ref[tile_rows, :] = dk_acc[j].T.astype(dk_ref.dtype)
                dv_ref[tile_rows, :] = dv_acc[j].T.astype(dv_ref.dtype)

        @pl.when(step == n_steps - 1)
        def _():
            plan.finish()

    qspec = pl.BlockSpec((SB_STEP, LANES), lambda p, b, i: (b * nq + i, q0 + p))
    ospec = pl.BlockSpec((SB_STEP, LANES), lambda p, b, i: (b * nq + i, p))
    dmspec = pl.BlockSpec((SB_STEP, LANES), lambda p, b, i: (b * nq + i, N_PAIRS + p))
    full = lambda k: pl.BlockSpec((seq, LANES), lambda p, b, i: (b, k + p))
    vec = pl.BlockSpec((1, LANES), lambda p, b, i: (0, p))
    hbm = pl.BlockSpec(memory_space=pltpu.HBM)
    piece = jax.ShapeDtypeStruct((t, GROUP), MXU_DTYPE)
    outs = pl.pallas_call(
        body, name="sb_bwd", grid=(N_PAIRS, n_seq, nq),
        in_specs=[qspec, full(k0), full(v0), vec, pl.BlockSpec(tri.shape, lambda p, b, i: (0, 0)), dmspec, ospec, ospec]
        + [hbm] * n_w,
        out_specs=[ospec, full(0), full(0), vec] + [hbm] * n_w,
        out_shape=[piece, piece, piece, jax.ShapeDtypeStruct((1, GROUP), F32)]
        + [jax.ShapeDtypeStruct(g.shape, g.dtype) for g in grads],
        scratch_shapes=[pltpu.VMEM((seq // SB_BLOCK, LANES, SB_BLOCK), F32),
                        pltpu.VMEM((seq // SB_BLOCK, LANES, SB_BLOCK), F32),
                        pltpu.SemaphoreType.DMA((n_w * (N_DEV - 1),)), pltpu.SemaphoreType.DMA((n_w * (N_DEV - 1),)),
                        pltpu.SemaphoreType.DMA((n_w,))],
        compiler_params=_params(3),
    )(proj, proj, proj, norm_g, tri, dmix, opre, ctot, *[pltpu.with_memory_space_constraint(g, pltpu.HBM) for g in grads])
    return outs[0], outs[1], outs[2], outs[3], list(outs[4:])


def _mesh_place():
    x, y, c = lax.axis_index("x"), lax.axis_index("y"), lax.axis_index("c")
    return x, y, c


def _peer(x, y, c, k):
    px = lax.rem(x + ((k >> 2) & 1), 2)
    py = lax.rem(y + ((k >> 1) & 1), 2)
    pc = lax.rem(c + (k & 1), 2)
    return (px, py, pc), 4 * px + 2 * py + pc


def _remote(src, dst, send_sem, recv_sem, to):
    return pltpu.make_async_remote_copy(src_ref=src, dst_ref=dst, send_sem=send_sem, recv_sem=recv_sem,
                                        device_id=to, device_id_type=pl.DeviceIdType.MESH)


class _GatherPlan:
    COPIES = 7

    def __init__(self, shards, gathered, send_sems, recv_sems, local_sems):
        x, y, c = _mesh_place()
        self.c = c
        self.me = (x, y, c)
        self.sibling = (x, y, 1 - c)
        self.chips = [(1 - x, y), (x, 1 - y), (1 - x, 1 - y)]
        self.tensors = list(zip(shards, gathered))
        self.send_sems, self.recv_sems, self.local_sems = send_sems, recv_sems, local_sems

    @staticmethod
    def _index(place):
        return 4 * place[0] + 2 * place[1] + place[2]

    def _copy(self, w, k, block, to, own=False):
        shard, gathered = self.tensors[w]
        slot = gathered.at[self._index(block)]
        n = w * self.COPIES + k
        return _remote(shard if own else slot, slot, self.send_sems.at[n], self.recv_sems.at[n], to)

    def _local(self, w):
        shard, gathered = self.tensors[w]
        return pltpu.make_async_copy(shard, gathered.at[self._index(self.me)], self.local_sems.at[w])

    def _first(self, w):
        return [self._copy(w, 0, self.me, self.sibling, own=True)] + [
            self._copy(w, 1 + j, self.me, (*chip, self.c), own=True) for j, chip in enumerate(self.chips)]

    def _passed(self, w):
        return [self._copy(w, 4 + j, (*chip, self.c), self.sibling) for j, chip in enumerate(self.chips)]

    def start(self):
        for w in range(len(self.tensors)):
            self._local(w).start()
            for cp in self._first(w):
                cp.start()

    def forward(self):
        for w in range(len(self.tensors)):
            passed = self._passed(w)
            for j, chip in enumerate(self.chips):
                self._copy(w, 1 + j, (*chip, self.c), self.me).wait_recv()
                passed[j].start()

    def finish(self):
        for w in range(len(self.tensors)):
            self._copy(w, 0, self.sibling, self.me).wait_recv()
            for j, chip in enumerate(self.chips):
                self._copy(w, 4 + j, (*chip, 1 - self.c), self.me).wait_recv()
            for cp in self._first(w) + self._passed(w):
                cp.wait_send()
            self._local(w).wait()


class _ScatterPlan:
    def __init__(self, grads, lands, send_sems, recv_sems, local_sems):
        self.place = _mesh_place()
        x, y, c = self.place
        self.me = 4 * x + 2 * y + c
        self.tensors = list(zip(grads, lands))
        self.send_sems, self.recv_sems, self.local_sems = send_sems, recv_sems, local_sems

    def _copies(self, w):
        grad, land = self.tensors[w]
        out = []
        for k in range(1, N_DEV):
            peer, pidx = _peer(*self.place, k)
            n = w * (N_DEV - 1) + k - 1
            sems = (self.send_sems.at[n], self.recv_sems.at[n], peer)
            out.append((_remote(grad.at[pidx], land.at[self.me], *sems), _remote(grad.at[pidx], land.at[pidx], *sems)))
        return out

    def _local(self, w):
        grad, land = self.tensors[w]
        return pltpu.make_async_copy(grad.at[self.me], land.at[self.me], self.local_sems.at[w])

    def start(self):
        for w in range(len(self.tensors)):
            self._local(w).start()
            for send, _ in self._copies(w):
                send.start()

    def finish(self):
        for w in range(len(self.tensors)):
            copies = self._copies(w)
            for _, arrival in copies:
                arrival.wait_recv()
            for send, _ in copies:
                send.wait_send()
            self._local(w).wait()


def _cast_shards(shards):
    def body(*refs):
        n = len(refs) // 2
        for src, dst in zip(refs[:n], refs[n:]):
            dst[...] = src[...].astype(dst.dtype)

    shapes = [jax.ShapeDtypeStruct(s.shape, BF16) for s in shards]
    return pl.pallas_call(
        body, name="cast_shards", grid=(1,),
        in_specs=[_whole(s) for s in shards], out_specs=[_whole(s) for s in shapes], out_shape=shapes,
        compiler_params=_params(1),
    )(*shards)


def _gather_w_in(shard, x, g, chunk=512):
    rows, cols = shard.shape
    t, d = x.shape
    chunk = min(chunk, t)
    n_chunks = t // chunk

    def body(w_ref, x_ref, g_ref, out_ref, h_ref, x_buf, send_sems, recv_sems, local_sems, x_sems):
        plan = _GatherPlan([w_ref], [out_ref], send_sems, recv_sems, local_sems)
        plan.start()

        def fetch(j):
            return pltpu.make_async_copy(x_ref.at[pl.ds(j * chunk, chunk)], x_buf.at[j % 2], x_sems.at[j % 2])

        fetch(0).start()
        gv = g_ref[...]
        for j in range(n_chunks):
            if j + 1 < n_chunks:
                fetch(j + 1).start()
            fetch(j).wait()
            xv = x_buf[j % 2]
            r = lax.rsqrt(jnp.mean(xv * xv, axis=-1, keepdims=True) + EPS)
            h_ref[j * chunk:(j + 1) * chunk, :] = (xv * r * gv).astype(h_ref.dtype)
        plan.forward()
        plan.finish()

    vmem = pl.BlockSpec(memory_space=pltpu.VMEM)
    hbm = pl.BlockSpec(memory_space=pltpu.HBM)
    return pl.pallas_call(
        body, name="gather_w_in", in_specs=[hbm, hbm, vmem], out_specs=[hbm, vmem],
        out_shape=[jax.ShapeDtypeStruct((N_DEV, rows, cols), shard.dtype), jax.ShapeDtypeStruct((t, d), MXU_DTYPE)],
        scratch_shapes=[pltpu.VMEM((2, chunk, d), F32), pltpu.SemaphoreType.DMA((_GatherPlan.COPIES,)),
                        pltpu.SemaphoreType.DMA((_GatherPlan.COPIES,)), pltpu.SemaphoreType.DMA((1,)),
                        pltpu.SemaphoreType.DMA((2,))],
        compiler_params=pltpu.CompilerParams(vmem_limit_bytes=VMEM_LIMIT),
    )(shard, x, g)


def _dw_rows(pieces, b, name, tn=512, tt=1024):
    t, n = b.shape
    widths = [p.shape[1] for p in pieces]
    rows = sum(widths)
    tt = min(tt, t)
    steps = t // tt
    n_p = len(pieces)

    def body(*refs):
        piece_refs, b_ref, o_ref, acc = refs[:n_p], refs[n_p], refs[n_p + 1], refs[n_p + 2]
        s = pl.program_id(1)

        @pl.when(s == 0)
        def _():
            acc[...] = jnp.zeros_like(acc)

        bv = b_ref[...]
        off = 0
        for p_ref, width in zip(piece_refs, widths):
            acc[off:off + width, :] += _dot_tn(p_ref[...], bv)
            off += width

        @pl.when(s == steps - 1)
        def _():
            o_ref[...] = acc[...].astype(o_ref.dtype)

    return pl.pallas_call(
        body, name=name, grid=(n // tn, steps),
        in_specs=[pl.BlockSpec((tt, width), lambda j, s: (s, 0)) for width in widths]
        + [pl.BlockSpec((tt, tn), lambda j, s: (s, j))],
        out_specs=pl.BlockSpec((rows, tn), lambda j, s: (0, j)),
        out_shape=jax.ShapeDtypeStruct((rows, n), BF16),
        scratch_shapes=[pltpu.VMEM((rows, tn), F32)],
        compiler_params=_params(2),
    )(*pieces, b)


def _dh_norm_bwd(pairs, x, g, res, name, tm=512, after=(), next_w=None):
    m, d = x.shape
    tm = min(tm, m)
    n_p = len(pairs)
    n_steps = m // tm
    n_in = 2 * n_p + 3 + len(after) + (next_w is not None)

    def body(*refs):
        x_ref, g_ref, res_ref = refs[2 * n_p:2 * n_p + 3]
        dx_ref, dg_ref = refs[n_in:n_in + 2]

        @pl.when(pl.program_id(0) == 0)
        def _():
            dg_ref[...] = jnp.zeros_like(dg_ref)

        dh = None
        for q in range(n_p):
            part = _dot(refs[2 * q][...], refs[2 * q + 1][...])
            dh = part if dh is None else dh + part
        xv = x_ref[...]
        r = lax.rsqrt(jnp.mean(xv * xv, axis=-1, keepdims=True) + EPS)
        xh = xv * r
        dxh = dh * g_ref[...]
        dxv = res_ref[...] + r * (dxh - xh * jnp.mean(dxh * xh, axis=-1, keepdims=True))
        dx_ref[...] = dxv
        dg_ref[...] += jnp.sum(dh * xh, axis=0, keepdims=True)
        if next_w is not None:
            refs[n_in + 2][...] = _dot_nt(dxv, refs[n_in - 1][...])

    in_specs, args = [], []
    for a, w, r in pairs:
        k = a.shape[1]
        in_specs += [pl.BlockSpec((tm, k), lambda i: (i, 0)),
                     pl.BlockSpec((k, d), functools.partial(lambda i, r: (r, 0), r=r or 0),
                                  pipeline_mode=pl.Buffered(1))]
        args += [a, w]
    row = pl.BlockSpec((tm, d), lambda i: (i, 0))
    vec = pl.BlockSpec((1, d), lambda i: (0, 0))
    extra_in, extra_out, extra_shape = [], [], []
    if next_w is not None:
        e = next_w.shape[0]
        extra_in = [pl.BlockSpec((e, d), lambda i: (0, 0), pipeline_mode=pl.Buffered(1))]
        extra_out = [pl.BlockSpec((tm, e), lambda i: (i, 0))]
        extra_shape = [jax.ShapeDtypeStruct((m, e), F32)]
        args_tail = (next_w,)
    else:
        args_tail = ()
    return pl.pallas_call(
        body, name=name, grid=(n_steps,),
        in_specs=in_specs + [row, vec, row] + [pl.BlockSpec(memory_space=pl.ANY)] * len(after) + extra_in,
        out_specs=[row, vec] + extra_out,
        out_shape=[jax.ShapeDtypeStruct((m, d), F32), jax.ShapeDtypeStruct((1, d), F32)] + extra_shape,
        compiler_params=_params(1),
    )(*args, x, g, res, *after, *args_tail)


def _exchange_start(grad, name):
    def body(g_ref, land_ref, send_sem, recv_sem, local_sem, g_thru, land_thru, token):
        del g_thru, land_thru
        x, y, c = _mesh_place()
        me = 4 * x + 2 * y + c
        pltpu.make_async_copy(g_ref.at[me], land_ref.at[me], local_sem).start()
        for k in range(1, N_DEV):
            peer, pidx = _peer(x, y, c, k)
            _remote(g_ref.at[pidx], land_ref.at[me], send_sem, recv_sem, peer).start()
        token[...] = jnp.zeros_like(token)

    hbm = pl.BlockSpec(memory_space=pltpu.HBM)
    sem = pl.BlockSpec(memory_space=pltpu.SEMAPHORE)
    buf = pltpu.HBM(grad.shape, grad.dtype)
    return pl.pallas_call(
        body, name=name,
        out_shape=(pltpu.SemaphoreType.DMA(()), pltpu.SemaphoreType.DMA(()), pltpu.SemaphoreType.DMA(()), buf, buf,
                   jax.ShapeDtypeStruct((8, LANES), F32)),
        in_specs=(hbm, hbm), out_specs=(sem, sem, sem, hbm, hbm, pl.BlockSpec(memory_space=pltpu.VMEM)),
        input_output_aliases={0: 3, 1: 4},
        compiler_params=pltpu.CompilerParams(has_side_effects=pltpu.SideEffectType.DATAFLOW_SIDE_EFFECTING),
    )(pltpu.with_memory_space_constraint(grad, pltpu.HBM),
      pltpu.with_memory_space_constraint(lax.empty(grad.shape, grad.dtype), pltpu.HBM))


def _exchange_wait(send_sem, recv_sem, local_sem, grad, land, after, name):
    def body(g_ref, land_ref, send_sem, recv_sem, local_sem, *rest):
        x, y, c = _mesh_place()
        peer, _ = _peer(x, y, c, 1)
        others = pl.ds(0, N_DEV - 1)
        seven = _remote(g_ref.at[others], land_ref.at[others], send_sem, recv_sem, peer)
        seven.wait_send()
        seven.wait_recv()
        pltpu.make_async_copy(g_ref.at[0], land_ref.at[0], local_sem).wait()

    hbm = pl.BlockSpec(memory_space=pltpu.HBM)
    sem = pl.BlockSpec(memory_space=pltpu.SEMAPHORE)
    buf = pltpu.HBM(grad.shape, grad.dtype)
    return pl.pallas_call(
        body, name=name, out_shape=(buf, buf),
        in_specs=(hbm, hbm, sem, sem, sem) + (pl.BlockSpec(memory_space=pl.ANY),) * len(after), out_specs=(hbm, hbm),
        input_output_aliases={0: 0, 1: 1},
        compiler_params=pltpu.CompilerParams(has_side_effects=pltpu.SideEffectType.DATAFLOW_SIDE_EFFECTING),
    )(grad, land, send_sem, recv_sem, local_sem, *after)[1]


SMALL_LAYOUT = ((0, 0, 0, 0, D_MODEL), (1, 0, 1, 0, GROUP), (1, 1, 1, GROUP, GROUP), (2, 0, 2, 0, GROUP),
                (3, 0, 2, GROUP, GROUP), (4, 0, 3, 0, D_MODEL), (5, 0, 4, 0, D_MODEL))
LOSS_ROW = 5
N_SMALL = 6


def _place_small(dst, srcs):
    dst[...] = jnp.zeros_like(dst)
    for p, sr, dr, dc, width in SMALL_LAYOUT:
        dst[dr:dr + 1, dc:dc + width] = srcs[p][sr:sr + 1, :]


def _small_allreduce(grads, loss_part):
    def body(*refs):
        g_in, loss_in, sum_ref = refs[:N_SMALL], refs[N_SMALL], refs[N_SMALL + 1]
        pack, land, send_sems, recv_sems = refs[N_SMALL + 2:]
        _place_small(pack, g_in)
        pack[LOSS_ROW:LOSS_ROW + 1, 0:LANES] = loss_in[...]

        x, y, c = _mesh_place()
        me = 4 * x + 2 * y + c
        land[me] = pack[...]
        sends = []
        for k in range(1, N_DEV):
            peer, _ = _peer(x, y, c, k)
            cp = _remote(pack, land.at[me], send_sems.at[k - 1], recv_sems.at[k - 1], peer)
            cp.start()
            sends.append(cp)
        for k in range(1, N_DEV):
            peer, pidx = _peer(x, y, c, k)
            _remote(pack, land.at[pidx], send_sems.at[k - 1], recv_sems.at[k - 1], peer).wait_recv()
        for cp in sends:
            cp.wait_send()

        g = land[0]
        for d in range(1, N_DEV):
            g = g + land[d]
        sum_ref[...] = g

    args = [*grads, loss_part]
    packed = jax.ShapeDtypeStruct((SMALL_ROWS, D_MODEL), F32)
    return pl.pallas_call(
        body, name="small_allreduce", grid=(1,),
        in_specs=[_whole(a) for a in args], out_specs=_whole(packed), out_shape=packed,
        scratch_shapes=[pltpu.VMEM(packed.shape, F32), pltpu.VMEM((N_DEV,) + packed.shape, F32),
                        pltpu.SemaphoreType.DMA((N_DEV - 1,)), pltpu.SemaphoreType.DMA((N_DEV - 1,))],
        compiler_params=_params(1),
    )(*args)


def _small_adamw(gsum, ws, ms, vs):
    def body(*refs):
        g_ref = refs[0]
        params = [refs[1 + q * N_SMALL:1 + (q + 1) * N_SMALL] for q in range(3)]
        o0 = 1 + 3 * N_SMALL
        outs = [refs[o0 + q * N_SMALL:o0 + (q + 1) * N_SMALL] for q in range(4)]
        loss_out = refs[o0 + 4 * N_SMALL]
        wp, mp, vp = refs[o0 + 4 * N_SMALL + 1:]
        for dst, srcs in zip((wp, mp, vp), params):
            _place_small(dst, srcs)
        g = g_ref[...]
        delta, nm, nv = _adam(wp[...], g, mp[...], vp[...])
        for val, out in zip((g, delta, nm, nv), outs):
            for p, sr, dr, dc, width in SMALL_LAYOUT:
                out[p][sr:sr + 1, :] = val[dr:dr + 1, dc:dc + width]
        loss_out[...] = g[LOSS_ROW:LOSS_ROW + 1, 0:LANES]

    args = [gsum, *ws, *ms, *vs]
    shapes = [jax.ShapeDtypeStruct(w.shape, F32) for w in ws] * 4 + [jax.ShapeDtypeStruct((1, LANES), F32)]
    packed = pltpu.VMEM((SMALL_ROWS, D_MODEL), F32)
    outs = pl.pallas_call(
        body, name="small_adamw", grid=(1,),
        in_specs=[_whole(a) for a in args], out_specs=[_whole(s) for s in shapes], out_shape=shapes,
        scratch_shapes=[packed, packed, packed], compiler_params=_params(1),
    )(*args)
    return [outs[q * N_SMALL:(q + 1) * N_SMALL] for q in range(4)], outs[4 * N_SMALL]


def _adam(w, g, m, v):
    m = ADAM_B1 * m + (1.0 - ADAM_B1) * g
    v = ADAM_B2 * v + (1.0 - ADAM_B2) * (g * g)
    m_hat = m / (1.0 - ADAM_B1 ** ADAM_STEP)
    v_hat = v / (1.0 - ADAM_B2 ** ADAM_STEP)
    delta = -ADAM_LR * (m_hat / (jnp.sqrt(v_hat) + ADAM_EPS) + ADAM_WD * w)
    return delta, m, v


def _reduce_adamw(land, w, m, v, name, tr, after=()):
    _, rows, width = land.shape

    def body(land_ref, w_ref, m_ref, v_ref, *rest):
        g_ref, d_ref, nm_ref, nv_ref = rest[len(after):]
        g = land_ref[0].astype(F32)
        for d in range(1, N_DEV):
            g = g + land_ref[d].astype(F32)
        delta, nm, nv = _adam(w_ref[...], g, m_ref[...], v_ref[...])
        g_ref[...] = g
        d_ref[...] = delta
        nm_ref[...] = nm
        nv_ref[...] = nv

    row = pl.BlockSpec((tr, width), lambda i: (i, 0))
    out = jax.ShapeDtypeStruct((rows, width), F32)
    return pl.pallas_call(
        body, name=name, grid=(rows // tr,),
        in_specs=[pl.BlockSpec((N_DEV, tr, width), lambda i: (0, i, 0)), row, row, row]
        + [pl.BlockSpec(memory_space=pl.ANY)] * len(after),
        out_specs=[row, row, row, row], out_shape=[out, out, out, out],
        compiler_params=_params(1),
    )(land, w, m, v, *after)


def kernel(x, mix_norm_g, w_in, lower_bounds, hgrn_norm_g, sb_norm_g, w_out, ffn_norm_g, w_gate, w_up, w_down, final_norm_g, loss_target, m_mix_norm_g, m_w_in, m_lower_bounds, m_hgrn_norm_g, m_sb_norm_g, m_w_out, m_ffn_norm_g, m_w_gate, m_w_up, m_w_down, m_final_norm_g, v_mix_norm_g, v_w_in, v_lower_bounds, v_hgrn_norm_g, v_sb_norm_g, v_w_out, v_ffn_norm_g, v_w_gate, v_w_up, v_w_down, v_final_norm_g):
    n_seq, seq, d = x.shape
    t = n_seq * seq
    x2d = x.reshape(t, d)
    tgt = loss_target.reshape(t, d)
    final_g = final_norm_g.reshape(1, d)
    col_sharded = (True, False, True, True, False)

    def as_rows(ws):
        return [w[0].T if tr else w[0] for w, tr in zip(ws, col_sharded)]

    big_w = as_rows([w_in, w_out, w_gate, w_up, w_down])
    big_m = as_rows([m_w_in, m_w_out, m_w_gate, m_w_up, m_w_down])
    big_v = as_rows([v_w_in, v_w_out, v_w_gate, v_w_up, v_w_down])

    sh_in, sh_out, sh_gate, sh_up, sh_down = _cast_shards(big_w)
    wt_in, h1 = _gather_w_in(sh_in, x2d, mix_norm_g)
    wt_in = wt_in.reshape(IN_COLS, d)
    proj_h = _mm_nt(h1, wt_in, "proj_hgrn", rows=(0, 4 * GROUP), tm=1024, tk=4 * GROUP)
    proj_s = _mm_nt(h1, wt_in, "proj_sb", rows=(4 * GROUP, 3 * GROUP), out_dtype=MXU_DTYPE, tm=1024, tk=512)
    mixed, oa_pre, states = _hgrn_fwd(proj_h, lower_bounds, hgrn_norm_g, n_seq, seq)
    mixed, ob_pre, ctot, gathered = _sb_fwd(proj_s, sb_norm_g, mixed, [sh_out, sh_gate, sh_up, sh_down], n_seq, seq)
    wf_out = gathered[0].reshape(d, d)
    wt_gate = gathered[1].reshape(D_FF, d)
    wt_up = gathered[2].reshape(D_FF, d)
    wf_down = gathered[3].reshape(D_FF, d)
    x1, h2 = _mix_out_norm(mixed, wf_out, x2d, ffn_norm_g, "mix_out")
    gate, up, ff = _ffn_up(h2, wt_gate, wt_up, "ffn_up")
    dx2, dx2m, d_final_g, loss_part = _ffn_down_loss(ff, wf_down, x1, tgt, final_g, "ffn_down_loss")

    dgate, dup = _ffn_bwd_act(dx2m, wf_down, gate, up, "ffn_bwd_act")
    dw_down = _mm_tn(ff, dx2m, "dw_down", tk=1408, tn=1024).reshape(N_DEV, D_FF // N_DEV, d)
    dw_gate = _mm_tn(dgate, h2, "dw_gate", tk=1408, tn=1024).reshape(N_DEV, D_FF // N_DEV, d)
    dw_up = _mm_tn(dup, h2, "dw_up", tk=1408, tn=1024).reshape(N_DEV, D_FF // N_DEV, d)
    dx1, d_ffn_g, dmix = _dh_norm_bwd([(dgate, wt_gate, None), (dup, wt_up, None)], x1, ffn_norm_g, dx2, "dh_ffn",
                                      tm=256, next_w=wf_out)
    dw_out = _mm_tn(mixed, dx1, "dw_out", tk=1024, tn=1024).reshape(N_DEV, d // N_DEV, d)
    dsq, dsk, dsv, d_sb_g, lands = _sb_bwd(proj_s, sb_norm_g, dmix, ob_pre, ctot, [dw_out, dw_gate, dw_up, dw_down],
                                            n_seq, seq)
    dhq, dhf, dhi, dhg, d_lb, d_hgrn_g = _hgrn_bwd(proj_h, lower_bounds, hgrn_norm_g, dmix, oa_pre, states, n_seq,
                                                   seq)
    dproj = [dhq, dhf, dhi, dhg, dsq, dsk, dsv]
    dw_in = _dw_rows(dproj, h1, "dw_in").reshape(N_DEV, IN_COLS // N_DEV, d)
    send_sem, recv_sem, local_sem, dw_in, land_in, token = _exchange_start(dw_in, "dw_in_send")
    dx, d_mix_g = _dh_norm_bwd([(piece, wt_in, k) for k, piece in enumerate(dproj)], x2d, mix_norm_g, dx1, "dh_mix",
                               after=(token,))

    tiles = {"in": 224, "out": 128, "gate": 176, "up": 176, "down": 176}
    keys = list(tiles)
    rest = [_reduce_adamw(land, w, m, v, "adamw_" + key, tr=tiles[key], after=(token,))
            for key, land, w, m, v in zip(keys[1:], lands, big_w[1:], big_m[1:], big_v[1:])]
    small_sum = _small_allreduce([d_mix_g, d_lb, d_hgrn_g, d_sb_g, d_ffn_g, d_final_g], loss_part)
    small, loss_row = _small_adamw(
        small_sum, [mix_norm_g, lower_bounds, hgrn_norm_g, sb_norm_g, ffn_norm_g, final_g],
        [m_mix_norm_g, m_lower_bounds, m_hgrn_norm_g, m_sb_norm_g, m_ffn_norm_g, m_final_norm_g.reshape(1, d)],
        [v_mix_norm_g, v_lower_bounds, v_hgrn_norm_g, v_sb_norm_g, v_ffn_norm_g, v_final_norm_g.reshape(1, d)])
    land_in = _exchange_wait(send_sem, recv_sem, local_sem, dw_in, land_in,
                             [dx, loss_row] + [res[0] for res in rest], "dw_in_await")
    big = [_reduce_adamw(land_in, big_w[0], big_m[0], big_v[0], "adamw_in", tr=tiles["in"])] + rest
    big = [[r.T if tr else r for r in res] for res, tr in zip(big, col_sharded)]

    outs = [loss_row[0, 0], dx.reshape(n_seq, seq, d)]
    for q in range(4):
        b_in, b_out, b_gate, b_up, b_down = [res[q][None] for res in big]
        s_mix, s_lb, s_hgrn, s_sb, s_ffn, s_final = small[q]
        outs += [s_mix, b_in, s_lb, s_hgrn, s_sb, b_out, s_ffn, b_gate, b_up, b_down, s_final.reshape(d)]
    return tuple(outs)
```

```python
import functools
import math

import jax
import jax.numpy as jnp
from jax import lax
from jax.experimental import pallas as pl
from jax.experimental.pallas import tpu as pltpu

F32 = jnp.float32
BF16 = jnp.bfloat16
MXU_DTYPE = BF16

EPS = 1e-6
D_MODEL = 1024
N_HEADS = 8
D_HEAD = 64
GROUP = N_HEADS * D_HEAD
IN_COLS = 7 * GROUP
D_FF = 2816
CHUNK = 64
LANES = 128
N_PAIRS = GROUP // LANES
SUPER = 256
SB_BLOCK = 256
N_DEV = 8

ADAM_LR = 0.001
ADAM_B1 = 0.9
ADAM_B2 = 0.999
ADAM_EPS = 1e-08
ADAM_WD = 0.01
ADAM_STEP = 10

SMALL_ROWS = 8
FF_TILE = D_FF // 2

VMEM_LIMIT = 60 * 1024 * 1024


def _params(n_axes, vmem=VMEM_LIMIT):
    return pltpu.CompilerParams(dimension_semantics=("arbitrary",) * n_axes, vmem_limit_bytes=vmem)


def _whole(a):
    return pl.BlockSpec(a.shape, functools.partial(lambda i, nd: (0,) * nd, nd=len(a.shape)))


def _dot(a, b):
    return jnp.dot(a.astype(MXU_DTYPE), b.astype(MXU_DTYPE), preferred_element_type=F32)


def _dot_nt(a, b):
    return lax.dot_general(a.astype(MXU_DTYPE), b.astype(MXU_DTYPE), (((1,), (1,)), ((), ())),
                           preferred_element_type=F32)


def _dot_tn(a, b):
    return lax.dot_general(a.astype(MXU_DTYPE), b.astype(MXU_DTYPE), (((0,), (0,)), ((), ())),
                           preferred_element_type=F32)


def _split(x, parts):
    out, r = [], x
    for _ in range(parts):
        h = r.astype(BF16)
        out.append(h)
        r = r - h.astype(F32)
    return out


def _rsum_left(u, x, parts):
    acc = None
    for h in _split(x, parts):
        d = jnp.dot(u, h, preferred_element_type=F32)
        acc = d if acc is None else acc + d
    return acc


def _ones_where(mask):
    return jnp.where(mask, 1.0, 0.0).astype(BF16)


def _sigmoid(x):
    return 1.0 / (1.0 + jnp.exp(-x))


def _softplus(x):
    return jnp.maximum(x, 0.0) + jnp.log(1.0 + jnp.exp(-jnp.abs(x)))


def _head_masks():
    lane = lax.broadcasted_iota(jnp.int32, (1, LANES), 1)
    return [jnp.where(lane < D_HEAD, 1.0, 0.0), jnp.where(lane >= D_HEAD, 1.0, 0.0)]


def _head_rstd(o, masks):
    sq = o * o
    r = None
    for m in masks:
        ms = jnp.sum(sq * m, axis=1, keepdims=True) * (1.0 / D_HEAD)
        t = lax.rsqrt(ms + EPS) * m
        r = t if r is None else r + t
    return r


def _head_mean(t, masks):
    out = None
    for m in masks:
        v = jnp.sum(t * m, axis=1, keepdims=True) * (1.0 / D_HEAD) * m
        out = v if out is None else out + v
    return out


def _mix_out_norm(a, w, res, g, name, tm=512):
    m, k = a.shape
    d = w.shape[1]
    tm = min(tm, m)

    def body(a_ref, w_ref, res_ref, g_ref, x_ref, h_ref):
        xv = res_ref[...] + _dot(a_ref[...], w_ref[...])
        x_ref[...] = xv
        r = lax.rsqrt(jnp.mean(xv * xv, axis=-1, keepdims=True) + EPS)
        h_ref[...] = (xv * r * g_ref[...]).astype(h_ref.dtype)

    row = pl.BlockSpec((tm, d), lambda i: (i, 0))
    return pl.pallas_call(
        body, name=name, grid=(m // tm,),
        in_specs=[pl.BlockSpec((tm, k), lambda i: (i, 0)), pl.BlockSpec((k, d), lambda i: (0, 0)), row,
                  pl.BlockSpec((1, d), lambda i: (0, 0))],
        out_specs=[row, row],
        out_shape=[jax.ShapeDtypeStruct((m, d), F32), jax.ShapeDtypeStruct((m, d), MXU_DTYPE)],
        compiler_params=_params(1),
    )(a, w, res, g)


def _ffn_down_loss(a, w, res, target, g, name, tm=512):
    m, k = a.shape
    d = w.shape[1]
    tm = min(tm, m)

    def body(a_ref, w_ref, res_ref, t_ref, g_ref, dx_ref, dxm_ref, dg_ref, loss_ref):
        xv = res_ref[...] + _dot(a_ref[...], w_ref[...])
        gv = g_ref[...]
        r = lax.rsqrt(jnp.mean(xv * xv, axis=-1, keepdims=True) + EPS)
        xh = xv * r
        e = xh * gv - t_ref[...]
        dy = e * (1.0 / d)
        dxh = dy * gv
        dxv = r * (dxh - xh * jnp.mean(dxh * xh, axis=-1, keepdims=True))
        dx_ref[...] = dxv
        dxm_ref[...] = dxv.astype(dxm_ref.dtype)

        @pl.when(pl.program_id(0) == 0)
        def _():
            dg_ref[...] = jnp.zeros_like(dg_ref)
            loss_ref[...] = jnp.zeros_like(loss_ref)

        dg_ref[...] += jnp.sum(dy * xh, axis=0, keepdims=True)
        part = 0.5 * jnp.sum(jnp.mean(e * e, axis=-1, keepdims=True), axis=0, keepdims=True)
        loss_ref[...] += jnp.broadcast_to(part, loss_ref.shape)

    row = pl.BlockSpec((tm, d), lambda i: (i, 0))
    vec = pl.BlockSpec((1, d), lambda i: (0, 0))
    return pl.pallas_call(
        body, name=name, grid=(m // tm,),
        in_specs=[pl.BlockSpec((tm, k), lambda i: (i, 0)), pl.BlockSpec((k, d), lambda i: (0, 0)), row, row, vec],
        out_specs=[row, row, vec, pl.BlockSpec((1, LANES), lambda i: (0, 0))],
        out_shape=[jax.ShapeDtypeStruct((m, d), F32), jax.ShapeDtypeStruct((m, d), MXU_DTYPE),
                   jax.ShapeDtypeStruct((1, d), F32), jax.ShapeDtypeStruct((1, LANES), F32)],
        compiler_params=_params(1),
    )(a, w, res, target, g)


def _mm_nt(a, w, name, rows=None, out_dtype=F32, tm=512, tk=512):
    m, n = a.shape
    row0, k = rows or (0, w.shape[0])
    tm, tk = min(tm, m), min(tk, k)
    first = row0 // tk

    def body(a_ref, w_ref, o_ref):
        o_ref[...] = _dot_nt(a_ref[...], w_ref[...]).astype(o_ref.dtype)

    return pl.pallas_call(
        body, name=name, grid=(k // tk, m // tm),
        in_specs=[pl.BlockSpec((tm, n), lambda j, i: (i, 0)), pl.BlockSpec((tk, n), lambda j, i: (first + j, 0))],
        out_specs=pl.BlockSpec((tm, tk), lambda j, i: (i, j)),
        out_shape=jax.ShapeDtypeStruct((m, k), out_dtype),
        compiler_params=_params(2),
    )(a, w)


def _mm_tn(a, b, name, tk, tn, tt=1024, out_dtype=BF16):
    t, k = a.shape
    n = b.shape[1]
    tt = min(tt, t)
    steps = t // tt

    def body(a_ref, b_ref, o_ref, acc):
        s = pl.program_id(2)

        @pl.when(s == 0)
        def _():
            acc[...] = jnp.zeros_like(acc)

        acc[...] += _dot_tn(a_ref[...], b_ref[...])

        @pl.when(s == steps - 1)
        def _():
            o_ref[...] = acc[...].astype(o_ref.dtype)

    return pl.pallas_call(
        body, name=name, grid=(k // tk, n // tn, steps),
        in_specs=[pl.BlockSpec((tt, tk), lambda i, j, s: (s, i)), pl.BlockSpec((tt, tn), lambda i, j, s: (s, j))],
        out_specs=pl.BlockSpec((tk, tn), lambda i, j, s: (i, j)),
        out_shape=jax.ShapeDtypeStruct((k, n), out_dtype),
        scratch_shapes=[pltpu.VMEM((tk, tn), F32)],
        compiler_params=_params(3),
    )(a, b)


def _ffn_up(h, wg_t, wu_t, name, tm=1024, tn=FF_TILE):
    m, k = h.shape
    n = wg_t.shape[0]
    tm = min(tm, m)

    def body(h_ref, wg_ref, wu_ref, gate_ref, up_ref, ff_ref):
        hv = h_ref[...]
        gate = _dot_nt(hv, wg_ref[...])
        up = _dot_nt(hv, wu_ref[...])
        gate_ref[...] = gate.astype(gate_ref.dtype)
        up_ref[...] = up.astype(up_ref.dtype)
        ff_ref[...] = (gate * _sigmoid(gate) * up).astype(ff_ref.dtype)

    wspec = pl.BlockSpec((tn, k), lambda j, i: (j, 0))
    ospec = pl.BlockSpec((tm, tn), lambda j, i: (i, j))
    return pl.pallas_call(
        body, name=name, grid=(n // tn, m // tm),
        in_specs=[pl.BlockSpec((tm, k), lambda j, i: (i, 0)), wspec, wspec],
        out_specs=[ospec, ospec, ospec],
        out_shape=[jax.ShapeDtypeStruct((m, n), MXU_DTYPE)] * 3,
        compiler_params=_params(2),
    )(h, wg_t, wu_t)


def _ffn_bwd_act(dx, wd, gate, up, name, tm=1024, tn=FF_TILE):
    m, k = dx.shape
    n = wd.shape[0]
    tm = min(tm, m)

    def body(dx_ref, wd_ref, gate_ref, up_ref, dgate_ref, dup_ref):
        dff = _dot_nt(dx_ref[...], wd_ref[...])
        gate = gate_ref[...].astype(F32)
        sg = _sigmoid(gate)
        dgate_ref[...] = (dff * up_ref[...].astype(F32) * sg * (1.0 + gate * (1.0 - sg))).astype(dgate_ref.dtype)
        dup_ref[...] = (dff * gate * sg).astype(dup_ref.dtype)

    ospec = pl.BlockSpec((tm, tn), lambda j, i: (i, j))
    return pl.pallas_call(
        body, name=name, grid=(n // tn, m // tm),
        in_specs=[pl.BlockSpec((tm, k), lambda j, i: (i, 0)), pl.BlockSpec((tn, k), lambda j, i: (j, 0)),
                  ospec, ospec],
        out_specs=[ospec, ospec],
        out_shape=[jax.ShapeDtypeStruct((m, n), MXU_DTYPE), jax.ShapeDtypeStruct((m, n), MXU_DTYPE)],
        compiler_params=_params(2),
    )(dx, wd, gate, up)


def _chunk_masks():
    r = lax.broadcasted_iota(jnp.int32, (SUPER, SUPER), 0)
    c = lax.broadcasted_iota(jnp.int32, (SUPER, SUPER), 1)
    same = jnp.right_shift(r, 6) == jnp.right_shift(c, 6)
    lower = jnp.logical_and(same, c <= r)
    upper = jnp.logical_and(same, c >= r)
    return same, lower, upper


def _head_block_mask():
    r = lax.broadcasted_iota(jnp.int32, (LANES, LANES), 0)
    c = lax.broadcasted_iota(jnp.int32, (LANES, LANES), 1)
    return jnp.where(jnp.right_shift(r, 6) == jnp.right_shift(c, 6), 1.0, 0.0)


def _lower_bound(lb_raw):
    return 1.0 / (1.0 + jnp.exp(lb_raw[1:2, :] - lb_raw[0:1, :]))


HGRN_UNROLL = 4
PER_SUPER = SUPER // CHUNK
CHUNK_ROWS = [slice(c * CHUNK, (c + 1) * CHUNK) for c in range(PER_SUPER)]


def _over_chunks(rows):
    return jnp.concatenate([jnp.broadcast_to(r, (CHUNK, LANES)) for r in rows], axis=0)


def _hgrn_gates(q, hf, lb, lower_b):
    sig = _sigmoid(hf)
    f = lb + (1.0 - lb) * sig
    k = 1.0 - f
    lf = jnp.log(f)
    b = _rsum_left(lower_b, lf, 2)
    ends = [b[cr.stop - 1:cr.stop, :] for cr in CHUNK_ROWS]
    eb = jnp.exp(b)
    enb = jnp.exp(-b)
    edb = jnp.exp(_over_chunks(ends) - b)
    decs = [jnp.exp(e) for e in ends]
    return sig, f, k, decs, eb, enb, edb, q * eb, k * enb, k * edb


def _hgrn_fwd(proj, lower_bounds, norm_g, n_seq, seq):
    t = n_seq * seq
    n_super = seq // SUPER
    n_chunks = seq // CHUNK

    def body(q_ref, f_ref, i_ref, g_ref, lb_ref, ng_ref, out_ref, opre_ref, st_ref):
        masks = _head_masks()
        _, lower, _ = _chunk_masks()
        lower_b = _ones_where(lower)
        bd = _head_block_mask()
        lb = _lower_bound(lb_ref[...])
        ng = ng_ref[...]

        def step(it, st):
            blocks = [HGRN_UNROLL * it + u for u in range(HGRN_UNROLL)]
            rows = [pl.ds(pl.multiple_of(sb * SUPER, SUPER), SUPER) for sb in blocks]
            vs = [i_ref[rw, :] for rw in rows]
            gates = [_hgrn_gates(q_ref[rw, :], f_ref[rw, :], lb, lower_b) for rw in rows]
            decs, qes, kes, kds = ([g[k] for g in gates] for k in (3, 7, 8, 9))
            scores = [[_dot_nt(qe * m, ke) for m in masks] for qe, ke in zip(qes, kes)]
            updates = [[_dot_tn(v[cr], kd[cr]) for cr in CHUNK_ROWS] for v, kd in zip(vs, kds)]
            states = [st]
            for dec_b, upd_b in zip(decs, updates):
                for dec, upd in zip(dec_b, upd_b):
                    states.append(states[-1] * dec + bd * upd)
            for u, sb in enumerate(blocks):
                for c in range(PER_SUPER):
                    st_ref[0, 0, sb * PER_SUPER + c] = states[u * PER_SUPER + c]
            intra = [[_dot(jnp.where(lower, p, 0.0), v) for p in sc] for sc, v in zip(scores, vs)]
            inter = [[_dot_nt(qe[cr], states[u * PER_SUPER + c]) for c, cr in enumerate(CHUNK_ROWS)]
                     for u, qe in enumerate(qes)]
            for rw, intra_b, inter_b in zip(rows, intra, inter):
                o = intra_b[0] * masks[0] + intra_b[1] * masks[1] + jnp.concatenate(inter_b, axis=0)
                opre_ref[rw, :] = o
                hg = g_ref[rw, :]
                on = o * _head_rstd(o, masks) * ng
                out_ref[rw, :] = (on * hg * _sigmoid(hg)).astype(out_ref.dtype)
            return states[-1]

        lax.fori_loop(0, n_super // HGRN_UNROLL, step, jnp.zeros((LANES, LANES), F32))

    def col(k):
        return pl.BlockSpec((seq, LANES), lambda p, b: (b, k * N_PAIRS + p))

    vec = lambda rows: pl.BlockSpec((rows, LANES), lambda p, b: (0, p))
    ospec = pl.BlockSpec((seq, LANES), lambda p, b: (b, p))
    return pl.pallas_call(
        body, name="hgrn_fwd", grid=(N_PAIRS, n_seq),
        in_specs=[col(0), col(1), col(2), col(3), vec(2), vec(1)],
        out_specs=[ospec, ospec,
                   pl.BlockSpec((1, 1, n_chunks, LANES, LANES), lambda p, b: (b, p, 0, 0, 0))],
        out_shape=[jax.ShapeDtypeStruct((t, 2 * GROUP), MXU_DTYPE), jax.ShapeDtypeStruct((t, GROUP), F32),
                   jax.ShapeDtypeStruct((n_seq, N_PAIRS, n_chunks, LANES, LANES), F32)],
        compiler_params=_params(2),
    )(proj, proj, proj, proj, lower_bounds, norm_g)


def _hgrn_bwd(proj, lower_bounds, norm_g, dmix, opre, states, n_seq, seq):
    t = n_seq * seq
    n_super = seq // SUPER
    n_chunks = seq // CHUNK
    per = SUPER // CHUNK

    def body(q_ref, f_ref, i_ref, g_ref, lb_ref, ng_ref, dm_ref, opre_ref, st_ref,
             dq_ref, df_ref, di_ref, dg_ref, dlb_ref, dng_ref):
        masks = _head_masks()
        _, lower, upper = _chunk_masks()
        lower_b, upper_b = _ones_where(lower), _ones_where(upper)
        bd = _head_block_mask()
        lb_raw = lb_ref[...]
        lb = _lower_bound(lb_raw)
        ng = ng_ref[...]

        @pl.when(pl.program_id(1) == 0)
        def _():
            dlb_ref[...] = jnp.zeros_like(dlb_ref)
            dng_ref[...] = jnp.zeros_like(dng_ref)

        def first_half(sb):
            rows = pl.ds(pl.multiple_of(sb * SUPER, SUPER), SUPER)
            q, hf, v, hg = q_ref[rows, :], f_ref[rows, :], i_ref[rows, :], g_ref[rows, :]
            sig, f, k, decs, eb, enb, edb, qe, ke, kd = _hgrn_gates(q, hf, lb, lower_b)
            o = opre_ref[rows, :]
            r = _head_rstd(o, masks)
            oh = o * r
            dm = dm_ref[rows, :]
            sg = _sigmoid(hg)
            dg_ref[rows, :] = (dm * oh * ng * sg * (1.0 + hg * (1.0 - sg))).astype(dg_ref.dtype)
            don = dm * hg * sg
            dng_ref[...] += jnp.sum(don * oh, axis=0, keepdims=True)
            doh = don * ng
            do = r * (doh - oh * _head_mean(doh * oh, masks))
            doms = [do * m for m in masks]
            qems = [qe * m for m in masks]
            scores = [_dot_nt(qem, ke) for qem in qems]
            dscores = [_dot_nt(dom, v) for dom in doms]
            prevs = [st_ref[0, 0, sb * per + c] for c in range(per)]
            dst_in = [_dot_tn(do[cr], qe[cr]) for cr in CHUNK_ROWS]
            dqe_i = [_dot(do[cr], prev) for cr, prev in zip(CHUNK_ROWS, prevs)]
            return dict(rows=rows, v=v, sig=sig, f=f, decs=decs, eb=eb, enb=enb, edb=edb, qe=qe, ke=ke, kd=kd,
                        doms=doms, qems=qems, scores=scores, dscores=dscores, prevs=prevs, dst_in=dst_in, dqe_i=dqe_i)

        def second_half(blk, dsts):
            v, qe, ke, kd = blk["v"], blk["qe"], blk["ke"], blk["kd"]
            ps = [jnp.where(lower, p, 0.0) for p in blk["scores"]]
            dps = [jnp.where(lower, dp, 0.0) for dp in blk["dscores"]]
            dqe_h = [_dot(dp, ke) for dp in dps]
            dke_h = [_dot_tn(dp, qem) for dp, qem in zip(dps, blk["qems"])]
            dv_h = [_dot_tn(p, dom) for p, dom in zip(ps, blk["doms"])]
            dus = [bd * d for d in dsts]
            dv_i = [_dot_nt(kd[cr], du) for cr, du in zip(CHUNK_ROWS, dus)]
            dkd_i = [_dot(v[cr], du) for cr, du in zip(CHUNK_ROWS, dus)]

            def finish():
                dqe = dqe_h[0] * masks[0] + dqe_h[1] * masks[1] + jnp.concatenate(blk["dqe_i"], axis=0)
                dke = dke_h[0] + dke_h[1]
                dv = dv_h[0] + dv_h[1] + jnp.concatenate(dv_i, axis=0)
                dkd = jnp.concatenate(dkd_i, axis=0)
                dk = dke * blk["enb"] + dkd * blk["edb"]
                db = dqe * qe - dke * ke - dkd * kd
                dkd_kd = dkd * kd
                dends = [jnp.sum(dkd_kd[cr], axis=0, keepdims=True)
                         + jnp.sum(dsts[c] * blk["prevs"][c], axis=0, keepdims=True) * blk["decs"][c]
                         for c, cr in enumerate(CHUNK_ROWS)]
                dlf = _rsum_left(upper_b, db, 2) + _over_chunks(dends)
                sig = blk["sig"]
                dfv = dlf / blk["f"] - dk
                rows = blk["rows"]
                dq_ref[rows, :] = (dqe * blk["eb"]).astype(dq_ref.dtype)
                di_ref[rows, :] = dv.astype(di_ref.dtype)
                df_ref[rows, :] = (dfv * (1.0 - lb) * sig * (1.0 - sig)).astype(df_ref.dtype)
                dlb = jnp.sum(dfv * (1.0 - sig), axis=0, keepdims=True)
                da0 = dlb * lb * (1.0 - lb)
                dlb_ref[0:1, :] += da0
                dlb_ref[1:2, :] -= da0

            return finish

        def step(it, dst):
            blocks = [first_half(n_super - 1 - HGRN_UNROLL * it - u) for u in range(HGRN_UNROLL)]
            all_dsts = []
            for blk in blocks:
                dsts = [None] * per
                for c in reversed(range(per)):
                    dsts[c] = dst
                    dst = bd * (dst * blk["decs"][c] + blk["dst_in"][c])
                all_dsts.append(dsts)
            for finish in [second_half(blk, dsts) for blk, dsts in zip(blocks, all_dsts)]:
                finish()
            return dst

        lax.fori_loop(0, n_super // HGRN_UNROLL, step, jnp.zeros((LANES, LANES), F32))

    def col(k):
        return pl.BlockSpec((seq, LANES), lambda p, b: (b, k * N_PAIRS + p))

    vec = lambda rows: pl.BlockSpec((rows, LANES), lambda p, b: (0, p))
    ospec = pl.BlockSpec((seq, LANES), lambda p, b: (b, p))
    piece = jax.ShapeDtypeStruct((t, GROUP), MXU_DTYPE)
    return pl.pallas_call(
        body, name="hgrn_bwd", grid=(N_PAIRS, n_seq),
        in_specs=[col(0), col(1), col(2), col(3), vec(2), vec(1), ospec, ospec,
                  pl.BlockSpec((1, 1, n_chunks, LANES, LANES), lambda p, b: (b, p, 0, 0, 0))],
        out_specs=[ospec, ospec, ospec, ospec, vec(2), vec(1)],
        out_shape=[piece, piece, piece, piece,
                   jax.ShapeDtypeStruct((2, GROUP), F32), jax.ShapeDtypeStruct((1, GROUP), F32)],
        compiler_params=_params(2),
    )(proj, proj, proj, proj, lower_bounds, norm_g, dmix, opre, states)


SB_SCALE = 1.0 / math.sqrt(D_HEAD)


SB_STEP = 2 * SB_BLOCK
QUERY_BLOCKS = (slice(0, SB_BLOCK), slice(SB_BLOCK, SB_STEP))


def _triangle(keep):
    row = lax.broadcasted_iota(jnp.int32, (SB_BLOCK, SB_BLOCK), 0)
    col = lax.broadcasted_iota(jnp.int32, (SB_BLOCK, SB_BLOCK), 1)
    return _ones_where(keep(row, col))


SB_HALF = SB_BLOCK // 2


def _keep(on_diagonal, x):
    if not on_diagonal:
        return x
    row = lax.broadcasted_iota(jnp.int32, (SB_HALF, SB_HALF), 0)
    col = lax.broadcasted_iota(jnp.int32, (SB_HALF, SB_HALF), 1)
    return jnp.where(col < row, x, 0.0)


def _lower_quadrants(fn, n_out, *tiles):
    def quadrant(r, c):
        return [t[r * SB_HALF:(r + 1) * SB_HALF, c * SB_HALF:(c + 1) * SB_HALF] for t in tiles]

    top, left, bottom = fn(True, *quadrant(0, 0)), fn(False, *quadrant(1, 0)), fn(True, *quadrant(1, 1))
    zero = jnp.zeros((SB_HALF, SB_HALF), F32)
    return [jnp.concatenate([jnp.concatenate([top[o], zero], axis=1), jnp.concatenate([left[o], bottom[o]], axis=1)],
                            axis=0) for o in range(n_out)]


def _sb_fwd(proj, norm_g, mixed, shards, n_seq, seq):
    t = n_seq * seq
    nq = seq // SB_STEP
    q0, k0, v0 = 0, N_PAIRS, 2 * N_PAIRS
    n_w = len(shards)
    n_steps = N_PAIRS * n_seq * nq
    tri = _triangle(lambda row, col: row >= col)

    def body(q_ref, k_ref, v_ref, ng_ref, tri_ref, mixed_in, *rest):
        del mixed_in
        shard_refs = rest[:n_w]
        out_ref, opre_ref, ctot_ref = rest[n_w:n_w + 3]
        gathered = rest[n_w + 3:2 * n_w + 3]
        send_sems, recv_sems, local_sems = rest[2 * n_w + 3:]
        i = pl.program_id(2)
        step = (pl.program_id(0) * n_seq + pl.program_id(1)) * nq + i
        plan = _GatherPlan(shard_refs, gathered, send_sems, recv_sems, local_sems)

        @pl.when(step == 0)
        def _():
            plan.start()

        @pl.when(step == (3 * n_steps) // 4)
        def _():
            plan.forward()

        masks = _head_masks()
        suffix = tri_ref[...]
        qhs = [[(q_ref[blk, :] * SB_SCALE * m).astype(MXU_DTYPE) for m in masks] for blk in QUERY_BLOCKS]

        def tiles(js, work, carry):
            rows = [pl.ds(pl.multiple_of(j * SB_BLOCK, SB_BLOCK), SB_BLOCK) for j in js]
            ks = [k_ref[rw, :].astype(MXU_DTYPE) for rw in rows]
            vs = [v_ref[rw, :].astype(MXU_DTYPE) for rw in rows]
            zs = [[_dot_nt(qh, ks[ts]) for qh in qhs[qb]] for qb, ts, _ in work]
            ccs = [[_lower_quadrants(lambda on, zq: (_keep(on, _softplus(zq)),), 1, z)[0] if diag else _softplus(z)
                    for z in zw] for zw, (_, _, diag) in zip(zs, work)]
            sums = [[jnp.dot(cc.astype(BF16), suffix, preferred_element_type=F32) for cc in cw] for cw in ccs]
            out = [list(per_block) for per_block in carry]
            for h in range(len(masks)):
                for w, (qb, ts, diag) in enumerate(work):
                    run, acc = out[qb][h]
                    logit = zs[w][h] - (sums[w][h] + run)
                    if diag:
                        a = _lower_quadrants(lambda on, lq: (_keep(on, jnp.exp(lq)),), 1, logit)[0]
                    else:
                        a = jnp.exp(logit)
                    out[qb][h] = (run + sums[w][h][:, 0:1], acc + _dot(a, vs[ts]))
            return tuple(tuple(per_block) for per_block in out)

        start = ((jnp.zeros((SB_BLOCK, 1), F32), jnp.zeros((SB_BLOCK, LANES), F32)),) * 2
        carry = tiles([2 * i, 2 * i + 1], [(0, 0, True), (1, 1, True), (1, 0, False)], (start, start))
        both = [(0, 0, False), (0, 1, False), (1, 0, False), (1, 1, False)]
        carry = lax.fori_loop(0, i, lambda s, cy: tiles([2 * (i - s) - 1, 2 * (i - s) - 2], both, cy), carry)
        opre = jnp.concatenate([cb[0][1] * masks[0] + cb[1][1] * masks[1] for cb in carry], axis=0)
        ctot = jnp.concatenate([cb[0][0] * masks[0] + cb[1][0] * masks[1] for cb in carry], axis=0)
        opre_ref[...] = opre
        ctot_ref[...] = ctot
        out_ref[...] = (opre * _head_rstd(opre, masks) * ng_ref[...]).astype(out_ref.dtype)

        @pl.when(step == n_steps - 1)
        def _():
            plan.finish()

    qspec = pl.BlockSpec((SB_STEP, LANES), lambda p, b, i: (b * nq + i, q0 + p))
    ospec = pl.BlockSpec((SB_STEP, LANES), lambda p, b, i: (b * nq + i, p))
    hbm = pl.BlockSpec(memory_space=pltpu.HBM)
    outs = pl.pallas_call(
        body, name="sb_fwd", grid=(N_PAIRS, n_seq, nq),
        in_specs=[qspec,
                  pl.BlockSpec((seq, LANES), lambda p, b, i: (b, k0 + p)),
                  pl.BlockSpec((seq, LANES), lambda p, b, i: (b, v0 + p)),
                  pl.BlockSpec((1, LANES), lambda p, b, i: (0, p)),
                  pl.BlockSpec(tri.shape, lambda p, b, i: (0, 0)), hbm] + [hbm] * n_w,
        out_specs=[pl.BlockSpec((SB_STEP, LANES), lambda p, b, i: (b * nq + i, N_PAIRS + p)), ospec, ospec]
        + [hbm] * n_w,
        out_shape=[jax.ShapeDtypeStruct(mixed.shape, mixed.dtype), jax.ShapeDtypeStruct((t, GROUP), F32),
                   jax.ShapeDtypeStruct((t, GROUP), F32)]
        + [jax.ShapeDtypeStruct((N_DEV,) + s.shape, s.dtype) for s in shards],
        scratch_shapes=[pltpu.SemaphoreType.DMA((n_w * _GatherPlan.COPIES,)),
                        pltpu.SemaphoreType.DMA((n_w * _GatherPlan.COPIES,)), pltpu.SemaphoreType.DMA((n_w,))],
        input_output_aliases={5: 0},
        compiler_params=_params(3),
    )(proj, proj, proj, norm_g, tri, mixed, *shards)
    return outs[0], outs[1], outs[2], list(outs[3:])


def _sb_bwd(proj, norm_g, dmix, opre, ctot, grads, n_seq, seq):
    t = n_seq * seq
    nq = seq // SB_STEP
    q0, k0, v0 = 0, N_PAIRS, 2 * N_PAIRS
    n_w = len(grads)
    n_steps = N_PAIRS * n_seq * nq
    tri = _triangle(lambda row, col: row <= col)

    def body(q_ref, k_ref, v_ref, ng_ref, tri_ref, dm_ref, opre_ref, ctot_ref, *rest):
        grad_refs = rest[:n_w]
        dq_ref, dk_ref, dv_ref, dng_ref = rest[n_w:n_w + 4]
        lands = rest[n_w + 4:2 * n_w + 4]
        dk_acc, dv_acc, send_sems, recv_sems, local_sems = rest[2 * n_w + 4:]
        p_id, b_id, i = pl.program_id(0), pl.program_id(1), pl.program_id(2)
        step = (p_id * n_seq + b_id) * nq + i
        plan = _ScatterPlan(grad_refs, lands, send_sems, recv_sems, local_sems)

        @pl.when(step == 0)
        def _():
            plan.start()

        masks = _head_masks()
        upto = tri_ref[...]

        def prefix(x):
            return jnp.dot(x.astype(BF16), upto, preferred_element_type=F32)

        @pl.when(i == 0)
        def _():
            dk_acc[...] = jnp.zeros_like(dk_acc)
            dv_acc[...] = jnp.zeros_like(dv_acc)

        @pl.when(jnp.logical_and(b_id == 0, i == 0))
        def _():
            dng_ref[...] = jnp.zeros_like(dng_ref)

        o = opre_ref[...]
        rs = _head_rstd(o, masks)
        oh = o * rs
        dm = dm_ref[...]
        dng_ref[...] += jnp.sum(dm * oh, axis=0, keepdims=True)
        doh = dm * ng_ref[...]
        do = rs * (doh - oh * _head_mean(doh * oh, masks))

        heads = range(len(masks))
        qs = [q_ref[blk, :] * SB_SCALE for blk in QUERY_BLOCKS]
        dos = [do[blk] for blk in QUERY_BLOCKS]
        qhs = [[(q * m).astype(MXU_DTYPE) for m in masks] for q in qs]
        doms = [[(d * m).astype(MXU_DTYPE) for m in masks] for d in dos]
        head_rows = [jnp.where(jnp.right_shift(lax.broadcasted_iota(jnp.int32, (LANES, 1), 0), 6) == h, 1.0, 0.0)
                     for h in heads]
        qhts = [[(qt * hr).astype(MXU_DTYPE) for hr in head_rows] for qt in [q.astype(F32).T for q in qs]]
        domts = [[(dt * hr).astype(MXU_DTYPE) for hr in head_rows] for dt in [d.T for d in dos]]
        totals = [[ctot_ref[blk, h * D_HEAD:h * D_HEAD + 1] for h in heads] for blk in QUERY_BLOCKS]

        def tiles(js, work, carry):
            rows = [pl.ds(pl.multiple_of(j * SB_BLOCK, SB_BLOCK), SB_BLOCK) for j in js]
            ks = [k_ref[rw, :].astype(MXU_DTYPE) for rw in rows]
            vs = [v_ref[rw, :].astype(MXU_DTYPE) for rw in rows]
            zs = [[_dot_nt(qhs[qb][h], ks[ts]) for h in heads] for qb, ts, _ in work]
            das = [[_dot_nt(doms[qb][h], vs[ts]) for h in heads] for qb, ts, _ in work]
            def gates(on, zq):
                sp = _softplus(zq)
                return _keep(on, sp), zq - sp, _keep(on, jnp.exp(zq - sp))

            trio = [[_lower_quadrants(gates, 3, z) if diag else gates(False, z) for z in zw]
                    for zw, (_, _, diag) in zip(zs, work)]
            ccs, lsigs, sigs = ([[t[o] for t in tw] for tw in trio] for o in range(3))
            pres = [[prefix(cc) for cc in cw] for cw in ccs]
            out = [list(per_block) for per_block in carry]
            for h in heads:
                for w, (qb, ts, diag) in enumerate(work):
                    pc, pdl, dq_h = out[qb][h]
                    logit = lsigs[w][h] + pres[w][h] - (totals[qb][h] - pc)
                    if diag:
                        a = _lower_quadrants(lambda on, lq: (_keep(on, jnp.exp(lq)),), 1, logit)[0]
                    else:
                        a = jnp.exp(logit)
                    dl = a * das[w][h]
                    dv_acc[js[ts]] += _dot(domts[qb][h], a)
                    dpre = prefix(dl)
                    dz = dl - sigs[w][h] * (pdl + dpre)
                    dzb = dz.astype(MXU_DTYPE)
                    dk_acc[js[ts]] += _dot(qhts[qb][h], dzb)
                    out[qb][h] = (pc + pres[w][h][:, SB_BLOCK - 1:SB_BLOCK], pdl + dpre[:, SB_BLOCK - 1:SB_BLOCK],
                                  dq_h + _dot(dzb, ks[ts]))
            return tuple(tuple(per_block) for per_block in out)

        zero = jnp.zeros((SB_BLOCK, 1), F32)
        start = ((zero, zero, jnp.zeros((SB_BLOCK, LANES), F32)),) * 2
        both = [(0, 0, False), (0, 1, False), (1, 0, False), (1, 1, False)]
        carry = lax.fori_loop(0, i, lambda s, cy: tiles([2 * s, 2 * s + 1], both, cy), (start, start))
        carry = tiles([2 * i, 2 * i + 1], [(0, 0, True), (1, 0, False), (1, 1, True)], carry)
        dq = jnp.concatenate([cb[0][2] * masks[0] + cb[1][2] * masks[1] for cb in carry], axis=0)
        dq_ref[...] = (dq * SB_SCALE).astype(dq_ref.dtype)

        @pl.when(i == nq - 1)
        def _():
            for j in range(seq // SB_BLOCK):
                tile_rows = slice(j * SB_BLOCK, (j + 1) * SB_BLOCK)
                dk_ref[tile_rows, :] = dk_acc[j].T.astype(dk_ref.dtype)
                dv_ref[tile_rows, :] = dv_acc[j].T.astype(dv_ref.dtype)

        @pl.when(step == n_steps - 1)
        def _():
            plan.finish()

    qspec = pl.BlockSpec((SB_STEP, LANES), lambda p, b, i: (b * nq + i, q0 + p))
    ospec = pl.BlockSpec((SB_STEP, LANES), lambda p, b, i: (b * nq + i, p))
    dmspec = pl.BlockSpec((SB_STEP, LANES), lambda p, b, i: (b * nq + i, N_PAIRS + p))
    full = lambda k: pl.BlockSpec((seq, LANES), lambda p, b, i: (b, k + p))
    vec = pl.BlockSpec((1, LANES), lambda p, b, i: (0, p))
    hbm = pl.BlockSpec(memory_space=pltpu.HBM)
    piece = jax.ShapeDtypeStruct((t, GROUP), MXU_DTYPE)
    outs = pl.pallas_call(
        body, name="sb_bwd", grid=(N_PAIRS, n_seq, nq),
        in_specs=[qspec, full(k0), full(v0), vec, pl.BlockSpec(tri.shape, lambda p, b, i: (0, 0)), dmspec, ospec, ospec]
        + [hbm] * n_w,
        out_specs=[ospec, full(0), full(0), vec] + [hbm] * n_w,
        out_shape=[piece, piece, piece, jax.ShapeDtypeStruct((1, GROUP), F32)]
        + [jax.ShapeDtypeStruct(g.shape, g.dtype) for g in grads],
        scratch_shapes=[pltpu.VMEM((seq // SB_BLOCK, LANES, SB_BLOCK), F32),
                        pltpu.VMEM((seq // SB_BLOCK, LANES, SB_BLOCK), F32),
                        pltpu.SemaphoreType.DMA((n_w * (N_DEV - 1),)), pltpu.SemaphoreType.DMA((n_w * (N_DEV - 1),)),
                        pltpu.SemaphoreType.DMA((n_w,))],
        compiler_params=_params(3),
    )(proj, proj, proj, norm_g, tri, dmix, opre, ctot, *[pltpu.with_memory_space_constraint(g, pltpu.HBM) for g in grads])
    return outs[0], outs[1], outs[2], outs[3], list(outs[4:])


def _mesh_place():
    x, y, c = lax.axis_index("x"), lax.axis_index("y"), lax.axis_index("c")
    return x, y, c


def _peer(x, y, c, k):
    px = lax.rem(x + ((k >> 2) & 1), 2)
    py = lax.rem(y + ((k >> 1) & 1), 2)
    pc = lax.rem(c + (k & 1), 2)
    return (px, py, pc), 4 * px + 2 * py + pc


def _remote(src, dst, send_sem, recv_sem, to):
    return pltpu.make_async_remote_copy(src_ref=src, dst_ref=dst, send_sem=send_sem, recv_sem=recv_sem,
                                        device_id=to, device_id_type=pl.DeviceIdType.MESH)


class _GatherPlan:
    COPIES = 7

    def __init__(self, shards, gathered, send_sems, recv_sems, local_sems):
        x, y, c = _mesh_place()
        self.c = c
        self.me = (x, y, c)
        self.sibling = (x, y, 1 - c)
        self.chips = [(1 - x, y), (x, 1 - y), (1 - x, 1 - y)]
        self.tensors = list(zip(shards, gathered))
        self.send_sems, self.recv_sems, self.local_sems = send_sems, recv_sems, local_sems

    @staticmethod
    def _index(place):
        return 4 * place[0] + 2 * place[1] + place[2]

    def _copy(self, w, k, block, to, own=False):
        shard, gathered = self.tensors[w]
        slot = gathered.at[self._index(block)]
        n = w * self.COPIES + k
        return _remote(shard if own else slot, slot, self.send_sems.at[n], self.recv_sems.at[n], to)

    def _local(self, w):
        shard, gathered = self.tensors[w]
        return pltpu.make_async_copy(shard, gathered.at[self._index(self.me)], self.local_sems.at[w])

    def _first(self, w):
        return [self._copy(w, 0, self.me, self.sibling, own=True)] + [
            self._copy(w, 1 + j, self.me, (*chip, self.c), own=True) for j, chip in enumerate(self.chips)]

    def _passed(self, w):
        return [self._copy(w, 4 + j, (*chip, self.c), self.sibling) for j, chip in enumerate(self.chips)]

    def start(self):
        for w in range(len(self.tensors)):
            self._local(w).start()
            for cp in self._first(w):
                cp.start()

    def forward(self):
        for w in range(len(self.tensors)):
            passed = self._passed(w)
            for j, chip in enumerate(self.chips):
                self._copy(w, 1 + j, (*chip, self.c), self.me).wait_recv()
                passed[j].start()

    def finish(self):
        for w in range(len(self.tensors)):
            self._copy(w, 0, self.sibling, self.me).wait_recv()
            for j, chip in enumerate(self.chips):
                self._copy(w, 4 + j, (*chip, 1 - self.c), self.me).wait_recv()
            for cp in self._first(w) + self._passed(w):
                cp.wait_send()
            self._local(w).wait()


class _ScatterPlan:
    def __init__(self, grads, lands, send_sems, recv_sems, local_sems):
        self.place = _mesh_place()
        x, y, c = self.place
        self.me = 4 * x + 2 * y + c
        self.tensors = list(zip(grads, lands))
        self.send_sems, self.recv_sems, self.local_sems = send_sems, recv_sems, local_sems

    def _copies(self, w):
        grad, land = self.tensors[w]
        out = []
        for k in range(1, N_DEV):
            peer, pidx = _peer(*self.place, k)
            n = w * (N_DEV - 1) + k - 1
            sems = (self.send_sems.at[n], self.recv_sems.at[n], peer)
            out.append((_remote(grad.at[pidx], land.at[self.me], *sems), _remote(grad.at[pidx], land.at[pidx], *sems)))
        return out

    def _local(self, w):
        grad, land = self.tensors[w]
        return pltpu.make_async_copy(grad.at[self.me], land.at[self.me], self.local_sems.at[w])

    def start(self):
        for w in range(len(self.tensors)):
            self._local(w).start()
            for send, _ in self._copies(w):
                send.start()

    def finish(self):
        for w in range(len(self.tensors)):
            copies = self._copies(w)
            for _, arrival in copies:
                arrival.wait_recv()
            for send, _ in copies:
                send.wait_send()
            self._local(w).wait()


def _cast_shards(shards):
    def body(*refs):
        n = len(refs) // 2
        for src, dst in zip(refs[:n], refs[n:]):
            dst[...] = src[...].astype(dst.dtype)

    shapes = [jax.ShapeDtypeStruct(s.shape, BF16) for s in shards]
    return pl.pallas_call(
        body, name="cast_shards", grid=(1,),
        in_specs=[_whole(s) for s in shards], out_specs=[_whole(s) for s in shapes], out_shape=shapes,
        compiler_params=_params(1),
    )(*shards)


def _gather_w_in(shard, x, g, chunk=512):
    rows, cols = shard.shape
    t, d = x.shape
    chunk = min(chunk, t)
    n_chunks = t // chunk

    def body(w_ref, x_ref, g_ref, out_ref, h_ref, x_buf, send_sems, recv_sems, local_sems, x_sems):
        plan = _GatherPlan([w_ref], [out_ref], send_sems, recv_sems, local_sems)
        plan.start()

        def fetch(j):
            return pltpu.make_async_copy(x_ref.at[pl.ds(j * chunk, chunk)], x_buf.at[j % 2], x_sems.at[j % 2])

        fetch(0).start()
        gv = g_ref[...]
        for j in range(n_chunks):
            if j + 1 < n_chunks:
                fetch(j + 1).start()
            fetch(j).wait()
            xv = x_buf[j % 2]
            r = lax.rsqrt(jnp.mean(xv * xv, axis=-1, keepdims=True) + EPS)
            h_ref[j * chunk:(j + 1) * chunk, :] = (xv * r * gv).astype(h_ref.dtype)
        plan.forward()
        plan.finish()

    vmem = pl.BlockSpec(memory_space=pltpu.VMEM)
    hbm = pl.BlockSpec(memory_space=pltpu.HBM)
    return pl.pallas_call(
        body, name="gather_w_in", in_specs=[hbm, hbm, vmem], out_specs=[hbm, vmem],
        out_shape=[jax.ShapeDtypeStruct((N_DEV, rows, cols), shard.dtype), jax.ShapeDtypeStruct((t, d), MXU_DTYPE)],
        scratch_shapes=[pltpu.VMEM((2, chunk, d), F32), pltpu.SemaphoreType.DMA((_GatherPlan.COPIES,)),
                        pltpu.SemaphoreType.DMA((_GatherPlan.COPIES,)), pltpu.SemaphoreType.DMA((1,)),
                        pltpu.SemaphoreType.DMA((2,))],
        compiler_params=pltpu.CompilerParams(vmem_limit_bytes=VMEM_LIMIT),
    )(shard, x, g)


def _dw_rows(pieces, b, name, tn=512, tt=1024):
    t, n = b.shape
    widths = [p.shape[1] for p in pieces]
    rows = sum(widths)
    tt = min(tt, t)
    steps = t // tt
    n_p = len(pieces)

    def body(*refs):
        piece_refs, b_ref, o_ref, acc = refs[:n_p], refs[n_p], refs[n_p + 1], refs[n_p + 2]
        s = pl.program_id(1)

        @pl.when(s == 0)
        def _():
            acc[...] = jnp.zeros_like(acc)

        bv = b_ref[...]
        off = 0
        for p_ref, width in zip(piece_refs, widths):
            acc[off:off + width, :] += _dot_tn(p_ref[...], bv)
            off += width

        @pl.when(s == steps - 1)
        def _():
            o_ref[...] = acc[...].astype(o_ref.dtype)

    return pl.pallas_call(
        body, name=name, grid=(n // tn, steps),
        in_specs=[pl.BlockSpec((tt, width), lambda j, s: (s, 0)) for width in widths]
        + [pl.BlockSpec((tt, tn), lambda j, s: (s, j))],
        out_specs=pl.BlockSpec((rows, tn), lambda j, s: (0, j)),
        out_shape=jax.ShapeDtypeStruct((rows, n), BF16),
        scratch_shapes=[pltpu.VMEM((rows, tn), F32)],
        compiler_params=_params(2),
    )(*pieces, b)


def _dh_norm_bwd(pairs, x, g, res, name, tm=512, after=(), next_w=None):
    m, d = x.shape
    tm = min(tm, m)
    n_p = len(pairs)
    n_steps = m // tm
    n_in = 2 * n_p + 3 + len(after) + (next_w is not None)

    def body(*refs):
        x_ref, g_ref, res_ref = refs[2 * n_p:2 * n_p + 3]
        dx_ref, dg_ref = refs[n_in:n_in + 2]

        @pl.when(pl.program_id(0) == 0)
        def _():
            dg_ref[...] = jnp.zeros_like(dg_ref)

        dh = None
        for q in range(n_p):
            part = _dot(refs[2 * q][...], refs[2 * q + 1][...])
            dh = part if dh is None else dh + part
        xv = x_ref[...]
        r = lax.rsqrt(jnp.mean(xv * xv, axis=-1, keepdims=True) + EPS)
        xh = xv * r
        dxh = dh * g_ref[...]
        dxv = res_ref[...] + r * (dxh - xh * jnp.mean(dxh * xh, axis=-1, keepdims=True))
        dx_ref[...] = dxv
        dg_ref[...] += jnp.sum(dh * xh, axis=0, keepdims=True)
        if next_w is not None:
            refs[n_in + 2][...] = _dot_nt(dxv, refs[n_in - 1][...])

    in_specs, args = [], []
    for a, w, r in pairs:
        k = a.shape[1]
        in_specs += [pl.BlockSpec((tm, k), lambda i: (i, 0)),
                     pl.BlockSpec((k, d), functools.partial(lambda i, r: (r, 0), r=r or 0),
                                  pipeline_mode=pl.Buffered(1))]
        args += [a, w]
    row = pl.BlockSpec((tm, d), lambda i: (i, 0))
    vec = pl.BlockSpec((1, d), lambda i: (0, 0))
    extra_in, extra_out, extra_shape = [], [], []
    if next_w is not None:
        e = next_w.shape[0]
        extra_in = [pl.BlockSpec((e, d), lambda i: (0, 0), pipeline_mode=pl.Buffered(1))]
        extra_out = [pl.BlockSpec((tm, e), lambda i: (i, 0))]
        extra_shape = [jax.ShapeDtypeStruct((m, e), F32)]
        args_tail = (next_w,)
    else:
        args_tail = ()
    return pl.pallas_call(
        body, name=name, grid=(n_steps,),
        in_specs=in_specs + [row, vec, row] + [pl.BlockSpec(memory_space=pl.ANY)] * len(after) + extra_in,
        out_specs=[row, vec] + extra_out,
        out_shape=[jax.ShapeDtypeStruct((m, d), F32), jax.ShapeDtypeStruct((1, d), F32)] + extra_shape,
        compiler_params=_params(1),
    )(*args, x, g, res, *after, *args_tail)


def _exchange_start(grad, name):
    def body(g_ref, land_ref, send_sem, recv_sem, local_sem, g_thru, land_thru, token):
        del g_thru, land_thru
        x, y, c = _mesh_place()
        me = 4 * x + 2 * y + c
        pltpu.make_async_copy(g_ref.at[me], land_ref.at[me], local_sem).start()
        for k in range(1, N_DEV):
            peer, pidx = _peer(x, y, c, k)
            _remote(g_ref.at[pidx], land_ref.at[me], send_sem, recv_sem, peer).start()
        token[...] = jnp.zeros_like(token)

    hbm = pl.BlockSpec(memory_space=pltpu.HBM)
    sem = pl.BlockSpec(memory_space=pltpu.SEMAPHORE)
    buf = pltpu.HBM(grad.shape, grad.dtype)
    return pl.pallas_call(
        body, name=name,
        out_shape=(pltpu.SemaphoreType.DMA(()), pltpu.SemaphoreType.DMA(()), pltpu.SemaphoreType.DMA(()), buf, buf,
                   jax.ShapeDtypeStruct((8, LANES), F32)),
        in_specs=(hbm, hbm), out_specs=(sem, sem, sem, hbm, hbm, pl.BlockSpec(memory_space=pltpu.VMEM)),
        input_output_aliases={0: 3, 1: 4},
        compiler_params=pltpu.CompilerParams(has_side_effects=pltpu.SideEffectType.DATAFLOW_SIDE_EFFECTING),
    )(pltpu.with_memory_space_constraint(grad, pltpu.HBM),
      pltpu.with_memory_space_constraint(lax.empty(grad.shape, grad.dtype), pltpu.HBM))


def _exchange_wait(send_sem, recv_sem, local_sem, grad, land, after, name):
    def body(g_ref, land_ref, send_sem, recv_sem, local_sem, *rest):
        x, y, c = _mesh_place()
        peer, _ = _peer(x, y, c, 1)
        others = pl.ds(0, N_DEV - 1)
        seven = _remote(g_ref.at[others], land_ref.at[others], send_sem, recv_sem, peer)
        seven.wait_send()
        seven.wait_recv()
        pltpu.make_async_copy(g_ref.at[0], land_ref.at[0], local_sem).wait()

    hbm = pl.BlockSpec(memory_space=pltpu.HBM)
    sem = pl.BlockSpec(memory_space=pltpu.SEMAPHORE)
    buf = pltpu.HBM(grad.shape, grad.dtype)
    return pl.pallas_call(
        body, name=name, out_shape=(buf, buf),
        in_specs=(hbm, hbm, sem, sem, sem) + (pl.BlockSpec(memory_space=pl.ANY),) * len(after), out_specs=(hbm, hbm),
        input_output_aliases={0: 0, 1: 1},
        compiler_params=pltpu.CompilerParams(has_side_effects=pltpu.SideEffectType.DATAFLOW_SIDE_EFFECTING),
    )(grad, land, send_sem, recv_sem, local_sem, *after)[1]


SMALL_LAYOUT = ((0, 0, 0, 0, D_MODEL), (1, 0, 1, 0, GROUP), (1, 1, 1, GROUP, GROUP), (2, 0, 2, 0, GROUP),
                (3, 0, 2, GROUP, GROUP), (4, 0, 3, 0, D_MODEL), (5, 0, 4, 0, D_MODEL))
LOSS_ROW = 5
N_SMALL = 6


def _place_small(dst, srcs):
    dst[...] = jnp.zeros_like(dst)
    for p, sr, dr, dc, width in SMALL_LAYOUT:
        dst[dr:dr + 1, dc:dc + width] = srcs[p][sr:sr + 1, :]


def _small_allreduce(grads, loss_part):
    def body(*refs):
        g_in, loss_in, sum_ref = refs[:N_SMALL], refs[N_SMALL], refs[N_SMALL + 1]
        pack, land, send_sems, recv_sems = refs[N_SMALL + 2:]
        _place_small(pack, g_in)
        pack[LOSS_ROW:LOSS_ROW + 1, 0:LANES] = loss_in[...]

        x, y, c = _mesh_place()
        me = 4 * x + 2 * y + c
        land[me] = pack[...]
        sends = []
        for k in range(1, N_DEV):
            peer, _ = _peer(x, y, c, k)
            cp = _remote(pack, land.at[me], send_sems.at[k - 1], recv_sems.at[k - 1], peer)
            cp.start()
            sends.append(cp)
        for k in range(1, N_DEV):
            peer, pidx = _peer(x, y, c, k)
            _remote(pack, land.at[pidx], send_sems.at[k - 1], recv_sems.at[k - 1], peer).wait_recv()
        for cp in sends:
            cp.wait_send()

        g = land[0]
        for d in range(1, N_DEV):
            g = g + land[d]
        sum_ref[...] = g

    args = [*grads, loss_part]
    packed = jax.ShapeDtypeStruct((SMALL_ROWS, D_MODEL), F32)
    return pl.pallas_call(
        body, name="small_allreduce", grid=(1,),
        in_specs=[_whole(a) for a in args], out_specs=_whole(packed), out_shape=packed,
        scratch_shapes=[pltpu.VMEM(packed.shape, F32), pltpu.VMEM((N_DEV,) + packed.shape, F32),
                        pltpu.SemaphoreType.DMA((N_DEV - 1,)), pltpu.SemaphoreType.DMA((N_DEV - 1,))],
        compiler_params=_params(1),
    )(*args)


def _small_adamw(gsum, ws, ms, vs):
    def body(*refs):
        g_ref = refs[0]
        params = [refs[1 + q * N_SMALL:1 + (q + 1) * N_SMALL] for q in range(3)]
        o0 = 1 + 3 * N_SMALL
        outs = [refs[o0 + q * N_SMALL:o0 + (q + 1) * N_SMALL] for q in range(4)]
        loss_out = refs[o0 + 4 * N_SMALL]
        wp, mp, vp = refs[o0 + 4 * N_SMALL + 1:]
        for dst, srcs in zip((wp, mp, vp), params):
            _place_small(dst, srcs)
        g = g_ref[...]
        delta, nm, nv = _adam(wp[...], g, mp[...], vp[...])
        for val, out in zip((g, delta, nm, nv), outs):
            for p, sr, dr, dc, width in SMALL_LAYOUT:
                out[p][sr:sr + 1, :] = val[dr:dr + 1, dc:dc + width]
        loss_out[...] = g[LOSS_ROW:LOSS_ROW + 1, 0:LANES]

    args = [gsum, *ws, *ms, *vs]
    shapes = [jax.ShapeDtypeStruct(w.shape, F32) for w in ws] * 4 + [jax.ShapeDtypeStruct((1, LANES), F32)]
    packed = pltpu.VMEM((SMALL_ROWS, D_MODEL), F32)
    outs = pl.pallas_call(
        body, name="small_adamw", grid=(1,),
        in_specs=[_whole(a) for a in args], out_specs=[_whole(s) for s in shapes], out_shape=shapes,
        scratch_shapes=[packed, packed, packed], compiler_params=_params(1),
    )(*args)
    return [outs[q * N_SMALL:(q + 1) * N_SMALL] for q in range(4)], outs[4 * N_SMALL]


def _adam(w, g, m, v):
    m = ADAM_B1 * m + (1.0 - ADAM_B1) * g
    v = ADAM_B2 * v + (1.0 - ADAM_B2) * (g * g)
    m_hat = m / (1.0 - ADAM_B1 ** ADAM_STEP)
    v_hat = v / (1.0 - ADAM_B2 ** ADAM_STEP)
    delta = -ADAM_LR * (m_hat / (jnp.sqrt(v_hat) + ADAM_EPS) + ADAM_WD * w)
    return delta, m, v


def _reduce_adamw(land, w, m, v, name, tr, after=()):
    _, rows, width = land.shape

    def body(land_ref, w_ref, m_ref, v_ref, *rest):
        g_ref, d_ref, nm_ref, nv_ref = rest[len(after):]
        g = land_ref[0].astype(F32)
        for d in range(1, N_DEV):
            g = g + land_ref[d].astype(F32)
        delta, nm, nv = _adam(w_ref[...], g, m_ref[...], v_ref[...])
        g_ref[...] = g
        d_ref[...] = delta
        nm_ref[...] = nm
        nv_ref[...] = nv

    row = pl.BlockSpec((tr, width), lambda i: (i, 0))
    out = jax.ShapeDtypeStruct((rows, width), F32)
    return pl.pallas_call(
        body, name=name, grid=(rows // tr,),
        in_specs=[pl.BlockSpec((N_DEV, tr, width), lambda i: (0, i, 0)), row, row, row]
        + [pl.BlockSpec(memory_space=pl.ANY)] * len(after),
        out_specs=[row, row, row, row], out_shape=[out, out, out, out],
        compiler_params=_params(1),
    )(land, w, m, v, *after)


def kernel(x, mix_norm_g, w_in, lower_bounds, hgrn_norm_g, sb_norm_g, w_out, ffn_norm_g, w_gate, w_up, w_down, final_norm_g, loss_target, m_mix_norm_g, m_w_in, m_lower_bounds, m_hgrn_norm_g, m_sb_norm_g, m_w_out, m_ffn_norm_g, m_w_gate, m_w_up, m_w_down, m_final_norm_g, v_mix_norm_g, v_w_in, v_lower_bounds, v_hgrn_norm_g, v_sb_norm_g, v_w_out, v_ffn_norm_g, v_w_gate, v_w_up, v_w_down, v_final_norm_g):
    n_seq, seq, d = x.shape
    t = n_seq * seq
    x2d = x.reshape(t, d)
    tgt = loss_target.reshape(t, d)
    final_g = final_norm_g.reshape(1, d)
    col_sharded = (True, False, True, True, False)

    def as_rows(ws):
        return [w[0].T if tr else w[0] for w, tr in zip(ws, col_sharded)]

    big_w = as_rows([w_in, w_out, w_gate, w_up, w_down])
    big_m = as_rows([m_w_in, m_w_out, m_w_gate, m_w_up, m_w_down])
    big_v = as_rows([v_w_in, v_w_out, v_w_gate, v_w_up, v_w_down])

    sh_in, sh_out, sh_gate, sh_up, sh_down = _cast_shards(big_w)
    wt_in, h1 = _gather_w_in(sh_in, x2d, mix_norm_g)
    wt_in = wt_in.reshape(IN_COLS, d)
    proj_h = _mm_nt(h1, wt_in, "proj_hgrn", rows=(0, 4 * GROUP), tm=1024, tk=4 * GROUP)
    proj_s = _mm_nt(h1, wt_in, "proj_sb", rows=(4 * GROUP, 3 * GROUP), out_dtype=MXU_DTYPE, tm=1024, tk=512)
    mixed, oa_pre, states = _hgrn_fwd(proj_h, lower_bounds, hgrn_norm_g, n_seq, seq)
    mixed, ob_pre, ctot, gathered = _sb_fwd(proj_s, sb_norm_g, mixed, [sh_out, sh_gate, sh_up, sh_down], n_seq, seq)
    wf_out = gathered[0].reshape(d, d)
    wt_gate = gathered[1].reshape(D_FF, d)
    wt_up = gathered[2].reshape(D_FF, d)
    wf_down = gathered[3].reshape(D_FF, d)
    x1, h2 = _mix_out_norm(mixed, wf_out, x2d, ffn_norm_g, "mix_out")
    gate, up, ff = _ffn_up(h2, wt_gate, wt_up, "ffn_up")
    dx2, dx2m, d_final_g, loss_part = _ffn_down_loss(ff, wf_down, x1, tgt, final_g, "ffn_down_loss")

    dgate, dup = _ffn_bwd_act(dx2m, wf_down, gate, up, "ffn_bwd_act")
    dw_down = _mm_tn(ff, dx2m, "dw_down", tk=1408, tn=1024).reshape(N_DEV, D_FF // N_DEV, d)
    dw_gate = _mm_tn(dgate, h2, "dw_gate", tk=1408, tn=1024).reshape(N_DEV, D_FF // N_DEV, d)
    dw_up = _mm_tn(dup, h2, "dw_up", tk=1408, tn=1024).reshape(N_DEV, D_FF // N_DEV, d)
    dx1, d_ffn_g, dmix = _dh_norm_bwd([(dgate, wt_gate, None), (dup, wt_up, None)], x1, ffn_norm_g, dx2, "dh_ffn",
                                      tm=256, next_w=wf_out)
    dw_out = _mm_tn(mixed, dx1, "dw_out", tk=1024, tn=1024).reshape(N_DEV, d // N_DEV, d)
    dsq, dsk, dsv, d_sb_g, lands = _sb_bwd(proj_s, sb_norm_g, dmix, ob_pre, ctot, [dw_out, dw_gate, dw_up, dw_down],
                                            n_seq, seq)
    dhq, dhf, dhi, dhg, d_lb, d_hgrn_g = _hgrn_bwd(proj_h, lower_bounds, hgrn_norm_g, dmix, oa_pre, states, n_seq,
                                                   seq)
    dproj = [dhq, dhf, dhi, dhg, dsq, dsk, dsv]
    dw_in = _dw_rows(dproj, h1, "dw_in").reshape(N_DEV, IN_COLS // N_DEV, d)
    send_sem, recv_sem, local_sem, dw_in, land_in, token = _exchange_start(dw_in, "dw_in_send")
    dx, d_mix_g = _dh_norm_bwd([(piece, wt_in, k) for k, piece in enumerate(dproj)], x2d, mix_norm_g, dx1, "dh_mix",
                               after=(token,))

    tiles = {"in": 224, "out": 128, "gate": 176, "up": 176, "down": 176}
    keys = list(tiles)
    rest = [_reduce_adamw(land, w, m, v, "adamw_" + key, tr=tiles[key], after=(token,))
            for key, land, w, m, v in zip(keys[1:], lands, big_w[1:], big_m[1:], big_v[1:])]
    small_sum = _small_allreduce([d_mix_g, d_lb, d_hgrn_g, d_sb_g, d_ffn_g, d_final_g], loss_part)
    small, loss_row = _small_adamw(
        small_sum, [mix_norm_g, lower_bounds, hgrn_norm_g, sb_norm_g, ffn_norm_g, final_g],
        [m_mix_norm_g, m_lower_bounds, m_hgrn_norm_g, m_sb_norm_g, m_ffn_norm_g, m_final_norm_g.reshape(1, d)],
        [v_mix_norm_g, v_lower_bounds, v_hgrn_norm_g, v_sb_norm_g, v_ffn_norm_g, v_final_norm_g.reshape(1, d)])
    land_in = _exchange_wait(send_sem, recv_sem, local_sem, dw_in, land_in,
                             [dx, loss_row] + [res[0] for res in rest], "dw_in_await")
    big = [_reduce_adamw(land_in, big_w[0], big_m[0], big_v[0], "adamw_in", tr=tiles["in"])] + rest
    big = [[r.T if tr else r for r in res] for res, tr in zip(big, col_sharded)]

    outs = [loss_row[0, 0], dx.reshape(n_seq, seq, d)]
    for q in range(4):
        b_in, b_out, b_gate, b_up, b_down = [res[q][None] for res in big]
        s_mix, s_lb, s_hgrn, s_sb, s_ffn, s_final = small[q]
        outs += [s_mix, b_in, s_lb, s_hgrn, s_sb, b_out, s_ffn, b_gate, b_up, b_down, s_final.reshape(d)]
    return tuple(outs)
```

```python
import functools
import math

import jax
import jax.numpy as jnp
from jax import lax
from jax.experimental import pallas as pl
from jax.experimental.pallas import tpu as pltpu

F32 = jnp.float32
BF16 = jnp.bfloat16
MXU_DTYPE = BF16

EPS = 1e-6
D_MODEL = 1024
N_HEADS = 8
D_HEAD = 64
GROUP = N_HEADS * D_HEAD
IN_COLS = 7 * GROUP
D_FF = 2816
CHUNK = 64
LANES = 128
N_PAIRS = GROUP // LANES
SUPER = 256
SB_BLOCK = 256
N_DEV = 8

ADAM_LR = 0.001
ADAM_B1 = 0.9
ADAM_B2 = 0.999
ADAM_EPS = 1e-08
ADAM_WD = 0.01
ADAM_STEP = 10

SMALL_ROWS = 8
FF_TILE = D_FF // 2

VMEM_LIMIT = 32 * 1024 * 1024


def _params(n_axes, vmem=VMEM_LIMIT):
    return pltpu.CompilerParams(dimension_semantics=("arbitrary",) * n_axes, vmem_limit_bytes=vmem)


def _whole(a):
    return pl.BlockSpec(a.shape, functools.partial(lambda i, nd: (0,) * nd, nd=len(a.shape)))


def _dot(a, b):
    return jnp.dot(a.astype(MXU_DTYPE), b.astype(MXU_DTYPE), preferred_element_type=F32)


def _dot_nt(a, b):
    return lax.dot_general(a.astype(MXU_DTYPE), b.astype(MXU_DTYPE), (((1,), (1,)), ((), ())),
                           preferred_element_type=F32)


def _dot_tn(a, b):
    return lax.dot_general(a.astype(MXU_DTYPE), b.astype(MXU_DTYPE), (((0,), (0,)), ((), ())),
                           preferred_element_type=F32)


def _split(x, parts):
    out, r = [], x
    for _ in range(parts):
        h = r.astype(BF16)
        out.append(h)
        r = r - h.astype(F32)
    return out


def _rsum_left(u, x, parts):
    acc = None
    for h in _split(x, parts):
        d = jnp.dot(u, h, preferred_element_type=F32)
        acc = d if acc is None else acc + d
    return acc


def _ones_where(mask):
    return jnp.where(mask, 1.0, 0.0).astype(BF16)


def _sigmoid(x):
    return 1.0 / (1.0 + jnp.exp(-x))


def _softplus(x):
    return jnp.maximum(x, 0.0) + jnp.log(1.0 + jnp.exp(-jnp.abs(x)))


def _head_masks():
    lane = lax.broadcasted_iota(jnp.int32, (1, LANES), 1)
    return [jnp.where(lane < D_HEAD, 1.0, 0.0), jnp.where(lane >= D_HEAD, 1.0, 0.0)]


def _head_rstd(o, masks):
    sq = o * o
    r = None
    for m in masks:
        ms = jnp.sum(sq * m, axis=1, keepdims=True) * (1.0 / D_HEAD)
        t = lax.rsqrt(ms + EPS) * m
        r = t if r is None else r + t
    return r


def _head_mean(t, masks):
    out = None
    for m in masks:
        v = jnp.sum(t * m, axis=1, keepdims=True) * (1.0 / D_HEAD) * m
        out = v if out is None else out + v
    return out


def _mix_out_norm(a, w, res, g, name, tm=512):
    m, k = a.shape
    d = w.shape[1]
    tm = min(tm, m)

    def body(a_ref, w_ref, res_ref, g_ref, x_ref, h_ref):
        xv = res_ref[...] + _dot(a_ref[...], w_ref[...])
        x_ref[...] = xv
        r = lax.rsqrt(jnp.mean(xv * xv, axis=-1, keepdims=True) + EPS)
        h_ref[...] = (xv * r * g_ref[...]).astype(h_ref.dtype)

    row = pl.BlockSpec((tm, d), lambda i: (i, 0))
    return pl.pallas_call(
        body, name=name, grid=(m // tm,),
        in_specs=[pl.BlockSpec((tm, k), lambda i: (i, 0)), pl.BlockSpec((k, d), lambda i: (0, 0)), row,
                  pl.BlockSpec((1, d), lambda i: (0, 0))],
        out_specs=[row, row],
        out_shape=[jax.ShapeDtypeStruct((m, d), F32), jax.ShapeDtypeStruct((m, d), MXU_DTYPE)],
        compiler_params=_params(1),
    )(a, w, res, g)


def _ffn_down_loss(a, w, res, target, g, name, tm=512):
    m, k = a.shape
    d = w.shape[1]
    tm = min(tm, m)

    def body(a_ref, w_ref, res_ref, t_ref, g_ref, dx_ref, dxm_ref, dg_ref, loss_ref):
        xv = res_ref[...] + _dot(a_ref[...], w_ref[...])
        gv = g_ref[...]
        r = lax.rsqrt(jnp.mean(xv * xv, axis=-1, keepdims=True) + EPS)
        xh = xv * r
        e = xh * gv - t_ref[...]
        dy = e * (1.0 / d)
        dxh = dy * gv
        dxv = r * (dxh - xh * jnp.mean(dxh * xh, axis=-1, keepdims=True))
        dx_ref[...] = dxv
        dxm_ref[...] = dxv.astype(dxm_ref.dtype)

        @pl.when(pl.program_id(0) == 0)
        def _():
            dg_ref[...] = jnp.zeros_like(dg_ref)
            loss_ref[...] = jnp.zeros_like(loss_ref)

        dg_ref[...] += jnp.sum(dy * xh, axis=0, keepdims=True)
        part = 0.5 * jnp.sum(jnp.mean(e * e, axis=-1, keepdims=True), axis=0, keepdims=True)
        loss_ref[...] += jnp.broadcast_to(part, loss_ref.shape)

    row = pl.BlockSpec((tm, d), lambda i: (i, 0))
    vec = pl.BlockSpec((1, d), lambda i: (0, 0))
    return pl.pallas_call(
        body, name=name, grid=(m // tm,),
        in_specs=[pl.BlockSpec((tm, k), lambda i: (i, 0)), pl.BlockSpec((k, d), lambda i: (0, 0)), row, row, vec],
        out_specs=[row, row, vec, pl.BlockSpec((1, LANES), lambda i: (0, 0))],
        out_shape=[jax.ShapeDtypeStruct((m, d), F32), jax.ShapeDtypeStruct((m, d), MXU_DTYPE),
                   jax.ShapeDtypeStruct((1, d), F32), jax.ShapeDtypeStruct((1, LANES), F32)],
        compiler_params=_params(1),
    )(a, w, res, target, g)


def _mm_nt(a, w, name, rows=None, out_dtype=F32, tm=512, tk=512):
    m, n = a.shape
    row0, k = rows or (0, w.shape[0])
    tm, tk = min(tm, m), min(tk, k)
    first = row0 // tk

    def body(a_ref, w_ref, o_ref):
        o_ref[...] = _dot_nt(a_ref[...], w_ref[...]).astype(o_ref.dtype)

    return pl.pallas_call(
        body, name=name, grid=(k // tk, m // tm),
        in_specs=[pl.BlockSpec((tm, n), lambda j, i: (i, 0)), pl.BlockSpec((tk, n), lambda j, i: (first + j, 0))],
        out_specs=pl.BlockSpec((tm, tk), lambda j, i: (i, j)),
        out_shape=jax.ShapeDtypeStruct((m, k), out_dtype),
        compiler_params=_params(2),
    )(a, w)


def _mm_tn(a, b, name, tk, tn, tt=1024, out_dtype=BF16):
    t, k = a.shape
    n = b.shape[1]
    tt = min(tt, t)
    steps = t // tt

    def body(a_ref, b_ref, o_ref, acc):
        s = pl.program_id(2)

        @pl.when(s == 0)
        def _():
            acc[...] = jnp.zeros_like(acc)

        acc[...] += _dot_tn(a_ref[...], b_ref[...])

        @pl.when(s == steps - 1)
        def _():
            o_ref[...] = acc[...].astype(o_ref.dtype)

    return pl.pallas_call(
        body, name=name, grid=(k // tk, n // tn, steps),
        in_specs=[pl.BlockSpec((tt, tk), lambda i, j, s: (s, i)), pl.BlockSpec((tt, tn), lambda i, j, s: (s, j))],
        out_specs=pl.BlockSpec((tk, tn), lambda i, j, s: (i, j)),
        out_shape=jax.ShapeDtypeStruct((k, n), out_dtype),
        scratch_shapes=[pltpu.VMEM((tk, tn), F32)],
        compiler_params=_params(3),
    )(a, b)


def _ffn_up(h, wg_t, wu_t, name, tm=1024, tn=FF_TILE):
    m, k = h.shape
    n = wg_t.shape[0]
    tm = min(tm, m)

    def body(h_ref, wg_ref, wu_ref, gate_ref, up_ref, ff_ref):
        hv = h_ref[...]
        gate = _dot_nt(hv, wg_ref[...])
        up = _dot_nt(hv, wu_ref[...])
        gate_ref[...] = gate.astype(gate_ref.dtype)
        up_ref[...] = up.astype(up_ref.dtype)
        ff_ref[...] = (gate * _sigmoid(gate) * up).astype(ff_ref.dtype)

    wspec = pl.BlockSpec((tn, k), lambda j, i: (j, 0))
    ospec = pl.BlockSpec((tm, tn), lambda j, i: (i, j))
    return pl.pallas_call(
        body, name=name, grid=(n // tn, m // tm),
        in_specs=[pl.BlockSpec((tm, k), lambda j, i: (i, 0)), wspec, wspec],
        out_specs=[ospec, ospec, ospec],
        out_shape=[jax.ShapeDtypeStruct((m, n), MXU_DTYPE)] * 3,
        compiler_params=_params(2),
    )(h, wg_t, wu_t)


def _ffn_bwd_act(dx, wd, gate, up, name, tm=1024, tn=FF_TILE):
    m, k = dx.shape
    n = wd.shape[0]
    tm = min(tm, m)

    def body(dx_ref, wd_ref, gate_ref, up_ref, dgate_ref, dup_ref):
        dff = _dot_nt(dx_ref[...], wd_ref[...])
        gate = gate_ref[...].astype(F32)
        sg = _sigmoid(gate)
        dgate_ref[...] = (dff * up_ref[...].astype(F32) * sg * (1.0 + gate * (1.0 - sg))).astype(dgate_ref.dtype)
        dup_ref[...] = (dff * gate * sg).astype(dup_ref.dtype)

    ospec = pl.BlockSpec((tm, tn), lambda j, i: (i, j))
    return pl.pallas_call(
        body, name=name, grid=(n // tn, m // tm),
        in_specs=[pl.BlockSpec((tm, k), lambda j, i: (i, 0)), pl.BlockSpec((tn, k), lambda j, i: (j, 0)),
                  ospec, ospec],
        out_specs=[ospec, ospec],
        out_shape=[jax.ShapeDtypeStruct((m, n), MXU_DTYPE), jax.ShapeDtypeStruct((m, n), MXU_DTYPE)],
        compiler_params=_params(2),
    )(dx, wd, gate, up)


def _chunk_masks():
    r = lax.broadcasted_iota(jnp.int32, (SUPER, SUPER), 0)
    c = lax.broadcasted_iota(jnp.int32, (SUPER, SUPER), 1)
    same = jnp.right_shift(r, 6) == jnp.right_shift(c, 6)
    lower = jnp.logical_and(same, c <= r)
    upper = jnp.logical_and(same, c >= r)
    return same, lower, upper


def _head_block_mask():
    r = lax.broadcasted_iota(jnp.int32, (LANES, LANES), 0)
    c = lax.broadcasted_iota(jnp.int32, (LANES, LANES), 1)
    return jnp.where(jnp.right_shift(r, 6) == jnp.right_shift(c, 6), 1.0, 0.0)


def _lower_bound(lb_raw):
    return 1.0 / (1.0 + jnp.exp(lb_raw[1:2, :] - lb_raw[0:1, :]))


HGRN_UNROLL = 4
PER_SUPER = SUPER // CHUNK
CHUNK_ROWS = [slice(c * CHUNK, (c + 1) * CHUNK) for c in range(PER_SUPER)]


def _over_chunks(rows):
    return jnp.concatenate([jnp.broadcast_to(r, (CHUNK, LANES)) for r in rows], axis=0)


def _hgrn_gates(q, hf, lb, lower_b):
    sig = _sigmoid(hf)
    f = lb + (1.0 - lb) * sig
    k = 1.0 - f
    lf = jnp.log(f)
    b = _rsum_left(lower_b, lf, 2)
    ends = [b[cr.stop - 1:cr.stop, :] for cr in CHUNK_ROWS]
    eb = jnp.exp(b)
    enb = jnp.exp(-b)
    edb = jnp.exp(_over_chunks(ends) - b)
    decs = [jnp.exp(e) for e in ends]
    return sig, f, k, decs, eb, enb, edb, q * eb, k * enb, k * edb


def _hgrn_fwd(proj, lower_bounds, norm_g, n_seq, seq):
    t = n_seq * seq
    n_super = seq // SUPER
    n_chunks = seq // CHUNK

    def body(q_ref, f_ref, i_ref, g_ref, lb_ref, ng_ref, out_ref, opre_ref, st_ref):
        masks = _head_masks()
        _, lower, _ = _chunk_masks()
        lower_b = _ones_where(lower)
        bd = _head_block_mask()
        lb = _lower_bound(lb_ref[...])
        ng = ng_ref[...]

        def step(it, st):
            blocks = [HGRN_UNROLL * it + u for u in range(HGRN_UNROLL)]
            rows = [pl.ds(pl.multiple_of(sb * SUPER, SUPER), SUPER) for sb in blocks]
            vs = [i_ref[rw, :] for rw in rows]
            gates = [_hgrn_gates(q_ref[rw, :], f_ref[rw, :], lb, lower_b) for rw in rows]
            decs, qes, kes, kds = ([g[k] for g in gates] for k in (3, 7, 8, 9))
            scores = [[_dot_nt(qe * m, ke) for m in masks] for qe, ke in zip(qes, kes)]
            updates = [[_dot_tn(v[cr], kd[cr]) for cr in CHUNK_ROWS] for v, kd in zip(vs, kds)]
            states = [st]
            for dec_b, upd_b in zip(decs, updates):
                for dec, upd in zip(dec_b, upd_b):
                    states.append(states[-1] * dec + bd * upd)
            for u, sb in enumerate(blocks):
                for c in range(PER_SUPER):
                    st_ref[0, 0, sb * PER_SUPER + c] = states[u * PER_SUPER + c]
            intra = [[_dot(jnp.where(lower, p, 0.0), v) for p in sc] for sc, v in zip(scores, vs)]
            inter = [[_dot_nt(qe[cr], states[u * PER_SUPER + c]) for c, cr in enumerate(CHUNK_ROWS)]
                     for u, qe in enumerate(qes)]
            for rw, intra_b, inter_b in zip(rows, intra, inter):
                o = intra_b[0] * masks[0] + intra_b[1] * masks[1] + jnp.concatenate(inter_b, axis=0)
                opre_ref[rw, :] = o
                hg = g_ref[rw, :]
                on = o * _head_rstd(o, masks) * ng
                out_ref[rw, :] = (on * hg * _sigmoid(hg)).astype(out_ref.dtype)
            return states[-1]

        lax.fori_loop(0, n_super // HGRN_UNROLL, step, jnp.zeros((LANES, LANES), F32))

    def col(k):
        return pl.BlockSpec((seq, LANES), lambda p, b: (b, k * N_PAIRS + p))

    vec = lambda rows: pl.BlockSpec((rows, LANES), lambda p, b: (0, p))
    ospec = pl.BlockSpec((seq, LANES), lambda p, b: (b, p))
    return pl.pallas_call(
        body, name="hgrn_fwd", grid=(N_PAIRS, n_seq),
        in_specs=[col(0), col(1), col(2), col(3), vec(2), vec(1)],
        out_specs=[ospec, ospec,
                   pl.BlockSpec((1, 1, n_chunks, LANES, LANES), lambda p, b: (b, p, 0, 0, 0))],
        out_shape=[jax.ShapeDtypeStruct((t, 2 * GROUP), MXU_DTYPE), jax.ShapeDtypeStruct((t, GROUP), F32),
                   jax.ShapeDtypeStruct((n_seq, N_PAIRS, n_chunks, LANES, LANES), F32)],
        compiler_params=_params(2),
    )(proj, proj, proj, proj, lower_bounds, norm_g)


def _hgrn_bwd(proj, lower_bounds, norm_g, dmix, opre, states, n_seq, seq):
    t = n_seq * seq
    n_super = seq // SUPER
    n_chunks = seq // CHUNK
    per = SUPER // CHUNK

    def body(q_ref, f_ref, i_ref, g_ref, lb_ref, ng_ref, dm_ref, opre_ref, st_ref,
             dq_ref, df_ref, di_ref, dg_ref, dlb_ref, dng_ref):
        masks = _head_masks()
        _, lower, upper = _chunk_masks()
        lower_b, upper_b = _ones_where(lower), _ones_where(upper)
        bd = _head_block_mask()
        lb_raw = lb_ref[...]
        lb = _lower_bound(lb_raw)
        ng = ng_ref[...]

        @pl.when(pl.program_id(1) == 0)
        def _():
            dlb_ref[...] = jnp.zeros_like(dlb_ref)
            dng_ref[...] = jnp.zeros_like(dng_ref)

        def first_half(sb):
            rows = pl.ds(pl.multiple_of(sb * SUPER, SUPER), SUPER)
            q, hf, v, hg = q_ref[rows, :], f_ref[rows, :], i_ref[rows, :], g_ref[rows, :]
            sig, f, k, decs, eb, enb, edb, qe, ke, kd = _hgrn_gates(q, hf, lb, lower_b)
            o = opre_ref[rows, :]
            r = _head_rstd(o, masks)
            oh = o * r
            dm = dm_ref[rows, :]
            sg = _sigmoid(hg)
            dg_ref[rows, :] = (dm * oh * ng * sg * (1.0 + hg * (1.0 - sg))).astype(dg_ref.dtype)
            don = dm * hg * sg
            dng_ref[...] += jnp.sum(don * oh, axis=0, keepdims=True)
            doh = don * ng
            do = r * (doh - oh * _head_mean(doh * oh, masks))
            doms = [do * m for m in masks]
            qems = [qe * m for m in masks]
            scores = [_dot_nt(qem, ke) for qem in qems]
            dscores = [_dot_nt(dom, v) for dom in doms]
            prevs = [st_ref[0, 0, sb * per + c] for c in range(per)]
            dst_in = [_dot_tn(do[cr], qe[cr]) for cr in CHUNK_ROWS]
            dqe_i = [_dot(do[cr], prev) for cr, prev in zip(CHUNK_ROWS, prevs)]
            return dict(rows=rows, v=v, sig=sig, f=f, decs=decs, eb=eb, enb=enb, edb=edb, qe=qe, ke=ke, kd=kd,
                        doms=doms, qems=qems, scores=scores, dscores=dscores, prevs=prevs, dst_in=dst_in, dqe_i=dqe_i)

        def second_half(blk, dsts):
            v, qe, ke, kd = blk["v"], blk["qe"], blk["ke"], blk["kd"]
            ps = [jnp.where(lower, p, 0.0) for p in blk["scores"]]
            dps = [jnp.where(lower, dp, 0.0) for dp in blk["dscores"]]
            dqe_h = [_dot(dp, ke) for dp in dps]
            dke_h = [_dot_tn(dp, qem) for dp, qem in zip(dps, blk["qems"])]
            dv_h = [_dot_tn(p, dom) for p, dom in zip(ps, blk["doms"])]
            dus = [bd * d for d in dsts]
            dv_i = [_dot_nt(kd[cr], du) for cr, du in zip(CHUNK_ROWS, dus)]
            dkd_i = [_dot(v[cr], du) for cr, du in zip(CHUNK_ROWS, dus)]

            def finish():
                dqe = dqe_h[0] * masks[0] + dqe_h[1] * masks[1] + jnp.concatenate(blk["dqe_i"], axis=0)
                dke = dke_h[0] + dke_h[1]
                dv = dv_h[0] + dv_h[1] + jnp.concatenate(dv_i, axis=0)
                dkd = jnp.concatenate(dkd_i, axis=0)
                dk = dke * blk["enb"] + dkd * blk["edb"]
                db = dqe * qe - dke * ke - dkd * kd
                dkd_kd = dkd * kd
                dends = [jnp.sum(dkd_kd[cr], axis=0, keepdims=True)
                         + jnp.sum(dsts[c] * blk["prevs"][c], axis=0, keepdims=True) * blk["decs"][c]
                         for c, cr in enumerate(CHUNK_ROWS)]
                dlf = _rsum_left(upper_b, db, 2) + _over_chunks(dends)
                sig = blk["sig"]
                dfv = dlf / blk["f"] - dk
                rows = blk["rows"]
                dq_ref[rows, :] = (dqe * blk["eb"]).astype(dq_ref.dtype)
                di_ref[rows, :] = dv.astype(di_ref.dtype)
                df_ref[rows, :] = (dfv * (1.0 - lb) * sig * (1.0 - sig)).astype(df_ref.dtype)
                dlb = jnp.sum(dfv * (1.0 - sig), axis=0, keepdims=True)
                da0 = dlb * lb * (1.0 - lb)
                dlb_ref[0:1, :] += da0
                dlb_ref[1:2, :] -= da0

            return finish

        def step(it, dst):
            blocks = [first_half(n_super - 1 - HGRN_UNROLL * it - u) for u in range(HGRN_UNROLL)]
            all_dsts = []
            for blk in blocks:
                dsts = [None] * per
                for c in reversed(range(per)):
                    dsts[c] = dst
                    dst = bd * (dst * blk["decs"][c] + blk["dst_in"][c])
                all_dsts.append(dsts)
            for finish in [second_half(blk, dsts) for blk, dsts in zip(blocks, all_dsts)]:
                finish()
            return dst

        lax.fori_loop(0, n_super // HGRN_UNROLL, step, jnp.zeros((LANES, LANES), F32))

    def col(k):
        return pl.BlockSpec((seq, LANES), lambda p, b: (b, k * N_PAIRS + p))

    vec = lambda rows: pl.BlockSpec((rows, LANES), lambda p, b: (0, p))
    ospec = pl.BlockSpec((seq, LANES), lambda p, b: (b, p))
    piece = jax.ShapeDtypeStruct((t, GROUP), MXU_DTYPE)
    return pl.pallas_call(
        body, name="hgrn_bwd", grid=(N_PAIRS, n_seq),
        in_specs=[col(0), col(1), col(2), col(3), vec(2), vec(1), ospec, ospec,
                  pl.BlockSpec((1, 1, n_chunks, LANES, LANES), lambda p, b: (b, p, 0, 0, 0))],
        out_specs=[ospec, ospec, ospec, ospec, vec(2), vec(1)],
        out_shape=[piece, piece, piece, piece,
                   jax.ShapeDtypeStruct((2, GROUP), F32), jax.ShapeDtypeStruct((1, GROUP), F32)],
        compiler_params=_params(2),
    )(proj, proj, proj, proj, lower_bounds, norm_g, dmix, opre, states)


SB_SCALE = 1.0 / math.sqrt(D_HEAD)


SB_STEP = 2 * SB_BLOCK
QUERY_BLOCKS = (slice(0, SB_BLOCK), slice(SB_BLOCK, SB_STEP))


def _triangle(keep):
    row = lax.broadcasted_iota(jnp.int32, (SB_BLOCK, SB_BLOCK), 0)
    col = lax.broadcasted_iota(jnp.int32, (SB_BLOCK, SB_BLOCK), 1)
    return _ones_where(keep(row, col))


SB_HALF = SB_BLOCK // 2


def _keep(on_diagonal, x):
    if not on_diagonal:
        return x
    row = lax.broadcasted_iota(jnp.int32, (SB_HALF, SB_HALF), 0)
    col = lax.broadcasted_iota(jnp.int32, (SB_HALF, SB_HALF), 1)
    return jnp.where(col < row, x, 0.0)


def _lower_quadrants(fn, n_out, *tiles):
    def quadrant(r, c):
        return [t[r * SB_HALF:(r + 1) * SB_HALF, c * SB_HALF:(c + 1) * SB_HALF] for t in tiles]

    top, left, bottom = fn(True, *quadrant(0, 0)), fn(False, *quadrant(1, 0)), fn(True, *quadrant(1, 1))
    zero = jnp.zeros((SB_HALF, SB_HALF), F32)
    return [jnp.concatenate([jnp.concatenate([top[o], zero], axis=1), jnp.concatenate([left[o], bottom[o]], axis=1)],
                            axis=0) for o in range(n_out)]


def _sb_fwd(proj, norm_g, mixed, shards, n_seq, seq):
    t = n_seq * seq
    nq = seq // SB_STEP
    q0, k0, v0 = 0, N_PAIRS, 2 * N_PAIRS
    n_w = len(shards)
    n_steps = N_PAIRS * n_seq * nq
    tri = _triangle(lambda row, col: row >= col)

    def body(q_ref, k_ref, v_ref, ng_ref, tri_ref, mixed_in, *rest):
        del mixed_in
        shard_refs = rest[:n_w]
        out_ref, opre_ref, ctot_ref = rest[n_w:n_w + 3]
        gathered = rest[n_w + 3:2 * n_w + 3]
        send_sems, recv_sems, local_sems = rest[2 * n_w + 3:]
        i = pl.program_id(2)
        step = (pl.program_id(0) * n_seq + pl.program_id(1)) * nq + i
        plan = _GatherPlan(shard_refs, gathered, send_sems, recv_sems, local_sems)

        @pl.when(step == 0)
        def _():
            plan.start()

        @pl.when(step == (3 * n_steps) // 4)
        def _():
            plan.forward()

        masks = _head_masks()
        suffix = tri_ref[...]
        qhs = [[(q_ref[blk, :] * SB_SCALE * m).astype(MXU_DTYPE) for m in masks] for blk in QUERY_BLOCKS]

        def tiles(js, work, carry):
            rows = [pl.ds(pl.multiple_of(j * SB_BLOCK, SB_BLOCK), SB_BLOCK) for j in js]
            ks = [k_ref[rw, :].astype(MXU_DTYPE) for rw in rows]
            vs = [v_ref[rw, :].astype(MXU_DTYPE) for rw in rows]
            zs = [[_dot_nt(qh, ks[ts]) for qh in qhs[qb]] for qb, ts, _ in work]
            ccs = [[_lower_quadrants(lambda on, zq: (_keep(on, _softplus(zq)),), 1, z)[0] if diag else _softplus(z)
                    for z in zw] for zw, (_, _, diag) in zip(zs, work)]
            sums = [[jnp.dot(cc.astype(BF16), suffix, preferred_element_type=F32) for cc in cw] for cw in ccs]
            out = [list(per_block) for per_block in carry]
            for h in range(len(masks)):
                for w, (qb, ts, diag) in enumerate(work):
                    run, acc = out[qb][h]
                    logit = zs[w][h] - (sums[w][h] + run)
                    if diag:
                        a = _lower_quadrants(lambda on, lq: (_keep(on, jnp.exp(lq)),), 1, logit)[0]
                    else:
                        a = jnp.exp(logit)
                    out[qb][h] = (run + sums[w][h][:, 0:1], acc + _dot(a, vs[ts]))
            return tuple(tuple(per_block) for per_block in out)

        start = ((jnp.zeros((SB_BLOCK, 1), F32), jnp.zeros((SB_BLOCK, LANES), F32)),) * 2
        carry = tiles([2 * i, 2 * i + 1], [(0, 0, True), (1, 1, True), (1, 0, False)], (start, start))
        both = [(0, 0, False), (0, 1, False), (1, 0, False), (1, 1, False)]
        carry = lax.fori_loop(0, i, lambda s, cy: tiles([2 * (i - s) - 1, 2 * (i - s) - 2], both, cy), carry)
        opre = jnp.concatenate([cb[0][1] * masks[0] + cb[1][1] * masks[1] for cb in carry], axis=0)
        ctot = jnp.concatenate([cb[0][0] * masks[0] + cb[1][0] * masks[1] for cb in carry], axis=0)
        opre_ref[...] = opre
        ctot_ref[...] = ctot
        out_ref[...] = (opre * _head_rstd(opre, masks) * ng_ref[...]).astype(out_ref.dtype)

        @pl.when(step == n_steps - 1)
        def _():
            plan.finish()

    qspec = pl.BlockSpec((SB_STEP, LANES), lambda p, b, i: (b * nq + i, q0 + p))
    ospec = pl.BlockSpec((SB_STEP, LANES), lambda p, b, i: (b * nq + i, p))
    hbm = pl.BlockSpec(memory_space=pltpu.HBM)
    outs = pl.pallas_call(
        body, name="sb_fwd", grid=(N_PAIRS, n_seq, nq),
        in_specs=[qspec,
                  pl.BlockSpec((seq, LANES), lambda p, b, i: (b, k0 + p)),
                  pl.BlockSpec((seq, LANES), lambda p, b, i: (b, v0 + p)),
                  pl.BlockSpec((1, LANES), lambda p, b, i: (0, p)),
                  pl.BlockSpec(tri.shape, lambda p, b, i: (0, 0)), hbm] + [hbm] * n_w,
        out_specs=[pl.BlockSpec((SB_STEP, LANES), lambda p, b, i: (b * nq + i, N_PAIRS + p)), ospec, ospec]
        + [hbm] * n_w,
        out_shape=[jax.ShapeDtypeStruct(mixed.shape, mixed.dtype), jax.ShapeDtypeStruct((t, GROUP), F32),
                   jax.ShapeDtypeStruct((t, GROUP), F32)]
        + [jax.ShapeDtypeStruct((N_DEV,) + s.shape, s.dtype) for s in shards],
        scratch_shapes=[pltpu.SemaphoreType.DMA((n_w * _GatherPlan.COPIES,)),
                        pltpu.SemaphoreType.DMA((n_w * _GatherPlan.COPIES,)), pltpu.SemaphoreType.DMA((n_w,))],
        input_output_aliases={5: 0},
        compiler_params=_params(3),
    )(proj, proj, proj, norm_g, tri, mixed, *shards)
    return outs[0], outs[1], outs[2], list(outs[3:])


def _sb_bwd(proj, norm_g, dmix, opre, ctot, grads, n_seq, seq):
    t = n_seq * seq
    nq = seq // SB_STEP
    q0, k0, v0 = 0, N_PAIRS, 2 * N_PAIRS
    n_w = len(grads)
    n_steps = N_PAIRS * n_seq * nq
    tri = _triangle(lambda row, col: row <= col)

    def body(q_ref, k_ref, v_ref, ng_ref, tri_ref, dm_ref, opre_ref, ctot_ref, *rest):
        grad_refs = rest[:n_w]
        dq_ref, dk_ref, dv_ref, dng_ref = rest[n_w:n_w + 4]
        lands = rest[n_w + 4:2 * n_w + 4]
        dk_acc, dv_acc, send_sems, recv_sems, local_sems = rest[2 * n_w + 4:]
        p_id, b_id, i = pl.program_id(0), pl.program_id(1), pl.program_id(2)
        step = (p_id * n_seq + b_id) * nq + i
        plan = _ScatterPlan(grad_refs, lands, send_sems, recv_sems, local_sems)

        @pl.when(step == 0)
        def _():
            plan.start()

        masks = _head_masks()
        upto = tri_ref[...]

        def prefix(x):
            return jnp.dot(x.astype(BF16), upto, preferred_element_type=F32)

        @pl.when(i == 0)
        def _():
            dk_acc[...] = jnp.zeros_like(dk_acc)
            dv_acc[...] = jnp.zeros_like(dv_acc)

        @pl.when(jnp.logical_and(b_id == 0, i == 0))
        def _():
            dng_ref[...] = jnp.zeros_like(dng_ref)

        o = opre_ref[...]
        rs = _head_rstd(o, masks)
        oh = o * rs
        dm = dm_ref[...]
        dng_ref[...] += jnp.sum(dm * oh, axis=0, keepdims=True)
        doh = dm * ng_ref[...]
        do = rs * (doh - oh * _head_mean(doh * oh, masks))

        heads = range(len(masks))
        qs = [q_ref[blk, :] * SB_SCALE for blk in QUERY_BLOCKS]
        dos = [do[blk] for blk in QUERY_BLOCKS]
        qhs = [[(q * m).astype(MXU_DTYPE) for m in masks] for q in qs]
        doms = [[(d * m).astype(MXU_DTYPE) for m in masks] for d in dos]
        head_rows = [jnp.where(jnp.right_shift(lax.broadcasted_iota(jnp.int32, (LANES, 1), 0), 6) == h, 1.0, 0.0)
                     for h in heads]
        qhts = [[(qt * hr).astype(MXU_DTYPE) for hr in head_rows] for qt in [q.astype(F32).T for q in qs]]
        domts = [[(dt * hr).astype(MXU_DTYPE) for hr in head_rows] for dt in [d.T for d in dos]]
        totals = [[ctot_ref[blk, h * D_HEAD:h * D_HEAD + 1] for h in heads] for blk in QUERY_BLOCKS]

        def tiles(js, work, carry):
            rows = [pl.ds(pl.multiple_of(j * SB_BLOCK, SB_BLOCK), SB_BLOCK) for j in js]
            ks = [k_ref[rw, :].astype(MXU_DTYPE) for rw in rows]
            vs = [v_ref[rw, :].astype(MXU_DTYPE) for rw in rows]
            zs = [[_dot_nt(qhs[qb][h], ks[ts]) for h in heads] for qb, ts, _ in work]
            das = [[_dot_nt(doms[qb][h], vs[ts]) for h in heads] for qb, ts, _ in work]
            def gates(on, zq):
                sp = _softplus(zq)
                return _keep(on, sp), zq - sp, _keep(on, jnp.exp(zq - sp))

            trio = [[_lower_quadrants(gates, 3, z) if diag else gates(False, z) for z in zw]
                    for zw, (_, _, diag) in zip(zs, work)]
            ccs, lsigs, sigs = ([[t[o] for t in tw] for tw in trio] for o in range(3))
            pres = [[prefix(cc) for cc in cw] for cw in ccs]
            out = [list(per_block) for per_block in carry]
            for h in heads:
                for w, (qb, ts, diag) in enumerate(work):
                    pc, pdl, dq_h = out[qb][h]
                    logit = lsigs[w][h] + pres[w][h] - (totals[qb][h] - pc)
                    if diag:
                        a = _lower_quadrants(lambda on, lq: (_keep(on, jnp.exp(lq)),), 1, logit)[0]
                    else:
                        a = jnp.exp(logit)
                    dl = a * das[w][h]
                    dv_acc[js[ts]] += _dot(domts[qb][h], a)
                    dpre = prefix(dl)
                    dz = dl - sigs[w][h] * (pdl + dpre)
                    dzb = dz.astype(MXU_DTYPE)
                    dk_acc[js[ts]] += _dot(qhts[qb][h], dzb)
                    out[qb][h] = (pc + pres[w][h][:, SB_BLOCK - 1:SB_BLOCK], pdl + dpre[:, SB_BLOCK - 1:SB_BLOCK],
                                  dq_h + _dot(dzb, ks[ts]))
            return tuple(tuple(per_block) for per_block in out)

        zero = jnp.zeros((SB_BLOCK, 1), F32)
        start = ((zero, zero, jnp.zeros((SB_BLOCK, LANES), F32)),) * 2
        both = [(0, 0, False), (0, 1, False), (1, 0, False), (1, 1, False)]
        carry = lax.fori_loop(0, i, lambda s, cy: tiles([2 * s, 2 * s + 1], both, cy), (start, start))
        carry = tiles([2 * i, 2 * i + 1], [(0, 0, True), (1, 0, False), (1, 1, True)], carry)
        dq = jnp.concatenate([cb[0][2] * masks[0] + cb[1][2] * masks[1] for cb in carry], axis=0)
        dq_ref[...] = (dq * SB_SCALE).astype(dq_ref.dtype)

        @pl.when(i == nq - 1)
        def _():
            for j in range(seq // SB_BLOCK):
                tile_rows = slice(j * SB_BLOCK, (j + 1) * SB_BLOCK)
                dk_ref[tile_rows, :] = dk_acc[j].T.astype(dk_ref.dtype)
                dv_ref[tile_rows, :] = dv_acc[j].T.astype(dv_ref.dtype)

        @pl.when(step == n_steps - 1)
        def _():
            plan.finish()

    qspec = pl.BlockSpec((SB_STEP, LANES), lambda p, b, i: (b * nq + i, q0 + p))
    ospec = pl.BlockSpec((SB_STEP, LANES), lambda p, b, i: (b * nq + i, p))
    dmspec = pl.BlockSpec((SB_STEP, LANES), lambda p, b, i: (b * nq + i, N_PAIRS + p))
    full = lambda k: pl.BlockSpec((seq, LANES), lambda p, b, i: (b, k + p))
    vec = pl.BlockSpec((1, LANES), lambda p, b, i: (0, p))
    hbm = pl.BlockSpec(memory_space=pltpu.HBM)
    piece = jax.ShapeDtypeStruct((t, GROUP), MXU_DTYPE)
    outs = pl.pallas_call(
        body, name="sb_bwd", grid=(N_PAIRS, n_seq, nq),
        in_specs=[qspec, full(k0), full(v0), vec, pl.BlockSpec(tri.shape, lambda p, b, i: (0, 0)), dmspec, ospec, ospec]
        + [hbm] * n_w,
        out_specs=[ospec, full(0), full(0), vec] + [hbm] * n_w,
        out_shape=[piece, piece, piece, jax.ShapeDtypeStruct((1, GROUP), F32)]
        + [jax.ShapeDtypeStruct(g.shape, g.dtype) for g in grads],
        scratch_shapes=[pltpu.VMEM((seq // SB_BLOCK, LANES, SB_BLOCK), F32),
                        pltpu.VMEM((seq // SB_BLOCK, LANES, SB_BLOCK), F32),
                        pltpu.SemaphoreType.DMA((n_w * (N_DEV - 1),)), pltpu.SemaphoreType.DMA((n_w * (N_DEV - 1),)),
                        pltpu.SemaphoreType.DMA((n_w,))],
        compiler_params=_params(3),
    )(proj, proj, proj, norm_g, tri, dmix, opre, ctot, *[pltpu.with_memory_space_constraint(g, pltpu.HBM) for g in grads])
    return outs[0], outs[1], outs[2], outs[3], list(outs[4:])


def _mesh_place():
    x, y, c = lax.axis_index("x"), lax.axis_index("y"), lax.axis_index("c")
    return x, y, c


def _peer(x, y, c, k):
    px = lax.rem(x + ((k >> 2) & 1), 2)
    py = lax.rem(y + ((k >> 1) & 1), 2)
    pc = lax.rem(c + (k & 1), 2)
    return (px, py, pc), 4 * px + 2 * py + pc


def _remote(src, dst, send_sem, recv_sem, to):
    return pltpu.make_async_remote_copy(src_ref=src, dst_ref=dst, send_sem=send_sem, recv_sem=recv_sem,
                                        device_id=to, device_id_type=pl.DeviceIdType.MESH)


class _GatherPlan:
    COPIES = 7

    def __init__(self, shards, gathered, send_sems, recv_sems, local_sems):
        x, y, c = _mesh_place()
        self.c = c
        self.me = (x, y, c)
        self.sibling = (x, y, 1 - c)
        self.chips = [(1 - x, y), (x, 1 - y), (1 - x, 1 - y)]
        self.tensors = list(zip(shards, gathered))
        self.send_sems, self.recv_sems, self.local_sems = send_sems, recv_sems, local_sems

    @staticmethod
    def _index(place):
        return 4 * place[0] + 2 * place[1] + place[2]

    def _copy(self, w, k, block, to, own=False):
        shard, gathered = self.tensors[w]
        slot = gathered.at[self._index(block)]
        n = w * self.COPIES + k
        return _remote(shard if own else slot, slot, self.send_sems.at[n], self.recv_sems.at[n], to)

    def _local(self, w):
        shard, gathered = self.tensors[w]
        return pltpu.make_async_copy(shard, gathered.at[self._index(self.me)], self.local_sems.at[w])

    def _first(self, w):
        return [self._copy(w, 0, self.me, self.sibling, own=True)] + [
            self._copy(w, 1 + j, self.me, (*chip, self.c), own=True) for j, chip in enumerate(self.chips)]

    def _passed(self, w):
        return [self._copy(w, 4 + j, (*chip, self.c), self.sibling) for j, chip in enumerate(self.chips)]

    def start(self):
        for w in range(len(self.tensors)):
            self._local(w).start()
            for cp in self._first(w):
                cp.start()

    def forward(self):
        for w in range(len(self.tensors)):
            passed = self._passed(w)
            for j, chip in enumerate(self.chips):
                self._copy(w, 1 + j, (*chip, self.c), self.me).wait_recv()
                passed[j].start()

    def finish(self):
        for w in range(len(self.tensors)):
            self._copy(w, 0, self.sibling, self.me).wait_recv()
            for j, chip in enumerate(self.chips):
                self._copy(w, 4 + j, (*chip, 1 - self.c), self.me).wait_recv()
            for cp in self._first(w) + self._passed(w):
                cp.wait_send()
            self._local(w).wait()


class _ScatterPlan:
    def __init__(self, grads, lands, send_sems, recv_sems, local_sems):
        self.place = _mesh_place()
        x, y, c = self.place
        self.me = 4 * x + 2 * y + c
        self.tensors = list(zip(grads, lands))
        self.send_sems, self.recv_sems, self.local_sems = send_sems, recv_sems, local_sems

    def _copies(self, w):
        grad, land = self.tensors[w]
        out = []
        for k in range(1, N_DEV):
            peer, pidx = _peer(*self.place, k)
            n = w * (N_DEV - 1) + k - 1
            sems = (self.send_sems.at[n], self.recv_sems.at[n], peer)
            out.append((_remote(grad.at[pidx], land.at[self.me], *sems), _remote(grad.at[pidx], land.at[pidx], *sems)))
        return out

    def _local(self, w):
        grad, land = self.tensors[w]
        return pltpu.make_async_copy(grad.at[self.me], land.at[self.me], self.local_sems.at[w])

    def start(self):
        for w in range(len(self.tensors)):
            self._local(w).start()
            for send, _ in self._copies(w):
                send.start()

    def finish(self):
        for w in range(len(self.tensors)):
            copies = self._copies(w)
            for _, arrival in copies:
                arrival.wait_recv()
            for send, _ in copies:
                send.wait_send()
            self._local(w).wait()


def _cast_shards(shards):
    def body(*refs):
        n = len(refs) // 2
        for src, dst in zip(refs[:n], refs[n:]):
            dst[...] = src[...].astype(dst.dtype)

    shapes = [jax.ShapeDtypeStruct(s.shape, BF16) for s in shards]
    return pl.pallas_call(
        body, name="cast_shards", grid=(1,),
        in_specs=[_whole(s) for s in shards], out_specs=[_whole(s) for s in shapes], out_shape=shapes,
        compiler_params=_params(1),
    )(*shards)


def _gather_w_in(shard, x, g, chunk=512):
    rows, cols = shard.shape
    t, d = x.shape
    chunk = min(chunk, t)
    n_chunks = t // chunk

    def body(w_ref, x_ref, g_ref, out_ref, h_ref, x_buf, send_sems, recv_sems, local_sems, x_sems):
        plan = _GatherPlan([w_ref], [out_ref], send_sems, recv_sems, local_sems)
        plan.start()

        def fetch(j):
            return pltpu.make_async_copy(x_ref.at[pl.ds(j * chunk, chunk)], x_buf.at[j % 2], x_sems.at[j % 2])

        fetch(0).start()
        gv = g_ref[...]
        for j in range(n_chunks):
            if j + 1 < n_chunks:
                fetch(j + 1).start()
            fetch(j).wait()
            xv = x_buf[j % 2]
            r = lax.rsqrt(jnp.mean(xv * xv, axis=-1, keepdims=True) + EPS)
            h_ref[j * chunk:(j + 1) * chunk, :] = (xv * r * gv).astype(h_ref.dtype)
        plan.forward()
        plan.finish()

    vmem = pl.BlockSpec(memory_space=pltpu.VMEM)
    hbm = pl.BlockSpec(memory_space=pltpu.HBM)
    return pl.pallas_call(
        body, name="gather_w_in", in_specs=[hbm, hbm, vmem], out_specs=[hbm, vmem],
        out_shape=[jax.ShapeDtypeStruct((N_DEV, rows, cols), shard.dtype), jax.ShapeDtypeStruct((t, d), MXU_DTYPE)],
        scratch_shapes=[pltpu.VMEM((2, chunk, d), F32), pltpu.SemaphoreType.DMA((_GatherPlan.COPIES,)),
                        pltpu.SemaphoreType.DMA((_GatherPlan.COPIES,)), pltpu.SemaphoreType.DMA((1,)),
                        pltpu.SemaphoreType.DMA((2,))],
        compiler_params=pltpu.CompilerParams(vmem_limit_bytes=VMEM_LIMIT),
    )(shard, x, g)


def _dw_rows(pieces, b, name, tn=512, tt=1024):
    t, n = b.shape
    widths = [p.shape[1] for p in pieces]
    rows = sum(widths)
    tt = min(tt, t)
    steps = t // tt
    n_p = len(pieces)

    def body(*refs):
        piece_refs, b_ref, o_ref, acc = refs[:n_p], refs[n_p], refs[n_p + 1], refs[n_p + 2]
        s = pl.program_id(1)

        @pl.when(s == 0)
        def _():
            acc[...] = jnp.zeros_like(acc)

        bv = b_ref[...]
        off = 0
        for p_ref, width in zip(piece_refs, widths):
            acc[off:off + width, :] += _dot_tn(p_ref[...], bv)
            off += width

        @pl.when(s == steps - 1)
        def _():
            o_ref[...] = acc[...].astype(o_ref.dtype)

    return pl.pallas_call(
        body, name=name, grid=(n // tn, steps),
        in_specs=[pl.BlockSpec((tt, width), lambda j, s: (s, 0)) for width in widths]
        + [pl.BlockSpec((tt, tn), lambda j, s: (s, j))],
        out_specs=pl.BlockSpec((rows, tn), lambda j, s: (0, j)),
        out_shape=jax.ShapeDtypeStruct((rows, n), BF16),
        scratch_shapes=[pltpu.VMEM((rows, tn), F32)],
        compiler_params=_params(2),
    )(*pieces, b)


def _dh_norm_bwd(pairs, x, g, res, name, tm=512, after=(), next_w=None):
    m, d = x.shape
    tm = min(tm, m)
    n_p = len(pairs)
    n_steps = m // tm
    n_in = 2 * n_p + 3 + len(after) + (next_w is not None)

    def body(*refs):
        x_ref, g_ref, res_ref = refs[2 * n_p:2 * n_p + 3]
        dx_ref, dg_ref = refs[n_in:n_in + 2]

        @pl.when(pl.program_id(0) == 0)
        def _():
            dg_ref[...] = jnp.zeros_like(dg_ref)

        dh = None
        for q in range(n_p):
            part = _dot(refs[2 * q][...], refs[2 * q + 1][...])
            dh = part if dh is None else dh + part
        xv = x_ref[...]
        r = lax.rsqrt(jnp.mean(xv * xv, axis=-1, keepdims=True) + EPS)
        xh = xv * r
        dxh = dh * g_ref[...]
        dxv = res_ref[...] + r * (dxh - xh * jnp.mean(dxh * xh, axis=-1, keepdims=True))
        dx_ref[...] = dxv
        dg_ref[...] += jnp.sum(dh * xh, axis=0, keepdims=True)
        if next_w is not None:
            refs[n_in + 2][...] = _dot_nt(dxv, refs[n_in - 1][...])

    in_specs, args = [], []
    for a, w, r in pairs:
        k = a.shape[1]
        in_specs += [pl.BlockSpec((tm, k), lambda i: (i, 0)),
                     pl.BlockSpec((k, d), functools.partial(lambda i, r: (r, 0), r=r or 0),
                                  pipeline_mode=pl.Buffered(1))]
        args += [a, w]
    row = pl.BlockSpec((tm, d), lambda i: (i, 0))
    vec = pl.BlockSpec((1, d), lambda i: (0, 0))
    extra_in, extra_out, extra_shape = [], [], []
    if next_w is not None:
        e = next_w.shape[0]
        extra_in = [pl.BlockSpec((e, d), lambda i: (0, 0), pipeline_mode=pl.Buffered(1))]
        extra_out = [pl.BlockSpec((tm, e), lambda i: (i, 0))]
        extra_shape = [jax.ShapeDtypeStruct((m, e), F32)]
        args_tail = (next_w,)
    else:
        args_tail = ()
    return pl.pallas_call(
        body, name=name, grid=(n_steps,),
        in_specs=in_specs + [row, vec, row] + [pl.BlockSpec(memory_space=pl.ANY)] * len(after) + extra_in,
        out_specs=[row, vec] + extra_out,
        out_shape=[jax.ShapeDtypeStruct((m, d), F32), jax.ShapeDtypeStruct((1, d), F32)] + extra_shape,
        compiler_params=_params(1),
    )(*args, x, g, res, *after, *args_tail)


def _exchange_start(grad, name):
    def body(g_ref, land_ref, send_sem, recv_sem, local_sem, g_thru, land_thru, token):
        del g_thru, land_thru
        x, y, c = _mesh_place()
        me = 4 * x + 2 * y + c
        pltpu.make_async_copy(g_ref.at[me], land_ref.at[me], local_sem).start()
        for k in range(1, N_DEV):
            peer, pidx = _peer(x, y, c, k)
            _remote(g_ref.at[pidx], land_ref.at[me], send_sem, recv_sem, peer).start()
        token[...] = jnp.zeros_like(token)

    hbm = pl.BlockSpec(memory_space=pltpu.HBM)
    sem = pl.BlockSpec(memory_space=pltpu.SEMAPHORE)
    buf = pltpu.HBM(grad.shape, grad.dtype)
    return pl.pallas_call(
        body, name=name,
        out_shape=(pltpu.SemaphoreType.DMA(()), pltpu.SemaphoreType.DMA(()), pltpu.SemaphoreType.DMA(()), buf, buf,
                   jax.ShapeDtypeStruct((8, LANES), F32)),
        in_specs=(hbm, hbm), out_specs=(sem, sem, sem, hbm, hbm, pl.BlockSpec(memory_space=pltpu.VMEM)),
        input_output_aliases={0: 3, 1: 4},
        compiler_params=pltpu.CompilerParams(has_side_effects=pltpu.SideEffectType.DATAFLOW_SIDE_EFFECTING),
    )(pltpu.with_memory_space_constraint(grad, pltpu.HBM),
      pltpu.with_memory_space_constraint(lax.empty(grad.shape, grad.dtype), pltpu.HBM))


def _exchange_wait(send_sem, recv_sem, local_sem, grad, land, after, name):
    def body(g_ref, land_ref, send_sem, recv_sem, local_sem, *rest):
        x, y, c = _mesh_place()
        peer, _ = _peer(x, y, c, 1)
        others = pl.ds(0, N_DEV - 1)
        seven = _remote(g_ref.at[others], land_ref.at[others], send_sem, recv_sem, peer)
        seven.wait_send()
        seven.wait_recv()
        pltpu.make_async_copy(g_ref.at[0], land_ref.at[0], local_sem).wait()

    hbm = pl.BlockSpec(memory_space=pltpu.HBM)
    sem = pl.BlockSpec(memory_space=pltpu.SEMAPHORE)
    buf = pltpu.HBM(grad.shape, grad.dtype)
    return pl.pallas_call(
        body, name=name, out_shape=(buf, buf),
        in_specs=(hbm, hbm, sem, sem, sem) + (pl.BlockSpec(memory_space=pl.ANY),) * len(after), out_specs=(hbm, hbm),
        input_output_aliases={0: 0, 1: 1},
        compiler_params=pltpu.CompilerParams(has_side_effects=pltpu.SideEffectType.DATAFLOW_SIDE_EFFECTING),
    )(grad, land, send_sem, recv_sem, local_sem, *after)[1]


SMALL_LAYOUT = ((0, 0, 0, 0, D_MODEL), (1, 0, 1, 0, GROUP), (1, 1, 1, GROUP, GROUP), (2, 0, 2, 0, GROUP),
                (3, 0, 2, GROUP, GROUP), (4, 0, 3, 0, D_MODEL), (5, 0, 4, 0, D_MODEL))
LOSS_ROW = 5
N_SMALL = 6


def _place_small(dst, srcs):
    dst[...] = jnp.zeros_like(dst)
    for p, sr, dr, dc, width in SMALL_LAYOUT:
        dst[dr:dr + 1, dc:dc + width] = srcs[p][sr:sr + 1, :]


def _small_allreduce(grads, loss_part):
    def body(*refs):
        g_in, loss_in, sum_ref = refs[:N_SMALL], refs[N_SMALL], refs[N_SMALL + 1]
        pack, land, send_sems, recv_sems = refs[N_SMALL + 2:]
        _place_small(pack, g_in)
        pack[LOSS_ROW:LOSS_ROW + 1, 0:LANES] = loss_in[...]

        x, y, c = _mesh_place()
        me = 4 * x + 2 * y + c
        land[me] = pack[...]
        sends = []
        for k in range(1, N_DEV):
            peer, _ = _peer(x, y, c, k)
            cp = _remote(pack, land.at[me], send_sems.at[k - 1], recv_sems.at[k - 1], peer)
            cp.start()
            sends.append(cp)
        for k in range(1, N_DEV):
            peer, pidx = _peer(x, y, c, k)
            _remote(pack, land.at[pidx], send_sems.at[k - 1], recv_sems.at[k - 1], peer).wait_recv()
        for cp in sends:
            cp.wait_send()

        g = land[0]
        for d in range(1, N_DEV):
            g = g + land[d]
        sum_ref[...] = g

    args = [*grads, loss_part]
    packed = jax.ShapeDtypeStruct((SMALL_ROWS, D_MODEL), F32)
    return pl.pallas_call(
        body, name="small_allreduce", grid=(1,),
        in_specs=[_whole(a) for a in args], out_specs=_whole(packed), out_shape=packed,
        scratch_shapes=[pltpu.VMEM(packed.shape, F32), pltpu.VMEM((N_DEV,) + packed.shape, F32),
                        pltpu.SemaphoreType.DMA((N_DEV - 1,)), pltpu.SemaphoreType.DMA((N_DEV - 1,))],
        compiler_params=_params(1),
    )(*args)


def _small_adamw(gsum, ws, ms, vs):
    def body(*refs):
        g_ref = refs[0]
        params = [refs[1 + q * N_SMALL:1 + (q + 1) * N_SMALL] for q in range(3)]
        o0 = 1 + 3 * N_SMALL
        outs = [refs[o0 + q * N_SMALL:o0 + (q + 1) * N_SMALL] for q in range(4)]
        loss_out = refs[o0 + 4 * N_SMALL]
        wp, mp, vp = refs[o0 + 4 * N_SMALL + 1:]
        for dst, srcs in zip((wp, mp, vp), params):
            _place_small(dst, srcs)
        g = g_ref[...]
        delta, nm, nv = _adam(wp[...], g, mp[...], vp[...])
        for val, out in zip((g, delta, nm, nv), outs):
            for p, sr, dr, dc, width in SMALL_LAYOUT:
                out[p][sr:sr + 1, :] = val[dr:dr + 1, dc:dc + width]
        loss_out[...] = g[LOSS_ROW:LOSS_ROW + 1, 0:LANES]

    args = [gsum, *ws, *ms, *vs]
    shapes = [jax.ShapeDtypeStruct(w.shape, F32) for w in ws] * 4 + [jax.ShapeDtypeStruct((1, LANES), F32)]
    packed = pltpu.VMEM((SMALL_ROWS, D_MODEL), F32)
    outs = pl.pallas_call(
        body, name="small_adamw", grid=(1,),
        in_specs=[_whole(a) for a in args], out_specs=[_whole(s) for s in shapes], out_shape=shapes,
        scratch_shapes=[packed, packed, packed], compiler_params=_params(1),
    )(*args)
    return [outs[q * N_SMALL:(q + 1) * N_SMALL] for q in range(4)], outs[4 * N_SMALL]


def _adam(w, g, m, v):
    m = ADAM_B1 * m + (1.0 - ADAM_B1) * g
    v = ADAM_B2 * v + (1.0 - ADAM_B2) * (g * g)
    m_hat = m / (1.0 - ADAM_B1 ** ADAM_STEP)
    v_hat = v / (1.0 - ADAM_B2 ** ADAM_STEP)
    delta = -ADAM_LR * (m_hat / (jnp.sqrt(v_hat) + ADAM_EPS) + ADAM_WD * w)
    return delta, m, v


def _reduce_adamw(land, w, m, v, name, tr, after=()):
    _, rows, width = land.shape

    def body(land_ref, w_ref, m_ref, v_ref, *rest):
        g_ref, d_ref, nm_ref, nv_ref = rest[len(after):]
        g = land_ref[0].astype(F32)
        for d in range(1, N_DEV):
            g = g + land_ref[d].astype(F32)
        delta, nm, nv = _adam(w_ref[...], g, m_ref[...], v_ref[...])
        g_ref[...] = g
        d_ref[...] = delta
        nm_ref[...] = nm
        nv_ref[...] = nv

    row = pl.BlockSpec((tr, width), lambda i: (i, 0))
    out = jax.ShapeDtypeStruct((rows, width), F32)
    return pl.pallas_call(
        body, name=name, grid=(rows // tr,),
        in_specs=[pl.BlockSpec((N_DEV, tr, width), lambda i: (0, i, 0)), row, row, row]
        + [pl.BlockSpec(memory_space=pl.ANY)] * len(after),
        out_specs=[row, row, row, row], out_shape=[out, out, out, out],
        compiler_params=_params(1),
    )(land, w, m, v, *after)


def kernel(x, mix_norm_g, w_in, lower_bounds, hgrn_norm_g, sb_norm_g, w_out, ffn_norm_g, w_gate, w_up, w_down, final_norm_g, loss_target, m_mix_norm_g, m_w_in, m_lower_bounds, m_hgrn_norm_g, m_sb_norm_g, m_w_out, m_ffn_norm_g, m_w_gate, m_w_up, m_w_down, m_final_norm_g, v_mix_norm_g, v_w_in, v_lower_bounds, v_hgrn_norm_g, v_sb_norm_g, v_w_out, v_ffn_norm_g, v_w_gate, v_w_up, v_w_down, v_final_norm_g):
    n_seq, seq, d = x.shape
    t = n_seq * seq
    x2d = x.reshape(t, d)
    tgt = loss_target.reshape(t, d)
    final_g = final_norm_g.reshape(1, d)
    col_sharded = (True, False, True, True, False)

    def as_rows(ws):
        return [w[0].T if tr else w[0] for w, tr in zip(ws, col_sharded)]

    big_w = as_rows([w_in, w_out, w_gate, w_up, w_down])
    big_m = as_rows([m_w_in, m_w_out, m_w_gate, m_w_up, m_w_down])
    big_v = as_rows([v_w_in, v_w_out, v_w_gate, v_w_up, v_w_down])

    sh_in, sh_out, sh_gate, sh_up, sh_down = _cast_shards(big_w)
    wt_in, h1 = _gather_w_in(sh_in, x2d, mix_norm_g)
    wt_in = wt_in.reshape(IN_COLS, d)
    proj_h = _mm_nt(h1, wt_in, "proj_hgrn", rows=(0, 4 * GROUP), tm=1024, tk=4 * GROUP)
    proj_s = _mm_nt(h1, wt_in, "proj_sb", rows=(4 * GROUP, 3 * GROUP), out_dtype=MXU_DTYPE, tm=1024, tk=512)
    mixed, oa_pre, states = _hgrn_fwd(proj_h, lower_bounds, hgrn_norm_g, n_seq, seq)
    mixed, ob_pre, ctot, gathered = _sb_fwd(proj_s, sb_norm_g, mixed, [sh_out, sh_gate, sh_up, sh_down], n_seq, seq)
    wf_out = gathered[0].reshape(d, d)
    wt_gate = gathered[1].reshape(D_FF, d)
    wt_up = gathered[2].reshape(D_FF, d)
    wf_down = gathered[3].reshape(D_FF, d)
    x1, h2 = _mix_out_norm(mixed, wf_out, x2d, ffn_norm_g, "mix_out")
    gate, up, ff = _ffn_up(h2, wt_gate, wt_up, "ffn_up")
    dx2, dx2m, d_final_g, loss_part = _ffn_down_loss(ff, wf_down, x1, tgt, final_g, "ffn_down_loss")

    dgate, dup = _ffn_bwd_act(dx2m, wf_down, gate, up, "ffn_bwd_act")
    dw_down = _mm_tn(ff, dx2m, "dw_down", tk=1408, tn=1024).reshape(N_DEV, D_FF // N_DEV, d)
    dw_gate = _mm_tn(dgate, h2, "dw_gate", tk=1408, tn=1024).reshape(N_DEV, D_FF // N_DEV, d)
    dw_up = _mm_tn(dup, h2, "dw_up", tk=1408, tn=1024).reshape(N_DEV, D_FF // N_DEV, d)
    dx1, d_ffn_g, dmix = _dh_norm_bwd([(dgate, wt_gate, None), (dup, wt_up, None)], x1, ffn_norm_g, dx2, "dh_ffn",
                                      tm=256, next_w=wf_out)
    dw_out = _mm_tn(mixed, dx1, "dw_out", tk=1024, tn=1024).reshape(N_DEV, d // N_DEV, d)
    dsq, dsk, dsv, d_sb_g, lands = _sb_bwd(proj_s, sb_norm_g, dmix, ob_pre, ctot, [dw_out, dw_gate, dw_up, dw_down],
                                            n_seq, seq)
    dhq, dhf, dhi, dhg, d_lb, d_hgrn_g = _hgrn_bwd(proj_h, lower_bounds, hgrn_norm_g, dmix, oa_pre, states, n_seq,
                                                   seq)
    dproj = [dhq, dhf, dhi, dhg, dsq, dsk, dsv]
    dw_in = _dw_rows(dproj, h1, "dw_in").reshape(N_DEV, IN_COLS // N_DEV, d)
    send_sem, recv_sem, local_sem, dw_in, land_in, token = _exchange_start(dw_in, "dw_in_send")
    dx, d_mix_g = _dh_norm_bwd([(piece, wt_in, k) for k, piece in enumerate(dproj)], x2d, mix_norm_g, dx1, "dh_mix",
                               after=(token,))

    tiles = {"in": 224, "out": 128, "gate": 176, "up": 176, "down": 176}
    keys = list(tiles)
    rest = [_reduce_adamw(land, w, m, v, "adamw_" + key, tr=tiles[key], after=(token,))
            for key, land, w, m, v in zip(keys[1:], lands, big_w[1:], big_m[1:], big_v[1:])]
    small_sum = _small_allreduce([d_mix_g, d_lb, d_hgrn_g, d_sb_g, d_ffn_g, d_final_g], loss_part)
    small, loss_row = _small_adamw(
        small_sum, [mix_norm_g, lower_bounds, hgrn_norm_g, sb_norm_g, ffn_norm_g, final_g],
        [m_mix_norm_g, m_lower_bounds, m_hgrn_norm_g, m_sb_norm_g, m_ffn_norm_g, m_final_norm_g.reshape(1, d)],
        [v_mix_norm_g, v_lower_bounds, v_hgrn_norm_g, v_sb_norm_g, v_ffn_norm_g, v_final_norm_g.reshape(1, d)])
    land_in = _exchange_wait(send_sem, recv_sem, local_sem, dw_in, land_in,
                             [dx, loss_row] + [res[0] for res in rest], "dw_in_await")
    big = [_reduce_adamw(land_in, big_w[0], big_m[0], big_v[0], "adamw_in", tr=tiles["in"])] + rest
    big = [[r.T if tr else r for r in res] for res, tr in zip(big, col_sharded)]

    outs = [loss_row[0, 0], dx.reshape(n_seq, seq, d)]
    for q in range(4):
        b_in, b_out, b_gate, b_up, b_down = [res[q][None] for res in big]
        s_mix, s_lb, s_hgrn, s_sb, s_ffn, s_final = small[q]
        outs += [s_mix, b_in, s_lb, s_hgrn, s_sb, b_out, s_ffn, b_gate, b_up, b_down, s_final.reshape(d)]
    return tuple(outs)
```

```python
import functools
import math

import jax
import jax.numpy as jnp
from jax import lax
from jax.experimental import pallas as pl
from jax.experimental.pallas import tpu as pltpu

F32 = jnp.float32
BF16 = jnp.bfloat16
MXU_DTYPE = BF16

EPS = 1e-6
D_MODEL = 1024
N_HEADS = 8
D_HEAD = 64
GROUP = N_HEADS * D_HEAD
IN_COLS = 7 * GROUP
D_FF = 2816
CHUNK = 64
LANES = 128
N_PAIRS = GROUP // LANES
SUPER = 256
SB_BLOCK = 256
N_DEV = 8

ADAM_LR = 0.001
ADAM_B1 = 0.9
ADAM_B2 = 0.999
ADAM_EPS = 1e-08
ADAM_WD = 0.01
ADAM_STEP = 10

SMALL_ROWS = 8
FF_TILE = D_FF // 2

VMEM_LIMIT = 48 * 1024 * 1024


def _params(n_axes, vmem=VMEM_LIMIT):
    return pltpu.CompilerParams(dimension_semantics=("arbitrary",) * n_axes, vmem_limit_bytes=vmem)


def _whole(a):
    return pl.BlockSpec(a.shape, functools.partial(lambda i, nd: (0,) * nd, nd=len(a.shape)))


def _dot(a, b):
    return jnp.dot(a.astype(MXU_DTYPE), b.astype(MXU_DTYPE), preferred_element_type=F32)


def _dot_nt(a, b):
    return lax.dot_general(a.astype(MXU_DTYPE), b.astype(MXU_DTYPE), (((1,), (1,)), ((), ())),
                           preferred_element_type=F32)


def _dot_tn(a, b):
    return lax.dot_general(a.astype(MXU_DTYPE), b.astype(MXU_DTYPE), (((0,), (0,)), ((), ())),
                           preferred_element_type=F32)


def _split(x, parts):
    out, r = [], x
    for _ in range(parts):
        h = r.astype(BF16)
        out.append(h)
        r = r - h.astype(F32)
    return out


def _rsum_left(u, x, parts):
    acc = None
    for h in _split(x, parts):
        d = jnp.dot(u, h, preferred_element_type=F32)
        acc = d if acc is None else acc + d
    return acc


def _ones_where(mask):
    return jnp.where(mask, 1.0, 0.0).astype(BF16)


def _sigmoid(x):
    return 1.0 / (1.0 + jnp.exp(-x))


def _softplus(x):
    return jnp.maximum(x, 0.0) + jnp.log(1.0 + jnp.exp(-jnp.abs(x)))


def _head_masks():
    lane = lax.broadcasted_iota(jnp.int32, (1, LANES), 1)
    return [jnp.where(lane < D_HEAD, 1.0, 0.0), jnp.where(lane >= D_HEAD, 1.0, 0.0)]


def _head_rstd(o, masks):
    sq = o * o
    r = None
    for m in masks:
        ms = jnp.sum(sq * m, axis=1, keepdims=True) * (1.0 / D_HEAD)
        t = lax.rsqrt(ms + EPS) * m
        r = t if r is None else r + t
    return r


def _head_mean(t, masks):
    out = None
    for m in masks:
        v = jnp.sum(t * m, axis=1, keepdims=True) * (1.0 / D_HEAD) * m
        out = v if out is None else out + v
    return out


def _mix_out_norm(a, w, res, g, name, tm=512):
    m, k = a.shape
    d = w.shape[1]
    tm = min(tm, m)

    def body(a_ref, w_ref, res_ref, g_ref, x_ref, h_ref):
        xv = res_ref[...] + _dot(a_ref[...], w_ref[...])
        x_ref[...] = xv
        r = lax.rsqrt(jnp.mean(xv * xv, axis=-1, keepdims=True) + EPS)
        h_ref[...] = (xv * r * g_ref[...]).astype(h_ref.dtype)

    row = pl.BlockSpec((tm, d), lambda i: (i, 0))
    return pl.pallas_call(
        body, name=name, grid=(m // tm,),
        in_specs=[pl.BlockSpec((tm, k), lambda i: (i, 0)), pl.BlockSpec((k, d), lambda i: (0, 0)), row,
                  pl.BlockSpec((1, d), lambda i: (0, 0))],
        out_specs=[row, row],
        out_shape=[jax.ShapeDtypeStruct((m, d), F32), jax.ShapeDtypeStruct((m, d), MXU_DTYPE)],
        compiler_params=_params(1),
    )(a, w, res, g)


def _ffn_down_loss(a, w, res, target, g, name, tm=512):
    m, k = a.shape
    d = w.shape[1]
    tm = min(tm, m)

    def body(a_ref, w_ref, res_ref, t_ref, g_ref, dx_ref, dxm_ref, dg_ref, loss_ref):
        xv = res_ref[...] + _dot(a_ref[...], w_ref[...])
        gv = g_ref[...]
        r = lax.rsqrt(jnp.mean(xv * xv, axis=-1, keepdims=True) + EPS)
        xh = xv * r
        e = xh * gv - t_ref[...]
        dy = e * (1.0 / d)
        dxh = dy * gv
        dxv = r * (dxh - xh * jnp.mean(dxh * xh, axis=-1, keepdims=True))
        dx_ref[...] = dxv
        dxm_ref[...] = dxv.astype(dxm_ref.dtype)

        @pl.when(pl.program_id(0) == 0)
        def _():
            dg_ref[...] = jnp.zeros_like(dg_ref)
            loss_ref[...] = jnp.zeros_like(loss_ref)

        dg_ref[...] += jnp.sum(dy * xh, axis=0, keepdims=True)
        part = 0.5 * jnp.sum(jnp.mean(e * e, axis=-1, keepdims=True), axis=0, keepdims=True)
        loss_ref[...] += jnp.broadcast_to(part, loss_ref.shape)

    row = pl.BlockSpec((tm, d), lambda i: (i, 0))
    vec = pl.BlockSpec((1, d), lambda i: (0, 0))
    return pl.pallas_call(
        body, name=name, grid=(m // tm,),
        in_specs=[pl.BlockSpec((tm, k), lambda i: (i, 0)), pl.BlockSpec((k, d), lambda i: (0, 0)), row, row, vec],
        out_specs=[row, row, vec, pl.BlockSpec((1, LANES), lambda i: (0, 0))],
        out_shape=[jax.ShapeDtypeStruct((m, d), F32), jax.ShapeDtypeStruct((m, d), MXU_DTYPE),
                   jax.ShapeDtypeStruct((1, d), F32), jax.ShapeDtypeStruct((1, LANES), F32)],
        compiler_params=_params(1),
    )(a, w, res, target, g)


def _mm_nt(a, w, name, rows=None, out_dtype=F32, tm=512, tk=512):
    m, n = a.shape
    row0, k = rows or (0, w.shape[0])
    tm, tk = min(tm, m), min(tk, k)
    first = row0 // tk

    def body(a_ref, w_ref, o_ref):
        o_ref[...] = _dot_nt(a_ref[...], w_ref[...]).astype(o_ref.dtype)

    return pl.pallas_call(
        body, name=name, grid=(k // tk, m // tm),
        in_specs=[pl.BlockSpec((tm, n), lambda j, i: (i, 0)), pl.BlockSpec((tk, n), lambda j, i: (first + j, 0))],
        out_specs=pl.BlockSpec((tm, tk), lambda j, i: (i, j)),
        out_shape=jax.ShapeDtypeStruct((m, k), out_dtype),
        compiler_params=_params(2),
    )(a, w)


def _mm_tn(a, b, name, tk, tn, tt=1024, out_dtype=BF16):
    t, k = a.shape
    n = b.shape[1]
    tt = min(tt, t)
    steps = t // tt

    def body(a_ref, b_ref, o_ref, acc):
        s = pl.program_id(2)

        @pl.when(s == 0)
        def _():
            acc[...] = jnp.zeros_like(acc)

        acc[...] += _dot_tn(a_ref[...], b_ref[...])

        @pl.when(s == steps - 1)
        def _():
            o_ref[...] = acc[...].astype(o_ref.dtype)

    return pl.pallas_call(
        body, name=name, grid=(k // tk, n // tn, steps),
        in_specs=[pl.BlockSpec((tt, tk), lambda i, j, s: (s, i)), pl.BlockSpec((tt, tn), lambda i, j, s: (s, j))],
        out_specs=pl.BlockSpec((tk, tn), lambda i, j, s: (i, j)),
        out_shape=jax.ShapeDtypeStruct((k, n), out_dtype),
        scratch_shapes=[pltpu.VMEM((tk, tn), F32)],
        compiler_params=_params(3),
    )(a, b)


def _ffn_up(h, wg_t, wu_t, name, tm=1024, tn=FF_TILE):
    m, k = h.shape
    n = wg_t.shape[0]
    tm = min(tm, m)

    def body(h_ref, wg_ref, wu_ref, gate_ref, up_ref, ff_ref):
        hv = h_ref[...]
        gate = _dot_nt(hv, wg_ref[...])
        up = _dot_nt(hv, wu_ref[...])
        gate_ref[...] = gate.astype(gate_ref.dtype)
        up_ref[...] = up.astype(up_ref.dtype)
        ff_ref[...] = (gate * _sigmoid(gate) * up).astype(ff_ref.dtype)

    wspec = pl.BlockSpec((tn, k), lambda j, i: (j, 0))
    ospec = pl.BlockSpec((tm, tn), lambda j, i: (i, j))
    return pl.pallas_call(
        body, name=name, grid=(n // tn, m // tm),
        in_specs=[pl.BlockSpec((tm, k), lambda j, i: (i, 0)), wspec, wspec],
        out_specs=[ospec, ospec, ospec],
        out_shape=[jax.ShapeDtypeStruct((m, n), MXU_DTYPE)] * 3,
        compiler_params=_params(2),
    )(h, wg_t, wu_t)


def _ffn_bwd_act(dx, wd, gate, up, name, tm=1024, tn=FF_TILE):
    m, k = dx.shape
    n = wd.shape[0]
    tm = min(tm, m)

    def body(dx_ref, wd_ref, gate_ref, up_ref, dgate_ref, dup_ref):
        dff = _dot_nt(dx_ref[...], wd_ref[...])
        gate = gate_ref[...].astype(F32)
        sg = _sigmoid(gate)
        dgate_ref[...] = (dff * up_ref[...].astype(F32) * sg * (1.0 + gate * (1.0 - sg))).astype(dgate_ref.dtype)
        dup_ref[...] = (dff * gate * sg).astype(dup_ref.dtype)

    ospec = pl.BlockSpec((tm, tn), lambda j, i: (i, j))
    return pl.pallas_call(
        body, name=name, grid=(n // tn, m // tm),
        in_specs=[pl.BlockSpec((tm, k), lambda j, i: (i, 0)), pl.BlockSpec((tn, k), lambda j, i: (j, 0)),
                  ospec, ospec],
        out_specs=[ospec, ospec],
        out_shape=[jax.ShapeDtypeStruct((m, n), MXU_DTYPE), jax.ShapeDtypeStruct((m, n), MXU_DTYPE)],
        compiler_params=_params(2),
    )(dx, wd, gate, up)


def _chunk_masks():
    r = lax.broadcasted_iota(jnp.int32, (SUPER, SUPER), 0)
    c = lax.broadcasted_iota(jnp.int32, (SUPER, SUPER), 1)
    same = jnp.right_shift(r, 6) == jnp.right_shift(c, 6)
    lower = jnp.logical_and(same, c <= r)
    upper = jnp.logical_and(same, c >= r)
    return same, lower, upper


def _head_block_mask():
    r = lax.broadcasted_iota(jnp.int32, (LANES, LANES), 0)
    c = lax.broadcasted_iota(jnp.int32, (LANES, LANES), 1)
    return jnp.where(jnp.right_shift(r, 6) == jnp.right_shift(c, 6), 1.0, 0.0)


def _lower_bound(lb_raw):
    return 1.0 / (1.0 + jnp.exp(lb_raw[1:2, :] - lb_raw[0:1, :]))


HGRN_UNROLL = 4
PER_SUPER = SUPER // CHUNK
CHUNK_ROWS = [slice(c * CHUNK, (c + 1) * CHUNK) for c in range(PER_SUPER)]


def _over_chunks(rows):
    return jnp.concatenate([jnp.broadcast_to(r, (CHUNK, LANES)) for r in rows], axis=0)


def _hgrn_gates(q, hf, lb, lower_b):
    sig = _sigmoid(hf)
    f = lb + (1.0 - lb) * sig
    k = 1.0 - f
    lf = jnp.log(f)
    b = _rsum_left(lower_b, lf, 2)
    ends = [b[cr.stop - 1:cr.stop, :] for cr in CHUNK_ROWS]
    eb = jnp.exp(b)
    enb = jnp.exp(-b)
    edb = jnp.exp(_over_chunks(ends) - b)
    decs = [jnp.exp(e) for e in ends]
    return sig, f, k, decs, eb, enb, edb, q * eb, k * enb, k * edb


def _hgrn_fwd(proj, lower_bounds, norm_g, n_seq, seq):
    t = n_seq * seq
    n_super = seq // SUPER
    n_chunks = seq // CHUNK

    def body(q_ref, f_ref, i_ref, g_ref, lb_ref, ng_ref, out_ref, opre_ref, st_ref):
        masks = _head_masks()
        _, lower, _ = _chunk_masks()
        lower_b = _ones_where(lower)
        bd = _head_block_mask()
        lb = _lower_bound(lb_ref[...])
        ng = ng_ref[...]

        def step(it, st):
            blocks = [HGRN_UNROLL * it + u for u in range(HGRN_UNROLL)]
            rows = [pl.ds(pl.multiple_of(sb * SUPER, SUPER), SUPER) for sb in blocks]
            vs = [i_ref[rw, :] for rw in rows]
            gates = [_hgrn_gates(q_ref[rw, :], f_ref[rw, :], lb, lower_b) for rw in rows]
            decs, qes, kes, kds = ([g[k] for g in gates] for k in (3, 7, 8, 9))
            scores = [[_dot_nt(qe * m, ke) for m in masks] for qe, ke in zip(qes, kes)]
            updates = [[_dot_tn(v[cr], kd[cr]) for cr in CHUNK_ROWS] for v, kd in zip(vs, kds)]
            states = [st]
            for dec_b, upd_b in zip(decs, updates):
                for dec, upd in zip(dec_b, upd_b):
                    states.append(states[-1] * dec + bd * upd)
            for u, sb in enumerate(blocks):
                for c in range(PER_SUPER):
                    st_ref[0, 0, sb * PER_SUPER + c] = states[u * PER_SUPER + c]
            intra = [[_dot(jnp.where(lower, p, 0.0), v) for p in sc] for sc, v in zip(scores, vs)]
            inter = [[_dot_nt(qe[cr], states[u * PER_SUPER + c]) for c, cr in enumerate(CHUNK_ROWS)]
                     for u, qe in enumerate(qes)]
            for rw, intra_b, inter_b in zip(rows, intra, inter):
                o = intra_b[0] * masks[0] + intra_b[1] * masks[1] + jnp.concatenate(inter_b, axis=0)
                opre_ref[rw, :] = o
                hg = g_ref[rw, :]
                on = o * _head_rstd(o, masks) * ng
                out_ref[rw, :] = (on * hg * _sigmoid(hg)).astype(out_ref.dtype)
            return states[-1]

        lax.fori_loop(0, n_super // HGRN_UNROLL, step, jnp.zeros((LANES, LANES), F32))

    def col(k):
        return pl.BlockSpec((seq, LANES), lambda p, b: (b, k * N_PAIRS + p))

    vec = lambda rows: pl.BlockSpec((rows, LANES), lambda p, b: (0, p))
    ospec = pl.BlockSpec((seq, LANES), lambda p, b: (b, p))
    return pl.pallas_call(
        body, name="hgrn_fwd", grid=(N_PAIRS, n_seq),
        in_specs=[col(0), col(1), col(2), col(3), vec(2), vec(1)],
        out_specs=[ospec, ospec,
                   pl.BlockSpec((1, 1, n_chunks, LANES, LANES), lambda p, b: (b, p, 0, 0, 0))],
        out_shape=[jax.ShapeDtypeStruct((t, 2 * GROUP), MXU_DTYPE), jax.ShapeDtypeStruct((t, GROUP), F32),
                   jax.ShapeDtypeStruct((n_seq, N_PAIRS, n_chunks, LANES, LANES), F32)],
        compiler_params=_params(2),
    )(proj, proj, proj, proj, lower_bounds, norm_g)


def _hgrn_bwd(proj, lower_bounds, norm_g, dmix, opre, states, n_seq, seq):
    t = n_seq * seq
    n_super = seq // SUPER
    n_chunks = seq // CHUNK
    per = SUPER // CHUNK

    def body(q_ref, f_ref, i_ref, g_ref, lb_ref, ng_ref, dm_ref, opre_ref, st_ref,
             dq_ref, df_ref, di_ref, dg_ref, dlb_ref, dng_ref):
        masks = _head_masks()
        _, lower, upper = _chunk_masks()
        lower_b, upper_b = _ones_where(lower), _ones_where(upper)
        bd = _head_block_mask()
        lb_raw = lb_ref[...]
        lb = _lower_bound(lb_raw)
        ng = ng_ref[...]

        @pl.when(pl.program_id(1) == 0)
        def _():
            dlb_ref[...] = jnp.zeros_like(dlb_ref)
            dng_ref[...] = jnp.zeros_like(dng_ref)

        def first_half(sb):
            rows = pl.ds(pl.multiple_of(sb * SUPER, SUPER), SUPER)
            q, hf, v, hg = q_ref[rows, :], f_ref[rows, :], i_ref[rows, :], g_ref[rows, :]
            sig, f, k, decs, eb, enb, edb, qe, ke, kd = _hgrn_gates(q, hf, lb, lower_b)
            o = opre_ref[rows, :]
            r = _head_rstd(o, masks)
            oh = o * r
            dm = dm_ref[rows, :]
            sg = _sigmoid(hg)
            dg_ref[rows, :] = (dm * oh * ng * sg * (1.0 + hg * (1.0 - sg))).astype(dg_ref.dtype)
            don = dm * hg * sg
            dng_ref[...] += jnp.sum(don * oh, axis=0, keepdims=True)
            doh = don * ng
            do = r * (doh - oh * _head_mean(doh * oh, masks))
            doms = [do * m for m in masks]
            qems = [qe * m for m in masks]
            scores = [_dot_nt(qem, ke) for qem in qems]
            dscores = [_dot_nt(dom, v) for dom in doms]
            prevs = [st_ref[0, 0, sb * per + c] for c in range(per)]
            dst_in = [_dot_tn(do[cr], qe[cr]) for cr in CHUNK_ROWS]
            dqe_i = [_dot(do[cr], prev) for cr, prev in zip(CHUNK_ROWS, prevs)]
            return dict(rows=rows, v=v, sig=sig, f=f, decs=decs, eb=eb, enb=enb, edb=edb, qe=qe, ke=ke, kd=kd,
                        doms=doms, qems=qems, scores=scores, dscores=dscores, prevs=prevs, dst_in=dst_in, dqe_i=dqe_i)

        def second_half(blk, dsts):
            v, qe, ke, kd = blk["v"], blk["qe"], blk["ke"], blk["kd"]
            ps = [jnp.where(lower, p, 0.0) for p in blk["scores"]]
            dps = [jnp.where(lower, dp, 0.0) for dp in blk["dscores"]]
            dqe_h = [_dot(dp, ke) for dp in dps]
            dke_h = [_dot_tn(dp, qem) for dp, qem in zip(dps, blk["qems"])]
            dv_h = [_dot_tn(p, dom) for p, dom in zip(ps, blk["doms"])]
            dus = [bd * d for d in dsts]
            dv_i = [_dot_nt(kd[cr], du) for cr, du in zip(CHUNK_ROWS, dus)]
            dkd_i = [_dot(v[cr], du) for cr, du in zip(CHUNK_ROWS, dus)]

            def finish():
                dqe = dqe_h[0] * masks[0] + dqe_h[1] * masks[1] + jnp.concatenate(blk["dqe_i"], axis=0)
                dke = dke_h[0] + dke_h[1]
                dv = dv_h[0] + dv_h[1] + jnp.concatenate(dv_i, axis=0)
                dkd = jnp.concatenate(dkd_i, axis=0)
                dk = dke * blk["enb"] + dkd * blk["edb"]
                db = dqe * qe - dke * ke - dkd * kd
                dkd_kd = dkd * kd
                dends = [jnp.sum(dkd_kd[cr], axis=0, keepdims=True)
                         + jnp.sum(dsts[c] * blk["prevs"][c], axis=0, keepdims=True) * blk["decs"][c]
                         for c, cr in enumerate(CHUNK_ROWS)]
                dlf = _rsum_left(upper_b, db, 2) + _over_chunks(dends)
                sig = blk["sig"]
                dfv = dlf / blk["f"] - dk
                rows = blk["rows"]
                dq_ref[rows, :] = (dqe * blk["eb"]).astype(dq_ref.dtype)
                di_ref[rows, :] = dv.astype(di_ref.dtype)
                df_ref[rows, :] = (dfv * (1.0 - lb) * sig * (1.0 - sig)).astype(df_ref.dtype)
                dlb = jnp.sum(dfv * (1.0 - sig), axis=0, keepdims=True)
                da0 = dlb * lb * (1.0 - lb)
                dlb_ref[0:1, :] += da0
                dlb_ref[1:2, :] -= da0

            return finish

        def step(it, dst):
            blocks = [first_half(n_super - 1 - HGRN_UNROLL * it - u) for u in range(HGRN_UNROLL)]
            all_dsts = []
            for blk in blocks:
                dsts = [None] * per
                for c in reversed(range(per)):
                    dsts[c] = dst
                    dst = bd * (dst * blk["decs"][c] + blk["dst_in"][c])
                all_dsts.append(dsts)
            for finish in [second_half(blk, dsts) for blk, dsts in zip(blocks, all_dsts)]:
                finish()
            return dst

        lax.fori_loop(0, n_super // HGRN_UNROLL, step, jnp.zeros((LANES, LANES), F32))

    def col(k):
        return pl.BlockSpec((seq, LANES), lambda p, b: (b, k * N_PAIRS + p))

    vec = lambda rows: pl.BlockSpec((rows, LANES), lambda p, b: (0, p))
    ospec = pl.BlockSpec((seq, LANES), lambda p, b: (b, p))
    piece = jax.ShapeDtypeStruct((t, GROUP), MXU_DTYPE)
    return pl.pallas_call(
        body, name="hgrn_bwd", grid=(N_PAIRS, n_seq),
        in_specs=[col(0), col(1), col(2), col(3), vec(2), vec(1), ospec, ospec,
                  pl.BlockSpec((1, 1, n_chunks, LANES, LANES), lambda p, b: (b, p, 0, 0, 0))],
        out_specs=[ospec, ospec, ospec, ospec, vec(2), vec(1)],
        out_shape=[piece, piece, piece, piece,
                   jax.ShapeDtypeStruct((2, GROUP), F32), jax.ShapeDtypeStruct((1, GROUP), F32)],
        compiler_params=_params(2),
    )(proj, proj, proj, proj, lower_bounds, norm_g, dmix, opre, states)


SB_SCALE = 1.0 / math.sqrt(D_HEAD)


SB_STEP = 2 * SB_BLOCK
QUERY_BLOCKS = (slice(0, SB_BLOCK), slice(SB_BLOCK, SB_STEP))


def _triangle(keep):
    row = lax.broadcasted_iota(jnp.int32, (SB_BLOCK, SB_BLOCK), 0)
    col = lax.broadcasted_iota(jnp.int32, (SB_BLOCK, SB_BLOCK), 1)
    return _ones_where(keep(row, col))


SB_HALF = SB_BLOCK // 2


def _keep(on_diagonal, x):
    if not on_diagonal:
        return x
    row = lax.broadcasted_iota(jnp.int32, (SB_HALF, SB_HALF), 0)
    col = lax.broadcasted_iota(jnp.int32, (SB_HALF, SB_HALF), 1)
    return jnp.where(col < row, x, 0.0)


def _lower_quadrants(fn, n_out, *tiles):
    def quadrant(r, c):
        return [t[r * SB_HALF:(r + 1) * SB_HALF, c * SB_HALF:(c + 1) * SB_HALF] for t in tiles]

    top, left, bottom = fn(True, *quadrant(0, 0)), fn(False, *quadrant(1, 0)), fn(True, *quadrant(1, 1))
    zero = jnp.zeros((SB_HALF, SB_HALF), F32)
    return [jnp.concatenate([jnp.concatenate([top[o], zero], axis=1), jnp.concatenate([left[o], bottom[o]], axis=1)],
                            axis=0) for o in range(n_out)]


def _sb_fwd(proj, norm_g, mixed, shards, n_seq, seq):
    t = n_seq * seq
    nq = seq // SB_STEP
    q0, k0, v0 = 0, N_PAIRS, 2 * N_PAIRS
    n_w = len(shards)
    n_steps = N_PAIRS * n_seq * nq
    tri = _triangle(lambda row, col: row >= col)

    def body(q_ref, k_ref, v_ref, ng_ref, tri_ref, mixed_in, *rest):
        del mixed_in
        shard_refs = rest[:n_w]
        out_ref, opre_ref, ctot_ref = rest[n_w:n_w + 3]
        gathered = rest[n_w + 3:2 * n_w + 3]
        send_sems, recv_sems, local_sems = rest[2 * n_w + 3:]
        i = pl.program_id(2)
        step = (pl.program_id(0) * n_seq + pl.program_id(1)) * nq + i
        plan = _GatherPlan(shard_refs, gathered, send_sems, recv_sems, local_sems)

        @pl.when(step == 0)
        def _():
            plan.start()

        @pl.when(step == (3 * n_steps) // 4)
        def _():
            plan.forward()

        masks = _head_masks()
        suffix = tri_ref[...]
        qhs = [[(q_ref[blk, :] * SB_SCALE * m).astype(MXU_DTYPE) for m in masks] for blk in QUERY_BLOCKS]

        def tiles(js, work, carry):
            rows = [pl.ds(pl.multiple_of(j * SB_BLOCK, SB_BLOCK), SB_BLOCK) for j in js]
            ks = [k_ref[rw, :].astype(MXU_DTYPE) for rw in rows]
            vs = [v_ref[rw, :].astype(MXU_DTYPE) for rw in rows]
            zs = [[_dot_nt(qh, ks[ts]) for qh in qhs[qb]] for qb, ts, _ in work]
            ccs = [[_lower_quadrants(lambda on, zq: (_keep(on, _softplus(zq)),), 1, z)[0] if diag else _softplus(z)
                    for z in zw] for zw, (_, _, diag) in zip(zs, work)]
            sums = [[jnp.dot(cc.astype(BF16), suffix, preferred_element_type=F32) for cc in cw] for cw in ccs]
            out = [list(per_block) for per_block in carry]
            for h in range(len(masks)):
                for w, (qb, ts, diag) in enumerate(work):
                    run, acc = out[qb][h]
                    logit = zs[w][h] - (sums[w][h] + run)
                    if diag:
                        a = _lower_quadrants(lambda on, lq: (_keep(on, jnp.exp(lq)),), 1, logit)[0]
                    else:
                        a = jnp.exp(logit)
                    out[qb][h] = (run + sums[w][h][:, 0:1], acc + _dot(a, vs[ts]))
            return tuple(tuple(per_block) for per_block in out)

        start = ((jnp.zeros((SB_BLOCK, 1), F32), jnp.zeros((SB_BLOCK, LANES), F32)),) * 2
        carry = tiles([2 * i, 2 * i + 1], [(0, 0, True), (1, 1, True), (1, 0, False)], (start, start))
        both = [(0, 0, False), (0, 1, False), (1, 0, False), (1, 1, False)]
        carry = lax.fori_loop(0, i, lambda s, cy: tiles([2 * (i - s) - 1, 2 * (i - s) - 2], both, cy), carry)
        opre = jnp.concatenate([cb[0][1] * masks[0] + cb[1][1] * masks[1] for cb in carry], axis=0)
        ctot = jnp.concatenate([cb[0][0] * masks[0] + cb[1][0] * masks[1] for cb in carry], axis=0)
        opre_ref[...] = opre
        ctot_ref[...] = ctot
        out_ref[...] = (opre * _head_rstd(opre, masks) * ng_ref[...]).astype(out_ref.dtype)

        @pl.when(step == n_steps - 1)
        def _():
            plan.finish()

    qspec = pl.BlockSpec((SB_STEP, LANES), lambda p, b, i: (b * nq + i, q0 + p))
    ospec = pl.BlockSpec((SB_STEP, LANES), lambda p, b, i: (b * nq + i, p))
    hbm = pl.BlockSpec(memory_space=pltpu.HBM)
    outs = pl.pallas_call(
        body, name="sb_fwd", grid=(N_PAIRS, n_seq, nq),
        in_specs=[qspec,
                  pl.BlockSpec((seq, LANES), lambda p, b, i: (b, k0 + p)),
                  pl.BlockSpec((seq, LANES), lambda p, b, i: (b, v0 + p)),
                  pl.BlockSpec((1, LANES), lambda p, b, i: (0, p)),
                  pl.BlockSpec(tri.shape, lambda p, b, i: (0, 0)), hbm] + [hbm] * n_w,
        out_specs=[pl.BlockSpec((SB_STEP, LANES), lambda p, b, i: (b * nq + i, N_PAIRS + p)), ospec, ospec]
        + [hbm] * n_w,
        out_shape=[jax.ShapeDtypeStruct(mixed.shape, mixed.dtype), jax.ShapeDtypeStruct((t, GROUP), F32),
                   jax.ShapeDtypeStruct((t, GROUP), F32)]
        + [jax.ShapeDtypeStruct((N_DEV,) + s.shape, s.dtype) for s in shards],
        scratch_shapes=[pltpu.SemaphoreType.DMA((n_w * _GatherPlan.COPIES,)),
                        pltpu.SemaphoreType.DMA((n_w * _GatherPlan.COPIES,)), pltpu.SemaphoreType.DMA((n_w,))],
        input_output_aliases={5: 0},
        compiler_params=_params(3),
    )(proj, proj, proj, norm_g, tri, mixed, *shards)
    return outs[0], outs[1], outs[2], list(outs[3:])


def _sb_bwd(proj, norm_g, dmix, opre, ctot, grads, n_seq, seq):
    t = n_seq * seq
    nq = seq // SB_STEP
    q0, k0, v0 = 0, N_PAIRS, 2 * N_PAIRS
    n_w = len(grads)
    n_steps = N_PAIRS * n_seq * nq
    tri = _triangle(lambda row, col: row <= col)

    def body(q_ref, k_ref, v_ref, ng_ref, tri_ref, dm_ref, opre_ref, ctot_ref, *rest):
        grad_refs = rest[:n_w]
        dq_ref, dk_ref, dv_ref, dng_ref = rest[n_w:n_w + 4]
        lands = rest[n_w + 4:2 * n_w + 4]
        dk_acc, dv_acc, send_sems, recv_sems, local_sems = rest[2 * n_w + 4:]
        p_id, b_id, i = pl.program_id(0), pl.program_id(1), pl.program_id(2)
        step = (p_id * n_seq + b_id) * nq + i
        plan = _ScatterPlan(grad_refs, lands, send_sems, recv_sems, local_sems)

        @pl.when(step == 0)
        def _():
            plan.start()

        masks = _head_masks()
        upto = tri_ref[...]

        def prefix(x):
            return jnp.dot(x.astype(BF16), upto, preferred_element_type=F32)

        @pl.when(i == 0)
        def _():
            dk_acc[...] = jnp.zeros_like(dk_acc)
            dv_acc[...] = jnp.zeros_like(dv_acc)

        @pl.when(jnp.logical_and(b_id == 0, i == 0))
        def _():
            dng_ref[...] = jnp.zeros_like(dng_ref)

        o = opre_ref[...]
        rs = _head_rstd(o, masks)
        oh = o * rs
        dm = dm_ref[...]
        dng_ref[...] += jnp.sum(dm * oh, axis=0, keepdims=True)
        doh = dm * ng_ref[...]
        do = rs * (doh - oh * _head_mean(doh * oh, masks))

        heads = range(len(masks))
        qs = [q_ref[blk, :] * SB_SCALE for blk in QUERY_BLOCKS]
        dos = [do[blk] for blk in QUERY_BLOCKS]
        qhs = [[(q * m).astype(MXU_DTYPE) for m in masks] for q in qs]
        doms = [[(d * m).astype(MXU_DTYPE) for m in masks] for d in dos]
        head_rows = [jnp.where(jnp.right_shift(lax.broadcasted_iota(jnp.int32, (LANES, 1), 0), 6) == h, 1.0, 0.0)
                     for h in heads]
        qhts = [[(qt * hr).astype(MXU_DTYPE) for hr in head_rows] for qt in [q.astype(F32).T for q in qs]]
        domts = [[(dt * hr).astype(MXU_DTYPE) for hr in head_rows] for dt in [d.T for d in dos]]
        totals = [[ctot_ref[blk, h * D_HEAD:h * D_HEAD + 1] for h in heads] for blk in QUERY_BLOCKS]

        def tiles(js, work, carry):
            rows = [pl.ds(pl.multiple_of(j * SB_BLOCK, SB_BLOCK), SB_BLOCK) for j in js]
            ks = [k_ref[rw, :].astype(MXU_DTYPE) for rw in rows]
            vs = [v_ref[rw, :].astype(MXU_DTYPE) for rw in rows]
            zs = [[_dot_nt(qhs[qb][h], ks[ts]) for h in heads] for qb, ts, _ in work]
            das = [[_dot_nt(doms[qb][h], vs[ts]) for h in heads] for qb, ts, _ in work]
            def gates(on, zq):
                sp = _softplus(zq)
                return _keep(on, sp), zq - sp, _keep(on, jnp.exp(zq - sp))

            trio = [[_lower_quadrants(gates, 3, z) if diag else gates(False, z) for z in zw]
                    for zw, (_, _, diag) in zip(zs, work)]
            ccs, lsigs, sigs = ([[t[o] for t in tw] for tw in trio] for o in range(3))
            pres = [[prefix(cc) for cc in cw] for cw in ccs]
            out = [list(per_block) for per_block in carry]
            for h in heads:
                for w, (qb, ts, diag) in enumerate(work):
                    pc, pdl, dq_h = out[qb][h]
                    logit = lsigs[w][h] + pres[w][h] - (totals[qb][h] - pc)
                    if diag:
                        a = _lower_quadrants(lambda on, lq: (_keep(on, jnp.exp(lq)),), 1, logit)[0]
                    else:
                        a = jnp.exp(logit)
                    dl = a * das[w][h]
                    dv_acc[js[ts]] += _dot(domts[qb][h], a)
                    dpre = prefix(dl)
                    dz = dl - sigs[w][h] * (pdl + dpre)
                    dzb = dz.astype(MXU_DTYPE)
                    dk_acc[js[ts]] += _dot(qhts[qb][h], dzb)
                    out[qb][h] = (pc + pres[w][h][:, SB_BLOCK - 1:SB_BLOCK], pdl + dpre[:, SB_BLOCK - 1:SB_BLOCK],
                                  dq_h + _dot(dzb, ks[ts]))
            return tuple(tuple(per_block) for per_block in out)

        zero = jnp.zeros((SB_BLOCK, 1), F32)
        start = ((zero, zero, jnp.zeros((SB_BLOCK, LANES), F32)),) * 2
        both = [(0, 0, False), (0, 1, False), (1, 0, False), (1, 1, False)]
        carry = lax.fori_loop(0, i, lambda s, cy: tiles([2 * s, 2 * s + 1], both, cy), (start, start))
        carry = tiles([2 * i, 2 * i + 1], [(0, 0, True), (1, 0, False), (1, 1, True)], carry)
        dq = jnp.concatenate([cb[0][2] * masks[0] + cb[1][2] * masks[1] for cb in carry], axis=0)
        dq_ref[...] = (dq * SB_SCALE).astype(dq_ref.dtype)

        @pl.when(i == nq - 1)
        def _():
            for j in range(seq // SB_BLOCK):
                tile_rows = slice(j * SB_BLOCK, (j + 1) * SB_BLOCK)
                dk_ref[tile_rows, :] = dk_acc[j].T.astype(dk_ref.dtype)
                dv_ref[tile_rows, :] = dv_acc[j].T.astype(dv_ref.dtype)

        @pl.when(step == n_steps - 1)
        def _():
            plan.finish()

    qspec = pl.BlockSpec((SB_STEP, LANES), lambda p, b, i: (b * nq + i, q0 + p))
    ospec = pl.BlockSpec((SB_STEP, LANES), lambda p, b, i: (b * nq + i, p))
    dmspec = pl.BlockSpec((SB_STEP, LANES), lambda p, b, i: (b * nq + i, N_PAIRS + p))
    full = lambda k: pl.BlockSpec((seq, LANES), lambda p, b, i: (b, k + p))
    vec = pl.BlockSpec((1, LANES), lambda p, b, i: (0, p))
    hbm = pl.BlockSpec(memory_space=pltpu.HBM)
    piece = jax.ShapeDtypeStruct((t, GROUP), MXU_DTYPE)
    outs = pl.pallas_call(
        body, name="sb_bwd", grid=(N_PAIRS, n_seq, nq),
        in_specs=[qspec, full(k0), full(v0), vec, pl.BlockSpec(tri.shape, lambda p, b, i: (0, 0)), dmspec, ospec, ospec]
        + [hbm] * n_w,
        out_specs=[ospec, full(0), full(0), vec] + [hbm] * n_w,
        out_shape=[piece, piece, piece, jax.ShapeDtypeStruct((1, GROUP), F32)]
        + [jax.ShapeDtypeStruct(g.shape, g.dtype) for g in grads],
        scratch_shapes=[pltpu.VMEM((seq // SB_BLOCK, LANES, SB_BLOCK), F32),
                        pltpu.VMEM((seq // SB_BLOCK, LANES, SB_BLOCK), F32),
                        pltpu.SemaphoreType.DMA((n_w * (N_DEV - 1),)), pltpu.SemaphoreType.DMA((n_w * (N_DEV - 1),)),
                        pltpu.SemaphoreType.DMA((n_w,))],
        compiler_params=_params(3),
    )(proj, proj, proj, norm_g, tri, dmix, opre, ctot, *[pltpu.with_memory_space_constraint(g, pltpu.HBM) for g in grads])
    return outs[0], outs[1], outs[2], outs[3], list(outs[4:])


def _mesh_place():
    x, y, c = lax.axis_index("x"), lax.axis_index("y"), lax.axis_index("c")
    return x, y, c


def _peer(x, y, c, k):
    px = lax.rem(x + ((k >> 2) & 1), 2)
    py = lax.rem(y + ((k >> 1) & 1), 2)
    pc = lax.rem(c + (k & 1), 2)
    return (px, py, pc), 4 * px + 2 * py + pc


def _remote(src, dst, send_sem, recv_sem, to):
    return pltpu.make_async_remote_copy(src_ref=src, dst_ref=dst, send_sem=send_sem, recv_sem=recv_sem,
                                        device_id=to, device_id_type=pl.DeviceIdType.MESH)


class _GatherPlan:
    COPIES = 7

    def __init__(self, shards, gathered, send_sems, recv_sems, local_sems):
        x, y, c = _mesh_place()
        self.c = c
        self.me = (x, y, c)
        self.sibling = (x, y, 1 - c)
        self.chips = [(1 - x, y), (x, 1 - y), (1 - x, 1 - y)]
        self.tensors = list(zip(shards, gathered))
        self.send_sems, self.recv_sems, self.local_sems = send_sems, recv_sems, local_sems

    @staticmethod
    def _index(place):
        return 4 * place[0] + 2 * place[1] + place[2]

    def _copy(self, w, k, block, to, own=False):
        shard, gathered = self.tensors[w]
        slot = gathered.at[self._index(block)]
        n = w * self.COPIES + k
        return _remote(shard if own else slot, slot, self.send_sems.at[n], self.recv_sems.at[n], to)

    def _local(self, w):
        shard, gathered = self.tensors[w]
        return pltpu.make_async_copy(shard, gathered.at[self._index(self.me)], self.local_sems.at[w])

    def _first(self, w):
        return [self._copy(w, 0, self.me, self.sibling, own=True)] + [
            self._copy(w, 1 + j, self.me, (*chip, self.c), own=True) for j, chip in enumerate(self.chips)]

    def _passed(self, w):
        return [self._copy(w, 4 + j, (*chip, self.c), self.sibling) for j, chip in enumerate(self.chips)]

    def start(self):
        for w in range(len(self.tensors)):
            self._local(w).start()
            for cp in self._first(w):
                cp.start()

    def forward(self):
        for w in range(len(self.tensors)):
            passed = self._passed(w)
            for j, chip in enumerate(self.chips):
                self._copy(w, 1 + j, (*chip, self.c), self.me).wait_recv()
                passed[j].start()

    def finish(self):
        for w in range(len(self.tensors)):
            self._copy(w, 0, self.sibling, self.me).wait_recv()
            for j, chip in enumerate(self.chips):
                self._copy(w, 4 + j, (*chip, 1 - self.c), self.me).wait_recv()
            for cp in self._first(w) + self._passed(w):
                cp.wait_send()
            self._local(w).wait()


class _ScatterPlan:
    def __init__(self, grads, lands, send_sems, recv_sems, local_sems):
        self.place = _mesh_place()
        x, y, c = self.place
        self.me = 4 * x + 2 * y + c
        self.tensors = list(zip(grads, lands))
        self.send_sems, self.recv_sems, self.local_sems = send_sems, recv_sems, local_sems

    def _copies(self, w):
        grad, land = self.tensors[w]
        out = []
        for k in range(1, N_DEV):
            peer, pidx = _peer(*self.place, k)
            n = w * (N_DEV - 1) + k - 1
            sems = (self.send_sems.at[n], self.recv_sems.at[n], peer)
            out.append((_remote(grad.at[pidx], land.at[self.me], *sems), _remote(grad.at[pidx], land.at[pidx], *sems)))
        return out

    def _local(self, w):
        grad, land = self.tensors[w]
        return pltpu.make_async_copy(grad.at[self.me], land.at[self.me], self.local_sems.at[w])

    def start(self):
        for w in range(len(self.tensors)):
            self._local(w).start()
            for send, _ in self._copies(w):
                send.start()

    def finish(self):
        for w in range(len(self.tensors)):
            copies = self._copies(w)
            for _, arrival in copies:
                arrival.wait_recv()
            for send, _ in copies:
                send.wait_send()
            self._local(w).wait()


def _cast_shards(shards):
    def body(*refs):
        n = len(refs) // 2
        for src, dst in zip(refs[:n], refs[n:]):
            dst[...] = src[...].astype(dst.dtype)

    shapes = [jax.ShapeDtypeStruct(s.shape, BF16) for s in shards]
    return pl.pallas_call(
        body, name="cast_shards", grid=(1,),
        in_specs=[_whole(s) for s in shards], out_specs=[_whole(s) for s in shapes], out_shape=shapes,
        compiler_params=_params(1),
    )(*shards)


def _gather_w_in(shard, x, g, chunk=512):
    rows, cols = shard.shape
    t, d = x.shape
    chunk = min(chunk, t)
    n_chunks = t // chunk

    def body(w_ref, x_ref, g_ref, out_ref, h_ref, x_buf, send_sems, recv_sems, local_sems, x_sems):
        plan = _GatherPlan([w_ref], [out_ref], send_sems, recv_sems, local_sems)
        plan.start()

        def fetch(j):
            return pltpu.make_async_copy(x_ref.at[pl.ds(j * chunk, chunk)], x_buf.at[j % 2], x_sems.at[j % 2])

        fetch(0).start()
        gv = g_ref[...]
        for j in range(n_chunks):
            if j + 1 < n_chunks:
                fetch(j + 1).start()
            fetch(j).wait()
            xv = x_buf[j % 2]
            r = lax.rsqrt(jnp.mean(xv * xv, axis=-1, keepdims=True) + EPS)
            h_ref[j * chunk:(j + 1) * chunk, :] = (xv * r * gv).astype(h_ref.dtype)
        plan.forward()
        plan.finish()

    vmem = pl.BlockSpec(memory_space=pltpu.VMEM)
    hbm = pl.BlockSpec(memory_space=pltpu.HBM)
    return pl.pallas_call(
        body, name="gather_w_in", in_specs=[hbm, hbm, vmem], out_specs=[hbm, vmem],
        out_shape=[jax.ShapeDtypeStruct((N_DEV, rows, cols), shard.dtype), jax.ShapeDtypeStruct((t, d), MXU_DTYPE)],
        scratch_shapes=[pltpu.VMEM((2, chunk, d), F32), pltpu.SemaphoreType.DMA((_GatherPlan.COPIES,)),
                        pltpu.SemaphoreType.DMA((_GatherPlan.COPIES,)), pltpu.SemaphoreType.DMA((1,)),
                        pltpu.SemaphoreType.DMA((2,))],
        compiler_params=pltpu.CompilerParams(vmem_limit_bytes=VMEM_LIMIT),
    )(shard, x, g)


def _dw_rows(pieces, b, name, tn=512, tt=1024):
    t, n = b.shape
    widths = [p.shape[1] for p in pieces]
    rows = sum(widths)
    tt = min(tt, t)
    steps = t // tt
    n_p = len(pieces)

    def body(*refs):
        piece_refs, b_ref, o_ref, acc = refs[:n_p], refs[n_p], refs[n_p + 1], refs[n_p + 2]
        s = pl.program_id(1)

        @pl.when(s == 0)
        def _():
            acc[...] = jnp.zeros_like(acc)

        bv = b_ref[...]
        off = 0
        for p_ref, width in zip(piece_refs, widths):
            acc[off:off + width, :] += _dot_tn(p_ref[...], bv)
            off += width

        @pl.when(s == steps - 1)
        def _():
            o_ref[...] = acc[...].astype(o_ref.dtype)

    return pl.pallas_call(
        body, name=name, grid=(n // tn, steps),
        in_specs=[pl.BlockSpec((tt, width), lambda j, s: (s, 0)) for width in widths]
        + [pl.BlockSpec((tt, tn), lambda j, s: (s, j))],
        out_specs=pl.BlockSpec((rows, tn), lambda j, s: (0, j)),
        out_shape=jax.ShapeDtypeStruct((rows, n), BF16),
        scratch_shapes=[pltpu.VMEM((rows, tn), F32)],
        compiler_params=_params(2),
    )(*pieces, b)


def _dh_norm_bwd(pairs, x, g, res, name, tm=512, after=(), next_w=None):
    m, d = x.shape
    tm = min(tm, m)
    n_p = len(pairs)
    n_steps = m // tm
    n_in = 2 * n_p + 3 + len(after) + (next_w is not None)

    def body(*refs):
        x_ref, g_ref, res_ref = refs[2 * n_p:2 * n_p + 3]
        dx_ref, dg_ref = refs[n_in:n_in + 2]

        @pl.when(pl.program_id(0) == 0)
        def _():
            dg_ref[...] = jnp.zeros_like(dg_ref)

        dh = None
        for q in range(n_p):
            part = _dot(refs[2 * q][...], refs[2 * q + 1][...])
            dh = part if dh is None else dh + part
        xv = x_ref[...]
        r = lax.rsqrt(jnp.mean(xv * xv, axis=-1, keepdims=True) + EPS)
        xh = xv * r
        dxh = dh * g_ref[...]
        dxv = res_ref[...] + r * (dxh - xh * jnp.mean(dxh * xh, axis=-1, keepdims=True))
        dx_ref[...] = dxv
        dg_ref[...] += jnp.sum(dh * xh, axis=0, keepdims=True)
        if next_w is not None:
            refs[n_in + 2][...] = _dot_nt(dxv, refs[n_in - 1][...])

    in_specs, args = [], []
    for a, w, r in pairs:
        k = a.shape[1]
        in_specs += [pl.BlockSpec((tm, k), lambda i: (i, 0)),
                     pl.BlockSpec((k, d), functools.partial(lambda i, r: (r, 0), r=r or 0),
                                  pipeline_mode=pl.Buffered(1))]
        args += [a, w]
    row = pl.BlockSpec((tm, d), lambda i: (i, 0))
    vec = pl.BlockSpec((1, d), lambda i: (0, 0))
    extra_in, extra_out, extra_shape = [], [], []
    if next_w is not None:
        e = next_w.shape[0]
        extra_in = [pl.BlockSpec((e, d), lambda i: (0, 0), pipeline_mode=pl.Buffered(1))]
        extra_out = [pl.BlockSpec((tm, e), lambda i: (i, 0))]
        extra_shape = [jax.ShapeDtypeStruct((m, e), F32)]
        args_tail = (next_w,)
    else:
        args_tail = ()
    return pl.pallas_call(
        body, name=name, grid=(n_steps,),
        in_specs=in_specs + [row, vec, row] + [pl.BlockSpec(memory_space=pl.ANY)] * len(after) + extra_in,
        out_specs=[row, vec] + extra_out,
        out_shape=[jax.ShapeDtypeStruct((m, d), F32), jax.ShapeDtypeStruct((1, d), F32)] + extra_shape,
        compiler_params=_params(1),
    )(*args, x, g, res, *after, *args_tail)


def _exchange_start(grad, name):
    def body(g_ref, land_ref, send_sem, recv_sem, local_sem, g_thru, land_thru, token):
        del g_thru, land_thru
        x, y, c = _mesh_place()
        me = 4 * x + 2 * y + c
        pltpu.make_async_copy(g_ref.at[me], land_ref.at[me], local_sem).start()
        for k in range(1, N_DEV):
            peer, pidx = _peer(x, y, c, k)
            _remote(g_ref.at[pidx], land_ref.at[me], send_sem, recv_sem, peer).start()
        token[...] = jnp.zeros_like(token)

    hbm = pl.BlockSpec(memory_space=pltpu.HBM)
    sem = pl.BlockSpec(memory_space=pltpu.SEMAPHORE)
    buf = pltpu.HBM(grad.shape, grad.dtype)
    return pl.pallas_call(
        body, name=name,
        out_shape=(pltpu.SemaphoreType.DMA(()), pltpu.SemaphoreType.DMA(()), pltpu.SemaphoreType.DMA(()), buf, buf,
                   jax.ShapeDtypeStruct((8, LANES), F32)),
        in_specs=(hbm, hbm), out_specs=(sem, sem, sem, hbm, hbm, pl.BlockSpec(memory_space=pltpu.VMEM)),
        input_output_aliases={0: 3, 1: 4},
        compiler_params=pltpu.CompilerParams(has_side_effects=pltpu.SideEffectType.DATAFLOW_SIDE_EFFECTING),
    )(pltpu.with_memory_space_constraint(grad, pltpu.HBM),
      pltpu.with_memory_space_constraint(lax.empty(grad.shape, grad.dtype), pltpu.HBM))


def _exchange_wait(send_sem, recv_sem, local_sem, grad, land, after, name):
    def body(g_ref, land_ref, send_sem, recv_sem, local_sem, *rest):
        x, y, c = _mesh_place()
        peer, _ = _peer(x, y, c, 1)
        others = pl.ds(0, N_DEV - 1)
        seven = _remote(g_ref.at[others], land_ref.at[others], send_sem, recv_sem, peer)
        seven.wait_send()
        seven.wait_recv()
        pltpu.make_async_copy(g_ref.at[0], land_ref.at[0], local_sem).wait()

    hbm = pl.BlockSpec(memory_space=pltpu.HBM)
    sem = pl.BlockSpec(memory_space=pltpu.SEMAPHORE)
    buf = pltpu.HBM(grad.shape, grad.dtype)
    return pl.pallas_call(
        body, name=name, out_shape=(buf, buf),
        in_specs=(hbm, hbm, sem, sem, sem) + (pl.BlockSpec(memory_space=pl.ANY),) * len(after), out_specs=(hbm, hbm),
        input_output_aliases={0: 0, 1: 1},
        compiler_params=pltpu.CompilerParams(has_side_effects=pltpu.SideEffectType.DATAFLOW_SIDE_EFFECTING),
    )(grad, land, send_sem, recv_sem, local_sem, *after)[1]


SMALL_LAYOUT = ((0, 0, 0, 0, D_MODEL), (1, 0, 1, 0, GROUP), (1, 1, 1, GROUP, GROUP), (2, 0, 2, 0, GROUP),
                (3, 0, 2, GROUP, GROUP), (4, 0, 3, 0, D_MODEL), (5, 0, 4, 0, D_MODEL))
LOSS_ROW = 5
N_SMALL = 6


def _place_small(dst, srcs):
    dst[...] = jnp.zeros_like(dst)
    for p, sr, dr, dc, width in SMALL_LAYOUT:
        dst[dr:dr + 1, dc:dc + width] = srcs[p][sr:sr + 1, :]


def _small_pack(grads, loss_part):
    def body(*refs):
        g_in, loss_in, out_ref, pack = refs[:N_SMALL], refs[N_SMALL], refs[N_SMALL + 1], refs[N_SMALL + 2]
        _place_small(pack, g_in)
        pack[LOSS_ROW:LOSS_ROW + 1, 0:LANES] = loss_in[...]
        for d in range(N_DEV):
            out_ref[d] = pack[...]

    args = [*grads, loss_part]
    packs = jax.ShapeDtypeStruct((N_DEV, SMALL_ROWS, D_MODEL), F32)
    return pl.pallas_call(
        body, name="small_pack", grid=(1,),
        in_specs=[_whole(a) for a in args], out_specs=_whole(packs), out_shape=packs,
        scratch_shapes=[pltpu.VMEM((SMALL_ROWS, D_MODEL), F32)], compiler_params=_params(1),
    )(*args)


def _small_adamw(land, ws, ms, vs):
    def body(*refs):
        g_ref = refs[0]
        params = [refs[1 + q * N_SMALL:1 + (q + 1) * N_SMALL] for q in range(3)]
        o0 = 1 + 3 * N_SMALL
        outs = [refs[o0 + q * N_SMALL:o0 + (q + 1) * N_SMALL] for q in range(4)]
        loss_out = refs[o0 + 4 * N_SMALL]
        wp, mp, vp = refs[o0 + 4 * N_SMALL + 1:]
        for dst, srcs in zip((wp, mp, vp), params):
            _place_small(dst, srcs)
        g = g_ref[0]
        for d in range(1, N_DEV):
            g = g + g_ref[d]
        delta, nm, nv = _adam(wp[...], g, mp[...], vp[...])
        for val, out in zip((g, delta, nm, nv), outs):
            for p, sr, dr, dc, width in SMALL_LAYOUT:
                out[p][sr:sr + 1, :] = val[dr:dr + 1, dc:dc + width]
        loss_out[...] = g[LOSS_ROW:LOSS_ROW + 1, 0:LANES]

    args = [land, *ws, *ms, *vs]
    shapes = [jax.ShapeDtypeStruct(w.shape, F32) for w in ws] * 4 + [jax.ShapeDtypeStruct((1, LANES), F32)]
    packed = pltpu.VMEM((SMALL_ROWS, D_MODEL), F32)
    outs = pl.pallas_call(
        body, name="small_adamw", grid=(1,),
        in_specs=[_whole(a) for a in args], out_specs=[_whole(s) for s in shapes], out_shape=shapes,
        scratch_shapes=[packed, packed, packed], compiler_params=_params(1),
    )(*args)
    return [outs[q * N_SMALL:(q + 1) * N_SMALL] for q in range(4)], outs[4 * N_SMALL]


def _adam(w, g, m, v):
    m = ADAM_B1 * m + (1.0 - ADAM_B1) * g
    v = ADAM_B2 * v + (1.0 - ADAM_B2) * (g * g)
    m_hat = m / (1.0 - ADAM_B1 ** ADAM_STEP)
    v_hat = v / (1.0 - ADAM_B2 ** ADAM_STEP)
    delta = -ADAM_LR * (m_hat / (jnp.sqrt(v_hat) + ADAM_EPS) + ADAM_WD * w)
    return delta, m, v


def _reduce_adamw(land, w, m, v, name, tr, after=()):
    _, rows, width = land.shape

    def body(land_ref, w_ref, m_ref, v_ref, *rest):
        g_ref, d_ref, nm_ref, nv_ref = rest[len(after):]
        g = land_ref[0].astype(F32)
        for d in range(1, N_DEV):
            g = g + land_ref[d].astype(F32)
        delta, nm, nv = _adam(w_ref[...], g, m_ref[...], v_ref[...])
        g_ref[...] = g
        d_ref[...] = delta
        nm_ref[...] = nm
        nv_ref[...] = nv

    row = pl.BlockSpec((tr, width), lambda i: (i, 0))
    out = jax.ShapeDtypeStruct((rows, width), F32)
    return pl.pallas_call(
        body, name=name, grid=(rows // tr,),
        in_specs=[pl.BlockSpec((N_DEV, tr, width), lambda i: (0, i, 0)), row, row, row]
        + [pl.BlockSpec(memory_space=pl.ANY)] * len(after),
        out_specs=[row, row, row, row], out_shape=[out, out, out, out],
        compiler_params=_params(1),
    )(land, w, m, v, *after)


def kernel(x, mix_norm_g, w_in, lower_bounds, hgrn_norm_g, sb_norm_g, w_out, ffn_norm_g, w_gate, w_up, w_down, final_norm_g, loss_target, m_mix_norm_g, m_w_in, m_lower_bounds, m_hgrn_norm_g, m_sb_norm_g, m_w_out, m_ffn_norm_g, m_w_gate, m_w_up, m_w_down, m_final_norm_g, v_mix_norm_g, v_w_in, v_lower_bounds, v_hgrn_norm_g, v_sb_norm_g, v_w_out, v_ffn_norm_g, v_w_gate, v_w_up, v_w_down, v_final_norm_g):
    n_seq, seq, d = x.shape
    t = n_seq * seq
    x2d = x.reshape(t, d)
    tgt = loss_target.reshape(t, d)
    final_g = final_norm_g.reshape(1, d)
    col_sharded = (True, False, True, True, False)

    def as_rows(ws):
        return [w[0].T if tr else w[0] for w, tr in zip(ws, col_sharded)]

    big_w = as_rows([w_in, w_out, w_gate, w_up, w_down])
    big_m = as_rows([m_w_in, m_w_out, m_w_gate, m_w_up, m_w_down])
    big_v = as_rows([v_w_in, v_w_out, v_w_gate, v_w_up, v_w_down])

    sh_in, sh_out, sh_gate, sh_up, sh_down = _cast_shards(big_w)
    wt_in, h1 = _gather_w_in(sh_in, x2d, mix_norm_g)
    wt_in = wt_in.reshape(IN_COLS, d)
    proj_h = _mm_nt(h1, wt_in, "proj_hgrn", rows=(0, 4 * GROUP), tm=1024, tk=4 * GROUP)
    proj_s = _mm_nt(h1, wt_in, "proj_sb", rows=(4 * GROUP, 3 * GROUP), out_dtype=MXU_DTYPE, tm=1024, tk=512)
    mixed, oa_pre, states = _hgrn_fwd(proj_h, lower_bounds, hgrn_norm_g, n_seq, seq)
    mixed, ob_pre, ctot, gathered = _sb_fwd(proj_s, sb_norm_g, mixed, [sh_out, sh_gate, sh_up, sh_down], n_seq, seq)
    wf_out = gathered[0].reshape(d, d)
    wt_gate = gathered[1].reshape(D_FF, d)
    wt_up = gathered[2].reshape(D_FF, d)
    wf_down = gathered[3].reshape(D_FF, d)
    x1, h2 = _mix_out_norm(mixed, wf_out, x2d, ffn_norm_g, "mix_out")
    gate, up, ff = _ffn_up(h2, wt_gate, wt_up, "ffn_up")
    dx2, dx2m, d_final_g, loss_part = _ffn_down_loss(ff, wf_down, x1, tgt, final_g, "ffn_down_loss")

    dgate, dup = _ffn_bwd_act(dx2m, wf_down, gate, up, "ffn_bwd_act")
    dw_down = _mm_tn(ff, dx2m, "dw_down", tk=1408, tn=1024).reshape(N_DEV, D_FF // N_DEV, d)
    dw_gate = _mm_tn(dgate, h2, "dw_gate", tk=1408, tn=1024).reshape(N_DEV, D_FF // N_DEV, d)
    dw_up = _mm_tn(dup, h2, "dw_up", tk=1408, tn=1024).reshape(N_DEV, D_FF // N_DEV, d)
    dx1, d_ffn_g, dmix = _dh_norm_bwd([(dgate, wt_gate, None), (dup, wt_up, None)], x1, ffn_norm_g, dx2, "dh_ffn",
                                      tm=256, next_w=wf_out)
    dw_out = _mm_tn(mixed, dx1, "dw_out", tk=1024, tn=1024).reshape(N_DEV, d // N_DEV, d)
    dsq, dsk, dsv, d_sb_g, lands = _sb_bwd(proj_s, sb_norm_g, dmix, ob_pre, ctot, [dw_out, dw_gate, dw_up, dw_down],
                                            n_seq, seq)
    dhq, dhf, dhi, dhg, d_lb, d_hgrn_g = _hgrn_bwd(proj_h, lower_bounds, hgrn_norm_g, dmix, oa_pre, states, n_seq,
                                                   seq)
    dproj = [dhq, dhf, dhi, dhg, dsq, dsk, dsv]
    dw_in = _dw_rows(dproj, h1, "dw_in").reshape(N_DEV, IN_COLS // N_DEV, d)
    send_sem, recv_sem, local_sem, dw_in, land_in, token = _exchange_start(dw_in, "dw_in_send")
    dx, d_mix_g = _dh_norm_bwd([(piece, wt_in, k) for k, piece in enumerate(dproj)], x2d, mix_norm_g, dx1, "dh_mix",
                               after=(token,))

    tiles = {"in": 224, "out": 128, "gate": 176, "up": 176, "down": 176}
    keys = list(tiles)
    packs = _small_pack([d_mix_g, d_lb, d_hgrn_g, d_sb_g, d_ffn_g, d_final_g], loss_part)
    small_sems = _exchange_start(packs, "small_send")
    small_token = small_sems[-1]
    rest = [_reduce_adamw(land, w, m, v, "adamw_" + key, tr=tiles[key], after=(token, small_token))
            for key, land, w, m, v in zip(keys[1:], lands, big_w[1:], big_m[1:], big_v[1:])]
    small_land = _exchange_wait(*small_sems[:-1], [res[0] for res in rest], "small_await")
    small, loss_row = _small_adamw(
        small_land, [mix_norm_g, lower_bounds, hgrn_norm_g, sb_norm_g, ffn_norm_g, final_g],
        [m_mix_norm_g, m_lower_bounds, m_hgrn_norm_g, m_sb_norm_g, m_ffn_norm_g, m_final_norm_g.reshape(1, d)],
        [v_mix_norm_g, v_lower_bounds, v_hgrn_norm_g, v_sb_norm_g, v_ffn_norm_g, v_final_norm_g.reshape(1, d)])
    land_in = _exchange_wait(send_sem, recv_sem, local_sem, dw_in, land_in,
                             [dx, loss_row] + [res[0] for res in rest], "dw_in_await")
    big = [_reduce_adamw(land_in, big_w[0], big_m[0], big_v[0], "adamw_in", tr=tiles["in"])] + rest
    big = [[r.T if tr else r for r in res] for res, tr in zip(big, col_sharded)]

    outs = [loss_row[0, 0], dx.reshape(n_seq, seq, d)]
    for q in range(4):
        b_in, b_out, b_gate, b_up, b_down = [res[q][None] for res in big]
        s_mix, s_lb, s_hgrn, s_sb, s_ffn, s_final = small[q]
        outs += [s_mix, b_in, s_lb, s_hgrn, s_sb, b_out, s_ffn, b_gate, b_up, b_down, s_final.reshape(d)]
    return tuple(outs)
```

```python
import functools
import math

import jax
import jax.numpy as jnp
from jax import lax
from jax.experimental import pallas as pl
from jax.experimental.pallas import tpu as pltpu

F32 = jnp.float32
BF16 = jnp.bfloat16
MXU_DTYPE = BF16

EPS = 1e-6
D_MODEL = 1024
N_HEADS = 8
D_HEAD = 64
GROUP = N_HEADS * D_HEAD
IN_COLS = 7 * GROUP
D_FF = 2816
CHUNK = 64
LANES = 128
N_PAIRS = GROUP // LANES
SUPER = 256
SB_BLOCK = 256
N_DEV = 8

ADAM_LR = 0.001
ADAM_B1 = 0.9
ADAM_B2 = 0.999
ADAM_EPS = 1e-08
ADAM_WD = 0.01
ADAM_STEP = 10

SMALL_ROWS = 8
FF_TILE = D_FF // 2

VMEM_LIMIT = 48 * 1024 * 1024


def _params(n_axes, vmem=VMEM_LIMIT):
    return pltpu.CompilerParams(dimension_semantics=("arbitrary",) * n_axes, vmem_limit_bytes=vmem)


def _whole(a):
    return pl.BlockSpec(a.shape, functools.partial(lambda i, nd: (0,) * nd, nd=len(a.shape)))


def _dot(a, b):
    return jnp.dot(a.astype(MXU_DTYPE), b.astype(MXU_DTYPE), preferred_element_type=F32)


def _dot_nt(a, b):
    return lax.dot_general(a.astype(MXU_DTYPE), b.astype(MXU_DTYPE), (((1,), (1,)), ((), ())),
                           preferred_element_type=F32)


def _dot_tn(a, b):
    return lax.dot_general(a.astype(MXU_DTYPE), b.astype(MXU_DTYPE), (((0,), (0,)), ((), ())),
                           preferred_element_type=F32)


def _split(x, parts):
    out, r = [], x
    for _ in range(parts):
        h = r.astype(BF16)
        out.append(h)
        r = r - h.astype(F32)
    return out


def _rsum_left(u, x, parts):
    acc = None
    for h in _split(x, parts):
        d = jnp.dot(u, h, preferred_element_type=F32)
        acc = d if acc is None else acc + d
    return acc


def _ones_where(mask):
    return jnp.where(mask, 1.0, 0.0).astype(BF16)


def _sigmoid(x):
    return 1.0 / (1.0 + jnp.exp(-x))


def _softplus(x):
    return jnp.maximum(x, 0.0) + jnp.log(1.0 + jnp.exp(-jnp.abs(x)))


def _head_masks():
    lane = lax.broadcasted_iota(jnp.int32, (1, LANES), 1)
    return [jnp.where(lane < D_HEAD, 1.0, 0.0), jnp.where(lane >= D_HEAD, 1.0, 0.0)]


def _head_rstd(o, masks):
    sq = o * o
    r = None
    for m in masks:
        ms = jnp.sum(sq * m, axis=1, keepdims=True) * (1.0 / D_HEAD)
        t = lax.rsqrt(ms + EPS) * m
        r = t if r is None else r + t
    return r


def _head_mean(t, masks):
    out = None
    for m in masks:
        v = jnp.sum(t * m, axis=1, keepdims=True) * (1.0 / D_HEAD) * m
        out = v if out is None else out + v
    return out


def _mix_out_norm(a, w, res, g, name, tm=512):
    m, k = a.shape
    d = w.shape[1]
    tm = min(tm, m)

    def body(a_ref, w_ref, res_ref, g_ref, x_ref, h_ref):
        xv = res_ref[...] + _dot(a_ref[...], w_ref[...])
        x_ref[...] = xv
        r = lax.rsqrt(jnp.mean(xv * xv, axis=-1, keepdims=True) + EPS)
        h_ref[...] = (xv * r * g_ref[...]).astype(h_ref.dtype)

    row = pl.BlockSpec((tm, d), lambda i: (i, 0))
    return pl.pallas_call(
        body, name=name, grid=(m // tm,),
        in_specs=[pl.BlockSpec((tm, k), lambda i: (i, 0)), pl.BlockSpec((k, d), lambda i: (0, 0)), row,
                  pl.BlockSpec((1, d), lambda i: (0, 0))],
        out_specs=[row, row],
        out_shape=[jax.ShapeDtypeStruct((m, d), F32), jax.ShapeDtypeStruct((m, d), MXU_DTYPE)],
        compiler_params=_params(1),
    )(a, w, res, g)


def _ffn_down_loss(a, w, res, target, g, name, tm=512):
    m, k = a.shape
    d = w.shape[1]
    tm = min(tm, m)

    def body(a_ref, w_ref, res_ref, t_ref, g_ref, dx_ref, dxm_ref, dg_ref, loss_ref):
        xv = res_ref[...] + _dot(a_ref[...], w_ref[...])
        gv = g_ref[...]
        r = lax.rsqrt(jnp.mean(xv * xv, axis=-1, keepdims=True) + EPS)
        xh = xv * r
        e = xh * gv - t_ref[...]
        dy = e * (1.0 / d)
        dxh = dy * gv
        dxv = r * (dxh - xh * jnp.mean(dxh * xh, axis=-1, keepdims=True))
        dx_ref[...] = dxv
        dxm_ref[...] = dxv.astype(dxm_ref.dtype)

        @pl.when(pl.program_id(0) == 0)
        def _():
            dg_ref[...] = jnp.zeros_like(dg_ref)
            loss_ref[...] = jnp.zeros_like(loss_ref)

        dg_ref[...] += jnp.sum(dy * xh, axis=0, keepdims=True)
        part = 0.5 * jnp.sum(jnp.mean(e * e, axis=-1, keepdims=True), axis=0, keepdims=True)
        loss_ref[...] += jnp.broadcast_to(part, loss_ref.shape)

    row = pl.BlockSpec((tm, d), lambda i: (i, 0))
    vec = pl.BlockSpec((1, d), lambda i: (0, 0))
    return pl.pallas_call(
        body, name=name, grid=(m // tm,),
        in_specs=[pl.BlockSpec((tm, k), lambda i: (i, 0)), pl.BlockSpec((k, d), lambda i: (0, 0)), row, row, vec],
        out_specs=[row, row, vec, pl.BlockSpec((1, LANES), lambda i: (0, 0))],
        out_shape=[jax.ShapeDtypeStruct((m, d), F32), jax.ShapeDtypeStruct((m, d), MXU_DTYPE),
                   jax.ShapeDtypeStruct((1, d), F32), jax.ShapeDtypeStruct((1, LANES), F32)],
        compiler_params=_params(1),
    )(a, w, res, target, g)


def _mm_nt(a, w, name, rows=None, out_dtype=F32, tm=512, tk=512):
    m, n = a.shape
    row0, k = rows or (0, w.shape[0])
    tm, tk = min(tm, m), min(tk, k)
    first = row0 // tk

    def body(a_ref, w_ref, o_ref):
        o_ref[...] = _dot_nt(a_ref[...], w_ref[...]).astype(o_ref.dtype)

    return pl.pallas_call(
        body, name=name, grid=(k // tk, m // tm),
        in_specs=[pl.BlockSpec((tm, n), lambda j, i: (i, 0)), pl.BlockSpec((tk, n), lambda j, i: (first + j, 0))],
        out_specs=pl.BlockSpec((tm, tk), lambda j, i: (i, j)),
        out_shape=jax.ShapeDtypeStruct((m, k), out_dtype),
        compiler_params=_params(2),
    )(a, w)


def _mm_tn(a, b, name, tk, tn, tt=1024, out_dtype=BF16):
    t, k = a.shape
    n = b.shape[1]
    tt = min(tt, t)
    steps = t // tt

    def body(a_ref, b_ref, o_ref, acc):
        s = pl.program_id(2)

        @pl.when(s == 0)
        def _():
            acc[...] = jnp.zeros_like(acc)

        acc[...] += _dot_tn(a_ref[...], b_ref[...])

        @pl.when(s == steps - 1)
        def _():
            o_ref[...] = acc[...].astype(o_ref.dtype)

    return pl.pallas_call(
        body, name=name, grid=(k // tk, n // tn, steps),
        in_specs=[pl.BlockSpec((tt, tk), lambda i, j, s: (s, i)), pl.BlockSpec((tt, tn), lambda i, j, s: (s, j))],
        out_specs=pl.BlockSpec((tk, tn), lambda i, j, s: (i, j)),
        out_shape=jax.ShapeDtypeStruct((k, n), out_dtype),
        scratch_shapes=[pltpu.VMEM((tk, tn), F32)],
        compiler_params=_params(3),
    )(a, b)


def _ffn_up(h, wg_t, wu_t, name, tm=1024, tn=FF_TILE):
    m, k = h.shape
    n = wg_t.shape[0]
    tm = min(tm, m)

    def body(h_ref, wg_ref, wu_ref, gate_ref, up_ref, ff_ref):
        hv = h_ref[...]
        gate = _dot_nt(hv, wg_ref[...])
        up = _dot_nt(hv, wu_ref[...])
        gate_ref[...] = gate.astype(gate_ref.dtype)
        up_ref[...] = up.astype(up_ref.dtype)
        ff_ref[...] = (gate * _sigmoid(gate) * up).astype(ff_ref.dtype)

    wspec = pl.BlockSpec((tn, k), lambda j, i: (j, 0))
    ospec = pl.BlockSpec((tm, tn), lambda j, i: (i, j))
    return pl.pallas_call(
        body, name=name, grid=(n // tn, m // tm),
        in_specs=[pl.BlockSpec((tm, k), lambda j, i: (i, 0)), wspec, wspec],
        out_specs=[ospec, ospec, ospec],
        out_shape=[jax.ShapeDtypeStruct((m, n), MXU_DTYPE)] * 3,
        compiler_params=_params(2),
    )(h, wg_t, wu_t)


def _ffn_bwd_act(dx, wd, gate, up, name, tm=1024, tn=FF_TILE):
    m, k = dx.shape
    n = wd.shape[0]
    tm = min(tm, m)

    def body(dx_ref, wd_ref, gate_ref, up_ref, dgate_ref, dup_ref):
        dff = _dot_nt(dx_ref[...], wd_ref[...])
        gate = gate_ref[...].astype(F32)
        sg = _sigmoid(gate)
        dgate_ref[...] = (dff * up_ref[...].astype(F32) * sg * (1.0 + gate * (1.0 - sg))).astype(dgate_ref.dtype)
        dup_ref[...] = (dff * gate * sg).astype(dup_ref.dtype)

    ospec = pl.BlockSpec((tm, tn), lambda j, i: (i, j))
    return pl.pallas_call(
        body, name=name, grid=(n // tn, m // tm),
        in_specs=[pl.BlockSpec((tm, k), lambda j, i: (i, 0)), pl.BlockSpec((tn, k), lambda j, i: (j, 0)),
                  ospec, ospec],
        out_specs=[ospec, ospec],
        out_shape=[jax.ShapeDtypeStruct((m, n), MXU_DTYPE), jax.ShapeDtypeStruct((m, n), MXU_DTYPE)],
        compiler_params=_params(2),
    )(dx, wd, gate, up)


def _chunk_masks():
    r = lax.broadcasted_iota(jnp.int32, (SUPER, SUPER), 0)
    c = lax.broadcasted_iota(jnp.int32, (SUPER, SUPER), 1)
    same = jnp.right_shift(r, 6) == jnp.right_shift(c, 6)
    lower = jnp.logical_and(same, c <= r)
    upper = jnp.logical_and(same, c >= r)
    return same, lower, upper


def _head_block_mask():
    r = lax.broadcasted_iota(jnp.int32, (LANES, LANES), 0)
    c = lax.broadcasted_iota(jnp.int32, (LANES, LANES), 1)
    return jnp.where(jnp.right_shift(r, 6) == jnp.right_shift(c, 6), 1.0, 0.0)


def _lower_bound(lb_raw):
    return 1.0 / (1.0 + jnp.exp(lb_raw[1:2, :] - lb_raw[0:1, :]))


HGRN_UNROLL = 4
PER_SUPER = SUPER // CHUNK
CHUNK_ROWS = [slice(c * CHUNK, (c + 1) * CHUNK) for c in range(PER_SUPER)]


def _over_chunks(rows):
    return jnp.concatenate([jnp.broadcast_to(r, (CHUNK, LANES)) for r in rows], axis=0)


def _hgrn_gates(q, hf, lb, lower_b):
    sig = _sigmoid(hf)
    f = lb + (1.0 - lb) * sig
    k = 1.0 - f
    lf = jnp.log(f)
    b = _rsum_left(lower_b, lf, 2)
    ends = [b[cr.stop - 1:cr.stop, :] for cr in CHUNK_ROWS]
    eb = jnp.exp(b)
    enb = jnp.exp(-b)
    edb = jnp.exp(_over_chunks(ends) - b)
    decs = [jnp.exp(e) for e in ends]
    return sig, f, k, decs, eb, enb, edb, q * eb, k * enb, k * edb


def _hgrn_fwd(proj, lower_bounds, norm_g, n_seq, seq):
    t = n_seq * seq
    n_super = seq // SUPER
    n_chunks = seq // CHUNK

    def body(q_ref, f_ref, i_ref, g_ref, lb_ref, ng_ref, out_ref, opre_ref, st_ref):
        masks = _head_masks()
        _, lower, _ = _chunk_masks()
        lower_b = _ones_where(lower)
        bd = _head_block_mask()
        lb = _lower_bound(lb_ref[...])
        ng = ng_ref[...]

        def step(it, st):
            blocks = [HGRN_UNROLL * it + u for u in range(HGRN_UNROLL)]
            rows = [pl.ds(pl.multiple_of(sb * SUPER, SUPER), SUPER) for sb in blocks]
            vs = [i_ref[rw, :] for rw in rows]
            gates = [_hgrn_gates(q_ref[rw, :], f_ref[rw, :], lb, lower_b) for rw in rows]
            decs, qes, kes, kds = ([g[k] for g in gates] for k in (3, 7, 8, 9))
            scores = [[_dot_nt(qe * m, ke) for m in masks] for qe, ke in zip(qes, kes)]
            updates = [[_dot_tn(v[cr], kd[cr]) for cr in CHUNK_ROWS] for v, kd in zip(vs, kds)]
            states = [st]
            for dec_b, upd_b in zip(decs, updates):
                for dec, upd in zip(dec_b, upd_b):
                    states.append(states[-1] * dec + bd * upd)
            for u, sb in enumerate(blocks):
                for c in range(PER_SUPER):
                    st_ref[0, 0, sb * PER_SUPER + c] = states[u * PER_SUPER + c]
            intra = [[_dot(jnp.where(lower, p, 0.0), v) for p in sc] for sc, v in zip(scores, vs)]
            inter = [[_dot_nt(qe[cr], states[u * PER_SUPER + c]) for c, cr in enumerate(CHUNK_ROWS)]
                     for u, qe in enumerate(qes)]
            for rw, intra_b, inter_b in zip(rows, intra, inter):
                o = intra_b[0] * masks[0] + intra_b[1] * masks[1] + jnp.concatenate(inter_b, axis=0)
                opre_ref[rw, :] = o
                hg = g_ref[rw, :]
                on = o * _head_rstd(o, masks) * ng
                out_ref[rw, :] = (on * hg * _sigmoid(hg)).astype(out_ref.dtype)
            return states[-1]

        lax.fori_loop(0, n_super // HGRN_UNROLL, step, jnp.zeros((LANES, LANES), F32))

    def col(k):
        return pl.BlockSpec((seq, LANES), lambda p, b: (b, k * N_PAIRS + p))

    vec = lambda rows: pl.BlockSpec((rows, LANES), lambda p, b: (0, p))
    ospec = pl.BlockSpec((seq, LANES), lambda p, b: (b, p))
    return pl.pallas_call(
        body, name="hgrn_fwd", grid=(N_PAIRS, n_seq),
        in_specs=[col(0), col(1), col(2), col(3), vec(2), vec(1)],
        out_specs=[ospec, ospec,
                   pl.BlockSpec((1, 1, n_chunks, LANES, LANES), lambda p, b: (b, p, 0, 0, 0))],
        out_shape=[jax.ShapeDtypeStruct((t, 2 * GROUP), MXU_DTYPE), jax.ShapeDtypeStruct((t, GROUP), F32),
                   jax.ShapeDtypeStruct((n_seq, N_PAIRS, n_chunks, LANES, LANES), F32)],
        compiler_params=_params(2),
    )(proj, proj, proj, proj, lower_bounds, norm_g)


def _hgrn_bwd(proj, lower_bounds, norm_g, dmix, opre, states, n_seq, seq):
    t = n_seq * seq
    n_super = seq // SUPER
    n_chunks = seq // CHUNK
    per = SUPER // CHUNK

    def body(q_ref, f_ref, i_ref, g_ref, lb_ref, ng_ref, dm_ref, opre_ref, st_ref,
             dq_ref, df_ref, di_ref, dg_ref, dlb_ref, dng_ref):
        masks = _head_masks()
        _, lower, upper = _chunk_masks()
        lower_b, upper_b = _ones_where(lower), _ones_where(upper)
        bd = _head_block_mask()
        lb_raw = lb_ref[...]
        lb = _lower_bound(lb_raw)
        ng = ng_ref[...]

        @pl.when(pl.program_id(1) == 0)
        def _():
            dlb_ref[...] = jnp.zeros_like(dlb_ref)
            dng_ref[...] = jnp.zeros_like(dng_ref)

        def first_half(sb):
            rows = pl.ds(pl.multiple_of(sb * SUPER, SUPER), SUPER)
            q, hf, v, hg = q_ref[rows, :], f_ref[rows, :], i_ref[rows, :], g_ref[rows, :]
            sig, f, k, decs, eb, enb, edb, qe, ke, kd = _hgrn_gates(q, hf, lb, lower_b)
            o = opre_ref[rows, :]
            r = _head_rstd(o, masks)
            oh = o * r
            dm = dm_ref[rows, :]
            sg = _sigmoid(hg)
            dg_ref[rows, :] = (dm * oh * ng * sg * (1.0 + hg * (1.0 - sg))).astype(dg_ref.dtype)
            don = dm * hg * sg
            dng_ref[...] += jnp.sum(don * oh, axis=0, keepdims=True)
            doh = don * ng
            do = r * (doh - oh * _head_mean(doh * oh, masks))
            doms = [do * m for m in masks]
            qems = [qe * m for m in masks]
            scores = [_dot_nt(qem, ke) for qem in qems]
            dscores = [_dot_nt(dom, v) for dom in doms]
            prevs = [st_ref[0, 0, sb * per + c] for c in range(per)]
            dst_in = [_dot_tn(do[cr], qe[cr]) for cr in CHUNK_ROWS]
            dqe_i = [_dot(do[cr], prev) for cr, prev in zip(CHUNK_ROWS, prevs)]
            return dict(rows=rows, v=v, sig=sig, f=f, decs=decs, eb=eb, enb=enb, edb=edb, qe=qe, ke=ke, kd=kd,
                        doms=doms, qems=qems, scores=scores, dscores=dscores, prevs=prevs, dst_in=dst_in, dqe_i=dqe_i)

        def second_half(blk, dsts):
            v, qe, ke, kd = blk["v"], blk["qe"], blk["ke"], blk["kd"]
            ps = [jnp.where(lower, p, 0.0) for p in blk["scores"]]
            dps = [jnp.where(lower, dp, 0.0) for dp in blk["dscores"]]
            dqe_h = [_dot(dp, ke) for dp in dps]
            dke_h = [_dot_tn(dp, qem) for dp, qem in zip(dps, blk["qems"])]
            dv_h = [_dot_tn(p, dom) for p, dom in zip(ps, blk["doms"])]
            dus = [bd * d for d in dsts]
            dv_i = [_dot_nt(kd[cr], du) for cr, du in zip(CHUNK_ROWS, dus)]
            dkd_i = [_dot(v[cr], du) for cr, du in zip(CHUNK_ROWS, dus)]

            def finish():
                dqe = dqe_h[0] * masks[0] + dqe_h[1] * masks[1] + jnp.concatenate(blk["dqe_i"], axis=0)
                dke = dke_h[0] + dke_h[1]
                dv = dv_h[0] + dv_h[1] + jnp.concatenate(dv_i, axis=0)
                dkd = jnp.concatenate(dkd_i, axis=0)
                dk = dke * blk["enb"] + dkd * blk["edb"]
                db = dqe * qe - dke * ke - dkd * kd
                dkd_kd = dkd * kd
                dends = [jnp.sum(dkd_kd[cr], axis=0, keepdims=True)
                         + jnp.sum(dsts[c] * blk["prevs"][c], axis=0, keepdims=True) * blk["decs"][c]
                         for c, cr in enumerate(CHUNK_ROWS)]
                dlf = _rsum_left(upper_b, db, 2) + _over_chunks(dends)
                sig = blk["sig"]
                dfv = dlf / blk["f"] - dk
                rows = blk["rows"]
                dq_ref[rows, :] = (dqe * blk["eb"]).astype(dq_ref.dtype)
                di_ref[rows, :] = dv.astype(di_ref.dtype)
                df_ref[rows, :] = (dfv * (1.0 - lb) * sig * (1.0 - sig)).astype(df_ref.dtype)
                dlb = jnp.sum(dfv * (1.0 - sig), axis=0, keepdims=True)
                da0 = dlb * lb * (1.0 - lb)
                dlb_ref[0:1, :] += da0
                dlb_ref[1:2, :] -= da0

            return finish

        def step(it, dst):
            blocks = [first_half(n_super - 1 - HGRN_UNROLL * it - u) for u in range(HGRN_UNROLL)]
            all_dsts = []
            for blk in blocks:
                dsts = [None] * per
                for c in reversed(range(per)):
                    dsts[c] = dst
                    dst = bd * (dst * blk["decs"][c] + blk["dst_in"][c])
                all_dsts.append(dsts)
            for finish in [second_half(blk, dsts) for blk, dsts in zip(blocks, all_dsts)]:
                finish()
            return dst

        lax.fori_loop(0, n_super // HGRN_UNROLL, step, jnp.zeros((LANES, LANES), F32))

    def col(k):
        return pl.BlockSpec((seq, LANES), lambda p, b: (b, k * N_PAIRS + p))

    vec = lambda rows: pl.BlockSpec((rows, LANES), lambda p, b: (0, p))
    ospec = pl.BlockSpec((seq, LANES), lambda p, b: (b, p))
    piece = jax.ShapeDtypeStruct((t, GROUP), MXU_DTYPE)
    return pl.pallas_call(
        body, name="hgrn_bwd", grid=(N_PAIRS, n_seq),
        in_specs=[col(0), col(1), col(2), col(3), vec(2), vec(1), ospec, ospec,
                  pl.BlockSpec((1, 1, n_chunks, LANES, LANES), lambda p, b: (b, p, 0, 0, 0))],
        out_specs=[ospec, ospec, ospec, ospec, vec(2), vec(1)],
        out_shape=[piece, piece, piece, piece,
                   jax.ShapeDtypeStruct((2, GROUP), F32), jax.ShapeDtypeStruct((1, GROUP), F32)],
        compiler_params=_params(2),
    )(proj, proj, proj, proj, lower_bounds, norm_g, dmix, opre, states)


SB_SCALE = 1.0 / math.sqrt(D_HEAD)


SB_STEP = 2 * SB_BLOCK
QUERY_BLOCKS = (slice(0, SB_BLOCK), slice(SB_BLOCK, SB_STEP))


def _triangle(keep):
    row = lax.broadcasted_iota(jnp.int32, (SB_BLOCK, SB_BLOCK), 0)
    col = lax.broadcasted_iota(jnp.int32, (SB_BLOCK, SB_BLOCK), 1)
    return _ones_where(keep(row, col))


SB_HALF = SB_BLOCK // 2


def _keep(on_diagonal, x):
    if not on_diagonal:
        return x
    row = lax.broadcasted_iota(jnp.int32, (SB_HALF, SB_HALF), 0)
    col = lax.broadcasted_iota(jnp.int32, (SB_HALF, SB_HALF), 1)
    return jnp.where(col < row, x, 0.0)


def _lower_quadrants(fn, n_out, *tiles):
    def quadrant(r, c):
        return [t[r * SB_HALF:(r + 1) * SB_HALF, c * SB_HALF:(c + 1) * SB_HALF] for t in tiles]

    top, left, bottom = fn(True, *quadrant(0, 0)), fn(False, *quadrant(1, 0)), fn(True, *quadrant(1, 1))
    zero = jnp.zeros((SB_HALF, SB_HALF), F32)
    return [jnp.concatenate([jnp.concatenate([top[o], zero], axis=1), jnp.concatenate([left[o], bottom[o]], axis=1)],
                            axis=0) for o in range(n_out)]


def _sb_fwd(proj, norm_g, mixed, shards, n_seq, seq):
    t = n_seq * seq
    nq = seq // SB_STEP
    q0, k0, v0 = 0, N_PAIRS, 2 * N_PAIRS
    n_w = len(shards)
    n_steps = N_PAIRS * n_seq * nq
    tri = _triangle(lambda row, col: row >= col)

    def body(q_ref, k_ref, v_ref, ng_ref, tri_ref, mixed_in, *rest):
        del mixed_in
        shard_refs = rest[:n_w]
        out_ref, opre_ref, ctot_ref = rest[n_w:n_w + 3]
        gathered = rest[n_w + 3:2 * n_w + 3]
        send_sems, recv_sems, local_sems = rest[2 * n_w + 3:]
        i = pl.program_id(2)
        step = (pl.program_id(0) * n_seq + pl.program_id(1)) * nq + i
        plan = _GatherPlan(shard_refs, gathered, send_sems, recv_sems, local_sems)

        @pl.when(step == 0)
        def _():
            plan.start()

        @pl.when(step == (3 * n_steps) // 4)
        def _():
            plan.forward()

        masks = _head_masks()
        suffix = tri_ref[...]
        qhs = [[(q_ref[blk, :] * SB_SCALE * m).astype(MXU_DTYPE) for m in masks] for blk in QUERY_BLOCKS]

        def tiles(js, work, carry):
            rows = [pl.ds(pl.multiple_of(j * SB_BLOCK, SB_BLOCK), SB_BLOCK) for j in js]
            ks = [k_ref[rw, :].astype(MXU_DTYPE) for rw in rows]
            vs = [v_ref[rw, :].astype(MXU_DTYPE) for rw in rows]
            zs = [[_dot_nt(qh, ks[ts]) for qh in qhs[qb]] for qb, ts, _ in work]
            ccs = [[_lower_quadrants(lambda on, zq: (_keep(on, _softplus(zq)),), 1, z)[0] if diag else _softplus(z)
                    for z in zw] for zw, (_, _, diag) in zip(zs, work)]
            sums = [[jnp.dot(cc.astype(BF16), suffix, preferred_element_type=F32) for cc in cw] for cw in ccs]
            out = [list(per_block) for per_block in carry]
            for h in range(len(masks)):
                for w, (qb, ts, diag) in enumerate(work):
                    run, acc = out[qb][h]
                    logit = zs[w][h] - (sums[w][h] + run)
                    if diag:
                        a = _lower_quadrants(lambda on, lq: (_keep(on, jnp.exp(lq)),), 1, logit)[0]
                    else:
                        a = jnp.exp(logit)
                    out[qb][h] = (run + sums[w][h][:, 0:1], acc + _dot(a, vs[ts]))
            return tuple(tuple(per_block) for per_block in out)

        start = ((jnp.zeros((SB_BLOCK, 1), F32), jnp.zeros((SB_BLOCK, LANES), F32)),) * 2
        carry = tiles([2 * i, 2 * i + 1], [(0, 0, True), (1, 1, True), (1, 0, False)], (start, start))
        both = [(0, 0, False), (0, 1, False), (1, 0, False), (1, 1, False)]
        carry = lax.fori_loop(0, i, lambda s, cy: tiles([2 * (i - s) - 1, 2 * (i - s) - 2], both, cy), carry)
        opre = jnp.concatenate([cb[0][1] * masks[0] + cb[1][1] * masks[1] for cb in carry], axis=0)
        ctot = jnp.concatenate([cb[0][0] * masks[0] + cb[1][0] * masks[1] for cb in carry], axis=0)
        opre_ref[...] = opre
        ctot_ref[...] = ctot
        out_ref[...] = (opre * _head_rstd(opre, masks) * ng_ref[...]).astype(out_ref.dtype)

        @pl.when(step == n_steps - 1)
        def _():
            plan.finish()

    qspec = pl.BlockSpec((SB_STEP, LANES), lambda p, b, i: (b * nq + i, q0 + p))
    ospec = pl.BlockSpec((SB_STEP, LANES), lambda p, b, i: (b * nq + i, p))
    hbm = pl.BlockSpec(memory_space=pltpu.HBM)
    outs = pl.pallas_call(
        body, name="sb_fwd", grid=(N_PAIRS, n_seq, nq),
        in_specs=[qspec,
                  pl.BlockSpec((seq, LANES), lambda p, b, i: (b, k0 + p)),
                  pl.BlockSpec((seq, LANES), lambda p, b, i: (b, v0 + p)),
                  pl.BlockSpec((1, LANES), lambda p, b, i: (0, p)),
                  pl.BlockSpec(tri.shape, lambda p, b, i: (0, 0)), hbm] + [hbm] * n_w,
        out_specs=[pl.BlockSpec((SB_STEP, LANES), lambda p, b, i: (b * nq + i, N_PAIRS + p)), ospec, ospec]
        + [hbm] * n_w,
        out_shape=[jax.ShapeDtypeStruct(mixed.shape, mixed.dtype), jax.ShapeDtypeStruct((t, GROUP), F32),
                   jax.ShapeDtypeStruct((t, GROUP), F32)]
        + [jax.ShapeDtypeStruct((N_DEV,) + s.shape, s.dtype) for s in shards],
        scratch_shapes=[pltpu.SemaphoreType.DMA((n_w * _GatherPlan.COPIES,)),
                        pltpu.SemaphoreType.DMA((n_w * _GatherPlan.COPIES,)), pltpu.SemaphoreType.DMA((n_w,))],
        input_output_aliases={5: 0},
        compiler_params=_params(3),
    )(proj, proj, proj, norm_g, tri, mixed, *shards)
    return outs[0], outs[1], outs[2], list(outs[3:])


def _sb_bwd(proj, norm_g, dmix, opre, ctot, grads, n_seq, seq):
    t = n_seq * seq
    nq = seq // SB_STEP
    q0, k0, v0 = 0, N_PAIRS, 2 * N_PAIRS
    n_w = len(grads)
    n_steps = N_PAIRS * n_seq * nq
    tri = _triangle(lambda row, col: row <= col)

    def body(q_ref, k_ref, v_ref, ng_ref, tri_ref, dm_ref, opre_ref, ctot_ref, *rest):
        grad_refs = rest[:n_w]
        dq_ref, dk_ref, dv_ref, dng_ref = rest[n_w:n_w + 4]
        lands = rest[n_w + 4:2 * n_w + 4]
        dk_acc, dv_acc, send_sems, recv_sems, local_sems = rest[2 * n_w + 4:]
        p_id, b_id, i = pl.program_id(0), pl.program_id(1), pl.program_id(2)
        step = (p_id * n_seq + b_id) * nq + i
        plan = _ScatterPlan(grad_refs, lands, send_sems, recv_sems, local_sems)

        @pl.when(step == 0)
        def _():
            plan.start()

        masks = _head_masks()
        upto = tri_ref[...]

        def prefix(x):
            return jnp.dot(x.astype(BF16), upto, preferred_element_type=F32)

        @pl.when(i == 0)
        def _():
            dk_acc[...] = jnp.zeros_like(dk_acc)
            dv_acc[...] = jnp.zeros_like(dv_acc)

        @pl.when(jnp.logical_and(b_id == 0, i == 0))
        def _():
            dng_ref[...] = jnp.zeros_like(dng_ref)

        o = opre_ref[...]
        rs = _head_rstd(o, masks)
        oh = o * rs
        dm = dm_ref[...]
        dng_ref[...] += jnp.sum(dm * oh, axis=0, keepdims=True)
        doh = dm * ng_ref[...]
        do = rs * (doh - oh * _head_mean(doh * oh, masks))

        heads = range(len(masks))
        qs = [q_ref[blk, :] * SB_SCALE for blk in QUERY_BLOCKS]
        dos = [do[blk] for blk in QUERY_BLOCKS]
        qhs = [[(q * m).astype(MXU_DTYPE) for m in masks] for q in qs]
        doms = [[(d * m).astype(MXU_DTYPE) for m in masks] for d in dos]
        head_rows = [jnp.where(jnp.right_shift(lax.broadcasted_iota(jnp.int32, (LANES, 1), 0), 6) == h, 1.0, 0.0)
                     for h in heads]
        qhts = [[(qt * hr).astype(MXU_DTYPE) for hr in head_rows] for qt in [q.astype(F32).T for q in qs]]
        domts = [[(dt * hr).astype(MXU_DTYPE) for hr in head_rows] for dt in [d.T for d in dos]]
        totals = [[ctot_ref[blk, h * D_HEAD:h * D_HEAD + 1] for h in heads] for blk in QUERY_BLOCKS]

        def tiles(js, work, carry):
            rows = [pl.ds(pl.multiple_of(j * SB_BLOCK, SB_BLOCK), SB_BLOCK) for j in js]
            ks = [k_ref[rw, :].astype(MXU_DTYPE) for rw in rows]
            vs = [v_ref[rw, :].astype(MXU_DTYPE) for rw in rows]
            zs = [[_dot_nt(qhs[qb][h], ks[ts]) for h in heads] for qb, ts, _ in work]
            das = [[_dot_nt(doms[qb][h], vs[ts]) for h in heads] for qb, ts, _ in work]
            def gates(on, zq):
                sp = _softplus(zq)
                return _keep(on, sp), zq - sp, _keep(on, jnp.exp(zq - sp))

            trio = [[_lower_quadrants(gates, 3, z) if diag else gates(False, z) for z in zw]
                    for zw, (_, _, diag) in zip(zs, work)]
            ccs, lsigs, sigs = ([[t[o] for t in tw] for tw in trio] for o in range(3))
            pres = [[prefix(cc) for cc in cw] for cw in ccs]
            out = [list(per_block) for per_block in carry]
            for h in heads:
                for w, (qb, ts, diag) in enumerate(work):
                    pc, pdl, dq_h = out[qb][h]
                    logit = lsigs[w][h] + pres[w][h] - (totals[qb][h] - pc)
                    if diag:
                        a = _lower_quadrants(lambda on, lq: (_keep(on, jnp.exp(lq)),), 1, logit)[0]
                    else:
                        a = jnp.exp(logit)
                    dl = a * das[w][h]
                    dv_acc[js[ts]] += _dot(domts[qb][h], a)
                    dpre = prefix(dl)
                    dz = dl - sigs[w][h] * (pdl + dpre)
                    dzb = dz.astype(MXU_DTYPE)
                    dk_acc[js[ts]] += _dot(qhts[qb][h], dzb)
                    out[qb][h] = (pc + pres[w][h][:, SB_BLOCK - 1:SB_BLOCK], pdl + dpre[:, SB_BLOCK - 1:SB_BLOCK],
                                  dq_h + _dot(dzb, ks[ts]))
            return tuple(tuple(per_block) for per_block in out)

        zero = jnp.zeros((SB_BLOCK, 1), F32)
        start = ((zero, zero, jnp.zeros((SB_BLOCK, LANES), F32)),) * 2
        both = [(0, 0, False), (0, 1, False), (1, 0, False), (1, 1, False)]
        carry = lax.fori_loop(0, i, lambda s, cy: tiles([2 * s, 2 * s + 1], both, cy), (start, start))
        carry = tiles([2 * i, 2 * i + 1], [(0, 0, True), (1, 0, False), (1, 1, True)], carry)
        dq = jnp.concatenate([cb[0][2] * masks[0] + cb[1][2] * masks[1] for cb in carry], axis=0)
        dq_ref[...] = (dq * SB_SCALE).astype(dq_ref.dtype)

        @pl.when(i == nq - 1)
        def _():
            for j in range(seq // SB_BLOCK):
                tile_rows = slice(j * SB_BLOCK, (j + 1) * SB_BLOCK)
                dk_ref[tile_rows, :] = dk_acc[j].T.astype(dk_ref.dtype)
                dv_ref[tile_rows, :] = dv_acc[j].T.astype(dv_ref.dtype)

        @pl.when(step == n_steps - 1)
        def _():
            plan.finish()

    qspec = pl.BlockSpec((SB_STEP, LANES), lambda p, b, i: (b * nq + i, q0 + p))
    ospec = pl.BlockSpec((SB_STEP, LANES), lambda p, b, i: (b * nq + i, p))
    dmspec = pl.BlockSpec((SB_STEP, LANES), lambda p, b, i: (b * nq + i, N_PAIRS + p))
    full = lambda k: pl.BlockSpec((seq, LANES), lambda p, b, i: (b, k + p))
    vec = pl.BlockSpec((1, LANES), lambda p, b, i: (0, p))
    hbm = pl.BlockSpec(memory_space=pltpu.HBM)
    piece = jax.ShapeDtypeStruct((t, GROUP), MXU_DTYPE)
    outs = pl.pallas_call(
        body, name="sb_bwd", grid=(N_PAIRS, n_seq, nq),
        in_specs=[qspec, full(k0), full(v0), vec, pl.BlockSpec(tri.shape, lambda p, b, i: (0, 0)), dmspec, ospec, ospec]
        + [hbm] * n_w,
        out_specs=[ospec, full(0), full(0), vec] + [hbm] * n_w,
        out_shape=[piece, piece, piece, jax.ShapeDtypeStruct((1, GROUP), F32)]
        + [jax.ShapeDtypeStruct(g.shape, g.dtype) for g in grads],
        scratch_shapes=[pltpu.VMEM((seq // SB_BLOCK, LANES, SB_BLOCK), F32),
                        pltpu.VMEM((seq // SB_BLOCK, LANES, SB_BLOCK), F32),
                        pltpu.SemaphoreType.DMA((n_w * (N_DEV - 1),)), pltpu.SemaphoreType.DMA((n_w * (N_DEV - 1),)),
                        pltpu.SemaphoreType.DMA((n_w,))],
        compiler_params=_params(3),
    )(proj, proj, proj, norm_g, tri, dmix, opre, ctot, *[pltpu.with_memory_space_constraint(g, pltpu.HBM) for g in grads])
    return outs[0], outs[1], outs[2], outs[3], list(outs[4:])


def _mesh_place():
    x, y, c = lax.axis_index("x"), lax.axis_index("y"), lax.axis_index("c")
    return x, y, c


def _peer(x, y, c, k):
    px = lax.rem(x + ((k >> 2) & 1), 2)
    py = lax.rem(y + ((k >> 1) & 1), 2)
    pc = lax.rem(c + (k & 1), 2)
    return (px, py, pc), 4 * px + 2 * py + pc


def _remote(src, dst, send_sem, recv_sem, to):
    return pltpu.make_async_remote_copy(src_ref=src, dst_ref=dst, send_sem=send_sem, recv_sem=recv_sem,
                                        device_id=to, device_id_type=pl.DeviceIdType.MESH)


class _GatherPlan:
    COPIES = 7

    def __init__(self, shards, gathered, send_sems, recv_sems, local_sems):
        x, y, c = _mesh_place()
        self.c = c
        self.me = (x, y, c)
        self.sibling = (x, y, 1 - c)
        self.chips = [(1 - x, y), (x, 1 - y), (1 - x, 1 - y)]
        self.tensors = list(zip(shards, gathered))
        self.send_sems, self.recv_sems, self.local_sems = send_sems, recv_sems, local_sems

    @staticmethod
    def _index(place):
        return 4 * place[0] + 2 * place[1] + place[2]

    def _copy(self, w, k, block, to, own=False):
        shard, gathered = self.tensors[w]
        slot = gathered.at[self._index(block)]
        n = w * self.COPIES + k
        return _remote(shard if own else slot, slot, self.send_sems.at[n], self.recv_sems.at[n], to)

    def _local(self, w):
        shard, gathered = self.tensors[w]
        return pltpu.make_async_copy(shard, gathered.at[self._index(self.me)], self.local_sems.at[w])

    def _first(self, w):
        return [self._copy(w, 0, self.me, self.sibling, own=True)] + [
            self._copy(w, 1 + j, self.me, (*chip, self.c), own=True) for j, chip in enumerate(self.chips)]

    def _passed(self, w):
        return [self._copy(w, 4 + j, (*chip, self.c), self.sibling) for j, chip in enumerate(self.chips)]

    def start(self):
        for w in range(len(self.tensors)):
            self._local(w).start()
            for cp in self._first(w):
                cp.start()

    def forward(self):
        for w in range(len(self.tensors)):
            passed = self._passed(w)
            for j, chip in enumerate(self.chips):
                self._copy(w, 1 + j, (*chip, self.c), self.me).wait_recv()
                passed[j].start()

    def finish(self):
        for w in range(len(self.tensors)):
            self._copy(w, 0, self.sibling, self.me).wait_recv()
            for j, chip in enumerate(self.chips):
                self._copy(w, 4 + j, (*chip, 1 - self.c), self.me).wait_recv()
            for cp in self._first(w) + self._passed(w):
                cp.wait_send()
            self._local(w).wait()


class _ScatterPlan:
    def __init__(self, grads, lands, send_sems, recv_sems, local_sems):
        self.place = _mesh_place()
        x, y, c = self.place
        self.me = 4 * x + 2 * y + c
        self.tensors = list(zip(grads, lands))
        self.send_sems, self.recv_sems, self.local_sems = send_sems, recv_sems, local_sems

    def _copies(self, w):
        grad, land = self.tensors[w]
        out = []
        for k in range(1, N_DEV):
            peer, pidx = _peer(*self.place, k)
            n = w * (N_DEV - 1) + k - 1
            sems = (self.send_sems.at[n], self.recv_sems.at[n], peer)
            out.append((_remote(grad.at[pidx], land.at[self.me], *sems), _remote(grad.at[pidx], land.at[pidx], *sems)))
        return out

    def _local(self, w):
        grad, land = self.tensors[w]
        return pltpu.make_async_copy(grad.at[self.me], land.at[self.me], self.local_sems.at[w])

    def start(self):
        for w in range(len(self.tensors)):
            self._local(w).start()
            for send, _ in self._copies(w):
                send.start()

    def finish(self):
        for w in range(len(self.tensors)):
            copies = self._copies(w)
            for _, arrival in copies:
                arrival.wait_recv()
            for send, _ in copies:
                send.wait_send()
            self._local(w).wait()


def _cast_shards(shards):
    def body(*refs):
        n = len(refs) // 2
        for src, dst in zip(refs[:n], refs[n:]):
            dst[...] = src[...].astype(dst.dtype)

    shapes = [jax.ShapeDtypeStruct(s.shape, BF16) for s in shards]
    return pl.pallas_call(
        body, name="cast_shards", grid=(1,),
        in_specs=[_whole(s) for s in shards], out_specs=[_whole(s) for s in shapes], out_shape=shapes,
        compiler_params=_params(1),
    )(*shards)


def _gather_w_in(shard, x, g, chunk=512):
    rows, cols = shard.shape
    t, d = x.shape
    chunk = min(chunk, t)
    n_chunks = t // chunk

    def body(w_ref, x_ref, g_ref, out_ref, h_ref, x_buf, send_sems, recv_sems, local_sems, x_sems):
        plan = _GatherPlan([w_ref], [out_ref], send_sems, recv_sems, local_sems)
        plan.start()

        def fetch(j):
            return pltpu.make_async_copy(x_ref.at[pl.ds(j * chunk, chunk)], x_buf.at[j % 2], x_sems.at[j % 2])

        fetch(0).start()
        gv = g_ref[...]
        for j in range(n_chunks):
            if j + 1 < n_chunks:
                fetch(j + 1).start()
            fetch(j).wait()
            xv = x_buf[j % 2]
            r = lax.rsqrt(jnp.mean(xv * xv, axis=-1, keepdims=True) + EPS)
            h_ref[j * chunk:(j + 1) * chunk, :] = (xv * r * gv).astype(h_ref.dtype)
        plan.forward()
        plan.finish()

    vmem = pl.BlockSpec(memory_space=pltpu.VMEM)
    hbm = pl.BlockSpec(memory_space=pltpu.HBM)
    return pl.pallas_call(
        body, name="gather_w_in", in_specs=[hbm, hbm, vmem], out_specs=[hbm, vmem],
        out_shape=[jax.ShapeDtypeStruct((N_DEV, rows, cols), shard.dtype), jax.ShapeDtypeStruct((t, d), MXU_DTYPE)],
        scratch_shapes=[pltpu.VMEM((2, chunk, d), F32), pltpu.SemaphoreType.DMA((_GatherPlan.COPIES,)),
                        pltpu.SemaphoreType.DMA((_GatherPlan.COPIES,)), pltpu.SemaphoreType.DMA((1,)),
                        pltpu.SemaphoreType.DMA((2,))],
        compiler_params=pltpu.CompilerParams(vmem_limit_bytes=VMEM_LIMIT),
    )(shard, x, g)


def _dw_rows(pieces, b, name, tn=512, tt=1024):
    t, n = b.shape
    widths = [p.shape[1] for p in pieces]
    rows = sum(widths)
    tt = min(tt, t)
    steps = t // tt
    n_p = len(pieces)

    def body(*refs):
        piece_refs, b_ref, o_ref, acc = refs[:n_p], refs[n_p], refs[n_p + 1], refs[n_p + 2]
        s = pl.program_id(1)

        @pl.when(s == 0)
        def _():
            acc[...] = jnp.zeros_like(acc)

        bv = b_ref[...]
        off = 0
        for p_ref, width in zip(piece_refs, widths):
            acc[off:off + width, :] += _dot_tn(p_ref[...], bv)
            off += width

        @pl.when(s == steps - 1)
        def _():
            o_ref[...] = acc[...].astype(o_ref.dtype)

    return pl.pallas_call(
        body, name=name, grid=(n // tn, steps),
        in_specs=[pl.BlockSpec((tt, width), lambda j, s: (s, 0)) for width in widths]
        + [pl.BlockSpec((tt, tn), lambda j, s: (s, j))],
        out_specs=pl.BlockSpec((rows, tn), lambda j, s: (0, j)),
        out_shape=jax.ShapeDtypeStruct((rows, n), BF16),
        scratch_shapes=[pltpu.VMEM((rows, tn), F32)],
        compiler_params=_params(2),
    )(*pieces, b)


def _dh_norm_bwd(pairs, x, g, res, name, tm=512, after=(), next_w=None):
    m, d = x.shape
    tm = min(tm, m)
    n_p = len(pairs)
    n_steps = m // tm
    n_in = 2 * n_p + 3 + len(after) + (next_w is not None)

    def body(*refs):
        x_ref, g_ref, res_ref = refs[2 * n_p:2 * n_p + 3]
        dx_ref, dg_ref = refs[n_in:n_in + 2]

        @pl.when(pl.program_id(0) == 0)
        def _():
            dg_ref[...] = jnp.zeros_like(dg_ref)

        dh = None
        for q in range(n_p):
            part = _dot(refs[2 * q][...], refs[2 * q + 1][...])
            dh = part if dh is None else dh + part
        xv = x_ref[...]
        r = lax.rsqrt(jnp.mean(xv * xv, axis=-1, keepdims=True) + EPS)
        xh = xv * r
        dxh = dh * g_ref[...]
        dxv = res_ref[...] + r * (dxh - xh * jnp.mean(dxh * xh, axis=-1, keepdims=True))
        dx_ref[...] = dxv
        dg_ref[...] += jnp.sum(dh * xh, axis=0, keepdims=True)
        if next_w is not None:
            refs[n_in + 2][...] = _dot_nt(dxv, refs[n_in - 1][...])

    in_specs, args = [], []
    for a, w, r in pairs:
        k = a.shape[1]
        in_specs += [pl.BlockSpec((tm, k), lambda i: (i, 0)),
                     pl.BlockSpec((k, d), functools.partial(lambda i, r: (r, 0), r=r or 0),
                                  pipeline_mode=pl.Buffered(1))]
        args += [a, w]
    row = pl.BlockSpec((tm, d), lambda i: (i, 0))
    vec = pl.BlockSpec((1, d), lambda i: (0, 0))
    extra_in, extra_out, extra_shape = [], [], []
    if next_w is not None:
        e = next_w.shape[0]
        extra_in = [pl.BlockSpec((e, d), lambda i: (0, 0), pipeline_mode=pl.Buffered(1))]
        extra_out = [pl.BlockSpec((tm, e), lambda i: (i, 0))]
        extra_shape = [jax.ShapeDtypeStruct((m, e), F32)]
        args_tail = (next_w,)
    else:
        args_tail = ()
    return pl.pallas_call(
        body, name=name, grid=(n_steps,),
        in_specs=in_specs + [row, vec, row] + [pl.BlockSpec(memory_space=pl.ANY)] * len(after) + extra_in,
        out_specs=[row, vec] + extra_out,
        out_shape=[jax.ShapeDtypeStruct((m, d), F32), jax.ShapeDtypeStruct((1, d), F32)] + extra_shape,
        compiler_params=_params(1),
    )(*args, x, g, res, *after, *args_tail)


def _dw_exchange_start(pieces, b, name, tn=512, tt=1024):
    t, n = b.shape
    widths = [p.shape[1] for p in pieces]
    rows = sum(widths)
    slot = rows // N_DEV
    steps = t // tt
    n_col = n // tn
    n_p = len(pieces)

    def body(*refs):
        piece_refs, b_ref, land_ref = refs[:n_p], refs[n_p], refs[n_p + 1]
        send_sem, recv_sem, local_sem, g_ref, land_thru, token, acc, stage, stage_sem = refs[n_p + 2:]
        del land_thru
        j, s = pl.program_id(0), pl.program_id(1)
        token[...] = jnp.zeros_like(token)

        @pl.when(s == 0)
        def _():
            acc[...] = jnp.zeros_like(acc)

        bv = b_ref[...]
        off = 0
        for p_ref, width in zip(piece_refs, widths):
            acc[off:off + width, :] += _dot_tn(p_ref[...], bv)
            off += width

        x, y, c = _mesh_place()
        me = 4 * x + 2 * y + c
        for col in range(n_col):
            @pl.when((j == col) & (s == steps - 1))
            def _(col=col):
                cols = pl.ds(col * tn, tn)
                for d in range(N_DEV):
                    stage[d] = acc[d * slot:(d + 1) * slot, :].astype(stage.dtype)
                put = pltpu.make_async_copy(stage, g_ref.at[:, :, cols], stage_sem)
                put.start()
                put.wait()
                for k in range(1, N_DEV):
                    peer, pidx = _peer(x, y, c, k)
                    _remote(g_ref.at[pidx, :, cols], land_ref.at[me, :, cols], send_sem, recv_sem, peer).start()
                if col == n_col - 1:
                    pltpu.make_async_copy(g_ref.at[me], land_ref.at[me], local_sem).start()

    hbm = pl.BlockSpec(memory_space=pltpu.HBM)
    sem = pl.BlockSpec(memory_space=pltpu.SEMAPHORE)
    buf = pltpu.HBM((N_DEV, slot, n), BF16)
    return pl.pallas_call(
        body, name=name, grid=(n_col, steps),
        in_specs=[pl.BlockSpec((tt, width), lambda j, s: (s, 0)) for width in widths]
        + [pl.BlockSpec((tt, tn), lambda j, s: (s, j)), hbm],
        out_specs=(sem, sem, sem, hbm, hbm, pl.BlockSpec((8, LANES), lambda j, s: (0, 0))),
        out_shape=(pltpu.SemaphoreType.DMA(()), pltpu.SemaphoreType.DMA(()), pltpu.SemaphoreType.DMA(()), buf, buf,
                   jax.ShapeDtypeStruct((8, LANES), F32)),
        scratch_shapes=[pltpu.VMEM((rows, tn), F32), pltpu.VMEM((N_DEV, slot, tn), BF16),
                        pltpu.SemaphoreType.DMA(())],
        input_output_aliases={n_p + 1: 4},
        compiler_params=pltpu.CompilerParams(dimension_semantics=("arbitrary", "arbitrary"),
                                             vmem_limit_bytes=VMEM_LIMIT,
                                             has_side_effects=pltpu.SideEffectType.DATAFLOW_SIDE_EFFECTING),
    )(*pieces, b, pltpu.with_memory_space_constraint(lax.empty((N_DEV, slot, n), BF16), pltpu.HBM))


def _exchange_start(grad, name):
    def body(g_ref, land_ref, send_sem, recv_sem, local_sem, g_thru, land_thru, token):
        del g_thru, land_thru
        x, y, c = _mesh_place()
        me = 4 * x + 2 * y + c
        pltpu.make_async_copy(g_ref.at[me], land_ref.at[me], local_sem).start()
        for k in range(1, N_DEV):
            peer, pidx = _peer(x, y, c, k)
            _remote(g_ref.at[pidx], land_ref.at[me], send_sem, recv_sem, peer).start()
        token[...] = jnp.zeros_like(token)

    hbm = pl.BlockSpec(memory_space=pltpu.HBM)
    sem = pl.BlockSpec(memory_space=pltpu.SEMAPHORE)
    buf = pltpu.HBM(grad.shape, grad.dtype)
    return pl.pallas_call(
        body, name=name,
        out_shape=(pltpu.SemaphoreType.DMA(()), pltpu.SemaphoreType.DMA(()), pltpu.SemaphoreType.DMA(()), buf, buf,
                   jax.ShapeDtypeStruct((8, LANES), F32)),
        in_specs=(hbm, hbm), out_specs=(sem, sem, sem, hbm, hbm, pl.BlockSpec(memory_space=pltpu.VMEM)),
        input_output_aliases={0: 3, 1: 4},
        compiler_params=pltpu.CompilerParams(has_side_effects=pltpu.SideEffectType.DATAFLOW_SIDE_EFFECTING),
    )(pltpu.with_memory_space_constraint(grad, pltpu.HBM),
      pltpu.with_memory_space_constraint(lax.empty(grad.shape, grad.dtype), pltpu.HBM))


def _exchange_wait(send_sem, recv_sem, local_sem, grad, land, after, name):
    def body(g_ref, land_ref, send_sem, recv_sem, local_sem, *rest):
        x, y, c = _mesh_place()
        peer, _ = _peer(x, y, c, 1)
        others = pl.ds(0, N_DEV - 1)
        seven = _remote(g_ref.at[others], land_ref.at[others], send_sem, recv_sem, peer)
        seven.wait_send()
        seven.wait_recv()
        pltpu.make_async_copy(g_ref.at[0], land_ref.at[0], local_sem).wait()

    hbm = pl.BlockSpec(memory_space=pltpu.HBM)
    sem = pl.BlockSpec(memory_space=pltpu.SEMAPHORE)
    buf = pltpu.HBM(grad.shape, grad.dtype)
    return pl.pallas_call(
        body, name=name, out_shape=(buf, buf),
        in_specs=(hbm, hbm, sem, sem, sem) + (pl.BlockSpec(memory_space=pl.ANY),) * len(after), out_specs=(hbm, hbm),
        input_output_aliases={0: 0, 1: 1},
        compiler_params=pltpu.CompilerParams(has_side_effects=pltpu.SideEffectType.DATAFLOW_SIDE_EFFECTING),
    )(grad, land, send_sem, recv_sem, local_sem, *after)[1]


SMALL_LAYOUT = ((0, 0, 0, 0, D_MODEL), (1, 0, 1, 0, GROUP), (1, 1, 1, GROUP, GROUP), (2, 0, 2, 0, GROUP),
                (3, 0, 2, GROUP, GROUP), (4, 0, 3, 0, D_MODEL), (5, 0, 4, 0, D_MODEL))
LOSS_ROW = 5
N_SMALL = 6


def _place_small(dst, srcs):
    dst[...] = jnp.zeros_like(dst)
    for p, sr, dr, dc, width in SMALL_LAYOUT:
        dst[dr:dr + 1, dc:dc + width] = srcs[p][sr:sr + 1, :]


def _small_allreduce(grads, loss_part):
    def body(*refs):
        g_in, loss_in, sum_ref = refs[:N_SMALL], refs[N_SMALL], refs[N_SMALL + 1]
        pack, land, send_sems, recv_sems = refs[N_SMALL + 2:]
        _place_small(pack, g_in)
        pack[LOSS_ROW:LOSS_ROW + 1, 0:LANES] = loss_in[...]

        x, y, c = _mesh_place()
        me = 4 * x + 2 * y + c
        land[me] = pack[...]
        sends = []
        for k in range(1, N_DEV):
            peer, _ = _peer(x, y, c, k)
            cp = _remote(pack, land.at[me], send_sems.at[k - 1], recv_sems.at[k - 1], peer)
            cp.start()
            sends.append(cp)
        for k in range(1, N_DEV):
            peer, pidx = _peer(x, y, c, k)
            _remote(pack, land.at[pidx], send_sems.at[k - 1], recv_sems.at[k - 1], peer).wait_recv()
        for cp in sends:
            cp.wait_send()

        g = land[0]
        for d in range(1, N_DEV):
            g = g + land[d]
        sum_ref[...] = g

    args = [*grads, loss_part]
    packed = jax.ShapeDtypeStruct((SMALL_ROWS, D_MODEL), F32)
    return pl.pallas_call(
        body, name="small_allreduce", grid=(1,),
        in_specs=[_whole(a) for a in args], out_specs=_whole(packed), out_shape=packed,
        scratch_shapes=[pltpu.VMEM(packed.shape, F32), pltpu.VMEM((N_DEV,) + packed.shape, F32),
                        pltpu.SemaphoreType.DMA((N_DEV - 1,)), pltpu.SemaphoreType.DMA((N_DEV - 1,))],
        compiler_params=_params(1),
    )(*args)


def _small_adamw(gsum, ws, ms, vs):
    def body(*refs):
        g_ref = refs[0]
        params = [refs[1 + q * N_SMALL:1 + (q + 1) * N_SMALL] for q in range(3)]
        o0 = 1 + 3 * N_SMALL
        outs = [refs[o0 + q * N_SMALL:o0 + (q + 1) * N_SMALL] for q in range(4)]
        loss_out = refs[o0 + 4 * N_SMALL]
        wp, mp, vp = refs[o0 + 4 * N_SMALL + 1:]
        for dst, srcs in zip((wp, mp, vp), params):
            _place_small(dst, srcs)
        g = g_ref[...]
        delta, nm, nv = _adam(wp[...], g, mp[...], vp[...])
        for val, out in zip((g, delta, nm, nv), outs):
            for p, sr, dr, dc, width in SMALL_LAYOUT:
                out[p][sr:sr + 1, :] = val[dr:dr + 1, dc:dc + width]
        loss_out[...] = g[LOSS_ROW:LOSS_ROW + 1, 0:LANES]

    args = [gsum, *ws, *ms, *vs]
    shapes = [jax.ShapeDtypeStruct(w.shape, F32) for w in ws] * 4 + [jax.ShapeDtypeStruct((1, LANES), F32)]
    packed = pltpu.VMEM((SMALL_ROWS, D_MODEL), F32)
    outs = pl.pallas_call(
        body, name="small_adamw", grid=(1,),
        in_specs=[_whole(a) for a in args], out_specs=[_whole(s) for s in shapes], out_shape=shapes,
        scratch_shapes=[packed, packed, packed], compiler_params=_params(1),
    )(*args)
    return [outs[q * N_SMALL:(q + 1) * N_SMALL] for q in range(4)], outs[4 * N_SMALL]


def _adam(w, g, m, v):
    m = ADAM_B1 * m + (1.0 - ADAM_B1) * g
    v = ADAM_B2 * v + (1.0 - ADAM_B2) * (g * g)
    m_hat = m / (1.0 - ADAM_B1 ** ADAM_STEP)
    v_hat = v / (1.0 - ADAM_B2 ** ADAM_STEP)
    delta = -ADAM_LR * (m_hat / (jnp.sqrt(v_hat) + ADAM_EPS) + ADAM_WD * w)
    return delta, m, v


def _reduce_adamw(land, w, m, v, name, tr, after=()):
    _, rows, width = land.shape

    def body(land_ref, w_ref, m_ref, v_ref, *rest):
        g_ref, d_ref, nm_ref, nv_ref = rest[len(after):]
        g = land_ref[0].astype(F32)
        for d in range(1, N_DEV):
            g = g + land_ref[d].astype(F32)
        delta, nm, nv = _adam(w_ref[...], g, m_ref[...], v_ref[...])
        g_ref[...] = g
        d_ref[...] = delta
        nm_ref[...] = nm
        nv_ref[...] = nv

    row = pl.BlockSpec((tr, width), lambda i: (i, 0))
    out = jax.ShapeDtypeStruct((rows, width), F32)
    return pl.pallas_call(
        body, name=name, grid=(rows // tr,),
        in_specs=[pl.BlockSpec((N_DEV, tr, width), lambda i: (0, i, 0)), row, row, row]
        + [pl.BlockSpec(memory_space=pl.ANY)] * len(after),
        out_specs=[row, row, row, row], out_shape=[out, out, out, out],
        compiler_params=_params(1),
    )(land, w, m, v, *after)


def kernel(x, mix_norm_g, w_in, lower_bounds, hgrn_norm_g, sb_norm_g, w_out, ffn_norm_g, w_gate, w_up, w_down, final_norm_g, loss_target, m_mix_norm_g, m_w_in, m_lower_bounds, m_hgrn_norm_g, m_sb_norm_g, m_w_out, m_ffn_norm_g, m_w_gate, m_w_up, m_w_down, m_final_norm_g, v_mix_norm_g, v_w_in, v_lower_bounds, v_hgrn_norm_g, v_sb_norm_g, v_w_out, v_ffn_norm_g, v_w_gate, v_w_up, v_w_down, v_final_norm_g):
    n_seq, seq, d = x.shape
    t = n_seq * seq
    x2d = x.reshape(t, d)
    tgt = loss_target.reshape(t, d)
    final_g = final_norm_g.reshape(1, d)
    col_sharded = (True, False, True, True, False)

    def as_rows(ws):
        return [w[0].T if tr else w[0] for w, tr in zip(ws, col_sharded)]

    big_w = as_rows([w_in, w_out, w_gate, w_up, w_down])
    big_m = as_rows([m_w_in, m_w_out, m_w_gate, m_w_up, m_w_down])
    big_v = as_rows([v_w_in, v_w_out, v_w_gate, v_w_up, v_w_down])

    sh_in, sh_out, sh_gate, sh_up, sh_down = _cast_shards(big_w)
    wt_in, h1 = _gather_w_in(sh_in, x2d, mix_norm_g)
    wt_in = wt_in.reshape(IN_COLS, d)
    proj_h = _mm_nt(h1, wt_in, "proj_hgrn", rows=(0, 4 * GROUP), tm=1024, tk=4 * GROUP)
    proj_s = _mm_nt(h1, wt_in, "proj_sb", rows=(4 * GROUP, 3 * GROUP), out_dtype=MXU_DTYPE, tm=1024, tk=512)
    mixed, oa_pre, states = _hgrn_fwd(proj_h, lower_bounds, hgrn_norm_g, n_seq, seq)
    mixed, ob_pre, ctot, gathered = _sb_fwd(proj_s, sb_norm_g, mixed, [sh_out, sh_gate, sh_up, sh_down], n_seq, seq)
    wf_out = gathered[0].reshape(d, d)
    wt_gate = gathered[1].reshape(D_FF, d)
    wt_up = gathered[2].reshape(D_FF, d)
    wf_down = gathered[3].reshape(D_FF, d)
    x1, h2 = _mix_out_norm(mixed, wf_out, x2d, ffn_norm_g, "mix_out")
    gate, up, ff = _ffn_up(h2, wt_gate, wt_up, "ffn_up")
    dx2, dx2m, d_final_g, loss_part = _ffn_down_loss(ff, wf_down, x1, tgt, final_g, "ffn_down_loss")

    dgate, dup = _ffn_bwd_act(dx2m, wf_down, gate, up, "ffn_bwd_act")
    dw_down = _mm_tn(ff, dx2m, "dw_down", tk=1408, tn=1024).reshape(N_DEV, D_FF // N_DEV, d)
    dw_gate = _mm_tn(dgate, h2, "dw_gate", tk=1408, tn=1024).reshape(N_DEV, D_FF // N_DEV, d)
    dw_up = _mm_tn(dup, h2, "dw_up", tk=1408, tn=1024).reshape(N_DEV, D_FF // N_DEV, d)
    dx1, d_ffn_g, dmix = _dh_norm_bwd([(dgate, wt_gate, None), (dup, wt_up, None)], x1, ffn_norm_g, dx2, "dh_ffn",
                                      tm=256, next_w=wf_out)
    dw_out = _mm_tn(mixed, dx1, "dw_out", tk=1024, tn=1024).reshape(N_DEV, d // N_DEV, d)
    dsq, dsk, dsv, d_sb_g, lands = _sb_bwd(proj_s, sb_norm_g, dmix, ob_pre, ctot, [dw_out, dw_gate, dw_up, dw_down],
                                            n_seq, seq)
    dhq, dhf, dhi, dhg, d_lb, d_hgrn_g = _hgrn_bwd(proj_h, lower_bounds, hgrn_norm_g, dmix, oa_pre, states, n_seq,
                                                   seq)
    dproj = [dhq, dhf, dhi, dhg, dsq, dsk, dsv]
    send_sem, recv_sem, local_sem, dw_in, land_in, token = _dw_exchange_start(dproj, h1, "dw_in")
    dx, d_mix_g = _dh_norm_bwd([(piece, wt_in, k) for k, piece in enumerate(dproj)], x2d, mix_norm_g, dx1, "dh_mix",
                               after=(token,))

    tiles = {"in": 224, "out": 128, "gate": 176, "up": 176, "down": 176}
    keys = list(tiles)
    rest = [_reduce_adamw(land, w, m, v, "adamw_" + key, tr=tiles[key], after=(token,))
            for key, land, w, m, v in zip(keys[1:], lands, big_w[1:], big_m[1:], big_v[1:])]
    small_sum = _small_allreduce([d_mix_g, d_lb, d_hgrn_g, d_sb_g, d_ffn_g, d_final_g], loss_part)
    small, loss_row = _small_adamw(
        small_sum, [mix_norm_g, lower_bounds, hgrn_norm_g, sb_norm_g, ffn_norm_g, final_g],
        [m_mix_norm_g, m_lower_bounds, m_hgrn_norm_g, m_sb_norm_g, m_ffn_norm_g, m_final_norm_g.reshape(1, d)],
        [v_mix_norm_g, v_lower_bounds, v_hgrn_norm_g, v_sb_norm_g, v_ffn_norm_g, v_final_norm_g.reshape(1, d)])
    land_in = _exchange_wait(send_sem, recv_sem, local_sem, dw_in, land_in,
                             [dx, loss_row] + [res[0] for res in rest], "dw_in_await")
    big = [_reduce_adamw(land_in, big_w[0], big_m[0], big_v[0], "adamw_in", tr=tiles["in"])] + rest
    big = [[r.T if tr else r for r in res] for res, tr in zip(big, col_sharded)]

    outs = [loss_row[0, 0], dx.reshape(n_seq, seq, d)]
    for q in range(4):
        b_in, b_out, b_gate, b_up, b_down = [res[q][None] for res in big]
        s_mix, s_lb, s_hgrn, s_sb, s_ffn, s_final = small[q]
        outs += [s_mix, b_in, s_lb, s_hgrn, s_sb, b_out, s_ffn, b_gate, b_up, b_down, s_final.reshape(d)]
    return tuple(outs)
```

```python
import functools
import math

import jax
import jax.numpy as jnp
from jax import lax
from jax.experimental import pallas as pl
from jax.experimental.pallas import tpu as pltpu

F32 = jnp.float32
BF16 = jnp.bfloat16
MXU_DTYPE = BF16

EPS = 1e-6
D_MODEL = 1024
N_HEADS = 8
D_HEAD = 64
GROUP = N_HEADS * D_HEAD
IN_COLS = 7 * GROUP
D_FF = 2816
CHUNK = 64
LANES = 128
N_PAIRS = GROUP // LANES
SUPER = 256
SB_BLOCK = 256
N_DEV = 8

ADAM_LR = 0.001
ADAM_B1 = 0.9
ADAM_B2 = 0.999
ADAM_EPS = 1e-08
ADAM_WD = 0.01
ADAM_STEP = 10

SMALL_ROWS = 8
FF_TILE = D_FF // 2

VMEM_LIMIT = 48 * 1024 * 1024


def _params(n_axes, vmem=VMEM_LIMIT):
    return pltpu.CompilerParams(dimension_semantics=("arbitrary",) * n_axes, vmem_limit_bytes=vmem)


def _whole(a):
    return pl.BlockSpec(a.shape, functools.partial(lambda i, nd: (0,) * nd, nd=len(a.shape)))


def _dot(a, b):
    return jnp.dot(a.astype(MXU_DTYPE), b.astype(MXU_DTYPE), preferred_element_type=F32)


def _dot_nt(a, b):
    return lax.dot_general(a.astype(MXU_DTYPE), b.astype(MXU_DTYPE), (((1,), (1,)), ((), ())),
                           preferred_element_type=F32)


def _dot_tn(a, b):
    return lax.dot_general(a.astype(MXU_DTYPE), b.astype(MXU_DTYPE), (((0,), (0,)), ((), ())),
                           preferred_element_type=F32)


def _split(x, parts):
    out, r = [], x
    for _ in range(parts):
        h = r.astype(BF16)
        out.append(h)
        r = r - h.astype(F32)
    return out


def _rsum_left(u, x, parts):
    acc = None
    for h in _split(x, parts):
        d = jnp.dot(u, h, preferred_element_type=F32)
        acc = d if acc is None else acc + d
    return acc


def _ones_where(mask):
    return jnp.where(mask, 1.0, 0.0).astype(BF16)


def _sigmoid(x):
    return 1.0 / (1.0 + jnp.exp(-x))


def _softplus(x):
    return jnp.maximum(x, 0.0) + jnp.log(1.0 + jnp.exp(-jnp.abs(x)))


def _head_masks():
    lane = lax.broadcasted_iota(jnp.int32, (1, LANES), 1)
    return [jnp.where(lane < D_HEAD, 1.0, 0.0), jnp.where(lane >= D_HEAD, 1.0, 0.0)]


def _head_rstd(o, masks):
    sq = o * o
    r = None
    for m in masks:
        ms = jnp.sum(sq * m, axis=1, keepdims=True) * (1.0 / D_HEAD)
        t = lax.rsqrt(ms + EPS) * m
        r = t if r is None else r + t
    return r


def _head_mean(t, masks):
    out = None
    for m in masks:
        v = jnp.sum(t * m, axis=1, keepdims=True) * (1.0 / D_HEAD) * m
        out = v if out is None else out + v
    return out


def _mix_out_norm(a, w, res, g, name, tm=512):
    m, k = a.shape
    d = w.shape[1]
    tm = min(tm, m)

    def body(a_ref, w_ref, res_ref, g_ref, x_ref, h_ref):
        xv = res_ref[...] + _dot(a_ref[...], w_ref[...])
        x_ref[...] = xv
        r = lax.rsqrt(jnp.mean(xv * xv, axis=-1, keepdims=True) + EPS)
        h_ref[...] = (xv * r * g_ref[...]).astype(h_ref.dtype)

    row = pl.BlockSpec((tm, d), lambda i: (i, 0))
    return pl.pallas_call(
        body, name=name, grid=(m // tm,),
        in_specs=[pl.BlockSpec((tm, k), lambda i: (i, 0)), pl.BlockSpec((k, d), lambda i: (0, 0)), row,
                  pl.BlockSpec((1, d), lambda i: (0, 0))],
        out_specs=[row, row],
        out_shape=[jax.ShapeDtypeStruct((m, d), F32), jax.ShapeDtypeStruct((m, d), MXU_DTYPE)],
        compiler_params=_params(1),
    )(a, w, res, g)


def _ffn_down_loss(a, w, res, target, g, name, tm=512):
    m, k = a.shape
    d = w.shape[1]
    tm = min(tm, m)

    def body(a_ref, w_ref, res_ref, t_ref, g_ref, dx_ref, dxm_ref, dg_ref, loss_ref):
        xv = res_ref[...] + _dot(a_ref[...], w_ref[...])
        gv = g_ref[...]
        r = lax.rsqrt(jnp.mean(xv * xv, axis=-1, keepdims=True) + EPS)
        xh = xv * r
        e = xh * gv - t_ref[...]
        dy = e * (1.0 / d)
        dxh = dy * gv
        dxv = r * (dxh - xh * jnp.mean(dxh * xh, axis=-1, keepdims=True))
        dx_ref[...] = dxv
        dxm_ref[...] = dxv.astype(dxm_ref.dtype)

        @pl.when(pl.program_id(0) == 0)
        def _():
            dg_ref[...] = jnp.zeros_like(dg_ref)
            loss_ref[...] = jnp.zeros_like(loss_ref)

        dg_ref[...] += jnp.sum(dy * xh, axis=0, keepdims=True)
        part = 0.5 * jnp.sum(jnp.mean(e * e, axis=-1, keepdims=True), axis=0, keepdims=True)
        loss_ref[...] += jnp.broadcast_to(part, loss_ref.shape)

    row = pl.BlockSpec((tm, d), lambda i: (i, 0))
    vec = pl.BlockSpec((1, d), lambda i: (0, 0))
    return pl.pallas_call(
        body, name=name, grid=(m // tm,),
        in_specs=[pl.BlockSpec((tm, k), lambda i: (i, 0)), pl.BlockSpec((k, d), lambda i: (0, 0)), row, row, vec],
        out_specs=[row, row, vec, pl.BlockSpec((1, LANES), lambda i: (0, 0))],
        out_shape=[jax.ShapeDtypeStruct((m, d), F32), jax.ShapeDtypeStruct((m, d), MXU_DTYPE),
                   jax.ShapeDtypeStruct((1, d), F32), jax.ShapeDtypeStruct((1, LANES), F32)],
        compiler_params=_params(1),
    )(a, w, res, target, g)


def _mm_nt(a, w, name, rows=None, out_dtype=F32, tm=512, tk=512):
    m, n = a.shape
    row0, k = rows or (0, w.shape[0])
    tm, tk = min(tm, m), min(tk, k)
    first = row0 // tk

    def body(a_ref, w_ref, o_ref):
        o_ref[...] = _dot_nt(a_ref[...], w_ref[...]).astype(o_ref.dtype)

    return pl.pallas_call(
        body, name=name, grid=(k // tk, m // tm),
        in_specs=[pl.BlockSpec((tm, n), lambda j, i: (i, 0)), pl.BlockSpec((tk, n), lambda j, i: (first + j, 0))],
        out_specs=pl.BlockSpec((tm, tk), lambda j, i: (i, j)),
        out_shape=jax.ShapeDtypeStruct((m, k), out_dtype),
        compiler_params=_params(2),
    )(a, w)


def _mm_tn(a, b, name, tk, tn, tt=1024, out_dtype=BF16):
    t, k = a.shape
    n = b.shape[1]
    tt = min(tt, t)
    steps = t // tt

    def body(a_ref, b_ref, o_ref, acc):
        s = pl.program_id(2)

        @pl.when(s == 0)
        def _():
            acc[...] = jnp.zeros_like(acc)

        acc[...] += _dot_tn(a_ref[...], b_ref[...])

        @pl.when(s == steps - 1)
        def _():
            o_ref[...] = acc[...].astype(o_ref.dtype)

    return pl.pallas_call(
        body, name=name, grid=(k // tk, n // tn, steps),
        in_specs=[pl.BlockSpec((tt, tk), lambda i, j, s: (s, i)), pl.BlockSpec((tt, tn), lambda i, j, s: (s, j))],
        out_specs=pl.BlockSpec((tk, tn), lambda i, j, s: (i, j)),
        out_shape=jax.ShapeDtypeStruct((k, n), out_dtype),
        scratch_shapes=[pltpu.VMEM((tk, tn), F32)],
        compiler_params=_params(3),
    )(a, b)


def _ffn_up(h, wg_t, wu_t, name, tm=1024, tn=FF_TILE):
    m, k = h.shape
    n = wg_t.shape[0]
    tm = min(tm, m)

    def body(h_ref, wg_ref, wu_ref, gate_ref, up_ref, ff_ref):
        hv = h_ref[...]
        gate = _dot_nt(hv, wg_ref[...])
        up = _dot_nt(hv, wu_ref[...])
        gate_ref[...] = gate.astype(gate_ref.dtype)
        up_ref[...] = up.astype(up_ref.dtype)
        ff_ref[...] = (gate * _sigmoid(gate) * up).astype(ff_ref.dtype)

    wspec = pl.BlockSpec((tn, k), lambda j, i: (j, 0))
    ospec = pl.BlockSpec((tm, tn), lambda j, i: (i, j))
    return pl.pallas_call(
        body, name=name, grid=(n // tn, m // tm),
        in_specs=[pl.BlockSpec((tm, k), lambda j, i: (i, 0)), wspec, wspec],
        out_specs=[ospec, ospec, ospec],
        out_shape=[jax.ShapeDtypeStruct((m, n), MXU_DTYPE)] * 3,
        compiler_params=_params(2),
    )(h, wg_t, wu_t)


def _ffn_bwd_act(dx, wd, gate, up, name, tm=1024, tn=FF_TILE):
    m, k = dx.shape
    n = wd.shape[0]
    tm = min(tm, m)

    def body(dx_ref, wd_ref, gate_ref, up_ref, dgate_ref, dup_ref):
        dff = _dot_nt(dx_ref[...], wd_ref[...])
        gate = gate_ref[...].astype(F32)
        sg = _sigmoid(gate)
        dgate_ref[...] = (dff * up_ref[...].astype(F32) * sg * (1.0 + gate * (1.0 - sg))).astype(dgate_ref.dtype)
        dup_ref[...] = (dff * gate * sg).astype(dup_ref.dtype)

    ospec = pl.BlockSpec((tm, tn), lambda j, i: (i, j))
    return pl.pallas_call(
        body, name=name, grid=(n // tn, m // tm),
        in_specs=[pl.BlockSpec((tm, k), lambda j, i: (i, 0)), pl.BlockSpec((tn, k), lambda j, i: (j, 0)),
                  ospec, ospec],
        out_specs=[ospec, ospec],
        out_shape=[jax.ShapeDtypeStruct((m, n), MXU_DTYPE), jax.ShapeDtypeStruct((m, n), MXU_DTYPE)],
        compiler_params=_params(2),
    )(dx, wd, gate, up)


def _chunk_masks():
    r = lax.broadcasted_iota(jnp.int32, (SUPER, SUPER), 0)
    c = lax.broadcasted_iota(jnp.int32, (SUPER, SUPER), 1)
    same = jnp.right_shift(r, 6) == jnp.right_shift(c, 6)
    lower = jnp.logical_and(same, c <= r)
    upper = jnp.logical_and(same, c >= r)
    return same, lower, upper


def _head_block_mask():
    r = lax.broadcasted_iota(jnp.int32, (LANES, LANES), 0)
    c = lax.broadcasted_iota(jnp.int32, (LANES, LANES), 1)
    return jnp.where(jnp.right_shift(r, 6) == jnp.right_shift(c, 6), 1.0, 0.0)


def _lower_bound(lb_raw):
    return 1.0 / (1.0 + jnp.exp(lb_raw[1:2, :] - lb_raw[0:1, :]))


HGRN_UNROLL = 4
PER_SUPER = SUPER // CHUNK
CHUNK_ROWS = [slice(c * CHUNK, (c + 1) * CHUNK) for c in range(PER_SUPER)]


def _over_chunks(rows):
    return jnp.concatenate([jnp.broadcast_to(r, (CHUNK, LANES)) for r in rows], axis=0)


def _hgrn_gates(q, hf, lb, lower_b):
    sig = _sigmoid(hf)
    f = lb + (1.0 - lb) * sig
    k = 1.0 - f
    lf = jnp.log(f)
    b = _rsum_left(lower_b, lf, 2)
    ends = [b[cr.stop - 1:cr.stop, :] for cr in CHUNK_ROWS]
    eb = jnp.exp(b)
    enb = jnp.exp(-b)
    edb = jnp.exp(_over_chunks(ends) - b)
    decs = [jnp.exp(e) for e in ends]
    return sig, f, k, decs, eb, enb, edb, q * eb, k * enb, k * edb


def _hgrn_fwd(proj, lower_bounds, norm_g, n_seq, seq):
    t = n_seq * seq
    n_super = seq // SUPER
    n_chunks = seq // CHUNK

    def body(q_ref, f_ref, i_ref, g_ref, lb_ref, ng_ref, out_ref, opre_ref, st_ref):
        masks = _head_masks()
        _, lower, _ = _chunk_masks()
        lower_b = _ones_where(lower)
        bd = _head_block_mask()
        lb = _lower_bound(lb_ref[...])
        ng = ng_ref[...]

        def step(it, st):
            blocks = [HGRN_UNROLL * it + u for u in range(HGRN_UNROLL)]
            rows = [pl.ds(pl.multiple_of(sb * SUPER, SUPER), SUPER) for sb in blocks]
            vs = [i_ref[rw, :] for rw in rows]
            gates = [_hgrn_gates(q_ref[rw, :], f_ref[rw, :], lb, lower_b) for rw in rows]
            decs, qes, kes, kds = ([g[k] for g in gates] for k in (3, 7, 8, 9))
            scores = [[_dot_nt(qe * m, ke) for m in masks] for qe, ke in zip(qes, kes)]
            updates = [[_dot_tn(v[cr], kd[cr]) for cr in CHUNK_ROWS] for v, kd in zip(vs, kds)]
            states = [st]
            for dec_b, upd_b in zip(decs, updates):
                for dec, upd in zip(dec_b, upd_b):
                    states.append(states[-1] * dec + bd * upd)
            for u, sb in enumerate(blocks):
                for c in range(PER_SUPER):
                    st_ref[0, 0, sb * PER_SUPER + c] = states[u * PER_SUPER + c]
            intra = [[_dot(jnp.where(lower, p, 0.0), v) for p in sc] for sc, v in zip(scores, vs)]
            inter = [[_dot_nt(qe[cr], states[u * PER_SUPER + c]) for c, cr in enumerate(CHUNK_ROWS)]
                     for u, qe in enumerate(qes)]
            for rw, intra_b, inter_b in zip(rows, intra, inter):
                o = intra_b[0] * masks[0] + intra_b[1] * masks[1] + jnp.concatenate(inter_b, axis=0)
                opre_ref[rw, :] = o
                hg = g_ref[rw, :]
                on = o * _head_rstd(o, masks) * ng
                out_ref[rw, :] = (on * hg * _sigmoid(hg)).astype(out_ref.dtype)
            return states[-1]

        lax.fori_loop(0, n_super // HGRN_UNROLL, step, jnp.zeros((LANES, LANES), F32))

    def col(k):
        return pl.BlockSpec((seq, LANES), lambda p, b: (b, k * N_PAIRS + p))

    vec = lambda rows: pl.BlockSpec((rows, LANES), lambda p, b: (0, p))
    ospec = pl.BlockSpec((seq, LANES), lambda p, b: (b, p))
    return pl.pallas_call(
        body, name="hgrn_fwd", grid=(N_PAIRS, n_seq),
        in_specs=[col(0), col(1), col(2), col(3), vec(2), vec(1)],
        out_specs=[ospec, ospec,
                   pl.BlockSpec((1, 1, n_chunks, LANES, LANES), lambda p, b: (b, p, 0, 0, 0))],
        out_shape=[jax.ShapeDtypeStruct((t, 2 * GROUP), MXU_DTYPE), jax.ShapeDtypeStruct((t, GROUP), F32),
                   jax.ShapeDtypeStruct((n_seq, N_PAIRS, n_chunks, LANES, LANES), F32)],
        compiler_params=_params(2),
    )(proj, proj, proj, proj, lower_bounds, norm_g)


def _hgrn_bwd(proj, lower_bounds, norm_g, dmix, opre, states, n_seq, seq):
    t = n_seq * seq
    n_super = seq // SUPER
    n_chunks = seq // CHUNK
    per = SUPER // CHUNK

    def body(q_ref, f_ref, i_ref, g_ref, lb_ref, ng_ref, dm_ref, opre_ref, st_ref,
             dq_ref, df_ref, di_ref, dg_ref, dlb_ref, dng_ref):
        masks = _head_masks()
        _, lower, upper = _chunk_masks()
        lower_b, upper_b = _ones_where(lower), _ones_where(upper)
        bd = _head_block_mask()
        lb_raw = lb_ref[...]
        lb = _lower_bound(lb_raw)
        ng = ng_ref[...]

        @pl.when(pl.program_id(1) == 0)
        def _():
            dlb_ref[...] = jnp.zeros_like(dlb_ref)
            dng_ref[...] = jnp.zeros_like(dng_ref)

        def first_half(sb):
            rows = pl.ds(pl.multiple_of(sb * SUPER, SUPER), SUPER)
            q, hf, v, hg = q_ref[rows, :], f_ref[rows, :], i_ref[rows, :], g_ref[rows, :]
            sig, f, k, decs, eb, enb, edb, qe, ke, kd = _hgrn_gates(q, hf, lb, lower_b)
            o = opre_ref[rows, :]
            r = _head_rstd(o, masks)
            oh = o * r
            dm = dm_ref[rows, :]
            sg = _sigmoid(hg)
            dg_ref[rows, :] = (dm * oh * ng * sg * (1.0 + hg * (1.0 - sg))).astype(dg_ref.dtype)
            don = dm * hg * sg
            dng_ref[...] += jnp.sum(don * oh, axis=0, keepdims=True)
            doh = don * ng
            do = r * (doh - oh * _head_mean(doh * oh, masks))
            doms = [do * m for m in masks]
            qems = [qe * m for m in masks]
            scores = [_dot_nt(qem, ke) for qem in qems]
            dscores = [_dot_nt(dom, v) for dom in doms]
            prevs = [st_ref[0, 0, sb * per + c] for c in range(per)]
            dst_in = [_dot_tn(do[cr], qe[cr]) for cr in CHUNK_ROWS]
            dqe_i = [_dot(do[cr], prev) for cr, prev in zip(CHUNK_ROWS, prevs)]
            return dict(rows=rows, v=v, sig=sig, f=f, decs=decs, eb=eb, enb=enb, edb=edb, qe=qe, ke=ke, kd=kd,
                        doms=doms, qems=qems, scores=scores, dscores=dscores, prevs=prevs, dst_in=dst_in, dqe_i=dqe_i)

        def second_half(blk, dsts):
            v, qe, ke, kd = blk["v"], blk["qe"], blk["ke"], blk["kd"]
            ps = [jnp.where(lower, p, 0.0) for p in blk["scores"]]
            dps = [jnp.where(lower, dp, 0.0) for dp in blk["dscores"]]
            dqe_h = [_dot(dp, ke) for dp in dps]
            dke_h = [_dot_tn(dp, qem) for dp, qem in zip(dps, blk["qems"])]
            dv_h = [_dot_tn(p, dom) for p, dom in zip(ps, blk["doms"])]
            dus = [bd * d for d in dsts]
            dv_i = [_dot_nt(kd[cr], du) for cr, du in zip(CHUNK_ROWS, dus)]
            dkd_i = [_dot(v[cr], du) for cr, du in zip(CHUNK_ROWS, dus)]

            def finish():
                dqe = dqe_h[0] * masks[0] + dqe_h[1] * masks[1] + jnp.concatenate(blk["dqe_i"], axis=0)
                dke = dke_h[0] + dke_h[1]
                dv = dv_h[0] + dv_h[1] + jnp.concatenate(dv_i, axis=0)
                dkd = jnp.concatenate(dkd_i, axis=0)
                dk = dke * blk["enb"] + dkd * blk["edb"]
                db = dqe * qe - dke * ke - dkd * kd
                dkd_kd = dkd * kd
                dends = [jnp.sum(dkd_kd[cr], axis=0, keepdims=True)
                         + jnp.sum(dsts[c] * blk["prevs"][c], axis=0, keepdims=True) * blk["decs"][c]
                         for c, cr in enumerate(CHUNK_ROWS)]
                dlf = _rsum_left(upper_b, db, 2) + _over_chunks(dends)
                sig = blk["sig"]
                dfv = dlf / blk["f"] - dk
                rows = blk["rows"]
                dq_ref[rows, :] = (dqe * blk["eb"]).astype(dq_ref.dtype)
                di_ref[rows, :] = dv.astype(di_ref.dtype)
                df_ref[rows, :] = (dfv * (1.0 - lb) * sig * (1.0 - sig)).astype(df_ref.dtype)
                dlb = jnp.sum(dfv * (1.0 - sig), axis=0, keepdims=True)
                da0 = dlb * lb * (1.0 - lb)
                dlb_ref[0:1, :] += da0
                dlb_ref[1:2, :] -= da0

            return finish

        def step(it, dst):
            blocks = [first_half(n_super - 1 - HGRN_UNROLL * it - u) for u in range(HGRN_UNROLL)]
            all_dsts = []
            for blk in blocks:
                dsts = [None] * per
                for c in reversed(range(per)):
                    dsts[c] = dst
                    dst = bd * (dst * blk["decs"][c] + blk["dst_in"][c])
                all_dsts.append(dsts)
            for finish in [second_half(blk, dsts) for blk, dsts in zip(blocks, all_dsts)]:
                finish()
            return dst

        lax.fori_loop(0, n_super // HGRN_UNROLL, step, jnp.zeros((LANES, LANES), F32))

    def col(k):
        return pl.BlockSpec((seq, LANES), lambda p, b: (b, k * N_PAIRS + p))

    vec = lambda rows: pl.BlockSpec((rows, LANES), lambda p, b: (0, p))
    ospec = pl.BlockSpec((seq, LANES), lambda p, b: (b, p))
    piece = jax.ShapeDtypeStruct((t, GROUP), MXU_DTYPE)
    return pl.pallas_call(
        body, name="hgrn_bwd", grid=(N_PAIRS, n_seq),
        in_specs=[col(0), col(1), col(2), col(3), vec(2), vec(1), ospec, ospec,
                  pl.BlockSpec((1, 1, n_chunks, LANES, LANES), lambda p, b: (b, p, 0, 0, 0))],
        out_specs=[ospec, ospec, ospec, ospec, vec(2), vec(1)],
        out_shape=[piece, piece, piece, piece,
                   jax.ShapeDtypeStruct((2, GROUP), F32), jax.ShapeDtypeStruct((1, GROUP), F32)],
        compiler_params=_params(2),
    )(proj, proj, proj, proj, lower_bounds, norm_g, dmix, opre, states)


SB_SCALE = 1.0 / math.sqrt(D_HEAD)


SB_STEP = 2 * SB_BLOCK
QUERY_BLOCKS = (slice(0, SB_BLOCK), slice(SB_BLOCK, SB_STEP))


def _triangle(keep):
    row = lax.broadcasted_iota(jnp.int32, (SB_BLOCK, SB_BLOCK), 0)
    col = lax.broadcasted_iota(jnp.int32, (SB_BLOCK, SB_BLOCK), 1)
    return _ones_where(keep(row, col))


SB_HALF = SB_BLOCK // 2


def _keep(on_diagonal, x):
    if not on_diagonal:
        return x
    row = lax.broadcasted_iota(jnp.int32, (SB_HALF, SB_HALF), 0)
    col = lax.broadcasted_iota(jnp.int32, (SB_HALF, SB_HALF), 1)
    return jnp.where(col < row, x, 0.0)


def _lower_quadrants(fn, n_out, *tiles):
    def quadrant(r, c):
        return [t[r * SB_HALF:(r + 1) * SB_HALF, c * SB_HALF:(c + 1) * SB_HALF] for t in tiles]

    top, left, bottom = fn(True, *quadrant(0, 0)), fn(False, *quadrant(1, 0)), fn(True, *quadrant(1, 1))
    zero = jnp.zeros((SB_HALF, SB_HALF), F32)
    return [jnp.concatenate([jnp.concatenate([top[o], zero], axis=1), jnp.concatenate([left[o], bottom[o]], axis=1)],
                            axis=0) for o in range(n_out)]


def _sb_fwd(proj, norm_g, mixed, shards, n_seq, seq):
    t = n_seq * seq
    nq = seq // SB_STEP
    q0, k0, v0 = 0, N_PAIRS, 2 * N_PAIRS
    n_w = len(shards)
    n_steps = N_PAIRS * n_seq * nq
    tri = _triangle(lambda row, col: row >= col)

    def body(q_ref, k_ref, v_ref, ng_ref, tri_ref, mixed_in, *rest):
        del mixed_in
        shard_refs = rest[:n_w]
        out_ref, opre_ref, ctot_ref = rest[n_w:n_w + 3]
        gathered = rest[n_w + 3:2 * n_w + 3]
        send_sems, recv_sems, local_sems = rest[2 * n_w + 3:]
        i = pl.program_id(2)
        step = (pl.program_id(0) * n_seq + pl.program_id(1)) * nq + i
        plan = _GatherPlan(shard_refs, gathered, send_sems, recv_sems, local_sems)

        @pl.when(step == 0)
        def _():
            plan.start()

        @pl.when(step == (3 * n_steps) // 4)
        def _():
            plan.forward()

        masks = _head_masks()
        suffix = tri_ref[...]
        qhs = [[(q_ref[blk, :] * SB_SCALE * m).astype(MXU_DTYPE) for m in masks] for blk in QUERY_BLOCKS]

        def tiles(js, work, carry):
            rows = [pl.ds(pl.multiple_of(j * SB_BLOCK, SB_BLOCK), SB_BLOCK) for j in js]
            ks = [k_ref[rw, :].astype(MXU_DTYPE) for rw in rows]
            vs = [v_ref[rw, :].astype(MXU_DTYPE) for rw in rows]
            zs = [[_dot_nt(qh, ks[ts]) for qh in qhs[qb]] for qb, ts, _ in work]
            ccs = [[_lower_quadrants(lambda on, zq: (_keep(on, _softplus(zq)),), 1, z)[0] if diag else _softplus(z)
                    for z in zw] for zw, (_, _, diag) in zip(zs, work)]
            sums = [[jnp.dot(cc.astype(BF16), suffix, preferred_element_type=F32) for cc in cw] for cw in ccs]
            out = [list(per_block) for per_block in carry]
            for h in range(len(masks)):
                for w, (qb, ts, diag) in enumerate(work):
                    run, acc = out[qb][h]
                    logit = zs[w][h] - (sums[w][h] + run)
                    if diag:
                        a = _lower_quadrants(lambda on, lq: (_keep(on, jnp.exp(lq)),), 1, logit)[0]
                    else:
                        a = jnp.exp(logit)
                    out[qb][h] = (run + sums[w][h][:, 0:1], acc + _dot(a, vs[ts]))
            return tuple(tuple(per_block) for per_block in out)

        start = ((jnp.zeros((SB_BLOCK, 1), F32), jnp.zeros((SB_BLOCK, LANES), F32)),) * 2
        carry = tiles([2 * i, 2 * i + 1], [(0, 0, True), (1, 1, True), (1, 0, False)], (start, start))
        both = [(0, 0, False), (0, 1, False), (1, 0, False), (1, 1, False)]
        carry = lax.fori_loop(0, i, lambda s, cy: tiles([2 * (i - s) - 1, 2 * (i - s) - 2], both, cy), carry)
        opre = jnp.concatenate([cb[0][1] * masks[0] + cb[1][1] * masks[1] for cb in carry], axis=0)
        ctot = jnp.concatenate([cb[0][0] * masks[0] + cb[1][0] * masks[1] for cb in carry], axis=0)
        opre_ref[...] = opre
        ctot_ref[...] = ctot
        out_ref[...] = (opre * _head_rstd(opre, masks) * ng_ref[...]).astype(out_ref.dtype)

        @pl.when(step == n_steps - 1)
        def _():
            plan.finish()

    qspec = pl.BlockSpec((SB_STEP, LANES), lambda p, b, i: (b * nq + i, q0 + p))
    ospec = pl.BlockSpec((SB_STEP, LANES), lambda p, b, i: (b * nq + i, p))
    hbm = pl.BlockSpec(memory_space=pltpu.HBM)
    outs = pl.pallas_call(
        body, name="sb_fwd", grid=(N_PAIRS, n_seq, nq),
        in_specs=[qspec,
                  pl.BlockSpec((seq, LANES), lambda p, b, i: (b, k0 + p)),
                  pl.BlockSpec((seq, LANES), lambda p, b, i: (b, v0 + p)),
                  pl.BlockSpec((1, LANES), lambda p, b, i: (0, p)),
                  pl.BlockSpec(tri.shape, lambda p, b, i: (0, 0)), hbm] + [hbm] * n_w,
        out_specs=[pl.BlockSpec((SB_STEP, LANES), lambda p, b, i: (b * nq + i, N_PAIRS + p)), ospec, ospec]
        + [hbm] * n_w,
        out_shape=[jax.ShapeDtypeStruct(mixed.shape, mixed.dtype), jax.ShapeDtypeStruct((t, GROUP), F32),
                   jax.ShapeDtypeStruct((t, GROUP), F32)]
        + [jax.ShapeDtypeStruct((N_DEV,) + s.shape, s.dtype) for s in shards],
        scratch_shapes=[pltpu.SemaphoreType.DMA((n_w * _GatherPlan.COPIES,)),
                        pltpu.SemaphoreType.DMA((n_w * _GatherPlan.COPIES,)), pltpu.SemaphoreType.DMA((n_w,))],
        input_output_aliases={5: 0},
        compiler_params=_params(3),
    )(proj, proj, proj, norm_g, tri, mixed, *shards)
    return outs[0], outs[1], outs[2], list(outs[3:])


def _sb_bwd(proj, norm_g, dmix, opre, ctot, grads, n_seq, seq):
    t = n_seq * seq
    nq = seq // SB_STEP
    q0, k0, v0 = 0, N_PAIRS, 2 * N_PAIRS
    n_w = len(grads)
    n_steps = N_PAIRS * n_seq * nq
    tri = _triangle(lambda row, col: row <= col)

    def body(q_ref, k_ref, v_ref, ng_ref, tri_ref, dm_ref, opre_ref, ctot_ref, *rest):
        grad_refs = rest[:n_w]
        dq_ref, dk_ref, dv_ref, dng_ref = rest[n_w:n_w + 4]
        lands = rest[n_w + 4:2 * n_w + 4]
        dk_acc, dv_acc, send_sems, recv_sems, local_sems = rest[2 * n_w + 4:]
        p_id, b_id, i = pl.program_id(0), pl.program_id(1), pl.program_id(2)
        step = (p_id * n_seq + b_id) * nq + i
        plan = _ScatterPlan(grad_refs, lands, send_sems, recv_sems, local_sems)

        @pl.when(step == 0)
        def _():
            plan.start()

        masks = _head_masks()
        upto = tri_ref[...]

        def prefix(x):
            return jnp.dot(x.astype(BF16), upto, preferred_element_type=F32)

        @pl.when(i == 0)
        def _():
            dk_acc[...] = jnp.zeros_like(dk_acc)
            dv_acc[...] = jnp.zeros_like(dv_acc)

        @pl.when(jnp.logical_and(b_id == 0, i == 0))
        def _():
            dng_ref[...] = jnp.zeros_like(dng_ref)

        o = opre_ref[...]
        rs = _head_rstd(o, masks)
        oh = o * rs
        dm = dm_ref[...]
        dng_ref[...] += jnp.sum(dm * oh, axis=0, keepdims=True)
        doh = dm * ng_ref[...]
        do = rs * (doh - oh * _head_mean(doh * oh, masks))

        heads = range(len(masks))
        qs = [q_ref[blk, :] * SB_SCALE for blk in QUERY_BLOCKS]
        dos = [do[blk] for blk in QUERY_BLOCKS]
        qhs = [[(q * m).astype(MXU_DTYPE) for m in masks] for q in qs]
        doms = [[(d * m).astype(MXU_DTYPE) for m in masks] for d in dos]
        head_rows = [jnp.where(jnp.right_shift(lax.broadcasted_iota(jnp.int32, (LANES, 1), 0), 6) == h, 1.0, 0.0)
                     for h in heads]
        qhts = [[(qt * hr).astype(MXU_DTYPE) for hr in head_rows] for qt in [q.astype(F32).T for q in qs]]
        domts = [[(dt * hr).astype(MXU_DTYPE) for hr in head_rows] for dt in [d.T for d in dos]]
        totals = [[ctot_ref[blk, h * D_HEAD:h * D_HEAD + 1] for h in heads] for blk in QUERY_BLOCKS]

        def tiles(js, work, carry):
            rows = [pl.ds(pl.multiple_of(j * SB_BLOCK, SB_BLOCK), SB_BLOCK) for j in js]
            ks = [k_ref[rw, :].astype(MXU_DTYPE) for rw in rows]
            vs = [v_ref[rw, :].astype(MXU_DTYPE) for rw in rows]
            zs = [[_dot_nt(qhs[qb][h], ks[ts]) for h in heads] for qb, ts, _ in work]
            das = [[_dot_nt(doms[qb][h], vs[ts]) for h in heads] for qb, ts, _ in work]
            def gates(on, zq):
                sp = _softplus(zq)
                return _keep(on, sp), zq - sp, _keep(on, jnp.exp(zq - sp))

            trio = [[_lower_quadrants(gates, 3, z) if diag else gates(False, z) for z in zw]
                    for zw, (_, _, diag) in zip(zs, work)]
            ccs, lsigs, sigs = ([[t[o] for t in tw] for tw in trio] for o in range(3))
            pres = [[prefix(cc) for cc in cw] for cw in ccs]
            out = [list(per_block) for per_block in carry]
            for h in heads:
                for w, (qb, ts, diag) in enumerate(work):
                    pc, pdl, dq_h = out[qb][h]
                    logit = lsigs[w][h] + pres[w][h] - (totals[qb][h] - pc)
                    if diag:
                        a = _lower_quadrants(lambda on, lq: (_keep(on, jnp.exp(lq)),), 1, logit)[0]
                    else:
                        a = jnp.exp(logit)
                    dl = a * das[w][h]
                    dv_acc[js[ts]] += _dot(domts[qb][h], a)
                    dpre = prefix(dl)
                    dz = dl - sigs[w][h] * (pdl + dpre)
                    dzb = dz.astype(MXU_DTYPE)
                    dk_acc[js[ts]] += _dot(qhts[qb][h], dzb)
                    out[qb][h] = (pc + pres[w][h][:, SB_BLOCK - 1:SB_BLOCK], pdl + dpre[:, SB_BLOCK - 1:SB_BLOCK],
                                  dq_h + _dot(dzb, ks[ts]))
            return tuple(tuple(per_block) for per_block in out)

        zero = jnp.zeros((SB_BLOCK, 1), F32)
        start = ((zero, zero, jnp.zeros((SB_BLOCK, LANES), F32)),) * 2
        both = [(0, 0, False), (0, 1, False), (1, 0, False), (1, 1, False)]
        carry = lax.fori_loop(0, i, lambda s, cy: tiles([2 * s, 2 * s + 1], both, cy), (start, start))
        carry = tiles([2 * i, 2 * i + 1], [(0, 0, True), (1, 0, False), (1, 1, True)], carry)
        dq = jnp.concatenate([cb[0][2] * masks[0] + cb[1][2] * masks[1] for cb in carry], axis=0)
        dq_ref[...] = (dq * SB_SCALE).astype(dq_ref.dtype)

        @pl.when(i == nq - 1)
        def _():
            for j in range(seq // SB_BLOCK):
                tile_rows = slice(j * SB_BLOCK, (j + 1) * SB_BLOCK)
                dk_ref[tile_rows, :] = dk_acc[j].T.astype(dk_ref.dtype)
                dv_ref[tile_rows, :] = dv_acc[j].T.astype(dv_ref.dtype)

        @pl.when(step == n_steps - 1)
        def _():
            plan.finish()

    qspec = pl.BlockSpec((SB_STEP, LANES), lambda p, b, i: (b * nq + i, q0 + p))
    ospec = pl.BlockSpec((SB_STEP, LANES), lambda p, b, i: (b * nq + i, p))
    dmspec = pl.BlockSpec((SB_STEP, LANES), lambda p, b, i: (b * nq + i, N_PAIRS + p))
    full = lambda k: pl.BlockSpec((seq, LANES), lambda p, b, i: (b, k + p))
    vec = pl.BlockSpec((1, LANES), lambda p, b, i: (0, p))
    hbm = pl.BlockSpec(memory_space=pltpu.HBM)
    piece = jax.ShapeDtypeStruct((t, GROUP), MXU_DTYPE)
    outs = pl.pallas_call(
        body, name="sb_bwd", grid=(N_PAIRS, n_seq, nq),
        in_specs=[qspec, full(k0), full(v0), vec, pl.BlockSpec(tri.shape, lambda p, b, i: (0, 0)), dmspec, ospec, ospec]
        + [hbm] * n_w,
        out_specs=[ospec, full(0), full(0), vec] + [hbm] * n_w,
        out_shape=[piece, piece, piece, jax.ShapeDtypeStruct((1, GROUP), F32)]
        + [jax.ShapeDtypeStruct(g.shape, g.dtype) for g in grads],
        scratch_shapes=[pltpu.VMEM((seq // SB_BLOCK, LANES, SB_BLOCK), F32),
                        pltpu.VMEM((seq // SB_BLOCK, LANES, SB_BLOCK), F32),
                        pltpu.SemaphoreType.DMA((n_w * (N_DEV - 1),)), pltpu.SemaphoreType.DMA((n_w * (N_DEV - 1),)),
                        pltpu.SemaphoreType.DMA((n_w,))],
        compiler_params=_params(3),
    )(proj, proj, proj, norm_g, tri, dmix, opre, ctot, *[pltpu.with_memory_space_constraint(g, pltpu.HBM) for g in grads])
    return outs[0], outs[1], outs[2], outs[3], list(outs[4:])


def _mesh_place():
    x, y, c = lax.axis_index("x"), lax.axis_index("y"), lax.axis_index("c")
    return x, y, c


def _peer(x, y, c, k):
    px = lax.rem(x + ((k >> 2) & 1), 2)
    py = lax.rem(y + ((k >> 1) & 1), 2)
    pc = lax.rem(c + (k & 1), 2)
    return (px, py, pc), 4 * px + 2 * py + pc


def _remote(src, dst, send_sem, recv_sem, to):
    return pltpu.make_async_remote_copy(src_ref=src, dst_ref=dst, send_sem=send_sem, recv_sem=recv_sem,
                                        device_id=to, device_id_type=pl.DeviceIdType.MESH)


class _GatherPlan:
    COPIES = 7

    def __init__(self, shards, gathered, send_sems, recv_sems, local_sems):
        x, y, c = _mesh_place()
        self.c = c
        self.me = (x, y, c)
        self.sibling = (x, y, 1 - c)
        self.chips = [(1 - x, y), (x, 1 - y), (1 - x, 1 - y)]
        self.tensors = list(zip(shards, gathered))
        self.send_sems, self.recv_sems, self.local_sems = send_sems, recv_sems, local_sems

    @staticmethod
    def _index(place):
        return 4 * place[0] + 2 * place[1] + place[2]

    def _copy(self, w, k, block, to, own=False):
        shard, gathered = self.tensors[w]
        slot = gathered.at[self._index(block)]
        n = w * self.COPIES + k
        return _remote(shard if own else slot, slot, self.send_sems.at[n], self.recv_sems.at[n], to)

    def _local(self, w):
        shard, gathered = self.tensors[w]
        return pltpu.make_async_copy(shard, gathered.at[self._index(self.me)], self.local_sems.at[w])

    def _first(self, w):
        return [self._copy(w, 0, self.me, self.sibling, own=True)] + [
            self._copy(w, 1 + j, self.me, (*chip, self.c), own=True) for j, chip in enumerate(self.chips)]

    def _passed(self, w):
        return [self._copy(w, 4 + j, (*chip, self.c), self.sibling) for j, chip in enumerate(self.chips)]

    def start(self):
        for w in range(len(self.tensors)):
            self._local(w).start()
            for cp in self._first(w):
                cp.start()

    def forward(self):
        for w in range(len(self.tensors)):
            passed = self._passed(w)
            for j, chip in enumerate(self.chips):
                self._copy(w, 1 + j, (*chip, self.c), self.me).wait_recv()
                passed[j].start()

    def finish(self):
        for w in range(len(self.tensors)):
            self._copy(w, 0, self.sibling, self.me).wait_recv()
            for j, chip in enumerate(self.chips):
                self._copy(w, 4 + j, (*chip, 1 - self.c), self.me).wait_recv()
            for cp in self._first(w) + self._passed(w):
                cp.wait_send()
            self._local(w).wait()


class _ScatterPlan:
    def __init__(self, grads, lands, send_sems, recv_sems, local_sems):
        self.place = _mesh_place()
        x, y, c = self.place
        self.me = 4 * x + 2 * y + c
        self.tensors = list(zip(grads, lands))
        self.send_sems, self.recv_sems, self.local_sems = send_sems, recv_sems, local_sems

    def _copies(self, w):
        grad, land = self.tensors[w]
        out = []
        for k in range(1, N_DEV):
            peer, pidx = _peer(*self.place, k)
            n = w * (N_DEV - 1) + k - 1
            sems = (self.send_sems.at[n], self.recv_sems.at[n], peer)
            out.append((_remote(grad.at[pidx], land.at[self.me], *sems), _remote(grad.at[pidx], land.at[pidx], *sems)))
        return out

    def _local(self, w):
        grad, land = self.tensors[w]
        return pltpu.make_async_copy(grad.at[self.me], land.at[self.me], self.local_sems.at[w])

    def start(self):
        for w in range(len(self.tensors)):
            self._local(w).start()
            for send, _ in self._copies(w):
                send.start()

    def finish(self):
        for w in range(len(self.tensors)):
            copies = self._copies(w)
            for _, arrival in copies:
                arrival.wait_recv()
            for send, _ in copies:
                send.wait_send()
            self._local(w).wait()


def _cast_shards(shards):
    def body(*refs):
        n = len(refs) // 2
        for src, dst in zip(refs[:n], refs[n:]):
            dst[...] = src[...].astype(dst.dtype)

    shapes = [jax.ShapeDtypeStruct(s.shape, BF16) for s in shards]
    return pl.pallas_call(
        body, name="cast_shards", grid=(1,),
        in_specs=[_whole(s) for s in shards], out_specs=[_whole(s) for s in shapes], out_shape=shapes,
        compiler_params=_params(1),
    )(*shards)


def _gather_w_in(shard, x, g, chunk=512):
    rows, cols = shard.shape
    t, d = x.shape
    chunk = min(chunk, t)
    n_chunks = t // chunk

    def body(w_ref, x_ref, g_ref, out_ref, h_ref, x_buf, send_sems, recv_sems, local_sems, x_sems):
        plan = _GatherPlan([w_ref], [out_ref], send_sems, recv_sems, local_sems)
        plan.start()

        def fetch(j):
            return pltpu.make_async_copy(x_ref.at[pl.ds(j * chunk, chunk)], x_buf.at[j % 2], x_sems.at[j % 2])

        fetch(0).start()
        gv = g_ref[...]
        for j in range(n_chunks):
            if j + 1 < n_chunks:
                fetch(j + 1).start()
            fetch(j).wait()
            xv = x_buf[j % 2]
            r = lax.rsqrt(jnp.mean(xv * xv, axis=-1, keepdims=True) + EPS)
            h_ref[j * chunk:(j + 1) * chunk, :] = (xv * r * gv).astype(h_ref.dtype)
        plan.forward()
        plan.finish()

    vmem = pl.BlockSpec(memory_space=pltpu.VMEM)
    hbm = pl.BlockSpec(memory_space=pltpu.HBM)
    return pl.pallas_call(
        body, name="gather_w_in", in_specs=[hbm, hbm, vmem], out_specs=[hbm, vmem],
        out_shape=[jax.ShapeDtypeStruct((N_DEV, rows, cols), shard.dtype), jax.ShapeDtypeStruct((t, d), MXU_DTYPE)],
        scratch_shapes=[pltpu.VMEM((2, chunk, d), F32), pltpu.SemaphoreType.DMA((_GatherPlan.COPIES,)),
                        pltpu.SemaphoreType.DMA((_GatherPlan.COPIES,)), pltpu.SemaphoreType.DMA((1,)),
                        pltpu.SemaphoreType.DMA((2,))],
        compiler_params=pltpu.CompilerParams(vmem_limit_bytes=VMEM_LIMIT),
    )(shard, x, g)


def _dh_norm_bwd(pairs, x, g, res, name, tm=512, after=(), next_w=None):
    m, d = x.shape
    tm = min(tm, m)
    n_p = len(pairs)
    n_steps = m // tm
    n_in = 2 * n_p + 3 + len(after) + (next_w is not None)

    def body(*refs):
        x_ref, g_ref, res_ref = refs[2 * n_p:2 * n_p + 3]
        dx_ref, dg_ref = refs[n_in:n_in + 2]

        @pl.when(pl.program_id(0) == 0)
        def _():
            dg_ref[...] = jnp.zeros_like(dg_ref)

        dh = None
        for q in range(n_p):
            part = _dot(refs[2 * q][...], refs[2 * q + 1][...])
            dh = part if dh is None else dh + part
        xv = x_ref[...]
        r = lax.rsqrt(jnp.mean(xv * xv, axis=-1, keepdims=True) + EPS)
        xh = xv * r
        dxh = dh * g_ref[...]
        dxv = res_ref[...] + r * (dxh - xh * jnp.mean(dxh * xh, axis=-1, keepdims=True))
        dx_ref[...] = dxv
        dg_ref[...] += jnp.sum(dh * xh, axis=0, keepdims=True)
        if next_w is not None:
            refs[n_in + 2][...] = _dot_nt(dxv, refs[n_in - 1][...])

    in_specs, args = [], []
    for a, w, r in pairs:
        k = a.shape[1]
        in_specs += [pl.BlockSpec((tm, k), lambda i: (i, 0)),
                     pl.BlockSpec((k, d), functools.partial(lambda i, r: (r, 0), r=r or 0),
                                  pipeline_mode=pl.Buffered(1))]
        args += [a, w]
    row = pl.BlockSpec((tm, d), lambda i: (i, 0))
    vec = pl.BlockSpec((1, d), lambda i: (0, 0))
    extra_in, extra_out, extra_shape = [], [], []
    if next_w is not None:
        e = next_w.shape[0]
        extra_in = [pl.BlockSpec((e, d), lambda i: (0, 0), pipeline_mode=pl.Buffered(1))]
        extra_out = [pl.BlockSpec((tm, e), lambda i: (i, 0))]
        extra_shape = [jax.ShapeDtypeStruct((m, e), F32)]
        args_tail = (next_w,)
    else:
        args_tail = ()
    return pl.pallas_call(
        body, name=name, grid=(n_steps,),
        in_specs=in_specs + [row, vec, row] + [pl.BlockSpec(memory_space=pl.ANY)] * len(after) + extra_in,
        out_specs=[row, vec] + extra_out,
        out_shape=[jax.ShapeDtypeStruct((m, d), F32), jax.ShapeDtypeStruct((1, d), F32)] + extra_shape,
        compiler_params=_params(1),
    )(*args, x, g, res, *after, *args_tail)


def _dw_exchange_start(pieces, b, name, tn=512, tt=1024):
    t, n = b.shape
    widths = [p.shape[1] for p in pieces]
    rows = sum(widths)
    slot = rows // N_DEV
    steps = t // tt
    n_col = n // tn
    n_p = len(pieces)

    def body(*refs):
        piece_refs, b_ref, land_ref = refs[:n_p], refs[n_p], refs[n_p + 1]
        send_sem, recv_sem, local_sem, g_ref, land_thru, token, acc, stage, stage_sem = refs[n_p + 2:]
        del land_thru
        j, s = pl.program_id(0), pl.program_id(1)
        token[...] = jnp.zeros_like(token)

        @pl.when(s == 0)
        def _():
            acc[...] = jnp.zeros_like(acc)

        bv = b_ref[...]
        off = 0
        for p_ref, width in zip(piece_refs, widths):
            acc[off:off + width, :] += _dot_tn(p_ref[...], bv)
            off += width

        x, y, c = _mesh_place()
        me = 4 * x + 2 * y + c
        for col in range(n_col):
            @pl.when((j == col) & (s == steps - 1))
            def _(col=col):
                cols = pl.ds(col * tn, tn)
                for d in range(N_DEV):
                    stage[d] = acc[d * slot:(d + 1) * slot, :].astype(stage.dtype)
                put = pltpu.make_async_copy(stage, g_ref.at[:, :, cols], stage_sem)
                put.start()
                put.wait()
                for k in range(1, N_DEV):
                    peer, pidx = _peer(x, y, c, k)
                    _remote(g_ref.at[pidx, :, cols], land_ref.at[me, :, cols], send_sem, recv_sem, peer).start()
                if col == n_col - 1:
                    pltpu.make_async_copy(g_ref.at[me], land_ref.at[me], local_sem).start()

    hbm = pl.BlockSpec(memory_space=pltpu.HBM)
    sem = pl.BlockSpec(memory_space=pltpu.SEMAPHORE)
    buf = pltpu.HBM((N_DEV, slot, n), BF16)
    return pl.pallas_call(
        body, name=name, grid=(n_col, steps),
        in_specs=[pl.BlockSpec((tt, width), lambda j, s: (s, 0)) for width in widths]
        + [pl.BlockSpec((tt, tn), lambda j, s: (s, j)), hbm],
        out_specs=(sem, sem, sem, hbm, hbm, pl.BlockSpec((8, LANES), lambda j, s: (0, 0))),
        out_shape=(pltpu.SemaphoreType.DMA(()), pltpu.SemaphoreType.DMA(()), pltpu.SemaphoreType.DMA(()), buf, buf,
                   jax.ShapeDtypeStruct((8, LANES), F32)),
        scratch_shapes=[pltpu.VMEM((rows, tn), F32), pltpu.VMEM((N_DEV, slot, tn), BF16),
                        pltpu.SemaphoreType.DMA(())],
        input_output_aliases={n_p + 1: 4},
        compiler_params=pltpu.CompilerParams(dimension_semantics=("arbitrary", "arbitrary"),
                                             vmem_limit_bytes=VMEM_LIMIT,
                                             has_side_effects=pltpu.SideEffectType.DATAFLOW_SIDE_EFFECTING),
    )(*pieces, b, pltpu.with_memory_space_constraint(lax.empty((N_DEV, slot, n), BF16), pltpu.HBM))


def _exchange_start(grad, name):
    def body(g_ref, land_ref, send_sem, recv_sem, local_sem, g_thru, land_thru, token):
        del g_thru, land_thru
        x, y, c = _mesh_place()
        me = 4 * x + 2 * y + c
        pltpu.make_async_copy(g_ref.at[me], land_ref.at[me], local_sem).start()
        for k in range(1, N_DEV):
            peer, pidx = _peer(x, y, c, k)
            _remote(g_ref.at[pidx], land_ref.at[me], send_sem, recv_sem, peer).start()
        token[...] = jnp.zeros_like(token)

    hbm = pl.BlockSpec(memory_space=pltpu.HBM)
    sem = pl.BlockSpec(memory_space=pltpu.SEMAPHORE)
    buf = pltpu.HBM(grad.shape, grad.dtype)
    return pl.pallas_call(
        body, name=name,
        out_shape=(pltpu.SemaphoreType.DMA(()), pltpu.SemaphoreType.DMA(()), pltpu.SemaphoreType.DMA(()), buf, buf,
                   jax.ShapeDtypeStruct((8, LANES), F32)),
        in_specs=(hbm, hbm), out_specs=(sem, sem, sem, hbm, hbm, pl.BlockSpec(memory_space=pltpu.VMEM)),
        input_output_aliases={0: 3, 1: 4},
        compiler_params=pltpu.CompilerParams(has_side_effects=pltpu.SideEffectType.DATAFLOW_SIDE_EFFECTING),
    )(pltpu.with_memory_space_constraint(grad, pltpu.HBM),
      pltpu.with_memory_space_constraint(lax.empty(grad.shape, grad.dtype), pltpu.HBM))


def _exchange_wait(send_sem, recv_sem, local_sem, grad, land, after, name):
    def body(g_ref, land_ref, send_sem, recv_sem, local_sem, *rest):
        x, y, c = _mesh_place()
        peer, _ = _peer(x, y, c, 1)
        others = pl.ds(0, N_DEV - 1)
        seven = _remote(g_ref.at[others], land_ref.at[others], send_sem, recv_sem, peer)
        seven.wait_send()
        seven.wait_recv()
        pltpu.make_async_copy(g_ref.at[0], land_ref.at[0], local_sem).wait()

    hbm = pl.BlockSpec(memory_space=pltpu.HBM)
    sem = pl.BlockSpec(memory_space=pltpu.SEMAPHORE)
    buf = pltpu.HBM(grad.shape, grad.dtype)
    return pl.pallas_call(
        body, name=name, out_shape=(buf, buf),
        in_specs=(hbm, hbm, sem, sem, sem) + (pl.BlockSpec(memory_space=pl.ANY),) * len(after), out_specs=(hbm, hbm),
        input_output_aliases={0: 0, 1: 1},
        compiler_params=pltpu.CompilerParams(has_side_effects=pltpu.SideEffectType.DATAFLOW_SIDE_EFFECTING),
    )(grad, land, send_sem, recv_sem, local_sem, *after)[1]


SMALL_LAYOUT = ((0, 0, 0, 0, D_MODEL), (1, 0, 1, 0, GROUP), (1, 1, 1, GROUP, GROUP), (2, 0, 2, 0, GROUP),
                (3, 0, 2, GROUP, GROUP), (4, 0, 3, 0, D_MODEL), (5, 0, 4, 0, D_MODEL))
LOSS_ROW = 5
N_SMALL = 6


def _place_small(dst, srcs):
    dst[...] = jnp.zeros_like(dst)
    for p, sr, dr, dc, width in SMALL_LAYOUT:
        dst[dr:dr + 1, dc:dc + width] = srcs[p][sr:sr + 1, :]


def _small_pack(grads, loss_part):
    def body(*refs):
        g_in, loss_in, out_ref, pack = refs[:N_SMALL], refs[N_SMALL], refs[N_SMALL + 1], refs[N_SMALL + 2]
        _place_small(pack, g_in)
        pack[LOSS_ROW:LOSS_ROW + 1, 0:LANES] = loss_in[...]
        for d in range(N_DEV):
            out_ref[d] = pack[...]

    args = [*grads, loss_part]
    packs = jax.ShapeDtypeStruct((N_DEV, SMALL_ROWS, D_MODEL), F32)
    return pl.pallas_call(
        body, name="small_pack", grid=(1,),
        in_specs=[_whole(a) for a in args], out_specs=_whole(packs), out_shape=packs,
        scratch_shapes=[pltpu.VMEM((SMALL_ROWS, D_MODEL), F32)], compiler_params=_params(1),
    )(*args)


def _small_adamw(land, ws, ms, vs):
    def body(*refs):
        g_ref = refs[0]
        params = [refs[1 + q * N_SMALL:1 + (q + 1) * N_SMALL] for q in range(3)]
        o0 = 1 + 3 * N_SMALL
        outs = [refs[o0 + q * N_SMALL:o0 + (q + 1) * N_SMALL] for q in range(4)]
        loss_out = refs[o0 + 4 * N_SMALL]
        wp, mp, vp = refs[o0 + 4 * N_SMALL + 1:]
        for dst, srcs in zip((wp, mp, vp), params):
            _place_small(dst, srcs)
        g = g_ref[0]
        for d in range(1, N_DEV):
            g = g + g_ref[d]
        delta, nm, nv = _adam(wp[...], g, mp[...], vp[...])
        for val, out in zip((g, delta, nm, nv), outs):
            for p, sr, dr, dc, width in SMALL_LAYOUT:
                out[p][sr:sr + 1, :] = val[dr:dr + 1, dc:dc + width]
        loss_out[...] = g[LOSS_ROW:LOSS_ROW + 1, 0:LANES]

    args = [land, *ws, *ms, *vs]
    shapes = [jax.ShapeDtypeStruct(w.shape, F32) for w in ws] * 4 + [jax.ShapeDtypeStruct((1, LANES), F32)]
    packed = pltpu.VMEM((SMALL_ROWS, D_MODEL), F32)
    outs = pl.pallas_call(
        body, name="small_adamw", grid=(1,),
        in_specs=[_whole(a) for a in args], out_specs=[_whole(s) for s in shapes], out_shape=shapes,
        scratch_shapes=[packed, packed, packed], compiler_params=_params(1),
    )(*args)
    return [outs[q * N_SMALL:(q + 1) * N_SMALL] for q in range(4)], outs[4 * N_SMALL]


def _adam(w, g, m, v):
    m = ADAM_B1 * m + (1.0 - ADAM_B1) * g
    v = ADAM_B2 * v + (1.0 - ADAM_B2) * (g * g)
    m_hat = m / (1.0 - ADAM_B1 ** ADAM_STEP)
    v_hat = v / (1.0 - ADAM_B2 ** ADAM_STEP)
    delta = -ADAM_LR * (m_hat / (jnp.sqrt(v_hat) + ADAM_EPS) + ADAM_WD * w)
    return delta, m, v


def _reduce_adamw(land, w, m, v, name, tr, after=()):
    _, rows, width = land.shape

    def body(land_ref, w_ref, m_ref, v_ref, *rest):
        g_ref, d_ref, nm_ref, nv_ref = rest[len(after):]
        g = land_ref[0].astype(F32)
        for d in range(1, N_DEV):
            g = g + land_ref[d].astype(F32)
        delta, nm, nv = _adam(w_ref[...], g, m_ref[...], v_ref[...])
        g_ref[...] = g
        d_ref[...] = delta
        nm_ref[...] = nm
        nv_ref[...] = nv

    row = pl.BlockSpec((tr, width), lambda i: (i, 0))
    out = jax.ShapeDtypeStruct((rows, width), F32)
    return pl.pallas_call(
        body, name=name, grid=(rows // tr,),
        in_specs=[pl.BlockSpec((N_DEV, tr, width), lambda i: (0, i, 0)), row, row, row]
        + [pl.BlockSpec(memory_space=pl.ANY)] * len(after),
        out_specs=[row, row, row, row], out_shape=[out, out, out, out],
        compiler_params=_params(1),
    )(land, w, m, v, *after)


def kernel(x, mix_norm_g, w_in, lower_bounds, hgrn_norm_g, sb_norm_g, w_out, ffn_norm_g, w_gate, w_up, w_down, final_norm_g, loss_target, m_mix_norm_g, m_w_in, m_lower_bounds, m_hgrn_norm_g, m_sb_norm_g, m_w_out, m_ffn_norm_g, m_w_gate, m_w_up, m_w_down, m_final_norm_g, v_mix_norm_g, v_w_in, v_lower_bounds, v_hgrn_norm_g, v_sb_norm_g, v_w_out, v_ffn_norm_g, v_w_gate, v_w_up, v_w_down, v_final_norm_g):
    n_seq, seq, d = x.shape
    t = n_seq * seq
    x2d = x.reshape(t, d)
    tgt = loss_target.reshape(t, d)
    final_g = final_norm_g.reshape(1, d)
    col_sharded = (True, False, True, True, False)

    def as_rows(ws):
        return [w[0].T if tr else w[0] for w, tr in zip(ws, col_sharded)]

    big_w = as_rows([w_in, w_out, w_gate, w_up, w_down])
    big_m = as_rows([m_w_in, m_w_out, m_w_gate, m_w_up, m_w_down])
    big_v = as_rows([v_w_in, v_w_out, v_w_gate, v_w_up, v_w_down])

    sh_in, sh_out, sh_gate, sh_up, sh_down = _cast_shards(big_w)
    wt_in, h1 = _gather_w_in(sh_in, x2d, mix_norm_g)
    wt_in = wt_in.reshape(IN_COLS, d)
    proj_h = _mm_nt(h1, wt_in, "proj_hgrn", rows=(0, 4 * GROUP), tm=1024, tk=4 * GROUP)
    proj_s = _mm_nt(h1, wt_in, "proj_sb", rows=(4 * GROUP, 3 * GROUP), out_dtype=MXU_DTYPE, tm=1024, tk=512)
    mixed, oa_pre, states = _hgrn_fwd(proj_h, lower_bounds, hgrn_norm_g, n_seq, seq)
    mixed, ob_pre, ctot, gathered = _sb_fwd(proj_s, sb_norm_g, mixed, [sh_out, sh_gate, sh_up, sh_down], n_seq, seq)
    wf_out = gathered[0].reshape(d, d)
    wt_gate = gathered[1].reshape(D_FF, d)
    wt_up = gathered[2].reshape(D_FF, d)
    wf_down = gathered[3].reshape(D_FF, d)
    x1, h2 = _mix_out_norm(mixed, wf_out, x2d, ffn_norm_g, "mix_out")
    gate, up, ff = _ffn_up(h2, wt_gate, wt_up, "ffn_up")
    dx2, dx2m, d_final_g, loss_part = _ffn_down_loss(ff, wf_down, x1, tgt, final_g, "ffn_down_loss")

    dgate, dup = _ffn_bwd_act(dx2m, wf_down, gate, up, "ffn_bwd_act")
    dw_down = _mm_tn(ff, dx2m, "dw_down", tk=1408, tn=1024).reshape(N_DEV, D_FF // N_DEV, d)
    dw_gate = _mm_tn(dgate, h2, "dw_gate", tk=1408, tn=1024).reshape(N_DEV, D_FF // N_DEV, d)
    dw_up = _mm_tn(dup, h2, "dw_up", tk=1408, tn=1024).reshape(N_DEV, D_FF // N_DEV, d)
    dx1, d_ffn_g, dmix = _dh_norm_bwd([(dgate, wt_gate, None), (dup, wt_up, None)], x1, ffn_norm_g, dx2, "dh_ffn",
                                      tm=256, next_w=wf_out)
    dw_out = _mm_tn(mixed, dx1, "dw_out", tk=1024, tn=1024).reshape(N_DEV, d // N_DEV, d)
    dsq, dsk, dsv, d_sb_g, lands = _sb_bwd(proj_s, sb_norm_g, dmix, ob_pre, ctot, [dw_out, dw_gate, dw_up, dw_down],
                                            n_seq, seq)
    dhq, dhf, dhi, dhg, d_lb, d_hgrn_g = _hgrn_bwd(proj_h, lower_bounds, hgrn_norm_g, dmix, oa_pre, states, n_seq,
                                                   seq)
    dproj = [dhq, dhf, dhi, dhg, dsq, dsk, dsv]
    send_sem, recv_sem, local_sem, dw_in, land_in, token = _dw_exchange_start(dproj, h1, "dw_in")
    dx, d_mix_g = _dh_norm_bwd([(piece, wt_in, k) for k, piece in enumerate(dproj)], x2d, mix_norm_g, dx1, "dh_mix",
                               after=(token,))

    tiles = {"in": 224, "out": 128, "gate": 176, "up": 176, "down": 176}
    keys = list(tiles)
    packs = _small_pack([d_mix_g, d_lb, d_hgrn_g, d_sb_g, d_ffn_g, d_final_g], loss_part)
    small_sems = _exchange_start(packs, "small_send")
    rest = [_reduce_adamw(land, w, m, v, "adamw_" + key, tr=tiles[key], after=(token, small_sems[-1]))
            for key, land, w, m, v in zip(keys[1:], lands, big_w[1:], big_m[1:], big_v[1:])]
    small_land = _exchange_wait(*small_sems[:-1], [res[0] for res in rest], "small_await")
    small, loss_row = _small_adamw(
        small_land, [mix_norm_g, lower_bounds, hgrn_norm_g, sb_norm_g, ffn_norm_g, final_g],
        [m_mix_norm_g, m_lower_bounds, m_hgrn_norm_g, m_sb_norm_g, m_ffn_norm_g, m_final_norm_g.reshape(1, d)],
        [v_mix_norm_g, v_lower_bounds, v_hgrn_norm_g, v_sb_norm_g, v_ffn_norm_g, v_final_norm_g.reshape(1, d)])
    land_in = _exchange_wait(send_sem, recv_sem, local_sem, dw_in, land_in,
                             [dx, loss_row] + [res[0] for res in rest], "dw_in_await")
    big = [_reduce_adamw(land_in, big_w[0], big_m[0], big_v[0], "adamw_in", tr=tiles["in"])] + rest
    big = [[r.T if tr else r for r in res] for res, tr in zip(big, col_sharded)]

    outs = [loss_row[0, 0], dx.reshape(n_seq, seq, d)]
    for q in range(4):
        b_in, b_out, b_gate, b_up, b_down = [res[q][None] for res in big]
        s_mix, s_lb, s_hgrn, s_sb, s_ffn, s_final = small[q]
        outs += [s_mix, b_in, s_lb, s_hgrn, s_sb, b_out, s_ffn, b_gate, b_up, b_down, s_final.reshape(d)]
    return tuple(outs)
```

```python
import functools
import math

import jax
import jax.numpy as jnp
from jax import lax
from jax.experimental import pallas as pl
from jax.experimental.pallas import tpu as pltpu

F32 = jnp.float32
BF16 = jnp.bfloat16
MXU_DTYPE = BF16

EPS = 1e-6
D_MODEL = 1024
N_HEADS = 8
D_HEAD = 64
GROUP = N_HEADS * D_HEAD
IN_COLS = 7 * GROUP
D_FF = 2816
CHUNK = 64
LANES = 128
N_PAIRS = GROUP // LANES
SUPER = 256
SB_BLOCK = 256
N_DEV = 8

ADAM_LR = 0.001
ADAM_B1 = 0.9
ADAM_B2 = 0.999
ADAM_EPS = 1e-08
ADAM_WD = 0.01
ADAM_STEP = 10

SMALL_ROWS = 8
FF_TILE = D_FF // 2

VMEM_LIMIT = 48 * 1024 * 1024


def _params(n_axes, vmem=VMEM_LIMIT):
    return pltpu.CompilerParams(dimension_semantics=("arbitrary",) * n_axes, vmem_limit_bytes=vmem)


def _whole(a):
    return pl.BlockSpec(a.shape, functools.partial(lambda i, nd: (0,) * nd, nd=len(a.shape)))


def _dot(a, b):
    return jnp.dot(a.astype(MXU_DTYPE), b.astype(MXU_DTYPE), preferred_element_type=F32)


def _dot_nt(a, b):
    return lax.dot_general(a.astype(MXU_DTYPE), b.astype(MXU_DTYPE), (((1,), (1,)), ((), ())),
                           preferred_element_type=F32)


def _dot_tn(a, b):
    return lax.dot_general(a.astype(MXU_DTYPE), b.astype(MXU_DTYPE), (((0,), (0,)), ((), ())),
                           preferred_element_type=F32)


def _split(x, parts):
    out, r = [], x
    for _ in range(parts):
        h = r.astype(BF16)
        out.append(h)
        r = r - h.astype(F32)
    return out


def _rsum_left(u, x, parts):
    acc = None
    for h in _split(x, parts):
        d = jnp.dot(u, h, preferred_element_type=F32)
        acc = d if acc is None else acc + d
    return acc


def _ones_where(mask):
    return jnp.where(mask, 1.0, 0.0).astype(BF16)


def _sigmoid(x):
    return 1.0 / (1.0 + jnp.exp(-x))


def _softplus(x):
    return jnp.maximum(x, 0.0) + jnp.log(1.0 + jnp.exp(-jnp.abs(x)))


def _head_masks():
    lane = lax.broadcasted_iota(jnp.int32, (1, LANES), 1)
    return [jnp.where(lane < D_HEAD, 1.0, 0.0), jnp.where(lane >= D_HEAD, 1.0, 0.0)]


def _head_rstd(o, masks):
    sq = o * o
    r = None
    for m in masks:
        ms = jnp.sum(sq * m, axis=1, keepdims=True) * (1.0 / D_HEAD)
        t = lax.rsqrt(ms + EPS) * m
        r = t if r is None else r + t
    return r


def _head_mean(t, masks):
    out = None
    for m in masks:
        v = jnp.sum(t * m, axis=1, keepdims=True) * (1.0 / D_HEAD) * m
        out = v if out is None else out + v
    return out


def _mix_out_norm(a, w, res, g, name, tm=512):
    m, k = a.shape
    d = w.shape[1]
    tm = min(tm, m)

    def body(a_ref, w_ref, res_ref, g_ref, x_ref, h_ref):
        xv = res_ref[...] + _dot(a_ref[...], w_ref[...])
        x_ref[...] = xv
        r = lax.rsqrt(jnp.mean(xv * xv, axis=-1, keepdims=True) + EPS)
        h_ref[...] = (xv * r * g_ref[...]).astype(h_ref.dtype)

    row = pl.BlockSpec((tm, d), lambda i: (i, 0))
    return pl.pallas_call(
        body, name=name, grid=(m // tm,),
        in_specs=[pl.BlockSpec((tm, k), lambda i: (i, 0)), pl.BlockSpec((k, d), lambda i: (0, 0)), row,
                  pl.BlockSpec((1, d), lambda i: (0, 0))],
        out_specs=[row, row],
        out_shape=[jax.ShapeDtypeStruct((m, d), F32), jax.ShapeDtypeStruct((m, d), MXU_DTYPE)],
        compiler_params=_params(1),
    )(a, w, res, g)


def _ffn_down_loss(a, w, res, target, g, name, tm=512):
    m, k = a.shape
    d = w.shape[1]
    tm = min(tm, m)

    def body(a_ref, w_ref, res_ref, t_ref, g_ref, dx_ref, dxm_ref, dg_ref, loss_ref):
        xv = res_ref[...] + _dot(a_ref[...], w_ref[...])
        gv = g_ref[...]
        r = lax.rsqrt(jnp.mean(xv * xv, axis=-1, keepdims=True) + EPS)
        xh = xv * r
        e = xh * gv - t_ref[...]
        dy = e * (1.0 / d)
        dxh = dy * gv
        dxv = r * (dxh - xh * jnp.mean(dxh * xh, axis=-1, keepdims=True))
        dx_ref[...] = dxv
        dxm_ref[...] = dxv.astype(dxm_ref.dtype)

        @pl.when(pl.program_id(0) == 0)
        def _():
            dg_ref[...] = jnp.zeros_like(dg_ref)
            loss_ref[...] = jnp.zeros_like(loss_ref)

        dg_ref[...] += jnp.sum(dy * xh, axis=0, keepdims=True)
        part = 0.5 * jnp.sum(jnp.mean(e * e, axis=-1, keepdims=True), axis=0, keepdims=True)
        loss_ref[...] += jnp.broadcast_to(part, loss_ref.shape)

    row = pl.BlockSpec((tm, d), lambda i: (i, 0))
    vec = pl.BlockSpec((1, d), lambda i: (0, 0))
    return pl.pallas_call(
        body, name=name, grid=(m // tm,),
        in_specs=[pl.BlockSpec((tm, k), lambda i: (i, 0)), pl.BlockSpec((k, d), lambda i: (0, 0)), row, row, vec],
        out_specs=[row, row, vec, pl.BlockSpec((1, LANES), lambda i: (0, 0))],
        out_shape=[jax.ShapeDtypeStruct((m, d), F32), jax.ShapeDtypeStruct((m, d), MXU_DTYPE),
                   jax.ShapeDtypeStruct((1, d), F32), jax.ShapeDtypeStruct((1, LANES), F32)],
        compiler_params=_params(1),
    )(a, w, res, target, g)


def _mm_nt(a, w, name, rows=None, out_dtype=F32, tm=512, tk=512):
    m, n = a.shape
    row0, k = rows or (0, w.shape[0])
    tm, tk = min(tm, m), min(tk, k)
    first = row0 // tk

    def body(a_ref, w_ref, o_ref):
        o_ref[...] = _dot_nt(a_ref[...], w_ref[...]).astype(o_ref.dtype)

    return pl.pallas_call(
        body, name=name, grid=(k // tk, m // tm),
        in_specs=[pl.BlockSpec((tm, n), lambda j, i: (i, 0)), pl.BlockSpec((tk, n), lambda j, i: (first + j, 0))],
        out_specs=pl.BlockSpec((tm, tk), lambda j, i: (i, j)),
        out_shape=jax.ShapeDtypeStruct((m, k), out_dtype),
        compiler_params=_params(2),
    )(a, w)


def _mm_nt_split(a, w, name, split, dtypes, tm=1024):
    m, n = a.shape
    k = w.shape[0]

    def body(a_ref, w_ref, lo_ref, hi_ref):
        av = a_ref[...]
        lo_ref[...] = _dot_nt(av, w_ref[0:split, :]).astype(lo_ref.dtype)
        hi_ref[...] = _dot_nt(av, w_ref[split:k, :]).astype(hi_ref.dtype)

    return pl.pallas_call(
        body, name=name, grid=(m // tm,),
        in_specs=[pl.BlockSpec((tm, n), lambda i: (i, 0)),
                  pl.BlockSpec((k, n), lambda i: (0, 0), pipeline_mode=pl.Buffered(1))],
        out_specs=[pl.BlockSpec((tm, split), lambda i: (i, 0)), pl.BlockSpec((tm, k - split), lambda i: (i, 0))],
        out_shape=[jax.ShapeDtypeStruct((m, split), dtypes[0]), jax.ShapeDtypeStruct((m, k - split), dtypes[1])],
        compiler_params=_params(1),
    )(a, w)


def _mm_tn(a, b, name, tk, tn, tt=1024, out_dtype=BF16):
    t, k = a.shape
    n = b.shape[1]
    tt = min(tt, t)
    steps = t // tt

    def body(a_ref, b_ref, o_ref, acc):
        s = pl.program_id(2)

        @pl.when(s == 0)
        def _():
            acc[...] = jnp.zeros_like(acc)

        acc[...] += _dot_tn(a_ref[...], b_ref[...])

        @pl.when(s == steps - 1)
        def _():
            o_ref[...] = acc[...].astype(o_ref.dtype)

    return pl.pallas_call(
        body, name=name, grid=(k // tk, n // tn, steps),
        in_specs=[pl.BlockSpec((tt, tk), lambda i, j, s: (s, i)), pl.BlockSpec((tt, tn), lambda i, j, s: (s, j))],
        out_specs=pl.BlockSpec((tk, tn), lambda i, j, s: (i, j)),
        out_shape=jax.ShapeDtypeStruct((k, n), out_dtype),
        scratch_shapes=[pltpu.VMEM((tk, tn), F32)],
        compiler_params=_params(3),
    )(a, b)


def _ffn_up(h, wg_t, wu_t, name, tm=1024, tn=FF_TILE):
    m, k = h.shape
    n = wg_t.shape[0]
    tm = min(tm, m)

    def body(h_ref, wg_ref, wu_ref, gate_ref, up_ref, ff_ref):
        hv = h_ref[...]
        gate = _dot_nt(hv, wg_ref[...])
        up = _dot_nt(hv, wu_ref[...])
        gate_ref[...] = gate.astype(gate_ref.dtype)
        up_ref[...] = up.astype(up_ref.dtype)
        ff_ref[...] = (gate * _sigmoid(gate) * up).astype(ff_ref.dtype)

    wspec = pl.BlockSpec((tn, k), lambda j, i: (j, 0))
    ospec = pl.BlockSpec((tm, tn), lambda j, i: (i, j))
    return pl.pallas_call(
        body, name=name, grid=(n // tn, m // tm),
        in_specs=[pl.BlockSpec((tm, k), lambda j, i: (i, 0)), wspec, wspec],
        out_specs=[ospec, ospec, ospec],
        out_shape=[jax.ShapeDtypeStruct((m, n), MXU_DTYPE)] * 3,
        compiler_params=_params(2),
    )(h, wg_t, wu_t)


def _ffn_bwd_act(dx, wd, gate, up, name, tm=1024, tn=FF_TILE):
    m, k = dx.shape
    n = wd.shape[0]
    tm = min(tm, m)

    def body(dx_ref, wd_ref, gate_ref, up_ref, dgate_ref, dup_ref):
        dff = _dot_nt(dx_ref[...], wd_ref[...])
        gate = gate_ref[...].astype(F32)
        sg = _sigmoid(gate)
        dgate_ref[...] = (dff * up_ref[...].astype(F32) * sg * (1.0 + gate * (1.0 - sg))).astype(dgate_ref.dtype)
        dup_ref[...] = (dff * gate * sg).astype(dup_ref.dtype)

    ospec = pl.BlockSpec((tm, tn), lambda j, i: (i, j))
    return pl.pallas_call(
        body, name=name, grid=(n // tn, m // tm),
        in_specs=[pl.BlockSpec((tm, k), lambda j, i: (i, 0)), pl.BlockSpec((tn, k), lambda j, i: (j, 0)),
                  ospec, ospec],
        out_specs=[ospec, ospec],
        out_shape=[jax.ShapeDtypeStruct((m, n), MXU_DTYPE), jax.ShapeDtypeStruct((m, n), MXU_DTYPE)],
        compiler_params=_params(2),
    )(dx, wd, gate, up)


def _chunk_masks():
    r = lax.broadcasted_iota(jnp.int32, (SUPER, SUPER), 0)
    c = lax.broadcasted_iota(jnp.int32, (SUPER, SUPER), 1)
    same = jnp.right_shift(r, 6) == jnp.right_shift(c, 6)
    lower = jnp.logical_and(same, c <= r)
    upper = jnp.logical_and(same, c >= r)
    return same, lower, upper


def _head_block_mask():
    r = lax.broadcasted_iota(jnp.int32, (LANES, LANES), 0)
    c = lax.broadcasted_iota(jnp.int32, (LANES, LANES), 1)
    return jnp.where(jnp.right_shift(r, 6) == jnp.right_shift(c, 6), 1.0, 0.0)


def _lower_bound(lb_raw):
    return 1.0 / (1.0 + jnp.exp(lb_raw[1:2, :] - lb_raw[0:1, :]))


HGRN_UNROLL = 4
PER_SUPER = SUPER // CHUNK
CHUNK_ROWS = [slice(c * CHUNK, (c + 1) * CHUNK) for c in range(PER_SUPER)]


def _over_chunks(rows):
    return jnp.concatenate([jnp.broadcast_to(r, (CHUNK, LANES)) for r in rows], axis=0)


def _hgrn_gates(q, hf, lb, lower_b):
    sig = _sigmoid(hf)
    f = lb + (1.0 - lb) * sig
    k = 1.0 - f
    lf = jnp.log(f)
    b = _rsum_left(lower_b, lf, 2)
    ends = [b[cr.stop - 1:cr.stop, :] for cr in CHUNK_ROWS]
    eb = jnp.exp(b)
    enb = jnp.exp(-b)
    edb = jnp.exp(_over_chunks(ends) - b)
    decs = [jnp.exp(e) for e in ends]
    return sig, f, k, decs, eb, enb, edb, q * eb, k * enb, k * edb


def _hgrn_fwd(proj, lower_bounds, norm_g, n_seq, seq):
    t = n_seq * seq
    n_super = seq // SUPER
    n_chunks = seq // CHUNK

    def body(q_ref, f_ref, i_ref, g_ref, lb_ref, ng_ref, out_ref, opre_ref, st_ref):
        masks = _head_masks()
        _, lower, _ = _chunk_masks()
        lower_b = _ones_where(lower)
        bd = _head_block_mask()
        lb = _lower_bound(lb_ref[...])
        ng = ng_ref[...]

        def step(it, st):
            blocks = [HGRN_UNROLL * it + u for u in range(HGRN_UNROLL)]
            rows = [pl.ds(pl.multiple_of(sb * SUPER, SUPER), SUPER) for sb in blocks]
            vs = [i_ref[rw, :] for rw in rows]
            gates = [_hgrn_gates(q_ref[rw, :], f_ref[rw, :], lb, lower_b) for rw in rows]
            decs, qes, kes, kds = ([g[k] for g in gates] for k in (3, 7, 8, 9))
            scores = [[_dot_nt(qe * m, ke) for m in masks] for qe, ke in zip(qes, kes)]
            updates = [[_dot_tn(v[cr], kd[cr]) for cr in CHUNK_ROWS] for v, kd in zip(vs, kds)]
            states = [st]
            for dec_b, upd_b in zip(decs, updates):
                for dec, upd in zip(dec_b, upd_b):
                    states.append(states[-1] * dec + bd * upd)
            for u, sb in enumerate(blocks):
                for c in range(PER_SUPER):
                    st_ref[0, 0, sb * PER_SUPER + c] = states[u * PER_SUPER + c]
            intra = [[_dot(jnp.where(lower, p, 0.0), v) for p in sc] for sc, v in zip(scores, vs)]
            inter = [[_dot_nt(qe[cr], states[u * PER_SUPER + c]) for c, cr in enumerate(CHUNK_ROWS)]
                     for u, qe in enumerate(qes)]
            for rw, intra_b, inter_b in zip(rows, intra, inter):
                o = intra_b[0] * masks[0] + intra_b[1] * masks[1] + jnp.concatenate(inter_b, axis=0)
                opre_ref[rw, :] = o
                hg = g_ref[rw, :]
                on = o * _head_rstd(o, masks) * ng
                out_ref[rw, :] = (on * hg * _sigmoid(hg)).astype(out_ref.dtype)
            return states[-1]

        lax.fori_loop(0, n_super // HGRN_UNROLL, step, jnp.zeros((LANES, LANES), F32))

    def col(k):
        return pl.BlockSpec((seq, LANES), lambda p, b: (b, k * N_PAIRS + p))

    vec = lambda rows: pl.BlockSpec((rows, LANES), lambda p, b: (0, p))
    ospec = pl.BlockSpec((seq, LANES), lambda p, b: (b, p))
    return pl.pallas_call(
        body, name="hgrn_fwd", grid=(N_PAIRS, n_seq),
        in_specs=[col(0), col(1), col(2), col(3), vec(2), vec(1)],
        out_specs=[ospec, ospec,
                   pl.BlockSpec((1, 1, n_chunks, LANES, LANES), lambda p, b: (b, p, 0, 0, 0))],
        out_shape=[jax.ShapeDtypeStruct((t, 2 * GROUP), MXU_DTYPE), jax.ShapeDtypeStruct((t, GROUP), F32),
                   jax.ShapeDtypeStruct((n_seq, N_PAIRS, n_chunks, LANES, LANES), F32)],
        compiler_params=_params(2),
    )(proj, proj, proj, proj, lower_bounds, norm_g)


def _hgrn_bwd(proj, lower_bounds, norm_g, dmix, opre, states, n_seq, seq):
    t = n_seq * seq
    n_super = seq // SUPER
    n_chunks = seq // CHUNK
    per = SUPER // CHUNK

    def body(q_ref, f_ref, i_ref, g_ref, lb_ref, ng_ref, dm_ref, opre_ref, st_ref,
             dq_ref, df_ref, di_ref, dg_ref, dlb_ref, dng_ref):
        masks = _head_masks()
        _, lower, upper = _chunk_masks()
        lower_b, upper_b = _ones_where(lower), _ones_where(upper)
        bd = _head_block_mask()
        lb_raw = lb_ref[...]
        lb = _lower_bound(lb_raw)
        ng = ng_ref[...]

        @pl.when(pl.program_id(1) == 0)
        def _():
            dlb_ref[...] = jnp.zeros_like(dlb_ref)
            dng_ref[...] = jnp.zeros_like(dng_ref)

        def first_half(sb):
            rows = pl.ds(pl.multiple_of(sb * SUPER, SUPER), SUPER)
            q, hf, v, hg = q_ref[rows, :], f_ref[rows, :], i_ref[rows, :], g_ref[rows, :]
            sig, f, k, decs, eb, enb, edb, qe, ke, kd = _hgrn_gates(q, hf, lb, lower_b)
            o = opre_ref[rows, :]
            r = _head_rstd(o, masks)
            oh = o * r
            dm = dm_ref[rows, :]
            sg = _sigmoid(hg)
            dg_ref[rows, :] = (dm * oh * ng * sg * (1.0 + hg * (1.0 - sg))).astype(dg_ref.dtype)
            don = dm * hg * sg
            dng_ref[...] += jnp.sum(don * oh, axis=0, keepdims=True)
            doh = don * ng
            do = r * (doh - oh * _head_mean(doh * oh, masks))
            doms = [do * m for m in masks]
            qems = [qe * m for m in masks]
            scores = [_dot_nt(qem, ke) for qem in qems]
            dscores = [_dot_nt(dom, v) for dom in doms]
            prevs = [st_ref[0, 0, sb * per + c] for c in range(per)]
            dst_in = [_dot_tn(do[cr], qe[cr]) for cr in CHUNK_ROWS]
            dqe_i = [_dot(do[cr], prev) for cr, prev in zip(CHUNK_ROWS, prevs)]
            return dict(rows=rows, v=v, sig=sig, f=f, decs=decs, eb=eb, enb=enb, edb=edb, qe=qe, ke=ke, kd=kd,
                        doms=doms, qems=qems, scores=scores, dscores=dscores, prevs=prevs, dst_in=dst_in, dqe_i=dqe_i)

        def second_half(blk, dsts):
            v, qe, ke, kd = blk["v"], blk["qe"], blk["ke"], blk["kd"]
            ps = [jnp.where(lower, p, 0.0) for p in blk["scores"]]
            dps = [jnp.where(lower, dp, 0.0) for dp in blk["dscores"]]
            dqe_h = [_dot(dp, ke) for dp in dps]
            dke_h = [_dot_tn(dp, qem) for dp, qem in zip(dps, blk["qems"])]
            dv_h = [_dot_tn(p, dom) for p, dom in zip(ps, blk["doms"])]
            dus = [bd * d for d in dsts]
            dv_i = [_dot_nt(kd[cr], du) for cr, du in zip(CHUNK_ROWS, dus)]
            dkd_i = [_dot(v[cr], du) for cr, du in zip(CHUNK_ROWS, dus)]

            def finish():
                dqe = dqe_h[0] * masks[0] + dqe_h[1] * masks[1] + jnp.concatenate(blk["dqe_i"], axis=0)
                dke = dke_h[0] + dke_h[1]
                dv = dv_h[0] + dv_h[1] + jnp.concatenate(dv_i, axis=0)
                dkd = jnp.concatenate(dkd_i, axis=0)
                dk = dke * blk["enb"] + dkd * blk["edb"]
                db = dqe * qe - dke * ke - dkd * kd
                dkd_kd = dkd * kd
                dends = [jnp.sum(dkd_kd[cr], axis=0, keepdims=True)
                         + jnp.sum(dsts[c] * blk["prevs"][c], axis=0, keepdims=True) * blk["decs"][c]
                         for c, cr in enumerate(CHUNK_ROWS)]
                dlf = _rsum_left(upper_b, db, 2) + _over_chunks(dends)
                sig = blk["sig"]
                dfv = dlf / blk["f"] - dk
                rows = blk["rows"]
                dq_ref[rows, :] = (dqe * blk["eb"]).astype(dq_ref.dtype)
                di_ref[rows, :] = dv.astype(di_ref.dtype)
                df_ref[rows, :] = (dfv * (1.0 - lb) * sig * (1.0 - sig)).astype(df_ref.dtype)
                dlb = jnp.sum(dfv * (1.0 - sig), axis=0, keepdims=True)
                da0 = dlb * lb * (1.0 - lb)
                dlb_ref[0:1, :] += da0
                dlb_ref[1:2, :] -= da0

            return finish

        def step(it, dst):
            blocks = [first_half(n_super - 1 - HGRN_UNROLL * it - u) for u in range(HGRN_UNROLL)]
            all_dsts = []
            for blk in blocks:
                dsts = [None] * per
                for c in reversed(range(per)):
                    dsts[c] = dst
                    dst = bd * (dst * blk["decs"][c] + blk["dst_in"][c])
                all_dsts.append(dsts)
            for finish in [second_half(blk, dsts) for blk, dsts in zip(blocks, all_dsts)]:
                finish()
            return dst

        lax.fori_loop(0, n_super // HGRN_UNROLL, step, jnp.zeros((LANES, LANES), F32))

    def col(k):
        return pl.BlockSpec((seq, LANES), lambda p, b: (b, k * N_PAIRS + p))

    vec = lambda rows: pl.BlockSpec((rows, LANES), lambda p, b: (0, p))
    ospec = pl.BlockSpec((seq, LANES), lambda p, b: (b, p))
    piece = jax.ShapeDtypeStruct((t, GROUP), MXU_DTYPE)
    return pl.pallas_call(
        body, name="hgrn_bwd", grid=(N_PAIRS, n_seq),
        in_specs=[col(0), col(1), col(2), col(3), vec(2), vec(1), ospec, ospec,
                  pl.BlockSpec((1, 1, n_chunks, LANES, LANES), lambda p, b: (b, p, 0, 0, 0))],
        out_specs=[ospec, ospec, ospec, ospec, vec(2), vec(1)],
        out_shape=[piece, piece, piece, piece,
                   jax.ShapeDtypeStruct((2, GROUP), F32), jax.ShapeDtypeStruct((1, GROUP), F32)],
        compiler_params=_params(2),
    )(proj, proj, proj, proj, lower_bounds, norm_g, dmix, opre, states)


SB_SCALE = 1.0 / math.sqrt(D_HEAD)


SB_STEP = 2 * SB_BLOCK
QUERY_BLOCKS = (slice(0, SB_BLOCK), slice(SB_BLOCK, SB_STEP))


def _triangle(keep):
    row = lax.broadcasted_iota(jnp.int32, (SB_BLOCK, SB_BLOCK), 0)
    col = lax.broadcasted_iota(jnp.int32, (SB_BLOCK, SB_BLOCK), 1)
    return _ones_where(keep(row, col))


SB_HALF = SB_BLOCK // 2


def _keep(on_diagonal, x):
    if not on_diagonal:
        return x
    row = lax.broadcasted_iota(jnp.int32, (SB_HALF, SB_HALF), 0)
    col = lax.broadcasted_iota(jnp.int32, (SB_HALF, SB_HALF), 1)
    return jnp.where(col < row, x, 0.0)


def _lower_quadrants(fn, n_out, *tiles):
    def quadrant(r, c):
        return [t[r * SB_HALF:(r + 1) * SB_HALF, c * SB_HALF:(c + 1) * SB_HALF] for t in tiles]

    top, left, bottom = fn(True, *quadrant(0, 0)), fn(False, *quadrant(1, 0)), fn(True, *quadrant(1, 1))
    zero = jnp.zeros((SB_HALF, SB_HALF), F32)
    return [jnp.concatenate([jnp.concatenate([top[o], zero], axis=1), jnp.concatenate([left[o], bottom[o]], axis=1)],
                            axis=0) for o in range(n_out)]


def _sb_fwd(proj, norm_g, mixed, shards, n_seq, seq):
    t = n_seq * seq
    nq = seq // SB_STEP
    q0, k0, v0 = 0, N_PAIRS, 2 * N_PAIRS
    n_w = len(shards)
    n_steps = N_PAIRS * n_seq * nq
    tri = _triangle(lambda row, col: row >= col)

    def body(q_ref, k_ref, v_ref, ng_ref, tri_ref, mixed_in, *rest):
        del mixed_in
        shard_refs = rest[:n_w]
        out_ref, opre_ref, ctot_ref = rest[n_w:n_w + 3]
        gathered = rest[n_w + 3:2 * n_w + 3]
        send_sems, recv_sems, local_sems = rest[2 * n_w + 3:]
        i = pl.program_id(2)
        step = (pl.program_id(0) * n_seq + pl.program_id(1)) * nq + i
        plan = _GatherPlan(shard_refs, gathered, send_sems, recv_sems, local_sems)

        @pl.when(step == 0)
        def _():
            plan.start()

        @pl.when(step == (3 * n_steps) // 4)
        def _():
            plan.forward()

        masks = _head_masks()
        suffix = tri_ref[...]
        qhs = [[(q_ref[blk, :] * SB_SCALE * m).astype(MXU_DTYPE) for m in masks] for blk in QUERY_BLOCKS]

        def tiles(js, work, carry):
            rows = [pl.ds(pl.multiple_of(j * SB_BLOCK, SB_BLOCK), SB_BLOCK) for j in js]
            ks = [k_ref[rw, :].astype(MXU_DTYPE) for rw in rows]
            vs = [v_ref[rw, :].astype(MXU_DTYPE) for rw in rows]
            zs = [[_dot_nt(qh, ks[ts]) for qh in qhs[qb]] for qb, ts, _ in work]
            ccs = [[_lower_quadrants(lambda on, zq: (_keep(on, _softplus(zq)),), 1, z)[0] if diag else _softplus(z)
                    for z in zw] for zw, (_, _, diag) in zip(zs, work)]
            sums = [[jnp.dot(cc.astype(BF16), suffix, preferred_element_type=F32) for cc in cw] for cw in ccs]
            out = [list(per_block) for per_block in carry]
            for h in range(len(masks)):
                for w, (qb, ts, diag) in enumerate(work):
                    run, acc = out[qb][h]
                    logit = zs[w][h] - (sums[w][h] + run)
                    if diag:
                        a = _lower_quadrants(lambda on, lq: (_keep(on, jnp.exp(lq)),), 1, logit)[0]
                    else:
                        a = jnp.exp(logit)
                    out[qb][h] = (run + sums[w][h][:, 0:1], acc + _dot(a, vs[ts]))
            return tuple(tuple(per_block) for per_block in out)

        start = ((jnp.zeros((SB_BLOCK, 1), F32), jnp.zeros((SB_BLOCK, LANES), F32)),) * 2
        carry = tiles([2 * i, 2 * i + 1], [(0, 0, True), (1, 1, True), (1, 0, False)], (start, start))
        both = [(0, 0, False), (0, 1, False), (1, 0, False), (1, 1, False)]
        carry = lax.fori_loop(0, i, lambda s, cy: tiles([2 * (i - s) - 1, 2 * (i - s) - 2], both, cy), carry)
        opre = jnp.concatenate([cb[0][1] * masks[0] + cb[1][1] * masks[1] for cb in carry], axis=0)
        ctot = jnp.concatenate([cb[0][0] * masks[0] + cb[1][0] * masks[1] for cb in carry], axis=0)
        opre_ref[...] = opre
        ctot_ref[...] = ctot
        out_ref[...] = (opre * _head_rstd(opre, masks) * ng_ref[...]).astype(out_ref.dtype)

        @pl.when(step == n_steps - 1)
        def _():
            plan.finish()

    qspec = pl.BlockSpec((SB_STEP, LANES), lambda p, b, i: (b * nq + i, q0 + p))
    ospec = pl.BlockSpec((SB_STEP, LANES), lambda p, b, i: (b * nq + i, p))
    hbm = pl.BlockSpec(memory_space=pltpu.HBM)
    outs = pl.pallas_call(
        body, name="sb_fwd", grid=(N_PAIRS, n_seq, nq),
        in_specs=[qspec,
                  pl.BlockSpec((seq, LANES), lambda p, b, i: (b, k0 + p)),
                  pl.BlockSpec((seq, LANES), lambda p, b, i: (b, v0 + p)),
                  pl.BlockSpec((1, LANES), lambda p, b, i: (0, p)),
                  pl.BlockSpec(tri.shape, lambda p, b, i: (0, 0)), hbm] + [hbm] * n_w,
        out_specs=[pl.BlockSpec((SB_STEP, LANES), lambda p, b, i: (b * nq + i, N_PAIRS + p)), ospec, ospec]
        + [hbm] * n_w,
        out_shape=[jax.ShapeDtypeStruct(mixed.shape, mixed.dtype), jax.ShapeDtypeStruct((t, GROUP), F32),
                   jax.ShapeDtypeStruct((t, GROUP), F32)]
        + [jax.ShapeDtypeStruct((N_DEV,) + s.shape, s.dtype) for s in shards],
        scratch_shapes=[pltpu.SemaphoreType.DMA((n_w * _GatherPlan.COPIES,)),
                        pltpu.SemaphoreType.DMA((n_w * _GatherPlan.COPIES,)), pltpu.SemaphoreType.DMA((n_w,))],
        input_output_aliases={5: 0},
        compiler_params=_params(3),
    )(proj, proj, proj, norm_g, tri, mixed, *shards)
    return outs[0], outs[1], outs[2], list(outs[3:])


def _sb_bwd(proj, norm_g, dmix, opre, ctot, grads, n_seq, seq):
    t = n_seq * seq
    nq = seq // SB_STEP
    q0, k0, v0 = 0, N_PAIRS, 2 * N_PAIRS
    n_w = len(grads)
    n_steps = N_PAIRS * n_seq * nq
    tri = _triangle(lambda row, col: row <= col)

    def body(q_ref, k_ref, v_ref, ng_ref, tri_ref, dm_ref, opre_ref, ctot_ref, *rest):
        grad_refs = rest[:n_w]
        dq_ref, dk_ref, dv_ref, dng_ref = rest[n_w:n_w + 4]
        lands = rest[n_w + 4:2 * n_w + 4]
        dk_acc, dv_acc, send_sems, recv_sems, local_sems = rest[2 * n_w + 4:]
        p_id, b_id, i = pl.program_id(0), pl.program_id(1), pl.program_id(2)
        step = (p_id * n_seq + b_id) * nq + i
        plan = _ScatterPlan(grad_refs, lands, send_sems, recv_sems, local_sems)

        @pl.when(step == 0)
        def _():
            plan.start()

        masks = _head_masks()
        upto = tri_ref[...]

        def prefix(x):
            return jnp.dot(x.astype(BF16), upto, preferred_element_type=F32)

        @pl.when(i == 0)
        def _():
            dk_acc[...] = jnp.zeros_like(dk_acc)
            dv_acc[...] = jnp.zeros_like(dv_acc)

        @pl.when(jnp.logical_and(b_id == 0, i == 0))
        def _():
            dng_ref[...] = jnp.zeros_like(dng_ref)

        o = opre_ref[...]
        rs = _head_rstd(o, masks)
        oh = o * rs
        dm = dm_ref[...]
        dng_ref[...] += jnp.sum(dm * oh, axis=0, keepdims=True)
        doh = dm * ng_ref[...]
        do = rs * (doh - oh * _head_mean(doh * oh, masks))

        heads = range(len(masks))
        qs = [q_ref[blk, :] * SB_SCALE for blk in QUERY_BLOCKS]
        dos = [do[blk] for blk in QUERY_BLOCKS]
        qhs = [[(q * m).astype(MXU_DTYPE) for m in masks] for q in qs]
        doms = [[(d * m).astype(MXU_DTYPE) for m in masks] for d in dos]
        head_rows = [jnp.where(jnp.right_shift(lax.broadcasted_iota(jnp.int32, (LANES, 1), 0), 6) == h, 1.0, 0.0)
                     for h in heads]
        qhts = [[(qt * hr).astype(MXU_DTYPE) for hr in head_rows] for qt in [q.astype(F32).T for q in qs]]
        domts = [[(dt * hr).astype(MXU_DTYPE) for hr in head_rows] for dt in [d.T for d in dos]]
        totals = [[ctot_ref[blk, h * D_HEAD:h * D_HEAD + 1] for h in heads] for blk in QUERY_BLOCKS]

        def tiles(js, work, carry):
            rows = [pl.ds(pl.multiple_of(j * SB_BLOCK, SB_BLOCK), SB_BLOCK) for j in js]
            ks = [k_ref[rw, :].astype(MXU_DTYPE) for rw in rows]
            vs = [v_ref[rw, :].astype(MXU_DTYPE) for rw in rows]
            zs = [[_dot_nt(qhs[qb][h], ks[ts]) for h in heads] for qb, ts, _ in work]
            das = [[_dot_nt(doms[qb][h], vs[ts]) for h in heads] for qb, ts, _ in work]
            def gates(on, zq):
                sp = _softplus(zq)
                return _keep(on, sp), zq - sp, _keep(on, jnp.exp(zq - sp))

            trio = [[_lower_quadrants(gates, 3, z) if diag else gates(False, z) for z in zw]
                    for zw, (_, _, diag) in zip(zs, work)]
            ccs, lsigs, sigs = ([[t[o] for t in tw] for tw in trio] for o in range(3))
            pres = [[prefix(cc) for cc in cw] for cw in ccs]
            out = [list(per_block) for per_block in carry]
            for h in heads:
                for w, (qb, ts, diag) in enumerate(work):
                    pc, pdl, dq_h = out[qb][h]
                    logit = lsigs[w][h] + pres[w][h] - (totals[qb][h] - pc)
                    if diag:
                        a = _lower_quadrants(lambda on, lq: (_keep(on, jnp.exp(lq)),), 1, logit)[0]
                    else:
                        a = jnp.exp(logit)
                    dl = a * das[w][h]
                    dv_acc[js[ts]] += _dot(domts[qb][h], a)
                    dpre = prefix(dl)
                    dz = dl - sigs[w][h] * (pdl + dpre)
                    dzb = dz.astype(MXU_DTYPE)
                    dk_acc[js[ts]] += _dot(qhts[qb][h], dzb)
                    out[qb][h] = (pc + pres[w][h][:, SB_BLOCK - 1:SB_BLOCK], pdl + dpre[:, SB_BLOCK - 1:SB_BLOCK],
                                  dq_h + _dot(dzb, ks[ts]))
            return tuple(tuple(per_block) for per_block in out)

        zero = jnp.zeros((SB_BLOCK, 1), F32)
        start = ((zero, zero, jnp.zeros((SB_BLOCK, LANES), F32)),) * 2
        both = [(0, 0, False), (0, 1, False), (1, 0, False), (1, 1, False)]
        carry = lax.fori_loop(0, i, lambda s, cy: tiles([2 * s, 2 * s + 1], both, cy), (start, start))
        carry = tiles([2 * i, 2 * i + 1], [(0, 0, True), (1, 0, False), (1, 1, True)], carry)
        dq = jnp.concatenate([cb[0][2] * masks[0] + cb[1][2] * masks[1] for cb in carry], axis=0)
        dq_ref[...] = (dq * SB_SCALE).astype(dq_ref.dtype)

        @pl.when(i == nq - 1)
        def _():
            for j in range(seq // SB_BLOCK):
                tile_rows = slice(j * SB_BLOCK, (j + 1) * SB_BLOCK)
                dk_ref[tile_rows, :] = dk_acc[j].T.astype(dk_ref.dtype)
                dv_ref[tile_rows, :] = dv_acc[j].T.astype(dv_ref.dtype)

        @pl.when(step == n_steps - 1)
        def _():
            plan.finish()

    qspec = pl.BlockSpec((SB_STEP, LANES), lambda p, b, i: (b * nq + i, q0 + p))
    ospec = pl.BlockSpec((SB_STEP, LANES), lambda p, b, i: (b * nq + i, p))
    dmspec = pl.BlockSpec((SB_STEP, LANES), lambda p, b, i: (b * nq + i, N_PAIRS + p))
    full = lambda k: pl.BlockSpec((seq, LANES), lambda p, b, i: (b, k + p))
    vec = pl.BlockSpec((1, LANES), lambda p, b, i: (0, p))
    hbm = pl.BlockSpec(memory_space=pltpu.HBM)
    piece = jax.ShapeDtypeStruct((t, GROUP), MXU_DTYPE)
    outs = pl.pallas_call(
        body, name="sb_bwd", grid=(N_PAIRS, n_seq, nq),
        in_specs=[qspec, full(k0), full(v0), vec, pl.BlockSpec(tri.shape, lambda p, b, i: (0, 0)), dmspec, ospec, ospec]
        + [hbm] * n_w,
        out_specs=[ospec, full(0), full(0), vec] + [hbm] * n_w,
        out_shape=[piece, piece, piece, jax.ShapeDtypeStruct((1, GROUP), F32)]
        + [jax.ShapeDtypeStruct(g.shape, g.dtype) for g in grads],
        scratch_shapes=[pltpu.VMEM((seq // SB_BLOCK, LANES, SB_BLOCK), F32),
                        pltpu.VMEM((seq // SB_BLOCK, LANES, SB_BLOCK), F32),
                        pltpu.SemaphoreType.DMA((n_w * (N_DEV - 1),)), pltpu.SemaphoreType.DMA((n_w * (N_DEV - 1),)),
                        pltpu.SemaphoreType.DMA((n_w,))],
        compiler_params=_params(3),
    )(proj, proj, proj, norm_g, tri, dmix, opre, ctot, *[pltpu.with_memory_space_constraint(g, pltpu.HBM) for g in grads])
    return outs[0], outs[1], outs[2], outs[3], list(outs[4:])


def _mesh_place():
    x, y, c = lax.axis_index("x"), lax.axis_index("y"), lax.axis_index("c")
    return x, y, c


def _peer(x, y, c, k):
    px = lax.rem(x + ((k >> 2) & 1), 2)
    py = lax.rem(y + ((k >> 1) & 1), 2)
    pc = lax.rem(c + (k & 1), 2)
    return (px, py, pc), 4 * px + 2 * py + pc


def _remote(src, dst, send_sem, recv_sem, to):
    return pltpu.make_async_remote_copy(src_ref=src, dst_ref=dst, send_sem=send_sem, recv_sem=recv_sem,
                                        device_id=to, device_id_type=pl.DeviceIdType.MESH)


class _GatherPlan:
    COPIES = 7

    def __init__(self, shards, gathered, send_sems, recv_sems, local_sems):
        x, y, c = _mesh_place()
        self.c = c
        self.me = (x, y, c)
        self.sibling = (x, y, 1 - c)
        self.chips = [(1 - x, y), (x, 1 - y), (1 - x, 1 - y)]
        self.tensors = list(zip(shards, gathered))
        self.send_sems, self.recv_sems, self.local_sems = send_sems, recv_sems, local_sems

    @staticmethod
    def _index(place):
        return 4 * place[0] + 2 * place[1] + place[2]

    def _copy(self, w, k, block, to, own=False):
        shard, gathered = self.tensors[w]
        slot = gathered.at[self._index(block)]
        n = w * self.COPIES + k
        return _remote(shard if own else slot, slot, self.send_sems.at[n], self.recv_sems.at[n], to)

    def _local(self, w):
        shard, gathered = self.tensors[w]
        return pltpu.make_async_copy(shard, gathered.at[self._index(self.me)], self.local_sems.at[w])

    def _first(self, w):
        return [self._copy(w, 0, self.me, self.sibling, own=True)] + [
            self._copy(w, 1 + j, self.me, (*chip, self.c), own=True) for j, chip in enumerate(self.chips)]

    def _passed(self, w):
        return [self._copy(w, 4 + j, (*chip, self.c), self.sibling) for j, chip in enumerate(self.chips)]

    def start(self):
        for w in range(len(self.tensors)):
            self._local(w).start()
            for cp in self._first(w):
                cp.start()

    def forward(self):
        for w in range(len(self.tensors)):
            passed = self._passed(w)
            for j, chip in enumerate(self.chips):
                self._copy(w, 1 + j, (*chip, self.c), self.me).wait_recv()
                passed[j].start()

    def finish(self):
        for w in range(len(self.tensors)):
            self._copy(w, 0, self.sibling, self.me).wait_recv()
            for j, chip in enumerate(self.chips):
                self._copy(w, 4 + j, (*chip, 1 - self.c), self.me).wait_recv()
            for cp in self._first(w) + self._passed(w):
                cp.wait_send()
            self._local(w).wait()


class _ScatterPlan:
    def __init__(self, grads, lands, send_sems, recv_sems, local_sems):
        self.place = _mesh_place()
        x, y, c = self.place
        self.me = 4 * x + 2 * y + c
        self.tensors = list(zip(grads, lands))
        self.send_sems, self.recv_sems, self.local_sems = send_sems, recv_sems, local_sems

    def _copies(self, w):
        grad, land = self.tensors[w]
        out = []
        for k in range(1, N_DEV):
            peer, pidx = _peer(*self.place, k)
            n = w * (N_DEV - 1) + k - 1
            sems = (self.send_sems.at[n], self.recv_sems.at[n], peer)
            out.append((_remote(grad.at[pidx], land.at[self.me], *sems), _remote(grad.at[pidx], land.at[pidx], *sems)))
        return out

    def _local(self, w):
        grad, land = self.tensors[w]
        return pltpu.make_async_copy(grad.at[self.me], land.at[self.me], self.local_sems.at[w])

    def start(self):
        for w in range(len(self.tensors)):
            self._local(w).start()
            for send, _ in self._copies(w):
                send.start()

    def finish(self):
        for w in range(len(self.tensors)):
            copies = self._copies(w)
            for _, arrival in copies:
                arrival.wait_recv()
            for send, _ in copies:
                send.wait_send()
            self._local(w).wait()


def _cast_shards(shards):
    def body(*refs):
        n = len(refs) // 2
        for src, dst in zip(refs[:n], refs[n:]):
            dst[...] = src[...].astype(dst.dtype)

    shapes = [jax.ShapeDtypeStruct(s.shape, BF16) for s in shards]
    return pl.pallas_call(
        body, name="cast_shards", grid=(1,),
        in_specs=[_whole(s) for s in shards], out_specs=[_whole(s) for s in shapes], out_shape=shapes,
        compiler_params=_params(1),
    )(*shards)


def _gather_w_in(shard, x, g, chunk=512):
    rows, cols = shard.shape
    t, d = x.shape
    chunk = min(chunk, t)
    n_chunks = t // chunk

    def body(w_ref, x_ref, g_ref, out_ref, h_ref, x_buf, send_sems, recv_sems, local_sems, x_sems):
        plan = _GatherPlan([w_ref], [out_ref], send_sems, recv_sems, local_sems)
        plan.start()

        def fetch(j):
            return pltpu.make_async_copy(x_ref.at[pl.ds(j * chunk, chunk)], x_buf.at[j % 2], x_sems.at[j % 2])

        fetch(0).start()
        gv = g_ref[...]
        for j in range(n_chunks):
            if j + 1 < n_chunks:
                fetch(j + 1).start()
            fetch(j).wait()
            xv = x_buf[j % 2]
            r = lax.rsqrt(jnp.mean(xv * xv, axis=-1, keepdims=True) + EPS)
            h_ref[j * chunk:(j + 1) * chunk, :] = (xv * r * gv).astype(h_ref.dtype)
        plan.forward()
        plan.finish()

    vmem = pl.BlockSpec(memory_space=pltpu.VMEM)
    hbm = pl.BlockSpec(memory_space=pltpu.HBM)
    return pl.pallas_call(
        body, name="gather_w_in", in_specs=[hbm, hbm, vmem], out_specs=[hbm, vmem],
        out_shape=[jax.ShapeDtypeStruct((N_DEV, rows, cols), shard.dtype), jax.ShapeDtypeStruct((t, d), MXU_DTYPE)],
        scratch_shapes=[pltpu.VMEM((2, chunk, d), F32), pltpu.SemaphoreType.DMA((_GatherPlan.COPIES,)),
                        pltpu.SemaphoreType.DMA((_GatherPlan.COPIES,)), pltpu.SemaphoreType.DMA((1,)),
                        pltpu.SemaphoreType.DMA((2,))],
        compiler_params=pltpu.CompilerParams(vmem_limit_bytes=VMEM_LIMIT),
    )(shard, x, g)


def _dh_norm_bwd(pairs, x, g, res, name, tm=512, after=(), next_w=None):
    m, d = x.shape
    tm = min(tm, m)
    n_p = len(pairs)
    n_steps = m // tm
    n_in = 2 * n_p + 3 + len(after) + (next_w is not None)

    def body(*refs):
        x_ref, g_ref, res_ref = refs[2 * n_p:2 * n_p + 3]
        dx_ref, dg_ref = refs[n_in:n_in + 2]

        @pl.when(pl.program_id(0) == 0)
        def _():
            dg_ref[...] = jnp.zeros_like(dg_ref)

        dh = None
        for q in range(n_p):
            part = _dot(refs[2 * q][...], refs[2 * q + 1][...])
            dh = part if dh is None else dh + part
        xv = x_ref[...]
        r = lax.rsqrt(jnp.mean(xv * xv, axis=-1, keepdims=True) + EPS)
        xh = xv * r
        dxh = dh * g_ref[...]
        dxv = res_ref[...] + r * (dxh - xh * jnp.mean(dxh * xh, axis=-1, keepdims=True))
        dx_ref[...] = dxv
        dg_ref[...] += jnp.sum(dh * xh, axis=0, keepdims=True)
        if next_w is not None:
            refs[n_in + 2][...] = _dot_nt(dxv, refs[n_in - 1][...])

    in_specs, args = [], []
    for a, w, r in pairs:
        k = a.shape[1]
        in_specs += [pl.BlockSpec((tm, k), lambda i: (i, 0)),
                     pl.BlockSpec((k, d), functools.partial(lambda i, r: (r, 0), r=r or 0),
                                  pipeline_mode=pl.Buffered(1))]
        args += [a, w]
    row = pl.BlockSpec((tm, d), lambda i: (i, 0))
    vec = pl.BlockSpec((1, d), lambda i: (0, 0))
    extra_in, extra_out, extra_shape = [], [], []
    if next_w is not None:
        e = next_w.shape[0]
        extra_in = [pl.BlockSpec((e, d), lambda i: (0, 0), pipeline_mode=pl.Buffered(1))]
        extra_out = [pl.BlockSpec((tm, e), lambda i: (i, 0))]
        extra_shape = [jax.ShapeDtypeStruct((m, e), F32)]
        args_tail = (next_w,)
    else:
        args_tail = ()
    return pl.pallas_call(
        body, name=name, grid=(n_steps,),
        in_specs=in_specs + [row, vec, row] + [pl.BlockSpec(memory_space=pl.ANY)] * len(after) + extra_in,
        out_specs=[row, vec] + extra_out,
        out_shape=[jax.ShapeDtypeStruct((m, d), F32), jax.ShapeDtypeStruct((1, d), F32)] + extra_shape,
        compiler_params=_params(1),
    )(*args, x, g, res, *after, *args_tail)


def _dw_exchange_start(pieces, b, name, tn=512, tt=1024):
    t, n = b.shape
    widths = [p.shape[1] for p in pieces]
    rows = sum(widths)
    slot = rows // N_DEV
    steps = t // tt
    n_col = n // tn
    n_p = len(pieces)

    def body(*refs):
        piece_refs, b_ref, land_ref = refs[:n_p], refs[n_p], refs[n_p + 1]
        send_sem, recv_sem, local_sem, g_ref, land_thru, token, acc, stage, stage_sem = refs[n_p + 2:]
        del land_thru
        j, s = pl.program_id(0), pl.program_id(1)
        token[...] = jnp.zeros_like(token)

        @pl.when(s == 0)
        def _():
            acc[...] = jnp.zeros_like(acc)

        bv = b_ref[...]
        off = 0
        for p_ref, width in zip(piece_refs, widths):
            acc[off:off + width, :] += _dot_tn(p_ref[...], bv)
            off += width

        x, y, c = _mesh_place()
        me = 4 * x + 2 * y + c
        for col in range(n_col):
            @pl.when((j == col) & (s == steps - 1))
            def _(col=col):
                cols = pl.ds(col * tn, tn)
                for d in range(N_DEV):
                    stage[d] = acc[d * slot:(d + 1) * slot, :].astype(stage.dtype)
                put = pltpu.make_async_copy(stage, g_ref.at[:, :, cols], stage_sem)
                put.start()
                put.wait()
                for k in range(1, N_DEV):
                    peer, pidx = _peer(x, y, c, k)
                    _remote(g_ref.at[pidx, :, cols], land_ref.at[me, :, cols], send_sem, recv_sem, peer).start()
                if col == n_col - 1:
                    pltpu.make_async_copy(g_ref.at[me], land_ref.at[me], local_sem).start()

    hbm = pl.BlockSpec(memory_space=pltpu.HBM)
    sem = pl.BlockSpec(memory_space=pltpu.SEMAPHORE)
    buf = pltpu.HBM((N_DEV, slot, n), BF16)
    return pl.pallas_call(
        body, name=name, grid=(n_col, steps),
        in_specs=[pl.BlockSpec((tt, width), lambda j, s: (s, 0)) for width in widths]
        + [pl.BlockSpec((tt, tn), lambda j, s: (s, j)), hbm],
        out_specs=(sem, sem, sem, hbm, hbm, pl.BlockSpec((8, LANES), lambda j, s: (0, 0))),
        out_shape=(pltpu.SemaphoreType.DMA(()), pltpu.SemaphoreType.DMA(()), pltpu.SemaphoreType.DMA(()), buf, buf,
                   jax.ShapeDtypeStruct((8, LANES), F32)),
        scratch_shapes=[pltpu.VMEM((rows, tn), F32), pltpu.VMEM((N_DEV, slot, tn), BF16),
                        pltpu.SemaphoreType.DMA(())],
        input_output_aliases={n_p + 1: 4},
        compiler_params=pltpu.CompilerParams(dimension_semantics=("arbitrary", "arbitrary"),
                                             vmem_limit_bytes=VMEM_LIMIT,
                                             has_side_effects=pltpu.SideEffectType.DATAFLOW_SIDE_EFFECTING),
    )(*pieces, b, pltpu.with_memory_space_constraint(lax.empty((N_DEV, slot, n), BF16), pltpu.HBM))


def _exchange_wait(send_sem, recv_sem, local_sem, grad, land, after, name):
    def body(g_ref, land_ref, send_sem, recv_sem, local_sem, *rest):
        x, y, c = _mesh_place()
        peer, _ = _peer(x, y, c, 1)
        others = pl.ds(0, N_DEV - 1)
        seven = _remote(g_ref.at[others], land_ref.at[others], send_sem, recv_sem, peer)
        seven.wait_send()
        seven.wait_recv()
        pltpu.make_async_copy(g_ref.at[0], land_ref.at[0], local_sem).wait()

    hbm = pl.BlockSpec(memory_space=pltpu.HBM)
    sem = pl.BlockSpec(memory_space=pltpu.SEMAPHORE)
    buf = pltpu.HBM(grad.shape, grad.dtype)
    return pl.pallas_call(
        body, name=name, out_shape=(buf, buf),
        in_specs=(hbm, hbm, sem, sem, sem) + (pl.BlockSpec(memory_space=pl.ANY),) * len(after), out_specs=(hbm, hbm),
        input_output_aliases={0: 0, 1: 1},
        compiler_params=pltpu.CompilerParams(has_side_effects=pltpu.SideEffectType.DATAFLOW_SIDE_EFFECTING),
    )(grad, land, send_sem, recv_sem, local_sem, *after)[1]


SMALL_LAYOUT = ((0, 0, 0, 0, D_MODEL), (1, 0, 1, 0, GROUP), (1, 1, 1, GROUP, GROUP), (2, 0, 2, 0, GROUP),
                (3, 0, 2, GROUP, GROUP), (4, 0, 3, 0, D_MODEL), (5, 0, 4, 0, D_MODEL))
LOSS_ROW = 5
N_SMALL = 6


def _place_small(dst, srcs):
    dst[...] = jnp.zeros_like(dst)
    for p, sr, dr, dc, width in SMALL_LAYOUT:
        dst[dr:dr + 1, dc:dc + width] = srcs[p][sr:sr + 1, :]


def _small_allreduce(grads, loss_part):
    def body(*refs):
        g_in, loss_in, sum_ref = refs[:N_SMALL], refs[N_SMALL], refs[N_SMALL + 1]
        pack, land, send_sems, recv_sems = refs[N_SMALL + 2:]
        _place_small(pack, g_in)
        pack[LOSS_ROW:LOSS_ROW + 1, 0:LANES] = loss_in[...]

        x, y, c = _mesh_place()
        me = 4 * x + 2 * y + c
        land[me] = pack[...]
        sends = []
        for k in range(1, N_DEV):
            peer, _ = _peer(x, y, c, k)
            cp = _remote(pack, land.at[me], send_sems.at[k - 1], recv_sems.at[k - 1], peer)
            cp.start()
            sends.append(cp)
        for k in range(1, N_DEV):
            peer, pidx = _peer(x, y, c, k)
            _remote(pack, land.at[pidx], send_sems.at[k - 1], recv_sems.at[k - 1], peer).wait_recv()
        for cp in sends:
            cp.wait_send()

        g = land[0]
        for d in range(1, N_DEV):
            g = g + land[d]
        sum_ref[...] = g

    args = [*grads, loss_part]
    packed = jax.ShapeDtypeStruct((SMALL_ROWS, D_MODEL), F32)
    return pl.pallas_call(
        body, name="small_allreduce", grid=(1,),
        in_specs=[_whole(a) for a in args], out_specs=_whole(packed), out_shape=packed,
        scratch_shapes=[pltpu.VMEM(packed.shape, F32), pltpu.VMEM((N_DEV,) + packed.shape, F32),
                        pltpu.SemaphoreType.DMA((N_DEV - 1,)), pltpu.SemaphoreType.DMA((N_DEV - 1,))],
        compiler_params=_params(1),
    )(*args)


def _small_adamw(gsum, ws, ms, vs):
    def body(*refs):
        g_ref = refs[0]
        params = [refs[1 + q * N_SMALL:1 + (q + 1) * N_SMALL] for q in range(3)]
        o0 = 1 + 3 * N_SMALL
        outs = [refs[o0 + q * N_SMALL:o0 + (q + 1) * N_SMALL] for q in range(4)]
        loss_out = refs[o0 + 4 * N_SMALL]
        wp, mp, vp = refs[o0 + 4 * N_SMALL + 1:]
        for dst, srcs in zip((wp, mp, vp), params):
            _place_small(dst, srcs)
        g = g_ref[...]
        delta, nm, nv = _adam(wp[...], g, mp[...], vp[...])
        for val, out in zip((g, delta, nm, nv), outs):
            for p, sr, dr, dc, width in SMALL_LAYOUT:
                out[p][sr:sr + 1, :] = val[dr:dr + 1, dc:dc + width]
        loss_out[...] = g[LOSS_ROW:LOSS_ROW + 1, 0:LANES]

    args = [gsum, *ws, *ms, *vs]
    shapes = [jax.ShapeDtypeStruct(w.shape, F32) for w in ws] * 4 + [jax.ShapeDtypeStruct((1, LANES), F32)]
    packed = pltpu.VMEM((SMALL_ROWS, D_MODEL), F32)
    outs = pl.pallas_call(
        body, name="small_adamw", grid=(1,),
        in_specs=[_whole(a) for a in args], out_specs=[_whole(s) for s in shapes], out_shape=shapes,
        scratch_shapes=[packed, packed, packed], compiler_params=_params(1),
    )(*args)
    return [outs[q * N_SMALL:(q + 1) * N_SMALL] for q in range(4)], outs[4 * N_SMALL]


def _adam(w, g, m, v):
    m = ADAM_B1 * m + (1.0 - ADAM_B1) * g
    v = ADAM_B2 * v + (1.0 - ADAM_B2) * (g * g)
    m_hat = m / (1.0 - ADAM_B1 ** ADAM_STEP)
    v_hat = v / (1.0 - ADAM_B2 ** ADAM_STEP)
    delta = -ADAM_LR * (m_hat / (jnp.sqrt(v_hat) + ADAM_EPS) + ADAM_WD * w)
    return delta, m, v


def _reduce_adamw(land, w, m, v, name, tr, after=()):
    _, rows, width = land.shape

    def body(land_ref, w_ref, m_ref, v_ref, *rest):
        g_ref, d_ref, nm_ref, nv_ref = rest[len(after):]
        g = land_ref[0].astype(F32)
        for d in range(1, N_DEV):
            g = g + land_ref[d].astype(F32)
        delta, nm, nv = _adam(w_ref[...], g, m_ref[...], v_ref[...])
        g_ref[...] = g
        d_ref[...] = delta
        nm_ref[...] = nm
        nv_ref[...] = nv

    row = pl.BlockSpec((tr, width), lambda i: (i, 0))
    out = jax.ShapeDtypeStruct((rows, width), F32)
    return pl.pallas_call(
        body, name=name, grid=(rows // tr,),
        in_specs=[pl.BlockSpec((N_DEV, tr, width), lambda i: (0, i, 0)), row, row, row]
        + [pl.BlockSpec(memory_space=pl.ANY)] * len(after),
        out_specs=[row, row, row, row], out_shape=[out, out, out, out],
        compiler_params=_params(1),
    )(land, w, m, v, *after)


def kernel(x, mix_norm_g, w_in, lower_bounds, hgrn_norm_g, sb_norm_g, w_out, ffn_norm_g, w_gate, w_up, w_down, final_norm_g, loss_target, m_mix_norm_g, m_w_in, m_lower_bounds, m_hgrn_norm_g, m_sb_norm_g, m_w_out, m_ffn_norm_g, m_w_gate, m_w_up, m_w_down, m_final_norm_g, v_mix_norm_g, v_w_in, v_lower_bounds, v_hgrn_norm_g, v_sb_norm_g, v_w_out, v_ffn_norm_g, v_w_gate, v_w_up, v_w_down, v_final_norm_g):
    n_seq, seq, d = x.shape
    t = n_seq * seq
    x2d = x.reshape(t, d)
    tgt = loss_target.reshape(t, d)
    final_g = final_norm_g.reshape(1, d)
    col_sharded = (True, False, True, True, False)

    def as_rows(ws):
        return [w[0].T if tr else w[0] for w, tr in zip(ws, col_sharded)]

    big_w = as_rows([w_in, w_out, w_gate, w_up, w_down])
    big_m = as_rows([m_w_in, m_w_out, m_w_gate, m_w_up, m_w_down])
    big_v = as_rows([v_w_in, v_w_out, v_w_gate, v_w_up, v_w_down])

    sh_in, sh_out, sh_gate, sh_up, sh_down = _cast_shards(big_w)
    wt_in, h1 = _gather_w_in(sh_in, x2d, mix_norm_g)
    wt_in = wt_in.reshape(IN_COLS, d)
    proj_h, proj_s = _mm_nt_split(h1, wt_in, "proj", 4 * GROUP, (F32, MXU_DTYPE))
    mixed, oa_pre, states = _hgrn_fwd(proj_h, lower_bounds, hgrn_norm_g, n_seq, seq)
    mixed, ob_pre, ctot, gathered = _sb_fwd(proj_s, sb_norm_g, mixed, [sh_out, sh_gate, sh_up, sh_down], n_seq, seq)
    wf_out = gathered[0].reshape(d, d)
    wt_gate = gathered[1].reshape(D_FF, d)
    wt_up = gathered[2].reshape(D_FF, d)
    wf_down = gathered[3].reshape(D_FF, d)
    x1, h2 = _mix_out_norm(mixed, wf_out, x2d, ffn_norm_g, "mix_out")
    gate, up, ff = _ffn_up(h2, wt_gate, wt_up, "ffn_up")
    dx2, dx2m, d_final_g, loss_part = _ffn_down_loss(ff, wf_down, x1, tgt, final_g, "ffn_down_loss")

    dgate, dup = _ffn_bwd_act(dx2m, wf_down, gate, up, "ffn_bwd_act")
    dw_down = _mm_tn(ff, dx2m, "dw_down", tk=1408, tn=1024).reshape(N_DEV, D_FF // N_DEV, d)
    dw_gate = _mm_tn(dgate, h2, "dw_gate", tk=1408, tn=1024).reshape(N_DEV, D_FF // N_DEV, d)
    dw_up = _mm_tn(dup, h2, "dw_up", tk=1408, tn=1024).reshape(N_DEV, D_FF // N_DEV, d)
    dx1, d_ffn_g, dmix = _dh_norm_bwd([(dgate, wt_gate, None), (dup, wt_up, None)], x1, ffn_norm_g, dx2, "dh_ffn",
                                      tm=256, next_w=wf_out)
    dw_out = _mm_tn(mixed, dx1, "dw_out", tk=1024, tn=1024).reshape(N_DEV, d // N_DEV, d)
    dsq, dsk, dsv, d_sb_g, lands = _sb_bwd(proj_s, sb_norm_g, dmix, ob_pre, ctot, [dw_out, dw_gate, dw_up, dw_down],
                                            n_seq, seq)
    dhq, dhf, dhi, dhg, d_lb, d_hgrn_g = _hgrn_bwd(proj_h, lower_bounds, hgrn_norm_g, dmix, oa_pre, states, n_seq,
                                                   seq)
    dproj = [dhq, dhf, dhi, dhg, dsq, dsk, dsv]
    send_sem, recv_sem, local_sem, dw_in, land_in, token = _dw_exchange_start(dproj, h1, "dw_in")
    dx, d_mix_g = _dh_norm_bwd([(piece, wt_in, k) for k, piece in enumerate(dproj)], x2d, mix_norm_g, dx1, "dh_mix",
                               after=(token,))

    tiles = {"in": 224, "out": 128, "gate": 176, "up": 176, "down": 176}
    keys = list(tiles)
    rest = [_reduce_adamw(land, w, m, v, "adamw_" + key, tr=tiles[key], after=(token,))
            for key, land, w, m, v in zip(keys[1:], lands, big_w[1:], big_m[1:], big_v[1:])]
    small_sum = _small_allreduce([d_mix_g, d_lb, d_hgrn_g, d_sb_g, d_ffn_g, d_final_g], loss_part)
    small, loss_row = _small_adamw(
        small_sum, [mix_norm_g, lower_bounds, hgrn_norm_g, sb_norm_g, ffn_norm_g, final_g],
        [m_mix_norm_g, m_lower_bounds, m_hgrn_norm_g, m_sb_norm_g, m_ffn_norm_g, m_final_norm_g.reshape(1, d)],
        [v_mix_norm_g, v_lower_bounds, v_hgrn_norm_g, v_sb_norm_g, v_ffn_norm_g, v_final_norm_g.reshape(1, d)])
    land_in = _exchange_wait(send_sem, recv_sem, local_sem, dw_in, land_in,
                             [dx, loss_row] + [res[0] for res in rest], "dw_in_await")
    big = [_reduce_adamw(land_in, big_w[0], big_m[0], big_v[0], "adamw_in", tr=tiles["in"])] + rest
    big = [[r.T if tr else r for r in res] for res, tr in zip(big, col_sharded)]

    outs = [loss_row[0, 0], dx.reshape(n_seq, seq, d)]
    for q in range(4):
        b_in, b_out, b_gate, b_up, b_down = [res[q][None] for res in big]
        s_mix, s_lb, s_hgrn, s_sb, s_ffn, s_final = small[q]
        outs += [s_mix, b_in, s_lb, s_hgrn, s_sb, b_out, s_ffn, b_gate, b_up, b_down, s_final.reshape(d)]
    return tuple(outs)
```

```python
import functools
import math

import jax
import jax.numpy as jnp
from jax import lax
from jax.experimental import pallas as pl
from jax.experimental.pallas import tpu as pltpu

F32 = jnp.float32
BF16 = jnp.bfloat16
MXU_DTYPE = BF16

EPS = 1e-6
D_MODEL = 1024
N_HEADS = 8
D_HEAD = 64
GROUP = N_HEADS * D_HEAD
IN_COLS = 7 * GROUP
D_FF = 2816
CHUNK = 64
LANES = 128
N_PAIRS = GROUP // LANES
SUPER = 256
SB_BLOCK = 256
N_DEV = 8

ADAM_LR = 0.001
ADAM_B1 = 0.9
ADAM_B2 = 0.999
ADAM_EPS = 1e-08
ADAM_WD = 0.01
ADAM_STEP = 10

SMALL_ROWS = 8
FF_TILE = D_FF // 2

VMEM_LIMIT = 48 * 1024 * 1024


def _params(n_axes, vmem=VMEM_LIMIT):
    return pltpu.CompilerParams(dimension_semantics=("arbitrary",) * n_axes, vmem_limit_bytes=vmem)


def _whole(a):
    return pl.BlockSpec(a.shape, functools.partial(lambda i, nd: (0,) * nd, nd=len(a.shape)))


def _dot(a, b):
    return jnp.dot(a.astype(MXU_DTYPE), b.astype(MXU_DTYPE), preferred_element_type=F32)


def _dot_nt(a, b):
    return lax.dot_general(a.astype(MXU_DTYPE), b.astype(MXU_DTYPE), (((1,), (1,)), ((), ())),
                           preferred_element_type=F32)


def _dot_tn(a, b):
    return lax.dot_general(a.astype(MXU_DTYPE), b.astype(MXU_DTYPE), (((0,), (0,)), ((), ())),
                           preferred_element_type=F32)


def _split(x, parts):
    out, r = [], x
    for _ in range(parts):
        h = r.astype(BF16)
        out.append(h)
        r = r - h.astype(F32)
    return out


def _rsum_left(u, x, parts):
    acc = None
    for h in _split(x, parts):
        d = jnp.dot(u, h, preferred_element_type=F32)
        acc = d if acc is None else acc + d
    return acc


def _ones_where(mask):
    return jnp.where(mask, 1.0, 0.0).astype(BF16)


def _sigmoid(x):
    return 1.0 / (1.0 + jnp.exp(-x))


def _softplus(x):
    return jnp.maximum(x, 0.0) + jnp.log(1.0 + jnp.exp(-jnp.abs(x)))


def _head_masks():
    lane = lax.broadcasted_iota(jnp.int32, (1, LANES), 1)
    return [jnp.where(lane < D_HEAD, 1.0, 0.0), jnp.where(lane >= D_HEAD, 1.0, 0.0)]


def _head_rstd(o, masks):
    sq = o * o
    r = None
    for m in masks:
        ms = jnp.sum(sq * m, axis=1, keepdims=True) * (1.0 / D_HEAD)
        t = lax.rsqrt(ms + EPS) * m
        r = t if r is None else r + t
    return r


def _head_mean(t, masks):
    out = None
    for m in masks:
        v = jnp.sum(t * m, axis=1, keepdims=True) * (1.0 / D_HEAD) * m
        out = v if out is None else out + v
    return out


def _mix_out_norm(a, w, res, g, name, tm=512):
    m, k = a.shape
    d = w.shape[1]
    tm = min(tm, m)

    def body(a_ref, w_ref, res_ref, g_ref, x_ref, h_ref):
        xv = res_ref[...] + _dot(a_ref[...], w_ref[...])
        x_ref[...] = xv
        r = lax.rsqrt(jnp.mean(xv * xv, axis=-1, keepdims=True) + EPS)
        h_ref[...] = (xv * r * g_ref[...]).astype(h_ref.dtype)

    row = pl.BlockSpec((tm, d), lambda i: (i, 0))
    return pl.pallas_call(
        body, name=name, grid=(m // tm,),
        in_specs=[pl.BlockSpec((tm, k), lambda i: (i, 0)), pl.BlockSpec((k, d), lambda i: (0, 0)), row,
                  pl.BlockSpec((1, d), lambda i: (0, 0))],
        out_specs=[row, row],
        out_shape=[jax.ShapeDtypeStruct((m, d), F32), jax.ShapeDtypeStruct((m, d), MXU_DTYPE)],
        compiler_params=_params(1),
    )(a, w, res, g)


def _ffn_down_loss(a, w, res, target, g, name, tm=512):
    m, k = a.shape
    d = w.shape[1]
    tm = min(tm, m)

    def body(a_ref, w_ref, res_ref, t_ref, g_ref, dx_ref, dxm_ref, dg_ref, loss_ref):
        xv = res_ref[...] + _dot(a_ref[...], w_ref[...])
        gv = g_ref[...]
        r = lax.rsqrt(jnp.mean(xv * xv, axis=-1, keepdims=True) + EPS)
        xh = xv * r
        e = xh * gv - t_ref[...]
        dy = e * (1.0 / d)
        dxh = dy * gv
        dxv = r * (dxh - xh * jnp.mean(dxh * xh, axis=-1, keepdims=True))
        dx_ref[...] = dxv
        dxm_ref[...] = dxv.astype(dxm_ref.dtype)

        @pl.when(pl.program_id(0) == 0)
        def _():
            dg_ref[...] = jnp.zeros_like(dg_ref)
            loss_ref[...] = jnp.zeros_like(loss_ref)

        dg_ref[...] += jnp.sum(dy * xh, axis=0, keepdims=True)
        part = 0.5 * jnp.sum(jnp.mean(e * e, axis=-1, keepdims=True), axis=0, keepdims=True)
        loss_ref[...] += jnp.broadcast_to(part, loss_ref.shape)

    row = pl.BlockSpec((tm, d), lambda i: (i, 0))
    vec = pl.BlockSpec((1, d), lambda i: (0, 0))
    return pl.pallas_call(
        body, name=name, grid=(m // tm,),
        in_specs=[pl.BlockSpec((tm, k), lambda i: (i, 0)), pl.BlockSpec((k, d), lambda i: (0, 0)), row, row, vec],
        out_specs=[row, row, vec, pl.BlockSpec((1, LANES), lambda i: (0, 0))],
        out_shape=[jax.ShapeDtypeStruct((m, d), F32), jax.ShapeDtypeStruct((m, d), MXU_DTYPE),
                   jax.ShapeDtypeStruct((1, d), F32), jax.ShapeDtypeStruct((1, LANES), F32)],
        compiler_params=_params(1),
    )(a, w, res, target, g)


def _mm_nt_split(a, w, name, split, dtypes, tm=1024):
    m, n = a.shape
    k = w.shape[0]

    def body(a_ref, w_ref, lo_ref, hi_ref):
        av = a_ref[...]
        lo_ref[...] = _dot_nt(av, w_ref[0:split, :]).astype(lo_ref.dtype)
        hi_ref[...] = _dot_nt(av, w_ref[split:k, :]).astype(hi_ref.dtype)

    return pl.pallas_call(
        body, name=name, grid=(m // tm,),
        in_specs=[pl.BlockSpec((tm, n), lambda i: (i, 0)),
                  pl.BlockSpec((k, n), lambda i: (0, 0), pipeline_mode=pl.Buffered(1))],
        out_specs=[pl.BlockSpec((tm, split), lambda i: (i, 0)), pl.BlockSpec((tm, k - split), lambda i: (i, 0))],
        out_shape=[jax.ShapeDtypeStruct((m, split), dtypes[0]), jax.ShapeDtypeStruct((m, k - split), dtypes[1])],
        compiler_params=_params(1),
    )(a, w)


def _mm_tn(a, b, name, tk, tn, tt=1024, out_dtype=BF16):
    t, k = a.shape
    n = b.shape[1]
    tt = min(tt, t)
    steps = t // tt

    def body(a_ref, b_ref, o_ref, acc):
        s = pl.program_id(2)

        @pl.when(s == 0)
        def _():
            acc[...] = jnp.zeros_like(acc)

        acc[...] += _dot_tn(a_ref[...], b_ref[...])

        @pl.when(s == steps - 1)
        def _():
            o_ref[...] = acc[...].astype(o_ref.dtype)

    return pl.pallas_call(
        body, name=name, grid=(k // tk, n // tn, steps),
        in_specs=[pl.BlockSpec((tt, tk), lambda i, j, s: (s, i)), pl.BlockSpec((tt, tn), lambda i, j, s: (s, j))],
        out_specs=pl.BlockSpec((tk, tn), lambda i, j, s: (i, j)),
        out_shape=jax.ShapeDtypeStruct((k, n), out_dtype),
        scratch_shapes=[pltpu.VMEM((tk, tn), F32)],
        compiler_params=_params(3),
    )(a, b)


def _ffn_up(h, wg_t, wu_t, name, tm=1024, tn=FF_TILE):
    m, k = h.shape
    n = wg_t.shape[0]
    tm = min(tm, m)

    def body(h_ref, wg_ref, wu_ref, gate_ref, up_ref, ff_ref):
        hv = h_ref[...]
        gate = _dot_nt(hv, wg_ref[...])
        up = _dot_nt(hv, wu_ref[...])
        gate_ref[...] = gate.astype(gate_ref.dtype)
        up_ref[...] = up.astype(up_ref.dtype)
        ff_ref[...] = (gate * _sigmoid(gate) * up).astype(ff_ref.dtype)

    wspec = pl.BlockSpec((tn, k), lambda j, i: (j, 0))
    ospec = pl.BlockSpec((tm, tn), lambda j, i: (i, j))
    return pl.pallas_call(
        body, name=name, grid=(n // tn, m // tm),
        in_specs=[pl.BlockSpec((tm, k), lambda j, i: (i, 0)), wspec, wspec],
        out_specs=[ospec, ospec, ospec],
        out_shape=[jax.ShapeDtypeStruct((m, n), MXU_DTYPE)] * 3,
        compiler_params=_params(2),
    )(h, wg_t, wu_t)


def _ffn_bwd_act(dx, wd, gate, up, name, tm=1024, tn=FF_TILE):
    m, k = dx.shape
    n = wd.shape[0]
    tm = min(tm, m)

    def body(dx_ref, wd_ref, gate_ref, up_ref, dgate_ref, dup_ref):
        dff = _dot_nt(dx_ref[...], wd_ref[...])
        gate = gate_ref[...].astype(F32)
        sg = _sigmoid(gate)
        dgate_ref[...] = (dff * up_ref[...].astype(F32) * sg * (1.0 + gate * (1.0 - sg))).astype(dgate_ref.dtype)
        dup_ref[...] = (dff * gate * sg).astype(dup_ref.dtype)

    ospec = pl.BlockSpec((tm, tn), lambda j, i: (i, j))
    return pl.pallas_call(
        body, name=name, grid=(n // tn, m // tm),
        in_specs=[pl.BlockSpec((tm, k), lambda j, i: (i, 0)), pl.BlockSpec((tn, k), lambda j, i: (j, 0)),
                  ospec, ospec],
        out_specs=[ospec, ospec],
        out_shape=[jax.ShapeDtypeStruct((m, n), MXU_DTYPE), jax.ShapeDtypeStruct((m, n), MXU_DTYPE)],
        compiler_params=_params(2),
    )(dx, wd, gate, up)


def _chunk_masks():
    r = lax.broadcasted_iota(jnp.int32, (SUPER, SUPER), 0)
    c = lax.broadcasted_iota(jnp.int32, (SUPER, SUPER), 1)
    same = jnp.right_shift(r, 6) == jnp.right_shift(c, 6)
    lower = jnp.logical_and(same, c <= r)
    upper = jnp.logical_and(same, c >= r)
    return same, lower, upper


def _head_block_mask():
    r = lax.broadcasted_iota(jnp.int32, (LANES, LANES), 0)
    c = lax.broadcasted_iota(jnp.int32, (LANES, LANES), 1)
    return jnp.where(jnp.right_shift(r, 6) == jnp.right_shift(c, 6), 1.0, 0.0)


def _lower_bound(lb_raw):
    return 1.0 / (1.0 + jnp.exp(lb_raw[1:2, :] - lb_raw[0:1, :]))


HGRN_UNROLL = 4
PER_SUPER = SUPER // CHUNK
CHUNK_ROWS = [slice(c * CHUNK, (c + 1) * CHUNK) for c in range(PER_SUPER)]


def _over_chunks(rows):
    return jnp.concatenate([jnp.broadcast_to(r, (CHUNK, LANES)) for r in rows], axis=0)


def _hgrn_gates(q, hf, lb, lower_b):
    sig = _sigmoid(hf)
    f = lb + (1.0 - lb) * sig
    k = 1.0 - f
    lf = jnp.log(f)
    b = _rsum_left(lower_b, lf, 2)
    ends = [b[cr.stop - 1:cr.stop, :] for cr in CHUNK_ROWS]
    eb = jnp.exp(b)
    enb = jnp.exp(-b)
    edb = jnp.exp(_over_chunks(ends) - b)
    decs = [jnp.exp(e) for e in ends]
    return sig, f, k, decs, eb, enb, edb, q * eb, k * enb, k * edb


def _hgrn_fwd(proj, lower_bounds, norm_g, n_seq, seq):
    t = n_seq * seq
    n_super = seq // SUPER
    n_chunks = seq // CHUNK

    def body(q_ref, f_ref, i_ref, g_ref, lb_ref, ng_ref, out_ref, opre_ref, st_ref):
        masks = _head_masks()
        _, lower, _ = _chunk_masks()
        lower_b = _ones_where(lower)
        bd = _head_block_mask()
        lb = _lower_bound(lb_ref[...])
        ng = ng_ref[...]

        def step(it, st):
            blocks = [HGRN_UNROLL * it + u for u in range(HGRN_UNROLL)]
            rows = [pl.ds(pl.multiple_of(sb * SUPER, SUPER), SUPER) for sb in blocks]
            vs = [i_ref[rw, :] for rw in rows]
            gates = [_hgrn_gates(q_ref[rw, :], f_ref[rw, :], lb, lower_b) for rw in rows]
            decs, qes, kes, kds = ([g[k] for g in gates] for k in (3, 7, 8, 9))
            scores = [[_dot_nt(qe * m, ke) for m in masks] for qe, ke in zip(qes, kes)]
            updates = [[_dot_tn(v[cr], kd[cr]) for cr in CHUNK_ROWS] for v, kd in zip(vs, kds)]
            states = [st]
            for dec_b, upd_b in zip(decs, updates):
                for dec, upd in zip(dec_b, upd_b):
                    states.append(states[-1] * dec + bd * upd)
            for u, sb in enumerate(blocks):
                for c in range(PER_SUPER):
                    st_ref[0, 0, sb * PER_SUPER + c] = states[u * PER_SUPER + c]
            intra = [[_dot(jnp.where(lower, p, 0.0), v) for p in sc] for sc, v in zip(scores, vs)]
            inter = [[_dot_nt(qe[cr], states[u * PER_SUPER + c]) for c, cr in enumerate(CHUNK_ROWS)]
                     for u, qe in enumerate(qes)]
            for rw, intra_b, inter_b in zip(rows, intra, inter):
                o = intra_b[0] * masks[0] + intra_b[1] * masks[1] + jnp.concatenate(inter_b, axis=0)
                opre_ref[rw, :] = o
                hg = g_ref[rw, :]
                on = o * _head_rstd(o, masks) * ng
                out_ref[rw, :] = (on * hg * _sigmoid(hg)).astype(out_ref.dtype)
            return states[-1]

        lax.fori_loop(0, n_super // HGRN_UNROLL, step, jnp.zeros((LANES, LANES), F32))

    def col(k):
        return pl.BlockSpec((seq, LANES), lambda p, b: (b, k * N_PAIRS + p))

    vec = lambda rows: pl.BlockSpec((rows, LANES), lambda p, b: (0, p))
    ospec = pl.BlockSpec((seq, LANES), lambda p, b: (b, p))
    return pl.pallas_call(
        body, name="hgrn_fwd", grid=(N_PAIRS, n_seq),
        in_specs=[col(0), col(1), col(2), col(3), vec(2), vec(1)],
        out_specs=[ospec, ospec,
                   pl.BlockSpec((1, 1, n_chunks, LANES, LANES), lambda p, b: (b, p, 0, 0, 0))],
        out_shape=[jax.ShapeDtypeStruct((t, 2 * GROUP), MXU_DTYPE), jax.ShapeDtypeStruct((t, GROUP), F32),
                   jax.ShapeDtypeStruct((n_seq, N_PAIRS, n_chunks, LANES, LANES), F32)],
        compiler_params=_params(2),
    )(proj, proj, proj, proj, lower_bounds, norm_g)


def _hgrn_bwd(proj, lower_bounds, norm_g, dmix, opre, states, n_seq, seq):
    t = n_seq * seq
    n_super = seq // SUPER
    n_chunks = seq // CHUNK
    per = SUPER // CHUNK

    def body(q_ref, f_ref, i_ref, g_ref, lb_ref, ng_ref, dm_ref, opre_ref, st_ref,
             dq_ref, df_ref, di_ref, dg_ref, dlb_ref, dng_ref):
        masks = _head_masks()
        _, lower, upper = _chunk_masks()
        lower_b, upper_b = _ones_where(lower), _ones_where(upper)
        bd = _head_block_mask()
        lb_raw = lb_ref[...]
        lb = _lower_bound(lb_raw)
        ng = ng_ref[...]

        @pl.when(pl.program_id(1) == 0)
        def _():
            dlb_ref[...] = jnp.zeros_like(dlb_ref)
            dng_ref[...] = jnp.zeros_like(dng_ref)

        def first_half(sb):
            rows = pl.ds(pl.multiple_of(sb * SUPER, SUPER), SUPER)
            q, hf, v, hg = q_ref[rows, :], f_ref[rows, :], i_ref[rows, :], g_ref[rows, :]
            sig, f, k, decs, eb, enb, edb, qe, ke, kd = _hgrn_gates(q, hf, lb, lower_b)
            o = opre_ref[rows, :]
            r = _head_rstd(o, masks)
            oh = o * r
            dm = dm_ref[rows, :]
            sg = _sigmoid(hg)
            dg_ref[rows, :] = (dm * oh * ng * sg * (1.0 + hg * (1.0 - sg))).astype(dg_ref.dtype)
            don = dm * hg * sg
            dng_ref[...] += jnp.sum(don * oh, axis=0, keepdims=True)
            doh = don * ng
            do = r * (doh - oh * _head_mean(doh * oh, masks))
            doms = [do * m for m in masks]
            qems = [qe * m for m in masks]
            scores = [_dot_nt(qem, ke) for qem in qems]
            dscores = [_dot_nt(dom, v) for dom in doms]
            prevs = [st_ref[0, 0, sb * per + c] for c in range(per)]
            dst_in = [_dot_tn(do[cr], qe[cr]) for cr in CHUNK_ROWS]
            dqe_i = [_dot(do[cr], prev) for cr, prev in zip(CHUNK_ROWS, prevs)]
            return dict(rows=rows, v=v, sig=sig, f=f, decs=decs, eb=eb, enb=enb, edb=edb, qe=qe, ke=ke, kd=kd,
                        doms=doms, qems=qems, scores=scores, dscores=dscores, prevs=prevs, dst_in=dst_in, dqe_i=dqe_i)

        def second_half(blk, dsts):
            v, qe, ke, kd = blk["v"], blk["qe"], blk["ke"], blk["kd"]
            ps = [jnp.where(lower, p, 0.0) for p in blk["scores"]]
            dps = [jnp.where(lower, dp, 0.0) for dp in blk["dscores"]]
            dqe_h = [_dot(dp, ke) for dp in dps]
            dke_h = [_dot_tn(dp, qem) for dp, qem in zip(dps, blk["qems"])]
            dv_h = [_dot_tn(p, dom) for p, dom in zip(ps, blk["doms"])]
            dus = [bd * d for d in dsts]
            dv_i = [_dot_nt(kd[cr], du) for cr, du in zip(CHUNK_ROWS, dus)]
            dkd_i = [_dot(v[cr], du) for cr, du in zip(CHUNK_ROWS, dus)]

            def finish():
                dqe = dqe_h[0] * masks[0] + dqe_h[1] * masks[1] + jnp.concatenate(blk["dqe_i"], axis=0)
                dke = dke_h[0] + dke_h[1]
                dv = dv_h[0] + dv_h[1] + jnp.concatenate(dv_i, axis=0)
                dkd = jnp.concatenate(dkd_i, axis=0)
                dk = dke * blk["enb"] + dkd * blk["edb"]
                db = dqe * qe - dke * ke - dkd * kd
                dkd_kd = dkd * kd
                dends = [jnp.sum(dkd_kd[cr], axis=0, keepdims=True)
                         + jnp.sum(dsts[c] * blk["prevs"][c], axis=0, keepdims=True) * blk["decs"][c]
                         for c, cr in enumerate(CHUNK_ROWS)]
                dlf = _rsum_left(upper_b, db, 2) + _over_chunks(dends)
                sig = blk["sig"]
                dfv = dlf / blk["f"] - dk
                rows = blk["rows"]
                dq_ref[rows, :] = (dqe * blk["eb"]).astype(dq_ref.dtype)
                di_ref[rows, :] = dv.astype(di_ref.dtype)
                df_ref[rows, :] = (dfv * (1.0 - lb) * sig * (1.0 - sig)).astype(df_ref.dtype)
                dlb = jnp.sum(dfv * (1.0 - sig), axis=0, keepdims=True)
                da0 = dlb * lb * (1.0 - lb)
                dlb_ref[0:1, :] += da0
                dlb_ref[1:2, :] -= da0

            return finish

        def step(it, dst):
            blocks = [first_half(n_super - 1 - HGRN_UNROLL * it - u) for u in range(HGRN_UNROLL)]
            all_dsts = []
            for blk in blocks:
                dsts = [None] * per
                for c in reversed(range(per)):
                    dsts[c] = dst
                    dst = bd * (dst * blk["decs"][c] + blk["dst_in"][c])
                all_dsts.append(dsts)
            for finish in [second_half(blk, dsts) for blk, dsts in zip(blocks, all_dsts)]:
                finish()
            return dst

        lax.fori_loop(0, n_super // HGRN_UNROLL, step, jnp.zeros((LANES, LANES), F32))

    def col(k):
        return pl.BlockSpec((seq, LANES), lambda p, b: (b, k * N_PAIRS + p))

    vec = lambda rows: pl.BlockSpec((rows, LANES), lambda p, b: (0, p))
    ospec = pl.BlockSpec((seq, LANES), lambda p, b: (b, p))
    piece = jax.ShapeDtypeStruct((t, GROUP), MXU_DTYPE)
    return pl.pallas_call(
        body, name="hgrn_bwd", grid=(N_PAIRS, n_seq),
        in_specs=[col(0), col(1), col(2), col(3), vec(2), vec(1), ospec, ospec,
                  pl.BlockSpec((1, 1, n_chunks, LANES, LANES), lambda p, b: (b, p, 0, 0, 0))],
        out_specs=[ospec, ospec, ospec, ospec, vec(2), vec(1)],
        out_shape=[piece, piece, piece, piece,
                   jax.ShapeDtypeStruct((2, GROUP), F32), jax.ShapeDtypeStruct((1, GROUP), F32)],
        compiler_params=_params(2),
    )(proj, proj, proj, proj, lower_bounds, norm_g, dmix, opre, states)


SB_SCALE = 1.0 / math.sqrt(D_HEAD)


SB_STEP = 2 * SB_BLOCK
QUERY_BLOCKS = (slice(0, SB_BLOCK), slice(SB_BLOCK, SB_STEP))


def _triangle(keep):
    row = lax.broadcasted_iota(jnp.int32, (SB_BLOCK, SB_BLOCK), 0)
    col = lax.broadcasted_iota(jnp.int32, (SB_BLOCK, SB_BLOCK), 1)
    return _ones_where(keep(row, col))


SB_HALF = SB_BLOCK // 2


def _keep(on_diagonal, x):
    if not on_diagonal:
        return x
    row = lax.broadcasted_iota(jnp.int32, (SB_HALF, SB_HALF), 0)
    col = lax.broadcasted_iota(jnp.int32, (SB_HALF, SB_HALF), 1)
    return jnp.where(col < row, x, 0.0)


def _lower_quadrants(fn, n_out, *tiles):
    def quadrant(r, c):
        return [t[r * SB_HALF:(r + 1) * SB_HALF, c * SB_HALF:(c + 1) * SB_HALF] for t in tiles]

    top, left, bottom = fn(True, *quadrant(0, 0)), fn(False, *quadrant(1, 0)), fn(True, *quadrant(1, 1))
    zero = jnp.zeros((SB_HALF, SB_HALF), F32)
    return [jnp.concatenate([jnp.concatenate([top[o], zero], axis=1), jnp.concatenate([left[o], bottom[o]], axis=1)],
                            axis=0) for o in range(n_out)]


def _sb_fwd(proj, norm_g, mixed, shards, n_seq, seq):
    t = n_seq * seq
    nq = seq // SB_STEP
    q0, k0, v0 = 0, N_PAIRS, 2 * N_PAIRS
    n_w = len(shards)
    n_steps = N_PAIRS * n_seq * nq
    tri = _triangle(lambda row, col: row >= col)

    def body(q_ref, k_ref, v_ref, ng_ref, tri_ref, mixed_in, *rest):
        del mixed_in
        shard_refs = rest[:n_w]
        out_ref, opre_ref, ctot_ref = rest[n_w:n_w + 3]
        gathered = rest[n_w + 3:2 * n_w + 3]
        send_sems, recv_sems, local_sems = rest[2 * n_w + 3:]
        i = pl.program_id(2)
        step = (pl.program_id(0) * n_seq + pl.program_id(1)) * nq + i
        plan = _GatherPlan(shard_refs, gathered, send_sems, recv_sems, local_sems)

        @pl.when(step == 0)
        def _():
            plan.start()

        @pl.when(step == (3 * n_steps) // 4)
        def _():
            plan.forward()

        masks = _head_masks()
        suffix = tri_ref[...]
        qhs = [[(q_ref[blk, :] * SB_SCALE * m).astype(MXU_DTYPE) for m in masks] for blk in QUERY_BLOCKS]

        def tiles(js, work, carry):
            rows = [pl.ds(pl.multiple_of(j * SB_BLOCK, SB_BLOCK), SB_BLOCK) for j in js]
            ks = [k_ref[rw, :].astype(MXU_DTYPE) for rw in rows]
            vs = [v_ref[rw, :].astype(MXU_DTYPE) for rw in rows]
            zs = [[_dot_nt(qh, ks[ts]) for qh in qhs[qb]] for qb, ts, _ in work]
            ccs = [[_lower_quadrants(lambda on, zq: (_keep(on, _softplus(zq)),), 1, z)[0] if diag else _softplus(z)
                    for z in zw] for zw, (_, _, diag) in zip(zs, work)]
            sums = [[jnp.dot(cc.astype(BF16), suffix, preferred_element_type=F32) for cc in cw] for cw in ccs]
            out = [list(per_block) for per_block in carry]
            for h in range(len(masks)):
                for w, (qb, ts, diag) in enumerate(work):
                    run, acc = out[qb][h]
                    logit = zs[w][h] - (sums[w][h] + run)
                    if diag:
                        a = _lower_quadrants(lambda on, lq: (_keep(on, jnp.exp(lq)),), 1, logit)[0]
                    else:
                        a = jnp.exp(logit)
                    out[qb][h] = (run + sums[w][h][:, 0:1], acc + _dot(a, vs[ts]))
            return tuple(tuple(per_block) for per_block in out)

        start = ((jnp.zeros((SB_BLOCK, 1), F32), jnp.zeros((SB_BLOCK, LANES), F32)),) * 2
        carry = tiles([2 * i, 2 * i + 1], [(0, 0, True), (1, 1, True), (1, 0, False)], (start, start))
        both = [(0, 0, False), (0, 1, False), (1, 0, False), (1, 1, False)]
        carry = lax.fori_loop(0, i, lambda s, cy: tiles([2 * (i - s) - 1, 2 * (i - s) - 2], both, cy), carry)
        opre = jnp.concatenate([cb[0][1] * masks[0] + cb[1][1] * masks[1] for cb in carry], axis=0)
        ctot = jnp.concatenate([cb[0][0] * masks[0] + cb[1][0] * masks[1] for cb in carry], axis=0)
        opre_ref[...] = opre
        ctot_ref[...] = ctot
        out_ref[...] = (opre * _head_rstd(opre, masks) * ng_ref[...]).astype(out_ref.dtype)

        @pl.when(step == n_steps - 1)
        def _():
            plan.finish()

    qspec = pl.BlockSpec((SB_STEP, LANES), lambda p, b, i: (b * nq + i, q0 + p))
    ospec = pl.BlockSpec((SB_STEP, LANES), lambda p, b, i: (b * nq + i, p))
    hbm = pl.BlockSpec(memory_space=pltpu.HBM)
    outs = pl.pallas_call(
        body, name="sb_fwd", grid=(N_PAIRS, n_seq, nq),
        in_specs=[qspec,
                  pl.BlockSpec((seq, LANES), lambda p, b, i: (b, k0 + p)),
                  pl.BlockSpec((seq, LANES), lambda p, b, i: (b, v0 + p)),
                  pl.BlockSpec((1, LANES), lambda p, b, i: (0, p)),
                  pl.BlockSpec(tri.shape, lambda p, b, i: (0, 0)), hbm] + [hbm] * n_w,
        out_specs=[pl.BlockSpec((SB_STEP, LANES), lambda p, b, i: (b * nq + i, N_PAIRS + p)), ospec, ospec]
        + [hbm] * n_w,
        out_shape=[jax.ShapeDtypeStruct(mixed.shape, mixed.dtype), jax.ShapeDtypeStruct((t, GROUP), F32),
                   jax.ShapeDtypeStruct((t, GROUP), F32)]
        + [jax.ShapeDtypeStruct((N_DEV,) + s.shape, s.dtype) for s in shards],
        scratch_shapes=[pltpu.SemaphoreType.DMA((n_w * _GatherPlan.COPIES,)),
                        pltpu.SemaphoreType.DMA((n_w * _GatherPlan.COPIES,)), pltpu.SemaphoreType.DMA((n_w,))],
        input_output_aliases={5: 0},
        compiler_params=_params(3),
    )(proj, proj, proj, norm_g, tri, mixed, *shards)
    return outs[0], outs[1], outs[2], list(outs[3:])


def _sb_bwd(proj, norm_g, dmix, opre, ctot, grads, n_seq, seq):
    t = n_seq * seq
    nq = seq // SB_STEP
    q0, k0, v0 = 0, N_PAIRS, 2 * N_PAIRS
    n_w = len(grads)
    n_steps = N_PAIRS * n_seq * nq
    tri = _triangle(lambda row, col: row <= col)

    def body(q_ref, k_ref, v_ref, ng_ref, tri_ref, dm_ref, opre_ref, ctot_ref, *rest):
        grad_refs = rest[:n_w]
        dq_ref, dk_ref, dv_ref, dng_ref = rest[n_w:n_w + 4]
        lands = rest[n_w + 4:2 * n_w + 4]
        dk_acc, dv_acc, send_sems, recv_sems, local_sems = rest[2 * n_w + 4:]
        p_id, b_id, i = pl.program_id(0), pl.program_id(1), pl.program_id(2)
        step = (p_id * n_seq + b_id) * nq + i
        plan = _ScatterPlan(grad_refs, lands, send_sems, recv_sems, local_sems)

        @pl.when(step == 0)
        def _():
            plan.start()

        masks = _head_masks()
        upto = tri_ref[...]

        def prefix(x):
            return jnp.dot(x.astype(BF16), upto, preferred_element_type=F32)

        @pl.when(i == 0)
        def _():
            dk_acc[...] = jnp.zeros_like(dk_acc)
            dv_acc[...] = jnp.zeros_like(dv_acc)

        @pl.when(jnp.logical_and(b_id == 0, i == 0))
        def _():
            dng_ref[...] = jnp.zeros_like(dng_ref)

        o = opre_ref[...]
        rs = _head_rstd(o, masks)
        oh = o * rs
        dm = dm_ref[...]
        dng_ref[...] += jnp.sum(dm * oh, axis=0, keepdims=True)
        doh = dm * ng_ref[...]
        do = rs * (doh - oh * _head_mean(doh * oh, masks))

        heads = range(len(masks))
        qs = [q_ref[blk, :] * SB_SCALE for blk in QUERY_BLOCKS]
        dos = [do[blk] for blk in QUERY_BLOCKS]
        qhs = [[(q * m).astype(MXU_DTYPE) for m in masks] for q in qs]
        doms = [[(d * m).astype(MXU_DTYPE) for m in masks] for d in dos]
        head_rows = [jnp.where(jnp.right_shift(lax.broadcasted_iota(jnp.int32, (LANES, 1), 0), 6) == h, 1.0, 0.0)
                     for h in heads]
        qhts = [[(qt * hr).astype(MXU_DTYPE) for hr in head_rows] for qt in [q.astype(F32).T for q in qs]]
        domts = [[(dt * hr).astype(MXU_DTYPE) for hr in head_rows] for dt in [d.T for d in dos]]
        totals = [[ctot_ref[blk, h * D_HEAD:h * D_HEAD + 1] for h in heads] for blk in QUERY_BLOCKS]

        def tiles(js, work, carry):
            rows = [pl.ds(pl.multiple_of(j * SB_BLOCK, SB_BLOCK), SB_BLOCK) for j in js]
            ks = [k_ref[rw, :].astype(MXU_DTYPE) for rw in rows]
            vs = [v_ref[rw, :].astype(MXU_DTYPE) for rw in rows]
            zs = [[_dot_nt(qhs[qb][h], ks[ts]) for h in heads] for qb, ts, _ in work]
            das = [[_dot_nt(doms[qb][h], vs[ts]) for h in heads] for qb, ts, _ in work]
            def gates(on, zq):
                sp = _softplus(zq)
                return _keep(on, sp), zq - sp, _keep(on, jnp.exp(zq - sp))

            trio = [[_lower_quadrants(gates, 3, z) if diag else gates(False, z) for z in zw]
                    for zw, (_, _, diag) in zip(zs, work)]
            ccs, lsigs, sigs = ([[t[o] for t in tw] for tw in trio] for o in range(3))
            pres = [[prefix(cc) for cc in cw] for cw in ccs]
            out = [list(per_block) for per_block in carry]
            for h in heads:
                for w, (qb, ts, diag) in enumerate(work):
                    pc, pdl, dq_h = out[qb][h]
                    logit = lsigs[w][h] + pres[w][h] - (totals[qb][h] - pc)
                    if diag:
                        a = _lower_quadrants(lambda on, lq: (_keep(on, jnp.exp(lq)),), 1, logit)[0]
                    else:
                        a = jnp.exp(logit)
                    dl = a * das[w][h]
                    dv_acc[js[ts]] += _dot(domts[qb][h], a)
                    dpre = prefix(dl)
                    dz = dl - sigs[w][h] * (pdl + dpre)
                    dzb = dz.astype(MXU_DTYPE)
                    dk_acc[js[ts]] += _dot(qhts[qb][h], dzb)
                    out[qb][h] = (pc + pres[w][h][:, SB_BLOCK - 1:SB_BLOCK], pdl + dpre[:, SB_BLOCK - 1:SB_BLOCK],
                                  dq_h + _dot(dzb, ks[ts]))
            return tuple(tuple(per_block) for per_block in out)

        zero = jnp.zeros((SB_BLOCK, 1), F32)
        start = ((zero, zero, jnp.zeros((SB_BLOCK, LANES), F32)),) * 2
        both = [(0, 0, False), (0, 1, False), (1, 0, False), (1, 1, False)]
        carry = lax.fori_loop(0, i, lambda s, cy: tiles([2 * s, 2 * s + 1], both, cy), (start, start))
        carry = tiles([2 * i, 2 * i + 1], [(0, 0, True), (1, 0, False), (1, 1, True)], carry)
        dq = jnp.concatenate([cb[0][2] * masks[0] + cb[1][2] * masks[1] for cb in carry], axis=0)
        dq_ref[...] = (dq * SB_SCALE).astype(dq_ref.dtype)

        @pl.when(i == nq - 1)
        def _():
            for j in range(seq // SB_BLOCK):
                tile_rows = slice(j * SB_BLOCK, (j + 1) * SB_BLOCK)
                dk_ref[tile_rows, :] = dk_acc[j].T.astype(dk_ref.dtype)
                dv_ref[tile_rows, :] = dv_acc[j].T.astype(dv_ref.dtype)

        @pl.when(step == n_steps - 1)
        def _():
            plan.finish()

    qspec = pl.BlockSpec((SB_STEP, LANES), lambda p, b, i: (b * nq + i, q0 + p))
    ospec = pl.BlockSpec((SB_STEP, LANES), lambda p, b, i: (b * nq + i, p))
    dmspec = pl.BlockSpec((SB_STEP, LANES), lambda p, b, i: (b * nq + i, N_PAIRS + p))
    full = lambda k: pl.BlockSpec((seq, LANES), lambda p, b, i: (b, k + p))
    vec = pl.BlockSpec((1, LANES), lambda p, b, i: (0, p))
    hbm = pl.BlockSpec(memory_space=pltpu.HBM)
    piece = jax.ShapeDtypeStruct((t, GROUP), MXU_DTYPE)
    outs = pl.pallas_call(
        body, name="sb_bwd", grid=(N_PAIRS, n_seq, nq),
        in_specs=[qspec, full(k0), full(v0), vec, pl.BlockSpec(tri.shape, lambda p, b, i: (0, 0)), dmspec, ospec, ospec]
        + [hbm] * n_w,
        out_specs=[ospec, full(0), full(0), vec] + [hbm] * n_w,
        out_shape=[piece, piece, piece, jax.ShapeDtypeStruct((1, GROUP), F32)]
        + [jax.ShapeDtypeStruct(g.shape, g.dtype) for g in grads],
        scratch_shapes=[pltpu.VMEM((seq // SB_BLOCK, LANES, SB_BLOCK), F32),
                        pltpu.VMEM((seq // SB_BLOCK, LANES, SB_BLOCK), F32),
                        pltpu.SemaphoreType.DMA((n_w * (N_DEV - 1),)), pltpu.SemaphoreType.DMA((n_w * (N_DEV - 1),)),
                        pltpu.SemaphoreType.DMA((n_w,))],
        compiler_params=_params(3),
    )(proj, proj, proj, norm_g, tri, dmix, opre, ctot, *[pltpu.with_memory_space_constraint(g, pltpu.HBM) for g in grads])
    return outs[0], outs[1], outs[2], outs[3], list(outs[4:])


def _mesh_place():
    x, y, c = lax.axis_index("x"), lax.axis_index("y"), lax.axis_index("c")
    return x, y, c


def _peer(x, y, c, k):
    px = lax.rem(x + ((k >> 2) & 1), 2)
    py = lax.rem(y + ((k >> 1) & 1), 2)
    pc = lax.rem(c + (k & 1), 2)
    return (px, py, pc), 4 * px + 2 * py + pc


def _remote(src, dst, send_sem, recv_sem, to):
    return pltpu.make_async_remote_copy(src_ref=src, dst_ref=dst, send_sem=send_sem, recv_sem=recv_sem,
                                        device_id=to, device_id_type=pl.DeviceIdType.MESH)


class _GatherPlan:
    COPIES = 7

    def __init__(self, shards, gathered, send_sems, recv_sems, local_sems):
        x, y, c = _mesh_place()
        self.c = c
        self.me = (x, y, c)
        self.sibling = (x, y, 1 - c)
        self.chips = [(1 - x, y), (x, 1 - y), (1 - x, 1 - y)]
        self.tensors = list(zip(shards, gathered))
        self.send_sems, self.recv_sems, self.local_sems = send_sems, recv_sems, local_sems

    @staticmethod
    def _index(place):
        return 4 * place[0] + 2 * place[1] + place[2]

    def _copy(self, w, k, block, to, own=False):
        shard, gathered = self.tensors[w]
        slot = gathered.at[self._index(block)]
        n = w * self.COPIES + k
        return _remote(shard if own else slot, slot, self.send_sems.at[n], self.recv_sems.at[n], to)

    def _local(self, w):
        shard, gathered = self.tensors[w]
        return pltpu.make_async_copy(shard, gathered.at[self._index(self.me)], self.local_sems.at[w])

    def _first(self, w):
        return [self._copy(w, 0, self.me, self.sibling, own=True)] + [
            self._copy(w, 1 + j, self.me, (*chip, self.c), own=True) for j, chip in enumerate(self.chips)]

    def _passed(self, w):
        return [self._copy(w, 4 + j, (*chip, self.c), self.sibling) for j, chip in enumerate(self.chips)]

    def start(self):
        for w in range(len(self.tensors)):
            self._local(w).start()
            for cp in self._first(w):
                cp.start()

    def forward(self):
        for w in range(len(self.tensors)):
            passed = self._passed(w)
            for j, chip in enumerate(self.chips):
                self._copy(w, 1 + j, (*chip, self.c), self.me).wait_recv()
                passed[j].start()

    def finish(self):
        for w in range(len(self.tensors)):
            self._copy(w, 0, self.sibling, self.me).wait_recv()
            for j, chip in enumerate(self.chips):
                self._copy(w, 4 + j, (*chip, 1 - self.c), self.me).wait_recv()
            for cp in self._first(w) + self._passed(w):
                cp.wait_send()
            self._local(w).wait()


class _ScatterPlan:
    def __init__(self, grads, lands, send_sems, recv_sems, local_sems):
        self.place = _mesh_place()
        x, y, c = self.place
        self.me = 4 * x + 2 * y + c
        self.tensors = list(zip(grads, lands))
        self.send_sems, self.recv_sems, self.local_sems = send_sems, recv_sems, local_sems

    def _copies(self, w):
        grad, land = self.tensors[w]
        out = []
        for k in range(1, N_DEV):
            peer, pidx = _peer(*self.place, k)
            n = w * (N_DEV - 1) + k - 1
            sems = (self.send_sems.at[n], self.recv_sems.at[n], peer)
            out.append((_remote(grad.at[pidx], land.at[self.me], *sems), _remote(grad.at[pidx], land.at[pidx], *sems)))
        return out

    def _local(self, w):
        grad, land = self.tensors[w]
        return pltpu.make_async_copy(grad.at[self.me], land.at[self.me], self.local_sems.at[w])

    def start(self):
        for w in range(len(self.tensors)):
            self._local(w).start()
            for send, _ in self._copies(w):
                send.start()

    def finish(self):
        for w in range(len(self.tensors)):
            copies = self._copies(w)
            for _, arrival in copies:
                arrival.wait_recv()
            for send, _ in copies:
                send.wait_send()
            self._local(w).wait()


def _cast_shards(shards):
    def body(*refs):
        n = len(refs) // 2
        for src, dst in zip(refs[:n], refs[n:]):
            dst[...] = src[...].astype(dst.dtype)

    shapes = [jax.ShapeDtypeStruct(s.shape, BF16) for s in shards]
    return pl.pallas_call(
        body, name="cast_shards", grid=(1,),
        in_specs=[_whole(s) for s in shards], out_specs=[_whole(s) for s in shapes], out_shape=shapes,
        compiler_params=_params(1),
    )(*shards)


def _gather_w_in(shard, x, g, chunk=512):
    rows, cols = shard.shape
    t, d = x.shape
    chunk = min(chunk, t)
    n_chunks = t // chunk

    def body(w_ref, x_ref, g_ref, out_ref, h_ref, x_buf, send_sems, recv_sems, local_sems, x_sems):
        plan = _GatherPlan([w_ref], [out_ref], send_sems, recv_sems, local_sems)
        plan.start()

        def fetch(j):
            return pltpu.make_async_copy(x_ref.at[pl.ds(j * chunk, chunk)], x_buf.at[j % 2], x_sems.at[j % 2])

        fetch(0).start()
        gv = g_ref[...]
        for j in range(n_chunks):
            if j + 1 < n_chunks:
                fetch(j + 1).start()
            fetch(j).wait()
            xv = x_buf[j % 2]
            r = lax.rsqrt(jnp.mean(xv * xv, axis=-1, keepdims=True) + EPS)
            h_ref[j * chunk:(j + 1) * chunk, :] = (xv * r * gv).astype(h_ref.dtype)
        plan.forward()
        plan.finish()

    vmem = pl.BlockSpec(memory_space=pltpu.VMEM)
    hbm = pl.BlockSpec(memory_space=pltpu.HBM)
    return pl.pallas_call(
        body, name="gather_w_in", in_specs=[hbm, hbm, vmem], out_specs=[hbm, vmem],
        out_shape=[jax.ShapeDtypeStruct((N_DEV, rows, cols), shard.dtype), jax.ShapeDtypeStruct((t, d), MXU_DTYPE)],
        scratch_shapes=[pltpu.VMEM((2, chunk, d), F32), pltpu.SemaphoreType.DMA((_GatherPlan.COPIES,)),
                        pltpu.SemaphoreType.DMA((_GatherPlan.COPIES,)), pltpu.SemaphoreType.DMA((1,)),
                        pltpu.SemaphoreType.DMA((2,))],
        compiler_params=pltpu.CompilerParams(vmem_limit_bytes=VMEM_LIMIT),
    )(shard, x, g)


def _dh_norm_bwd(pairs, x, g, res, name, tm=512, after=(), next_w=None):
    m, d = x.shape
    tm = min(tm, m)
    n_p = len(pairs)
    n_steps = m // tm
    n_in = 2 * n_p + 3 + len(after) + (next_w is not None)

    def body(*refs):
        x_ref, g_ref, res_ref = refs[2 * n_p:2 * n_p + 3]
        dx_ref, dg_ref = refs[n_in:n_in + 2]

        @pl.when(pl.program_id(0) == 0)
        def _():
            dg_ref[...] = jnp.zeros_like(dg_ref)

        dh = None
        for q in range(n_p):
            part = _dot(refs[2 * q][...], refs[2 * q + 1][...])
            dh = part if dh is None else dh + part
        xv = x_ref[...]
        r = lax.rsqrt(jnp.mean(xv * xv, axis=-1, keepdims=True) + EPS)
        xh = xv * r
        dxh = dh * g_ref[...]
        dxv = res_ref[...] + r * (dxh - xh * jnp.mean(dxh * xh, axis=-1, keepdims=True))
        dx_ref[...] = dxv
        dg_ref[...] += jnp.sum(dh * xh, axis=0, keepdims=True)
        if next_w is not None:
            refs[n_in + 2][...] = _dot_nt(dxv, refs[n_in - 1][...])

    in_specs, args = [], []
    for a, w, r in pairs:
        k = a.shape[1]
        in_specs += [pl.BlockSpec((tm, k), lambda i: (i, 0)),
                     pl.BlockSpec((k, d), functools.partial(lambda i, r: (r, 0), r=r or 0),
                                  pipeline_mode=pl.Buffered(1))]
        args += [a, w]
    row = pl.BlockSpec((tm, d), lambda i: (i, 0))
    vec = pl.BlockSpec((1, d), lambda i: (0, 0))
    extra_in, extra_out, extra_shape = [], [], []
    if next_w is not None:
        e = next_w.shape[0]
        extra_in = [pl.BlockSpec((e, d), lambda i: (0, 0), pipeline_mode=pl.Buffered(1))]
        extra_out = [pl.BlockSpec((tm, e), lambda i: (i, 0))]
        extra_shape = [jax.ShapeDtypeStruct((m, e), F32)]
        args_tail = (next_w,)
    else:
        args_tail = ()
    return pl.pallas_call(
        body, name=name, grid=(n_steps,),
        in_specs=in_specs + [row, vec, row] + [pl.BlockSpec(memory_space=pl.ANY)] * len(after) + extra_in,
        out_specs=[row, vec] + extra_out,
        out_shape=[jax.ShapeDtypeStruct((m, d), F32), jax.ShapeDtypeStruct((1, d), F32)] + extra_shape,
        compiler_params=_params(1),
    )(*args, x, g, res, *after, *args_tail)


def _dw_exchange_start(pieces, b, name, tn=512, tt=1024):
    t, n = b.shape
    widths = [p.shape[1] for p in pieces]
    rows = sum(widths)
    slot = rows // N_DEV
    steps = t // tt
    n_col = n // tn
    n_p = len(pieces)

    def body(*refs):
        piece_refs, b_ref, land_ref = refs[:n_p], refs[n_p], refs[n_p + 1]
        send_sem, recv_sem, local_sem, g_ref, land_thru, token, acc, stage, stage_sem = refs[n_p + 2:]
        del land_thru
        j, s = pl.program_id(0), pl.program_id(1)
        token[...] = jnp.zeros_like(token)

        @pl.when(s == 0)
        def _():
            acc[...] = jnp.zeros_like(acc)

        bv = b_ref[...]
        off = 0
        for p_ref, width in zip(piece_refs, widths):
            acc[off:off + width, :] += _dot_tn(p_ref[...], bv)
            off += width

        x, y, c = _mesh_place()
        me = 4 * x + 2 * y + c
        for col in range(n_col):
            @pl.when((j == col) & (s == steps - 1))
            def _(col=col):
                cols = pl.ds(col * tn, tn)
                for d in range(N_DEV):
                    stage[d] = acc[d * slot:(d + 1) * slot, :].astype(stage.dtype)
                put = pltpu.make_async_copy(stage, g_ref.at[:, :, cols], stage_sem)
                put.start()
                put.wait()
                for k in range(1, N_DEV):
                    peer, pidx = _peer(x, y, c, k)
                    _remote(g_ref.at[pidx, :, cols], land_ref.at[me, :, cols], send_sem, recv_sem, peer).start()
                if col == n_col - 1:
                    pltpu.make_async_copy(g_ref.at[me], land_ref.at[me], local_sem).start()

    hbm = pl.BlockSpec(memory_space=pltpu.HBM)
    sem = pl.BlockSpec(memory_space=pltpu.SEMAPHORE)
    buf = pltpu.HBM((N_DEV, slot, n), BF16)
    return pl.pallas_call(
        body, name=name, grid=(n_col, steps),
        in_specs=[pl.BlockSpec((tt, width), lambda j, s: (s, 0)) for width in widths]
        + [pl.BlockSpec((tt, tn), lambda j, s: (s, j)), hbm],
        out_specs=(sem, sem, sem, hbm, hbm, pl.BlockSpec((8, LANES), lambda j, s: (0, 0))),
        out_shape=(pltpu.SemaphoreType.DMA(()), pltpu.SemaphoreType.DMA(()), pltpu.SemaphoreType.DMA(()), buf, buf,
                   jax.ShapeDtypeStruct((8, LANES), F32)),
        scratch_shapes=[pltpu.VMEM((rows, tn), F32), pltpu.VMEM((N_DEV, slot, tn), BF16),
                        pltpu.SemaphoreType.DMA(())],
        input_output_aliases={n_p + 1: 4},
        compiler_params=pltpu.CompilerParams(dimension_semantics=("arbitrary", "arbitrary"),
                                             vmem_limit_bytes=VMEM_LIMIT,
                                             has_side_effects=pltpu.SideEffectType.DATAFLOW_SIDE_EFFECTING),
    )(*pieces, b, pltpu.with_memory_space_constraint(lax.empty((N_DEV, slot, n), BF16), pltpu.HBM))


def _exchange_wait(send_sem, recv_sem, local_sem, grad, land, after, name):
    def body(g_ref, land_ref, send_sem, recv_sem, local_sem, *rest):
        x, y, c = _mesh_place()
        peer, _ = _peer(x, y, c, 1)
        others = pl.ds(0, N_DEV - 1)
        seven = _remote(g_ref.at[others], land_ref.at[others], send_sem, recv_sem, peer)
        seven.wait_send()
        seven.wait_recv()
        pltpu.make_async_copy(g_ref.at[0], land_ref.at[0], local_sem).wait()

    hbm = pl.BlockSpec(memory_space=pltpu.HBM)
    sem = pl.BlockSpec(memory_space=pltpu.SEMAPHORE)
    buf = pltpu.HBM(grad.shape, grad.dtype)
    return pl.pallas_call(
        body, name=name, out_shape=(buf, buf),
        in_specs=(hbm, hbm, sem, sem, sem) + (pl.BlockSpec(memory_space=pl.ANY),) * len(after), out_specs=(hbm, hbm),
        input_output_aliases={0: 0, 1: 1},
        compiler_params=pltpu.CompilerParams(has_side_effects=pltpu.SideEffectType.DATAFLOW_SIDE_EFFECTING),
    )(grad, land, send_sem, recv_sem, local_sem, *after)[1]


SMALL_LAYOUT = ((0, 0, 0, 0, D_MODEL), (1, 0, 1, 0, GROUP), (1, 1, 1, GROUP, GROUP), (2, 0, 2, 0, GROUP),
                (3, 0, 2, GROUP, GROUP), (4, 0, 3, 0, D_MODEL), (5, 0, 4, 0, D_MODEL))
LOSS_ROW = 5
N_SMALL = 6


def _place_small(dst, srcs):
    dst[...] = jnp.zeros_like(dst)
    for p, sr, dr, dc, width in SMALL_LAYOUT:
        dst[dr:dr + 1, dc:dc + width] = srcs[p][sr:sr + 1, :]


def _small_allreduce(grads, loss_part):
    def body(*refs):
        g_in, loss_in, sum_ref = refs[:N_SMALL], refs[N_SMALL], refs[N_SMALL + 1]
        pack, land, send_sems, recv_sems = refs[N_SMALL + 2:]
        _place_small(pack, g_in)
        pack[LOSS_ROW:LOSS_ROW + 1, 0:LANES] = loss_in[...]

        x, y, c = _mesh_place()
        me = 4 * x + 2 * y + c
        land[me] = pack[...]
        sends = []
        for k in range(1, N_DEV):
            peer, _ = _peer(x, y, c, k)
            cp = _remote(pack, land.at[me], send_sems.at[k - 1], recv_sems.at[k - 1], peer)
            cp.start()
            sends.append(cp)
        for k in range(1, N_DEV):
            peer, pidx = _peer(x, y, c, k)
            _remote(pack, land.at[pidx], send_sems.at[k - 1], recv_sems.at[k - 1], peer).wait_recv()
        for cp in sends:
            cp.wait_send()

        g = land[0]
        for d in range(1, N_DEV):
            g = g + land[d]
        sum_ref[...] = g

    args = [*grads, loss_part]
    packed = jax.ShapeDtypeStruct((SMALL_ROWS, D_MODEL), F32)
    return pl.pallas_call(
        body, name="small_allreduce", grid=(1,),
        in_specs=[_whole(a) for a in args], out_specs=_whole(packed), out_shape=packed,
        scratch_shapes=[pltpu.VMEM(packed.shape, F32), pltpu.VMEM((N_DEV,) + packed.shape, F32),
                        pltpu.SemaphoreType.DMA((N_DEV - 1,)), pltpu.SemaphoreType.DMA((N_DEV - 1,))],
        compiler_params=_params(1),
    )(*args)


def _small_adamw(gsum, ws, ms, vs):
    def body(*refs):
        g_ref = refs[0]
        params = [refs[1 + q * N_SMALL:1 + (q + 1) * N_SMALL] for q in range(3)]
        o0 = 1 + 3 * N_SMALL
        outs = [refs[o0 + q * N_SMALL:o0 + (q + 1) * N_SMALL] for q in range(4)]
        loss_out = refs[o0 + 4 * N_SMALL]
        wp, mp, vp = refs[o0 + 4 * N_SMALL + 1:]
        for dst, srcs in zip((wp, mp, vp), params):
            _place_small(dst, srcs)
        g = g_ref[...]
        delta, nm, nv = _adam(wp[...], g, mp[...], vp[...])
        for val, out in zip((g, delta, nm, nv), outs):
            for p, sr, dr, dc, width in SMALL_LAYOUT:
                out[p][sr:sr + 1, :] = val[dr:dr + 1, dc:dc + width]
        loss_out[...] = g[LOSS_ROW:LOSS_ROW + 1, 0:LANES]

    args = [gsum, *ws, *ms, *vs]
    shapes = [jax.ShapeDtypeStruct(w.shape, F32) for w in ws] * 4 + [jax.ShapeDtypeStruct((1, LANES), F32)]
    packed = pltpu.VMEM((SMALL_ROWS, D_MODEL), F32)
    outs = pl.pallas_call(
        body, name="small_adamw", grid=(1,),
        in_specs=[_whole(a) for a in args], out_specs=[_whole(s) for s in shapes], out_shape=shapes,
        scratch_shapes=[packed, packed, packed], compiler_params=_params(1),
    )(*args)
    return [outs[q * N_SMALL:(q + 1) * N_SMALL] for q in range(4)], outs[4 * N_SMALL]


def _adam(w, g, m, v):
    m = ADAM_B1 * m + (1.0 - ADAM_B1) * g
    v = ADAM_B2 * v + (1.0 - ADAM_B2) * (g * g)
    m_hat = m / (1.0 - ADAM_B1 ** ADAM_STEP)
    v_hat = v / (1.0 - ADAM_B2 ** ADAM_STEP)
    delta = -ADAM_LR * (m_hat / (jnp.sqrt(v_hat) + ADAM_EPS) + ADAM_WD * w)
    return delta, m, v


def _reduce_adamw(land, w, m, v, name, tr, after=()):
    _, rows, width = land.shape

    def body(land_ref, w_ref, m_ref, v_ref, *rest):
        g_ref, d_ref, nm_ref, nv_ref = rest[len(after):]
        g = land_ref[0].astype(F32)
        for d in range(1, N_DEV):
            g = g + land_ref[d].astype(F32)
        delta, nm, nv = _adam(w_ref[...], g, m_ref[...], v_ref[...])
        g_ref[...] = g
        d_ref[...] = delta
        nm_ref[...] = nm
        nv_ref[...] = nv

    row = pl.BlockSpec((tr, width), lambda i: (i, 0))
    out = jax.ShapeDtypeStruct((rows, width), F32)
    return pl.pallas_call(
        body, name=name, grid=(rows // tr,),
        in_specs=[pl.BlockSpec((N_DEV, tr, width), lambda i: (0, i, 0)), row, row, row]
        + [pl.BlockSpec(memory_space=pl.ANY)] * len(after),
        out_specs=[row, row, row, row], out_shape=[out, out, out, out],
        compiler_params=_params(1),
    )(land, w, m, v, *after)


def kernel(x, mix_norm_g, w_in, lower_bounds, hgrn_norm_g, sb_norm_g, w_out, ffn_norm_g, w_gate, w_up, w_down, final_norm_g, loss_target, m_mix_norm_g, m_w_in, m_lower_bounds, m_hgrn_norm_g, m_sb_norm_g, m_w_out, m_ffn_norm_g, m_w_gate, m_w_up, m_w_down, m_final_norm_g, v_mix_norm_g, v_w_in, v_lower_bounds, v_hgrn_norm_g, v_sb_norm_g, v_w_out, v_ffn_norm_g, v_w_gate, v_w_up, v_w_down, v_final_norm_g):
    n_seq, seq, d = x.shape
    t = n_seq * seq
    x2d = x.reshape(t, d)
    tgt = loss_target.reshape(t, d)
    final_g = final_norm_g.reshape(1, d)
    col_sharded = (True, False, True, True, False)

    def as_rows(ws):
        return [w[0].T if tr else w[0] for w, tr in zip(ws, col_sharded)]

    big_w = as_rows([w_in, w_out, w_gate, w_up, w_down])
    big_m = as_rows([m_w_in, m_w_out, m_w_gate, m_w_up, m_w_down])
    big_v = as_rows([v_w_in, v_w_out, v_w_gate, v_w_up, v_w_down])

    sh_in, sh_out, sh_gate, sh_up, sh_down = _cast_shards(big_w)
    wt_in, h1 = _gather_w_in(sh_in, x2d, mix_norm_g)
    wt_in = wt_in.reshape(IN_COLS, d)
    proj_h, proj_s = _mm_nt_split(h1, wt_in, "proj", 4 * GROUP, (F32, MXU_DTYPE))
    mixed, oa_pre, states = _hgrn_fwd(proj_h, lower_bounds, hgrn_norm_g, n_seq, seq)
    mixed, ob_pre, ctot, gathered = _sb_fwd(proj_s, sb_norm_g, mixed, [sh_out, sh_gate, sh_up, sh_down], n_seq, seq)
    wf_out = gathered[0].reshape(d, d)
    wt_gate = gathered[1].reshape(D_FF, d)
    wt_up = gathered[2].reshape(D_FF, d)
    wf_down = gathered[3].reshape(D_FF, d)
    x1, h2 = _mix_out_norm(mixed, wf_out, x2d, ffn_norm_g, "mix_out")
    gate, up, ff = _ffn_up(h2, wt_gate, wt_up, "ffn_up")
    dx2, dx2m, d_final_g, loss_part = _ffn_down_loss(ff, wf_down, x1, tgt, final_g, "ffn_down_loss")

    dgate, dup = _ffn_bwd_act(dx2m, wf_down, gate, up, "ffn_bwd_act")
    dw_down = _mm_tn(ff, dx2m, "dw_down", tk=1408, tn=1024).reshape(N_DEV, D_FF // N_DEV, d)
    dw_gate = _mm_tn(dgate, h2, "dw_gate", tk=1408, tn=1024).reshape(N_DEV, D_FF // N_DEV, d)
    dw_up = _mm_tn(dup, h2, "dw_up", tk=1408, tn=1024).reshape(N_DEV, D_FF // N_DEV, d)
    dx1, d_ffn_g, dmix = _dh_norm_bwd([(dgate, wt_gate, None), (dup, wt_up, None)], x1, ffn_norm_g, dx2, "dh_ffn",
                                      tm=512, next_w=wf_out)
    dw_out = _mm_tn(mixed, dx1, "dw_out", tk=1024, tn=1024).reshape(N_DEV, d // N_DEV, d)
    dsq, dsk, dsv, d_sb_g, lands = _sb_bwd(proj_s, sb_norm_g, dmix, ob_pre, ctot, [dw_out, dw_gate, dw_up, dw_down],
                                            n_seq, seq)
    dhq, dhf, dhi, dhg, d_lb, d_hgrn_g = _hgrn_bwd(proj_h, lower_bounds, hgrn_norm_g, dmix, oa_pre, states, n_seq,
                                                   seq)
    dproj = [dhq, dhf, dhi, dhg, dsq, dsk, dsv]
    send_sem, recv_sem, local_sem, dw_in, land_in, token = _dw_exchange_start(dproj, h1, "dw_in")
    dx, d_mix_g = _dh_norm_bwd([(piece, wt_in, k) for k, piece in enumerate(dproj)], x2d, mix_norm_g, dx1, "dh_mix",
                               after=(token,))

    tiles = {"in": 224, "out": 128, "gate": 176, "up": 176, "down": 176}
    keys = list(tiles)
    rest = [_reduce_adamw(land, w, m, v, "adamw_" + key, tr=tiles[key], after=(token,))
            for key, land, w, m, v in zip(keys[1:], lands, big_w[1:], big_m[1:], big_v[1:])]
    small_sum = _small_allreduce([d_mix_g, d_lb, d_hgrn_g, d_sb_g, d_ffn_g, d_final_g], loss_part)
    small, loss_row = _small_adamw(
        small_sum, [mix_norm_g, lower_bounds, hgrn_norm_g, sb_norm_g, ffn_norm_g, final_g],
        [m_mix_norm_g, m_lower_bounds, m_hgrn_norm_g, m_sb_norm_g, m_ffn_norm_g, m_final_norm_g.reshape(1, d)],
        [v_mix_norm_g, v_lower_bounds, v_hgrn_norm_g, v_sb_norm_g, v_ffn_norm_g, v_final_norm_g.reshape(1, d)])
    land_in = _exchange_wait(send_sem, recv_sem, local_sem, dw_in, land_in,
                             [dx, loss_row] + [res[0] for res in rest], "dw_in_await")
    big = [_reduce_adamw(land_in, big_w[0], big_m[0], big_v[0], "adamw_in", tr=tiles["in"])] + rest
    big = [[r.T if tr else r for r in res] for res, tr in zip(big, col_sharded)]

    outs = [loss_row[0, 0], dx.reshape(n_seq, seq, d)]
    for q in range(4):
        b_in, b_out, b_gate, b_up, b_down = [res[q][None] for res in big]
        s_mix, s_lb, s_hgrn, s_sb, s_ffn, s_final = small[q]
        outs += [s_mix, b_in, s_lb, s_hgrn, s_sb, b_out, s_ffn, b_gate, b_up, b_down, s_final.reshape(d)]
    return tuple(outs)
```

```python
import functools
import math

import jax
import jax.numpy as jnp
from jax import lax
from jax.experimental import pallas as pl
from jax.experimental.pallas import tpu as pltpu

F32 = jnp.float32
BF16 = jnp.bfloat16
MXU_DTYPE = BF16

EPS = 1e-6
D_MODEL = 1024
N_HEADS = 8
D_HEAD = 64
GROUP = N_HEADS * D_HEAD
IN_COLS = 7 * GROUP
D_FF = 2816
CHUNK = 64
LANES = 128
N_PAIRS = GROUP // LANES
SUPER = 256
SB_BLOCK = 256
N_DEV = 8

ADAM_LR = 0.001
ADAM_B1 = 0.9
ADAM_B2 = 0.999
ADAM_EPS = 1e-08
ADAM_WD = 0.01
ADAM_STEP = 10

SMALL_ROWS = 8
FF_TILE = D_FF // 2

VMEM_LIMIT = 48 * 1024 * 1024


def _params(n_axes, vmem=VMEM_LIMIT):
    return pltpu.CompilerParams(dimension_semantics=("arbitrary",) * n_axes, vmem_limit_bytes=vmem)


def _whole(a):
    return pl.BlockSpec(a.shape, functools.partial(lambda i, nd: (0,) * nd, nd=len(a.shape)))


def _dot(a, b):
    return jnp.dot(a.astype(MXU_DTYPE), b.astype(MXU_DTYPE), preferred_element_type=F32)


def _dot_nt(a, b):
    return lax.dot_general(a.astype(MXU_DTYPE), b.astype(MXU_DTYPE), (((1,), (1,)), ((), ())),
                           preferred_element_type=F32)


def _dot_tn(a, b):
    return lax.dot_general(a.astype(MXU_DTYPE), b.astype(MXU_DTYPE), (((0,), (0,)), ((), ())),
                           preferred_element_type=F32)


def _split(x, parts):
    out, r = [], x
    for _ in range(parts):
        h = r.astype(BF16)
        out.append(h)
        r = r - h.astype(F32)
    return out


def _rsum_left(u, x, parts):
    acc = None
    for h in _split(x, parts):
        d = jnp.dot(u, h, preferred_element_type=F32)
        acc = d if acc is None else acc + d
    return acc


def _ones_where(mask):
    return jnp.where(mask, 1.0, 0.0).astype(BF16)


def _sigmoid(x):
    return 1.0 / (1.0 + jnp.exp(-x))


def _softplus(x):
    return jnp.maximum(x, 0.0) + jnp.log(1.0 + jnp.exp(-jnp.abs(x)))


def _head_masks():
    lane = lax.broadcasted_iota(jnp.int32, (1, LANES), 1)
    return [jnp.where(lane < D_HEAD, 1.0, 0.0), jnp.where(lane >= D_HEAD, 1.0, 0.0)]


def _head_rstd(o, masks):
    sq = o * o
    r = None
    for m in masks:
        ms = jnp.sum(sq * m, axis=1, keepdims=True) * (1.0 / D_HEAD)
        t = lax.rsqrt(ms + EPS) * m
        r = t if r is None else r + t
    return r


def _head_mean(t, masks):
    out = None
    for m in masks:
        v = jnp.sum(t * m, axis=1, keepdims=True) * (1.0 / D_HEAD) * m
        out = v if out is None else out + v
    return out


def _mix_out_norm(a, w, res, g, name, tm=512):
    m, k = a.shape
    d = w.shape[1]
    tm = min(tm, m)

    def body(a_ref, w_ref, res_ref, g_ref, x_ref, h_ref):
        xv = res_ref[...] + _dot(a_ref[...], w_ref[...])
        x_ref[...] = xv
        r = lax.rsqrt(jnp.mean(xv * xv, axis=-1, keepdims=True) + EPS)
        h_ref[...] = (xv * r * g_ref[...]).astype(h_ref.dtype)

    row = pl.BlockSpec((tm, d), lambda i: (i, 0))
    return pl.pallas_call(
        body, name=name, grid=(m // tm,),
        in_specs=[pl.BlockSpec((tm, k), lambda i: (i, 0)), pl.BlockSpec((k, d), lambda i: (0, 0)), row,
                  pl.BlockSpec((1, d), lambda i: (0, 0))],
        out_specs=[row, row],
        out_shape=[jax.ShapeDtypeStruct((m, d), F32), jax.ShapeDtypeStruct((m, d), MXU_DTYPE)],
        compiler_params=_params(1),
    )(a, w, res, g)


def _ffn_down_loss(a, w, res, target, g, name, tm=512):
    m, k = a.shape
    d = w.shape[1]
    tm = min(tm, m)

    def body(a_ref, w_ref, res_ref, t_ref, g_ref, dx_ref, dxm_ref, dg_ref, loss_ref):
        xv = res_ref[...] + _dot(a_ref[...], w_ref[...])
        gv = g_ref[...]
        r = lax.rsqrt(jnp.mean(xv * xv, axis=-1, keepdims=True) + EPS)
        xh = xv * r
        e = xh * gv - t_ref[...]
        dy = e * (1.0 / d)
        dxh = dy * gv
        dxv = r * (dxh - xh * jnp.mean(dxh * xh, axis=-1, keepdims=True))
        dx_ref[...] = dxv
        dxm_ref[...] = dxv.astype(dxm_ref.dtype)

        @pl.when(pl.program_id(0) == 0)
        def _():
            dg_ref[...] = jnp.zeros_like(dg_ref)
            loss_ref[...] = jnp.zeros_like(loss_ref)

        dg_ref[...] += jnp.sum(dy * xh, axis=0, keepdims=True)
        part = 0.5 * jnp.sum(jnp.mean(e * e, axis=-1, keepdims=True), axis=0, keepdims=True)
        loss_ref[...] += jnp.broadcast_to(part, loss_ref.shape)

    row = pl.BlockSpec((tm, d), lambda i: (i, 0))
    vec = pl.BlockSpec((1, d), lambda i: (0, 0))
    return pl.pallas_call(
        body, name=name, grid=(m // tm,),
        in_specs=[pl.BlockSpec((tm, k), lambda i: (i, 0)), pl.BlockSpec((k, d), lambda i: (0, 0)), row, row, vec],
        out_specs=[row, row, vec, pl.BlockSpec((1, LANES), lambda i: (0, 0))],
        out_shape=[jax.ShapeDtypeStruct((m, d), F32), jax.ShapeDtypeStruct((m, d), MXU_DTYPE),
                   jax.ShapeDtypeStruct((1, d), F32), jax.ShapeDtypeStruct((1, LANES), F32)],
        compiler_params=_params(1),
    )(a, w, res, target, g)


def _mm_nt_split(a, w, name, split, dtypes, tm=1024):
    m, n = a.shape
    k = w.shape[0]

    def body(a_ref, w_ref, lo_ref, hi_ref):
        av = a_ref[...]
        lo_ref[...] = _dot_nt(av, w_ref[0:split, :]).astype(lo_ref.dtype)
        hi_ref[...] = _dot_nt(av, w_ref[split:k, :]).astype(hi_ref.dtype)

    return pl.pallas_call(
        body, name=name, grid=(m // tm,),
        in_specs=[pl.BlockSpec((tm, n), lambda i: (i, 0)),
                  pl.BlockSpec((k, n), lambda i: (0, 0), pipeline_mode=pl.Buffered(1))],
        out_specs=[pl.BlockSpec((tm, split), lambda i: (i, 0)), pl.BlockSpec((tm, k - split), lambda i: (i, 0))],
        out_shape=[jax.ShapeDtypeStruct((m, split), dtypes[0]), jax.ShapeDtypeStruct((m, k - split), dtypes[1])],
        compiler_params=_params(1),
    )(a, w)


def _mm_tn(a, b, name, tk, tn, tt=2048, out_dtype=BF16):
    t, k = a.shape
    n = b.shape[1]
    tt = min(tt, t)
    steps = t // tt

    def body(a_ref, b_ref, o_ref, acc):
        s = pl.program_id(2)

        @pl.when(s == 0)
        def _():
            acc[...] = jnp.zeros_like(acc)

        acc[...] += _dot_tn(a_ref[...], b_ref[...])

        @pl.when(s == steps - 1)
        def _():
            o_ref[...] = acc[...].astype(o_ref.dtype)

    return pl.pallas_call(
        body, name=name, grid=(k // tk, n // tn, steps),
        in_specs=[pl.BlockSpec((tt, tk), lambda i, j, s: (s, i)), pl.BlockSpec((tt, tn), lambda i, j, s: (s, j))],
        out_specs=pl.BlockSpec((tk, tn), lambda i, j, s: (i, j)),
        out_shape=jax.ShapeDtypeStruct((k, n), out_dtype),
        scratch_shapes=[pltpu.VMEM((tk, tn), F32)],
        compiler_params=_params(3),
    )(a, b)


def _ffn_up(h, wg_t, wu_t, name, tm=1024, tn=FF_TILE):
    m, k = h.shape
    n = wg_t.shape[0]
    tm = min(tm, m)

    def body(h_ref, wg_ref, wu_ref, gate_ref, up_ref, ff_ref):
        hv = h_ref[...]
        gate = _dot_nt(hv, wg_ref[...])
        up = _dot_nt(hv, wu_ref[...])
        gate_ref[...] = gate.astype(gate_ref.dtype)
        up_ref[...] = up.astype(up_ref.dtype)
        ff_ref[...] = (gate * _sigmoid(gate) * up).astype(ff_ref.dtype)

    wspec = pl.BlockSpec((tn, k), lambda j, i: (j, 0))
    ospec = pl.BlockSpec((tm, tn), lambda j, i: (i, j))
    return pl.pallas_call(
        body, name=name, grid=(n // tn, m // tm),
        in_specs=[pl.BlockSpec((tm, k), lambda j, i: (i, 0)), wspec, wspec],
        out_specs=[ospec, ospec, ospec],
        out_shape=[jax.ShapeDtypeStruct((m, n), MXU_DTYPE)] * 3,
        compiler_params=_params(2),
    )(h, wg_t, wu_t)


def _ffn_bwd_act(dx, wd, gate, up, name, tm=1024, tn=FF_TILE):
    m, k = dx.shape
    n = wd.shape[0]
    tm = min(tm, m)

    def body(dx_ref, wd_ref, gate_ref, up_ref, dgate_ref, dup_ref):
        dff = _dot_nt(dx_ref[...], wd_ref[...])
        gate = gate_ref[...].astype(F32)
        sg = _sigmoid(gate)
        dgate_ref[...] = (dff * up_ref[...].astype(F32) * sg * (1.0 + gate * (1.0 - sg))).astype(dgate_ref.dtype)
        dup_ref[...] = (dff * gate * sg).astype(dup_ref.dtype)

    ospec = pl.BlockSpec((tm, tn), lambda j, i: (i, j))
    return pl.pallas_call(
        body, name=name, grid=(n // tn, m // tm),
        in_specs=[pl.BlockSpec((tm, k), lambda j, i: (i, 0)), pl.BlockSpec((tn, k), lambda j, i: (j, 0)),
                  ospec, ospec],
        out_specs=[ospec, ospec],
        out_shape=[jax.ShapeDtypeStruct((m, n), MXU_DTYPE), jax.ShapeDtypeStruct((m, n), MXU_DTYPE)],
        compiler_params=_params(2),
    )(dx, wd, gate, up)


def _chunk_masks():
    r = lax.broadcasted_iota(jnp.int32, (SUPER, SUPER), 0)
    c = lax.broadcasted_iota(jnp.int32, (SUPER, SUPER), 1)
    same = jnp.right_shift(r, 6) == jnp.right_shift(c, 6)
    lower = jnp.logical_and(same, c <= r)
    upper = jnp.logical_and(same, c >= r)
    return same, lower, upper


def _head_block_mask():
    r = lax.broadcasted_iota(jnp.int32, (LANES, LANES), 0)
    c = lax.broadcasted_iota(jnp.int32, (LANES, LANES), 1)
    return jnp.where(jnp.right_shift(r, 6) == jnp.right_shift(c, 6), 1.0, 0.0)


def _lower_bound(lb_raw):
    return 1.0 / (1.0 + jnp.exp(lb_raw[1:2, :] - lb_raw[0:1, :]))


HGRN_UNROLL = 4
PER_SUPER = SUPER // CHUNK
CHUNK_ROWS = [slice(c * CHUNK, (c + 1) * CHUNK) for c in range(PER_SUPER)]


def _over_chunks(rows):
    return jnp.concatenate([jnp.broadcast_to(r, (CHUNK, LANES)) for r in rows], axis=0)


def _hgrn_gates(q, hf, lb, lower_b):
    sig = _sigmoid(hf)
    f = lb + (1.0 - lb) * sig
    k = 1.0 - f
    lf = jnp.log(f)
    b = _rsum_left(lower_b, lf, 2)
    ends = [b[cr.stop - 1:cr.stop, :] for cr in CHUNK_ROWS]
    eb = jnp.exp(b)
    enb = jnp.exp(-b)
    edb = jnp.exp(_over_chunks(ends) - b)
    decs = [jnp.exp(e) for e in ends]
    return sig, f, k, decs, eb, enb, edb, q * eb, k * enb, k * edb


def _hgrn_fwd(proj, lower_bounds, norm_g, n_seq, seq):
    t = n_seq * seq
    n_super = seq // SUPER
    n_chunks = seq // CHUNK

    def body(q_ref, f_ref, i_ref, g_ref, lb_ref, ng_ref, out_ref, opre_ref, st_ref):
        masks = _head_masks()
        _, lower, _ = _chunk_masks()
        lower_b = _ones_where(lower)
        bd = _head_block_mask()
        lb = _lower_bound(lb_ref[...])
        ng = ng_ref[...]

        def step(it, st):
            blocks = [HGRN_UNROLL * it + u for u in range(HGRN_UNROLL)]
            rows = [pl.ds(pl.multiple_of(sb * SUPER, SUPER), SUPER) for sb in blocks]
            vs = [i_ref[rw, :] for rw in rows]
            gates = [_hgrn_gates(q_ref[rw, :], f_ref[rw, :], lb, lower_b) for rw in rows]
            decs, qes, kes, kds = ([g[k] for g in gates] for k in (3, 7, 8, 9))
            scores = [[_dot_nt(qe * m, ke) for m in masks] for qe, ke in zip(qes, kes)]
            updates = [[_dot_tn(v[cr], kd[cr]) for cr in CHUNK_ROWS] for v, kd in zip(vs, kds)]
            states = [st]
            for dec_b, upd_b in zip(decs, updates):
                for dec, upd in zip(dec_b, upd_b):
                    states.append(states[-1] * dec + bd * upd)
            for u, sb in enumerate(blocks):
                for c in range(PER_SUPER):
                    st_ref[0, 0, sb * PER_SUPER + c] = states[u * PER_SUPER + c]
            intra = [[_dot(jnp.where(lower, p, 0.0), v) for p in sc] for sc, v in zip(scores, vs)]
            inter = [[_dot_nt(qe[cr], states[u * PER_SUPER + c]) for c, cr in enumerate(CHUNK_ROWS)]
                     for u, qe in enumerate(qes)]
            for rw, intra_b, inter_b in zip(rows, intra, inter):
                o = intra_b[0] * masks[0] + intra_b[1] * masks[1] + jnp.concatenate(inter_b, axis=0)
                opre_ref[rw, :] = o
                hg = g_ref[rw, :]
                on = o * _head_rstd(o, masks) * ng
                out_ref[rw, :] = (on * hg * _sigmoid(hg)).astype(out_ref.dtype)
            return states[-1]

        lax.fori_loop(0, n_super // HGRN_UNROLL, step, jnp.zeros((LANES, LANES), F32))

    def col(k):
        return pl.BlockSpec((seq, LANES), lambda p, b: (b, k * N_PAIRS + p))

    vec = lambda rows: pl.BlockSpec((rows, LANES), lambda p, b: (0, p))
    ospec = pl.BlockSpec((seq, LANES), lambda p, b: (b, p))
    return pl.pallas_call(
        body, name="hgrn_fwd", grid=(N_PAIRS, n_seq),
        in_specs=[col(0), col(1), col(2), col(3), vec(2), vec(1)],
        out_specs=[ospec, ospec,
                   pl.BlockSpec((1, 1, n_chunks, LANES, LANES), lambda p, b: (b, p, 0, 0, 0))],
        out_shape=[jax.ShapeDtypeStruct((t, 2 * GROUP), MXU_DTYPE), jax.ShapeDtypeStruct((t, GROUP), F32),
                   jax.ShapeDtypeStruct((n_seq, N_PAIRS, n_chunks, LANES, LANES), F32)],
        compiler_params=_params(2),
    )(proj, proj, proj, proj, lower_bounds, norm_g)


def _hgrn_bwd(proj, lower_bounds, norm_g, dmix, opre, states, n_seq, seq):
    t = n_seq * seq
    n_super = seq // SUPER
    n_chunks = seq // CHUNK
    per = SUPER // CHUNK

    def body(q_ref, f_ref, i_ref, g_ref, lb_ref, ng_ref, dm_ref, opre_ref, st_ref,
             dq_ref, df_ref, di_ref, dg_ref, dlb_ref, dng_ref):
        masks = _head_masks()
        _, lower, upper = _chunk_masks()
        lower_b, upper_b = _ones_where(lower), _ones_where(upper)
        bd = _head_block_mask()
        lb_raw = lb_ref[...]
        lb = _lower_bound(lb_raw)
        ng = ng_ref[...]

        @pl.when(pl.program_id(1) == 0)
        def _():
            dlb_ref[...] = jnp.zeros_like(dlb_ref)
            dng_ref[...] = jnp.zeros_like(dng_ref)

        def first_half(sb):
            rows = pl.ds(pl.multiple_of(sb * SUPER, SUPER), SUPER)
            q, hf, v, hg = q_ref[rows, :], f_ref[rows, :], i_ref[rows, :], g_ref[rows, :]
            sig, f, k, decs, eb, enb, edb, qe, ke, kd = _hgrn_gates(q, hf, lb, lower_b)
            o = opre_ref[rows, :]
            r = _head_rstd(o, masks)
            oh = o * r
            dm = dm_ref[rows, :]
            sg = _sigmoid(hg)
            dg_ref[rows, :] = (dm * oh * ng * sg * (1.0 + hg * (1.0 - sg))).astype(dg_ref.dtype)
            don = dm * hg * sg
            dng_ref[...] += jnp.sum(don * oh, axis=0, keepdims=True)
            doh = don * ng
            do = r * (doh - oh * _head_mean(doh * oh, masks))
            doms = [do * m for m in masks]
            qems = [qe * m for m in masks]
            scores = [_dot_nt(qem, ke) for qem in qems]
            dscores = [_dot_nt(dom, v) for dom in doms]
            prevs = [st_ref[0, 0, sb * per + c] for c in range(per)]
            dst_in = [_dot_tn(do[cr], qe[cr]) for cr in CHUNK_ROWS]
            dqe_i = [_dot(do[cr], prev) for cr, prev in zip(CHUNK_ROWS, prevs)]
            return dict(rows=rows, v=v, sig=sig, f=f, decs=decs, eb=eb, enb=enb, edb=edb, qe=qe, ke=ke, kd=kd,
                        doms=doms, qems=qems, scores=scores, dscores=dscores, prevs=prevs, dst_in=dst_in, dqe_i=dqe_i)

        def second_half(blk, dsts):
            v, qe, ke, kd = blk["v"], blk["qe"], blk["ke"], blk["kd"]
            ps = [jnp.where(lower, p, 0.0) for p in blk["scores"]]
            dps = [jnp.where(lower, dp, 0.0) for dp in blk["dscores"]]
            dqe_h = [_dot(dp, ke) for dp in dps]
            dke_h = [_dot_tn(dp, qem) for dp, qem in zip(dps, blk["qems"])]
            dv_h = [_dot_tn(p, dom) for p, dom in zip(ps, blk["doms"])]
            dus = [bd * d for d in dsts]
            dv_i = [_dot_nt(kd[cr], du) for cr, du in zip(CHUNK_ROWS, dus)]
            dkd_i = [_dot(v[cr], du) for cr, du in zip(CHUNK_ROWS, dus)]

            def finish():
                dqe = dqe_h[0] * masks[0] + dqe_h[1] * masks[1] + jnp.concatenate(blk["dqe_i"], axis=0)
                dke = dke_h[0] + dke_h[1]
                dv = dv_h[0] + dv_h[1] + jnp.concatenate(dv_i, axis=0)
                dkd = jnp.concatenate(dkd_i, axis=0)
                dk = dke * blk["enb"] + dkd * blk["edb"]
                db = dqe * qe - dke * ke - dkd * kd
                dkd_kd = dkd * kd
                dends = [jnp.sum(dkd_kd[cr], axis=0, keepdims=True)
                         + jnp.sum(dsts[c] * blk["prevs"][c], axis=0, keepdims=True) * blk["decs"][c]
                         for c, cr in enumerate(CHUNK_ROWS)]
                dlf = _rsum_left(upper_b, db, 2) + _over_chunks(dends)
                sig = blk["sig"]
                dfv = dlf / blk["f"] - dk
                rows = blk["rows"]
                dq_ref[rows, :] = (dqe * blk["eb"]).astype(dq_ref.dtype)
                di_ref[rows, :] = dv.astype(di_ref.dtype)
                df_ref[rows, :] = (dfv * (1.0 - lb) * sig * (1.0 - sig)).astype(df_ref.dtype)
                dlb = jnp.sum(dfv * (1.0 - sig), axis=0, keepdims=True)
                da0 = dlb * lb * (1.0 - lb)
                dlb_ref[0:1, :] += da0
                dlb_ref[1:2, :] -= da0

            return finish

        def step(it, dst):
            blocks = [first_half(n_super - 1 - HGRN_UNROLL * it - u) for u in range(HGRN_UNROLL)]
            all_dsts = []
            for blk in blocks:
                dsts = [None] * per
                for c in reversed(range(per)):
                    dsts[c] = dst
                    dst = bd * (dst * blk["decs"][c] + blk["dst_in"][c])
                all_dsts.append(dsts)
            for finish in [second_half(blk, dsts) for blk, dsts in zip(blocks, all_dsts)]:
                finish()
            return dst

        lax.fori_loop(0, n_super // HGRN_UNROLL, step, jnp.zeros((LANES, LANES), F32))

    def col(k):
        return pl.BlockSpec((seq, LANES), lambda p, b: (b, k * N_PAIRS + p))

    vec = lambda rows: pl.BlockSpec((rows, LANES), lambda p, b: (0, p))
    ospec = pl.BlockSpec((seq, LANES), lambda p, b: (b, p))
    piece = jax.ShapeDtypeStruct((t, GROUP), MXU_DTYPE)
    return pl.pallas_call(
        body, name="hgrn_bwd", grid=(N_PAIRS, n_seq),
        in_specs=[col(0), col(1), col(2), col(3), vec(2), vec(1), ospec, ospec,
                  pl.BlockSpec((1, 1, n_chunks, LANES, LANES), lambda p, b: (b, p, 0, 0, 0))],
        out_specs=[ospec, ospec, ospec, ospec, vec(2), vec(1)],
        out_shape=[piece, piece, piece, piece,
                   jax.ShapeDtypeStruct((2, GROUP), F32), jax.ShapeDtypeStruct((1, GROUP), F32)],
        compiler_params=_params(2),
    )(proj, proj, proj, proj, lower_bounds, norm_g, dmix, opre, states)


SB_SCALE = 1.0 / math.sqrt(D_HEAD)


SB_STEP = 2 * SB_BLOCK
QUERY_BLOCKS = (slice(0, SB_BLOCK), slice(SB_BLOCK, SB_STEP))


def _triangle(keep):
    row = lax.broadcasted_iota(jnp.int32, (SB_BLOCK, SB_BLOCK), 0)
    col = lax.broadcasted_iota(jnp.int32, (SB_BLOCK, SB_BLOCK), 1)
    return _ones_where(keep(row, col))


SB_HALF = SB_BLOCK // 2


def _keep(on_diagonal, x):
    if not on_diagonal:
        return x
    row = lax.broadcasted_iota(jnp.int32, (SB_HALF, SB_HALF), 0)
    col = lax.broadcasted_iota(jnp.int32, (SB_HALF, SB_HALF), 1)
    return jnp.where(col < row, x, 0.0)


def _lower_quadrants(fn, n_out, *tiles):
    def quadrant(r, c):
        return [t[r * SB_HALF:(r + 1) * SB_HALF, c * SB_HALF:(c + 1) * SB_HALF] for t in tiles]

    top, left, bottom = fn(True, *quadrant(0, 0)), fn(False, *quadrant(1, 0)), fn(True, *quadrant(1, 1))
    zero = jnp.zeros((SB_HALF, SB_HALF), F32)
    return [jnp.concatenate([jnp.concatenate([top[o], zero], axis=1), jnp.concatenate([left[o], bottom[o]], axis=1)],
                            axis=0) for o in range(n_out)]


def _sb_fwd(proj, norm_g, mixed, shards, n_seq, seq):
    t = n_seq * seq
    nq = seq // SB_STEP
    q0, k0, v0 = 0, N_PAIRS, 2 * N_PAIRS
    n_w = len(shards)
    n_steps = N_PAIRS * n_seq * nq
    tri = _triangle(lambda row, col: row >= col)

    def body(q_ref, k_ref, v_ref, ng_ref, tri_ref, mixed_in, *rest):
        del mixed_in
        shard_refs = rest[:n_w]
        out_ref, opre_ref, ctot_ref = rest[n_w:n_w + 3]
        gathered = rest[n_w + 3:2 * n_w + 3]
        send_sems, recv_sems, local_sems = rest[2 * n_w + 3:]
        i = pl.program_id(2)
        step = (pl.program_id(0) * n_seq + pl.program_id(1)) * nq + i
        plan = _GatherPlan(shard_refs, gathered, send_sems, recv_sems, local_sems)

        @pl.when(step == 0)
        def _():
            plan.start()

        @pl.when(step == (3 * n_steps) // 4)
        def _():
            plan.forward()

        masks = _head_masks()
        suffix = tri_ref[...]
        qhs = [[(q_ref[blk, :] * SB_SCALE * m).astype(MXU_DTYPE) for m in masks] for blk in QUERY_BLOCKS]

        def tiles(js, work, carry):
            rows = [pl.ds(pl.multiple_of(j * SB_BLOCK, SB_BLOCK), SB_BLOCK) for j in js]
            ks = [k_ref[rw, :].astype(MXU_DTYPE) for rw in rows]
            vs = [v_ref[rw, :].astype(MXU_DTYPE) for rw in rows]
            zs = [[_dot_nt(qh, ks[ts]) for qh in qhs[qb]] for qb, ts, _ in work]
            ccs = [[_lower_quadrants(lambda on, zq: (_keep(on, _softplus(zq)),), 1, z)[0] if diag else _softplus(z)
                    for z in zw] for zw, (_, _, diag) in zip(zs, work)]
            sums = [[jnp.dot(cc.astype(BF16), suffix, preferred_element_type=F32) for cc in cw] for cw in ccs]
            out = [list(per_block) for per_block in carry]
            for h in range(len(masks)):
                for w, (qb, ts, diag) in enumerate(work):
                    run, acc = out[qb][h]
                    logit = zs[w][h] - (sums[w][h] + run)
                    if diag:
                        a = _lower_quadrants(lambda on, lq: (_keep(on, jnp.exp(lq)),), 1, logit)[0]
                    else:
                        a = jnp.exp(logit)
                    out[qb][h] = (run + sums[w][h][:, 0:1], acc + _dot(a, vs[ts]))
            return tuple(tuple(per_block) for per_block in out)

        start = ((jnp.zeros((SB_BLOCK, 1), F32), jnp.zeros((SB_BLOCK, LANES), F32)),) * 2
        carry = tiles([2 * i, 2 * i + 1], [(0, 0, True), (1, 1, True), (1, 0, False)], (start, start))
        both = [(0, 0, False), (0, 1, False), (1, 0, False), (1, 1, False)]
        carry = lax.fori_loop(0, i, lambda s, cy: tiles([2 * (i - s) - 1, 2 * (i - s) - 2], both, cy), carry)
        opre = jnp.concatenate([cb[0][1] * masks[0] + cb[1][1] * masks[1] for cb in carry], axis=0)
        ctot = jnp.concatenate([cb[0][0] * masks[0] + cb[1][0] * masks[1] for cb in carry], axis=0)
        opre_ref[...] = opre
        ctot_ref[...] = ctot
        out_ref[...] = (opre * _head_rstd(opre, masks) * ng_ref[...]).astype(out_ref.dtype)

        @pl.when(step == n_steps - 1)
        def _():
            plan.finish()

    qspec = pl.BlockSpec((SB_STEP, LANES), lambda p, b, i: (b * nq + i, q0 + p))
    ospec = pl.BlockSpec((SB_STEP, LANES), lambda p, b, i: (b * nq + i, p))
    hbm = pl.BlockSpec(memory_space=pltpu.HBM)
    outs = pl.pallas_call(
        body, name="sb_fwd", grid=(N_PAIRS, n_seq, nq),
        in_specs=[qspec,
                  pl.BlockSpec((seq, LANES), lambda p, b, i: (b, k0 + p)),
                  pl.BlockSpec((seq, LANES), lambda p, b, i: (b, v0 + p)),
                  pl.BlockSpec((1, LANES), lambda p, b, i: (0, p)),
                  pl.BlockSpec(tri.shape, lambda p, b, i: (0, 0)), hbm] + [hbm] * n_w,
        out_specs=[pl.BlockSpec((SB_STEP, LANES), lambda p, b, i: (b * nq + i, N_PAIRS + p)), ospec, ospec]
        + [hbm] * n_w,
        out_shape=[jax.ShapeDtypeStruct(mixed.shape, mixed.dtype), jax.ShapeDtypeStruct((t, GROUP), F32),
                   jax.ShapeDtypeStruct((t, GROUP), F32)]
        + [jax.ShapeDtypeStruct((N_DEV,) + s.shape, s.dtype) for s in shards],
        scratch_shapes=[pltpu.SemaphoreType.DMA((n_w * _GatherPlan.COPIES,)),
                        pltpu.SemaphoreType.DMA((n_w * _GatherPlan.COPIES,)), pltpu.SemaphoreType.DMA((n_w,))],
        input_output_aliases={5: 0},
        compiler_params=_params(3),
    )(proj, proj, proj, norm_g, tri, mixed, *shards)
    return outs[0], outs[1], outs[2], list(outs[3:])


def _sb_bwd(proj, norm_g, dmix, opre, ctot, grads, n_seq, seq):
    t = n_seq * seq
    nq = seq // SB_STEP
    q0, k0, v0 = 0, N_PAIRS, 2 * N_PAIRS
    n_w = len(grads)
    n_steps = N_PAIRS * n_seq * nq
    tri = _triangle(lambda row, col: row <= col)

    def body(q_ref, k_ref, v_ref, ng_ref, tri_ref, dm_ref, opre_ref, ctot_ref, *rest):
        grad_refs = rest[:n_w]
        dq_ref, dk_ref, dv_ref, dng_ref = rest[n_w:n_w + 4]
        lands = rest[n_w + 4:2 * n_w + 4]
        dk_acc, dv_acc, send_sems, recv_sems, local_sems = rest[2 * n_w + 4:]
        p_id, b_id, i = pl.program_id(0), pl.program_id(1), pl.program_id(2)
        step = (p_id * n_seq + b_id) * nq + i
        plan = _ScatterPlan(grad_refs, lands, send_sems, recv_sems, local_sems)

        @pl.when(step == 0)
        def _():
            plan.start()

        masks = _head_masks()
        upto = tri_ref[...]

        def prefix(x):
            return jnp.dot(x.astype(BF16), upto, preferred_element_type=F32)

        @pl.when(i == 0)
        def _():
            dk_acc[...] = jnp.zeros_like(dk_acc)
            dv_acc[...] = jnp.zeros_like(dv_acc)

        @pl.when(jnp.logical_and(b_id == 0, i == 0))
        def _():
            dng_ref[...] = jnp.zeros_like(dng_ref)

        o = opre_ref[...]
        rs = _head_rstd(o, masks)
        oh = o * rs
        dm = dm_ref[...]
        dng_ref[...] += jnp.sum(dm * oh, axis=0, keepdims=True)
        doh = dm * ng_ref[...]
        do = rs * (doh - oh * _head_mean(doh * oh, masks))

        heads = range(len(masks))
        qs = [q_ref[blk, :] * SB_SCALE for blk in QUERY_BLOCKS]
        dos = [do[blk] for blk in QUERY_BLOCKS]
        qhs = [[(q * m).astype(MXU_DTYPE) for m in masks] for q in qs]
        doms = [[(d * m).astype(MXU_DTYPE) for m in masks] for d in dos]
        head_rows = [jnp.where(jnp.right_shift(lax.broadcasted_iota(jnp.int32, (LANES, 1), 0), 6) == h, 1.0, 0.0)
                     for h in heads]
        qhts = [[(qt * hr).astype(MXU_DTYPE) for hr in head_rows] for qt in [q.astype(F32).T for q in qs]]
        domts = [[(dt * hr).astype(MXU_DTYPE) for hr in head_rows] for dt in [d.T for d in dos]]
        totals = [[ctot_ref[blk, h * D_HEAD:h * D_HEAD + 1] for h in heads] for blk in QUERY_BLOCKS]

        def tiles(js, work, carry):
            rows = [pl.ds(pl.multiple_of(j * SB_BLOCK, SB_BLOCK), SB_BLOCK) for j in js]
            ks = [k_ref[rw, :].astype(MXU_DTYPE) for rw in rows]
            vs = [v_ref[rw, :].astype(MXU_DTYPE) for rw in rows]
            zs = [[_dot_nt(qhs[qb][h], ks[ts]) for h in heads] for qb, ts, _ in work]
            das = [[_dot_nt(doms[qb][h], vs[ts]) for h in heads] for qb, ts, _ in work]
            def gates(on, zq):
                sp = _softplus(zq)
                return _keep(on, sp), zq - sp, _keep(on, jnp.exp(zq - sp))

            trio = [[_lower_quadrants(gates, 3, z) if diag else gates(False, z) for z in zw]
                    for zw, (_, _, diag) in zip(zs, work)]
            ccs, lsigs, sigs = ([[t[o] for t in tw] for tw in trio] for o in range(3))
            pres = [[prefix(cc) for cc in cw] for cw in ccs]
            out = [list(per_block) for per_block in carry]
            for h in heads:
                for w, (qb, ts, diag) in enumerate(work):
                    pc, pdl, dq_h = out[qb][h]
                    logit = lsigs[w][h] + pres[w][h] - (totals[qb][h] - pc)
                    if diag:
                        a = _lower_quadrants(lambda on, lq: (_keep(on, jnp.exp(lq)),), 1, logit)[0]
                    else:
                        a = jnp.exp(logit)
                    dl = a * das[w][h]
                    dv_acc[js[ts]] += _dot(domts[qb][h], a)
                    dpre = prefix(dl)
                    dz = dl - sigs[w][h] * (pdl + dpre)
                    dzb = dz.astype(MXU_DTYPE)
                    dk_acc[js[ts]] += _dot(qhts[qb][h], dzb)
                    out[qb][h] = (pc + pres[w][h][:, SB_BLOCK - 1:SB_BLOCK], pdl + dpre[:, SB_BLOCK - 1:SB_BLOCK],
                                  dq_h + _dot(dzb, ks[ts]))
            return tuple(tuple(per_block) for per_block in out)

        zero = jnp.zeros((SB_BLOCK, 1), F32)
        start = ((zero, zero, jnp.zeros((SB_BLOCK, LANES), F32)),) * 2
        both = [(0, 0, False), (0, 1, False), (1, 0, False), (1, 1, False)]
        carry = lax.fori_loop(0, i, lambda s, cy: tiles([2 * s, 2 * s + 1], both, cy), (start, start))
        carry = tiles([2 * i, 2 * i + 1], [(0, 0, True), (1, 0, False), (1, 1, True)], carry)
        dq = jnp.concatenate([cb[0][2] * masks[0] + cb[1][2] * masks[1] for cb in carry], axis=0)
        dq_ref[...] = (dq * SB_SCALE).astype(dq_ref.dtype)

        @pl.when(i == nq - 1)
        def _():
            for j in range(seq // SB_BLOCK):
                tile_rows = slice(j * SB_BLOCK, (j + 1) * SB_BLOCK)
                dk_ref[tile_rows, :] = dk_acc[j].T.astype(dk_ref.dtype)
                dv_ref[tile_rows, :] = dv_acc[j].T.astype(dv_ref.dtype)

        @pl.when(step == n_steps - 1)
        def _():
            plan.finish()

    qspec = pl.BlockSpec((SB_STEP, LANES), lambda p, b, i: (b * nq + i, q0 + p))
    ospec = pl.BlockSpec((SB_STEP, LANES), lambda p, b, i: (b * nq + i, p))
    dmspec = pl.BlockSpec((SB_STEP, LANES), lambda p, b, i: (b * nq + i, N_PAIRS + p))
    full = lambda k: pl.BlockSpec((seq, LANES), lambda p, b, i: (b, k + p))
    vec = pl.BlockSpec((1, LANES), lambda p, b, i: (0, p))
    hbm = pl.BlockSpec(memory_space=pltpu.HBM)
    piece = jax.ShapeDtypeStruct((t, GROUP), MXU_DTYPE)
    outs = pl.pallas_call(
        body, name="sb_bwd", grid=(N_PAIRS, n_seq, nq),
        in_specs=[qspec, full(k0), full(v0), vec, pl.BlockSpec(tri.shape, lambda p, b, i: (0, 0)), dmspec, ospec, ospec]
        + [hbm] * n_w,
        out_specs=[ospec, full(0), full(0), vec] + [hbm] * n_w,
        out_shape=[piece, piece, piece, jax.ShapeDtypeStruct((1, GROUP), F32)]
        + [jax.ShapeDtypeStruct(g.shape, g.dtype) for g in grads],
        scratch_shapes=[pltpu.VMEM((seq // SB_BLOCK, LANES, SB_BLOCK), F32),
                        pltpu.VMEM((seq // SB_BLOCK, LANES, SB_BLOCK), F32),
                        pltpu.SemaphoreType.DMA((n_w * (N_DEV - 1),)), pltpu.SemaphoreType.DMA((n_w * (N_DEV - 1),)),
                        pltpu.SemaphoreType.DMA((n_w,))],
        compiler_params=_params(3),
    )(proj, proj, proj, norm_g, tri, dmix, opre, ctot, *[pltpu.with_memory_space_constraint(g, pltpu.HBM) for g in grads])
    return outs[0], outs[1], outs[2], outs[3], list(outs[4:])


def _mesh_place():
    x, y, c = lax.axis_index("x"), lax.axis_index("y"), lax.axis_index("c")
    return x, y, c


def _peer(x, y, c, k):
    px = lax.rem(x + ((k >> 2) & 1), 2)
    py = lax.rem(y + ((k >> 1) & 1), 2)
    pc = lax.rem(c + (k & 1), 2)
    return (px, py, pc), 4 * px + 2 * py + pc


def _remote(src, dst, send_sem, recv_sem, to):
    return pltpu.make_async_remote_copy(src_ref=src, dst_ref=dst, send_sem=send_sem, recv_sem=recv_sem,
                                        device_id=to, device_id_type=pl.DeviceIdType.MESH)


class _GatherPlan:
    COPIES = 7

    def __init__(self, shards, gathered, send_sems, recv_sems, local_sems):
        x, y, c = _mesh_place()
        self.c = c
        self.me = (x, y, c)
        self.sibling = (x, y, 1 - c)
        self.chips = [(1 - x, y), (x, 1 - y), (1 - x, 1 - y)]
        self.tensors = list(zip(shards, gathered))
        self.send_sems, self.recv_sems, self.local_sems = send_sems, recv_sems, local_sems

    @staticmethod
    def _index(place):
        return 4 * place[0] + 2 * place[1] + place[2]

    def _copy(self, w, k, block, to, own=False):
        shard, gathered = self.tensors[w]
        slot = gathered.at[self._index(block)]
        n = w * self.COPIES + k
        return _remote(shard if own else slot, slot, self.send_sems.at[n], self.recv_sems.at[n], to)

    def _local(self, w):
        shard, gathered = self.tensors[w]
        return pltpu.make_async_copy(shard, gathered.at[self._index(self.me)], self.local_sems.at[w])

    def _first(self, w):
        return [self._copy(w, 0, self.me, self.sibling, own=True)] + [
            self._copy(w, 1 + j, self.me, (*chip, self.c), own=True) for j, chip in enumerate(self.chips)]

    def _passed(self, w):
        return [self._copy(w, 4 + j, (*chip, self.c), self.sibling) for j, chip in enumerate(self.chips)]

    def start(self):
        for w in range(len(self.tensors)):
            self._local(w).start()
            for cp in self._first(w):
                cp.start()

    def forward(self):
        for w in range(len(self.tensors)):
            passed = self._passed(w)
            for j, chip in enumerate(self.chips):
                self._copy(w, 1 + j, (*chip, self.c), self.me).wait_recv()
                passed[j].start()

    def finish(self):
        for w in range(len(self.tensors)):
            self._copy(w, 0, self.sibling, self.me).wait_recv()
            for j, chip in enumerate(self.chips):
                self._copy(w, 4 + j, (*chip, 1 - self.c), self.me).wait_recv()
            for cp in self._first(w) + self._passed(w):
                cp.wait_send()
            self._local(w).wait()


class _ScatterPlan:
    def __init__(self, grads, lands, send_sems, recv_sems, local_sems):
        self.place = _mesh_place()
        x, y, c = self.place
        self.me = 4 * x + 2 * y + c
        self.tensors = list(zip(grads, lands))
        self.send_sems, self.recv_sems, self.local_sems = send_sems, recv_sems, local_sems

    def _copies(self, w):
        grad, land = self.tensors[w]
        out = []
        for k in range(1, N_DEV):
            peer, pidx = _peer(*self.place, k)
            n = w * (N_DEV - 1) + k - 1
            sems = (self.send_sems.at[n], self.recv_sems.at[n], peer)
            out.append((_remote(grad.at[pidx], land.at[self.me], *sems), _remote(grad.at[pidx], land.at[pidx], *sems)))
        return out

    def _local(self, w):
        grad, land = self.tensors[w]
        return pltpu.make_async_copy(grad.at[self.me], land.at[self.me], self.local_sems.at[w])

    def start(self):
        for w in range(len(self.tensors)):
            self._local(w).start()
            for send, _ in self._copies(w):
                send.start()

    def finish(self):
        for w in range(len(self.tensors)):
            copies = self._copies(w)
            for _, arrival in copies:
                arrival.wait_recv()
            for send, _ in copies:
                send.wait_send()
            self._local(w).wait()


def _cast_shards(shards):
    def body(*refs):
        n = len(refs) // 2
        for src, dst in zip(refs[:n], refs[n:]):
            dst[...] = src[...].astype(dst.dtype)

    shapes = [jax.ShapeDtypeStruct(s.shape, BF16) for s in shards]
    return pl.pallas_call(
        body, name="cast_shards", grid=(1,),
        in_specs=[_whole(s) for s in shards], out_specs=[_whole(s) for s in shapes], out_shape=shapes,
        compiler_params=_params(1),
    )(*shards)


def _gather_w_in(shard, x, g, chunk=512):
    rows, cols = shard.shape
    t, d = x.shape
    chunk = min(chunk, t)
    n_chunks = t // chunk

    def body(w_ref, x_ref, g_ref, out_ref, h_ref, x_buf, send_sems, recv_sems, local_sems, x_sems):
        plan = _GatherPlan([w_ref], [out_ref], send_sems, recv_sems, local_sems)
        plan.start()

        def fetch(j):
            return pltpu.make_async_copy(x_ref.at[pl.ds(j * chunk, chunk)], x_buf.at[j % 2], x_sems.at[j % 2])

        fetch(0).start()
        gv = g_ref[...]
        for j in range(n_chunks):
            if j + 1 < n_chunks:
                fetch(j + 1).start()
            fetch(j).wait()
            xv = x_buf[j % 2]
            r = lax.rsqrt(jnp.mean(xv * xv, axis=-1, keepdims=True) + EPS)
            h_ref[j * chunk:(j + 1) * chunk, :] = (xv * r * gv).astype(h_ref.dtype)
        plan.forward()
        plan.finish()

    vmem = pl.BlockSpec(memory_space=pltpu.VMEM)
    hbm = pl.BlockSpec(memory_space=pltpu.HBM)
    return pl.pallas_call(
        body, name="gather_w_in", in_specs=[hbm, hbm, vmem], out_specs=[hbm, vmem],
        out_shape=[jax.ShapeDtypeStruct((N_DEV, rows, cols), shard.dtype), jax.ShapeDtypeStruct((t, d), MXU_DTYPE)],
        scratch_shapes=[pltpu.VMEM((2, chunk, d), F32), pltpu.SemaphoreType.DMA((_GatherPlan.COPIES,)),
                        pltpu.SemaphoreType.DMA((_GatherPlan.COPIES,)), pltpu.SemaphoreType.DMA((1,)),
                        pltpu.SemaphoreType.DMA((2,))],
        compiler_params=pltpu.CompilerParams(vmem_limit_bytes=VMEM_LIMIT),
    )(shard, x, g)


def _dh_norm_bwd(pairs, x, g, res, name, tm=512, after=(), next_w=None):
    m, d = x.shape
    tm = min(tm, m)
    n_p = len(pairs)
    n_steps = m // tm
    n_in = 2 * n_p + 3 + len(after) + (next_w is not None)

    def body(*refs):
        x_ref, g_ref, res_ref = refs[2 * n_p:2 * n_p + 3]
        dx_ref, dg_ref = refs[n_in:n_in + 2]

        @pl.when(pl.program_id(0) == 0)
        def _():
            dg_ref[...] = jnp.zeros_like(dg_ref)

        dh = None
        for q in range(n_p):
            part = _dot(refs[2 * q][...], refs[2 * q + 1][...])
            dh = part if dh is None else dh + part
        xv = x_ref[...]
        r = lax.rsqrt(jnp.mean(xv * xv, axis=-1, keepdims=True) + EPS)
        xh = xv * r
        dxh = dh * g_ref[...]
        dxv = res_ref[...] + r * (dxh - xh * jnp.mean(dxh * xh, axis=-1, keepdims=True))
        dx_ref[...] = dxv
        dg_ref[...] += jnp.sum(dh * xh, axis=0, keepdims=True)
        if next_w is not None:
            refs[n_in + 2][...] = _dot_nt(dxv, refs[n_in - 1][...])

    in_specs, args = [], []
    for a, w, r in pairs:
        k = a.shape[1]
        in_specs += [pl.BlockSpec((tm, k), lambda i: (i, 0)),
                     pl.BlockSpec((k, d), functools.partial(lambda i, r: (r, 0), r=r or 0),
                                  pipeline_mode=pl.Buffered(1))]
        args += [a, w]
    row = pl.BlockSpec((tm, d), lambda i: (i, 0))
    vec = pl.BlockSpec((1, d), lambda i: (0, 0))
    extra_in, extra_out, extra_shape = [], [], []
    if next_w is not None:
        e = next_w.shape[0]
        extra_in = [pl.BlockSpec((e, d), lambda i: (0, 0), pipeline_mode=pl.Buffered(1))]
        extra_out = [pl.BlockSpec((tm, e), lambda i: (i, 0))]
        extra_shape = [jax.ShapeDtypeStruct((m, e), F32)]
        args_tail = (next_w,)
    else:
        args_tail = ()
    return pl.pallas_call(
        body, name=name, grid=(n_steps,),
        in_specs=in_specs + [row, vec, row] + [pl.BlockSpec(memory_space=pl.ANY)] * len(after) + extra_in,
        out_specs=[row, vec] + extra_out,
        out_shape=[jax.ShapeDtypeStruct((m, d), F32), jax.ShapeDtypeStruct((1, d), F32)] + extra_shape,
        compiler_params=_params(1),
    )(*args, x, g, res, *after, *args_tail)


def _dw_exchange_start(pieces, b, name, tn=512, tt=1024):
    t, n = b.shape
    widths = [p.shape[1] for p in pieces]
    rows = sum(widths)
    slot = rows // N_DEV
    steps = t // tt
    n_col = n // tn
    n_p = len(pieces)

    def body(*refs):
        piece_refs, b_ref, land_ref = refs[:n_p], refs[n_p], refs[n_p + 1]
        send_sem, recv_sem, local_sem, g_ref, land_thru, token, acc, stage, stage_sem = refs[n_p + 2:]
        del land_thru
        j, s = pl.program_id(0), pl.program_id(1)
        token[...] = jnp.zeros_like(token)

        @pl.when(s == 0)
        def _():
            acc[...] = jnp.zeros_like(acc)

        bv = b_ref[...]
        off = 0
        for p_ref, width in zip(piece_refs, widths):
            acc[off:off + width, :] += _dot_tn(p_ref[...], bv)
            off += width

        x, y, c = _mesh_place()
        me = 4 * x + 2 * y + c
        for col in range(n_col):
            @pl.when((j == col) & (s == steps - 1))
            def _(col=col):
                cols = pl.ds(col * tn, tn)
                for d in range(N_DEV):
                    stage[d] = acc[d * slot:(d + 1) * slot, :].astype(stage.dtype)
                put = pltpu.make_async_copy(stage, g_ref.at[:, :, cols], stage_sem)
                put.start()
                put.wait()
                for k in range(1, N_DEV):
                    peer, pidx = _peer(x, y, c, k)
                    _remote(g_ref.at[pidx, :, cols], land_ref.at[me, :, cols], send_sem, recv_sem, peer).start()
                if col == n_col - 1:
                    pltpu.make_async_copy(g_ref.at[me], land_ref.at[me], local_sem).start()

    hbm = pl.BlockSpec(memory_space=pltpu.HBM)
    sem = pl.BlockSpec(memory_space=pltpu.SEMAPHORE)
    buf = pltpu.HBM((N_DEV, slot, n), BF16)
    return pl.pallas_call(
        body, name=name, grid=(n_col, steps),
        in_specs=[pl.BlockSpec((tt, width), lambda j, s: (s, 0)) for width in widths]
        + [pl.BlockSpec((tt, tn), lambda j, s: (s, j)), hbm],
        out_specs=(sem, sem, sem, hbm, hbm, pl.BlockSpec((8, LANES), lambda j, s: (0, 0))),
        out_shape=(pltpu.SemaphoreType.DMA(()), pltpu.SemaphoreType.DMA(()), pltpu.SemaphoreType.DMA(()), buf, buf,
                   jax.ShapeDtypeStruct((8, LANES), F32)),
        scratch_shapes=[pltpu.VMEM((rows, tn), F32), pltpu.VMEM((N_DEV, slot, tn), BF16),
                        pltpu.SemaphoreType.DMA(())],
        input_output_aliases={n_p + 1: 4},
        compiler_params=pltpu.CompilerParams(dimension_semantics=("arbitrary", "arbitrary"),
                                             vmem_limit_bytes=VMEM_LIMIT,
                                             has_side_effects=pltpu.SideEffectType.DATAFLOW_SIDE_EFFECTING),
    )(*pieces, b, pltpu.with_memory_space_constraint(lax.empty((N_DEV, slot, n), BF16), pltpu.HBM))


def _exchange_wait(send_sem, recv_sem, local_sem, grad, land, after, name):
    def body(g_ref, land_ref, send_sem, recv_sem, local_sem, *rest):
        x, y, c = _mesh_place()
        peer, _ = _peer(x, y, c, 1)
        others = pl.ds(0, N_DEV - 1)
        seven = _remote(g_ref.at[others], land_ref.at[others], send_sem, recv_sem, peer)
        seven.wait_send()
        seven.wait_recv()
        pltpu.make_async_copy(g_ref.at[0], land_ref.at[0], local_sem).wait()

    hbm = pl.BlockSpec(memory_space=pltpu.HBM)
    sem = pl.BlockSpec(memory_space=pltpu.SEMAPHORE)
    buf = pltpu.HBM(grad.shape, grad.dtype)
    return pl.pallas_call(
        body, name=name, out_shape=(buf, buf),
        in_specs=(hbm, hbm, sem, sem, sem) + (pl.BlockSpec(memory_space=pl.ANY),) * len(after), out_specs=(hbm, hbm),
        input_output_aliases={0: 0, 1: 1},
        compiler_params=pltpu.CompilerParams(has_side_effects=pltpu.SideEffectType.DATAFLOW_SIDE_EFFECTING),
    )(grad, land, send_sem, recv_sem, local_sem, *after)[1]


SMALL_LAYOUT = ((0, 0, 0, 0, D_MODEL), (1, 0, 1, 0, GROUP), (1, 1, 1, GROUP, GROUP), (2, 0, 2, 0, GROUP),
                (3, 0, 2, GROUP, GROUP), (4, 0, 3, 0, D_MODEL), (5, 0, 4, 0, D_MODEL))
LOSS_ROW = 5
N_SMALL = 6


def _place_small(dst, srcs):
    dst[...] = jnp.zeros_like(dst)
    for p, sr, dr, dc, width in SMALL_LAYOUT:
        dst[dr:dr + 1, dc:dc + width] = srcs[p][sr:sr + 1, :]


def _small_allreduce(grads, loss_part):
    def body(*refs):
        g_in, loss_in, sum_ref = refs[:N_SMALL], refs[N_SMALL], refs[N_SMALL + 1]
        pack, land, send_sems, recv_sems = refs[N_SMALL + 2:]
        _place_small(pack, g_in)
        pack[LOSS_ROW:LOSS_ROW + 1, 0:LANES] = loss_in[...]

        x, y, c = _mesh_place()
        me = 4 * x + 2 * y + c
        land[me] = pack[...]
        sends = []
        for k in range(1, N_DEV):
            peer, _ = _peer(x, y, c, k)
            cp = _remote(pack, land.at[me], send_sems.at[k - 1], recv_sems.at[k - 1], peer)
            cp.start()
            sends.append(cp)
        for k in range(1, N_DEV):
            peer, pidx = _peer(x, y, c, k)
            _remote(pack, land.at[pidx], send_sems.at[k - 1], recv_sems.at[k - 1], peer).wait_recv()
        for cp in sends:
            cp.wait_send()

        g = land[0]
        for d in range(1, N_DEV):
            g = g + land[d]
        sum_ref[...] = g

    args = [*grads, loss_part]
    packed = jax.ShapeDtypeStruct((SMALL_ROWS, D_MODEL), F32)
    return pl.pallas_call(
        body, name="small_allreduce", grid=(1,),
        in_specs=[_whole(a) for a in args], out_specs=_whole(packed), out_shape=packed,
        scratch_shapes=[pltpu.VMEM(packed.shape, F32), pltpu.VMEM((N_DEV,) + packed.shape, F32),
                        pltpu.SemaphoreType.DMA((N_DEV - 1,)), pltpu.SemaphoreType.DMA((N_DEV - 1,))],
        compiler_params=_params(1),
    )(*args)


def _small_adamw(gsum, ws, ms, vs):
    def body(*refs):
        g_ref = refs[0]
        params = [refs[1 + q * N_SMALL:1 + (q + 1) * N_SMALL] for q in range(3)]
        o0 = 1 + 3 * N_SMALL
        outs = [refs[o0 + q * N_SMALL:o0 + (q + 1) * N_SMALL] for q in range(4)]
        loss_out = refs[o0 + 4 * N_SMALL]
        wp, mp, vp = refs[o0 + 4 * N_SMALL + 1:]
        for dst, srcs in zip((wp, mp, vp), params):
            _place_small(dst, srcs)
        g = g_ref[...]
        delta, nm, nv = _adam(wp[...], g, mp[...], vp[...])
        for val, out in zip((g, delta, nm, nv), outs):
            for p, sr, dr, dc, width in SMALL_LAYOUT:
                out[p][sr:sr + 1, :] = val[dr:dr + 1, dc:dc + width]
        loss_out[...] = g[LOSS_ROW:LOSS_ROW + 1, 0:LANES]

    args = [gsum, *ws, *ms, *vs]
    shapes = [jax.ShapeDtypeStruct(w.shape, F32) for w in ws] * 4 + [jax.ShapeDtypeStruct((1, LANES), F32)]
    packed = pltpu.VMEM((SMALL_ROWS, D_MODEL), F32)
    outs = pl.pallas_call(
        body, name="small_adamw", grid=(1,),
        in_specs=[_whole(a) for a in args], out_specs=[_whole(s) for s in shapes], out_shape=shapes,
        scratch_shapes=[packed, packed, packed], compiler_params=_params(1),
    )(*args)
    return [outs[q * N_SMALL:(q + 1) * N_SMALL] for q in range(4)], outs[4 * N_SMALL]


def _adam(w, g, m, v):
    m = ADAM_B1 * m + (1.0 - ADAM_B1) * g
    v = ADAM_B2 * v + (1.0 - ADAM_B2) * (g * g)
    m_hat = m / (1.0 - ADAM_B1 ** ADAM_STEP)
    v_hat = v / (1.0 - ADAM_B2 ** ADAM_STEP)
    delta = -ADAM_LR * (m_hat / (jnp.sqrt(v_hat) + ADAM_EPS) + ADAM_WD * w)
    return delta, m, v


def _reduce_adamw(land, w, m, v, name, tr, after=()):
    _, rows, width = land.shape

    def body(land_ref, w_ref, m_ref, v_ref, *rest):
        g_ref, d_ref, nm_ref, nv_ref = rest[len(after):]
        g = land_ref[0].astype(F32)
        for d in range(1, N_DEV):
            g = g + land_ref[d].astype(F32)
        delta, nm, nv = _adam(w_ref[...], g, m_ref[...], v_ref[...])
        g_ref[...] = g
        d_ref[...] = delta
        nm_ref[...] = nm
        nv_ref[...] = nv

    row = pl.BlockSpec((tr, width), lambda i: (i, 0))
    out = jax.ShapeDtypeStruct((rows, width), F32)
    return pl.pallas_call(
        body, name=name, grid=(rows // tr,),
        in_specs=[pl.BlockSpec((N_DEV, tr, width), lambda i: (0, i, 0)), row, row, row]
        + [pl.BlockSpec(memory_space=pl.ANY)] * len(after),
        out_specs=[row, row, row, row], out_shape=[out, out, out, out],
        compiler_params=_params(1),
    )(land, w, m, v, *after)


def kernel(x, mix_norm_g, w_in, lower_bounds, hgrn_norm_g, sb_norm_g, w_out, ffn_norm_g, w_gate, w_up, w_down, final_norm_g, loss_target, m_mix_norm_g, m_w_in, m_lower_bounds, m_hgrn_norm_g, m_sb_norm_g, m_w_out, m_ffn_norm_g, m_w_gate, m_w_up, m_w_down, m_final_norm_g, v_mix_norm_g, v_w_in, v_lower_bounds, v_hgrn_norm_g, v_sb_norm_g, v_w_out, v_ffn_norm_g, v_w_gate, v_w_up, v_w_down, v_final_norm_g):
    n_seq, seq, d = x.shape
    t = n_seq * seq
    x2d = x.reshape(t, d)
    tgt = loss_target.reshape(t, d)
    final_g = final_norm_g.reshape(1, d)
    col_sharded = (True, False, True, True, False)

    def as_rows(ws):
        return [w[0].T if tr else w[0] for w, tr in zip(ws, col_sharded)]

    big_w = as_rows([w_in, w_out, w_gate, w_up, w_down])
    big_m = as_rows([m_w_in, m_w_out, m_w_gate, m_w_up, m_w_down])
    big_v = as_rows([v_w_in, v_w_out, v_w_gate, v_w_up, v_w_down])

    sh_in, sh_out, sh_gate, sh_up, sh_down = _cast_shards(big_w)
    wt_in, h1 = _gather_w_in(sh_in, x2d, mix_norm_g)
    wt_in = wt_in.reshape(IN_COLS, d)
    proj_h, proj_s = _mm_nt_split(h1, wt_in, "proj", 4 * GROUP, (F32, MXU_DTYPE))
    mixed, oa_pre, states = _hgrn_fwd(proj_h, lower_bounds, hgrn_norm_g, n_seq, seq)
    mixed, ob_pre, ctot, gathered = _sb_fwd(proj_s, sb_norm_g, mixed, [sh_out, sh_gate, sh_up, sh_down], n_seq, seq)
    wf_out = gathered[0].reshape(d, d)
    wt_gate = gathered[1].reshape(D_FF, d)
    wt_up = gathered[2].reshape(D_FF, d)
    wf_down = gathered[3].reshape(D_FF, d)
    x1, h2 = _mix_out_norm(mixed, wf_out, x2d, ffn_norm_g, "mix_out")
    gate, up, ff = _ffn_up(h2, wt_gate, wt_up, "ffn_up")
    dx2, dx2m, d_final_g, loss_part = _ffn_down_loss(ff, wf_down, x1, tgt, final_g, "ffn_down_loss")

    dgate, dup = _ffn_bwd_act(dx2m, wf_down, gate, up, "ffn_bwd_act")
    dw_down = _mm_tn(ff, dx2m, "dw_down", tk=1408, tn=1024).reshape(N_DEV, D_FF // N_DEV, d)
    dw_gate = _mm_tn(dgate, h2, "dw_gate", tk=1408, tn=1024).reshape(N_DEV, D_FF // N_DEV, d)
    dw_up = _mm_tn(dup, h2, "dw_up", tk=1408, tn=1024).reshape(N_DEV, D_FF // N_DEV, d)
    dx1, d_ffn_g, dmix = _dh_norm_bwd([(dgate, wt_gate, None), (dup, wt_up, None)], x1, ffn_norm_g, dx2, "dh_ffn",
                                      tm=512, next_w=wf_out)
    dw_out = _mm_tn(mixed, dx1, "dw_out", tk=1024, tn=1024).reshape(N_DEV, d // N_DEV, d)
    dsq, dsk, dsv, d_sb_g, lands = _sb_bwd(proj_s, sb_norm_g, dmix, ob_pre, ctot, [dw_out, dw_gate, dw_up, dw_down],
                                            n_seq, seq)
    dhq, dhf, dhi, dhg, d_lb, d_hgrn_g = _hgrn_bwd(proj_h, lower_bounds, hgrn_norm_g, dmix, oa_pre, states, n_seq,
                                                   seq)
    dproj = [dhq, dhf, dhi, dhg, dsq, dsk, dsv]
    send_sem, recv_sem, local_sem, dw_in, land_in, token = _dw_exchange_start(dproj, h1, "dw_in")
    dx, d_mix_g = _dh_norm_bwd([(piece, wt_in, k) for k, piece in enumerate(dproj)], x2d, mix_norm_g, dx1, "dh_mix",
                               after=(token,))

    tiles = {"in": 224, "out": 128, "gate": 176, "up": 176, "down": 176}
    keys = list(tiles)
    rest = [_reduce_adamw(land, w, m, v, "adamw_" + key, tr=tiles[key], after=(token,))
            for key, land, w, m, v in zip(keys[1:], lands, big_w[1:], big_m[1:], big_v[1:])]
    small_sum = _small_allreduce([d_mix_g, d_lb, d_hgrn_g, d_sb_g, d_ffn_g, d_final_g], loss_part)
    small, loss_row = _small_adamw(
        small_sum, [mix_norm_g, lower_bounds, hgrn_norm_g, sb_norm_g, ffn_norm_g, final_g],
        [m_mix_norm_g, m_lower_bounds, m_hgrn_norm_g, m_sb_norm_g, m_ffn_norm_g, m_final_norm_g.reshape(1, d)],
        [v_mix_norm_g, v_lower_bounds, v_hgrn_norm_g, v_sb_norm_g, v_ffn_norm_g, v_final_norm_g.reshape(1, d)])
    land_in = _exchange_wait(send_sem, recv_sem, local_sem, dw_in, land_in,
                             [dx, loss_row] + [res[0] for res in rest], "dw_in_await")
    big = [_reduce_adamw(land_in, big_w[0], big_m[0], big_v[0], "adamw_in", tr=tiles["in"])] + rest
    big = [[r.T if tr else r for r in res] for res, tr in zip(big, col_sharded)]

    outs = [loss_row[0, 0], dx.reshape(n_seq, seq, d)]
    for q in range(4):
        b_in, b_out, b_gate, b_up, b_down = [res[q][None] for res in big]
        s_mix, s_lb, s_hgrn, s_sb, s_ffn, s_final = small[q]
        outs += [s_mix, b_in, s_lb, s_hgrn, s_sb, b_out, s_ffn, b_gate, b_up, b_down, s_final.reshape(d)]
    return tuple(outs)
```

```python
import functools
import math

import jax
import jax.numpy as jnp
from jax import lax
from jax.experimental import pallas as pl
from jax.experimental.pallas import tpu as pltpu

F32 = jnp.float32
BF16 = jnp.bfloat16
MXU_DTYPE = BF16

EPS = 1e-6
D_MODEL = 1024
N_HEADS = 8
D_HEAD = 64
GROUP = N_HEADS * D_HEAD
IN_COLS = 7 * GROUP
D_FF = 2816
CHUNK = 64
LANES = 128
N_PAIRS = GROUP // LANES
SUPER = 256
SB_BLOCK = 256
N_DEV = 8

ADAM_LR = 0.001
ADAM_B1 = 0.9
ADAM_B2 = 0.999
ADAM_EPS = 1e-08
ADAM_WD = 0.01
ADAM_STEP = 10

SMALL_ROWS = 8
FF_TILE = D_FF // 2

VMEM_LIMIT = 48 * 1024 * 1024


def _params(n_axes, vmem=VMEM_LIMIT):
    return pltpu.CompilerParams(dimension_semantics=("arbitrary",) * n_axes, vmem_limit_bytes=vmem)


def _whole(a):
    return pl.BlockSpec(a.shape, functools.partial(lambda i, nd: (0,) * nd, nd=len(a.shape)))


def _dot(a, b):
    return jnp.dot(a.astype(MXU_DTYPE), b.astype(MXU_DTYPE), preferred_element_type=F32)


def _dot_nt(a, b):
    return lax.dot_general(a.astype(MXU_DTYPE), b.astype(MXU_DTYPE), (((1,), (1,)), ((), ())),
                           preferred_element_type=F32)


def _dot_tn(a, b):
    return lax.dot_general(a.astype(MXU_DTYPE), b.astype(MXU_DTYPE), (((0,), (0,)), ((), ())),
                           preferred_element_type=F32)


def _split(x, parts):
    out, r = [], x
    for _ in range(parts):
        h = r.astype(BF16)
        out.append(h)
        r = r - h.astype(F32)
    return out


def _rsum_left(u, x, parts):
    acc = None
    for h in _split(x, parts):
        d = jnp.dot(u, h, preferred_element_type=F32)
        acc = d if acc is None else acc + d
    return acc


def _ones_where(mask):
    return jnp.where(mask, 1.0, 0.0).astype(BF16)


def _sigmoid(x):
    return 1.0 / (1.0 + jnp.exp(-x))


def _softplus(x):
    return jnp.maximum(x, 0.0) + jnp.log(1.0 + jnp.exp(-jnp.abs(x)))


def _head_masks():
    lane = lax.broadcasted_iota(jnp.int32, (1, LANES), 1)
    return [jnp.where(lane < D_HEAD, 1.0, 0.0), jnp.where(lane >= D_HEAD, 1.0, 0.0)]


def _head_rstd(o, masks):
    sq = o * o
    r = None
    for m in masks:
        ms = jnp.sum(sq * m, axis=1, keepdims=True) * (1.0 / D_HEAD)
        t = lax.rsqrt(ms + EPS) * m
        r = t if r is None else r + t
    return r


def _head_mean(t, masks):
    out = None
    for m in masks:
        v = jnp.sum(t * m, axis=1, keepdims=True) * (1.0 / D_HEAD) * m
        out = v if out is None else out + v
    return out


def _mix_out_norm(a, w, res, g, name, tm=512):
    m, k = a.shape
    d = w.shape[1]
    tm = min(tm, m)

    steps = m // tm
    ahead = 2

    def body(a_ref, w_ref, res_hbm, g_ref, x_ref, h_ref, ring, sems):
        i = pl.program_id(0)

        def fetch(step):
            slot = lax.rem(step, ahead + 1)
            return pltpu.make_async_copy(res_hbm.at[pl.ds(step * tm, tm)], ring.at[slot], sems.at[slot])

        @pl.when(i == 0)
        def _():
            for s in range(min(ahead, steps)):
                fetch(s).start()

        @pl.when(i + ahead < steps)
        def _():
            fetch(i + ahead).start()

        fetch(i).wait()
        xv = ring[lax.rem(i, ahead + 1)] + _dot(a_ref[...], w_ref[...])
        x_ref[...] = xv
        r = lax.rsqrt(jnp.mean(xv * xv, axis=-1, keepdims=True) + EPS)
        h_ref[...] = (xv * r * g_ref[...]).astype(h_ref.dtype)

    row = pl.BlockSpec((tm, d), lambda i: (i, 0))
    return pl.pallas_call(
        body, name=name, grid=(steps,),
        in_specs=[pl.BlockSpec((tm, k), lambda i: (i, 0)), pl.BlockSpec((k, d), lambda i: (0, 0)),
                  pl.BlockSpec(memory_space=pl.ANY), pl.BlockSpec((1, d), lambda i: (0, 0))],
        out_specs=[row, row],
        out_shape=[jax.ShapeDtypeStruct((m, d), F32), jax.ShapeDtypeStruct((m, d), MXU_DTYPE)],
        scratch_shapes=[pltpu.VMEM((ahead + 1, tm, d), F32), pltpu.SemaphoreType.DMA((ahead + 1,))],
        compiler_params=_params(1),
    )(a, w, res, g)


def _ffn_down_loss(a, w, res, target, g, name, tm=512):
    m, k = a.shape
    d = w.shape[1]
    tm = min(tm, m)

    def body(a_ref, w_ref, res_ref, t_ref, g_ref, dx_ref, dxm_ref, dg_ref, loss_ref):
        xv = res_ref[...] + _dot(a_ref[...], w_ref[...])
        gv = g_ref[...]
        r = lax.rsqrt(jnp.mean(xv * xv, axis=-1, keepdims=True) + EPS)
        xh = xv * r
        e = xh * gv - t_ref[...]
        dy = e * (1.0 / d)
        dxh = dy * gv
        dxv = r * (dxh - xh * jnp.mean(dxh * xh, axis=-1, keepdims=True))
        dx_ref[...] = dxv
        dxm_ref[...] = dxv.astype(dxm_ref.dtype)

        @pl.when(pl.program_id(0) == 0)
        def _():
            dg_ref[...] = jnp.zeros_like(dg_ref)
            loss_ref[...] = jnp.zeros_like(loss_ref)

        dg_ref[...] += jnp.sum(dy * xh, axis=0, keepdims=True)
        part = 0.5 * jnp.sum(jnp.mean(e * e, axis=-1, keepdims=True), axis=0, keepdims=True)
        loss_ref[...] += jnp.broadcast_to(part, loss_ref.shape)

    row = pl.BlockSpec((tm, d), lambda i: (i, 0))
    vec = pl.BlockSpec((1, d), lambda i: (0, 0))
    return pl.pallas_call(
        body, name=name, grid=(m // tm,),
        in_specs=[pl.BlockSpec((tm, k), lambda i: (i, 0)), pl.BlockSpec((k, d), lambda i: (0, 0)), row, row, vec],
        out_specs=[row, row, vec, pl.BlockSpec((1, LANES), lambda i: (0, 0))],
        out_shape=[jax.ShapeDtypeStruct((m, d), F32), jax.ShapeDtypeStruct((m, d), MXU_DTYPE),
                   jax.ShapeDtypeStruct((1, d), F32), jax.ShapeDtypeStruct((1, LANES), F32)],
        compiler_params=_params(1),
    )(a, w, res, target, g)


def _mm_nt_split(a, w, name, split, dtypes, tm=1024):
    m, n = a.shape
    k = w.shape[0]

    def body(a_ref, w_ref, lo_ref, hi_ref):
        av = a_ref[...]
        lo_ref[...] = _dot_nt(av, w_ref[0:split, :]).astype(lo_ref.dtype)
        hi_ref[...] = _dot_nt(av, w_ref[split:k, :]).astype(hi_ref.dtype)

    return pl.pallas_call(
        body, name=name, grid=(m // tm,),
        in_specs=[pl.BlockSpec((tm, n), lambda i: (i, 0)),
                  pl.BlockSpec((k, n), lambda i: (0, 0), pipeline_mode=pl.Buffered(1))],
        out_specs=[pl.BlockSpec((tm, split), lambda i: (i, 0)), pl.BlockSpec((tm, k - split), lambda i: (i, 0))],
        out_shape=[jax.ShapeDtypeStruct((m, split), dtypes[0]), jax.ShapeDtypeStruct((m, k - split), dtypes[1])],
        compiler_params=_params(1),
    )(a, w)


def _mm_tn(a, b, name, tk, tn, tt=1024, out_dtype=BF16):
    t, k = a.shape
    n = b.shape[1]
    tt = min(tt, t)
    steps = t // tt

    def body(a_ref, b_ref, o_ref, acc):
        s = pl.program_id(2)

        @pl.when(s == 0)
        def _():
            acc[...] = jnp.zeros_like(acc)

        acc[...] += _dot_tn(a_ref[...], b_ref[...])

        @pl.when(s == steps - 1)
        def _():
            o_ref[...] = acc[...].astype(o_ref.dtype)

    return pl.pallas_call(
        body, name=name, grid=(k // tk, n // tn, steps),
        in_specs=[pl.BlockSpec((tt, tk), lambda i, j, s: (s, i)), pl.BlockSpec((tt, tn), lambda i, j, s: (s, j))],
        out_specs=pl.BlockSpec((tk, tn), lambda i, j, s: (i, j)),
        out_shape=jax.ShapeDtypeStruct((k, n), out_dtype),
        scratch_shapes=[pltpu.VMEM((tk, tn), F32)],
        compiler_params=_params(3),
    )(a, b)


def _ffn_up(h, wg_t, wu_t, name, tm=1024, tn=FF_TILE):
    m, k = h.shape
    n = wg_t.shape[0]
    tm = min(tm, m)

    def body(h_ref, wg_ref, wu_ref, gate_ref, up_ref, ff_ref):
        hv = h_ref[...]
        gate = _dot_nt(hv, wg_ref[...])
        up = _dot_nt(hv, wu_ref[...])
        gate_ref[...] = gate.astype(gate_ref.dtype)
        up_ref[...] = up.astype(up_ref.dtype)
        ff_ref[...] = (gate * _sigmoid(gate) * up).astype(ff_ref.dtype)

    wspec = pl.BlockSpec((tn, k), lambda j, i: (j, 0))
    ospec = pl.BlockSpec((tm, tn), lambda j, i: (i, j))
    return pl.pallas_call(
        body, name=name, grid=(n // tn, m // tm),
        in_specs=[pl.BlockSpec((tm, k), lambda j, i: (i, 0)), wspec, wspec],
        out_specs=[ospec, ospec, ospec],
        out_shape=[jax.ShapeDtypeStruct((m, n), MXU_DTYPE)] * 3,
        compiler_params=_params(2),
    )(h, wg_t, wu_t)


def _ffn_bwd_act(dx, wd, gate, up, name, tm=1024, tn=FF_TILE):
    m, k = dx.shape
    n = wd.shape[0]
    tm = min(tm, m)

    def body(dx_ref, wd_ref, gate_ref, up_ref, dgate_ref, dup_ref):
        dff = _dot_nt(dx_ref[...], wd_ref[...])
        gate = gate_ref[...].astype(F32)
        sg = _sigmoid(gate)
        dgate_ref[...] = (dff * up_ref[...].astype(F32) * sg * (1.0 + gate * (1.0 - sg))).astype(dgate_ref.dtype)
        dup_ref[...] = (dff * gate * sg).astype(dup_ref.dtype)

    ospec = pl.BlockSpec((tm, tn), lambda j, i: (i, j))
    return pl.pallas_call(
        body, name=name, grid=(n // tn, m // tm),
        in_specs=[pl.BlockSpec((tm, k), lambda j, i: (i, 0)), pl.BlockSpec((tn, k), lambda j, i: (j, 0)),
                  ospec, ospec],
        out_specs=[ospec, ospec],
        out_shape=[jax.ShapeDtypeStruct((m, n), MXU_DTYPE), jax.ShapeDtypeStruct((m, n), MXU_DTYPE)],
        compiler_params=_params(2),
    )(dx, wd, gate, up)


def _chunk_masks():
    r = lax.broadcasted_iota(jnp.int32, (SUPER, SUPER), 0)
    c = lax.broadcasted_iota(jnp.int32, (SUPER, SUPER), 1)
    same = jnp.right_shift(r, 6) == jnp.right_shift(c, 6)
    lower = jnp.logical_and(same, c <= r)
    upper = jnp.logical_and(same, c >= r)
    return same, lower, upper


def _head_block_mask():
    r = lax.broadcasted_iota(jnp.int32, (LANES, LANES), 0)
    c = lax.broadcasted_iota(jnp.int32, (LANES, LANES), 1)
    return jnp.where(jnp.right_shift(r, 6) == jnp.right_shift(c, 6), 1.0, 0.0)


def _lower_bound(lb_raw):
    return 1.0 / (1.0 + jnp.exp(lb_raw[1:2, :] - lb_raw[0:1, :]))


HGRN_UNROLL = 4
PER_SUPER = SUPER // CHUNK
CHUNK_ROWS = [slice(c * CHUNK, (c + 1) * CHUNK) for c in range(PER_SUPER)]


def _over_chunks(rows):
    return jnp.concatenate([jnp.broadcast_to(r, (CHUNK, LANES)) for r in rows], axis=0)


def _hgrn_gates(q, hf, lb, lower_b):
    sig = _sigmoid(hf)
    f = lb + (1.0 - lb) * sig
    k = 1.0 - f
    lf = jnp.log(f)
    b = _rsum_left(lower_b, lf, 2)
    ends = [b[cr.stop - 1:cr.stop, :] for cr in CHUNK_ROWS]
    eb = jnp.exp(b)
    enb = jnp.exp(-b)
    edb = jnp.exp(_over_chunks(ends) - b)
    decs = [jnp.exp(e) for e in ends]
    return sig, f, k, decs, eb, enb, edb, q * eb, k * enb, k * edb


def _hgrn_fwd(proj, lower_bounds, norm_g, n_seq, seq):
    t = n_seq * seq
    n_super = seq // SUPER
    n_chunks = seq // CHUNK

    def body(q_ref, f_ref, i_ref, g_ref, lb_ref, ng_ref, out_ref, opre_ref, st_ref):
        masks = _head_masks()
        _, lower, _ = _chunk_masks()
        lower_b = _ones_where(lower)
        bd = _head_block_mask()
        lb = _lower_bound(lb_ref[...])
        ng = ng_ref[...]

        def step(it, st):
            blocks = [HGRN_UNROLL * it + u for u in range(HGRN_UNROLL)]
            rows = [pl.ds(pl.multiple_of(sb * SUPER, SUPER), SUPER) for sb in blocks]
            vs = [i_ref[rw, :] for rw in rows]
            gates = [_hgrn_gates(q_ref[rw, :], f_ref[rw, :], lb, lower_b) for rw in rows]
            decs, qes, kes, kds = ([g[k] for g in gates] for k in (3, 7, 8, 9))
            scores = [[_dot_nt(qe * m, ke) for m in masks] for qe, ke in zip(qes, kes)]
            updates = [[_dot_tn(v[cr], kd[cr]) for cr in CHUNK_ROWS] for v, kd in zip(vs, kds)]
            states = [st]
            for dec_b, upd_b in zip(decs, updates):
                for dec, upd in zip(dec_b, upd_b):
                    states.append(states[-1] * dec + bd * upd)
            for u, sb in enumerate(blocks):
                for c in range(PER_SUPER):
                    st_ref[0, 0, sb * PER_SUPER + c] = states[u * PER_SUPER + c]
            intra = [[_dot(jnp.where(lower, p, 0.0), v) for p in sc] for sc, v in zip(scores, vs)]
            inter = [[_dot_nt(qe[cr], states[u * PER_SUPER + c]) for c, cr in enumerate(CHUNK_ROWS)]
                     for u, qe in enumerate(qes)]
            for rw, intra_b, inter_b in zip(rows, intra, inter):
                o = intra_b[0] * masks[0] + intra_b[1] * masks[1] + jnp.concatenate(inter_b, axis=0)
                opre_ref[rw, :] = o
                hg = g_ref[rw, :]
                on = o * _head_rstd(o, masks) * ng
                out_ref[rw, :] = (on * hg * _sigmoid(hg)).astype(out_ref.dtype)
            return states[-1]

        lax.fori_loop(0, n_super // HGRN_UNROLL, step, jnp.zeros((LANES, LANES), F32))

    def col(k):
        return pl.BlockSpec((seq, LANES), lambda p, b: (b, k * N_PAIRS + p))

    vec = lambda rows: pl.BlockSpec((rows, LANES), lambda p, b: (0, p))
    ospec = pl.BlockSpec((seq, LANES), lambda p, b: (b, p))
    return pl.pallas_call(
        body, name="hgrn_fwd", grid=(N_PAIRS, n_seq),
        in_specs=[col(0), col(1), col(2), col(3), vec(2), vec(1)],
        out_specs=[ospec, ospec,
                   pl.BlockSpec((1, 1, n_chunks, LANES, LANES), lambda p, b: (b, p, 0, 0, 0))],
        out_shape=[jax.ShapeDtypeStruct((t, 2 * GROUP), MXU_DTYPE), jax.ShapeDtypeStruct((t, GROUP), F32),
                   jax.ShapeDtypeStruct((n_seq, N_PAIRS, n_chunks, LANES, LANES), F32)],
        compiler_params=_params(2),
    )(proj, proj, proj, proj, lower_bounds, norm_g)


def _hgrn_bwd(proj, lower_bounds, norm_g, dmix, opre, states, n_seq, seq):
    t = n_seq * seq
    n_super = seq // SUPER
    n_chunks = seq // CHUNK
    per = SUPER // CHUNK

    def body(q_ref, f_ref, i_ref, g_ref, lb_ref, ng_ref, dm_ref, opre_ref, st_ref,
             dq_ref, df_ref, di_ref, dg_ref, dlb_ref, dng_ref):
        masks = _head_masks()
        _, lower, upper = _chunk_masks()
        lower_b, upper_b = _ones_where(lower), _ones_where(upper)
        bd = _head_block_mask()
        lb_raw = lb_ref[...]
        lb = _lower_bound(lb_raw)
        ng = ng_ref[...]

        @pl.when(pl.program_id(1) == 0)
        def _():
            dlb_ref[...] = jnp.zeros_like(dlb_ref)
            dng_ref[...] = jnp.zeros_like(dng_ref)

        def first_half(sb):
            rows = pl.ds(pl.multiple_of(sb * SUPER, SUPER), SUPER)
            q, hf, v, hg = q_ref[rows, :], f_ref[rows, :], i_ref[rows, :], g_ref[rows, :]
            sig, f, k, decs, eb, enb, edb, qe, ke, kd = _hgrn_gates(q, hf, lb, lower_b)
            o = opre_ref[rows, :]
            r = _head_rstd(o, masks)
            oh = o * r
            dm = dm_ref[rows, :]
            sg = _sigmoid(hg)
            dg_ref[rows, :] = (dm * oh * ng * sg * (1.0 + hg * (1.0 - sg))).astype(dg_ref.dtype)
            don = dm * hg * sg
            dng_ref[...] += jnp.sum(don * oh, axis=0, keepdims=True)
            doh = don * ng
            do = r * (doh - oh * _head_mean(doh * oh, masks))
            doms = [do * m for m in masks]
            qems = [qe * m for m in masks]
            scores = [_dot_nt(qem, ke) for qem in qems]
            dscores = [_dot_nt(dom, v) for dom in doms]
            prevs = [st_ref[0, 0, sb * per + c] for c in range(per)]
            dst_in = [_dot_tn(do[cr], qe[cr]) for cr in CHUNK_ROWS]
            dqe_i = [_dot(do[cr], prev) for cr, prev in zip(CHUNK_ROWS, prevs)]
            return dict(rows=rows, v=v, sig=sig, f=f, decs=decs, eb=eb, enb=enb, edb=edb, qe=qe, ke=ke, kd=kd,
                        doms=doms, qems=qems, scores=scores, dscores=dscores, prevs=prevs, dst_in=dst_in, dqe_i=dqe_i)

        def second_half(blk, dsts):
            v, qe, ke, kd = blk["v"], blk["qe"], blk["ke"], blk["kd"]
            ps = [jnp.where(lower, p, 0.0) for p in blk["scores"]]
            dps = [jnp.where(lower, dp, 0.0) for dp in blk["dscores"]]
            dqe_h = [_dot(dp, ke) for dp in dps]
            dke_h = [_dot_tn(dp, qem) for dp, qem in zip(dps, blk["qems"])]
            dv_h = [_dot_tn(p, dom) for p, dom in zip(ps, blk["doms"])]
            dus = [bd * d for d in dsts]
            dv_i = [_dot_nt(kd[cr], du) for cr, du in zip(CHUNK_ROWS, dus)]
            dkd_i = [_dot(v[cr], du) for cr, du in zip(CHUNK_ROWS, dus)]

            def finish():
                dqe = dqe_h[0] * masks[0] + dqe_h[1] * masks[1] + jnp.concatenate(blk["dqe_i"], axis=0)
                dke = dke_h[0] + dke_h[1]
                dv = dv_h[0] + dv_h[1] + jnp.concatenate(dv_i, axis=0)
                dkd = jnp.concatenate(dkd_i, axis=0)
                dk = dke * blk["enb"] + dkd * blk["edb"]
                db = dqe * qe - dke * ke - dkd * kd
                dkd_kd = dkd * kd
                dends = [jnp.sum(dkd_kd[cr], axis=0, keepdims=True)
                         + jnp.sum(dsts[c] * blk["prevs"][c], axis=0, keepdims=True) * blk["decs"][c]
                         for c, cr in enumerate(CHUNK_ROWS)]
                dlf = _rsum_left(upper_b, db, 2) + _over_chunks(dends)
                sig = blk["sig"]
                dfv = dlf / blk["f"] - dk
                rows = blk["rows"]
                dq_ref[rows, :] = (dqe * blk["eb"]).astype(dq_ref.dtype)
                di_ref[rows, :] = dv.astype(di_ref.dtype)
                df_ref[rows, :] = (dfv * (1.0 - lb) * sig * (1.0 - sig)).astype(df_ref.dtype)
                dlb = jnp.sum(dfv * (1.0 - sig), axis=0, keepdims=True)
                da0 = dlb * lb * (1.0 - lb)
                dlb_ref[0:1, :] += da0
                dlb_ref[1:2, :] -= da0

            return finish

        def step(it, dst):
            blocks = [first_half(n_super - 1 - HGRN_UNROLL * it - u) for u in range(HGRN_UNROLL)]
            all_dsts = []
            for blk in blocks:
                dsts = [None] * per
                for c in reversed(range(per)):
                    dsts[c] = dst
                    dst = bd * (dst * blk["decs"][c] + blk["dst_in"][c])
                all_dsts.append(dsts)
            for finish in [second_half(blk, dsts) for blk, dsts in zip(blocks, all_dsts)]:
                finish()
            return dst

        lax.fori_loop(0, n_super // HGRN_UNROLL, step, jnp.zeros((LANES, LANES), F32))

    def col(k):
        return pl.BlockSpec((seq, LANES), lambda p, b: (b, k * N_PAIRS + p))

    vec = lambda rows: pl.BlockSpec((rows, LANES), lambda p, b: (0, p))
    ospec = pl.BlockSpec((seq, LANES), lambda p, b: (b, p))
    piece = jax.ShapeDtypeStruct((t, GROUP), MXU_DTYPE)
    return pl.pallas_call(
        body, name="hgrn_bwd", grid=(N_PAIRS, n_seq),
        in_specs=[col(0), col(1), col(2), col(3), vec(2), vec(1), ospec, ospec,
                  pl.BlockSpec((1, 1, n_chunks, LANES, LANES), lambda p, b: (b, p, 0, 0, 0))],
        out_specs=[ospec, ospec, ospec, ospec, vec(2), vec(1)],
        out_shape=[piece, piece, piece, piece,
                   jax.ShapeDtypeStruct((2, GROUP), F32), jax.ShapeDtypeStruct((1, GROUP), F32)],
        compiler_params=_params(2),
    )(proj, proj, proj, proj, lower_bounds, norm_g, dmix, opre, states)


SB_SCALE = 1.0 / math.sqrt(D_HEAD)


SB_STEP = 2 * SB_BLOCK
QUERY_BLOCKS = (slice(0, SB_BLOCK), slice(SB_BLOCK, SB_STEP))


def _triangle(keep):
    row = lax.broadcasted_iota(jnp.int32, (SB_BLOCK, SB_BLOCK), 0)
    col = lax.broadcasted_iota(jnp.int32, (SB_BLOCK, SB_BLOCK), 1)
    return _ones_where(keep(row, col))


SB_HALF = SB_BLOCK // 2


def _keep(on_diagonal, x):
    if not on_diagonal:
        return x
    row = lax.broadcasted_iota(jnp.int32, (SB_HALF, SB_HALF), 0)
    col = lax.broadcasted_iota(jnp.int32, (SB_HALF, SB_HALF), 1)
    return jnp.where(col < row, x, 0.0)


def _lower_quadrants(fn, n_out, *tiles):
    def quadrant(r, c):
        return [t[r * SB_HALF:(r + 1) * SB_HALF, c * SB_HALF:(c + 1) * SB_HALF] for t in tiles]

    top, left, bottom = fn(True, *quadrant(0, 0)), fn(False, *quadrant(1, 0)), fn(True, *quadrant(1, 1))
    zero = jnp.zeros((SB_HALF, SB_HALF), F32)
    return [jnp.concatenate([jnp.concatenate([top[o], zero], axis=1), jnp.concatenate([left[o], bottom[o]], axis=1)],
                            axis=0) for o in range(n_out)]


def _sb_fwd(proj, norm_g, mixed, shards, n_seq, seq):
    t = n_seq * seq
    nq = seq // SB_STEP
    q0, k0, v0 = 0, N_PAIRS, 2 * N_PAIRS
    n_w = len(shards)
    n_steps = N_PAIRS * n_seq * nq
    tri = _triangle(lambda row, col: row >= col)

    def body(q_ref, k_ref, v_ref, ng_ref, tri_ref, mixed_in, *rest):
        del mixed_in
        shard_refs = rest[:n_w]
        out_ref, opre_ref, ctot_ref = rest[n_w:n_w + 3]
        gathered = rest[n_w + 3:2 * n_w + 3]
        send_sems, recv_sems, local_sems = rest[2 * n_w + 3:]
        i = pl.program_id(2)
        step = (pl.program_id(0) * n_seq + pl.program_id(1)) * nq + i
        plan = _GatherPlan(shard_refs, gathered, send_sems, recv_sems, local_sems)

        @pl.when(step == 0)
        def _():
            plan.start()

        @pl.when(step == (3 * n_steps) // 4)
        def _():
            plan.forward()

        masks = _head_masks()
        suffix = tri_ref[...]
        qhs = [[(q_ref[blk, :] * SB_SCALE * m).astype(MXU_DTYPE) for m in masks] for blk in QUERY_BLOCKS]

        def tiles(js, work, carry):
            rows = [pl.ds(pl.multiple_of(j * SB_BLOCK, SB_BLOCK), SB_BLOCK) for j in js]
            ks = [k_ref[rw, :].astype(MXU_DTYPE) for rw in rows]
            vs = [v_ref[rw, :].astype(MXU_DTYPE) for rw in rows]
            zs = [[_dot_nt(qh, ks[ts]) for qh in qhs[qb]] for qb, ts, _ in work]
            ccs = [[_lower_quadrants(lambda on, zq: (_keep(on, _softplus(zq)),), 1, z)[0] if diag else _softplus(z)
                    for z in zw] for zw, (_, _, diag) in zip(zs, work)]
            sums = [[jnp.dot(cc.astype(BF16), suffix, preferred_element_type=F32) for cc in cw] for cw in ccs]
            out = [list(per_block) for per_block in carry]
            for h in range(len(masks)):
                for w, (qb, ts, diag) in enumerate(work):
                    run, acc = out[qb][h]
                    logit = zs[w][h] - (sums[w][h] + run)
                    if diag:
                        a = _lower_quadrants(lambda on, lq: (_keep(on, jnp.exp(lq)),), 1, logit)[0]
                    else:
                        a = jnp.exp(logit)
                    out[qb][h] = (run + sums[w][h][:, 0:1], acc + _dot(a, vs[ts]))
            return tuple(tuple(per_block) for per_block in out)

        start = ((jnp.zeros((SB_BLOCK, 1), F32), jnp.zeros((SB_BLOCK, LANES), F32)),) * 2
        carry = tiles([2 * i, 2 * i + 1], [(0, 0, True), (1, 1, True), (1, 0, False)], (start, start))
        both = [(0, 0, False), (0, 1, False), (1, 0, False), (1, 1, False)]
        carry = lax.fori_loop(0, i, lambda s, cy: tiles([2 * (i - s) - 1, 2 * (i - s) - 2], both, cy), carry)
        opre = jnp.concatenate([cb[0][1] * masks[0] + cb[1][1] * masks[1] for cb in carry], axis=0)
        ctot = jnp.concatenate([cb[0][0] * masks[0] + cb[1][0] * masks[1] for cb in carry], axis=0)
        opre_ref[...] = opre
        ctot_ref[...] = ctot
        out_ref[...] = (opre * _head_rstd(opre, masks) * ng_ref[...]).astype(out_ref.dtype)

        @pl.when(step == n_steps - 1)
        def _():
            plan.finish()

    qspec = pl.BlockSpec((SB_STEP, LANES), lambda p, b, i: (b * nq + i, q0 + p))
    ospec = pl.BlockSpec((SB_STEP, LANES), lambda p, b, i: (b * nq + i, p))
    hbm = pl.BlockSpec(memory_space=pltpu.HBM)
    outs = pl.pallas_call(
        body, name="sb_fwd", grid=(N_PAIRS, n_seq, nq),
        in_specs=[qspec,
                  pl.BlockSpec((seq, LANES), lambda p, b, i: (b, k0 + p)),
                  pl.BlockSpec((seq, LANES), lambda p, b, i: (b, v0 + p)),
                  pl.BlockSpec((1, LANES), lambda p, b, i: (0, p)),
                  pl.BlockSpec(tri.shape, lambda p, b, i: (0, 0)), hbm] + [hbm] * n_w,
        out_specs=[pl.BlockSpec((SB_STEP, LANES), lambda p, b, i: (b * nq + i, N_PAIRS + p)), ospec, ospec]
        + [hbm] * n_w,
        out_shape=[jax.ShapeDtypeStruct(mixed.shape, mixed.dtype), jax.ShapeDtypeStruct((t, GROUP), F32),
                   jax.ShapeDtypeStruct((t, GROUP), F32)]
        + [jax.ShapeDtypeStruct((N_DEV,) + s.shape, s.dtype) for s in shards],
        scratch_shapes=[pltpu.SemaphoreType.DMA((n_w * _GatherPlan.COPIES,)),
                        pltpu.SemaphoreType.DMA((n_w * _GatherPlan.COPIES,)), pltpu.SemaphoreType.DMA((n_w,))],
        input_output_aliases={5: 0},
        compiler_params=_params(3),
    )(proj, proj, proj, norm_g, tri, mixed, *shards)
    return outs[0], outs[1], outs[2], list(outs[3:])


def _sb_bwd(proj, norm_g, dmix, opre, ctot, grads, n_seq, seq):
    t = n_seq * seq
    nq = seq // SB_STEP
    q0, k0, v0 = 0, N_PAIRS, 2 * N_PAIRS
    n_w = len(grads)
    n_steps = N_PAIRS * n_seq * nq
    tri = _triangle(lambda row, col: row <= col)

    def body(q_ref, k_ref, v_ref, ng_ref, tri_ref, dm_ref, opre_ref, ctot_ref, *rest):
        grad_refs = rest[:n_w]
        dq_ref, dk_ref, dv_ref, dng_ref = rest[n_w:n_w + 4]
        lands = rest[n_w + 4:2 * n_w + 4]
        dk_acc, dv_acc, send_sems, recv_sems, local_sems = rest[2 * n_w + 4:]
        p_id, b_id, i = pl.program_id(0), pl.program_id(1), pl.program_id(2)
        step = (p_id * n_seq + b_id) * nq + i
        plan = _ScatterPlan(grad_refs, lands, send_sems, recv_sems, local_sems)

        @pl.when(step == 0)
        def _():
            plan.start()

        masks = _head_masks()
        upto = tri_ref[...]

        def prefix(x):
            return jnp.dot(x.astype(BF16), upto, preferred_element_type=F32)

        @pl.when(i == 0)
        def _():
            dk_acc[...] = jnp.zeros_like(dk_acc)
            dv_acc[...] = jnp.zeros_like(dv_acc)

        @pl.when(jnp.logical_and(b_id == 0, i == 0))
        def _():
            dng_ref[...] = jnp.zeros_like(dng_ref)

        o = opre_ref[...]
        rs = _head_rstd(o, masks)
        oh = o * rs
        dm = dm_ref[...]
        dng_ref[...] += jnp.sum(dm * oh, axis=0, keepdims=True)
        doh = dm * ng_ref[...]
        do = rs * (doh - oh * _head_mean(doh * oh, masks))

        heads = range(len(masks))
        qs = [q_ref[blk, :] * SB_SCALE for blk in QUERY_BLOCKS]
        dos = [do[blk] for blk in QUERY_BLOCKS]
        qhs = [[(q * m).astype(MXU_DTYPE) for m in masks] for q in qs]
        doms = [[(d * m).astype(MXU_DTYPE) for m in masks] for d in dos]
        head_rows = [jnp.where(jnp.right_shift(lax.broadcasted_iota(jnp.int32, (LANES, 1), 0), 6) == h, 1.0, 0.0)
                     for h in heads]
        qhts = [[(qt * hr).astype(MXU_DTYPE) for hr in head_rows] for qt in [q.astype(F32).T for q in qs]]
        domts = [[(dt * hr).astype(MXU_DTYPE) for hr in head_rows] for dt in [d.T for d in dos]]
        totals = [[ctot_ref[blk, h * D_HEAD:h * D_HEAD + 1] for h in heads] for blk in QUERY_BLOCKS]

        def tiles(js, work, carry):
            rows = [pl.ds(pl.multiple_of(j * SB_BLOCK, SB_BLOCK), SB_BLOCK) for j in js]
            ks = [k_ref[rw, :].astype(MXU_DTYPE) for rw in rows]
            vs = [v_ref[rw, :].astype(MXU_DTYPE) for rw in rows]
            zs = [[_dot_nt(qhs[qb][h], ks[ts]) for h in heads] for qb, ts, _ in work]
            das = [[_dot_nt(doms[qb][h], vs[ts]) for h in heads] for qb, ts, _ in work]
            def gates(on, zq):
                sp = _softplus(zq)
                return _keep(on, sp), zq - sp, _keep(on, jnp.exp(zq - sp))

            trio = [[_lower_quadrants(gates, 3, z) if diag else gates(False, z) for z in zw]
                    for zw, (_, _, diag) in zip(zs, work)]
            ccs, lsigs, sigs = ([[t[o] for t in tw] for tw in trio] for o in range(3))
            pres = [[prefix(cc) for cc in cw] for cw in ccs]
            out = [list(per_block) for per_block in carry]
            for h in heads:
                for w, (qb, ts, diag) in enumerate(work):
                    pc, pdl, dq_h = out[qb][h]
                    logit = lsigs[w][h] + pres[w][h] - (totals[qb][h] - pc)
                    if diag:
                        a = _lower_quadrants(lambda on, lq: (_keep(on, jnp.exp(lq)),), 1, logit)[0]
                    else:
                        a = jnp.exp(logit)
                    dl = a * das[w][h]
                    dv_acc[js[ts]] += _dot(domts[qb][h], a)
                    dpre = prefix(dl)
                    dz = dl - sigs[w][h] * (pdl + dpre)
                    dzb = dz.astype(MXU_DTYPE)
                    dk_acc[js[ts]] += _dot(qhts[qb][h], dzb)
                    out[qb][h] = (pc + pres[w][h][:, SB_BLOCK - 1:SB_BLOCK], pdl + dpre[:, SB_BLOCK - 1:SB_BLOCK],
                                  dq_h + _dot(dzb, ks[ts]))
            return tuple(tuple(per_block) for per_block in out)

        zero = jnp.zeros((SB_BLOCK, 1), F32)
        start = ((zero, zero, jnp.zeros((SB_BLOCK, LANES), F32)),) * 2
        both = [(0, 0, False), (0, 1, False), (1, 0, False), (1, 1, False)]
        carry = lax.fori_loop(0, i, lambda s, cy: tiles([2 * s, 2 * s + 1], both, cy), (start, start))
        carry = tiles([2 * i, 2 * i + 1], [(0, 0, True), (1, 0, False), (1, 1, True)], carry)
        dq = jnp.concatenate([cb[0][2] * masks[0] + cb[1][2] * masks[1] for cb in carry], axis=0)
        dq_ref[...] = (dq * SB_SCALE).astype(dq_ref.dtype)

        @pl.when(i == nq - 1)
        def _():
            for j in range(seq // SB_BLOCK):
                tile_rows = slice(j * SB_BLOCK, (j + 1) * SB_BLOCK)
                dk_ref[tile_rows, :] = dk_acc[j].T.astype(dk_ref.dtype)
                dv_ref[tile_rows, :] = dv_acc[j].T.astype(dv_ref.dtype)

        @pl.when(step == n_steps - 1)
        def _():
            plan.finish()

    qspec = pl.BlockSpec((SB_STEP, LANES), lambda p, b, i: (b * nq + i, q0 + p))
    ospec = pl.BlockSpec((SB_STEP, LANES), lambda p, b, i: (b * nq + i, p))
    dmspec = pl.BlockSpec((SB_STEP, LANES), lambda p, b, i: (b * nq + i, N_PAIRS + p))
    full = lambda k: pl.BlockSpec((seq, LANES), lambda p, b, i: (b, k + p))
    vec = pl.BlockSpec((1, LANES), lambda p, b, i: (0, p))
    hbm = pl.BlockSpec(memory_space=pltpu.HBM)
    piece = jax.ShapeDtypeStruct((t, GROUP), MXU_DTYPE)
    outs = pl.pallas_call(
        body, name="sb_bwd", grid=(N_PAIRS, n_seq, nq),
        in_specs=[qspec, full(k0), full(v0), vec, pl.BlockSpec(tri.shape, lambda p, b, i: (0, 0)), dmspec, ospec, ospec]
        + [hbm] * n_w,
        out_specs=[ospec, full(0), full(0), vec] + [hbm] * n_w,
        out_shape=[piece, piece, piece, jax.ShapeDtypeStruct((1, GROUP), F32)]
        + [jax.ShapeDtypeStruct(g.shape, g.dtype) for g in grads],
        scratch_shapes=[pltpu.VMEM((seq // SB_BLOCK, LANES, SB_BLOCK), F32),
                        pltpu.VMEM((seq // SB_BLOCK, LANES, SB_BLOCK), F32),
                        pltpu.SemaphoreType.DMA((n_w * (N_DEV - 1),)), pltpu.SemaphoreType.DMA((n_w * (N_DEV - 1),)),
                        pltpu.SemaphoreType.DMA((n_w,))],
        compiler_params=_params(3),
    )(proj, proj, proj, norm_g, tri, dmix, opre, ctot, *[pltpu.with_memory_space_constraint(g, pltpu.HBM) for g in grads])
    return outs[0], outs[1], outs[2], outs[3], list(outs[4:])


def _mesh_place():
    x, y, c = lax.axis_index("x"), lax.axis_index("y"), lax.axis_index("c")
    return x, y, c


def _peer(x, y, c, k):
    px = lax.rem(x + ((k >> 2) & 1), 2)
    py = lax.rem(y + ((k >> 1) & 1), 2)
    pc = lax.rem(c + (k & 1), 2)
    return (px, py, pc), 4 * px + 2 * py + pc


def _remote(src, dst, send_sem, recv_sem, to):
    return pltpu.make_async_remote_copy(src_ref=src, dst_ref=dst, send_sem=send_sem, recv_sem=recv_sem,
                                        device_id=to, device_id_type=pl.DeviceIdType.MESH)


class _GatherPlan:
    COPIES = 7

    def __init__(self, shards, gathered, send_sems, recv_sems, local_sems):
        x, y, c = _mesh_place()
        self.c = c
        self.me = (x, y, c)
        self.sibling = (x, y, 1 - c)
        self.chips = [(1 - x, y), (x, 1 - y), (1 - x, 1 - y)]
        self.tensors = list(zip(shards, gathered))
        self.send_sems, self.recv_sems, self.local_sems = send_sems, recv_sems, local_sems

    @staticmethod
    def _index(place):
        return 4 * place[0] + 2 * place[1] + place[2]

    def _copy(self, w, k, block, to, own=False):
        shard, gathered = self.tensors[w]
        slot = gathered.at[self._index(block)]
        n = w * self.COPIES + k
        return _remote(shard if own else slot, slot, self.send_sems.at[n], self.recv_sems.at[n], to)

    def _local(self, w):
        shard, gathered = self.tensors[w]
        return pltpu.make_async_copy(shard, gathered.at[self._index(self.me)], self.local_sems.at[w])

    def _first(self, w):
        return [self._copy(w, 0, self.me, self.sibling, own=True)] + [
            self._copy(w, 1 + j, self.me, (*chip, self.c), own=True) for j, chip in enumerate(self.chips)]

    def _passed(self, w):
        return [self._copy(w, 4 + j, (*chip, self.c), self.sibling) for j, chip in enumerate(self.chips)]

    def start(self):
        for w in range(len(self.tensors)):
            self._local(w).start()
            for cp in self._first(w):
                cp.start()

    def forward(self):
        for w in range(len(self.tensors)):
            passed = self._passed(w)
            for j, chip in enumerate(self.chips):
                self._copy(w, 1 + j, (*chip, self.c), self.me).wait_recv()
                passed[j].start()

    def finish(self):
        for w in range(len(self.tensors)):
            self._copy(w, 0, self.sibling, self.me).wait_recv()
            for j, chip in enumerate(self.chips):
                self._copy(w, 4 + j, (*chip, 1 - self.c), self.me).wait_recv()
            for cp in self._first(w) + self._passed(w):
                cp.wait_send()
            self._local(w).wait()


class _ScatterPlan:
    def __init__(self, grads, lands, send_sems, recv_sems, local_sems):
        self.place = _mesh_place()
        x, y, c = self.place
        self.me = 4 * x + 2 * y + c
        self.tensors = list(zip(grads, lands))
        self.send_sems, self.recv_sems, self.local_sems = send_sems, recv_sems, local_sems

    def _copies(self, w):
        grad, land = self.tensors[w]
        out = []
        for k in range(1, N_DEV):
            peer, pidx = _peer(*self.place, k)
            n = w * (N_DEV - 1) + k - 1
            sems = (self.send_sems.at[n], self.recv_sems.at[n], peer)
            out.append((_remote(grad.at[pidx], land.at[self.me], *sems), _remote(grad.at[pidx], land.at[pidx], *sems)))
        return out

    def _local(self, w):
        grad, land = self.tensors[w]
        return pltpu.make_async_copy(grad.at[self.me], land.at[self.me], self.local_sems.at[w])

    def start(self):
        for w in range(len(self.tensors)):
            self._local(w).start()
            for send, _ in self._copies(w):
                send.start()

    def finish(self):
        for w in range(len(self.tensors)):
            copies = self._copies(w)
            for _, arrival in copies:
                arrival.wait_recv()
            for send, _ in copies:
                send.wait_send()
            self._local(w).wait()


def _cast_shards(shards):
    def body(*refs):
        n = len(refs) // 2
        for src, dst in zip(refs[:n], refs[n:]):
            dst[...] = src[...].astype(dst.dtype)

    shapes = [jax.ShapeDtypeStruct(s.shape, BF16) for s in shards]
    return pl.pallas_call(
        body, name="cast_shards", grid=(1,),
        in_specs=[_whole(s) for s in shards], out_specs=[_whole(s) for s in shapes], out_shape=shapes,
        compiler_params=_params(1),
    )(*shards)


def _gather_w_in(shard, x, g, chunk=512):
    rows, cols = shard.shape
    t, d = x.shape
    chunk = min(chunk, t)
    n_chunks = t // chunk

    def body(w_ref, x_ref, g_ref, out_ref, h_ref, x_buf, send_sems, recv_sems, local_sems, x_sems):
        plan = _GatherPlan([w_ref], [out_ref], send_sems, recv_sems, local_sems)
        plan.start()

        def fetch(j):
            return pltpu.make_async_copy(x_ref.at[pl.ds(j * chunk, chunk)], x_buf.at[j % 2], x_sems.at[j % 2])

        fetch(0).start()
        gv = g_ref[...]
        for j in range(n_chunks):
            if j + 1 < n_chunks:
                fetch(j + 1).start()
            fetch(j).wait()
            xv = x_buf[j % 2]
            r = lax.rsqrt(jnp.mean(xv * xv, axis=-1, keepdims=True) + EPS)
            h_ref[j * chunk:(j + 1) * chunk, :] = (xv * r * gv).astype(h_ref.dtype)
        plan.forward()
        plan.finish()

    vmem = pl.BlockSpec(memory_space=pltpu.VMEM)
    hbm = pl.BlockSpec(memory_space=pltpu.HBM)
    return pl.pallas_call(
        body, name="gather_w_in", in_specs=[hbm, hbm, vmem], out_specs=[hbm, vmem],
        out_shape=[jax.ShapeDtypeStruct((N_DEV, rows, cols), shard.dtype), jax.ShapeDtypeStruct((t, d), MXU_DTYPE)],
        scratch_shapes=[pltpu.VMEM((2, chunk, d), F32), pltpu.SemaphoreType.DMA((_GatherPlan.COPIES,)),
                        pltpu.SemaphoreType.DMA((_GatherPlan.COPIES,)), pltpu.SemaphoreType.DMA((1,)),
                        pltpu.SemaphoreType.DMA((2,))],
        compiler_params=pltpu.CompilerParams(vmem_limit_bytes=VMEM_LIMIT),
    )(shard, x, g)


def _dh_norm_bwd(pairs, x, g, res, name, tm=512, after=(), next_w=None):
    m, d = x.shape
    tm = min(tm, m)
    n_p = len(pairs)
    n_steps = m // tm
    n_in = 2 * n_p + 3 + len(after) + (next_w is not None)

    def body(*refs):
        x_ref, g_ref, res_ref = refs[2 * n_p:2 * n_p + 3]
        dx_ref, dg_ref = refs[n_in:n_in + 2]

        @pl.when(pl.program_id(0) == 0)
        def _():
            dg_ref[...] = jnp.zeros_like(dg_ref)

        dh = None
        for q in range(n_p):
            part = _dot(refs[2 * q][...], refs[2 * q + 1][...])
            dh = part if dh is None else dh + part
        xv = x_ref[...]
        r = lax.rsqrt(jnp.mean(xv * xv, axis=-1, keepdims=True) + EPS)
        xh = xv * r
        dxh = dh * g_ref[...]
        dxv = res_ref[...] + r * (dxh - xh * jnp.mean(dxh * xh, axis=-1, keepdims=True))
        dx_ref[...] = dxv
        dg_ref[...] += jnp.sum(dh * xh, axis=0, keepdims=True)
        if next_w is not None:
            refs[n_in + 2][...] = _dot_nt(dxv, refs[n_in - 1][...])

    in_specs, args = [], []
    for a, w, r in pairs:
        k = a.shape[1]
        in_specs += [pl.BlockSpec((tm, k), lambda i: (i, 0)),
                     pl.BlockSpec((k, d), functools.partial(lambda i, r: (r, 0), r=r or 0),
                                  pipeline_mode=pl.Buffered(1))]
        args += [a, w]
    row = pl.BlockSpec((tm, d), lambda i: (i, 0))
    vec = pl.BlockSpec((1, d), lambda i: (0, 0))
    extra_in, extra_out, extra_shape = [], [], []
    if next_w is not None:
        e = next_w.shape[0]
        extra_in = [pl.BlockSpec((e, d), lambda i: (0, 0), pipeline_mode=pl.Buffered(1))]
        extra_out = [pl.BlockSpec((tm, e), lambda i: (i, 0))]
        extra_shape = [jax.ShapeDtypeStruct((m, e), F32)]
        args_tail = (next_w,)
    else:
        args_tail = ()
    return pl.pallas_call(
        body, name=name, grid=(n_steps,),
        in_specs=in_specs + [row, vec, row] + [pl.BlockSpec(memory_space=pl.ANY)] * len(after) + extra_in,
        out_specs=[row, vec] + extra_out,
        out_shape=[jax.ShapeDtypeStruct((m, d), F32), jax.ShapeDtypeStruct((1, d), F32)] + extra_shape,
        compiler_params=_params(1),
    )(*args, x, g, res, *after, *args_tail)


def _dw_exchange_start(pieces, b, name, tn=512, tt=1024):
    t, n = b.shape
    widths = [p.shape[1] for p in pieces]
    rows = sum(widths)
    slot = rows // N_DEV
    steps = t // tt
    n_col = n // tn
    n_p = len(pieces)

    def body(*refs):
        piece_refs, b_ref, land_ref = refs[:n_p], refs[n_p], refs[n_p + 1]
        send_sem, recv_sem, local_sem, g_ref, land_thru, token, acc, stage, stage_sem = refs[n_p + 2:]
        del land_thru
        j, s = pl.program_id(0), pl.program_id(1)
        token[...] = jnp.zeros_like(token)

        @pl.when(s == 0)
        def _():
            acc[...] = jnp.zeros_like(acc)

        bv = b_ref[...]
        off = 0
        for p_ref, width in zip(piece_refs, widths):
            acc[off:off + width, :] += _dot_tn(p_ref[...], bv)
            off += width

        x, y, c = _mesh_place()
        me = 4 * x + 2 * y + c
        for col in range(n_col):
            @pl.when((j == col) & (s == steps - 1))
            def _(col=col):
                cols = pl.ds(col * tn, tn)
                for d in range(N_DEV):
                    stage[d] = acc[d * slot:(d + 1) * slot, :].astype(stage.dtype)
                put = pltpu.make_async_copy(stage, g_ref.at[:, :, cols], stage_sem)
                put.start()
                put.wait()
                for k in range(1, N_DEV):
                    peer, pidx = _peer(x, y, c, k)
                    _remote(g_ref.at[pidx, :, cols], land_ref.at[me, :, cols], send_sem, recv_sem, peer).start()
                if col == n_col - 1:
                    pltpu.make_async_copy(g_ref.at[me], land_ref.at[me], local_sem).start()

    hbm = pl.BlockSpec(memory_space=pltpu.HBM)
    sem = pl.BlockSpec(memory_space=pltpu.SEMAPHORE)
    buf = pltpu.HBM((N_DEV, slot, n), BF16)
    return pl.pallas_call(
        body, name=name, grid=(n_col, steps),
        in_specs=[pl.BlockSpec((tt, width), lambda j, s: (s, 0)) for width in widths]
        + [pl.BlockSpec((tt, tn), lambda j, s: (s, j)), hbm],
        out_specs=(sem, sem, sem, hbm, hbm, pl.BlockSpec((8, LANES), lambda j, s: (0, 0))),
        out_shape=(pltpu.SemaphoreType.DMA(()), pltpu.SemaphoreType.DMA(()), pltpu.SemaphoreType.DMA(()), buf, buf,
                   jax.ShapeDtypeStruct((8, LANES), F32)),
        scratch_shapes=[pltpu.VMEM((rows, tn), F32), pltpu.VMEM((N_DEV, slot, tn), BF16),
                        pltpu.SemaphoreType.DMA(())],
        input_output_aliases={n_p + 1: 4},
        compiler_params=pltpu.CompilerParams(dimension_semantics=("arbitrary", "arbitrary"),
                                             vmem_limit_bytes=VMEM_LIMIT,
                                             has_side_effects=pltpu.SideEffectType.DATAFLOW_SIDE_EFFECTING),
    )(*pieces, b, pltpu.with_memory_space_constraint(lax.empty((N_DEV, slot, n), BF16), pltpu.HBM))


def _exchange_wait(send_sem, recv_sem, local_sem, grad, land, after, name):
    def body(g_ref, land_ref, send_sem, recv_sem, local_sem, *rest):
        x, y, c = _mesh_place()
        peer, _ = _peer(x, y, c, 1)
        others = pl.ds(0, N_DEV - 1)
        seven = _remote(g_ref.at[others], land_ref.at[others], send_sem, recv_sem, peer)
        seven.wait_send()
        seven.wait_recv()
        pltpu.make_async_copy(g_ref.at[0], land_ref.at[0], local_sem).wait()

    hbm = pl.BlockSpec(memory_space=pltpu.HBM)
    sem = pl.BlockSpec(memory_space=pltpu.SEMAPHORE)
    buf = pltpu.HBM(grad.shape, grad.dtype)
    return pl.pallas_call(
        body, name=name, out_shape=(buf, buf),
        in_specs=(hbm, hbm, sem, sem, sem) + (pl.BlockSpec(memory_space=pl.ANY),) * len(after), out_specs=(hbm, hbm),
        input_output_aliases={0: 0, 1: 1},
        compiler_params=pltpu.CompilerParams(has_side_effects=pltpu.SideEffectType.DATAFLOW_SIDE_EFFECTING),
    )(grad, land, send_sem, recv_sem, local_sem, *after)[1]


SMALL_LAYOUT = ((0, 0, 0, 0, D_MODEL), (1, 0, 1, 0, GROUP), (1, 1, 1, GROUP, GROUP), (2, 0, 2, 0, GROUP),
                (3, 0, 2, GROUP, GROUP), (4, 0, 3, 0, D_MODEL), (5, 0, 4, 0, D_MODEL))
LOSS_ROW = 5
N_SMALL = 6


def _place_small(dst, srcs):
    dst[...] = jnp.zeros_like(dst)
    for p, sr, dr, dc, width in SMALL_LAYOUT:
        dst[dr:dr + 1, dc:dc + width] = srcs[p][sr:sr + 1, :]


def _small_allreduce(grads, loss_part):
    def body(*refs):
        g_in, loss_in, sum_ref = refs[:N_SMALL], refs[N_SMALL], refs[N_SMALL + 1]
        pack, land, send_sems, recv_sems = refs[N_SMALL + 2:]
        _place_small(pack, g_in)
        pack[LOSS_ROW:LOSS_ROW + 1, 0:LANES] = loss_in[...]

        x, y, c = _mesh_place()
        me = 4 * x + 2 * y + c
        land[me] = pack[...]
        sends = []
        for k in range(1, N_DEV):
            peer, _ = _peer(x, y, c, k)
            cp = _remote(pack, land.at[me], send_sems.at[k - 1], recv_sems.at[k - 1], peer)
            cp.start()
            sends.append(cp)
        for k in range(1, N_DEV):
            peer, pidx = _peer(x, y, c, k)
            _remote(pack, land.at[pidx], send_sems.at[k - 1], recv_sems.at[k - 1], peer).wait_recv()
        for cp in sends:
            cp.wait_send()

        g = land[0]
        for d in range(1, N_DEV):
            g = g + land[d]
        sum_ref[...] = g

    args = [*grads, loss_part]
    packed = jax.ShapeDtypeStruct((SMALL_ROWS, D_MODEL), F32)
    return pl.pallas_call(
        body, name="small_allreduce", grid=(1,),
        in_specs=[_whole(a) for a in args], out_specs=_whole(packed), out_shape=packed,
        scratch_shapes=[pltpu.VMEM(packed.shape, F32), pltpu.VMEM((N_DEV,) + packed.shape, F32),
                        pltpu.SemaphoreType.DMA((N_DEV - 1,)), pltpu.SemaphoreType.DMA((N_DEV - 1,))],
        compiler_params=_params(1),
    )(*args)


def _small_adamw(gsum, ws, ms, vs):
    def body(*refs):
        g_ref = refs[0]
        params = [refs[1 + q * N_SMALL:1 + (q + 1) * N_SMALL] for q in range(3)]
        o0 = 1 + 3 * N_SMALL
        outs = [refs[o0 + q * N_SMALL:o0 + (q + 1) * N_SMALL] for q in range(4)]
        loss_out = refs[o0 + 4 * N_SMALL]
        wp, mp, vp = refs[o0 + 4 * N_SMALL + 1:]
        for dst, srcs in zip((wp, mp, vp), params):
            _place_small(dst, srcs)
        g = g_ref[...]
        delta, nm, nv = _adam(wp[...], g, mp[...], vp[...])
        for val, out in zip((g, delta, nm, nv), outs):
            for p, sr, dr, dc, width in SMALL_LAYOUT:
                out[p][sr:sr + 1, :] = val[dr:dr + 1, dc:dc + width]
        loss_out[...] = g[LOSS_ROW:LOSS_ROW + 1, 0:LANES]

    args = [gsum, *ws, *ms, *vs]
    shapes = [jax.ShapeDtypeStruct(w.shape, F32) for w in ws] * 4 + [jax.ShapeDtypeStruct((1, LANES), F32)]
    packed = pltpu.VMEM((SMALL_ROWS, D_MODEL), F32)
    outs = pl.pallas_call(
        body, name="small_adamw", grid=(1,),
        in_specs=[_whole(a) for a in args], out_specs=[_whole(s) for s in shapes], out_shape=shapes,
        scratch_shapes=[packed, packed, packed], compiler_params=_params(1),
    )(*args)
    return [outs[q * N_SMALL:(q + 1) * N_SMALL] for q in range(4)], outs[4 * N_SMALL]


def _adam(w, g, m, v):
    m = ADAM_B1 * m + (1.0 - ADAM_B1) * g
    v = ADAM_B2 * v + (1.0 - ADAM_B2) * (g * g)
    m_hat = m / (1.0 - ADAM_B1 ** ADAM_STEP)
    v_hat = v / (1.0 - ADAM_B2 ** ADAM_STEP)
    delta = -ADAM_LR * (m_hat / (jnp.sqrt(v_hat) + ADAM_EPS) + ADAM_WD * w)
    return delta, m, v


def _reduce_adamw(land, w, m, v, name, tr, after=()):
    _, rows, width = land.shape

    def body(land_ref, w_ref, m_ref, v_ref, *rest):
        g_ref, d_ref, nm_ref, nv_ref = rest[len(after):]
        g = land_ref[0].astype(F32)
        for d in range(1, N_DEV):
            g = g + land_ref[d].astype(F32)
        delta, nm, nv = _adam(w_ref[...], g, m_ref[...], v_ref[...])
        g_ref[...] = g
        d_ref[...] = delta
        nm_ref[...] = nm
        nv_ref[...] = nv

    row = pl.BlockSpec((tr, width), lambda i: (i, 0))
    out = jax.ShapeDtypeStruct((rows, width), F32)
    return pl.pallas_call(
        body, name=name, grid=(rows // tr,),
        in_specs=[pl.BlockSpec((N_DEV, tr, width), lambda i: (0, i, 0)), row, row, row]
        + [pl.BlockSpec(memory_space=pl.ANY)] * len(after),
        out_specs=[row, row, row, row], out_shape=[out, out, out, out],
        compiler_params=_params(1),
    )(land, w, m, v, *after)


def kernel(x, mix_norm_g, w_in, lower_bounds, hgrn_norm_g, sb_norm_g, w_out, ffn_norm_g, w_gate, w_up, w_down, final_norm_g, loss_target, m_mix_norm_g, m_w_in, m_lower_bounds, m_hgrn_norm_g, m_sb_norm_g, m_w_out, m_ffn_norm_g, m_w_gate, m_w_up, m_w_down, m_final_norm_g, v_mix_norm_g, v_w_in, v_lower_bounds, v_hgrn_norm_g, v_sb_norm_g, v_w_out, v_ffn_norm_g, v_w_gate, v_w_up, v_w_down, v_final_norm_g):
    n_seq, seq, d = x.shape
    t = n_seq * seq
    x2d = x.reshape(t, d)
    tgt = loss_target.reshape(t, d)
    final_g = final_norm_g.reshape(1, d)
    col_sharded = (True, False, True, True, False)

    def as_rows(ws):
        return [w[0].T if tr else w[0] for w, tr in zip(ws, col_sharded)]

    big_w = as_rows([w_in, w_out, w_gate, w_up, w_down])
    big_m = as_rows([m_w_in, m_w_out, m_w_gate, m_w_up, m_w_down])
    big_v = as_rows([v_w_in, v_w_out, v_w_gate, v_w_up, v_w_down])

    sh_in, sh_out, sh_gate, sh_up, sh_down = _cast_shards(big_w)
    wt_in, h1 = _gather_w_in(sh_in, x2d, mix_norm_g)
    wt_in = wt_in.reshape(IN_COLS, d)
    proj_h, proj_s = _mm_nt_split(h1, wt_in, "proj", 4 * GROUP, (F32, MXU_DTYPE))
    mixed, oa_pre, states = _hgrn_fwd(proj_h, lower_bounds, hgrn_norm_g, n_seq, seq)
    mixed, ob_pre, ctot, gathered = _sb_fwd(proj_s, sb_norm_g, mixed, [sh_out, sh_gate, sh_up, sh_down], n_seq, seq)
    wf_out = gathered[0].reshape(d, d)
    wt_gate = gathered[1].reshape(D_FF, d)
    wt_up = gathered[2].reshape(D_FF, d)
    wf_down = gathered[3].reshape(D_FF, d)
    x1, h2 = _mix_out_norm(mixed, wf_out, x2d, ffn_norm_g, "mix_out")
    gate, up, ff = _ffn_up(h2, wt_gate, wt_up, "ffn_up")
    dx2, dx2m, d_final_g, loss_part = _ffn_down_loss(ff, wf_down, x1, tgt, final_g, "ffn_down_loss")

    dgate, dup = _ffn_bwd_act(dx2m, wf_down, gate, up, "ffn_bwd_act")
    dw_down = _mm_tn(ff, dx2m, "dw_down", tk=1408, tn=1024).reshape(N_DEV, D_FF // N_DEV, d)
    dw_gate = _mm_tn(dgate, h2, "dw_gate", tk=1408, tn=1024).reshape(N_DEV, D_FF // N_DEV, d)
    dw_up = _mm_tn(dup, h2, "dw_up", tk=1408, tn=1024).reshape(N_DEV, D_FF // N_DEV, d)
    dx1, d_ffn_g, dmix = _dh_norm_bwd([(dgate, wt_gate, None), (dup, wt_up, None)], x1, ffn_norm_g, dx2, "dh_ffn",
                                      tm=512, next_w=wf_out)
    dw_out = _mm_tn(mixed, dx1, "dw_out", tk=1024, tn=1024).reshape(N_DEV, d // N_DEV, d)
    dsq, dsk, dsv, d_sb_g, lands = _sb_bwd(proj_s, sb_norm_g, dmix, ob_pre, ctot, [dw_out, dw_gate, dw_up, dw_down],
                                            n_seq, seq)
    dhq, dhf, dhi, dhg, d_lb, d_hgrn_g = _hgrn_bwd(proj_h, lower_bounds, hgrn_norm_g, dmix, oa_pre, states, n_seq,
                                                   seq)
    dproj = [dhq, dhf, dhi, dhg, dsq, dsk, dsv]
    send_sem, recv_sem, local_sem, dw_in, land_in, token = _dw_exchange_start(dproj, h1, "dw_in")
    dx, d_mix_g = _dh_norm_bwd([(piece, wt_in, k) for k, piece in enumerate(dproj)], x2d, mix_norm_g, dx1, "dh_mix",
                               after=(token,))

    tiles = {"in": 224, "out": 128, "gate": 176, "up": 176, "down": 176}
    keys = list(tiles)
    rest = [_reduce_adamw(land, w, m, v, "adamw_" + key, tr=tiles[key], after=(token,))
            for key, land, w, m, v in zip(keys[1:], lands, big_w[1:], big_m[1:], big_v[1:])]
    small_sum = _small_allreduce([d_mix_g, d_lb, d_hgrn_g, d_sb_g, d_ffn_g, d_final_g], loss_part)
    small, loss_row = _small_adamw(
        small_sum, [mix_norm_g, lower_bounds, hgrn_norm_g, sb_norm_g, ffn_norm_g, final_g],
        [m_mix_norm_g, m_lower_bounds, m_hgrn_norm_g, m_sb_norm_g, m_ffn_norm_g, m_final_norm_g.reshape(1, d)],
        [v_mix_norm_g, v_lower_bounds, v_hgrn_norm_g, v_sb_norm_g, v_ffn_norm_g, v_final_norm_g.reshape(1, d)])
    land_in = _exchange_wait(send_sem, recv_sem, local_sem, dw_in, land_in,
                             [dx, loss_row] + [res[0] for res in rest], "dw_in_await")
    big = [_reduce_adamw(land_in, big_w[0], big_m[0], big_v[0], "adamw_in", tr=tiles["in"])] + rest
    big = [[r.T if tr else r for r in res] for res, tr in zip(big, col_sharded)]

    outs = [loss_row[0, 0], dx.reshape(n_seq, seq, d)]
    for q in range(4):
        b_in, b_out, b_gate, b_up, b_down = [res[q][None] for res in big]
        s_mix, s_lb, s_hgrn, s_sb, s_ffn, s_final = small[q]
        outs += [s_mix, b_in, s_lb, s_hgrn, s_sb, b_out, s_ffn, b_gate, b_up, b_down, s_final.reshape(d)]
    return tuple(outs)
```
